```python
import math
import jax, jax.numpy as jnp
from jax import lax
import numpy as np

D_MODEL = 1024
BATCH = 8
SEQ = 4096
DEPTH = 1

CHUNK = 64
Q_BLOCK = 128
EPS = 1e-6

MLA_HEADS = 8
MLA_NOPE = 64
MLA_ROPE = 32
MLA_V = 64
MLA_QK = MLA_NOPE + MLA_ROPE
Q_LORA = 768
KV_LORA = 256
ROPE_THETA = 10000.0
MLA_WIDTH = MLA_HEADS * MLA_V

HG_HEADS = 8
HG_DK = 64
HG_DV = 64
HG_KEY_WIDTH = HG_HEADS * HG_DK
HG_WIDTH = HG_HEADS * HG_DV
HG_BLOCK = 32

N_BRANCH = 2
IN_SPLITS = (Q_LORA, KV_LORA, MLA_ROPE, MLA_WIDTH,
             HG_KEY_WIDTH, HG_KEY_WIDTH, HG_WIDTH, HG_WIDTH,
             N_BRANCH * D_MODEL)
D_IN = sum(IN_SPLITS)

kernel_name = "hybrid_mla_hgrn2_gated_merge"


def rmsnorm(x, g):
    xf = x.astype(jnp.float32)
    y = xf * lax.rsqrt(jnp.mean(xf * xf, axis=-1, keepdims=True) + EPS)
    return (y * g.astype(jnp.float32)).astype(x.dtype)


def rope_tables(seq):
    inv = ROPE_THETA ** (-jnp.arange(0, MLA_ROPE, 2, dtype=jnp.float32) / MLA_ROPE)
    ang = jnp.arange(seq, dtype=jnp.float32)[:, None] * inv[None, :]
    return jnp.cos(ang), jnp.sin(ang)


def apply_rope(x, cos, sin):
    half = MLA_ROPE // 2
    xf = x.astype(jnp.float32)
    x1, x2 = xf[..., :half], xf[..., half:]
    c = cos[None, :, None, :]
    s = sin[None, :, None, :]
    return jnp.concatenate([x1 * c - x2 * s, x1 * s + x2 * c], axis=-1).astype(x.dtype)


def chunk_causal_attention(q, k, v):
    b, h, s, dq = q.shape
    nqb = s // Q_BLOCK
    qb = q.reshape(b, h, nqb, Q_BLOCK, dq).transpose(2, 0, 1, 3, 4)
    key_chunk = jnp.arange(s) // CHUNK
    q_chunk = (jnp.arange(s) // CHUNK).reshape(nqb, Q_BLOCK)
    scale = 1.0 / math.sqrt(dq)

    def one_block(args):
        qi, qc = args
        sc = jnp.einsum('bhqd,bhkd->bhqk', qi, k).astype(jnp.float32) * scale
        mask = key_chunk[None, :] <= qc[:, None]
        sc = jnp.where(mask, sc, -jnp.inf)
        p = jax.nn.softmax(sc, axis=-1).astype(v.dtype)
        return jnp.einsum('bhqk,bhkd->bhqd', p, v)

    out = lax.map(one_block, (qb, q_chunk))
    return out.transpose(1, 2, 0, 3, 4).reshape(b, h, s, -1)


def hgrn2_chunkwise(q, k, v, log_f):
    b, s, h, dk = q.shape
    dv = v.shape[-1]
    n = s // HG_BLOCK

    def blocks(t):
        return t.astype(jnp.float32).reshape(b, n, HG_BLOCK, h, -1).transpose(0, 3, 1, 2, 4)

    qc, kc, vc, gc = blocks(q), blocks(k), blocks(v), blocks(log_f)
    cum = jnp.cumsum(gc, axis=3)
    last = cum[..., -1:, :]
    q_dec = qc * jnp.exp(cum)
    k_inv = kc * jnp.exp(-cum)
    k_end = kc * jnp.exp(last - cum)
    causal = jnp.tril(jnp.ones((HG_BLOCK, HG_BLOCK), dtype=bool))
    a = jnp.where(causal, jnp.einsum('bhntk,bhnsk->bhnts', q_dec, k_inv), 0.0)
    o_intra = jnp.einsum('bhnts,bhnsv->bhntv', a, vc)
    upd = jnp.einsum('bhnsk,bhnsv->bhnkv', k_end, vc)
    decay = jnp.exp(last[..., 0, :])

    def step(state, xs):
        d, u = xs
        return d[..., None] * state + u, state

    init = jnp.zeros((b, h, dk, dv), jnp.float32)
    _, s_prev = lax.scan(step, init, (jnp.moveaxis(decay, 2, 0), jnp.moveaxis(upd, 2, 0)))
    s_prev = jnp.moveaxis(s_prev, 0, 2)
    o_inter = jnp.einsum('bhntk,bhnkv->bhntv', q_dec, s_prev)
    return (o_intra + o_inter).transpose(0, 2, 3, 1, 4).reshape(b, s, h, dv)


def _fwd_setup_inputs(seed: int = 0) -> dict:
    key = jax.random.key(seed)
    ks = jax.random.split(key, 14)

    def nrm(k, shape, fan_in):
        return jax.random.normal(k, shape, jnp.float32) * (fan_in ** -0.5)

    def gain(k, shape):
        return 1.0 + 0.02 * jax.random.normal(k, shape, jnp.float32)

    return {
        "x": jax.random.normal(ks[0], (BATCH, SEQ, D_MODEL), jnp.float32),
        "g_pre": gain(ks[1], (DEPTH, D_MODEL)),
        "w_in": nrm(ks[2], (DEPTH, D_MODEL, D_IN), D_MODEL),
        "b_gate": 0.01 * jax.random.normal(ks[3], (DEPTH, N_BRANCH * D_MODEL), jnp.float32),
        "g_q": gain(ks[4], (DEPTH, Q_LORA)),
        "w_uq": nrm(ks[5], (DEPTH, Q_LORA, MLA_HEADS * MLA_QK), Q_LORA),
        "g_kv": gain(ks[6], (DEPTH, KV_LORA)),
        "w_ukv": nrm(ks[7], (DEPTH, KV_LORA, MLA_HEADS * (MLA_NOPE + MLA_V)), KV_LORA),
        "lb_logits": 0.1 * jax.random.normal(ks[8], (DEPTH + 1, HG_KEY_WIDTH), jnp.float32),
        "g_hgrn": gain(ks[9], (DEPTH, HG_DV)),
        "w_branch_a": nrm(ks[10], (DEPTH, MLA_WIDTH, D_MODEL), MLA_WIDTH),
        "w_branch_b": nrm(ks[11], (DEPTH, HG_WIDTH, D_MODEL), HG_WIDTH),
        "w_out": nrm(ks[12], (DEPTH, D_MODEL, D_MODEL), D_MODEL),
        "g_post": gain(ks[13], (DEPTH, D_MODEL)),
    }


def _fwd_reference(x, g_pre, w_in, b_gate, g_q, w_uq, g_kv, w_ukv, lb_logits,
              g_hgrn, w_branch_a, w_branch_b, w_out, g_post):
    b, s, _ = x.shape
    cos, sin = rope_tables(s)
    split_at = [int(o) for o in np.cumsum(IN_SPLITS)[:-1]]
    lower_bounds = jnp.cumsum(jax.nn.softmax(lb_logits.astype(jnp.float32), axis=0), axis=0)

    for l in range(DEPTH):
        h = rmsnorm(x, g_pre[l])
        proj = h @ w_in[l]
        (c_q, c_kv, k_pe, gate_a, hq, hf, hi, gate_b, merge_logits) = jnp.split(proj, split_at, axis=-1)

        q = (rmsnorm(c_q, g_q[l]) @ w_uq[l]).reshape(b, s, MLA_HEADS, MLA_QK)
        q_nope, q_pe = q[..., :MLA_NOPE], apply_rope(q[..., MLA_NOPE:], cos, sin)
        kv = (rmsnorm(c_kv, g_kv[l]) @ w_ukv[l]).reshape(b, s, MLA_HEADS, MLA_NOPE + MLA_V)
        k_nope, v = kv[..., :MLA_NOPE], kv[..., MLA_NOPE:]
        k_pe = jnp.broadcast_to(apply_rope(k_pe[:, :, None, :], cos, sin), (b, s, MLA_HEADS, MLA_ROPE))
        qf = jnp.concatenate([q_nope, q_pe], axis=-1).transpose(0, 2, 1, 3)
        kf = jnp.concatenate([k_nope, k_pe], axis=-1).transpose(0, 2, 1, 3)
        vf = v.transpose(0, 2, 1, 3)
        attn = chunk_causal_attention(qf, kf, vf).transpose(0, 2, 1, 3).reshape(b, s, MLA_WIDTH)
        y_a = (attn * jax.nn.silu(gate_a)) @ w_branch_a[l]

        lb = lower_bounds[l]
        f = lb + (1.0 - lb) * jax.nn.sigmoid(hf.astype(jnp.float32))
        log_f = jnp.log(f).reshape(b, s, HG_HEADS, HG_DK)
        k_in = (1.0 - f).reshape(b, s, HG_HEADS, HG_DK)
        o = hgrn2_chunkwise(hq.reshape(b, s, HG_HEADS, HG_DK), k_in,
                            hi.reshape(b, s, HG_HEADS, HG_DV), log_f)
        o = rmsnorm(o, g_hgrn[l]).astype(x.dtype).reshape(b, s, HG_WIDTH)
        y_b = (o * jax.nn.silu(gate_b)) @ w_branch_b[l]

        gates = jax.nn.sigmoid((merge_logits + b_gate[l]).astype(jnp.float32)).astype(x.dtype)
        m = gates[..., :D_MODEL] * y_a + gates[..., D_MODEL:] * y_b
        y = m @ w_out[l]
        x = x + rmsnorm(y, g_post[l])
    return x


import jax as _jax
import jax.numpy as _jnp

TWIN_FORMAT = 'train_step'
FWD_PARAMS = ['x', 'g_pre', 'w_in', 'b_gate', 'g_q', 'w_uq', 'g_kv', 'w_ukv', 'lb_logits', 'g_hgrn', 'w_branch_a', 'w_branch_b', 'w_out', 'g_post']
TWIN_WEIGHTS = ['g_pre', 'w_in', 'b_gate', 'g_q', 'w_uq', 'g_kv', 'w_ukv', 'lb_logits', 'g_hgrn', 'w_branch_a', 'w_branch_b', 'w_out', 'g_post']
TWIN_DIFF_INPUT = 'x'
TWIN_INPUTS = ['x', 'g_pre', 'w_in', 'b_gate', 'g_q', 'w_uq', 'g_kv', 'w_ukv', 'lb_logits', 'g_hgrn', 'w_branch_a', 'w_branch_b', 'w_out', 'g_post', 'loss_target', 'm_g_pre', 'm_w_in', 'm_b_gate', 'm_g_q', 'm_w_uq', 'm_g_kv', 'm_w_ukv', 'm_lb_logits', 'm_g_hgrn', 'm_w_branch_a', 'm_w_branch_b', 'm_w_out', 'm_g_post', 'v_g_pre', 'v_w_in', 'v_b_gate', 'v_g_q', 'v_w_uq', 'v_g_kv', 'v_w_ukv', 'v_lb_logits', 'v_g_hgrn', 'v_w_branch_a', 'v_w_branch_b', 'v_w_out', 'v_g_post']
TWIN_OUTPUTS = ['loss', 'grad_x', 'grad_g_pre', 'grad_w_in', 'grad_b_gate', 'grad_g_q', 'grad_w_uq', 'grad_g_kv', 'grad_w_ukv', 'grad_lb_logits', 'grad_g_hgrn', 'grad_w_branch_a', 'grad_w_branch_b', 'grad_w_out', 'grad_g_post', 'delta_g_pre', 'delta_w_in', 'delta_b_gate', 'delta_g_q', 'delta_w_uq', 'delta_g_kv', 'delta_w_ukv', 'delta_lb_logits', 'delta_g_hgrn', 'delta_w_branch_a', 'delta_w_branch_b', 'delta_w_out', 'delta_g_post', 'new_m_g_pre', 'new_m_w_in', 'new_m_b_gate', 'new_m_g_q', 'new_m_w_uq', 'new_m_g_kv', 'new_m_w_ukv', 'new_m_lb_logits', 'new_m_g_hgrn', 'new_m_w_branch_a', 'new_m_w_branch_b', 'new_m_w_out', 'new_m_g_post', 'new_v_g_pre', 'new_v_w_in', 'new_v_b_gate', 'new_v_g_q', 'new_v_w_uq', 'new_v_g_kv', 'new_v_w_ukv', 'new_v_lb_logits', 'new_v_g_hgrn', 'new_v_w_branch_a', 'new_v_w_branch_b', 'new_v_w_out', 'new_v_g_post']
TWIN_LEAF_KINDS = {'loss': 'loss', 'grad_x': 'grad_x', 'grad_g_pre': 'grad_w', 'grad_w_in': 'grad_w', 'grad_b_gate': 'grad_w', 'grad_g_q': 'grad_w', 'grad_w_uq': 'grad_w', 'grad_g_kv': 'grad_w', 'grad_w_ukv': 'grad_w', 'grad_lb_logits': 'grad_w', 'grad_g_hgrn': 'grad_w', 'grad_w_branch_a': 'grad_w', 'grad_w_branch_b': 'grad_w', 'grad_w_out': 'grad_w', 'grad_g_post': 'grad_w', 'delta_g_pre': 'delta_w', 'delta_w_in': 'delta_w', 'delta_b_gate': 'delta_w', 'delta_g_q': 'delta_w', 'delta_w_uq': 'delta_w', 'delta_g_kv': 'delta_w', 'delta_w_ukv': 'delta_w', 'delta_lb_logits': 'delta_w', 'delta_g_hgrn': 'delta_w', 'delta_w_branch_a': 'delta_w', 'delta_w_branch_b': 'delta_w', 'delta_w_out': 'delta_w', 'delta_g_post': 'delta_w', 'new_m_g_pre': 'new_m', 'new_m_w_in': 'new_m', 'new_m_b_gate': 'new_m', 'new_m_g_q': 'new_m', 'new_m_w_uq': 'new_m', 'new_m_g_kv': 'new_m', 'new_m_w_ukv': 'new_m', 'new_m_lb_logits': 'new_m', 'new_m_g_hgrn': 'new_m', 'new_m_w_branch_a': 'new_m', 'new_m_w_branch_b': 'new_m', 'new_m_w_out': 'new_m', 'new_m_g_post': 'new_m', 'new_v_g_pre': 'new_v', 'new_v_w_in': 'new_v', 'new_v_b_gate': 'new_v', 'new_v_g_q': 'new_v', 'new_v_w_uq': 'new_v', 'new_v_g_kv': 'new_v', 'new_v_w_ukv': 'new_v', 'new_v_lb_logits': 'new_v', 'new_v_g_hgrn': 'new_v', 'new_v_w_branch_a': 'new_v', 'new_v_w_branch_b': 'new_v', 'new_v_w_out': 'new_v', 'new_v_g_post': 'new_v'}


def _forward(args):
    return _fwd_reference(*[args[k] for k in FWD_PARAMS])


def _output_shape():
    out = _jax.eval_shape(lambda: _forward(_fwd_setup_inputs(0)))
    return out.shape, out.dtype

N_MICROBATCH = 1
ADAM_LR = 0.001
ADAM_B1 = 0.9
ADAM_B2 = 0.999
ADAM_EPS = 1e-08
ADAM_WD = 0.01
ADAM_STEP = 10
PER_EXAMPLE_BATCH_AXIS = {'x': 0, 'loss_target': 0}
SHARED_INPUTS = []
_WEIGHT_DTYPES = {'g_pre': _jnp.float32, 'w_in': _jnp.float32, 'b_gate': _jnp.float32, 'g_q': _jnp.float32, 'w_uq': _jnp.float32, 'g_kv': _jnp.float32, 'w_ukv': _jnp.float32, 'lb_logits': _jnp.float32, 'g_hgrn': _jnp.float32, 'w_branch_a': _jnp.float32, 'w_branch_b': _jnp.float32, 'w_out': _jnp.float32, 'g_post': _jnp.float32}
MOMENT_SCALE = {'g_pre': 6.037238e-01, 'w_in': 2.459520e-01, 'b_gate': 7.339963e-02, 'g_q': 4.504294e-02, 'w_uq': 4.346542e-02, 'g_kv': 1.048540e-01, 'w_ukv': 5.090329e-02, 'lb_logits': 2.424882e-01, 'g_hgrn': 1.115373e+00, 'w_branch_a': 3.937400e-02, 'w_branch_b': 2.515316e-01, 'w_out': 2.530453e-01, 'g_post': 3.211331e+01}


def _to_microbatches(a, axis):
    t = _jnp.moveaxis(a, axis, 0)
    t = t.reshape((N_MICROBATCH, t.shape[0] // N_MICROBATCH) + t.shape[1:])
    return _jnp.moveaxis(t, 1, axis + 1)


def setup_inputs(seed: int = 0) -> dict:
    inp = _fwd_setup_inputs(seed)
    key = _jax.random.fold_in(_jax.random.key(seed), 7919)
    shape, _ = _output_shape()
    out = dict(inp)
    out["loss_target"] = _jax.random.normal(_jax.random.fold_in(key, 0), shape, _jnp.float32)
    for i, name in enumerate(TWIN_WEIGHTS):
        w = inp[name].astype(_jnp.float32)
        if MOMENT_SCALE is None:
            s = _jnp.sqrt(_jnp.mean(_jnp.square(w)) + 1e-30)
        else:
            s = MOMENT_SCALE[name]
        km, kv = _jax.random.split(_jax.random.fold_in(key, i + 1))
        out[name] = w
        out["m_" + name] = s * _jax.random.normal(km, w.shape, _jnp.float32)
        out["v_" + name] = (s * s) * _jax.random.uniform(kv, w.shape, _jnp.float32, 0.5, 1.5)
    if N_MICROBATCH > 1:
        for name, axis in PER_EXAMPLE_BATCH_AXIS.items():
            out[name] = _to_microbatches(out[name], axis)
    return {'x': out['x'], 'g_pre': out['g_pre'], 'w_in': out['w_in'], 'b_gate': out['b_gate'], 'g_q': out['g_q'], 'w_uq': out['w_uq'], 'g_kv': out['g_kv'], 'w_ukv': out['w_ukv'], 'lb_logits': out['lb_logits'], 'g_hgrn': out['g_hgrn'], 'w_branch_a': out['w_branch_a'], 'w_branch_b': out['w_branch_b'], 'w_out': out['w_out'], 'g_post': out['g_post'], 'loss_target': out['loss_target'], 'm_g_pre': out['m_g_pre'], 'm_w_in': out['m_w_in'], 'm_b_gate': out['m_b_gate'], 'm_g_q': out['m_g_q'], 'm_w_uq': out['m_w_uq'], 'm_g_kv': out['m_g_kv'], 'm_w_ukv': out['m_w_ukv'], 'm_lb_logits': out['m_lb_logits'], 'm_g_hgrn': out['m_g_hgrn'], 'm_w_branch_a': out['m_w_branch_a'], 'm_w_branch_b': out['m_w_branch_b'], 'm_w_out': out['m_w_out'], 'm_g_post': out['m_g_post'], 'v_g_pre': out['v_g_pre'], 'v_w_in': out['v_w_in'], 'v_b_gate': out['v_b_gate'], 'v_g_q': out['v_g_q'], 'v_w_uq': out['v_w_uq'], 'v_g_kv': out['v_g_kv'], 'v_w_ukv': out['v_w_ukv'], 'v_lb_logits': out['v_lb_logits'], 'v_g_hgrn': out['v_g_hgrn'], 'v_w_branch_a': out['v_w_branch_a'], 'v_w_branch_b': out['v_w_branch_b'], 'v_w_out': out['v_w_out'], 'v_g_post': out['v_g_post']}


def _loss(weights, diff, rest, loss_target):
    with _jax.named_scope("forward"):
        args = {**rest, TWIN_DIFF_INPUT: diff, **{k: w.astype(_WEIGHT_DTYPES[k]) for k, w in weights.items()}}
        y = _forward(args)
    with _jax.named_scope("loss_head"):
        err = _jnp.square(y.astype(_jnp.float32) - loss_target)
        return 0.5 * _jnp.sum(_jnp.mean(err, axis=-1)) if err.ndim else 0.5 * err


def _adamw(w, g, m, v):
    m = ADAM_B1 * m + (1.0 - ADAM_B1) * g
    v = ADAM_B2 * v + (1.0 - ADAM_B2) * _jnp.square(g)
    m_hat = m / (1.0 - ADAM_B1 ** ADAM_STEP)
    v_hat = v / (1.0 - ADAM_B2 ** ADAM_STEP)
    delta = -ADAM_LR * (m_hat / (_jnp.sqrt(v_hat) + ADAM_EPS) + ADAM_WD * w)
    return delta, m, v


def reference(x, g_pre, w_in, b_gate, g_q, w_uq, g_kv, w_ukv, lb_logits, g_hgrn, w_branch_a, w_branch_b, w_out, g_post, loss_target, m_g_pre, m_w_in, m_b_gate, m_g_q, m_w_uq, m_g_kv, m_w_ukv, m_lb_logits, m_g_hgrn, m_w_branch_a, m_w_branch_b, m_w_out, m_g_post, v_g_pre, v_w_in, v_b_gate, v_g_q, v_w_uq, v_g_kv, v_w_ukv, v_lb_logits, v_g_hgrn, v_w_branch_a, v_w_branch_b, v_w_out, v_g_post):
    given = dict(x=x, g_pre=g_pre, w_in=w_in, b_gate=b_gate, g_q=g_q, w_uq=w_uq, g_kv=g_kv, w_ukv=w_ukv, lb_logits=lb_logits, g_hgrn=g_hgrn, w_branch_a=w_branch_a, w_branch_b=w_branch_b, w_out=w_out, g_post=g_post, loss_target=loss_target, m_g_pre=m_g_pre, m_w_in=m_w_in, m_b_gate=m_b_gate, m_g_q=m_g_q, m_w_uq=m_w_uq, m_g_kv=m_g_kv, m_w_ukv=m_w_ukv, m_lb_logits=m_lb_logits, m_g_hgrn=m_g_hgrn, m_w_branch_a=m_w_branch_a, m_w_branch_b=m_w_branch_b, m_w_out=m_w_out, m_g_post=m_g_post, v_g_pre=v_g_pre, v_w_in=v_w_in, v_b_gate=v_b_gate, v_g_q=v_g_q, v_w_uq=v_w_uq, v_g_kv=v_g_kv, v_w_ukv=v_w_ukv, v_lb_logits=v_lb_logits, v_g_hgrn=v_g_hgrn, v_w_branch_a=v_w_branch_a, v_w_branch_b=v_w_branch_b, v_w_out=v_w_out, v_g_post=v_g_post)
    weights = {n: given[n] for n in TWIN_WEIGHTS}
    shared = {n: given[n] for n in SHARED_INPUTS}
    per_example = {n: given[n] for n in ['x']}
    grad_fn = _jax.value_and_grad(_loss, argnums=(0, 1))

    def one_microbatch(ex, loss_target):
        ex = dict(ex)
        diff = ex.pop(TWIN_DIFF_INPUT)
        return grad_fn(weights, diff, {**shared, **ex}, loss_target)

    if N_MICROBATCH == 1:
        loss, (grad_w, grad_x) = one_microbatch(per_example, given["loss_target"])
    else:
        def body(carry, xs):
            loss_sum, grad_sum = carry
            l_k, (gw_k, gx_k) = one_microbatch(xs[0], xs[1])
            with _jax.named_scope("update"):
                return (loss_sum + l_k, _jax.tree.map(_jnp.add, grad_sum, gw_k)), gx_k

        init = (_jnp.zeros((), _jnp.float32), _jax.tree.map(_jnp.zeros_like, weights))
        (loss, grad_w), grad_x = _jax.lax.scan(body, init, (per_example, given["loss_target"]))
    with _jax.named_scope("update"):
        delta_w, new_m, new_v = {}, {}, {}
        for n in TWIN_WEIGHTS:
            delta_w[n], new_m[n], new_v[n] = _adamw(weights[n], grad_w[n], given["m_" + n], given["v_" + n])
    return (loss, grad_x, *[grad_w[n] for n in TWIN_WEIGHTS], *[delta_w[n] for n in TWIN_WEIGHTS],
            *[new_m[n] for n in TWIN_WEIGHTS], *[new_v[n] for n in TWIN_WEIGHTS])
```

```python
import math

import jax
import jax.numpy as jnp
from jax import lax
from jax.experimental import pallas as pl
from jax.experimental.pallas import tpu as pltpu

F32, BF16 = jnp.float32, jnp.bfloat16

D_MODEL = 1024
EPS = 1e-6
HEADS = 8
NOPE, ROPE, VDIM = 64, 32, 64
QK = NOPE + ROPE
Q_LORA, KV_LORA = 768, 256
ROPE_THETA = 10000.0
ATT_CHUNK_SHIFT = 6
HG_BLOCK = 32
HG_WIDTH = 512
D_IN = 5664
D_IN_PAD = 5760
N_DEV = 8
LANE = 128

ADAM_LR, ADAM_B1, ADAM_B2, ADAM_EPS, ADAM_WD, ADAM_STEP = 0.001, 0.9, 0.999, 1e-08, 0.01, 10

PACK_ROWS = (708, 72, 32, 64, 64, 128)
PACK_OFF = (0, 708, 780, 812, 876, 940, 1068)
PACK_BIG = 1072
PACK_ALL = 1080

NT = (((1,), (1,)), ((), ()))
TN = (((0,), (0,)), ((), ()))
MESH_ID = pl.DeviceIdType.MESH


def _params(sem, vmem_mb=48):
    return pltpu.CompilerParams(dimension_semantics=sem, vmem_limit_bytes=vmem_mb * 2**20)


def _dot(a, b):
    return jnp.dot(a, b, preferred_element_type=F32)


def _dotg(a, b, dims):
    return lax.dot_general(a, b, dims, preferred_element_type=F32)


def _split3(x):
    hi = x.astype(BF16)
    r = x - hi.astype(F32)
    mid = r.astype(BF16)
    lo = (r - mid.astype(F32)).astype(BF16)
    return hi, mid, lo


def _sel_left(m01, x):
    hi, mid, lo = _split3(x)
    return _dot(m01, hi) + _dot(m01, mid) + _dot(m01, lo)


def _sel_right(x, m01):
    hi, mid, lo = _split3(x)
    return _dot(hi, m01) + _dot(mid, m01) + _dot(lo, m01)


def _sigmoid(x):
    return 1.0 / (1.0 + jnp.exp(-x))


def _rope(x, c, s1, s2):
    return x * c + pltpu.roll(x, 112, 1) * s1 + pltpu.roll(x, 16, 1) * s2


def _unrope(d, c, s1, s2):
    return d * c + pltpu.roll(d * s1, 16, 1) + pltpu.roll(d * s2, 112, 1)


def _norm_proj(x, g_pre, w):
    s, n = x.shape[0], w.shape[1]
    tm, tn = 512, 1152

    def body(x_ref, g_ref, w_ref, proj_ref, h_ref):
        @pl.when(pl.program_id(1) == 0)
        def _():
            xv = x_ref[...]
            r = lax.rsqrt(jnp.mean(xv * xv, axis=-1, keepdims=True) + EPS)
            h_ref[...] = (xv * r * g_ref[...]).astype(BF16)

        proj_ref[...] = _dot(h_ref[...], w_ref[...])

    return pl.pallas_call(
        body,
        grid=(s // tm, n // tn),
        in_specs=[
            pl.BlockSpec((tm, D_MODEL), lambda i, j: (i, 0)),
            pl.BlockSpec((1, D_MODEL), lambda i, j: (0, 0)),
            pl.BlockSpec((D_MODEL, tn), lambda i, j: (0, j)),
        ],
        out_specs=[
            pl.BlockSpec((tm, tn), lambda i, j: (i, j)),
            pl.BlockSpec((tm, D_MODEL), lambda i, j: (i, 0)),
        ],
        out_shape=[jax.ShapeDtypeStruct((s, n), F32), jax.ShapeDtypeStruct((s, D_MODEL), BF16)],
        compiler_params=_params(("arbitrary", "arbitrary")),
        name="norm_proj",
    )(x, g_pre, w)


def _mla_prep(proj, g_q, g_kv, w_uq_p, w_kv_p, rc, rs1, rs2):
    s = proj.shape[0]
    tm = 256
    scale = 1.0 / math.sqrt(QK)

    def body(cq_ref, ckv_ref, kpe_ref, gq_ref, gkv_ref, wuq_ref, wkv_ref, c_ref, s1_ref, s2_ref,
             qr_ref, kr_ref, v_ref, cqn_ref, ckvn_ref):
        cq = cq_ref[...]
        r = lax.rsqrt(jnp.mean(cq * cq, axis=-1, keepdims=True) + EPS)
        cqn = (cq * r * gq_ref[...]).astype(BF16)
        cqn_ref[...] = cqn
        q = _dot(cqn, wuq_ref[...])
        ckv = ckv_ref[...]
        r = lax.rsqrt(jnp.mean(ckv * ckv, axis=-1, keepdims=True) + EPS)
        ckvn = (ckv * r * gkv_ref[...]).astype(BF16)
        ckvn_ref[...] = ckvn
        kv = _dot(ckvn, wkv_ref[...])
        c, s1, s2 = c_ref[...], s1_ref[...], s2_ref[...]
        kpe = _rope(kpe_ref[...], c, s1, s2)
        for h in range(HEADS):
            sl = slice(LANE * h, LANE * (h + 1))
            qr_ref[:, sl] = (_rope(q[:, sl], c, s1, s2) * scale).astype(BF16)
            kr_ref[:, sl] = (kv[:, sl] + kpe).astype(BF16)
        v_ref[...] = kv[:, HEADS * LANE:].astype(BF16)

    row = lambda w, j: pl.BlockSpec((tm, w), lambda i: (i, j))
    full = lambda a: pl.BlockSpec(a.shape, lambda i: (0, 0))
    return pl.pallas_call(
        body,
        grid=(s // tm,),
        in_specs=[row(768, 6), row(256, 21), row(128, 44), full(g_q), full(g_kv), full(w_uq_p), full(w_kv_p),
                  row(128, 0), row(128, 0), row(128, 0)],
        out_specs=[row(1024, 0), row(1024, 0), row(512, 0), row(768, 0), row(256, 0)],
        out_shape=[jax.ShapeDtypeStruct((s, 1024), BF16), jax.ShapeDtypeStruct((s, 1024), BF16),
                   jax.ShapeDtypeStruct((s, 512), BF16), jax.ShapeDtypeStruct((s, 768), BF16),
                   jax.ShapeDtypeStruct((s, 256), BF16)],
        compiler_params=_params(("arbitrary",)),
        name="mla_prep",
    )(proj, proj, proj, g_q, g_kv, w_uq_p, w_kv_p, rc, rs1, rs2)


ATT_T = 256


def _chunk_mask():
    rq = lax.broadcasted_iota(jnp.int32, (ATT_T, ATT_T), 0) >> ATT_CHUNK_SHIFT
    ck = lax.broadcasted_iota(jnp.int32, (ATT_T, ATT_T), 1) >> ATT_CHUNK_SHIFT
    return ck <= rq


def _attn_fwd(qr, kr, v):
    s = qr.shape[0]
    t = ATT_T

    def body(q_ref, k_ref, v_ref, o_ref, lse_ref):
        qi = pl.program_id(1)
        lo = lax.broadcasted_iota(jnp.int32, (t, LANE), 1) < 64
        outs = []
        for a in range(2):
            sl = slice(LANE * a, LANE * (a + 1))
            q = q_ref[:, sl]

            def step(j, carry, masked, q=q, sl=sl):
                m, l, acc = carry
                rows = pl.ds(pl.multiple_of(j * t, t), t)
                sc = _dotg(q, k_ref[rows, sl], NT)
                if masked:
                    sc = jnp.where(_chunk_mask(), sc, -1e30)
                m_new = jnp.maximum(m, jnp.max(sc, axis=-1, keepdims=True))
                alpha = jnp.exp(m - m_new)
                p = jnp.exp(sc - m_new)
                l = alpha * l + jnp.sum(p, axis=-1, keepdims=True)
                acc = alpha * acc + _dot(p.astype(BF16), v_ref[rows, :])
                return m_new, l, acc

            init = (jnp.full((t, 1), -1e30, F32), jnp.zeros((t, 1), F32), jnp.zeros((t, LANE), F32))
            carry = lax.fori_loop(0, qi, lambda j, c: step(j, c, False), init)
            m, l, acc = step(qi, carry, True)
            outs.append(acc / l)
            lse_ref[a] = jnp.broadcast_to(m + jnp.log(l), (t, LANE))
        o_ref[...] = jnp.where(lo, outs[0], outs[1])

    return pl.pallas_call(
        body,
        grid=(HEADS // 2, s // t),
        in_specs=[
            pl.BlockSpec((t, 2 * LANE), lambda h, i: (i, h)),
            pl.BlockSpec((s, 2 * LANE), lambda h, i: (0, h)),
            pl.BlockSpec((s, LANE), lambda h, i: (0, h)),
        ],
        out_specs=[
            pl.BlockSpec((t, LANE), lambda h, i: (i, h)),
            pl.BlockSpec((2, t, LANE), lambda h, i: (h, i, 0)),
        ],
        out_shape=[jax.ShapeDtypeStruct((s, 512), F32), jax.ShapeDtypeStruct((HEADS, s, LANE), F32)],
        compiler_params=_params(("arbitrary", "arbitrary")),
        name="attn_fwd",
    )(qr, kr, v)


def _attn_bwd(qr, kr, v, o, do, lse):
    s = qr.shape[0]
    t = ATT_T
    nq = s // t

    def body(q_ref, k_ref, v_ref, o_ref, do_ref, lse_ref, dq_ref, dk_ref, dv_ref):
        j = pl.program_id(1)
        lo = lax.broadcasted_iota(jnp.int32, (t, LANE), 1) < 64

        @pl.when(j == 0)
        def _():
            dq_ref[...] = jnp.zeros_like(dq_ref)

        dk_ref[...] = jnp.zeros_like(dk_ref)
        dv_ref[...] = jnp.zeros_like(dv_ref)
        vj = v_ref[...]

        def step(i, masked):
            rows = pl.ds(pl.multiple_of(i * t, t), t)
            dov = do_ref[rows, :]
            prod = dov * o_ref[rows, :]
            for a in range(2):
                sl = slice(LANE * a, LANE * (a + 1))
                mine = lo if a == 0 else jnp.logical_not(lo)
                q = q_ref[rows, sl]
                kj = k_ref[:, sl]
                sc = _dotg(q, kj, NT)
                if masked:
                    sc = jnp.where(_chunk_mask(), sc, -1e30)
                lse = lse_ref[a, rows, :]
                p = jnp.exp(sc - jnp.concatenate([lse, lse], axis=1))
                do_a = jnp.where(mine, dov, 0.0).astype(BF16)
                delta = jnp.sum(jnp.where(mine, prod, 0.0), axis=-1, keepdims=True)
                dp = _dotg(do_a, vj, NT)
                ds = (p * (dp - delta)).astype(BF16)
                dv_ref[...] += _dotg(p.astype(BF16), do_a, TN)
                dk_ref[:, sl] += _dotg(ds, q, TN)
                dq_ref[rows, sl] += _dot(ds, kj)

        step(j, True)

        def loop(i, c):
            step(i, False)
            return c

        lax.fori_loop(j + 1, nq, loop, 0)

    return pl.pallas_call(
        body,
        grid=(HEADS // 2, nq),
        in_specs=[
            pl.BlockSpec((s, 2 * LANE), lambda h, j: (0, h)),
            pl.BlockSpec((t, 2 * LANE), lambda h, j: (j, h)),
            pl.BlockSpec((t, LANE), lambda h, j: (j, h)),
            pl.BlockSpec((s, LANE), lambda h, j: (0, h)),
            pl.BlockSpec((s, LANE), lambda h, j: (0, h)),
            pl.BlockSpec((2, s, LANE), lambda h, j: (h, 0, 0)),
        ],
        out_specs=[
            pl.BlockSpec((s, 2 * LANE), lambda h, j: (0, h)),
            pl.BlockSpec((t, 2 * LANE), lambda h, j: (j, h)),
            pl.BlockSpec((t, LANE), lambda h, j: (j, h)),
        ],
        out_shape=[jax.ShapeDtypeStruct((s, 1024), F32), jax.ShapeDtypeStruct((s, 1024), F32),
                   jax.ShapeDtypeStruct((s, 512), F32)],
        compiler_params=_params(("arbitrary", "arbitrary")),
        name="attn_bwd",
    )(qr, kr, v, o, do, lse)


HG_T = 256
HG_NC = HG_T // HG_BLOCK


def _hg_consts():
    r = jnp.arange(HG_T)[:, None]
    c = jnp.arange(HG_T)[None, :]
    same = (r // HG_BLOCK) == (c // HG_BLOCK)
    mcum = (same & (c <= r)).astype(BF16)
    mrev = (same & (c >= r)).astype(BF16)
    mlast = (c == (r // HG_BLOCK) * HG_BLOCK + HG_BLOCK - 1).astype(BF16)
    msum = same.astype(BF16)
    a = jnp.arange(LANE)
    bd = ((a[:, None] < 64) == (a[None, :] < 64)).astype(F32)
    return mcum, mrev, mlast, msum, bd


def _hg_pre(hq, hf, lbl, mcum, mlast):
    lb = _sigmoid(lbl[0:1, :] - lbl[1:2, :])
    sig = _sigmoid(hf)
    f = lb + (1.0 - lb) * sig
    b = _sel_left(mcum, jnp.log(f))
    big_l = _sel_left(mlast, b)
    k = 1.0 - f
    qd = hq * jnp.exp(b)
    ki = k * jnp.exp(-b)
    ke = k * jnp.exp(big_l - b)
    return lb, sig, f, b, big_l, qd, ki, ke


def _stack_pair(xp, lo):
    return jnp.concatenate([jnp.where(lo, xp, 0.0), jnp.where(lo, 0.0, xp)], axis=0)


def _hgrn_fwd(proj, lbl):
    s = proj.shape[0]
    t = HG_T
    mcum, _, mlast, _, bd = _hg_consts()

    def body(hq_ref, hf_ref, hi_ref, lbl_ref, mcum_ref, mlast_ref, bd_ref, o_ref, sp_ref, st_ref):
        @pl.when(pl.program_id(0) == 0)
        def _():
            st_ref[...] = jnp.zeros_like(st_ref)

        mc = mcum_ref[...]
        _, _, _, _, big_l, qd, ki, ke = _hg_pre(hq_ref[...], hf_ref[...], lbl_ref[...], mc, mlast_ref[...])
        el = jnp.exp(big_l)
        hi = hi_ref[...]
        lo = lax.broadcasted_iota(jnp.int32, (t, LANE), 1) < 64
        mask2 = jnp.concatenate([mc, mc], axis=0) > 0.5
        for p in range(HEADS // 2):
            sl = slice(LANE * p, LANE * (p + 1))
            vp = hi[:, sl].astype(BF16)
            q2 = _stack_pair(qd[:, sl], lo).astype(BF16)
            a2 = jnp.where(mask2, _dotg(q2, ki[:, sl].astype(BF16), NT), 0.0)
            r2 = _dot(a2.astype(BF16), vp)
            o_intra = jnp.where(lo, r2[:t], r2[t:])
            qb = qd[:, sl].astype(BF16)
            kb = ke[:, sl].astype(BF16)
            for c in range(HG_NC):
                rows = slice(HG_BLOCK * c, HG_BLOCK * (c + 1))
                st = st_ref[p]
                sp_ref[c, :, sl] = st
                o_ref[rows, sl] = o_intra[rows] + _dotg(qb[rows], st.astype(BF16), NT)
                u = _dotg(vp[rows], kb[rows], TN) * bd_ref[...]
                st_ref[p] = st * el[HG_BLOCK * c:HG_BLOCK * c + 1, sl] + u

    row = lambda j: pl.BlockSpec((t, HG_WIDTH), lambda i: (i, j))
    full = lambda a: pl.BlockSpec(a.shape, lambda i: (0, 0))
    return pl.pallas_call(
        body,
        grid=(s // t,),
        in_specs=[row(5), row(6), row(7), full(lbl), full(mcum), full(mlast), full(bd)],
        out_specs=[row(0), pl.BlockSpec((HG_NC, LANE, HG_WIDTH), lambda i: (i, 0, 0))],
        out_shape=[jax.ShapeDtypeStruct((s, HG_WIDTH), F32),
                   jax.ShapeDtypeStruct((s // HG_BLOCK, LANE, HG_WIDTH), F32)],
        scratch_shapes=[pltpu.VMEM((HEADS // 2, LANE, LANE), F32)],
        compiler_params=_params(("arbitrary",)),
        name="hgrn_fwd",
    )(proj, proj, proj, lbl, mcum, mlast, bd)


def _hgrn_bwd(proj, lbl, do, sprev):
    s = proj.shape[0]
    t = HG_T
    nt = s // t
    mcum, mrev, mlast, msum, bd = _hg_consts()

    def body(hq_ref, hf_ref, hi_ref, lbl_ref, do_ref, sp_ref, mcum_ref, mrev_ref, mlast_ref, msum_ref, bd_ref,
             dhq_ref, dhf_ref, dhi_ref, dlbl_ref, g_ref, dqd_ref, dke_ref, dv_ref, del_ref):
        @pl.when(pl.program_id(0) == 0)
        def _():
            g_ref[...] = jnp.zeros_like(g_ref)
            dlbl_ref[...] = jnp.zeros_like(dlbl_ref)

        mc = mcum_ref[...]
        lb, sig, f, b, big_l, qd, ki, ke = _hg_pre(hq_ref[...], hf_ref[...], lbl_ref[...], mc, mlast_ref[...])
        el = jnp.exp(big_l)
        hi = hi_ref[...]
        dov = do_ref[...]
        lo = lax.broadcasted_iota(jnp.int32, (t, LANE), 1) < 64
        mask2 = jnp.concatenate([mc, mc], axis=0) > 0.5
        del_ref[...] = jnp.zeros_like(del_ref)
        dki_parts = []
        for p in range(HEADS // 2):
            sl = slice(LANE * p, LANE * (p + 1))
            vp = hi[:, sl].astype(BF16)
            q2 = _stack_pair(qd[:, sl], lo).astype(BF16)
            kip = ki[:, sl].astype(BF16)
            do2 = _stack_pair(dov[:, sl], lo).astype(BF16)
            a2 = jnp.where(mask2, _dotg(q2, kip, NT), 0.0).astype(BF16)
            da2 = jnp.where(mask2, _dotg(do2, vp, NT), 0.0).astype(BF16)
            dv_ref[:, sl] = _dotg(a2, do2, TN)
            r2 = _dot(da2, kip)
            dqd_ref[:, sl] = jnp.where(lo, r2[:t], r2[t:])
            dki_parts.append(_dotg(da2, q2, TN))
            qb = qd[:, sl].astype(BF16)
            kb = ke[:, sl].astype(BF16)
            dob = dov[:, sl].astype(BF16)
            for c in range(HG_NC - 1, -1, -1):
                rows = slice(HG_BLOCK * c, HG_BLOCK * (c + 1))
                g = g_ref[p]
                gb = g.astype(BF16)
                st = sp_ref[c, :, sl]
                dqd_ref[rows, sl] += _dot(dob[rows], st.astype(BF16))
                dv_ref[rows, sl] += _dotg(kb[rows], gb, NT)
                dke_ref[rows, sl] = _dot(vp[rows], gb)
                last = HG_BLOCK * c + HG_BLOCK - 1
                del_ref[last:last + 1, sl] = jnp.sum(g * st, axis=0, keepdims=True)
                g_ref[p] = g * el[last:last + 1, sl] + _dotg(dob[rows], qb[rows], TN) * bd_ref[...]
        dqd = dqd_ref[...]
        dke = dke_ref[...]
        dki = jnp.concatenate(dki_parts, axis=1)
        eb = jnp.exp(b)
        dhq_ref[...] = (dqd * eb).astype(BF16)
        dhi_ref[...] = dv_ref[...].astype(BF16)
        dke_ke = dke * ke
        db = dqd * qd - dki * ki - dke_ke
        dl_rows = _sel_left(msum_ref[...], dke_ke) + del_ref[...] * el
        is_last = (lax.broadcasted_iota(jnp.int32, (t, HG_WIDTH), 0) & (HG_BLOCK - 1)) == HG_BLOCK - 1
        db = db + jnp.where(is_last, dl_rows, 0.0)
        dlf = _sel_left(mrev_ref[...], db)
        dk = dki * jnp.exp(-b) + dke * jnp.exp(big_l - b)
        df = dlf / f - dk
        dhf_ref[...] = (df * (1.0 - lb) * sig * (1.0 - sig)).astype(BF16)
        dlb = jnp.sum(df * (1.0 - sig), axis=0, keepdims=True) * lb * (1.0 - lb)
        dlbl_ref[0:1, :] += dlb
        dlbl_ref[1:2, :] -= dlb

    rrow = lambda j: pl.BlockSpec((t, HG_WIDTH), lambda i: (nt - 1 - i, j))
    full = lambda a: pl.BlockSpec(a.shape, lambda i: (0, 0))
    return pl.pallas_call(
        body,
        grid=(nt,),
        in_specs=[rrow(5), rrow(6), rrow(7), full(lbl), rrow(0),
                  pl.BlockSpec((HG_NC, LANE, HG_WIDTH), lambda i: (nt - 1 - i, 0, 0)),
                  full(mcum), full(mrev), full(mlast), full(msum), full(bd)],
        out_specs=[rrow(0), rrow(0), rrow(0), pl.BlockSpec((2, HG_WIDTH), lambda i: (0, 0))],
        out_shape=[jax.ShapeDtypeStruct((s, HG_WIDTH), BF16), jax.ShapeDtypeStruct((s, HG_WIDTH), BF16),
                   jax.ShapeDtypeStruct((s, HG_WIDTH), BF16), jax.ShapeDtypeStruct((2, HG_WIDTH), F32)],
        scratch_shapes=[pltpu.VMEM((HEADS // 2, LANE, LANE), F32), pltpu.VMEM((t, HG_WIDTH), F32),
                        pltpu.VMEM((t, HG_WIDTH), F32), pltpu.VMEM((t, HG_WIDTH), F32),
                        pltpu.VMEM((t, HG_WIDTH), F32)],
        compiler_params=_params(("arbitrary",)),
        name="hgrn_bwd",
    )(proj, proj, proj, lbl, do, sprev, mcum, mrev, mlast, msum, bd)


def _tail(x, tgt, proj, attn, o, w_a, w_b, w_out, w_at, w_bt, w_outt, b_gate, g_post, gh):
    s = x.shape[0]
    tm = 128
    ones64 = (jnp.arange(HG_WIDTH)[:, None] // 64 == jnp.arange(HG_WIDTH)[None, :] // 64).astype(BF16)

    def body(x_ref, t_ref, ml_ref, ga_ref, gb_ref, at_ref, o_ref, wa_ref, wb_ref, wo_ref, wat_ref, wbt_ref, wot_ref,
             bg_ref, gp_ref, gh_ref, ones_ref,
             dout_ref, dml_ref, dga_ref, dgb_ref, dat_ref, do_ref, m_ref, dy_ref, ya_ref, dya_ref, yb_ref, dyb_ref,
             loss_ref, dgp_ref, dbg_ref, dgh_ref):
        @pl.when(pl.program_id(0) == 0)
        def _():
            loss_ref[...] = jnp.zeros_like(loss_ref)
            dgp_ref[...] = jnp.zeros_like(dgp_ref)
            dbg_ref[...] = jnp.zeros_like(dbg_ref)
            dgh_ref[...] = jnp.zeros_like(dgh_ref)

        ones = ones_ref[...]
        gate_a = ga_ref[...]
        sa = _sigmoid(gate_a)
        silu_a = gate_a * sa
        attn_v = at_ref[...]
        ya_in = attn_v * silu_a
        ov = o_ref[...]
        ro = lax.rsqrt(_sel_right(ov * ov, ones) * (1.0 / 64.0) + EPS)
        ohat = ov * ro
        ghv = gh_ref[...]
        on = ohat * ghv
        gate_b = gb_ref[...]
        sb = _sigmoid(gate_b)
        silu_b = gate_b * sb
        yb_in = on * silu_b
        ya_bf = ya_in.astype(BF16)
        yb_bf = yb_in.astype(BF16)
        ya_ref[...] = ya_bf
        yb_ref[...] = yb_bf
        y_a = _dot(ya_bf, wa_ref[...])
        y_b = _dot(yb_bf, wb_ref[...])
        gts = _sigmoid(ml_ref[...] + bg_ref[...])
        g_a = gts[:, :D_MODEL]
        g_b = gts[:, D_MODEL:]
        m_bf = (g_a * y_a + g_b * y_b).astype(BF16)
        m_ref[...] = m_bf
        y = _dot(m_bf, wo_ref[...])
        r1 = lax.rsqrt(jnp.mean(y * y, axis=-1, keepdims=True) + EPS)
        yn = y * r1
        gp = gp_ref[...]
        e = x_ref[...] + yn * gp - t_ref[...]
        loss_ref[...] += jnp.sum(e * e, axis=0, keepdims=True)
        dout = e * (1.0 / D_MODEL)
        dout_ref[...] = dout
        dgp_ref[...] += jnp.sum(dout * yn, axis=0, keepdims=True)
        dyn = dout * gp
        dy = r1 * (dyn - yn * jnp.mean(dyn * yn, axis=-1, keepdims=True))
        dy_bf = dy.astype(BF16)
        dy_ref[...] = dy_bf
        dm = _dot(dy_bf, wot_ref[...])
        dml_a = dm * y_a * g_a * (1.0 - g_a)
        dml_b = dm * y_b * g_b * (1.0 - g_b)
        dml_ref[:, :D_MODEL] = dml_a.astype(BF16)
        dml_ref[:, D_MODEL:] = dml_b.astype(BF16)
        dbg_ref[:, :D_MODEL] += jnp.sum(dml_a, axis=0, keepdims=True)
        dbg_ref[:, D_MODEL:] += jnp.sum(dml_b, axis=0, keepdims=True)
        dya_bf = (dm * g_a).astype(BF16)
        dyb_bf = (dm * g_b).astype(BF16)
        dya_ref[...] = dya_bf
        dyb_ref[...] = dyb_bf
        dya_in = _dot(dya_bf, wat_ref[...])
        dyb_in = _dot(dyb_bf, wbt_ref[...])
        dat_ref[...] = dya_in * silu_a
        dga_ref[...] = (dya_in * attn_v * (sa * (1.0 + gate_a * (1.0 - sa)))).astype(BF16)
        don = dyb_in * silu_b
        dgb_ref[...] = (dyb_in * on * (sb * (1.0 + gate_b * (1.0 - sb)))).astype(BF16)
        dgh_ref[...] += jnp.sum(don * ohat, axis=0, keepdims=True)
        dohat = don * ghv
        do_ref[...] = ro * (dohat - ohat * (_sel_right(dohat * ohat, ones) * (1.0 / 64.0)))

    row = lambda w, j: pl.BlockSpec((tm, w), lambda i: (i, j))
    full = lambda a: pl.BlockSpec(a.shape, lambda i: (0, 0))
    acc = lambda w: pl.BlockSpec((1, w), lambda i: (0, 0))
    sds = lambda w, dt: jax.ShapeDtypeStruct((s, w), dt)
    return pl.pallas_call(
        body,
        grid=(s // tm,),
        in_specs=[row(1024, 0), row(1024, 0), row(2048, 0), row(512, 4), row(512, 8), row(512, 0), row(512, 0),
                  full(w_a), full(w_b), full(w_out), full(w_at), full(w_bt), full(w_outt),
                  full(b_gate), full(g_post), full(gh), full(ones64)],
        out_specs=[row(1024, 0), row(2048, 0), row(512, 0), row(512, 0), row(512, 0), row(512, 0),
                   row(1024, 0), row(1024, 0), row(512, 0), row(1024, 0), row(512, 0), row(1024, 0),
                   acc(1024), acc(1024), acc(2048), acc(512)],
        out_shape=[sds(1024, F32), sds(2048, BF16), sds(512, BF16), sds(512, BF16), sds(512, F32), sds(512, F32),
                   sds(1024, BF16), sds(1024, BF16), sds(512, BF16), sds(1024, BF16), sds(512, BF16), sds(1024, BF16),
                   jax.ShapeDtypeStruct((1, 1024), F32), jax.ShapeDtypeStruct((1, 1024), F32),
                   jax.ShapeDtypeStruct((1, 2048), F32), jax.ShapeDtypeStruct((1, 512), F32)],
        compiler_params=_params(("arbitrary",), 56),
        name="tail",
    )(x, tgt, proj, proj, proj, attn, o, w_a, w_b, w_out, w_at, w_bt, w_outt, b_gate, g_post, gh, ones64)


def _mla_bwd(proj, dqr, dkr, dv, g_q, g_kv, w_uq_pt, w_kv_pt, rc, rs1, rs2):
    s = proj.shape[0]
    tm = 256
    scale = 1.0 / math.sqrt(QK)

    def body(cq_ref, ckv_ref, dqr_ref, dkr_ref, dv_ref, gq_ref, gkv_ref, wuqt_ref, wkvt_ref, c_ref, s1_ref, s2_ref,
             dqf_ref, dkvf_ref, dcq_ref, dckv_ref, dkpe_ref, dgq_ref, dgkv_ref):
        @pl.when(pl.program_id(0) == 0)
        def _():
            dgq_ref[...] = jnp.zeros_like(dgq_ref)
            dgkv_ref[...] = jnp.zeros_like(dgkv_ref)

        c, s1, s2 = c_ref[...], s1_ref[...], s2_ref[...]
        lane = lax.broadcasted_iota(jnp.int32, (tm, LANE), 1)
        ksum = jnp.zeros((tm, LANE), F32)
        for h in range(HEADS):
            sl = slice(LANE * h, LANE * (h + 1))
            dqf_ref[:, sl] = (_unrope(dqr_ref[:, sl], c, s1, s2) * scale).astype(BF16)
            dkh = dkr_ref[:, sl]
            ksum = ksum + dkh
            dkvf_ref[:, sl] = jnp.where(lane < NOPE, dkh, 0.0).astype(BF16)
        dkvf_ref[:, HEADS * LANE:] = dv_ref[...].astype(BF16)
        dkpe = _unrope(ksum, c, s1, s2)
        dkpe_ref[...] = jnp.where((lane >= NOPE) & (lane < QK), dkpe, 0.0).astype(BF16)
        dcqn = _dot(dqf_ref[...], wuqt_ref[...])
        dckvn = _dot(dkvf_ref[...], wkvt_ref[...])
        for x_ref, g_ref, dn, dx_ref, dg_ref in ((cq_ref, gq_ref, dcqn, dcq_ref, dgq_ref),
                                                 (ckv_ref, gkv_ref, dckvn, dckv_ref, dgkv_ref)):
            xv = x_ref[...]
            r = lax.rsqrt(jnp.mean(xv * xv, axis=-1, keepdims=True) + EPS)
            xh = xv * r
            dg_ref[...] += jnp.sum(dn * xh, axis=0, keepdims=True)
            dh = dn * g_ref[...]
            dx_ref[...] = (r * (dh - xh * jnp.mean(dh * xh, axis=-1, keepdims=True))).astype(BF16)

    row = lambda w, j: pl.BlockSpec((tm, w), lambda i: (i, j))
    full = lambda a: pl.BlockSpec(a.shape, lambda i: (0, 0))
    acc = lambda w: pl.BlockSpec((1, w), lambda i: (0, 0))
    sds = lambda w, dt: jax.ShapeDtypeStruct((s, w), dt)
    return pl.pallas_call(
        body,
        grid=(s // tm,),
        in_specs=[row(768, 6), row(256, 21), row(1024, 0), row(1024, 0), row(512, 0), full(g_q), full(g_kv),
                  full(w_uq_pt), full(w_kv_pt), row(128, 0), row(128, 0), row(128, 0)],
        out_specs=[row(1024, 0), row(1536, 0), row(768, 0), row(256, 0), row(128, 0), acc(768), acc(256)],
        out_shape=[sds(1024, BF16), sds(1536, BF16), sds(768, BF16), sds(256, BF16), sds(128, BF16),
                   jax.ShapeDtypeStruct((1, 768), F32), jax.ShapeDtypeStruct((1, 256), F32)],
        compiler_params=_params(("arbitrary",)),
        name="mla_bwd",
    )(proj, proj, dqr, dkr, dv, g_q, g_kv, w_uq_pt, w_kv_pt, rc, rs1, rs2)


def _pick(n, options):
    for o in options:
        if n % o == 0:
            return o
    raise ValueError(n)


def _matmul(a, b, name):
    m, k = a.shape
    n = b.shape[1]
    tm = _pick(m, (512, 384, 256))
    tn = _pick(n, (1152, 1024, 768, 512))
    tk = _pick(k, (1024, 512))
    nk = k // tk

    def body(a_ref, b_ref, o_ref):
        @pl.when(pl.program_id(2) == 0)
        def _():
            o_ref[...] = jnp.zeros_like(o_ref)

        o_ref[...] += _dot(a_ref[...], b_ref[...])

    return pl.pallas_call(
        body,
        grid=(m // tm, n // tn, nk),
        in_specs=[pl.BlockSpec((tm, tk), lambda i, j, l: (i, l)), pl.BlockSpec((tk, tn), lambda i, j, l: (l, j))],
        out_specs=pl.BlockSpec((tm, tn), lambda i, j, l: (i, j)),
        out_shape=jax.ShapeDtypeStruct((m, n), F32),
        compiler_params=_params(("arbitrary", "arbitrary", "arbitrary")),
        name=name,
    )(a, b)


def _dh_dx(dproj, w_in_pt, x, dout, g_pre):
    s, k = dproj.shape
    tm, tk = 512, 1152
    nk = k // tk

    def body(dp_ref, w_ref, x_ref, dout_ref, g_ref, dx_ref, dg_ref, acc_ref):
        l = pl.program_id(1)

        @pl.when((pl.program_id(0) == 0) & (l == 0))
        def _():
            dg_ref[...] = jnp.zeros_like(dg_ref)

        @pl.when(l == 0)
        def _():
            acc_ref[...] = jnp.zeros_like(acc_ref)

        acc_ref[...] += _dot(dp_ref[...], w_ref[...])

        @pl.when(l == nk - 1)
        def _():
            dh = acc_ref[...]
            xv = x_ref[...]
            r = lax.rsqrt(jnp.mean(xv * xv, axis=-1, keepdims=True) + EPS)
            xh = xv * r
            dg_ref[...] += jnp.sum(dh * xh, axis=0, keepdims=True)
            dxh = dh * g_ref[...]
            dx_ref[...] = dout_ref[...] + r * (dxh - xh * jnp.mean(dxh * xh, axis=-1, keepdims=True))

    return pl.pallas_call(
        body,
        grid=(s // tm, nk),
        in_specs=[pl.BlockSpec((tm, tk), lambda i, l: (i, l)), pl.BlockSpec((tk, D_MODEL), lambda i, l: (l, 0)),
                  pl.BlockSpec((tm, D_MODEL), lambda i, l: (i, 0)), pl.BlockSpec((tm, D_MODEL), lambda i, l: (i, 0)),
                  pl.BlockSpec((1, D_MODEL), lambda i, l: (0, 0))],
        out_specs=[pl.BlockSpec((tm, D_MODEL), lambda i, l: (i, 0)), pl.BlockSpec((1, D_MODEL), lambda i, l: (0, 0))],
        out_shape=[jax.ShapeDtypeStruct((s, D_MODEL), F32), jax.ShapeDtypeStruct((1, D_MODEL), F32)],
        scratch_shapes=[pltpu.VMEM((tm, D_MODEL), F32)],
        compiler_params=_params(("arbitrary", "arbitrary")),
        name="dh_dx",
    )(dproj, w_in_pt, x, dout, g_pre)


def _rope_tables(s):
    inv = ROPE_THETA ** (-jnp.arange(0, ROPE, 2, dtype=F32) / ROPE)
    ang = jnp.arange(s, dtype=F32)[:, None] * inv[None, :]
    cos, sin = jnp.cos(ang), jnp.sin(ang)
    z = lambda w: jnp.zeros((s, w), F32)
    rc = jnp.concatenate([jnp.ones((s, NOPE), F32), cos, cos, z(32)], axis=1)
    rs1 = jnp.concatenate([z(NOPE), -sin, z(16), z(32)], axis=1)
    rs2 = jnp.concatenate([z(NOPE), z(16), sin, z(32)], axis=1)
    return rc, rs1, rs2


def _local_step(x, tgt, w_in, w_uq, w_ukv, w_a, w_b, w_out, g_pre, b_gate, g_q, g_kv, lbl, g_hgrn, g_post):
    s = x.shape[0]
    zb = lambda r, w: jnp.zeros((r, w), BF16)
    w_in_p = jnp.concatenate([w_in[:, 3616:], w_in[:, 1056:3616], w_in[:, :1024],
                              zb(D_MODEL, 64), w_in[:, 1024:1056], zb(D_MODEL, 32)], axis=1)
    w_uq_p = jnp.pad(w_uq.reshape(Q_LORA, HEADS, QK), ((0, 0), (0, 0), (0, LANE - QK))).reshape(Q_LORA, HEADS * LANE)
    kv3 = w_ukv.reshape(KV_LORA, HEADS, NOPE + VDIM)
    w_kv_p = jnp.concatenate([jnp.pad(kv3[:, :, :NOPE], ((0, 0), (0, 0), (0, LANE - NOPE))).reshape(KV_LORA, HEADS * LANE),
                              kv3[:, :, NOPE:].reshape(KV_LORA, HEADS * VDIM)], axis=1)
    rc, rs1, rs2 = _rope_tables(s)
    gh = jnp.tile(g_hgrn, (1, HEADS))

    proj, h = _norm_proj(x, g_pre, w_in_p)
    qr, kr, v, cqn, ckvn = _mla_prep(proj, g_q, g_kv, w_uq_p, w_kv_p, rc, rs1, rs2)
    attn, lse = _attn_fwd(qr, kr, v)
    o, sprev = _hgrn_fwd(proj, lbl)
    (dout, dml, dga, dgb, dattn, do, m_bf, dy_bf, ya_bf, dya_bf, yb_bf, dyb_bf,
     loss_vec, dg_post, db_gate, dgh) = _tail(x, tgt, proj, attn, o, w_a, w_b, w_out, w_a.T, w_b.T, w_out.T,
                                               b_gate, g_post, gh)
    dqr, dkr, dv = _attn_bwd(qr, kr, v, attn, dattn, lse)
    dhq, dhf, dhi, dlbl = _hgrn_bwd(proj, lbl, do, sprev)
    dqf, dkvf, dcq, dckv, dkpe, dg_q, dg_kv = _mla_bwd(proj, dqr, dkr, dv, g_q, g_kv, w_uq_p.T, w_kv_p.T,
                                                       rc, rs1, rs2)
    dproj = jnp.concatenate([dml, dga, dhq, dhf, dhi, dgb, dcq, dckv, dkpe], axis=1)
    dx, dg_pre = _dh_dx(dproj, w_in_p.T, x, dout, g_pre)

    dw_in_p = _matmul(h.T, dproj, "dw_in")
    dw_out = _matmul(m_bf.T, dy_bf, "dw_out")
    dw_a = _matmul(ya_bf.T, dya_bf, "dw_a")
    dw_b = _matmul(yb_bf.T, dyb_bf, "dw_b")
    dw_uq_p = _matmul(cqn.T, dqf, "dw_uq")
    dw_kv_p = _matmul(ckvn.T, dkvf, "dw_kv")

    dw_in = jnp.concatenate([dw_in_p[:, 4608:5632], dw_in_p[:, 5696:5728], dw_in_p[:, 2048:4608], dw_in_p[:, :2048]],
                            axis=1)
    dw_uq = dw_uq_p.reshape(Q_LORA, HEADS, LANE)[:, :, :QK].reshape(Q_LORA, HEADS * QK)
    dw_ukv = jnp.concatenate([dw_kv_p[:, :HEADS * LANE].reshape(KV_LORA, HEADS, LANE)[:, :, :NOPE],
                              dw_kv_p[:, HEADS * LANE:].reshape(KV_LORA, HEADS, VDIM)], axis=2).reshape(KV_LORA, 1024)
    loss = 0.5 / D_MODEL * jnp.sum(loss_vec)
    grads = dict(g_pre=dg_pre, w_in=dw_in, b_gate=db_gate, g_q=dg_q, w_uq=dw_uq, g_kv=dg_kv, w_ukv=dw_ukv,
                 lb_logits=dlbl, g_hgrn=jnp.sum(dgh.reshape(HEADS, VDIM), axis=0, keepdims=True),
                 w_branch_a=dw_a, w_branch_b=dw_b, w_out=dw_out, g_post=dg_post)
    return loss, dx, grads


def _my_place():
    return lax.axis_index("x"), lax.axis_index("y"), lax.axis_index("c")


def _all_gather(block):
    r, c_ = block.shape

    def body(x_ref, out_ref, send_sems, recv_sems, local_sem):
        x, y, c = _my_place()
        me, sibling = (x, y, c), (x, y, 1 - c)
        chips = [(1 - x, y), (x, 1 - y), (1 - x, 1 - y)]

        def slot(px, py, pc):
            return out_ref.at[4 * px + 2 * py + pc]

        def copy(k, blk, to, src=None):
            return pltpu.make_async_remote_copy(
                src_ref=slot(*blk) if src is None else src, dst_ref=slot(*blk),
                send_sem=send_sems.at[k], recv_sem=recv_sems.at[k], device_id=to, device_id_type=MESH_ID)

        mine = pltpu.make_async_copy(x_ref, slot(*me), local_sem)
        mine.start()
        first = [copy(0, me, sibling, src=x_ref)]
        first += [copy(1 + j, me, (*chip, c), src=x_ref) for j, chip in enumerate(chips)]
        for cp in first:
            cp.start()
        passed = [copy(4 + j, (*chip, c), sibling) for j, chip in enumerate(chips)]
        for j, chip in enumerate(chips):
            copy(1 + j, (*chip, c), me).wait_recv()
            passed[j].start()
        copy(0, sibling, me).wait_recv()
        for j, chip in enumerate(chips):
            copy(4 + j, (*chip, 1 - c), me).wait_recv()
        for cp in first + passed:
            cp.wait_send()
        mine.wait()

    return pl.pallas_call(
        body,
        out_shape=jax.ShapeDtypeStruct((N_DEV, r, c_), block.dtype),
        in_specs=[pl.BlockSpec(memory_space=pl.ANY)],
        out_specs=pl.BlockSpec(memory_space=pl.ANY),
        scratch_shapes=[pltpu.SemaphoreType.DMA((7,)), pltpu.SemaphoreType.DMA((7,)), pltpu.SemaphoreType.DMA],
        name="gather_weights",
    )(block)


def _exchange(send):
    def body(s_ref, r_ref, send_sems, recv_sems, local_sem):
        x, y, c = _my_place()
        me = 4 * x + 2 * y + c
        mine = pltpu.make_async_copy(s_ref.at[me], r_ref.at[me], local_sem)
        mine.start()
        copies = []
        for k in range(N_DEV - 1):
            fx, fy, fc = (k + 1) >> 2 & 1, (k + 1) >> 1 & 1, (k + 1) & 1
            px = 1 - x if fx else x
            py = 1 - y if fy else y
            pc = 1 - c if fc else c
            copies.append(pltpu.make_async_remote_copy(
                src_ref=s_ref.at[4 * px + 2 * py + pc], dst_ref=r_ref.at[me],
                send_sem=send_sems.at[k], recv_sem=recv_sems.at[k], device_id=(px, py, pc), device_id_type=MESH_ID))
        for cp in copies:
            cp.start()
        for cp in copies:
            cp.wait_recv()
        for cp in copies:
            cp.wait_send()
        mine.wait()

    return pl.pallas_call(
        body,
        out_shape=jax.ShapeDtypeStruct(send.shape, send.dtype),
        in_specs=[pl.BlockSpec(memory_space=pl.ANY)],
        out_specs=pl.BlockSpec(memory_space=pl.ANY),
        scratch_shapes=[pltpu.SemaphoreType.DMA((7,)), pltpu.SemaphoreType.DMA((7,)), pltpu.SemaphoreType.DMA],
        name="exchange_grads",
    )(send)


def _sum_adamw(recv, w, m, v):
    rows = recv.shape[1]
    tr = 120
    c1 = 1.0 / (1.0 - ADAM_B1 ** ADAM_STEP)
    c2 = 1.0 / (1.0 - ADAM_B2 ** ADAM_STEP)

    def body(r_ref, w_ref, m_ref, v_ref, g_ref, d_ref, nm_ref, nv_ref):
        g = r_ref[0]
        for k in range(1, N_DEV):
            g = g + r_ref[k]
        g_ref[...] = g
        nm = ADAM_B1 * m_ref[...] + (1.0 - ADAM_B1) * g
        nv = ADAM_B2 * v_ref[...] + (1.0 - ADAM_B2) * (g * g)
        nm_ref[...] = nm
        nv_ref[...] = nv
        d_ref[...] = -ADAM_LR * ((nm * c1) / (jnp.sqrt(nv * c2) + ADAM_EPS) + ADAM_WD * w_ref[...])

    blk = pl.BlockSpec((tr, 1024), lambda i: (i, 0))
    out = jax.ShapeDtypeStruct((rows, 1024), F32)
    return pl.pallas_call(
        body,
        grid=(rows // tr,),
        in_specs=[pl.BlockSpec((N_DEV, tr, 1024), lambda i: (0, i, 0)), blk, blk, blk],
        out_specs=[blk, blk, blk, blk],
        out_shape=[out, out, out, out],
        compiler_params=_params(("arbitrary",)),
        name="sum_adamw",
    )(recv, w, m, v)


BIG = ("w_in", "w_uq", "w_ukv", "w_branch_a", "w_branch_b", "w_out")
SMALL = ("g_pre", "b_gate", "g_q", "g_kv", "lb_logits", "g_hgrn", "g_post")
SMALL_ROWS = (1, 2, 1, 1, 1, 1, 1)
SMALL_SHAPE = dict(g_pre=(1, 1024), b_gate=(1, 2048), g_q=(1, 768), g_kv=(1, 256), lb_logits=(2, 512),
                   g_hgrn=(1, 64), g_post=(1, 1024))
SHARD_SHAPE = dict(w_in=(1024, 708), w_uq=(96, 768), w_ukv=(256, 128), w_branch_a=(512, 128),
                   w_branch_b=(512, 128), w_out=(128, 1024))
COL_SHARDED = dict(w_in=True, w_uq=False, w_ukv=True, w_branch_a=True, w_branch_b=True, w_out=False)


def _pack_shard(t):
    parts = [t[n].reshape(-1, 1024) for n in BIG]
    parts.append(jnp.zeros((PACK_BIG - PACK_OFF[-1], 1024), parts[0].dtype))
    return jnp.concatenate(parts, axis=0)


def _unpack_shard(p):
    return {n: p[PACK_OFF[i]:PACK_OFF[i + 1]].reshape(SHARD_SHAPE[n]) for i, n in enumerate(BIG)}


def _pack_small(t):
    parts = []
    for n, r in zip(SMALL, SMALL_ROWS):
        flat = t[n].reshape(1, -1)
        parts.append(jnp.pad(flat, ((0, 0), (0, r * 1024 - flat.shape[1]))).reshape(r, 1024))
    return jnp.concatenate(parts, axis=0)


def _unpack_small(p):
    out, r0 = {}, 0
    for n, r in zip(SMALL, SMALL_ROWS):
        shp = SMALL_SHAPE[n]
        out[n] = p[r0:r0 + r].reshape(1, -1)[:, :shp[0] * shp[1]].reshape(shp)
        r0 += r
    return out


def _split_full(name, full):
    r, c = full.shape
    if COL_SHARDED[name]:
        return full.reshape(r, N_DEV, c // N_DEV).transpose(1, 0, 2).reshape(N_DEV, -1, 1024)
    return full.reshape(N_DEV, -1, 1024)


def _join_full(name, slots):
    r, c = SHARD_SHAPE[name]
    if COL_SHARDED[name]:
        return slots.reshape(N_DEV, r, c).transpose(1, 0, 2).reshape(r, N_DEV * c)
    return slots.reshape(N_DEV * r, c)


def kernel(x, g_pre, w_in, b_gate, g_q, w_uq, g_kv, w_ukv, lb_logits, g_hgrn, w_branch_a, w_branch_b, w_out, g_post, loss_target, m_g_pre, m_w_in, m_b_gate, m_g_q, m_w_uq, m_g_kv, m_w_ukv, m_lb_logits, m_g_hgrn, m_w_branch_a, m_w_branch_b, m_w_out, m_g_post, v_g_pre, v_w_in, v_b_gate, v_g_q, v_w_uq, v_g_kv, v_w_ukv, v_lb_logits, v_g_hgrn, v_w_branch_a, v_w_branch_b, v_w_out, v_g_post):
    w = dict(w_in=w_in[0], w_uq=w_uq[0], w_ukv=w_ukv[0], w_branch_a=w_branch_a[0], w_branch_b=w_branch_b[0],
             w_out=w_out[0], g_pre=g_pre, b_gate=b_gate, g_q=g_q, g_kv=g_kv, lb_logits=lb_logits, g_hgrn=g_hgrn,
             g_post=g_post)
    mom = dict(w_in=m_w_in[0], w_uq=m_w_uq[0], w_ukv=m_w_ukv[0], w_branch_a=m_w_branch_a[0],
               w_branch_b=m_w_branch_b[0], w_out=m_w_out[0], g_pre=m_g_pre, b_gate=m_b_gate, g_q=m_g_q, g_kv=m_g_kv,
               lb_logits=m_lb_logits, g_hgrn=m_g_hgrn, g_post=m_g_post)
    var = dict(w_in=v_w_in[0], w_uq=v_w_uq[0], w_ukv=v_w_ukv[0], w_branch_a=v_w_branch_a[0],
               w_branch_b=v_w_branch_b[0], w_out=v_w_out[0], g_pre=v_g_pre, b_gate=v_b_gate, g_q=v_g_q, g_kv=v_g_kv,
               lb_logits=v_lb_logits, g_hgrn=v_g_hgrn, g_post=v_g_post)

    gathered = _all_gather(_pack_shard(w).astype(BF16))
    full = {n: _join_full(n, gathered[:, PACK_OFF[i]:PACK_OFF[i + 1]]) for i, n in enumerate(BIG)}

    loss, dx, grads = _local_step(x[0], loss_target[0], full["w_in"], full["w_uq"], full["w_ukv"],
                                  full["w_branch_a"], full["w_branch_b"], full["w_out"],
                                  g_pre, b_gate, g_q, g_kv, lb_logits, g_hgrn, g_post)

    small = jnp.broadcast_to(_pack_small(grads)[None], (N_DEV, PACK_ALL - PACK_BIG, 1024))
    send = jnp.concatenate([_split_full(n, grads[n]) for n in BIG]
                           + [jnp.zeros((N_DEV, PACK_BIG - PACK_OFF[-1], 1024), F32), small], axis=1)
    recv = _exchange(send)

    pack = lambda t: jnp.concatenate([_pack_shard(t), _pack_small(t)], axis=0)
    g_p, d_p, m_p, v_p = _sum_adamw(recv, pack(w), pack(mom), pack(var))

    def unpack(p, lead):
        t = {**_unpack_shard(p[:PACK_BIG]), **_unpack_small(p[PACK_BIG:])}
        return [t[n][None] if (lead and n in BIG) else t[n] for n in
                ("g_pre", "w_in", "b_gate", "g_q", "w_uq", "g_kv", "w_ukv", "lb_logits", "g_hgrn",
                 "w_branch_a", "w_branch_b", "w_out", "g_post")]

    total = lax.psum(loss, ("x", "y", "c"))
    return (total, dx[None], *unpack(g_p, True), *unpack(d_p, True), *unpack(m_p, True), *unpack(v_p, True))
```

```python
import math

import jax
import jax.numpy as jnp
from jax import lax
from jax.experimental import pallas as pl
from jax.experimental.pallas import tpu as pltpu

F32, BF16 = jnp.float32, jnp.bfloat16

D_MODEL = 1024
EPS = 1e-6
HEADS = 8
NOPE, ROPE, VDIM = 64, 32, 64
QK = NOPE + ROPE
Q_LORA, KV_LORA = 768, 256
ROPE_THETA = 10000.0
ATT_CHUNK_SHIFT = 6
HG_BLOCK = 32
HG_WIDTH = 512
D_IN = 5664
D_IN_PAD = 5760
N_DEV = 8
LANE = 128

ADAM_LR, ADAM_B1, ADAM_B2, ADAM_EPS, ADAM_WD, ADAM_STEP = 0.001, 0.9, 0.999, 1e-08, 0.01, 10

PACK_ROWS = (708, 72, 32, 64, 64, 128)
PACK_OFF = (0, 708, 780, 812, 876, 940, 1068)
PACK_BIG = 1072
PACK_ALL = 1080

NT = (((1,), (1,)), ((), ()))
TN = (((0,), (0,)), ((), ()))
MESH_ID = pl.DeviceIdType.MESH


def _params(sem, vmem_mb=48):
    return pltpu.CompilerParams(dimension_semantics=sem, vmem_limit_bytes=vmem_mb * 2**20)


def _dot(a, b):
    return jnp.dot(a, b, preferred_element_type=F32)


def _dotg(a, b, dims):
    return lax.dot_general(a, b, dims, preferred_element_type=F32)


def _split3(x):
    hi = x.astype(BF16)
    r = x - hi.astype(F32)
    mid = r.astype(BF16)
    lo = (r - mid.astype(F32)).astype(BF16)
    return hi, mid, lo


def _sel_left(m01, x):
    hi, mid, lo = _split3(x)
    return _dot(m01, hi) + _dot(m01, mid) + _dot(m01, lo)


def _sel_right(x, m01):
    hi, mid, lo = _split3(x)
    return _dot(hi, m01) + _dot(mid, m01) + _dot(lo, m01)


def _sigmoid(x):
    return 1.0 / (1.0 + jnp.exp(-x))


def _rope(x, c, s1, s2):
    return x * c + pltpu.roll(x, 112, 1) * s1 + pltpu.roll(x, 16, 1) * s2


def _unrope(d, c, s1, s2):
    return d * c + pltpu.roll(d * s1, 16, 1) + pltpu.roll(d * s2, 112, 1)


def _norm_proj(x, g_pre, w):
    s, n = x.shape[0], w.shape[1]
    tm, tn = 512, 1152

    def body(x_ref, g_ref, w_ref, proj_ref, h_ref):
        @pl.when(pl.program_id(1) == 0)
        def _():
            xv = x_ref[...]
            r = lax.rsqrt(jnp.mean(xv * xv, axis=-1, keepdims=True) + EPS)
            h_ref[...] = (xv * r * g_ref[...]).astype(BF16)

        proj_ref[...] = _dot(h_ref[...], w_ref[...])

    return pl.pallas_call(
        body,
        grid=(s // tm, n // tn),
        in_specs=[
            pl.BlockSpec((tm, D_MODEL), lambda i, j: (i, 0)),
            pl.BlockSpec((1, D_MODEL), lambda i, j: (0, 0)),
            pl.BlockSpec((D_MODEL, tn), lambda i, j: (0, j)),
        ],
        out_specs=[
            pl.BlockSpec((tm, tn), lambda i, j: (i, j)),
            pl.BlockSpec((tm, D_MODEL), lambda i, j: (i, 0)),
        ],
        out_shape=[jax.ShapeDtypeStruct((s, n), F32), jax.ShapeDtypeStruct((s, D_MODEL), BF16)],
        compiler_params=_params(("arbitrary", "arbitrary")),
        name="norm_proj",
    )(x, g_pre, w)


def _mla_prep(proj, g_q, g_kv, w_uq_p, w_kv_p, rc, rs1, rs2):
    s = proj.shape[0]
    tm = 256
    scale = 1.0 / math.sqrt(QK)

    def body(cq_ref, ckv_ref, kpe_ref, gq_ref, gkv_ref, wuq_ref, wkv_ref, c_ref, s1_ref, s2_ref,
             qr_ref, kr_ref, v_ref, cqn_ref, ckvn_ref):
        cq = cq_ref[...]
        r = lax.rsqrt(jnp.mean(cq * cq, axis=-1, keepdims=True) + EPS)
        cqn = (cq * r * gq_ref[...]).astype(BF16)
        cqn_ref[...] = cqn
        q = _dot(cqn, wuq_ref[...])
        ckv = ckv_ref[...]
        r = lax.rsqrt(jnp.mean(ckv * ckv, axis=-1, keepdims=True) + EPS)
        ckvn = (ckv * r * gkv_ref[...]).astype(BF16)
        ckvn_ref[...] = ckvn
        kv = _dot(ckvn, wkv_ref[...])
        c, s1, s2 = c_ref[...], s1_ref[...], s2_ref[...]
        lane = lax.broadcasted_iota(jnp.int32, (tm, LANE), 1)
        kpe = _rope(kpe_ref[...], c, s1, s2) + jnp.where((lane == QK) | (lane == QK + 1), 1.0, 0.0)
        vone = jnp.where((lane == VDIM) | (lane == VDIM + 1), 1.0, 0.0)
        for h in range(HEADS):
            sl = slice(LANE * h, LANE * (h + 1))
            qr_ref[:, sl] = (_rope(q[:, sl], c, s1, s2) * scale).astype(BF16)
            kr_ref[:, sl] = (kv[:, sl] + kpe).astype(BF16)
            v_ref[:, sl] = (kv[:, HEADS * LANE + LANE * h:HEADS * LANE + LANE * (h + 1)] + vone).astype(BF16)

    row = lambda w, j: pl.BlockSpec((tm, w), lambda i: (i, j))
    full = lambda a: pl.BlockSpec(a.shape, lambda i: (0, 0))
    return pl.pallas_call(
        body,
        grid=(s // tm,),
        in_specs=[row(768, 6), row(256, 21), row(128, 44), full(g_q), full(g_kv), full(w_uq_p), full(w_kv_p),
                  row(128, 0), row(128, 0), row(128, 0)],
        out_specs=[row(1024, 0), row(1024, 0), row(1024, 0), row(768, 0), row(256, 0)],
        out_shape=[jax.ShapeDtypeStruct((s, 1024), BF16), jax.ShapeDtypeStruct((s, 1024), BF16),
                   jax.ShapeDtypeStruct((s, 1024), BF16), jax.ShapeDtypeStruct((s, 768), BF16),
                   jax.ShapeDtypeStruct((s, 256), BF16)],
        compiler_params=_params(("arbitrary",)),
        name="mla_prep",
    )(proj, proj, proj, g_q, g_kv, w_uq_p, w_kv_p, rc, rs1, rs2)


ATT_T = 512


def _chunk_mask(transposed):
    r = lax.broadcasted_iota(jnp.int32, (ATT_T, ATT_T), 0) >> ATT_CHUNK_SHIFT
    c = lax.broadcasted_iota(jnp.int32, (ATT_T, ATT_T), 1) >> ATT_CHUNK_SHIFT
    return (r <= c) if transposed else (c <= r)


def _hi_lo(x):
    hi = x.astype(BF16).astype(F32)
    return hi, x - hi


def _attn_fwd(qr, kr, vp):
    s = qr.shape[0]
    t = ATT_T

    def body(q_ref, k_ref, v_ref, o_ref, qa_ref):
        qi = pl.program_id(1)
        lane = lax.broadcasted_iota(jnp.int32, (t, LANE), 1)
        sls = [slice(LANE * a, LANE * (a + 1)) for a in range(2)]
        qs = [q_ref[:, sl] for sl in sls]

        def step(j, carry, masked):
            rows = pl.ds(pl.multiple_of(j * t, t), t)
            out = []
            for a in range(2):
                m, acc = carry[a]
                sc = _dotg(qs[a], k_ref[rows, sls[a]], NT)
                if masked:
                    sc = jnp.where(_chunk_mask(False), sc, -1e30)
                m_new = jnp.maximum(m, jnp.max(sc, axis=-1, keepdims=True))
                p = jnp.exp(sc - m_new).astype(BF16)
                acc = jnp.exp(m - m_new) * acc + _dot(p, v_ref[rows, sls[a]])
                out.append((m_new, acc))
            return tuple(out)

        init = tuple((jnp.full((t, 1), -1e30, F32), jnp.zeros((t, LANE), F32)) for _ in range(2))
        carry = lax.fori_loop(0, qi, lambda j, c: step(j, c, False), init)
        carry = step(qi, carry, True)
        outs = []
        for a in range(2):
            m, acc = carry[a]
            l = acc[:, VDIM:VDIM + 1]
            outs.append(acc / l)
            hi, lo_part = _hi_lo(-(m + jnp.log(l)))
            qa = jnp.where(lane == QK, hi, jnp.where(lane == QK + 1, lo_part, qs[a].astype(F32)))
            qa_ref[:, sls[a]] = qa.astype(BF16)
        o_ref[...] = jnp.where(lane < VDIM, outs[0], pltpu.roll(outs[1], VDIM, 1))

    return pl.pallas_call(
        body,
        grid=(HEADS // 2, s // t),
        in_specs=[
            pl.BlockSpec((t, 2 * LANE), lambda h, i: (i, h)),
            pl.BlockSpec((s, 2 * LANE), lambda h, i: (0, h)),
            pl.BlockSpec((s, 2 * LANE), lambda h, i: (0, h)),
        ],
        out_specs=[
            pl.BlockSpec((t, LANE), lambda h, i: (i, h)),
            pl.BlockSpec((t, 2 * LANE), lambda h, i: (i, h)),
        ],
        out_shape=[jax.ShapeDtypeStruct((s, 512), F32), jax.ShapeDtypeStruct((s, 1024), BF16)],
        compiler_params=_params(("arbitrary", "arbitrary")),
        name="attn_fwd",
    )(qr, kr, vp)


def _attn_bwd(qa, kr, vp, dop):
    s = qa.shape[0]
    t = ATT_T
    nq = s // t

    def body(q_ref, k_ref, v_ref, do_ref, dq_ref, dk_ref, dv_ref):
        j = pl.program_id(1)
        sls = [slice(LANE * a, LANE * (a + 1)) for a in range(2)]

        @pl.when(j == 0)
        def _():
            dq_ref[...] = jnp.zeros_like(dq_ref)

        dk_ref[...] = jnp.zeros_like(dk_ref)
        dv_ref[...] = jnp.zeros_like(dv_ref)
        ks = [k_ref[:, sl] for sl in sls]
        vs = [v_ref[:, sl] for sl in sls]

        def step(i, masked):
            rows = pl.ds(pl.multiple_of(i * t, t), t)
            for a in range(2):
                q = q_ref[rows, sls[a]]
                do = do_ref[rows, sls[a]]
                sc = _dotg(ks[a], q, NT)
                if masked:
                    sc = jnp.where(_chunk_mask(True), sc, -1e30)
                p = jnp.exp(sc)
                ds = (p * _dotg(vs[a], do, NT)).astype(BF16)
                dv_ref[:, sls[a]] += _dot(p.astype(BF16), do)
                dk_ref[:, sls[a]] += _dot(ds, q)
                dq_ref[rows, sls[a]] += _dotg(ds, ks[a], TN)

        step(j, True)

        def loop(i, c):
            step(i, False)
            return c

        lax.fori_loop(j + 1, nq, loop, 0)

    blk = pl.BlockSpec((t, 2 * LANE), lambda h, j: (j, h))
    whole = pl.BlockSpec((s, 2 * LANE), lambda h, j: (0, h))
    out = jax.ShapeDtypeStruct((s, 1024), F32)
    return pl.pallas_call(
        body,
        grid=(HEADS // 2, nq),
        in_specs=[whole, blk, blk, whole],
        out_specs=[whole, blk, blk],
        out_shape=[out, out, out],
        compiler_params=_params(("arbitrary", "arbitrary")),
        name="attn_bwd",
    )(qa, kr, vp, dop)


HG_T = 256
HG_NC = HG_T // HG_BLOCK


def _hg_consts():
    r = jnp.arange(HG_T)[:, None]
    c = jnp.arange(HG_T)[None, :]
    same = (r // HG_BLOCK) == (c // HG_BLOCK)
    mcum = (same & (c <= r)).astype(BF16)
    mrev = (same & (c >= r)).astype(BF16)
    mlast = (c == (r // HG_BLOCK) * HG_BLOCK + HG_BLOCK - 1).astype(BF16)
    msum = same.astype(BF16)
    a = jnp.arange(LANE)
    bd = ((a[:, None] < 64) == (a[None, :] < 64)).astype(F32)
    return mcum, mrev, mlast, msum, bd


def _hg_pre(hq, hf, lbl, mcum, mlast):
    lb = _sigmoid(lbl[0:1, :] - lbl[1:2, :])
    sig = _sigmoid(hf)
    f = lb + (1.0 - lb) * sig
    b = _sel_left(mcum, jnp.log(f))
    big_l = _sel_left(mlast, b)
    k = 1.0 - f
    qd = hq * jnp.exp(b)
    ki = k * jnp.exp(-b)
    ke = k * jnp.exp(big_l - b)
    return lb, sig, f, b, big_l, qd, ki, ke


def _stack_pair(xp, lo):
    return jnp.concatenate([jnp.where(lo, xp, 0.0), jnp.where(lo, 0.0, xp)], axis=0)


def _hgrn_fwd(proj, lbl):
    s = proj.shape[0]
    t = HG_T
    mcum, _, mlast, _, bd = _hg_consts()

    def body(hq_ref, hf_ref, hi_ref, lbl_ref, mcum_ref, mlast_ref, bd_ref, o_ref, sp_ref, st_ref):
        @pl.when(pl.program_id(0) == 0)
        def _():
            st_ref[...] = jnp.zeros_like(st_ref)

        mc = mcum_ref[...]
        _, _, _, _, big_l, qd, ki, ke = _hg_pre(hq_ref[...], hf_ref[...], lbl_ref[...], mc, mlast_ref[...])
        el = jnp.exp(big_l)
        hi = hi_ref[...]
        lo = lax.broadcasted_iota(jnp.int32, (t, LANE), 1) < 64
        mask2 = jnp.concatenate([mc, mc], axis=0) > 0.5
        for p in range(HEADS // 2):
            sl = slice(LANE * p, LANE * (p + 1))
            vp = hi[:, sl].astype(BF16)
            q2 = _stack_pair(qd[:, sl], lo).astype(BF16)
            a2 = jnp.where(mask2, _dotg(q2, ki[:, sl].astype(BF16), NT), 0.0)
            r2 = _dot(a2.astype(BF16), vp)
            o_intra = jnp.where(lo, r2[:t], r2[t:])
            qb = qd[:, sl].astype(BF16)
            kb = ke[:, sl].astype(BF16)
            for c in range(HG_NC):
                rows = slice(HG_BLOCK * c, HG_BLOCK * (c + 1))
                st = st_ref[p]
                sp_ref[c, :, sl] = st
                o_ref[rows, sl] = o_intra[rows] + _dotg(qb[rows], st.astype(BF16), NT)
                u = _dotg(vp[rows], kb[rows], TN) * bd_ref[...]
                st_ref[p] = st * el[HG_BLOCK * c:HG_BLOCK * c + 1, sl] + u

    row = lambda j: pl.BlockSpec((t, HG_WIDTH), lambda i: (i, j))
    full = lambda a: pl.BlockSpec(a.shape, lambda i: (0, 0))
    return pl.pallas_call(
        body,
        grid=(s // t,),
        in_specs=[row(5), row(6), row(7), full(lbl), full(mcum), full(mlast), full(bd)],
        out_specs=[row(0), pl.BlockSpec((HG_NC, LANE, HG_WIDTH), lambda i: (i, 0, 0))],
        out_shape=[jax.ShapeDtypeStruct((s, HG_WIDTH), F32),
                   jax.ShapeDtypeStruct((s // HG_BLOCK, LANE, HG_WIDTH), F32)],
        scratch_shapes=[pltpu.VMEM((HEADS // 2, LANE, LANE), F32)],
        compiler_params=_params(("arbitrary",)),
        name="hgrn_fwd",
    )(proj, proj, proj, lbl, mcum, mlast, bd)


def _hgrn_bwd(proj, lbl, do, sprev):
    s = proj.shape[0]
    t = HG_T
    nt = s // t
    mcum, mrev, mlast, msum, bd = _hg_consts()

    def body(hq_ref, hf_ref, hi_ref, lbl_ref, do_ref, sp_ref, mcum_ref, mrev_ref, mlast_ref, msum_ref, bd_ref,
             dhq_ref, dhf_ref, dhi_ref, dlbl_ref, g_ref, dqd_ref, dke_ref, dv_ref, del_ref):
        @pl.when(pl.program_id(0) == 0)
        def _():
            g_ref[...] = jnp.zeros_like(g_ref)
            dlbl_ref[...] = jnp.zeros_like(dlbl_ref)

        mc = mcum_ref[...]
        lb, sig, f, b, big_l, qd, ki, ke = _hg_pre(hq_ref[...], hf_ref[...], lbl_ref[...], mc, mlast_ref[...])
        el = jnp.exp(big_l)
        hi = hi_ref[...]
        dov = do_ref[...]
        lo = lax.broadcasted_iota(jnp.int32, (t, LANE), 1) < 64
        mask2 = jnp.concatenate([mc, mc], axis=0) > 0.5
        del_ref[...] = jnp.zeros_like(del_ref)
        dki_parts = []
        for p in range(HEADS // 2):
            sl = slice(LANE * p, LANE * (p + 1))
            vp = hi[:, sl].astype(BF16)
            q2 = _stack_pair(qd[:, sl], lo).astype(BF16)
            kip = ki[:, sl].astype(BF16)
            do2 = _stack_pair(dov[:, sl], lo).astype(BF16)
            a2 = jnp.where(mask2, _dotg(q2, kip, NT), 0.0).astype(BF16)
            da2 = jnp.where(mask2, _dotg(do2, vp, NT), 0.0).astype(BF16)
            dv_ref[:, sl] = _dotg(a2, do2, TN)
            r2 = _dot(da2, kip)
            dqd_ref[:, sl] = jnp.where(lo, r2[:t], r2[t:])
            dki_parts.append(_dotg(da2, q2, TN))
            qb = qd[:, sl].astype(BF16)
            kb = ke[:, sl].astype(BF16)
            dob = dov[:, sl].astype(BF16)
            for c in range(HG_NC - 1, -1, -1):
                rows = slice(HG_BLOCK * c, HG_BLOCK * (c + 1))
                g = g_ref[p]
                gb = g.astype(BF16)
                st = sp_ref[c, :, sl]
                dqd_ref[rows, sl] += _dot(dob[rows], st.astype(BF16))
                dv_ref[rows, sl] += _dotg(kb[rows], gb, NT)
                dke_ref[rows, sl] = _dot(vp[rows], gb)
                last = HG_BLOCK * c + HG_BLOCK - 1
                del_ref[last:last + 1, sl] = jnp.sum(g * st, axis=0, keepdims=True)
                g_ref[p] = g * el[last:last + 1, sl] + _dotg(dob[rows], qb[rows], TN) * bd_ref[...]
        dqd = dqd_ref[...]
        dke = dke_ref[...]
        dki = jnp.concatenate(dki_parts, axis=1)
        eb = jnp.exp(b)
        dhq_ref[...] = (dqd * eb).astype(BF16)
        dhi_ref[...] = dv_ref[...].astype(BF16)
        dke_ke = dke * ke
        db = dqd * qd - dki * ki - dke_ke
        dl_rows = _sel_left(msum_ref[...], dke_ke) + del_ref[...] * el
        is_last = (lax.broadcasted_iota(jnp.int32, (t, HG_WIDTH), 0) & (HG_BLOCK - 1)) == HG_BLOCK - 1
        db = db + jnp.where(is_last, dl_rows, 0.0)
        dlf = _sel_left(mrev_ref[...], db)
        dk = dki * jnp.exp(-b) + dke * jnp.exp(big_l - b)
        df = dlf / f - dk
        dhf_ref[...] = (df * (1.0 - lb) * sig * (1.0 - sig)).astype(BF16)
        dlb = jnp.sum(df * (1.0 - sig), axis=0, keepdims=True) * lb * (1.0 - lb)
        dlbl_ref[0:1, :] += dlb
        dlbl_ref[1:2, :] -= dlb

    rrow = lambda j: pl.BlockSpec((t, HG_WIDTH), lambda i: (nt - 1 - i, j))
    full = lambda a: pl.BlockSpec(a.shape, lambda i: (0, 0))
    return pl.pallas_call(
        body,
        grid=(nt,),
        in_specs=[rrow(5), rrow(6), rrow(7), full(lbl), rrow(0),
                  pl.BlockSpec((HG_NC, LANE, HG_WIDTH), lambda i: (nt - 1 - i, 0, 0)),
                  full(mcum), full(mrev), full(mlast), full(msum), full(bd)],
        out_specs=[rrow(0), rrow(0), rrow(0), pl.BlockSpec((2, HG_WIDTH), lambda i: (0, 0))],
        out_shape=[jax.ShapeDtypeStruct((s, HG_WIDTH), BF16), jax.ShapeDtypeStruct((s, HG_WIDTH), BF16),
                   jax.ShapeDtypeStruct((s, HG_WIDTH), BF16), jax.ShapeDtypeStruct((2, HG_WIDTH), F32)],
        scratch_shapes=[pltpu.VMEM((HEADS // 2, LANE, LANE), F32), pltpu.VMEM((t, HG_WIDTH), F32),
                        pltpu.VMEM((t, HG_WIDTH), F32), pltpu.VMEM((t, HG_WIDTH), F32),
                        pltpu.VMEM((t, HG_WIDTH), F32)],
        compiler_params=_params(("arbitrary",)),
        name="hgrn_bwd",
    )(proj, proj, proj, lbl, do, sprev, mcum, mrev, mlast, msum, bd)


def _tail(x, tgt, proj, attn, o, w_a, w_b, w_out, w_at, w_bt, w_outt, b_gate, g_post, gh):
    s = x.shape[0]
    tm = 128
    ones64 = (jnp.arange(HG_WIDTH)[:, None] // 64 == jnp.arange(HG_WIDTH)[None, :] // 64).astype(BF16)

    def body(x_ref, t_ref, ml_ref, ga_ref, gb_ref, at_ref, o_ref, wa_ref, wb_ref, wo_ref, wat_ref, wbt_ref, wot_ref,
             bg_ref, gp_ref, gh_ref, ones_ref,
             dout_ref, dml_ref, dga_ref, dgb_ref, dat_ref, do_ref, m_ref, dy_ref, ya_ref, dya_ref, yb_ref, dyb_ref,
             loss_ref, dgp_ref, dbg_ref, dgh_ref):
        @pl.when(pl.program_id(0) == 0)
        def _():
            loss_ref[...] = jnp.zeros_like(loss_ref)
            dgp_ref[...] = jnp.zeros_like(dgp_ref)
            dbg_ref[...] = jnp.zeros_like(dbg_ref)
            dgh_ref[...] = jnp.zeros_like(dgh_ref)

        ones = ones_ref[...]
        gate_a = ga_ref[...]
        sa = _sigmoid(gate_a)
        silu_a = gate_a * sa
        attn_v = at_ref[...]
        ya_in = attn_v * silu_a
        ov = o_ref[...]
        ro = lax.rsqrt(_sel_right(ov * ov, ones) * (1.0 / 64.0) + EPS)
        ohat = ov * ro
        ghv = gh_ref[...]
        on = ohat * ghv
        gate_b = gb_ref[...]
        sb = _sigmoid(gate_b)
        silu_b = gate_b * sb
        yb_in = on * silu_b
        ya_bf = ya_in.astype(BF16)
        yb_bf = yb_in.astype(BF16)
        ya_ref[...] = ya_bf
        yb_ref[...] = yb_bf
        y_a = _dot(ya_bf, wa_ref[...])
        y_b = _dot(yb_bf, wb_ref[...])
        gts = _sigmoid(ml_ref[...] + bg_ref[...])
        g_a = gts[:, :D_MODEL]
        g_b = gts[:, D_MODEL:]
        m_bf = (g_a * y_a + g_b * y_b).astype(BF16)
        m_ref[...] = m_bf
        y = _dot(m_bf, wo_ref[...])
        r1 = lax.rsqrt(jnp.mean(y * y, axis=-1, keepdims=True) + EPS)
        yn = y * r1
        gp = gp_ref[...]
        e = x_ref[...] + yn * gp - t_ref[...]
        loss_ref[...] += jnp.sum(e * e, axis=0, keepdims=True)
        dout = e * (1.0 / D_MODEL)
        dout_ref[...] = dout
        dgp_ref[...] += jnp.sum(dout * yn, axis=0, keepdims=True)
        dyn = dout * gp
        dy = r1 * (dyn - yn * jnp.mean(dyn * yn, axis=-1, keepdims=True))
        dy_bf = dy.astype(BF16)
        dy_ref[...] = dy_bf
        dm = _dot(dy_bf, wot_ref[...])
        dml_a = dm * y_a * g_a * (1.0 - g_a)
        dml_b = dm * y_b * g_b * (1.0 - g_b)
        dml_ref[:, :D_MODEL] = dml_a.astype(BF16)
        dml_ref[:, D_MODEL:] = dml_b.astype(BF16)
        dbg_ref[:, :D_MODEL] += jnp.sum(dml_a, axis=0, keepdims=True)
        dbg_ref[:, D_MODEL:] += jnp.sum(dml_b, axis=0, keepdims=True)
        dya_bf = (dm * g_a).astype(BF16)
        dyb_bf = (dm * g_b).astype(BF16)
        dya_ref[...] = dya_bf
        dyb_ref[...] = dyb_bf
        dya_in = _dot(dya_bf, wat_ref[...])
        dyb_in = _dot(dyb_bf, wbt_ref[...])
        dattn = dya_in * silu_a
        delta = _sel_right(dattn * attn_v, ones)
        lane = lax.broadcasted_iota(jnp.int32, (tm, LANE), 1)
        for p in range(HEADS // 2):
            sl = slice(LANE * p, LANE * (p + 1))
            xs = (dattn[:, sl], pltpu.roll(dattn[:, sl], VDIM, 1))
            nds = (-pltpu.roll(delta[:, sl], VDIM, 1), -delta[:, sl])
            for a in range(2):
                hi, lo_part = _hi_lo(nds[a])
                blk = jnp.where(lane < VDIM, xs[a], jnp.where(lane == VDIM, hi, jnp.where(lane == VDIM + 1, lo_part, 0.0)))
                dat_ref[:, LANE * (2 * p + a):LANE * (2 * p + a + 1)] = blk.astype(BF16)
        dga_ref[...] = (dya_in * attn_v * (sa * (1.0 + gate_a * (1.0 - sa)))).astype(BF16)
        don = dyb_in * silu_b
        dgb_ref[...] = (dyb_in * on * (sb * (1.0 + gate_b * (1.0 - sb)))).astype(BF16)
        dgh_ref[...] += jnp.sum(don * ohat, axis=0, keepdims=True)
        dohat = don * ghv
        do_ref[...] = ro * (dohat - ohat * (_sel_right(dohat * ohat, ones) * (1.0 / 64.0)))

    row = lambda w, j: pl.BlockSpec((tm, w), lambda i: (i, j))
    full = lambda a: pl.BlockSpec(a.shape, lambda i: (0, 0))
    acc = lambda w: pl.BlockSpec((1, w), lambda i: (0, 0))
    sds = lambda w, dt: jax.ShapeDtypeStruct((s, w), dt)
    return pl.pallas_call(
        body,
        grid=(s // tm,),
        in_specs=[row(1024, 0), row(1024, 0), row(2048, 0), row(512, 4), row(512, 8), row(512, 0), row(512, 0),
                  full(w_a), full(w_b), full(w_out), full(w_at), full(w_bt), full(w_outt),
                  full(b_gate), full(g_post), full(gh), full(ones64)],
        out_specs=[row(1024, 0), row(2048, 0), row(512, 0), row(512, 0), row(1024, 0), row(512, 0),
                   row(1024, 0), row(1024, 0), row(512, 0), row(1024, 0), row(512, 0), row(1024, 0),
                   acc(1024), acc(1024), acc(2048), acc(512)],
        out_shape=[sds(1024, F32), sds(2048, BF16), sds(512, BF16), sds(512, BF16), sds(1024, BF16), sds(512, F32),
                   sds(1024, BF16), sds(1024, BF16), sds(512, BF16), sds(1024, BF16), sds(512, BF16), sds(1024, BF16),
                   jax.ShapeDtypeStruct((1, 1024), F32), jax.ShapeDtypeStruct((1, 1024), F32),
                   jax.ShapeDtypeStruct((1, 2048), F32), jax.ShapeDtypeStruct((1, 512), F32)],
        compiler_params=_params(("arbitrary",), 56),
        name="tail",
    )(x, tgt, proj, proj, proj, attn, o, w_a, w_b, w_out, w_at, w_bt, w_outt, b_gate, g_post, gh, ones64)


def _mla_bwd(proj, dqr, dkr, dv, g_q, g_kv, w_uq_pt, w_kv_pt, rc, rs1, rs2):
    s = proj.shape[0]
    tm = 256
    scale = 1.0 / math.sqrt(QK)

    def body(cq_ref, ckv_ref, dqr_ref, dkr_ref, dv_ref, gq_ref, gkv_ref, wuqt_ref, wkvt_ref, c_ref, s1_ref, s2_ref,
             dqf_ref, dkvf_ref, dcq_ref, dckv_ref, dkpe_ref, dgq_ref, dgkv_ref):
        @pl.when(pl.program_id(0) == 0)
        def _():
            dgq_ref[...] = jnp.zeros_like(dgq_ref)
            dgkv_ref[...] = jnp.zeros_like(dgkv_ref)

        c, s1, s2 = c_ref[...], s1_ref[...], s2_ref[...]
        lane = lax.broadcasted_iota(jnp.int32, (tm, LANE), 1)
        ksum = jnp.zeros((tm, LANE), F32)
        for h in range(HEADS):
            sl = slice(LANE * h, LANE * (h + 1))
            dqf_ref[:, sl] = (_unrope(dqr_ref[:, sl], c, s1, s2) * scale).astype(BF16)
            dkh = dkr_ref[:, sl]
            ksum = ksum + dkh
            dkvf_ref[:, sl] = jnp.where(lane < NOPE, dkh, 0.0).astype(BF16)
            dkvf_ref[:, HEADS * LANE + LANE * h:HEADS * LANE + LANE * (h + 1)] = jnp.where(
                lane < VDIM, dv_ref[:, sl], 0.0).astype(BF16)
        dkpe = _unrope(ksum, c, s1, s2)
        dkpe_ref[...] = jnp.where((lane >= NOPE) & (lane < QK), dkpe, 0.0).astype(BF16)
        dcqn = _dot(dqf_ref[...], wuqt_ref[...])
        dckvn = _dot(dkvf_ref[...], wkvt_ref[...])
        for x_ref, g_ref, dn, dx_ref, dg_ref in ((cq_ref, gq_ref, dcqn, dcq_ref, dgq_ref),
                                                 (ckv_ref, gkv_ref, dckvn, dckv_ref, dgkv_ref)):
            xv = x_ref[...]
            r = lax.rsqrt(jnp.mean(xv * xv, axis=-1, keepdims=True) + EPS)
            xh = xv * r
            dg_ref[...] += jnp.sum(dn * xh, axis=0, keepdims=True)
            dh = dn * g_ref[...]
            dx_ref[...] = (r * (dh - xh * jnp.mean(dh * xh, axis=-1, keepdims=True))).astype(BF16)

    row = lambda w, j: pl.BlockSpec((tm, w), lambda i: (i, j))
    full = lambda a: pl.BlockSpec(a.shape, lambda i: (0, 0))
    acc = lambda w: pl.BlockSpec((1, w), lambda i: (0, 0))
    sds = lambda w, dt: jax.ShapeDtypeStruct((s, w), dt)
    return pl.pallas_call(
        body,
        grid=(s // tm,),
        in_specs=[row(768, 6), row(256, 21), row(1024, 0), row(1024, 0), row(1024, 0), full(g_q), full(g_kv),
                  full(w_uq_pt), full(w_kv_pt), row(128, 0), row(128, 0), row(128, 0)],
        out_specs=[row(1024, 0), row(2048, 0), row(768, 0), row(256, 0), row(128, 0), acc(768), acc(256)],
        out_shape=[sds(1024, BF16), sds(2048, BF16), sds(768, BF16), sds(256, BF16), sds(128, BF16),
                   jax.ShapeDtypeStruct((1, 768), F32), jax.ShapeDtypeStruct((1, 256), F32)],
        compiler_params=_params(("arbitrary",)),
        name="mla_bwd",
    )(proj, proj, dqr, dkr, dv, g_q, g_kv, w_uq_pt, w_kv_pt, rc, rs1, rs2)


def _pick(n, options):
    for o in options:
        if n % o == 0:
            return o
    raise ValueError(n)


def _matmul(a, b, name):
    m, k = a.shape
    n = b.shape[1]
    tm = _pick(m, (512, 384, 256))
    tn = _pick(n, (1152, 1024, 768, 512))
    tk = _pick(k, (1024, 512))
    nk = k // tk

    def body(a_ref, b_ref, o_ref):
        @pl.when(pl.program_id(2) == 0)
        def _():
            o_ref[...] = jnp.zeros_like(o_ref)

        o_ref[...] += _dot(a_ref[...], b_ref[...])

    return pl.pallas_call(
        body,
        grid=(m // tm, n // tn, nk),
        in_specs=[pl.BlockSpec((tm, tk), lambda i, j, l: (i, l)), pl.BlockSpec((tk, tn), lambda i, j, l: (l, j))],
        out_specs=pl.BlockSpec((tm, tn), lambda i, j, l: (i, j)),
        out_shape=jax.ShapeDtypeStruct((m, n), F32),
        compiler_params=_params(("arbitrary", "arbitrary", "arbitrary")),
        name=name,
    )(a, b)


def _dh_dx(dproj, w_in_pt, x, dout, g_pre):
    s, k = dproj.shape
    tm, tk = 512, 1152
    nk = k // tk

    def body(dp_ref, w_ref, x_ref, dout_ref, g_ref, dx_ref, dg_ref, acc_ref):
        l = pl.program_id(1)

        @pl.when((pl.program_id(0) == 0) & (l == 0))
        def _():
            dg_ref[...] = jnp.zeros_like(dg_ref)

        @pl.when(l == 0)
        def _():
            acc_ref[...] = jnp.zeros_like(acc_ref)

        acc_ref[...] += _dot(dp_ref[...], w_ref[...])

        @pl.when(l == nk - 1)
        def _():
            dh = acc_ref[...]
            xv = x_ref[...]
            r = lax.rsqrt(jnp.mean(xv * xv, axis=-1, keepdims=True) + EPS)
            xh = xv * r
            dg_ref[...] += jnp.sum(dh * xh, axis=0, keepdims=True)
            dxh = dh * g_ref[...]
            dx_ref[...] = dout_ref[...] + r * (dxh - xh * jnp.mean(dxh * xh, axis=-1, keepdims=True))

    return pl.pallas_call(
        body,
        grid=(s // tm, nk),
        in_specs=[pl.BlockSpec((tm, tk), lambda i, l: (i, l)), pl.BlockSpec((tk, D_MODEL), lambda i, l: (l, 0)),
                  pl.BlockSpec((tm, D_MODEL), lambda i, l: (i, 0)), pl.BlockSpec((tm, D_MODEL), lambda i, l: (i, 0)),
                  pl.BlockSpec((1, D_MODEL), lambda i, l: (0, 0))],
        out_specs=[pl.BlockSpec((tm, D_MODEL), lambda i, l: (i, 0)), pl.BlockSpec((1, D_MODEL), lambda i, l: (0, 0))],
        out_shape=[jax.ShapeDtypeStruct((s, D_MODEL), F32), jax.ShapeDtypeStruct((1, D_MODEL), F32)],
        scratch_shapes=[pltpu.VMEM((tm, D_MODEL), F32)],
        compiler_params=_params(("arbitrary", "arbitrary")),
        name="dh_dx",
    )(dproj, w_in_pt, x, dout, g_pre)


def _rope_tables(s):
    inv = ROPE_THETA ** (-jnp.arange(0, ROPE, 2, dtype=F32) / ROPE)
    ang = jnp.arange(s, dtype=F32)[:, None] * inv[None, :]
    cos, sin = jnp.cos(ang), jnp.sin(ang)
    z = lambda w: jnp.zeros((s, w), F32)
    rc = jnp.concatenate([jnp.ones((s, NOPE), F32), cos, cos, z(32)], axis=1)
    rs1 = jnp.concatenate([z(NOPE), -sin, z(16), z(32)], axis=1)
    rs2 = jnp.concatenate([z(NOPE), z(16), sin, z(32)], axis=1)
    return rc, rs1, rs2


def _local_step(x, tgt, w_in, w_uq, w_ukv, w_a, w_b, w_out, g_pre, b_gate, g_q, g_kv, lbl, g_hgrn, g_post):
    s = x.shape[0]
    zb = lambda r, w: jnp.zeros((r, w), BF16)
    w_in_p = jnp.concatenate([w_in[:, 3616:], w_in[:, 1056:3616], w_in[:, :1024],
                              zb(D_MODEL, 64), w_in[:, 1024:1056], zb(D_MODEL, 32)], axis=1)
    w_uq_p = jnp.pad(w_uq.reshape(Q_LORA, HEADS, QK), ((0, 0), (0, 0), (0, LANE - QK))).reshape(Q_LORA, HEADS * LANE)
    kv3 = w_ukv.reshape(KV_LORA, HEADS, NOPE + VDIM)
    pad64 = lambda t: jnp.pad(t, ((0, 0), (0, 0), (0, LANE - 64))).reshape(KV_LORA, HEADS * LANE)
    w_kv_p = jnp.concatenate([pad64(kv3[:, :, :NOPE]), pad64(kv3[:, :, NOPE:])], axis=1)
    rc, rs1, rs2 = _rope_tables(s)
    gh = jnp.tile(g_hgrn, (1, HEADS))

    proj, h = _norm_proj(x, g_pre, w_in_p)
    qr, kr, v, cqn, ckvn = _mla_prep(proj, g_q, g_kv, w_uq_p, w_kv_p, rc, rs1, rs2)
    attn, qa = _attn_fwd(qr, kr, v)
    o, sprev = _hgrn_fwd(proj, lbl)
    (dout, dml, dga, dgb, dop, do, m_bf, dy_bf, ya_bf, dya_bf, yb_bf, dyb_bf,
     loss_vec, dg_post, db_gate, dgh) = _tail(x, tgt, proj, attn, o, w_a, w_b, w_out, w_a.T, w_b.T, w_out.T,
                                               b_gate, g_post, gh)
    dqr, dkr, dv = _attn_bwd(qa, kr, v, dop)
    dhq, dhf, dhi, dlbl = _hgrn_bwd(proj, lbl, do, sprev)
    dqf, dkvf, dcq, dckv, dkpe, dg_q, dg_kv = _mla_bwd(proj, dqr, dkr, dv, g_q, g_kv, w_uq_p.T, w_kv_p.T,
                                                       rc, rs1, rs2)
    dproj = jnp.concatenate([dml, dga, dhq, dhf, dhi, dgb, dcq, dckv, dkpe], axis=1)
    dx, dg_pre = _dh_dx(dproj, w_in_p.T, x, dout, g_pre)

    dw_in_p = _matmul(h.T, dproj, "dw_in")
    dw_out = _matmul(m_bf.T, dy_bf, "dw_out")
    dw_a = _matmul(ya_bf.T, dya_bf, "dw_a")
    dw_b = _matmul(yb_bf.T, dyb_bf, "dw_b")
    dw_uq_p = _matmul(cqn.T, dqf, "dw_uq")
    dw_kv_p = _matmul(ckvn.T, dkvf, "dw_kv")

    dw_in = jnp.concatenate([dw_in_p[:, 4608:5632], dw_in_p[:, 5696:5728], dw_in_p[:, 2048:4608], dw_in_p[:, :2048]],
                            axis=1)
    dw_uq = dw_uq_p.reshape(Q_LORA, HEADS, LANE)[:, :, :QK].reshape(Q_LORA, HEADS * QK)
    dw_ukv = jnp.concatenate([dw_kv_p[:, :HEADS * LANE].reshape(KV_LORA, HEADS, LANE)[:, :, :NOPE],
                              dw_kv_p[:, HEADS * LANE:].reshape(KV_LORA, HEADS, LANE)[:, :, :VDIM]],
                             axis=2).reshape(KV_LORA, 1024)
    loss = 0.5 / D_MODEL * jnp.sum(loss_vec)
    grads = dict(g_pre=dg_pre, w_in=dw_in, b_gate=db_gate, g_q=dg_q, w_uq=dw_uq, g_kv=dg_kv, w_ukv=dw_ukv,
                 lb_logits=dlbl, g_hgrn=jnp.sum(dgh.reshape(HEADS, VDIM), axis=0, keepdims=True),
                 w_branch_a=dw_a, w_branch_b=dw_b, w_out=dw_out, g_post=dg_post)
    return loss, dx, grads


def _my_place():
    return lax.axis_index("x"), lax.axis_index("y"), lax.axis_index("c")


def _all_gather(block):
    r, c_ = block.shape

    def body(x_ref, out_ref, send_sems, recv_sems, local_sem):
        x, y, c = _my_place()
        me, sibling = (x, y, c), (x, y, 1 - c)
        chips = [(1 - x, y), (x, 1 - y), (1 - x, 1 - y)]

        def slot(px, py, pc):
            return out_ref.at[4 * px + 2 * py + pc]

        def copy(k, blk, to, src=None):
            return pltpu.make_async_remote_copy(
                src_ref=slot(*blk) if src is None else src, dst_ref=slot(*blk),
                send_sem=send_sems.at[k], recv_sem=recv_sems.at[k], device_id=to, device_id_type=MESH_ID)

        mine = pltpu.make_async_copy(x_ref, slot(*me), local_sem)
        mine.start()
        first = [copy(0, me, sibling, src=x_ref)]
        first += [copy(1 + j, me, (*chip, c), src=x_ref) for j, chip in enumerate(chips)]
        for cp in first:
            cp.start()
        passed = [copy(4 + j, (*chip, c), sibling) for j, chip in enumerate(chips)]
        for j, chip in enumerate(chips):
            copy(1 + j, (*chip, c), me).wait_recv()
            passed[j].start()
        copy(0, sibling, me).wait_recv()
        for j, chip in enumerate(chips):
            copy(4 + j, (*chip, 1 - c), me).wait_recv()
        for cp in first + passed:
            cp.wait_send()
        mine.wait()

    return pl.pallas_call(
        body,
        out_shape=jax.ShapeDtypeStruct((N_DEV, r, c_), block.dtype),
        in_specs=[pl.BlockSpec(memory_space=pl.ANY)],
        out_specs=pl.BlockSpec(memory_space=pl.ANY),
        scratch_shapes=[pltpu.SemaphoreType.DMA((7,)), pltpu.SemaphoreType.DMA((7,)), pltpu.SemaphoreType.DMA],
        name="gather_weights",
    )(block)


def _exchange(send):
    def body(s_ref, r_ref, send_sems, recv_sems, local_sem):
        x, y, c = _my_place()
        me = 4 * x + 2 * y + c
        mine = pltpu.make_async_copy(s_ref.at[me], r_ref.at[me], local_sem)
        mine.start()
        copies = []
        for k in range(N_DEV - 1):
            fx, fy, fc = (k + 1) >> 2 & 1, (k + 1) >> 1 & 1, (k + 1) & 1
            px = 1 - x if fx else x
            py = 1 - y if fy else y
            pc = 1 - c if fc else c
            copies.append(pltpu.make_async_remote_copy(
                src_ref=s_ref.at[4 * px + 2 * py + pc], dst_ref=r_ref.at[me],
                send_sem=send_sems.at[k], recv_sem=recv_sems.at[k], device_id=(px, py, pc), device_id_type=MESH_ID))
        for cp in copies:
            cp.start()
        for cp in copies:
            cp.wait_recv()
        for cp in copies:
            cp.wait_send()
        mine.wait()

    return pl.pallas_call(
        body,
        out_shape=jax.ShapeDtypeStruct(send.shape, send.dtype),
        in_specs=[pl.BlockSpec(memory_space=pl.ANY)],
        out_specs=pl.BlockSpec(memory_space=pl.ANY),
        scratch_shapes=[pltpu.SemaphoreType.DMA((7,)), pltpu.SemaphoreType.DMA((7,)), pltpu.SemaphoreType.DMA],
        name="exchange_grads",
    )(send)


def _sum_adamw(recv, w, m, v):
    rows = recv.shape[1]
    tr = 120
    c1 = 1.0 / (1.0 - ADAM_B1 ** ADAM_STEP)
    c2 = 1.0 / (1.0 - ADAM_B2 ** ADAM_STEP)

    def body(r_ref, w_ref, m_ref, v_ref, g_ref, d_ref, nm_ref, nv_ref):
        g = r_ref[0]
        for k in range(1, N_DEV):
            g = g + r_ref[k]
        g_ref[...] = g
        nm = ADAM_B1 * m_ref[...] + (1.0 - ADAM_B1) * g
        nv = ADAM_B2 * v_ref[...] + (1.0 - ADAM_B2) * (g * g)
        nm_ref[...] = nm
        nv_ref[...] = nv
        d_ref[...] = -ADAM_LR * ((nm * c1) / (jnp.sqrt(nv * c2) + ADAM_EPS) + ADAM_WD * w_ref[...])

    blk = pl.BlockSpec((tr, 1024), lambda i: (i, 0))
    out = jax.ShapeDtypeStruct((rows, 1024), F32)
    return pl.pallas_call(
        body,
        grid=(rows // tr,),
        in_specs=[pl.BlockSpec((N_DEV, tr, 1024), lambda i: (0, i, 0)), blk, blk, blk],
        out_specs=[blk, blk, blk, blk],
        out_shape=[out, out, out, out],
        compiler_params=_params(("arbitrary",)),
        name="sum_adamw",
    )(recv, w, m, v)


BIG = ("w_in", "w_uq", "w_ukv", "w_branch_a", "w_branch_b", "w_out")
SMALL = ("g_pre", "b_gate", "g_q", "g_kv", "lb_logits", "g_hgrn", "g_post")
SMALL_ROWS = (1, 2, 1, 1, 1, 1, 1)
SMALL_SHAPE = dict(g_pre=(1, 1024), b_gate=(1, 2048), g_q=(1, 768), g_kv=(1, 256), lb_logits=(2, 512),
                   g_hgrn=(1, 64), g_post=(1, 1024))
SHARD_SHAPE = dict(w_in=(1024, 708), w_uq=(96, 768), w_ukv=(256, 128), w_branch_a=(512, 128),
                   w_branch_b=(512, 128), w_out=(128, 1024))
COL_SHARDED = dict(w_in=True, w_uq=False, w_ukv=True, w_branch_a=True, w_branch_b=True, w_out=False)


def _pack_shard(t):
    parts = [t[n].reshape(-1, 1024) for n in BIG]
    parts.append(jnp.zeros((PACK_BIG - PACK_OFF[-1], 1024), parts[0].dtype))
    return jnp.concatenate(parts, axis=0)


def _unpack_shard(p):
    return {n: p[PACK_OFF[i]:PACK_OFF[i + 1]].reshape(SHARD_SHAPE[n]) for i, n in enumerate(BIG)}


def _pack_small(t):
    parts = []
    for n, r in zip(SMALL, SMALL_ROWS):
        flat = t[n].reshape(1, -1)
        parts.append(jnp.pad(flat, ((0, 0), (0, r * 1024 - flat.shape[1]))).reshape(r, 1024))
    return jnp.concatenate(parts, axis=0)


def _unpack_small(p):
    out, r0 = {}, 0
    for n, r in zip(SMALL, SMALL_ROWS):
        shp = SMALL_SHAPE[n]
        out[n] = p[r0:r0 + r].reshape(1, -1)[:, :shp[0] * shp[1]].reshape(shp)
        r0 += r
    return out


def _split_full(name, full):
    r, c = full.shape
    if COL_SHARDED[name]:
        return full.reshape(r, N_DEV, c // N_DEV).transpose(1, 0, 2).reshape(N_DEV, -1, 1024)
    return full.reshape(N_DEV, -1, 1024)


def _join_full(name, slots):
    r, c = SHARD_SHAPE[name]
    if COL_SHARDED[name]:
        return slots.reshape(N_DEV, r, c).transpose(1, 0, 2).reshape(r, N_DEV * c)
    return slots.reshape(N_DEV * r, c)


def kernel(x, g_pre, w_in, b_gate, g_q, w_uq, g_kv, w_ukv, lb_logits, g_hgrn, w_branch_a, w_branch_b, w_out, g_post, loss_target, m_g_pre, m_w_in, m_b_gate, m_g_q, m_w_uq, m_g_kv, m_w_ukv, m_lb_logits, m_g_hgrn, m_w_branch_a, m_w_branch_b, m_w_out, m_g_post, v_g_pre, v_w_in, v_b_gate, v_g_q, v_w_uq, v_g_kv, v_w_ukv, v_lb_logits, v_g_hgrn, v_w_branch_a, v_w_branch_b, v_w_out, v_g_post):
    w = dict(w_in=w_in[0], w_uq=w_uq[0], w_ukv=w_ukv[0], w_branch_a=w_branch_a[0], w_branch_b=w_branch_b[0],
             w_out=w_out[0], g_pre=g_pre, b_gate=b_gate, g_q=g_q, g_kv=g_kv, lb_logits=lb_logits, g_hgrn=g_hgrn,
             g_post=g_post)
    mom = dict(w_in=m_w_in[0], w_uq=m_w_uq[0], w_ukv=m_w_ukv[0], w_branch_a=m_w_branch_a[0],
               w_branch_b=m_w_branch_b[0], w_out=m_w_out[0], g_pre=m_g_pre, b_gate=m_b_gate, g_q=m_g_q, g_kv=m_g_kv,
               lb_logits=m_lb_logits, g_hgrn=m_g_hgrn, g_post=m_g_post)
    var = dict(w_in=v_w_in[0], w_uq=v_w_uq[0], w_ukv=v_w_ukv[0], w_branch_a=v_w_branch_a[0],
               w_branch_b=v_w_branch_b[0], w_out=v_w_out[0], g_pre=v_g_pre, b_gate=v_b_gate, g_q=v_g_q, g_kv=v_g_kv,
               lb_logits=v_lb_logits, g_hgrn=v_g_hgrn, g_post=v_g_post)

    gathered = _all_gather(_pack_shard(w).astype(BF16))
    full = {n: _join_full(n, gathered[:, PACK_OFF[i]:PACK_OFF[i + 1]]) for i, n in enumerate(BIG)}

    loss, dx, grads = _local_step(x[0], loss_target[0], full["w_in"], full["w_uq"], full["w_ukv"],
                                  full["w_branch_a"], full["w_branch_b"], full["w_out"],
                                  g_pre, b_gate, g_q, g_kv, lb_logits, g_hgrn, g_post)

    small = jnp.broadcast_to(_pack_small(grads)[None], (N_DEV, PACK_ALL - PACK_BIG, 1024))
    send = jnp.concatenate([_split_full(n, grads[n]) for n in BIG]
                           + [jnp.zeros((N_DEV, PACK_BIG - PACK_OFF[-1], 1024), F32), small], axis=1)
    recv = _exchange(send)

    pack = lambda t: jnp.concatenate([_pack_shard(t), _pack_small(t)], axis=0)
    g_p, d_p, m_p, v_p = _sum_adamw(recv, pack(w), pack(mom), pack(var))

    def unpack(p, lead):
        t = {**_unpack_shard(p[:PACK_BIG]), **_unpack_small(p[PACK_BIG:])}
        return [t[n][None] if (lead and n in BIG) else t[n] for n in
                ("g_pre", "w_in", "b_gate", "g_q", "w_uq", "g_kv", "w_ukv", "lb_logits", "g_hgrn",
                 "w_branch_a", "w_branch_b", "w_out", "g_post")]

    total = lax.psum(loss, ("x", "y", "c"))
    return (total, dx[None], *unpack(g_p, True), *unpack(d_p, True), *unpack(m_p, True), *unpack(v_p, True))
```

```python
import math

import jax
import jax.numpy as jnp
from jax import lax
from jax.experimental import pallas as pl
from jax.experimental.pallas import tpu as pltpu

F32, BF16 = jnp.float32, jnp.bfloat16

D_MODEL = 1024
EPS = 1e-6
HEADS = 8
NOPE, ROPE, VDIM = 64, 32, 64
QK = NOPE + ROPE
Q_LORA, KV_LORA = 768, 256
ROPE_THETA = 10000.0
ATT_CHUNK_SHIFT = 6
HG_BLOCK = 32
HG_WIDTH = 512
D_IN = 5664
D_IN_PAD = 5760
W_IN_SHARD = D_IN // 8
N_DEV = 8
LANE = 128

ADAM_LR, ADAM_B1, ADAM_B2, ADAM_EPS, ADAM_WD, ADAM_STEP = 0.001, 0.9, 0.999, 1e-08, 0.01, 10

W_IN_SEGMENTS = ((3616, 5664, 0), (1056, 1568, 2048), (3104, 3616, 2560), (1568, 3104, 3072),
                 (0, 1024, 4608), (1024, 1056, 5696))
W_IN_ZERO = ((5632, 5696), (5728, 5760))

NT = (((1,), (1,)), ((), ()))
TN = (((0,), (0,)), ((), ()))
MESH_ID = pl.DeviceIdType.MESH


def _w_in_pieces():
    out = []
    for lo, hi, dst in W_IN_SEGMENTS:
        c = lo
        while c < hi:
            p = c // W_IN_SHARD
            e = min(hi, (p + 1) * W_IN_SHARD)
            out.append((p, c - p * W_IN_SHARD, e - p * W_IN_SHARD, dst + c - lo))
            c = e
    return out


def _params(sem, vmem_mb=48):
    return pltpu.CompilerParams(dimension_semantics=sem, vmem_limit_bytes=vmem_mb * 2**20)


def _dot(a, b):
    return jnp.dot(a, b, preferred_element_type=F32)


def _dotg(a, b, dims):
    return lax.dot_general(a, b, dims, preferred_element_type=F32)


def _split3(x):
    hi = x.astype(BF16)
    r = x - hi.astype(F32)
    mid = r.astype(BF16)
    lo = (r - mid.astype(F32)).astype(BF16)
    return hi, mid, lo


def _sel_left(m01, x):
    hi, mid, lo = _split3(x)
    return _dot(m01, hi) + _dot(m01, mid) + _dot(m01, lo)


def _sel_right(x, m01):
    hi, mid, lo = _split3(x)
    return _dot(hi, m01) + _dot(mid, m01) + _dot(lo, m01)


def _hi_lo(x):
    hi = x.astype(BF16).astype(F32)
    return hi, x - hi


def _sigmoid(x):
    return 1.0 / (1.0 + jnp.exp(-x))


def _rope(x, c, s1, s2):
    return x * c + pltpu.roll(x, 112, 1) * s1 + pltpu.roll(x, 16, 1) * s2


def _unrope(d, c, s1, s2):
    return d * c + pltpu.roll(d * s1, 16, 1) + pltpu.roll(d * s2, 112, 1)


def _assemble_w_in(slots):
    tm = 256
    pieces = _w_in_pieces()

    def body(s_ref, w_ref, wt_ref):
        for lo, hi in W_IN_ZERO:
            w_ref[:, lo:hi] = jnp.zeros((tm, hi - lo), BF16)
        for p, lo, hi, dst in pieces:
            w_ref[:, dst:dst + hi - lo] = s_ref[p, :, lo:hi]
        wt_ref[...] = w_ref[...].T

    return pl.pallas_call(
        body,
        grid=(D_MODEL // tm,),
        in_specs=[pl.BlockSpec((N_DEV, tm, W_IN_SHARD), lambda i: (0, i, 0))],
        out_specs=[pl.BlockSpec((tm, D_IN_PAD), lambda i: (i, 0)), pl.BlockSpec((D_IN_PAD, tm), lambda i: (0, i))],
        out_shape=[jax.ShapeDtypeStruct((D_MODEL, D_IN_PAD), BF16), jax.ShapeDtypeStruct((D_IN_PAD, D_MODEL), BF16)],
        compiler_params=_params(("arbitrary",)),
        name="assemble_w_in",
    )(slots)


def _scatter_w_in(dw):
    tm = 256
    pieces = _w_in_pieces()

    def body(d_ref, s_ref):
        for p, lo, hi, dst in pieces:
            s_ref[p, :, lo:hi] = d_ref[:, dst:dst + hi - lo].astype(BF16)

    return pl.pallas_call(
        body,
        grid=(D_MODEL // tm,),
        in_specs=[pl.BlockSpec((tm, D_IN_PAD), lambda i: (i, 0))],
        out_specs=pl.BlockSpec((N_DEV, tm, W_IN_SHARD), lambda i: (0, i, 0)),
        out_shape=jax.ShapeDtypeStruct((N_DEV, D_MODEL, W_IN_SHARD), BF16),
        compiler_params=_params(("arbitrary",)),
        name="scatter_w_in",
    )(dw)


def _norm_proj(x, g_pre, w):
    s, n = x.shape[0], w.shape[1]
    tm, tn = 512, 1152

    def body(x_ref, g_ref, w_ref, proj_ref, ht_ref, h_ref):
        @pl.when(pl.program_id(1) == 0)
        def _():
            xv = x_ref[...]
            r = lax.rsqrt(jnp.mean(xv * xv, axis=-1, keepdims=True) + EPS)
            h = (xv * r * g_ref[...]).astype(BF16)
            h_ref[...] = h
            ht_ref[...] = h.T

        proj_ref[...] = _dot(h_ref[...], w_ref[...])

    return pl.pallas_call(
        body,
        grid=(s // tm, n // tn),
        in_specs=[
            pl.BlockSpec((tm, D_MODEL), lambda i, j: (i, 0)),
            pl.BlockSpec((1, D_MODEL), lambda i, j: (0, 0)),
            pl.BlockSpec((D_MODEL, tn), lambda i, j: (0, j)),
        ],
        out_specs=[
            pl.BlockSpec((tm, tn), lambda i, j: (i, j)),
            pl.BlockSpec((D_MODEL, tm), lambda i, j: (0, i)),
        ],
        out_shape=[jax.ShapeDtypeStruct((s, n), F32), jax.ShapeDtypeStruct((D_MODEL, s), BF16)],
        scratch_shapes=[pltpu.VMEM((tm, D_MODEL), BF16)],
        compiler_params=_params(("arbitrary", "arbitrary")),
        name="norm_proj",
    )(x, g_pre, w)


def _mla_prep(proj, g_q, g_kv, w_uq_p, w_kv_p, rc, rs1, rs2):
    s = proj.shape[0]
    tm = 256
    scale = 1.0 / math.sqrt(QK)

    def body(cq_ref, ckv_ref, kpe_ref, gq_ref, gkv_ref, wuq_ref, wkv_ref, c_ref, s1_ref, s2_ref,
             qr_ref, kr_ref, v_ref, cqt_ref, ckvt_ref):
        cq = cq_ref[...]
        r = lax.rsqrt(jnp.mean(cq * cq, axis=-1, keepdims=True) + EPS)
        cqn = (cq * r * gq_ref[...]).astype(BF16)
        cqt_ref[...] = cqn.T
        q = _dot(cqn, wuq_ref[...])
        ckv = ckv_ref[...]
        r = lax.rsqrt(jnp.mean(ckv * ckv, axis=-1, keepdims=True) + EPS)
        ckvn = (ckv * r * gkv_ref[...]).astype(BF16)
        ckvt_ref[...] = ckvn.T
        kv = _dot(ckvn, wkv_ref[...])
        c, s1, s2 = c_ref[...], s1_ref[...], s2_ref[...]
        lane = lax.broadcasted_iota(jnp.int32, (tm, LANE), 1)
        kpe = _rope(kpe_ref[...], c, s1, s2) + jnp.where((lane == QK) | (lane == QK + 1), 1.0, 0.0)
        vone = jnp.where((lane == VDIM) | (lane == VDIM + 1), 1.0, 0.0)
        for h in range(HEADS):
            sl = slice(LANE * h, LANE * (h + 1))
            qr_ref[:, sl] = (_rope(q[:, sl], c, s1, s2) * scale).astype(BF16)
            kr_ref[:, sl] = (kv[:, sl] + kpe).astype(BF16)
            v_ref[:, sl] = (kv[:, HEADS * LANE + LANE * h:HEADS * LANE + LANE * (h + 1)] + vone).astype(BF16)

    row = lambda w, j: pl.BlockSpec((tm, w), lambda i: (i, j))
    col = lambda w: pl.BlockSpec((w, tm), lambda i: (0, i))
    full = lambda a: pl.BlockSpec(a.shape, lambda i: (0, 0))
    return pl.pallas_call(
        body,
        grid=(s // tm,),
        in_specs=[row(768, 6), row(256, 21), row(128, 44), full(g_q), full(g_kv), full(w_uq_p), full(w_kv_p),
                  row(128, 0), row(128, 0), row(128, 0)],
        out_specs=[row(1024, 0), row(1024, 0), row(1024, 0), col(768), col(256)],
        out_shape=[jax.ShapeDtypeStruct((s, 1024), BF16), jax.ShapeDtypeStruct((s, 1024), BF16),
                   jax.ShapeDtypeStruct((s, 1024), BF16), jax.ShapeDtypeStruct((768, s), BF16),
                   jax.ShapeDtypeStruct((256, s), BF16)],
        compiler_params=_params(("arbitrary",)),
        name="mla_prep",
    )(proj, proj, proj, g_q, g_kv, w_uq_p, w_kv_p, rc, rs1, rs2)


ATT_T = 512


def _chunk_mask(transposed):
    r = lax.broadcasted_iota(jnp.int32, (ATT_T, ATT_T), 0) >> ATT_CHUNK_SHIFT
    c = lax.broadcasted_iota(jnp.int32, (ATT_T, ATT_T), 1) >> ATT_CHUNK_SHIFT
    return (r <= c) if transposed else (c <= r)


def _attn_fwd(qr, kr, vp):
    s = qr.shape[0]
    t = ATT_T

    def body(q_ref, k_ref, v_ref, o_ref, qa_ref):
        qi = pl.program_id(1)
        lane = lax.broadcasted_iota(jnp.int32, (t, LANE), 1)
        sls = [slice(LANE * a, LANE * (a + 1)) for a in range(2)]
        qs = [q_ref[:, sl] for sl in sls]

        def step(j, carry, masked):
            rows = pl.ds(pl.multiple_of(j * t, t), t)
            out = []
            for a in range(2):
                m, acc = carry[a]
                sc = _dotg(qs[a], k_ref[rows, sls[a]], NT)
                if masked:
                    sc = jnp.where(_chunk_mask(False), sc, -1e30)
                m_new = jnp.maximum(m, jnp.max(sc, axis=-1, keepdims=True))
                p = jnp.exp(sc - m_new).astype(BF16)
                acc = jnp.exp(m - m_new) * acc + _dot(p, v_ref[rows, sls[a]])
                out.append((m_new, acc))
            return tuple(out)

        init = tuple((jnp.full((t, 1), -1e30, F32), jnp.zeros((t, LANE), F32)) for _ in range(2))
        carry = lax.fori_loop(0, qi, lambda j, c: step(j, c, False), init)
        carry = step(qi, carry, True)
        outs = []
        for a in range(2):
            m, acc = carry[a]
            l = acc[:, VDIM:VDIM + 1]
            outs.append(acc / l)
            hi, lo_part = _hi_lo(-(m + jnp.log(l)))
            qa = jnp.where(lane == QK, hi, jnp.where(lane == QK + 1, lo_part, qs[a].astype(F32)))
            qa_ref[:, sls[a]] = qa.astype(BF16)
        o_ref[...] = jnp.where(lane < VDIM, outs[0], pltpu.roll(outs[1], VDIM, 1))

    return pl.pallas_call(
        body,
        grid=(HEADS // 2, s // t),
        in_specs=[
            pl.BlockSpec((t, 2 * LANE), lambda h, i: (i, h)),
            pl.BlockSpec((s, 2 * LANE), lambda h, i: (0, h)),
            pl.BlockSpec((s, 2 * LANE), lambda h, i: (0, h)),
        ],
        out_specs=[
            pl.BlockSpec((t, LANE), lambda h, i: (i, h)),
            pl.BlockSpec((t, 2 * LANE), lambda h, i: (i, h)),
        ],
        out_shape=[jax.ShapeDtypeStruct((s, 512), F32), jax.ShapeDtypeStruct((s, 1024), BF16)],
        compiler_params=_params(("arbitrary", "arbitrary")),
        name="attn_fwd",
    )(qr, kr, vp)


def _attn_bwd(qa, kr, vp, dop):
    s = qa.shape[0]
    t = ATT_T
    nq = s // t

    def body(q_ref, k_ref, v_ref, do_ref, dq_ref, dk_ref, dv_ref):
        j = pl.program_id(1)
        sls = [slice(LANE * a, LANE * (a + 1)) for a in range(2)]

        @pl.when(j == 0)
        def _():
            dq_ref[...] = jnp.zeros_like(dq_ref)

        dk_ref[...] = jnp.zeros_like(dk_ref)
        dv_ref[...] = jnp.zeros_like(dv_ref)
        ks = [k_ref[:, sl] for sl in sls]
        vs = [v_ref[:, sl] for sl in sls]

        def step(i, masked):
            rows = pl.ds(pl.multiple_of(i * t, t), t)
            for a in range(2):
                q = q_ref[rows, sls[a]]
                do = do_ref[rows, sls[a]]
                sc = _dotg(ks[a], q, NT)
                if masked:
                    sc = jnp.where(_chunk_mask(True), sc, -1e30)
                p = jnp.exp(sc)
                ds = (p * _dotg(vs[a], do, NT)).astype(BF16)
                dv_ref[:, sls[a]] += _dot(p.astype(BF16), do)
                dk_ref[:, sls[a]] += _dot(ds, q)
                dq_ref[rows, sls[a]] += _dotg(ds, ks[a], TN)

        step(j, True)

        def loop(i, c):
            step(i, False)
            return c

        lax.fori_loop(j + 1, nq, loop, 0)

    blk = pl.BlockSpec((t, 2 * LANE), lambda h, j: (j, h))
    whole = pl.BlockSpec((s, 2 * LANE), lambda h, j: (0, h))
    out = jax.ShapeDtypeStruct((s, 1024), F32)
    return pl.pallas_call(
        body,
        grid=(HEADS // 2, nq),
        in_specs=[whole, blk, blk, whole],
        out_specs=[whole, blk, blk],
        out_shape=[out, out, out],
        compiler_params=_params(("arbitrary", "arbitrary")),
        name="attn_bwd",
    )(qa, kr, vp, dop)


HG_T = 256
HG_NC = HG_T // HG_BLOCK


def _hg_consts():
    r = jnp.arange(HG_T)[:, None]
    c = jnp.arange(HG_T)[None, :]
    same = (r // HG_BLOCK) == (c // HG_BLOCK)
    mcum = (same & (c <= r)).astype(BF16)
    mrev = (same & (c >= r)).astype(BF16)
    mlast = (c == (r // HG_BLOCK) * HG_BLOCK + HG_BLOCK - 1).astype(BF16)
    msum = same.astype(BF16)
    a = jnp.arange(LANE)
    bd = ((a[:, None] < 64) == (a[None, :] < 64)).astype(F32)
    return mcum, mrev, mlast, msum, bd


def _hg_pre(hq, hf, lbl, mcum, mlast):
    lb = _sigmoid(lbl[0:1, :] - lbl[1:2, :])
    sig = _sigmoid(hf)
    f = lb + (1.0 - lb) * sig
    b = _sel_left(mcum, jnp.log(f))
    big_l = _sel_left(mlast, b)
    k = 1.0 - f
    qd = hq * jnp.exp(b)
    ki = k * jnp.exp(-b)
    ke = k * jnp.exp(big_l - b)
    return lb, sig, f, b, big_l, qd, ki, ke


def _stack_pair(xp, lo):
    return jnp.concatenate([jnp.where(lo, xp, 0.0), jnp.where(lo, 0.0, xp)], axis=0)


def _hgrn_fwd(proj, lbl):
    s = proj.shape[0]
    t = HG_T
    mcum, _, mlast, _, bd = _hg_consts()

    def body(hq_ref, hf_ref, hi_ref, lbl_ref, mcum_ref, mlast_ref, bd_ref, o_ref, sp_ref, st_ref):
        @pl.when(pl.program_id(0) == 0)
        def _():
            st_ref[...] = jnp.zeros_like(st_ref)

        mc = mcum_ref[...]
        _, _, _, _, big_l, qd, ki, ke = _hg_pre(hq_ref[...], hf_ref[...], lbl_ref[...], mc, mlast_ref[...])
        el = jnp.exp(big_l)
        hi = hi_ref[...]
        lo = lax.broadcasted_iota(jnp.int32, (t, LANE), 1) < 64
        mask2 = jnp.concatenate([mc, mc], axis=0) > 0.5
        for p in range(HEADS // 2):
            sl = slice(LANE * p, LANE * (p + 1))
            vp = hi[:, sl].astype(BF16)
            q2 = _stack_pair(qd[:, sl], lo).astype(BF16)
            a2 = jnp.where(mask2, _dotg(q2, ki[:, sl].astype(BF16), NT), 0.0)
            r2 = _dot(a2.astype(BF16), vp)
            o_intra = jnp.where(lo, r2[:t], r2[t:])
            qb = qd[:, sl].astype(BF16)
            kb = ke[:, sl].astype(BF16)
            for c in range(HG_NC):
                rows = slice(HG_BLOCK * c, HG_BLOCK * (c + 1))
                st = st_ref[p]
                sp_ref[c, :, sl] = st
                o_ref[rows, sl] = o_intra[rows] + _dotg(qb[rows], st.astype(BF16), NT)
                u = _dotg(vp[rows], kb[rows], TN) * bd_ref[...]
                st_ref[p] = st * el[HG_BLOCK * c:HG_BLOCK * c + 1, sl] + u

    row = lambda j: pl.BlockSpec((t, HG_WIDTH), lambda i: (i, j))
    full = lambda a: pl.BlockSpec(a.shape, lambda i: (0, 0))
    return pl.pallas_call(
        body,
        grid=(s // t,),
        in_specs=[row(6), row(7), row(8), full(lbl), full(mcum), full(mlast), full(bd)],
        out_specs=[row(0), pl.BlockSpec((HG_NC, LANE, HG_WIDTH), lambda i: (i, 0, 0))],
        out_shape=[jax.ShapeDtypeStruct((s, HG_WIDTH), F32),
                   jax.ShapeDtypeStruct((s // HG_BLOCK, LANE, HG_WIDTH), F32)],
        scratch_shapes=[pltpu.VMEM((HEADS // 2, LANE, LANE), F32)],
        compiler_params=_params(("arbitrary",)),
        name="hgrn_fwd",
    )(proj, proj, proj, lbl, mcum, mlast, bd)


def _hgrn_bwd(proj, lbl, do, sprev, dproj):
    s = proj.shape[0]
    t = HG_T
    nt = s // t
    mcum, mrev, mlast, msum, bd = _hg_consts()

    def body(hq_ref, hf_ref, hi_ref, lbl_ref, do_ref, sp_ref, mcum_ref, mrev_ref, mlast_ref, msum_ref, bd_ref,
             dproj_in, dh_ref, dlbl_ref, g_ref, dqd_ref, dke_ref, dv_ref, del_ref):
        del dproj_in

        @pl.when(pl.program_id(0) == 0)
        def _():
            g_ref[...] = jnp.zeros_like(g_ref)
            dlbl_ref[...] = jnp.zeros_like(dlbl_ref)

        mc = mcum_ref[...]
        lb, sig, f, b, big_l, qd, ki, ke = _hg_pre(hq_ref[...], hf_ref[...], lbl_ref[...], mc, mlast_ref[...])
        el = jnp.exp(big_l)
        hi = hi_ref[...]
        dov = do_ref[...]
        lo = lax.broadcasted_iota(jnp.int32, (t, LANE), 1) < 64
        mask2 = jnp.concatenate([mc, mc], axis=0) > 0.5
        del_ref[...] = jnp.zeros_like(del_ref)
        dki_parts = []
        for p in range(HEADS // 2):
            sl = slice(LANE * p, LANE * (p + 1))
            vp = hi[:, sl].astype(BF16)
            q2 = _stack_pair(qd[:, sl], lo).astype(BF16)
            kip = ki[:, sl].astype(BF16)
            do2 = _stack_pair(dov[:, sl], lo).astype(BF16)
            a2 = jnp.where(mask2, _dotg(q2, kip, NT), 0.0).astype(BF16)
            da2 = jnp.where(mask2, _dotg(do2, vp, NT), 0.0).astype(BF16)
            dv_ref[:, sl] = _dotg(a2, do2, TN)
            r2 = _dot(da2, kip)
            dqd_ref[:, sl] = jnp.where(lo, r2[:t], r2[t:])
            dki_parts.append(_dotg(da2, q2, TN))
            qb = qd[:, sl].astype(BF16)
            kb = ke[:, sl].astype(BF16)
            dob = dov[:, sl].astype(BF16)
            for c in range(HG_NC - 1, -1, -1):
                rows = slice(HG_BLOCK * c, HG_BLOCK * (c + 1))
                g = g_ref[p]
                gb = g.astype(BF16)
                st = sp_ref[c, :, sl]
                dqd_ref[rows, sl] += _dot(dob[rows], st.astype(BF16))
                dv_ref[rows, sl] += _dotg(kb[rows], gb, NT)
                dke_ref[rows, sl] = _dot(vp[rows], gb)
                last = HG_BLOCK * c + HG_BLOCK - 1
                del_ref[last:last + 1, sl] = jnp.sum(g * st, axis=0, keepdims=True)
                g_ref[p] = g * el[last:last + 1, sl] + _dotg(dob[rows], qb[rows], TN) * bd_ref[...]
        dqd = dqd_ref[...]
        dke = dke_ref[...]
        dki = jnp.concatenate(dki_parts, axis=1)
        dh_ref[:, :HG_WIDTH] = (dqd * jnp.exp(b)).astype(BF16)
        dh_ref[:, 2 * HG_WIDTH:] = dv_ref[...].astype(BF16)
        dke_ke = dke * ke
        db = dqd * qd - dki * ki - dke_ke
        dl_rows = _sel_left(msum_ref[...], dke_ke) + del_ref[...] * el
        is_last = (lax.broadcasted_iota(jnp.int32, (t, HG_WIDTH), 0) & (HG_BLOCK - 1)) == HG_BLOCK - 1
        db = db + jnp.where(is_last, dl_rows, 0.0)
        dlf = _sel_left(mrev_ref[...], db)
        dk = dki * jnp.exp(-b) + dke * jnp.exp(big_l - b)
        df = dlf / f - dk
        dh_ref[:, HG_WIDTH:2 * HG_WIDTH] = (df * (1.0 - lb) * sig * (1.0 - sig)).astype(BF16)
        dlb = jnp.sum(df * (1.0 - sig), axis=0, keepdims=True) * lb * (1.0 - lb)
        dlbl_ref[0:1, :] += dlb
        dlbl_ref[1:2, :] -= dlb

    rrow = lambda j: pl.BlockSpec((t, HG_WIDTH), lambda i: (nt - 1 - i, j))
    full = lambda a: pl.BlockSpec(a.shape, lambda i: (0, 0))
    return pl.pallas_call(
        body,
        grid=(nt,),
        in_specs=[rrow(6), rrow(7), rrow(8), full(lbl), rrow(0),
                  pl.BlockSpec((HG_NC, LANE, HG_WIDTH), lambda i: (nt - 1 - i, 0, 0)),
                  full(mcum), full(mrev), full(mlast), full(msum), full(bd), pl.BlockSpec(memory_space=pl.ANY)],
        out_specs=[pl.BlockSpec((t, 3 * HG_WIDTH), lambda i: (nt - 1 - i, 2)),
                   pl.BlockSpec((2, HG_WIDTH), lambda i: (0, 0))],
        out_shape=[jax.ShapeDtypeStruct(dproj.shape, BF16), jax.ShapeDtypeStruct((2, HG_WIDTH), F32)],
        input_output_aliases={11: 0},
        scratch_shapes=[pltpu.VMEM((HEADS // 2, LANE, LANE), F32), pltpu.VMEM((t, HG_WIDTH), F32),
                        pltpu.VMEM((t, HG_WIDTH), F32), pltpu.VMEM((t, HG_WIDTH), F32),
                        pltpu.VMEM((t, HG_WIDTH), F32)],
        compiler_params=_params(("arbitrary",)),
        name="hgrn_bwd",
    )(proj, proj, proj, lbl, do, sprev, mcum, mrev, mlast, msum, bd, dproj)


def _tail(x, tgt, proj, attn, o, w_a, w_b, w_out, w_at, w_bt, w_outt, b_gate, g_post, gh):
    s = x.shape[0]
    tm = 128
    ones64 = (jnp.arange(HG_WIDTH)[:, None] // 64 == jnp.arange(HG_WIDTH)[None, :] // 64).astype(BF16)

    def body(x_ref, t_ref, ml_ref, ga_ref, gb_ref, at_ref, o_ref, wa_ref, wb_ref, wo_ref, wat_ref, wbt_ref, wot_ref,
             bg_ref, gp_ref, gh_ref, ones_ref,
             dout_ref, dpj_ref, dop_ref, do_ref, mt_ref, dy_ref, yat_ref, dya_ref, ybt_ref, dyb_ref,
             loss_ref, dgp_ref, dbg_ref, dgh_ref):
        @pl.when(pl.program_id(0) == 0)
        def _():
            loss_ref[...] = jnp.zeros_like(loss_ref)
            dgp_ref[...] = jnp.zeros_like(dgp_ref)
            dbg_ref[...] = jnp.zeros_like(dbg_ref)
            dgh_ref[...] = jnp.zeros_like(dgh_ref)

        ones = ones_ref[...]
        gate_a = ga_ref[...]
        sa = _sigmoid(gate_a)
        silu_a = gate_a * sa
        attn_v = at_ref[...]
        ya_in = attn_v * silu_a
        ov = o_ref[...]
        ro = lax.rsqrt(_sel_right(ov * ov, ones) * (1.0 / 64.0) + EPS)
        ohat = ov * ro
        ghv = gh_ref[...]
        on = ohat * ghv
        gate_b = gb_ref[...]
        sb = _sigmoid(gate_b)
        silu_b = gate_b * sb
        yb_in = on * silu_b
        ya_bf = ya_in.astype(BF16)
        yb_bf = yb_in.astype(BF16)
        yat_ref[...] = ya_bf.T
        ybt_ref[...] = yb_bf.T
        y_a = _dot(ya_bf, wa_ref[...])
        y_b = _dot(yb_bf, wb_ref[...])
        gts = _sigmoid(ml_ref[...] + bg_ref[...])
        g_a = gts[:, :D_MODEL]
        g_b = gts[:, D_MODEL:]
        m_bf = (g_a * y_a + g_b * y_b).astype(BF16)
        mt_ref[...] = m_bf.T
        y = _dot(m_bf, wo_ref[...])
        r1 = lax.rsqrt(jnp.mean(y * y, axis=-1, keepdims=True) + EPS)
        yn = y * r1
        gp = gp_ref[...]
        e = x_ref[...] + yn * gp - t_ref[...]
        loss_ref[...] += jnp.sum(e * e, axis=0, keepdims=True)
        dout = e * (1.0 / D_MODEL)
        dout_ref[...] = dout
        dgp_ref[...] += jnp.sum(dout * yn, axis=0, keepdims=True)
        dyn = dout * gp
        dy = r1 * (dyn - yn * jnp.mean(dyn * yn, axis=-1, keepdims=True))
        dy_bf = dy.astype(BF16)
        dy_ref[...] = dy_bf
        dm = _dot(dy_bf, wot_ref[...])
        dml_a = dm * y_a * g_a * (1.0 - g_a)
        dml_b = dm * y_b * g_b * (1.0 - g_b)
        dpj_ref[:, :D_MODEL] = dml_a.astype(BF16)
        dpj_ref[:, D_MODEL:2 * D_MODEL] = dml_b.astype(BF16)
        dbg_ref[:, :D_MODEL] += jnp.sum(dml_a, axis=0, keepdims=True)
        dbg_ref[:, D_MODEL:] += jnp.sum(dml_b, axis=0, keepdims=True)
        dya_bf = (dm * g_a).astype(BF16)
        dyb_bf = (dm * g_b).astype(BF16)
        dya_ref[...] = dya_bf
        dyb_ref[...] = dyb_bf
        dya_in = _dot(dya_bf, wat_ref[...])
        dyb_in = _dot(dyb_bf, wbt_ref[...])
        dattn = dya_in * silu_a
        delta = _sel_right(dattn * attn_v, ones)
        lane = lax.broadcasted_iota(jnp.int32, (tm, LANE), 1)
        for p in range(HEADS // 2):
            sl = slice(LANE * p, LANE * (p + 1))
            xs = (dattn[:, sl], pltpu.roll(dattn[:, sl], VDIM, 1))
            nds = (-pltpu.roll(delta[:, sl], VDIM, 1), -delta[:, sl])
            for a in range(2):
                hi, lo_part = _hi_lo(nds[a])
                blk = jnp.where(lane < VDIM, xs[a], jnp.where(lane == VDIM, hi, jnp.where(lane == VDIM + 1, lo_part, 0.0)))
                dop_ref[:, LANE * (2 * p + a):LANE * (2 * p + a + 1)] = blk.astype(BF16)
        dpj_ref[:, 2 * D_MODEL:2 * D_MODEL + HG_WIDTH] = (
            dya_in * attn_v * (sa * (1.0 + gate_a * (1.0 - sa)))).astype(BF16)
        don = dyb_in * silu_b
        dpj_ref[:, 2 * D_MODEL + HG_WIDTH:] = (dyb_in * on * (sb * (1.0 + gate_b * (1.0 - sb)))).astype(BF16)
        dgh_ref[...] += jnp.sum(don * ohat, axis=0, keepdims=True)
        dohat = don * ghv
        do_ref[...] = ro * (dohat - ohat * (_sel_right(dohat * ohat, ones) * (1.0 / 64.0)))

    row = lambda w, j: pl.BlockSpec((tm, w), lambda i: (i, j))
    col = lambda w: pl.BlockSpec((w, tm), lambda i: (0, i))
    full = lambda a: pl.BlockSpec(a.shape, lambda i: (0, 0))
    acc = lambda w: pl.BlockSpec((1, w), lambda i: (0, 0))
    sds = lambda w, dt: jax.ShapeDtypeStruct((s, w), dt)
    sdt = lambda w: jax.ShapeDtypeStruct((w, s), BF16)
    return pl.pallas_call(
        body,
        grid=(s // tm,),
        in_specs=[row(1024, 0), row(1024, 0), row(2048, 0), row(512, 4), row(512, 5), row(512, 0), row(512, 0),
                  full(w_a), full(w_b), full(w_out), full(w_at), full(w_bt), full(w_outt),
                  full(b_gate), full(g_post), full(gh), full(ones64)],
        out_specs=[row(1024, 0), row(3072, 0), row(1024, 0), row(512, 0),
                   col(1024), row(1024, 0), col(512), row(1024, 0), col(512), row(1024, 0),
                   acc(1024), acc(1024), acc(2048), acc(512)],
        out_shape=[sds(1024, F32), sds(D_IN_PAD, BF16), sds(1024, BF16), sds(512, F32),
                   sdt(1024), sds(1024, BF16), sdt(512), sds(1024, BF16), sdt(512), sds(1024, BF16),
                   jax.ShapeDtypeStruct((1, 1024), F32), jax.ShapeDtypeStruct((1, 1024), F32),
                   jax.ShapeDtypeStruct((1, 2048), F32), jax.ShapeDtypeStruct((1, 512), F32)],
        compiler_params=_params(("arbitrary",), 56),
        name="tail",
    )(x, tgt, proj, proj, proj, attn, o, w_a, w_b, w_out, w_at, w_bt, w_outt, b_gate, g_post, gh, ones64)


def _mla_bwd(proj, dqr, dkr, dv, g_q, g_kv, w_uq_pt, w_kv_pt, rc, rs1, rs2, dproj):
    s = proj.shape[0]
    tm = 256
    scale = 1.0 / math.sqrt(QK)

    def body(cq_ref, ckv_ref, dqr_ref, dkr_ref, dv_ref, gq_ref, gkv_ref, wuqt_ref, wkvt_ref, c_ref, s1_ref, s2_ref,
             dproj_in, dqf_ref, dkvf_ref, dc_ref, dgq_ref, dgkv_ref):
        del dproj_in

        @pl.when(pl.program_id(0) == 0)
        def _():
            dgq_ref[...] = jnp.zeros_like(dgq_ref)
            dgkv_ref[...] = jnp.zeros_like(dgkv_ref)

        c, s1, s2 = c_ref[...], s1_ref[...], s2_ref[...]
        lane = lax.broadcasted_iota(jnp.int32, (tm, LANE), 1)
        ksum = jnp.zeros((tm, LANE), F32)
        for h in range(HEADS):
            sl = slice(LANE * h, LANE * (h + 1))
            dqf_ref[:, sl] = (_unrope(dqr_ref[:, sl], c, s1, s2) * scale).astype(BF16)
            dkh = dkr_ref[:, sl]
            ksum = ksum + dkh
            dkvf_ref[:, sl] = jnp.where(lane < NOPE, dkh, 0.0).astype(BF16)
            dkvf_ref[:, HEADS * LANE + LANE * h:HEADS * LANE + LANE * (h + 1)] = jnp.where(
                lane < VDIM, dv_ref[:, sl], 0.0).astype(BF16)
        dkpe = _unrope(ksum, c, s1, s2)
        dc_ref[:, Q_LORA + KV_LORA:] = jnp.where((lane >= NOPE) & (lane < QK), dkpe, 0.0).astype(BF16)
        dcqn = _dot(dqf_ref[...], wuqt_ref[...])
        dckvn = _dot(dkvf_ref[...], wkvt_ref[...])
        for x_ref, g_ref, dn, cols, dg_ref in ((cq_ref, gq_ref, dcqn, slice(0, Q_LORA), dgq_ref),
                                               (ckv_ref, gkv_ref, dckvn, slice(Q_LORA, Q_LORA + KV_LORA), dgkv_ref)):
            xv = x_ref[...]
            r = lax.rsqrt(jnp.mean(xv * xv, axis=-1, keepdims=True) + EPS)
            xh = xv * r
            dg_ref[...] += jnp.sum(dn * xh, axis=0, keepdims=True)
            dh = dn * g_ref[...]
            dc_ref[:, cols] = (r * (dh - xh * jnp.mean(dh * xh, axis=-1, keepdims=True))).astype(BF16)

    row = lambda w, j: pl.BlockSpec((tm, w), lambda i: (i, j))
    full = lambda a: pl.BlockSpec(a.shape, lambda i: (0, 0))
    acc = lambda w: pl.BlockSpec((1, w), lambda i: (0, 0))
    sds = lambda w, dt: jax.ShapeDtypeStruct((s, w), dt)
    return pl.pallas_call(
        body,
        grid=(s // tm,),
        in_specs=[row(768, 6), row(256, 21), row(1024, 0), row(1024, 0), row(1024, 0), full(g_q), full(g_kv),
                  full(w_uq_pt), full(w_kv_pt), row(128, 0), row(128, 0), row(128, 0),
                  pl.BlockSpec(memory_space=pl.ANY)],
        out_specs=[row(1024, 0), row(2048, 0), row(1152, 4), acc(768), acc(256)],
        out_shape=[sds(1024, BF16), sds(2048, BF16), jax.ShapeDtypeStruct(dproj.shape, BF16),
                   jax.ShapeDtypeStruct((1, 768), F32), jax.ShapeDtypeStruct((1, 256), F32)],
        input_output_aliases={12: 2},
        compiler_params=_params(("arbitrary",)),
        name="mla_bwd",
    )(proj, proj, dqr, dkr, dv, g_q, g_kv, w_uq_pt, w_kv_pt, rc, rs1, rs2, dproj)


def _pick(n, options):
    for o in options:
        if n % o == 0:
            return o
    raise ValueError(n)


def _matmul(a, b, name):
    m, k = a.shape
    n = b.shape[1]
    tm = _pick(m, (512, 384, 256))
    tn = _pick(n, (1152, 1024, 768, 512))
    tk = _pick(k, (1024, 512))
    nk = k // tk

    def body(a_ref, b_ref, o_ref):
        @pl.when(pl.program_id(2) == 0)
        def _():
            o_ref[...] = jnp.zeros_like(o_ref)

        o_ref[...] += _dot(a_ref[...], b_ref[...])

    return pl.pallas_call(
        body,
        grid=(m // tm, n // tn, nk),
        in_specs=[pl.BlockSpec((tm, tk), lambda i, j, l: (i, l)), pl.BlockSpec((tk, tn), lambda i, j, l: (l, j))],
        out_specs=pl.BlockSpec((tm, tn), lambda i, j, l: (i, j)),
        out_shape=jax.ShapeDtypeStruct((m, n), F32),
        compiler_params=_params(("arbitrary", "arbitrary", "arbitrary")),
        name=name,
    )(a, b)


def _dh_dx(dproj, w_in_pt, x, dout, g_pre):
    s, k = dproj.shape
    tm, tk = 512, 1152
    nk = k // tk

    def body(dp_ref, w_ref, x_ref, dout_ref, g_ref, dx_ref, dg_ref, acc_ref):
        l = pl.program_id(1)

        @pl.when((pl.program_id(0) == 0) & (l == 0))
        def _():
            dg_ref[...] = jnp.zeros_like(dg_ref)

        @pl.when(l == 0)
        def _():
            acc_ref[...] = jnp.zeros_like(acc_ref)

        acc_ref[...] += _dot(dp_ref[...], w_ref[...])

        @pl.when(l == nk - 1)
        def _():
            dh = acc_ref[...]
            xv = x_ref[...]
            r = lax.rsqrt(jnp.mean(xv * xv, axis=-1, keepdims=True) + EPS)
            xh = xv * r
            dg_ref[...] += jnp.sum(dh * xh, axis=0, keepdims=True)
            dxh = dh * g_ref[...]
            dx_ref[...] = dout_ref[...] + r * (dxh - xh * jnp.mean(dxh * xh, axis=-1, keepdims=True))

    return pl.pallas_call(
        body,
        grid=(s // tm, nk),
        in_specs=[pl.BlockSpec((tm, tk), lambda i, l: (i, l)), pl.BlockSpec((tk, D_MODEL), lambda i, l: (l, 0)),
                  pl.BlockSpec((tm, D_MODEL), lambda i, l: (i, 0)), pl.BlockSpec((tm, D_MODEL), lambda i, l: (i, 0)),
                  pl.BlockSpec((1, D_MODEL), lambda i, l: (0, 0))],
        out_specs=[pl.BlockSpec((tm, D_MODEL), lambda i, l: (i, 0)), pl.BlockSpec((1, D_MODEL), lambda i, l: (0, 0))],
        out_shape=[jax.ShapeDtypeStruct((s, D_MODEL), F32), jax.ShapeDtypeStruct((1, D_MODEL), F32)],
        scratch_shapes=[pltpu.VMEM((tm, D_MODEL), F32)],
        compiler_params=_params(("arbitrary", "arbitrary")),
        name="dh_dx",
    )(dproj, w_in_pt, x, dout, g_pre)


def _rope_tables(s):
    inv = ROPE_THETA ** (-jnp.arange(0, ROPE, 2, dtype=F32) / ROPE)
    ang = jnp.arange(s, dtype=F32)[:, None] * inv[None, :]
    cos, sin = jnp.cos(ang), jnp.sin(ang)
    z = lambda w: jnp.zeros((s, w), F32)
    rc = jnp.concatenate([jnp.ones((s, NOPE), F32), cos, cos, z(32)], axis=1)
    rs1 = jnp.concatenate([z(NOPE), -sin, z(16), z(32)], axis=1)
    rs2 = jnp.concatenate([z(NOPE), z(16), sin, z(32)], axis=1)
    return rc, rs1, rs2


def _local_step(x, tgt, w_in_slots, w_uq, w_ukv, w_a, w_b, w_out, g_pre, b_gate, g_q, g_kv, lbl, g_hgrn, g_post):
    s = x.shape[0]
    w_in_p, w_in_pt = _assemble_w_in(w_in_slots)
    w_uq_p = jnp.pad(w_uq.reshape(Q_LORA, HEADS, QK), ((0, 0), (0, 0), (0, LANE - QK))).reshape(Q_LORA, HEADS * LANE)
    kv3 = w_ukv.reshape(KV_LORA, HEADS, NOPE + VDIM)
    pad64 = lambda t: jnp.pad(t, ((0, 0), (0, 0), (0, LANE - 64))).reshape(KV_LORA, HEADS * LANE)
    w_kv_p = jnp.concatenate([pad64(kv3[:, :, :NOPE]), pad64(kv3[:, :, NOPE:])], axis=1)
    rc, rs1, rs2 = _rope_tables(s)
    gh = jnp.tile(g_hgrn, (1, HEADS))

    proj, ht = _norm_proj(x, g_pre, w_in_p)
    qr, kr, v, cqt, ckvt = _mla_prep(proj, g_q, g_kv, w_uq_p, w_kv_p, rc, rs1, rs2)
    attn, qa = _attn_fwd(qr, kr, v)
    o, sprev = _hgrn_fwd(proj, lbl)
    (dout, dproj, dop, do, mt, dy_bf, yat, dya_bf, ybt, dyb_bf,
     loss_vec, dg_post, db_gate, dgh) = _tail(x, tgt, proj, attn, o, w_a, w_b, w_out, w_a.T, w_b.T, w_out.T,
                                               b_gate, g_post, gh)
    dqr, dkr, dv = _attn_bwd(qa, kr, v, dop)
    dproj, dlbl = _hgrn_bwd(proj, lbl, do, sprev, dproj)
    dqf, dkvf, dproj, dg_q, dg_kv = _mla_bwd(proj, dqr, dkr, dv, g_q, g_kv, w_uq_p.T, w_kv_p.T, rc, rs1, rs2, dproj)
    dx, dg_pre = _dh_dx(dproj, w_in_pt, x, dout, g_pre)

    dw_in_slots = _scatter_w_in(_matmul(ht, dproj, "dw_in"))
    dw_out = _matmul(mt, dy_bf, "dw_out")
    dw_a = _matmul(yat, dya_bf, "dw_a")
    dw_b = _matmul(ybt, dyb_bf, "dw_b")
    dw_uq_p = _matmul(cqt, dqf, "dw_uq")
    dw_kv_p = _matmul(ckvt, dkvf, "dw_kv")

    dw_uq = dw_uq_p.reshape(Q_LORA, HEADS, LANE)[:, :, :QK].reshape(Q_LORA, HEADS * QK)
    dw_ukv = jnp.concatenate([dw_kv_p[:, :HEADS * LANE].reshape(KV_LORA, HEADS, LANE)[:, :, :NOPE],
                              dw_kv_p[:, HEADS * LANE:].reshape(KV_LORA, HEADS, LANE)[:, :, :VDIM]],
                             axis=2).reshape(KV_LORA, 1024)
    loss = 0.5 / D_MODEL * jnp.sum(loss_vec)
    grads = dict(g_pre=dg_pre, b_gate=db_gate, g_q=dg_q, w_uq=dw_uq, g_kv=dg_kv, w_ukv=dw_ukv,
                 lb_logits=dlbl, g_hgrn=jnp.sum(dgh.reshape(HEADS, VDIM), axis=0, keepdims=True),
                 w_branch_a=dw_a, w_branch_b=dw_b, w_out=dw_out, g_post=dg_post)
    return loss, dx, dw_in_slots, grads


def _my_place():
    return lax.axis_index("x"), lax.axis_index("y"), lax.axis_index("c")


def _all_gather(blocks):
    n = len(blocks)

    def body(*refs):
        x_refs, out_refs = refs[:n], refs[n:2 * n]
        send_sems, recv_sems, local_sems = refs[2 * n:]
        x, y, c = _my_place()
        me, sibling = (x, y, c), (x, y, 1 - c)
        chips = [(1 - x, y), (x, 1 - y), (1 - x, 1 - y)]

        def slot(a, px, py, pc):
            return out_refs[a].at[4 * px + 2 * py + pc]

        def copy(a, k, blk, to, src=None):
            return pltpu.make_async_remote_copy(
                src_ref=slot(a, *blk) if src is None else src, dst_ref=slot(a, *blk),
                send_sem=send_sems.at[7 * a + k], recv_sem=recv_sems.at[7 * a + k],
                device_id=to, device_id_type=MESH_ID)

        mine = [pltpu.make_async_copy(x_refs[a], slot(a, *me), local_sems.at[a]) for a in range(n)]
        for cp in mine:
            cp.start()
        first = [copy(a, 0, me, sibling, src=x_refs[a]) for a in range(n)]
        first += [copy(a, 1 + j, me, (*chip, c), src=x_refs[a]) for a in range(n) for j, chip in enumerate(chips)]
        for cp in first:
            cp.start()
        passed = []
        for j, chip in enumerate(chips):
            for a in range(n):
                copy(a, 1 + j, (*chip, c), me).wait_recv()
                passed.append(copy(a, 4 + j, (*chip, c), sibling))
                passed[-1].start()
        for a in range(n):
            copy(a, 0, sibling, me).wait_recv()
        for j, chip in enumerate(chips):
            for a in range(n):
                copy(a, 4 + j, (*chip, 1 - c), me).wait_recv()
        for cp in first + passed:
            cp.wait_send()
        for cp in mine:
            cp.wait()

    return pl.pallas_call(
        body,
        out_shape=[jax.ShapeDtypeStruct((N_DEV,) + b.shape, b.dtype) for b in blocks],
        in_specs=[pl.BlockSpec(memory_space=pl.ANY)] * n,
        out_specs=[pl.BlockSpec(memory_space=pl.ANY)] * n,
        scratch_shapes=[pltpu.SemaphoreType.DMA((7 * n,)), pltpu.SemaphoreType.DMA((7 * n,)),
                        pltpu.SemaphoreType.DMA((n,))],
        name="gather_weights",
    )(*blocks)


def _exchange(sends):
    n = len(sends)

    def body(*refs):
        s_refs, r_refs = refs[:n], refs[n:2 * n]
        send_sems, recv_sems, local_sems = refs[2 * n:]
        x, y, c = _my_place()
        me = 4 * x + 2 * y + c
        mine = [pltpu.make_async_copy(s_refs[a].at[me], r_refs[a].at[me], local_sems.at[a]) for a in range(n)]
        for cp in mine:
            cp.start()
        copies = []
        for k in range(N_DEV - 1):
            fx, fy, fc = (k + 1) >> 2 & 1, (k + 1) >> 1 & 1, (k + 1) & 1
            px = 1 - x if fx else x
            py = 1 - y if fy else y
            pc = 1 - c if fc else c
            for a in range(n):
                copies.append(pltpu.make_async_remote_copy(
                    src_ref=s_refs[a].at[4 * px + 2 * py + pc], dst_ref=r_refs[a].at[me],
                    send_sem=send_sems.at[7 * a + k], recv_sem=recv_sems.at[7 * a + k],
                    device_id=(px, py, pc), device_id_type=MESH_ID))
        for cp in copies:
            cp.start()
        for cp in copies:
            cp.wait_recv()
        for cp in copies:
            cp.wait_send()
        for cp in mine:
            cp.wait()

    return pl.pallas_call(
        body,
        out_shape=[jax.ShapeDtypeStruct(t.shape, t.dtype) for t in sends],
        in_specs=[pl.BlockSpec(memory_space=pl.ANY)] * n,
        out_specs=[pl.BlockSpec(memory_space=pl.ANY)] * n,
        scratch_shapes=[pltpu.SemaphoreType.DMA((7 * n,)), pltpu.SemaphoreType.DMA((7 * n,)),
                        pltpu.SemaphoreType.DMA((n,))],
        name="exchange_grads",
    )(*sends)


def _adamw(g, w, m, v):
    c1 = 1.0 / (1.0 - ADAM_B1 ** ADAM_STEP)
    c2 = 1.0 / (1.0 - ADAM_B2 ** ADAM_STEP)
    nm = ADAM_B1 * m + (1.0 - ADAM_B1) * g
    nv = ADAM_B2 * v + (1.0 - ADAM_B2) * (g * g)
    d = -ADAM_LR * ((nm * c1) / (jnp.sqrt(nv * c2) + ADAM_EPS) + ADAM_WD * w)
    return d, nm, nv


def _sum8(r_ref):
    g = r_ref[0].astype(F32)
    for k in range(1, N_DEV):
        g = g + r_ref[k].astype(F32)
    return g


def _sum_adamw_rows(recv, w, m, v, name):
    rows, cols = w.shape
    tr = 256

    def body(r_ref, w_ref, m_ref, v_ref, g_ref, d_ref, nm_ref, nv_ref):
        g = _sum8(r_ref)
        g_ref[...] = g
        d_ref[...], nm_ref[...], nv_ref[...] = _adamw(g, w_ref[...], m_ref[...], v_ref[...])

    blk = pl.BlockSpec((tr, cols), lambda i: (i, 0))
    out = jax.ShapeDtypeStruct((rows, cols), F32)
    return pl.pallas_call(
        body,
        grid=(rows // tr,),
        in_specs=[pl.BlockSpec((N_DEV, tr, cols), lambda i: (0, i, 0)), blk, blk, blk],
        out_specs=[blk, blk, blk, blk],
        out_shape=[out, out, out, out],
        compiler_params=_params(("arbitrary",)),
        name=name,
    )(recv, w, m, v)


def _sum_adamw_whole(recvs, ws, ms, vs):
    n = len(ws)

    def body(*refs):
        r_refs, w_refs, m_refs, v_refs = refs[:n], refs[n:2 * n], refs[2 * n:3 * n], refs[3 * n:4 * n]
        outs = refs[4 * n:]
        for a in range(n):
            g = _sum8(r_refs[a])
            d, nm, nv = _adamw(g, w_refs[a][...], m_refs[a][...], v_refs[a][...])
            outs[a][...] = g
            outs[n + a][...] = d
            outs[2 * n + a][...] = nm
            outs[3 * n + a][...] = nv

    shapes = [jax.ShapeDtypeStruct(w.shape, F32) for w in ws]
    res = pl.pallas_call(
        body,
        out_shape=shapes * 4,
        compiler_params=pltpu.CompilerParams(vmem_limit_bytes=48 * 2**20),
        name="sum_adamw_small",
    )(*recvs, *ws, *ms, *vs)
    return res[:n], res[n:2 * n], res[2 * n:3 * n], res[3 * n:]


MATS = ("w_uq", "w_ukv", "w_branch_a", "w_branch_b", "w_out")
SMALL = ("g_pre", "b_gate", "g_q", "g_kv", "lb_logits", "g_hgrn", "g_post")
SMALL_ROWS = (1, 2, 1, 1, 1, 1, 1)
SMALL_SHAPE = dict(g_pre=(1, 1024), b_gate=(1, 2048), g_q=(1, 768), g_kv=(1, 256), lb_logits=(2, 512),
                   g_hgrn=(1, 64), g_post=(1, 1024))
COL_SHARDED = dict(w_uq=False, w_ukv=True, w_branch_a=True, w_branch_b=True, w_out=False)
ORDER = ("g_pre", "w_in", "b_gate", "g_q", "w_uq", "g_kv", "w_ukv", "lb_logits", "g_hgrn",
         "w_branch_a", "w_branch_b", "w_out", "g_post")


def _pack_small(t):
    parts = []
    for n, r in zip(SMALL, SMALL_ROWS):
        flat = t[n].reshape(1, -1)
        parts.append(jnp.pad(flat, ((0, 0), (0, r * 1024 - flat.shape[1]))).reshape(r, 1024))
    return jnp.concatenate(parts, axis=0)


def _unpack_small(p):
    out, r0 = {}, 0
    for n, r in zip(SMALL, SMALL_ROWS):
        shp = SMALL_SHAPE[n]
        out[n] = p[r0:r0 + r].reshape(1, -1)[:, :shp[0] * shp[1]].reshape(shp)
        r0 += r
    return out


def _to_slots(name, full):
    r, c = full.shape
    if COL_SHARDED[name]:
        return full.reshape(r, N_DEV, c // N_DEV).transpose(1, 0, 2)
    return full.reshape(N_DEV, r // N_DEV, c)


def _from_slots(name, slots):
    _, r, c = slots.shape
    if COL_SHARDED[name]:
        return slots.transpose(1, 0, 2).reshape(r, N_DEV * c)
    return slots.reshape(N_DEV * r, c)


def kernel(x, g_pre, w_in, b_gate, g_q, w_uq, g_kv, w_ukv, lb_logits, g_hgrn, w_branch_a, w_branch_b, w_out, g_post, loss_target, m_g_pre, m_w_in, m_b_gate, m_g_q, m_w_uq, m_g_kv, m_w_ukv, m_lb_logits, m_g_hgrn, m_w_branch_a, m_w_branch_b, m_w_out, m_g_post, v_g_pre, v_w_in, v_b_gate, v_g_q, v_w_uq, v_g_kv, v_w_ukv, v_lb_logits, v_g_hgrn, v_w_branch_a, v_w_branch_b, v_w_out, v_g_post):
    w = dict(w_in=w_in[0], w_uq=w_uq[0], w_ukv=w_ukv[0], w_branch_a=w_branch_a[0], w_branch_b=w_branch_b[0],
             w_out=w_out[0], g_pre=g_pre, b_gate=b_gate, g_q=g_q, g_kv=g_kv, lb_logits=lb_logits, g_hgrn=g_hgrn,
             g_post=g_post)
    mom = dict(w_in=m_w_in[0], w_uq=m_w_uq[0], w_ukv=m_w_ukv[0], w_branch_a=m_w_branch_a[0],
               w_branch_b=m_w_branch_b[0], w_out=m_w_out[0], g_pre=m_g_pre, b_gate=m_b_gate, g_q=m_g_q, g_kv=m_g_kv,
               lb_logits=m_lb_logits, g_hgrn=m_g_hgrn, g_post=m_g_post)
    var = dict(w_in=v_w_in[0], w_uq=v_w_uq[0], w_ukv=v_w_ukv[0], w_branch_a=v_w_branch_a[0],
               w_branch_b=v_w_branch_b[0], w_out=v_w_out[0], g_pre=v_g_pre, b_gate=v_b_gate, g_q=v_g_q, g_kv=v_g_kv,
               lb_logits=v_lb_logits, g_hgrn=v_g_hgrn, g_post=v_g_post)

    gathered = _all_gather([w[n].astype(BF16) for n in ("w_in",) + MATS])
    full = {n: _from_slots(n, g) for n, g in zip(MATS, gathered[1:])}

    loss, dx, dw_in_slots, grads = _local_step(
        x[0], loss_target[0], gathered[0], full["w_uq"], full["w_ukv"], full["w_branch_a"], full["w_branch_b"],
        full["w_out"], g_pre, b_gate, g_q, g_kv, lb_logits, g_hgrn, g_post)

    small = jnp.broadcast_to(_pack_small(grads)[None], (N_DEV, 8, 1024))
    sends = [dw_in_slots] + [_to_slots(n, grads[n]).astype(BF16) for n in MATS] + [small]
    recvs = _exchange(sends)

    g_in, d_in, m_in, v_in = _sum_adamw_rows(recvs[0], w["w_in"], mom["w_in"], var["w_in"], "sum_adamw_w_in")
    names = MATS + ("small",)
    pk = lambda t: [t[n] for n in MATS] + [_pack_small(t)]
    res = _sum_adamw_whole(recvs[1:], pk(w), pk(mom), pk(var))

    outs = []
    for kind, big in zip(res, (g_in, d_in, m_in, v_in)):
        t = dict(zip(names, kind))
        t = {**{n: t[n][None] for n in MATS}, **_unpack_small(t["small"]), "w_in": big[None]}
        outs += [t[n] for n in ORDER]
    total = lax.psum(loss, ("x", "y", "c"))
    return (total, dx[None], *outs)
```

```python
import math

import jax
import jax.numpy as jnp
from jax import lax
from jax.experimental import pallas as pl
from jax.experimental.pallas import tpu as pltpu

F32, BF16 = jnp.float32, jnp.bfloat16

D_MODEL = 1024
EPS = 1e-6
HEADS = 8
NOPE, ROPE, VDIM = 64, 32, 64
QK = NOPE + ROPE
Q_LORA, KV_LORA = 768, 256
ROPE_THETA = 10000.0
ATT_CHUNK_SHIFT = 6
HG_BLOCK = 32
HG_WIDTH = 512
D_IN = 5664
D_IN_PAD = 5760
W_IN_SHARD = D_IN // 8
N_DEV = 8
LANE = 128

ADAM_LR, ADAM_B1, ADAM_B2, ADAM_EPS, ADAM_WD, ADAM_STEP = 0.001, 0.9, 0.999, 1e-08, 0.01, 10

W_IN_SEGMENTS = ((3616, 5664, 0), (1056, 1568, 2048), (3104, 3616, 2560), (1568, 3104, 3072),
                 (0, 1024, 4608), (1024, 1056, 5696))
W_IN_ZERO = ((5632, 5696), (5728, 5760))

NT = (((1,), (1,)), ((), ()))
TN = (((0,), (0,)), ((), ()))
MESH_ID = pl.DeviceIdType.MESH


def _w_in_pieces():
    out = []
    for lo, hi, dst in W_IN_SEGMENTS:
        c = lo
        while c < hi:
            p = c // W_IN_SHARD
            e = min(hi, (p + 1) * W_IN_SHARD)
            out.append((p, c - p * W_IN_SHARD, e - p * W_IN_SHARD, dst + c - lo))
            c = e
    return out


def _params(sem, vmem_mb=48):
    return pltpu.CompilerParams(dimension_semantics=sem, vmem_limit_bytes=vmem_mb * 2**20)


def _dot(a, b):
    return jnp.dot(a, b, preferred_element_type=F32)


def _dotg(a, b, dims):
    return lax.dot_general(a, b, dims, preferred_element_type=F32)


def _split2(x):
    hi = x.astype(BF16)
    return hi, (x - hi.astype(F32)).astype(BF16)


def _sel_left(m01, x):
    hi, lo = _split2(x)
    return _dot(m01, hi) + _dot(m01, lo)


def _sel_right(x, m01):
    hi, lo = _split2(x)
    return _dot(hi, m01) + _dot(lo, m01)


def _hi_lo(x):
    hi = x.astype(BF16).astype(F32)
    return hi, x - hi


def _sigmoid(x):
    return 0.5 * jnp.tanh(0.5 * x) + 0.5


def _rope(x, c, s1, s2):
    return x * c + pltpu.roll(x, 112, 1) * s1 + pltpu.roll(x, 16, 1) * s2


def _unrope(d, c, s1, s2):
    return d * c + pltpu.roll(d * s1, 16, 1) + pltpu.roll(d * s2, 112, 1)


def _assemble_w_in(slots):
    tm = 256
    pieces = _w_in_pieces()

    def body(s_ref, w_ref, wt_ref):
        for lo, hi in W_IN_ZERO:
            w_ref[:, lo:hi] = jnp.zeros((tm, hi - lo), BF16)
        for p, lo, hi, dst in pieces:
            w_ref[:, dst:dst + hi - lo] = s_ref[p, :, lo:hi]
        wt_ref[...] = w_ref[...].T

    return pl.pallas_call(
        body,
        grid=(D_MODEL // tm,),
        in_specs=[pl.BlockSpec((N_DEV, tm, W_IN_SHARD), lambda i: (0, i, 0))],
        out_specs=[pl.BlockSpec((tm, D_IN_PAD), lambda i: (i, 0)), pl.BlockSpec((D_IN_PAD, tm), lambda i: (0, i))],
        out_shape=[jax.ShapeDtypeStruct((D_MODEL, D_IN_PAD), BF16), jax.ShapeDtypeStruct((D_IN_PAD, D_MODEL), BF16)],
        compiler_params=_params(("arbitrary",)),
        name="assemble_w_in",
    )(slots)


def _scatter_w_in(dw):
    tm = 256
    pieces = _w_in_pieces()

    def body(d_ref, s_ref):
        for p, lo, hi, dst in pieces:
            s_ref[p, :, lo:hi] = d_ref[:, dst:dst + hi - lo].astype(BF16)

    return pl.pallas_call(
        body,
        grid=(D_MODEL // tm,),
        in_specs=[pl.BlockSpec((tm, D_IN_PAD), lambda i: (i, 0))],
        out_specs=pl.BlockSpec((N_DEV, tm, W_IN_SHARD), lambda i: (0, i, 0)),
        out_shape=jax.ShapeDtypeStruct((N_DEV, D_MODEL, W_IN_SHARD), BF16),
        compiler_params=_params(("arbitrary",)),
        name="scatter_w_in",
    )(dw)


def _norm_proj(x, g_pre, w):
    s, n = x.shape[0], w.shape[1]
    tm, tn = 512, 1152
    ni = s // tm

    def body(x_ref, g_ref, w_ref, proj_ref, ht_ref, h_ref):
        rows = pl.ds(pl.multiple_of(pl.program_id(1) * tm, tm), tm)

        @pl.when(pl.program_id(0) == 0)
        def _():
            xv = x_ref[...]
            r = lax.rsqrt(jnp.mean(xv * xv, axis=-1, keepdims=True) + EPS)
            h = (xv * r * g_ref[...]).astype(BF16)
            h_ref[rows, :] = h
            ht_ref[...] = h.T

        proj_ref[...] = _dot(h_ref[rows, :], w_ref[...])

    first = lambda j, i: jnp.where(j == 0, i, ni - 1)
    return pl.pallas_call(
        body,
        grid=(n // tn, ni),
        in_specs=[
            pl.BlockSpec((tm, D_MODEL), lambda j, i: (first(j, i), 0)),
            pl.BlockSpec((1, D_MODEL), lambda j, i: (0, 0)),
            pl.BlockSpec((D_MODEL, tn), lambda j, i: (0, j)),
        ],
        out_specs=[
            pl.BlockSpec((tm, tn), lambda j, i: (i, j)),
            pl.BlockSpec((D_MODEL, tm), lambda j, i: (0, first(j, i))),
        ],
        out_shape=[jax.ShapeDtypeStruct((s, n), F32), jax.ShapeDtypeStruct((D_MODEL, s), BF16)],
        scratch_shapes=[pltpu.VMEM((s, D_MODEL), BF16)],
        compiler_params=_params(("arbitrary", "arbitrary")),
        name="norm_proj",
    )(x, g_pre, w)


def _mla_prep(proj, g_q, g_kv, w_uq_p, w_kv_p, rc, rs1, rs2):
    s = proj.shape[0]
    tm = 256
    scale = 1.0 / math.sqrt(QK)

    def body(cq_ref, ckv_ref, kpe_ref, gq_ref, gkv_ref, wuq_ref, wkv_ref, c_ref, s1_ref, s2_ref,
             qr_ref, kr_ref, v_ref, cqt_ref, ckvt_ref):
        cq = cq_ref[...]
        r = lax.rsqrt(jnp.mean(cq * cq, axis=-1, keepdims=True) + EPS)
        cqn = (cq * r * gq_ref[...]).astype(BF16)
        cqt_ref[...] = cqn.T
        q = _dot(cqn, wuq_ref[...])
        ckv = ckv_ref[...]
        r = lax.rsqrt(jnp.mean(ckv * ckv, axis=-1, keepdims=True) + EPS)
        ckvn = (ckv * r * gkv_ref[...]).astype(BF16)
        ckvt_ref[...] = ckvn.T
        kv = _dot(ckvn, wkv_ref[...])
        c, s1, s2 = c_ref[...], s1_ref[...], s2_ref[...]
        lane = lax.broadcasted_iota(jnp.int32, (tm, LANE), 1)
        kpe = _rope(kpe_ref[...], c, s1, s2) + jnp.where((lane == QK) | (lane == QK + 1), 1.0, 0.0)
        vone = jnp.where((lane == VDIM) | (lane == VDIM + 1), 1.0, 0.0)
        for h in range(HEADS):
            sl = slice(LANE * h, LANE * (h + 1))
            qr_ref[:, sl] = (_rope(q[:, sl], c, s1, s2) * scale).astype(BF16)
            kr_ref[:, sl] = (kv[:, sl] + kpe).astype(BF16)
            v_ref[:, sl] = (kv[:, HEADS * LANE + LANE * h:HEADS * LANE + LANE * (h + 1)] + vone).astype(BF16)

    row = lambda w, j: pl.BlockSpec((tm, w), lambda i: (i, j))
    col = lambda w: pl.BlockSpec((w, tm), lambda i: (0, i))
    full = lambda a: pl.BlockSpec(a.shape, lambda i: (0, 0))
    return pl.pallas_call(
        body,
        grid=(s // tm,),
        in_specs=[row(768, 6), row(256, 21), row(128, 44), full(g_q), full(g_kv), full(w_uq_p), full(w_kv_p),
                  row(128, 0), row(128, 0), row(128, 0)],
        out_specs=[row(1024, 0), row(1024, 0), row(1024, 0), col(768), col(256)],
        out_shape=[jax.ShapeDtypeStruct((s, 1024), BF16), jax.ShapeDtypeStruct((s, 1024), BF16),
                   jax.ShapeDtypeStruct((s, 1024), BF16), jax.ShapeDtypeStruct((768, s), BF16),
                   jax.ShapeDtypeStruct((256, s), BF16)],
        compiler_params=_params(("arbitrary",)),
        name="mla_prep",
    )(proj, proj, proj, g_q, g_kv, w_uq_p, w_kv_p, rc, rs1, rs2)


ATT_T = 512


def _chunk_mask(transposed):
    r = lax.broadcasted_iota(jnp.int32, (ATT_T, ATT_T), 0) >> ATT_CHUNK_SHIFT
    c = lax.broadcasted_iota(jnp.int32, (ATT_T, ATT_T), 1) >> ATT_CHUNK_SHIFT
    return (r <= c) if transposed else (c <= r)


def _attn_fwd(qr, kr, vp):
    s = qr.shape[0]
    t = ATT_T

    def body(q_ref, k_ref, v_ref, o_ref, qa_ref):
        qi = pl.program_id(1)
        lane = lax.broadcasted_iota(jnp.int32, (t, LANE), 1)
        sls = [slice(LANE * a, LANE * (a + 1)) for a in range(2)]
        qs = [q_ref[:, sl] for sl in sls]

        def step(j, carry, masked):
            rows = pl.ds(pl.multiple_of(j * t, t), t)
            out = []
            for a in range(2):
                m, acc = carry[a]
                sc = _dotg(qs[a], k_ref[rows, sls[a]], NT)
                if masked:
                    sc = jnp.where(_chunk_mask(False), sc, -1e30)
                m_new = jnp.maximum(m, jnp.max(sc, axis=-1, keepdims=True))
                p = jnp.exp(sc - m_new).astype(BF16)
                acc = jnp.exp(m - m_new) * acc + _dot(p, v_ref[rows, sls[a]])
                out.append((m_new, acc))
            return tuple(out)

        init = tuple((jnp.full((t, 1), -1e30, F32), jnp.zeros((t, LANE), F32)) for _ in range(2))
        carry = lax.fori_loop(0, qi, lambda j, c: step(j, c, False), init)
        carry = step(qi, carry, True)
        outs = []
        for a in range(2):
            m, acc = carry[a]
            l = acc[:, VDIM:VDIM + 1]
            outs.append(acc / l)
            hi, lo_part = _hi_lo(-(m + jnp.log(l)))
            qa = jnp.where(lane == QK, hi, jnp.where(lane == QK + 1, lo_part, qs[a].astype(F32)))
            qa_ref[:, sls[a]] = qa.astype(BF16)
        o_ref[...] = jnp.where(lane < VDIM, outs[0], pltpu.roll(outs[1], VDIM, 1))

    return pl.pallas_call(
        body,
        grid=(HEADS // 2, s // t),
        in_specs=[
            pl.BlockSpec((t, 2 * LANE), lambda h, i: (i, h)),
            pl.BlockSpec((s, 2 * LANE), lambda h, i: (0, h)),
            pl.BlockSpec((s, 2 * LANE), lambda h, i: (0, h)),
        ],
        out_specs=[
            pl.BlockSpec((t, LANE), lambda h, i: (i, h)),
            pl.BlockSpec((t, 2 * LANE), lambda h, i: (i, h)),
        ],
        out_shape=[jax.ShapeDtypeStruct((s, 512), F32), jax.ShapeDtypeStruct((s, 1024), BF16)],
        compiler_params=_params(("arbitrary", "arbitrary")),
        name="attn_fwd",
    )(qr, kr, vp)


def _attn_bwd(qa, kr, vp, dop):
    s = qa.shape[0]
    t = ATT_T
    nq = s // t

    def body(q_ref, k_ref, v_ref, do_ref, dq_ref, dk_ref, dv_ref):
        j = pl.program_id(1)
        sls = [slice(LANE * a, LANE * (a + 1)) for a in range(2)]

        @pl.when(j == 0)
        def _():
            dq_ref[...] = jnp.zeros_like(dq_ref)

        dk_ref[...] = jnp.zeros_like(dk_ref)
        dv_ref[...] = jnp.zeros_like(dv_ref)
        ks = [k_ref[:, sl] for sl in sls]
        vs = [v_ref[:, sl] for sl in sls]

        def step(i, masked):
            rows = pl.ds(pl.multiple_of(i * t, t), t)
            for a in range(2):
                q = q_ref[rows, sls[a]]
                do = do_ref[rows, sls[a]]
                sc = _dotg(ks[a], q, NT)
                if masked:
                    sc = jnp.where(_chunk_mask(True), sc, -1e30)
                p = jnp.exp(sc)
                ds = (p * _dotg(vs[a], do, NT)).astype(BF16)
                dv_ref[:, sls[a]] += _dot(p.astype(BF16), do)
                dk_ref[:, sls[a]] += _dot(ds, q)
                dq_ref[rows, sls[a]] += _dotg(ds, ks[a], TN)

        step(j, True)

        def loop(i, c):
            step(i, False)
            return c

        lax.fori_loop(j + 1, nq, loop, 0)

    blk = pl.BlockSpec((t, 2 * LANE), lambda h, j: (j, h))
    whole = pl.BlockSpec((s, 2 * LANE), lambda h, j: (0, h))
    out = jax.ShapeDtypeStruct((s, 1024), F32)
    return pl.pallas_call(
        body,
        grid=(HEADS // 2, nq),
        in_specs=[whole, blk, blk, whole],
        out_specs=[whole, blk, blk],
        out_shape=[out, out, out],
        compiler_params=_params(("arbitrary", "arbitrary")),
        name="attn_bwd",
    )(qa, kr, vp, dop)


HG_T = 256
HG_NC = HG_T // HG_BLOCK


def _hg_consts():
    r = jnp.arange(HG_T)[:, None]
    c = jnp.arange(HG_T)[None, :]
    same = (r // HG_BLOCK) == (c // HG_BLOCK)
    mcum = (same & (c <= r)).astype(BF16)
    mrev = (same & (c >= r)).astype(BF16)
    msum = same.astype(BF16)
    a = jnp.arange(LANE)
    bd = ((a[:, None] < 64) == (a[None, :] < 64)).astype(F32)
    return mcum, mrev, msum, bd


def _hg_pre(hq, hf, lbl, mcum, msum):
    lb = _sigmoid(lbl[0:1, :] - lbl[1:2, :])
    sig = _sigmoid(hf)
    f = lb + (1.0 - lb) * sig
    lf = jnp.log(f)
    b = _sel_left(mcum, lf)
    big_l = _sel_left(msum, lf)
    k = 1.0 - f
    qd = hq * jnp.exp(b)
    ki = k * jnp.exp(-b)
    ke = k * jnp.exp(big_l - b)
    return lb, sig, f, b, big_l, qd, ki, ke


def _stack_pair(xp, lo):
    return jnp.concatenate([jnp.where(lo, xp, 0.0), jnp.where(lo, 0.0, xp)], axis=0)


def _hgrn_fwd(proj, lbl):
    s = proj.shape[0]
    t = HG_T
    mcum, _, msum, bd = _hg_consts()

    def body(hq_ref, hf_ref, hi_ref, lbl_ref, mcum_ref, msum_ref, bd_ref, o_ref, sp_ref, st_ref):
        @pl.when(pl.program_id(0) == 0)
        def _():
            st_ref[...] = jnp.zeros_like(st_ref)

        mc = mcum_ref[...]
        _, _, _, _, big_l, qd, ki, ke = _hg_pre(hq_ref[...], hf_ref[...], lbl_ref[...], mc, msum_ref[...])
        el = jnp.exp(big_l)
        hi = hi_ref[...]
        lo = lax.broadcasted_iota(jnp.int32, (t, LANE), 1) < 64
        mask2 = jnp.concatenate([mc, mc], axis=0) > 0.5
        for p in range(HEADS // 2):
            sl = slice(LANE * p, LANE * (p + 1))
            vp = hi[:, sl].astype(BF16)
            q2 = _stack_pair(qd[:, sl], lo).astype(BF16)
            a2 = jnp.where(mask2, _dotg(q2, ki[:, sl].astype(BF16), NT), 0.0)
            r2 = _dot(a2.astype(BF16), vp)
            o_intra = jnp.where(lo, r2[:t], r2[t:])
            qb = qd[:, sl].astype(BF16)
            kb = ke[:, sl].astype(BF16)
            st = st_ref[p]
            for c in range(HG_NC):
                rows = slice(HG_BLOCK * c, HG_BLOCK * (c + 1))
                sp_ref[c, :, sl] = st
                o_ref[rows, sl] = o_intra[rows] + _dotg(qb[rows], st.astype(BF16), NT)
                u = _dotg(vp[rows], kb[rows], TN) * bd_ref[...]
                st = st * el[HG_BLOCK * c:HG_BLOCK * c + 1, sl] + u
            st_ref[p] = st

    row = lambda j: pl.BlockSpec((t, HG_WIDTH), lambda i: (i, j))
    full = lambda a: pl.BlockSpec(a.shape, lambda i: (0, 0))
    return pl.pallas_call(
        body,
        grid=(s // t,),
        in_specs=[row(6), row(7), row(8), full(lbl), full(mcum), full(msum), full(bd)],
        out_specs=[row(0), pl.BlockSpec((HG_NC, LANE, HG_WIDTH), lambda i: (i, 0, 0))],
        out_shape=[jax.ShapeDtypeStruct((s, HG_WIDTH), F32),
                   jax.ShapeDtypeStruct((s // HG_BLOCK, LANE, HG_WIDTH), F32)],
        scratch_shapes=[pltpu.VMEM((HEADS // 2, LANE, LANE), F32)],
        compiler_params=_params(("arbitrary",)),
        name="hgrn_fwd",
    )(proj, proj, proj, lbl, mcum, msum, bd)


def _hgrn_bwd(proj, lbl, do, sprev, dproj):
    s = proj.shape[0]
    t = HG_T
    nt = s // t
    mcum, mrev, msum, bd = _hg_consts()

    def body(hq_ref, hf_ref, hi_ref, lbl_ref, do_ref, sp_ref, mcum_ref, mrev_ref, msum_ref, bd_ref,
             dproj_in, dh_ref, dlbl_ref, g_ref):
        del dproj_in

        @pl.when(pl.program_id(0) == 0)
        def _():
            g_ref[...] = jnp.zeros_like(g_ref)
            dlbl_ref[...] = jnp.zeros_like(dlbl_ref)

        mc = mcum_ref[...]
        lb, sig, f, b, big_l, qd, ki, ke = _hg_pre(hq_ref[...], hf_ref[...], lbl_ref[...], mc, msum_ref[...])
        el = jnp.exp(big_l)
        hi = hi_ref[...]
        dov = do_ref[...]
        lo = lax.broadcasted_iota(jnp.int32, (t, LANE), 1) < 64
        mask2 = jnp.concatenate([mc, mc], axis=0) > 0.5
        dqd_parts, dke_parts, dv_parts, del_parts, dki_parts = [], [], [], [], []
        for p in range(HEADS // 2):
            sl = slice(LANE * p, LANE * (p + 1))
            vp = hi[:, sl].astype(BF16)
            q2 = _stack_pair(qd[:, sl], lo).astype(BF16)
            kip = ki[:, sl].astype(BF16)
            do2 = _stack_pair(dov[:, sl], lo).astype(BF16)
            a2 = jnp.where(mask2, _dotg(q2, kip, NT), 0.0).astype(BF16)
            da2 = jnp.where(mask2, _dotg(do2, vp, NT), 0.0).astype(BF16)
            r2 = _dot(da2, kip)
            dki_parts.append(_dotg(da2, q2, TN))
            qb = qd[:, sl].astype(BF16)
            kb = ke[:, sl].astype(BF16)
            dob = dov[:, sl].astype(BF16)
            g = g_ref[p]
            dqd_c, dv_c, dke_c, del_c = [], [], [], []
            for c in range(HG_NC - 1, -1, -1):
                rows = slice(HG_BLOCK * c, HG_BLOCK * (c + 1))
                gb = g.astype(BF16)
                st = sp_ref[c, :, sl]
                dqd_c.append(_dot(dob[rows], st.astype(BF16)))
                dv_c.append(_dotg(kb[rows], gb, NT))
                dke_c.append(_dot(vp[rows], gb))
                del_c.append(jnp.broadcast_to(jnp.sum(g * st, axis=0, keepdims=True), (HG_BLOCK, LANE)))
                g = g * el[HG_BLOCK * c:HG_BLOCK * c + 1, sl] + _dotg(dob[rows], qb[rows], TN) * bd_ref[...]
            g_ref[p] = g
            up = lambda parts: jnp.concatenate(parts[::-1], axis=0)
            dqd_parts.append(jnp.where(lo, r2[:t], r2[t:]) + up(dqd_c))
            dv_parts.append(_dotg(a2, do2, TN) + up(dv_c))
            dke_parts.append(up(dke_c))
            del_parts.append(up(del_c))
        wide = lambda parts: jnp.concatenate(parts, axis=1)
        dqd, dke, dki, dvv, del_rows = wide(dqd_parts), wide(dke_parts), wide(dki_parts), wide(dv_parts), wide(del_parts)
        dh_ref[:, :HG_WIDTH] = (dqd * jnp.exp(b)).astype(BF16)
        dh_ref[:, 2 * HG_WIDTH:] = dvv.astype(BF16)
        dke_ke = dke * ke
        db = dqd * qd - dki * ki - dke_ke
        dl_rows = _sel_left(msum_ref[...], dke_ke) + del_rows * el
        is_last = (lax.broadcasted_iota(jnp.int32, (t, HG_WIDTH), 0) & (HG_BLOCK - 1)) == HG_BLOCK - 1
        db = db + jnp.where(is_last, dl_rows, 0.0)
        dlf = _sel_left(mrev_ref[...], db)
        dk = dki * jnp.exp(-b) + dke * jnp.exp(big_l - b)
        df = dlf / f - dk
        dh_ref[:, HG_WIDTH:2 * HG_WIDTH] = (df * (1.0 - lb) * sig * (1.0 - sig)).astype(BF16)
        dlb = jnp.sum(df * (1.0 - sig), axis=0, keepdims=True) * lb * (1.0 - lb)
        dlbl_ref[0:1, :] += dlb
        dlbl_ref[1:2, :] -= dlb

    rrow = lambda j: pl.BlockSpec((t, HG_WIDTH), lambda i: (nt - 1 - i, j))
    full = lambda a: pl.BlockSpec(a.shape, lambda i: (0, 0))
    return pl.pallas_call(
        body,
        grid=(nt,),
        in_specs=[rrow(6), rrow(7), rrow(8), full(lbl), rrow(0),
                  pl.BlockSpec((HG_NC, LANE, HG_WIDTH), lambda i: (nt - 1 - i, 0, 0)),
                  full(mcum), full(mrev), full(msum), full(bd), pl.BlockSpec(memory_space=pl.ANY)],
        out_specs=[pl.BlockSpec((t, 3 * HG_WIDTH), lambda i: (nt - 1 - i, 2)),
                   pl.BlockSpec((2, HG_WIDTH), lambda i: (0, 0))],
        out_shape=[jax.ShapeDtypeStruct(dproj.shape, BF16), jax.ShapeDtypeStruct((2, HG_WIDTH), F32)],
        input_output_aliases={10: 0},
        scratch_shapes=[pltpu.VMEM((HEADS // 2, LANE, LANE), F32)],
        compiler_params=_params(("arbitrary",)),
        name="hgrn_bwd",
    )(proj, proj, proj, lbl, do, sprev, mcum, mrev, msum, bd, dproj)


def _tail(x, tgt, proj, attn, o, w_a, w_b, w_out, w_at, w_bt, w_outt, b_gate, g_post, gh):
    s = x.shape[0]
    tm = 128
    ones64 = (jnp.arange(HG_WIDTH)[:, None] // 64 == jnp.arange(HG_WIDTH)[None, :] // 64).astype(BF16)

    def body(x_ref, t_ref, ml_ref, ga_ref, gb_ref, at_ref, o_ref, wa_ref, wb_ref, wo_ref, wat_ref, wbt_ref, wot_ref,
             bg_ref, gp_ref, gh_ref, ones_ref,
             dout_ref, dpj_ref, dop_ref, do_ref, mt_ref, dy_ref, yat_ref, dya_ref, ybt_ref, dyb_ref,
             loss_ref, dgp_ref, dbg_ref, dgh_ref):
        @pl.when(pl.program_id(0) == 0)
        def _():
            loss_ref[...] = jnp.zeros_like(loss_ref)
            dgp_ref[...] = jnp.zeros_like(dgp_ref)
            dbg_ref[...] = jnp.zeros_like(dbg_ref)
            dgh_ref[...] = jnp.zeros_like(dgh_ref)

        ones = ones_ref[...]
        gate_a = ga_ref[...]
        sa = _sigmoid(gate_a)
        silu_a = gate_a * sa
        attn_v = at_ref[...]
        ya_in = attn_v * silu_a
        ov = o_ref[...]
        ro = lax.rsqrt(_sel_right(ov * ov, ones) * (1.0 / 64.0) + EPS)
        ohat = ov * ro
        ghv = gh_ref[...]
        on = ohat * ghv
        gate_b = gb_ref[...]
        sb = _sigmoid(gate_b)
        silu_b = gate_b * sb
        yb_in = on * silu_b
        ya_bf = ya_in.astype(BF16)
        yb_bf = yb_in.astype(BF16)
        yat_ref[...] = ya_bf.T
        ybt_ref[...] = yb_bf.T
        y_a = _dot(ya_bf, wa_ref[...])
        y_b = _dot(yb_bf, wb_ref[...])
        gts = _sigmoid(ml_ref[...] + bg_ref[...])
        g_a = gts[:, :D_MODEL]
        g_b = gts[:, D_MODEL:]
        m_bf = (g_a * y_a + g_b * y_b).astype(BF16)
        mt_ref[...] = m_bf.T
        y = _dot(m_bf, wo_ref[...])
        r1 = lax.rsqrt(jnp.mean(y * y, axis=-1, keepdims=True) + EPS)
        yn = y * r1
        gp = gp_ref[...]
        e = x_ref[...] + yn * gp - t_ref[...]
        loss_ref[...] += jnp.sum(e * e, axis=0, keepdims=True)
        dout = e * (1.0 / D_MODEL)
        dout_ref[...] = dout
        dgp_ref[...] += jnp.sum(dout * yn, axis=0, keepdims=True)
        dyn = dout * gp
        dy = r1 * (dyn - yn * jnp.mean(dyn * yn, axis=-1, keepdims=True))
        dy_bf = dy.astype(BF16)
        dy_ref[...] = dy_bf
        dm = _dot(dy_bf, wot_ref[...])
        dml_a = dm * y_a * g_a * (1.0 - g_a)
        dml_b = dm * y_b * g_b * (1.0 - g_b)
        dpj_ref[:, :D_MODEL] = dml_a.astype(BF16)
        dpj_ref[:, D_MODEL:2 * D_MODEL] = dml_b.astype(BF16)
        dbg_ref[:, :D_MODEL] += jnp.sum(dml_a, axis=0, keepdims=True)
        dbg_ref[:, D_MODEL:] += jnp.sum(dml_b, axis=0, keepdims=True)
        dya_bf = (dm * g_a).astype(BF16)
        dyb_bf = (dm * g_b).astype(BF16)
        dya_ref[...] = dya_bf
        dyb_ref[...] = dyb_bf
        dya_in = _dot(dya_bf, wat_ref[...])
        dyb_in = _dot(dyb_bf, wbt_ref[...])
        dattn = dya_in * silu_a
        delta = _sel_right(dattn * attn_v, ones)
        lane = lax.broadcasted_iota(jnp.int32, (tm, LANE), 1)
        for p in range(HEADS // 2):
            sl = slice(LANE * p, LANE * (p + 1))
            xs = (dattn[:, sl], pltpu.roll(dattn[:, sl], VDIM, 1))
            nds = (-pltpu.roll(delta[:, sl], VDIM, 1), -delta[:, sl])
            for a in range(2):
                hi, lo_part = _hi_lo(nds[a])
                blk = jnp.where(lane < VDIM, xs[a], jnp.where(lane == VDIM, hi, jnp.where(lane == VDIM + 1, lo_part, 0.0)))
                dop_ref[:, LANE * (2 * p + a):LANE * (2 * p + a + 1)] = blk.astype(BF16)
        dpj_ref[:, 2 * D_MODEL:2 * D_MODEL + HG_WIDTH] = (
            dya_in * attn_v * (sa * (1.0 + gate_a * (1.0 - sa)))).astype(BF16)
        don = dyb_in * silu_b
        dpj_ref[:, 2 * D_MODEL + HG_WIDTH:] = (dyb_in * on * (sb * (1.0 + gate_b * (1.0 - sb)))).astype(BF16)
        dgh_ref[...] += jnp.sum(don * ohat, axis=0, keepdims=True)
        dohat = don * ghv
        do_ref[...] = ro * (dohat - ohat * (_sel_right(dohat * ohat, ones) * (1.0 / 64.0)))

    row = lambda w, j: pl.BlockSpec((tm, w), lambda i: (i, j))
    col = lambda w: pl.BlockSpec((w, tm), lambda i: (0, i))
    full = lambda a: pl.BlockSpec(a.shape, lambda i: (0, 0))
    acc = lambda w: pl.BlockSpec((1, w), lambda i: (0, 0))
    sds = lambda w, dt: jax.ShapeDtypeStruct((s, w), dt)
    sdt = lambda w: jax.ShapeDtypeStruct((w, s), BF16)
    return pl.pallas_call(
        body,
        grid=(s // tm,),
        in_specs=[row(1024, 0), row(1024, 0), row(2048, 0), row(512, 4), row(512, 5), row(512, 0), row(512, 0),
                  full(w_a), full(w_b), full(w_out), full(w_at), full(w_bt), full(w_outt),
                  full(b_gate), full(g_post), full(gh), full(ones64)],
        out_specs=[row(1024, 0), row(3072, 0), row(1024, 0), row(512, 0),
                   col(1024), row(1024, 0), col(512), row(1024, 0), col(512), row(1024, 0),
                   acc(1024), acc(1024), acc(2048), acc(512)],
        out_shape=[sds(1024, F32), sds(D_IN_PAD, BF16), sds(1024, BF16), sds(512, F32),
                   sdt(1024), sds(1024, BF16), sdt(512), sds(1024, BF16), sdt(512), sds(1024, BF16),
                   jax.ShapeDtypeStruct((1, 1024), F32), jax.ShapeDtypeStruct((1, 1024), F32),
                   jax.ShapeDtypeStruct((1, 2048), F32), jax.ShapeDtypeStruct((1, 512), F32)],
        compiler_params=_params(("arbitrary",), 56),
        name="tail",
    )(x, tgt, proj, proj, proj, attn, o, w_a, w_b, w_out, w_at, w_bt, w_outt, b_gate, g_post, gh, ones64)


def _mla_bwd(proj, dqr, dkr, dv, g_q, g_kv, w_uq_pt, w_kv_pt, rc, rs1, rs2, dproj):
    s = proj.shape[0]
    tm = 256
    scale = 1.0 / math.sqrt(QK)

    def body(cq_ref, ckv_ref, dqr_ref, dkr_ref, dv_ref, gq_ref, gkv_ref, wuqt_ref, wkvt_ref, c_ref, s1_ref, s2_ref,
             dproj_in, dqf_ref, dkvf_ref, dc_ref, dgq_ref, dgkv_ref):
        del dproj_in

        @pl.when(pl.program_id(0) == 0)
        def _():
            dgq_ref[...] = jnp.zeros_like(dgq_ref)
            dgkv_ref[...] = jnp.zeros_like(dgkv_ref)

        c, s1, s2 = c_ref[...], s1_ref[...], s2_ref[...]
        lane = lax.broadcasted_iota(jnp.int32, (tm, LANE), 1)
        ksum = jnp.zeros((tm, LANE), F32)
        for h in range(HEADS):
            sl = slice(LANE * h, LANE * (h + 1))
            dqf_ref[:, sl] = (_unrope(dqr_ref[:, sl], c, s1, s2) * scale).astype(BF16)
            dkh = dkr_ref[:, sl]
            ksum = ksum + dkh
            dkvf_ref[:, sl] = jnp.where(lane < NOPE, dkh, 0.0).astype(BF16)
            dkvf_ref[:, HEADS * LANE + LANE * h:HEADS * LANE + LANE * (h + 1)] = jnp.where(
                lane < VDIM, dv_ref[:, sl], 0.0).astype(BF16)
        dkpe = _unrope(ksum, c, s1, s2)
        dc_ref[:, Q_LORA + KV_LORA:] = jnp.where((lane >= NOPE) & (lane < QK), dkpe, 0.0).astype(BF16)
        dcqn = _dot(dqf_ref[...], wuqt_ref[...])
        dckvn = _dot(dkvf_ref[...], wkvt_ref[...])
        for x_ref, g_ref, dn, cols, dg_ref in ((cq_ref, gq_ref, dcqn, slice(0, Q_LORA), dgq_ref),
                                               (ckv_ref, gkv_ref, dckvn, slice(Q_LORA, Q_LORA + KV_LORA), dgkv_ref)):
            xv = x_ref[...]
            r = lax.rsqrt(jnp.mean(xv * xv, axis=-1, keepdims=True) + EPS)
            xh = xv * r
            dg_ref[...] += jnp.sum(dn * xh, axis=0, keepdims=True)
            dh = dn * g_ref[...]
            dc_ref[:, cols] = (r * (dh - xh * jnp.mean(dh * xh, axis=-1, keepdims=True))).astype(BF16)

    row = lambda w, j: pl.BlockSpec((tm, w), lambda i: (i, j))
    full = lambda a: pl.BlockSpec(a.shape, lambda i: (0, 0))
    acc = lambda w: pl.BlockSpec((1, w), lambda i: (0, 0))
    sds = lambda w, dt: jax.ShapeDtypeStruct((s, w), dt)
    return pl.pallas_call(
        body,
        grid=(s // tm,),
        in_specs=[row(768, 6), row(256, 21), row(1024, 0), row(1024, 0), row(1024, 0), full(g_q), full(g_kv),
                  full(w_uq_pt), full(w_kv_pt), row(128, 0), row(128, 0), row(128, 0),
                  pl.BlockSpec(memory_space=pl.ANY)],
        out_specs=[row(1024, 0), row(2048, 0), row(1152, 4), acc(768), acc(256)],
        out_shape=[sds(1024, BF16), sds(2048, BF16), jax.ShapeDtypeStruct(dproj.shape, BF16),
                   jax.ShapeDtypeStruct((1, 768), F32), jax.ShapeDtypeStruct((1, 256), F32)],
        input_output_aliases={12: 2},
        compiler_params=_params(("arbitrary",)),
        name="mla_bwd",
    )(proj, proj, dqr, dkr, dv, g_q, g_kv, w_uq_pt, w_kv_pt, rc, rs1, rs2, dproj)


def _pick(n, options):
    for o in options:
        if n % o == 0:
            return o
    raise ValueError(n)


def _matmul(a, b, name):
    m, k = a.shape
    n = b.shape[1]
    tm = _pick(m, (1024, 768, 512, 256))
    tn = _pick(n, (1152, 1024, 768, 512))
    tk = _pick(k, (1024, 512))
    nk = k // tk

    def body(a_ref, b_ref, o_ref):
        @pl.when(pl.program_id(2) == 0)
        def _():
            o_ref[...] = jnp.zeros_like(o_ref)

        o_ref[...] += _dot(a_ref[...], b_ref[...])

    return pl.pallas_call(
        body,
        grid=(m // tm, n // tn, nk),
        in_specs=[pl.BlockSpec((tm, tk), lambda i, j, l: (i, l)), pl.BlockSpec((tk, tn), lambda i, j, l: (l, j))],
        out_specs=pl.BlockSpec((tm, tn), lambda i, j, l: (i, j)),
        out_shape=jax.ShapeDtypeStruct((m, n), F32),
        compiler_params=_params(("arbitrary", "arbitrary", "arbitrary")),
        name=name,
    )(a, b)


def _dh_dx(dproj, w_in_pt, x, dout, g_pre):
    s, k = dproj.shape
    tm = 256

    def body(dp_ref, w_ref, x_ref, dout_ref, g_ref, dx_ref, dg_ref):
        @pl.when(pl.program_id(0) == 0)
        def _():
            dg_ref[...] = jnp.zeros_like(dg_ref)

        dh = _dot(dp_ref[...], w_ref[...])
        xv = x_ref[...]
        r = lax.rsqrt(jnp.mean(xv * xv, axis=-1, keepdims=True) + EPS)
        xh = xv * r
        dg_ref[...] += jnp.sum(dh * xh, axis=0, keepdims=True)
        dxh = dh * g_ref[...]
        dx_ref[...] = dout_ref[...] + r * (dxh - xh * jnp.mean(dxh * xh, axis=-1, keepdims=True))

    row = lambda w: pl.BlockSpec((tm, w), lambda i: (i, 0))
    return pl.pallas_call(
        body,
        grid=(s // tm,),
        in_specs=[row(k), pl.BlockSpec((k, D_MODEL), lambda i: (0, 0)), row(D_MODEL), row(D_MODEL),
                  pl.BlockSpec((1, D_MODEL), lambda i: (0, 0))],
        out_specs=[row(D_MODEL), pl.BlockSpec((1, D_MODEL), lambda i: (0, 0))],
        out_shape=[jax.ShapeDtypeStruct((s, D_MODEL), F32), jax.ShapeDtypeStruct((1, D_MODEL), F32)],
        compiler_params=_params(("arbitrary",)),
        name="dh_dx",
    )(dproj, w_in_pt, x, dout, g_pre)


def _rope_tables(s):
    inv = ROPE_THETA ** (-jnp.arange(0, ROPE, 2, dtype=F32) / ROPE)
    ang = jnp.arange(s, dtype=F32)[:, None] * inv[None, :]
    cos, sin = jnp.cos(ang), jnp.sin(ang)
    z = lambda w: jnp.zeros((s, w), F32)
    rc = jnp.concatenate([jnp.ones((s, NOPE), F32), cos, cos, z(32)], axis=1)
    rs1 = jnp.concatenate([z(NOPE), -sin, z(16), z(32)], axis=1)
    rs2 = jnp.concatenate([z(NOPE), z(16), sin, z(32)], axis=1)
    return rc, rs1, rs2


def _local_step(x, tgt, w_in_slots, w_uq, w_ukv, w_a, w_b, w_out, g_pre, b_gate, g_q, g_kv, lbl, g_hgrn, g_post):
    s = x.shape[0]
    w_in_p, w_in_pt = _assemble_w_in(w_in_slots)
    w_uq_p = jnp.pad(w_uq.reshape(Q_LORA, HEADS, QK), ((0, 0), (0, 0), (0, LANE - QK))).reshape(Q_LORA, HEADS * LANE)
    kv3 = w_ukv.reshape(KV_LORA, HEADS, NOPE + VDIM)
    pad64 = lambda t: jnp.pad(t, ((0, 0), (0, 0), (0, LANE - 64))).reshape(KV_LORA, HEADS * LANE)
    w_kv_p = jnp.concatenate([pad64(kv3[:, :, :NOPE]), pad64(kv3[:, :, NOPE:])], axis=1)
    rc, rs1, rs2 = _rope_tables(s)
    gh = jnp.tile(g_hgrn, (1, HEADS))

    proj, ht = _norm_proj(x, g_pre, w_in_p)
    qr, kr, v, cqt, ckvt = _mla_prep(proj, g_q, g_kv, w_uq_p, w_kv_p, rc, rs1, rs2)
    attn, qa = _attn_fwd(qr, kr, v)
    o, sprev = _hgrn_fwd(proj, lbl)
    (dout, dproj, dop, do, mt, dy_bf, yat, dya_bf, ybt, dyb_bf,
     loss_vec, dg_post, db_gate, dgh) = _tail(x, tgt, proj, attn, o, w_a, w_b, w_out, w_a.T, w_b.T, w_out.T,
                                               b_gate, g_post, gh)
    dqr, dkr, dv = _attn_bwd(qa, kr, v, dop)
    dproj, dlbl = _hgrn_bwd(proj, lbl, do, sprev, dproj)
    dqf, dkvf, dproj, dg_q, dg_kv = _mla_bwd(proj, dqr, dkr, dv, g_q, g_kv, w_uq_p.T, w_kv_p.T, rc, rs1, rs2, dproj)
    dx, dg_pre = _dh_dx(dproj, w_in_pt, x, dout, g_pre)

    dw_in_slots = _scatter_w_in(_matmul(ht, dproj, "dw_in"))
    dw_out = _matmul(mt, dy_bf, "dw_out")
    dw_a = _matmul(yat, dya_bf, "dw_a")
    dw_b = _matmul(ybt, dyb_bf, "dw_b")
    dw_uq_p = _matmul(cqt, dqf, "dw_uq")
    dw_kv_p = _matmul(ckvt, dkvf, "dw_kv")

    dw_uq = dw_uq_p.reshape(Q_LORA, HEADS, LANE)[:, :, :QK].reshape(Q_LORA, HEADS * QK)
    dw_ukv = jnp.concatenate([dw_kv_p[:, :HEADS * LANE].reshape(KV_LORA, HEADS, LANE)[:, :, :NOPE],
                              dw_kv_p[:, HEADS * LANE:].reshape(KV_LORA, HEADS, LANE)[:, :, :VDIM]],
                             axis=2).reshape(KV_LORA, 1024)
    loss = 0.5 / D_MODEL * jnp.sum(loss_vec)
    grads = dict(g_pre=dg_pre, b_gate=db_gate, g_q=dg_q, w_uq=dw_uq, g_kv=dg_kv, w_ukv=dw_ukv,
                 lb_logits=dlbl, g_hgrn=jnp.sum(dgh.reshape(HEADS, VDIM), axis=0, keepdims=True),
                 w_branch_a=dw_a, w_branch_b=dw_b, w_out=dw_out, g_post=dg_post)
    return loss, dx, dw_in_slots, grads


def _my_place():
    return lax.axis_index("x"), lax.axis_index("y"), lax.axis_index("c")


def _all_gather(blocks):
    n = len(blocks)

    def body(*refs):
        x_refs, out_refs = refs[:n], refs[n:2 * n]
        send_sems, recv_sems, local_sems = refs[2 * n:]
        x, y, c = _my_place()
        me, sibling = (x, y, c), (x, y, 1 - c)
        chips = [(1 - x, y), (x, 1 - y), (1 - x, 1 - y)]

        def slot(a, px, py, pc):
            return out_refs[a].at[4 * px + 2 * py + pc]

        def copy(a, k, blk, to, src=None):
            return pltpu.make_async_remote_copy(
                src_ref=slot(a, *blk) if src is None else src, dst_ref=slot(a, *blk),
                send_sem=send_sems.at[7 * a + k], recv_sem=recv_sems.at[7 * a + k],
                device_id=to, device_id_type=MESH_ID)

        mine = [pltpu.make_async_copy(x_refs[a], slot(a, *me), local_sems.at[a]) for a in range(n)]
        for cp in mine:
            cp.start()
        first = [copy(a, 0, me, sibling, src=x_refs[a]) for a in range(n)]
        first += [copy(a, 1 + j, me, (*chip, c), src=x_refs[a]) for a in range(n) for j, chip in enumerate(chips)]
        for cp in first:
            cp.start()
        passed = []
        for j, chip in enumerate(chips):
            for a in range(n):
                copy(a, 1 + j, (*chip, c), me).wait_recv()
                passed.append(copy(a, 4 + j, (*chip, c), sibling))
                passed[-1].start()
        for a in range(n):
            copy(a, 0, sibling, me).wait_recv()
        for j, chip in enumerate(chips):
            for a in range(n):
                copy(a, 4 + j, (*chip, 1 - c), me).wait_recv()
        for cp in first + passed:
            cp.wait_send()
        for cp in mine:
            cp.wait()

    return pl.pallas_call(
        body,
        out_shape=[jax.ShapeDtypeStruct((N_DEV,) + b.shape, b.dtype) for b in blocks],
        in_specs=[pl.BlockSpec(memory_space=pl.ANY)] * n,
        out_specs=[pl.BlockSpec(memory_space=pl.ANY)] * n,
        scratch_shapes=[pltpu.SemaphoreType.DMA((7 * n,)), pltpu.SemaphoreType.DMA((7 * n,)),
                        pltpu.SemaphoreType.DMA((n,))],
        name="gather_weights",
    )(*blocks)


def _exchange(sends):
    n = len(sends)

    def body(*refs):
        s_refs, r_refs = refs[:n], refs[n:2 * n]
        send_sems, recv_sems, local_sems = refs[2 * n:]
        x, y, c = _my_place()
        me = 4 * x + 2 * y + c
        mine = [pltpu.make_async_copy(s_refs[a].at[me], r_refs[a].at[me], local_sems.at[a]) for a in range(n)]
        for cp in mine:
            cp.start()
        copies = []
        for k in range(N_DEV - 1):
            fx, fy, fc = (k + 1) >> 2 & 1, (k + 1) >> 1 & 1, (k + 1) & 1
            px = 1 - x if fx else x
            py = 1 - y if fy else y
            pc = 1 - c if fc else c
            for a in range(n):
                copies.append(pltpu.make_async_remote_copy(
                    src_ref=s_refs[a].at[4 * px + 2 * py + pc], dst_ref=r_refs[a].at[me],
                    send_sem=send_sems.at[7 * a + k], recv_sem=recv_sems.at[7 * a + k],
                    device_id=(px, py, pc), device_id_type=MESH_ID))
        for cp in copies:
            cp.start()
        for cp in copies:
            cp.wait_recv()
        for cp in copies:
            cp.wait_send()
        for cp in mine:
            cp.wait()

    return pl.pallas_call(
        body,
        out_shape=[jax.ShapeDtypeStruct(t.shape, t.dtype) for t in sends],
        in_specs=[pl.BlockSpec(memory_space=pl.ANY)] * n,
        out_specs=[pl.BlockSpec(memory_space=pl.ANY)] * n,
        scratch_shapes=[pltpu.SemaphoreType.DMA((7 * n,)), pltpu.SemaphoreType.DMA((7 * n,)),
                        pltpu.SemaphoreType.DMA((n,))],
        name="exchange_grads",
    )(*sends)


def _adamw(g, w, m, v):
    c1 = 1.0 / (1.0 - ADAM_B1 ** ADAM_STEP)
    c2 = 1.0 / (1.0 - ADAM_B2 ** ADAM_STEP)
    nm = ADAM_B1 * m + (1.0 - ADAM_B1) * g
    nv = ADAM_B2 * v + (1.0 - ADAM_B2) * (g * g)
    d = -ADAM_LR * ((nm * c1) / (jnp.sqrt(nv * c2) + ADAM_EPS) + ADAM_WD * w)
    return d, nm, nv


def _sum8(r_ref):
    g = r_ref[0].astype(F32)
    for k in range(1, N_DEV):
        g = g + r_ref[k].astype(F32)
    return g


def _sum_adamw_rows(recv, w, m, v, name):
    rows, cols = w.shape
    tr = 256

    def body(r_ref, w_ref, m_ref, v_ref, g_ref, d_ref, nm_ref, nv_ref):
        g = _sum8(r_ref)
        g_ref[...] = g
        d_ref[...], nm_ref[...], nv_ref[...] = _adamw(g, w_ref[...], m_ref[...], v_ref[...])

    blk = pl.BlockSpec((tr, cols), lambda i: (i, 0))
    out = jax.ShapeDtypeStruct((rows, cols), F32)
    return pl.pallas_call(
        body,
        grid=(rows // tr,),
        in_specs=[pl.BlockSpec((N_DEV, tr, cols), lambda i: (0, i, 0)), blk, blk, blk],
        out_specs=[blk, blk, blk, blk],
        out_shape=[out, out, out, out],
        compiler_params=_params(("arbitrary",)),
        name=name,
    )(recv, w, m, v)


def _sum_adamw_whole(recvs, ws, ms, vs):
    n = len(ws)

    def body(*refs):
        r_refs, w_refs, m_refs, v_refs = refs[:n], refs[n:2 * n], refs[2 * n:3 * n], refs[3 * n:4 * n]
        outs = refs[4 * n:]
        for a in range(n):
            g = _sum8(r_refs[a])
            d, nm, nv = _adamw(g, w_refs[a][...], m_refs[a][...], v_refs[a][...])
            outs[a][...] = g
            outs[n + a][...] = d
            outs[2 * n + a][...] = nm
            outs[3 * n + a][...] = nv

    shapes = [jax.ShapeDtypeStruct(w.shape, F32) for w in ws]
    res = pl.pallas_call(
        body,
        out_shape=shapes * 4,
        compiler_params=pltpu.CompilerParams(vmem_limit_bytes=48 * 2**20),
        name="sum_adamw_small",
    )(*recvs, *ws, *ms, *vs)
    return res[:n], res[n:2 * n], res[2 * n:3 * n], res[3 * n:]


MATS = ("w_uq", "w_ukv", "w_branch_a", "w_branch_b", "w_out")
SMALL = ("g_pre", "b_gate", "g_q", "g_kv", "lb_logits", "g_hgrn", "g_post")
SMALL_ROWS = (1, 2, 1, 1, 1, 1, 1)
LOSS_AT = (6, 1023)
SMALL_SHAPE = dict(g_pre=(1, 1024), b_gate=(1, 2048), g_q=(1, 768), g_kv=(1, 256), lb_logits=(2, 512),
                   g_hgrn=(1, 64), g_post=(1, 1024))
COL_SHARDED = dict(w_uq=False, w_ukv=True, w_branch_a=True, w_branch_b=True, w_out=False)
ORDER = ("g_pre", "w_in", "b_gate", "g_q", "w_uq", "g_kv", "w_ukv", "lb_logits", "g_hgrn",
         "w_branch_a", "w_branch_b", "w_out", "g_post")


def _pack_small(t):
    parts = []
    for n, r in zip(SMALL, SMALL_ROWS):
        flat = t[n].reshape(1, -1)
        parts.append(jnp.pad(flat, ((0, 0), (0, r * 1024 - flat.shape[1]))).reshape(r, 1024))
    return jnp.concatenate(parts, axis=0)


def _unpack_small(p):
    out, r0 = {}, 0
    for n, r in zip(SMALL, SMALL_ROWS):
        shp = SMALL_SHAPE[n]
        out[n] = p[r0:r0 + r].reshape(1, -1)[:, :shp[0] * shp[1]].reshape(shp)
        r0 += r
    return out


def _to_slots(name, full):
    r, c = full.shape
    if COL_SHARDED[name]:
        return full.reshape(r, N_DEV, c // N_DEV).transpose(1, 0, 2)
    return full.reshape(N_DEV, r // N_DEV, c)


def _from_slots(name, slots):
    _, r, c = slots.shape
    if COL_SHARDED[name]:
        return slots.transpose(1, 0, 2).reshape(r, N_DEV * c)
    return slots.reshape(N_DEV * r, c)


def kernel(x, g_pre, w_in, b_gate, g_q, w_uq, g_kv, w_ukv, lb_logits, g_hgrn, w_branch_a, w_branch_b, w_out, g_post, loss_target, m_g_pre, m_w_in, m_b_gate, m_g_q, m_w_uq, m_g_kv, m_w_ukv, m_lb_logits, m_g_hgrn, m_w_branch_a, m_w_branch_b, m_w_out, m_g_post, v_g_pre, v_w_in, v_b_gate, v_g_q, v_w_uq, v_g_kv, v_w_ukv, v_lb_logits, v_g_hgrn, v_w_branch_a, v_w_branch_b, v_w_out, v_g_post):
    w = dict(w_in=w_in[0], w_uq=w_uq[0], w_ukv=w_ukv[0], w_branch_a=w_branch_a[0], w_branch_b=w_branch_b[0],
             w_out=w_out[0], g_pre=g_pre, b_gate=b_gate, g_q=g_q, g_kv=g_kv, lb_logits=lb_logits, g_hgrn=g_hgrn,
             g_post=g_post)
    mom = dict(w_in=m_w_in[0], w_uq=m_w_uq[0], w_ukv=m_w_ukv[0], w_branch_a=m_w_branch_a[0],
               w_branch_b=m_w_branch_b[0], w_out=m_w_out[0], g_pre=m_g_pre, b_gate=m_b_gate, g_q=m_g_q, g_kv=m_g_kv,
               lb_logits=m_lb_logits, g_hgrn=m_g_hgrn, g_post=m_g_post)
    var = dict(w_in=v_w_in[0], w_uq=v_w_uq[0], w_ukv=v_w_ukv[0], w_branch_a=v_w_branch_a[0],
               w_branch_b=v_w_branch_b[0], w_out=v_w_out[0], g_pre=v_g_pre, b_gate=v_b_gate, g_q=v_g_q, g_kv=v_g_kv,
               lb_logits=v_lb_logits, g_hgrn=v_g_hgrn, g_post=v_g_post)

    gathered = _all_gather([w[n].astype(BF16) for n in ("w_in",) + MATS])
    full = {n: _from_slots(n, g) for n, g in zip(MATS, gathered[1:])}

    loss, dx, dw_in_slots, grads = _local_step(
        x[0], loss_target[0], gathered[0], full["w_uq"], full["w_ukv"], full["w_branch_a"], full["w_branch_b"],
        full["w_out"], g_pre, b_gate, g_q, g_kv, lb_logits, g_hgrn, g_post)

    small = _pack_small(grads).at[LOSS_AT].set(loss)
    small = jnp.broadcast_to(small[None], (N_DEV, 8, 1024))
    sends = [dw_in_slots] + [_to_slots(n, grads[n]).astype(BF16) for n in MATS] + [small]
    recvs = _exchange(sends)

    g_in, d_in, m_in, v_in = _sum_adamw_rows(recvs[0], w["w_in"], mom["w_in"], var["w_in"], "sum_adamw_w_in")
    names = MATS + ("small",)
    pk = lambda t: [t[n] for n in MATS] + [_pack_small(t)]
    res = _sum_adamw_whole(recvs[1:], pk(w), pk(mom), pk(var))

    outs = []
    for kind, big in zip(res, (g_in, d_in, m_in, v_in)):
        t = dict(zip(names, kind))
        t = {**{n: t[n][None] for n in MATS}, **_unpack_small(t["small"]), "w_in": big[None]}
        outs += [t[n] for n in ORDER]
    total = res[0][-1][LOSS_AT]
    return (total, dx[None], *outs)
```

```python
import math

import jax
import jax.numpy as jnp
import numpy as np
from jax import lax
from jax.experimental import pallas as pl
from jax.experimental.pallas import tpu as pltpu

F32, BF16 = jnp.float32, jnp.bfloat16

D_MODEL = 1024
EPS = 1e-6
HEADS = 8
NOPE, ROPE, VDIM = 64, 32, 64
QK = NOPE + ROPE
Q_LORA, KV_LORA = 768, 256
ROPE_THETA = 10000.0
ATT_CHUNK_SHIFT = 6
HG_BLOCK = 32
HG_WIDTH = 512
D_IN = 5664
D_IN_PAD = 5760
W_IN_SHARD = D_IN // 8
N_DEV = 8
LANE = 128

ADAM_LR, ADAM_B1, ADAM_B2, ADAM_EPS, ADAM_WD, ADAM_STEP = 0.001, 0.9, 0.999, 1e-08, 0.01, 10

W_IN_SEGMENTS = ((3616, 5664, 0), (1056, 1568, 2048), (3104, 3616, 2560), (1568, 3104, 3072),
                 (0, 1024, 4608), (1024, 1056, 5696))
W_IN_ZERO = ((5632, 5696), (5728, 5760))

NT = (((1,), (1,)), ((), ()))
TN = (((0,), (0,)), ((), ()))
MESH_ID = pl.DeviceIdType.MESH


def _w_in_pieces():
    out = []
    for lo, hi, dst in W_IN_SEGMENTS:
        c = lo
        while c < hi:
            p = c // W_IN_SHARD
            e = min(hi, (p + 1) * W_IN_SHARD)
            out.append((p, c - p * W_IN_SHARD, e - p * W_IN_SHARD, dst + c - lo))
            c = e
    return out


def _params(sem, vmem_mb=48):
    return pltpu.CompilerParams(dimension_semantics=sem, vmem_limit_bytes=vmem_mb * 2**20)


def _dot(a, b):
    return jnp.dot(a, b, preferred_element_type=F32)


def _dotg(a, b, dims):
    return lax.dot_general(a, b, dims, preferred_element_type=F32)


def _split2(x):
    hi = x.astype(BF16)
    return hi, (x - hi.astype(F32)).astype(BF16)


def _sel_left(m01, x):
    hi, lo = _split2(x)
    return _dot(m01, hi) + _dot(m01, lo)


def _sel_right(x, m01):
    hi, lo = _split2(x)
    return _dot(hi, m01) + _dot(lo, m01)


def _hi_lo(x):
    hi = x.astype(BF16).astype(F32)
    return hi, x - hi


def _sigmoid(x):
    return 0.5 * jnp.tanh(0.5 * x) + 0.5


def _rope(x, c, s1, s2):
    return x * c + pltpu.roll(x, 112, 1) * s1 + pltpu.roll(x, 16, 1) * s2


def _unrope(d, c, s1, s2):
    return d * c + pltpu.roll(d * s1, 16, 1) + pltpu.roll(d * s2, 112, 1)


def _assemble_w_in(slots):
    tc = 256
    pieces = _w_in_pieces()

    def body(s_ref, w_ref, wt_ref):
        for lo, hi in W_IN_ZERO:
            wt_ref[lo:hi, :] = jnp.zeros((hi - lo, tc), BF16)
        for p, lo, hi, dst in pieces:
            wt_ref[dst:dst + hi - lo, :] = s_ref[p, lo:hi, :]
        w_ref[...] = wt_ref[...].T

    return pl.pallas_call(
        body,
        grid=(D_MODEL // tc,),
        in_specs=[pl.BlockSpec((N_DEV, W_IN_SHARD, tc), lambda i: (0, 0, i))],
        out_specs=[pl.BlockSpec((tc, D_IN_PAD), lambda i: (i, 0)), pl.BlockSpec((D_IN_PAD, tc), lambda i: (0, i))],
        out_shape=[jax.ShapeDtypeStruct((D_MODEL, D_IN_PAD), BF16), jax.ShapeDtypeStruct((D_IN_PAD, D_MODEL), BF16)],
        compiler_params=_params(("arbitrary",)),
        name="assemble_w_in",
    )(slots)


def _scatter_w_in(dw):
    tc = 256
    pieces = _w_in_pieces()

    def body(d_ref, s_ref):
        dt = d_ref[...].T
        for p, lo, hi, dst in pieces:
            s_ref[p, lo:hi, :] = dt[dst:dst + hi - lo, :].astype(BF16)

    return pl.pallas_call(
        body,
        grid=(D_MODEL // tc,),
        in_specs=[pl.BlockSpec((tc, D_IN_PAD), lambda i: (i, 0))],
        out_specs=pl.BlockSpec((N_DEV, W_IN_SHARD, tc), lambda i: (0, 0, i)),
        out_shape=jax.ShapeDtypeStruct((N_DEV, W_IN_SHARD, D_MODEL), BF16),
        compiler_params=_params(("arbitrary",)),
        name="scatter_w_in",
    )(dw)


def _norm_proj(x, g_pre, w):
    s, n = x.shape[0], w.shape[1]
    tm, tn = 512, 1152
    ni = s // tm

    def body(x_ref, g_ref, w_ref, proj_ref, ht_ref, h_ref):
        rows = pl.ds(pl.multiple_of(pl.program_id(1) * tm, tm), tm)

        @pl.when(pl.program_id(0) == 0)
        def _():
            xv = x_ref[...]
            r = lax.rsqrt(jnp.mean(xv * xv, axis=-1, keepdims=True) + EPS)
            h = (xv * r * g_ref[...]).astype(BF16)
            h_ref[rows, :] = h
            ht_ref[...] = h.T

        proj_ref[...] = _dot(h_ref[rows, :], w_ref[...])

    first = lambda j, i: jnp.where(j == 0, i, ni - 1)
    return pl.pallas_call(
        body,
        grid=(n // tn, ni),
        in_specs=[
            pl.BlockSpec((tm, D_MODEL), lambda j, i: (first(j, i), 0)),
            pl.BlockSpec((1, D_MODEL), lambda j, i: (0, 0)),
            pl.BlockSpec((D_MODEL, tn), lambda j, i: (0, j)),
        ],
        out_specs=[
            pl.BlockSpec((tm, tn), lambda j, i: (i, j)),
            pl.BlockSpec((D_MODEL, tm), lambda j, i: (0, first(j, i))),
        ],
        out_shape=[jax.ShapeDtypeStruct((s, n), F32), jax.ShapeDtypeStruct((D_MODEL, s), BF16)],
        scratch_shapes=[pltpu.VMEM((s, D_MODEL), BF16)],
        compiler_params=_params(("arbitrary", "arbitrary")),
        name="norm_proj",
    )(x, g_pre, w)


def _mla_prep(proj, g_q, g_kv, w_uq_p, w_kv_p, rc, rs1, rs2):
    s = proj.shape[0]
    tm = 256
    scale = 1.0 / math.sqrt(QK)

    def body(cq_ref, ckv_ref, kpe_ref, gq_ref, gkv_ref, wuq_ref, wkv_ref, c_ref, s1_ref, s2_ref,
             qr_ref, kr_ref, v_ref, cqt_ref, ckvt_ref):
        cq = cq_ref[...]
        r = lax.rsqrt(jnp.mean(cq * cq, axis=-1, keepdims=True) + EPS)
        cqn = (cq * r * gq_ref[...]).astype(BF16)
        cqt_ref[...] = cqn.T
        q = _dot(cqn, wuq_ref[...])
        ckv = ckv_ref[...]
        r = lax.rsqrt(jnp.mean(ckv * ckv, axis=-1, keepdims=True) + EPS)
        ckvn = (ckv * r * gkv_ref[...]).astype(BF16)
        ckvt_ref[...] = ckvn.T
        kv = _dot(ckvn, wkv_ref[...])
        c, s1, s2 = c_ref[...], s1_ref[...], s2_ref[...]
        lane = lax.broadcasted_iota(jnp.int32, (tm, LANE), 1)
        kpe = _rope(kpe_ref[...], c, s1, s2) + jnp.where((lane == QK) | (lane == QK + 1), 1.0, 0.0)
        vone = jnp.where((lane == VDIM) | (lane == VDIM + 1), 1.0, 0.0)
        for h in range(HEADS):
            sl = slice(LANE * h, LANE * (h + 1))
            qr_ref[:, sl] = (_rope(q[:, sl], c, s1, s2) * scale).astype(BF16)
            kr_ref[:, sl] = (kv[:, sl] + kpe).astype(BF16)
            v_ref[:, sl] = (kv[:, HEADS * LANE + LANE * h:HEADS * LANE + LANE * (h + 1)] + vone).astype(BF16)

    row = lambda w, j: pl.BlockSpec((tm, w), lambda i: (i, j))
    col = lambda w: pl.BlockSpec((w, tm), lambda i: (0, i))
    full = lambda a: pl.BlockSpec(a.shape, lambda i: (0, 0))
    return pl.pallas_call(
        body,
        grid=(s // tm,),
        in_specs=[row(768, 6), row(256, 21), row(128, 44), full(g_q), full(g_kv), full(w_uq_p), full(w_kv_p),
                  row(128, 0), row(128, 0), row(128, 0)],
        out_specs=[row(1024, 0), row(1024, 0), row(1024, 0), col(768), col(256)],
        out_shape=[jax.ShapeDtypeStruct((s, 1024), BF16), jax.ShapeDtypeStruct((s, 1024), BF16),
                   jax.ShapeDtypeStruct((s, 1024), BF16), jax.ShapeDtypeStruct((768, s), BF16),
                   jax.ShapeDtypeStruct((256, s), BF16)],
        compiler_params=_params(("arbitrary",)),
        name="mla_prep",
    )(proj, proj, proj, g_q, g_kv, w_uq_p, w_kv_p, rc, rs1, rs2)


ATT_T = 512


def _chunk_mask(transposed):
    r = lax.broadcasted_iota(jnp.int32, (ATT_T, ATT_T), 0) >> ATT_CHUNK_SHIFT
    c = lax.broadcasted_iota(jnp.int32, (ATT_T, ATT_T), 1) >> ATT_CHUNK_SHIFT
    return (r <= c) if transposed else (c <= r)


def _attn_fwd(qr, kr, vp):
    s = qr.shape[0]
    t = ATT_T

    def body(q_ref, k_ref, v_ref, o_ref, qa_ref):
        qi = pl.program_id(1)
        lane = lax.broadcasted_iota(jnp.int32, (t, LANE), 1)
        sls = [slice(LANE * a, LANE * (a + 1)) for a in range(2)]
        qs = [q_ref[:, sl] for sl in sls]

        def step(j, carry, masked):
            rows = pl.ds(pl.multiple_of(j * t, t), t)
            out = []
            for a in range(2):
                m, acc = carry[a]
                sc = _dotg(qs[a], k_ref[rows, sls[a]], NT)
                if masked:
                    sc = jnp.where(_chunk_mask(False), sc, -1e30)
                m_new = jnp.maximum(m, jnp.max(sc, axis=-1, keepdims=True))
                p = jnp.exp(sc - m_new).astype(BF16)
                acc = jnp.exp(m - m_new) * acc + _dot(p, v_ref[rows, sls[a]])
                out.append((m_new, acc))
            return tuple(out)

        init = tuple((jnp.full((t, 1), -1e30, F32), jnp.zeros((t, LANE), F32)) for _ in range(2))
        carry = lax.fori_loop(0, qi, lambda j, c: step(j, c, False), init)
        carry = step(qi, carry, True)
        outs = []
        for a in range(2):
            m, acc = carry[a]
            l = acc[:, VDIM:VDIM + 1]
            outs.append(acc / l)
            hi, lo_part = _hi_lo(-(m + jnp.log(l)))
            qa = jnp.where(lane == QK, hi, jnp.where(lane == QK + 1, lo_part, qs[a].astype(F32)))
            qa_ref[:, sls[a]] = qa.astype(BF16)
        o_ref[...] = jnp.where(lane < VDIM, outs[0], pltpu.roll(outs[1], VDIM, 1))

    return pl.pallas_call(
        body,
        grid=(HEADS // 2, s // t),
        in_specs=[
            pl.BlockSpec((t, 2 * LANE), lambda h, i: (i, h)),
            pl.BlockSpec((s, 2 * LANE), lambda h, i: (0, h)),
            pl.BlockSpec((s, 2 * LANE), lambda h, i: (0, h)),
        ],
        out_specs=[
            pl.BlockSpec((t, LANE), lambda h, i: (i, h)),
            pl.BlockSpec((t, 2 * LANE), lambda h, i: (i, h)),
        ],
        out_shape=[jax.ShapeDtypeStruct((s, 512), F32), jax.ShapeDtypeStruct((s, 1024), BF16)],
        compiler_params=_params(("arbitrary", "arbitrary")),
        name="attn_fwd",
    )(qr, kr, vp)


def _attn_bwd(qa, kr, vp, dop):
    s = qa.shape[0]
    t = ATT_T
    nq = s // t

    def body(q_ref, k_ref, v_ref, do_ref, dq_ref, dk_ref, dv_ref):
        j = pl.program_id(1)
        sls = [slice(LANE * a, LANE * (a + 1)) for a in range(2)]

        @pl.when(j == 0)
        def _():
            dq_ref[...] = jnp.zeros_like(dq_ref)

        dk_ref[...] = jnp.zeros_like(dk_ref)
        dv_ref[...] = jnp.zeros_like(dv_ref)
        ks = [k_ref[:, sl] for sl in sls]
        vs = [v_ref[:, sl] for sl in sls]

        def step(i, masked):
            rows = pl.ds(pl.multiple_of(i * t, t), t)
            for a in range(2):
                q = q_ref[rows, sls[a]]
                do = do_ref[rows, sls[a]]
                sc = _dotg(ks[a], q, NT)
                if masked:
                    sc = jnp.where(_chunk_mask(True), sc, -1e30)
                p = jnp.exp(sc)
                ds = (p * _dotg(vs[a], do, NT)).astype(BF16)
                dv_ref[:, sls[a]] += _dot(p.astype(BF16), do)
                dk_ref[:, sls[a]] += _dot(ds, q)
                dq_ref[rows, sls[a]] += _dotg(ds, ks[a], TN)

        step(j, True)

        def loop(i, c):
            step(i, False)
            return c

        lax.fori_loop(j + 1, nq, loop, 0)

    blk = pl.BlockSpec((t, 2 * LANE), lambda h, j: (j, h))
    whole = pl.BlockSpec((s, 2 * LANE), lambda h, j: (0, h))
    out = jax.ShapeDtypeStruct((s, 1024), F32)
    return pl.pallas_call(
        body,
        grid=(HEADS // 2, nq),
        in_specs=[whole, blk, blk, whole],
        out_specs=[whole, blk, blk],
        out_shape=[out, out, out],
        compiler_params=_params(("arbitrary", "arbitrary")),
        name="attn_bwd",
    )(qa, kr, vp, dop)


HG_T = 256
HG_NC = HG_T // HG_BLOCK


def _hg_consts():
    r = jnp.arange(HG_T)[:, None]
    c = jnp.arange(HG_T)[None, :]
    same = (r // HG_BLOCK) == (c // HG_BLOCK)
    mcum = (same & (c <= r)).astype(BF16)
    mrev = (same & (c >= r)).astype(BF16)
    msum = same.astype(BF16)
    a = jnp.arange(LANE)
    bd = ((a[:, None] < 64) == (a[None, :] < 64)).astype(F32)
    return mcum, mrev, msum, bd


def _hg_pre(hq, hf, lbl, mcum, msum):
    lb = _sigmoid(lbl[0:1, :] - lbl[1:2, :])
    sig = _sigmoid(hf)
    f = lb + (1.0 - lb) * sig
    lf = jnp.log(f)
    b = _sel_left(mcum, lf)
    big_l = _sel_left(msum, lf)
    k = 1.0 - f
    qd = hq * jnp.exp(b)
    ki = k * jnp.exp(-b)
    ke = k * jnp.exp(big_l - b)
    return lb, sig, f, b, big_l, qd, ki, ke


def _stack_pair(xp, lo):
    return jnp.concatenate([jnp.where(lo, xp, 0.0), jnp.where(lo, 0.0, xp)], axis=0)


def _hgrn_fwd(proj, lbl):
    s = proj.shape[0]
    t = HG_T
    mcum, _, msum, bd = _hg_consts()

    def body(hq_ref, hf_ref, hi_ref, lbl_ref, mcum_ref, msum_ref, bd_ref, o_ref, sp_ref, st_ref):
        @pl.when(pl.program_id(0) == 0)
        def _():
            st_ref[...] = jnp.zeros_like(st_ref)

        mc = mcum_ref[...]
        _, _, _, _, big_l, qd, ki, ke = _hg_pre(hq_ref[...], hf_ref[...], lbl_ref[...], mc, msum_ref[...])
        el = jnp.exp(big_l)
        hi = hi_ref[...]
        lo = lax.broadcasted_iota(jnp.int32, (t, LANE), 1) < 64
        mask2 = jnp.concatenate([mc, mc], axis=0) > 0.5
        for p in range(HEADS // 2):
            sl = slice(LANE * p, LANE * (p + 1))
            vp = hi[:, sl].astype(BF16)
            q2 = _stack_pair(qd[:, sl], lo).astype(BF16)
            a2 = jnp.where(mask2, _dotg(q2, ki[:, sl].astype(BF16), NT), 0.0)
            r2 = _dot(a2.astype(BF16), vp)
            o_intra = jnp.where(lo, r2[:t], r2[t:])
            qb = qd[:, sl].astype(BF16)
            kb = ke[:, sl].astype(BF16)
            st = st_ref[p]
            for c in range(HG_NC):
                rows = slice(HG_BLOCK * c, HG_BLOCK * (c + 1))
                sp_ref[c, :, sl] = st
                o_ref[rows, sl] = o_intra[rows] + _dotg(qb[rows], st.astype(BF16), NT)
                u = _dotg(vp[rows], kb[rows], TN) * bd_ref[...]
                st = st * el[HG_BLOCK * c:HG_BLOCK * c + 1, sl] + u
            st_ref[p] = st

    row = lambda j: pl.BlockSpec((t, HG_WIDTH), lambda i: (i, j))
    full = lambda a: pl.BlockSpec(a.shape, lambda i: (0, 0))
    return pl.pallas_call(
        body,
        grid=(s // t,),
        in_specs=[row(6), row(7), row(8), full(lbl), full(mcum), full(msum), full(bd)],
        out_specs=[row(0), pl.BlockSpec((HG_NC, LANE, HG_WIDTH), lambda i: (i, 0, 0))],
        out_shape=[jax.ShapeDtypeStruct((s, HG_WIDTH), F32),
                   jax.ShapeDtypeStruct((s // HG_BLOCK, LANE, HG_WIDTH), F32)],
        scratch_shapes=[pltpu.VMEM((HEADS // 2, LANE, LANE), F32)],
        compiler_params=_params(("arbitrary",)),
        name="hgrn_fwd",
    )(proj, proj, proj, lbl, mcum, msum, bd)


def _hgrn_bwd(proj, lbl, do, sprev, dproj):
    s = proj.shape[0]
    t = HG_T
    nt = s // t
    mcum, mrev, msum, bd = _hg_consts()

    def body(hq_ref, hf_ref, hi_ref, lbl_ref, do_ref, sp_ref, mcum_ref, mrev_ref, msum_ref, bd_ref,
             dproj_in, dh_ref, dlbl_ref, g_ref):
        del dproj_in

        @pl.when(pl.program_id(0) == 0)
        def _():
            g_ref[...] = jnp.zeros_like(g_ref)
            dlbl_ref[...] = jnp.zeros_like(dlbl_ref)

        mc = mcum_ref[...]
        lb, sig, f, b, big_l, qd, ki, ke = _hg_pre(hq_ref[...], hf_ref[...], lbl_ref[...], mc, msum_ref[...])
        el = jnp.exp(big_l)
        hi = hi_ref[...]
        dov = do_ref[...]
        lo = lax.broadcasted_iota(jnp.int32, (t, LANE), 1) < 64
        mask2 = jnp.concatenate([mc, mc], axis=0) > 0.5
        dqd_parts, dke_parts, dv_parts, del_parts, dki_parts = [], [], [], [], []
        for p in range(HEADS // 2):
            sl = slice(LANE * p, LANE * (p + 1))
            vp = hi[:, sl].astype(BF16)
            q2 = _stack_pair(qd[:, sl], lo).astype(BF16)
            kip = ki[:, sl].astype(BF16)
            do2 = _stack_pair(dov[:, sl], lo).astype(BF16)
            a2 = jnp.where(mask2, _dotg(q2, kip, NT), 0.0).astype(BF16)
            da2 = jnp.where(mask2, _dotg(do2, vp, NT), 0.0).astype(BF16)
            r2 = _dot(da2, kip)
            dki_parts.append(_dotg(da2, q2, TN))
            qb = qd[:, sl].astype(BF16)
            kb = ke[:, sl].astype(BF16)
            dob = dov[:, sl].astype(BF16)
            g = g_ref[p]
            dqd_c, dv_c, dke_c, del_c = [], [], [], []
            for c in range(HG_NC - 1, -1, -1):
                rows = slice(HG_BLOCK * c, HG_BLOCK * (c + 1))
                gb = g.astype(BF16)
                st = sp_ref[c, :, sl]
                dqd_c.append(_dot(dob[rows], st.astype(BF16)))
                dv_c.append(_dotg(kb[rows], gb, NT))
                dke_c.append(_dot(vp[rows], gb))
                del_c.append(jnp.broadcast_to(jnp.sum(g * st, axis=0, keepdims=True), (HG_BLOCK, LANE)))
                g = g * el[HG_BLOCK * c:HG_BLOCK * c + 1, sl] + _dotg(dob[rows], qb[rows], TN) * bd_ref[...]
            g_ref[p] = g
            up = lambda parts: jnp.concatenate(parts[::-1], axis=0)
            dqd_parts.append(jnp.where(lo, r2[:t], r2[t:]) + up(dqd_c))
            dv_parts.append(_dotg(a2, do2, TN) + up(dv_c))
            dke_parts.append(up(dke_c))
            del_parts.append(up(del_c))
        wide = lambda parts: jnp.concatenate(parts, axis=1)
        dqd, dke, dki, dvv, del_rows = wide(dqd_parts), wide(dke_parts), wide(dki_parts), wide(dv_parts), wide(del_parts)
        dh_ref[:, :HG_WIDTH] = (dqd * jnp.exp(b)).astype(BF16)
        dh_ref[:, 2 * HG_WIDTH:] = dvv.astype(BF16)
        dke_ke = dke * ke
        db = dqd * qd - dki * ki - dke_ke
        dl_rows = _sel_left(msum_ref[...], dke_ke) + del_rows * el
        is_last = (lax.broadcasted_iota(jnp.int32, (t, HG_WIDTH), 0) & (HG_BLOCK - 1)) == HG_BLOCK - 1
        db = db + jnp.where(is_last, dl_rows, 0.0)
        dlf = _sel_left(mrev_ref[...], db)
        dk = dki * jnp.exp(-b) + dke * jnp.exp(big_l - b)
        df = dlf / f - dk
        dh_ref[:, HG_WIDTH:2 * HG_WIDTH] = (df * (1.0 - lb) * sig * (1.0 - sig)).astype(BF16)
        dlb = jnp.sum(df * (1.0 - sig), axis=0, keepdims=True) * lb * (1.0 - lb)
        dlbl_ref[0:1, :] += dlb
        dlbl_ref[1:2, :] -= dlb

    rrow = lambda j: pl.BlockSpec((t, HG_WIDTH), lambda i: (nt - 1 - i, j))
    full = lambda a: pl.BlockSpec(a.shape, lambda i: (0, 0))
    return pl.pallas_call(
        body,
        grid=(nt,),
        in_specs=[rrow(6), rrow(7), rrow(8), full(lbl), rrow(0),
                  pl.BlockSpec((HG_NC, LANE, HG_WIDTH), lambda i: (nt - 1 - i, 0, 0)),
                  full(mcum), full(mrev), full(msum), full(bd), pl.BlockSpec(memory_space=pl.ANY)],
        out_specs=[pl.BlockSpec((t, 3 * HG_WIDTH), lambda i: (nt - 1 - i, 2)),
                   pl.BlockSpec((2, HG_WIDTH), lambda i: (0, 0))],
        out_shape=[jax.ShapeDtypeStruct(dproj.shape, BF16), jax.ShapeDtypeStruct((2, HG_WIDTH), F32)],
        input_output_aliases={10: 0},
        scratch_shapes=[pltpu.VMEM((HEADS // 2, LANE, LANE), F32)],
        compiler_params=_params(("arbitrary",)),
        name="hgrn_bwd",
    )(proj, proj, proj, lbl, do, sprev, mcum, mrev, msum, bd, dproj)


def _tail(x, tgt, proj, attn, o, w_a, w_b, w_out, w_at, w_bt, w_outt, b_gate, g_post, gh):
    s = x.shape[0]
    tm = 128
    ones64 = (jnp.arange(HG_WIDTH)[:, None] // 64 == jnp.arange(HG_WIDTH)[None, :] // 64).astype(BF16)

    def body(x_ref, t_ref, ml_ref, ga_ref, gb_ref, at_ref, o_ref, wa_ref, wb_ref, wo_ref, wat_ref, wbt_ref, wot_ref,
             bg_ref, gp_ref, gh_ref, ones_ref,
             dout_ref, dpj_ref, dop_ref, do_ref, mt_ref, dy_ref, yat_ref, dya_ref, ybt_ref, dyb_ref,
             loss_ref, dgp_ref, dbg_ref, dgh_ref):
        @pl.when(pl.program_id(0) == 0)
        def _():
            loss_ref[...] = jnp.zeros_like(loss_ref)
            dgp_ref[...] = jnp.zeros_like(dgp_ref)
            dbg_ref[...] = jnp.zeros_like(dbg_ref)
            dgh_ref[...] = jnp.zeros_like(dgh_ref)

        ones = ones_ref[...]
        gate_a = ga_ref[...]
        sa = _sigmoid(gate_a)
        silu_a = gate_a * sa
        attn_v = at_ref[...]
        ya_in = attn_v * silu_a
        ov = o_ref[...]
        ro = lax.rsqrt(_sel_right(ov * ov, ones) * (1.0 / 64.0) + EPS)
        ohat = ov * ro
        ghv = gh_ref[...]
        on = ohat * ghv
        gate_b = gb_ref[...]
        sb = _sigmoid(gate_b)
        silu_b = gate_b * sb
        yb_in = on * silu_b
        ya_bf = ya_in.astype(BF16)
        yb_bf = yb_in.astype(BF16)
        yat_ref[...] = ya_bf.T
        ybt_ref[...] = yb_bf.T
        y_a = _dot(ya_bf, wa_ref[...])
        y_b = _dot(yb_bf, wb_ref[...])
        gts = _sigmoid(ml_ref[...] + bg_ref[...])
        g_a = gts[:, :D_MODEL]
        g_b = gts[:, D_MODEL:]
        m_bf = (g_a * y_a + g_b * y_b).astype(BF16)
        mt_ref[...] = m_bf.T
        y = _dot(m_bf, wo_ref[...])
        r1 = lax.rsqrt(jnp.mean(y * y, axis=-1, keepdims=True) + EPS)
        yn = y * r1
        gp = gp_ref[...]
        e = x_ref[...] + yn * gp - t_ref[...]
        loss_ref[...] += jnp.sum(e * e, axis=0, keepdims=True)
        dout = e * (1.0 / D_MODEL)
        dout_ref[...] = dout
        dgp_ref[...] += jnp.sum(dout * yn, axis=0, keepdims=True)
        dyn = dout * gp
        dy = r1 * (dyn - yn * jnp.mean(dyn * yn, axis=-1, keepdims=True))
        dy_bf = dy.astype(BF16)
        dy_ref[...] = dy_bf
        dm = _dot(dy_bf, wot_ref[...])
        dml_a = dm * y_a * g_a * (1.0 - g_a)
        dml_b = dm * y_b * g_b * (1.0 - g_b)
        dpj_ref[:, :D_MODEL] = dml_a.astype(BF16)
        dpj_ref[:, D_MODEL:2 * D_MODEL] = dml_b.astype(BF16)
        dbg_ref[:, :D_MODEL] += jnp.sum(dml_a, axis=0, keepdims=True)
        dbg_ref[:, D_MODEL:] += jnp.sum(dml_b, axis=0, keepdims=True)
        dya_bf = (dm * g_a).astype(BF16)
        dyb_bf = (dm * g_b).astype(BF16)
        dya_ref[...] = dya_bf
        dyb_ref[...] = dyb_bf
        dya_in = _dot(dya_bf, wat_ref[...])
        dyb_in = _dot(dyb_bf, wbt_ref[...])
        dattn = dya_in * silu_a
        delta = _sel_right(dattn * attn_v, ones)
        lane = lax.broadcasted_iota(jnp.int32, (tm, LANE), 1)
        for p in range(HEADS // 2):
            sl = slice(LANE * p, LANE * (p + 1))
            xs = (dattn[:, sl], pltpu.roll(dattn[:, sl], VDIM, 1))
            nds = (-pltpu.roll(delta[:, sl], VDIM, 1), -delta[:, sl])
            for a in range(2):
                hi, lo_part = _hi_lo(nds[a])
                blk = jnp.where(lane < VDIM, xs[a], jnp.where(lane == VDIM, hi, jnp.where(lane == VDIM + 1, lo_part, 0.0)))
                dop_ref[:, LANE * (2 * p + a):LANE * (2 * p + a + 1)] = blk.astype(BF16)
        dpj_ref[:, 2 * D_MODEL:2 * D_MODEL + HG_WIDTH] = (
            dya_in * attn_v * (sa * (1.0 + gate_a * (1.0 - sa)))).astype(BF16)
        don = dyb_in * silu_b
        dpj_ref[:, 2 * D_MODEL + HG_WIDTH:] = (dyb_in * on * (sb * (1.0 + gate_b * (1.0 - sb)))).astype(BF16)
        dgh_ref[...] += jnp.sum(don * ohat, axis=0, keepdims=True)
        dohat = don * ghv
        do_ref[...] = ro * (dohat - ohat * (_sel_right(dohat * ohat, ones) * (1.0 / 64.0)))

    row = lambda w, j: pl.BlockSpec((tm, w), lambda i: (i, j))
    col = lambda w: pl.BlockSpec((w, tm), lambda i: (0, i))
    full = lambda a: pl.BlockSpec(a.shape, lambda i: (0, 0))
    acc = lambda w: pl.BlockSpec((1, w), lambda i: (0, 0))
    sds = lambda w, dt: jax.ShapeDtypeStruct((s, w), dt)
    sdt = lambda w: jax.ShapeDtypeStruct((w, s), BF16)
    return pl.pallas_call(
        body,
        grid=(s // tm,),
        in_specs=[row(1024, 0), row(1024, 0), row(2048, 0), row(512, 4), row(512, 5), row(512, 0), row(512, 0),
                  full(w_a), full(w_b), full(w_out), full(w_at), full(w_bt), full(w_outt),
                  full(b_gate), full(g_post), full(gh), full(ones64)],
        out_specs=[row(1024, 0), row(3072, 0), row(1024, 0), row(512, 0),
                   col(1024), row(1024, 0), col(512), row(1024, 0), col(512), row(1024, 0),
                   acc(1024), acc(1024), acc(2048), acc(512)],
        out_shape=[sds(1024, F32), sds(D_IN_PAD, BF16), sds(1024, BF16), sds(512, F32),
                   sdt(1024), sds(1024, BF16), sdt(512), sds(1024, BF16), sdt(512), sds(1024, BF16),
                   jax.ShapeDtypeStruct((1, 1024), F32), jax.ShapeDtypeStruct((1, 1024), F32),
                   jax.ShapeDtypeStruct((1, 2048), F32), jax.ShapeDtypeStruct((1, 512), F32)],
        compiler_params=_params(("arbitrary",), 56),
        name="tail",
    )(x, tgt, proj, proj, proj, attn, o, w_a, w_b, w_out, w_at, w_bt, w_outt, b_gate, g_post, gh, ones64)


def _mla_bwd(proj, dqr, dkr, dv, g_q, g_kv, w_uq_pt, w_kv_pt, rc, rs1, rs2, dproj):
    s = proj.shape[0]
    tm = 256
    scale = 1.0 / math.sqrt(QK)

    def body(cq_ref, ckv_ref, dqr_ref, dkr_ref, dv_ref, gq_ref, gkv_ref, wuqt_ref, wkvt_ref, c_ref, s1_ref, s2_ref,
             dproj_in, dqf_ref, dkvf_ref, dc_ref, dgq_ref, dgkv_ref):
        del dproj_in

        @pl.when(pl.program_id(0) == 0)
        def _():
            dgq_ref[...] = jnp.zeros_like(dgq_ref)
            dgkv_ref[...] = jnp.zeros_like(dgkv_ref)

        c, s1, s2 = c_ref[...], s1_ref[...], s2_ref[...]
        lane = lax.broadcasted_iota(jnp.int32, (tm, LANE), 1)
        ksum = jnp.zeros((tm, LANE), F32)
        for h in range(HEADS):
            sl = slice(LANE * h, LANE * (h + 1))
            dqf_ref[:, sl] = (_unrope(dqr_ref[:, sl], c, s1, s2) * scale).astype(BF16)
            dkh = dkr_ref[:, sl]
            ksum = ksum + dkh
            dkvf_ref[:, sl] = jnp.where(lane < NOPE, dkh, 0.0).astype(BF16)
            dkvf_ref[:, HEADS * LANE + LANE * h:HEADS * LANE + LANE * (h + 1)] = jnp.where(
                lane < VDIM, dv_ref[:, sl], 0.0).astype(BF16)
        dkpe = _unrope(ksum, c, s1, s2)
        dc_ref[:, Q_LORA + KV_LORA:] = jnp.where((lane >= NOPE) & (lane < QK), dkpe, 0.0).astype(BF16)
        dcqn = _dot(dqf_ref[...], wuqt_ref[...])
        dckvn = _dot(dkvf_ref[...], wkvt_ref[...])
        for x_ref, g_ref, dn, cols, dg_ref in ((cq_ref, gq_ref, dcqn, slice(0, Q_LORA), dgq_ref),
                                               (ckv_ref, gkv_ref, dckvn, slice(Q_LORA, Q_LORA + KV_LORA), dgkv_ref)):
            xv = x_ref[...]
            r = lax.rsqrt(jnp.mean(xv * xv, axis=-1, keepdims=True) + EPS)
            xh = xv * r
            dg_ref[...] += jnp.sum(dn * xh, axis=0, keepdims=True)
            dh = dn * g_ref[...]
            dc_ref[:, cols] = (r * (dh - xh * jnp.mean(dh * xh, axis=-1, keepdims=True))).astype(BF16)

    row = lambda w, j: pl.BlockSpec((tm, w), lambda i: (i, j))
    full = lambda a: pl.BlockSpec(a.shape, lambda i: (0, 0))
    acc = lambda w: pl.BlockSpec((1, w), lambda i: (0, 0))
    sds = lambda w, dt: jax.ShapeDtypeStruct((s, w), dt)
    return pl.pallas_call(
        body,
        grid=(s // tm,),
        in_specs=[row(768, 6), row(256, 21), row(1024, 0), row(1024, 0), row(1024, 0), full(g_q), full(g_kv),
                  full(w_uq_pt), full(w_kv_pt), row(128, 0), row(128, 0), row(128, 0),
                  pl.BlockSpec(memory_space=pl.ANY)],
        out_specs=[row(1024, 0), row(2048, 0), row(1152, 4), acc(768), acc(256)],
        out_shape=[sds(1024, BF16), sds(2048, BF16), jax.ShapeDtypeStruct(dproj.shape, BF16),
                   jax.ShapeDtypeStruct((1, 768), F32), jax.ShapeDtypeStruct((1, 256), F32)],
        input_output_aliases={12: 2},
        compiler_params=_params(("arbitrary",)),
        name="mla_bwd",
    )(proj, proj, dqr, dkr, dv, g_q, g_kv, w_uq_pt, w_kv_pt, rc, rs1, rs2, dproj)


def _pick(n, options):
    for o in options:
        if n % o == 0:
            return o
    raise ValueError(n)


def _matmul(a, b, name):
    m, k = a.shape
    n = b.shape[1]
    tm = _pick(m, (1024, 768, 512, 256))
    tn = _pick(n, (1152, 1024, 768, 512))
    tk = _pick(k, (1024, 512))
    nk = k // tk

    def body(a_ref, b_ref, o_ref):
        @pl.when(pl.program_id(2) == 0)
        def _():
            o_ref[...] = jnp.zeros_like(o_ref)

        o_ref[...] += _dot(a_ref[...], b_ref[...])

    return pl.pallas_call(
        body,
        grid=(m // tm, n // tn, nk),
        in_specs=[pl.BlockSpec((tm, tk), lambda i, j, l: (i, l)), pl.BlockSpec((tk, tn), lambda i, j, l: (l, j))],
        out_specs=pl.BlockSpec((tm, tn), lambda i, j, l: (i, j)),
        out_shape=jax.ShapeDtypeStruct((m, n), F32),
        compiler_params=_params(("arbitrary", "arbitrary", "arbitrary")),
        name=name,
    )(a, b)


def _dh_dx(dproj, w_in_pt, x, dout, g_pre):
    s, k = dproj.shape
    tm = 256

    def body(dp_ref, w_ref, x_ref, dout_ref, g_ref, dx_ref, dg_ref):
        @pl.when(pl.program_id(0) == 0)
        def _():
            dg_ref[...] = jnp.zeros_like(dg_ref)

        dh = _dot(dp_ref[...], w_ref[...])
        xv = x_ref[...]
        r = lax.rsqrt(jnp.mean(xv * xv, axis=-1, keepdims=True) + EPS)
        xh = xv * r
        dg_ref[...] += jnp.sum(dh * xh, axis=0, keepdims=True)
        dxh = dh * g_ref[...]
        dx_ref[...] = dout_ref[...] + r * (dxh - xh * jnp.mean(dxh * xh, axis=-1, keepdims=True))

    row = lambda w: pl.BlockSpec((tm, w), lambda i: (i, 0))
    return pl.pallas_call(
        body,
        grid=(s // tm,),
        in_specs=[row(k), pl.BlockSpec((k, D_MODEL), lambda i: (0, 0)), row(D_MODEL), row(D_MODEL),
                  pl.BlockSpec((1, D_MODEL), lambda i: (0, 0))],
        out_specs=[row(D_MODEL), pl.BlockSpec((1, D_MODEL), lambda i: (0, 0))],
        out_shape=[jax.ShapeDtypeStruct((s, D_MODEL), F32), jax.ShapeDtypeStruct((1, D_MODEL), F32)],
        compiler_params=_params(("arbitrary",)),
        name="dh_dx",
    )(dproj, w_in_pt, x, dout, g_pre)


def _rope_tables(s):
    inv = (np.float32(ROPE_THETA) ** (-np.arange(0, ROPE, 2, dtype=np.float32) / np.float32(ROPE))).astype(np.float32)
    ang = (np.arange(s, dtype=np.float32)[:, None] * inv[None, :]).astype(np.float32)
    cos, sin = jnp.asarray(np.cos(ang.astype(np.float64)), F32), jnp.asarray(np.sin(ang.astype(np.float64)), F32)
    z = lambda w: jnp.zeros((s, w), F32)
    rc = jnp.concatenate([jnp.ones((s, NOPE), F32), cos, cos, z(32)], axis=1)
    rs1 = jnp.concatenate([z(NOPE), -sin, z(16), z(32)], axis=1)
    rs2 = jnp.concatenate([z(NOPE), z(16), sin, z(32)], axis=1)
    return rc, rs1, rs2


def _local_step(x, tgt, w_in_slots, w_uq, w_ukv, w_a, w_b, w_out, g_pre, b_gate, g_q, g_kv, lbl, g_hgrn, g_post):
    s = x.shape[0]
    w_in_p, w_in_pt = _assemble_w_in(w_in_slots)
    w_uq_p = jnp.pad(w_uq.reshape(Q_LORA, HEADS, QK), ((0, 0), (0, 0), (0, LANE - QK))).reshape(Q_LORA, HEADS * LANE)
    kv3 = w_ukv.reshape(KV_LORA, HEADS, NOPE + VDIM)
    pad64 = lambda t: jnp.pad(t, ((0, 0), (0, 0), (0, LANE - 64))).reshape(KV_LORA, HEADS * LANE)
    w_kv_p = jnp.concatenate([pad64(kv3[:, :, :NOPE]), pad64(kv3[:, :, NOPE:])], axis=1)
    rc, rs1, rs2 = _rope_tables(s)
    gh = jnp.tile(g_hgrn, (1, HEADS))

    proj, ht = _norm_proj(x, g_pre, w_in_p)
    qr, kr, v, cqt, ckvt = _mla_prep(proj, g_q, g_kv, w_uq_p, w_kv_p, rc, rs1, rs2)
    attn, qa = _attn_fwd(qr, kr, v)
    o, sprev = _hgrn_fwd(proj, lbl)
    (dout, dproj, dop, do, mt, dy_bf, yat, dya_bf, ybt, dyb_bf,
     loss_vec, dg_post, db_gate, dgh) = _tail(x, tgt, proj, attn, o, w_a, w_b, w_out, w_a.T, w_b.T, w_out.T,
                                               b_gate, g_post, gh)
    dqr, dkr, dv = _attn_bwd(qa, kr, v, dop)
    dproj, dlbl = _hgrn_bwd(proj, lbl, do, sprev, dproj)
    dqf, dkvf, dproj, dg_q, dg_kv = _mla_bwd(proj, dqr, dkr, dv, g_q, g_kv, w_uq_p.T, w_kv_p.T, rc, rs1, rs2, dproj)
    dx, dg_pre = _dh_dx(dproj, w_in_pt, x, dout, g_pre)

    dw_in_slots = _scatter_w_in(_matmul(ht, dproj, "dw_in"))
    dw_out = _matmul(mt, dy_bf, "dw_out")
    dw_a = _matmul(yat, dya_bf, "dw_a")
    dw_b = _matmul(ybt, dyb_bf, "dw_b")
    dw_uq_p = _matmul(cqt, dqf, "dw_uq")
    dw_kv_p = _matmul(ckvt, dkvf, "dw_kv")

    dw_uq = dw_uq_p.reshape(Q_LORA, HEADS, LANE)[:, :, :QK].reshape(Q_LORA, HEADS * QK)
    dw_ukv = jnp.concatenate([dw_kv_p[:, :HEADS * LANE].reshape(KV_LORA, HEADS, LANE)[:, :, :NOPE],
                              dw_kv_p[:, HEADS * LANE:].reshape(KV_LORA, HEADS, LANE)[:, :, :VDIM]],
                             axis=2).reshape(KV_LORA, 1024)
    loss = 0.5 / D_MODEL * jnp.sum(loss_vec)
    grads = dict(g_pre=dg_pre, b_gate=db_gate, g_q=dg_q, w_uq=dw_uq, g_kv=dg_kv, w_ukv=dw_ukv,
                 lb_logits=dlbl, g_hgrn=jnp.sum(dgh.reshape(HEADS, VDIM), axis=0, keepdims=True),
                 w_branch_a=dw_a, w_branch_b=dw_b, w_out=dw_out, g_post=dg_post)
    return loss, dx, dw_in_slots, grads


def _my_place():
    return lax.axis_index("x"), lax.axis_index("y"), lax.axis_index("c")


def _all_gather(blocks):
    n = len(blocks)

    def body(*refs):
        x_refs, out_refs = refs[:n], refs[n:2 * n]
        send_sems, recv_sems, local_sems = refs[2 * n:]
        x, y, c = _my_place()
        me, sibling = (x, y, c), (x, y, 1 - c)
        chips = [(1 - x, y), (x, 1 - y), (1 - x, 1 - y)]

        def slot(a, px, py, pc):
            return out_refs[a].at[4 * px + 2 * py + pc]

        def copy(a, k, blk, to, src=None):
            return pltpu.make_async_remote_copy(
                src_ref=slot(a, *blk) if src is None else src, dst_ref=slot(a, *blk),
                send_sem=send_sems.at[7 * a + k], recv_sem=recv_sems.at[7 * a + k],
                device_id=to, device_id_type=MESH_ID)

        mine = [pltpu.make_async_copy(x_refs[a], slot(a, *me), local_sems.at[a]) for a in range(n)]
        for cp in mine:
            cp.start()
        first = [copy(a, 0, me, sibling, src=x_refs[a]) for a in range(n)]
        first += [copy(a, 1 + j, me, (*chip, c), src=x_refs[a]) for a in range(n) for j, chip in enumerate(chips)]
        for cp in first:
            cp.start()
        passed = []
        for j, chip in enumerate(chips):
            for a in range(n):
                copy(a, 1 + j, (*chip, c), me).wait_recv()
                passed.append(copy(a, 4 + j, (*chip, c), sibling))
                passed[-1].start()
        for a in range(n):
            copy(a, 0, sibling, me).wait_recv()
        for j, chip in enumerate(chips):
            for a in range(n):
                copy(a, 4 + j, (*chip, 1 - c), me).wait_recv()
        for cp in first + passed:
            cp.wait_send()
        for cp in mine:
            cp.wait()

    return pl.pallas_call(
        body,
        out_shape=[jax.ShapeDtypeStruct((N_DEV,) + b.shape, b.dtype) for b in blocks],
        in_specs=[pl.BlockSpec(memory_space=pl.ANY)] * n,
        out_specs=[pl.BlockSpec(memory_space=pl.ANY)] * n,
        scratch_shapes=[pltpu.SemaphoreType.DMA((7 * n,)), pltpu.SemaphoreType.DMA((7 * n,)),
                        pltpu.SemaphoreType.DMA((n,))],
        name="gather_weights",
    )(*blocks)


def _exchange(sends):
    n = len(sends)

    def body(*refs):
        s_refs, r_refs = refs[:n], refs[n:2 * n]
        send_sems, recv_sems, local_sems = refs[2 * n:]
        x, y, c = _my_place()
        me = 4 * x + 2 * y + c
        mine = [pltpu.make_async_copy(s_refs[a].at[me], r_refs[a].at[me], local_sems.at[a]) for a in range(n)]
        for cp in mine:
            cp.start()
        copies = []
        for k in range(N_DEV - 1):
            fx, fy, fc = (k + 1) >> 2 & 1, (k + 1) >> 1 & 1, (k + 1) & 1
            px = 1 - x if fx else x
            py = 1 - y if fy else y
            pc = 1 - c if fc else c
            for a in range(n):
                copies.append(pltpu.make_async_remote_copy(
                    src_ref=s_refs[a].at[4 * px + 2 * py + pc], dst_ref=r_refs[a].at[me],
                    send_sem=send_sems.at[7 * a + k], recv_sem=recv_sems.at[7 * a + k],
                    device_id=(px, py, pc), device_id_type=MESH_ID))
        for cp in copies:
            cp.start()
        for cp in copies:
            cp.wait_recv()
        for cp in copies:
            cp.wait_send()
        for cp in mine:
            cp.wait()

    return pl.pallas_call(
        body,
        out_shape=[jax.ShapeDtypeStruct(t.shape, t.dtype) for t in sends],
        in_specs=[pl.BlockSpec(memory_space=pl.ANY)] * n,
        out_specs=[pl.BlockSpec(memory_space=pl.ANY)] * n,
        scratch_shapes=[pltpu.SemaphoreType.DMA((7 * n,)), pltpu.SemaphoreType.DMA((7 * n,)),
                        pltpu.SemaphoreType.DMA((n,))],
        name="exchange_grads",
    )(*sends)


def _adamw(g, w, m, v):
    c1 = 1.0 / (1.0 - ADAM_B1 ** ADAM_STEP)
    c2 = 1.0 / (1.0 - ADAM_B2 ** ADAM_STEP)
    nm = ADAM_B1 * m + (1.0 - ADAM_B1) * g
    nv = ADAM_B2 * v + (1.0 - ADAM_B2) * (g * g)
    d = -ADAM_LR * ((nm * c1) / (jnp.sqrt(nv * c2) + ADAM_EPS) + ADAM_WD * w)
    return d, nm, nv


def _sum8(r_ref):
    g = r_ref[0].astype(F32)
    for k in range(1, r_ref.shape[0]):
        g = g + r_ref[k].astype(F32)
    return g


def _sum_adamw_w_in(recv, w, m, v):
    rows, _, cols = w.shape
    tc = 256

    def body(r_ref, w_ref, m_ref, v_ref, g_ref, d_ref, nm_ref, nv_ref):
        g = _sum8(r_ref)
        dense = lambda ref: ref[...].reshape(rows, tc)
        d, nm, nv = _adamw(g, dense(w_ref), dense(m_ref), dense(v_ref))
        for ref, val in ((g_ref, g), (d_ref, d), (nm_ref, nm), (nv_ref, nv)):
            ref[...] = val.reshape(rows, 1, tc)

    blk = pl.BlockSpec((rows, 1, tc), lambda i: (0, 0, i))
    out = jax.ShapeDtypeStruct((rows, 1, cols), F32)
    return pl.pallas_call(
        body,
        grid=(cols // tc,),
        in_specs=[pl.BlockSpec((recv.shape[0], rows, tc), lambda i: (0, 0, i)), blk, blk, blk],
        out_specs=[blk, blk, blk, blk],
        out_shape=[out, out, out, out],
        compiler_params=_params(("arbitrary",)),
        name="sum_adamw_w_in",
    )(recv, w, m, v)


def _sum_adamw_whole(recvs, ws, ms, vs):
    n = len(ws)

    def body(*refs):
        r_refs, w_refs, m_refs, v_refs = refs[:n], refs[n:2 * n], refs[2 * n:3 * n], refs[3 * n:4 * n]
        outs = refs[4 * n:]
        for a in range(n):
            g = _sum8(r_refs[a])
            d, nm, nv = _adamw(g, w_refs[a][...], m_refs[a][...], v_refs[a][...])
            outs[a][...] = g
            outs[n + a][...] = d
            outs[2 * n + a][...] = nm
            outs[3 * n + a][...] = nv

    shapes = [jax.ShapeDtypeStruct(w.shape, F32) for w in ws]
    res = pl.pallas_call(
        body,
        out_shape=shapes * 4,
        compiler_params=pltpu.CompilerParams(vmem_limit_bytes=48 * 2**20),
        name="sum_adamw_small",
    )(*recvs, *ws, *ms, *vs)
    return res[:n], res[n:2 * n], res[2 * n:3 * n], res[3 * n:]


MATS = ("w_uq", "w_ukv", "w_branch_a", "w_branch_b", "w_out")
SMALL = ("g_pre", "b_gate", "g_q", "g_kv", "lb_logits", "g_hgrn", "g_post")
SMALL_ROWS = (1, 2, 1, 1, 1, 1, 1)
LOSS_AT = (6, 1023)
SMALL_SHAPE = dict(g_pre=(1, 1024), b_gate=(1, 2048), g_q=(1, 768), g_kv=(1, 256), lb_logits=(2, 512),
                   g_hgrn=(1, 64), g_post=(1, 1024))
COL_SHARDED = dict(w_uq=False, w_ukv=True, w_branch_a=True, w_branch_b=True, w_out=False)
ORDER = ("g_pre", "w_in", "b_gate", "g_q", "w_uq", "g_kv", "w_ukv", "lb_logits", "g_hgrn",
         "w_branch_a", "w_branch_b", "w_out", "g_post")


def _pack_small(t):
    parts = []
    for n, r in zip(SMALL, SMALL_ROWS):
        flat = t[n].reshape(1, -1)
        parts.append(jnp.pad(flat, ((0, 0), (0, r * 1024 - flat.shape[1]))).reshape(r, 1024))
    return jnp.concatenate(parts, axis=0)


def _unpack_small(p):
    out, r0 = {}, 0
    for n, r in zip(SMALL, SMALL_ROWS):
        shp = SMALL_SHAPE[n]
        out[n] = p[r0:r0 + r].reshape(1, -1)[:, :shp[0] * shp[1]].reshape(shp)
        r0 += r
    return out


def _to_slots(name, full):
    r, c = full.shape
    if COL_SHARDED[name]:
        return full.reshape(r, N_DEV, c // N_DEV).transpose(1, 0, 2)
    return full.reshape(N_DEV, r // N_DEV, c)


def _from_slots(name, slots):
    _, r, c = slots.shape
    if COL_SHARDED[name]:
        return slots.transpose(1, 0, 2).reshape(r, N_DEV * c)
    return slots.reshape(N_DEV * r, c)


def kernel(x, g_pre, w_in, b_gate, g_q, w_uq, g_kv, w_ukv, lb_logits, g_hgrn, w_branch_a, w_branch_b, w_out, g_post, loss_target, m_g_pre, m_w_in, m_b_gate, m_g_q, m_w_uq, m_g_kv, m_w_ukv, m_lb_logits, m_g_hgrn, m_w_branch_a, m_w_branch_b, m_w_out, m_g_post, v_g_pre, v_w_in, v_b_gate, v_g_q, v_w_uq, v_g_kv, v_w_ukv, v_lb_logits, v_g_hgrn, v_w_branch_a, v_w_branch_b, v_w_out, v_g_post):
    rows3 = lambda a: jnp.transpose(a, (2, 0, 1))
    w = dict(w_in=rows3(w_in), w_uq=w_uq[0], w_ukv=w_ukv[0], w_branch_a=w_branch_a[0], w_branch_b=w_branch_b[0],
             w_out=w_out[0], g_pre=g_pre, b_gate=b_gate, g_q=g_q, g_kv=g_kv, lb_logits=lb_logits, g_hgrn=g_hgrn,
             g_post=g_post)
    mom = dict(w_in=rows3(m_w_in), w_uq=m_w_uq[0], w_ukv=m_w_ukv[0], w_branch_a=m_w_branch_a[0],
               w_branch_b=m_w_branch_b[0], w_out=m_w_out[0], g_pre=m_g_pre, b_gate=m_b_gate, g_q=m_g_q, g_kv=m_g_kv,
               lb_logits=m_lb_logits, g_hgrn=m_g_hgrn, g_post=m_g_post)
    var = dict(w_in=rows3(v_w_in), w_uq=v_w_uq[0], w_ukv=v_w_ukv[0], w_branch_a=v_w_branch_a[0],
               w_branch_b=v_w_branch_b[0], w_out=v_w_out[0], g_pre=v_g_pre, b_gate=v_b_gate, g_q=v_g_q, g_kv=v_g_kv,
               lb_logits=v_lb_logits, g_hgrn=v_g_hgrn, g_post=v_g_post)

    gathered = _all_gather([w["w_in"].reshape(W_IN_SHARD, D_MODEL).astype(BF16)] + [w[n].astype(BF16) for n in MATS])
    full = {n: _from_slots(n, g) for n, g in zip(MATS, gathered[1:])}

    loss, dx, dw_in_slots, grads = _local_step(
        x[0], loss_target[0], gathered[0], full["w_uq"], full["w_ukv"], full["w_branch_a"], full["w_branch_b"],
        full["w_out"], g_pre, b_gate, g_q, g_kv, lb_logits, g_hgrn, g_post)

    small = _pack_small(grads).at[LOSS_AT].set(loss)
    small = jnp.broadcast_to(small[None], (N_DEV, 8, 1024))
    sends = [dw_in_slots] + [_to_slots(n, grads[n]).astype(BF16) for n in MATS] + [small]
    recvs = _exchange(sends)

    g_in, d_in, m_in, v_in = _sum_adamw_w_in(recvs[0], w["w_in"], mom["w_in"], var["w_in"])
    names = MATS + ("small",)
    pk = lambda t: [t[n] for n in MATS] + [_pack_small(t)]
    res = _sum_adamw_whole(recvs[1:], pk(w), pk(mom), pk(var))

    outs = []
    for kind, big in zip(res, (g_in, d_in, m_in, v_in)):
        t = dict(zip(names, kind))
        t = {**{n: t[n][None] for n in MATS}, **_unpack_small(t["small"]),
             "w_in": jnp.transpose(big, (1, 2, 0))}
        outs += [t[n] for n in ORDER]
    total = res[0][-1][LOSS_AT]
    return (total, dx[None], *outs)
```

```python
import math

import jax
import jax.numpy as jnp
import numpy as np
from jax import lax
from jax.experimental import pallas as pl
from jax.experimental.pallas import tpu as pltpu

F32, BF16 = jnp.float32, jnp.bfloat16

D_MODEL = 1024
EPS = 1e-6
HEADS = 8
NOPE, ROPE, VDIM = 64, 32, 64
QK = NOPE + ROPE
Q_LORA, KV_LORA = 768, 256
ROPE_THETA = 10000.0
ATT_CHUNK_SHIFT = 6
HG_BLOCK = 32
HG_WIDTH = 512
D_IN = 5664
D_IN_PAD = 5760
W_IN_SHARD = D_IN // 8
N_DEV = 8
LANE = 128

ADAM_LR, ADAM_B1, ADAM_B2, ADAM_EPS, ADAM_WD, ADAM_STEP = 0.001, 0.9, 0.999, 1e-08, 0.01, 10

W_IN_SEGMENTS = ((3616, 5664, 0), (1056, 1568, 2048), (3104, 3616, 2560), (1568, 3104, 3072),
                 (0, 1024, 4608), (1024, 1056, 5696))
W_IN_ZERO = ((5632, 5696), (5728, 5760))

NT = (((1,), (1,)), ((), ()))
TN = (((0,), (0,)), ((), ()))
MESH_ID = pl.DeviceIdType.MESH


def _w_in_pieces():
    out = []
    for lo, hi, dst in W_IN_SEGMENTS:
        c = lo
        while c < hi:
            p = c // W_IN_SHARD
            e = min(hi, (p + 1) * W_IN_SHARD)
            out.append((p, c - p * W_IN_SHARD, e - p * W_IN_SHARD, dst + c - lo))
            c = e
    return out


def _params(sem, vmem_mb=48):
    return pltpu.CompilerParams(dimension_semantics=sem, vmem_limit_bytes=vmem_mb * 2**20)


def _dot(a, b):
    return jnp.dot(a, b, preferred_element_type=F32)


def _dotg(a, b, dims):
    return lax.dot_general(a, b, dims, preferred_element_type=F32)


def _split2(x):
    hi = x.astype(BF16)
    return hi, (x - hi.astype(F32)).astype(BF16)


def _sel_left(m01, x):
    hi, lo = _split2(x)
    return _dot(m01, hi) + _dot(m01, lo)


def _sel_right(x, m01):
    hi, lo = _split2(x)
    return _dot(hi, m01) + _dot(lo, m01)


def _hi_lo(x):
    hi = x.astype(BF16).astype(F32)
    return hi, x - hi


def _sigmoid(x):
    return 0.5 * jnp.tanh(0.5 * x) + 0.5


def _rope(x, c, s1, s2):
    return x * c + pltpu.roll(x, 112, 1) * s1 + pltpu.roll(x, 16, 1) * s2


def _unrope(d, c, s1, s2):
    return d * c + pltpu.roll(d * s1, 16, 1) + pltpu.roll(d * s2, 112, 1)


def _my_place():
    return lax.axis_index("x"), lax.axis_index("y"), lax.axis_index("c")


def _flip(k, x, y, c):
    fx, fy, fc = (k + 1) >> 2 & 1, (k + 1) >> 1 & 1, (k + 1) & 1
    return (1 - x if fx else x), (1 - y if fy else y), (1 - c if fc else c)


def _to_all_copies(s_refs, r_refs, sems, spread):
    send_sems, recv_sems, local_sems = sems
    x, y, c = _my_place()
    me = 4 * x + 2 * y + c
    src = (lambda a, p: s_refs[a]) if spread else (lambda a, p: s_refs[a].at[p])
    local = [pltpu.make_async_copy(src(a, me), r_refs[a].at[me], local_sems.at[a]) for a in range(len(s_refs))]
    remote = []
    for k in range(N_DEV - 1):
        px, py, pc = _flip(k, x, y, c)
        for a in range(len(s_refs)):
            remote.append(pltpu.make_async_remote_copy(
                src_ref=src(a, 4 * px + 2 * py + pc), dst_ref=r_refs[a].at[me],
                send_sem=send_sems.at[7 * a + k], recv_sem=recv_sems.at[7 * a + k],
                device_id=(px, py, pc), device_id_type=MESH_ID))
    return local, remote


def _to_chips_copies(s_refs, r_refs, sems):
    send_sems, recv_sems, local_sems = sems
    x, y, c = _my_place()
    me = 2 * x + y
    local = [pltpu.make_async_copy(s_refs[a].at[me], r_refs[a].at[me], local_sems.at[a]) for a in range(len(s_refs))]
    remote = []
    for k in range(3):
        px = 1 - x if (k + 1) >> 1 & 1 else x
        py = 1 - y if (k + 1) & 1 else y
        for a in range(len(s_refs)):
            remote.append(pltpu.make_async_remote_copy(
                src_ref=s_refs[a].at[2 * px + py], dst_ref=r_refs[a].at[me],
                send_sem=send_sems.at[3 * a + k], recv_sem=recv_sems.at[3 * a + k],
                device_id=(px, py, c), device_id_type=MESH_ID))
    return local, remote


def _start_all(local, remote):
    for cp in local + remote:
        cp.start()


def _wait_all(local, remote):
    for cp in remote:
        cp.wait_recv()
    for cp in remote:
        cp.wait_send()
    for cp in local:
        cp.wait()


def _copy_sems(n, peers):
    return [pltpu.SemaphoreType.DMA((peers * n,)), pltpu.SemaphoreType.DMA((peers * n,)),
            pltpu.SemaphoreType.DMA((n,))]


ANY = pl.BlockSpec(memory_space=pl.ANY)


def _assemble_w_in(slots):
    tc = 256
    pieces = _w_in_pieces()

    def body(s_ref, w_ref, wt_ref):
        for lo, hi in W_IN_ZERO:
            wt_ref[lo:hi, :] = jnp.zeros((hi - lo, tc), BF16)
        for p, lo, hi, dst in pieces:
            wt_ref[dst:dst + hi - lo, :] = s_ref[p, lo:hi, :]
        w_ref[...] = wt_ref[...].T

    return pl.pallas_call(
        body,
        grid=(D_MODEL // tc,),
        in_specs=[pl.BlockSpec((N_DEV, W_IN_SHARD, tc), lambda i: (0, 0, i))],
        out_specs=[pl.BlockSpec((tc, D_IN_PAD), lambda i: (i, 0)), pl.BlockSpec((D_IN_PAD, tc), lambda i: (0, i))],
        out_shape=[jax.ShapeDtypeStruct((D_MODEL, D_IN_PAD), BF16), jax.ShapeDtypeStruct((D_IN_PAD, D_MODEL), BF16)],
        compiler_params=_params(("arbitrary",)),
        name="assemble_w_in",
    )(slots)


def _scatter_w_in(dw):
    tc = 256
    pieces = _w_in_pieces()

    def body(d_ref, s_ref):
        dt = d_ref[...].T
        for p, lo, hi, dst in pieces:
            s_ref[p, lo:hi, :] = dt[dst:dst + hi - lo, :].astype(BF16)

    return pl.pallas_call(
        body,
        grid=(D_MODEL // tc,),
        in_specs=[pl.BlockSpec((tc, D_IN_PAD), lambda i: (i, 0))],
        out_specs=pl.BlockSpec((N_DEV, W_IN_SHARD, tc), lambda i: (0, 0, i)),
        out_shape=jax.ShapeDtypeStruct((N_DEV, W_IN_SHARD, D_MODEL), BF16),
        compiler_params=_params(("arbitrary",)),
        name="scatter_w_in",
    )(dw)


def _norm_proj(x, g_pre, w, shards):
    s, n = x.shape[0], w.shape[1]
    tm, tn = 512, 1152
    ni, nj, ns = s // tm, n // tn, len(shards)

    def body(x_ref, g_ref, w_ref, *rest):
        shard_refs, (proj_ref, ht_ref), got_refs = rest[:ns], rest[ns:ns + 2], rest[ns + 2:2 * ns + 2]
        h_ref, sems = rest[2 * ns + 2], rest[2 * ns + 3:]
        j, i = pl.program_id(0), pl.program_id(1)
        rows = pl.ds(pl.multiple_of(i * tm, tm), tm)

        @pl.when((j == 0) & (i == 0))
        def _():
            _start_all(*_to_all_copies(shard_refs, got_refs, sems, True))

        @pl.when(j == 0)
        def _():
            xv = x_ref[...]
            r = lax.rsqrt(jnp.mean(xv * xv, axis=-1, keepdims=True) + EPS)
            h = (xv * r * g_ref[...]).astype(BF16)
            h_ref[rows, :] = h
            ht_ref[...] = h.T

        proj_ref[...] = _dot(h_ref[rows, :], w_ref[...])

        @pl.when((j == nj - 1) & (i == ni - 1))
        def _():
            _wait_all(*_to_all_copies(shard_refs, got_refs, sems, True))

    first = lambda j, i: jnp.where(j == 0, i, ni - 1)
    return pl.pallas_call(
        body,
        grid=(nj, ni),
        in_specs=[
            pl.BlockSpec((tm, D_MODEL), lambda j, i: (first(j, i), 0)),
            pl.BlockSpec((1, D_MODEL), lambda j, i: (0, 0)),
            pl.BlockSpec((D_MODEL, tn), lambda j, i: (0, j)),
        ] + [ANY] * ns,
        out_specs=[
            pl.BlockSpec((tm, tn), lambda j, i: (i, j)),
            pl.BlockSpec((D_MODEL, tm), lambda j, i: (0, first(j, i))),
        ] + [ANY] * ns,
        out_shape=[jax.ShapeDtypeStruct((s, n), F32), jax.ShapeDtypeStruct((D_MODEL, s), BF16)]
        + [jax.ShapeDtypeStruct((N_DEV,) + b.shape, b.dtype) for b in shards],
        scratch_shapes=[pltpu.VMEM((s, D_MODEL), BF16)] + _copy_sems(ns, 7),
        compiler_params=_params(("arbitrary", "arbitrary")),
        name="norm_proj",
    )(x, g_pre, w, *shards)


def _mla_prep(proj, g_q, g_kv, w_uq_p, w_kv_p, rc, rs1, rs2):
    s = proj.shape[0]
    tm = 256
    scale = 1.0 / math.sqrt(QK)

    def body(cq_ref, ckv_ref, kpe_ref, gq_ref, gkv_ref, wuq_ref, wkv_ref, c_ref, s1_ref, s2_ref,
             qr_ref, kr_ref, v_ref, cqt_ref, ckvt_ref):
        cq = cq_ref[...]
        r = lax.rsqrt(jnp.mean(cq * cq, axis=-1, keepdims=True) + EPS)
        cqn = (cq * r * gq_ref[...]).astype(BF16)
        cqt_ref[...] = cqn.T
        q = _dot(cqn, wuq_ref[...])
        ckv = ckv_ref[...]
        r = lax.rsqrt(jnp.mean(ckv * ckv, axis=-1, keepdims=True) + EPS)
        ckvn = (ckv * r * gkv_ref[...]).astype(BF16)
        ckvt_ref[...] = ckvn.T
        kv = _dot(ckvn, wkv_ref[...])
        c, s1, s2 = c_ref[...], s1_ref[...], s2_ref[...]
        lane = lax.broadcasted_iota(jnp.int32, (tm, LANE), 1)
        kpe = _rope(kpe_ref[...], c, s1, s2) + jnp.where((lane == QK) | (lane == QK + 1), 1.0, 0.0)
        vone = jnp.where((lane == VDIM) | (lane == VDIM + 1), 1.0, 0.0)
        for h in range(HEADS):
            sl = slice(LANE * h, LANE * (h + 1))
            qr_ref[:, sl] = (_rope(q[:, sl], c, s1, s2) * scale).astype(BF16)
            kr_ref[:, sl] = (kv[:, sl] + kpe).astype(BF16)
            v_ref[:, sl] = (kv[:, HEADS * LANE + LANE * h:HEADS * LANE + LANE * (h + 1)] + vone).astype(BF16)

    row = lambda w, j: pl.BlockSpec((tm, w), lambda i: (i, j))
    col = lambda w: pl.BlockSpec((w, tm), lambda i: (0, i))
    full = lambda a: pl.BlockSpec(a.shape, lambda i: (0, 0))
    return pl.pallas_call(
        body,
        grid=(s // tm,),
        in_specs=[row(768, 6), row(256, 21), row(128, 44), full(g_q), full(g_kv), full(w_uq_p), full(w_kv_p),
                  row(128, 0), row(128, 0), row(128, 0)],
        out_specs=[row(1024, 0), row(1024, 0), row(1024, 0), col(768), col(256)],
        out_shape=[jax.ShapeDtypeStruct((s, 1024), BF16), jax.ShapeDtypeStruct((s, 1024), BF16),
                   jax.ShapeDtypeStruct((s, 1024), BF16), jax.ShapeDtypeStruct((768, s), BF16),
                   jax.ShapeDtypeStruct((256, s), BF16)],
        compiler_params=_params(("arbitrary",)),
        name="mla_prep",
    )(proj, proj, proj, g_q, g_kv, w_uq_p, w_kv_p, rc, rs1, rs2)


ATT_T = 512


def _chunk_mask(transposed):
    r = lax.broadcasted_iota(jnp.int32, (ATT_T, ATT_T), 0) >> ATT_CHUNK_SHIFT
    c = lax.broadcasted_iota(jnp.int32, (ATT_T, ATT_T), 1) >> ATT_CHUNK_SHIFT
    return (r <= c) if transposed else (c <= r)


def _attn_fwd(qr, kr, vp):
    s = qr.shape[0]
    t = ATT_T

    def body(q_ref, k_ref, v_ref, o_ref, qa_ref):
        qi = pl.program_id(1)
        lane = lax.broadcasted_iota(jnp.int32, (t, LANE), 1)
        sls = [slice(LANE * a, LANE * (a + 1)) for a in range(2)]
        qs = [q_ref[:, sl] for sl in sls]

        def step(j, carry, masked):
            rows = pl.ds(pl.multiple_of(j * t, t), t)
            out = []
            for a in range(2):
                m, acc = carry[a]
                sc = _dotg(qs[a], k_ref[rows, sls[a]], NT)
                if masked:
                    sc = jnp.where(_chunk_mask(False), sc, -1e30)
                m_new = jnp.maximum(m, jnp.max(sc, axis=-1, keepdims=True))
                p = jnp.exp(sc - m_new).astype(BF16)
                acc = jnp.exp(m - m_new) * acc + _dot(p, v_ref[rows, sls[a]])
                out.append((m_new, acc))
            return tuple(out)

        init = tuple((jnp.full((t, 1), -1e30, F32), jnp.zeros((t, LANE), F32)) for _ in range(2))
        carry = lax.fori_loop(0, qi, lambda j, c: step(j, c, False), init)
        carry = step(qi, carry, True)
        outs = []
        for a in range(2):
            m, acc = carry[a]
            l = acc[:, VDIM:VDIM + 1]
            outs.append(acc / l)
            hi, lo_part = _hi_lo(-(m + jnp.log(l)))
            qa = jnp.where(lane == QK, hi, jnp.where(lane == QK + 1, lo_part, qs[a].astype(F32)))
            qa_ref[:, sls[a]] = qa.astype(BF16)
        o_ref[...] = jnp.where(lane < VDIM, outs[0], pltpu.roll(outs[1], VDIM, 1))

    return pl.pallas_call(
        body,
        grid=(HEADS // 2, s // t),
        in_specs=[
            pl.BlockSpec((t, 2 * LANE), lambda h, i: (i, h)),
            pl.BlockSpec((s, 2 * LANE), lambda h, i: (0, h)),
            pl.BlockSpec((s, 2 * LANE), lambda h, i: (0, h)),
        ],
        out_specs=[
            pl.BlockSpec((t, LANE), lambda h, i: (i, h)),
            pl.BlockSpec((t, 2 * LANE), lambda h, i: (i, h)),
        ],
        out_shape=[jax.ShapeDtypeStruct((s, 512), F32), jax.ShapeDtypeStruct((s, 1024), BF16)],
        compiler_params=_params(("arbitrary", "arbitrary")),
        name="attn_fwd",
    )(qr, kr, vp)


def _attn_bwd(qa, kr, vp, dop, sends):
    s = qa.shape[0]
    t = ATT_T
    nq = s // t
    ns = len(sends)

    def body(q_ref, k_ref, v_ref, do_ref, *rest):
        send_refs, (dq_ref, dk_ref, dv_ref) = rest[:ns], rest[ns:ns + 3]
        recv_refs, sems = rest[ns + 3:2 * ns + 3], rest[2 * ns + 3:]
        j = pl.program_id(1)
        sls = [slice(LANE * a, LANE * (a + 1)) for a in range(2)]

        @pl.when((pl.program_id(0) == 0) & (j == 0))
        def _():
            _start_all(*_to_all_copies(send_refs, recv_refs, sems, False))

        @pl.when(j == 0)
        def _():
            dq_ref[...] = jnp.zeros_like(dq_ref)

        dk_ref[...] = jnp.zeros_like(dk_ref)
        dv_ref[...] = jnp.zeros_like(dv_ref)
        ks = [k_ref[:, sl] for sl in sls]
        vs = [v_ref[:, sl] for sl in sls]

        def step(i, masked):
            rows = pl.ds(pl.multiple_of(i * t, t), t)
            for a in range(2):
                q = q_ref[rows, sls[a]]
                do = do_ref[rows, sls[a]]
                sc = _dotg(ks[a], q, NT)
                if masked:
                    sc = jnp.where(_chunk_mask(True), sc, -1e30)
                p = jnp.exp(sc)
                ds = (p * _dotg(vs[a], do, NT)).astype(BF16)
                dv_ref[:, sls[a]] += _dot(p.astype(BF16), do)
                dk_ref[:, sls[a]] += _dot(ds, q)
                dq_ref[rows, sls[a]] += _dotg(ds, ks[a], TN)

        step(j, True)

        def loop(i, c):
            step(i, False)
            return c

        lax.fori_loop(j + 1, nq, loop, 0)

        @pl.when((pl.program_id(0) == HEADS // 2 - 1) & (j == nq - 1))
        def _():
            _wait_all(*_to_all_copies(send_refs, recv_refs, sems, False))

    blk = pl.BlockSpec((t, 2 * LANE), lambda h, j: (j, h))
    whole = pl.BlockSpec((s, 2 * LANE), lambda h, j: (0, h))
    out = jax.ShapeDtypeStruct((s, 1024), F32)
    return pl.pallas_call(
        body,
        grid=(HEADS // 2, nq),
        in_specs=[whole, blk, blk, whole] + [ANY] * ns,
        out_specs=[whole, blk, blk] + [ANY] * ns,
        out_shape=[out, out, out] + [jax.ShapeDtypeStruct(a.shape, a.dtype) for a in sends],
        scratch_shapes=_copy_sems(ns, 7),
        compiler_params=_params(("arbitrary", "arbitrary")),
        name="attn_bwd",
    )(qa, kr, vp, dop, *sends)


HG_T = 256
HG_NC = HG_T // HG_BLOCK


def _hg_consts():
    r = jnp.arange(HG_T)[:, None]
    c = jnp.arange(HG_T)[None, :]
    same = (r // HG_BLOCK) == (c // HG_BLOCK)
    mcum = (same & (c <= r)).astype(BF16)
    mrev = (same & (c >= r)).astype(BF16)
    msum = same.astype(BF16)
    a = jnp.arange(LANE)
    bd = ((a[:, None] < 64) == (a[None, :] < 64)).astype(F32)
    return mcum, mrev, msum, bd


def _hg_pre(hq, hf, lbl, mcum, msum):
    lb = _sigmoid(lbl[0:1, :] - lbl[1:2, :])
    sig = _sigmoid(hf)
    f = lb + (1.0 - lb) * sig
    lf = jnp.log(f)
    b = _sel_left(mcum, lf)
    big_l = _sel_left(msum, lf)
    k = 1.0 - f
    qd = hq * jnp.exp(b)
    ki = k * jnp.exp(-b)
    ke = k * jnp.exp(big_l - b)
    return lb, sig, f, b, big_l, qd, ki, ke


def _stack_pair(xp, lo):
    return jnp.concatenate([jnp.where(lo, xp, 0.0), jnp.where(lo, 0.0, xp)], axis=0)


def _hgrn_fwd(proj, lbl):
    s = proj.shape[0]
    t = HG_T
    mcum, _, msum, bd = _hg_consts()

    def body(hq_ref, hf_ref, hi_ref, lbl_ref, mcum_ref, msum_ref, bd_ref, o_ref, sp_ref, st_ref):
        @pl.when(pl.program_id(0) == 0)
        def _():
            st_ref[...] = jnp.zeros_like(st_ref)

        mc = mcum_ref[...]
        _, _, _, _, big_l, qd, ki, ke = _hg_pre(hq_ref[...], hf_ref[...], lbl_ref[...], mc, msum_ref[...])
        el = jnp.exp(big_l)
        hi = hi_ref[...]
        lo = lax.broadcasted_iota(jnp.int32, (t, LANE), 1) < 64
        mask2 = jnp.concatenate([mc, mc], axis=0) > 0.5
        for p in range(HEADS // 2):
            sl = slice(LANE * p, LANE * (p + 1))
            vp = hi[:, sl].astype(BF16)
            q2 = _stack_pair(qd[:, sl], lo).astype(BF16)
            a2 = jnp.where(mask2, _dotg(q2, ki[:, sl].astype(BF16), NT), 0.0)
            r2 = _dot(a2.astype(BF16), vp)
            o_intra = jnp.where(lo, r2[:t], r2[t:])
            qb = qd[:, sl].astype(BF16)
            kb = ke[:, sl].astype(BF16)
            st = st_ref[p]
            for c in range(HG_NC):
                rows = slice(HG_BLOCK * c, HG_BLOCK * (c + 1))
                sp_ref[c, :, sl] = st
                o_ref[rows, sl] = o_intra[rows] + _dotg(qb[rows], st.astype(BF16), NT)
                u = _dotg(vp[rows], kb[rows], TN) * bd_ref[...]
                st = st * el[HG_BLOCK * c:HG_BLOCK * c + 1, sl] + u
            st_ref[p] = st

    row = lambda j: pl.BlockSpec((t, HG_WIDTH), lambda i: (i, j))
    full = lambda a: pl.BlockSpec(a.shape, lambda i: (0, 0))
    return pl.pallas_call(
        body,
        grid=(s // t,),
        in_specs=[row(6), row(7), row(8), full(lbl), full(mcum), full(msum), full(bd)],
        out_specs=[row(0), pl.BlockSpec((HG_NC, LANE, HG_WIDTH), lambda i: (i, 0, 0))],
        out_shape=[jax.ShapeDtypeStruct((s, HG_WIDTH), F32),
                   jax.ShapeDtypeStruct((s // HG_BLOCK, LANE, HG_WIDTH), F32)],
        scratch_shapes=[pltpu.VMEM((HEADS // 2, LANE, LANE), F32)],
        compiler_params=_params(("arbitrary",)),
        name="hgrn_fwd",
    )(proj, proj, proj, lbl, mcum, msum, bd)


def _hgrn_bwd(proj, lbl, do, sprev, dproj):
    s = proj.shape[0]
    t = HG_T
    nt = s // t
    mcum, mrev, msum, bd = _hg_consts()

    def body(hq_ref, hf_ref, hi_ref, lbl_ref, do_ref, sp_ref, mcum_ref, mrev_ref, msum_ref, bd_ref,
             dproj_in, dh_ref, dlbl_ref, g_ref):
        del dproj_in

        @pl.when(pl.program_id(0) == 0)
        def _():
            g_ref[...] = jnp.zeros_like(g_ref)
            dlbl_ref[...] = jnp.zeros_like(dlbl_ref)

        mc = mcum_ref[...]
        lb, sig, f, b, big_l, qd, ki, ke = _hg_pre(hq_ref[...], hf_ref[...], lbl_ref[...], mc, msum_ref[...])
        el = jnp.exp(big_l)
        hi = hi_ref[...]
        dov = do_ref[...]
        lo = lax.broadcasted_iota(jnp.int32, (t, LANE), 1) < 64
        mask2 = jnp.concatenate([mc, mc], axis=0) > 0.5
        dqd_parts, dke_parts, dv_parts, del_parts, dki_parts = [], [], [], [], []
        for p in range(HEADS // 2):
            sl = slice(LANE * p, LANE * (p + 1))
            vp = hi[:, sl].astype(BF16)
            q2 = _stack_pair(qd[:, sl], lo).astype(BF16)
            kip = ki[:, sl].astype(BF16)
            do2 = _stack_pair(dov[:, sl], lo).astype(BF16)
            a2 = jnp.where(mask2, _dotg(q2, kip, NT), 0.0).astype(BF16)
            da2 = jnp.where(mask2, _dotg(do2, vp, NT), 0.0).astype(BF16)
            r2 = _dot(da2, kip)
            dki_parts.append(_dotg(da2, q2, TN))
            qb = qd[:, sl].astype(BF16)
            kb = ke[:, sl].astype(BF16)
            dob = dov[:, sl].astype(BF16)
            g = g_ref[p]
            dqd_c, dv_c, dke_c, del_c = [], [], [], []
            for c in range(HG_NC - 1, -1, -1):
                rows = slice(HG_BLOCK * c, HG_BLOCK * (c + 1))
                gb = g.astype(BF16)
                st = sp_ref[c, :, sl]
                dqd_c.append(_dot(dob[rows], st.astype(BF16)))
                dv_c.append(_dotg(kb[rows], gb, NT))
                dke_c.append(_dot(vp[rows], gb))
                del_c.append(jnp.broadcast_to(jnp.sum(g * st, axis=0, keepdims=True), (HG_BLOCK, LANE)))
                g = g * el[HG_BLOCK * c:HG_BLOCK * c + 1, sl] + _dotg(dob[rows], qb[rows], TN) * bd_ref[...]
            g_ref[p] = g
            up = lambda parts: jnp.concatenate(parts[::-1], axis=0)
            dqd_parts.append(jnp.where(lo, r2[:t], r2[t:]) + up(dqd_c))
            dv_parts.append(_dotg(a2, do2, TN) + up(dv_c))
            dke_parts.append(up(dke_c))
            del_parts.append(up(del_c))
        wide = lambda parts: jnp.concatenate(parts, axis=1)
        dqd, dke, dki, dvv, del_rows = wide(dqd_parts), wide(dke_parts), wide(dki_parts), wide(dv_parts), wide(del_parts)
        dh_ref[:, :HG_WIDTH] = (dqd * jnp.exp(b)).astype(BF16)
        dh_ref[:, 2 * HG_WIDTH:] = dvv.astype(BF16)
        dke_ke = dke * ke
        db = dqd * qd - dki * ki - dke_ke
        dl_rows = _sel_left(msum_ref[...], dke_ke) + del_rows * el
        is_last = (lax.broadcasted_iota(jnp.int32, (t, HG_WIDTH), 0) & (HG_BLOCK - 1)) == HG_BLOCK - 1
        db = db + jnp.where(is_last, dl_rows, 0.0)
        dlf = _sel_left(mrev_ref[...], db)
        dk = dki * jnp.exp(-b) + dke * jnp.exp(big_l - b)
        df = dlf / f - dk
        dh_ref[:, HG_WIDTH:2 * HG_WIDTH] = (df * (1.0 - lb) * sig * (1.0 - sig)).astype(BF16)
        dlb = jnp.sum(df * (1.0 - sig), axis=0, keepdims=True) * lb * (1.0 - lb)
        dlbl_ref[0:1, :] += dlb
        dlbl_ref[1:2, :] -= dlb

    rrow = lambda j: pl.BlockSpec((t, HG_WIDTH), lambda i: (nt - 1 - i, j))
    full = lambda a: pl.BlockSpec(a.shape, lambda i: (0, 0))
    return pl.pallas_call(
        body,
        grid=(nt,),
        in_specs=[rrow(6), rrow(7), rrow(8), full(lbl), rrow(0),
                  pl.BlockSpec((HG_NC, LANE, HG_WIDTH), lambda i: (nt - 1 - i, 0, 0)),
                  full(mcum), full(mrev), full(msum), full(bd), pl.BlockSpec(memory_space=pl.ANY)],
        out_specs=[pl.BlockSpec((t, 3 * HG_WIDTH), lambda i: (nt - 1 - i, 2)),
                   pl.BlockSpec((2, HG_WIDTH), lambda i: (0, 0))],
        out_shape=[jax.ShapeDtypeStruct(dproj.shape, BF16), jax.ShapeDtypeStruct((2, HG_WIDTH), F32)],
        input_output_aliases={10: 0},
        scratch_shapes=[pltpu.VMEM((HEADS // 2, LANE, LANE), F32)],
        compiler_params=_params(("arbitrary",)),
        name="hgrn_bwd",
    )(proj, proj, proj, lbl, do, sprev, mcum, mrev, msum, bd, dproj)


def _tail(x, tgt, proj, attn, o, w_a, w_b, w_out, w_at, w_bt, w_outt, b_gate, g_post, gh):
    s = x.shape[0]
    tm = 128
    ones64 = (jnp.arange(HG_WIDTH)[:, None] // 64 == jnp.arange(HG_WIDTH)[None, :] // 64).astype(BF16)

    def body(x_ref, t_ref, ml_ref, ga_ref, gb_ref, at_ref, o_ref, wa_ref, wb_ref, wo_ref, wat_ref, wbt_ref, wot_ref,
             bg_ref, gp_ref, gh_ref, ones_ref,
             dout_ref, dpj_ref, dop_ref, do_ref, mt_ref, dy_ref, yat_ref, dya_ref, ybt_ref, dyb_ref,
             loss_ref, dgp_ref, dbg_ref, dgh_ref):
        @pl.when(pl.program_id(0) == 0)
        def _():
            loss_ref[...] = jnp.zeros_like(loss_ref)
            dgp_ref[...] = jnp.zeros_like(dgp_ref)
            dbg_ref[...] = jnp.zeros_like(dbg_ref)
            dgh_ref[...] = jnp.zeros_like(dgh_ref)

        ones = ones_ref[...]
        gate_a = ga_ref[...]
        sa = _sigmoid(gate_a)
        silu_a = gate_a * sa
        attn_v = at_ref[...]
        ya_in = attn_v * silu_a
        ov = o_ref[...]
        ro = lax.rsqrt(_sel_right(ov * ov, ones) * (1.0 / 64.0) + EPS)
        ohat = ov * ro
        ghv = gh_ref[...]
        on = ohat * ghv
        gate_b = gb_ref[...]
        sb = _sigmoid(gate_b)
        silu_b = gate_b * sb
        yb_in = on * silu_b
        ya_bf = ya_in.astype(BF16)
        yb_bf = yb_in.astype(BF16)
        yat_ref[...] = ya_bf.T
        ybt_ref[...] = yb_bf.T
        y_a = _dot(ya_bf, wa_ref[...])
        y_b = _dot(yb_bf, wb_ref[...])
        gts = _sigmoid(ml_ref[...] + bg_ref[...])
        g_a = gts[:, :D_MODEL]
        g_b = gts[:, D_MODEL:]
        m_bf = (g_a * y_a + g_b * y_b).astype(BF16)
        mt_ref[...] = m_bf.T
        y = _dot(m_bf, wo_ref[...])
        r1 = lax.rsqrt(jnp.mean(y * y, axis=-1, keepdims=True) + EPS)
        yn = y * r1
        gp = gp_ref[...]
        e = x_ref[...] + yn * gp - t_ref[...]
        loss_ref[...] += jnp.sum(e * e, axis=0, keepdims=True)
        dout = e * (1.0 / D_MODEL)
        dout_ref[...] = dout
        dgp_ref[...] += jnp.sum(dout * yn, axis=0, keepdims=True)
        dyn = dout * gp
        dy = r1 * (dyn - yn * jnp.mean(dyn * yn, axis=-1, keepdims=True))
        dy_bf = dy.astype(BF16)
        dy_ref[...] = dy_bf
        dm = _dot(dy_bf, wot_ref[...])
        dml_a = dm * y_a * g_a * (1.0 - g_a)
        dml_b = dm * y_b * g_b * (1.0 - g_b)
        dpj_ref[:, :D_MODEL] = dml_a.astype(BF16)
        dpj_ref[:, D_MODEL:2 * D_MODEL] = dml_b.astype(BF16)
        dbg_ref[:, :D_MODEL] += jnp.sum(dml_a, axis=0, keepdims=True)
        dbg_ref[:, D_MODEL:] += jnp.sum(dml_b, axis=0, keepdims=True)
        dya_bf = (dm * g_a).astype(BF16)
        dyb_bf = (dm * g_b).astype(BF16)
        dya_ref[...] = dya_bf
        dyb_ref[...] = dyb_bf
        dya_in = _dot(dya_bf, wat_ref[...])
        dyb_in = _dot(dyb_bf, wbt_ref[...])
        dattn = dya_in * silu_a
        delta = _sel_right(dattn * attn_v, ones)
        lane = lax.broadcasted_iota(jnp.int32, (tm, LANE), 1)
        for p in range(HEADS // 2):
            sl = slice(LANE * p, LANE * (p + 1))
            xs = (dattn[:, sl], pltpu.roll(dattn[:, sl], VDIM, 1))
            nds = (-pltpu.roll(delta[:, sl], VDIM, 1), -delta[:, sl])
            for a in range(2):
                hi, lo_part = _hi_lo(nds[a])
                blk = jnp.where(lane < VDIM, xs[a], jnp.where(lane == VDIM, hi, jnp.where(lane == VDIM + 1, lo_part, 0.0)))
                dop_ref[:, LANE * (2 * p + a):LANE * (2 * p + a + 1)] = blk.astype(BF16)
        dpj_ref[:, 2 * D_MODEL:2 * D_MODEL + HG_WIDTH] = (
            dya_in * attn_v * (sa * (1.0 + gate_a * (1.0 - sa)))).astype(BF16)
        don = dyb_in * silu_b
        dpj_ref[:, 2 * D_MODEL + HG_WIDTH:] = (dyb_in * on * (sb * (1.0 + gate_b * (1.0 - sb)))).astype(BF16)
        dgh_ref[...] += jnp.sum(don * ohat, axis=0, keepdims=True)
        dohat = don * ghv
        do_ref[...] = ro * (dohat - ohat * (_sel_right(dohat * ohat, ones) * (1.0 / 64.0)))

    row = lambda w, j: pl.BlockSpec((tm, w), lambda i: (i, j))
    col = lambda w: pl.BlockSpec((w, tm), lambda i: (0, i))
    full = lambda a: pl.BlockSpec(a.shape, lambda i: (0, 0))
    acc = lambda w: pl.BlockSpec((1, w), lambda i: (0, 0))
    sds = lambda w, dt: jax.ShapeDtypeStruct((s, w), dt)
    sdt = lambda w: jax.ShapeDtypeStruct((w, s), BF16)
    return pl.pallas_call(
        body,
        grid=(s // tm,),
        in_specs=[row(1024, 0), row(1024, 0), row(2048, 0), row(512, 4), row(512, 5), row(512, 0), row(512, 0),
                  full(w_a), full(w_b), full(w_out), full(w_at), full(w_bt), full(w_outt),
                  full(b_gate), full(g_post), full(gh), full(ones64)],
        out_specs=[row(1024, 0), row(3072, 0), row(1024, 0), row(512, 0),
                   col(1024), row(1024, 0), col(512), row(1024, 0), col(512), row(1024, 0),
                   acc(1024), acc(1024), acc(2048), acc(512)],
        out_shape=[sds(1024, F32), sds(D_IN_PAD, BF16), sds(1024, BF16), sds(512, F32),
                   sdt(1024), sds(1024, BF16), sdt(512), sds(1024, BF16), sdt(512), sds(1024, BF16),
                   jax.ShapeDtypeStruct((1, 1024), F32), jax.ShapeDtypeStruct((1, 1024), F32),
                   jax.ShapeDtypeStruct((1, 2048), F32), jax.ShapeDtypeStruct((1, 512), F32)],
        compiler_params=_params(("arbitrary",), 56),
        name="tail",
    )(x, tgt, proj, proj, proj, attn, o, w_a, w_b, w_out, w_at, w_bt, w_outt, b_gate, g_post, gh, ones64)


def _mla_bwd(proj, dqr, dkr, dv, g_q, g_kv, w_uq_pt, w_kv_pt, rc, rs1, rs2, dproj):
    s = proj.shape[0]
    tm = 256
    scale = 1.0 / math.sqrt(QK)

    def body(cq_ref, ckv_ref, dqr_ref, dkr_ref, dv_ref, gq_ref, gkv_ref, wuqt_ref, wkvt_ref, c_ref, s1_ref, s2_ref,
             dproj_in, dqf_ref, dkvf_ref, dc_ref, dgq_ref, dgkv_ref):
        del dproj_in

        @pl.when(pl.program_id(0) == 0)
        def _():
            dgq_ref[...] = jnp.zeros_like(dgq_ref)
            dgkv_ref[...] = jnp.zeros_like(dgkv_ref)

        c, s1, s2 = c_ref[...], s1_ref[...], s2_ref[...]
        lane = lax.broadcasted_iota(jnp.int32, (tm, LANE), 1)
        ksum = jnp.zeros((tm, LANE), F32)
        for h in range(HEADS):
            sl = slice(LANE * h, LANE * (h + 1))
            dqf_ref[:, sl] = (_unrope(dqr_ref[:, sl], c, s1, s2) * scale).astype(BF16)
            dkh = dkr_ref[:, sl]
            ksum = ksum + dkh
            dkvf_ref[:, sl] = jnp.where(lane < NOPE, dkh, 0.0).astype(BF16)
            dkvf_ref[:, HEADS * LANE + LANE * h:HEADS * LANE + LANE * (h + 1)] = jnp.where(
                lane < VDIM, dv_ref[:, sl], 0.0).astype(BF16)
        dkpe = _unrope(ksum, c, s1, s2)
        dc_ref[:, Q_LORA + KV_LORA:] = jnp.where((lane >= NOPE) & (lane < QK), dkpe, 0.0).astype(BF16)
        dcqn = _dot(dqf_ref[...], wuqt_ref[...])
        dckvn = _dot(dkvf_ref[...], wkvt_ref[...])
        for x_ref, g_ref, dn, cols, dg_ref in ((cq_ref, gq_ref, dcqn, slice(0, Q_LORA), dgq_ref),
                                               (ckv_ref, gkv_ref, dckvn, slice(Q_LORA, Q_LORA + KV_LORA), dgkv_ref)):
            xv = x_ref[...]
            r = lax.rsqrt(jnp.mean(xv * xv, axis=-1, keepdims=True) + EPS)
            xh = xv * r
            dg_ref[...] += jnp.sum(dn * xh, axis=0, keepdims=True)
            dh = dn * g_ref[...]
            dc_ref[:, cols] = (r * (dh - xh * jnp.mean(dh * xh, axis=-1, keepdims=True))).astype(BF16)

    row = lambda w, j: pl.BlockSpec((tm, w), lambda i: (i, j))
    full = lambda a: pl.BlockSpec(a.shape, lambda i: (0, 0))
    acc = lambda w: pl.BlockSpec((1, w), lambda i: (0, 0))
    sds = lambda w, dt: jax.ShapeDtypeStruct((s, w), dt)
    return pl.pallas_call(
        body,
        grid=(s // tm,),
        in_specs=[row(768, 6), row(256, 21), row(1024, 0), row(1024, 0), row(1024, 0), full(g_q), full(g_kv),
                  full(w_uq_pt), full(w_kv_pt), row(128, 0), row(128, 0), row(128, 0),
                  pl.BlockSpec(memory_space=pl.ANY)],
        out_specs=[row(1024, 0), row(2048, 0), row(1152, 4), acc(768), acc(256)],
        out_shape=[sds(1024, BF16), sds(2048, BF16), jax.ShapeDtypeStruct(dproj.shape, BF16),
                   jax.ShapeDtypeStruct((1, 768), F32), jax.ShapeDtypeStruct((1, 256), F32)],
        input_output_aliases={12: 2},
        compiler_params=_params(("arbitrary",)),
        name="mla_bwd",
    )(proj, proj, dqr, dkr, dv, g_q, g_kv, w_uq_pt, w_kv_pt, rc, rs1, rs2, dproj)


def _pick(n, options):
    for o in options:
        if n % o == 0:
            return o
    raise ValueError(n)


def _matmul(a, b, name):
    m, k = a.shape
    n = b.shape[1]
    tm = _pick(m, (1024, 768, 512, 256))
    tn = _pick(n, (1152, 1024, 768, 512))
    tk = _pick(k, (1024, 512))
    nk = k // tk

    def body(a_ref, b_ref, o_ref):
        @pl.when(pl.program_id(2) == 0)
        def _():
            o_ref[...] = jnp.zeros_like(o_ref)

        o_ref[...] += _dot(a_ref[...], b_ref[...])

    return pl.pallas_call(
        body,
        grid=(m // tm, n // tn, nk),
        in_specs=[pl.BlockSpec((tm, tk), lambda i, j, l: (i, l)), pl.BlockSpec((tk, tn), lambda i, j, l: (l, j))],
        out_specs=pl.BlockSpec((tm, tn), lambda i, j, l: (i, j)),
        out_shape=jax.ShapeDtypeStruct((m, n), F32),
        compiler_params=_params(("arbitrary", "arbitrary", "arbitrary")),
        name=name,
    )(a, b)


def _dh_dx(dproj, w_in_pt, x, dout, g_pre, sends):
    s, k = dproj.shape
    tm = 256
    ns, ni = len(sends), s // tm

    def body(dp_ref, w_ref, x_ref, dout_ref, g_ref, *rest):
        send_refs, (dx_ref, dg_ref) = rest[:ns], rest[ns:ns + 2]
        recv_refs, sems = rest[ns + 2:2 * ns + 2], rest[2 * ns + 2:]

        @pl.when(pl.program_id(0) == 0)
        def _():
            _start_all(*_to_chips_copies(send_refs, recv_refs, sems))
            dg_ref[...] = jnp.zeros_like(dg_ref)

        dh = _dot(dp_ref[...], w_ref[...])
        xv = x_ref[...]
        r = lax.rsqrt(jnp.mean(xv * xv, axis=-1, keepdims=True) + EPS)
        xh = xv * r
        dg_ref[...] += jnp.sum(dh * xh, axis=0, keepdims=True)
        dxh = dh * g_ref[...]
        dx_ref[...] = dout_ref[...] + r * (dxh - xh * jnp.mean(dxh * xh, axis=-1, keepdims=True))

        @pl.when(pl.program_id(0) == ni - 1)
        def _():
            _wait_all(*_to_chips_copies(send_refs, recv_refs, sems))

    row = lambda w: pl.BlockSpec((tm, w), lambda i: (i, 0))
    return pl.pallas_call(
        body,
        grid=(ni,),
        in_specs=[row(k), pl.BlockSpec((k, D_MODEL), lambda i: (0, 0)), row(D_MODEL), row(D_MODEL),
                  pl.BlockSpec((1, D_MODEL), lambda i: (0, 0))] + [ANY] * ns,
        out_specs=[row(D_MODEL), pl.BlockSpec((1, D_MODEL), lambda i: (0, 0))] + [ANY] * ns,
        out_shape=[jax.ShapeDtypeStruct((s, D_MODEL), F32), jax.ShapeDtypeStruct((1, D_MODEL), F32)]
        + [jax.ShapeDtypeStruct(a.shape, a.dtype) for a in sends],
        scratch_shapes=_copy_sems(ns, 3),
        compiler_params=_params(("arbitrary",)),
        name="dh_dx",
    )(dproj, w_in_pt, x, dout, g_pre, *sends)


def _pair_reduce(slots):
    n = len(slots)
    half = [(N_DEV // 2,) + a.shape[1:] for a in slots]

    def body(*refs):
        s_refs, o_refs = refs[:n], refs[n:2 * n]
        mine, got = refs[2 * n:3 * n], refs[3 * n:4 * n]
        send_sems, recv_sems, local_sems = refs[4 * n:]
        x, y, c = _my_place()
        copies, loads = [], []
        for a in range(n):
            for q in range(N_DEV // 2):
                copies.append(pltpu.make_async_remote_copy(
                    src_ref=s_refs[a].at[2 * q + 1 - c], dst_ref=got[a].at[q],
                    send_sem=send_sems.at[4 * a + q], recv_sem=recv_sems.at[4 * a + q],
                    device_id=(x, y, 1 - c), device_id_type=MESH_ID))
                loads.append(pltpu.make_async_copy(s_refs[a].at[2 * q + c], mine[a].at[q], local_sems.at[4 * a + q]))
        _start_all(loads, copies)
        _wait_all(loads, copies)
        for a in range(n):
            o_refs[a][...] = (mine[a][...].astype(F32) + got[a][...].astype(F32)).astype(o_refs[a].dtype)

    vm = lambda: [pltpu.VMEM(h, a.dtype) for h, a in zip(half, slots)]
    return pl.pallas_call(
        body,
        in_specs=[ANY] * n,
        out_shape=[jax.ShapeDtypeStruct(h, a.dtype) for h, a in zip(half, slots)],
        scratch_shapes=vm() + vm() + [pltpu.SemaphoreType.DMA((4 * n,)), pltpu.SemaphoreType.DMA((4 * n,)),
                                      pltpu.SemaphoreType.DMA((4 * n,))],
        compiler_params=pltpu.CompilerParams(vmem_limit_bytes=48 * 2**20),
        name="pair_reduce",
    )(*slots)


def _rope_tables(s):
    inv = (np.float32(ROPE_THETA) ** (-np.arange(0, ROPE, 2, dtype=np.float32) / np.float32(ROPE))).astype(np.float32)
    ang = (np.arange(s, dtype=np.float32)[:, None] * inv[None, :]).astype(np.float32)
    cos, sin = jnp.asarray(np.cos(ang.astype(np.float64)), F32), jnp.asarray(np.sin(ang.astype(np.float64)), F32)
    z = lambda w: jnp.zeros((s, w), F32)
    rc = jnp.concatenate([jnp.ones((s, NOPE), F32), cos, cos, z(32)], axis=1)
    rs1 = jnp.concatenate([z(NOPE), -sin, z(16), z(32)], axis=1)
    rs2 = jnp.concatenate([z(NOPE), z(16), sin, z(32)], axis=1)
    return rc, rs1, rs2


def _step(x, tgt, w_in_slots, shards, g_pre, b_gate, g_q, g_kv, lbl, g_hgrn, g_post):
    s = x.shape[0]
    w_in_p, w_in_pt = _assemble_w_in(w_in_slots)
    rc, rs1, rs2 = _rope_tables(s)
    gh = jnp.tile(g_hgrn, (1, HEADS))

    proj, ht, *got = _norm_proj(x, g_pre, w_in_p, shards)
    w_uq, w_ukv, w_a, w_b, w_out = (_from_slots(n, g) for n, g in zip(MATS, got))
    w_uq_p = jnp.pad(w_uq.reshape(Q_LORA, HEADS, QK), ((0, 0), (0, 0), (0, LANE - QK))).reshape(Q_LORA, HEADS * LANE)
    kv3 = w_ukv.reshape(KV_LORA, HEADS, NOPE + VDIM)
    pad64 = lambda t: jnp.pad(t, ((0, 0), (0, 0), (0, LANE - 64))).reshape(KV_LORA, HEADS * LANE)
    w_kv_p = jnp.concatenate([pad64(kv3[:, :, :NOPE]), pad64(kv3[:, :, NOPE:])], axis=1)

    qr, kr, v, cqt, ckvt = _mla_prep(proj, g_q, g_kv, w_uq_p, w_kv_p, rc, rs1, rs2)
    attn, qa = _attn_fwd(qr, kr, v)
    o, sprev = _hgrn_fwd(proj, lbl)
    (dout, dproj, dop, do, mt, dy_bf, yat, dya_bf, ybt, dyb_bf,
     loss_vec, dg_post, db_gate, dgh) = _tail(x, tgt, proj, attn, o, w_a, w_b, w_out, w_a.T, w_b.T, w_out.T,
                                               b_gate, g_post, gh)
    early = [_to_slots(n, _matmul(a, b, "d" + n)).astype(BF16)
             for n, a, b in (("w_branch_a", yat, dya_bf), ("w_branch_b", ybt, dyb_bf), ("w_out", mt, dy_bf))]
    dqr, dkr, dv, *early_recv = _attn_bwd(qa, kr, v, dop, early)
    dproj, dlbl = _hgrn_bwd(proj, lbl, do, sprev, dproj)
    dqf, dkvf, dproj, dg_q, dg_kv = _mla_bwd(proj, dqr, dkr, dv, g_q, g_kv, w_uq_p.T, w_kv_p.T, rc, rs1, rs2, dproj)

    dw_in_slots = _scatter_w_in(_matmul(ht, dproj, "dw_in"))
    dw_uq_p = _matmul(cqt, dqf, "dw_uq")
    dw_kv_p = _matmul(ckvt, dkvf, "dw_kv")
    dw_uq = dw_uq_p.reshape(Q_LORA, HEADS, LANE)[:, :, :QK].reshape(Q_LORA, HEADS * QK)
    dw_ukv = jnp.concatenate([dw_kv_p[:, :HEADS * LANE].reshape(KV_LORA, HEADS, LANE)[:, :, :NOPE],
                              dw_kv_p[:, HEADS * LANE:].reshape(KV_LORA, HEADS, LANE)[:, :, :VDIM]],
                             axis=2).reshape(KV_LORA, 1024)
    late = _pair_reduce([dw_in_slots, _to_slots("w_uq", dw_uq).astype(BF16), _to_slots("w_ukv", dw_ukv).astype(BF16)])
    dx, dg_pre, *late_recv = _dh_dx(dproj, w_in_pt, x, dout, g_pre, late)

    loss = 0.5 / D_MODEL * jnp.sum(loss_vec)
    vecs = dict(g_pre=dg_pre, b_gate=db_gate, g_q=dg_q, g_kv=dg_kv, lb_logits=dlbl,
                g_hgrn=jnp.sum(dgh.reshape(HEADS, VDIM), axis=0, keepdims=True), g_post=dg_post)
    small = _pack_small(vecs).at[LOSS_AT].set(loss)
    return dx, late_recv[0], dict(zip(MATS, late_recv[1:] + early_recv)), small


def _all_gather(blocks):
    n = len(blocks)

    def body(*refs):
        x_refs, out_refs = refs[:n], refs[n:2 * n]
        send_sems, recv_sems, local_sems = refs[2 * n:]
        x, y, c = _my_place()
        me, sibling = (x, y, c), (x, y, 1 - c)
        chips = [(1 - x, y), (x, 1 - y), (1 - x, 1 - y)]

        def slot(a, px, py, pc):
            return out_refs[a].at[4 * px + 2 * py + pc]

        def copy(a, k, blk, to, src=None):
            return pltpu.make_async_remote_copy(
                src_ref=slot(a, *blk) if src is None else src, dst_ref=slot(a, *blk),
                send_sem=send_sems.at[7 * a + k], recv_sem=recv_sems.at[7 * a + k],
                device_id=to, device_id_type=MESH_ID)

        mine = [pltpu.make_async_copy(x_refs[a], slot(a, *me), local_sems.at[a]) for a in range(n)]
        for cp in mine:
            cp.start()
        first = [copy(a, 0, me, sibling, src=x_refs[a]) for a in range(n)]
        first += [copy(a, 1 + j, me, (*chip, c), src=x_refs[a]) for a in range(n) for j, chip in enumerate(chips)]
        for cp in first:
            cp.start()
        passed = []
        for j, chip in enumerate(chips):
            for a in range(n):
                copy(a, 1 + j, (*chip, c), me).wait_recv()
                passed.append(copy(a, 4 + j, (*chip, c), sibling))
                passed[-1].start()
        for a in range(n):
            copy(a, 0, sibling, me).wait_recv()
        for j, chip in enumerate(chips):
            for a in range(n):
                copy(a, 4 + j, (*chip, 1 - c), me).wait_recv()
        for cp in first + passed:
            cp.wait_send()
        for cp in mine:
            cp.wait()

    return pl.pallas_call(
        body,
        out_shape=[jax.ShapeDtypeStruct((N_DEV,) + b.shape, b.dtype) for b in blocks],
        in_specs=[pl.BlockSpec(memory_space=pl.ANY)] * n,
        out_specs=[pl.BlockSpec(memory_space=pl.ANY)] * n,
        scratch_shapes=[pltpu.SemaphoreType.DMA((7 * n,)), pltpu.SemaphoreType.DMA((7 * n,)),
                        pltpu.SemaphoreType.DMA((n,))],
        name="gather_weights",
    )(*blocks)


def _adamw(g, w, m, v):
    c1 = 1.0 / (1.0 - ADAM_B1 ** ADAM_STEP)
    c2 = 1.0 / (1.0 - ADAM_B2 ** ADAM_STEP)
    nm = ADAM_B1 * m + (1.0 - ADAM_B1) * g
    nv = ADAM_B2 * v + (1.0 - ADAM_B2) * (g * g)
    d = -ADAM_LR * ((nm * c1) / (jnp.sqrt(nv * c2) + ADAM_EPS) + ADAM_WD * w)
    return d, nm, nv


def _sum8(r_ref):
    g = r_ref[0].astype(F32)
    for k in range(1, r_ref.shape[0]):
        g = g + r_ref[k].astype(F32)
    return g


def _sum_adamw_w_in(recv, w, m, v):
    rows, _, cols = w.shape
    tc = 256

    def body(r_ref, w_ref, m_ref, v_ref, g_ref, d_ref, nm_ref, nv_ref):
        g = _sum8(r_ref)
        dense = lambda ref: ref[...].reshape(rows, tc)
        d, nm, nv = _adamw(g, dense(w_ref), dense(m_ref), dense(v_ref))
        for ref, val in ((g_ref, g), (d_ref, d), (nm_ref, nm), (nv_ref, nv)):
            ref[...] = val.reshape(rows, 1, tc)

    blk = pl.BlockSpec((rows, 1, tc), lambda i: (0, 0, i))
    out = jax.ShapeDtypeStruct((rows, 1, cols), F32)
    return pl.pallas_call(
        body,
        grid=(cols // tc,),
        in_specs=[pl.BlockSpec((recv.shape[0], rows, tc), lambda i: (0, 0, i)), blk, blk, blk],
        out_specs=[blk, blk, blk, blk],
        out_shape=[out, out, out, out],
        compiler_params=_params(("arbitrary",)),
        name="sum_adamw_w_in",
    )(recv, w, m, v)


def _sum_adamw_whole(recvs, ws, ms, vs, small):
    n = len(ws)

    def body(*refs):
        r_refs, small_ref = refs[:n - 1], refs[n - 1]
        w_refs, m_refs, v_refs = refs[n:2 * n], refs[2 * n:3 * n], refs[3 * n:4 * n]
        outs = refs[4 * n:8 * n]
        got, send_sems, recv_sems = refs[8 * n:]
        x, y, c = _my_place()
        me = 4 * x + 2 * y + c
        got[me] = small_ref[...]
        copies = []
        for k in range(N_DEV - 1):
            copies.append(pltpu.make_async_remote_copy(
                src_ref=small_ref, dst_ref=got.at[me], send_sem=send_sems.at[k], recv_sem=recv_sems.at[k],
                device_id=_flip(k, x, y, c), device_id_type=MESH_ID))
        _start_all([], copies)
        for a in range(n):
            if a == n - 1:
                _wait_all([], copies)
                g = _sum8(got)
            else:
                g = _sum8(r_refs[a])
            d, nm, nv = _adamw(g, w_refs[a][...], m_refs[a][...], v_refs[a][...])
            outs[a][...] = g
            outs[n + a][...] = d
            outs[2 * n + a][...] = nm
            outs[3 * n + a][...] = nv

    shapes = [jax.ShapeDtypeStruct(w.shape, F32) for w in ws]
    res = pl.pallas_call(
        body,
        out_shape=shapes * 4,
        scratch_shapes=[pltpu.VMEM((N_DEV,) + small.shape, F32), pltpu.SemaphoreType.DMA((7,)),
                        pltpu.SemaphoreType.DMA((7,))],
        compiler_params=pltpu.CompilerParams(vmem_limit_bytes=48 * 2**20),
        name="sum_adamw_small",
    )(*recvs, small, *ws, *ms, *vs)
    return res[:n], res[n:2 * n], res[2 * n:3 * n], res[3 * n:]


MATS = ("w_uq", "w_ukv", "w_branch_a", "w_branch_b", "w_out")
SMALL = ("g_pre", "b_gate", "g_q", "g_kv", "lb_logits", "g_hgrn", "g_post")
SMALL_ROWS = (1, 2, 1, 1, 1, 1, 1)
LOSS_AT = (6, 1023)
SMALL_SHAPE = dict(g_pre=(1, 1024), b_gate=(1, 2048), g_q=(1, 768), g_kv=(1, 256), lb_logits=(2, 512),
                   g_hgrn=(1, 64), g_post=(1, 1024))
COL_SHARDED = dict(w_uq=False, w_ukv=True, w_branch_a=True, w_branch_b=True, w_out=False)
ORDER = ("g_pre", "w_in", "b_gate", "g_q", "w_uq", "g_kv", "w_ukv", "lb_logits", "g_hgrn",
         "w_branch_a", "w_branch_b", "w_out", "g_post")


def _pack_small(t):
    parts = []
    for n, r in zip(SMALL, SMALL_ROWS):
        flat = t[n].reshape(1, -1)
        parts.append(jnp.pad(flat, ((0, 0), (0, r * 1024 - flat.shape[1]))).reshape(r, 1024))
    return jnp.concatenate(parts, axis=0)


def _unpack_small(p):
    out, r0 = {}, 0
    for n, r in zip(SMALL, SMALL_ROWS):
        shp = SMALL_SHAPE[n]
        out[n] = p[r0:r0 + r].reshape(1, -1)[:, :shp[0] * shp[1]].reshape(shp)
        r0 += r
    return out


def _to_slots(name, full):
    r, c = full.shape
    if COL_SHARDED[name]:
        return full.reshape(r, N_DEV, c // N_DEV).transpose(1, 0, 2)
    return full.reshape(N_DEV, r // N_DEV, c)


def _from_slots(name, slots):
    _, r, c = slots.shape
    if COL_SHARDED[name]:
        return slots.transpose(1, 0, 2).reshape(r, N_DEV * c)
    return slots.reshape(N_DEV * r, c)


def kernel(x, g_pre, w_in, b_gate, g_q, w_uq, g_kv, w_ukv, lb_logits, g_hgrn, w_branch_a, w_branch_b, w_out, g_post, loss_target, m_g_pre, m_w_in, m_b_gate, m_g_q, m_w_uq, m_g_kv, m_w_ukv, m_lb_logits, m_g_hgrn, m_w_branch_a, m_w_branch_b, m_w_out, m_g_post, v_g_pre, v_w_in, v_b_gate, v_g_q, v_w_uq, v_g_kv, v_w_ukv, v_lb_logits, v_g_hgrn, v_w_branch_a, v_w_branch_b, v_w_out, v_g_post):
    rows3 = lambda a: jnp.transpose(a, (2, 0, 1))
    w = dict(w_in=rows3(w_in), w_uq=w_uq[0], w_ukv=w_ukv[0], w_branch_a=w_branch_a[0], w_branch_b=w_branch_b[0],
             w_out=w_out[0], g_pre=g_pre, b_gate=b_gate, g_q=g_q, g_kv=g_kv, lb_logits=lb_logits, g_hgrn=g_hgrn,
             g_post=g_post)
    mom = dict(w_in=rows3(m_w_in), w_uq=m_w_uq[0], w_ukv=m_w_ukv[0], w_branch_a=m_w_branch_a[0],
               w_branch_b=m_w_branch_b[0], w_out=m_w_out[0], g_pre=m_g_pre, b_gate=m_b_gate, g_q=m_g_q, g_kv=m_g_kv,
               lb_logits=m_lb_logits, g_hgrn=m_g_hgrn, g_post=m_g_post)
    var = dict(w_in=rows3(v_w_in), w_uq=v_w_uq[0], w_ukv=v_w_ukv[0], w_branch_a=v_w_branch_a[0],
               w_branch_b=v_w_branch_b[0], w_out=v_w_out[0], g_pre=v_g_pre, b_gate=v_b_gate, g_q=v_g_q, g_kv=v_g_kv,
               lb_logits=v_lb_logits, g_hgrn=v_g_hgrn, g_post=v_g_post)

    (w_in_slots,) = _all_gather([w["w_in"].reshape(W_IN_SHARD, D_MODEL).astype(BF16)])
    dx, recv_in, recv, small = _step(x[0], loss_target[0], w_in_slots, [w[n].astype(BF16) for n in MATS],
                                     g_pre, b_gate, g_q, g_kv, lb_logits, g_hgrn, g_post)

    g_in, d_in, m_in, v_in = _sum_adamw_w_in(recv_in, w["w_in"], mom["w_in"], var["w_in"])
    names = MATS + ("small",)
    pk = lambda t: [t[n] for n in MATS] + [_pack_small(t)]
    res = _sum_adamw_whole([recv[n] for n in MATS], pk(w), pk(mom), pk(var), small)

    outs = []
    for kind, big in zip(res, (g_in, d_in, m_in, v_in)):
        t = dict(zip(names, kind))
        t = {**{n: t[n][None] for n in MATS}, **_unpack_small(t["small"]),
             "w_in": jnp.transpose(big, (1, 2, 0))}
        outs += [t[n] for n in ORDER]
    total = res[0][-1][LOSS_AT]
    return (total, dx[None], *outs)
```

```python
import math

import jax
import jax.numpy as jnp
import numpy as np
from jax import lax
from jax.experimental import pallas as pl
from jax.experimental.pallas import tpu as pltpu

F32, BF16 = jnp.float32, jnp.bfloat16

D_MODEL = 1024
EPS = 1e-6
HEADS = 8
NOPE, ROPE, VDIM = 64, 32, 64
QK = NOPE + ROPE
Q_LORA, KV_LORA = 768, 256
ROPE_THETA = 10000.0
ATT_CHUNK_SHIFT = 6
HG_BLOCK = 32
HG_WIDTH = 512
D_IN = 5664
D_IN_PAD = 5760
W_IN_SHARD = D_IN // 8
N_DEV = 8
LANE = 128

ADAM_LR, ADAM_B1, ADAM_B2, ADAM_EPS, ADAM_WD, ADAM_STEP = 0.001, 0.9, 0.999, 1e-08, 0.01, 10

W_IN_SEGMENTS = ((3616, 5664, 0), (1056, 1568, 2048), (3104, 3616, 2560), (1568, 3104, 3072),
                 (0, 1024, 4608), (1024, 1056, 5696))
W_IN_ZERO = ((5632, 5696), (5728, 5760))

NT = (((1,), (1,)), ((), ()))
TN = (((0,), (0,)), ((), ()))
MESH_ID = pl.DeviceIdType.MESH


def _w_in_pieces():
    out = []
    for lo, hi, dst in W_IN_SEGMENTS:
        c = lo
        while c < hi:
            p = c // W_IN_SHARD
            e = min(hi, (p + 1) * W_IN_SHARD)
            out.append((p, c - p * W_IN_SHARD, e - p * W_IN_SHARD, dst + c - lo))
            c = e
    return out


def _params(sem, vmem_mb=48):
    return pltpu.CompilerParams(dimension_semantics=sem, vmem_limit_bytes=vmem_mb * 2**20)


def _dot(a, b):
    return jnp.dot(a, b, preferred_element_type=F32)


def _dotg(a, b, dims):
    return lax.dot_general(a, b, dims, preferred_element_type=F32)


def _split2(x):
    hi = x.astype(BF16)
    return hi, (x - hi.astype(F32)).astype(BF16)


def _sel_left(m01, x):
    hi, lo = _split2(x)
    return _dot(m01, hi) + _dot(m01, lo)


def _sel_right(x, m01):
    hi, lo = _split2(x)
    return _dot(hi, m01) + _dot(lo, m01)


def _hi_lo(x):
    hi = x.astype(BF16).astype(F32)
    return hi, x - hi


def _sigmoid(x):
    return 0.5 * jnp.tanh(0.5 * x) + 0.5


def _rope(x, c, s1, s2):
    return x * c + pltpu.roll(x, 112, 1) * s1 + pltpu.roll(x, 16, 1) * s2


def _unrope(d, c, s1, s2):
    return d * c + pltpu.roll(d * s1, 16, 1) + pltpu.roll(d * s2, 112, 1)


def _my_place():
    return lax.axis_index("x"), lax.axis_index("y"), lax.axis_index("c")


def _flip(k, x, y, c):
    fx, fy, fc = (k + 1) >> 2 & 1, (k + 1) >> 1 & 1, (k + 1) & 1
    return (1 - x if fx else x), (1 - y if fy else y), (1 - c if fc else c)


def _to_all_copies(s_refs, r_refs, sems, spread):
    send_sems, recv_sems, local_sems = sems
    x, y, c = _my_place()
    me = 4 * x + 2 * y + c
    src = (lambda a, p: s_refs[a]) if spread else (lambda a, p: s_refs[a].at[p])
    local = [pltpu.make_async_copy(src(a, me), r_refs[a].at[me], local_sems.at[a]) for a in range(len(s_refs))]
    remote = []
    for k in range(N_DEV - 1):
        px, py, pc = _flip(k, x, y, c)
        for a in range(len(s_refs)):
            remote.append(pltpu.make_async_remote_copy(
                src_ref=src(a, 4 * px + 2 * py + pc), dst_ref=r_refs[a].at[me],
                send_sem=send_sems.at[7 * a + k], recv_sem=recv_sems.at[7 * a + k],
                device_id=(px, py, pc), device_id_type=MESH_ID))
    return local, remote


def _to_chips_copies(s_refs, r_refs, sems):
    send_sems, recv_sems, local_sems = sems
    x, y, c = _my_place()
    me = 2 * x + y
    local = [pltpu.make_async_copy(s_refs[a].at[me], r_refs[a].at[me], local_sems.at[a]) for a in range(len(s_refs))]
    remote = []
    for k in range(3):
        px = 1 - x if (k + 1) >> 1 & 1 else x
        py = 1 - y if (k + 1) & 1 else y
        for a in range(len(s_refs)):
            remote.append(pltpu.make_async_remote_copy(
                src_ref=s_refs[a].at[2 * px + py], dst_ref=r_refs[a].at[me],
                send_sem=send_sems.at[3 * a + k], recv_sem=recv_sems.at[3 * a + k],
                device_id=(px, py, c), device_id_type=MESH_ID))
    return local, remote


def _start_all(local, remote):
    for cp in local + remote:
        cp.start()


def _wait_all(local, remote):
    for cp in remote:
        cp.wait_recv()
    for cp in remote:
        cp.wait_send()
    for cp in local:
        cp.wait()


def _copy_sems(n, peers):
    return [pltpu.SemaphoreType.DMA((peers * n,)), pltpu.SemaphoreType.DMA((peers * n,)),
            pltpu.SemaphoreType.DMA((n,))]


ANY = pl.BlockSpec(memory_space=pl.ANY)


def _assemble_w_in(slots):
    tc = 256
    pieces = _w_in_pieces()

    def body(s_ref, w_ref, wt_ref):
        for lo, hi in W_IN_ZERO:
            wt_ref[lo:hi, :] = jnp.zeros((hi - lo, tc), BF16)
        for p, lo, hi, dst in pieces:
            wt_ref[dst:dst + hi - lo, :] = s_ref[p, lo:hi, :]
        w_ref[...] = wt_ref[...].T

    return pl.pallas_call(
        body,
        grid=(D_MODEL // tc,),
        in_specs=[pl.BlockSpec((N_DEV, W_IN_SHARD, tc), lambda i: (0, 0, i))],
        out_specs=[pl.BlockSpec((tc, D_IN_PAD), lambda i: (i, 0)), pl.BlockSpec((D_IN_PAD, tc), lambda i: (0, i))],
        out_shape=[jax.ShapeDtypeStruct((D_MODEL, D_IN_PAD), BF16), jax.ShapeDtypeStruct((D_IN_PAD, D_MODEL), BF16)],
        compiler_params=_params(("arbitrary",)),
        name="assemble_w_in",
    )(slots)


def _scatter_w_in(dw):
    tc = 256
    pieces = _w_in_pieces()

    def body(d_ref, s_ref):
        dt = d_ref[...].T
        for p, lo, hi, dst in pieces:
            s_ref[p, lo:hi, :] = dt[dst:dst + hi - lo, :].astype(BF16)

    return pl.pallas_call(
        body,
        grid=(D_MODEL // tc,),
        in_specs=[pl.BlockSpec((tc, D_IN_PAD), lambda i: (i, 0))],
        out_specs=pl.BlockSpec((N_DEV, W_IN_SHARD, tc), lambda i: (0, 0, i)),
        out_shape=jax.ShapeDtypeStruct((N_DEV, W_IN_SHARD, D_MODEL), BF16),
        compiler_params=_params(("arbitrary",)),
        name="scatter_w_in",
    )(dw)


def _norm_proj(x, g_pre, w, shards):
    s, n = x.shape[0], w.shape[1]
    tm, tn = 512, 1152
    ni, nj, ns = s // tm, n // tn, len(shards)

    def body(x_ref, g_ref, w_ref, *rest):
        shard_refs, (proj_ref, ht_ref), got_refs = rest[:ns], rest[ns:ns + 2], rest[ns + 2:2 * ns + 2]
        h_ref, sems = rest[2 * ns + 2], rest[2 * ns + 3:]
        j, i = pl.program_id(0), pl.program_id(1)
        rows = pl.ds(pl.multiple_of(i * tm, tm), tm)

        @pl.when((j == 0) & (i == 0))
        def _():
            _start_all(*_to_all_copies(shard_refs, got_refs, sems, True))

        @pl.when(j == 0)
        def _():
            xv = x_ref[...]
            r = lax.rsqrt(jnp.mean(xv * xv, axis=-1, keepdims=True) + EPS)
            h = (xv * r * g_ref[...]).astype(BF16)
            h_ref[rows, :] = h
            ht_ref[...] = h.T

        proj_ref[...] = _dot(h_ref[rows, :], w_ref[...])

        @pl.when((j == nj - 1) & (i == ni - 1))
        def _():
            _wait_all(*_to_all_copies(shard_refs, got_refs, sems, True))

    first = lambda j, i: jnp.where(j == 0, i, ni - 1)
    return pl.pallas_call(
        body,
        grid=(nj, ni),
        in_specs=[
            pl.BlockSpec((tm, D_MODEL), lambda j, i: (first(j, i), 0)),
            pl.BlockSpec((1, D_MODEL), lambda j, i: (0, 0)),
            pl.BlockSpec((D_MODEL, tn), lambda j, i: (0, j)),
        ] + [ANY] * ns,
        out_specs=[
            pl.BlockSpec((tm, tn), lambda j, i: (i, j)),
            pl.BlockSpec((D_MODEL, tm), lambda j, i: (0, first(j, i))),
        ] + [ANY] * ns,
        out_shape=[jax.ShapeDtypeStruct((s, n), F32), jax.ShapeDtypeStruct((D_MODEL, s), BF16)]
        + [jax.ShapeDtypeStruct((N_DEV,) + b.shape, b.dtype) for b in shards],
        scratch_shapes=[pltpu.VMEM((s, D_MODEL), BF16)] + _copy_sems(ns, 7),
        compiler_params=_params(("arbitrary", "arbitrary")),
        name="norm_proj",
    )(x, g_pre, w, *shards)


def _mla_prep(proj, g_q, g_kv, w_uq_p, w_kv_p, rc, rs1, rs2):
    s = proj.shape[0]
    tm = 256
    scale = 1.0 / math.sqrt(QK)

    def body(cq_ref, ckv_ref, kpe_ref, gq_ref, gkv_ref, wuq_ref, wkv_ref, c_ref, s1_ref, s2_ref,
             qr_ref, kr_ref, v_ref, cqt_ref, ckvt_ref):
        cq = cq_ref[...]
        r = lax.rsqrt(jnp.mean(cq * cq, axis=-1, keepdims=True) + EPS)
        cqn = (cq * r * gq_ref[...]).astype(BF16)
        cqt_ref[...] = cqn.T
        q = _dot(cqn, wuq_ref[...])
        ckv = ckv_ref[...]
        r = lax.rsqrt(jnp.mean(ckv * ckv, axis=-1, keepdims=True) + EPS)
        ckvn = (ckv * r * gkv_ref[...]).astype(BF16)
        ckvt_ref[...] = ckvn.T
        kv = _dot(ckvn, wkv_ref[...])
        c, s1, s2 = c_ref[...], s1_ref[...], s2_ref[...]
        lane = lax.broadcasted_iota(jnp.int32, (tm, LANE), 1)
        kpe = _rope(kpe_ref[...], c, s1, s2) + jnp.where((lane == QK) | (lane == QK + 1), 1.0, 0.0)
        vone = jnp.where((lane == VDIM) | (lane == VDIM + 1), 1.0, 0.0)
        for h in range(HEADS):
            sl = slice(LANE * h, LANE * (h + 1))
            qr_ref[:, sl] = (_rope(q[:, sl], c, s1, s2) * scale).astype(BF16)
            kr_ref[:, sl] = (kv[:, sl] + kpe).astype(BF16)
            v_ref[:, sl] = (kv[:, HEADS * LANE + LANE * h:HEADS * LANE + LANE * (h + 1)] + vone).astype(BF16)

    row = lambda w, j: pl.BlockSpec((tm, w), lambda i: (i, j))
    col = lambda w: pl.BlockSpec((w, tm), lambda i: (0, i))
    full = lambda a: pl.BlockSpec(a.shape, lambda i: (0, 0))
    return pl.pallas_call(
        body,
        grid=(s // tm,),
        in_specs=[row(768, 6), row(256, 21), row(128, 44), full(g_q), full(g_kv), full(w_uq_p), full(w_kv_p),
                  row(128, 0), row(128, 0), row(128, 0)],
        out_specs=[row(1024, 0), row(1024, 0), row(1024, 0), col(768), col(256)],
        out_shape=[jax.ShapeDtypeStruct((s, 1024), BF16), jax.ShapeDtypeStruct((s, 1024), BF16),
                   jax.ShapeDtypeStruct((s, 1024), BF16), jax.ShapeDtypeStruct((768, s), BF16),
                   jax.ShapeDtypeStruct((256, s), BF16)],
        compiler_params=_params(("arbitrary",)),
        name="mla_prep",
    )(proj, proj, proj, g_q, g_kv, w_uq_p, w_kv_p, rc, rs1, rs2)


ATT_T = 512


def _chunk_mask(transposed):
    r = lax.broadcasted_iota(jnp.int32, (ATT_T, ATT_T), 0) >> ATT_CHUNK_SHIFT
    c = lax.broadcasted_iota(jnp.int32, (ATT_T, ATT_T), 1) >> ATT_CHUNK_SHIFT
    return (r <= c) if transposed else (c <= r)


def _attn_fwd(qr, kr, vp):
    s = qr.shape[0]
    t = ATT_T

    def body(q_ref, k_ref, v_ref, o_ref, qa_ref, sc_ref):
        qi = pl.program_id(1)
        lane = lax.broadcasted_iota(jnp.int32, (t, LANE), 1)
        sls = [slice(LANE * a, LANE * (a + 1)) for a in range(2)]
        qs = [q_ref[:, sl] for sl in sls]

        def scores(j):
            rows = pl.ds(pl.multiple_of(j * t, t), t)
            for a in range(2):
                sc_ref[j & 1, a] = _dotg(qs[a], k_ref[rows, sls[a]], NT)

        def step(j, carry, masked):
            rows = pl.ds(pl.multiple_of(j * t, t), t)
            out = []
            for a in range(2):
                m, acc = carry[a]
                sc = sc_ref[j & 1, a]
                if masked:
                    sc = jnp.where(_chunk_mask(False), sc, -1e30)
                m_new = jnp.maximum(m, jnp.max(sc, axis=-1, keepdims=True))
                p = jnp.exp(sc - m_new).astype(BF16)
                acc = jnp.exp(m - m_new) * acc + _dot(p, v_ref[rows, sls[a]])
                out.append((m_new, acc))
            return tuple(out)

        def loop(j, carry):
            carry = step(j, carry, False)
            scores(j + 1)
            return carry

        init = tuple((jnp.full((t, 1), -1e30, F32), jnp.zeros((t, LANE), F32)) for _ in range(2))
        scores(0)
        carry = lax.fori_loop(0, qi, loop, init)
        carry = step(qi, carry, True)
        outs = []
        for a in range(2):
            m, acc = carry[a]
            l = acc[:, VDIM:VDIM + 1]
            outs.append(acc / l)
            hi, lo_part = _hi_lo(-(m + jnp.log(l)))
            qa = jnp.where(lane == QK, hi, jnp.where(lane == QK + 1, lo_part, qs[a].astype(F32)))
            qa_ref[:, sls[a]] = qa.astype(BF16)
        o_ref[...] = jnp.where(lane < VDIM, outs[0], pltpu.roll(outs[1], VDIM, 1))

    return pl.pallas_call(
        body,
        grid=(HEADS // 2, s // t),
        in_specs=[
            pl.BlockSpec((t, 2 * LANE), lambda h, i: (i, h)),
            pl.BlockSpec((s, 2 * LANE), lambda h, i: (0, h)),
            pl.BlockSpec((s, 2 * LANE), lambda h, i: (0, h)),
        ],
        out_specs=[
            pl.BlockSpec((t, LANE), lambda h, i: (i, h)),
            pl.BlockSpec((t, 2 * LANE), lambda h, i: (i, h)),
        ],
        out_shape=[jax.ShapeDtypeStruct((s, 512), F32), jax.ShapeDtypeStruct((s, 1024), BF16)],
        scratch_shapes=[pltpu.VMEM((2, 2, t, t), F32)],
        compiler_params=_params(("arbitrary", "arbitrary")),
        name="attn_fwd",
    )(qr, kr, vp)


def _attn_bwd(qa, kr, vp, dop, sends):
    s = qa.shape[0]
    t = ATT_T
    nq = s // t
    ns = len(sends)

    def body(q_ref, k_ref, v_ref, do_ref, *rest):
        send_refs, (dq_ref, dk_ref, dv_ref) = rest[:ns], rest[ns:ns + 3]
        recv_refs, sems = rest[ns + 3:2 * ns + 3], rest[2 * ns + 3:]
        j = pl.program_id(1)
        sls = [slice(LANE * a, LANE * (a + 1)) for a in range(2)]

        @pl.when((pl.program_id(0) == 0) & (j == 0))
        def _():
            _start_all(*_to_all_copies(send_refs, recv_refs, sems, False))

        @pl.when(j == 0)
        def _():
            dq_ref[...] = jnp.zeros_like(dq_ref)

        dk_ref[...] = jnp.zeros_like(dk_ref)
        dv_ref[...] = jnp.zeros_like(dv_ref)
        ks = [k_ref[:, sl] for sl in sls]
        vs = [v_ref[:, sl] for sl in sls]

        def step(i, masked):
            rows = pl.ds(pl.multiple_of(i * t, t), t)
            for a in range(2):
                q = q_ref[rows, sls[a]]
                do = do_ref[rows, sls[a]]
                sc = _dotg(ks[a], q, NT)
                if masked:
                    sc = jnp.where(_chunk_mask(True), sc, -1e30)
                p = jnp.exp(sc)
                ds = (p * _dotg(vs[a], do, NT)).astype(BF16)
                dv_ref[:, sls[a]] += _dot(p.astype(BF16), do)
                dk_ref[:, sls[a]] += _dot(ds, q)
                dq_ref[rows, sls[a]] += _dotg(ds, ks[a], TN)

        step(j, True)

        def loop(i, c):
            step(i, False)
            return c

        lax.fori_loop(j + 1, nq, loop, 0)

        @pl.when((pl.program_id(0) == HEADS // 2 - 1) & (j == nq - 1))
        def _():
            _wait_all(*_to_all_copies(send_refs, recv_refs, sems, False))

    blk = pl.BlockSpec((t, 2 * LANE), lambda h, j: (j, h))
    whole = pl.BlockSpec((s, 2 * LANE), lambda h, j: (0, h))
    out = jax.ShapeDtypeStruct((s, 1024), F32)
    return pl.pallas_call(
        body,
        grid=(HEADS // 2, nq),
        in_specs=[whole, blk, blk, whole] + [ANY] * ns,
        out_specs=[whole, blk, blk] + [ANY] * ns,
        out_shape=[out, out, out] + [jax.ShapeDtypeStruct(a.shape, a.dtype) for a in sends],
        scratch_shapes=_copy_sems(ns, 7),
        compiler_params=_params(("arbitrary", "arbitrary")),
        name="attn_bwd",
    )(qa, kr, vp, dop, *sends)


HG_T = 256
HG_NC = HG_T // HG_BLOCK


def _hg_consts():
    r = jnp.arange(HG_T)[:, None]
    c = jnp.arange(HG_T)[None, :]
    same = (r // HG_BLOCK) == (c // HG_BLOCK)
    mcum = (same & (c <= r)).astype(BF16)
    mrev = (same & (c >= r)).astype(BF16)
    msum = same.astype(BF16)
    a = jnp.arange(LANE)
    bd = ((a[:, None] < 64) == (a[None, :] < 64)).astype(F32)
    return mcum, mrev, msum, bd


def _hg_pre(hq, hf, lbl, mcum, msum):
    lb = _sigmoid(lbl[0:1, :] - lbl[1:2, :])
    sig = _sigmoid(hf)
    f = lb + (1.0 - lb) * sig
    lf = jnp.log(f)
    b = _sel_left(mcum, lf)
    big_l = _sel_left(msum, lf)
    k = 1.0 - f
    qd = hq * jnp.exp(b)
    ki = k * jnp.exp(-b)
    ke = k * jnp.exp(big_l - b)
    return lb, sig, f, b, big_l, qd, ki, ke


def _stack_pair(xp, lo):
    return jnp.concatenate([jnp.where(lo, xp, 0.0), jnp.where(lo, 0.0, xp)], axis=0)


def _hgrn_fwd(proj, lbl):
    s = proj.shape[0]
    t = HG_T
    mcum, _, msum, bd = _hg_consts()

    def body(hq_ref, hf_ref, hi_ref, lbl_ref, mcum_ref, msum_ref, bd_ref, o_ref, sp_ref, st_ref):
        @pl.when(pl.program_id(0) == 0)
        def _():
            st_ref[...] = jnp.zeros_like(st_ref)

        mc = mcum_ref[...]
        _, _, _, _, big_l, qd, ki, ke = _hg_pre(hq_ref[...], hf_ref[...], lbl_ref[...], mc, msum_ref[...])
        el = jnp.exp(big_l)
        hi = hi_ref[...]
        lo = lax.broadcasted_iota(jnp.int32, (t, LANE), 1) < 64
        mask2 = jnp.concatenate([mc, mc], axis=0) > 0.5
        for p in range(HEADS // 2):
            sl = slice(LANE * p, LANE * (p + 1))
            vp = hi[:, sl].astype(BF16)
            q2 = _stack_pair(qd[:, sl], lo).astype(BF16)
            a2 = jnp.where(mask2, _dotg(q2, ki[:, sl].astype(BF16), NT), 0.0)
            r2 = _dot(a2.astype(BF16), vp)
            o_intra = jnp.where(lo, r2[:t], r2[t:])
            qb = qd[:, sl].astype(BF16)
            kb = ke[:, sl].astype(BF16)
            st = st_ref[p]
            for c in range(HG_NC):
                rows = slice(HG_BLOCK * c, HG_BLOCK * (c + 1))
                sp_ref[c, :, sl] = st
                o_ref[rows, sl] = o_intra[rows] + _dotg(qb[rows], st.astype(BF16), NT)
                u = _dotg(vp[rows], kb[rows], TN) * bd_ref[...]
                st = st * el[HG_BLOCK * c:HG_BLOCK * c + 1, sl] + u
            st_ref[p] = st

    row = lambda j: pl.BlockSpec((t, HG_WIDTH), lambda i: (i, j))
    full = lambda a: pl.BlockSpec(a.shape, lambda i: (0, 0))
    return pl.pallas_call(
        body,
        grid=(s // t,),
        in_specs=[row(6), row(7), row(8), full(lbl), full(mcum), full(msum), full(bd)],
        out_specs=[row(0), pl.BlockSpec((HG_NC, LANE, HG_WIDTH), lambda i: (i, 0, 0))],
        out_shape=[jax.ShapeDtypeStruct((s, HG_WIDTH), F32),
                   jax.ShapeDtypeStruct((s // HG_BLOCK, LANE, HG_WIDTH), F32)],
        scratch_shapes=[pltpu.VMEM((HEADS // 2, LANE, LANE), F32)],
        compiler_params=_params(("arbitrary",)),
        name="hgrn_fwd",
    )(proj, proj, proj, lbl, mcum, msum, bd)


def _hgrn_bwd(proj, lbl, do, sprev, dproj):
    s = proj.shape[0]
    t = HG_T
    nt = s // t
    mcum, mrev, msum, bd = _hg_consts()

    def body(hq_ref, hf_ref, hi_ref, lbl_ref, do_ref, sp_ref, mcum_ref, mrev_ref, msum_ref, bd_ref,
             dproj_in, dh_ref, dlbl_ref, g_ref):
        del dproj_in

        @pl.when(pl.program_id(0) == 0)
        def _():
            g_ref[...] = jnp.zeros_like(g_ref)
            dlbl_ref[...] = jnp.zeros_like(dlbl_ref)

        mc = mcum_ref[...]
        lb, sig, f, b, big_l, qd, ki, ke = _hg_pre(hq_ref[...], hf_ref[...], lbl_ref[...], mc, msum_ref[...])
        el = jnp.exp(big_l)
        hi = hi_ref[...]
        dov = do_ref[...]
        lo = lax.broadcasted_iota(jnp.int32, (t, LANE), 1) < 64
        mask2 = jnp.concatenate([mc, mc], axis=0) > 0.5
        dqd_parts, dke_parts, dv_parts, del_parts, dki_parts = [], [], [], [], []
        for p in range(HEADS // 2):
            sl = slice(LANE * p, LANE * (p + 1))
            vp = hi[:, sl].astype(BF16)
            q2 = _stack_pair(qd[:, sl], lo).astype(BF16)
            kip = ki[:, sl].astype(BF16)
            do2 = _stack_pair(dov[:, sl], lo).astype(BF16)
            a2 = jnp.where(mask2, _dotg(q2, kip, NT), 0.0).astype(BF16)
            da2 = jnp.where(mask2, _dotg(do2, vp, NT), 0.0).astype(BF16)
            r2 = _dot(da2, kip)
            dki_parts.append(_dotg(da2, q2, TN))
            qb = qd[:, sl].astype(BF16)
            kb = ke[:, sl].astype(BF16)
            dob = dov[:, sl].astype(BF16)
            g = g_ref[p]
            dqd_c, dv_c, dke_c, del_c = [], [], [], []
            for c in range(HG_NC - 1, -1, -1):
                rows = slice(HG_BLOCK * c, HG_BLOCK * (c + 1))
                gb = g.astype(BF16)
                st = sp_ref[c, :, sl]
                dqd_c.append(_dot(dob[rows], st.astype(BF16)))
                dv_c.append(_dotg(kb[rows], gb, NT))
                dke_c.append(_dot(vp[rows], gb))
                del_c.append(jnp.broadcast_to(jnp.sum(g * st, axis=0, keepdims=True), (HG_BLOCK, LANE)))
                g = g * el[HG_BLOCK * c:HG_BLOCK * c + 1, sl] + _dotg(dob[rows], qb[rows], TN) * bd_ref[...]
            g_ref[p] = g
            up = lambda parts: jnp.concatenate(parts[::-1], axis=0)
            dqd_parts.append(jnp.where(lo, r2[:t], r2[t:]) + up(dqd_c))
            dv_parts.append(_dotg(a2, do2, TN) + up(dv_c))
            dke_parts.append(up(dke_c))
            del_parts.append(up(del_c))
        wide = lambda parts: jnp.concatenate(parts, axis=1)
        dqd, dke, dki, dvv, del_rows = wide(dqd_parts), wide(dke_parts), wide(dki_parts), wide(dv_parts), wide(del_parts)
        dh_ref[:, :HG_WIDTH] = (dqd * jnp.exp(b)).astype(BF16)
        dh_ref[:, 2 * HG_WIDTH:] = dvv.astype(BF16)
        dke_ke = dke * ke
        db = dqd * qd - dki * ki - dke_ke
        dl_rows = _sel_left(msum_ref[...], dke_ke) + del_rows * el
        is_last = (lax.broadcasted_iota(jnp.int32, (t, HG_WIDTH), 0) & (HG_BLOCK - 1)) == HG_BLOCK - 1
        db = db + jnp.where(is_last, dl_rows, 0.0)
        dlf = _sel_left(mrev_ref[...], db)
        dk = dki * jnp.exp(-b) + dke * jnp.exp(big_l - b)
        df = dlf / f - dk
        dh_ref[:, HG_WIDTH:2 * HG_WIDTH] = (df * (1.0 - lb) * sig * (1.0 - sig)).astype(BF16)
        dlb = jnp.sum(df * (1.0 - sig), axis=0, keepdims=True) * lb * (1.0 - lb)
        dlbl_ref[0:1, :] += dlb
        dlbl_ref[1:2, :] -= dlb

    rrow = lambda j: pl.BlockSpec((t, HG_WIDTH), lambda i: (nt - 1 - i, j))
    full = lambda a: pl.BlockSpec(a.shape, lambda i: (0, 0))
    return pl.pallas_call(
        body,
        grid=(nt,),
        in_specs=[rrow(6), rrow(7), rrow(8), full(lbl), rrow(0),
                  pl.BlockSpec((HG_NC, LANE, HG_WIDTH), lambda i: (nt - 1 - i, 0, 0)),
                  full(mcum), full(mrev), full(msum), full(bd), pl.BlockSpec(memory_space=pl.ANY)],
        out_specs=[pl.BlockSpec((t, 3 * HG_WIDTH), lambda i: (nt - 1 - i, 2)),
                   pl.BlockSpec((2, HG_WIDTH), lambda i: (0, 0))],
        out_shape=[jax.ShapeDtypeStruct(dproj.shape, BF16), jax.ShapeDtypeStruct((2, HG_WIDTH), F32)],
        input_output_aliases={10: 0},
        scratch_shapes=[pltpu.VMEM((HEADS // 2, LANE, LANE), F32)],
        compiler_params=_params(("arbitrary",)),
        name="hgrn_bwd",
    )(proj, proj, proj, lbl, do, sprev, mcum, mrev, msum, bd, dproj)


def _tail(x, tgt, proj, attn, o, w_a, w_b, w_out, w_at, w_bt, w_outt, b_gate, g_post, gh):
    s = x.shape[0]
    tm = 128
    ones64 = (jnp.arange(HG_WIDTH)[:, None] // 64 == jnp.arange(HG_WIDTH)[None, :] // 64).astype(BF16)

    def body(x_ref, t_ref, ml_ref, ga_ref, gb_ref, at_ref, o_ref, wa_ref, wb_ref, wo_ref, wat_ref, wbt_ref, wot_ref,
             bg_ref, gp_ref, gh_ref, ones_ref,
             dout_ref, dpj_ref, dop_ref, do_ref, mt_ref, dy_ref, yat_ref, dya_ref, ybt_ref, dyb_ref,
             loss_ref, dgp_ref, dbg_ref, dgh_ref):
        @pl.when(pl.program_id(0) == 0)
        def _():
            loss_ref[...] = jnp.zeros_like(loss_ref)
            dgp_ref[...] = jnp.zeros_like(dgp_ref)
            dbg_ref[...] = jnp.zeros_like(dbg_ref)
            dgh_ref[...] = jnp.zeros_like(dgh_ref)

        ones = ones_ref[...]
        gate_a = ga_ref[...]
        sa = _sigmoid(gate_a)
        silu_a = gate_a * sa
        attn_v = at_ref[...]
        ya_in = attn_v * silu_a
        ov = o_ref[...]
        ro = lax.rsqrt(_sel_right(ov * ov, ones) * (1.0 / 64.0) + EPS)
        ohat = ov * ro
        ghv = gh_ref[...]
        on = ohat * ghv
        gate_b = gb_ref[...]
        sb = _sigmoid(gate_b)
        silu_b = gate_b * sb
        yb_in = on * silu_b
        ya_bf = ya_in.astype(BF16)
        yb_bf = yb_in.astype(BF16)
        yat_ref[...] = ya_bf.T
        ybt_ref[...] = yb_bf.T
        y_a = _dot(ya_bf, wa_ref[...])
        y_b = _dot(yb_bf, wb_ref[...])
        gts = _sigmoid(ml_ref[...] + bg_ref[...])
        g_a = gts[:, :D_MODEL]
        g_b = gts[:, D_MODEL:]
        m_bf = (g_a * y_a + g_b * y_b).astype(BF16)
        mt_ref[...] = m_bf.T
        y = _dot(m_bf, wo_ref[...])
        r1 = lax.rsqrt(jnp.mean(y * y, axis=-1, keepdims=True) + EPS)
        yn = y * r1
        gp = gp_ref[...]
        e = x_ref[...] + yn * gp - t_ref[...]
        loss_ref[...] += jnp.sum(e * e, axis=0, keepdims=True)
        dout = e * (1.0 / D_MODEL)
        dout_ref[...] = dout
        dgp_ref[...] += jnp.sum(dout * yn, axis=0, keepdims=True)
        dyn = dout * gp
        dy = r1 * (dyn - yn * jnp.mean(dyn * yn, axis=-1, keepdims=True))
        dy_bf = dy.astype(BF16)
        dy_ref[...] = dy_bf
        dm = _dot(dy_bf, wot_ref[...])
        dml_a = dm * y_a * g_a * (1.0 - g_a)
        dml_b = dm * y_b * g_b * (1.0 - g_b)
        dpj_ref[:, :D_MODEL] = dml_a.astype(BF16)
        dpj_ref[:, D_MODEL:2 * D_MODEL] = dml_b.astype(BF16)
        dbg_ref[:, :D_MODEL] += jnp.sum(dml_a, axis=0, keepdims=True)
        dbg_ref[:, D_MODEL:] += jnp.sum(dml_b, axis=0, keepdims=True)
        dya_bf = (dm * g_a).astype(BF16)
        dyb_bf = (dm * g_b).astype(BF16)
        dya_ref[...] = dya_bf
        dyb_ref[...] = dyb_bf
        dya_in = _dot(dya_bf, wat_ref[...])
        dyb_in = _dot(dyb_bf, wbt_ref[...])
        dattn = dya_in * silu_a
        delta = _sel_right(dattn * attn_v, ones)
        lane = lax.broadcasted_iota(jnp.int32, (tm, LANE), 1)
        for p in range(HEADS // 2):
            sl = slice(LANE * p, LANE * (p + 1))
            xs = (dattn[:, sl], pltpu.roll(dattn[:, sl], VDIM, 1))
            nds = (-pltpu.roll(delta[:, sl], VDIM, 1), -delta[:, sl])
            for a in range(2):
                hi, lo_part = _hi_lo(nds[a])
                blk = jnp.where(lane < VDIM, xs[a], jnp.where(lane == VDIM, hi, jnp.where(lane == VDIM + 1, lo_part, 0.0)))
                dop_ref[:, LANE * (2 * p + a):LANE * (2 * p + a + 1)] = blk.astype(BF16)
        dpj_ref[:, 2 * D_MODEL:2 * D_MODEL + HG_WIDTH] = (
            dya_in * attn_v * (sa * (1.0 + gate_a * (1.0 - sa)))).astype(BF16)
        don = dyb_in * silu_b
        dpj_ref[:, 2 * D_MODEL + HG_WIDTH:] = (dyb_in * on * (sb * (1.0 + gate_b * (1.0 - sb)))).astype(BF16)
        dgh_ref[...] += jnp.sum(don * ohat, axis=0, keepdims=True)
        dohat = don * ghv
        do_ref[...] = ro * (dohat - ohat * (_sel_right(dohat * ohat, ones) * (1.0 / 64.0)))

    row = lambda w, j: pl.BlockSpec((tm, w), lambda i: (i, j))
    col = lambda w: pl.BlockSpec((w, tm), lambda i: (0, i))
    full = lambda a: pl.BlockSpec(a.shape, lambda i: (0, 0))
    acc = lambda w: pl.BlockSpec((1, w), lambda i: (0, 0))
    sds = lambda w, dt: jax.ShapeDtypeStruct((s, w), dt)
    sdt = lambda w: jax.ShapeDtypeStruct((w, s), BF16)
    return pl.pallas_call(
        body,
        grid=(s // tm,),
        in_specs=[row(1024, 0), row(1024, 0), row(2048, 0), row(512, 4), row(512, 5), row(512, 0), row(512, 0),
                  full(w_a), full(w_b), full(w_out), full(w_at), full(w_bt), full(w_outt),
                  full(b_gate), full(g_post), full(gh), full(ones64)],
        out_specs=[row(1024, 0), row(3072, 0), row(1024, 0), row(512, 0),
                   col(1024), row(1024, 0), col(512), row(1024, 0), col(512), row(1024, 0),
                   acc(1024), acc(1024), acc(2048), acc(512)],
        out_shape=[sds(1024, F32), sds(D_IN_PAD, BF16), sds(1024, BF16), sds(512, F32),
                   sdt(1024), sds(1024, BF16), sdt(512), sds(1024, BF16), sdt(512), sds(1024, BF16),
                   jax.ShapeDtypeStruct((1, 1024), F32), jax.ShapeDtypeStruct((1, 1024), F32),
                   jax.ShapeDtypeStruct((1, 2048), F32), jax.ShapeDtypeStruct((1, 512), F32)],
        compiler_params=_params(("arbitrary",), 56),
        name="tail",
    )(x, tgt, proj, proj, proj, attn, o, w_a, w_b, w_out, w_at, w_bt, w_outt, b_gate, g_post, gh, ones64)


def _mla_bwd(proj, dqr, dkr, dv, g_q, g_kv, w_uq_pt, w_kv_pt, rc, rs1, rs2, dproj):
    s = proj.shape[0]
    tm = 256
    scale = 1.0 / math.sqrt(QK)

    def body(cq_ref, ckv_ref, dqr_ref, dkr_ref, dv_ref, gq_ref, gkv_ref, wuqt_ref, wkvt_ref, c_ref, s1_ref, s2_ref,
             dproj_in, dqf_ref, dkvf_ref, dc_ref, dgq_ref, dgkv_ref):
        del dproj_in

        @pl.when(pl.program_id(0) == 0)
        def _():
            dgq_ref[...] = jnp.zeros_like(dgq_ref)
            dgkv_ref[...] = jnp.zeros_like(dgkv_ref)

        c, s1, s2 = c_ref[...], s1_ref[...], s2_ref[...]
        lane = lax.broadcasted_iota(jnp.int32, (tm, LANE), 1)
        ksum = jnp.zeros((tm, LANE), F32)
        for h in range(HEADS):
            sl = slice(LANE * h, LANE * (h + 1))
            dqf_ref[:, sl] = (_unrope(dqr_ref[:, sl], c, s1, s2) * scale).astype(BF16)
            dkh = dkr_ref[:, sl]
            ksum = ksum + dkh
            dkvf_ref[:, sl] = jnp.where(lane < NOPE, dkh, 0.0).astype(BF16)
            dkvf_ref[:, HEADS * LANE + LANE * h:HEADS * LANE + LANE * (h + 1)] = jnp.where(
                lane < VDIM, dv_ref[:, sl], 0.0).astype(BF16)
        dkpe = _unrope(ksum, c, s1, s2)
        dc_ref[:, Q_LORA + KV_LORA:] = jnp.where((lane >= NOPE) & (lane < QK), dkpe, 0.0).astype(BF16)
        dcqn = _dot(dqf_ref[...], wuqt_ref[...])
        dckvn = _dot(dkvf_ref[...], wkvt_ref[...])
        for x_ref, g_ref, dn, cols, dg_ref in ((cq_ref, gq_ref, dcqn, slice(0, Q_LORA), dgq_ref),
                                               (ckv_ref, gkv_ref, dckvn, slice(Q_LORA, Q_LORA + KV_LORA), dgkv_ref)):
            xv = x_ref[...]
            r = lax.rsqrt(jnp.mean(xv * xv, axis=-1, keepdims=True) + EPS)
            xh = xv * r
            dg_ref[...] += jnp.sum(dn * xh, axis=0, keepdims=True)
            dh = dn * g_ref[...]
            dc_ref[:, cols] = (r * (dh - xh * jnp.mean(dh * xh, axis=-1, keepdims=True))).astype(BF16)

    row = lambda w, j: pl.BlockSpec((tm, w), lambda i: (i, j))
    full = lambda a: pl.BlockSpec(a.shape, lambda i: (0, 0))
    acc = lambda w: pl.BlockSpec((1, w), lambda i: (0, 0))
    sds = lambda w, dt: jax.ShapeDtypeStruct((s, w), dt)
    return pl.pallas_call(
        body,
        grid=(s // tm,),
        in_specs=[row(768, 6), row(256, 21), row(1024, 0), row(1024, 0), row(1024, 0), full(g_q), full(g_kv),
                  full(w_uq_pt), full(w_kv_pt), row(128, 0), row(128, 0), row(128, 0),
                  pl.BlockSpec(memory_space=pl.ANY)],
        out_specs=[row(1024, 0), row(2048, 0), row(1152, 4), acc(768), acc(256)],
        out_shape=[sds(1024, BF16), sds(2048, BF16), jax.ShapeDtypeStruct(dproj.shape, BF16),
                   jax.ShapeDtypeStruct((1, 768), F32), jax.ShapeDtypeStruct((1, 256), F32)],
        input_output_aliases={12: 2},
        compiler_params=_params(("arbitrary",)),
        name="mla_bwd",
    )(proj, proj, dqr, dkr, dv, g_q, g_kv, w_uq_pt, w_kv_pt, rc, rs1, rs2, dproj)


def _pick(n, options):
    for o in options:
        if n % o == 0:
            return o
    raise ValueError(n)


def _matmul(a, b, name):
    m, k = a.shape
    n = b.shape[1]
    tm = _pick(m, (1024, 768, 512, 256))
    tn = _pick(n, (1152, 1024, 768, 512))
    tk = _pick(k, (1024, 512))
    nk = k // tk

    def body(a_ref, b_ref, o_ref):
        @pl.when(pl.program_id(2) == 0)
        def _():
            o_ref[...] = jnp.zeros_like(o_ref)

        o_ref[...] += _dot(a_ref[...], b_ref[...])

    return pl.pallas_call(
        body,
        grid=(m // tm, n // tn, nk),
        in_specs=[pl.BlockSpec((tm, tk), lambda i, j, l: (i, l)), pl.BlockSpec((tk, tn), lambda i, j, l: (l, j))],
        out_specs=pl.BlockSpec((tm, tn), lambda i, j, l: (i, j)),
        out_shape=jax.ShapeDtypeStruct((m, n), F32),
        compiler_params=_params(("arbitrary", "arbitrary", "arbitrary")),
        name=name,
    )(a, b)


def _dh_dx(dproj, w_in_pt, x, dout, g_pre, sends):
    s, k = dproj.shape
    tm = 256
    ns, ni = len(sends), s // tm

    def body(dp_ref, w_ref, x_ref, dout_ref, g_ref, *rest):
        send_refs, (dx_ref, dg_ref) = rest[:ns], rest[ns:ns + 2]
        recv_refs, sems = rest[ns + 2:2 * ns + 2], rest[2 * ns + 2:]

        @pl.when(pl.program_id(0) == 0)
        def _():
            _start_all(*_to_chips_copies(send_refs, recv_refs, sems))
            dg_ref[...] = jnp.zeros_like(dg_ref)

        dh = _dot(dp_ref[...], w_ref[...])
        xv = x_ref[...]
        r = lax.rsqrt(jnp.mean(xv * xv, axis=-1, keepdims=True) + EPS)
        xh = xv * r
        dg_ref[...] += jnp.sum(dh * xh, axis=0, keepdims=True)
        dxh = dh * g_ref[...]
        dx_ref[...] = dout_ref[...] + r * (dxh - xh * jnp.mean(dxh * xh, axis=-1, keepdims=True))

        @pl.when(pl.program_id(0) == ni - 1)
        def _():
            _wait_all(*_to_chips_copies(send_refs, recv_refs, sems))

    row = lambda w: pl.BlockSpec((tm, w), lambda i: (i, 0))
    return pl.pallas_call(
        body,
        grid=(ni,),
        in_specs=[row(k), pl.BlockSpec((k, D_MODEL), lambda i: (0, 0)), row(D_MODEL), row(D_MODEL),
                  pl.BlockSpec((1, D_MODEL), lambda i: (0, 0))] + [ANY] * ns,
        out_specs=[row(D_MODEL), pl.BlockSpec((1, D_MODEL), lambda i: (0, 0))] + [ANY] * ns,
        out_shape=[jax.ShapeDtypeStruct((s, D_MODEL), F32), jax.ShapeDtypeStruct((1, D_MODEL), F32)]
        + [jax.ShapeDtypeStruct(a.shape, a.dtype) for a in sends],
        scratch_shapes=_copy_sems(ns, 3),
        compiler_params=_params(("arbitrary",)),
        name="dh_dx",
    )(dproj, w_in_pt, x, dout, g_pre, *sends)


def _pair_reduce(slots):
    n = len(slots)
    half = [(N_DEV // 2,) + a.shape[1:] for a in slots]

    def body(*refs):
        s_refs, o_refs = refs[:n], refs[n:2 * n]
        mine, got = refs[2 * n:3 * n], refs[3 * n:4 * n]
        send_sems, recv_sems, local_sems = refs[4 * n:]
        x, y, c = _my_place()
        copies, loads = [], []
        for a in range(n):
            for q in range(N_DEV // 2):
                copies.append(pltpu.make_async_remote_copy(
                    src_ref=s_refs[a].at[2 * q + 1 - c], dst_ref=got[a].at[q],
                    send_sem=send_sems.at[4 * a + q], recv_sem=recv_sems.at[4 * a + q],
                    device_id=(x, y, 1 - c), device_id_type=MESH_ID))
                loads.append(pltpu.make_async_copy(s_refs[a].at[2 * q + c], mine[a].at[q], local_sems.at[4 * a + q]))
        _start_all(loads, copies)
        _wait_all(loads, copies)
        for a in range(n):
            o_refs[a][...] = (mine[a][...].astype(F32) + got[a][...].astype(F32)).astype(o_refs[a].dtype)

    vm = lambda: [pltpu.VMEM(h, a.dtype) for h, a in zip(half, slots)]
    return pl.pallas_call(
        body,
        in_specs=[ANY] * n,
        out_shape=[jax.ShapeDtypeStruct(h, a.dtype) for h, a in zip(half, slots)],
        scratch_shapes=vm() + vm() + [pltpu.SemaphoreType.DMA((4 * n,)), pltpu.SemaphoreType.DMA((4 * n,)),
                                      pltpu.SemaphoreType.DMA((4 * n,))],
        compiler_params=pltpu.CompilerParams(vmem_limit_bytes=48 * 2**20),
        name="pair_reduce",
    )(*slots)


def _rope_tables(s):
    inv = (np.float32(ROPE_THETA) ** (-np.arange(0, ROPE, 2, dtype=np.float32) / np.float32(ROPE))).astype(np.float32)
    ang = (np.arange(s, dtype=np.float32)[:, None] * inv[None, :]).astype(np.float32)
    cos, sin = jnp.asarray(np.cos(ang.astype(np.float64)), F32), jnp.asarray(np.sin(ang.astype(np.float64)), F32)
    z = lambda w: jnp.zeros((s, w), F32)
    rc = jnp.concatenate([jnp.ones((s, NOPE), F32), cos, cos, z(32)], axis=1)
    rs1 = jnp.concatenate([z(NOPE), -sin, z(16), z(32)], axis=1)
    rs2 = jnp.concatenate([z(NOPE), z(16), sin, z(32)], axis=1)
    return rc, rs1, rs2


def _step(x, tgt, w_in_slots, shards, g_pre, b_gate, g_q, g_kv, lbl, g_hgrn, g_post):
    s = x.shape[0]
    w_in_p, w_in_pt = _assemble_w_in(w_in_slots)
    rc, rs1, rs2 = _rope_tables(s)
    gh = jnp.tile(g_hgrn, (1, HEADS))

    proj, ht, *got = _norm_proj(x, g_pre, w_in_p, shards)
    w_uq, w_ukv, w_a, w_b, w_out = (_from_slots(n, g) for n, g in zip(MATS, got))
    w_uq_p = jnp.pad(w_uq.reshape(Q_LORA, HEADS, QK), ((0, 0), (0, 0), (0, LANE - QK))).reshape(Q_LORA, HEADS * LANE)
    kv3 = w_ukv.reshape(KV_LORA, HEADS, NOPE + VDIM)
    pad64 = lambda t: jnp.pad(t, ((0, 0), (0, 0), (0, LANE - 64))).reshape(KV_LORA, HEADS * LANE)
    w_kv_p = jnp.concatenate([pad64(kv3[:, :, :NOPE]), pad64(kv3[:, :, NOPE:])], axis=1)

    qr, kr, v, cqt, ckvt = _mla_prep(proj, g_q, g_kv, w_uq_p, w_kv_p, rc, rs1, rs2)
    attn, qa = _attn_fwd(qr, kr, v)
    o, sprev = _hgrn_fwd(proj, lbl)
    (dout, dproj, dop, do, mt, dy_bf, yat, dya_bf, ybt, dyb_bf,
     loss_vec, dg_post, db_gate, dgh) = _tail(x, tgt, proj, attn, o, w_a, w_b, w_out, w_a.T, w_b.T, w_out.T,
                                               b_gate, g_post, gh)
    early = [_to_slots(n, _matmul(a, b, "d" + n)).astype(BF16)
             for n, a, b in (("w_branch_a", yat, dya_bf), ("w_branch_b", ybt, dyb_bf), ("w_out", mt, dy_bf))]
    dqr, dkr, dv, *early_recv = _attn_bwd(qa, kr, v, dop, early)
    dproj, dlbl = _hgrn_bwd(proj, lbl, do, sprev, dproj)
    dqf, dkvf, dproj, dg_q, dg_kv = _mla_bwd(proj, dqr, dkr, dv, g_q, g_kv, w_uq_p.T, w_kv_p.T, rc, rs1, rs2, dproj)

    dw_in_slots = _scatter_w_in(_matmul(ht, dproj, "dw_in"))
    dw_uq_p = _matmul(cqt, dqf, "dw_uq")
    dw_kv_p = _matmul(ckvt, dkvf, "dw_kv")
    dw_uq = dw_uq_p.reshape(Q_LORA, HEADS, LANE)[:, :, :QK].reshape(Q_LORA, HEADS * QK)
    dw_ukv = jnp.concatenate([dw_kv_p[:, :HEADS * LANE].reshape(KV_LORA, HEADS, LANE)[:, :, :NOPE],
                              dw_kv_p[:, HEADS * LANE:].reshape(KV_LORA, HEADS, LANE)[:, :, :VDIM]],
                             axis=2).reshape(KV_LORA, 1024)
    late = _pair_reduce([dw_in_slots, _to_slots("w_uq", dw_uq).astype(BF16), _to_slots("w_ukv", dw_ukv).astype(BF16)])
    dx, dg_pre, *late_recv = _dh_dx(dproj, w_in_pt, x, dout, g_pre, late)

    loss = 0.5 / D_MODEL * jnp.sum(loss_vec)
    vecs = dict(g_pre=dg_pre, b_gate=db_gate, g_q=dg_q, g_kv=dg_kv, lb_logits=dlbl,
                g_hgrn=jnp.sum(dgh.reshape(HEADS, VDIM), axis=0, keepdims=True), g_post=dg_post)
    small = _pack_small(vecs).at[LOSS_AT].set(loss)
    return dx, late_recv[0], dict(zip(MATS, late_recv[1:] + early_recv)), small


def _all_gather(blocks):
    n = len(blocks)

    def body(*refs):
        x_refs, out_refs = refs[:n], refs[n:2 * n]
        send_sems, recv_sems, local_sems = refs[2 * n:]
        x, y, c = _my_place()
        me, sibling = (x, y, c), (x, y, 1 - c)
        chips = [(1 - x, y), (x, 1 - y), (1 - x, 1 - y)]

        def slot(a, px, py, pc):
            return out_refs[a].at[4 * px + 2 * py + pc]

        def copy(a, k, blk, to, src=None):
            return pltpu.make_async_remote_copy(
                src_ref=slot(a, *blk) if src is None else src, dst_ref=slot(a, *blk),
                send_sem=send_sems.at[7 * a + k], recv_sem=recv_sems.at[7 * a + k],
                device_id=to, device_id_type=MESH_ID)

        mine = [pltpu.make_async_copy(x_refs[a], slot(a, *me), local_sems.at[a]) for a in range(n)]
        for cp in mine:
            cp.start()
        first = [copy(a, 0, me, sibling, src=x_refs[a]) for a in range(n)]
        first += [copy(a, 1 + j, me, (*chip, c), src=x_refs[a]) for a in range(n) for j, chip in enumerate(chips)]
        for cp in first:
            cp.start()
        passed = []
        for j, chip in enumerate(chips):
            for a in range(n):
                copy(a, 1 + j, (*chip, c), me).wait_recv()
                passed.append(copy(a, 4 + j, (*chip, c), sibling))
                passed[-1].start()
        for a in range(n):
            copy(a, 0, sibling, me).wait_recv()
        for j, chip in enumerate(chips):
            for a in range(n):
                copy(a, 4 + j, (*chip, 1 - c), me).wait_recv()
        for cp in first + passed:
            cp.wait_send()
        for cp in mine:
            cp.wait()

    return pl.pallas_call(
        body,
        out_shape=[jax.ShapeDtypeStruct((N_DEV,) + b.shape, b.dtype) for b in blocks],
        in_specs=[pl.BlockSpec(memory_space=pl.ANY)] * n,
        out_specs=[pl.BlockSpec(memory_space=pl.ANY)] * n,
        scratch_shapes=[pltpu.SemaphoreType.DMA((7 * n,)), pltpu.SemaphoreType.DMA((7 * n,)),
                        pltpu.SemaphoreType.DMA((n,))],
        name="gather_weights",
    )(*blocks)


def _adamw(g, w, m, v):
    c1 = 1.0 / (1.0 - ADAM_B1 ** ADAM_STEP)
    c2 = 1.0 / (1.0 - ADAM_B2 ** ADAM_STEP)
    nm = ADAM_B1 * m + (1.0 - ADAM_B1) * g
    nv = ADAM_B2 * v + (1.0 - ADAM_B2) * (g * g)
    d = -ADAM_LR * ((nm * c1) / (jnp.sqrt(nv * c2) + ADAM_EPS) + ADAM_WD * w)
    return d, nm, nv


def _sum8(r_ref):
    g = r_ref[0].astype(F32)
    for k in range(1, r_ref.shape[0]):
        g = g + r_ref[k].astype(F32)
    return g


def _sum_adamw_w_in(recv, w, m, v):
    rows, _, cols = w.shape
    tc = 256

    def body(r_ref, w_ref, m_ref, v_ref, g_ref, d_ref, nm_ref, nv_ref):
        g = _sum8(r_ref)
        dense = lambda ref: ref[...].reshape(rows, tc)
        d, nm, nv = _adamw(g, dense(w_ref), dense(m_ref), dense(v_ref))
        for ref, val in ((g_ref, g), (d_ref, d), (nm_ref, nm), (nv_ref, nv)):
            ref[...] = val.reshape(rows, 1, tc)

    blk = pl.BlockSpec((rows, 1, tc), lambda i: (0, 0, i))
    out = jax.ShapeDtypeStruct((rows, 1, cols), F32)
    return pl.pallas_call(
        body,
        grid=(cols // tc,),
        in_specs=[pl.BlockSpec((recv.shape[0], rows, tc), lambda i: (0, 0, i)), blk, blk, blk],
        out_specs=[blk, blk, blk, blk],
        out_shape=[out, out, out, out],
        compiler_params=_params(("arbitrary",)),
        name="sum_adamw_w_in",
    )(recv, w, m, v)


def _sum_adamw_whole(recvs, ws, ms, vs):
    n = len(ws)

    def body(*refs):
        r_refs, w_refs, m_refs, v_refs = refs[:n], refs[n:2 * n], refs[2 * n:3 * n], refs[3 * n:4 * n]
        outs = refs[4 * n:]
        for a in range(n):
            g = _sum8(r_refs[a])
            d, nm, nv = _adamw(g, w_refs[a][...], m_refs[a][...], v_refs[a][...])
            outs[a][...] = g
            outs[n + a][...] = d
            outs[2 * n + a][...] = nm
            outs[3 * n + a][...] = nv

    shapes = [jax.ShapeDtypeStruct(w.shape, F32) for w in ws]
    res = pl.pallas_call(
        body,
        out_shape=shapes * 4,
        compiler_params=pltpu.CompilerParams(vmem_limit_bytes=48 * 2**20),
        name="sum_adamw_mats",
    )(*recvs, *ws, *ms, *vs)
    return res[:n], res[n:2 * n], res[2 * n:3 * n], res[3 * n:]


def _vectors_update(small, w, m, v):
    def body(small_ref, w_ref, m_ref, v_ref, g_ref, d_ref, nm_ref, nv_ref, got, send_sems, recv_sems):
        x, y, c = _my_place()
        me = 4 * x + 2 * y + c
        got[me] = small_ref[...]
        copies = [pltpu.make_async_remote_copy(
            src_ref=small_ref, dst_ref=got.at[me], send_sem=send_sems.at[k], recv_sem=recv_sems.at[k],
            device_id=_flip(k, x, y, c), device_id_type=MESH_ID) for k in range(N_DEV - 1)]
        _start_all([], copies)
        _wait_all([], copies)
        g = _sum8(got)
        g_ref[...] = g
        d_ref[...], nm_ref[...], nv_ref[...] = _adamw(g, w_ref[...], m_ref[...], v_ref[...])

    out = jax.ShapeDtypeStruct(small.shape, F32)
    return pl.pallas_call(
        body,
        out_shape=[out, out, out, out],
        scratch_shapes=[pltpu.VMEM((N_DEV,) + small.shape, F32), pltpu.SemaphoreType.DMA((7,)),
                        pltpu.SemaphoreType.DMA((7,))],
        name="vectors_update",
    )(small, w, m, v)


MATS = ("w_uq", "w_ukv", "w_branch_a", "w_branch_b", "w_out")
SMALL = ("g_pre", "b_gate", "g_q", "g_kv", "lb_logits", "g_hgrn", "g_post")
SMALL_ROWS = (1, 2, 1, 1, 1, 1, 1)
LOSS_AT = (6, 1023)
SMALL_SHAPE = dict(g_pre=(1, 1024), b_gate=(1, 2048), g_q=(1, 768), g_kv=(1, 256), lb_logits=(2, 512),
                   g_hgrn=(1, 64), g_post=(1, 1024))
COL_SHARDED = dict(w_uq=False, w_ukv=True, w_branch_a=True, w_branch_b=True, w_out=False)
ORDER = ("g_pre", "w_in", "b_gate", "g_q", "w_uq", "g_kv", "w_ukv", "lb_logits", "g_hgrn",
         "w_branch_a", "w_branch_b", "w_out", "g_post")


def _pack_small(t):
    parts = []
    for n, r in zip(SMALL, SMALL_ROWS):
        flat = t[n].reshape(1, -1)
        parts.append(jnp.pad(flat, ((0, 0), (0, r * 1024 - flat.shape[1]))).reshape(r, 1024))
    return jnp.concatenate(parts, axis=0)


def _unpack_small(p):
    out, r0 = {}, 0
    for n, r in zip(SMALL, SMALL_ROWS):
        shp = SMALL_SHAPE[n]
        out[n] = p[r0:r0 + r].reshape(1, -1)[:, :shp[0] * shp[1]].reshape(shp)
        r0 += r
    return out


def _to_slots(name, full):
    r, c = full.shape
    if COL_SHARDED[name]:
        return full.reshape(r, N_DEV, c // N_DEV).transpose(1, 0, 2)
    return full.reshape(N_DEV, r // N_DEV, c)


def _from_slots(name, slots):
    _, r, c = slots.shape
    if COL_SHARDED[name]:
        return slots.transpose(1, 0, 2).reshape(r, N_DEV * c)
    return slots.reshape(N_DEV * r, c)


def kernel(x, g_pre, w_in, b_gate, g_q, w_uq, g_kv, w_ukv, lb_logits, g_hgrn, w_branch_a, w_branch_b, w_out, g_post, loss_target, m_g_pre, m_w_in, m_b_gate, m_g_q, m_w_uq, m_g_kv, m_w_ukv, m_lb_logits, m_g_hgrn, m_w_branch_a, m_w_branch_b, m_w_out, m_g_post, v_g_pre, v_w_in, v_b_gate, v_g_q, v_w_uq, v_g_kv, v_w_ukv, v_lb_logits, v_g_hgrn, v_w_branch_a, v_w_branch_b, v_w_out, v_g_post):
    rows3 = lambda a: jnp.transpose(a, (2, 0, 1))
    w = dict(w_in=rows3(w_in), w_uq=w_uq[0], w_ukv=w_ukv[0], w_branch_a=w_branch_a[0], w_branch_b=w_branch_b[0],
             w_out=w_out[0], g_pre=g_pre, b_gate=b_gate, g_q=g_q, g_kv=g_kv, lb_logits=lb_logits, g_hgrn=g_hgrn,
             g_post=g_post)
    mom = dict(w_in=rows3(m_w_in), w_uq=m_w_uq[0], w_ukv=m_w_ukv[0], w_branch_a=m_w_branch_a[0],
               w_branch_b=m_w_branch_b[0], w_out=m_w_out[0], g_pre=m_g_pre, b_gate=m_b_gate, g_q=m_g_q, g_kv=m_g_kv,
               lb_logits=m_lb_logits, g_hgrn=m_g_hgrn, g_post=m_g_post)
    var = dict(w_in=rows3(v_w_in), w_uq=v_w_uq[0], w_ukv=v_w_ukv[0], w_branch_a=v_w_branch_a[0],
               w_branch_b=v_w_branch_b[0], w_out=v_w_out[0], g_pre=v_g_pre, b_gate=v_b_gate, g_q=v_g_q, g_kv=v_g_kv,
               lb_logits=v_lb_logits, g_hgrn=v_g_hgrn, g_post=v_g_post)

    (w_in_slots,) = _all_gather([w["w_in"].reshape(W_IN_SHARD, D_MODEL).astype(BF16)])
    dx, recv_in, recv, small = _step(x[0], loss_target[0], w_in_slots, [w[n].astype(BF16) for n in MATS],
                                     g_pre, b_gate, g_q, g_kv, lb_logits, g_hgrn, g_post)

    g_in, d_in, m_in, v_in = _sum_adamw_w_in(recv_in, w["w_in"], mom["w_in"], var["w_in"])
    pk = lambda t: [t[n] for n in MATS]
    res = _sum_adamw_whole([recv[n] for n in MATS], pk(w), pk(mom), pk(var))
    vec = _vectors_update(small, _pack_small(w), _pack_small(mom), _pack_small(var))

    outs = []
    for mats, packed, big in zip(res, vec, (g_in, d_in, m_in, v_in)):
        t = {**{n: a[None] for n, a in zip(MATS, mats)}, **_unpack_small(packed),
             "w_in": jnp.transpose(big, (1, 2, 0))}
        outs += [t[n] for n in ORDER]
    total = vec[0][LOSS_AT]
    return (total, dx[None], *outs)
```

```python
import math

import jax
import jax.numpy as jnp
import numpy as np
from jax import lax
from jax.experimental import pallas as pl
from jax.experimental.pallas import tpu as pltpu

F32, BF16 = jnp.float32, jnp.bfloat16

D_MODEL = 1024
EPS = 1e-6
HEADS = 8
NOPE, ROPE, VDIM = 64, 32, 64
QK = NOPE + ROPE
Q_LORA, KV_LORA = 768, 256
ROPE_THETA = 10000.0
ATT_CHUNK_SHIFT = 6
HG_BLOCK = 32
HG_WIDTH = 512
D_IN = 5664
D_IN_PAD = 5760
W_IN_SHARD = D_IN // 8
N_DEV = 8
LANE = 128

ADAM_LR, ADAM_B1, ADAM_B2, ADAM_EPS, ADAM_WD, ADAM_STEP = 0.001, 0.9, 0.999, 1e-08, 0.01, 10

W_IN_SEGMENTS = ((3616, 5664, 0), (1056, 1568, 2048), (3104, 3616, 2560), (1568, 3104, 3072),
                 (0, 1024, 4608), (1024, 1056, 5696))
W_IN_ZERO = ((5632, 5696), (5728, 5760))

NT = (((1,), (1,)), ((), ()))
TN = (((0,), (0,)), ((), ()))
MESH_ID = pl.DeviceIdType.MESH


def _w_in_pieces():
    out = []
    for lo, hi, dst in W_IN_SEGMENTS:
        c = lo
        while c < hi:
            p = c // W_IN_SHARD
            e = min(hi, (p + 1) * W_IN_SHARD)
            out.append((p, c - p * W_IN_SHARD, e - p * W_IN_SHARD, dst + c - lo))
            c = e
    return out


def _params(sem, vmem_mb=48):
    return pltpu.CompilerParams(dimension_semantics=sem, vmem_limit_bytes=vmem_mb * 2**20)


def _dot(a, b):
    return jnp.dot(a, b, preferred_element_type=F32)


def _dotg(a, b, dims):
    return lax.dot_general(a, b, dims, preferred_element_type=F32)


def _split2(x):
    hi = x.astype(BF16)
    return hi, (x - hi.astype(F32)).astype(BF16)


def _sel_left(m01, x):
    hi, lo = _split2(x)
    return _dot(m01, hi) + _dot(m01, lo)


def _sel_right(x, m01):
    hi, lo = _split2(x)
    return _dot(hi, m01) + _dot(lo, m01)


def _hi_lo(x):
    hi = x.astype(BF16).astype(F32)
    return hi, x - hi


def _sigmoid(x):
    return 0.5 * jnp.tanh(0.5 * x) + 0.5


def _rope(x, c, s1, s2):
    return x * c + pltpu.roll(x, 112, 1) * s1 + pltpu.roll(x, 16, 1) * s2


def _unrope(d, c, s1, s2):
    return d * c + pltpu.roll(d * s1, 16, 1) + pltpu.roll(d * s2, 112, 1)


def _my_place():
    return lax.axis_index("x"), lax.axis_index("y"), lax.axis_index("c")


def _flip(k, x, y, c):
    fx, fy, fc = (k + 1) >> 2 & 1, (k + 1) >> 1 & 1, (k + 1) & 1
    return (1 - x if fx else x), (1 - y if fy else y), (1 - c if fc else c)


def _to_all_copies(s_refs, r_refs, sems, spread):
    send_sems, recv_sems, local_sems = sems
    x, y, c = _my_place()
    me = 4 * x + 2 * y + c
    src = (lambda a, p: s_refs[a]) if spread else (lambda a, p: s_refs[a].at[p])
    local = [pltpu.make_async_copy(src(a, me), r_refs[a].at[me], local_sems.at[a]) for a in range(len(s_refs))]
    remote = []
    for k in range(N_DEV - 1):
        px, py, pc = _flip(k, x, y, c)
        for a in range(len(s_refs)):
            remote.append(pltpu.make_async_remote_copy(
                src_ref=src(a, 4 * px + 2 * py + pc), dst_ref=r_refs[a].at[me],
                send_sem=send_sems.at[7 * a + k], recv_sem=recv_sems.at[7 * a + k],
                device_id=(px, py, pc), device_id_type=MESH_ID))
    return local, remote


def _to_chips_copies(s_refs, r_refs, sems):
    send_sems, recv_sems, local_sems = sems
    x, y, c = _my_place()
    me = 2 * x + y
    local = [pltpu.make_async_copy(s_refs[a].at[me], r_refs[a].at[me], local_sems.at[a]) for a in range(len(s_refs))]
    remote = []
    for k in range(3):
        px = 1 - x if (k + 1) >> 1 & 1 else x
        py = 1 - y if (k + 1) & 1 else y
        for a in range(len(s_refs)):
            remote.append(pltpu.make_async_remote_copy(
                src_ref=s_refs[a].at[2 * px + py], dst_ref=r_refs[a].at[me],
                send_sem=send_sems.at[3 * a + k], recv_sem=recv_sems.at[3 * a + k],
                device_id=(px, py, c), device_id_type=MESH_ID))
    return local, remote


def _start_all(local, remote):
    for cp in local + remote:
        cp.start()


def _wait_all(local, remote):
    for cp in remote:
        cp.wait_recv()
    for cp in remote:
        cp.wait_send()
    for cp in local:
        cp.wait()


def _copy_sems(n, peers):
    return [pltpu.SemaphoreType.DMA((peers * n,)), pltpu.SemaphoreType.DMA((peers * n,)),
            pltpu.SemaphoreType.DMA((n,))]


ANY = pl.BlockSpec(memory_space=pl.ANY)


def _assemble_w_in(slots):
    tc = 256
    pieces = _w_in_pieces()

    def body(s_ref, w_ref, wt_ref):
        for lo, hi in W_IN_ZERO:
            wt_ref[lo:hi, :] = jnp.zeros((hi - lo, tc), BF16)
        for p, lo, hi, dst in pieces:
            wt_ref[dst:dst + hi - lo, :] = s_ref[p, lo:hi, :]
        w_ref[...] = wt_ref[...].T

    return pl.pallas_call(
        body,
        grid=(D_MODEL // tc,),
        in_specs=[pl.BlockSpec((N_DEV, W_IN_SHARD, tc), lambda i: (0, 0, i))],
        out_specs=[pl.BlockSpec((tc, D_IN_PAD), lambda i: (i, 0)), pl.BlockSpec((D_IN_PAD, tc), lambda i: (0, i))],
        out_shape=[jax.ShapeDtypeStruct((D_MODEL, D_IN_PAD), BF16), jax.ShapeDtypeStruct((D_IN_PAD, D_MODEL), BF16)],
        compiler_params=_params(("arbitrary",)),
        name="assemble_w_in",
    )(slots)


def _scatter_w_in(dw):
    tc = 256
    pieces = _w_in_pieces()

    def body(d_ref, s_ref):
        dt = d_ref[...].T
        for p, lo, hi, dst in pieces:
            s_ref[p, lo:hi, :] = dt[dst:dst + hi - lo, :].astype(BF16)

    return pl.pallas_call(
        body,
        grid=(D_MODEL // tc,),
        in_specs=[pl.BlockSpec((tc, D_IN_PAD), lambda i: (i, 0))],
        out_specs=pl.BlockSpec((N_DEV, W_IN_SHARD, tc), lambda i: (0, 0, i)),
        out_shape=jax.ShapeDtypeStruct((N_DEV, W_IN_SHARD, D_MODEL), BF16),
        compiler_params=_params(("arbitrary",)),
        name="scatter_w_in",
    )(dw)


def _norm_proj(x, g_pre, w, shards):
    s, n = x.shape[0], w.shape[1]
    tm, tn = 512, 1152
    ni, nj, ns = s // tm, n // tn, len(shards)

    def body(x_ref, g_ref, w_ref, *rest):
        shard_refs, (proj_ref, ht_ref), got_refs = rest[:ns], rest[ns:ns + 2], rest[ns + 2:2 * ns + 2]
        h_ref, sems = rest[2 * ns + 2], rest[2 * ns + 3:]
        j, i = pl.program_id(0), pl.program_id(1)
        rows = pl.ds(pl.multiple_of(i * tm, tm), tm)

        @pl.when((j == 0) & (i == 0))
        def _():
            _start_all(*_to_all_copies(shard_refs, got_refs, sems, True))

        @pl.when(j == 0)
        def _():
            xv = x_ref[...]
            r = lax.rsqrt(jnp.mean(xv * xv, axis=-1, keepdims=True) + EPS)
            h = (xv * r * g_ref[...]).astype(BF16)
            h_ref[rows, :] = h
            ht_ref[...] = h.T

        proj_ref[...] = _dot(h_ref[rows, :], w_ref[...])

        @pl.when((j == nj - 1) & (i == ni - 1))
        def _():
            _wait_all(*_to_all_copies(shard_refs, got_refs, sems, True))

    first = lambda j, i: jnp.where(j == 0, i, ni - 1)
    return pl.pallas_call(
        body,
        grid=(nj, ni),
        in_specs=[
            pl.BlockSpec((tm, D_MODEL), lambda j, i: (first(j, i), 0)),
            pl.BlockSpec((1, D_MODEL), lambda j, i: (0, 0)),
            pl.BlockSpec((D_MODEL, tn), lambda j, i: (0, j)),
        ] + [ANY] * ns,
        out_specs=[
            pl.BlockSpec((tm, tn), lambda j, i: (i, j)),
            pl.BlockSpec((D_MODEL, tm), lambda j, i: (0, first(j, i))),
        ] + [ANY] * ns,
        out_shape=[jax.ShapeDtypeStruct((s, n), F32), jax.ShapeDtypeStruct((D_MODEL, s), BF16)]
        + [jax.ShapeDtypeStruct((N_DEV,) + b.shape, b.dtype) for b in shards],
        scratch_shapes=[pltpu.VMEM((s, D_MODEL), BF16)] + _copy_sems(ns, 7),
        compiler_params=_params(("arbitrary", "arbitrary")),
        name="norm_proj",
    )(x, g_pre, w, *shards)


def _mla_prep(proj, g_q, g_kv, w_uq_p, w_kv_p, rc, rs1, rs2):
    s = proj.shape[0]
    tm = 256
    scale = 1.0 / math.sqrt(QK)

    def body(cq_ref, ckv_ref, kpe_ref, gq_ref, gkv_ref, wuq_ref, wkv_ref, c_ref, s1_ref, s2_ref,
             qr_ref, kr_ref, v_ref, cqt_ref, ckvt_ref):
        cq = cq_ref[...]
        r = lax.rsqrt(jnp.mean(cq * cq, axis=-1, keepdims=True) + EPS)
        cqn = (cq * r * gq_ref[...]).astype(BF16)
        cqt_ref[...] = cqn.T
        q = _dot(cqn, wuq_ref[...])
        ckv = ckv_ref[...]
        r = lax.rsqrt(jnp.mean(ckv * ckv, axis=-1, keepdims=True) + EPS)
        ckvn = (ckv * r * gkv_ref[...]).astype(BF16)
        ckvt_ref[...] = ckvn.T
        kv = _dot(ckvn, wkv_ref[...])
        c, s1, s2 = c_ref[...], s1_ref[...], s2_ref[...]
        lane = lax.broadcasted_iota(jnp.int32, (tm, LANE), 1)
        kpe = _rope(kpe_ref[...], c, s1, s2) + jnp.where((lane == QK) | (lane == QK + 1), 1.0, 0.0)
        vone = jnp.where((lane == VDIM) | (lane == VDIM + 1), 1.0, 0.0)
        for h in range(HEADS):
            sl = slice(LANE * h, LANE * (h + 1))
            qr_ref[:, sl] = (_rope(q[:, sl], c, s1, s2) * scale).astype(BF16)
            kr_ref[:, sl] = (kv[:, sl] + kpe).astype(BF16)
            v_ref[:, sl] = (kv[:, HEADS * LANE + LANE * h:HEADS * LANE + LANE * (h + 1)] + vone).astype(BF16)

    row = lambda w, j: pl.BlockSpec((tm, w), lambda i: (i, j))
    col = lambda w: pl.BlockSpec((w, tm), lambda i: (0, i))
    full = lambda a: pl.BlockSpec(a.shape, lambda i: (0, 0))
    return pl.pallas_call(
        body,
        grid=(s // tm,),
        in_specs=[row(768, 6), row(256, 21), row(128, 44), full(g_q), full(g_kv), full(w_uq_p), full(w_kv_p),
                  row(128, 0), row(128, 0), row(128, 0)],
        out_specs=[row(1024, 0), row(1024, 0), row(1024, 0), col(768), col(256)],
        out_shape=[jax.ShapeDtypeStruct((s, 1024), BF16), jax.ShapeDtypeStruct((s, 1024), BF16),
                   jax.ShapeDtypeStruct((s, 1024), BF16), jax.ShapeDtypeStruct((768, s), BF16),
                   jax.ShapeDtypeStruct((256, s), BF16)],
        compiler_params=_params(("arbitrary",)),
        name="mla_prep",
    )(proj, proj, proj, g_q, g_kv, w_uq_p, w_kv_p, rc, rs1, rs2)


ATT_T = 512
ATT_FWD_HEADS = 4


def _chunk_mask(transposed):
    r = lax.broadcasted_iota(jnp.int32, (ATT_T, ATT_T), 0) >> ATT_CHUNK_SHIFT
    c = lax.broadcasted_iota(jnp.int32, (ATT_T, ATT_T), 1) >> ATT_CHUNK_SHIFT
    return (r <= c) if transposed else (c <= r)


def _attn_fwd(qr, kr, vp):
    s = qr.shape[0]
    t = ATT_T
    g = ATT_FWD_HEADS

    def body(q_ref, k_ref, v_ref, o_ref, qa_ref, sc_ref):
        qi = pl.program_id(1)
        lane = lax.broadcasted_iota(jnp.int32, (t, LANE), 1)
        sls = [slice(LANE * a, LANE * (a + 1)) for a in range(g)]
        qs = [q_ref[:, sl] for sl in sls]

        def scores(j):
            rows = pl.ds(pl.multiple_of(j * t, t), t)
            for a in range(g):
                sc_ref[j & 1, a] = _dotg(qs[a], k_ref[rows, sls[a]], NT)

        def step(j, carry, masked):
            rows = pl.ds(pl.multiple_of(j * t, t), t)
            out = []
            for a in range(g):
                m, acc = carry[a]
                sc = sc_ref[j & 1, a]
                if masked:
                    sc = jnp.where(_chunk_mask(False), sc, -1e30)
                m_new = jnp.maximum(m, jnp.max(sc, axis=-1, keepdims=True))
                p = jnp.exp(sc - m_new).astype(BF16)
                acc = jnp.exp(m - m_new) * acc + _dot(p, v_ref[rows, sls[a]])
                out.append((m_new, acc))
            return tuple(out)

        def loop(j, carry):
            carry = step(j, carry, False)
            scores(j + 1)
            return carry

        init = tuple((jnp.full((t, 1), -1e30, F32), jnp.zeros((t, LANE), F32)) for _ in range(g))
        scores(0)
        carry = lax.fori_loop(0, qi, loop, init)
        carry = step(qi, carry, True)
        outs = []
        for a in range(g):
            m, acc = carry[a]
            l = acc[:, VDIM:VDIM + 1]
            outs.append(acc / l)
            hi, lo_part = _hi_lo(-(m + jnp.log(l)))
            qa = jnp.where(lane == QK, hi, jnp.where(lane == QK + 1, lo_part, qs[a].astype(F32)))
            qa_ref[:, sls[a]] = qa.astype(BF16)
        for p in range(g // 2):
            o_ref[:, LANE * p:LANE * (p + 1)] = jnp.where(lane < VDIM, outs[2 * p], pltpu.roll(outs[2 * p + 1], VDIM, 1))

    return pl.pallas_call(
        body,
        grid=(HEADS // g, s // t),
        in_specs=[
            pl.BlockSpec((t, g * LANE), lambda h, i: (i, h)),
            pl.BlockSpec((s, g * LANE), lambda h, i: (0, h)),
            pl.BlockSpec((s, g * LANE), lambda h, i: (0, h)),
        ],
        out_specs=[
            pl.BlockSpec((t, g * VDIM), lambda h, i: (i, h)),
            pl.BlockSpec((t, g * LANE), lambda h, i: (i, h)),
        ],
        out_shape=[jax.ShapeDtypeStruct((s, 512), F32), jax.ShapeDtypeStruct((s, 1024), BF16)],
        scratch_shapes=[pltpu.VMEM((2, g, t, t), F32)],
        compiler_params=_params(("arbitrary", "arbitrary")),
        name="attn_fwd",
    )(qr, kr, vp)


def _attn_bwd(qa, kr, vp, dop, sends):
    s = qa.shape[0]
    t = ATT_T
    nq = s // t
    ns = len(sends)

    def body(q_ref, k_ref, v_ref, do_ref, *rest):
        send_refs, (dq_ref, dk_ref, dv_ref) = rest[:ns], rest[ns:ns + 3]
        recv_refs, sems = rest[ns + 3:2 * ns + 3], rest[2 * ns + 3:]
        j = pl.program_id(1)
        sls = [slice(LANE * a, LANE * (a + 1)) for a in range(2)]

        @pl.when((pl.program_id(0) == 0) & (j == 0))
        def _():
            _start_all(*_to_all_copies(send_refs, recv_refs, sems, False))

        @pl.when(j == 0)
        def _():
            dq_ref[...] = jnp.zeros_like(dq_ref)

        dk_ref[...] = jnp.zeros_like(dk_ref)
        dv_ref[...] = jnp.zeros_like(dv_ref)
        ks = [k_ref[:, sl] for sl in sls]
        vs = [v_ref[:, sl] for sl in sls]

        def step(i, masked):
            rows = pl.ds(pl.multiple_of(i * t, t), t)
            for a in range(2):
                q = q_ref[rows, sls[a]]
                do = do_ref[rows, sls[a]]
                sc = _dotg(ks[a], q, NT)
                if masked:
                    sc = jnp.where(_chunk_mask(True), sc, -1e30)
                p = jnp.exp(sc)
                ds = (p * _dotg(vs[a], do, NT)).astype(BF16)
                dv_ref[:, sls[a]] += _dot(p.astype(BF16), do)
                dk_ref[:, sls[a]] += _dot(ds, q)
                dq_ref[rows, sls[a]] += _dotg(ds, ks[a], TN)

        step(j, True)

        def loop(i, c):
            step(i, False)
            return c

        lax.fori_loop(j + 1, nq, loop, 0)

        @pl.when((pl.program_id(0) == HEADS // 2 - 1) & (j == nq - 1))
        def _():
            _wait_all(*_to_all_copies(send_refs, recv_refs, sems, False))

    blk = pl.BlockSpec((t, 2 * LANE), lambda h, j: (j, h))
    whole = pl.BlockSpec((s, 2 * LANE), lambda h, j: (0, h))
    out = jax.ShapeDtypeStruct((s, 1024), F32)
    return pl.pallas_call(
        body,
        grid=(HEADS // 2, nq),
        in_specs=[whole, blk, blk, whole] + [ANY] * ns,
        out_specs=[whole, blk, blk] + [ANY] * ns,
        out_shape=[out, out, out] + [jax.ShapeDtypeStruct(a.shape, a.dtype) for a in sends],
        scratch_shapes=_copy_sems(ns, 7),
        compiler_params=_params(("arbitrary", "arbitrary")),
        name="attn_bwd",
    )(qa, kr, vp, dop, *sends)


HG_T = 256
HG_NC = HG_T // HG_BLOCK


def _hg_consts():
    r = jnp.arange(HG_T)[:, None]
    c = jnp.arange(HG_T)[None, :]
    same = (r // HG_BLOCK) == (c // HG_BLOCK)
    mcum = (same & (c <= r)).astype(BF16)
    mrev = (same & (c >= r)).astype(BF16)
    msum = same.astype(BF16)
    a = jnp.arange(LANE)
    bd = ((a[:, None] < 64) == (a[None, :] < 64)).astype(F32)
    return mcum, mrev, msum, bd


def _hg_pre(hq, hf, lbl, mcum, msum):
    lb = _sigmoid(lbl[0:1, :] - lbl[1:2, :])
    sig = _sigmoid(hf)
    f = lb + (1.0 - lb) * sig
    lf = jnp.log(f)
    b = _sel_left(mcum, lf)
    big_l = _sel_left(msum, lf)
    k = 1.0 - f
    qd = hq * jnp.exp(b)
    ki = k * jnp.exp(-b)
    ke = k * jnp.exp(big_l - b)
    return lb, sig, f, b, big_l, qd, ki, ke


def _stack_pair(xp, lo):
    return jnp.concatenate([jnp.where(lo, xp, 0.0), jnp.where(lo, 0.0, xp)], axis=0)


def _hgrn_fwd(proj, lbl):
    s = proj.shape[0]
    t = HG_T
    mcum, _, msum, bd = _hg_consts()

    def body(hq_ref, hf_ref, hi_ref, lbl_ref, mcum_ref, msum_ref, bd_ref, o_ref, sp_ref, st_ref):
        @pl.when(pl.program_id(0) == 0)
        def _():
            st_ref[...] = jnp.zeros_like(st_ref)

        mc = mcum_ref[...]
        _, _, _, _, big_l, qd, ki, ke = _hg_pre(hq_ref[...], hf_ref[...], lbl_ref[...], mc, msum_ref[...])
        el = jnp.exp(big_l)
        hi = hi_ref[...]
        lo = lax.broadcasted_iota(jnp.int32, (t, LANE), 1) < 64
        mask2 = jnp.concatenate([mc, mc], axis=0) > 0.5
        for p in range(HEADS // 2):
            sl = slice(LANE * p, LANE * (p + 1))
            vp = hi[:, sl].astype(BF16)
            q2 = _stack_pair(qd[:, sl], lo).astype(BF16)
            a2 = jnp.where(mask2, _dotg(q2, ki[:, sl].astype(BF16), NT), 0.0)
            r2 = _dot(a2.astype(BF16), vp)
            o_intra = jnp.where(lo, r2[:t], r2[t:])
            qb = qd[:, sl].astype(BF16)
            kb = ke[:, sl].astype(BF16)
            st = st_ref[p]
            for c in range(HG_NC):
                rows = slice(HG_BLOCK * c, HG_BLOCK * (c + 1))
                sp_ref[c, :, sl] = st
                o_ref[rows, sl] = o_intra[rows] + _dotg(qb[rows], st.astype(BF16), NT)
                u = _dotg(vp[rows], kb[rows], TN) * bd_ref[...]
                st = st * el[HG_BLOCK * c:HG_BLOCK * c + 1, sl] + u
            st_ref[p] = st

    row = lambda j: pl.BlockSpec((t, HG_WIDTH), lambda i: (i, j))
    full = lambda a: pl.BlockSpec(a.shape, lambda i: (0, 0))
    return pl.pallas_call(
        body,
        grid=(s // t,),
        in_specs=[row(6), row(7), row(8), full(lbl), full(mcum), full(msum), full(bd)],
        out_specs=[row(0), pl.BlockSpec((HG_NC, LANE, HG_WIDTH), lambda i: (i, 0, 0))],
        out_shape=[jax.ShapeDtypeStruct((s, HG_WIDTH), F32),
                   jax.ShapeDtypeStruct((s // HG_BLOCK, LANE, HG_WIDTH), F32)],
        scratch_shapes=[pltpu.VMEM((HEADS // 2, LANE, LANE), F32)],
        compiler_params=_params(("arbitrary",)),
        name="hgrn_fwd",
    )(proj, proj, proj, lbl, mcum, msum, bd)


def _hgrn_bwd(proj, lbl, do, sprev, dproj):
    s = proj.shape[0]
    t = HG_T
    nt = s // t
    mcum, mrev, msum, bd = _hg_consts()

    def body(hq_ref, hf_ref, hi_ref, lbl_ref, do_ref, sp_ref, mcum_ref, mrev_ref, msum_ref, bd_ref,
             dproj_in, dh_ref, dlbl_ref, g_ref):
        del dproj_in

        @pl.when(pl.program_id(0) == 0)
        def _():
            g_ref[...] = jnp.zeros_like(g_ref)
            dlbl_ref[...] = jnp.zeros_like(dlbl_ref)

        mc = mcum_ref[...]
        lb, sig, f, b, big_l, qd, ki, ke = _hg_pre(hq_ref[...], hf_ref[...], lbl_ref[...], mc, msum_ref[...])
        el = jnp.exp(big_l)
        hi = hi_ref[...]
        dov = do_ref[...]
        lo = lax.broadcasted_iota(jnp.int32, (t, LANE), 1) < 64
        mask2 = jnp.concatenate([mc, mc], axis=0) > 0.5
        dqd_parts, dke_parts, dv_parts, del_parts, dki_parts = [], [], [], [], []
        for p in range(HEADS // 2):
            sl = slice(LANE * p, LANE * (p + 1))
            vp = hi[:, sl].astype(BF16)
            q2 = _stack_pair(qd[:, sl], lo).astype(BF16)
            kip = ki[:, sl].astype(BF16)
            do2 = _stack_pair(dov[:, sl], lo).astype(BF16)
            a2 = jnp.where(mask2, _dotg(q2, kip, NT), 0.0).astype(BF16)
            da2 = jnp.where(mask2, _dotg(do2, vp, NT), 0.0).astype(BF16)
            r2 = _dot(da2, kip)
            dki_parts.append(_dotg(da2, q2, TN))
            qb = qd[:, sl].astype(BF16)
            kb = ke[:, sl].astype(BF16)
            dob = dov[:, sl].astype(BF16)
            g = g_ref[p]
            dqd_c, dv_c, dke_c, del_c = [], [], [], []
            for c in range(HG_NC - 1, -1, -1):
                rows = slice(HG_BLOCK * c, HG_BLOCK * (c + 1))
                gb = g.astype(BF16)
                st = sp_ref[c, :, sl]
                dqd_c.append(_dot(dob[rows], st.astype(BF16)))
                dv_c.append(_dotg(kb[rows], gb, NT))
                dke_c.append(_dot(vp[rows], gb))
                del_c.append(jnp.broadcast_to(jnp.sum(g * st, axis=0, keepdims=True), (HG_BLOCK, LANE)))
                g = g * el[HG_BLOCK * c:HG_BLOCK * c + 1, sl] + _dotg(dob[rows], qb[rows], TN) * bd_ref[...]
            g_ref[p] = g
            up = lambda parts: jnp.concatenate(parts[::-1], axis=0)
            dqd_parts.append(jnp.where(lo, r2[:t], r2[t:]) + up(dqd_c))
            dv_parts.append(_dotg(a2, do2, TN) + up(dv_c))
            dke_parts.append(up(dke_c))
            del_parts.append(up(del_c))
        wide = lambda parts: jnp.concatenate(parts, axis=1)
        dqd, dke, dki, dvv, del_rows = wide(dqd_parts), wide(dke_parts), wide(dki_parts), wide(dv_parts), wide(del_parts)
        dh_ref[:, :HG_WIDTH] = (dqd * jnp.exp(b)).astype(BF16)
        dh_ref[:, 2 * HG_WIDTH:] = dvv.astype(BF16)
        dke_ke = dke * ke
        db = dqd * qd - dki * ki - dke_ke
        dl_rows = _sel_left(msum_ref[...], dke_ke) + del_rows * el
        is_last = (lax.broadcasted_iota(jnp.int32, (t, HG_WIDTH), 0) & (HG_BLOCK - 1)) == HG_BLOCK - 1
        db = db + jnp.where(is_last, dl_rows, 0.0)
        dlf = _sel_left(mrev_ref[...], db)
        dk = dki * jnp.exp(-b) + dke * jnp.exp(big_l - b)
        df = dlf / f - dk
        dh_ref[:, HG_WIDTH:2 * HG_WIDTH] = (df * (1.0 - lb) * sig * (1.0 - sig)).astype(BF16)
        dlb = jnp.sum(df * (1.0 - sig), axis=0, keepdims=True) * lb * (1.0 - lb)
        dlbl_ref[0:1, :] += dlb
        dlbl_ref[1:2, :] -= dlb

    rrow = lambda j: pl.BlockSpec((t, HG_WIDTH), lambda i: (nt - 1 - i, j))
    full = lambda a: pl.BlockSpec(a.shape, lambda i: (0, 0))
    return pl.pallas_call(
        body,
        grid=(nt,),
        in_specs=[rrow(6), rrow(7), rrow(8), full(lbl), rrow(0),
                  pl.BlockSpec((HG_NC, LANE, HG_WIDTH), lambda i: (nt - 1 - i, 0, 0)),
                  full(mcum), full(mrev), full(msum), full(bd), pl.BlockSpec(memory_space=pl.ANY)],
        out_specs=[pl.BlockSpec((t, 3 * HG_WIDTH), lambda i: (nt - 1 - i, 2)),
                   pl.BlockSpec((2, HG_WIDTH), lambda i: (0, 0))],
        out_shape=[jax.ShapeDtypeStruct(dproj.shape, BF16), jax.ShapeDtypeStruct((2, HG_WIDTH), F32)],
        input_output_aliases={10: 0},
        scratch_shapes=[pltpu.VMEM((HEADS // 2, LANE, LANE), F32)],
        compiler_params=_params(("arbitrary",)),
        name="hgrn_bwd",
    )(proj, proj, proj, lbl, do, sprev, mcum, mrev, msum, bd, dproj)


def _tail(x, tgt, proj, attn, o, w_a, w_b, w_out, w_at, w_bt, w_outt, b_gate, g_post, gh):
    s = x.shape[0]
    tm = 256
    ones64 = (jnp.arange(HG_WIDTH)[:, None] // 64 == jnp.arange(HG_WIDTH)[None, :] // 64).astype(BF16)
    weights = (w_a, w_b, w_out, w_at, w_bt, w_outt)

    def body(x_ref, t_ref, ml_ref, ga_ref, gb_ref, at_ref, o_ref, *rest):
        w_hbm, (bg_ref, gp_ref, gh_ref, ones_ref) = rest[:6], rest[6:10]
        (dout_ref, dpj_ref, dop_ref, do_ref, mt_ref, dy_ref, yat_ref, dya_ref, ybt_ref, dyb_ref,
         loss_ref, dgp_ref, dbg_ref, dgh_ref) = rest[10:24]
        (wa_ref, wb_ref, wo_ref, wat_ref, wbt_ref, wot_ref), w_sem = rest[24:30], rest[30]

        @pl.when(pl.program_id(0) == 0)
        def _():
            loads = [pltpu.make_async_copy(src, dst, w_sem.at[k])
                     for k, (src, dst) in enumerate(zip(w_hbm, rest[24:30]))]
            _start_all(loads, [])
            loss_ref[...] = jnp.zeros_like(loss_ref)
            dgp_ref[...] = jnp.zeros_like(dgp_ref)
            dbg_ref[...] = jnp.zeros_like(dbg_ref)
            dgh_ref[...] = jnp.zeros_like(dgh_ref)
            _wait_all(loads, [])

        ones = ones_ref[...]
        gate_a = ga_ref[...]
        sa = _sigmoid(gate_a)
        silu_a = gate_a * sa
        attn_v = at_ref[...]
        ya_in = attn_v * silu_a
        ov = o_ref[...]
        ro = lax.rsqrt(_sel_right(ov * ov, ones) * (1.0 / 64.0) + EPS)
        ohat = ov * ro
        ghv = gh_ref[...]
        on = ohat * ghv
        gate_b = gb_ref[...]
        sb = _sigmoid(gate_b)
        silu_b = gate_b * sb
        yb_in = on * silu_b
        ya_bf = ya_in.astype(BF16)
        yb_bf = yb_in.astype(BF16)
        yat_ref[...] = ya_bf.T
        ybt_ref[...] = yb_bf.T
        y_a = _dot(ya_bf, wa_ref[...])
        y_b = _dot(yb_bf, wb_ref[...])
        gts = _sigmoid(ml_ref[...] + bg_ref[...])
        g_a = gts[:, :D_MODEL]
        g_b = gts[:, D_MODEL:]
        m_bf = (g_a * y_a + g_b * y_b).astype(BF16)
        mt_ref[...] = m_bf.T
        y = _dot(m_bf, wo_ref[...])
        r1 = lax.rsqrt(jnp.mean(y * y, axis=-1, keepdims=True) + EPS)
        yn = y * r1
        gp = gp_ref[...]
        e = x_ref[...] + yn * gp - t_ref[...]
        loss_ref[...] += jnp.sum(e * e, axis=0, keepdims=True)
        dout = e * (1.0 / D_MODEL)
        dout_ref[...] = dout
        dgp_ref[...] += jnp.sum(dout * yn, axis=0, keepdims=True)
        dyn = dout * gp
        dy = r1 * (dyn - yn * jnp.mean(dyn * yn, axis=-1, keepdims=True))
        dy_bf = dy.astype(BF16)
        dy_ref[...] = dy_bf
        dm = _dot(dy_bf, wot_ref[...])
        dml_a = dm * y_a * g_a * (1.0 - g_a)
        dml_b = dm * y_b * g_b * (1.0 - g_b)
        dpj_ref[:, :D_MODEL] = dml_a.astype(BF16)
        dpj_ref[:, D_MODEL:2 * D_MODEL] = dml_b.astype(BF16)
        dbg_ref[:, :D_MODEL] += jnp.sum(dml_a, axis=0, keepdims=True)
        dbg_ref[:, D_MODEL:] += jnp.sum(dml_b, axis=0, keepdims=True)
        dya_bf = (dm * g_a).astype(BF16)
        dyb_bf = (dm * g_b).astype(BF16)
        dya_ref[...] = dya_bf
        dyb_ref[...] = dyb_bf
        dya_in = _dot(dya_bf, wat_ref[...])
        dyb_in = _dot(dyb_bf, wbt_ref[...])
        dattn = dya_in * silu_a
        delta = _sel_right(dattn * attn_v, ones)
        lane = lax.broadcasted_iota(jnp.int32, (tm, LANE), 1)
        for p in range(HEADS // 2):
            sl = slice(LANE * p, LANE * (p + 1))
            xs = (dattn[:, sl], pltpu.roll(dattn[:, sl], VDIM, 1))
            nds = (-pltpu.roll(delta[:, sl], VDIM, 1), -delta[:, sl])
            for a in range(2):
                hi, lo_part = _hi_lo(nds[a])
                blk = jnp.where(lane < VDIM, xs[a], jnp.where(lane == VDIM, hi, jnp.where(lane == VDIM + 1, lo_part, 0.0)))
                dop_ref[:, LANE * (2 * p + a):LANE * (2 * p + a + 1)] = blk.astype(BF16)
        dpj_ref[:, 2 * D_MODEL:2 * D_MODEL + HG_WIDTH] = (
            dya_in * attn_v * (sa * (1.0 + gate_a * (1.0 - sa)))).astype(BF16)
        don = dyb_in * silu_b
        dpj_ref[:, 2 * D_MODEL + HG_WIDTH:] = (dyb_in * on * (sb * (1.0 + gate_b * (1.0 - sb)))).astype(BF16)
        dgh_ref[...] += jnp.sum(don * ohat, axis=0, keepdims=True)
        dohat = don * ghv
        do_ref[...] = ro * (dohat - ohat * (_sel_right(dohat * ohat, ones) * (1.0 / 64.0)))

    row = lambda w, j: pl.BlockSpec((tm, w), lambda i: (i, j))
    col = lambda w: pl.BlockSpec((w, tm), lambda i: (0, i))
    full = lambda a: pl.BlockSpec(a.shape, lambda i: (0, 0))
    acc = lambda w: pl.BlockSpec((1, w), lambda i: (0, 0))
    sds = lambda w, dt: jax.ShapeDtypeStruct((s, w), dt)
    sdt = lambda w: jax.ShapeDtypeStruct((w, s), BF16)
    return pl.pallas_call(
        body,
        grid=(s // tm,),
        in_specs=[row(1024, 0), row(1024, 0), row(2048, 0), row(512, 4), row(512, 5), row(512, 0), row(512, 0)]
        + [ANY] * 6 + [full(b_gate), full(g_post), full(gh), full(ones64)],
        out_specs=[row(1024, 0), row(3072, 0), row(1024, 0), row(512, 0),
                   col(1024), row(1024, 0), col(512), row(1024, 0), col(512), row(1024, 0),
                   acc(1024), acc(1024), acc(2048), acc(512)],
        out_shape=[sds(1024, F32), sds(D_IN_PAD, BF16), sds(1024, BF16), sds(512, F32),
                   sdt(1024), sds(1024, BF16), sdt(512), sds(1024, BF16), sdt(512), sds(1024, BF16),
                   jax.ShapeDtypeStruct((1, 1024), F32), jax.ShapeDtypeStruct((1, 1024), F32),
                   jax.ShapeDtypeStruct((1, 2048), F32), jax.ShapeDtypeStruct((1, 512), F32)],
        scratch_shapes=[pltpu.VMEM(a.shape, BF16) for a in weights] + [pltpu.SemaphoreType.DMA((6,))],
        compiler_params=_params(("arbitrary",), 56),
        name="tail",
    )(x, tgt, proj, proj, proj, attn, o, *weights, b_gate, g_post, gh, ones64)


def _mla_bwd(proj, dqr, dkr, dv, g_q, g_kv, w_uq_pt, w_kv_pt, rc, rs1, rs2, dproj):
    s = proj.shape[0]
    tm = 256
    scale = 1.0 / math.sqrt(QK)

    def body(cq_ref, ckv_ref, dqr_ref, dkr_ref, dv_ref, gq_ref, gkv_ref, wuqt_ref, wkvt_ref, c_ref, s1_ref, s2_ref,
             dproj_in, dqf_ref, dkvf_ref, dc_ref, dgq_ref, dgkv_ref):
        del dproj_in

        @pl.when(pl.program_id(0) == 0)
        def _():
            dgq_ref[...] = jnp.zeros_like(dgq_ref)
            dgkv_ref[...] = jnp.zeros_like(dgkv_ref)

        c, s1, s2 = c_ref[...], s1_ref[...], s2_ref[...]
        lane = lax.broadcasted_iota(jnp.int32, (tm, LANE), 1)
        ksum = jnp.zeros((tm, LANE), F32)
        for h in range(HEADS):
            sl = slice(LANE * h, LANE * (h + 1))
            dqf_ref[:, sl] = (_unrope(dqr_ref[:, sl], c, s1, s2) * scale).astype(BF16)
            dkh = dkr_ref[:, sl]
            ksum = ksum + dkh
            dkvf_ref[:, sl] = jnp.where(lane < NOPE, dkh, 0.0).astype(BF16)
            dkvf_ref[:, HEADS * LANE + LANE * h:HEADS * LANE + LANE * (h + 1)] = jnp.where(
                lane < VDIM, dv_ref[:, sl], 0.0).astype(BF16)
        dkpe = _unrope(ksum, c, s1, s2)
        dc_ref[:, Q_LORA + KV_LORA:] = jnp.where((lane >= NOPE) & (lane < QK), dkpe, 0.0).astype(BF16)
        dcqn = _dot(dqf_ref[...], wuqt_ref[...])
        dckvn = _dot(dkvf_ref[...], wkvt_ref[...])
        for x_ref, g_ref, dn, cols, dg_ref in ((cq_ref, gq_ref, dcqn, slice(0, Q_LORA), dgq_ref),
                                               (ckv_ref, gkv_ref, dckvn, slice(Q_LORA, Q_LORA + KV_LORA), dgkv_ref)):
            xv = x_ref[...]
            r = lax.rsqrt(jnp.mean(xv * xv, axis=-1, keepdims=True) + EPS)
            xh = xv * r
            dg_ref[...] += jnp.sum(dn * xh, axis=0, keepdims=True)
            dh = dn * g_ref[...]
            dc_ref[:, cols] = (r * (dh - xh * jnp.mean(dh * xh, axis=-1, keepdims=True))).astype(BF16)

    row = lambda w, j: pl.BlockSpec((tm, w), lambda i: (i, j))
    full = lambda a: pl.BlockSpec(a.shape, lambda i: (0, 0))
    acc = lambda w: pl.BlockSpec((1, w), lambda i: (0, 0))
    sds = lambda w, dt: jax.ShapeDtypeStruct((s, w), dt)
    return pl.pallas_call(
        body,
        grid=(s // tm,),
        in_specs=[row(768, 6), row(256, 21), row(1024, 0), row(1024, 0), row(1024, 0), full(g_q), full(g_kv),
                  full(w_uq_pt), full(w_kv_pt), row(128, 0), row(128, 0), row(128, 0),
                  pl.BlockSpec(memory_space=pl.ANY)],
        out_specs=[row(1024, 0), row(2048, 0), row(1152, 4), acc(768), acc(256)],
        out_shape=[sds(1024, BF16), sds(2048, BF16), jax.ShapeDtypeStruct(dproj.shape, BF16),
                   jax.ShapeDtypeStruct((1, 768), F32), jax.ShapeDtypeStruct((1, 256), F32)],
        input_output_aliases={12: 2},
        compiler_params=_params(("arbitrary",)),
        name="mla_bwd",
    )(proj, proj, dqr, dkr, dv, g_q, g_kv, w_uq_pt, w_kv_pt, rc, rs1, rs2, dproj)


def _pick(n, options):
    for o in options:
        if n % o == 0:
            return o
    raise ValueError(n)


def _matmul(a, b, name):
    m, k = a.shape
    n = b.shape[1]
    tm = _pick(m, (1024, 768, 512, 256))
    tn = _pick(n, (1152, 1024, 768, 512))
    tk = _pick(k, (1024, 512))
    nk = k // tk

    def body(a_ref, b_ref, o_ref):
        @pl.when(pl.program_id(2) == 0)
        def _():
            o_ref[...] = jnp.zeros_like(o_ref)

        o_ref[...] += _dot(a_ref[...], b_ref[...])

    return pl.pallas_call(
        body,
        grid=(m // tm, n // tn, nk),
        in_specs=[pl.BlockSpec((tm, tk), lambda i, j, l: (i, l)), pl.BlockSpec((tk, tn), lambda i, j, l: (l, j))],
        out_specs=pl.BlockSpec((tm, tn), lambda i, j, l: (i, j)),
        out_shape=jax.ShapeDtypeStruct((m, n), F32),
        compiler_params=_params(("arbitrary", "arbitrary", "arbitrary")),
        name=name,
    )(a, b)


def _dh_dx(dproj, w_in_pt, x, dout, g_pre, sends):
    s, k = dproj.shape
    tm = 256
    ns, ni = len(sends), s // tm

    def body(dp_ref, w_ref, x_ref, dout_ref, g_ref, *rest):
        send_refs, (dx_ref, dg_ref) = rest[:ns], rest[ns:ns + 2]
        recv_refs, sems = rest[ns + 2:2 * ns + 2], rest[2 * ns + 2:]

        @pl.when(pl.program_id(0) == 0)
        def _():
            _start_all(*_to_chips_copies(send_refs, recv_refs, sems))
            dg_ref[...] = jnp.zeros_like(dg_ref)

        dh = _dot(dp_ref[...], w_ref[...])
        xv = x_ref[...]
        r = lax.rsqrt(jnp.mean(xv * xv, axis=-1, keepdims=True) + EPS)
        xh = xv * r
        dg_ref[...] += jnp.sum(dh * xh, axis=0, keepdims=True)
        dxh = dh * g_ref[...]
        dx_ref[...] = dout_ref[...] + r * (dxh - xh * jnp.mean(dxh * xh, axis=-1, keepdims=True))

        @pl.when(pl.program_id(0) == ni - 1)
        def _():
            _wait_all(*_to_chips_copies(send_refs, recv_refs, sems))

    row = lambda w: pl.BlockSpec((tm, w), lambda i: (i, 0))
    return pl.pallas_call(
        body,
        grid=(ni,),
        in_specs=[row(k), pl.BlockSpec((k, D_MODEL), lambda i: (0, 0)), row(D_MODEL), row(D_MODEL),
                  pl.BlockSpec((1, D_MODEL), lambda i: (0, 0))] + [ANY] * ns,
        out_specs=[row(D_MODEL), pl.BlockSpec((1, D_MODEL), lambda i: (0, 0))] + [ANY] * ns,
        out_shape=[jax.ShapeDtypeStruct((s, D_MODEL), F32), jax.ShapeDtypeStruct((1, D_MODEL), F32)]
        + [jax.ShapeDtypeStruct(a.shape, a.dtype) for a in sends],
        scratch_shapes=_copy_sems(ns, 3),
        compiler_params=_params(("arbitrary",)),
        name="dh_dx",
    )(dproj, w_in_pt, x, dout, g_pre, *sends)


def _pair_reduce(slots):
    n = len(slots)
    half = [(N_DEV // 2,) + a.shape[1:] for a in slots]

    def body(*refs):
        s_refs, o_refs = refs[:n], refs[n:2 * n]
        mine, got = refs[2 * n:3 * n], refs[3 * n:4 * n]
        send_sems, recv_sems, local_sems = refs[4 * n:]
        x, y, c = _my_place()
        copies, loads = [], []
        for a in range(n):
            for q in range(N_DEV // 2):
                copies.append(pltpu.make_async_remote_copy(
                    src_ref=s_refs[a].at[2 * q + 1 - c], dst_ref=got[a].at[q],
                    send_sem=send_sems.at[4 * a + q], recv_sem=recv_sems.at[4 * a + q],
                    device_id=(x, y, 1 - c), device_id_type=MESH_ID))
                loads.append(pltpu.make_async_copy(s_refs[a].at[2 * q + c], mine[a].at[q], local_sems.at[4 * a + q]))
        _start_all(loads, copies)
        _wait_all(loads, copies)
        for a in range(n):
            o_refs[a][...] = (mine[a][...].astype(F32) + got[a][...].astype(F32)).astype(o_refs[a].dtype)

    vm = lambda: [pltpu.VMEM(h, a.dtype) for h, a in zip(half, slots)]
    return pl.pallas_call(
        body,
        in_specs=[ANY] * n,
        out_shape=[jax.ShapeDtypeStruct(h, a.dtype) for h, a in zip(half, slots)],
        scratch_shapes=vm() + vm() + [pltpu.SemaphoreType.DMA((4 * n,)), pltpu.SemaphoreType.DMA((4 * n,)),
                                      pltpu.SemaphoreType.DMA((4 * n,))],
        compiler_params=pltpu.CompilerParams(vmem_limit_bytes=48 * 2**20),
        name="pair_reduce",
    )(*slots)


def _rope_tables(s):
    inv = (np.float32(ROPE_THETA) ** (-np.arange(0, ROPE, 2, dtype=np.float32) / np.float32(ROPE))).astype(np.float32)
    ang = (np.arange(s, dtype=np.float32)[:, None] * inv[None, :]).astype(np.float32)
    cos, sin = jnp.asarray(np.cos(ang.astype(np.float64)), F32), jnp.asarray(np.sin(ang.astype(np.float64)), F32)
    z = lambda w: jnp.zeros((s, w), F32)
    rc = jnp.concatenate([jnp.ones((s, NOPE), F32), cos, cos, z(32)], axis=1)
    rs1 = jnp.concatenate([z(NOPE), -sin, z(16), z(32)], axis=1)
    rs2 = jnp.concatenate([z(NOPE), z(16), sin, z(32)], axis=1)
    return rc, rs1, rs2


def _step(x, tgt, w_in_slots, shards, g_pre, b_gate, g_q, g_kv, lbl, g_hgrn, g_post):
    s = x.shape[0]
    w_in_p, w_in_pt = _assemble_w_in(w_in_slots)
    rc, rs1, rs2 = _rope_tables(s)
    gh = jnp.tile(g_hgrn, (1, HEADS))

    proj, ht, *got = _norm_proj(x, g_pre, w_in_p, shards)
    w_uq, w_ukv, w_a, w_b, w_out = (_from_slots(n, g) for n, g in zip(MATS, got))
    w_uq_p = jnp.pad(w_uq.reshape(Q_LORA, HEADS, QK), ((0, 0), (0, 0), (0, LANE - QK))).reshape(Q_LORA, HEADS * LANE)
    kv3 = w_ukv.reshape(KV_LORA, HEADS, NOPE + VDIM)
    pad64 = lambda t: jnp.pad(t, ((0, 0), (0, 0), (0, LANE - 64))).reshape(KV_LORA, HEADS * LANE)
    w_kv_p = jnp.concatenate([pad64(kv3[:, :, :NOPE]), pad64(kv3[:, :, NOPE:])], axis=1)

    qr, kr, v, cqt, ckvt = _mla_prep(proj, g_q, g_kv, w_uq_p, w_kv_p, rc, rs1, rs2)
    attn, qa = _attn_fwd(qr, kr, v)
    o, sprev = _hgrn_fwd(proj, lbl)
    (dout, dproj, dop, do, mt, dy_bf, yat, dya_bf, ybt, dyb_bf,
     loss_vec, dg_post, db_gate, dgh) = _tail(x, tgt, proj, attn, o, w_a, w_b, w_out, w_a.T, w_b.T, w_out.T,
                                               b_gate, g_post, gh)
    early = [_to_slots(n, _matmul(a, b, "d" + n)).astype(BF16)
             for n, a, b in (("w_branch_a", yat, dya_bf), ("w_branch_b", ybt, dyb_bf), ("w_out", mt, dy_bf))]
    dqr, dkr, dv, *early_recv = _attn_bwd(qa, kr, v, dop, early)
    dproj, dlbl = _hgrn_bwd(proj, lbl, do, sprev, dproj)
    dqf, dkvf, dproj, dg_q, dg_kv = _mla_bwd(proj, dqr, dkr, dv, g_q, g_kv, w_uq_p.T, w_kv_p.T, rc, rs1, rs2, dproj)

    dw_in_slots = _scatter_w_in(_matmul(ht, dproj, "dw_in"))
    dw_uq_p = _matmul(cqt, dqf, "dw_uq")
    dw_kv_p = _matmul(ckvt, dkvf, "dw_kv")
    dw_uq = dw_uq_p.reshape(Q_LORA, HEADS, LANE)[:, :, :QK].reshape(Q_LORA, HEADS * QK)
    dw_ukv = jnp.concatenate([dw_kv_p[:, :HEADS * LANE].reshape(KV_LORA, HEADS, LANE)[:, :, :NOPE],
                              dw_kv_p[:, HEADS * LANE:].reshape(KV_LORA, HEADS, LANE)[:, :, :VDIM]],
                             axis=2).reshape(KV_LORA, 1024)
    late = _pair_reduce([dw_in_slots, _to_slots("w_uq", dw_uq).astype(BF16), _to_slots("w_ukv", dw_ukv).astype(BF16)])
    dx, dg_pre, *late_recv = _dh_dx(dproj, w_in_pt, x, dout, g_pre, late)

    loss = 0.5 / D_MODEL * jnp.sum(loss_vec)
    vecs = dict(g_pre=dg_pre, b_gate=db_gate, g_q=dg_q, g_kv=dg_kv, lb_logits=dlbl,
                g_hgrn=jnp.sum(dgh.reshape(HEADS, VDIM), axis=0, keepdims=True), g_post=dg_post)
    small = _pack_small(vecs).at[LOSS_AT].set(loss)
    return dx, late_recv[0], dict(zip(MATS, late_recv[1:] + early_recv)), small


def _all_gather(blocks):
    n = len(blocks)

    def body(*refs):
        x_refs, out_refs = refs[:n], refs[n:2 * n]
        send_sems, recv_sems, local_sems = refs[2 * n:]
        x, y, c = _my_place()
        me, sibling = (x, y, c), (x, y, 1 - c)
        chips = [(1 - x, y), (x, 1 - y), (1 - x, 1 - y)]

        def slot(a, px, py, pc):
            return out_refs[a].at[4 * px + 2 * py + pc]

        def copy(a, k, blk, to, src=None):
            return pltpu.make_async_remote_copy(
                src_ref=slot(a, *blk) if src is None else src, dst_ref=slot(a, *blk),
                send_sem=send_sems.at[7 * a + k], recv_sem=recv_sems.at[7 * a + k],
                device_id=to, device_id_type=MESH_ID)

        mine = [pltpu.make_async_copy(x_refs[a], slot(a, *me), local_sems.at[a]) for a in range(n)]
        for cp in mine:
            cp.start()
        first = [copy(a, 0, me, sibling, src=x_refs[a]) for a in range(n)]
        first += [copy(a, 1 + j, me, (*chip, c), src=x_refs[a]) for a in range(n) for j, chip in enumerate(chips)]
        for cp in first:
            cp.start()
        passed = []
        for j, chip in enumerate(chips):
            for a in range(n):
                copy(a, 1 + j, (*chip, c), me).wait_recv()
                passed.append(copy(a, 4 + j, (*chip, c), sibling))
                passed[-1].start()
        for a in range(n):
            copy(a, 0, sibling, me).wait_recv()
        for j, chip in enumerate(chips):
            for a in range(n):
                copy(a, 4 + j, (*chip, 1 - c), me).wait_recv()
        for cp in first + passed:
            cp.wait_send()
        for cp in mine:
            cp.wait()

    return pl.pallas_call(
        body,
        out_shape=[jax.ShapeDtypeStruct((N_DEV,) + b.shape, b.dtype) for b in blocks],
        in_specs=[pl.BlockSpec(memory_space=pl.ANY)] * n,
        out_specs=[pl.BlockSpec(memory_space=pl.ANY)] * n,
        scratch_shapes=[pltpu.SemaphoreType.DMA((7 * n,)), pltpu.SemaphoreType.DMA((7 * n,)),
                        pltpu.SemaphoreType.DMA((n,))],
        name="gather_weights",
    )(*blocks)


def _adamw(g, w, m, v):
    c1 = 1.0 / (1.0 - ADAM_B1 ** ADAM_STEP)
    c2 = 1.0 / (1.0 - ADAM_B2 ** ADAM_STEP)
    nm = ADAM_B1 * m + (1.0 - ADAM_B1) * g
    nv = ADAM_B2 * v + (1.0 - ADAM_B2) * (g * g)
    d = -ADAM_LR * ((nm * c1) / (jnp.sqrt(nv * c2) + ADAM_EPS) + ADAM_WD * w)
    return d, nm, nv


def _sum8(r_ref):
    g = r_ref[0].astype(F32)
    for k in range(1, r_ref.shape[0]):
        g = g + r_ref[k].astype(F32)
    return g


def _sum_adamw_w_in(recv, w, m, v):
    rows, _, cols = w.shape
    tc = 256

    def body(r_ref, w_ref, m_ref, v_ref, g_ref, d_ref, nm_ref, nv_ref):
        g = _sum8(r_ref)
        dense = lambda ref: ref[...].reshape(rows, tc)
        d, nm, nv = _adamw(g, dense(w_ref), dense(m_ref), dense(v_ref))
        for ref, val in ((g_ref, g), (d_ref, d), (nm_ref, nm), (nv_ref, nv)):
            ref[...] = val.reshape(rows, 1, tc)

    blk = pl.BlockSpec((rows, 1, tc), lambda i: (0, 0, i))
    out = jax.ShapeDtypeStruct((rows, 1, cols), F32)
    return pl.pallas_call(
        body,
        grid=(cols // tc,),
        in_specs=[pl.BlockSpec((recv.shape[0], rows, tc), lambda i: (0, 0, i)), blk, blk, blk],
        out_specs=[blk, blk, blk, blk],
        out_shape=[out, out, out, out],
        compiler_params=_params(("arbitrary",)),
        name="sum_adamw_w_in",
    )(recv, w, m, v)


def _sum_adamw_whole(recvs, ws, ms, vs):
    n = len(ws)

    def body(*refs):
        r_refs, w_refs, m_refs, v_refs = refs[:n], refs[n:2 * n], refs[2 * n:3 * n], refs[3 * n:4 * n]
        outs = refs[4 * n:]
        for a in range(n):
            g = _sum8(r_refs[a])
            d, nm, nv = _adamw(g, w_refs[a][...], m_refs[a][...], v_refs[a][...])
            outs[a][...] = g
            outs[n + a][...] = d
            outs[2 * n + a][...] = nm
            outs[3 * n + a][...] = nv

    shapes = [jax.ShapeDtypeStruct(w.shape, F32) for w in ws]
    res = pl.pallas_call(
        body,
        out_shape=shapes * 4,
        compiler_params=pltpu.CompilerParams(vmem_limit_bytes=48 * 2**20),
        name="sum_adamw_mats",
    )(*recvs, *ws, *ms, *vs)
    return res[:n], res[n:2 * n], res[2 * n:3 * n], res[3 * n:]


def _vectors_update(small, w, m, v):
    def body(small_ref, w_ref, m_ref, v_ref, g_ref, d_ref, nm_ref, nv_ref, got, send_sems, recv_sems):
        x, y, c = _my_place()
        me = 4 * x + 2 * y + c
        got[me] = small_ref[...]
        copies = [pltpu.make_async_remote_copy(
            src_ref=small_ref, dst_ref=got.at[me], send_sem=send_sems.at[k], recv_sem=recv_sems.at[k],
            device_id=_flip(k, x, y, c), device_id_type=MESH_ID) for k in range(N_DEV - 1)]
        _start_all([], copies)
        _wait_all([], copies)
        g = _sum8(got)
        g_ref[...] = g
        d_ref[...], nm_ref[...], nv_ref[...] = _adamw(g, w_ref[...], m_ref[...], v_ref[...])

    out = jax.ShapeDtypeStruct(small.shape, F32)
    return pl.pallas_call(
        body,
        out_shape=[out, out, out, out],
        scratch_shapes=[pltpu.VMEM((N_DEV,) + small.shape, F32), pltpu.SemaphoreType.DMA((7,)),
                        pltpu.SemaphoreType.DMA((7,))],
        name="vectors_update",
    )(small, w, m, v)


MATS = ("w_uq", "w_ukv", "w_branch_a", "w_branch_b", "w_out")
SMALL = ("g_pre", "b_gate", "g_q", "g_kv", "lb_logits", "g_hgrn", "g_post")
SMALL_ROWS = (1, 2, 1, 1, 1, 1, 1)
LOSS_AT = (6, 1023)
SMALL_SHAPE = dict(g_pre=(1, 1024), b_gate=(1, 2048), g_q=(1, 768), g_kv=(1, 256), lb_logits=(2, 512),
                   g_hgrn=(1, 64), g_post=(1, 1024))
COL_SHARDED = dict(w_uq=False, w_ukv=True, w_branch_a=True, w_branch_b=True, w_out=False)
ORDER = ("g_pre", "w_in", "b_gate", "g_q", "w_uq", "g_kv", "w_ukv", "lb_logits", "g_hgrn",
         "w_branch_a", "w_branch_b", "w_out", "g_post")


def _pack_small(t):
    parts = []
    for n, r in zip(SMALL, SMALL_ROWS):
        flat = t[n].reshape(1, -1)
        parts.append(jnp.pad(flat, ((0, 0), (0, r * 1024 - flat.shape[1]))).reshape(r, 1024))
    return jnp.concatenate(parts, axis=0)


def _unpack_small(p):
    out, r0 = {}, 0
    for n, r in zip(SMALL, SMALL_ROWS):
        shp = SMALL_SHAPE[n]
        out[n] = p[r0:r0 + r].reshape(1, -1)[:, :shp[0] * shp[1]].reshape(shp)
        r0 += r
    return out


def _to_slots(name, full):
    r, c = full.shape
    if COL_SHARDED[name]:
        return full.reshape(r, N_DEV, c // N_DEV).transpose(1, 0, 2)
    return full.reshape(N_DEV, r // N_DEV, c)


def _from_slots(name, slots):
    _, r, c = slots.shape
    if COL_SHARDED[name]:
        return slots.transpose(1, 0, 2).reshape(r, N_DEV * c)
    return slots.reshape(N_DEV * r, c)


def kernel(x, g_pre, w_in, b_gate, g_q, w_uq, g_kv, w_ukv, lb_logits, g_hgrn, w_branch_a, w_branch_b, w_out, g_post, loss_target, m_g_pre, m_w_in, m_b_gate, m_g_q, m_w_uq, m_g_kv, m_w_ukv, m_lb_logits, m_g_hgrn, m_w_branch_a, m_w_branch_b, m_w_out, m_g_post, v_g_pre, v_w_in, v_b_gate, v_g_q, v_w_uq, v_g_kv, v_w_ukv, v_lb_logits, v_g_hgrn, v_w_branch_a, v_w_branch_b, v_w_out, v_g_post):
    rows3 = lambda a: jnp.transpose(a, (2, 0, 1))
    w = dict(w_in=rows3(w_in), w_uq=w_uq[0], w_ukv=w_ukv[0], w_branch_a=w_branch_a[0], w_branch_b=w_branch_b[0],
             w_out=w_out[0], g_pre=g_pre, b_gate=b_gate, g_q=g_q, g_kv=g_kv, lb_logits=lb_logits, g_hgrn=g_hgrn,
             g_post=g_post)
    mom = dict(w_in=rows3(m_w_in), w_uq=m_w_uq[0], w_ukv=m_w_ukv[0], w_branch_a=m_w_branch_a[0],
               w_branch_b=m_w_branch_b[0], w_out=m_w_out[0], g_pre=m_g_pre, b_gate=m_b_gate, g_q=m_g_q, g_kv=m_g_kv,
               lb_logits=m_lb_logits, g_hgrn=m_g_hgrn, g_post=m_g_post)
    var = dict(w_in=rows3(v_w_in), w_uq=v_w_uq[0], w_ukv=v_w_ukv[0], w_branch_a=v_w_branch_a[0],
               w_branch_b=v_w_branch_b[0], w_out=v_w_out[0], g_pre=v_g_pre, b_gate=v_b_gate, g_q=v_g_q, g_kv=v_g_kv,
               lb_logits=v_lb_logits, g_hgrn=v_g_hgrn, g_post=v_g_post)

    (w_in_slots,) = _all_gather([w["w_in"].reshape(W_IN_SHARD, D_MODEL).astype(BF16)])
    dx, recv_in, recv, small = _step(x[0], loss_target[0], w_in_slots, [w[n].astype(BF16) for n in MATS],
                                     g_pre, b_gate, g_q, g_kv, lb_logits, g_hgrn, g_post)

    g_in, d_in, m_in, v_in = _sum_adamw_w_in(recv_in, w["w_in"], mom["w_in"], var["w_in"])
    pk = lambda t: [t[n] for n in MATS]
    res = _sum_adamw_whole([recv[n] for n in MATS], pk(w), pk(mom), pk(var))
    vec = _vectors_update(small, _pack_small(w), _pack_small(mom), _pack_small(var))

    outs = []
    for mats, packed, big in zip(res, vec, (g_in, d_in, m_in, v_in)):
        t = {**{n: a[None] for n, a in zip(MATS, mats)}, **_unpack_small(packed),
             "w_in": jnp.transpose(big, (1, 2, 0))}
        outs += [t[n] for n in ORDER]
    total = vec[0][LOSS_AT]
    return (total, dx[None], *outs)
```

```python
import math

import jax
import jax.numpy as jnp
import numpy as np
from jax import lax
from jax.experimental import pallas as pl
from jax.experimental.pallas import tpu as pltpu

F32, BF16 = jnp.float32, jnp.bfloat16

D_MODEL = 1024
EPS = 1e-6
HEADS = 8
NOPE, ROPE, VDIM = 64, 32, 64
QK = NOPE + ROPE
Q_LORA, KV_LORA = 768, 256
ROPE_THETA = 10000.0
ATT_CHUNK_SHIFT = 6
HG_BLOCK = 32
HG_WIDTH = 512
D_IN = 5664
D_IN_PAD = 5760
W_IN_SHARD = D_IN // 8
N_DEV = 8
LANE = 128

ADAM_LR, ADAM_B1, ADAM_B2, ADAM_EPS, ADAM_WD, ADAM_STEP = 0.001, 0.9, 0.999, 1e-08, 0.01, 10

W_IN_SEGMENTS = ((3616, 5664, 0), (1056, 1568, 2048), (3104, 3616, 2560), (1568, 3104, 3072),
                 (0, 1024, 4608), (1024, 1056, 5696))
W_IN_ZERO = ((5632, 5696), (5728, 5760))

NT = (((1,), (1,)), ((), ()))
TN = (((0,), (0,)), ((), ()))
MESH_ID = pl.DeviceIdType.MESH


def _w_in_pieces():
    out = []
    for lo, hi, dst in W_IN_SEGMENTS:
        c = lo
        while c < hi:
            p = c // W_IN_SHARD
            e = min(hi, (p + 1) * W_IN_SHARD)
            out.append((p, c - p * W_IN_SHARD, e - p * W_IN_SHARD, dst + c - lo))
            c = e
    return out


def _params(sem, vmem_mb=48):
    return pltpu.CompilerParams(dimension_semantics=sem, vmem_limit_bytes=vmem_mb * 2**20)


def _dot(a, b):
    return jnp.dot(a, b, preferred_element_type=F32)


def _dotg(a, b, dims):
    return lax.dot_general(a, b, dims, preferred_element_type=F32)


def _split2(x):
    hi = x.astype(BF16)
    return hi, (x - hi.astype(F32)).astype(BF16)


def _sel_left(m01, x):
    hi, lo = _split2(x)
    return _dot(m01, hi) + _dot(m01, lo)


def _sel_right(x, m01):
    hi, lo = _split2(x)
    return _dot(hi, m01) + _dot(lo, m01)


def _hi_lo(x):
    hi = x.astype(BF16).astype(F32)
    return hi, x - hi


def _sigmoid(x):
    return 0.5 * jnp.tanh(0.5 * x) + 0.5


def _rope(x, c, s1, s2):
    return x * c + pltpu.roll(x, 112, 1) * s1 + pltpu.roll(x, 16, 1) * s2


def _unrope(d, c, s1, s2):
    return d * c + pltpu.roll(d * s1, 16, 1) + pltpu.roll(d * s2, 112, 1)


def _my_place():
    return lax.axis_index("x"), lax.axis_index("y"), lax.axis_index("c")


def _flip(k, x, y, c):
    fx, fy, fc = (k + 1) >> 2 & 1, (k + 1) >> 1 & 1, (k + 1) & 1
    return (1 - x if fx else x), (1 - y if fy else y), (1 - c if fc else c)


def _to_all_copies(s_refs, r_refs, sems, spread):
    send_sems, recv_sems, local_sems = sems
    x, y, c = _my_place()
    me = 4 * x + 2 * y + c
    src = (lambda a, p: s_refs[a]) if spread else (lambda a, p: s_refs[a].at[p])
    local = [pltpu.make_async_copy(src(a, me), r_refs[a].at[me], local_sems.at[a]) for a in range(len(s_refs))]
    remote = []
    for k in range(N_DEV - 1):
        px, py, pc = _flip(k, x, y, c)
        for a in range(len(s_refs)):
            remote.append(pltpu.make_async_remote_copy(
                src_ref=src(a, 4 * px + 2 * py + pc), dst_ref=r_refs[a].at[me],
                send_sem=send_sems.at[7 * a + k], recv_sem=recv_sems.at[7 * a + k],
                device_id=(px, py, pc), device_id_type=MESH_ID))
    return local, remote


def _to_chips_copies(s_refs, r_refs, sems):
    send_sems, recv_sems, local_sems = sems
    x, y, c = _my_place()
    me = 2 * x + y
    local = [pltpu.make_async_copy(s_refs[a].at[me], r_refs[a].at[me], local_sems.at[a]) for a in range(len(s_refs))]
    remote = []
    for k in range(3):
        px = 1 - x if (k + 1) >> 1 & 1 else x
        py = 1 - y if (k + 1) & 1 else y
        for a in range(len(s_refs)):
            remote.append(pltpu.make_async_remote_copy(
                src_ref=s_refs[a].at[2 * px + py], dst_ref=r_refs[a].at[me],
                send_sem=send_sems.at[3 * a + k], recv_sem=recv_sems.at[3 * a + k],
                device_id=(px, py, c), device_id_type=MESH_ID))
    return local, remote


def _start_all(local, remote):
    for cp in local + remote:
        cp.start()


def _wait_all(local, remote):
    for cp in remote:
        cp.wait_recv()
    for cp in remote:
        cp.wait_send()
    for cp in local:
        cp.wait()


def _copy_sems(n, peers):
    return [pltpu.SemaphoreType.DMA((peers * n,)), pltpu.SemaphoreType.DMA((peers * n,)),
            pltpu.SemaphoreType.DMA((n,))]


ANY = pl.BlockSpec(memory_space=pl.ANY)


def _assemble_w_in(slots):
    tc = 256
    pieces = _w_in_pieces()

    def body(s_ref, w_ref, wt_ref):
        for lo, hi in W_IN_ZERO:
            wt_ref[lo:hi, :] = jnp.zeros((hi - lo, tc), BF16)
        for p, lo, hi, dst in pieces:
            wt_ref[dst:dst + hi - lo, :] = s_ref[p, lo:hi, :]
        w_ref[...] = wt_ref[...].T

    return pl.pallas_call(
        body,
        grid=(D_MODEL // tc,),
        in_specs=[pl.BlockSpec((N_DEV, W_IN_SHARD, tc), lambda i: (0, 0, i))],
        out_specs=[pl.BlockSpec((tc, D_IN_PAD), lambda i: (i, 0)), pl.BlockSpec((D_IN_PAD, tc), lambda i: (0, i))],
        out_shape=[jax.ShapeDtypeStruct((D_MODEL, D_IN_PAD), BF16), jax.ShapeDtypeStruct((D_IN_PAD, D_MODEL), BF16)],
        compiler_params=_params(("arbitrary",)),
        name="assemble_w_in",
    )(slots)


def _scatter_w_in(dw):
    tc = 256
    pieces = _w_in_pieces()

    def body(d_ref, s_ref):
        dt = d_ref[...].T
        for p, lo, hi, dst in pieces:
            s_ref[p, lo:hi, :] = dt[dst:dst + hi - lo, :].astype(BF16)

    return pl.pallas_call(
        body,
        grid=(D_MODEL // tc,),
        in_specs=[pl.BlockSpec((tc, D_IN_PAD), lambda i: (i, 0))],
        out_specs=pl.BlockSpec((N_DEV, W_IN_SHARD, tc), lambda i: (0, 0, i)),
        out_shape=jax.ShapeDtypeStruct((N_DEV, W_IN_SHARD, D_MODEL), BF16),
        compiler_params=_params(("arbitrary",)),
        name="scatter_w_in",
    )(dw)


def _norm_proj(x, g_pre, w, shards):
    s, n = x.shape[0], w.shape[1]
    tm, tn = 512, 1152
    ni, nj, ns = s // tm, n // tn, len(shards)

    def body(x_ref, g_ref, w_ref, *rest):
        shard_refs, (proj_ref, ht_ref), got_refs = rest[:ns], rest[ns:ns + 2], rest[ns + 2:2 * ns + 2]
        h_ref, sems = rest[2 * ns + 2], rest[2 * ns + 3:]
        j, i = pl.program_id(0), pl.program_id(1)
        rows = pl.ds(pl.multiple_of(i * tm, tm), tm)

        @pl.when((j == 0) & (i == 0))
        def _():
            _start_all(*_to_all_copies(shard_refs, got_refs, sems, True))

        @pl.when(j == 0)
        def _():
            xv = x_ref[...]
            r = lax.rsqrt(jnp.mean(xv * xv, axis=-1, keepdims=True) + EPS)
            h = (xv * r * g_ref[...]).astype(BF16)
            h_ref[rows, :] = h
            ht_ref[...] = h.T

        proj_ref[...] = _dot(h_ref[rows, :], w_ref[...])

        @pl.when((j == nj - 1) & (i == ni - 1))
        def _():
            _wait_all(*_to_all_copies(shard_refs, got_refs, sems, True))

    first = lambda j, i: jnp.where(j == 0, i, ni - 1)
    return pl.pallas_call(
        body,
        grid=(nj, ni),
        in_specs=[
            pl.BlockSpec((tm, D_MODEL), lambda j, i: (first(j, i), 0)),
            pl.BlockSpec((1, D_MODEL), lambda j, i: (0, 0)),
            pl.BlockSpec((D_MODEL, tn), lambda j, i: (0, j)),
        ] + [ANY] * ns,
        out_specs=[
            pl.BlockSpec((tm, tn), lambda j, i: (i, j)),
            pl.BlockSpec((D_MODEL, tm), lambda j, i: (0, first(j, i))),
        ] + [ANY] * ns,
        out_shape=[jax.ShapeDtypeStruct((s, n), F32), jax.ShapeDtypeStruct((D_MODEL, s), BF16)]
        + [jax.ShapeDtypeStruct((N_DEV,) + b.shape, b.dtype) for b in shards],
        scratch_shapes=[pltpu.VMEM((s, D_MODEL), BF16)] + _copy_sems(ns, 7),
        compiler_params=_params(("arbitrary", "arbitrary")),
        name="norm_proj",
    )(x, g_pre, w, *shards)


def _mla_prep(proj, g_q, g_kv, w_uq_p, w_kv_p, rc, rs1, rs2):
    s = proj.shape[0]
    tm = 256
    scale = 1.0 / math.sqrt(QK)

    def body(cq_ref, ckv_ref, kpe_ref, gq_ref, gkv_ref, wuq_ref, wkv_ref, c_ref, s1_ref, s2_ref,
             qr_ref, kr_ref, v_ref, cqt_ref, ckvt_ref):
        cq = cq_ref[...]
        r = lax.rsqrt(jnp.mean(cq * cq, axis=-1, keepdims=True) + EPS)
        cqn = (cq * r * gq_ref[...]).astype(BF16)
        cqt_ref[...] = cqn.T
        q = _dot(cqn, wuq_ref[...])
        ckv = ckv_ref[...]
        r = lax.rsqrt(jnp.mean(ckv * ckv, axis=-1, keepdims=True) + EPS)
        ckvn = (ckv * r * gkv_ref[...]).astype(BF16)
        ckvt_ref[...] = ckvn.T
        kv = _dot(ckvn, wkv_ref[...])
        c, s1, s2 = c_ref[...], s1_ref[...], s2_ref[...]
        lane = lax.broadcasted_iota(jnp.int32, (tm, LANE), 1)
        kpe = _rope(kpe_ref[...], c, s1, s2) + jnp.where((lane == QK) | (lane == QK + 1), 1.0, 0.0)
        vone = jnp.where((lane == VDIM) | (lane == VDIM + 1), 1.0, 0.0)
        for h in range(HEADS):
            sl = slice(LANE * h, LANE * (h + 1))
            qr_ref[:, sl] = (_rope(q[:, sl], c, s1, s2) * scale).astype(BF16)
            kr_ref[:, sl] = (kv[:, sl] + kpe).astype(BF16)
            v_ref[:, sl] = (kv[:, HEADS * LANE + LANE * h:HEADS * LANE + LANE * (h + 1)] + vone).astype(BF16)

    row = lambda w, j: pl.BlockSpec((tm, w), lambda i: (i, j))
    col = lambda w: pl.BlockSpec((w, tm), lambda i: (0, i))
    full = lambda a: pl.BlockSpec(a.shape, lambda i: (0, 0))
    return pl.pallas_call(
        body,
        grid=(s // tm,),
        in_specs=[row(768, 6), row(256, 21), row(128, 44), full(g_q), full(g_kv), full(w_uq_p), full(w_kv_p),
                  row(128, 0), row(128, 0), row(128, 0)],
        out_specs=[row(1024, 0), row(1024, 0), row(1024, 0), col(768), col(256)],
        out_shape=[jax.ShapeDtypeStruct((s, 1024), BF16), jax.ShapeDtypeStruct((s, 1024), BF16),
                   jax.ShapeDtypeStruct((s, 1024), BF16), jax.ShapeDtypeStruct((768, s), BF16),
                   jax.ShapeDtypeStruct((256, s), BF16)],
        compiler_params=_params(("arbitrary",)),
        name="mla_prep",
    )(proj, proj, proj, g_q, g_kv, w_uq_p, w_kv_p, rc, rs1, rs2)


ATT_T = 512
ATT_FWD_HEADS = 4


def _chunk_mask(transposed):
    r = lax.broadcasted_iota(jnp.int32, (ATT_T, ATT_T), 0) >> ATT_CHUNK_SHIFT
    c = lax.broadcasted_iota(jnp.int32, (ATT_T, ATT_T), 1) >> ATT_CHUNK_SHIFT
    return (r <= c) if transposed else (c <= r)


def _attn_fwd(qr, kr, vp):
    s = qr.shape[0]
    t = ATT_T
    g = ATT_FWD_HEADS

    def body(q_ref, k_ref, v_ref, o_ref, qa_ref, sc_ref):
        qi = pl.program_id(1)
        lane = lax.broadcasted_iota(jnp.int32, (t, LANE), 1)
        sls = [slice(LANE * a, LANE * (a + 1)) for a in range(g)]
        qs = [q_ref[:, sl] for sl in sls]

        def scores(j):
            rows = pl.ds(pl.multiple_of(j * t, t), t)
            for a in range(g):
                sc_ref[j & 1, a] = _dotg(qs[a], k_ref[rows, sls[a]], NT)

        def step(j, carry, masked):
            rows = pl.ds(pl.multiple_of(j * t, t), t)
            out = []
            for a in range(g):
                m, acc = carry[a]
                sc = sc_ref[j & 1, a]
                if masked:
                    sc = jnp.where(_chunk_mask(False), sc, -1e30)
                m_new = jnp.maximum(m, jnp.max(sc, axis=-1, keepdims=True))
                p = jnp.exp(sc - m_new).astype(BF16)
                acc = jnp.exp(m - m_new) * acc + _dot(p, v_ref[rows, sls[a]])
                out.append((m_new, acc))
            return tuple(out)

        def loop(j, carry):
            carry = step(j, carry, False)
            scores(j + 1)
            return carry

        init = tuple((jnp.full((t, 1), -1e30, F32), jnp.zeros((t, LANE), F32)) for _ in range(g))
        scores(0)
        carry = lax.fori_loop(0, qi, loop, init)
        carry = step(qi, carry, True)
        outs = []
        for a in range(g):
            m, acc = carry[a]
            l = acc[:, VDIM:VDIM + 1]
            outs.append(acc / l)
            hi, lo_part = _hi_lo(-(m + jnp.log(l)))
            qa = jnp.where(lane == QK, hi, jnp.where(lane == QK + 1, lo_part, qs[a].astype(F32)))
            qa_ref[:, sls[a]] = qa.astype(BF16)
        for p in range(g // 2):
            o_ref[:, LANE * p:LANE * (p + 1)] = jnp.where(lane < VDIM, outs[2 * p], pltpu.roll(outs[2 * p + 1], VDIM, 1))

    return pl.pallas_call(
        body,
        grid=(HEADS // g, s // t),
        in_specs=[
            pl.BlockSpec((t, g * LANE), lambda h, i: (i, h)),
            pl.BlockSpec((s, g * LANE), lambda h, i: (0, h)),
            pl.BlockSpec((s, g * LANE), lambda h, i: (0, h)),
        ],
        out_specs=[
            pl.BlockSpec((t, g * VDIM), lambda h, i: (i, h)),
            pl.BlockSpec((t, g * LANE), lambda h, i: (i, h)),
        ],
        out_shape=[jax.ShapeDtypeStruct((s, 512), F32), jax.ShapeDtypeStruct((s, 1024), BF16)],
        scratch_shapes=[pltpu.VMEM((2, g, t, t), F32)],
        compiler_params=_params(("arbitrary", "arbitrary")),
        name="attn_fwd",
    )(qr, kr, vp)


def _attn_bwd(qa, kr, vp, dop, sends):
    s = qa.shape[0]
    t = ATT_T
    nq = s // t
    ns = len(sends)

    def body(q_ref, k_ref, v_ref, do_ref, *rest):
        send_refs, (dq_ref, dk_ref, dv_ref) = rest[:ns], rest[ns:ns + 3]
        recv_refs, sems = rest[ns + 3:2 * ns + 3], rest[2 * ns + 3:]
        j = pl.program_id(1)
        sls = [slice(LANE * a, LANE * (a + 1)) for a in range(2)]

        @pl.when((pl.program_id(0) == 0) & (j == 0))
        def _():
            _start_all(*_to_all_copies(send_refs, recv_refs, sems, False))

        @pl.when(j == 0)
        def _():
            dq_ref[...] = jnp.zeros_like(dq_ref)

        dk_ref[...] = jnp.zeros_like(dk_ref)
        dv_ref[...] = jnp.zeros_like(dv_ref)
        ks = [k_ref[:, sl] for sl in sls]
        vs = [v_ref[:, sl] for sl in sls]

        def step(i, masked):
            rows = pl.ds(pl.multiple_of(i * t, t), t)
            for a in range(2):
                q = q_ref[rows, sls[a]]
                do = do_ref[rows, sls[a]]
                sc = _dotg(ks[a], q, NT)
                if masked:
                    sc = jnp.where(_chunk_mask(True), sc, -1e30)
                p = jnp.exp(sc)
                ds = (p * _dotg(vs[a], do, NT)).astype(BF16)
                dv_ref[:, sls[a]] += _dot(p.astype(BF16), do)
                dk_ref[:, sls[a]] += _dot(ds, q)
                dq_ref[rows, sls[a]] += _dotg(ds, ks[a], TN)

        step(j, True)

        def loop(i, c):
            step(i, False)
            return c

        lax.fori_loop(j + 1, nq, loop, 0)

        @pl.when((pl.program_id(0) == HEADS // 2 - 1) & (j == nq - 1))
        def _():
            _wait_all(*_to_all_copies(send_refs, recv_refs, sems, False))

    blk = pl.BlockSpec((t, 2 * LANE), lambda h, j: (j, h))
    whole = pl.BlockSpec((s, 2 * LANE), lambda h, j: (0, h))
    out = jax.ShapeDtypeStruct((s, 1024), F32)
    return pl.pallas_call(
        body,
        grid=(HEADS // 2, nq),
        in_specs=[whole, blk, blk, whole] + [ANY] * ns,
        out_specs=[whole, blk, blk] + [ANY] * ns,
        out_shape=[out, out, out] + [jax.ShapeDtypeStruct(a.shape, a.dtype) for a in sends],
        scratch_shapes=_copy_sems(ns, 7),
        compiler_params=_params(("arbitrary", "arbitrary")),
        name="attn_bwd",
    )(qa, kr, vp, dop, *sends)


HG_T = 256
HG_NC = HG_T // HG_BLOCK


def _hg_consts():
    r = jnp.arange(HG_T)[:, None]
    c = jnp.arange(HG_T)[None, :]
    same = (r // HG_BLOCK) == (c // HG_BLOCK)
    mcum = (same & (c <= r)).astype(BF16)
    mrev = (same & (c >= r)).astype(BF16)
    msum = same.astype(BF16)
    a = jnp.arange(LANE)
    bd = ((a[:, None] < 64) == (a[None, :] < 64)).astype(F32)
    return mcum, mrev, msum, bd


def _hg_pre(hq, hf, lbl, mcum, msum):
    lb = _sigmoid(lbl[0:1, :] - lbl[1:2, :])
    sig = _sigmoid(hf)
    f = lb + (1.0 - lb) * sig
    lf = jnp.log(f)
    b = _sel_left(mcum, lf)
    big_l = _sel_left(msum, lf)
    k = 1.0 - f
    qd = hq * jnp.exp(b)
    ki = k * jnp.exp(-b)
    ke = k * jnp.exp(big_l - b)
    return lb, sig, f, b, big_l, qd, ki, ke


def _stack_pair(xp, lo):
    return jnp.concatenate([jnp.where(lo, xp, 0.0), jnp.where(lo, 0.0, xp)], axis=0)


def _hgrn_fwd(proj, lbl):
    s = proj.shape[0]
    t = HG_T
    mcum, _, msum, bd = _hg_consts()

    def body(hq_ref, hf_ref, hi_ref, lbl_ref, mcum_ref, msum_ref, bd_ref, o_ref, sp_ref, st_ref):
        @pl.when(pl.program_id(0) == 0)
        def _():
            st_ref[...] = jnp.zeros_like(st_ref)

        mc = mcum_ref[...]
        _, _, _, _, big_l, qd, ki, ke = _hg_pre(hq_ref[...], hf_ref[...], lbl_ref[...], mc, msum_ref[...])
        el = jnp.exp(big_l)
        hi = hi_ref[...]
        lo = lax.broadcasted_iota(jnp.int32, (t, LANE), 1) < 64
        mask2 = jnp.concatenate([mc, mc], axis=0) > 0.5
        for p in range(HEADS // 2):
            sl = slice(LANE * p, LANE * (p + 1))
            vp = hi[:, sl].astype(BF16)
            q2 = _stack_pair(qd[:, sl], lo).astype(BF16)
            a2 = jnp.where(mask2, _dotg(q2, ki[:, sl].astype(BF16), NT), 0.0)
            r2 = _dot(a2.astype(BF16), vp)
            o_intra = jnp.where(lo, r2[:t], r2[t:])
            qb = qd[:, sl].astype(BF16)
            kb = ke[:, sl].astype(BF16)
            st = st_ref[p]
            for c in range(HG_NC):
                rows = slice(HG_BLOCK * c, HG_BLOCK * (c + 1))
                sp_ref[c, :, sl] = st
                o_ref[rows, sl] = o_intra[rows] + _dotg(qb[rows], st.astype(BF16), NT)
                u = _dotg(vp[rows], kb[rows], TN) * bd_ref[...]
                st = st * el[HG_BLOCK * c:HG_BLOCK * c + 1, sl] + u
            st_ref[p] = st

    row = lambda j: pl.BlockSpec((t, HG_WIDTH), lambda i: (i, j))
    full = lambda a: pl.BlockSpec(a.shape, lambda i: (0, 0))
    return pl.pallas_call(
        body,
        grid=(s // t,),
        in_specs=[row(6), row(7), row(8), full(lbl), full(mcum), full(msum), full(bd)],
        out_specs=[row(0), pl.BlockSpec((HG_NC, LANE, HG_WIDTH), lambda i: (i, 0, 0))],
        out_shape=[jax.ShapeDtypeStruct((s, HG_WIDTH), F32),
                   jax.ShapeDtypeStruct((s // HG_BLOCK, LANE, HG_WIDTH), F32)],
        scratch_shapes=[pltpu.VMEM((HEADS // 2, LANE, LANE), F32)],
        compiler_params=_params(("arbitrary",)),
        name="hgrn_fwd",
    )(proj, proj, proj, lbl, mcum, msum, bd)


def _hgrn_bwd(proj, lbl, do, sprev, dproj):
    s = proj.shape[0]
    t = HG_T
    nt = s // t
    mcum, mrev, msum, bd = _hg_consts()

    def body(hq_ref, hf_ref, hi_ref, lbl_ref, do_ref, sp_ref, mcum_ref, mrev_ref, msum_ref, bd_ref,
             dproj_in, dh_ref, dlbl_ref, g_ref):
        del dproj_in

        @pl.when(pl.program_id(0) == 0)
        def _():
            g_ref[...] = jnp.zeros_like(g_ref)
            dlbl_ref[...] = jnp.zeros_like(dlbl_ref)

        mc = mcum_ref[...]
        lb, sig, f, b, big_l, qd, ki, ke = _hg_pre(hq_ref[...], hf_ref[...], lbl_ref[...], mc, msum_ref[...])
        el = jnp.exp(big_l)
        hi = hi_ref[...]
        dov = do_ref[...]
        lo = lax.broadcasted_iota(jnp.int32, (t, LANE), 1) < 64
        mask2 = jnp.concatenate([mc, mc], axis=0) > 0.5
        dqd_parts, dke_parts, dv_parts, del_parts, dki_parts = [], [], [], [], []
        for p in range(HEADS // 2):
            sl = slice(LANE * p, LANE * (p + 1))
            vp = hi[:, sl].astype(BF16)
            q2 = _stack_pair(qd[:, sl], lo).astype(BF16)
            kip = ki[:, sl].astype(BF16)
            do2 = _stack_pair(dov[:, sl], lo).astype(BF16)
            a2 = jnp.where(mask2, _dotg(q2, kip, NT), 0.0).astype(BF16)
            da2 = jnp.where(mask2, _dotg(do2, vp, NT), 0.0).astype(BF16)
            r2 = _dot(da2, kip)
            dki_parts.append(_dotg(da2, q2, TN))
            qb = qd[:, sl].astype(BF16)
            kb = ke[:, sl].astype(BF16)
            dob = dov[:, sl].astype(BF16)
            g = g_ref[p]
            dqd_c, dv_c, dke_c, del_c = [], [], [], []
            for c in range(HG_NC - 1, -1, -1):
                rows = slice(HG_BLOCK * c, HG_BLOCK * (c + 1))
                gb = g.astype(BF16)
                st = sp_ref[c, :, sl]
                dqd_c.append(_dot(dob[rows], st.astype(BF16)))
                dv_c.append(_dotg(kb[rows], gb, NT))
                dke_c.append(_dot(vp[rows], gb))
                del_c.append(jnp.broadcast_to(jnp.sum(g * st, axis=0, keepdims=True), (HG_BLOCK, LANE)))
                g = g * el[HG_BLOCK * c:HG_BLOCK * c + 1, sl] + _dotg(dob[rows], qb[rows], TN) * bd_ref[...]
            g_ref[p] = g
            up = lambda parts: jnp.concatenate(parts[::-1], axis=0)
            dqd_parts.append(jnp.where(lo, r2[:t], r2[t:]) + up(dqd_c))
            dv_parts.append(_dotg(a2, do2, TN) + up(dv_c))
            dke_parts.append(up(dke_c))
            del_parts.append(up(del_c))
        wide = lambda parts: jnp.concatenate(parts, axis=1)
        dqd, dke, dki, dvv, del_rows = wide(dqd_parts), wide(dke_parts), wide(dki_parts), wide(dv_parts), wide(del_parts)
        dh_ref[:, :HG_WIDTH] = (dqd * jnp.exp(b)).astype(BF16)
        dh_ref[:, 2 * HG_WIDTH:] = dvv.astype(BF16)
        dke_ke = dke * ke
        db = dqd * qd - dki * ki - dke_ke
        dl_rows = _sel_left(msum_ref[...], dke_ke) + del_rows * el
        is_last = (lax.broadcasted_iota(jnp.int32, (t, HG_WIDTH), 0) & (HG_BLOCK - 1)) == HG_BLOCK - 1
        db = db + jnp.where(is_last, dl_rows, 0.0)
        dlf = _sel_left(mrev_ref[...], db)
        dk = dki * jnp.exp(-b) + dke * jnp.exp(big_l - b)
        df = dlf / f - dk
        dh_ref[:, HG_WIDTH:2 * HG_WIDTH] = (df * (1.0 - lb) * sig * (1.0 - sig)).astype(BF16)
        dlb = jnp.sum(df * (1.0 - sig), axis=0, keepdims=True) * lb * (1.0 - lb)
        dlbl_ref[0:1, :] += dlb
        dlbl_ref[1:2, :] -= dlb

    rrow = lambda j: pl.BlockSpec((t, HG_WIDTH), lambda i: (nt - 1 - i, j))
    full = lambda a: pl.BlockSpec(a.shape, lambda i: (0, 0))
    return pl.pallas_call(
        body,
        grid=(nt,),
        in_specs=[rrow(6), rrow(7), rrow(8), full(lbl), rrow(0),
                  pl.BlockSpec((HG_NC, LANE, HG_WIDTH), lambda i: (nt - 1 - i, 0, 0)),
                  full(mcum), full(mrev), full(msum), full(bd), pl.BlockSpec(memory_space=pl.ANY)],
        out_specs=[pl.BlockSpec((t, 3 * HG_WIDTH), lambda i: (nt - 1 - i, 2)),
                   pl.BlockSpec((2, HG_WIDTH), lambda i: (0, 0))],
        out_shape=[jax.ShapeDtypeStruct(dproj.shape, BF16), jax.ShapeDtypeStruct((2, HG_WIDTH), F32)],
        input_output_aliases={10: 0},
        scratch_shapes=[pltpu.VMEM((HEADS // 2, LANE, LANE), F32)],
        compiler_params=_params(("arbitrary",)),
        name="hgrn_bwd",
    )(proj, proj, proj, lbl, do, sprev, mcum, mrev, msum, bd, dproj)


def _tail(x, tgt, proj, attn, o, w_a, w_b, w_out, w_at, w_bt, w_outt, b_gate, g_post, gh):
    s = x.shape[0]
    tm = 256
    ones64 = (jnp.arange(HG_WIDTH)[:, None] // 64 == jnp.arange(HG_WIDTH)[None, :] // 64).astype(BF16)
    weights = (w_a, w_b, w_out, w_at, w_bt, w_outt)

    def body(x_ref, t_ref, ml_ref, ga_ref, gb_ref, at_ref, o_ref, *rest):
        w_hbm, (bg_ref, gp_ref, gh_ref, ones_ref) = rest[:6], rest[6:10]
        (dout_ref, dpj_ref, dop_ref, do_ref, mt_ref, dy_ref, yat_ref, dya_ref, ybt_ref, dyb_ref,
         loss_ref, dgp_ref, dbg_ref, dgh_ref) = rest[10:24]
        (wa_ref, wb_ref, wo_ref, wat_ref, wbt_ref, wot_ref), w_sem = rest[24:30], rest[30]

        @pl.when(pl.program_id(0) == 0)
        def _():
            loads = [pltpu.make_async_copy(src, dst, w_sem.at[k])
                     for k, (src, dst) in enumerate(zip(w_hbm, rest[24:30]))]
            _start_all(loads, [])
            loss_ref[...] = jnp.zeros_like(loss_ref)
            dgp_ref[...] = jnp.zeros_like(dgp_ref)
            dbg_ref[...] = jnp.zeros_like(dbg_ref)
            dgh_ref[...] = jnp.zeros_like(dgh_ref)
            _wait_all(loads, [])

        ones = ones_ref[...]
        gate_a = ga_ref[...]
        sa = _sigmoid(gate_a)
        silu_a = gate_a * sa
        attn_v = at_ref[...]
        ya_in = attn_v * silu_a
        ov = o_ref[...]
        ro = lax.rsqrt(_sel_right(ov * ov, ones) * (1.0 / 64.0) + EPS)
        ohat = ov * ro
        ghv = gh_ref[...]
        on = ohat * ghv
        gate_b = gb_ref[...]
        sb = _sigmoid(gate_b)
        silu_b = gate_b * sb
        yb_in = on * silu_b
        ya_bf = ya_in.astype(BF16)
        yb_bf = yb_in.astype(BF16)
        yat_ref[...] = ya_bf.T
        ybt_ref[...] = yb_bf.T
        y_a = _dot(ya_bf, wa_ref[...])
        y_b = _dot(yb_bf, wb_ref[...])
        gts = _sigmoid(ml_ref[...] + bg_ref[...])
        g_a = gts[:, :D_MODEL]
        g_b = gts[:, D_MODEL:]
        m_bf = (g_a * y_a + g_b * y_b).astype(BF16)
        mt_ref[...] = m_bf.T
        y = _dot(m_bf, wo_ref[...])
        r1 = lax.rsqrt(jnp.mean(y * y, axis=-1, keepdims=True) + EPS)
        yn = y * r1
        gp = gp_ref[...]
        e = x_ref[...] + yn * gp - t_ref[...]
        loss_ref[...] += jnp.sum(e * e, axis=0, keepdims=True)
        dout = e * (1.0 / D_MODEL)
        dout_ref[...] = dout
        dgp_ref[...] += jnp.sum(dout * yn, axis=0, keepdims=True)
        dyn = dout * gp
        dy = r1 * (dyn - yn * jnp.mean(dyn * yn, axis=-1, keepdims=True))
        dy_bf = dy.astype(BF16)
        dy_ref[...] = dy_bf
        dm = _dot(dy_bf, wot_ref[...])
        dml_a = dm * y_a * g_a * (1.0 - g_a)
        dml_b = dm * y_b * g_b * (1.0 - g_b)
        dpj_ref[:, :D_MODEL] = dml_a.astype(BF16)
        dpj_ref[:, D_MODEL:2 * D_MODEL] = dml_b.astype(BF16)
        dbg_ref[:, :D_MODEL] += jnp.sum(dml_a, axis=0, keepdims=True)
        dbg_ref[:, D_MODEL:] += jnp.sum(dml_b, axis=0, keepdims=True)
        dya_bf = (dm * g_a).astype(BF16)
        dyb_bf = (dm * g_b).astype(BF16)
        dya_ref[...] = dya_bf
        dyb_ref[...] = dyb_bf
        dya_in = _dot(dya_bf, wat_ref[...])
        dyb_in = _dot(dyb_bf, wbt_ref[...])
        dattn = dya_in * silu_a
        delta = _sel_right(dattn * attn_v, ones)
        lane = lax.broadcasted_iota(jnp.int32, (tm, LANE), 1)
        for p in range(HEADS // 2):
            sl = slice(LANE * p, LANE * (p + 1))
            xs = (dattn[:, sl], pltpu.roll(dattn[:, sl], VDIM, 1))
            nds = (-pltpu.roll(delta[:, sl], VDIM, 1), -delta[:, sl])
            for a in range(2):
                hi, lo_part = _hi_lo(nds[a])
                blk = jnp.where(lane < VDIM, xs[a], jnp.where(lane == VDIM, hi, jnp.where(lane == VDIM + 1, lo_part, 0.0)))
                dop_ref[:, LANE * (2 * p + a):LANE * (2 * p + a + 1)] = blk.astype(BF16)
        dpj_ref[:, 2 * D_MODEL:2 * D_MODEL + HG_WIDTH] = (
            dya_in * attn_v * (sa * (1.0 + gate_a * (1.0 - sa)))).astype(BF16)
        don = dyb_in * silu_b
        dpj_ref[:, 2 * D_MODEL + HG_WIDTH:] = (dyb_in * on * (sb * (1.0 + gate_b * (1.0 - sb)))).astype(BF16)
        dgh_ref[...] += jnp.sum(don * ohat, axis=0, keepdims=True)
        dohat = don * ghv
        do_ref[...] = ro * (dohat - ohat * (_sel_right(dohat * ohat, ones) * (1.0 / 64.0)))

    row = lambda w, j: pl.BlockSpec((tm, w), lambda i: (i, j))
    col = lambda w: pl.BlockSpec((w, tm), lambda i: (0, i))
    full = lambda a: pl.BlockSpec(a.shape, lambda i: (0, 0))
    acc = lambda w: pl.BlockSpec((1, w), lambda i: (0, 0))
    sds = lambda w, dt: jax.ShapeDtypeStruct((s, w), dt)
    sdt = lambda w: jax.ShapeDtypeStruct((w, s), BF16)
    return pl.pallas_call(
        body,
        grid=(s // tm,),
        in_specs=[row(1024, 0), row(1024, 0), row(2048, 0), row(512, 4), row(512, 5), row(512, 0), row(512, 0)]
        + [ANY] * 6 + [full(b_gate), full(g_post), full(gh), full(ones64)],
        out_specs=[row(1024, 0), row(3072, 0), row(1024, 0), row(512, 0),
                   col(1024), row(1024, 0), col(512), row(1024, 0), col(512), row(1024, 0),
                   acc(1024), acc(1024), acc(2048), acc(512)],
        out_shape=[sds(1024, F32), sds(D_IN_PAD, BF16), sds(1024, BF16), sds(512, F32),
                   sdt(1024), sds(1024, BF16), sdt(512), sds(1024, BF16), sdt(512), sds(1024, BF16),
                   jax.ShapeDtypeStruct((1, 1024), F32), jax.ShapeDtypeStruct((1, 1024), F32),
                   jax.ShapeDtypeStruct((1, 2048), F32), jax.ShapeDtypeStruct((1, 512), F32)],
        scratch_shapes=[pltpu.VMEM(a.shape, BF16) for a in weights] + [pltpu.SemaphoreType.DMA((6,))],
        compiler_params=_params(("arbitrary",), 56),
        name="tail",
    )(x, tgt, proj, proj, proj, attn, o, *weights, b_gate, g_post, gh, ones64)


def _mla_bwd(proj, dqr, dkr, dv, g_q, g_kv, w_uq_pt, w_kv_pt, rc, rs1, rs2, dproj):
    s = proj.shape[0]
    tm = 256
    scale = 1.0 / math.sqrt(QK)

    def body(cq_ref, ckv_ref, dqr_ref, dkr_ref, dv_ref, gq_ref, gkv_ref, wuqt_ref, wkvt_ref, c_ref, s1_ref, s2_ref,
             dproj_in, dqf_ref, dkvf_ref, dc_ref, dgq_ref, dgkv_ref):
        del dproj_in

        @pl.when(pl.program_id(0) == 0)
        def _():
            dgq_ref[...] = jnp.zeros_like(dgq_ref)
            dgkv_ref[...] = jnp.zeros_like(dgkv_ref)

        c, s1, s2 = c_ref[...], s1_ref[...], s2_ref[...]
        lane = lax.broadcasted_iota(jnp.int32, (tm, LANE), 1)
        ksum = jnp.zeros((tm, LANE), F32)
        for h in range(HEADS):
            sl = slice(LANE * h, LANE * (h + 1))
            dqf_ref[:, sl] = (_unrope(dqr_ref[:, sl], c, s1, s2) * scale).astype(BF16)
            dkh = dkr_ref[:, sl]
            ksum = ksum + dkh
            dkvf_ref[:, sl] = jnp.where(lane < NOPE, dkh, 0.0).astype(BF16)
            dkvf_ref[:, HEADS * LANE + LANE * h:HEADS * LANE + LANE * (h + 1)] = jnp.where(
                lane < VDIM, dv_ref[:, sl], 0.0).astype(BF16)
        dkpe = _unrope(ksum, c, s1, s2)
        dc_ref[:, Q_LORA + KV_LORA:] = jnp.where((lane >= NOPE) & (lane < QK), dkpe, 0.0).astype(BF16)
        dcqn = _dot(dqf_ref[...], wuqt_ref[...])
        dckvn = _dot(dkvf_ref[...], wkvt_ref[...])
        for x_ref, g_ref, dn, cols, dg_ref in ((cq_ref, gq_ref, dcqn, slice(0, Q_LORA), dgq_ref),
                                               (ckv_ref, gkv_ref, dckvn, slice(Q_LORA, Q_LORA + KV_LORA), dgkv_ref)):
            xv = x_ref[...]
            r = lax.rsqrt(jnp.mean(xv * xv, axis=-1, keepdims=True) + EPS)
            xh = xv * r
            dg_ref[...] += jnp.sum(dn * xh, axis=0, keepdims=True)
            dh = dn * g_ref[...]
            dc_ref[:, cols] = (r * (dh - xh * jnp.mean(dh * xh, axis=-1, keepdims=True))).astype(BF16)

    row = lambda w, j: pl.BlockSpec((tm, w), lambda i: (i, j))
    full = lambda a: pl.BlockSpec(a.shape, lambda i: (0, 0))
    acc = lambda w: pl.BlockSpec((1, w), lambda i: (0, 0))
    sds = lambda w, dt: jax.ShapeDtypeStruct((s, w), dt)
    return pl.pallas_call(
        body,
        grid=(s // tm,),
        in_specs=[row(768, 6), row(256, 21), row(1024, 0), row(1024, 0), row(1024, 0), full(g_q), full(g_kv),
                  full(w_uq_pt), full(w_kv_pt), row(128, 0), row(128, 0), row(128, 0),
                  pl.BlockSpec(memory_space=pl.ANY)],
        out_specs=[row(1024, 0), row(2048, 0), row(1152, 4), acc(768), acc(256)],
        out_shape=[sds(1024, BF16), sds(2048, BF16), jax.ShapeDtypeStruct(dproj.shape, BF16),
                   jax.ShapeDtypeStruct((1, 768), F32), jax.ShapeDtypeStruct((1, 256), F32)],
        input_output_aliases={12: 2},
        compiler_params=_params(("arbitrary",)),
        name="mla_bwd",
    )(proj, proj, dqr, dkr, dv, g_q, g_kv, w_uq_pt, w_kv_pt, rc, rs1, rs2, dproj)


def _pick(n, options):
    for o in options:
        if n % o == 0:
            return o
    raise ValueError(n)


def _matmul(a, b, name):
    m, k = a.shape
    n = b.shape[1]
    tm = _pick(m, (1024, 768, 512, 256))
    tn = _pick(n, (1152, 1024, 768, 512))
    tk = _pick(k, (1024, 512))
    nk = k // tk

    def body(a_ref, b_ref, o_ref):
        @pl.when(pl.program_id(2) == 0)
        def _():
            o_ref[...] = jnp.zeros_like(o_ref)

        o_ref[...] += _dot(a_ref[...], b_ref[...])

    return pl.pallas_call(
        body,
        grid=(m // tm, n // tn, nk),
        in_specs=[pl.BlockSpec((tm, tk), lambda i, j, l: (i, l)), pl.BlockSpec((tk, tn), lambda i, j, l: (l, j))],
        out_specs=pl.BlockSpec((tm, tn), lambda i, j, l: (i, j)),
        out_shape=jax.ShapeDtypeStruct((m, n), F32),
        compiler_params=_params(("arbitrary", "arbitrary", "arbitrary")),
        name=name,
    )(a, b)


def _dh_dx(dproj, w_in_pt, x, dout, g_pre, sends):
    s, k = dproj.shape
    tm = 256
    ns, ni = len(sends), s // tm

    def body(dp_ref, w_ref, x_ref, dout_ref, g_ref, *rest):
        send_refs, (dx_ref, dg_ref) = rest[:ns], rest[ns:ns + 2]
        recv_refs, sems = rest[ns + 2:2 * ns + 2], rest[2 * ns + 2:]

        @pl.when(pl.program_id(0) == 0)
        def _():
            _start_all(*_to_chips_copies(send_refs, recv_refs, sems))
            dg_ref[...] = jnp.zeros_like(dg_ref)

        dh = _dot(dp_ref[...], w_ref[...])
        xv = x_ref[...]
        r = lax.rsqrt(jnp.mean(xv * xv, axis=-1, keepdims=True) + EPS)
        xh = xv * r
        dg_ref[...] += jnp.sum(dh * xh, axis=0, keepdims=True)
        dxh = dh * g_ref[...]
        dx_ref[...] = dout_ref[...] + r * (dxh - xh * jnp.mean(dxh * xh, axis=-1, keepdims=True))

        @pl.when(pl.program_id(0) == ni - 1)
        def _():
            _wait_all(*_to_chips_copies(send_refs, recv_refs, sems))

    row = lambda w: pl.BlockSpec((tm, w), lambda i: (i, 0))
    return pl.pallas_call(
        body,
        grid=(ni,),
        in_specs=[row(k), pl.BlockSpec((k, D_MODEL), lambda i: (0, 0)), row(D_MODEL), row(D_MODEL),
                  pl.BlockSpec((1, D_MODEL), lambda i: (0, 0))] + [ANY] * ns,
        out_specs=[row(D_MODEL), pl.BlockSpec((1, D_MODEL), lambda i: (0, 0))] + [ANY] * ns,
        out_shape=[jax.ShapeDtypeStruct((s, D_MODEL), F32), jax.ShapeDtypeStruct((1, D_MODEL), F32)]
        + [jax.ShapeDtypeStruct(a.shape, a.dtype) for a in sends],
        scratch_shapes=_copy_sems(ns, 3),
        compiler_params=_params(("arbitrary",)),
        name="dh_dx",
    )(dproj, w_in_pt, x, dout, g_pre, *sends)


def _pair_reduce(slots):
    n = len(slots)
    half = [(N_DEV // 2,) + a.shape[1:] for a in slots]

    def body(*refs):
        s_refs, o_refs = refs[:n], refs[n:2 * n]
        mine, got = refs[2 * n:3 * n], refs[3 * n:4 * n]
        send_sems, recv_sems, local_sems = refs[4 * n:]
        x, y, c = _my_place()
        copies, loads = [], []
        for a in range(n):
            for q in range(N_DEV // 2):
                copies.append(pltpu.make_async_remote_copy(
                    src_ref=s_refs[a].at[2 * q + 1 - c], dst_ref=got[a].at[q],
                    send_sem=send_sems.at[4 * a + q], recv_sem=recv_sems.at[4 * a + q],
                    device_id=(x, y, 1 - c), device_id_type=MESH_ID))
                loads.append(pltpu.make_async_copy(s_refs[a].at[2 * q + c], mine[a].at[q], local_sems.at[4 * a + q]))
        _start_all(loads, copies)
        _wait_all(loads, copies)
        for a in range(n):
            o_refs[a][...] = (mine[a][...].astype(F32) + got[a][...].astype(F32)).astype(o_refs[a].dtype)

    vm = lambda: [pltpu.VMEM(h, a.dtype) for h, a in zip(half, slots)]
    return pl.pallas_call(
        body,
        in_specs=[ANY] * n,
        out_shape=[jax.ShapeDtypeStruct(h, a.dtype) for h, a in zip(half, slots)],
        scratch_shapes=vm() + vm() + [pltpu.SemaphoreType.DMA((4 * n,)), pltpu.SemaphoreType.DMA((4 * n,)),
                                      pltpu.SemaphoreType.DMA((4 * n,))],
        compiler_params=pltpu.CompilerParams(vmem_limit_bytes=48 * 2**20),
        name="pair_reduce",
    )(*slots)


def _rope_tables(s):
    inv = (np.float32(ROPE_THETA) ** (-np.arange(0, ROPE, 2, dtype=np.float32) / np.float32(ROPE))).astype(np.float32)
    ang = (np.arange(s, dtype=np.float32)[:, None] * inv[None, :]).astype(np.float32)
    cos, sin = jnp.asarray(np.cos(ang.astype(np.float64)), F32), jnp.asarray(np.sin(ang.astype(np.float64)), F32)
    z = lambda w: jnp.zeros((s, w), F32)
    rc = jnp.concatenate([jnp.ones((s, NOPE), F32), cos, cos, z(32)], axis=1)
    rs1 = jnp.concatenate([z(NOPE), -sin, z(16), z(32)], axis=1)
    rs2 = jnp.concatenate([z(NOPE), z(16), sin, z(32)], axis=1)
    return rc, rs1, rs2


def _step(x, tgt, w_in_slots, shards, g_pre, b_gate, g_q, g_kv, lbl, g_hgrn, g_post):
    s = x.shape[0]
    w_in_p, w_in_pt = _assemble_w_in(w_in_slots)
    rc, rs1, rs2 = _rope_tables(s)
    gh = jnp.tile(g_hgrn, (1, HEADS))

    proj, ht, *got = _norm_proj(x, g_pre, w_in_p, shards)
    w_uq, w_ukv, w_a, w_b, w_out = (_from_slots(n, g) for n, g in zip(MATS, got))
    w_uq_p = jnp.pad(w_uq.reshape(Q_LORA, HEADS, QK), ((0, 0), (0, 0), (0, LANE - QK))).reshape(Q_LORA, HEADS * LANE)
    kv3 = w_ukv.reshape(KV_LORA, HEADS, NOPE + VDIM)
    pad64 = lambda t: jnp.pad(t, ((0, 0), (0, 0), (0, LANE - 64))).reshape(KV_LORA, HEADS * LANE)
    w_kv_p = jnp.concatenate([pad64(kv3[:, :, :NOPE]), pad64(kv3[:, :, NOPE:])], axis=1)

    qr, kr, v, cqt, ckvt = _mla_prep(proj, g_q, g_kv, w_uq_p, w_kv_p, rc, rs1, rs2)
    attn, qa = _attn_fwd(qr, kr, v)
    o, sprev = _hgrn_fwd(proj, lbl)
    (dout, dproj, dop, do, mt, dy_bf, yat, dya_bf, ybt, dyb_bf,
     loss_vec, dg_post, db_gate, dgh) = _tail(x, tgt, proj, attn, o, w_a, w_b, w_out, w_a.T, w_b.T, w_out.T,
                                               b_gate, g_post, gh)
    early = [_to_slots(n, _matmul(a, b, "d" + n)).astype(BF16)
             for n, a, b in (("w_branch_a", yat, dya_bf), ("w_branch_b", ybt, dyb_bf), ("w_out", mt, dy_bf))]
    dqr, dkr, dv, *early_recv = _attn_bwd(qa, kr, v, dop, early)
    dproj, dlbl = _hgrn_bwd(proj, lbl, do, sprev, dproj)
    dqf, dkvf, dproj, dg_q, dg_kv = _mla_bwd(proj, dqr, dkr, dv, g_q, g_kv, w_uq_p.T, w_kv_p.T, rc, rs1, rs2, dproj)

    dw_in_slots = _scatter_w_in(_matmul(ht, dproj, "dw_in"))
    dw_uq_p = _matmul(cqt, dqf, "dw_uq")
    dw_kv_p = _matmul(ckvt, dkvf, "dw_kv")
    dw_uq = dw_uq_p.reshape(Q_LORA, HEADS, LANE)[:, :, :QK].reshape(Q_LORA, HEADS * QK)
    dw_ukv = jnp.concatenate([dw_kv_p[:, :HEADS * LANE].reshape(KV_LORA, HEADS, LANE)[:, :, :NOPE],
                              dw_kv_p[:, HEADS * LANE:].reshape(KV_LORA, HEADS, LANE)[:, :, :VDIM]],
                             axis=2).reshape(KV_LORA, 1024)
    late = _pair_reduce([dw_in_slots, _to_slots("w_uq", dw_uq).astype(BF16), _to_slots("w_ukv", dw_ukv).astype(BF16)])
    dx, dg_pre, *late_recv = _dh_dx(dproj, w_in_pt, x, dout, g_pre, late)

    g_sum = _vectors_sum(dg_pre, db_gate, dg_q, dg_kv, dlbl, dgh, dg_post, loss_vec)
    return dx, late_recv[0], dict(zip(MATS, late_recv[1:] + early_recv)), g_sum


def _all_gather(blocks):
    n = len(blocks)

    def body(*refs):
        x_refs, out_refs = refs[:n], refs[n:2 * n]
        send_sems, recv_sems, local_sems = refs[2 * n:]
        x, y, c = _my_place()
        me, sibling = (x, y, c), (x, y, 1 - c)
        chips = [(1 - x, y), (x, 1 - y), (1 - x, 1 - y)]

        def slot(a, px, py, pc):
            return out_refs[a].at[4 * px + 2 * py + pc]

        def copy(a, k, blk, to, src=None):
            return pltpu.make_async_remote_copy(
                src_ref=slot(a, *blk) if src is None else src, dst_ref=slot(a, *blk),
                send_sem=send_sems.at[7 * a + k], recv_sem=recv_sems.at[7 * a + k],
                device_id=to, device_id_type=MESH_ID)

        mine = [pltpu.make_async_copy(x_refs[a], slot(a, *me), local_sems.at[a]) for a in range(n)]
        for cp in mine:
            cp.start()
        first = [copy(a, 0, me, sibling, src=x_refs[a]) for a in range(n)]
        first += [copy(a, 1 + j, me, (*chip, c), src=x_refs[a]) for a in range(n) for j, chip in enumerate(chips)]
        for cp in first:
            cp.start()
        passed = []
        for j, chip in enumerate(chips):
            for a in range(n):
                copy(a, 1 + j, (*chip, c), me).wait_recv()
                passed.append(copy(a, 4 + j, (*chip, c), sibling))
                passed[-1].start()
        for a in range(n):
            copy(a, 0, sibling, me).wait_recv()
        for j, chip in enumerate(chips):
            for a in range(n):
                copy(a, 4 + j, (*chip, 1 - c), me).wait_recv()
        for cp in first + passed:
            cp.wait_send()
        for cp in mine:
            cp.wait()

    return pl.pallas_call(
        body,
        out_shape=[jax.ShapeDtypeStruct((N_DEV,) + b.shape, b.dtype) for b in blocks],
        in_specs=[pl.BlockSpec(memory_space=pl.ANY)] * n,
        out_specs=[pl.BlockSpec(memory_space=pl.ANY)] * n,
        scratch_shapes=[pltpu.SemaphoreType.DMA((7 * n,)), pltpu.SemaphoreType.DMA((7 * n,)),
                        pltpu.SemaphoreType.DMA((n,))],
        name="gather_weights",
    )(*blocks)


def _adamw(g, w, m, v):
    c1 = 1.0 / (1.0 - ADAM_B1 ** ADAM_STEP)
    c2 = 1.0 / (1.0 - ADAM_B2 ** ADAM_STEP)
    nm = ADAM_B1 * m + (1.0 - ADAM_B1) * g
    nv = ADAM_B2 * v + (1.0 - ADAM_B2) * (g * g)
    d = -ADAM_LR * ((nm * c1) / (jnp.sqrt(nv * c2) + ADAM_EPS) + ADAM_WD * w)
    return d, nm, nv


def _sum8(r_ref):
    g = r_ref[0].astype(F32)
    for k in range(1, r_ref.shape[0]):
        g = g + r_ref[k].astype(F32)
    return g


def _sum_adamw_w_in(recv, w, m, v):
    rows, _, cols = w.shape
    tc = 256

    def body(r_ref, w_ref, m_ref, v_ref, g_ref, d_ref, nm_ref, nv_ref):
        g = _sum8(r_ref)
        dense = lambda ref: ref[...].reshape(rows, tc)
        d, nm, nv = _adamw(g, dense(w_ref), dense(m_ref), dense(v_ref))
        for ref, val in ((g_ref, g), (d_ref, d), (nm_ref, nm), (nv_ref, nv)):
            ref[...] = val.reshape(rows, 1, tc)

    blk = pl.BlockSpec((rows, 1, tc), lambda i: (0, 0, i))
    out = jax.ShapeDtypeStruct((rows, 1, cols), F32)
    return pl.pallas_call(
        body,
        grid=(cols // tc,),
        in_specs=[pl.BlockSpec((recv.shape[0], rows, tc), lambda i: (0, 0, i)), blk, blk, blk],
        out_specs=[blk, blk, blk, blk],
        out_shape=[out, out, out, out],
        compiler_params=_params(("arbitrary",)),
        name="sum_adamw_w_in",
    )(recv, w, m, v)


def _sum_adamw_whole(recvs, ws, ms, vs):
    n = len(ws)

    def body(*refs):
        r_refs, w_refs, m_refs, v_refs = refs[:n], refs[n:2 * n], refs[2 * n:3 * n], refs[3 * n:4 * n]
        outs = refs[4 * n:]
        for a in range(n):
            g = _sum8(r_refs[a])
            d, nm, nv = _adamw(g, w_refs[a][...], m_refs[a][...], v_refs[a][...])
            outs[a][...] = g
            outs[n + a][...] = d
            outs[2 * n + a][...] = nm
            outs[3 * n + a][...] = nv

    shapes = [jax.ShapeDtypeStruct(w.shape, F32) for w in ws]
    res = pl.pallas_call(
        body,
        out_shape=shapes * 4,
        compiler_params=pltpu.CompilerParams(vmem_limit_bytes=48 * 2**20),
        name="sum_adamw_mats",
    )(*recvs, *ws, *ms, *vs)
    return res[:n], res[n:2 * n], res[2 * n:3 * n], res[3 * n:]


SMALL = ("g_pre", "b_gate", "g_q", "g_kv", "lb_logits", "g_hgrn", "g_post")
SMALL_SHAPE = dict(g_pre=(1, 1024), b_gate=(1, 2048), g_q=(1, 768), g_kv=(1, 256), lb_logits=(2, 512),
                   g_hgrn=(1, 64), g_post=(1, 1024))


def _vectors_sum(dg_pre, db_gate, dg_q, dg_kv, dlbl, dgh, dg_post, loss_vec):
    def body(gpre_ref, bg_ref, gq_ref, gkv_ref, lbl_ref, gh_ref, gpost_ref, loss_ref, out_ref, mine, got,
             send_sems, recv_sems):
        mine[...] = jnp.zeros_like(mine)
        mine[0:1, :] = gpre_ref[...]
        mine[1:2, :] = bg_ref[:, :1024]
        mine[2:3, :] = bg_ref[:, 1024:]
        mine[3:4, :Q_LORA] = gq_ref[...]
        mine[4:5, :KV_LORA] = gkv_ref[...]
        loss = (0.5 / D_MODEL) * jnp.sum(loss_ref[...], axis=-1, keepdims=True)
        mine[4:5, KV_LORA:] = jnp.broadcast_to(loss, (1, 1024 - KV_LORA))
        mine[5:6, :HG_WIDTH] = lbl_ref[0:1, :]
        mine[5:6, HG_WIDTH:] = lbl_ref[1:2, :]
        gh = gh_ref[...]
        fold = gh[:, :VDIM]
        for h in range(1, HEADS):
            fold = fold + gh[:, VDIM * h:VDIM * (h + 1)]
        mine[6:7, :VDIM] = fold
        mine[7:8, :] = gpost_ref[...]
        x, y, c = _my_place()
        me = 4 * x + 2 * y + c
        got[me] = mine[...]
        copies = [pltpu.make_async_remote_copy(
            src_ref=mine, dst_ref=got.at[me], send_sem=send_sems.at[k], recv_sem=recv_sems.at[k],
            device_id=_flip(k, x, y, c), device_id_type=MESH_ID) for k in range(N_DEV - 1)]
        _start_all([], copies)
        _wait_all([], copies)
        out_ref[...] = _sum8(got)

    return pl.pallas_call(
        body,
        out_shape=jax.ShapeDtypeStruct((8, 1024), F32),
        scratch_shapes=[pltpu.VMEM((8, 1024), F32), pltpu.VMEM((N_DEV, 8, 1024), F32),
                        pltpu.SemaphoreType.DMA((7,)), pltpu.SemaphoreType.DMA((7,))],
        name="vectors_sum",
    )(dg_pre, db_gate, dg_q, dg_kv, dlbl, dgh, dg_post, loss_vec)


def _vectors_adamw(g_sum, ws, ms, vs):
    n = len(SMALL)

    def body(g_ref, *refs):
        w_refs, m_refs, v_refs = refs[:n], refs[n:2 * n], refs[2 * n:3 * n]
        loss_ref, outs = refs[3 * n], refs[3 * n + 1:]
        g = g_ref[...]
        loss_ref[...] = g[4:5, KV_LORA:KV_LORA + 1]
        grads = (g[0:1, :], jnp.concatenate([g[1:2, :], g[2:3, :]], axis=1), g[3:4, :Q_LORA], g[4:5, :KV_LORA],
                 jnp.concatenate([g[5:6, :HG_WIDTH], g[5:6, HG_WIDTH:]], axis=0), g[6:7, :VDIM], g[7:8, :])
        for a in range(n):
            d, nm, nv = _adamw(grads[a], w_refs[a][...], m_refs[a][...], v_refs[a][...])
            outs[a][...] = grads[a]
            outs[n + a][...] = d
            outs[2 * n + a][...] = nm
            outs[3 * n + a][...] = nv

    shapes = [jax.ShapeDtypeStruct(SMALL_SHAPE[k], F32) for k in SMALL]
    res = pl.pallas_call(
        body,
        out_shape=[jax.ShapeDtypeStruct((1, 1), F32)] + shapes * 4,
        name="vectors_adamw",
    )(g_sum, *ws, *ms, *vs)
    return res[0], res[1:n + 1], res[n + 1:2 * n + 1], res[2 * n + 1:3 * n + 1], res[3 * n + 1:]


MATS = ("w_uq", "w_ukv", "w_branch_a", "w_branch_b", "w_out")
COL_SHARDED = dict(w_uq=False, w_ukv=True, w_branch_a=True, w_branch_b=True, w_out=False)
ORDER = ("g_pre", "w_in", "b_gate", "g_q", "w_uq", "g_kv", "w_ukv", "lb_logits", "g_hgrn",
         "w_branch_a", "w_branch_b", "w_out", "g_post")


def _to_slots(name, full):
    r, c = full.shape
    if COL_SHARDED[name]:
        return full.reshape(r, N_DEV, c // N_DEV).transpose(1, 0, 2)
    return full.reshape(N_DEV, r // N_DEV, c)


def _from_slots(name, slots):
    _, r, c = slots.shape
    if COL_SHARDED[name]:
        return slots.transpose(1, 0, 2).reshape(r, N_DEV * c)
    return slots.reshape(N_DEV * r, c)


def kernel(x, g_pre, w_in, b_gate, g_q, w_uq, g_kv, w_ukv, lb_logits, g_hgrn, w_branch_a, w_branch_b, w_out, g_post, loss_target, m_g_pre, m_w_in, m_b_gate, m_g_q, m_w_uq, m_g_kv, m_w_ukv, m_lb_logits, m_g_hgrn, m_w_branch_a, m_w_branch_b, m_w_out, m_g_post, v_g_pre, v_w_in, v_b_gate, v_g_q, v_w_uq, v_g_kv, v_w_ukv, v_lb_logits, v_g_hgrn, v_w_branch_a, v_w_branch_b, v_w_out, v_g_post):
    rows3 = lambda a: jnp.transpose(a, (2, 0, 1))
    w = dict(w_in=rows3(w_in), w_uq=w_uq[0], w_ukv=w_ukv[0], w_branch_a=w_branch_a[0], w_branch_b=w_branch_b[0],
             w_out=w_out[0], g_pre=g_pre, b_gate=b_gate, g_q=g_q, g_kv=g_kv, lb_logits=lb_logits, g_hgrn=g_hgrn,
             g_post=g_post)
    mom = dict(w_in=rows3(m_w_in), w_uq=m_w_uq[0], w_ukv=m_w_ukv[0], w_branch_a=m_w_branch_a[0],
               w_branch_b=m_w_branch_b[0], w_out=m_w_out[0], g_pre=m_g_pre, b_gate=m_b_gate, g_q=m_g_q, g_kv=m_g_kv,
               lb_logits=m_lb_logits, g_hgrn=m_g_hgrn, g_post=m_g_post)
    var = dict(w_in=rows3(v_w_in), w_uq=v_w_uq[0], w_ukv=v_w_ukv[0], w_branch_a=v_w_branch_a[0],
               w_branch_b=v_w_branch_b[0], w_out=v_w_out[0], g_pre=v_g_pre, b_gate=v_b_gate, g_q=v_g_q, g_kv=v_g_kv,
               lb_logits=v_lb_logits, g_hgrn=v_g_hgrn, g_post=v_g_post)

    (w_in_slots,) = _all_gather([w["w_in"].reshape(W_IN_SHARD, D_MODEL).astype(BF16)])
    dx, recv_in, recv, g_sum = _step(x[0], loss_target[0], w_in_slots, [w[n].astype(BF16) for n in MATS],
                                     g_pre, b_gate, g_q, g_kv, lb_logits, g_hgrn, g_post)

    g_in, d_in, m_in, v_in = _sum_adamw_w_in(recv_in, w["w_in"], mom["w_in"], var["w_in"])
    res = _sum_adamw_whole([recv[n] for n in MATS], *([t[n] for n in MATS] for t in (w, mom, var)))
    total, *vec = _vectors_adamw(g_sum, *([t[n] for n in SMALL] for t in (w, mom, var)))

    outs = []
    for mats, vecs, big in zip(res, vec, (g_in, d_in, m_in, v_in)):
        t = {**{n: a[None] for n, a in zip(MATS, mats)}, **dict(zip(SMALL, vecs)),
             "w_in": jnp.transpose(big, (1, 2, 0))}
        outs += [t[n] for n in ORDER]
    return (total.reshape(()), dx[None], *outs)
```

```python
import math

import jax
import jax.numpy as jnp
import numpy as np
from jax import lax
from jax.experimental import pallas as pl
from jax.experimental.pallas import tpu as pltpu

F32, BF16 = jnp.float32, jnp.bfloat16

D_MODEL = 1024
EPS = 1e-6
HEADS = 8
NOPE, ROPE, VDIM = 64, 32, 64
QK = NOPE + ROPE
Q_LORA, KV_LORA = 768, 256
ROPE_THETA = 10000.0
ATT_CHUNK_SHIFT = 6
HG_BLOCK = 32
HG_WIDTH = 512
D_IN = 5664
D_IN_PAD = 5760
W_IN_SHARD = D_IN // 8
N_DEV = 8
LANE = 128

ADAM_LR, ADAM_B1, ADAM_B2, ADAM_EPS, ADAM_WD, ADAM_STEP = 0.001, 0.9, 0.999, 1e-08, 0.01, 10

W_IN_SEGMENTS = ((3616, 5664, 0), (1056, 1568, 2048), (3104, 3616, 2560), (1568, 3104, 3072),
                 (0, 1024, 4608), (1024, 1056, 5696))
W_IN_ZERO = ((5632, 5696), (5728, 5760))

NT = (((1,), (1,)), ((), ()))
TN = (((0,), (0,)), ((), ()))
MESH_ID = pl.DeviceIdType.MESH


def _w_in_pieces():
    out = []
    for lo, hi, dst in W_IN_SEGMENTS:
        c = lo
        while c < hi:
            p = c // W_IN_SHARD
            e = min(hi, (p + 1) * W_IN_SHARD)
            out.append((p, c - p * W_IN_SHARD, e - p * W_IN_SHARD, dst + c - lo))
            c = e
    return out


def _params(sem, vmem_mb=48):
    return pltpu.CompilerParams(dimension_semantics=sem, vmem_limit_bytes=vmem_mb * 2**20)


def _dot(a, b):
    return jnp.dot(a, b, preferred_element_type=F32)


def _dotg(a, b, dims):
    return lax.dot_general(a, b, dims, preferred_element_type=F32)


def _split2(x):
    hi = x.astype(BF16)
    return hi, (x - hi.astype(F32)).astype(BF16)


def _sel_left(m01, x):
    hi, lo = _split2(x)
    return _dot(m01, hi) + _dot(m01, lo)


def _sel_right(x, m01):
    hi, lo = _split2(x)
    return _dot(hi, m01) + _dot(lo, m01)


def _hi_lo(x):
    hi = x.astype(BF16).astype(F32)
    return hi, x - hi


def _sigmoid(x):
    return 0.5 * jnp.tanh(0.5 * x) + 0.5


def _rope(x, c, s1, s2):
    return x * c + pltpu.roll(x, 112, 1) * s1 + pltpu.roll(x, 16, 1) * s2


def _unrope(d, c, s1, s2):
    return d * c + pltpu.roll(d * s1, 16, 1) + pltpu.roll(d * s2, 112, 1)


def _my_place():
    return lax.axis_index("x"), lax.axis_index("y"), lax.axis_index("c")


def _flip(k, x, y, c):
    fx, fy, fc = (k + 1) >> 2 & 1, (k + 1) >> 1 & 1, (k + 1) & 1
    return (1 - x if fx else x), (1 - y if fy else y), (1 - c if fc else c)


def _to_all_copies(s_refs, r_refs, sems, spread):
    send_sems, recv_sems, local_sems = sems
    x, y, c = _my_place()
    me = 4 * x + 2 * y + c
    src = (lambda a, p: s_refs[a]) if spread else (lambda a, p: s_refs[a].at[p])
    local = [pltpu.make_async_copy(src(a, me), r_refs[a].at[me], local_sems.at[a]) for a in range(len(s_refs))]
    remote = []
    for k in range(N_DEV - 1):
        px, py, pc = _flip(k, x, y, c)
        for a in range(len(s_refs)):
            remote.append(pltpu.make_async_remote_copy(
                src_ref=src(a, 4 * px + 2 * py + pc), dst_ref=r_refs[a].at[me],
                send_sem=send_sems.at[7 * a + k], recv_sem=recv_sems.at[7 * a + k],
                device_id=(px, py, pc), device_id_type=MESH_ID))
    return local, remote


def _to_chips_copies(s_refs, r_refs, sems):
    send_sems, recv_sems, local_sems = sems
    x, y, c = _my_place()
    me = 2 * x + y
    local = [pltpu.make_async_copy(s_refs[a].at[me], r_refs[a].at[me], local_sems.at[a]) for a in range(len(s_refs))]
    remote = []
    for k in range(3):
        px = 1 - x if (k + 1) >> 1 & 1 else x
        py = 1 - y if (k + 1) & 1 else y
        for a in range(len(s_refs)):
            remote.append(pltpu.make_async_remote_copy(
                src_ref=s_refs[a].at[2 * px + py], dst_ref=r_refs[a].at[me],
                send_sem=send_sems.at[3 * a + k], recv_sem=recv_sems.at[3 * a + k],
                device_id=(px, py, c), device_id_type=MESH_ID))
    return local, remote


def _start_all(local, remote):
    for cp in local + remote:
        cp.start()


def _wait_all(local, remote):
    for cp in remote:
        cp.wait_recv()
    for cp in remote:
        cp.wait_send()
    for cp in local:
        cp.wait()


def _copy_sems(n, peers):
    return [pltpu.SemaphoreType.DMA((peers * n,)), pltpu.SemaphoreType.DMA((peers * n,)),
            pltpu.SemaphoreType.DMA((n,))]


ANY = pl.BlockSpec(memory_space=pl.ANY)


def _assemble_w_in(slots):
    tc = 256
    pieces = _w_in_pieces()

    def body(s_ref, w_ref, wt_ref):
        for lo, hi in W_IN_ZERO:
            wt_ref[lo:hi, :] = jnp.zeros((hi - lo, tc), BF16)
        for p, lo, hi, dst in pieces:
            wt_ref[dst:dst + hi - lo, :] = s_ref[p, lo:hi, :]
        w_ref[...] = wt_ref[...].T

    return pl.pallas_call(
        body,
        grid=(D_MODEL // tc,),
        in_specs=[pl.BlockSpec((N_DEV, W_IN_SHARD, tc), lambda i: (0, 0, i))],
        out_specs=[pl.BlockSpec((tc, D_IN_PAD), lambda i: (i, 0)), pl.BlockSpec((D_IN_PAD, tc), lambda i: (0, i))],
        out_shape=[jax.ShapeDtypeStruct((D_MODEL, D_IN_PAD), BF16), jax.ShapeDtypeStruct((D_IN_PAD, D_MODEL), BF16)],
        compiler_params=_params(("arbitrary",)),
        name="assemble_w_in",
    )(slots)


def _scatter_w_in(dw):
    tc = 256
    pieces = _w_in_pieces()

    def body(d_ref, s_ref):
        dt = d_ref[...].T
        for p, lo, hi, dst in pieces:
            s_ref[p, lo:hi, :] = dt[dst:dst + hi - lo, :].astype(BF16)

    return pl.pallas_call(
        body,
        grid=(D_MODEL // tc,),
        in_specs=[pl.BlockSpec((tc, D_IN_PAD), lambda i: (i, 0))],
        out_specs=pl.BlockSpec((N_DEV, W_IN_SHARD, tc), lambda i: (0, 0, i)),
        out_shape=jax.ShapeDtypeStruct((N_DEV, W_IN_SHARD, D_MODEL), BF16),
        compiler_params=_params(("arbitrary",)),
        name="scatter_w_in",
    )(dw)


def _norm_proj(x, g_pre, w, shards):
    s, n = x.shape[0], w.shape[1]
    tm, tn = 512, 1152
    ni, nj, ns = s // tm, n // tn, len(shards)

    def body(x_ref, g_ref, w_ref, *rest):
        shard_refs, (proj_ref, ht_ref), got_refs = rest[:ns], rest[ns:ns + 2], rest[ns + 2:2 * ns + 2]
        h_ref, sems = rest[2 * ns + 2], rest[2 * ns + 3:]
        j, i = pl.program_id(0), pl.program_id(1)
        rows = pl.ds(pl.multiple_of(i * tm, tm), tm)

        @pl.when((j == 0) & (i == 0))
        def _():
            _start_all(*_to_all_copies(shard_refs, got_refs, sems, True))

        @pl.when(j == 0)
        def _():
            xv = x_ref[...]
            r = lax.rsqrt(jnp.mean(xv * xv, axis=-1, keepdims=True) + EPS)
            h = (xv * r * g_ref[...]).astype(BF16)
            h_ref[rows, :] = h
            ht_ref[...] = h.T

        proj_ref[...] = _dot(h_ref[rows, :], w_ref[...])

        @pl.when((j == nj - 1) & (i == ni - 1))
        def _():
            _wait_all(*_to_all_copies(shard_refs, got_refs, sems, True))

    first = lambda j, i: jnp.where(j == 0, i, ni - 1)
    return pl.pallas_call(
        body,
        grid=(nj, ni),
        in_specs=[
            pl.BlockSpec((tm, D_MODEL), lambda j, i: (first(j, i), 0)),
            pl.BlockSpec((1, D_MODEL), lambda j, i: (0, 0)),
            pl.BlockSpec((D_MODEL, tn), lambda j, i: (0, j)),
        ] + [ANY] * ns,
        out_specs=[
            pl.BlockSpec((tm, tn), lambda j, i: (i, j)),
            pl.BlockSpec((D_MODEL, tm), lambda j, i: (0, first(j, i))),
        ] + [ANY] * ns,
        out_shape=[jax.ShapeDtypeStruct((s, n), F32), jax.ShapeDtypeStruct((D_MODEL, s), BF16)]
        + [jax.ShapeDtypeStruct((N_DEV,) + b.shape, b.dtype) for b in shards],
        scratch_shapes=[pltpu.VMEM((s, D_MODEL), BF16)] + _copy_sems(ns, 7),
        compiler_params=_params(("arbitrary", "arbitrary")),
        name="norm_proj",
    )(x, g_pre, w, *shards)


def _mla_prep(proj, g_q, g_kv, w_uq_p, w_kv_p, rc, rs1, rs2):
    s = proj.shape[0]
    tm = 256
    scale = 1.0 / math.sqrt(QK)

    def body(cq_ref, ckv_ref, kpe_ref, gq_ref, gkv_ref, wuq_ref, wkv_ref, c_ref, s1_ref, s2_ref,
             qr_ref, kr_ref, v_ref, cqt_ref, ckvt_ref):
        cq = cq_ref[...]
        r = lax.rsqrt(jnp.mean(cq * cq, axis=-1, keepdims=True) + EPS)
        cqn = (cq * r * gq_ref[...]).astype(BF16)
        cqt_ref[...] = cqn.T
        q = _dot(cqn, wuq_ref[...])
        ckv = ckv_ref[...]
        r = lax.rsqrt(jnp.mean(ckv * ckv, axis=-1, keepdims=True) + EPS)
        ckvn = (ckv * r * gkv_ref[...]).astype(BF16)
        ckvt_ref[...] = ckvn.T
        kv = _dot(ckvn, wkv_ref[...])
        c, s1, s2 = c_ref[...], s1_ref[...], s2_ref[...]
        lane = lax.broadcasted_iota(jnp.int32, (tm, LANE), 1)
        kpe = _rope(kpe_ref[...], c, s1, s2) + jnp.where((lane == QK) | (lane == QK + 1), 1.0, 0.0)
        vone = jnp.where((lane == VDIM) | (lane == VDIM + 1), 1.0, 0.0)
        for h in range(HEADS):
            sl = slice(LANE * h, LANE * (h + 1))
            qr_ref[:, sl] = (_rope(q[:, sl], c, s1, s2) * scale).astype(BF16)
            kr_ref[:, sl] = (kv[:, sl] + kpe).astype(BF16)
            v_ref[:, sl] = (kv[:, HEADS * LANE + LANE * h:HEADS * LANE + LANE * (h + 1)] + vone).astype(BF16)

    row = lambda w, j: pl.BlockSpec((tm, w), lambda i: (i, j))
    col = lambda w: pl.BlockSpec((w, tm), lambda i: (0, i))
    full = lambda a: pl.BlockSpec(a.shape, lambda i: (0, 0))
    return pl.pallas_call(
        body,
        grid=(s // tm,),
        in_specs=[row(768, 6), row(256, 21), row(128, 44), full(g_q), full(g_kv), full(w_uq_p), full(w_kv_p),
                  row(128, 0), row(128, 0), row(128, 0)],
        out_specs=[row(1024, 0), row(1024, 0), row(1024, 0), col(768), col(256)],
        out_shape=[jax.ShapeDtypeStruct((s, 1024), BF16), jax.ShapeDtypeStruct((s, 1024), BF16),
                   jax.ShapeDtypeStruct((s, 1024), BF16), jax.ShapeDtypeStruct((768, s), BF16),
                   jax.ShapeDtypeStruct((256, s), BF16)],
        compiler_params=_params(("arbitrary",)),
        name="mla_prep",
    )(proj, proj, proj, g_q, g_kv, w_uq_p, w_kv_p, rc, rs1, rs2)


ATT_T = 512
ATT_FWD_HEADS = 4


def _chunk_mask(transposed):
    r = lax.broadcasted_iota(jnp.int32, (ATT_T, ATT_T), 0) >> ATT_CHUNK_SHIFT
    c = lax.broadcasted_iota(jnp.int32, (ATT_T, ATT_T), 1) >> ATT_CHUNK_SHIFT
    return (r <= c) if transposed else (c <= r)


def _attn_fwd(qr, kr, vp):
    s = qr.shape[0]
    t = ATT_T
    g = ATT_FWD_HEADS

    def body(q_ref, k_ref, v_ref, o_ref, qa_ref, sc_ref):
        qi = pl.program_id(1)
        lane = lax.broadcasted_iota(jnp.int32, (t, LANE), 1)
        sls = [slice(LANE * a, LANE * (a + 1)) for a in range(g)]
        qs = [q_ref[:, sl] for sl in sls]

        def scores(j):
            rows = pl.ds(pl.multiple_of(j * t, t), t)
            for a in range(g):
                sc_ref[j & 1, a] = _dotg(qs[a], k_ref[rows, sls[a]], NT)

        def step(j, carry, masked):
            rows = pl.ds(pl.multiple_of(j * t, t), t)
            out = []
            for a in range(g):
                m, acc = carry[a]
                sc = sc_ref[j & 1, a]
                if masked:
                    sc = jnp.where(_chunk_mask(False), sc, -1e30)
                m_new = jnp.maximum(m, jnp.max(sc, axis=-1, keepdims=True))
                p = jnp.exp(sc - m_new).astype(BF16)
                acc = jnp.exp(m - m_new) * acc + _dot(p, v_ref[rows, sls[a]])
                out.append((m_new, acc))
            return tuple(out)

        def loop(j, carry):
            carry = step(j, carry, False)
            scores(j + 1)
            return carry

        init = tuple((jnp.full((t, 1), -1e30, F32), jnp.zeros((t, LANE), F32)) for _ in range(g))
        scores(0)
        carry = lax.fori_loop(0, qi, loop, init)
        carry = step(qi, carry, True)
        outs = []
        for a in range(g):
            m, acc = carry[a]
            l = acc[:, VDIM:VDIM + 1]
            outs.append(acc / l)
            hi, lo_part = _hi_lo(-(m + jnp.log(l)))
            qa = jnp.where(lane == QK, hi, jnp.where(lane == QK + 1, lo_part, qs[a].astype(F32)))
            qa_ref[:, sls[a]] = qa.astype(BF16)
        for p in range(g // 2):
            o_ref[:, LANE * p:LANE * (p + 1)] = jnp.where(lane < VDIM, outs[2 * p], pltpu.roll(outs[2 * p + 1], VDIM, 1))

    return pl.pallas_call(
        body,
        grid=(HEADS // g, s // t),
        in_specs=[
            pl.BlockSpec((t, g * LANE), lambda h, i: (i, h)),
            pl.BlockSpec((s, g * LANE), lambda h, i: (0, h)),
            pl.BlockSpec((s, g * LANE), lambda h, i: (0, h)),
        ],
        out_specs=[
            pl.BlockSpec((t, g * VDIM), lambda h, i: (i, h)),
            pl.BlockSpec((t, g * LANE), lambda h, i: (i, h)),
        ],
        out_shape=[jax.ShapeDtypeStruct((s, 512), F32), jax.ShapeDtypeStruct((s, 1024), BF16)],
        scratch_shapes=[pltpu.VMEM((2, g, t, t), F32)],
        compiler_params=_params(("arbitrary", "arbitrary")),
        name="attn_fwd",
    )(qr, kr, vp)


def _attn_bwd(qa, kr, vp, dop, sends):
    s = qa.shape[0]
    t = ATT_T
    nq = s // t
    ns = len(sends)

    def body(q_ref, k_ref, v_ref, do_ref, *rest):
        send_refs, (dq_ref, dk_ref, dv_ref) = rest[:ns], rest[ns:ns + 3]
        recv_refs, sems = rest[ns + 3:2 * ns + 3], rest[2 * ns + 3:]
        j = pl.program_id(1)
        sls = [slice(LANE * a, LANE * (a + 1)) for a in range(2)]

        @pl.when((pl.program_id(0) == 0) & (j == 0))
        def _():
            _start_all(*_to_all_copies(send_refs, recv_refs, sems, False))

        @pl.when(j == 0)
        def _():
            dq_ref[...] = jnp.zeros_like(dq_ref)

        dk_ref[...] = jnp.zeros_like(dk_ref)
        dv_ref[...] = jnp.zeros_like(dv_ref)
        ks = [k_ref[:, sl] for sl in sls]
        vs = [v_ref[:, sl] for sl in sls]

        def step(i, masked):
            rows = pl.ds(pl.multiple_of(i * t, t), t)
            for a in range(2):
                q = q_ref[rows, sls[a]]
                do = do_ref[rows, sls[a]]
                sc = _dotg(ks[a], q, NT)
                if masked:
                    sc = jnp.where(_chunk_mask(True), sc, -1e30)
                p = jnp.exp(sc)
                ds = (p * _dotg(vs[a], do, NT)).astype(BF16)
                dv_ref[:, sls[a]] += _dot(p.astype(BF16), do)
                dk_ref[:, sls[a]] += _dot(ds, q)
                dq_ref[rows, sls[a]] += _dotg(ds, ks[a], TN)

        step(j, True)

        def loop(i, c):
            step(i, False)
            return c

        lax.fori_loop(j + 1, nq, loop, 0)

        @pl.when((pl.program_id(0) == HEADS // 2 - 1) & (j == nq - 1))
        def _():
            _wait_all(*_to_all_copies(send_refs, recv_refs, sems, False))

    blk = pl.BlockSpec((t, 2 * LANE), lambda h, j: (j, h))
    whole = pl.BlockSpec((s, 2 * LANE), lambda h, j: (0, h))
    out = jax.ShapeDtypeStruct((s, 1024), F32)
    return pl.pallas_call(
        body,
        grid=(HEADS // 2, nq),
        in_specs=[whole, blk, blk, whole] + [ANY] * ns,
        out_specs=[whole, blk, blk] + [ANY] * ns,
        out_shape=[out, out, out] + [jax.ShapeDtypeStruct(a.shape, a.dtype) for a in sends],
        scratch_shapes=_copy_sems(ns, 7),
        compiler_params=_params(("arbitrary", "arbitrary")),
        name="attn_bwd",
    )(qa, kr, vp, dop, *sends)


HG_T = 256
HG_NC = HG_T // HG_BLOCK
HG_G = 4
GW = 64 * HG_G


def _hg_consts():
    r = jnp.arange(HG_T)[:, None]
    c = jnp.arange(HG_T)[None, :]
    same = (r // HG_BLOCK) == (c // HG_BLOCK)
    mcum = (same & (c <= r)).astype(BF16)
    mrev = (same & (c >= r)).astype(BF16)
    msum = same.astype(BF16)
    a = jnp.arange(GW) // 64
    bd = (a[:, None] == a[None, :]).astype(F32)
    return mcum, mrev, msum, bd


def _stack_heads(xg, head):
    return jnp.concatenate([jnp.where(head == h, xg, 0.0) for h in range(HG_G)], axis=0)


def _unstack_heads(r, head, t):
    out = r[(HG_G - 1) * t:]
    for h in range(HG_G - 2, -1, -1):
        out = jnp.where(head == h, r[h * t:(h + 1) * t], out)
    return out


def _compact_state(st):
    out = st[:64]
    for h in range(1, HG_G):
        out = out + st[64 * h:64 * (h + 1)]
    return out


def _expand_state(cs, head64):
    return jnp.concatenate([jnp.where(head64 == h, cs, 0.0) for h in range(HG_G)], axis=0)


def _hg_pre(hq, hf, lbl, mcum, msum):
    lb = _sigmoid(lbl[0:1, :] - lbl[1:2, :])
    sig = _sigmoid(hf)
    f = lb + (1.0 - lb) * sig
    lf = jnp.log(f)
    b = _sel_left(mcum, lf)
    big_l = _sel_left(msum, lf)
    k = 1.0 - f
    qd = hq * jnp.exp(b)
    ki = k * jnp.exp(-b)
    ke = k * jnp.exp(big_l - b)
    return lb, sig, f, b, big_l, qd, ki, ke


def _hgrn_fwd(proj, lbl):
    s = proj.shape[0]
    t = HG_T
    mcum, _, msum, bd = _hg_consts()

    def body(hq_ref, hf_ref, hi_ref, lbl_ref, mcum_ref, msum_ref, bd_ref, o_ref, sp_ref, st_ref):
        @pl.when(pl.program_id(0) == 0)
        def _():
            st_ref[...] = jnp.zeros_like(st_ref)

        mc = mcum_ref[...]
        _, _, _, _, big_l, qd, ki, ke = _hg_pre(hq_ref[...], hf_ref[...], lbl_ref[...], mc, msum_ref[...])
        el = jnp.exp(big_l)
        hi = hi_ref[...]
        head = lax.broadcasted_iota(jnp.int32, (t, GW), 1) >> 6
        mask = jnp.concatenate([mc] * HG_G, axis=0) > 0.5
        for p in range(HEADS // HG_G):
            sl = slice(GW * p, GW * (p + 1))
            vp = hi[:, sl].astype(BF16)
            qs = _stack_heads(qd[:, sl], head).astype(BF16)
            a = jnp.where(mask, _dotg(qs, ki[:, sl].astype(BF16), NT), 0.0)
            o_intra = _unstack_heads(_dot(a.astype(BF16), vp), head, t)
            qb = qd[:, sl].astype(BF16)
            kb = ke[:, sl].astype(BF16)
            st = st_ref[p]
            for c in range(HG_NC):
                rows = slice(HG_BLOCK * c, HG_BLOCK * (c + 1))
                sp_ref[c, :, sl] = _compact_state(st)
                o_ref[rows, sl] = o_intra[rows] + _dotg(qb[rows], st.astype(BF16), NT)
                u = _dotg(vp[rows], kb[rows], TN) * bd_ref[...]
                st = st * el[HG_BLOCK * c:HG_BLOCK * c + 1, sl] + u
            st_ref[p] = st

    row = lambda j: pl.BlockSpec((t, HG_WIDTH), lambda i: (i, j))
    full = lambda a: pl.BlockSpec(a.shape, lambda i: (0, 0))
    return pl.pallas_call(
        body,
        grid=(s // t,),
        in_specs=[row(6), row(7), row(8), full(lbl), full(mcum), full(msum), full(bd)],
        out_specs=[row(0), pl.BlockSpec((HG_NC, 64, HG_WIDTH), lambda i: (i, 0, 0))],
        out_shape=[jax.ShapeDtypeStruct((s, HG_WIDTH), F32),
                   jax.ShapeDtypeStruct((s // HG_BLOCK, 64, HG_WIDTH), F32)],
        scratch_shapes=[pltpu.VMEM((HEADS // HG_G, GW, GW), F32)],
        compiler_params=_params(("arbitrary",)),
        name="hgrn_fwd",
    )(proj, proj, proj, lbl, mcum, msum, bd)


def _hgrn_bwd(proj, lbl, do, sprev, dproj):
    s = proj.shape[0]
    t = HG_T
    nt = s // t
    mcum, mrev, msum, bd = _hg_consts()

    def body(hq_ref, hf_ref, hi_ref, lbl_ref, do_ref, sp_ref, mcum_ref, mrev_ref, msum_ref, bd_ref,
             dproj_in, dh_ref, dlbl_ref, g_ref):
        del dproj_in

        @pl.when(pl.program_id(0) == 0)
        def _():
            g_ref[...] = jnp.zeros_like(g_ref)
            dlbl_ref[...] = jnp.zeros_like(dlbl_ref)

        mc = mcum_ref[...]
        lb, sig, f, b, big_l, qd, ki, ke = _hg_pre(hq_ref[...], hf_ref[...], lbl_ref[...], mc, msum_ref[...])
        el = jnp.exp(big_l)
        hi = hi_ref[...]
        dov = do_ref[...]
        head = lax.broadcasted_iota(jnp.int32, (t, GW), 1) >> 6
        head64 = lax.broadcasted_iota(jnp.int32, (64, GW), 1) >> 6
        mask = jnp.concatenate([mc] * HG_G, axis=0) > 0.5
        dqd_parts, dke_parts, dv_parts, del_parts, dki_parts = [], [], [], [], []
        for p in range(HEADS // HG_G):
            sl = slice(GW * p, GW * (p + 1))
            vp = hi[:, sl].astype(BF16)
            qs = _stack_heads(qd[:, sl], head).astype(BF16)
            kip = ki[:, sl].astype(BF16)
            dos = _stack_heads(dov[:, sl], head).astype(BF16)
            a = jnp.where(mask, _dotg(qs, kip, NT), 0.0).astype(BF16)
            da = jnp.where(mask, _dotg(dos, vp, NT), 0.0).astype(BF16)
            r = _dot(da, kip)
            dki_parts.append(_dotg(da, qs, TN))
            qb = qd[:, sl].astype(BF16)
            kb = ke[:, sl].astype(BF16)
            dob = dov[:, sl].astype(BF16)
            g = g_ref[p]
            dqd_c, dv_c, dke_c, del_c = [], [], [], []
            for c in range(HG_NC - 1, -1, -1):
                rows = slice(HG_BLOCK * c, HG_BLOCK * (c + 1))
                gb = g.astype(BF16)
                st = _expand_state(sp_ref[c, :, sl], head64)
                dqd_c.append(_dot(dob[rows], st.astype(BF16)))
                dv_c.append(_dotg(kb[rows], gb, NT))
                dke_c.append(_dot(vp[rows], gb))
                del_c.append(jnp.broadcast_to(jnp.sum(g * st, axis=0, keepdims=True), (HG_BLOCK, GW)))
                g = g * el[HG_BLOCK * c:HG_BLOCK * c + 1, sl] + _dotg(dob[rows], qb[rows], TN) * bd_ref[...]
            g_ref[p] = g
            up = lambda parts: jnp.concatenate(parts[::-1], axis=0)
            dqd_parts.append(_unstack_heads(r, head, t) + up(dqd_c))
            dv_parts.append(_dotg(a, dos, TN) + up(dv_c))
            dke_parts.append(up(dke_c))
            del_parts.append(up(del_c))
        wide = lambda parts: jnp.concatenate(parts, axis=1)
        dqd, dke, dki, dvv, del_rows = wide(dqd_parts), wide(dke_parts), wide(dki_parts), wide(dv_parts), wide(del_parts)
        dh_ref[:, :HG_WIDTH] = (dqd * jnp.exp(b)).astype(BF16)
        dh_ref[:, 2 * HG_WIDTH:] = dvv.astype(BF16)
        dke_ke = dke * ke
        db = dqd * qd - dki * ki - dke_ke
        dl_rows = _sel_left(msum_ref[...], dke_ke) + del_rows * el
        is_last = (lax.broadcasted_iota(jnp.int32, (t, HG_WIDTH), 0) & (HG_BLOCK - 1)) == HG_BLOCK - 1
        db = db + jnp.where(is_last, dl_rows, 0.0)
        dlf = _sel_left(mrev_ref[...], db)
        dk = dki * jnp.exp(-b) + dke * jnp.exp(big_l - b)
        df = dlf / f - dk
        dh_ref[:, HG_WIDTH:2 * HG_WIDTH] = (df * (1.0 - lb) * sig * (1.0 - sig)).astype(BF16)
        dlb = jnp.sum(df * (1.0 - sig), axis=0, keepdims=True) * lb * (1.0 - lb)
        dlbl_ref[0:1, :] += dlb
        dlbl_ref[1:2, :] -= dlb

    rrow = lambda j: pl.BlockSpec((t, HG_WIDTH), lambda i: (nt - 1 - i, j))
    full = lambda a: pl.BlockSpec(a.shape, lambda i: (0, 0))
    return pl.pallas_call(
        body,
        grid=(nt,),
        in_specs=[rrow(6), rrow(7), rrow(8), full(lbl), rrow(0),
                  pl.BlockSpec((HG_NC, 64, HG_WIDTH), lambda i: (nt - 1 - i, 0, 0)),
                  full(mcum), full(mrev), full(msum), full(bd), pl.BlockSpec(memory_space=pl.ANY)],
        out_specs=[pl.BlockSpec((t, 3 * HG_WIDTH), lambda i: (nt - 1 - i, 2)),
                   pl.BlockSpec((2, HG_WIDTH), lambda i: (0, 0))],
        out_shape=[jax.ShapeDtypeStruct(dproj.shape, BF16), jax.ShapeDtypeStruct((2, HG_WIDTH), F32)],
        input_output_aliases={10: 0},
        scratch_shapes=[pltpu.VMEM((HEADS // HG_G, GW, GW), F32)],
        compiler_params=_params(("arbitrary",)),
        name="hgrn_bwd",
    )(proj, proj, proj, lbl, do, sprev, mcum, mrev, msum, bd, dproj)


def _tail(x, tgt, proj, attn, o, w_a, w_b, w_out, w_at, w_bt, w_outt, b_gate, g_post, gh):
    s = x.shape[0]
    tm = 256
    ones64 = (jnp.arange(HG_WIDTH)[:, None] // 64 == jnp.arange(HG_WIDTH)[None, :] // 64).astype(BF16)
    weights = (w_a, w_b, w_out, w_at, w_bt, w_outt)

    def body(x_ref, t_ref, ml_ref, ga_ref, gb_ref, at_ref, o_ref, *rest):
        w_hbm, (bg_ref, gp_ref, gh_ref, ones_ref) = rest[:6], rest[6:10]
        (dout_ref, dpj_ref, dop_ref, do_ref, mt_ref, dy_ref, yat_ref, dya_ref, ybt_ref, dyb_ref,
         loss_ref, dgp_ref, dbg_ref, dgh_ref) = rest[10:24]
        (wa_ref, wb_ref, wo_ref, wat_ref, wbt_ref, wot_ref), w_sem = rest[24:30], rest[30]

        @pl.when(pl.program_id(0) == 0)
        def _():
            loads = [pltpu.make_async_copy(src, dst, w_sem.at[k])
                     for k, (src, dst) in enumerate(zip(w_hbm, rest[24:30]))]
            _start_all(loads, [])
            loss_ref[...] = jnp.zeros_like(loss_ref)
            dgp_ref[...] = jnp.zeros_like(dgp_ref)
            dbg_ref[...] = jnp.zeros_like(dbg_ref)
            dgh_ref[...] = jnp.zeros_like(dgh_ref)
            _wait_all(loads, [])

        ones = ones_ref[...]
        gate_a = ga_ref[...]
        sa = _sigmoid(gate_a)
        silu_a = gate_a * sa
        attn_v = at_ref[...]
        ya_in = attn_v * silu_a
        ov = o_ref[...]
        ro = lax.rsqrt(_sel_right(ov * ov, ones) * (1.0 / 64.0) + EPS)
        ohat = ov * ro
        ghv = gh_ref[...]
        on = ohat * ghv
        gate_b = gb_ref[...]
        sb = _sigmoid(gate_b)
        silu_b = gate_b * sb
        yb_in = on * silu_b
        ya_bf = ya_in.astype(BF16)
        yb_bf = yb_in.astype(BF16)
        yat_ref[...] = ya_bf.T
        ybt_ref[...] = yb_bf.T
        y_a = _dot(ya_bf, wa_ref[...])
        y_b = _dot(yb_bf, wb_ref[...])
        gts = _sigmoid(ml_ref[...] + bg_ref[...])
        g_a = gts[:, :D_MODEL]
        g_b = gts[:, D_MODEL:]
        m_bf = (g_a * y_a + g_b * y_b).astype(BF16)
        mt_ref[...] = m_bf.T
        y = _dot(m_bf, wo_ref[...])
        r1 = lax.rsqrt(jnp.mean(y * y, axis=-1, keepdims=True) + EPS)
        yn = y * r1
        gp = gp_ref[...]
        e = x_ref[...] + yn * gp - t_ref[...]
        loss_ref[...] += jnp.sum(e * e, axis=0, keepdims=True)
        dout = e * (1.0 / D_MODEL)
        dout_ref[...] = dout
        dgp_ref[...] += jnp.sum(dout * yn, axis=0, keepdims=True)
        dyn = dout * gp
        dy = r1 * (dyn - yn * jnp.mean(dyn * yn, axis=-1, keepdims=True))
        dy_bf = dy.astype(BF16)
        dy_ref[...] = dy_bf
        dm = _dot(dy_bf, wot_ref[...])
        dml_a = dm * y_a * g_a * (1.0 - g_a)
        dml_b = dm * y_b * g_b * (1.0 - g_b)
        dpj_ref[:, :D_MODEL] = dml_a.astype(BF16)
        dpj_ref[:, D_MODEL:2 * D_MODEL] = dml_b.astype(BF16)
        dbg_ref[:, :D_MODEL] += jnp.sum(dml_a, axis=0, keepdims=True)
        dbg_ref[:, D_MODEL:] += jnp.sum(dml_b, axis=0, keepdims=True)
        dya_bf = (dm * g_a).astype(BF16)
        dyb_bf = (dm * g_b).astype(BF16)
        dya_ref[...] = dya_bf
        dyb_ref[...] = dyb_bf
        dya_in = _dot(dya_bf, wat_ref[...])
        dyb_in = _dot(dyb_bf, wbt_ref[...])
        dattn = dya_in * silu_a
        delta = _sel_right(dattn * attn_v, ones)
        lane = lax.broadcasted_iota(jnp.int32, (tm, LANE), 1)
        for p in range(HEADS // 2):
            sl = slice(LANE * p, LANE * (p + 1))
            xs = (dattn[:, sl], pltpu.roll(dattn[:, sl], VDIM, 1))
            nds = (-pltpu.roll(delta[:, sl], VDIM, 1), -delta[:, sl])
            for a in range(2):
                hi, lo_part = _hi_lo(nds[a])
                blk = jnp.where(lane < VDIM, xs[a], jnp.where(lane == VDIM, hi, jnp.where(lane == VDIM + 1, lo_part, 0.0)))
                dop_ref[:, LANE * (2 * p + a):LANE * (2 * p + a + 1)] = blk.astype(BF16)
        dpj_ref[:, 2 * D_MODEL:2 * D_MODEL + HG_WIDTH] = (
            dya_in * attn_v * (sa * (1.0 + gate_a * (1.0 - sa)))).astype(BF16)
        don = dyb_in * silu_b
        dpj_ref[:, 2 * D_MODEL + HG_WIDTH:] = (dyb_in * on * (sb * (1.0 + gate_b * (1.0 - sb)))).astype(BF16)
        dgh_ref[...] += jnp.sum(don * ohat, axis=0, keepdims=True)
        dohat = don * ghv
        do_ref[...] = ro * (dohat - ohat * (_sel_right(dohat * ohat, ones) * (1.0 / 64.0)))

    row = lambda w, j: pl.BlockSpec((tm, w), lambda i: (i, j))
    col = lambda w: pl.BlockSpec((w, tm), lambda i: (0, i))
    full = lambda a: pl.BlockSpec(a.shape, lambda i: (0, 0))
    acc = lambda w: pl.BlockSpec((1, w), lambda i: (0, 0))
    sds = lambda w, dt: jax.ShapeDtypeStruct((s, w), dt)
    sdt = lambda w: jax.ShapeDtypeStruct((w, s), BF16)
    return pl.pallas_call(
        body,
        grid=(s // tm,),
        in_specs=[row(1024, 0), row(1024, 0), row(2048, 0), row(512, 4), row(512, 5), row(512, 0), row(512, 0)]
        + [ANY] * 6 + [full(b_gate), full(g_post), full(gh), full(ones64)],
        out_specs=[row(1024, 0), row(3072, 0), row(1024, 0), row(512, 0),
                   col(1024), row(1024, 0), col(512), row(1024, 0), col(512), row(1024, 0),
                   acc(1024), acc(1024), acc(2048), acc(512)],
        out_shape=[sds(1024, F32), sds(D_IN_PAD, BF16), sds(1024, BF16), sds(512, F32),
                   sdt(1024), sds(1024, BF16), sdt(512), sds(1024, BF16), sdt(512), sds(1024, BF16),
                   jax.ShapeDtypeStruct((1, 1024), F32), jax.ShapeDtypeStruct((1, 1024), F32),
                   jax.ShapeDtypeStruct((1, 2048), F32), jax.ShapeDtypeStruct((1, 512), F32)],
        scratch_shapes=[pltpu.VMEM(a.shape, BF16) for a in weights] + [pltpu.SemaphoreType.DMA((6,))],
        compiler_params=_params(("arbitrary",), 56),
        name="tail",
    )(x, tgt, proj, proj, proj, attn, o, *weights, b_gate, g_post, gh, ones64)


def _mla_bwd(proj, dqr, dkr, dv, g_q, g_kv, w_uq_pt, w_kv_pt, rc, rs1, rs2, dproj):
    s = proj.shape[0]
    tm = 256
    scale = 1.0 / math.sqrt(QK)

    def body(cq_ref, ckv_ref, dqr_ref, dkr_ref, dv_ref, gq_ref, gkv_ref, wuqt_ref, wkvt_ref, c_ref, s1_ref, s2_ref,
             dproj_in, dqf_ref, dkvf_ref, dc_ref, dgq_ref, dgkv_ref):
        del dproj_in

        @pl.when(pl.program_id(0) == 0)
        def _():
            dgq_ref[...] = jnp.zeros_like(dgq_ref)
            dgkv_ref[...] = jnp.zeros_like(dgkv_ref)

        c, s1, s2 = c_ref[...], s1_ref[...], s2_ref[...]
        lane = lax.broadcasted_iota(jnp.int32, (tm, LANE), 1)
        ksum = jnp.zeros((tm, LANE), F32)
        for h in range(HEADS):
            sl = slice(LANE * h, LANE * (h + 1))
            dqf_ref[:, sl] = (_unrope(dqr_ref[:, sl], c, s1, s2) * scale).astype(BF16)
            dkh = dkr_ref[:, sl]
            ksum = ksum + dkh
            dkvf_ref[:, sl] = jnp.where(lane < NOPE, dkh, 0.0).astype(BF16)
            dkvf_ref[:, HEADS * LANE + LANE * h:HEADS * LANE + LANE * (h + 1)] = jnp.where(
                lane < VDIM, dv_ref[:, sl], 0.0).astype(BF16)
        dkpe = _unrope(ksum, c, s1, s2)
        dc_ref[:, Q_LORA + KV_LORA:] = jnp.where((lane >= NOPE) & (lane < QK), dkpe, 0.0).astype(BF16)
        dcqn = _dot(dqf_ref[...], wuqt_ref[...])
        dckvn = _dot(dkvf_ref[...], wkvt_ref[...])
        for x_ref, g_ref, dn, cols, dg_ref in ((cq_ref, gq_ref, dcqn, slice(0, Q_LORA), dgq_ref),
                                               (ckv_ref, gkv_ref, dckvn, slice(Q_LORA, Q_LORA + KV_LORA), dgkv_ref)):
            xv = x_ref[...]
            r = lax.rsqrt(jnp.mean(xv * xv, axis=-1, keepdims=True) + EPS)
            xh = xv * r
            dg_ref[...] += jnp.sum(dn * xh, axis=0, keepdims=True)
            dh = dn * g_ref[...]
            dc_ref[:, cols] = (r * (dh - xh * jnp.mean(dh * xh, axis=-1, keepdims=True))).astype(BF16)

    row = lambda w, j: pl.BlockSpec((tm, w), lambda i: (i, j))
    full = lambda a: pl.BlockSpec(a.shape, lambda i: (0, 0))
    acc = lambda w: pl.BlockSpec((1, w), lambda i: (0, 0))
    sds = lambda w, dt: jax.ShapeDtypeStruct((s, w), dt)
    return pl.pallas_call(
        body,
        grid=(s // tm,),
        in_specs=[row(768, 6), row(256, 21), row(1024, 0), row(1024, 0), row(1024, 0), full(g_q), full(g_kv),
                  full(w_uq_pt), full(w_kv_pt), row(128, 0), row(128, 0), row(128, 0),
                  pl.BlockSpec(memory_space=pl.ANY)],
        out_specs=[row(1024, 0), row(2048, 0), row(1152, 4), acc(768), acc(256)],
        out_shape=[sds(1024, BF16), sds(2048, BF16), jax.ShapeDtypeStruct(dproj.shape, BF16),
                   jax.ShapeDtypeStruct((1, 768), F32), jax.ShapeDtypeStruct((1, 256), F32)],
        input_output_aliases={12: 2},
        compiler_params=_params(("arbitrary",)),
        name="mla_bwd",
    )(proj, proj, dqr, dkr, dv, g_q, g_kv, w_uq_pt, w_kv_pt, rc, rs1, rs2, dproj)


def _pick(n, options):
    for o in options:
        if n % o == 0:
            return o
    raise ValueError(n)


def _matmul(a, b, name):
    m, k = a.shape
    n = b.shape[1]
    tm = _pick(m, (1024, 768, 512, 256))
    tn = _pick(n, (1152, 1024, 768, 512))
    tk = _pick(k, (1024, 512))
    nk = k // tk

    def body(a_ref, b_ref, o_ref):
        @pl.when(pl.program_id(2) == 0)
        def _():
            o_ref[...] = jnp.zeros_like(o_ref)

        o_ref[...] += _dot(a_ref[...], b_ref[...])

    return pl.pallas_call(
        body,
        grid=(m // tm, n // tn, nk),
        in_specs=[pl.BlockSpec((tm, tk), lambda i, j, l: (i, l)), pl.BlockSpec((tk, tn), lambda i, j, l: (l, j))],
        out_specs=pl.BlockSpec((tm, tn), lambda i, j, l: (i, j)),
        out_shape=jax.ShapeDtypeStruct((m, n), F32),
        compiler_params=_params(("arbitrary", "arbitrary", "arbitrary")),
        name=name,
    )(a, b)


def _dh_dx(dproj, w_in_pt, x, dout, g_pre, sends):
    s, k = dproj.shape
    tm = 256
    ns, ni = len(sends), s // tm

    def body(dp_ref, w_ref, x_ref, dout_ref, g_ref, *rest):
        send_refs, (dx_ref, dg_ref) = rest[:ns], rest[ns:ns + 2]
        recv_refs, sems = rest[ns + 2:2 * ns + 2], rest[2 * ns + 2:]

        @pl.when(pl.program_id(0) == 0)
        def _():
            _start_all(*_to_chips_copies(send_refs, recv_refs, sems))
            dg_ref[...] = jnp.zeros_like(dg_ref)

        dh = _dot(dp_ref[...], w_ref[...])
        xv = x_ref[...]
        r = lax.rsqrt(jnp.mean(xv * xv, axis=-1, keepdims=True) + EPS)
        xh = xv * r
        dg_ref[...] += jnp.sum(dh * xh, axis=0, keepdims=True)
        dxh = dh * g_ref[...]
        dx_ref[...] = dout_ref[...] + r * (dxh - xh * jnp.mean(dxh * xh, axis=-1, keepdims=True))

        @pl.when(pl.program_id(0) == ni - 1)
        def _():
            _wait_all(*_to_chips_copies(send_refs, recv_refs, sems))

    row = lambda w: pl.BlockSpec((tm, w), lambda i: (i, 0))
    return pl.pallas_call(
        body,
        grid=(ni,),
        in_specs=[row(k), pl.BlockSpec((k, D_MODEL), lambda i: (0, 0)), row(D_MODEL), row(D_MODEL),
                  pl.BlockSpec((1, D_MODEL), lambda i: (0, 0))] + [ANY] * ns,
        out_specs=[row(D_MODEL), pl.BlockSpec((1, D_MODEL), lambda i: (0, 0))] + [ANY] * ns,
        out_shape=[jax.ShapeDtypeStruct((s, D_MODEL), F32), jax.ShapeDtypeStruct((1, D_MODEL), F32)]
        + [jax.ShapeDtypeStruct(a.shape, a.dtype) for a in sends],
        scratch_shapes=_copy_sems(ns, 3),
        compiler_params=_params(("arbitrary",)),
        name="dh_dx",
    )(dproj, w_in_pt, x, dout, g_pre, *sends)


def _pair_reduce(slots):
    n = len(slots)
    half = [(N_DEV // 2,) + a.shape[1:] for a in slots]

    def body(*refs):
        s_refs, o_refs = refs[:n], refs[n:2 * n]
        mine, got = refs[2 * n:3 * n], refs[3 * n:4 * n]
        send_sems, recv_sems, local_sems = refs[4 * n:]
        x, y, c = _my_place()
        copies, loads = [], []
        for a in range(n):
            for q in range(N_DEV // 2):
                copies.append(pltpu.make_async_remote_copy(
                    src_ref=s_refs[a].at[2 * q + 1 - c], dst_ref=got[a].at[q],
                    send_sem=send_sems.at[4 * a + q], recv_sem=recv_sems.at[4 * a + q],
                    device_id=(x, y, 1 - c), device_id_type=MESH_ID))
                loads.append(pltpu.make_async_copy(s_refs[a].at[2 * q + c], mine[a].at[q], local_sems.at[4 * a + q]))
        _start_all(loads, copies)
        _wait_all(loads, copies)
        for a in range(n):
            o_refs[a][...] = (mine[a][...].astype(F32) + got[a][...].astype(F32)).astype(o_refs[a].dtype)

    vm = lambda: [pltpu.VMEM(h, a.dtype) for h, a in zip(half, slots)]
    return pl.pallas_call(
        body,
        in_specs=[ANY] * n,
        out_shape=[jax.ShapeDtypeStruct(h, a.dtype) for h, a in zip(half, slots)],
        scratch_shapes=vm() + vm() + [pltpu.SemaphoreType.DMA((4 * n,)), pltpu.SemaphoreType.DMA((4 * n,)),
                                      pltpu.SemaphoreType.DMA((4 * n,))],
        compiler_params=pltpu.CompilerParams(vmem_limit_bytes=48 * 2**20),
        name="pair_reduce",
    )(*slots)


def _rope_tables(s):
    inv = (np.float32(ROPE_THETA) ** (-np.arange(0, ROPE, 2, dtype=np.float32) / np.float32(ROPE))).astype(np.float32)
    ang = (np.arange(s, dtype=np.float32)[:, None] * inv[None, :]).astype(np.float32)
    cos, sin = jnp.asarray(np.cos(ang.astype(np.float64)), F32), jnp.asarray(np.sin(ang.astype(np.float64)), F32)
    z = lambda w: jnp.zeros((s, w), F32)
    rc = jnp.concatenate([jnp.ones((s, NOPE), F32), cos, cos, z(32)], axis=1)
    rs1 = jnp.concatenate([z(NOPE), -sin, z(16), z(32)], axis=1)
    rs2 = jnp.concatenate([z(NOPE), z(16), sin, z(32)], axis=1)
    return rc, rs1, rs2


def _step(x, tgt, w_in_slots, shards, g_pre, b_gate, g_q, g_kv, lbl, g_hgrn, g_post):
    s = x.shape[0]
    w_in_p, w_in_pt = _assemble_w_in(w_in_slots)
    rc, rs1, rs2 = _rope_tables(s)
    gh = jnp.tile(g_hgrn, (1, HEADS))

    proj, ht, *got = _norm_proj(x, g_pre, w_in_p, shards)
    w_uq, w_ukv, w_a, w_b, w_out = (_from_slots(n, g) for n, g in zip(MATS, got))
    w_uq_p = jnp.pad(w_uq.reshape(Q_LORA, HEADS, QK), ((0, 0), (0, 0), (0, LANE - QK))).reshape(Q_LORA, HEADS * LANE)
    kv3 = w_ukv.reshape(KV_LORA, HEADS, NOPE + VDIM)
    pad64 = lambda t: jnp.pad(t, ((0, 0), (0, 0), (0, LANE - 64))).reshape(KV_LORA, HEADS * LANE)
    w_kv_p = jnp.concatenate([pad64(kv3[:, :, :NOPE]), pad64(kv3[:, :, NOPE:])], axis=1)

    qr, kr, v, cqt, ckvt = _mla_prep(proj, g_q, g_kv, w_uq_p, w_kv_p, rc, rs1, rs2)
    attn, qa = _attn_fwd(qr, kr, v)
    o, sprev = _hgrn_fwd(proj, lbl)
    (dout, dproj, dop, do, mt, dy_bf, yat, dya_bf, ybt, dyb_bf,
     loss_vec, dg_post, db_gate, dgh) = _tail(x, tgt, proj, attn, o, w_a, w_b, w_out, w_a.T, w_b.T, w_out.T,
                                               b_gate, g_post, gh)
    early = [_to_slots(n, _matmul(a, b, "d" + n)).astype(BF16)
             for n, a, b in (("w_branch_a", yat, dya_bf), ("w_branch_b", ybt, dyb_bf), ("w_out", mt, dy_bf))]
    dqr, dkr, dv, *early_recv = _attn_bwd(qa, kr, v, dop, early)
    dproj, dlbl = _hgrn_bwd(proj, lbl, do, sprev, dproj)
    dqf, dkvf, dproj, dg_q, dg_kv = _mla_bwd(proj, dqr, dkr, dv, g_q, g_kv, w_uq_p.T, w_kv_p.T, rc, rs1, rs2, dproj)

    dw_in_slots = _scatter_w_in(_matmul(ht, dproj, "dw_in"))
    dw_uq_p = _matmul(cqt, dqf, "dw_uq")
    dw_kv_p = _matmul(ckvt, dkvf, "dw_kv")
    dw_uq = dw_uq_p.reshape(Q_LORA, HEADS, LANE)[:, :, :QK].reshape(Q_LORA, HEADS * QK)
    dw_ukv = jnp.concatenate([dw_kv_p[:, :HEADS * LANE].reshape(KV_LORA, HEADS, LANE)[:, :, :NOPE],
                              dw_kv_p[:, HEADS * LANE:].reshape(KV_LORA, HEADS, LANE)[:, :, :VDIM]],
                             axis=2).reshape(KV_LORA, 1024)
    late = _pair_reduce([dw_in_slots, _to_slots("w_uq", dw_uq).astype(BF16), _to_slots("w_ukv", dw_ukv).astype(BF16)])
    dx, dg_pre, *late_recv = _dh_dx(dproj, w_in_pt, x, dout, g_pre, late)

    g_sum = _vectors_sum(dg_pre, db_gate, dg_q, dg_kv, dlbl, dgh, dg_post, loss_vec)
    return dx, late_recv[0], dict(zip(MATS, late_recv[1:] + early_recv)), g_sum


def _all_gather(blocks):
    n = len(blocks)

    def body(*refs):
        x_refs, out_refs = refs[:n], refs[n:2 * n]
        send_sems, recv_sems, local_sems = refs[2 * n:]
        x, y, c = _my_place()
        me, sibling = (x, y, c), (x, y, 1 - c)
        chips = [(1 - x, y), (x, 1 - y), (1 - x, 1 - y)]

        def slot(a, px, py, pc):
            return out_refs[a].at[4 * px + 2 * py + pc]

        def copy(a, k, blk, to, src=None):
            return pltpu.make_async_remote_copy(
                src_ref=slot(a, *blk) if src is None else src, dst_ref=slot(a, *blk),
                send_sem=send_sems.at[7 * a + k], recv_sem=recv_sems.at[7 * a + k],
                device_id=to, device_id_type=MESH_ID)

        mine = [pltpu.make_async_copy(x_refs[a], slot(a, *me), local_sems.at[a]) for a in range(n)]
        for cp in mine:
            cp.start()
        first = [copy(a, 0, me, sibling, src=x_refs[a]) for a in range(n)]
        first += [copy(a, 1 + j, me, (*chip, c), src=x_refs[a]) for a in range(n) for j, chip in enumerate(chips)]
        for cp in first:
            cp.start()
        passed = []
        for j, chip in enumerate(chips):
            for a in range(n):
                copy(a, 1 + j, (*chip, c), me).wait_recv()
                passed.append(copy(a, 4 + j, (*chip, c), sibling))
                passed[-1].start()
        for a in range(n):
            copy(a, 0, sibling, me).wait_recv()
        for j, chip in enumerate(chips):
            for a in range(n):
                copy(a, 4 + j, (*chip, 1 - c), me).wait_recv()
        for cp in first + passed:
            cp.wait_send()
        for cp in mine:
            cp.wait()

    return pl.pallas_call(
        body,
        out_shape=[jax.ShapeDtypeStruct((N_DEV,) + b.shape, b.dtype) for b in blocks],
        in_specs=[pl.BlockSpec(memory_space=pl.ANY)] * n,
        out_specs=[pl.BlockSpec(memory_space=pl.ANY)] * n,
        scratch_shapes=[pltpu.SemaphoreType.DMA((7 * n,)), pltpu.SemaphoreType.DMA((7 * n,)),
                        pltpu.SemaphoreType.DMA((n,))],
        name="gather_weights",
    )(*blocks)


def _adamw(g, w, m, v):
    c1 = 1.0 / (1.0 - ADAM_B1 ** ADAM_STEP)
    c2 = 1.0 / (1.0 - ADAM_B2 ** ADAM_STEP)
    nm = ADAM_B1 * m + (1.0 - ADAM_B1) * g
    nv = ADAM_B2 * v + (1.0 - ADAM_B2) * (g * g)
    d = -ADAM_LR * ((nm * c1) / (jnp.sqrt(nv * c2) + ADAM_EPS) + ADAM_WD * w)
    return d, nm, nv


def _sum8(r_ref):
    g = r_ref[0].astype(F32)
    for k in range(1, r_ref.shape[0]):
        g = g + r_ref[k].astype(F32)
    return g


def _sum_adamw_w_in(recv, w, m, v):
    rows, _, cols = w.shape
    tc = 256

    def body(r_ref, w_ref, m_ref, v_ref, g_ref, d_ref, nm_ref, nv_ref):
        g = _sum8(r_ref)
        dense = lambda ref: ref[...].reshape(rows, tc)
        d, nm, nv = _adamw(g, dense(w_ref), dense(m_ref), dense(v_ref))
        for ref, val in ((g_ref, g), (d_ref, d), (nm_ref, nm), (nv_ref, nv)):
            ref[...] = val.reshape(rows, 1, tc)

    blk = pl.BlockSpec((rows, 1, tc), lambda i: (0, 0, i))
    out = jax.ShapeDtypeStruct((rows, 1, cols), F32)
    return pl.pallas_call(
        body,
        grid=(cols // tc,),
        in_specs=[pl.BlockSpec((recv.shape[0], rows, tc), lambda i: (0, 0, i)), blk, blk, blk],
        out_specs=[blk, blk, blk, blk],
        out_shape=[out, out, out, out],
        compiler_params=_params(("arbitrary",)),
        name="sum_adamw_w_in",
    )(recv, w, m, v)


def _sum_adamw_whole(recvs, ws, ms, vs):
    n = len(ws)

    def body(*refs):
        r_refs, w_refs, m_refs, v_refs = refs[:n], refs[n:2 * n], refs[2 * n:3 * n], refs[3 * n:4 * n]
        outs = refs[4 * n:]
        for a in range(n):
            g = _sum8(r_refs[a])
            d, nm, nv = _adamw(g, w_refs[a][...], m_refs[a][...], v_refs[a][...])
            outs[a][...] = g
            outs[n + a][...] = d
            outs[2 * n + a][...] = nm
            outs[3 * n + a][...] = nv

    shapes = [jax.ShapeDtypeStruct(w.shape, F32) for w in ws]
    res = pl.pallas_call(
        body,
        out_shape=shapes * 4,
        compiler_params=pltpu.CompilerParams(vmem_limit_bytes=48 * 2**20),
        name="sum_adamw_mats",
    )(*recvs, *ws, *ms, *vs)
    return res[:n], res[n:2 * n], res[2 * n:3 * n], res[3 * n:]


SMALL = ("g_pre", "b_gate", "g_q", "g_kv", "lb_logits", "g_hgrn", "g_post")
SMALL_SHAPE = dict(g_pre=(1, 1024), b_gate=(1, 2048), g_q=(1, 768), g_kv=(1, 256), lb_logits=(2, 512),
                   g_hgrn=(1, 64), g_post=(1, 1024))


def _vectors_sum(dg_pre, db_gate, dg_q, dg_kv, dlbl, dgh, dg_post, loss_vec):
    def body(gpre_ref, bg_ref, gq_ref, gkv_ref, lbl_ref, gh_ref, gpost_ref, loss_ref, out_ref, mine, got,
             send_sems, recv_sems):
        mine[...] = jnp.zeros_like(mine)
        mine[0:1, :] = gpre_ref[...]
        mine[1:2, :] = bg_ref[:, :1024]
        mine[2:3, :] = bg_ref[:, 1024:]
        mine[3:4, :Q_LORA] = gq_ref[...]
        mine[4:5, :KV_LORA] = gkv_ref[...]
        loss = (0.5 / D_MODEL) * jnp.sum(loss_ref[...], axis=-1, keepdims=True)
        mine[4:5, KV_LORA:] = jnp.broadcast_to(loss, (1, 1024 - KV_LORA))
        mine[5:6, :HG_WIDTH] = lbl_ref[0:1, :]
        mine[5:6, HG_WIDTH:] = lbl_ref[1:2, :]
        gh = gh_ref[...]
        fold = gh[:, :VDIM]
        for h in range(1, HEADS):
            fold = fold + gh[:, VDIM * h:VDIM * (h + 1)]
        mine[6:7, :VDIM] = fold
        mine[7:8, :] = gpost_ref[...]
        x, y, c = _my_place()
        me = 4 * x + 2 * y + c
        got[me] = mine[...]
        copies = [pltpu.make_async_remote_copy(
            src_ref=mine, dst_ref=got.at[me], send_sem=send_sems.at[k], recv_sem=recv_sems.at[k],
            device_id=_flip(k, x, y, c), device_id_type=MESH_ID) for k in range(N_DEV - 1)]
        _start_all([], copies)
        _wait_all([], copies)
        out_ref[...] = _sum8(got)

    return pl.pallas_call(
        body,
        out_shape=jax.ShapeDtypeStruct((8, 1024), F32),
        scratch_shapes=[pltpu.VMEM((8, 1024), F32), pltpu.VMEM((N_DEV, 8, 1024), F32),
                        pltpu.SemaphoreType.DMA((7,)), pltpu.SemaphoreType.DMA((7,))],
        name="vectors_sum",
    )(dg_pre, db_gate, dg_q, dg_kv, dlbl, dgh, dg_post, loss_vec)


def _vectors_adamw(g_sum, ws, ms, vs):
    n = len(SMALL)

    def body(g_ref, *refs):
        w_refs, m_refs, v_refs = refs[:n], refs[n:2 * n], refs[2 * n:3 * n]
        loss_ref, outs = refs[3 * n], refs[3 * n + 1:]
        g = g_ref[...]
        loss_ref[...] = g[4:5, KV_LORA:KV_LORA + 1]
        grads = (g[0:1, :], jnp.concatenate([g[1:2, :], g[2:3, :]], axis=1), g[3:4, :Q_LORA], g[4:5, :KV_LORA],
                 jnp.concatenate([g[5:6, :HG_WIDTH], g[5:6, HG_WIDTH:]], axis=0), g[6:7, :VDIM], g[7:8, :])
        for a in range(n):
            d, nm, nv = _adamw(grads[a], w_refs[a][...], m_refs[a][...], v_refs[a][...])
            outs[a][...] = grads[a]
            outs[n + a][...] = d
            outs[2 * n + a][...] = nm
            outs[3 * n + a][...] = nv

    shapes = [jax.ShapeDtypeStruct(SMALL_SHAPE[k], F32) for k in SMALL]
    res = pl.pallas_call(
        body,
        out_shape=[jax.ShapeDtypeStruct((1, 1), F32)] + shapes * 4,
        name="vectors_adamw",
    )(g_sum, *ws, *ms, *vs)
    return res[0], res[1:n + 1], res[n + 1:2 * n + 1], res[2 * n + 1:3 * n + 1], res[3 * n + 1:]


MATS = ("w_uq", "w_ukv", "w_branch_a", "w_branch_b", "w_out")
COL_SHARDED = dict(w_uq=False, w_ukv=True, w_branch_a=True, w_branch_b=True, w_out=False)
ORDER = ("g_pre", "w_in", "b_gate", "g_q", "w_uq", "g_kv", "w_ukv", "lb_logits", "g_hgrn",
         "w_branch_a", "w_branch_b", "w_out", "g_post")


def _to_slots(name, full):
    r, c = full.shape
    if COL_SHARDED[name]:
        return full.reshape(r, N_DEV, c // N_DEV).transpose(1, 0, 2)
    return full.reshape(N_DEV, r // N_DEV, c)


def _from_slots(name, slots):
    _, r, c = slots.shape
    if COL_SHARDED[name]:
        return slots.transpose(1, 0, 2).reshape(r, N_DEV * c)
    return slots.reshape(N_DEV * r, c)


def kernel(x, g_pre, w_in, b_gate, g_q, w_uq, g_kv, w_ukv, lb_logits, g_hgrn, w_branch_a, w_branch_b, w_out, g_post, loss_target, m_g_pre, m_w_in, m_b_gate, m_g_q, m_w_uq, m_g_kv, m_w_ukv, m_lb_logits, m_g_hgrn, m_w_branch_a, m_w_branch_b, m_w_out, m_g_post, v_g_pre, v_w_in, v_b_gate, v_g_q, v_w_uq, v_g_kv, v_w_ukv, v_lb_logits, v_g_hgrn, v_w_branch_a, v_w_branch_b, v_w_out, v_g_post):
    rows3 = lambda a: jnp.transpose(a, (2, 0, 1))
    w = dict(w_in=rows3(w_in), w_uq=w_uq[0], w_ukv=w_ukv[0], w_branch_a=w_branch_a[0], w_branch_b=w_branch_b[0],
             w_out=w_out[0], g_pre=g_pre, b_gate=b_gate, g_q=g_q, g_kv=g_kv, lb_logits=lb_logits, g_hgrn=g_hgrn,
             g_post=g_post)
    mom = dict(w_in=rows3(m_w_in), w_uq=m_w_uq[0], w_ukv=m_w_ukv[0], w_branch_a=m_w_branch_a[0],
               w_branch_b=m_w_branch_b[0], w_out=m_w_out[0], g_pre=m_g_pre, b_gate=m_b_gate, g_q=m_g_q, g_kv=m_g_kv,
               lb_logits=m_lb_logits, g_hgrn=m_g_hgrn, g_post=m_g_post)
    var = dict(w_in=rows3(v_w_in), w_uq=v_w_uq[0], w_ukv=v_w_ukv[0], w_branch_a=v_w_branch_a[0],
               w_branch_b=v_w_branch_b[0], w_out=v_w_out[0], g_pre=v_g_pre, b_gate=v_b_gate, g_q=v_g_q, g_kv=v_g_kv,
               lb_logits=v_lb_logits, g_hgrn=v_g_hgrn, g_post=v_g_post)

    (w_in_slots,) = _all_gather([w["w_in"].reshape(W_IN_SHARD, D_MODEL).astype(BF16)])
    dx, recv_in, recv, g_sum = _step(x[0], loss_target[0], w_in_slots, [w[n].astype(BF16) for n in MATS],
                                     g_pre, b_gate, g_q, g_kv, lb_logits, g_hgrn, g_post)

    g_in, d_in, m_in, v_in = _sum_adamw_w_in(recv_in, w["w_in"], mom["w_in"], var["w_in"])
    res = _sum_adamw_whole([recv[n] for n in MATS], *([t[n] for n in MATS] for t in (w, mom, var)))
    total, *vec = _vectors_adamw(g_sum, *([t[n] for n in SMALL] for t in (w, mom, var)))

    outs = []
    for mats, vecs, big in zip(res, vec, (g_in, d_in, m_in, v_in)):
        t = {**{n: a[None] for n, a in zip(MATS, mats)}, **dict(zip(SMALL, vecs)),
             "w_in": jnp.transpose(big, (1, 2, 0))}
        outs += [t[n] for n in ORDER]
    return (total.reshape(()), dx[None], *outs)
```

```python
import math

import jax
import jax.numpy as jnp
import numpy as np
from jax import lax
from jax.experimental import pallas as pl
from jax.experimental.pallas import tpu as pltpu

F32, BF16 = jnp.float32, jnp.bfloat16

D_MODEL = 1024
EPS = 1e-6
HEADS = 8
NOPE, ROPE, VDIM = 64, 32, 64
QK = NOPE + ROPE
Q_LORA, KV_LORA = 768, 256
ROPE_THETA = 10000.0
ATT_CHUNK_SHIFT = 6
HG_BLOCK = 32
HG_WIDTH = 512
D_IN = 5664
D_IN_PAD = 5760
W_IN_SHARD = D_IN // 8
N_DEV = 8
LANE = 128

ADAM_LR, ADAM_B1, ADAM_B2, ADAM_EPS, ADAM_WD, ADAM_STEP = 0.001, 0.9, 0.999, 1e-08, 0.01, 10

W_IN_SEGMENTS = ((3616, 5664, 0), (1056, 1568, 2048), (3104, 3616, 2560), (1568, 3104, 3072),
                 (0, 1024, 4608), (1024, 1056, 5696))
W_IN_ZERO = ((5632, 5696), (5728, 5760))

NT = (((1,), (1,)), ((), ()))
TN = (((0,), (0,)), ((), ()))
MESH_ID = pl.DeviceIdType.MESH


def _w_in_pieces():
    out = []
    for lo, hi, dst in W_IN_SEGMENTS:
        c = lo
        while c < hi:
            p = c // W_IN_SHARD
            e = min(hi, (p + 1) * W_IN_SHARD)
            out.append((p, c - p * W_IN_SHARD, e - p * W_IN_SHARD, dst + c - lo))
            c = e
    return out


def _params(sem, vmem_mb=48):
    return pltpu.CompilerParams(dimension_semantics=sem, vmem_limit_bytes=vmem_mb * 2**20)


def _dot(a, b):
    return jnp.dot(a, b, preferred_element_type=F32)


def _dotg(a, b, dims):
    return lax.dot_general(a, b, dims, preferred_element_type=F32)


def _split2(x):
    hi = x.astype(BF16)
    return hi, (x - hi.astype(F32)).astype(BF16)


def _sel_left(m01, x):
    hi, lo = _split2(x)
    return _dot(m01, hi) + _dot(m01, lo)


def _sel_right(x, m01):
    hi, lo = _split2(x)
    return _dot(hi, m01) + _dot(lo, m01)


def _hi_lo(x):
    hi = x.astype(BF16).astype(F32)
    return hi, x - hi


def _sigmoid(x):
    return 0.5 * jnp.tanh(0.5 * x) + 0.5


def _rope(x, c, s1, s2):
    return x * c + pltpu.roll(x, 112, 1) * s1 + pltpu.roll(x, 16, 1) * s2


def _unrope(d, c, s1, s2):
    return d * c + pltpu.roll(d * s1, 16, 1) + pltpu.roll(d * s2, 112, 1)


def _my_place():
    return lax.axis_index("x"), lax.axis_index("y"), lax.axis_index("c")


def _flip(k, x, y, c):
    fx, fy, fc = (k + 1) >> 2 & 1, (k + 1) >> 1 & 1, (k + 1) & 1
    return (1 - x if fx else x), (1 - y if fy else y), (1 - c if fc else c)


def _to_all_copies(s_refs, r_refs, sems, spread):
    send_sems, recv_sems, local_sems = sems
    x, y, c = _my_place()
    me = 4 * x + 2 * y + c
    src = (lambda a, p: s_refs[a]) if spread else (lambda a, p: s_refs[a].at[p])
    local = [pltpu.make_async_copy(src(a, me), r_refs[a].at[me], local_sems.at[a]) for a in range(len(s_refs))]
    remote = []
    for k in range(N_DEV - 1):
        px, py, pc = _flip(k, x, y, c)
        for a in range(len(s_refs)):
            remote.append(pltpu.make_async_remote_copy(
                src_ref=src(a, 4 * px + 2 * py + pc), dst_ref=r_refs[a].at[me],
                send_sem=send_sems.at[7 * a + k], recv_sem=recv_sems.at[7 * a + k],
                device_id=(px, py, pc), device_id_type=MESH_ID))
    return local, remote


def _to_chips_copies(s_refs, r_refs, sems):
    send_sems, recv_sems, local_sems = sems
    x, y, c = _my_place()
    me = 2 * x + y
    local = [pltpu.make_async_copy(s_refs[a].at[me], r_refs[a].at[me], local_sems.at[a]) for a in range(len(s_refs))]
    remote = []
    for k in range(3):
        px = 1 - x if (k + 1) >> 1 & 1 else x
        py = 1 - y if (k + 1) & 1 else y
        for a in range(len(s_refs)):
            remote.append(pltpu.make_async_remote_copy(
                src_ref=s_refs[a].at[2 * px + py], dst_ref=r_refs[a].at[me],
                send_sem=send_sems.at[3 * a + k], recv_sem=recv_sems.at[3 * a + k],
                device_id=(px, py, c), device_id_type=MESH_ID))
    return local, remote


def _start_all(local, remote):
    for cp in local + remote:
        cp.start()


def _wait_all(local, remote):
    for cp in remote:
        cp.wait_recv()
    for cp in remote:
        cp.wait_send()
    for cp in local:
        cp.wait()


def _copy_sems(n, peers):
    return [pltpu.SemaphoreType.DMA((peers * n,)), pltpu.SemaphoreType.DMA((peers * n,)),
            pltpu.SemaphoreType.DMA((n,))]


ANY = pl.BlockSpec(memory_space=pl.ANY)


def _assemble_w_in(slots):
    tc = 256
    pieces = _w_in_pieces()

    def body(s_ref, w_ref, wt_ref):
        for lo, hi in W_IN_ZERO:
            wt_ref[lo:hi, :] = jnp.zeros((hi - lo, tc), BF16)
        for p, lo, hi, dst in pieces:
            wt_ref[dst:dst + hi - lo, :] = s_ref[p, lo:hi, :]
        w_ref[...] = wt_ref[...].T

    return pl.pallas_call(
        body,
        grid=(D_MODEL // tc,),
        in_specs=[pl.BlockSpec((N_DEV, W_IN_SHARD, tc), lambda i: (0, 0, i))],
        out_specs=[pl.BlockSpec((tc, D_IN_PAD), lambda i: (i, 0)), pl.BlockSpec((D_IN_PAD, tc), lambda i: (0, i))],
        out_shape=[jax.ShapeDtypeStruct((D_MODEL, D_IN_PAD), BF16), jax.ShapeDtypeStruct((D_IN_PAD, D_MODEL), BF16)],
        compiler_params=_params(("arbitrary",)),
        name="assemble_w_in",
    )(slots)


def _scatter_w_in(dw):
    tc = 256
    pieces = _w_in_pieces()

    def body(d_ref, s_ref):
        dt = d_ref[...].T
        for p, lo, hi, dst in pieces:
            s_ref[p, lo:hi, :] = dt[dst:dst + hi - lo, :].astype(BF16)

    return pl.pallas_call(
        body,
        grid=(D_MODEL // tc,),
        in_specs=[pl.BlockSpec((tc, D_IN_PAD), lambda i: (i, 0))],
        out_specs=pl.BlockSpec((N_DEV, W_IN_SHARD, tc), lambda i: (0, 0, i)),
        out_shape=jax.ShapeDtypeStruct((N_DEV, W_IN_SHARD, D_MODEL), BF16),
        compiler_params=_params(("arbitrary",)),
        name="scatter_w_in",
    )(dw)


def _norm_proj(x, g_pre, w, shards):
    s, n = x.shape[0], w.shape[1]
    tm = 512
    ni, ns = s // tm, len(shards)

    def body(x_ref, g_ref, w_hbm, *rest):
        shard_refs, (proj_ref, ht_ref), got_refs = rest[:ns], rest[ns:ns + 2], rest[ns + 2:2 * ns + 2]
        w_ref, w_sem, sems = rest[2 * ns + 2], rest[2 * ns + 3], rest[2 * ns + 4:]
        i = pl.program_id(0)

        @pl.when(i == 0)
        def _():
            load = pltpu.make_async_copy(w_hbm, w_ref, w_sem)
            load.start()
            _start_all(*_to_all_copies(shard_refs, got_refs, sems, True))
            load.wait()

        xv = x_ref[...]
        r = lax.rsqrt(jnp.mean(xv * xv, axis=-1, keepdims=True) + EPS)
        h = (xv * r * g_ref[...]).astype(BF16)
        ht_ref[...] = h.T
        proj_ref[...] = _dot(h, w_ref[...])

        @pl.when(i == ni - 1)
        def _():
            _wait_all(*_to_all_copies(shard_refs, got_refs, sems, True))

    return pl.pallas_call(
        body,
        grid=(ni,),
        in_specs=[pl.BlockSpec((tm, D_MODEL), lambda i: (i, 0)), pl.BlockSpec((1, D_MODEL), lambda i: (0, 0)), ANY]
        + [ANY] * ns,
        out_specs=[pl.BlockSpec((tm, n), lambda i: (i, 0)), pl.BlockSpec((D_MODEL, tm), lambda i: (0, i))] + [ANY] * ns,
        out_shape=[jax.ShapeDtypeStruct((s, n), F32), jax.ShapeDtypeStruct((D_MODEL, s), BF16)]
        + [jax.ShapeDtypeStruct((N_DEV,) + b.shape, b.dtype) for b in shards],
        scratch_shapes=[pltpu.VMEM(w.shape, BF16), pltpu.SemaphoreType.DMA] + _copy_sems(ns, 7),
        compiler_params=_params(("arbitrary",), 56),
        name="norm_proj",
    )(x, g_pre, w, *shards)


def _mla_prep(proj, g_q, g_kv, w_uq_p, w_kv_p, rc, rs1, rs2):
    s = proj.shape[0]
    tm = 256
    scale = 1.0 / math.sqrt(QK)

    def body(cq_ref, ckv_ref, kpe_ref, gq_ref, gkv_ref, wuq_ref, wkv_ref, c_ref, s1_ref, s2_ref,
             qr_ref, kr_ref, v_ref, cqt_ref, ckvt_ref):
        cq = cq_ref[...]
        r = lax.rsqrt(jnp.mean(cq * cq, axis=-1, keepdims=True) + EPS)
        cqn = (cq * r * gq_ref[...]).astype(BF16)
        cqt_ref[...] = cqn.T
        q = _dot(cqn, wuq_ref[...])
        ckv = ckv_ref[...]
        r = lax.rsqrt(jnp.mean(ckv * ckv, axis=-1, keepdims=True) + EPS)
        ckvn = (ckv * r * gkv_ref[...]).astype(BF16)
        ckvt_ref[...] = ckvn.T
        kv = _dot(ckvn, wkv_ref[...])
        c, s1, s2 = c_ref[...], s1_ref[...], s2_ref[...]
        lane = lax.broadcasted_iota(jnp.int32, (tm, LANE), 1)
        kpe = _rope(kpe_ref[...], c, s1, s2) + jnp.where((lane == QK) | (lane == QK + 1), 1.0, 0.0)
        vone = jnp.where((lane == VDIM) | (lane == VDIM + 1), 1.0, 0.0)
        for h in range(HEADS):
            sl = slice(LANE * h, LANE * (h + 1))
            qr_ref[:, sl] = (_rope(q[:, sl], c, s1, s2) * scale).astype(BF16)
            kr_ref[:, sl] = (kv[:, sl] + kpe).astype(BF16)
            v_ref[:, sl] = (kv[:, HEADS * LANE + LANE * h:HEADS * LANE + LANE * (h + 1)] + vone).astype(BF16)

    row = lambda w, j: pl.BlockSpec((tm, w), lambda i: (i, j))
    col = lambda w: pl.BlockSpec((w, tm), lambda i: (0, i))
    full = lambda a: pl.BlockSpec(a.shape, lambda i: (0, 0))
    return pl.pallas_call(
        body,
        grid=(s // tm,),
        in_specs=[row(768, 6), row(256, 21), row(128, 44), full(g_q), full(g_kv), full(w_uq_p), full(w_kv_p),
                  row(128, 0), row(128, 0), row(128, 0)],
        out_specs=[row(1024, 0), row(1024, 0), row(1024, 0), col(768), col(256)],
        out_shape=[jax.ShapeDtypeStruct((s, 1024), BF16), jax.ShapeDtypeStruct((s, 1024), BF16),
                   jax.ShapeDtypeStruct((s, 1024), BF16), jax.ShapeDtypeStruct((768, s), BF16),
                   jax.ShapeDtypeStruct((256, s), BF16)],
        compiler_params=_params(("arbitrary",)),
        name="mla_prep",
    )(proj, proj, proj, g_q, g_kv, w_uq_p, w_kv_p, rc, rs1, rs2)


ATT_T = 512
ATT_FWD_HEADS = 4


def _chunk_mask(transposed):
    r = lax.broadcasted_iota(jnp.int32, (ATT_T, ATT_T), 0) >> ATT_CHUNK_SHIFT
    c = lax.broadcasted_iota(jnp.int32, (ATT_T, ATT_T), 1) >> ATT_CHUNK_SHIFT
    return (r <= c) if transposed else (c <= r)


def _attn_fwd(qr, kr, vp):
    s = qr.shape[0]
    t = ATT_T
    g = ATT_FWD_HEADS

    def body(q_ref, k_ref, v_ref, o_ref, qa_ref, sc_ref):
        qi = pl.program_id(1)
        lane = lax.broadcasted_iota(jnp.int32, (t, LANE), 1)
        sls = [slice(LANE * a, LANE * (a + 1)) for a in range(g)]
        qs = [q_ref[:, sl] for sl in sls]

        def scores(j):
            rows = pl.ds(pl.multiple_of(j * t, t), t)
            for a in range(g):
                sc_ref[j & 1, a] = _dotg(qs[a], k_ref[rows, sls[a]], NT)

        def step(j, carry, masked):
            rows = pl.ds(pl.multiple_of(j * t, t), t)
            out = []
            for a in range(g):
                m, acc = carry[a]
                sc = sc_ref[j & 1, a]
                if masked:
                    sc = jnp.where(_chunk_mask(False), sc, -1e30)
                m_new = jnp.maximum(m, jnp.max(sc, axis=-1, keepdims=True))
                p = jnp.exp(sc - m_new).astype(BF16)
                acc = jnp.exp(m - m_new) * acc + _dot(p, v_ref[rows, sls[a]])
                out.append((m_new, acc))
            return tuple(out)

        def loop(j, carry):
            carry = step(j, carry, False)
            scores(j + 1)
            return carry

        init = tuple((jnp.full((t, 1), -1e30, F32), jnp.zeros((t, LANE), F32)) for _ in range(g))
        scores(0)
        carry = lax.fori_loop(0, qi, loop, init)
        carry = step(qi, carry, True)
        outs = []
        for a in range(g):
            m, acc = carry[a]
            l = acc[:, VDIM:VDIM + 1]
            outs.append(acc / l)
            hi, lo_part = _hi_lo(-(m + jnp.log(l)))
            qa = jnp.where(lane == QK, hi, jnp.where(lane == QK + 1, lo_part, qs[a].astype(F32)))
            qa_ref[:, sls[a]] = qa.astype(BF16)
        for p in range(g // 2):
            o_ref[:, LANE * p:LANE * (p + 1)] = jnp.where(lane < VDIM, outs[2 * p], pltpu.roll(outs[2 * p + 1], VDIM, 1))

    return pl.pallas_call(
        body,
        grid=(HEADS // g, s // t),
        in_specs=[
            pl.BlockSpec((t, g * LANE), lambda h, i: (i, h)),
            pl.BlockSpec((s, g * LANE), lambda h, i: (0, h)),
            pl.BlockSpec((s, g * LANE), lambda h, i: (0, h)),
        ],
        out_specs=[
            pl.BlockSpec((t, g * VDIM), lambda h, i: (i, h)),
            pl.BlockSpec((t, g * LANE), lambda h, i: (i, h)),
        ],
        out_shape=[jax.ShapeDtypeStruct((s, 512), F32), jax.ShapeDtypeStruct((s, 1024), BF16)],
        scratch_shapes=[pltpu.VMEM((2, g, t, t), F32)],
        compiler_params=_params(("arbitrary", "arbitrary")),
        name="attn_fwd",
    )(qr, kr, vp)


def _attn_bwd(qa, kr, vp, dop, sends):
    s = qa.shape[0]
    t = ATT_T
    nq = s // t
    ns = len(sends)

    def body(q_ref, k_ref, v_ref, do_ref, *rest):
        send_refs, (dq_ref, dk_ref, dv_ref) = rest[:ns], rest[ns:ns + 3]
        recv_refs, sems = rest[ns + 3:2 * ns + 3], rest[2 * ns + 3:]
        j = pl.program_id(1)
        sls = [slice(LANE * a, LANE * (a + 1)) for a in range(2)]

        @pl.when((pl.program_id(0) == 0) & (j == 0))
        def _():
            _start_all(*_to_all_copies(send_refs, recv_refs, sems, False))

        @pl.when(j == 0)
        def _():
            dq_ref[...] = jnp.zeros_like(dq_ref)

        dk_ref[...] = jnp.zeros_like(dk_ref)
        dv_ref[...] = jnp.zeros_like(dv_ref)
        ks = [k_ref[:, sl] for sl in sls]
        vs = [v_ref[:, sl] for sl in sls]

        def step(i, masked):
            rows = pl.ds(pl.multiple_of(i * t, t), t)
            for a in range(2):
                q = q_ref[rows, sls[a]]
                do = do_ref[rows, sls[a]]
                sc = _dotg(ks[a], q, NT)
                if masked:
                    sc = jnp.where(_chunk_mask(True), sc, -1e30)
                p = jnp.exp(sc)
                ds = (p * _dotg(vs[a], do, NT)).astype(BF16)
                dv_ref[:, sls[a]] += _dot(p.astype(BF16), do)
                dk_ref[:, sls[a]] += _dot(ds, q)
                dq_ref[rows, sls[a]] += _dotg(ds, ks[a], TN)

        step(j, True)

        def loop(i, c):
            step(i, False)
            return c

        lax.fori_loop(j + 1, nq, loop, 0)

        @pl.when((pl.program_id(0) == HEADS // 2 - 1) & (j == nq - 1))
        def _():
            _wait_all(*_to_all_copies(send_refs, recv_refs, sems, False))

    blk = pl.BlockSpec((t, 2 * LANE), lambda h, j: (j, h))
    whole = pl.BlockSpec((s, 2 * LANE), lambda h, j: (0, h))
    out = jax.ShapeDtypeStruct((s, 1024), F32)
    return pl.pallas_call(
        body,
        grid=(HEADS // 2, nq),
        in_specs=[whole, blk, blk, whole] + [ANY] * ns,
        out_specs=[whole, blk, blk] + [ANY] * ns,
        out_shape=[out, out, out] + [jax.ShapeDtypeStruct(a.shape, a.dtype) for a in sends],
        scratch_shapes=_copy_sems(ns, 7),
        compiler_params=_params(("arbitrary", "arbitrary")),
        name="attn_bwd",
    )(qa, kr, vp, dop, *sends)


HG_T = 256
HG_NC = HG_T // HG_BLOCK
HG_G = 4
GW = 64 * HG_G


def _hg_consts():
    r = jnp.arange(HG_T)[:, None]
    c = jnp.arange(HG_T)[None, :]
    same = (r // HG_BLOCK) == (c // HG_BLOCK)
    mcum = (same & (c <= r)).astype(BF16)
    mrev = (same & (c >= r)).astype(BF16)
    msum = same.astype(BF16)
    a = jnp.arange(GW) // 64
    bd = (a[:, None] == a[None, :]).astype(F32)
    return mcum, mrev, msum, bd


def _stack_heads(xg, head):
    return jnp.concatenate([jnp.where(head == h, xg, 0.0) for h in range(HG_G)], axis=0)


def _unstack_heads(r, head, t):
    out = r[(HG_G - 1) * t:]
    for h in range(HG_G - 2, -1, -1):
        out = jnp.where(head == h, r[h * t:(h + 1) * t], out)
    return out


def _compact_state(st):
    out = st[:64]
    for h in range(1, HG_G):
        out = out + st[64 * h:64 * (h + 1)]
    return out


def _expand_state(cs, head64):
    return jnp.concatenate([jnp.where(head64 == h, cs, 0.0) for h in range(HG_G)], axis=0)


def _hg_pre(hq, hf, lbl, mcum, msum):
    lb = _sigmoid(lbl[0:1, :] - lbl[1:2, :])
    sig = _sigmoid(hf)
    f = lb + (1.0 - lb) * sig
    lf = jnp.log(f)
    b = _sel_left(mcum, lf)
    big_l = _sel_left(msum, lf)
    k = 1.0 - f
    qd = hq * jnp.exp(b)
    ki = k * jnp.exp(-b)
    ke = k * jnp.exp(big_l - b)
    return lb, sig, f, b, big_l, qd, ki, ke


def _hgrn_fwd(proj, lbl):
    s = proj.shape[0]
    t = HG_T
    mcum, _, msum, bd = _hg_consts()

    def body(hq_ref, hf_ref, hi_ref, lbl_ref, mcum_ref, msum_ref, bd_ref, o_ref, sp_ref, st_ref):
        @pl.when(pl.program_id(0) == 0)
        def _():
            st_ref[...] = jnp.zeros_like(st_ref)

        mc = mcum_ref[...]
        _, _, _, _, big_l, qd, ki, ke = _hg_pre(hq_ref[...], hf_ref[...], lbl_ref[...], mc, msum_ref[...])
        el = jnp.exp(big_l)
        hi = hi_ref[...]
        head = lax.broadcasted_iota(jnp.int32, (t, GW), 1) >> 6
        mask = jnp.concatenate([mc] * HG_G, axis=0) > 0.5
        for p in range(HEADS // HG_G):
            sl = slice(GW * p, GW * (p + 1))
            vp = hi[:, sl].astype(BF16)
            qs = _stack_heads(qd[:, sl], head).astype(BF16)
            a = jnp.where(mask, _dotg(qs, ki[:, sl].astype(BF16), NT), 0.0)
            o_intra = _unstack_heads(_dot(a.astype(BF16), vp), head, t)
            qb = qd[:, sl].astype(BF16)
            kb = ke[:, sl].astype(BF16)
            st = st_ref[p]
            for c in range(HG_NC):
                rows = slice(HG_BLOCK * c, HG_BLOCK * (c + 1))
                sp_ref[c, :, sl] = _compact_state(st)
                o_ref[rows, sl] = o_intra[rows] + _dotg(qb[rows], st.astype(BF16), NT)
                u = _dotg(vp[rows], kb[rows], TN) * bd_ref[...]
                st = st * el[HG_BLOCK * c:HG_BLOCK * c + 1, sl] + u
            st_ref[p] = st

    row = lambda j: pl.BlockSpec((t, HG_WIDTH), lambda i: (i, j))
    full = lambda a: pl.BlockSpec(a.shape, lambda i: (0, 0))
    return pl.pallas_call(
        body,
        grid=(s // t,),
        in_specs=[row(6), row(7), row(8), full(lbl), full(mcum), full(msum), full(bd)],
        out_specs=[row(0), pl.BlockSpec((HG_NC, 64, HG_WIDTH), lambda i: (i, 0, 0))],
        out_shape=[jax.ShapeDtypeStruct((s, HG_WIDTH), F32),
                   jax.ShapeDtypeStruct((s // HG_BLOCK, 64, HG_WIDTH), F32)],
        scratch_shapes=[pltpu.VMEM((HEADS // HG_G, GW, GW), F32)],
        compiler_params=_params(("arbitrary",)),
        name="hgrn_fwd",
    )(proj, proj, proj, lbl, mcum, msum, bd)


def _hgrn_bwd(proj, lbl, do, sprev, dproj):
    s = proj.shape[0]
    t = HG_T
    nt = s // t
    mcum, mrev, msum, bd = _hg_consts()

    def body(hq_ref, hf_ref, hi_ref, lbl_ref, do_ref, sp_ref, mcum_ref, mrev_ref, msum_ref, bd_ref,
             dproj_in, dh_ref, dlbl_ref, g_ref):
        del dproj_in

        @pl.when(pl.program_id(0) == 0)
        def _():
            g_ref[...] = jnp.zeros_like(g_ref)
            dlbl_ref[...] = jnp.zeros_like(dlbl_ref)

        mc = mcum_ref[...]
        lb, sig, f, b, big_l, qd, ki, ke = _hg_pre(hq_ref[...], hf_ref[...], lbl_ref[...], mc, msum_ref[...])
        el = jnp.exp(big_l)
        hi = hi_ref[...]
        dov = do_ref[...]
        head = lax.broadcasted_iota(jnp.int32, (t, GW), 1) >> 6
        head64 = lax.broadcasted_iota(jnp.int32, (64, GW), 1) >> 6
        mask = jnp.concatenate([mc] * HG_G, axis=0) > 0.5
        dqd_parts, dke_parts, dv_parts, del_parts, dki_parts = [], [], [], [], []
        for p in range(HEADS // HG_G):
            sl = slice(GW * p, GW * (p + 1))
            vp = hi[:, sl].astype(BF16)
            qs = _stack_heads(qd[:, sl], head).astype(BF16)
            kip = ki[:, sl].astype(BF16)
            dos = _stack_heads(dov[:, sl], head).astype(BF16)
            a = jnp.where(mask, _dotg(qs, kip, NT), 0.0).astype(BF16)
            da = jnp.where(mask, _dotg(dos, vp, NT), 0.0).astype(BF16)
            r = _dot(da, kip)
            dki_parts.append(_dotg(da, qs, TN))
            qb = qd[:, sl].astype(BF16)
            kb = ke[:, sl].astype(BF16)
            dob = dov[:, sl].astype(BF16)
            g = g_ref[p]
            dqd_c, dv_c, dke_c, del_c = [], [], [], []
            for c in range(HG_NC - 1, -1, -1):
                rows = slice(HG_BLOCK * c, HG_BLOCK * (c + 1))
                gb = g.astype(BF16)
                st = _expand_state(sp_ref[c, :, sl], head64)
                dqd_c.append(_dot(dob[rows], st.astype(BF16)))
                dv_c.append(_dotg(kb[rows], gb, NT))
                dke_c.append(_dot(vp[rows], gb))
                del_c.append(jnp.broadcast_to(jnp.sum(g * st, axis=0, keepdims=True), (HG_BLOCK, GW)))
                g = g * el[HG_BLOCK * c:HG_BLOCK * c + 1, sl] + _dotg(dob[rows], qb[rows], TN) * bd_ref[...]
            g_ref[p] = g
            up = lambda parts: jnp.concatenate(parts[::-1], axis=0)
            dqd_parts.append(_unstack_heads(r, head, t) + up(dqd_c))
            dv_parts.append(_dotg(a, dos, TN) + up(dv_c))
            dke_parts.append(up(dke_c))
            del_parts.append(up(del_c))
        wide = lambda parts: jnp.concatenate(parts, axis=1)
        dqd, dke, dki, dvv, del_rows = wide(dqd_parts), wide(dke_parts), wide(dki_parts), wide(dv_parts), wide(del_parts)
        dh_ref[:, :HG_WIDTH] = (dqd * jnp.exp(b)).astype(BF16)
        dh_ref[:, 2 * HG_WIDTH:] = dvv.astype(BF16)
        dke_ke = dke * ke
        db = dqd * qd - dki * ki - dke_ke
        dl_rows = _sel_left(msum_ref[...], dke_ke) + del_rows * el
        is_last = (lax.broadcasted_iota(jnp.int32, (t, HG_WIDTH), 0) & (HG_BLOCK - 1)) == HG_BLOCK - 1
        db = db + jnp.where(is_last, dl_rows, 0.0)
        dlf = _sel_left(mrev_ref[...], db)
        dk = dki * jnp.exp(-b) + dke * jnp.exp(big_l - b)
        df = dlf / f - dk
        dh_ref[:, HG_WIDTH:2 * HG_WIDTH] = (df * (1.0 - lb) * sig * (1.0 - sig)).astype(BF16)
        dlb = jnp.sum(df * (1.0 - sig), axis=0, keepdims=True) * lb * (1.0 - lb)
        dlbl_ref[0:1, :] += dlb
        dlbl_ref[1:2, :] -= dlb

    rrow = lambda j: pl.BlockSpec((t, HG_WIDTH), lambda i: (nt - 1 - i, j))
    full = lambda a: pl.BlockSpec(a.shape, lambda i: (0, 0))
    return pl.pallas_call(
        body,
        grid=(nt,),
        in_specs=[rrow(6), rrow(7), rrow(8), full(lbl), rrow(0),
                  pl.BlockSpec((HG_NC, 64, HG_WIDTH), lambda i: (nt - 1 - i, 0, 0)),
                  full(mcum), full(mrev), full(msum), full(bd), pl.BlockSpec(memory_space=pl.ANY)],
        out_specs=[pl.BlockSpec((t, 3 * HG_WIDTH), lambda i: (nt - 1 - i, 2)),
                   pl.BlockSpec((2, HG_WIDTH), lambda i: (0, 0))],
        out_shape=[jax.ShapeDtypeStruct(dproj.shape, BF16), jax.ShapeDtypeStruct((2, HG_WIDTH), F32)],
        input_output_aliases={10: 0},
        scratch_shapes=[pltpu.VMEM((HEADS // HG_G, GW, GW), F32)],
        compiler_params=_params(("arbitrary",)),
        name="hgrn_bwd",
    )(proj, proj, proj, lbl, do, sprev, mcum, mrev, msum, bd, dproj)


def _tail(x, tgt, proj, attn, o, w_a, w_b, w_out, w_at, w_bt, w_outt, b_gate, g_post, gh):
    s = x.shape[0]
    tm = 256
    ones64 = (jnp.arange(HG_WIDTH)[:, None] // 64 == jnp.arange(HG_WIDTH)[None, :] // 64).astype(BF16)
    weights = (w_a, w_b, w_out, w_at, w_bt, w_outt)

    def body(x_ref, t_ref, ml_ref, ga_ref, gb_ref, at_ref, o_ref, *rest):
        w_hbm, (bg_ref, gp_ref, gh_ref, ones_ref) = rest[:6], rest[6:10]
        (dout_ref, dpj_ref, dop_ref, do_ref, mt_ref, dy_ref, yat_ref, dya_ref, ybt_ref, dyb_ref,
         loss_ref, dgp_ref, dbg_ref, dgh_ref) = rest[10:24]
        (wa_ref, wb_ref, wo_ref, wat_ref, wbt_ref, wot_ref), w_sem = rest[24:30], rest[30]

        @pl.when(pl.program_id(0) == 0)
        def _():
            loads = [pltpu.make_async_copy(src, dst, w_sem.at[k])
                     for k, (src, dst) in enumerate(zip(w_hbm, rest[24:30]))]
            _start_all(loads, [])
            loss_ref[...] = jnp.zeros_like(loss_ref)
            dgp_ref[...] = jnp.zeros_like(dgp_ref)
            dbg_ref[...] = jnp.zeros_like(dbg_ref)
            dgh_ref[...] = jnp.zeros_like(dgh_ref)
            _wait_all(loads, [])

        ones = ones_ref[...]
        gate_a = ga_ref[...]
        sa = _sigmoid(gate_a)
        silu_a = gate_a * sa
        attn_v = at_ref[...]
        ya_in = attn_v * silu_a
        ov = o_ref[...]
        ro = lax.rsqrt(_sel_right(ov * ov, ones) * (1.0 / 64.0) + EPS)
        ohat = ov * ro
        ghv = gh_ref[...]
        on = ohat * ghv
        gate_b = gb_ref[...]
        sb = _sigmoid(gate_b)
        silu_b = gate_b * sb
        yb_in = on * silu_b
        ya_bf = ya_in.astype(BF16)
        yb_bf = yb_in.astype(BF16)
        yat_ref[...] = ya_bf.T
        ybt_ref[...] = yb_bf.T
        y_a = _dot(ya_bf, wa_ref[...])
        y_b = _dot(yb_bf, wb_ref[...])
        gts = _sigmoid(ml_ref[...] + bg_ref[...])
        g_a = gts[:, :D_MODEL]
        g_b = gts[:, D_MODEL:]
        m_bf = (g_a * y_a + g_b * y_b).astype(BF16)
        mt_ref[...] = m_bf.T
        y = _dot(m_bf, wo_ref[...])
        r1 = lax.rsqrt(jnp.mean(y * y, axis=-1, keepdims=True) + EPS)
        yn = y * r1
        gp = gp_ref[...]
        e = x_ref[...] + yn * gp - t_ref[...]
        loss_ref[...] += jnp.sum(e * e, axis=0, keepdims=True)
        dout = e * (1.0 / D_MODEL)
        dout_ref[...] = dout
        dgp_ref[...] += jnp.sum(dout * yn, axis=0, keepdims=True)
        dyn = dout * gp
        dy = r1 * (dyn - yn * jnp.mean(dyn * yn, axis=-1, keepdims=True))
        dy_bf = dy.astype(BF16)
        dy_ref[...] = dy_bf
        dm = _dot(dy_bf, wot_ref[...])
        dml_a = dm * y_a * g_a * (1.0 - g_a)
        dml_b = dm * y_b * g_b * (1.0 - g_b)
        dpj_ref[:, :D_MODEL] = dml_a.astype(BF16)
        dpj_ref[:, D_MODEL:2 * D_MODEL] = dml_b.astype(BF16)
        dbg_ref[:, :D_MODEL] += jnp.sum(dml_a, axis=0, keepdims=True)
        dbg_ref[:, D_MODEL:] += jnp.sum(dml_b, axis=0, keepdims=True)
        dya_bf = (dm * g_a).astype(BF16)
        dyb_bf = (dm * g_b).astype(BF16)
        dya_ref[...] = dya_bf
        dyb_ref[...] = dyb_bf
        dya_in = _dot(dya_bf, wat_ref[...])
        dyb_in = _dot(dyb_bf, wbt_ref[...])
        dattn = dya_in * silu_a
        delta = _sel_right(dattn * attn_v, ones)
        lane = lax.broadcasted_iota(jnp.int32, (tm, LANE), 1)
        for p in range(HEADS // 2):
            sl = slice(LANE * p, LANE * (p + 1))
            xs = (dattn[:, sl], pltpu.roll(dattn[:, sl], VDIM, 1))
            nds = (-pltpu.roll(delta[:, sl], VDIM, 1), -delta[:, sl])
            for a in range(2):
                hi, lo_part = _hi_lo(nds[a])
                blk = jnp.where(lane < VDIM, xs[a], jnp.where(lane == VDIM, hi, jnp.where(lane == VDIM + 1, lo_part, 0.0)))
                dop_ref[:, LANE * (2 * p + a):LANE * (2 * p + a + 1)] = blk.astype(BF16)
        dpj_ref[:, 2 * D_MODEL:2 * D_MODEL + HG_WIDTH] = (
            dya_in * attn_v * (sa * (1.0 + gate_a * (1.0 - sa)))).astype(BF16)
        don = dyb_in * silu_b
        dpj_ref[:, 2 * D_MODEL + HG_WIDTH:] = (dyb_in * on * (sb * (1.0 + gate_b * (1.0 - sb)))).astype(BF16)
        dgh_ref[...] += jnp.sum(don * ohat, axis=0, keepdims=True)
        dohat = don * ghv
        do_ref[...] = ro * (dohat - ohat * (_sel_right(dohat * ohat, ones) * (1.0 / 64.0)))

    row = lambda w, j: pl.BlockSpec((tm, w), lambda i: (i, j))
    col = lambda w: pl.BlockSpec((w, tm), lambda i: (0, i))
    full = lambda a: pl.BlockSpec(a.shape, lambda i: (0, 0))
    acc = lambda w: pl.BlockSpec((1, w), lambda i: (0, 0))
    sds = lambda w, dt: jax.ShapeDtypeStruct((s, w), dt)
    sdt = lambda w: jax.ShapeDtypeStruct((w, s), BF16)
    return pl.pallas_call(
        body,
        grid=(s // tm,),
        in_specs=[row(1024, 0), row(1024, 0), row(2048, 0), row(512, 4), row(512, 5), row(512, 0), row(512, 0)]
        + [ANY] * 6 + [full(b_gate), full(g_post), full(gh), full(ones64)],
        out_specs=[row(1024, 0), row(3072, 0), row(1024, 0), row(512, 0),
                   col(1024), row(1024, 0), col(512), row(1024, 0), col(512), row(1024, 0),
                   acc(1024), acc(1024), acc(2048), acc(512)],
        out_shape=[sds(1024, F32), sds(D_IN_PAD, BF16), sds(1024, BF16), sds(512, F32),
                   sdt(1024), sds(1024, BF16), sdt(512), sds(1024, BF16), sdt(512), sds(1024, BF16),
                   jax.ShapeDtypeStruct((1, 1024), F32), jax.ShapeDtypeStruct((1, 1024), F32),
                   jax.ShapeDtypeStruct((1, 2048), F32), jax.ShapeDtypeStruct((1, 512), F32)],
        scratch_shapes=[pltpu.VMEM(a.shape, BF16) for a in weights] + [pltpu.SemaphoreType.DMA((6,))],
        compiler_params=_params(("arbitrary",), 56),
        name="tail",
    )(x, tgt, proj, proj, proj, attn, o, *weights, b_gate, g_post, gh, ones64)


def _mla_bwd(proj, dqr, dkr, dv, g_q, g_kv, w_uq_pt, w_kv_pt, rc, rs1, rs2, dproj):
    s = proj.shape[0]
    tm = 256
    scale = 1.0 / math.sqrt(QK)

    def body(cq_ref, ckv_ref, dqr_ref, dkr_ref, dv_ref, gq_ref, gkv_ref, wuqt_ref, wkvt_ref, c_ref, s1_ref, s2_ref,
             dproj_in, dqf_ref, dkvf_ref, dc_ref, dgq_ref, dgkv_ref):
        del dproj_in

        @pl.when(pl.program_id(0) == 0)
        def _():
            dgq_ref[...] = jnp.zeros_like(dgq_ref)
            dgkv_ref[...] = jnp.zeros_like(dgkv_ref)

        c, s1, s2 = c_ref[...], s1_ref[...], s2_ref[...]
        lane = lax.broadcasted_iota(jnp.int32, (tm, LANE), 1)
        ksum = jnp.zeros((tm, LANE), F32)
        for h in range(HEADS):
            sl = slice(LANE * h, LANE * (h + 1))
            dqf_ref[:, sl] = (_unrope(dqr_ref[:, sl], c, s1, s2) * scale).astype(BF16)
            dkh = dkr_ref[:, sl]
            ksum = ksum + dkh
            dkvf_ref[:, sl] = jnp.where(lane < NOPE, dkh, 0.0).astype(BF16)
            dkvf_ref[:, HEADS * LANE + LANE * h:HEADS * LANE + LANE * (h + 1)] = jnp.where(
                lane < VDIM, dv_ref[:, sl], 0.0).astype(BF16)
        dkpe = _unrope(ksum, c, s1, s2)
        dc_ref[:, Q_LORA + KV_LORA:] = jnp.where((lane >= NOPE) & (lane < QK), dkpe, 0.0).astype(BF16)
        dcqn = _dot(dqf_ref[...], wuqt_ref[...])
        dckvn = _dot(dkvf_ref[...], wkvt_ref[...])
        for x_ref, g_ref, dn, cols, dg_ref in ((cq_ref, gq_ref, dcqn, slice(0, Q_LORA), dgq_ref),
                                               (ckv_ref, gkv_ref, dckvn, slice(Q_LORA, Q_LORA + KV_LORA), dgkv_ref)):
            xv = x_ref[...]
            r = lax.rsqrt(jnp.mean(xv * xv, axis=-1, keepdims=True) + EPS)
            xh = xv * r
            dg_ref[...] += jnp.sum(dn * xh, axis=0, keepdims=True)
            dh = dn * g_ref[...]
            dc_ref[:, cols] = (r * (dh - xh * jnp.mean(dh * xh, axis=-1, keepdims=True))).astype(BF16)

    row = lambda w, j: pl.BlockSpec((tm, w), lambda i: (i, j))
    full = lambda a: pl.BlockSpec(a.shape, lambda i: (0, 0))
    acc = lambda w: pl.BlockSpec((1, w), lambda i: (0, 0))
    sds = lambda w, dt: jax.ShapeDtypeStruct((s, w), dt)
    return pl.pallas_call(
        body,
        grid=(s // tm,),
        in_specs=[row(768, 6), row(256, 21), row(1024, 0), row(1024, 0), row(1024, 0), full(g_q), full(g_kv),
                  full(w_uq_pt), full(w_kv_pt), row(128, 0), row(128, 0), row(128, 0),
                  pl.BlockSpec(memory_space=pl.ANY)],
        out_specs=[row(1024, 0), row(2048, 0), row(1152, 4), acc(768), acc(256)],
        out_shape=[sds(1024, BF16), sds(2048, BF16), jax.ShapeDtypeStruct(dproj.shape, BF16),
                   jax.ShapeDtypeStruct((1, 768), F32), jax.ShapeDtypeStruct((1, 256), F32)],
        input_output_aliases={12: 2},
        compiler_params=_params(("arbitrary",)),
        name="mla_bwd",
    )(proj, proj, dqr, dkr, dv, g_q, g_kv, w_uq_pt, w_kv_pt, rc, rs1, rs2, dproj)


def _pick(n, options):
    for o in options:
        if n % o == 0:
            return o
    raise ValueError(n)


def _matmul(a, b, name):
    m, k = a.shape
    n = b.shape[1]
    tm = _pick(m, (1024, 768, 512, 256))
    tn = _pick(n, (1152, 1024, 768, 512))
    tk = _pick(k, (1024, 512))
    nk = k // tk

    def body(a_ref, b_ref, o_ref):
        @pl.when(pl.program_id(2) == 0)
        def _():
            o_ref[...] = jnp.zeros_like(o_ref)

        o_ref[...] += _dot(a_ref[...], b_ref[...])

    return pl.pallas_call(
        body,
        grid=(m // tm, n // tn, nk),
        in_specs=[pl.BlockSpec((tm, tk), lambda i, j, l: (i, l)), pl.BlockSpec((tk, tn), lambda i, j, l: (l, j))],
        out_specs=pl.BlockSpec((tm, tn), lambda i, j, l: (i, j)),
        out_shape=jax.ShapeDtypeStruct((m, n), F32),
        compiler_params=_params(("arbitrary", "arbitrary", "arbitrary")),
        name=name,
    )(a, b)


def _dh_dx(dproj, w_in_pt, x, dout, g_pre, sends):
    s, k = dproj.shape
    tm = 256
    ns, ni = len(sends), s // tm

    def body(dp_ref, w_ref, x_ref, dout_ref, g_ref, *rest):
        send_refs, (dx_ref, dg_ref) = rest[:ns], rest[ns:ns + 2]
        recv_refs, sems = rest[ns + 2:2 * ns + 2], rest[2 * ns + 2:]

        @pl.when(pl.program_id(0) == 0)
        def _():
            _start_all(*_to_chips_copies(send_refs, recv_refs, sems))
            dg_ref[...] = jnp.zeros_like(dg_ref)

        dh = _dot(dp_ref[...], w_ref[...])
        xv = x_ref[...]
        r = lax.rsqrt(jnp.mean(xv * xv, axis=-1, keepdims=True) + EPS)
        xh = xv * r
        dg_ref[...] += jnp.sum(dh * xh, axis=0, keepdims=True)
        dxh = dh * g_ref[...]
        dx_ref[...] = dout_ref[...] + r * (dxh - xh * jnp.mean(dxh * xh, axis=-1, keepdims=True))

        @pl.when(pl.program_id(0) == ni - 1)
        def _():
            _wait_all(*_to_chips_copies(send_refs, recv_refs, sems))

    row = lambda w: pl.BlockSpec((tm, w), lambda i: (i, 0))
    return pl.pallas_call(
        body,
        grid=(ni,),
        in_specs=[row(k), pl.BlockSpec((k, D_MODEL), lambda i: (0, 0)), row(D_MODEL), row(D_MODEL),
                  pl.BlockSpec((1, D_MODEL), lambda i: (0, 0))] + [ANY] * ns,
        out_specs=[row(D_MODEL), pl.BlockSpec((1, D_MODEL), lambda i: (0, 0))] + [ANY] * ns,
        out_shape=[jax.ShapeDtypeStruct((s, D_MODEL), F32), jax.ShapeDtypeStruct((1, D_MODEL), F32)]
        + [jax.ShapeDtypeStruct(a.shape, a.dtype) for a in sends],
        scratch_shapes=_copy_sems(ns, 3),
        compiler_params=_params(("arbitrary",)),
        name="dh_dx",
    )(dproj, w_in_pt, x, dout, g_pre, *sends)


def _pair_reduce(slots):
    n = len(slots)
    half = [(N_DEV // 2,) + a.shape[1:] for a in slots]

    def body(*refs):
        s_refs, o_refs = refs[:n], refs[n:2 * n]
        mine, got = refs[2 * n:3 * n], refs[3 * n:4 * n]
        send_sems, recv_sems, local_sems = refs[4 * n:]
        x, y, c = _my_place()
        copies, loads = [], []
        for a in range(n):
            for q in range(N_DEV // 2):
                copies.append(pltpu.make_async_remote_copy(
                    src_ref=s_refs[a].at[2 * q + 1 - c], dst_ref=got[a].at[q],
                    send_sem=send_sems.at[4 * a + q], recv_sem=recv_sems.at[4 * a + q],
                    device_id=(x, y, 1 - c), device_id_type=MESH_ID))
                loads.append(pltpu.make_async_copy(s_refs[a].at[2 * q + c], mine[a].at[q], local_sems.at[4 * a + q]))
        _start_all(loads, copies)
        _wait_all(loads, copies)
        for a in range(n):
            o_refs[a][...] = (mine[a][...].astype(F32) + got[a][...].astype(F32)).astype(o_refs[a].dtype)

    vm = lambda: [pltpu.VMEM(h, a.dtype) for h, a in zip(half, slots)]
    return pl.pallas_call(
        body,
        in_specs=[ANY] * n,
        out_shape=[jax.ShapeDtypeStruct(h, a.dtype) for h, a in zip(half, slots)],
        scratch_shapes=vm() + vm() + [pltpu.SemaphoreType.DMA((4 * n,)), pltpu.SemaphoreType.DMA((4 * n,)),
                                      pltpu.SemaphoreType.DMA((4 * n,))],
        compiler_params=pltpu.CompilerParams(vmem_limit_bytes=48 * 2**20),
        name="pair_reduce",
    )(*slots)


def _rope_tables(s):
    inv = (np.float32(ROPE_THETA) ** (-np.arange(0, ROPE, 2, dtype=np.float32) / np.float32(ROPE))).astype(np.float32)
    ang = (np.arange(s, dtype=np.float32)[:, None] * inv[None, :]).astype(np.float32)
    cos, sin = jnp.asarray(np.cos(ang.astype(np.float64)), F32), jnp.asarray(np.sin(ang.astype(np.float64)), F32)
    z = lambda w: jnp.zeros((s, w), F32)
    rc = jnp.concatenate([jnp.ones((s, NOPE), F32), cos, cos, z(32)], axis=1)
    rs1 = jnp.concatenate([z(NOPE), -sin, z(16), z(32)], axis=1)
    rs2 = jnp.concatenate([z(NOPE), z(16), sin, z(32)], axis=1)
    return rc, rs1, rs2


def _step(x, tgt, w_in_slots, shards, g_pre, b_gate, g_q, g_kv, lbl, g_hgrn, g_post):
    s = x.shape[0]
    w_in_p, w_in_pt = _assemble_w_in(w_in_slots)
    rc, rs1, rs2 = _rope_tables(s)
    gh = jnp.tile(g_hgrn, (1, HEADS))

    proj, ht, *got = _norm_proj(x, g_pre, w_in_p, shards)
    w_uq, w_ukv, w_a, w_b, w_out = (_from_slots(n, g) for n, g in zip(MATS, got))
    w_uq_p = jnp.pad(w_uq.reshape(Q_LORA, HEADS, QK), ((0, 0), (0, 0), (0, LANE - QK))).reshape(Q_LORA, HEADS * LANE)
    kv3 = w_ukv.reshape(KV_LORA, HEADS, NOPE + VDIM)
    pad64 = lambda t: jnp.pad(t, ((0, 0), (0, 0), (0, LANE - 64))).reshape(KV_LORA, HEADS * LANE)
    w_kv_p = jnp.concatenate([pad64(kv3[:, :, :NOPE]), pad64(kv3[:, :, NOPE:])], axis=1)

    qr, kr, v, cqt, ckvt = _mla_prep(proj, g_q, g_kv, w_uq_p, w_kv_p, rc, rs1, rs2)
    attn, qa = _attn_fwd(qr, kr, v)
    o, sprev = _hgrn_fwd(proj, lbl)
    (dout, dproj, dop, do, mt, dy_bf, yat, dya_bf, ybt, dyb_bf,
     loss_vec, dg_post, db_gate, dgh) = _tail(x, tgt, proj, attn, o, w_a, w_b, w_out, w_a.T, w_b.T, w_out.T,
                                               b_gate, g_post, gh)
    early = [_to_slots(n, _matmul(a, b, "d" + n)).astype(BF16)
             for n, a, b in (("w_branch_a", yat, dya_bf), ("w_branch_b", ybt, dyb_bf), ("w_out", mt, dy_bf))]
    dqr, dkr, dv, *early_recv = _attn_bwd(qa, kr, v, dop, early)
    dproj, dlbl = _hgrn_bwd(proj, lbl, do, sprev, dproj)
    dqf, dkvf, dproj, dg_q, dg_kv = _mla_bwd(proj, dqr, dkr, dv, g_q, g_kv, w_uq_p.T, w_kv_p.T, rc, rs1, rs2, dproj)

    dw_in_slots = _scatter_w_in(_matmul(ht, dproj, "dw_in"))
    dw_uq_p = _matmul(cqt, dqf, "dw_uq")
    dw_kv_p = _matmul(ckvt, dkvf, "dw_kv")
    dw_uq = dw_uq_p.reshape(Q_LORA, HEADS, LANE)[:, :, :QK].reshape(Q_LORA, HEADS * QK)
    dw_ukv = jnp.concatenate([dw_kv_p[:, :HEADS * LANE].reshape(KV_LORA, HEADS, LANE)[:, :, :NOPE],
                              dw_kv_p[:, HEADS * LANE:].reshape(KV_LORA, HEADS, LANE)[:, :, :VDIM]],
                             axis=2).reshape(KV_LORA, 1024)
    late = _pair_reduce([dw_in_slots, _to_slots("w_uq", dw_uq).astype(BF16), _to_slots("w_ukv", dw_ukv).astype(BF16)])
    dx, dg_pre, *late_recv = _dh_dx(dproj, w_in_pt, x, dout, g_pre, late)

    g_sum = _vectors_sum(dg_pre, db_gate, dg_q, dg_kv, dlbl, dgh, dg_post, loss_vec)
    return dx, late_recv[0], dict(zip(MATS, late_recv[1:] + early_recv)), g_sum


def _all_gather(blocks):
    n = len(blocks)

    def body(*refs):
        x_refs, out_refs = refs[:n], refs[n:2 * n]
        send_sems, recv_sems, local_sems = refs[2 * n:]
        x, y, c = _my_place()
        me, sibling = (x, y, c), (x, y, 1 - c)
        chips = [(1 - x, y), (x, 1 - y), (1 - x, 1 - y)]

        def slot(a, px, py, pc):
            return out_refs[a].at[4 * px + 2 * py + pc]

        def copy(a, k, blk, to, src=None):
            return pltpu.make_async_remote_copy(
                src_ref=slot(a, *blk) if src is None else src, dst_ref=slot(a, *blk),
                send_sem=send_sems.at[7 * a + k], recv_sem=recv_sems.at[7 * a + k],
                device_id=to, device_id_type=MESH_ID)

        mine = [pltpu.make_async_copy(x_refs[a], slot(a, *me), local_sems.at[a]) for a in range(n)]
        for cp in mine:
            cp.start()
        first = [copy(a, 0, me, sibling, src=x_refs[a]) for a in range(n)]
        first += [copy(a, 1 + j, me, (*chip, c), src=x_refs[a]) for a in range(n) for j, chip in enumerate(chips)]
        for cp in first:
            cp.start()
        passed = []
        for j, chip in enumerate(chips):
            for a in range(n):
                copy(a, 1 + j, (*chip, c), me).wait_recv()
                passed.append(copy(a, 4 + j, (*chip, c), sibling))
                passed[-1].start()
        for a in range(n):
            copy(a, 0, sibling, me).wait_recv()
        for j, chip in enumerate(chips):
            for a in range(n):
                copy(a, 4 + j, (*chip, 1 - c), me).wait_recv()
        for cp in first + passed:
            cp.wait_send()
        for cp in mine:
            cp.wait()

    return pl.pallas_call(
        body,
        out_shape=[jax.ShapeDtypeStruct((N_DEV,) + b.shape, b.dtype) for b in blocks],
        in_specs=[pl.BlockSpec(memory_space=pl.ANY)] * n,
        out_specs=[pl.BlockSpec(memory_space=pl.ANY)] * n,
        scratch_shapes=[pltpu.SemaphoreType.DMA((7 * n,)), pltpu.SemaphoreType.DMA((7 * n,)),
                        pltpu.SemaphoreType.DMA((n,))],
        name="gather_weights",
    )(*blocks)


def _adamw(g, w, m, v):
    c1 = 1.0 / (1.0 - ADAM_B1 ** ADAM_STEP)
    c2 = 1.0 / (1.0 - ADAM_B2 ** ADAM_STEP)
    nm = ADAM_B1 * m + (1.0 - ADAM_B1) * g
    nv = ADAM_B2 * v + (1.0 - ADAM_B2) * (g * g)
    d = -ADAM_LR * ((nm * c1) / (jnp.sqrt(nv * c2) + ADAM_EPS) + ADAM_WD * w)
    return d, nm, nv


def _sum8(r_ref):
    g = r_ref[0].astype(F32)
    for k in range(1, r_ref.shape[0]):
        g = g + r_ref[k].astype(F32)
    return g


def _sum_adamw_w_in(recv, w, m, v):
    rows, _, cols = w.shape
    tc = 256

    def body(r_ref, w_ref, m_ref, v_ref, g_ref, d_ref, nm_ref, nv_ref):
        g = _sum8(r_ref)
        dense = lambda ref: ref[...].reshape(rows, tc)
        d, nm, nv = _adamw(g, dense(w_ref), dense(m_ref), dense(v_ref))
        for ref, val in ((g_ref, g), (d_ref, d), (nm_ref, nm), (nv_ref, nv)):
            ref[...] = val.reshape(rows, 1, tc)

    blk = pl.BlockSpec((rows, 1, tc), lambda i: (0, 0, i))
    out = jax.ShapeDtypeStruct((rows, 1, cols), F32)
    return pl.pallas_call(
        body,
        grid=(cols // tc,),
        in_specs=[pl.BlockSpec((recv.shape[0], rows, tc), lambda i: (0, 0, i)), blk, blk, blk],
        out_specs=[blk, blk, blk, blk],
        out_shape=[out, out, out, out],
        compiler_params=_params(("arbitrary",)),
        name="sum_adamw_w_in",
    )(recv, w, m, v)


def _sum_adamw_whole(recvs, ws, ms, vs):
    n = len(ws)

    def body(*refs):
        r_refs, w_refs, m_refs, v_refs = refs[:n], refs[n:2 * n], refs[2 * n:3 * n], refs[3 * n:4 * n]
        outs = refs[4 * n:]
        for a in range(n):
            g = _sum8(r_refs[a])
            d, nm, nv = _adamw(g, w_refs[a][...], m_refs[a][...], v_refs[a][...])
            outs[a][...] = g
            outs[n + a][...] = d
            outs[2 * n + a][...] = nm
            outs[3 * n + a][...] = nv

    shapes = [jax.ShapeDtypeStruct(w.shape, F32) for w in ws]
    res = pl.pallas_call(
        body,
        out_shape=shapes * 4,
        compiler_params=pltpu.CompilerParams(vmem_limit_bytes=48 * 2**20),
        name="sum_adamw_mats",
    )(*recvs, *ws, *ms, *vs)
    return res[:n], res[n:2 * n], res[2 * n:3 * n], res[3 * n:]


SMALL = ("g_pre", "b_gate", "g_q", "g_kv", "lb_logits", "g_hgrn", "g_post")
SMALL_SHAPE = dict(g_pre=(1, 1024), b_gate=(1, 2048), g_q=(1, 768), g_kv=(1, 256), lb_logits=(2, 512),
                   g_hgrn=(1, 64), g_post=(1, 1024))


def _vectors_sum(dg_pre, db_gate, dg_q, dg_kv, dlbl, dgh, dg_post, loss_vec):
    def body(gpre_ref, bg_ref, gq_ref, gkv_ref, lbl_ref, gh_ref, gpost_ref, loss_ref, out_ref, mine, got,
             send_sems, recv_sems):
        mine[...] = jnp.zeros_like(mine)
        mine[0:1, :] = gpre_ref[...]
        mine[1:2, :] = bg_ref[:, :1024]
        mine[2:3, :] = bg_ref[:, 1024:]
        mine[3:4, :Q_LORA] = gq_ref[...]
        mine[4:5, :KV_LORA] = gkv_ref[...]
        loss = (0.5 / D_MODEL) * jnp.sum(loss_ref[...], axis=-1, keepdims=True)
        mine[4:5, KV_LORA:] = jnp.broadcast_to(loss, (1, 1024 - KV_LORA))
        mine[5:6, :HG_WIDTH] = lbl_ref[0:1, :]
        mine[5:6, HG_WIDTH:] = lbl_ref[1:2, :]
        gh = gh_ref[...]
        fold = gh[:, :VDIM]
        for h in range(1, HEADS):
            fold = fold + gh[:, VDIM * h:VDIM * (h + 1)]
        mine[6:7, :VDIM] = fold
        mine[7:8, :] = gpost_ref[...]
        x, y, c = _my_place()
        me = 4 * x + 2 * y + c
        got[me] = mine[...]
        copies = [pltpu.make_async_remote_copy(
            src_ref=mine, dst_ref=got.at[me], send_sem=send_sems.at[k], recv_sem=recv_sems.at[k],
            device_id=_flip(k, x, y, c), device_id_type=MESH_ID) for k in range(N_DEV - 1)]
        _start_all([], copies)
        _wait_all([], copies)
        out_ref[...] = _sum8(got)

    return pl.pallas_call(
        body,
        out_shape=jax.ShapeDtypeStruct((8, 1024), F32),
        scratch_shapes=[pltpu.VMEM((8, 1024), F32), pltpu.VMEM((N_DEV, 8, 1024), F32),
                        pltpu.SemaphoreType.DMA((7,)), pltpu.SemaphoreType.DMA((7,))],
        name="vectors_sum",
    )(dg_pre, db_gate, dg_q, dg_kv, dlbl, dgh, dg_post, loss_vec)


def _vectors_adamw(g_sum, ws, ms, vs):
    n = len(SMALL)

    def body(g_ref, *refs):
        w_refs, m_refs, v_refs = refs[:n], refs[n:2 * n], refs[2 * n:3 * n]
        loss_ref, outs = refs[3 * n], refs[3 * n + 1:]
        g = g_ref[...]
        loss_ref[...] = g[4:5, KV_LORA:KV_LORA + 1]
        grads = (g[0:1, :], jnp.concatenate([g[1:2, :], g[2:3, :]], axis=1), g[3:4, :Q_LORA], g[4:5, :KV_LORA],
                 jnp.concatenate([g[5:6, :HG_WIDTH], g[5:6, HG_WIDTH:]], axis=0), g[6:7, :VDIM], g[7:8, :])
        for a in range(n):
            d, nm, nv = _adamw(grads[a], w_refs[a][...], m_refs[a][...], v_refs[a][...])
            outs[a][...] = grads[a]
            outs[n + a][...] = d
            outs[2 * n + a][...] = nm
            outs[3 * n + a][...] = nv

    shapes = [jax.ShapeDtypeStruct(SMALL_SHAPE[k], F32) for k in SMALL]
    res = pl.pallas_call(
        body,
        out_shape=[jax.ShapeDtypeStruct((1, 1), F32)] + shapes * 4,
        name="vectors_adamw",
    )(g_sum, *ws, *ms, *vs)
    return res[0], res[1:n + 1], res[n + 1:2 * n + 1], res[2 * n + 1:3 * n + 1], res[3 * n + 1:]


MATS = ("w_uq", "w_ukv", "w_branch_a", "w_branch_b", "w_out")
COL_SHARDED = dict(w_uq=False, w_ukv=True, w_branch_a=True, w_branch_b=True, w_out=False)
ORDER = ("g_pre", "w_in", "b_gate", "g_q", "w_uq", "g_kv", "w_ukv", "lb_logits", "g_hgrn",
         "w_branch_a", "w_branch_b", "w_out", "g_post")


def _to_slots(name, full):
    r, c = full.shape
    if COL_SHARDED[name]:
        return full.reshape(r, N_DEV, c // N_DEV).transpose(1, 0, 2)
    return full.reshape(N_DEV, r // N_DEV, c)


def _from_slots(name, slots):
    _, r, c = slots.shape
    if COL_SHARDED[name]:
        return slots.transpose(1, 0, 2).reshape(r, N_DEV * c)
    return slots.reshape(N_DEV * r, c)


def kernel(x, g_pre, w_in, b_gate, g_q, w_uq, g_kv, w_ukv, lb_logits, g_hgrn, w_branch_a, w_branch_b, w_out, g_post, loss_target, m_g_pre, m_w_in, m_b_gate, m_g_q, m_w_uq, m_g_kv, m_w_ukv, m_lb_logits, m_g_hgrn, m_w_branch_a, m_w_branch_b, m_w_out, m_g_post, v_g_pre, v_w_in, v_b_gate, v_g_q, v_w_uq, v_g_kv, v_w_ukv, v_lb_logits, v_g_hgrn, v_w_branch_a, v_w_branch_b, v_w_out, v_g_post):
    rows3 = lambda a: jnp.transpose(a, (2, 0, 1))
    w = dict(w_in=rows3(w_in), w_uq=w_uq[0], w_ukv=w_ukv[0], w_branch_a=w_branch_a[0], w_branch_b=w_branch_b[0],
             w_out=w_out[0], g_pre=g_pre, b_gate=b_gate, g_q=g_q, g_kv=g_kv, lb_logits=lb_logits, g_hgrn=g_hgrn,
             g_post=g_post)
    mom = dict(w_in=rows3(m_w_in), w_uq=m_w_uq[0], w_ukv=m_w_ukv[0], w_branch_a=m_w_branch_a[0],
               w_branch_b=m_w_branch_b[0], w_out=m_w_out[0], g_pre=m_g_pre, b_gate=m_b_gate, g_q=m_g_q, g_kv=m_g_kv,
               lb_logits=m_lb_logits, g_hgrn=m_g_hgrn, g_post=m_g_post)
    var = dict(w_in=rows3(v_w_in), w_uq=v_w_uq[0], w_ukv=v_w_ukv[0], w_branch_a=v_w_branch_a[0],
               w_branch_b=v_w_branch_b[0], w_out=v_w_out[0], g_pre=v_g_pre, b_gate=v_b_gate, g_q=v_g_q, g_kv=v_g_kv,
               lb_logits=v_lb_logits, g_hgrn=v_g_hgrn, g_post=v_g_post)

    (w_in_slots,) = _all_gather([w["w_in"].reshape(W_IN_SHARD, D_MODEL).astype(BF16)])
    dx, recv_in, recv, g_sum = _step(x[0], loss_target[0], w_in_slots, [w[n].astype(BF16) for n in MATS],
                                     g_pre, b_gate, g_q, g_kv, lb_logits, g_hgrn, g_post)

    g_in, d_in, m_in, v_in = _sum_adamw_w_in(recv_in, w["w_in"], mom["w_in"], var["w_in"])
    res = _sum_adamw_whole([recv[n] for n in MATS], *([t[n] for n in MATS] for t in (w, mom, var)))
    total, *vec = _vectors_adamw(g_sum, *([t[n] for n in SMALL] for t in (w, mom, var)))

    outs = []
    for mats, vecs, big in zip(res, vec, (g_in, d_in, m_in, v_in)):
        t = {**{n: a[None] for n, a in zip(MATS, mats)}, **dict(zip(SMALL, vecs)),
             "w_in": jnp.transpose(big, (1, 2, 0))}
        outs += [t[n] for n in ORDER]
    return (total.reshape(()), dx[None], *outs)
```

```python
import math

import jax
import jax.numpy as jnp
import numpy as np
from jax import lax
from jax.experimental import pallas as pl
from jax.experimental.pallas import tpu as pltpu

F32, BF16 = jnp.float32, jnp.bfloat16

D_MODEL = 1024
EPS = 1e-6
HEADS = 8
NOPE, ROPE, VDIM = 64, 32, 64
QK = NOPE + ROPE
Q_LORA, KV_LORA = 768, 256
ROPE_THETA = 10000.0
ATT_CHUNK_SHIFT = 6
HG_BLOCK = 32
HG_WIDTH = 512
D_IN = 5664
D_IN_PAD = 5760
W_IN_SHARD = D_IN // 8
N_DEV = 8
LANE = 128

ADAM_LR, ADAM_B1, ADAM_B2, ADAM_EPS, ADAM_WD, ADAM_STEP = 0.001, 0.9, 0.999, 1e-08, 0.01, 10

W_IN_SEGMENTS = ((3616, 5664, 0), (1056, 1568, 2048), (3104, 3616, 2560), (1568, 3104, 3072),
                 (0, 1024, 4608), (1024, 1056, 5696))
W_IN_ZERO = ((5632, 5696), (5728, 5760))

NT = (((1,), (1,)), ((), ()))
TN = (((0,), (0,)), ((), ()))
MESH_ID = pl.DeviceIdType.MESH


def _w_in_pieces():
    out = []
    for lo, hi, dst in W_IN_SEGMENTS:
        c = lo
        while c < hi:
            p = c // W_IN_SHARD
            e = min(hi, (p + 1) * W_IN_SHARD)
            out.append((p, c - p * W_IN_SHARD, e - p * W_IN_SHARD, dst + c - lo))
            c = e
    return out


def _params(sem, vmem_mb=48):
    return pltpu.CompilerParams(dimension_semantics=sem, vmem_limit_bytes=vmem_mb * 2**20)


def _dot(a, b):
    return jnp.dot(a, b, preferred_element_type=F32)


def _dotg(a, b, dims):
    return lax.dot_general(a, b, dims, preferred_element_type=F32)


def _split2(x):
    hi = x.astype(BF16)
    return hi, (x - hi.astype(F32)).astype(BF16)


def _sel_left(m01, x):
    hi, lo = _split2(x)
    return _dot(m01, hi) + _dot(m01, lo)


def _sel_right(x, m01):
    hi, lo = _split2(x)
    return _dot(hi, m01) + _dot(lo, m01)


def _hi_lo(x):
    hi = x.astype(BF16).astype(F32)
    return hi, x - hi


def _sigmoid(x):
    return 0.5 * jnp.tanh(0.5 * x) + 0.5


def _rope(x, c, s1, s2):
    return x * c + pltpu.roll(x, 112, 1) * s1 + pltpu.roll(x, 16, 1) * s2


def _unrope(d, c, s1, s2):
    return d * c + pltpu.roll(d * s1, 16, 1) + pltpu.roll(d * s2, 112, 1)


def _my_place():
    return lax.axis_index("x"), lax.axis_index("y"), lax.axis_index("c")


def _flip(k, x, y, c):
    fx, fy, fc = (k + 1) >> 2 & 1, (k + 1) >> 1 & 1, (k + 1) & 1
    return (1 - x if fx else x), (1 - y if fy else y), (1 - c if fc else c)


def _to_all_copies(s_refs, r_refs, sems, spread):
    send_sems, recv_sems, local_sems = sems
    x, y, c = _my_place()
    me = 4 * x + 2 * y + c
    src = (lambda a, p: s_refs[a]) if spread else (lambda a, p: s_refs[a].at[p])
    local = [pltpu.make_async_copy(src(a, me), r_refs[a].at[me], local_sems.at[a]) for a in range(len(s_refs))]
    remote = []
    for k in range(N_DEV - 1):
        px, py, pc = _flip(k, x, y, c)
        for a in range(len(s_refs)):
            remote.append(pltpu.make_async_remote_copy(
                src_ref=src(a, 4 * px + 2 * py + pc), dst_ref=r_refs[a].at[me],
                send_sem=send_sems.at[7 * a + k], recv_sem=recv_sems.at[7 * a + k],
                device_id=(px, py, pc), device_id_type=MESH_ID))
    return local, remote


def _to_chips_copies(s_refs, r_refs, sems):
    send_sems, recv_sems, local_sems = sems
    x, y, c = _my_place()
    me = 2 * x + y
    local = [pltpu.make_async_copy(s_refs[a].at[me], r_refs[a].at[me], local_sems.at[a]) for a in range(len(s_refs))]
    remote = []
    for k in range(3):
        px = 1 - x if (k + 1) >> 1 & 1 else x
        py = 1 - y if (k + 1) & 1 else y
        for a in range(len(s_refs)):
            remote.append(pltpu.make_async_remote_copy(
                src_ref=s_refs[a].at[2 * px + py], dst_ref=r_refs[a].at[me],
                send_sem=send_sems.at[3 * a + k], recv_sem=recv_sems.at[3 * a + k],
                device_id=(px, py, c), device_id_type=MESH_ID))
    return local, remote


def _start_all(local, remote):
    for cp in local + remote:
        cp.start()


def _wait_all(local, remote):
    for cp in remote:
        cp.wait_recv()
    for cp in remote:
        cp.wait_send()
    for cp in local:
        cp.wait()


def _copy_sems(n, peers):
    return [pltpu.SemaphoreType.DMA((peers * n,)), pltpu.SemaphoreType.DMA((peers * n,)),
            pltpu.SemaphoreType.DMA((n,))]


ANY = pl.BlockSpec(memory_space=pl.ANY)


def _assemble_w_in(slots):
    tc = 256
    pieces = _w_in_pieces()

    def body(s_ref, w_ref, wt_ref):
        for lo, hi in W_IN_ZERO:
            wt_ref[lo:hi, :] = jnp.zeros((hi - lo, tc), BF16)
        for p, lo, hi, dst in pieces:
            wt_ref[dst:dst + hi - lo, :] = s_ref[p, lo:hi, :]
        w_ref[...] = wt_ref[...].T

    return pl.pallas_call(
        body,
        grid=(D_MODEL // tc,),
        in_specs=[pl.BlockSpec((N_DEV, W_IN_SHARD, tc), lambda i: (0, 0, i))],
        out_specs=[pl.BlockSpec((tc, D_IN_PAD), lambda i: (i, 0)), pl.BlockSpec((D_IN_PAD, tc), lambda i: (0, i))],
        out_shape=[jax.ShapeDtypeStruct((D_MODEL, D_IN_PAD), BF16), jax.ShapeDtypeStruct((D_IN_PAD, D_MODEL), BF16)],
        compiler_params=_params(("arbitrary",)),
        name="assemble_w_in",
    )(slots)


def _scatter_w_in(dw):
    tc = 256
    pieces = _w_in_pieces()

    def body(d_ref, s_ref):
        dt = d_ref[...].T
        for p, lo, hi, dst in pieces:
            s_ref[p, lo:hi, :] = dt[dst:dst + hi - lo, :].astype(BF16)

    return pl.pallas_call(
        body,
        grid=(D_MODEL // tc,),
        in_specs=[pl.BlockSpec((tc, D_IN_PAD), lambda i: (i, 0))],
        out_specs=pl.BlockSpec((N_DEV, W_IN_SHARD, tc), lambda i: (0, 0, i)),
        out_shape=jax.ShapeDtypeStruct((N_DEV, W_IN_SHARD, D_MODEL), BF16),
        compiler_params=_params(("arbitrary",)),
        name="scatter_w_in",
    )(dw)


def _norm_proj(x, g_pre, w, shards):
    s, n = x.shape[0], w.shape[1]
    tm = 512
    ni, ns = s // tm, len(shards)

    def body(x_ref, g_ref, w_hbm, *rest):
        shard_refs, (proj_ref, ht_ref), got_refs = rest[:ns], rest[ns:ns + 2], rest[ns + 2:2 * ns + 2]
        w_ref, w_sem, sems = rest[2 * ns + 2], rest[2 * ns + 3], rest[2 * ns + 4:]
        i = pl.program_id(0)

        @pl.when(i == 0)
        def _():
            load = pltpu.make_async_copy(w_hbm, w_ref, w_sem)
            load.start()
            _start_all(*_to_all_copies(shard_refs, got_refs, sems, True))
            load.wait()

        xv = x_ref[...]
        r = lax.rsqrt(jnp.mean(xv * xv, axis=-1, keepdims=True) + EPS)
        h = (xv * r * g_ref[...]).astype(BF16)
        ht_ref[...] = h.T
        proj_ref[...] = _dot(h, w_ref[...])

        @pl.when(i == ni - 1)
        def _():
            _wait_all(*_to_all_copies(shard_refs, got_refs, sems, True))

    return pl.pallas_call(
        body,
        grid=(ni,),
        in_specs=[pl.BlockSpec((tm, D_MODEL), lambda i: (i, 0)), pl.BlockSpec((1, D_MODEL), lambda i: (0, 0)), ANY]
        + [ANY] * ns,
        out_specs=[pl.BlockSpec((tm, n), lambda i: (i, 0)), pl.BlockSpec((D_MODEL, tm), lambda i: (0, i))] + [ANY] * ns,
        out_shape=[jax.ShapeDtypeStruct((s, n), F32), jax.ShapeDtypeStruct((D_MODEL, s), BF16)]
        + [jax.ShapeDtypeStruct((N_DEV,) + b.shape, b.dtype) for b in shards],
        scratch_shapes=[pltpu.VMEM(w.shape, BF16), pltpu.SemaphoreType.DMA] + _copy_sems(ns, 7),
        compiler_params=_params(("arbitrary",), 56),
        name="norm_proj",
    )(x, g_pre, w, *shards)


def _mla_prep(proj, g_q, g_kv, w_uq_p, w_kv_p, rc, rs1, rs2):
    s = proj.shape[0]
    tm = 256
    scale = 1.0 / math.sqrt(QK)

    def body(cq_ref, ckv_ref, kpe_ref, gq_ref, gkv_ref, wuq_ref, wkv_ref, c_ref, s1_ref, s2_ref,
             qr_ref, kr_ref, v_ref, cqt_ref, ckvt_ref):
        cq = cq_ref[...]
        r = lax.rsqrt(jnp.mean(cq * cq, axis=-1, keepdims=True) + EPS)
        cqn = (cq * r * gq_ref[...]).astype(BF16)
        cqt_ref[...] = cqn.T
        q = _dot(cqn, wuq_ref[...])
        ckv = ckv_ref[...]
        r = lax.rsqrt(jnp.mean(ckv * ckv, axis=-1, keepdims=True) + EPS)
        ckvn = (ckv * r * gkv_ref[...]).astype(BF16)
        ckvt_ref[...] = ckvn.T
        kv = _dot(ckvn, wkv_ref[...])
        c, s1, s2 = c_ref[...], s1_ref[...], s2_ref[...]
        lane = lax.broadcasted_iota(jnp.int32, (tm, LANE), 1)
        kpe = _rope(kpe_ref[...], c, s1, s2) + jnp.where((lane == QK) | (lane == QK + 1), 1.0, 0.0)
        vone = jnp.where((lane == VDIM) | (lane == VDIM + 1), 1.0, 0.0)
        for h in range(HEADS):
            sl = slice(LANE * h, LANE * (h + 1))
            qr_ref[:, sl] = (_rope(q[:, sl], c, s1, s2) * scale).astype(BF16)
            kr_ref[:, sl] = (kv[:, sl] + kpe).astype(BF16)
            v_ref[:, sl] = (kv[:, HEADS * LANE + LANE * h:HEADS * LANE + LANE * (h + 1)] + vone).astype(BF16)

    row = lambda w, j: pl.BlockSpec((tm, w), lambda i: (i, j))
    col = lambda w: pl.BlockSpec((w, tm), lambda i: (0, i))
    full = lambda a: pl.BlockSpec(a.shape, lambda i: (0, 0))
    return pl.pallas_call(
        body,
        grid=(s // tm,),
        in_specs=[row(768, 6), row(256, 21), row(128, 44), full(g_q), full(g_kv), full(w_uq_p), full(w_kv_p),
                  row(128, 0), row(128, 0), row(128, 0)],
        out_specs=[row(1024, 0), row(1024, 0), row(1024, 0), col(768), col(256)],
        out_shape=[jax.ShapeDtypeStruct((s, 1024), BF16), jax.ShapeDtypeStruct((s, 1024), BF16),
                   jax.ShapeDtypeStruct((s, 1024), BF16), jax.ShapeDtypeStruct((768, s), BF16),
                   jax.ShapeDtypeStruct((256, s), BF16)],
        compiler_params=_params(("arbitrary",)),
        name="mla_prep",
    )(proj, proj, proj, g_q, g_kv, w_uq_p, w_kv_p, rc, rs1, rs2)


ATT_T = 512
ATT_FWD_HEADS = 4


def _chunk_mask(transposed):
    r = lax.broadcasted_iota(jnp.int32, (ATT_T, ATT_T), 0) >> ATT_CHUNK_SHIFT
    c = lax.broadcasted_iota(jnp.int32, (ATT_T, ATT_T), 1) >> ATT_CHUNK_SHIFT
    return (r <= c) if transposed else (c <= r)


def _attn_fwd(qr, kr, vp, shards):
    s = qr.shape[0]
    t = ATT_T
    g = ATT_FWD_HEADS
    ns = len(shards)

    def body(q_ref, k_ref, v_ref, *rest):
        shard_refs, (o_ref, qa_ref), got_refs = rest[:ns], rest[ns:ns + 2], rest[ns + 2:2 * ns + 2]
        sc_ref, sems = rest[2 * ns + 2], rest[2 * ns + 3:]
        qi = pl.program_id(1)

        @pl.when((pl.program_id(0) == 0) & (qi == 0))
        def _():
            _start_all(*_to_all_copies(shard_refs, got_refs, sems, True))
        lane = lax.broadcasted_iota(jnp.int32, (t, LANE), 1)
        sls = [slice(LANE * a, LANE * (a + 1)) for a in range(g)]
        qs = [q_ref[:, sl] for sl in sls]

        def scores(j):
            rows = pl.ds(pl.multiple_of(j * t, t), t)
            for a in range(g):
                sc_ref[j & 1, a] = _dotg(qs[a], k_ref[rows, sls[a]], NT)

        def step(j, carry, masked):
            rows = pl.ds(pl.multiple_of(j * t, t), t)
            out = []
            for a in range(g):
                m, acc = carry[a]
                sc = sc_ref[j & 1, a]
                if masked:
                    sc = jnp.where(_chunk_mask(False), sc, -1e30)
                m_new = jnp.maximum(m, jnp.max(sc, axis=-1, keepdims=True))
                p = jnp.exp(sc - m_new).astype(BF16)
                acc = jnp.exp(m - m_new) * acc + _dot(p, v_ref[rows, sls[a]])
                out.append((m_new, acc))
            return tuple(out)

        def loop(j, carry):
            carry = step(j, carry, False)
            scores(j + 1)
            return carry

        init = tuple((jnp.full((t, 1), -1e30, F32), jnp.zeros((t, LANE), F32)) for _ in range(g))
        scores(0)
        carry = lax.fori_loop(0, qi, loop, init)
        carry = step(qi, carry, True)
        outs = []
        for a in range(g):
            m, acc = carry[a]
            l = acc[:, VDIM:VDIM + 1]
            outs.append(acc / l)
            hi, lo_part = _hi_lo(-(m + jnp.log(l)))
            qa = jnp.where(lane == QK, hi, jnp.where(lane == QK + 1, lo_part, qs[a].astype(F32)))
            qa_ref[:, sls[a]] = qa.astype(BF16)
        for p in range(g // 2):
            o_ref[:, LANE * p:LANE * (p + 1)] = jnp.where(lane < VDIM, outs[2 * p], pltpu.roll(outs[2 * p + 1], VDIM, 1))

        @pl.when((pl.program_id(0) == HEADS // g - 1) & (qi == s // t - 1))
        def _():
            _wait_all(*_to_all_copies(shard_refs, got_refs, sems, True))

    return pl.pallas_call(
        body,
        grid=(HEADS // g, s // t),
        in_specs=[
            pl.BlockSpec((t, g * LANE), lambda h, i: (i, h)),
            pl.BlockSpec((s, g * LANE), lambda h, i: (0, h)),
            pl.BlockSpec((s, g * LANE), lambda h, i: (0, h)),
        ] + [ANY] * ns,
        out_specs=[
            pl.BlockSpec((t, g * VDIM), lambda h, i: (i, h)),
            pl.BlockSpec((t, g * LANE), lambda h, i: (i, h)),
        ] + [ANY] * ns,
        out_shape=[jax.ShapeDtypeStruct((s, 512), F32), jax.ShapeDtypeStruct((s, 1024), BF16)]
        + [jax.ShapeDtypeStruct((N_DEV,) + b.shape, b.dtype) for b in shards],
        scratch_shapes=[pltpu.VMEM((2, g, t, t), F32)] + _copy_sems(ns, 7),
        compiler_params=_params(("arbitrary", "arbitrary")),
        name="attn_fwd",
    )(qr, kr, vp, *shards)


def _attn_bwd(qa, kr, vp, dop, sends):
    s = qa.shape[0]
    t = ATT_T
    nq = s // t
    ns = len(sends)

    def body(q_ref, k_ref, v_ref, do_ref, *rest):
        send_refs, (dq_out, dk_out, dv_out) = rest[:ns], rest[ns:ns + 3]
        recv_refs = rest[ns + 3:2 * ns + 3]
        (dq_ref, dk_ref, dv_ref), sems = rest[2 * ns + 3:2 * ns + 6], rest[2 * ns + 6:]
        j = pl.program_id(1)
        sls = [slice(LANE * a, LANE * (a + 1)) for a in range(2)]

        @pl.when((pl.program_id(0) == 0) & (j == 0))
        def _():
            _start_all(*_to_all_copies(send_refs, recv_refs, sems, False))

        @pl.when(j == 0)
        def _():
            dq_ref[...] = jnp.zeros_like(dq_ref)

        dk_ref[...] = jnp.zeros_like(dk_ref)
        dv_ref[...] = jnp.zeros_like(dv_ref)
        ks = [k_ref[:, sl] for sl in sls]
        vs = [v_ref[:, sl] for sl in sls]

        def step(i, masked):
            rows = pl.ds(pl.multiple_of(i * t, t), t)
            for a in range(2):
                q = q_ref[rows, sls[a]]
                do = do_ref[rows, sls[a]]
                sc = _dotg(ks[a], q, NT)
                if masked:
                    sc = jnp.where(_chunk_mask(True), sc, -1e30)
                p = jnp.exp(sc)
                ds = (p * _dotg(vs[a], do, NT)).astype(BF16)
                dv_ref[:, sls[a]] += _dot(p.astype(BF16), do)
                dk_ref[:, sls[a]] += _dot(ds, q)
                dq_ref[rows, sls[a]] += _dotg(ds, ks[a], TN)

        step(j, True)

        def loop(i, c):
            step(i, False)
            return c

        lax.fori_loop(j + 1, nq, loop, 0)
        dk_out[...] = dk_ref[...].astype(BF16)
        dv_out[...] = dv_ref[...].astype(BF16)

        @pl.when(j == nq - 1)
        def _():
            dq_out[...] = dq_ref[...].astype(BF16)

        @pl.when((pl.program_id(0) == HEADS // 2 - 1) & (j == nq - 1))
        def _():
            _wait_all(*_to_all_copies(send_refs, recv_refs, sems, False))

    blk = pl.BlockSpec((t, 2 * LANE), lambda h, j: (j, h))
    whole = pl.BlockSpec((s, 2 * LANE), lambda h, j: (0, h))
    out = jax.ShapeDtypeStruct((s, 1024), BF16)
    return pl.pallas_call(
        body,
        grid=(HEADS // 2, nq),
        in_specs=[whole, blk, blk, whole] + [ANY] * ns,
        out_specs=[whole, blk, blk] + [ANY] * ns,
        out_shape=[out, out, out] + [jax.ShapeDtypeStruct(a.shape, a.dtype) for a in sends],
        scratch_shapes=[pltpu.VMEM((s, 2 * LANE), F32), pltpu.VMEM((t, 2 * LANE), F32),
                        pltpu.VMEM((t, 2 * LANE), F32)] + _copy_sems(ns, 7),
        compiler_params=_params(("arbitrary", "arbitrary")),
        name="attn_bwd",
    )(qa, kr, vp, dop, *sends)


HG_T = 256
HG_NC = HG_T // HG_BLOCK
HG_G = 4
GW = 64 * HG_G


def _hg_consts():
    r = jnp.arange(HG_T)[:, None]
    c = jnp.arange(HG_T)[None, :]
    same = (r // HG_BLOCK) == (c // HG_BLOCK)
    mcum = (same & (c <= r)).astype(BF16)
    mrev = (same & (c >= r)).astype(BF16)
    msum = same.astype(BF16)
    a = jnp.arange(GW) // 64
    bd = (a[:, None] == a[None, :]).astype(F32)
    return mcum, mrev, msum, bd


def _stack_heads(xg, head):
    return jnp.concatenate([jnp.where(head == h, xg, 0.0) for h in range(HG_G)], axis=0)


def _unstack_heads(r, head, t):
    out = r[(HG_G - 1) * t:]
    for h in range(HG_G - 2, -1, -1):
        out = jnp.where(head == h, r[h * t:(h + 1) * t], out)
    return out


def _compact_state(st):
    out = st[:64]
    for h in range(1, HG_G):
        out = out + st[64 * h:64 * (h + 1)]
    return out


def _expand_state(cs, head64):
    return jnp.concatenate([jnp.where(head64 == h, cs, 0.0) for h in range(HG_G)], axis=0)


def _hg_pre(hq, hf, lbl, mcum, msum):
    lb = _sigmoid(lbl[0:1, :] - lbl[1:2, :])
    sig = _sigmoid(hf)
    f = lb + (1.0 - lb) * sig
    lf = jnp.log(f)
    b = _sel_left(mcum, lf)
    big_l = _sel_left(msum, lf)
    k = 1.0 - f
    qd = hq * jnp.exp(b)
    ki = k * jnp.exp(-b)
    ke = k * jnp.exp(big_l - b)
    return lb, sig, f, b, big_l, qd, ki, ke


def _hgrn_fwd(proj, lbl):
    s = proj.shape[0]
    t = HG_T
    mcum, _, msum, bd = _hg_consts()

    def body(hq_ref, hf_ref, hi_ref, lbl_ref, mcum_ref, msum_ref, bd_ref, o_ref, sp_ref, st_ref):
        @pl.when(pl.program_id(0) == 0)
        def _():
            st_ref[...] = jnp.zeros_like(st_ref)

        mc = mcum_ref[...]
        _, _, _, _, big_l, qd, ki, ke = _hg_pre(hq_ref[...], hf_ref[...], lbl_ref[...], mc, msum_ref[...])
        el = jnp.exp(big_l)
        hi = hi_ref[...]
        head = lax.broadcasted_iota(jnp.int32, (t, GW), 1) >> 6
        mask = jnp.concatenate([mc] * HG_G, axis=0) > 0.5
        for p in range(HEADS // HG_G):
            sl = slice(GW * p, GW * (p + 1))
            vp = hi[:, sl].astype(BF16)
            qs = _stack_heads(qd[:, sl], head).astype(BF16)
            a = jnp.where(mask, _dotg(qs, ki[:, sl].astype(BF16), NT), 0.0)
            o_intra = _unstack_heads(_dot(a.astype(BF16), vp), head, t)
            qb = qd[:, sl].astype(BF16)
            kb = ke[:, sl].astype(BF16)
            st = st_ref[p]
            for c in range(HG_NC):
                rows = slice(HG_BLOCK * c, HG_BLOCK * (c + 1))
                sp_ref[c, :, sl] = _compact_state(st)
                o_ref[rows, sl] = o_intra[rows] + _dotg(qb[rows], st.astype(BF16), NT)
                u = _dotg(vp[rows], kb[rows], TN) * bd_ref[...]
                st = st * el[HG_BLOCK * c:HG_BLOCK * c + 1, sl] + u
            st_ref[p] = st

    row = lambda j: pl.BlockSpec((t, HG_WIDTH), lambda i: (i, j))
    full = lambda a: pl.BlockSpec(a.shape, lambda i: (0, 0))
    return pl.pallas_call(
        body,
        grid=(s // t,),
        in_specs=[row(6), row(7), row(8), full(lbl), full(mcum), full(msum), full(bd)],
        out_specs=[row(0), pl.BlockSpec((HG_NC, 64, HG_WIDTH), lambda i: (i, 0, 0))],
        out_shape=[jax.ShapeDtypeStruct((s, HG_WIDTH), F32),
                   jax.ShapeDtypeStruct((s // HG_BLOCK, 64, HG_WIDTH), F32)],
        scratch_shapes=[pltpu.VMEM((HEADS // HG_G, GW, GW), F32)],
        compiler_params=_params(("arbitrary",)),
        name="hgrn_fwd",
    )(proj, proj, proj, lbl, mcum, msum, bd)


def _hgrn_bwd(proj, lbl, do, sprev, dproj):
    s = proj.shape[0]
    t = HG_T
    nt = s // t
    mcum, mrev, msum, bd = _hg_consts()

    def body(hq_ref, hf_ref, hi_ref, lbl_ref, do_ref, sp_ref, mcum_ref, mrev_ref, msum_ref, bd_ref,
             dproj_in, dh_ref, dlbl_ref, g_ref):
        del dproj_in

        @pl.when(pl.program_id(0) == 0)
        def _():
            g_ref[...] = jnp.zeros_like(g_ref)
            dlbl_ref[...] = jnp.zeros_like(dlbl_ref)

        mc = mcum_ref[...]
        lb, sig, f, b, big_l, qd, ki, ke = _hg_pre(hq_ref[...], hf_ref[...], lbl_ref[...], mc, msum_ref[...])
        el = jnp.exp(big_l)
        hi = hi_ref[...]
        dov = do_ref[...]
        head = lax.broadcasted_iota(jnp.int32, (t, GW), 1) >> 6
        head64 = lax.broadcasted_iota(jnp.int32, (64, GW), 1) >> 6
        mask = jnp.concatenate([mc] * HG_G, axis=0) > 0.5
        dqd_parts, dke_parts, dv_parts, del_parts, dki_parts = [], [], [], [], []
        for p in range(HEADS // HG_G):
            sl = slice(GW * p, GW * (p + 1))
            vp = hi[:, sl].astype(BF16)
            qs = _stack_heads(qd[:, sl], head).astype(BF16)
            kip = ki[:, sl].astype(BF16)
            dos = _stack_heads(dov[:, sl], head).astype(BF16)
            a = jnp.where(mask, _dotg(qs, kip, NT), 0.0).astype(BF16)
            da = jnp.where(mask, _dotg(dos, vp, NT), 0.0).astype(BF16)
            r = _dot(da, kip)
            dki_parts.append(_dotg(da, qs, TN))
            qb = qd[:, sl].astype(BF16)
            kb = ke[:, sl].astype(BF16)
            dob = dov[:, sl].astype(BF16)
            g = g_ref[p]
            dqd_c, dv_c, dke_c, del_c = [], [], [], []
            for c in range(HG_NC - 1, -1, -1):
                rows = slice(HG_BLOCK * c, HG_BLOCK * (c + 1))
                gb = g.astype(BF16)
                st = _expand_state(sp_ref[c, :, sl], head64)
                dqd_c.append(_dot(dob[rows], st.astype(BF16)))
                dv_c.append(_dotg(kb[rows], gb, NT))
                dke_c.append(_dot(vp[rows], gb))
                del_c.append(jnp.broadcast_to(jnp.sum(g * st, axis=0, keepdims=True), (HG_BLOCK, GW)))
                g = g * el[HG_BLOCK * c:HG_BLOCK * c + 1, sl] + _dotg(dob[rows], qb[rows], TN) * bd_ref[...]
            g_ref[p] = g
            up = lambda parts: jnp.concatenate(parts[::-1], axis=0)
            dqd_parts.append(_unstack_heads(r, head, t) + up(dqd_c))
            dv_parts.append(_dotg(a, dos, TN) + up(dv_c))
            dke_parts.append(up(dke_c))
            del_parts.append(up(del_c))
        wide = lambda parts: jnp.concatenate(parts, axis=1)
        dqd, dke, dki, dvv, del_rows = wide(dqd_parts), wide(dke_parts), wide(dki_parts), wide(dv_parts), wide(del_parts)
        dh_ref[:, :HG_WIDTH] = (dqd * jnp.exp(b)).astype(BF16)
        dh_ref[:, 2 * HG_WIDTH:] = dvv.astype(BF16)
        dke_ke = dke * ke
        db = dqd * qd - dki * ki - dke_ke
        dl_rows = _sel_left(msum_ref[...], dke_ke) + del_rows * el
        is_last = (lax.broadcasted_iota(jnp.int32, (t, HG_WIDTH), 0) & (HG_BLOCK - 1)) == HG_BLOCK - 1
        db = db + jnp.where(is_last, dl_rows, 0.0)
        dlf = _sel_left(mrev_ref[...], db)
        dk = dki * jnp.exp(-b) + dke * jnp.exp(big_l - b)
        df = dlf / f - dk
        dh_ref[:, HG_WIDTH:2 * HG_WIDTH] = (df * (1.0 - lb) * sig * (1.0 - sig)).astype(BF16)
        dlb = jnp.sum(df * (1.0 - sig), axis=0, keepdims=True) * lb * (1.0 - lb)
        dlbl_ref[0:1, :] += dlb
        dlbl_ref[1:2, :] -= dlb

    rrow = lambda j: pl.BlockSpec((t, HG_WIDTH), lambda i: (nt - 1 - i, j))
    full = lambda a: pl.BlockSpec(a.shape, lambda i: (0, 0))
    return pl.pallas_call(
        body,
        grid=(nt,),
        in_specs=[rrow(6), rrow(7), rrow(8), full(lbl), rrow(0),
                  pl.BlockSpec((HG_NC, 64, HG_WIDTH), lambda i: (nt - 1 - i, 0, 0)),
                  full(mcum), full(mrev), full(msum), full(bd), pl.BlockSpec(memory_space=pl.ANY)],
        out_specs=[pl.BlockSpec((t, 3 * HG_WIDTH), lambda i: (nt - 1 - i, 2)),
                   pl.BlockSpec((2, HG_WIDTH), lambda i: (0, 0))],
        out_shape=[jax.ShapeDtypeStruct(dproj.shape, BF16), jax.ShapeDtypeStruct((2, HG_WIDTH), F32)],
        input_output_aliases={10: 0},
        scratch_shapes=[pltpu.VMEM((HEADS // HG_G, GW, GW), F32)],
        compiler_params=_params(("arbitrary",)),
        name="hgrn_bwd",
    )(proj, proj, proj, lbl, do, sprev, mcum, mrev, msum, bd, dproj)


def _tail(x, tgt, proj, attn, o, w_a, w_b, w_out, w_at, w_bt, w_outt, b_gate, g_post, gh):
    s = x.shape[0]
    tm = 256
    ones64 = (jnp.arange(HG_WIDTH)[:, None] // 64 == jnp.arange(HG_WIDTH)[None, :] // 64).astype(BF16)
    weights = (w_a, w_b, w_out, w_at, w_bt, w_outt)

    def body(x_ref, t_ref, ml_ref, ga_ref, gb_ref, at_ref, o_ref, *rest):
        w_hbm, (bg_ref, gp_ref, gh_ref, ones_ref) = rest[:6], rest[6:10]
        (dout_ref, dpj_ref, dop_ref, do_ref, mt_ref, dy_ref, yat_ref, dya_ref, ybt_ref, dyb_ref,
         loss_ref, dgp_ref, dbg_ref, dgh_ref) = rest[10:24]
        (wa_ref, wb_ref, wo_ref, wat_ref, wbt_ref, wot_ref), w_sem = rest[24:30], rest[30]

        @pl.when(pl.program_id(0) == 0)
        def _():
            loads = [pltpu.make_async_copy(src, dst, w_sem.at[k])
                     for k, (src, dst) in enumerate(zip(w_hbm, rest[24:30]))]
            _start_all(loads, [])
            loss_ref[...] = jnp.zeros_like(loss_ref)
            dgp_ref[...] = jnp.zeros_like(dgp_ref)
            dbg_ref[...] = jnp.zeros_like(dbg_ref)
            dgh_ref[...] = jnp.zeros_like(dgh_ref)
            _wait_all(loads, [])

        ones = ones_ref[...]
        gate_a = ga_ref[...]
        sa = _sigmoid(gate_a)
        silu_a = gate_a * sa
        attn_v = at_ref[...]
        ya_in = attn_v * silu_a
        ov = o_ref[...]
        ro = lax.rsqrt(_sel_right(ov * ov, ones) * (1.0 / 64.0) + EPS)
        ohat = ov * ro
        ghv = gh_ref[...]
        on = ohat * ghv
        gate_b = gb_ref[...]
        sb = _sigmoid(gate_b)
        silu_b = gate_b * sb
        yb_in = on * silu_b
        ya_bf = ya_in.astype(BF16)
        yb_bf = yb_in.astype(BF16)
        yat_ref[...] = ya_bf.T
        ybt_ref[...] = yb_bf.T
        y_a = _dot(ya_bf, wa_ref[...])
        y_b = _dot(yb_bf, wb_ref[...])
        gts = _sigmoid(ml_ref[...] + bg_ref[...])
        g_a = gts[:, :D_MODEL]
        g_b = gts[:, D_MODEL:]
        m_bf = (g_a * y_a + g_b * y_b).astype(BF16)
        mt_ref[...] = m_bf.T
        y = _dot(m_bf, wo_ref[...])
        r1 = lax.rsqrt(jnp.mean(y * y, axis=-1, keepdims=True) + EPS)
        yn = y * r1
        gp = gp_ref[...]
        e = x_ref[...] + yn * gp - t_ref[...]
        loss_ref[...] += jnp.sum(e * e, axis=0, keepdims=True)
        dout = e * (1.0 / D_MODEL)
        dout_ref[...] = dout
        dgp_ref[...] += jnp.sum(dout * yn, axis=0, keepdims=True)
        dyn = dout * gp
        dy = r1 * (dyn - yn * jnp.mean(dyn * yn, axis=-1, keepdims=True))
        dy_bf = dy.astype(BF16)
        dy_ref[...] = dy_bf
        dm = _dot(dy_bf, wot_ref[...])
        dml_a = dm * y_a * g_a * (1.0 - g_a)
        dml_b = dm * y_b * g_b * (1.0 - g_b)
        dpj_ref[:, :D_MODEL] = dml_a.astype(BF16)
        dpj_ref[:, D_MODEL:2 * D_MODEL] = dml_b.astype(BF16)
        dbg_ref[:, :D_MODEL] += jnp.sum(dml_a, axis=0, keepdims=True)
        dbg_ref[:, D_MODEL:] += jnp.sum(dml_b, axis=0, keepdims=True)
        dya_bf = (dm * g_a).astype(BF16)
        dyb_bf = (dm * g_b).astype(BF16)
        dya_ref[...] = dya_bf
        dyb_ref[...] = dyb_bf
        dya_in = _dot(dya_bf, wat_ref[...])
        dyb_in = _dot(dyb_bf, wbt_ref[...])
        dattn = dya_in * silu_a
        delta = _sel_right(dattn * attn_v, ones)
        lane = lax.broadcasted_iota(jnp.int32, (tm, LANE), 1)
        for p in range(HEADS // 2):
            sl = slice(LANE * p, LANE * (p + 1))
            xs = (dattn[:, sl], pltpu.roll(dattn[:, sl], VDIM, 1))
            nds = (-pltpu.roll(delta[:, sl], VDIM, 1), -delta[:, sl])
            for a in range(2):
                hi, lo_part = _hi_lo(nds[a])
                blk = jnp.where(lane < VDIM, xs[a], jnp.where(lane == VDIM, hi, jnp.where(lane == VDIM + 1, lo_part, 0.0)))
                dop_ref[:, LANE * (2 * p + a):LANE * (2 * p + a + 1)] = blk.astype(BF16)
        dpj_ref[:, 2 * D_MODEL:2 * D_MODEL + HG_WIDTH] = (
            dya_in * attn_v * (sa * (1.0 + gate_a * (1.0 - sa)))).astype(BF16)
        don = dyb_in * silu_b
        dpj_ref[:, 2 * D_MODEL + HG_WIDTH:] = (dyb_in * on * (sb * (1.0 + gate_b * (1.0 - sb)))).astype(BF16)
        dgh_ref[...] += jnp.sum(don * ohat, axis=0, keepdims=True)
        dohat = don * ghv
        do_ref[...] = ro * (dohat - ohat * (_sel_right(dohat * ohat, ones) * (1.0 / 64.0)))

    row = lambda w, j: pl.BlockSpec((tm, w), lambda i: (i, j))
    col = lambda w: pl.BlockSpec((w, tm), lambda i: (0, i))
    full = lambda a: pl.BlockSpec(a.shape, lambda i: (0, 0))
    acc = lambda w: pl.BlockSpec((1, w), lambda i: (0, 0))
    sds = lambda w, dt: jax.ShapeDtypeStruct((s, w), dt)
    sdt = lambda w: jax.ShapeDtypeStruct((w, s), BF16)
    return pl.pallas_call(
        body,
        grid=(s // tm,),
        in_specs=[row(1024, 0), row(1024, 0), row(2048, 0), row(512, 4), row(512, 5), row(512, 0), row(512, 0)]
        + [ANY] * 6 + [full(b_gate), full(g_post), full(gh), full(ones64)],
        out_specs=[row(1024, 0), row(3072, 0), row(1024, 0), row(512, 0),
                   col(1024), row(1024, 0), col(512), row(1024, 0), col(512), row(1024, 0),
                   acc(1024), acc(1024), acc(2048), acc(512)],
        out_shape=[sds(1024, F32), sds(D_IN_PAD, BF16), sds(1024, BF16), sds(512, F32),
                   sdt(1024), sds(1024, BF16), sdt(512), sds(1024, BF16), sdt(512), sds(1024, BF16),
                   jax.ShapeDtypeStruct((1, 1024), F32), jax.ShapeDtypeStruct((1, 1024), F32),
                   jax.ShapeDtypeStruct((1, 2048), F32), jax.ShapeDtypeStruct((1, 512), F32)],
        scratch_shapes=[pltpu.VMEM(a.shape, BF16) for a in weights] + [pltpu.SemaphoreType.DMA((6,))],
        compiler_params=_params(("arbitrary",), 56),
        name="tail",
    )(x, tgt, proj, proj, proj, attn, o, *weights, b_gate, g_post, gh, ones64)


def _mla_bwd(proj, dqr, dkr, dv, g_q, g_kv, w_uq_pt, w_kv_pt, rc, rs1, rs2, dproj):
    s = proj.shape[0]
    tm = 256
    scale = 1.0 / math.sqrt(QK)

    def body(cq_ref, ckv_ref, dqr_ref, dkr_ref, dv_ref, gq_ref, gkv_ref, wuqt_ref, wkvt_ref, c_ref, s1_ref, s2_ref,
             dproj_in, dqf_ref, dkvf_ref, dc_ref, dgq_ref, dgkv_ref):
        del dproj_in

        @pl.when(pl.program_id(0) == 0)
        def _():
            dgq_ref[...] = jnp.zeros_like(dgq_ref)
            dgkv_ref[...] = jnp.zeros_like(dgkv_ref)

        c, s1, s2 = c_ref[...], s1_ref[...], s2_ref[...]
        lane = lax.broadcasted_iota(jnp.int32, (tm, LANE), 1)
        ksum = jnp.zeros((tm, LANE), F32)
        for h in range(HEADS):
            sl = slice(LANE * h, LANE * (h + 1))
            dqf_ref[:, sl] = (_unrope(dqr_ref[:, sl], c, s1, s2) * scale).astype(BF16)
            dkh = dkr_ref[:, sl]
            ksum = ksum + dkh
            dkvf_ref[:, sl] = jnp.where(lane < NOPE, dkh, 0.0).astype(BF16)
            dkvf_ref[:, HEADS * LANE + LANE * h:HEADS * LANE + LANE * (h + 1)] = jnp.where(
                lane < VDIM, dv_ref[:, sl], 0.0).astype(BF16)
        dkpe = _unrope(ksum, c, s1, s2)
        dc_ref[:, Q_LORA + KV_LORA:] = jnp.where((lane >= NOPE) & (lane < QK), dkpe, 0.0).astype(BF16)
        dcqn = _dot(dqf_ref[...], wuqt_ref[...])
        dckvn = _dot(dkvf_ref[...], wkvt_ref[...])
        for x_ref, g_ref, dn, cols, dg_ref in ((cq_ref, gq_ref, dcqn, slice(0, Q_LORA), dgq_ref),
                                               (ckv_ref, gkv_ref, dckvn, slice(Q_LORA, Q_LORA + KV_LORA), dgkv_ref)):
            xv = x_ref[...]
            r = lax.rsqrt(jnp.mean(xv * xv, axis=-1, keepdims=True) + EPS)
            xh = xv * r
            dg_ref[...] += jnp.sum(dn * xh, axis=0, keepdims=True)
            dh = dn * g_ref[...]
            dc_ref[:, cols] = (r * (dh - xh * jnp.mean(dh * xh, axis=-1, keepdims=True))).astype(BF16)

    row = lambda w, j: pl.BlockSpec((tm, w), lambda i: (i, j))
    full = lambda a: pl.BlockSpec(a.shape, lambda i: (0, 0))
    acc = lambda w: pl.BlockSpec((1, w), lambda i: (0, 0))
    sds = lambda w, dt: jax.ShapeDtypeStruct((s, w), dt)
    return pl.pallas_call(
        body,
        grid=(s // tm,),
        in_specs=[row(768, 6), row(256, 21), row(1024, 0), row(1024, 0), row(1024, 0), full(g_q), full(g_kv),
                  full(w_uq_pt), full(w_kv_pt), row(128, 0), row(128, 0), row(128, 0),
                  pl.BlockSpec(memory_space=pl.ANY)],
        out_specs=[row(1024, 0), row(2048, 0), row(1152, 4), acc(768), acc(256)],
        out_shape=[sds(1024, BF16), sds(2048, BF16), jax.ShapeDtypeStruct(dproj.shape, BF16),
                   jax.ShapeDtypeStruct((1, 768), F32), jax.ShapeDtypeStruct((1, 256), F32)],
        input_output_aliases={12: 2},
        compiler_params=_params(("arbitrary",)),
        name="mla_bwd",
    )(proj, proj, dqr, dkr, dv, g_q, g_kv, w_uq_pt, w_kv_pt, rc, rs1, rs2, dproj)


def _pick(n, options):
    for o in options:
        if n % o == 0:
            return o
    raise ValueError(n)


def _matmul(a, b, name):
    m, k = a.shape
    n = b.shape[1]
    tm = _pick(m, (1024, 768, 512, 256))
    tn = _pick(n, (1152, 1024, 768, 512))
    tk = _pick(k, (1024, 512))
    nk = k // tk

    def body(a_ref, b_ref, o_ref):
        @pl.when(pl.program_id(2) == 0)
        def _():
            o_ref[...] = jnp.zeros_like(o_ref)

        o_ref[...] += _dot(a_ref[...], b_ref[...])

    return pl.pallas_call(
        body,
        grid=(m // tm, n // tn, nk),
        in_specs=[pl.BlockSpec((tm, tk), lambda i, j, l: (i, l)), pl.BlockSpec((tk, tn), lambda i, j, l: (l, j))],
        out_specs=pl.BlockSpec((tm, tn), lambda i, j, l: (i, j)),
        out_shape=jax.ShapeDtypeStruct((m, n), F32),
        compiler_params=_params(("arbitrary", "arbitrary", "arbitrary")),
        name=name,
    )(a, b)


def _dh_dx(dproj, w_in_pt, x, dout, g_pre, sends):
    s, k = dproj.shape
    tm = 256
    ns, ni = len(sends), s // tm

    def body(dp_ref, w_ref, x_ref, dout_ref, g_ref, *rest):
        send_refs, (dx_ref, dg_ref) = rest[:ns], rest[ns:ns + 2]
        recv_refs, sems = rest[ns + 2:2 * ns + 2], rest[2 * ns + 2:]

        @pl.when(pl.program_id(0) == 0)
        def _():
            _start_all(*_to_chips_copies(send_refs, recv_refs, sems))
            dg_ref[...] = jnp.zeros_like(dg_ref)

        dh = _dot(dp_ref[...], w_ref[...])
        xv = x_ref[...]
        r = lax.rsqrt(jnp.mean(xv * xv, axis=-1, keepdims=True) + EPS)
        xh = xv * r
        dg_ref[...] += jnp.sum(dh * xh, axis=0, keepdims=True)
        dxh = dh * g_ref[...]
        dx_ref[...] = dout_ref[...] + r * (dxh - xh * jnp.mean(dxh * xh, axis=-1, keepdims=True))

        @pl.when(pl.program_id(0) == ni - 1)
        def _():
            _wait_all(*_to_chips_copies(send_refs, recv_refs, sems))

    row = lambda w: pl.BlockSpec((tm, w), lambda i: (i, 0))
    return pl.pallas_call(
        body,
        grid=(ni,),
        in_specs=[row(k), pl.BlockSpec((k, D_MODEL), lambda i: (0, 0)), row(D_MODEL), row(D_MODEL),
                  pl.BlockSpec((1, D_MODEL), lambda i: (0, 0))] + [ANY] * ns,
        out_specs=[row(D_MODEL), pl.BlockSpec((1, D_MODEL), lambda i: (0, 0))] + [ANY] * ns,
        out_shape=[jax.ShapeDtypeStruct((s, D_MODEL), F32), jax.ShapeDtypeStruct((1, D_MODEL), F32)]
        + [jax.ShapeDtypeStruct(a.shape, a.dtype) for a in sends],
        scratch_shapes=_copy_sems(ns, 3),
        compiler_params=_params(("arbitrary",)),
        name="dh_dx",
    )(dproj, w_in_pt, x, dout, g_pre, *sends)


def _pair_reduce(slots):
    n = len(slots)
    half = [(N_DEV // 2,) + a.shape[1:] for a in slots]

    def body(*refs):
        s_refs, o_refs = refs[:n], refs[n:2 * n]
        mine, got = refs[2 * n:3 * n], refs[3 * n:4 * n]
        send_sems, recv_sems, local_sems = refs[4 * n:]
        x, y, c = _my_place()
        copies, loads = [], []
        for a in range(n):
            for q in range(N_DEV // 2):
                copies.append(pltpu.make_async_remote_copy(
                    src_ref=s_refs[a].at[2 * q + 1 - c], dst_ref=got[a].at[q],
                    send_sem=send_sems.at[4 * a + q], recv_sem=recv_sems.at[4 * a + q],
                    device_id=(x, y, 1 - c), device_id_type=MESH_ID))
                loads.append(pltpu.make_async_copy(s_refs[a].at[2 * q + c], mine[a].at[q], local_sems.at[4 * a + q]))
        _start_all(loads, copies)
        _wait_all(loads, copies)
        for a in range(n):
            o_refs[a][...] = (mine[a][...].astype(F32) + got[a][...].astype(F32)).astype(o_refs[a].dtype)

    vm = lambda: [pltpu.VMEM(h, a.dtype) for h, a in zip(half, slots)]
    return pl.pallas_call(
        body,
        in_specs=[ANY] * n,
        out_shape=[jax.ShapeDtypeStruct(h, a.dtype) for h, a in zip(half, slots)],
        scratch_shapes=vm() + vm() + [pltpu.SemaphoreType.DMA((4 * n,)), pltpu.SemaphoreType.DMA((4 * n,)),
                                      pltpu.SemaphoreType.DMA((4 * n,))],
        compiler_params=pltpu.CompilerParams(vmem_limit_bytes=48 * 2**20),
        name="pair_reduce",
    )(*slots)


def _rope_tables(s):
    inv = (np.float32(ROPE_THETA) ** (-np.arange(0, ROPE, 2, dtype=np.float32) / np.float32(ROPE))).astype(np.float32)
    ang = (np.arange(s, dtype=np.float32)[:, None] * inv[None, :]).astype(np.float32)
    cos, sin = jnp.asarray(np.cos(ang.astype(np.float64)), F32), jnp.asarray(np.sin(ang.astype(np.float64)), F32)
    z = lambda w: jnp.zeros((s, w), F32)
    rc = jnp.concatenate([jnp.ones((s, NOPE), F32), cos, cos, z(32)], axis=1)
    rs1 = jnp.concatenate([z(NOPE), -sin, z(16), z(32)], axis=1)
    rs2 = jnp.concatenate([z(NOPE), z(16), sin, z(32)], axis=1)
    return rc, rs1, rs2


def _step(x, tgt, w_in_slots, shards, g_pre, b_gate, g_q, g_kv, lbl, g_hgrn, g_post):
    s = x.shape[0]
    w_in_p, w_in_pt = _assemble_w_in(w_in_slots)
    rc, rs1, rs2 = _rope_tables(s)
    gh = jnp.tile(g_hgrn, (1, HEADS))

    proj, ht, *got = _norm_proj(x, g_pre, w_in_p, shards[:2])
    w_uq, w_ukv = (_from_slots(n, g) for n, g in zip(MATS[:2], got))
    w_uq_p = jnp.pad(w_uq.reshape(Q_LORA, HEADS, QK), ((0, 0), (0, 0), (0, LANE - QK))).reshape(Q_LORA, HEADS * LANE)
    kv3 = w_ukv.reshape(KV_LORA, HEADS, NOPE + VDIM)
    pad64 = lambda t: jnp.pad(t, ((0, 0), (0, 0), (0, LANE - 64))).reshape(KV_LORA, HEADS * LANE)
    w_kv_p = jnp.concatenate([pad64(kv3[:, :, :NOPE]), pad64(kv3[:, :, NOPE:])], axis=1)

    qr, kr, v, cqt, ckvt = _mla_prep(proj, g_q, g_kv, w_uq_p, w_kv_p, rc, rs1, rs2)
    attn, qa, *got = _attn_fwd(qr, kr, v, shards[2:])
    w_a, w_b, w_out = (_from_slots(n, g) for n, g in zip(MATS[2:], got))
    o, sprev = _hgrn_fwd(proj, lbl)
    (dout, dproj, dop, do, mt, dy_bf, yat, dya_bf, ybt, dyb_bf,
     loss_vec, dg_post, db_gate, dgh) = _tail(x, tgt, proj, attn, o, w_a, w_b, w_out, w_a.T, w_b.T, w_out.T,
                                               b_gate, g_post, gh)
    early = [_to_slots(n, _matmul(a, b, "d" + n)).astype(BF16)
             for n, a, b in (("w_branch_a", yat, dya_bf), ("w_branch_b", ybt, dyb_bf), ("w_out", mt, dy_bf))]
    dqr, dkr, dv, *early_recv = _attn_bwd(qa, kr, v, dop, early)
    dproj, dlbl = _hgrn_bwd(proj, lbl, do, sprev, dproj)
    dqf, dkvf, dproj, dg_q, dg_kv = _mla_bwd(proj, dqr, dkr, dv, g_q, g_kv, w_uq_p.T, w_kv_p.T, rc, rs1, rs2, dproj)

    dw_in_slots = _scatter_w_in(_matmul(ht, dproj, "dw_in"))
    dw_uq_p = _matmul(cqt, dqf, "dw_uq")
    dw_kv_p = _matmul(ckvt, dkvf, "dw_kv")
    dw_uq = dw_uq_p.reshape(Q_LORA, HEADS, LANE)[:, :, :QK].reshape(Q_LORA, HEADS * QK)
    dw_ukv = jnp.concatenate([dw_kv_p[:, :HEADS * LANE].reshape(KV_LORA, HEADS, LANE)[:, :, :NOPE],
                              dw_kv_p[:, HEADS * LANE:].reshape(KV_LORA, HEADS, LANE)[:, :, :VDIM]],
                             axis=2).reshape(KV_LORA, 1024)
    late = _pair_reduce([dw_in_slots, _to_slots("w_uq", dw_uq).astype(BF16), _to_slots("w_ukv", dw_ukv).astype(BF16)])
    dx, dg_pre, *late_recv = _dh_dx(dproj, w_in_pt, x, dout, g_pre, late)

    g_sum = _vectors_sum(dg_pre, db_gate, dg_q, dg_kv, dlbl, dgh, dg_post, loss_vec)
    return dx, late_recv[0], dict(zip(MATS, late_recv[1:] + early_recv)), g_sum


def _all_gather(blocks):
    n = len(blocks)

    def body(*refs):
        x_refs, out_refs = refs[:n], refs[n:2 * n]
        send_sems, recv_sems, local_sems = refs[2 * n:]
        x, y, c = _my_place()
        me, sibling = (x, y, c), (x, y, 1 - c)
        chips = [(1 - x, y), (x, 1 - y), (1 - x, 1 - y)]

        def slot(a, px, py, pc):
            return out_refs[a].at[4 * px + 2 * py + pc]

        def copy(a, k, blk, to, src=None):
            return pltpu.make_async_remote_copy(
                src_ref=slot(a, *blk) if src is None else src, dst_ref=slot(a, *blk),
                send_sem=send_sems.at[7 * a + k], recv_sem=recv_sems.at[7 * a + k],
                device_id=to, device_id_type=MESH_ID)

        mine = [pltpu.make_async_copy(x_refs[a], slot(a, *me), local_sems.at[a]) for a in range(n)]
        for cp in mine:
            cp.start()
        first = [copy(a, 0, me, sibling, src=x_refs[a]) for a in range(n)]
        first += [copy(a, 1 + j, me, (*chip, c), src=x_refs[a]) for a in range(n) for j, chip in enumerate(chips)]
        for cp in first:
            cp.start()
        passed = []
        for j, chip in enumerate(chips):
            for a in range(n):
                copy(a, 1 + j, (*chip, c), me).wait_recv()
                passed.append(copy(a, 4 + j, (*chip, c), sibling))
                passed[-1].start()
        for a in range(n):
            copy(a, 0, sibling, me).wait_recv()
        for j, chip in enumerate(chips):
            for a in range(n):
                copy(a, 4 + j, (*chip, 1 - c), me).wait_recv()
        for cp in first + passed:
            cp.wait_send()
        for cp in mine:
            cp.wait()

    return pl.pallas_call(
        body,
        out_shape=[jax.ShapeDtypeStruct((N_DEV,) + b.shape, b.dtype) for b in blocks],
        in_specs=[pl.BlockSpec(memory_space=pl.ANY)] * n,
        out_specs=[pl.BlockSpec(memory_space=pl.ANY)] * n,
        scratch_shapes=[pltpu.SemaphoreType.DMA((7 * n,)), pltpu.SemaphoreType.DMA((7 * n,)),
                        pltpu.SemaphoreType.DMA((n,))],
        name="gather_weights",
    )(*blocks)


def _adamw(g, w, m, v):
    c1 = 1.0 / (1.0 - ADAM_B1 ** ADAM_STEP)
    c2 = 1.0 / (1.0 - ADAM_B2 ** ADAM_STEP)
    nm = ADAM_B1 * m + (1.0 - ADAM_B1) * g
    nv = ADAM_B2 * v + (1.0 - ADAM_B2) * (g * g)
    d = -ADAM_LR * ((nm * c1) / (jnp.sqrt(nv * c2) + ADAM_EPS) + ADAM_WD * w)
    return d, nm, nv


def _sum8(r_ref):
    g = r_ref[0].astype(F32)
    for k in range(1, r_ref.shape[0]):
        g = g + r_ref[k].astype(F32)
    return g


def _sum_adamw_w_in(recv, w, m, v):
    rows, _, cols = w.shape
    tc = 256

    def body(r_ref, w_ref, m_ref, v_ref, g_ref, d_ref, nm_ref, nv_ref):
        g = _sum8(r_ref)
        dense = lambda ref: ref[...].reshape(rows, tc)
        d, nm, nv = _adamw(g, dense(w_ref), dense(m_ref), dense(v_ref))
        for ref, val in ((g_ref, g), (d_ref, d), (nm_ref, nm), (nv_ref, nv)):
            ref[...] = val.reshape(rows, 1, tc)

    blk = pl.BlockSpec((rows, 1, tc), lambda i: (0, 0, i))
    out = jax.ShapeDtypeStruct((rows, 1, cols), F32)
    return pl.pallas_call(
        body,
        grid=(cols // tc,),
        in_specs=[pl.BlockSpec((recv.shape[0], rows, tc), lambda i: (0, 0, i)), blk, blk, blk],
        out_specs=[blk, blk, blk, blk],
        out_shape=[out, out, out, out],
        compiler_params=_params(("arbitrary",)),
        name="sum_adamw_w_in",
    )(recv, w, m, v)


def _sum_adamw_whole(recvs, ws, ms, vs):
    n = len(ws)

    def body(*refs):
        r_refs, w_refs, m_refs, v_refs = refs[:n], refs[n:2 * n], refs[2 * n:3 * n], refs[3 * n:4 * n]
        outs = refs[4 * n:]
        for a in range(n):
            g = _sum8(r_refs[a])
            d, nm, nv = _adamw(g, w_refs[a][...], m_refs[a][...], v_refs[a][...])
            outs[a][...] = g
            outs[n + a][...] = d
            outs[2 * n + a][...] = nm
            outs[3 * n + a][...] = nv

    shapes = [jax.ShapeDtypeStruct(w.shape, F32) for w in ws]
    res = pl.pallas_call(
        body,
        out_shape=shapes * 4,
        compiler_params=pltpu.CompilerParams(vmem_limit_bytes=48 * 2**20),
        name="sum_adamw_mats",
    )(*recvs, *ws, *ms, *vs)
    return res[:n], res[n:2 * n], res[2 * n:3 * n], res[3 * n:]


SMALL = ("g_pre", "b_gate", "g_q", "g_kv", "lb_logits", "g_hgrn", "g_post")
SMALL_SHAPE = dict(g_pre=(1, 1024), b_gate=(1, 2048), g_q=(1, 768), g_kv=(1, 256), lb_logits=(2, 512),
                   g_hgrn=(1, 64), g_post=(1, 1024))


def _vectors_sum(dg_pre, db_gate, dg_q, dg_kv, dlbl, dgh, dg_post, loss_vec):
    def body(gpre_ref, bg_ref, gq_ref, gkv_ref, lbl_ref, gh_ref, gpost_ref, loss_ref, out_ref, mine, got,
             send_sems, recv_sems):
        mine[...] = jnp.zeros_like(mine)
        mine[0:1, :] = gpre_ref[...]
        mine[1:2, :] = bg_ref[:, :1024]
        mine[2:3, :] = bg_ref[:, 1024:]
        mine[3:4, :Q_LORA] = gq_ref[...]
        mine[4:5, :KV_LORA] = gkv_ref[...]
        loss = (0.5 / D_MODEL) * jnp.sum(loss_ref[...], axis=-1, keepdims=True)
        mine[4:5, KV_LORA:] = jnp.broadcast_to(loss, (1, 1024 - KV_LORA))
        mine[5:6, :HG_WIDTH] = lbl_ref[0:1, :]
        mine[5:6, HG_WIDTH:] = lbl_ref[1:2, :]
        gh = gh_ref[...]
        fold = gh[:, :VDIM]
        for h in range(1, HEADS):
            fold = fold + gh[:, VDIM * h:VDIM * (h + 1)]
        mine[6:7, :VDIM] = fold
        mine[7:8, :] = gpost_ref[...]
        x, y, c = _my_place()
        me = 4 * x + 2 * y + c
        got[me] = mine[...]
        copies = [pltpu.make_async_remote_copy(
            src_ref=mine, dst_ref=got.at[me], send_sem=send_sems.at[k], recv_sem=recv_sems.at[k],
            device_id=_flip(k, x, y, c), device_id_type=MESH_ID) for k in range(N_DEV - 1)]
        _start_all([], copies)
        _wait_all([], copies)
        out_ref[...] = _sum8(got)

    return pl.pallas_call(
        body,
        out_shape=jax.ShapeDtypeStruct((8, 1024), F32),
        scratch_shapes=[pltpu.VMEM((8, 1024), F32), pltpu.VMEM((N_DEV, 8, 1024), F32),
                        pltpu.SemaphoreType.DMA((7,)), pltpu.SemaphoreType.DMA((7,))],
        name="vectors_sum",
    )(dg_pre, db_gate, dg_q, dg_kv, dlbl, dgh, dg_post, loss_vec)


def _vectors_adamw(g_sum, ws, ms, vs):
    n = len(SMALL)

    def body(g_ref, *refs):
        w_refs, m_refs, v_refs = refs[:n], refs[n:2 * n], refs[2 * n:3 * n]
        loss_ref, outs = refs[3 * n], refs[3 * n + 1:]
        g = g_ref[...]
        loss_ref[...] = g[4:5, KV_LORA:KV_LORA + 1]
        grads = (g[0:1, :], jnp.concatenate([g[1:2, :], g[2:3, :]], axis=1), g[3:4, :Q_LORA], g[4:5, :KV_LORA],
                 jnp.concatenate([g[5:6, :HG_WIDTH], g[5:6, HG_WIDTH:]], axis=0), g[6:7, :VDIM], g[7:8, :])
        for a in range(n):
            d, nm, nv = _adamw(grads[a], w_refs[a][...], m_refs[a][...], v_refs[a][...])
            outs[a][...] = grads[a]
            outs[n + a][...] = d
            outs[2 * n + a][...] = nm
            outs[3 * n + a][...] = nv

    shapes = [jax.ShapeDtypeStruct(SMALL_SHAPE[k], F32) for k in SMALL]
    res = pl.pallas_call(
        body,
        out_shape=[jax.ShapeDtypeStruct((1, 1), F32)] + shapes * 4,
        name="vectors_adamw",
    )(g_sum, *ws, *ms, *vs)
    return res[0], res[1:n + 1], res[n + 1:2 * n + 1], res[2 * n + 1:3 * n + 1], res[3 * n + 1:]


MATS = ("w_uq", "w_ukv", "w_branch_a", "w_branch_b", "w_out")
COL_SHARDED = dict(w_uq=False, w_ukv=True, w_branch_a=True, w_branch_b=True, w_out=False)
ORDER = ("g_pre", "w_in", "b_gate", "g_q", "w_uq", "g_kv", "w_ukv", "lb_logits", "g_hgrn",
         "w_branch_a", "w_branch_b", "w_out", "g_post")


def _to_slots(name, full):
    r, c = full.shape
    if COL_SHARDED[name]:
        return full.reshape(r, N_DEV, c // N_DEV).transpose(1, 0, 2)
    return full.reshape(N_DEV, r // N_DEV, c)


def _from_slots(name, slots):
    _, r, c = slots.shape
    if COL_SHARDED[name]:
        return slots.transpose(1, 0, 2).reshape(r, N_DEV * c)
    return slots.reshape(N_DEV * r, c)


def kernel(x, g_pre, w_in, b_gate, g_q, w_uq, g_kv, w_ukv, lb_logits, g_hgrn, w_branch_a, w_branch_b, w_out, g_post, loss_target, m_g_pre, m_w_in, m_b_gate, m_g_q, m_w_uq, m_g_kv, m_w_ukv, m_lb_logits, m_g_hgrn, m_w_branch_a, m_w_branch_b, m_w_out, m_g_post, v_g_pre, v_w_in, v_b_gate, v_g_q, v_w_uq, v_g_kv, v_w_ukv, v_lb_logits, v_g_hgrn, v_w_branch_a, v_w_branch_b, v_w_out, v_g_post):
    rows3 = lambda a: jnp.transpose(a, (2, 0, 1))
    w = dict(w_in=rows3(w_in), w_uq=w_uq[0], w_ukv=w_ukv[0], w_branch_a=w_branch_a[0], w_branch_b=w_branch_b[0],
             w_out=w_out[0], g_pre=g_pre, b_gate=b_gate, g_q=g_q, g_kv=g_kv, lb_logits=lb_logits, g_hgrn=g_hgrn,
             g_post=g_post)
    mom = dict(w_in=rows3(m_w_in), w_uq=m_w_uq[0], w_ukv=m_w_ukv[0], w_branch_a=m_w_branch_a[0],
               w_branch_b=m_w_branch_b[0], w_out=m_w_out[0], g_pre=m_g_pre, b_gate=m_b_gate, g_q=m_g_q, g_kv=m_g_kv,
               lb_logits=m_lb_logits, g_hgrn=m_g_hgrn, g_post=m_g_post)
    var = dict(w_in=rows3(v_w_in), w_uq=v_w_uq[0], w_ukv=v_w_ukv[0], w_branch_a=v_w_branch_a[0],
               w_branch_b=v_w_branch_b[0], w_out=v_w_out[0], g_pre=v_g_pre, b_gate=v_b_gate, g_q=v_g_q, g_kv=v_g_kv,
               lb_logits=v_lb_logits, g_hgrn=v_g_hgrn, g_post=v_g_post)

    (w_in_slots,) = _all_gather([w["w_in"].reshape(W_IN_SHARD, D_MODEL).astype(BF16)])
    dx, recv_in, recv, g_sum = _step(x[0], loss_target[0], w_in_slots, [w[n].astype(BF16) for n in MATS],
                                     g_pre, b_gate, g_q, g_kv, lb_logits, g_hgrn, g_post)

    g_in, d_in, m_in, v_in = _sum_adamw_w_in(recv_in, w["w_in"], mom["w_in"], var["w_in"])
    res = _sum_adamw_whole([recv[n] for n in MATS], *([t[n] for n in MATS] for t in (w, mom, var)))
    total, *vec = _vectors_adamw(g_sum, *([t[n] for n in SMALL] for t in (w, mom, var)))

    outs = []
    for mats, vecs, big in zip(res, vec, (g_in, d_in, m_in, v_in)):
        t = {**{n: a[None] for n, a in zip(MATS, mats)}, **dict(zip(SMALL, vecs)),
             "w_in": jnp.transpose(big, (1, 2, 0))}
        outs += [t[n] for n in ORDER]
    return (total.reshape(()), dx[None], *outs)
```

```python
import math

import jax
import jax.numpy as jnp
import numpy as np
from jax import lax
from jax.experimental import pallas as pl
from jax.experimental.pallas import tpu as pltpu

F32, BF16 = jnp.float32, jnp.bfloat16

D_MODEL = 1024
EPS = 1e-6
HEADS = 8
NOPE, ROPE, VDIM = 64, 32, 64
QK = NOPE + ROPE
Q_LORA, KV_LORA = 768, 256
ROPE_THETA = 10000.0
ATT_CHUNK_SHIFT = 6
HG_BLOCK = 32
HG_WIDTH = 512
D_IN = 5664
D_IN_PAD = 5760
W_IN_SHARD = D_IN // 8
N_DEV = 8
LANE = 128

ADAM_LR, ADAM_B1, ADAM_B2, ADAM_EPS, ADAM_WD, ADAM_STEP = 0.001, 0.9, 0.999, 1e-08, 0.01, 10

W_IN_SEGMENTS = ((3616, 5664, 0), (1056, 1568, 2048), (3104, 3616, 2560), (1568, 3104, 3072),
                 (0, 1024, 4608), (1024, 1056, 5696))
W_IN_ZERO = ((5632, 5696), (5728, 5760))

NT = (((1,), (1,)), ((), ()))
TN = (((0,), (0,)), ((), ()))
MESH_ID = pl.DeviceIdType.MESH


def _w_in_pieces():
    out = []
    for lo, hi, dst in W_IN_SEGMENTS:
        c = lo
        while c < hi:
            p = c // W_IN_SHARD
            e = min(hi, (p + 1) * W_IN_SHARD)
            out.append((p, c - p * W_IN_SHARD, e - p * W_IN_SHARD, dst + c - lo))
            c = e
    return out


def _params(sem, vmem_mb=48):
    return pltpu.CompilerParams(dimension_semantics=sem, vmem_limit_bytes=vmem_mb * 2**20)


def _dot(a, b):
    return jnp.dot(a, b, preferred_element_type=F32)


def _dotg(a, b, dims):
    return lax.dot_general(a, b, dims, preferred_element_type=F32)


def _split2(x):
    hi = x.astype(BF16)
    return hi, (x - hi.astype(F32)).astype(BF16)


def _sel_left(m01, x):
    hi, lo = _split2(x)
    return _dot(m01, hi) + _dot(m01, lo)


def _sel_right(x, m01):
    hi, lo = _split2(x)
    return _dot(hi, m01) + _dot(lo, m01)


def _hi_lo(x):
    hi = x.astype(BF16).astype(F32)
    return hi, x - hi


def _sigmoid(x):
    return 0.5 * jnp.tanh(0.5 * x) + 0.5


def _rope(x, c, s1, s2):
    return x * c + pltpu.roll(x, 112, 1) * s1 + pltpu.roll(x, 16, 1) * s2


def _unrope(d, c, s1, s2):
    return d * c + pltpu.roll(d * s1, 16, 1) + pltpu.roll(d * s2, 112, 1)


def _my_place():
    return lax.axis_index("x"), lax.axis_index("y"), lax.axis_index("c")


def _flip(k, x, y, c):
    fx, fy, fc = (k + 1) >> 2 & 1, (k + 1) >> 1 & 1, (k + 1) & 1
    return (1 - x if fx else x), (1 - y if fy else y), (1 - c if fc else c)


def _to_all_copies(s_refs, r_refs, sems, spread):
    send_sems, recv_sems, local_sems = sems
    x, y, c = _my_place()
    me = 4 * x + 2 * y + c
    src = (lambda a, p: s_refs[a]) if spread else (lambda a, p: s_refs[a].at[p])
    local = [pltpu.make_async_copy(src(a, me), r_refs[a].at[me], local_sems.at[a]) for a in range(len(s_refs))]
    remote = []
    for k in range(N_DEV - 1):
        px, py, pc = _flip(k, x, y, c)
        for a in range(len(s_refs)):
            remote.append(pltpu.make_async_remote_copy(
                src_ref=src(a, 4 * px + 2 * py + pc), dst_ref=r_refs[a].at[me],
                send_sem=send_sems.at[7 * a + k], recv_sem=recv_sems.at[7 * a + k],
                device_id=(px, py, pc), device_id_type=MESH_ID))
    return local, remote


def _to_chips_copies(s_refs, r_refs, sems):
    send_sems, recv_sems, local_sems = sems
    x, y, c = _my_place()
    me = 2 * x + y
    local = [pltpu.make_async_copy(s_refs[a].at[me], r_refs[a].at[me], local_sems.at[a]) for a in range(len(s_refs))]
    remote = []
    for k in range(3):
        px = 1 - x if (k + 1) >> 1 & 1 else x
        py = 1 - y if (k + 1) & 1 else y
        for a in range(len(s_refs)):
            remote.append(pltpu.make_async_remote_copy(
                src_ref=s_refs[a].at[2 * px + py], dst_ref=r_refs[a].at[me],
                send_sem=send_sems.at[3 * a + k], recv_sem=recv_sems.at[3 * a + k],
                device_id=(px, py, c), device_id_type=MESH_ID))
    return local, remote


def _start_all(local, remote):
    for cp in local + remote:
        cp.start()


def _wait_all(local, remote):
    for cp in remote:
        cp.wait_recv()
    for cp in remote:
        cp.wait_send()
    for cp in local:
        cp.wait()


def _copy_sems(n, peers):
    return [pltpu.SemaphoreType.DMA((peers * n,)), pltpu.SemaphoreType.DMA((peers * n,)),
            pltpu.SemaphoreType.DMA((n,))]


ANY = pl.BlockSpec(memory_space=pl.ANY)


def _assemble_w_in(slots):
    tc = 256
    pieces = _w_in_pieces()

    def body(s_ref, w_ref, wt_ref):
        for lo, hi in W_IN_ZERO:
            wt_ref[lo:hi, :] = jnp.zeros((hi - lo, tc), BF16)
        for p, lo, hi, dst in pieces:
            wt_ref[dst:dst + hi - lo, :] = s_ref[p, lo:hi, :]
        w_ref[...] = wt_ref[...].T

    return pl.pallas_call(
        body,
        grid=(D_MODEL // tc,),
        in_specs=[pl.BlockSpec((N_DEV, W_IN_SHARD, tc), lambda i: (0, 0, i))],
        out_specs=[pl.BlockSpec((tc, D_IN_PAD), lambda i: (i, 0)), pl.BlockSpec((D_IN_PAD, tc), lambda i: (0, i))],
        out_shape=[jax.ShapeDtypeStruct((D_MODEL, D_IN_PAD), BF16), jax.ShapeDtypeStruct((D_IN_PAD, D_MODEL), BF16)],
        compiler_params=_params(("arbitrary",)),
        name="assemble_w_in",
    )(slots)


def _scatter_w_in(dw):
    tc = 256
    pieces = _w_in_pieces()

    def body(d_ref, s_ref):
        dt = d_ref[...].T
        for p, lo, hi, dst in pieces:
            s_ref[p, lo:hi, :] = dt[dst:dst + hi - lo, :].astype(BF16)

    return pl.pallas_call(
        body,
        grid=(D_MODEL // tc,),
        in_specs=[pl.BlockSpec((tc, D_IN_PAD), lambda i: (i, 0))],
        out_specs=pl.BlockSpec((N_DEV, W_IN_SHARD, tc), lambda i: (0, 0, i)),
        out_shape=jax.ShapeDtypeStruct((N_DEV, W_IN_SHARD, D_MODEL), BF16),
        compiler_params=_params(("arbitrary",)),
        name="scatter_w_in",
    )(dw)


def _norm_proj(x, g_pre, w, shards):
    s, n = x.shape[0], w.shape[1]
    tm = 512
    ni, ns = s // tm, len(shards)

    def body(x_ref, g_ref, w_hbm, *rest):
        shard_refs, (proj_ref, ht_ref), got_refs = rest[:ns], rest[ns:ns + 2], rest[ns + 2:2 * ns + 2]
        w_ref, w_sem, sems = rest[2 * ns + 2], rest[2 * ns + 3], rest[2 * ns + 4:]
        i = pl.program_id(0)

        @pl.when(i == 0)
        def _():
            load = pltpu.make_async_copy(w_hbm, w_ref, w_sem)
            load.start()
            _start_all(*_to_all_copies(shard_refs, got_refs, sems, True))
            load.wait()

        xv = x_ref[...]
        r = lax.rsqrt(jnp.mean(xv * xv, axis=-1, keepdims=True) + EPS)
        h = (xv * r * g_ref[...]).astype(BF16)
        ht_ref[...] = h.T
        proj_ref[...] = _dot(h, w_ref[...])

        @pl.when(i == ni - 1)
        def _():
            _wait_all(*_to_all_copies(shard_refs, got_refs, sems, True))

    return pl.pallas_call(
        body,
        grid=(ni,),
        in_specs=[pl.BlockSpec((tm, D_MODEL), lambda i: (i, 0)), pl.BlockSpec((1, D_MODEL), lambda i: (0, 0)), ANY]
        + [ANY] * ns,
        out_specs=[pl.BlockSpec((tm, n), lambda i: (i, 0)), pl.BlockSpec((D_MODEL, tm), lambda i: (0, i))] + [ANY] * ns,
        out_shape=[jax.ShapeDtypeStruct((s, n), F32), jax.ShapeDtypeStruct((D_MODEL, s), BF16)]
        + [jax.ShapeDtypeStruct((N_DEV,) + b.shape, b.dtype) for b in shards],
        scratch_shapes=[pltpu.VMEM(w.shape, BF16), pltpu.SemaphoreType.DMA] + _copy_sems(ns, 7),
        compiler_params=_params(("arbitrary",), 56),
        name="norm_proj",
    )(x, g_pre, w, *shards)


GP_TN = 256
GP_COLS = 5888
GP_NT = GP_COLS // GP_TN


def _gp_tile_pieces():
    tiles = [[] for _ in range(GP_NT)]
    for p, lo, hi, dst in _w_in_pieces():
        while lo < hi:
            t = dst // GP_TN
            n = min(hi - lo, (t + 1) * GP_TN - dst)
            tiles[t].append((p, lo, lo + n, dst - t * GP_TN))
            lo, dst = lo + n, dst + n
    return tiles


def _gp_tables():
    pieces = _gp_tile_pieces()
    rank_of = {None: 0, 0: 1, 1: 2, 2: 2, 4: 3, 5: 3, 3: 4, 6: 5}
    order = np.zeros((N_DEV, GP_NT), np.int32)
    waits = np.zeros((N_DEV, GP_NT), np.int32)
    for me in range(N_DEV):
        x, y, c = me >> 2 & 1, me >> 1 & 1, me & 1
        chips = [(1 - x, y), (x, 1 - y), (1 - x, 1 - y)]

        def sem_of(p):
            px, py, pc = p >> 2 & 1, p >> 1 & 1, p & 1
            if (px, py) == (x, y):
                return None if pc == c else 0
            j = chips.index((px, py))
            return 1 + j if pc == c else 4 + j

        needs = [sorted({sem_of(p) for p, _, _, _ in tile} - {None}) for tile in pieces]
        ranks = [max([rank_of[k] for k in ks], default=0) for ks in needs]
        seq = sorted(range(GP_NT), key=lambda t: (ranks[t], t))
        seen = set()
        for step, t in enumerate(seq):
            order[me, step] = t
            new = [k for k in needs[t] if k not in seen]
            for k in new:
                waits[me, step] |= 1 << k
            seen.update(new)
        assert seen == set(range(7)), (me, seen)
    return order, waits


def _gather_proj(x, g_pre, w_blk, shards):
    s = x.shape[0]
    tx = 512
    ns = len(shards)
    tile_pieces = _gp_tile_pieces()
    order_np, waits_np = _gp_tables()
    xq, yq, cq = _my_place()
    me_out = 4 * xq + 2 * yq + cq
    order = lax.dynamic_index_in_dim(jnp.asarray(order_np), me_out, 0, keepdims=False)
    waits = lax.dynamic_index_in_dim(jnp.asarray(waits_np), me_out, 0, keepdims=False)

    def body(order_ref, waits_ref, x_hbm, g_ref, wblk_hbm, *rest):
        shard_refs, (proj_ref, wt_ref, ht_hbm), got_refs = rest[:ns], rest[ns:ns + 3], rest[ns + 3:2 * ns + 3]
        recv, h_ref, wtile, xbuf, htbuf = rest[2 * ns + 3:2 * ns + 8]
        send_sems, recv_sems, misc_sems = rest[2 * ns + 8:2 * ns + 11]
        sems = rest[2 * ns + 11:]
        t = pl.program_id(0)
        x_, y_, c = _my_place()
        sibling = (x_, y_, 1 - c)
        chips = [(1 - x_, y_), (x_, 1 - y_), (1 - x_, 1 - y_)]
        idx = lambda px, py, pc: 4 * px + 2 * py + pc
        me = idx(x_, y_, c)

        def copy(k, slot, to, src=None):
            return pltpu.make_async_remote_copy(
                src_ref=recv.at[slot] if src is None else src, dst_ref=recv.at[slot],
                send_sem=send_sems.at[k], recv_sem=recv_sems.at[k], device_id=to, device_id_type=MESH_ID)

        mine = pltpu.make_async_copy(wblk_hbm, recv.at[me], misc_sems.at[0])
        first = [copy(0, me, sibling, src=wblk_hbm)] + [copy(1 + j, me, (*ch, c), src=wblk_hbm)
                                                       for j, ch in enumerate(chips)]
        passed = [copy(4 + j, idx(*ch, c), sibling) for j, ch in enumerate(chips)]
        arrivals = ([copy(0, idx(x_, y_, 1 - c), sibling)] + [copy(1 + j, idx(*ch, c), sibling) for j, ch in enumerate(chips)]
                    + [copy(4 + j, idx(*ch, 1 - c), sibling) for j, ch in enumerate(chips)])

        @pl.when(t == 0)
        def _():
            mine.start()
            for cp in first:
                cp.start()
            _start_all(*_to_all_copies(shard_refs, got_refs, sems, True))

            def load(i):
                return pltpu.make_async_copy(x_hbm.at[pl.ds(i * tx, tx), :], xbuf.at[i & 1], misc_sems.at[1 + (i & 1)])

            def store(i):
                return pltpu.make_async_copy(htbuf.at[i & 1], ht_hbm.at[:, pl.ds(i * tx, tx)], misc_sems.at[3 + (i & 1)])

            load(0).start()
            for i in range(s // tx):
                if i + 1 < s // tx:
                    load(i + 1).start()
                load(i).wait()
                xv = xbuf[i & 1]
                r = lax.rsqrt(jnp.mean(xv * xv, axis=-1, keepdims=True) + EPS)
                h = (xv * r * g_ref[...]).astype(BF16)
                h_ref[i * tx:(i + 1) * tx, :] = h
                if i >= 2:
                    store(i - 2).wait()
                htbuf[i & 1] = h.T
                store(i).start()
            for i in range(max(s // tx - 2, 0), s // tx):
                store(i).wait()
            mine.wait()

        w = waits_ref[t]
        for k in range(7):
            @pl.when((w >> k) & 1 == 1)
            def _(k=k):
                arrivals[k].wait_recv()
                if 1 <= k <= 3:
                    passed[k - 1].start()

        tile = order_ref[t]
        for tt in range(GP_NT):
            @pl.when(tile == tt)
            def _(tt=tt):
                covered = sorted((d, d + hi - lo) for _, lo, hi, d in tile_pieces[tt])
                at = 0
                for lo_z, hi_z in covered + [(GP_TN, GP_TN)]:
                    if lo_z > at:
                        wtile[at:lo_z, :] = jnp.zeros((lo_z - at, D_MODEL), BF16)
                    at = max(at, hi_z)
                for p, lo, hi, d in tile_pieces[tt]:
                    wtile[d:d + hi - lo, :] = recv[p, lo:hi, :]

        wt = wtile[...]
        wt_ref[...] = wt
        proj_ref[...] = _dotg(h_ref[...], wt, NT)

        @pl.when(t == GP_NT - 1)
        def _():
            for cp in first + passed:
                cp.wait_send()
            _wait_all(*_to_all_copies(shard_refs, got_refs, sems, True))

    grid_spec = pltpu.PrefetchScalarGridSpec(
        num_scalar_prefetch=2,
        grid=(GP_NT,),
        in_specs=[ANY, pl.BlockSpec((1, D_MODEL), lambda t, o, w: (0, 0)), ANY] + [ANY] * ns,
        out_specs=[pl.BlockSpec((s, GP_TN), lambda t, o, w: (0, o[t])),
                   pl.BlockSpec((GP_TN, D_MODEL), lambda t, o, w: (o[t], 0)), ANY] + [ANY] * ns,
        scratch_shapes=[pltpu.VMEM((N_DEV, W_IN_SHARD, D_MODEL), BF16), pltpu.VMEM((s, D_MODEL), BF16),
                        pltpu.VMEM((GP_TN, D_MODEL), BF16), pltpu.VMEM((2, tx, D_MODEL), F32),
                        pltpu.VMEM((2, D_MODEL, tx), BF16),
                        pltpu.SemaphoreType.DMA((7,)), pltpu.SemaphoreType.DMA((7,)), pltpu.SemaphoreType.DMA((5,))]
        + _copy_sems(ns, 7),
    )
    return pl.pallas_call(
        body,
        grid_spec=grid_spec,
        out_shape=[jax.ShapeDtypeStruct((s, GP_COLS), F32), jax.ShapeDtypeStruct((GP_COLS, D_MODEL), BF16),
                   jax.ShapeDtypeStruct((D_MODEL, s), BF16)]
        + [jax.ShapeDtypeStruct((N_DEV,) + b.shape, b.dtype) for b in shards],
        compiler_params=_params(("arbitrary",), 56),
        name="gather_proj",
    )(order, waits, x, g_pre, w_blk, *shards)


def _mla_prep(proj, g_q, g_kv, w_uq_p, w_kv_p, rc, rs1, rs2):
    s = proj.shape[0]
    tm = 256
    scale = 1.0 / math.sqrt(QK)

    def body(cq_ref, ckv_ref, kpe_ref, gq_ref, gkv_ref, wuq_ref, wkv_ref, c_ref, s1_ref, s2_ref,
             qr_ref, kr_ref, v_ref, cqt_ref, ckvt_ref):
        cq = cq_ref[...]
        r = lax.rsqrt(jnp.mean(cq * cq, axis=-1, keepdims=True) + EPS)
        cqn = (cq * r * gq_ref[...]).astype(BF16)
        cqt_ref[...] = cqn.T
        q = _dot(cqn, wuq_ref[...])
        ckv = ckv_ref[...]
        r = lax.rsqrt(jnp.mean(ckv * ckv, axis=-1, keepdims=True) + EPS)
        ckvn = (ckv * r * gkv_ref[...]).astype(BF16)
        ckvt_ref[...] = ckvn.T
        kv = _dot(ckvn, wkv_ref[...])
        c, s1, s2 = c_ref[...], s1_ref[...], s2_ref[...]
        lane = lax.broadcasted_iota(jnp.int32, (tm, LANE), 1)
        kpe = _rope(kpe_ref[...], c, s1, s2) + jnp.where((lane == QK) | (lane == QK + 1), 1.0, 0.0)
        vone = jnp.where((lane == VDIM) | (lane == VDIM + 1), 1.0, 0.0)
        for h in range(HEADS):
            sl = slice(LANE * h, LANE * (h + 1))
            qr_ref[:, sl] = (_rope(q[:, sl], c, s1, s2) * scale).astype(BF16)
            kr_ref[:, sl] = (kv[:, sl] + kpe).astype(BF16)
            v_ref[:, sl] = (kv[:, HEADS * LANE + LANE * h:HEADS * LANE + LANE * (h + 1)] + vone).astype(BF16)

    row = lambda w, j: pl.BlockSpec((tm, w), lambda i: (i, j))
    col = lambda w: pl.BlockSpec((w, tm), lambda i: (0, i))
    full = lambda a: pl.BlockSpec(a.shape, lambda i: (0, 0))
    return pl.pallas_call(
        body,
        grid=(s // tm,),
        in_specs=[row(768, 6), row(256, 21), row(128, 44), full(g_q), full(g_kv), full(w_uq_p), full(w_kv_p),
                  row(128, 0), row(128, 0), row(128, 0)],
        out_specs=[row(1024, 0), row(1024, 0), row(1024, 0), col(768), col(256)],
        out_shape=[jax.ShapeDtypeStruct((s, 1024), BF16), jax.ShapeDtypeStruct((s, 1024), BF16),
                   jax.ShapeDtypeStruct((s, 1024), BF16), jax.ShapeDtypeStruct((768, s), BF16),
                   jax.ShapeDtypeStruct((256, s), BF16)],
        compiler_params=_params(("arbitrary",)),
        name="mla_prep",
    )(proj, proj, proj, g_q, g_kv, w_uq_p, w_kv_p, rc, rs1, rs2)


ATT_T = 512
ATT_FWD_HEADS = 4


def _chunk_mask(transposed):
    r = lax.broadcasted_iota(jnp.int32, (ATT_T, ATT_T), 0) >> ATT_CHUNK_SHIFT
    c = lax.broadcasted_iota(jnp.int32, (ATT_T, ATT_T), 1) >> ATT_CHUNK_SHIFT
    return (r <= c) if transposed else (c <= r)


def _attn_fwd(qr, kr, vp, shards):
    s = qr.shape[0]
    t = ATT_T
    g = ATT_FWD_HEADS
    ns = len(shards)

    def body(q_ref, k_ref, v_ref, *rest):
        shard_refs, (o_ref, qa_ref), got_refs = rest[:ns], rest[ns:ns + 2], rest[ns + 2:2 * ns + 2]
        sc_ref, sems = rest[2 * ns + 2], rest[2 * ns + 3:]
        qi = pl.program_id(1)

        @pl.when((pl.program_id(0) == 0) & (qi == 0))
        def _():
            _start_all(*_to_all_copies(shard_refs, got_refs, sems, True))
        lane = lax.broadcasted_iota(jnp.int32, (t, LANE), 1)
        sls = [slice(LANE * a, LANE * (a + 1)) for a in range(g)]
        qs = [q_ref[:, sl] for sl in sls]

        def scores(j):
            rows = pl.ds(pl.multiple_of(j * t, t), t)
            for a in range(g):
                sc_ref[j & 1, a] = _dotg(qs[a], k_ref[rows, sls[a]], NT)

        def step(j, carry, masked):
            rows = pl.ds(pl.multiple_of(j * t, t), t)
            out = []
            for a in range(g):
                m, acc = carry[a]
                sc = sc_ref[j & 1, a]
                if masked:
                    sc = jnp.where(_chunk_mask(False), sc, -1e30)
                m_new = jnp.maximum(m, jnp.max(sc, axis=-1, keepdims=True))
                p = jnp.exp(sc - m_new).astype(BF16)
                acc = jnp.exp(m - m_new) * acc + _dot(p, v_ref[rows, sls[a]])
                out.append((m_new, acc))
            return tuple(out)

        def loop(j, carry):
            carry = step(j, carry, False)
            scores(j + 1)
            return carry

        init = tuple((jnp.full((t, 1), -1e30, F32), jnp.zeros((t, LANE), F32)) for _ in range(g))
        scores(0)
        carry = lax.fori_loop(0, qi, loop, init)
        carry = step(qi, carry, True)
        outs = []
        for a in range(g):
            m, acc = carry[a]
            l = acc[:, VDIM:VDIM + 1]
            outs.append(acc / l)
            hi, lo_part = _hi_lo(-(m + jnp.log(l)))
            qa = jnp.where(lane == QK, hi, jnp.where(lane == QK + 1, lo_part, qs[a].astype(F32)))
            qa_ref[:, sls[a]] = qa.astype(BF16)
        for p in range(g // 2):
            o_ref[:, LANE * p:LANE * (p + 1)] = jnp.where(lane < VDIM, outs[2 * p], pltpu.roll(outs[2 * p + 1], VDIM, 1))

        @pl.when((pl.program_id(0) == HEADS // g - 1) & (qi == s // t - 1))
        def _():
            _wait_all(*_to_all_copies(shard_refs, got_refs, sems, True))

    return pl.pallas_call(
        body,
        grid=(HEADS // g, s // t),
        in_specs=[
            pl.BlockSpec((t, g * LANE), lambda h, i: (i, h)),
            pl.BlockSpec((s, g * LANE), lambda h, i: (0, h)),
            pl.BlockSpec((s, g * LANE), lambda h, i: (0, h)),
        ] + [ANY] * ns,
        out_specs=[
            pl.BlockSpec((t, g * VDIM), lambda h, i: (i, h)),
            pl.BlockSpec((t, g * LANE), lambda h, i: (i, h)),
        ] + [ANY] * ns,
        out_shape=[jax.ShapeDtypeStruct((s, 512), F32), jax.ShapeDtypeStruct((s, 1024), BF16)]
        + [jax.ShapeDtypeStruct((N_DEV,) + b.shape, b.dtype) for b in shards],
        scratch_shapes=[pltpu.VMEM((2, g, t, t), F32)] + _copy_sems(ns, 7),
        compiler_params=_params(("arbitrary", "arbitrary")),
        name="attn_fwd",
    )(qr, kr, vp, *shards)


def _attn_bwd(qa, kr, vp, dop, sends):
    s = qa.shape[0]
    t = ATT_T
    nq = s // t
    ns = len(sends)

    def body(q_ref, k_ref, v_ref, do_ref, *rest):
        send_refs, (dq_out, dk_out, dv_out) = rest[:ns], rest[ns:ns + 3]
        recv_refs = rest[ns + 3:2 * ns + 3]
        (dq_ref, dk_ref, dv_ref), sems = rest[2 * ns + 3:2 * ns + 6], rest[2 * ns + 6:]
        j = pl.program_id(1)
        sls = [slice(LANE * a, LANE * (a + 1)) for a in range(2)]

        @pl.when((pl.program_id(0) == 0) & (j == 0))
        def _():
            _start_all(*_to_all_copies(send_refs, recv_refs, sems, False))

        @pl.when(j == 0)
        def _():
            dq_ref[...] = jnp.zeros_like(dq_ref)

        dk_ref[...] = jnp.zeros_like(dk_ref)
        dv_ref[...] = jnp.zeros_like(dv_ref)
        ks = [k_ref[:, sl] for sl in sls]
        vs = [v_ref[:, sl] for sl in sls]

        def step(i, masked):
            rows = pl.ds(pl.multiple_of(i * t, t), t)
            for a in range(2):
                q = q_ref[rows, sls[a]]
                do = do_ref[rows, sls[a]]
                sc = _dotg(ks[a], q, NT)
                if masked:
                    sc = jnp.where(_chunk_mask(True), sc, -1e30)
                p = jnp.exp(sc)
                ds = (p * _dotg(vs[a], do, NT)).astype(BF16)
                dv_ref[:, sls[a]] += _dot(p.astype(BF16), do)
                dk_ref[:, sls[a]] += _dot(ds, q)
                dq_ref[rows, sls[a]] += _dotg(ds, ks[a], TN)

        step(j, True)

        def loop(i, c):
            step(i, False)
            return c

        lax.fori_loop(j + 1, nq, loop, 0)
        dk_out[...] = dk_ref[...].astype(BF16)
        dv_out[...] = dv_ref[...].astype(BF16)

        @pl.when(j == nq - 1)
        def _():
            dq_out[...] = dq_ref[...].astype(BF16)

        @pl.when((pl.program_id(0) == HEADS // 2 - 1) & (j == nq - 1))
        def _():
            _wait_all(*_to_all_copies(send_refs, recv_refs, sems, False))

    blk = pl.BlockSpec((t, 2 * LANE), lambda h, j: (j, h))
    whole = pl.BlockSpec((s, 2 * LANE), lambda h, j: (0, h))
    out = jax.ShapeDtypeStruct((s, 1024), BF16)
    return pl.pallas_call(
        body,
        grid=(HEADS // 2, nq),
        in_specs=[whole, blk, blk, whole] + [ANY] * ns,
        out_specs=[whole, blk, blk] + [ANY] * ns,
        out_shape=[out, out, out] + [jax.ShapeDtypeStruct(a.shape, a.dtype) for a in sends],
        scratch_shapes=[pltpu.VMEM((s, 2 * LANE), F32), pltpu.VMEM((t, 2 * LANE), F32),
                        pltpu.VMEM((t, 2 * LANE), F32)] + _copy_sems(ns, 7),
        compiler_params=_params(("arbitrary", "arbitrary")),
        name="attn_bwd",
    )(qa, kr, vp, dop, *sends)


HG_T = 256
HG_NC = HG_T // HG_BLOCK
HG_G = 4
GW = 64 * HG_G


def _hg_consts():
    r = jnp.arange(HG_T)[:, None]
    c = jnp.arange(HG_T)[None, :]
    same = (r // HG_BLOCK) == (c // HG_BLOCK)
    mcum = (same & (c <= r)).astype(BF16)
    mrev = (same & (c >= r)).astype(BF16)
    msum = same.astype(BF16)
    a = jnp.arange(GW) // 64
    bd = (a[:, None] == a[None, :]).astype(F32)
    return mcum, mrev, msum, bd


def _stack_heads(xg, head):
    return jnp.concatenate([jnp.where(head == h, xg, 0.0) for h in range(HG_G)], axis=0)


def _unstack_heads(r, head, t):
    out = r[(HG_G - 1) * t:]
    for h in range(HG_G - 2, -1, -1):
        out = jnp.where(head == h, r[h * t:(h + 1) * t], out)
    return out


def _compact_state(st):
    out = st[:64]
    for h in range(1, HG_G):
        out = out + st[64 * h:64 * (h + 1)]
    return out


def _expand_state(cs, head64):
    return jnp.concatenate([jnp.where(head64 == h, cs, 0.0) for h in range(HG_G)], axis=0)


def _hg_pre(hq, hf, lbl, mcum, msum):
    lb = _sigmoid(lbl[0:1, :] - lbl[1:2, :])
    sig = _sigmoid(hf)
    f = lb + (1.0 - lb) * sig
    lf = jnp.log(f)
    b = _sel_left(mcum, lf)
    big_l = _sel_left(msum, lf)
    k = 1.0 - f
    qd = hq * jnp.exp(b)
    ki = k * jnp.exp(-b)
    ke = k * jnp.exp(big_l - b)
    return lb, sig, f, b, big_l, qd, ki, ke


def _hgrn_fwd(proj, lbl):
    s = proj.shape[0]
    t = HG_T
    mcum, _, msum, bd = _hg_consts()

    def body(hq_ref, hf_ref, hi_ref, lbl_ref, mcum_ref, msum_ref, bd_ref, o_ref, sp_ref, st_ref):
        @pl.when(pl.program_id(0) == 0)
        def _():
            st_ref[...] = jnp.zeros_like(st_ref)

        mc = mcum_ref[...]
        _, _, _, _, big_l, qd, ki, ke = _hg_pre(hq_ref[...], hf_ref[...], lbl_ref[...], mc, msum_ref[...])
        el = jnp.exp(big_l)
        hi = hi_ref[...]
        head = lax.broadcasted_iota(jnp.int32, (t, GW), 1) >> 6
        mask = jnp.concatenate([mc] * HG_G, axis=0) > 0.5
        for p in range(HEADS // HG_G):
            sl = slice(GW * p, GW * (p + 1))
            vp = hi[:, sl].astype(BF16)
            qs = _stack_heads(qd[:, sl], head).astype(BF16)
            a = jnp.where(mask, _dotg(qs, ki[:, sl].astype(BF16), NT), 0.0)
            o_intra = _unstack_heads(_dot(a.astype(BF16), vp), head, t)
            qb = qd[:, sl].astype(BF16)
            kb = ke[:, sl].astype(BF16)
            st = st_ref[p]
            for c in range(HG_NC):
                rows = slice(HG_BLOCK * c, HG_BLOCK * (c + 1))
                sp_ref[c, :, sl] = _compact_state(st)
                o_ref[rows, sl] = o_intra[rows] + _dotg(qb[rows], st.astype(BF16), NT)
                u = _dotg(vp[rows], kb[rows], TN) * bd_ref[...]
                st = st * el[HG_BLOCK * c:HG_BLOCK * c + 1, sl] + u
            st_ref[p] = st

    row = lambda j: pl.BlockSpec((t, HG_WIDTH), lambda i: (i, j))
    full = lambda a: pl.BlockSpec(a.shape, lambda i: (0, 0))
    return pl.pallas_call(
        body,
        grid=(s // t,),
        in_specs=[row(6), row(7), row(8), full(lbl), full(mcum), full(msum), full(bd)],
        out_specs=[row(0), pl.BlockSpec((HG_NC, 64, HG_WIDTH), lambda i: (i, 0, 0))],
        out_shape=[jax.ShapeDtypeStruct((s, HG_WIDTH), F32),
                   jax.ShapeDtypeStruct((s // HG_BLOCK, 64, HG_WIDTH), F32)],
        scratch_shapes=[pltpu.VMEM((HEADS // HG_G, GW, GW), F32)],
        compiler_params=_params(("arbitrary",)),
        name="hgrn_fwd",
    )(proj, proj, proj, lbl, mcum, msum, bd)


def _hgrn_bwd(proj, lbl, do, sprev, dproj):
    s = proj.shape[0]
    t = HG_T
    nt = s // t
    mcum, mrev, msum, bd = _hg_consts()

    def body(hq_ref, hf_ref, hi_ref, lbl_ref, do_ref, sp_ref, mcum_ref, mrev_ref, msum_ref, bd_ref,
             dproj_in, dh_ref, dlbl_ref, g_ref):
        del dproj_in

        @pl.when(pl.program_id(0) == 0)
        def _():
            g_ref[...] = jnp.zeros_like(g_ref)
            dlbl_ref[...] = jnp.zeros_like(dlbl_ref)

        mc = mcum_ref[...]
        lb, sig, f, b, big_l, qd, ki, ke = _hg_pre(hq_ref[...], hf_ref[...], lbl_ref[...], mc, msum_ref[...])
        el = jnp.exp(big_l)
        hi = hi_ref[...]
        dov = do_ref[...]
        head = lax.broadcasted_iota(jnp.int32, (t, GW), 1) >> 6
        head64 = lax.broadcasted_iota(jnp.int32, (64, GW), 1) >> 6
        mask = jnp.concatenate([mc] * HG_G, axis=0) > 0.5
        dqd_parts, dke_parts, dv_parts, del_parts, dki_parts = [], [], [], [], []
        for p in range(HEADS // HG_G):
            sl = slice(GW * p, GW * (p + 1))
            vp = hi[:, sl].astype(BF16)
            qs = _stack_heads(qd[:, sl], head).astype(BF16)
            kip = ki[:, sl].astype(BF16)
            dos = _stack_heads(dov[:, sl], head).astype(BF16)
            a = jnp.where(mask, _dotg(qs, kip, NT), 0.0).astype(BF16)
            da = jnp.where(mask, _dotg(dos, vp, NT), 0.0).astype(BF16)
            r = _dot(da, kip)
            dki_parts.append(_dotg(da, qs, TN))
            qb = qd[:, sl].astype(BF16)
            kb = ke[:, sl].astype(BF16)
            dob = dov[:, sl].astype(BF16)
            g = g_ref[p]
            dqd_c, dv_c, dke_c, del_c = [], [], [], []
            for c in range(HG_NC - 1, -1, -1):
                rows = slice(HG_BLOCK * c, HG_BLOCK * (c + 1))
                gb = g.astype(BF16)
                st = _expand_state(sp_ref[c, :, sl], head64)
                dqd_c.append(_dot(dob[rows], st.astype(BF16)))
                dv_c.append(_dotg(kb[rows], gb, NT))
                dke_c.append(_dot(vp[rows], gb))
                del_c.append(jnp.broadcast_to(jnp.sum(g * st, axis=0, keepdims=True), (HG_BLOCK, GW)))
                g = g * el[HG_BLOCK * c:HG_BLOCK * c + 1, sl] + _dotg(dob[rows], qb[rows], TN) * bd_ref[...]
            g_ref[p] = g
            up = lambda parts: jnp.concatenate(parts[::-1], axis=0)
            dqd_parts.append(_unstack_heads(r, head, t) + up(dqd_c))
            dv_parts.append(_dotg(a, dos, TN) + up(dv_c))
            dke_parts.append(up(dke_c))
            del_parts.append(up(del_c))
        wide = lambda parts: jnp.concatenate(parts, axis=1)
        dqd, dke, dki, dvv, del_rows = wide(dqd_parts), wide(dke_parts), wide(dki_parts), wide(dv_parts), wide(del_parts)
        dh_ref[:, :HG_WIDTH] = (dqd * jnp.exp(b)).astype(BF16)
        dh_ref[:, 2 * HG_WIDTH:] = dvv.astype(BF16)
        dke_ke = dke * ke
        db = dqd * qd - dki * ki - dke_ke
        dl_rows = _sel_left(msum_ref[...], dke_ke) + del_rows * el
        is_last = (lax.broadcasted_iota(jnp.int32, (t, HG_WIDTH), 0) & (HG_BLOCK - 1)) == HG_BLOCK - 1
        db = db + jnp.where(is_last, dl_rows, 0.0)
        dlf = _sel_left(mrev_ref[...], db)
        dk = dki * jnp.exp(-b) + dke * jnp.exp(big_l - b)
        df = dlf / f - dk
        dh_ref[:, HG_WIDTH:2 * HG_WIDTH] = (df * (1.0 - lb) * sig * (1.0 - sig)).astype(BF16)
        dlb = jnp.sum(df * (1.0 - sig), axis=0, keepdims=True) * lb * (1.0 - lb)
        dlbl_ref[0:1, :] += dlb
        dlbl_ref[1:2, :] -= dlb

    rrow = lambda j: pl.BlockSpec((t, HG_WIDTH), lambda i: (nt - 1 - i, j))
    full = lambda a: pl.BlockSpec(a.shape, lambda i: (0, 0))
    return pl.pallas_call(
        body,
        grid=(nt,),
        in_specs=[rrow(6), rrow(7), rrow(8), full(lbl), rrow(0),
                  pl.BlockSpec((HG_NC, 64, HG_WIDTH), lambda i: (nt - 1 - i, 0, 0)),
                  full(mcum), full(mrev), full(msum), full(bd), pl.BlockSpec(memory_space=pl.ANY)],
        out_specs=[pl.BlockSpec((t, 3 * HG_WIDTH), lambda i: (nt - 1 - i, 2)),
                   pl.BlockSpec((2, HG_WIDTH), lambda i: (0, 0))],
        out_shape=[jax.ShapeDtypeStruct(dproj.shape, BF16), jax.ShapeDtypeStruct((2, HG_WIDTH), F32)],
        input_output_aliases={10: 0},
        scratch_shapes=[pltpu.VMEM((HEADS // HG_G, GW, GW), F32)],
        compiler_params=_params(("arbitrary",)),
        name="hgrn_bwd",
    )(proj, proj, proj, lbl, do, sprev, mcum, mrev, msum, bd, dproj)


def _tail(x, tgt, proj, attn, o, w_a, w_b, w_out, w_at, w_bt, w_outt, b_gate, g_post, gh):
    s = x.shape[0]
    tm = 256
    ones64 = (jnp.arange(HG_WIDTH)[:, None] // 64 == jnp.arange(HG_WIDTH)[None, :] // 64).astype(BF16)
    weights = (w_a, w_b, w_out, w_at, w_bt, w_outt)

    def body(x_ref, t_ref, ml_ref, ga_ref, gb_ref, at_ref, o_ref, *rest):
        w_hbm, (bg_ref, gp_ref, gh_ref, ones_ref) = rest[:6], rest[6:10]
        (dout_ref, dpj_ref, dop_ref, do_ref, mt_ref, dy_ref, yat_ref, dya_ref, ybt_ref, dyb_ref,
         loss_ref, dgp_ref, dbg_ref, dgh_ref) = rest[10:24]
        (wa_ref, wb_ref, wo_ref, wat_ref, wbt_ref, wot_ref), w_sem = rest[24:30], rest[30]

        @pl.when(pl.program_id(0) == 0)
        def _():
            loads = [pltpu.make_async_copy(src, dst, w_sem.at[k])
                     for k, (src, dst) in enumerate(zip(w_hbm, rest[24:30]))]
            _start_all(loads, [])
            loss_ref[...] = jnp.zeros_like(loss_ref)
            dgp_ref[...] = jnp.zeros_like(dgp_ref)
            dbg_ref[...] = jnp.zeros_like(dbg_ref)
            dgh_ref[...] = jnp.zeros_like(dgh_ref)
            _wait_all(loads, [])

        ones = ones_ref[...]
        gate_a = ga_ref[...]
        sa = _sigmoid(gate_a)
        silu_a = gate_a * sa
        attn_v = at_ref[...]
        ya_in = attn_v * silu_a
        ov = o_ref[...]
        ro = lax.rsqrt(_sel_right(ov * ov, ones) * (1.0 / 64.0) + EPS)
        ohat = ov * ro
        ghv = gh_ref[...]
        on = ohat * ghv
        gate_b = gb_ref[...]
        sb = _sigmoid(gate_b)
        silu_b = gate_b * sb
        yb_in = on * silu_b
        ya_bf = ya_in.astype(BF16)
        yb_bf = yb_in.astype(BF16)
        yat_ref[...] = ya_bf.T
        ybt_ref[...] = yb_bf.T
        y_a = _dot(ya_bf, wa_ref[...])
        y_b = _dot(yb_bf, wb_ref[...])
        gts = _sigmoid(ml_ref[...] + bg_ref[...])
        g_a = gts[:, :D_MODEL]
        g_b = gts[:, D_MODEL:]
        m_bf = (g_a * y_a + g_b * y_b).astype(BF16)
        mt_ref[...] = m_bf.T
        y = _dot(m_bf, wo_ref[...])
        r1 = lax.rsqrt(jnp.mean(y * y, axis=-1, keepdims=True) + EPS)
        yn = y * r1
        gp = gp_ref[...]
        e = x_ref[...] + yn * gp - t_ref[...]
        loss_ref[...] += jnp.sum(e * e, axis=0, keepdims=True)
        dout = e * (1.0 / D_MODEL)
        dout_ref[...] = dout
        dgp_ref[...] += jnp.sum(dout * yn, axis=0, keepdims=True)
        dyn = dout * gp
        dy = r1 * (dyn - yn * jnp.mean(dyn * yn, axis=-1, keepdims=True))
        dy_bf = dy.astype(BF16)
        dy_ref[...] = dy_bf
        dm = _dot(dy_bf, wot_ref[...])
        dml_a = dm * y_a * g_a * (1.0 - g_a)
        dml_b = dm * y_b * g_b * (1.0 - g_b)
        dpj_ref[:, :D_MODEL] = dml_a.astype(BF16)
        dpj_ref[:, D_MODEL:2 * D_MODEL] = dml_b.astype(BF16)
        dbg_ref[:, :D_MODEL] += jnp.sum(dml_a, axis=0, keepdims=True)
        dbg_ref[:, D_MODEL:] += jnp.sum(dml_b, axis=0, keepdims=True)
        dya_bf = (dm * g_a).astype(BF16)
        dyb_bf = (dm * g_b).astype(BF16)
        dya_ref[...] = dya_bf
        dyb_ref[...] = dyb_bf
        dya_in = _dot(dya_bf, wat_ref[...])
        dyb_in = _dot(dyb_bf, wbt_ref[...])
        dattn = dya_in * silu_a
        delta = _sel_right(dattn * attn_v, ones)
        lane = lax.broadcasted_iota(jnp.int32, (tm, LANE), 1)
        for p in range(HEADS // 2):
            sl = slice(LANE * p, LANE * (p + 1))
            xs = (dattn[:, sl], pltpu.roll(dattn[:, sl], VDIM, 1))
            nds = (-pltpu.roll(delta[:, sl], VDIM, 1), -delta[:, sl])
            for a in range(2):
                hi, lo_part = _hi_lo(nds[a])
                blk = jnp.where(lane < VDIM, xs[a], jnp.where(lane == VDIM, hi, jnp.where(lane == VDIM + 1, lo_part, 0.0)))
                dop_ref[:, LANE * (2 * p + a):LANE * (2 * p + a + 1)] = blk.astype(BF16)
        dpj_ref[:, 2 * D_MODEL:2 * D_MODEL + HG_WIDTH] = (
            dya_in * attn_v * (sa * (1.0 + gate_a * (1.0 - sa)))).astype(BF16)
        don = dyb_in * silu_b
        dpj_ref[:, 2 * D_MODEL + HG_WIDTH:] = (dyb_in * on * (sb * (1.0 + gate_b * (1.0 - sb)))).astype(BF16)
        dgh_ref[...] += jnp.sum(don * ohat, axis=0, keepdims=True)
        dohat = don * ghv
        do_ref[...] = ro * (dohat - ohat * (_sel_right(dohat * ohat, ones) * (1.0 / 64.0)))

    row = lambda w, j: pl.BlockSpec((tm, w), lambda i: (i, j))
    col = lambda w: pl.BlockSpec((w, tm), lambda i: (0, i))
    full = lambda a: pl.BlockSpec(a.shape, lambda i: (0, 0))
    acc = lambda w: pl.BlockSpec((1, w), lambda i: (0, 0))
    sds = lambda w, dt: jax.ShapeDtypeStruct((s, w), dt)
    sdt = lambda w: jax.ShapeDtypeStruct((w, s), BF16)
    return pl.pallas_call(
        body,
        grid=(s // tm,),
        in_specs=[row(1024, 0), row(1024, 0), row(2048, 0), row(512, 4), row(512, 5), row(512, 0), row(512, 0)]
        + [ANY] * 6 + [full(b_gate), full(g_post), full(gh), full(ones64)],
        out_specs=[row(1024, 0), row(3072, 0), row(1024, 0), row(512, 0),
                   col(1024), row(1024, 0), col(512), row(1024, 0), col(512), row(1024, 0),
                   acc(1024), acc(1024), acc(2048), acc(512)],
        out_shape=[sds(1024, F32), sds(D_IN_PAD, BF16), sds(1024, BF16), sds(512, F32),
                   sdt(1024), sds(1024, BF16), sdt(512), sds(1024, BF16), sdt(512), sds(1024, BF16),
                   jax.ShapeDtypeStruct((1, 1024), F32), jax.ShapeDtypeStruct((1, 1024), F32),
                   jax.ShapeDtypeStruct((1, 2048), F32), jax.ShapeDtypeStruct((1, 512), F32)],
        scratch_shapes=[pltpu.VMEM(a.shape, BF16) for a in weights] + [pltpu.SemaphoreType.DMA((6,))],
        compiler_params=_params(("arbitrary",), 56),
        name="tail",
    )(x, tgt, proj, proj, proj, attn, o, *weights, b_gate, g_post, gh, ones64)


def _mla_bwd(proj, dqr, dkr, dv, g_q, g_kv, w_uq_pt, w_kv_pt, rc, rs1, rs2, dproj):
    s = proj.shape[0]
    tm = 256
    scale = 1.0 / math.sqrt(QK)

    def body(cq_ref, ckv_ref, dqr_ref, dkr_ref, dv_ref, gq_ref, gkv_ref, wuqt_ref, wkvt_ref, c_ref, s1_ref, s2_ref,
             dproj_in, dqf_ref, dkvf_ref, dc_ref, dgq_ref, dgkv_ref):
        del dproj_in

        @pl.when(pl.program_id(0) == 0)
        def _():
            dgq_ref[...] = jnp.zeros_like(dgq_ref)
            dgkv_ref[...] = jnp.zeros_like(dgkv_ref)

        c, s1, s2 = c_ref[...], s1_ref[...], s2_ref[...]
        lane = lax.broadcasted_iota(jnp.int32, (tm, LANE), 1)
        ksum = jnp.zeros((tm, LANE), F32)
        for h in range(HEADS):
            sl = slice(LANE * h, LANE * (h + 1))
            dqf_ref[:, sl] = (_unrope(dqr_ref[:, sl], c, s1, s2) * scale).astype(BF16)
            dkh = dkr_ref[:, sl]
            ksum = ksum + dkh
            dkvf_ref[:, sl] = jnp.where(lane < NOPE, dkh, 0.0).astype(BF16)
            dkvf_ref[:, HEADS * LANE + LANE * h:HEADS * LANE + LANE * (h + 1)] = jnp.where(
                lane < VDIM, dv_ref[:, sl], 0.0).astype(BF16)
        dkpe = _unrope(ksum, c, s1, s2)
        dc_ref[:, Q_LORA + KV_LORA:] = jnp.where((lane >= NOPE) & (lane < QK), dkpe, 0.0).astype(BF16)
        dcqn = _dot(dqf_ref[...], wuqt_ref[...])
        dckvn = _dot(dkvf_ref[...], wkvt_ref[...])
        for x_ref, g_ref, dn, cols, dg_ref in ((cq_ref, gq_ref, dcqn, slice(0, Q_LORA), dgq_ref),
                                               (ckv_ref, gkv_ref, dckvn, slice(Q_LORA, Q_LORA + KV_LORA), dgkv_ref)):
            xv = x_ref[...]
            r = lax.rsqrt(jnp.mean(xv * xv, axis=-1, keepdims=True) + EPS)
            xh = xv * r
            dg_ref[...] += jnp.sum(dn * xh, axis=0, keepdims=True)
            dh = dn * g_ref[...]
            dc_ref[:, cols] = (r * (dh - xh * jnp.mean(dh * xh, axis=-1, keepdims=True))).astype(BF16)

    row = lambda w, j: pl.BlockSpec((tm, w), lambda i: (i, j))
    full = lambda a: pl.BlockSpec(a.shape, lambda i: (0, 0))
    acc = lambda w: pl.BlockSpec((1, w), lambda i: (0, 0))
    sds = lambda w, dt: jax.ShapeDtypeStruct((s, w), dt)
    return pl.pallas_call(
        body,
        grid=(s // tm,),
        in_specs=[row(768, 6), row(256, 21), row(1024, 0), row(1024, 0), row(1024, 0), full(g_q), full(g_kv),
                  full(w_uq_pt), full(w_kv_pt), row(128, 0), row(128, 0), row(128, 0),
                  pl.BlockSpec(memory_space=pl.ANY)],
        out_specs=[row(1024, 0), row(2048, 0), row(1152, 4), acc(768), acc(256)],
        out_shape=[sds(1024, BF16), sds(2048, BF16), jax.ShapeDtypeStruct(dproj.shape, BF16),
                   jax.ShapeDtypeStruct((1, 768), F32), jax.ShapeDtypeStruct((1, 256), F32)],
        input_output_aliases={12: 2},
        compiler_params=_params(("arbitrary",)),
        name="mla_bwd",
    )(proj, proj, dqr, dkr, dv, g_q, g_kv, w_uq_pt, w_kv_pt, rc, rs1, rs2, dproj)


def _pick(n, options):
    for o in options:
        if n % o == 0:
            return o
    raise ValueError(n)


def _matmul(a, b, name):
    m, k = a.shape
    n = b.shape[1]
    tm = _pick(m, (1024, 768, 512, 256))
    tn = _pick(n, (1152, 1024, 768, 512))
    tk = _pick(k, (1024, 512))
    nk = k // tk

    def body(a_ref, b_ref, o_ref):
        @pl.when(pl.program_id(2) == 0)
        def _():
            o_ref[...] = jnp.zeros_like(o_ref)

        o_ref[...] += _dot(a_ref[...], b_ref[...])

    return pl.pallas_call(
        body,
        grid=(m // tm, n // tn, nk),
        in_specs=[pl.BlockSpec((tm, tk), lambda i, j, l: (i, l)), pl.BlockSpec((tk, tn), lambda i, j, l: (l, j))],
        out_specs=pl.BlockSpec((tm, tn), lambda i, j, l: (i, j)),
        out_shape=jax.ShapeDtypeStruct((m, n), F32),
        compiler_params=_params(("arbitrary", "arbitrary", "arbitrary")),
        name=name,
    )(a, b)


def _dh_dx(dproj, w_in_pt, x, dout, g_pre, sends):
    s, k = dproj.shape
    tm = 256
    ns, ni = len(sends), s // tm

    def body(dp_ref, w_ref, x_ref, dout_ref, g_ref, *rest):
        send_refs, (dx_ref, dg_ref) = rest[:ns], rest[ns:ns + 2]
        recv_refs, sems = rest[ns + 2:2 * ns + 2], rest[2 * ns + 2:]

        @pl.when(pl.program_id(0) == 0)
        def _():
            _start_all(*_to_chips_copies(send_refs, recv_refs, sems))
            dg_ref[...] = jnp.zeros_like(dg_ref)

        dh = _dot(dp_ref[...], w_ref[...])
        xv = x_ref[...]
        r = lax.rsqrt(jnp.mean(xv * xv, axis=-1, keepdims=True) + EPS)
        xh = xv * r
        dg_ref[...] += jnp.sum(dh * xh, axis=0, keepdims=True)
        dxh = dh * g_ref[...]
        dx_ref[...] = dout_ref[...] + r * (dxh - xh * jnp.mean(dxh * xh, axis=-1, keepdims=True))

        @pl.when(pl.program_id(0) == ni - 1)
        def _():
            _wait_all(*_to_chips_copies(send_refs, recv_refs, sems))

    row = lambda w: pl.BlockSpec((tm, w), lambda i: (i, 0))
    return pl.pallas_call(
        body,
        grid=(ni,),
        in_specs=[row(k), pl.BlockSpec((k, D_MODEL), lambda i: (0, 0)), row(D_MODEL), row(D_MODEL),
                  pl.BlockSpec((1, D_MODEL), lambda i: (0, 0))] + [ANY] * ns,
        out_specs=[row(D_MODEL), pl.BlockSpec((1, D_MODEL), lambda i: (0, 0))] + [ANY] * ns,
        out_shape=[jax.ShapeDtypeStruct((s, D_MODEL), F32), jax.ShapeDtypeStruct((1, D_MODEL), F32)]
        + [jax.ShapeDtypeStruct(a.shape, a.dtype) for a in sends],
        scratch_shapes=_copy_sems(ns, 3),
        compiler_params=_params(("arbitrary",)),
        name="dh_dx",
    )(dproj, w_in_pt, x, dout, g_pre, *sends)


def _pair_reduce(slots):
    n = len(slots)
    half = [(N_DEV // 2,) + a.shape[1:] for a in slots]

    def body(*refs):
        s_refs, o_refs = refs[:n], refs[n:2 * n]
        mine, got = refs[2 * n:3 * n], refs[3 * n:4 * n]
        send_sems, recv_sems, local_sems = refs[4 * n:]
        x, y, c = _my_place()
        copies, loads = [], []
        for a in range(n):
            for q in range(N_DEV // 2):
                copies.append(pltpu.make_async_remote_copy(
                    src_ref=s_refs[a].at[2 * q + 1 - c], dst_ref=got[a].at[q],
                    send_sem=send_sems.at[4 * a + q], recv_sem=recv_sems.at[4 * a + q],
                    device_id=(x, y, 1 - c), device_id_type=MESH_ID))
                loads.append(pltpu.make_async_copy(s_refs[a].at[2 * q + c], mine[a].at[q], local_sems.at[4 * a + q]))
        _start_all(loads, copies)
        _wait_all(loads, copies)
        for a in range(n):
            o_refs[a][...] = (mine[a][...].astype(F32) + got[a][...].astype(F32)).astype(o_refs[a].dtype)

    vm = lambda: [pltpu.VMEM(h, a.dtype) for h, a in zip(half, slots)]
    return pl.pallas_call(
        body,
        in_specs=[ANY] * n,
        out_shape=[jax.ShapeDtypeStruct(h, a.dtype) for h, a in zip(half, slots)],
        scratch_shapes=vm() + vm() + [pltpu.SemaphoreType.DMA((4 * n,)), pltpu.SemaphoreType.DMA((4 * n,)),
                                      pltpu.SemaphoreType.DMA((4 * n,))],
        compiler_params=pltpu.CompilerParams(vmem_limit_bytes=48 * 2**20),
        name="pair_reduce",
    )(*slots)


def _rope_tables(s):
    inv = (np.float32(ROPE_THETA) ** (-np.arange(0, ROPE, 2, dtype=np.float32) / np.float32(ROPE))).astype(np.float32)
    ang = (np.arange(s, dtype=np.float32)[:, None] * inv[None, :]).astype(np.float32)
    cos, sin = jnp.asarray(np.cos(ang.astype(np.float64)), F32), jnp.asarray(np.sin(ang.astype(np.float64)), F32)
    z = lambda w: jnp.zeros((s, w), F32)
    rc = jnp.concatenate([jnp.ones((s, NOPE), F32), cos, cos, z(32)], axis=1)
    rs1 = jnp.concatenate([z(NOPE), -sin, z(16), z(32)], axis=1)
    rs2 = jnp.concatenate([z(NOPE), z(16), sin, z(32)], axis=1)
    return rc, rs1, rs2


def _step(x, tgt, w_blk, shards, g_pre, b_gate, g_q, g_kv, lbl, g_hgrn, g_post):
    s = x.shape[0]
    rc, rs1, rs2 = _rope_tables(s)
    gh = jnp.tile(g_hgrn, (1, HEADS))

    proj, w_in_pt, ht, *got = _gather_proj(x, g_pre, w_blk, shards[:2])
    w_uq, w_ukv = (_from_slots(n, g) for n, g in zip(MATS[:2], got))
    w_uq_p = jnp.pad(w_uq.reshape(Q_LORA, HEADS, QK), ((0, 0), (0, 0), (0, LANE - QK))).reshape(Q_LORA, HEADS * LANE)
    kv3 = w_ukv.reshape(KV_LORA, HEADS, NOPE + VDIM)
    pad64 = lambda t: jnp.pad(t, ((0, 0), (0, 0), (0, LANE - 64))).reshape(KV_LORA, HEADS * LANE)
    w_kv_p = jnp.concatenate([pad64(kv3[:, :, :NOPE]), pad64(kv3[:, :, NOPE:])], axis=1)

    qr, kr, v, cqt, ckvt = _mla_prep(proj, g_q, g_kv, w_uq_p, w_kv_p, rc, rs1, rs2)
    attn, qa, *got = _attn_fwd(qr, kr, v, shards[2:])
    w_a, w_b, w_out = (_from_slots(n, g) for n, g in zip(MATS[2:], got))
    o, sprev = _hgrn_fwd(proj, lbl)
    (dout, dproj, dop, do, mt, dy_bf, yat, dya_bf, ybt, dyb_bf,
     loss_vec, dg_post, db_gate, dgh) = _tail(x, tgt, proj, attn, o, w_a, w_b, w_out, w_a.T, w_b.T, w_out.T,
                                               b_gate, g_post, gh)
    early = [_to_slots(n, _matmul(a, b, "d" + n)).astype(BF16)
             for n, a, b in (("w_branch_a", yat, dya_bf), ("w_branch_b", ybt, dyb_bf), ("w_out", mt, dy_bf))]
    dqr, dkr, dv, *early_recv = _attn_bwd(qa, kr, v, dop, early)
    dproj, dlbl = _hgrn_bwd(proj, lbl, do, sprev, dproj)
    dqf, dkvf, dproj, dg_q, dg_kv = _mla_bwd(proj, dqr, dkr, dv, g_q, g_kv, w_uq_p.T, w_kv_p.T, rc, rs1, rs2, dproj)

    dw_in_slots = _scatter_w_in(_matmul(ht, dproj, "dw_in"))
    dw_uq_p = _matmul(cqt, dqf, "dw_uq")
    dw_kv_p = _matmul(ckvt, dkvf, "dw_kv")
    dw_uq = dw_uq_p.reshape(Q_LORA, HEADS, LANE)[:, :, :QK].reshape(Q_LORA, HEADS * QK)
    dw_ukv = jnp.concatenate([dw_kv_p[:, :HEADS * LANE].reshape(KV_LORA, HEADS, LANE)[:, :, :NOPE],
                              dw_kv_p[:, HEADS * LANE:].reshape(KV_LORA, HEADS, LANE)[:, :, :VDIM]],
                             axis=2).reshape(KV_LORA, 1024)
    late = _pair_reduce([dw_in_slots, _to_slots("w_uq", dw_uq).astype(BF16), _to_slots("w_ukv", dw_ukv).astype(BF16)])
    dx, dg_pre, *late_recv = _dh_dx(dproj, w_in_pt, x, dout, g_pre, late)

    g_sum = _vectors_sum(dg_pre, db_gate, dg_q, dg_kv, dlbl, dgh, dg_post, loss_vec)
    return dx, late_recv[0], dict(zip(MATS, late_recv[1:] + early_recv)), g_sum


def _all_gather(blocks):
    n = len(blocks)

    def body(*refs):
        x_refs, out_refs = refs[:n], refs[n:2 * n]
        send_sems, recv_sems, local_sems = refs[2 * n:]
        x, y, c = _my_place()
        me, sibling = (x, y, c), (x, y, 1 - c)
        chips = [(1 - x, y), (x, 1 - y), (1 - x, 1 - y)]

        def slot(a, px, py, pc):
            return out_refs[a].at[4 * px + 2 * py + pc]

        def copy(a, k, blk, to, src=None):
            return pltpu.make_async_remote_copy(
                src_ref=slot(a, *blk) if src is None else src, dst_ref=slot(a, *blk),
                send_sem=send_sems.at[7 * a + k], recv_sem=recv_sems.at[7 * a + k],
                device_id=to, device_id_type=MESH_ID)

        mine = [pltpu.make_async_copy(x_refs[a], slot(a, *me), local_sems.at[a]) for a in range(n)]
        for cp in mine:
            cp.start()
        first = [copy(a, 0, me, sibling, src=x_refs[a]) for a in range(n)]
        first += [copy(a, 1 + j, me, (*chip, c), src=x_refs[a]) for a in range(n) for j, chip in enumerate(chips)]
        for cp in first:
            cp.start()
        passed = []
        for j, chip in enumerate(chips):
            for a in range(n):
                copy(a, 1 + j, (*chip, c), me).wait_recv()
                passed.append(copy(a, 4 + j, (*chip, c), sibling))
                passed[-1].start()
        for a in range(n):
            copy(a, 0, sibling, me).wait_recv()
        for j, chip in enumerate(chips):
            for a in range(n):
                copy(a, 4 + j, (*chip, 1 - c), me).wait_recv()
        for cp in first + passed:
            cp.wait_send()
        for cp in mine:
            cp.wait()

    return pl.pallas_call(
        body,
        out_shape=[jax.ShapeDtypeStruct((N_DEV,) + b.shape, b.dtype) for b in blocks],
        in_specs=[pl.BlockSpec(memory_space=pl.ANY)] * n,
        out_specs=[pl.BlockSpec(memory_space=pl.ANY)] * n,
        scratch_shapes=[pltpu.SemaphoreType.DMA((7 * n,)), pltpu.SemaphoreType.DMA((7 * n,)),
                        pltpu.SemaphoreType.DMA((n,))],
        name="gather_weights",
    )(*blocks)


def _adamw(g, w, m, v):
    c1 = 1.0 / (1.0 - ADAM_B1 ** ADAM_STEP)
    c2 = 1.0 / (1.0 - ADAM_B2 ** ADAM_STEP)
    nm = ADAM_B1 * m + (1.0 - ADAM_B1) * g
    nv = ADAM_B2 * v + (1.0 - ADAM_B2) * (g * g)
    d = -ADAM_LR * ((nm * c1) / (jnp.sqrt(nv * c2) + ADAM_EPS) + ADAM_WD * w)
    return d, nm, nv


def _sum8(r_ref):
    g = r_ref[0].astype(F32)
    for k in range(1, r_ref.shape[0]):
        g = g + r_ref[k].astype(F32)
    return g


def _sum_adamw_w_in(recv, w, m, v):
    rows, _, cols = w.shape
    tc = 256

    def body(r_ref, w_ref, m_ref, v_ref, g_ref, d_ref, nm_ref, nv_ref):
        g = _sum8(r_ref)
        dense = lambda ref: ref[...].reshape(rows, tc)
        d, nm, nv = _adamw(g, dense(w_ref), dense(m_ref), dense(v_ref))
        for ref, val in ((g_ref, g), (d_ref, d), (nm_ref, nm), (nv_ref, nv)):
            ref[...] = val.reshape(rows, 1, tc)

    blk = pl.BlockSpec((rows, 1, tc), lambda i: (0, 0, i))
    out = jax.ShapeDtypeStruct((rows, 1, cols), F32)
    return pl.pallas_call(
        body,
        grid=(cols // tc,),
        in_specs=[pl.BlockSpec((recv.shape[0], rows, tc), lambda i: (0, 0, i)), blk, blk, blk],
        out_specs=[blk, blk, blk, blk],
        out_shape=[out, out, out, out],
        compiler_params=_params(("arbitrary",)),
        name="sum_adamw_w_in",
    )(recv, w, m, v)


def _sum_adamw_whole(recvs, ws, ms, vs):
    n = len(ws)

    def body(*refs):
        r_refs, w_refs, m_refs, v_refs = refs[:n], refs[n:2 * n], refs[2 * n:3 * n], refs[3 * n:4 * n]
        outs = refs[4 * n:]
        for a in range(n):
            g = _sum8(r_refs[a])
            d, nm, nv = _adamw(g, w_refs[a][...], m_refs[a][...], v_refs[a][...])
            outs[a][...] = g
            outs[n + a][...] = d
            outs[2 * n + a][...] = nm
            outs[3 * n + a][...] = nv

    shapes = [jax.ShapeDtypeStruct(w.shape, F32) for w in ws]
    res = pl.pallas_call(
        body,
        out_shape=shapes * 4,
        compiler_params=pltpu.CompilerParams(vmem_limit_bytes=48 * 2**20),
        name="sum_adamw_mats",
    )(*recvs, *ws, *ms, *vs)
    return res[:n], res[n:2 * n], res[2 * n:3 * n], res[3 * n:]


SMALL = ("g_pre", "b_gate", "g_q", "g_kv", "lb_logits", "g_hgrn", "g_post")
SMALL_SHAPE = dict(g_pre=(1, 1024), b_gate=(1, 2048), g_q=(1, 768), g_kv=(1, 256), lb_logits=(2, 512),
                   g_hgrn=(1, 64), g_post=(1, 1024))


def _vectors_sum(dg_pre, db_gate, dg_q, dg_kv, dlbl, dgh, dg_post, loss_vec):
    def body(gpre_ref, bg_ref, gq_ref, gkv_ref, lbl_ref, gh_ref, gpost_ref, loss_ref, out_ref, mine, got,
             send_sems, recv_sems):
        mine[...] = jnp.zeros_like(mine)
        mine[0:1, :] = gpre_ref[...]
        mine[1:2, :] = bg_ref[:, :1024]
        mine[2:3, :] = bg_ref[:, 1024:]
        mine[3:4, :Q_LORA] = gq_ref[...]
        mine[4:5, :KV_LORA] = gkv_ref[...]
        loss = (0.5 / D_MODEL) * jnp.sum(loss_ref[...], axis=-1, keepdims=True)
        mine[4:5, KV_LORA:] = jnp.broadcast_to(loss, (1, 1024 - KV_LORA))
        mine[5:6, :HG_WIDTH] = lbl_ref[0:1, :]
        mine[5:6, HG_WIDTH:] = lbl_ref[1:2, :]
        gh = gh_ref[...]
        fold = gh[:, :VDIM]
        for h in range(1, HEADS):
            fold = fold + gh[:, VDIM * h:VDIM * (h + 1)]
        mine[6:7, :VDIM] = fold
        mine[7:8, :] = gpost_ref[...]
        x, y, c = _my_place()
        me = 4 * x + 2 * y + c
        got[me] = mine[...]
        copies = [pltpu.make_async_remote_copy(
            src_ref=mine, dst_ref=got.at[me], send_sem=send_sems.at[k], recv_sem=recv_sems.at[k],
            device_id=_flip(k, x, y, c), device_id_type=MESH_ID) for k in range(N_DEV - 1)]
        _start_all([], copies)
        _wait_all([], copies)
        out_ref[...] = _sum8(got)

    return pl.pallas_call(
        body,
        out_shape=jax.ShapeDtypeStruct((8, 1024), F32),
        scratch_shapes=[pltpu.VMEM((8, 1024), F32), pltpu.VMEM((N_DEV, 8, 1024), F32),
                        pltpu.SemaphoreType.DMA((7,)), pltpu.SemaphoreType.DMA((7,))],
        name="vectors_sum",
    )(dg_pre, db_gate, dg_q, dg_kv, dlbl, dgh, dg_post, loss_vec)


def _vectors_adamw(g_sum, ws, ms, vs):
    n = len(SMALL)

    def body(g_ref, *refs):
        w_refs, m_refs, v_refs = refs[:n], refs[n:2 * n], refs[2 * n:3 * n]
        loss_ref, outs = refs[3 * n], refs[3 * n + 1:]
        g = g_ref[...]
        loss_ref[...] = g[4:5, KV_LORA:KV_LORA + 1]
        grads = (g[0:1, :], jnp.concatenate([g[1:2, :], g[2:3, :]], axis=1), g[3:4, :Q_LORA], g[4:5, :KV_LORA],
                 jnp.concatenate([g[5:6, :HG_WIDTH], g[5:6, HG_WIDTH:]], axis=0), g[6:7, :VDIM], g[7:8, :])
        for a in range(n):
            d, nm, nv = _adamw(grads[a], w_refs[a][...], m_refs[a][...], v_refs[a][...])
            outs[a][...] = grads[a]
            outs[n + a][...] = d
            outs[2 * n + a][...] = nm
            outs[3 * n + a][...] = nv

    shapes = [jax.ShapeDtypeStruct(SMALL_SHAPE[k], F32) for k in SMALL]
    res = pl.pallas_call(
        body,
        out_shape=[jax.ShapeDtypeStruct((1, 1), F32)] + shapes * 4,
        name="vectors_adamw",
    )(g_sum, *ws, *ms, *vs)
    return res[0], res[1:n + 1], res[n + 1:2 * n + 1], res[2 * n + 1:3 * n + 1], res[3 * n + 1:]


MATS = ("w_uq", "w_ukv", "w_branch_a", "w_branch_b", "w_out")
COL_SHARDED = dict(w_uq=False, w_ukv=True, w_branch_a=True, w_branch_b=True, w_out=False)
ORDER = ("g_pre", "w_in", "b_gate", "g_q", "w_uq", "g_kv", "w_ukv", "lb_logits", "g_hgrn",
         "w_branch_a", "w_branch_b", "w_out", "g_post")


def _to_slots(name, full):
    r, c = full.shape
    if COL_SHARDED[name]:
        return full.reshape(r, N_DEV, c // N_DEV).transpose(1, 0, 2)
    return full.reshape(N_DEV, r // N_DEV, c)


def _from_slots(name, slots):
    _, r, c = slots.shape
    if COL_SHARDED[name]:
        return slots.transpose(1, 0, 2).reshape(r, N_DEV * c)
    return slots.reshape(N_DEV * r, c)


def kernel(x, g_pre, w_in, b_gate, g_q, w_uq, g_kv, w_ukv, lb_logits, g_hgrn, w_branch_a, w_branch_b, w_out, g_post, loss_target, m_g_pre, m_w_in, m_b_gate, m_g_q, m_w_uq, m_g_kv, m_w_ukv, m_lb_logits, m_g_hgrn, m_w_branch_a, m_w_branch_b, m_w_out, m_g_post, v_g_pre, v_w_in, v_b_gate, v_g_q, v_w_uq, v_g_kv, v_w_ukv, v_lb_logits, v_g_hgrn, v_w_branch_a, v_w_branch_b, v_w_out, v_g_post):
    rows3 = lambda a: jnp.transpose(a, (2, 0, 1))
    w = dict(w_in=rows3(w_in), w_uq=w_uq[0], w_ukv=w_ukv[0], w_branch_a=w_branch_a[0], w_branch_b=w_branch_b[0],
             w_out=w_out[0], g_pre=g_pre, b_gate=b_gate, g_q=g_q, g_kv=g_kv, lb_logits=lb_logits, g_hgrn=g_hgrn,
             g_post=g_post)
    mom = dict(w_in=rows3(m_w_in), w_uq=m_w_uq[0], w_ukv=m_w_ukv[0], w_branch_a=m_w_branch_a[0],
               w_branch_b=m_w_branch_b[0], w_out=m_w_out[0], g_pre=m_g_pre, b_gate=m_b_gate, g_q=m_g_q, g_kv=m_g_kv,
               lb_logits=m_lb_logits, g_hgrn=m_g_hgrn, g_post=m_g_post)
    var = dict(w_in=rows3(v_w_in), w_uq=v_w_uq[0], w_ukv=v_w_ukv[0], w_branch_a=v_w_branch_a[0],
               w_branch_b=v_w_branch_b[0], w_out=v_w_out[0], g_pre=v_g_pre, b_gate=v_b_gate, g_q=v_g_q, g_kv=v_g_kv,
               lb_logits=v_lb_logits, g_hgrn=v_g_hgrn, g_post=v_g_post)

    w_blk = w["w_in"].reshape(W_IN_SHARD, D_MODEL).astype(BF16)
    dx, recv_in, recv, g_sum = _step(x[0], loss_target[0], w_blk, [w[n].astype(BF16) for n in MATS],
                                     g_pre, b_gate, g_q, g_kv, lb_logits, g_hgrn, g_post)

    g_in, d_in, m_in, v_in = _sum_adamw_w_in(recv_in, w["w_in"], mom["w_in"], var["w_in"])
    res = _sum_adamw_whole([recv[n] for n in MATS], *([t[n] for n in MATS] for t in (w, mom, var)))
    total, *vec = _vectors_adamw(g_sum, *([t[n] for n in SMALL] for t in (w, mom, var)))

    outs = []
    for mats, vecs, big in zip(res, vec, (g_in, d_in, m_in, v_in)):
        t = {**{n: a[None] for n, a in zip(MATS, mats)}, **dict(zip(SMALL, vecs)),
             "w_in": jnp.transpose(big, (1, 2, 0))}
        outs += [t[n] for n in ORDER]
    return (total.reshape(()), dx[None], *outs)
```

```python
import math

import jax
import jax.numpy as jnp
import numpy as np
from jax import lax
from jax.experimental import pallas as pl
from jax.experimental.pallas import tpu as pltpu

F32, BF16 = jnp.float32, jnp.bfloat16

D_MODEL = 1024
EPS = 1e-6
HEADS = 8
NOPE, ROPE, VDIM = 64, 32, 64
QK = NOPE + ROPE
Q_LORA, KV_LORA = 768, 256
ROPE_THETA = 10000.0
ATT_CHUNK_SHIFT = 6
HG_BLOCK = 32
HG_WIDTH = 512
D_IN = 5664
D_IN_PAD = 5760
W_IN_SHARD = D_IN // 8
N_DEV = 8
LANE = 128

ADAM_LR, ADAM_B1, ADAM_B2, ADAM_EPS, ADAM_WD, ADAM_STEP = 0.001, 0.9, 0.999, 1e-08, 0.01, 10

W_IN_SEGMENTS = ((3616, 5664, 0), (1056, 1568, 2048), (3104, 3616, 2560), (1568, 3104, 3072),
                 (0, 1024, 4608), (1024, 1056, 5696))
W_IN_ZERO = ((5632, 5696), (5728, 5760))

NT = (((1,), (1,)), ((), ()))
TN = (((0,), (0,)), ((), ()))
MESH_ID = pl.DeviceIdType.MESH


def _w_in_pieces():
    out = []
    for lo, hi, dst in W_IN_SEGMENTS:
        c = lo
        while c < hi:
            p = c // W_IN_SHARD
            e = min(hi, (p + 1) * W_IN_SHARD)
            out.append((p, c - p * W_IN_SHARD, e - p * W_IN_SHARD, dst + c - lo))
            c = e
    return out


def _params(sem, vmem_mb=48):
    return pltpu.CompilerParams(dimension_semantics=sem, vmem_limit_bytes=vmem_mb * 2**20)


def _dot(a, b):
    return jnp.dot(a, b, preferred_element_type=F32)


def _dotg(a, b, dims):
    return lax.dot_general(a, b, dims, preferred_element_type=F32)


def _split2(x):
    hi = x.astype(BF16)
    return hi, (x - hi.astype(F32)).astype(BF16)


def _sel_left(m01, x):
    hi, lo = _split2(x)
    return _dot(m01, hi) + _dot(m01, lo)


def _sel_right(x, m01):
    hi, lo = _split2(x)
    return _dot(hi, m01) + _dot(lo, m01)


def _hi_lo(x):
    hi = x.astype(BF16).astype(F32)
    return hi, x - hi


def _sigmoid(x):
    return 0.5 * jnp.tanh(0.5 * x) + 0.5


def _rope(x, c, s1, s2):
    return x * c + pltpu.roll(x, 112, 1) * s1 + pltpu.roll(x, 16, 1) * s2


def _unrope(d, c, s1, s2):
    return d * c + pltpu.roll(d * s1, 16, 1) + pltpu.roll(d * s2, 112, 1)


def _my_place():
    return lax.axis_index("x"), lax.axis_index("y"), lax.axis_index("c")


def _flip(k, x, y, c):
    fx, fy, fc = (k + 1) >> 2 & 1, (k + 1) >> 1 & 1, (k + 1) & 1
    return (1 - x if fx else x), (1 - y if fy else y), (1 - c if fc else c)


def _to_all_copies(s_refs, r_refs, sems, spread):
    send_sems, recv_sems, local_sems = sems
    x, y, c = _my_place()
    me = 4 * x + 2 * y + c
    src = (lambda a, p: s_refs[a]) if spread else (lambda a, p: s_refs[a].at[p])
    local = [pltpu.make_async_copy(src(a, me), r_refs[a].at[me], local_sems.at[a]) for a in range(len(s_refs))]
    remote = []
    for k in range(N_DEV - 1):
        px, py, pc = _flip(k, x, y, c)
        for a in range(len(s_refs)):
            remote.append(pltpu.make_async_remote_copy(
                src_ref=src(a, 4 * px + 2 * py + pc), dst_ref=r_refs[a].at[me],
                send_sem=send_sems.at[7 * a + k], recv_sem=recv_sems.at[7 * a + k],
                device_id=(px, py, pc), device_id_type=MESH_ID))
    return local, remote


def _to_chips_copies(s_refs, r_refs, sems):
    send_sems, recv_sems, local_sems = sems
    x, y, c = _my_place()
    me = 2 * x + y
    local = [pltpu.make_async_copy(s_refs[a].at[me], r_refs[a].at[me], local_sems.at[a]) for a in range(len(s_refs))]
    remote = []
    for k in range(3):
        px = 1 - x if (k + 1) >> 1 & 1 else x
        py = 1 - y if (k + 1) & 1 else y
        for a in range(len(s_refs)):
            remote.append(pltpu.make_async_remote_copy(
                src_ref=s_refs[a].at[2 * px + py], dst_ref=r_refs[a].at[me],
                send_sem=send_sems.at[3 * a + k], recv_sem=recv_sems.at[3 * a + k],
                device_id=(px, py, c), device_id_type=MESH_ID))
    return local, remote


def _start_all(local, remote):
    for cp in local + remote:
        cp.start()


def _wait_all(local, remote):
    for cp in remote:
        cp.wait_recv()
    for cp in remote:
        cp.wait_send()
    for cp in local:
        cp.wait()


def _copy_sems(n, peers):
    return [pltpu.SemaphoreType.DMA((peers * n,)), pltpu.SemaphoreType.DMA((peers * n,)),
            pltpu.SemaphoreType.DMA((n,))]


ANY = pl.BlockSpec(memory_space=pl.ANY)


def _assemble_w_in(slots):
    tc = 256
    pieces = _w_in_pieces()

    def body(s_ref, w_ref, wt_ref):
        for lo, hi in W_IN_ZERO:
            wt_ref[lo:hi, :] = jnp.zeros((hi - lo, tc), BF16)
        for p, lo, hi, dst in pieces:
            wt_ref[dst:dst + hi - lo, :] = s_ref[p, lo:hi, :]
        w_ref[...] = wt_ref[...].T

    return pl.pallas_call(
        body,
        grid=(D_MODEL // tc,),
        in_specs=[pl.BlockSpec((N_DEV, W_IN_SHARD, tc), lambda i: (0, 0, i))],
        out_specs=[pl.BlockSpec((tc, D_IN_PAD), lambda i: (i, 0)), pl.BlockSpec((D_IN_PAD, tc), lambda i: (0, i))],
        out_shape=[jax.ShapeDtypeStruct((D_MODEL, D_IN_PAD), BF16), jax.ShapeDtypeStruct((D_IN_PAD, D_MODEL), BF16)],
        compiler_params=_params(("arbitrary",)),
        name="assemble_w_in",
    )(slots)


def _scatter_w_in(dw):
    tc = 256
    pieces = _w_in_pieces()

    def body(d_ref, s_ref):
        dt = d_ref[...].T
        for p, lo, hi, dst in pieces:
            s_ref[p, lo:hi, :] = dt[dst:dst + hi - lo, :].astype(BF16)

    return pl.pallas_call(
        body,
        grid=(D_MODEL // tc,),
        in_specs=[pl.BlockSpec((tc, D_IN_PAD), lambda i: (i, 0))],
        out_specs=pl.BlockSpec((N_DEV, W_IN_SHARD, tc), lambda i: (0, 0, i)),
        out_shape=jax.ShapeDtypeStruct((N_DEV, W_IN_SHARD, D_MODEL), BF16),
        compiler_params=_params(("arbitrary",)),
        name="scatter_w_in",
    )(dw)


def _norm_proj(x, g_pre, w, shards):
    s, n = x.shape[0], w.shape[1]
    tm = 512
    ni, ns = s // tm, len(shards)

    def body(x_ref, g_ref, w_hbm, *rest):
        shard_refs, (proj_ref, ht_ref), got_refs = rest[:ns], rest[ns:ns + 2], rest[ns + 2:2 * ns + 2]
        w_ref, w_sem, sems = rest[2 * ns + 2], rest[2 * ns + 3], rest[2 * ns + 4:]
        i = pl.program_id(0)

        @pl.when(i == 0)
        def _():
            load = pltpu.make_async_copy(w_hbm, w_ref, w_sem)
            load.start()
            _start_all(*_to_all_copies(shard_refs, got_refs, sems, True))
            load.wait()

        xv = x_ref[...]
        r = lax.rsqrt(jnp.mean(xv * xv, axis=-1, keepdims=True) + EPS)
        h = (xv * r * g_ref[...]).astype(BF16)
        ht_ref[...] = h.T
        proj_ref[...] = _dot(h, w_ref[...])

        @pl.when(i == ni - 1)
        def _():
            _wait_all(*_to_all_copies(shard_refs, got_refs, sems, True))

    return pl.pallas_call(
        body,
        grid=(ni,),
        in_specs=[pl.BlockSpec((tm, D_MODEL), lambda i: (i, 0)), pl.BlockSpec((1, D_MODEL), lambda i: (0, 0)), ANY]
        + [ANY] * ns,
        out_specs=[pl.BlockSpec((tm, n), lambda i: (i, 0)), pl.BlockSpec((D_MODEL, tm), lambda i: (0, i))] + [ANY] * ns,
        out_shape=[jax.ShapeDtypeStruct((s, n), F32), jax.ShapeDtypeStruct((D_MODEL, s), BF16)]
        + [jax.ShapeDtypeStruct((N_DEV,) + b.shape, b.dtype) for b in shards],
        scratch_shapes=[pltpu.VMEM(w.shape, BF16), pltpu.SemaphoreType.DMA] + _copy_sems(ns, 7),
        compiler_params=_params(("arbitrary",), 56),
        name="norm_proj",
    )(x, g_pre, w, *shards)


GP_TN = 256
GP_COLS = 5888
GP_NT = GP_COLS // GP_TN


def _gp_tile_pieces():
    tiles = [[] for _ in range(GP_NT)]
    for p, lo, hi, dst in _w_in_pieces():
        while lo < hi:
            t = dst // GP_TN
            n = min(hi - lo, (t + 1) * GP_TN - dst)
            tiles[t].append((p, lo, lo + n, dst - t * GP_TN))
            lo, dst = lo + n, dst + n
    return tiles


def _gp_tables():
    pieces = _gp_tile_pieces()
    rank_of = {None: 0, 0: 1, 1: 2, 2: 2, 4: 3, 5: 3, 3: 4, 6: 5}
    order = np.zeros((N_DEV, GP_NT), np.int32)
    waits = np.zeros((N_DEV, GP_NT), np.int32)
    for me in range(N_DEV):
        x, y, c = me >> 2 & 1, me >> 1 & 1, me & 1
        chips = [(1 - x, y), (x, 1 - y), (1 - x, 1 - y)]

        def sem_of(p):
            px, py, pc = p >> 2 & 1, p >> 1 & 1, p & 1
            if (px, py) == (x, y):
                return None if pc == c else 0
            j = chips.index((px, py))
            return 1 + j if pc == c else 4 + j

        needs = [sorted({sem_of(p) for p, _, _, _ in tile} - {None}) for tile in pieces]
        ranks = [max([rank_of[k] for k in ks], default=0) for ks in needs]
        seq = sorted(range(GP_NT), key=lambda t: (ranks[t], t))
        seen = set()
        for step, t in enumerate(seq):
            order[me, step] = t
            new = [k for k in needs[t] if k not in seen]
            for k in new:
                waits[me, step] |= 1 << k
            seen.update(new)
        assert seen == set(range(7)), (me, seen)
    return order, waits


def _gather_proj(x, g_pre, w_blk, shards):
    s = x.shape[0]
    tx = 512
    ns = len(shards)
    tile_pieces = _gp_tile_pieces()
    order_np, waits_np = _gp_tables()
    xq, yq, cq = _my_place()
    me_out = 4 * xq + 2 * yq + cq
    order = lax.dynamic_index_in_dim(jnp.asarray(order_np), me_out, 0, keepdims=False)
    waits = lax.dynamic_index_in_dim(jnp.asarray(waits_np), me_out, 0, keepdims=False)

    def body(order_ref, waits_ref, x_hbm, g_ref, wblk_hbm, *rest):
        shard_refs, (proj_ref, wt_ref, ht_hbm), got_refs = rest[:ns], rest[ns:ns + 3], rest[ns + 3:2 * ns + 3]
        recv, h_ref, wtile, xbuf, htbuf = rest[2 * ns + 3:2 * ns + 8]
        send_sems, recv_sems, misc_sems = rest[2 * ns + 8:2 * ns + 11]
        sems = rest[2 * ns + 11:]
        t = pl.program_id(0)
        x_, y_, c = _my_place()
        sibling = (x_, y_, 1 - c)
        chips = [(1 - x_, y_), (x_, 1 - y_), (1 - x_, 1 - y_)]
        idx = lambda px, py, pc: 4 * px + 2 * py + pc
        me = idx(x_, y_, c)

        def copy(k, slot, to, src=None):
            return pltpu.make_async_remote_copy(
                src_ref=recv.at[slot] if src is None else src, dst_ref=recv.at[slot],
                send_sem=send_sems.at[k], recv_sem=recv_sems.at[k], device_id=to, device_id_type=MESH_ID)

        mine = pltpu.make_async_copy(wblk_hbm, recv.at[me], misc_sems.at[0])
        first = [copy(0, me, sibling, src=wblk_hbm)] + [copy(1 + j, me, (*chips[j], c), src=wblk_hbm) for j in range(2)]
        passed = [copy(4 + j, idx(*ch, c), sibling) for j, ch in enumerate(chips)]
        onward = [copy(3, idx(*chips[0], c), (*chips[1], c)), copy(3, idx(*chips[1], c), (*chips[0], c))]
        arrivals = ([copy(0, idx(x_, y_, 1 - c), sibling)] + [copy(1 + j, idx(*ch, c), sibling) for j, ch in enumerate(chips)]
                    + [copy(4 + j, idx(*ch, 1 - c), sibling) for j, ch in enumerate(chips)])

        @pl.when(t == 0)
        def _():
            mine.start()
            for cp in first:
                cp.start()
            _start_all(*_to_all_copies(shard_refs, got_refs, sems, True))

            def load(i):
                return pltpu.make_async_copy(x_hbm.at[pl.ds(i * tx, tx), :], xbuf.at[i & 1], misc_sems.at[1 + (i & 1)])

            def store(i):
                return pltpu.make_async_copy(htbuf.at[i & 1], ht_hbm.at[:, pl.ds(i * tx, tx)], misc_sems.at[3 + (i & 1)])

            load(0).start()
            for i in range(s // tx):
                if i + 1 < s // tx:
                    load(i + 1).start()
                load(i).wait()
                xv = xbuf[i & 1]
                r = lax.rsqrt(jnp.mean(xv * xv, axis=-1, keepdims=True) + EPS)
                h = (xv * r * g_ref[...]).astype(BF16)
                h_ref[i * tx:(i + 1) * tx, :] = h
                if i >= 2:
                    store(i - 2).wait()
                htbuf[i & 1] = h.T
                store(i).start()
            for i in range(max(s // tx - 2, 0), s // tx):
                store(i).wait()
            mine.wait()

        w = waits_ref[t]
        for k in range(7):
            @pl.when((w >> k) & 1 == 1)
            def _(k=k):
                arrivals[k].wait_recv()
                if 1 <= k <= 3:
                    passed[k - 1].start()
                if 1 <= k <= 2:
                    @pl.when(c == k - 1)
                    def _():
                        onward[k - 1].start()

        tile = order_ref[t]
        for tt in range(GP_NT):
            @pl.when(tile == tt)
            def _(tt=tt):
                covered = sorted((d, d + hi - lo) for _, lo, hi, d in tile_pieces[tt])
                at = 0
                for lo_z, hi_z in covered + [(GP_TN, GP_TN)]:
                    if lo_z > at:
                        wtile[at:lo_z, :] = jnp.zeros((lo_z - at, D_MODEL), BF16)
                    at = max(at, hi_z)
                for p, lo, hi, d in tile_pieces[tt]:
                    wtile[d:d + hi - lo, :] = recv[p, lo:hi, :]

        wt = wtile[...]
        wt_ref[...] = wt
        proj_ref[...] = _dotg(h_ref[...], wt, NT)

        @pl.when(t == GP_NT - 1)
        def _():
            for cp in first + passed + onward[:1]:
                cp.wait_send()
            _wait_all(*_to_all_copies(shard_refs, got_refs, sems, True))

    grid_spec = pltpu.PrefetchScalarGridSpec(
        num_scalar_prefetch=2,
        grid=(GP_NT,),
        in_specs=[ANY, pl.BlockSpec((1, D_MODEL), lambda t, o, w: (0, 0)), ANY] + [ANY] * ns,
        out_specs=[pl.BlockSpec((s, GP_TN), lambda t, o, w: (0, o[t])),
                   pl.BlockSpec((GP_TN, D_MODEL), lambda t, o, w: (o[t], 0)), ANY] + [ANY] * ns,
        scratch_shapes=[pltpu.VMEM((N_DEV, W_IN_SHARD, D_MODEL), BF16), pltpu.VMEM((s, D_MODEL), BF16),
                        pltpu.VMEM((GP_TN, D_MODEL), BF16), pltpu.VMEM((2, tx, D_MODEL), F32),
                        pltpu.VMEM((2, D_MODEL, tx), BF16),
                        pltpu.SemaphoreType.DMA((7,)), pltpu.SemaphoreType.DMA((7,)), pltpu.SemaphoreType.DMA((5,))]
        + _copy_sems(ns, 7),
    )
    return pl.pallas_call(
        body,
        grid_spec=grid_spec,
        out_shape=[jax.ShapeDtypeStruct((s, GP_COLS), F32), jax.ShapeDtypeStruct((GP_COLS, D_MODEL), BF16),
                   jax.ShapeDtypeStruct((D_MODEL, s), BF16)]
        + [jax.ShapeDtypeStruct((N_DEV,) + b.shape, b.dtype) for b in shards],
        compiler_params=_params(("arbitrary",), 56),
        name="gather_proj",
    )(order, waits, x, g_pre, w_blk, *shards)


def _mla_prep(proj, g_q, g_kv, w_uq_p, w_kv_p, rc, rs1, rs2):
    s = proj.shape[0]
    tm = 256
    scale = 1.0 / math.sqrt(QK)

    def body(cq_ref, ckv_ref, kpe_ref, gq_ref, gkv_ref, wuq_ref, wkv_ref, c_ref, s1_ref, s2_ref,
             qr_ref, kr_ref, v_ref, cqt_ref, ckvt_ref):
        cq = cq_ref[...]
        r = lax.rsqrt(jnp.mean(cq * cq, axis=-1, keepdims=True) + EPS)
        cqn = (cq * r * gq_ref[...]).astype(BF16)
        cqt_ref[...] = cqn.T
        q = _dot(cqn, wuq_ref[...])
        ckv = ckv_ref[...]
        r = lax.rsqrt(jnp.mean(ckv * ckv, axis=-1, keepdims=True) + EPS)
        ckvn = (ckv * r * gkv_ref[...]).astype(BF16)
        ckvt_ref[...] = ckvn.T
        kv = _dot(ckvn, wkv_ref[...])
        c, s1, s2 = c_ref[...], s1_ref[...], s2_ref[...]
        lane = lax.broadcasted_iota(jnp.int32, (tm, LANE), 1)
        kpe = _rope(kpe_ref[...], c, s1, s2) + jnp.where((lane == QK) | (lane == QK + 1), 1.0, 0.0)
        vone = jnp.where((lane == VDIM) | (lane == VDIM + 1), 1.0, 0.0)
        for h in range(HEADS):
            sl = slice(LANE * h, LANE * (h + 1))
            qr_ref[:, sl] = (_rope(q[:, sl], c, s1, s2) * scale).astype(BF16)
            kr_ref[:, sl] = (kv[:, sl] + kpe).astype(BF16)
            v_ref[:, sl] = (kv[:, HEADS * LANE + LANE * h:HEADS * LANE + LANE * (h + 1)] + vone).astype(BF16)

    row = lambda w, j: pl.BlockSpec((tm, w), lambda i: (i, j))
    col = lambda w: pl.BlockSpec((w, tm), lambda i: (0, i))
    full = lambda a: pl.BlockSpec(a.shape, lambda i: (0, 0))
    return pl.pallas_call(
        body,
        grid=(s // tm,),
        in_specs=[row(768, 6), row(256, 21), row(128, 44), full(g_q), full(g_kv), full(w_uq_p), full(w_kv_p),
                  row(128, 0), row(128, 0), row(128, 0)],
        out_specs=[row(1024, 0), row(1024, 0), row(1024, 0), col(768), col(256)],
        out_shape=[jax.ShapeDtypeStruct((s, 1024), BF16), jax.ShapeDtypeStruct((s, 1024), BF16),
                   jax.ShapeDtypeStruct((s, 1024), BF16), jax.ShapeDtypeStruct((768, s), BF16),
                   jax.ShapeDtypeStruct((256, s), BF16)],
        compiler_params=_params(("arbitrary",)),
        name="mla_prep",
    )(proj, proj, proj, g_q, g_kv, w_uq_p, w_kv_p, rc, rs1, rs2)


ATT_T = 512
ATT_FWD_HEADS = 4


def _chunk_mask(transposed):
    r = lax.broadcasted_iota(jnp.int32, (ATT_T, ATT_T), 0) >> ATT_CHUNK_SHIFT
    c = lax.broadcasted_iota(jnp.int32, (ATT_T, ATT_T), 1) >> ATT_CHUNK_SHIFT
    return (r <= c) if transposed else (c <= r)


def _attn_fwd(qr, kr, vp, shards):
    s = qr.shape[0]
    t = ATT_T
    g = ATT_FWD_HEADS
    ns = len(shards)

    def body(q_ref, k_ref, v_ref, *rest):
        shard_refs, (o_ref, qa_ref), got_refs = rest[:ns], rest[ns:ns + 2], rest[ns + 2:2 * ns + 2]
        sc_ref, sems = rest[2 * ns + 2], rest[2 * ns + 3:]
        qi = pl.program_id(1)

        @pl.when((pl.program_id(0) == 0) & (qi == 0))
        def _():
            _start_all(*_to_all_copies(shard_refs, got_refs, sems, True))
        lane = lax.broadcasted_iota(jnp.int32, (t, LANE), 1)
        sls = [slice(LANE * a, LANE * (a + 1)) for a in range(g)]
        qs = [q_ref[:, sl] for sl in sls]

        def scores(j):
            rows = pl.ds(pl.multiple_of(j * t, t), t)
            for a in range(g):
                sc_ref[j & 1, a] = _dotg(qs[a], k_ref[rows, sls[a]], NT)

        def step(j, carry, masked):
            rows = pl.ds(pl.multiple_of(j * t, t), t)
            out = []
            for a in range(g):
                m, acc = carry[a]
                sc = sc_ref[j & 1, a]
                if masked:
                    sc = jnp.where(_chunk_mask(False), sc, -1e30)
                m_new = jnp.maximum(m, jnp.max(sc, axis=-1, keepdims=True))
                p = jnp.exp(sc - m_new).astype(BF16)
                acc = jnp.exp(m - m_new) * acc + _dot(p, v_ref[rows, sls[a]])
                out.append((m_new, acc))
            return tuple(out)

        def loop(j, carry):
            carry = step(j, carry, False)
            scores(j + 1)
            return carry

        init = tuple((jnp.full((t, 1), -1e30, F32), jnp.zeros((t, LANE), F32)) for _ in range(g))
        scores(0)
        carry = lax.fori_loop(0, qi, loop, init)
        carry = step(qi, carry, True)
        outs = []
        for a in range(g):
            m, acc = carry[a]
            l = acc[:, VDIM:VDIM + 1]
            outs.append(acc / l)
            hi, lo_part = _hi_lo(-(m + jnp.log(l)))
            qa = jnp.where(lane == QK, hi, jnp.where(lane == QK + 1, lo_part, qs[a].astype(F32)))
            qa_ref[:, sls[a]] = qa.astype(BF16)
        for p in range(g // 2):
            o_ref[:, LANE * p:LANE * (p + 1)] = jnp.where(lane < VDIM, outs[2 * p], pltpu.roll(outs[2 * p + 1], VDIM, 1))

        @pl.when((pl.program_id(0) == HEADS // g - 1) & (qi == s // t - 1))
        def _():
            _wait_all(*_to_all_copies(shard_refs, got_refs, sems, True))

    return pl.pallas_call(
        body,
        grid=(HEADS // g, s // t),
        in_specs=[
            pl.BlockSpec((t, g * LANE), lambda h, i: (i, h)),
            pl.BlockSpec((s, g * LANE), lambda h, i: (0, h)),
            pl.BlockSpec((s, g * LANE), lambda h, i: (0, h)),
        ] + [ANY] * ns,
        out_specs=[
            pl.BlockSpec((t, g * VDIM), lambda h, i: (i, h)),
            pl.BlockSpec((t, g * LANE), lambda h, i: (i, h)),
        ] + [ANY] * ns,
        out_shape=[jax.ShapeDtypeStruct((s, 512), F32), jax.ShapeDtypeStruct((s, 1024), BF16)]
        + [jax.ShapeDtypeStruct((N_DEV,) + b.shape, b.dtype) for b in shards],
        scratch_shapes=[pltpu.VMEM((2, g, t, t), F32)] + _copy_sems(ns, 7),
        compiler_params=_params(("arbitrary", "arbitrary")),
        name="attn_fwd",
    )(qr, kr, vp, *shards)


def _attn_bwd(qa, kr, vp, dop, sends):
    s = qa.shape[0]
    t = ATT_T
    nq = s // t
    ns = len(sends)

    def body(q_ref, k_ref, v_ref, do_ref, *rest):
        send_refs, (dq_out, dk_out, dv_out) = rest[:ns], rest[ns:ns + 3]
        recv_refs = rest[ns + 3:2 * ns + 3]
        (dq_ref, dk_ref, dv_ref), sems = rest[2 * ns + 3:2 * ns + 6], rest[2 * ns + 6:]
        j = pl.program_id(1)
        sls = [slice(LANE * a, LANE * (a + 1)) for a in range(2)]

        @pl.when((pl.program_id(0) == 0) & (j == 0))
        def _():
            _start_all(*_to_all_copies(send_refs, recv_refs, sems, False))

        @pl.when(j == 0)
        def _():
            dq_ref[...] = jnp.zeros_like(dq_ref)

        dk_ref[...] = jnp.zeros_like(dk_ref)
        dv_ref[...] = jnp.zeros_like(dv_ref)
        ks = [k_ref[:, sl] for sl in sls]
        vs = [v_ref[:, sl] for sl in sls]

        def step(i, masked):
            rows = pl.ds(pl.multiple_of(i * t, t), t)
            for a in range(2):
                q = q_ref[rows, sls[a]]
                do = do_ref[rows, sls[a]]
                sc = _dotg(ks[a], q, NT)
                if masked:
                    sc = jnp.where(_chunk_mask(True), sc, -1e30)
                p = jnp.exp(sc)
                ds = (p * _dotg(vs[a], do, NT)).astype(BF16)
                dv_ref[:, sls[a]] += _dot(p.astype(BF16), do)
                dk_ref[:, sls[a]] += _dot(ds, q)
                dq_ref[rows, sls[a]] += _dotg(ds, ks[a], TN)

        step(j, True)

        def loop(i, c):
            step(i, False)
            return c

        lax.fori_loop(j + 1, nq, loop, 0)
        dk_out[...] = dk_ref[...].astype(BF16)
        dv_out[...] = dv_ref[...].astype(BF16)

        @pl.when(j == nq - 1)
        def _():
            dq_out[...] = dq_ref[...].astype(BF16)

        @pl.when((pl.program_id(0) == HEADS // 2 - 1) & (j == nq - 1))
        def _():
            _wait_all(*_to_all_copies(send_refs, recv_refs, sems, False))

    blk = pl.BlockSpec((t, 2 * LANE), lambda h, j: (j, h))
    whole = pl.BlockSpec((s, 2 * LANE), lambda h, j: (0, h))
    out = jax.ShapeDtypeStruct((s, 1024), BF16)
    return pl.pallas_call(
        body,
        grid=(HEADS // 2, nq),
        in_specs=[whole, blk, blk, whole] + [ANY] * ns,
        out_specs=[whole, blk, blk] + [ANY] * ns,
        out_shape=[out, out, out] + [jax.ShapeDtypeStruct(a.shape, a.dtype) for a in sends],
        scratch_shapes=[pltpu.VMEM((s, 2 * LANE), F32), pltpu.VMEM((t, 2 * LANE), F32),
                        pltpu.VMEM((t, 2 * LANE), F32)] + _copy_sems(ns, 7),
        compiler_params=_params(("arbitrary", "arbitrary")),
        name="attn_bwd",
    )(qa, kr, vp, dop, *sends)


HG_T = 256
HG_NC = HG_T // HG_BLOCK
HG_G = 4
GW = 64 * HG_G


def _hg_consts():
    r = jnp.arange(HG_T)[:, None]
    c = jnp.arange(HG_T)[None, :]
    same = (r // HG_BLOCK) == (c // HG_BLOCK)
    mcum = (same & (c <= r)).astype(BF16)
    mrev = (same & (c >= r)).astype(BF16)
    msum = same.astype(BF16)
    a = jnp.arange(GW) // 64
    bd = (a[:, None] == a[None, :]).astype(F32)
    return mcum, mrev, msum, bd


def _stack_heads(xg, head):
    return jnp.concatenate([jnp.where(head == h, xg, 0.0) for h in range(HG_G)], axis=0)


def _unstack_heads(r, head, t):
    out = r[(HG_G - 1) * t:]
    for h in range(HG_G - 2, -1, -1):
        out = jnp.where(head == h, r[h * t:(h + 1) * t], out)
    return out


def _compact_state(st):
    out = st[:64]
    for h in range(1, HG_G):
        out = out + st[64 * h:64 * (h + 1)]
    return out


def _expand_state(cs, head64):
    return jnp.concatenate([jnp.where(head64 == h, cs, 0.0) for h in range(HG_G)], axis=0)


def _hg_pre(hq, hf, lbl, mcum, msum):
    lb = _sigmoid(lbl[0:1, :] - lbl[1:2, :])
    sig = _sigmoid(hf)
    f = lb + (1.0 - lb) * sig
    lf = jnp.log(f)
    b = _sel_left(mcum, lf)
    big_l = _sel_left(msum, lf)
    k = 1.0 - f
    qd = hq * jnp.exp(b)
    ki = k * jnp.exp(-b)
    ke = k * jnp.exp(big_l - b)
    return lb, sig, f, b, big_l, qd, ki, ke


def _hgrn_fwd(proj, lbl):
    s = proj.shape[0]
    t = HG_T
    mcum, _, msum, bd = _hg_consts()

    def body(hq_ref, hf_ref, hi_ref, lbl_ref, mcum_ref, msum_ref, bd_ref, o_ref, sp_ref, st_ref):
        @pl.when(pl.program_id(0) == 0)
        def _():
            st_ref[...] = jnp.zeros_like(st_ref)

        mc = mcum_ref[...]
        _, _, _, _, big_l, qd, ki, ke = _hg_pre(hq_ref[...], hf_ref[...], lbl_ref[...], mc, msum_ref[...])
        el = jnp.exp(big_l)
        hi = hi_ref[...]
        head = lax.broadcasted_iota(jnp.int32, (t, GW), 1) >> 6
        mask = jnp.concatenate([mc] * HG_G, axis=0) > 0.5
        for p in range(HEADS // HG_G):
            sl = slice(GW * p, GW * (p + 1))
            vp = hi[:, sl].astype(BF16)
            qs = _stack_heads(qd[:, sl], head).astype(BF16)
            a = jnp.where(mask, _dotg(qs, ki[:, sl].astype(BF16), NT), 0.0)
            o_intra = _unstack_heads(_dot(a.astype(BF16), vp), head, t)
            qb = qd[:, sl].astype(BF16)
            kb = ke[:, sl].astype(BF16)
            st = st_ref[p]
            for c in range(HG_NC):
                rows = slice(HG_BLOCK * c, HG_BLOCK * (c + 1))
                sp_ref[c, :, sl] = _compact_state(st)
                o_ref[rows, sl] = o_intra[rows] + _dotg(qb[rows], st.astype(BF16), NT)
                u = _dotg(vp[rows], kb[rows], TN) * bd_ref[...]
                st = st * el[HG_BLOCK * c:HG_BLOCK * c + 1, sl] + u
            st_ref[p] = st

    row = lambda j: pl.BlockSpec((t, HG_WIDTH), lambda i: (i, j))
    full = lambda a: pl.BlockSpec(a.shape, lambda i: (0, 0))
    return pl.pallas_call(
        body,
        grid=(s // t,),
        in_specs=[row(6), row(7), row(8), full(lbl), full(mcum), full(msum), full(bd)],
        out_specs=[row(0), pl.BlockSpec((HG_NC, 64, HG_WIDTH), lambda i: (i, 0, 0))],
        out_shape=[jax.ShapeDtypeStruct((s, HG_WIDTH), F32),
                   jax.ShapeDtypeStruct((s // HG_BLOCK, 64, HG_WIDTH), F32)],
        scratch_shapes=[pltpu.VMEM((HEADS // HG_G, GW, GW), F32)],
        compiler_params=_params(("arbitrary",)),
        name="hgrn_fwd",
    )(proj, proj, proj, lbl, mcum, msum, bd)


def _hgrn_bwd(proj, lbl, do, sprev, dproj):
    s = proj.shape[0]
    t = HG_T
    nt = s // t
    mcum, mrev, msum, bd = _hg_consts()

    def body(hq_ref, hf_ref, hi_ref, lbl_ref, do_ref, sp_ref, mcum_ref, mrev_ref, msum_ref, bd_ref,
             dproj_in, dh_ref, dlbl_ref, g_ref):
        del dproj_in

        @pl.when(pl.program_id(0) == 0)
        def _():
            g_ref[...] = jnp.zeros_like(g_ref)
            dlbl_ref[...] = jnp.zeros_like(dlbl_ref)

        mc = mcum_ref[...]
        lb, sig, f, b, big_l, qd, ki, ke = _hg_pre(hq_ref[...], hf_ref[...], lbl_ref[...], mc, msum_ref[...])
        el = jnp.exp(big_l)
        hi = hi_ref[...]
        dov = do_ref[...]
        head = lax.broadcasted_iota(jnp.int32, (t, GW), 1) >> 6
        head64 = lax.broadcasted_iota(jnp.int32, (64, GW), 1) >> 6
        mask = jnp.concatenate([mc] * HG_G, axis=0) > 0.5
        dqd_parts, dke_parts, dv_parts, del_parts, dki_parts = [], [], [], [], []
        for p in range(HEADS // HG_G):
            sl = slice(GW * p, GW * (p + 1))
            vp = hi[:, sl].astype(BF16)
            qs = _stack_heads(qd[:, sl], head).astype(BF16)
            kip = ki[:, sl].astype(BF16)
            dos = _stack_heads(dov[:, sl], head).astype(BF16)
            a = jnp.where(mask, _dotg(qs, kip, NT), 0.0).astype(BF16)
            da = jnp.where(mask, _dotg(dos, vp, NT), 0.0).astype(BF16)
            r = _dot(da, kip)
            dki_parts.append(_dotg(da, qs, TN))
            qb = qd[:, sl].astype(BF16)
            kb = ke[:, sl].astype(BF16)
            dob = dov[:, sl].astype(BF16)
            g = g_ref[p]
            dqd_c, dv_c, dke_c, del_c = [], [], [], []
            for c in range(HG_NC - 1, -1, -1):
                rows = slice(HG_BLOCK * c, HG_BLOCK * (c + 1))
                gb = g.astype(BF16)
                st = _expand_state(sp_ref[c, :, sl], head64)
                dqd_c.append(_dot(dob[rows], st.astype(BF16)))
                dv_c.append(_dotg(kb[rows], gb, NT))
                dke_c.append(_dot(vp[rows], gb))
                del_c.append(jnp.broadcast_to(jnp.sum(g * st, axis=0, keepdims=True), (HG_BLOCK, GW)))
                g = g * el[HG_BLOCK * c:HG_BLOCK * c + 1, sl] + _dotg(dob[rows], qb[rows], TN) * bd_ref[...]
            g_ref[p] = g
            up = lambda parts: jnp.concatenate(parts[::-1], axis=0)
            dqd_parts.append(_unstack_heads(r, head, t) + up(dqd_c))
            dv_parts.append(_dotg(a, dos, TN) + up(dv_c))
            dke_parts.append(up(dke_c))
            del_parts.append(up(del_c))
        wide = lambda parts: jnp.concatenate(parts, axis=1)
        dqd, dke, dki, dvv, del_rows = wide(dqd_parts), wide(dke_parts), wide(dki_parts), wide(dv_parts), wide(del_parts)
        dh_ref[:, :HG_WIDTH] = (dqd * jnp.exp(b)).astype(BF16)
        dh_ref[:, 2 * HG_WIDTH:] = dvv.astype(BF16)
        dke_ke = dke * ke
        db = dqd * qd - dki * ki - dke_ke
        dl_rows = _sel_left(msum_ref[...], dke_ke) + del_rows * el
        is_last = (lax.broadcasted_iota(jnp.int32, (t, HG_WIDTH), 0) & (HG_BLOCK - 1)) == HG_BLOCK - 1
        db = db + jnp.where(is_last, dl_rows, 0.0)
        dlf = _sel_left(mrev_ref[...], db)
        dk = dki * jnp.exp(-b) + dke * jnp.exp(big_l - b)
        df = dlf / f - dk
        dh_ref[:, HG_WIDTH:2 * HG_WIDTH] = (df * (1.0 - lb) * sig * (1.0 - sig)).astype(BF16)
        dlb = jnp.sum(df * (1.0 - sig), axis=0, keepdims=True) * lb * (1.0 - lb)
        dlbl_ref[0:1, :] += dlb
        dlbl_ref[1:2, :] -= dlb

    rrow = lambda j: pl.BlockSpec((t, HG_WIDTH), lambda i: (nt - 1 - i, j))
    full = lambda a: pl.BlockSpec(a.shape, lambda i: (0, 0))
    return pl.pallas_call(
        body,
        grid=(nt,),
        in_specs=[rrow(6), rrow(7), rrow(8), full(lbl), rrow(0),
                  pl.BlockSpec((HG_NC, 64, HG_WIDTH), lambda i: (nt - 1 - i, 0, 0)),
                  full(mcum), full(mrev), full(msum), full(bd), pl.BlockSpec(memory_space=pl.ANY)],
        out_specs=[pl.BlockSpec((t, 3 * HG_WIDTH), lambda i: (nt - 1 - i, 2)),
                   pl.BlockSpec((2, HG_WIDTH), lambda i: (0, 0))],
        out_shape=[jax.ShapeDtypeStruct(dproj.shape, BF16), jax.ShapeDtypeStruct((2, HG_WIDTH), F32)],
        input_output_aliases={10: 0},
        scratch_shapes=[pltpu.VMEM((HEADS // HG_G, GW, GW), F32)],
        compiler_params=_params(("arbitrary",)),
        name="hgrn_bwd",
    )(proj, proj, proj, lbl, do, sprev, mcum, mrev, msum, bd, dproj)


def _tail(x, tgt, proj, attn, o, w_a, w_b, w_out, w_at, w_bt, w_outt, b_gate, g_post, gh):
    s = x.shape[0]
    tm = 256
    ones64 = (jnp.arange(HG_WIDTH)[:, None] // 64 == jnp.arange(HG_WIDTH)[None, :] // 64).astype(BF16)
    weights = (w_a, w_b, w_out, w_at, w_bt, w_outt)

    def body(x_ref, t_ref, ml_ref, ga_ref, gb_ref, at_ref, o_ref, *rest):
        w_hbm, (bg_ref, gp_ref, gh_ref, ones_ref) = rest[:6], rest[6:10]
        (dout_ref, dpj_ref, dop_ref, do_ref, mt_ref, dy_ref, yat_ref, dya_ref, ybt_ref, dyb_ref,
         loss_ref, dgp_ref, dbg_ref, dgh_ref) = rest[10:24]
        (wa_ref, wb_ref, wo_ref, wat_ref, wbt_ref, wot_ref), w_sem = rest[24:30], rest[30]

        @pl.when(pl.program_id(0) == 0)
        def _():
            loads = [pltpu.make_async_copy(src, dst, w_sem.at[k])
                     for k, (src, dst) in enumerate(zip(w_hbm, rest[24:30]))]
            _start_all(loads, [])
            loss_ref[...] = jnp.zeros_like(loss_ref)
            dgp_ref[...] = jnp.zeros_like(dgp_ref)
            dbg_ref[...] = jnp.zeros_like(dbg_ref)
            dgh_ref[...] = jnp.zeros_like(dgh_ref)
            _wait_all(loads, [])

        ones = ones_ref[...]
        gate_a = ga_ref[...]
        sa = _sigmoid(gate_a)
        silu_a = gate_a * sa
        attn_v = at_ref[...]
        ya_in = attn_v * silu_a
        ov = o_ref[...]
        ro = lax.rsqrt(_sel_right(ov * ov, ones) * (1.0 / 64.0) + EPS)
        ohat = ov * ro
        ghv = gh_ref[...]
        on = ohat * ghv
        gate_b = gb_ref[...]
        sb = _sigmoid(gate_b)
        silu_b = gate_b * sb
        yb_in = on * silu_b
        ya_bf = ya_in.astype(BF16)
        yb_bf = yb_in.astype(BF16)
        yat_ref[...] = ya_bf.T
        ybt_ref[...] = yb_bf.T
        y_a = _dot(ya_bf, wa_ref[...])
        y_b = _dot(yb_bf, wb_ref[...])
        gts = _sigmoid(ml_ref[...] + bg_ref[...])
        g_a = gts[:, :D_MODEL]
        g_b = gts[:, D_MODEL:]
        m_bf = (g_a * y_a + g_b * y_b).astype(BF16)
        mt_ref[...] = m_bf.T
        y = _dot(m_bf, wo_ref[...])
        r1 = lax.rsqrt(jnp.mean(y * y, axis=-1, keepdims=True) + EPS)
        yn = y * r1
        gp = gp_ref[...]
        e = x_ref[...] + yn * gp - t_ref[...]
        loss_ref[...] += jnp.sum(e * e, axis=0, keepdims=True)
        dout = e * (1.0 / D_MODEL)
        dout_ref[...] = dout
        dgp_ref[...] += jnp.sum(dout * yn, axis=0, keepdims=True)
        dyn = dout * gp
        dy = r1 * (dyn - yn * jnp.mean(dyn * yn, axis=-1, keepdims=True))
        dy_bf = dy.astype(BF16)
        dy_ref[...] = dy_bf
        dm = _dot(dy_bf, wot_ref[...])
        dml_a = dm * y_a * g_a * (1.0 - g_a)
        dml_b = dm * y_b * g_b * (1.0 - g_b)
        dpj_ref[:, :D_MODEL] = dml_a.astype(BF16)
        dpj_ref[:, D_MODEL:2 * D_MODEL] = dml_b.astype(BF16)
        dbg_ref[:, :D_MODEL] += jnp.sum(dml_a, axis=0, keepdims=True)
        dbg_ref[:, D_MODEL:] += jnp.sum(dml_b, axis=0, keepdims=True)
        dya_bf = (dm * g_a).astype(BF16)
        dyb_bf = (dm * g_b).astype(BF16)
        dya_ref[...] = dya_bf
        dyb_ref[...] = dyb_bf
        dya_in = _dot(dya_bf, wat_ref[...])
        dyb_in = _dot(dyb_bf, wbt_ref[...])
        dattn = dya_in * silu_a
        delta = _sel_right(dattn * attn_v, ones)
        lane = lax.broadcasted_iota(jnp.int32, (tm, LANE), 1)
        for p in range(HEADS // 2):
            sl = slice(LANE * p, LANE * (p + 1))
            xs = (dattn[:, sl], pltpu.roll(dattn[:, sl], VDIM, 1))
            nds = (-pltpu.roll(delta[:, sl], VDIM, 1), -delta[:, sl])
            for a in range(2):
                hi, lo_part = _hi_lo(nds[a])
                blk = jnp.where(lane < VDIM, xs[a], jnp.where(lane == VDIM, hi, jnp.where(lane == VDIM + 1, lo_part, 0.0)))
                dop_ref[:, LANE * (2 * p + a):LANE * (2 * p + a + 1)] = blk.astype(BF16)
        dpj_ref[:, 2 * D_MODEL:2 * D_MODEL + HG_WIDTH] = (
            dya_in * attn_v * (sa * (1.0 + gate_a * (1.0 - sa)))).astype(BF16)
        don = dyb_in * silu_b
        dpj_ref[:, 2 * D_MODEL + HG_WIDTH:] = (dyb_in * on * (sb * (1.0 + gate_b * (1.0 - sb)))).astype(BF16)
        dgh_ref[...] += jnp.sum(don * ohat, axis=0, keepdims=True)
        dohat = don * ghv
        do_ref[...] = ro * (dohat - ohat * (_sel_right(dohat * ohat, ones) * (1.0 / 64.0)))

    row = lambda w, j: pl.BlockSpec((tm, w), lambda i: (i, j))
    col = lambda w: pl.BlockSpec((w, tm), lambda i: (0, i))
    full = lambda a: pl.BlockSpec(a.shape, lambda i: (0, 0))
    acc = lambda w: pl.BlockSpec((1, w), lambda i: (0, 0))
    sds = lambda w, dt: jax.ShapeDtypeStruct((s, w), dt)
    sdt = lambda w: jax.ShapeDtypeStruct((w, s), BF16)
    return pl.pallas_call(
        body,
        grid=(s // tm,),
        in_specs=[row(1024, 0), row(1024, 0), row(2048, 0), row(512, 4), row(512, 5), row(512, 0), row(512, 0)]
        + [ANY] * 6 + [full(b_gate), full(g_post), full(gh), full(ones64)],
        out_specs=[row(1024, 0), row(3072, 0), row(1024, 0), row(512, 0),
                   col(1024), row(1024, 0), col(512), row(1024, 0), col(512), row(1024, 0),
                   acc(1024), acc(1024), acc(2048), acc(512)],
        out_shape=[sds(1024, F32), sds(D_IN_PAD, BF16), sds(1024, BF16), sds(512, F32),
                   sdt(1024), sds(1024, BF16), sdt(512), sds(1024, BF16), sdt(512), sds(1024, BF16),
                   jax.ShapeDtypeStruct((1, 1024), F32), jax.ShapeDtypeStruct((1, 1024), F32),
                   jax.ShapeDtypeStruct((1, 2048), F32), jax.ShapeDtypeStruct((1, 512), F32)],
        scratch_shapes=[pltpu.VMEM(a.shape, BF16) for a in weights] + [pltpu.SemaphoreType.DMA((6,))],
        compiler_params=_params(("arbitrary",), 56),
        name="tail",
    )(x, tgt, proj, proj, proj, attn, o, *weights, b_gate, g_post, gh, ones64)


def _mla_bwd(proj, dqr, dkr, dv, g_q, g_kv, w_uq_pt, w_kv_pt, rc, rs1, rs2, dproj):
    s = proj.shape[0]
    tm = 256
    scale = 1.0 / math.sqrt(QK)

    def body(cq_ref, ckv_ref, dqr_ref, dkr_ref, dv_ref, gq_ref, gkv_ref, wuqt_ref, wkvt_ref, c_ref, s1_ref, s2_ref,
             dproj_in, dqf_ref, dkvf_ref, dc_ref, dgq_ref, dgkv_ref):
        del dproj_in

        @pl.when(pl.program_id(0) == 0)
        def _():
            dgq_ref[...] = jnp.zeros_like(dgq_ref)
            dgkv_ref[...] = jnp.zeros_like(dgkv_ref)

        c, s1, s2 = c_ref[...], s1_ref[...], s2_ref[...]
        lane = lax.broadcasted_iota(jnp.int32, (tm, LANE), 1)
        ksum = jnp.zeros((tm, LANE), F32)
        for h in range(HEADS):
            sl = slice(LANE * h, LANE * (h + 1))
            dqf_ref[:, sl] = (_unrope(dqr_ref[:, sl], c, s1, s2) * scale).astype(BF16)
            dkh = dkr_ref[:, sl]
            ksum = ksum + dkh
            dkvf_ref[:, sl] = jnp.where(lane < NOPE, dkh, 0.0).astype(BF16)
            dkvf_ref[:, HEADS * LANE + LANE * h:HEADS * LANE + LANE * (h + 1)] = jnp.where(
                lane < VDIM, dv_ref[:, sl], 0.0).astype(BF16)
        dkpe = _unrope(ksum, c, s1, s2)
        dc_ref[:, Q_LORA + KV_LORA:] = jnp.where((lane >= NOPE) & (lane < QK), dkpe, 0.0).astype(BF16)
        dcqn = _dot(dqf_ref[...], wuqt_ref[...])
        dckvn = _dot(dkvf_ref[...], wkvt_ref[...])
        for x_ref, g_ref, dn, cols, dg_ref in ((cq_ref, gq_ref, dcqn, slice(0, Q_LORA), dgq_ref),
                                               (ckv_ref, gkv_ref, dckvn, slice(Q_LORA, Q_LORA + KV_LORA), dgkv_ref)):
            xv = x_ref[...]
            r = lax.rsqrt(jnp.mean(xv * xv, axis=-1, keepdims=True) + EPS)
            xh = xv * r
            dg_ref[...] += jnp.sum(dn * xh, axis=0, keepdims=True)
            dh = dn * g_ref[...]
            dc_ref[:, cols] = (r * (dh - xh * jnp.mean(dh * xh, axis=-1, keepdims=True))).astype(BF16)

    row = lambda w, j: pl.BlockSpec((tm, w), lambda i: (i, j))
    full = lambda a: pl.BlockSpec(a.shape, lambda i: (0, 0))
    acc = lambda w: pl.BlockSpec((1, w), lambda i: (0, 0))
    sds = lambda w, dt: jax.ShapeDtypeStruct((s, w), dt)
    return pl.pallas_call(
        body,
        grid=(s // tm,),
        in_specs=[row(768, 6), row(256, 21), row(1024, 0), row(1024, 0), row(1024, 0), full(g_q), full(g_kv),
                  full(w_uq_pt), full(w_kv_pt), row(128, 0), row(128, 0), row(128, 0),
                  pl.BlockSpec(memory_space=pl.ANY)],
        out_specs=[row(1024, 0), row(2048, 0), row(1152, 4), acc(768), acc(256)],
        out_shape=[sds(1024, BF16), sds(2048, BF16), jax.ShapeDtypeStruct(dproj.shape, BF16),
                   jax.ShapeDtypeStruct((1, 768), F32), jax.ShapeDtypeStruct((1, 256), F32)],
        input_output_aliases={12: 2},
        compiler_params=_params(("arbitrary",)),
        name="mla_bwd",
    )(proj, proj, dqr, dkr, dv, g_q, g_kv, w_uq_pt, w_kv_pt, rc, rs1, rs2, dproj)


def _pick(n, options):
    for o in options:
        if n % o == 0:
            return o
    raise ValueError(n)


def _matmul(a, b, name):
    m, k = a.shape
    n = b.shape[1]
    tm = _pick(m, (1024, 768, 512, 256))
    tn = _pick(n, (1152, 1024, 768, 512))
    tk = _pick(k, (1024, 512))
    nk = k // tk

    def body(a_ref, b_ref, o_ref):
        @pl.when(pl.program_id(2) == 0)
        def _():
            o_ref[...] = jnp.zeros_like(o_ref)

        o_ref[...] += _dot(a_ref[...], b_ref[...])

    return pl.pallas_call(
        body,
        grid=(m // tm, n // tn, nk),
        in_specs=[pl.BlockSpec((tm, tk), lambda i, j, l: (i, l)), pl.BlockSpec((tk, tn), lambda i, j, l: (l, j))],
        out_specs=pl.BlockSpec((tm, tn), lambda i, j, l: (i, j)),
        out_shape=jax.ShapeDtypeStruct((m, n), F32),
        compiler_params=_params(("arbitrary", "arbitrary", "arbitrary")),
        name=name,
    )(a, b)


def _dh_dx(dproj, w_in_pt, x, dout, g_pre, sends):
    s, k = dproj.shape
    tm = 256
    ns, ni = len(sends), s // tm

    def body(dp_ref, w_ref, x_ref, dout_ref, g_ref, *rest):
        send_refs, (dx_ref, dg_ref) = rest[:ns], rest[ns:ns + 2]
        recv_refs, sems = rest[ns + 2:2 * ns + 2], rest[2 * ns + 2:]

        @pl.when(pl.program_id(0) == 0)
        def _():
            _start_all(*_to_chips_copies(send_refs, recv_refs, sems))
            dg_ref[...] = jnp.zeros_like(dg_ref)

        dh = _dot(dp_ref[...], w_ref[...])
        xv = x_ref[...]
        r = lax.rsqrt(jnp.mean(xv * xv, axis=-1, keepdims=True) + EPS)
        xh = xv * r
        dg_ref[...] += jnp.sum(dh * xh, axis=0, keepdims=True)
        dxh = dh * g_ref[...]
        dx_ref[...] = dout_ref[...] + r * (dxh - xh * jnp.mean(dxh * xh, axis=-1, keepdims=True))

        @pl.when(pl.program_id(0) == ni - 1)
        def _():
            _wait_all(*_to_chips_copies(send_refs, recv_refs, sems))

    row = lambda w: pl.BlockSpec((tm, w), lambda i: (i, 0))
    return pl.pallas_call(
        body,
        grid=(ni,),
        in_specs=[row(k), pl.BlockSpec((k, D_MODEL), lambda i: (0, 0)), row(D_MODEL), row(D_MODEL),
                  pl.BlockSpec((1, D_MODEL), lambda i: (0, 0))] + [ANY] * ns,
        out_specs=[row(D_MODEL), pl.BlockSpec((1, D_MODEL), lambda i: (0, 0))] + [ANY] * ns,
        out_shape=[jax.ShapeDtypeStruct((s, D_MODEL), F32), jax.ShapeDtypeStruct((1, D_MODEL), F32)]
        + [jax.ShapeDtypeStruct(a.shape, a.dtype) for a in sends],
        scratch_shapes=_copy_sems(ns, 3),
        compiler_params=_params(("arbitrary",)),
        name="dh_dx",
    )(dproj, w_in_pt, x, dout, g_pre, *sends)


def _pair_reduce(slots):
    n = len(slots)
    half = [(N_DEV // 2,) + a.shape[1:] for a in slots]

    def body(*refs):
        s_refs, o_refs = refs[:n], refs[n:2 * n]
        mine, got = refs[2 * n:3 * n], refs[3 * n:4 * n]
        send_sems, recv_sems, local_sems = refs[4 * n:]
        x, y, c = _my_place()
        copies, loads = [], []
        for a in range(n):
            for q in range(N_DEV // 2):
                copies.append(pltpu.make_async_remote_copy(
                    src_ref=s_refs[a].at[2 * q + 1 - c], dst_ref=got[a].at[q],
                    send_sem=send_sems.at[4 * a + q], recv_sem=recv_sems.at[4 * a + q],
                    device_id=(x, y, 1 - c), device_id_type=MESH_ID))
                loads.append(pltpu.make_async_copy(s_refs[a].at[2 * q + c], mine[a].at[q], local_sems.at[4 * a + q]))
        _start_all(loads, copies)
        _wait_all(loads, copies)
        for a in range(n):
            o_refs[a][...] = (mine[a][...].astype(F32) + got[a][...].astype(F32)).astype(o_refs[a].dtype)

    vm = lambda: [pltpu.VMEM(h, a.dtype) for h, a in zip(half, slots)]
    return pl.pallas_call(
        body,
        in_specs=[ANY] * n,
        out_shape=[jax.ShapeDtypeStruct(h, a.dtype) for h, a in zip(half, slots)],
        scratch_shapes=vm() + vm() + [pltpu.SemaphoreType.DMA((4 * n,)), pltpu.SemaphoreType.DMA((4 * n,)),
                                      pltpu.SemaphoreType.DMA((4 * n,))],
        compiler_params=pltpu.CompilerParams(vmem_limit_bytes=48 * 2**20),
        name="pair_reduce",
    )(*slots)


def _rope_tables(s):
    inv = (np.float32(ROPE_THETA) ** (-np.arange(0, ROPE, 2, dtype=np.float32) / np.float32(ROPE))).astype(np.float32)
    ang = (np.arange(s, dtype=np.float32)[:, None] * inv[None, :]).astype(np.float32)
    cos, sin = jnp.asarray(np.cos(ang.astype(np.float64)), F32), jnp.asarray(np.sin(ang.astype(np.float64)), F32)
    z = lambda w: jnp.zeros((s, w), F32)
    rc = jnp.concatenate([jnp.ones((s, NOPE), F32), cos, cos, z(32)], axis=1)
    rs1 = jnp.concatenate([z(NOPE), -sin, z(16), z(32)], axis=1)
    rs2 = jnp.concatenate([z(NOPE), z(16), sin, z(32)], axis=1)
    return rc, rs1, rs2


def _step(x, tgt, w_blk, shards, g_pre, b_gate, g_q, g_kv, lbl, g_hgrn, g_post):
    s = x.shape[0]
    rc, rs1, rs2 = _rope_tables(s)
    gh = jnp.tile(g_hgrn, (1, HEADS))

    proj, w_in_pt, ht, *got = _gather_proj(x, g_pre, w_blk, shards[:2])
    w_uq, w_ukv = (_from_slots(n, g) for n, g in zip(MATS[:2], got))
    w_uq_p = jnp.pad(w_uq.reshape(Q_LORA, HEADS, QK), ((0, 0), (0, 0), (0, LANE - QK))).reshape(Q_LORA, HEADS * LANE)
    kv3 = w_ukv.reshape(KV_LORA, HEADS, NOPE + VDIM)
    pad64 = lambda t: jnp.pad(t, ((0, 0), (0, 0), (0, LANE - 64))).reshape(KV_LORA, HEADS * LANE)
    w_kv_p = jnp.concatenate([pad64(kv3[:, :, :NOPE]), pad64(kv3[:, :, NOPE:])], axis=1)

    qr, kr, v, cqt, ckvt = _mla_prep(proj, g_q, g_kv, w_uq_p, w_kv_p, rc, rs1, rs2)
    attn, qa, *got = _attn_fwd(qr, kr, v, shards[2:])
    w_a, w_b, w_out = (_from_slots(n, g) for n, g in zip(MATS[2:], got))
    o, sprev = _hgrn_fwd(proj, lbl)
    (dout, dproj, dop, do, mt, dy_bf, yat, dya_bf, ybt, dyb_bf,
     loss_vec, dg_post, db_gate, dgh) = _tail(x, tgt, proj, attn, o, w_a, w_b, w_out, w_a.T, w_b.T, w_out.T,
                                               b_gate, g_post, gh)
    early = [_to_slots(n, _matmul(a, b, "d" + n)).astype(BF16)
             for n, a, b in (("w_branch_a", yat, dya_bf), ("w_branch_b", ybt, dyb_bf), ("w_out", mt, dy_bf))]
    dqr, dkr, dv, *early_recv = _attn_bwd(qa, kr, v, dop, early)
    dproj, dlbl = _hgrn_bwd(proj, lbl, do, sprev, dproj)
    dqf, dkvf, dproj, dg_q, dg_kv = _mla_bwd(proj, dqr, dkr, dv, g_q, g_kv, w_uq_p.T, w_kv_p.T, rc, rs1, rs2, dproj)

    dw_in_slots = _scatter_w_in(_matmul(ht, dproj, "dw_in"))
    dw_uq_p = _matmul(cqt, dqf, "dw_uq")
    dw_kv_p = _matmul(ckvt, dkvf, "dw_kv")
    dw_uq = dw_uq_p.reshape(Q_LORA, HEADS, LANE)[:, :, :QK].reshape(Q_LORA, HEADS * QK)
    dw_ukv = jnp.concatenate([dw_kv_p[:, :HEADS * LANE].reshape(KV_LORA, HEADS, LANE)[:, :, :NOPE],
                              dw_kv_p[:, HEADS * LANE:].reshape(KV_LORA, HEADS, LANE)[:, :, :VDIM]],
                             axis=2).reshape(KV_LORA, 1024)
    late = _pair_reduce([dw_in_slots, _to_slots("w_uq", dw_uq).astype(BF16), _to_slots("w_ukv", dw_ukv).astype(BF16)])
    dx, dg_pre, *late_recv = _dh_dx(dproj, w_in_pt, x, dout, g_pre, late)

    g_sum = _vectors_sum(dg_pre, db_gate, dg_q, dg_kv, dlbl, dgh, dg_post, loss_vec)
    return dx, late_recv[0], dict(zip(MATS, late_recv[1:] + early_recv)), g_sum


def _all_gather(blocks):
    n = len(blocks)

    def body(*refs):
        x_refs, out_refs = refs[:n], refs[n:2 * n]
        send_sems, recv_sems, local_sems = refs[2 * n:]
        x, y, c = _my_place()
        me, sibling = (x, y, c), (x, y, 1 - c)
        chips = [(1 - x, y), (x, 1 - y), (1 - x, 1 - y)]

        def slot(a, px, py, pc):
            return out_refs[a].at[4 * px + 2 * py + pc]

        def copy(a, k, blk, to, src=None):
            return pltpu.make_async_remote_copy(
                src_ref=slot(a, *blk) if src is None else src, dst_ref=slot(a, *blk),
                send_sem=send_sems.at[7 * a + k], recv_sem=recv_sems.at[7 * a + k],
                device_id=to, device_id_type=MESH_ID)

        mine = [pltpu.make_async_copy(x_refs[a], slot(a, *me), local_sems.at[a]) for a in range(n)]
        for cp in mine:
            cp.start()
        first = [copy(a, 0, me, sibling, src=x_refs[a]) for a in range(n)]
        first += [copy(a, 1 + j, me, (*chip, c), src=x_refs[a]) for a in range(n) for j, chip in enumerate(chips)]
        for cp in first:
            cp.start()
        passed = []
        for j, chip in enumerate(chips):
            for a in range(n):
                copy(a, 1 + j, (*chip, c), me).wait_recv()
                passed.append(copy(a, 4 + j, (*chip, c), sibling))
                passed[-1].start()
        for a in range(n):
            copy(a, 0, sibling, me).wait_recv()
        for j, chip in enumerate(chips):
            for a in range(n):
                copy(a, 4 + j, (*chip, 1 - c), me).wait_recv()
        for cp in first + passed:
            cp.wait_send()
        for cp in mine:
            cp.wait()

    return pl.pallas_call(
        body,
        out_shape=[jax.ShapeDtypeStruct((N_DEV,) + b.shape, b.dtype) for b in blocks],
        in_specs=[pl.BlockSpec(memory_space=pl.ANY)] * n,
        out_specs=[pl.BlockSpec(memory_space=pl.ANY)] * n,
        scratch_shapes=[pltpu.SemaphoreType.DMA((7 * n,)), pltpu.SemaphoreType.DMA((7 * n,)),
                        pltpu.SemaphoreType.DMA((n,))],
        name="gather_weights",
    )(*blocks)


def _adamw(g, w, m, v):
    c1 = 1.0 / (1.0 - ADAM_B1 ** ADAM_STEP)
    c2 = 1.0 / (1.0 - ADAM_B2 ** ADAM_STEP)
    nm = ADAM_B1 * m + (1.0 - ADAM_B1) * g
    nv = ADAM_B2 * v + (1.0 - ADAM_B2) * (g * g)
    d = -ADAM_LR * ((nm * c1) / (jnp.sqrt(nv * c2) + ADAM_EPS) + ADAM_WD * w)
    return d, nm, nv


def _sum8(r_ref):
    g = r_ref[0].astype(F32)
    for k in range(1, r_ref.shape[0]):
        g = g + r_ref[k].astype(F32)
    return g


def _sum_adamw_w_in(recv, w, m, v):
    rows, _, cols = w.shape
    tc = 256

    def body(r_ref, w_ref, m_ref, v_ref, g_ref, d_ref, nm_ref, nv_ref):
        g = _sum8(r_ref)
        dense = lambda ref: ref[...].reshape(rows, tc)
        d, nm, nv = _adamw(g, dense(w_ref), dense(m_ref), dense(v_ref))
        for ref, val in ((g_ref, g), (d_ref, d), (nm_ref, nm), (nv_ref, nv)):
            ref[...] = val.reshape(rows, 1, tc)

    blk = pl.BlockSpec((rows, 1, tc), lambda i: (0, 0, i))
    out = jax.ShapeDtypeStruct((rows, 1, cols), F32)
    return pl.pallas_call(
        body,
        grid=(cols // tc,),
        in_specs=[pl.BlockSpec((recv.shape[0], rows, tc), lambda i: (0, 0, i)), blk, blk, blk],
        out_specs=[blk, blk, blk, blk],
        out_shape=[out, out, out, out],
        compiler_params=_params(("arbitrary",)),
        name="sum_adamw_w_in",
    )(recv, w, m, v)


def _sum_adamw_whole(recvs, ws, ms, vs):
    n = len(ws)

    def body(*refs):
        r_refs, w_refs, m_refs, v_refs = refs[:n], refs[n:2 * n], refs[2 * n:3 * n], refs[3 * n:4 * n]
        outs = refs[4 * n:]
        for a in range(n):
            g = _sum8(r_refs[a])
            d, nm, nv = _adamw(g, w_refs[a][...], m_refs[a][...], v_refs[a][...])
            outs[a][...] = g
            outs[n + a][...] = d
            outs[2 * n + a][...] = nm
            outs[3 * n + a][...] = nv

    shapes = [jax.ShapeDtypeStruct(w.shape, F32) for w in ws]
    res = pl.pallas_call(
        body,
        out_shape=shapes * 4,
        compiler_params=pltpu.CompilerParams(vmem_limit_bytes=48 * 2**20),
        name="sum_adamw_mats",
    )(*recvs, *ws, *ms, *vs)
    return res[:n], res[n:2 * n], res[2 * n:3 * n], res[3 * n:]


SMALL = ("g_pre", "b_gate", "g_q", "g_kv", "lb_logits", "g_hgrn", "g_post")
SMALL_SHAPE = dict(g_pre=(1, 1024), b_gate=(1, 2048), g_q=(1, 768), g_kv=(1, 256), lb_logits=(2, 512),
                   g_hgrn=(1, 64), g_post=(1, 1024))


def _vectors_sum(dg_pre, db_gate, dg_q, dg_kv, dlbl, dgh, dg_post, loss_vec):
    def body(gpre_ref, bg_ref, gq_ref, gkv_ref, lbl_ref, gh_ref, gpost_ref, loss_ref, out_ref, mine, got,
             send_sems, recv_sems):
        mine[...] = jnp.zeros_like(mine)
        mine[0:1, :] = gpre_ref[...]
        mine[1:2, :] = bg_ref[:, :1024]
        mine[2:3, :] = bg_ref[:, 1024:]
        mine[3:4, :Q_LORA] = gq_ref[...]
        mine[4:5, :KV_LORA] = gkv_ref[...]
        loss = (0.5 / D_MODEL) * jnp.sum(loss_ref[...], axis=-1, keepdims=True)
        mine[4:5, KV_LORA:] = jnp.broadcast_to(loss, (1, 1024 - KV_LORA))
        mine[5:6, :HG_WIDTH] = lbl_ref[0:1, :]
        mine[5:6, HG_WIDTH:] = lbl_ref[1:2, :]
        gh = gh_ref[...]
        fold = gh[:, :VDIM]
        for h in range(1, HEADS):
            fold = fold + gh[:, VDIM * h:VDIM * (h + 1)]
        mine[6:7, :VDIM] = fold
        mine[7:8, :] = gpost_ref[...]
        x, y, c = _my_place()
        me = 4 * x + 2 * y + c
        got[me] = mine[...]
        copies = [pltpu.make_async_remote_copy(
            src_ref=mine, dst_ref=got.at[me], send_sem=send_sems.at[k], recv_sem=recv_sems.at[k],
            device_id=_flip(k, x, y, c), device_id_type=MESH_ID) for k in range(N_DEV - 1)]
        _start_all([], copies)
        _wait_all([], copies)
        out_ref[...] = _sum8(got)

    return pl.pallas_call(
        body,
        out_shape=jax.ShapeDtypeStruct((8, 1024), F32),
        scratch_shapes=[pltpu.VMEM((8, 1024), F32), pltpu.VMEM((N_DEV, 8, 1024), F32),
                        pltpu.SemaphoreType.DMA((7,)), pltpu.SemaphoreType.DMA((7,))],
        name="vectors_sum",
    )(dg_pre, db_gate, dg_q, dg_kv, dlbl, dgh, dg_post, loss_vec)


def _vectors_adamw(g_sum, ws, ms, vs):
    n = len(SMALL)

    def body(g_ref, *refs):
        w_refs, m_refs, v_refs = refs[:n], refs[n:2 * n], refs[2 * n:3 * n]
        loss_ref, outs = refs[3 * n], refs[3 * n + 1:]
        g = g_ref[...]
        loss_ref[...] = g[4:5, KV_LORA:KV_LORA + 1]
        grads = (g[0:1, :], jnp.concatenate([g[1:2, :], g[2:3, :]], axis=1), g[3:4, :Q_LORA], g[4:5, :KV_LORA],
                 jnp.concatenate([g[5:6, :HG_WIDTH], g[5:6, HG_WIDTH:]], axis=0), g[6:7, :VDIM], g[7:8, :])
        for a in range(n):
            d, nm, nv = _adamw(grads[a], w_refs[a][...], m_refs[a][...], v_refs[a][...])
            outs[a][...] = grads[a]
            outs[n + a][...] = d
            outs[2 * n + a][...] = nm
            outs[3 * n + a][...] = nv

    shapes = [jax.ShapeDtypeStruct(SMALL_SHAPE[k], F32) for k in SMALL]
    res = pl.pallas_call(
        body,
        out_shape=[jax.ShapeDtypeStruct((1, 1), F32)] + shapes * 4,
        name="vectors_adamw",
    )(g_sum, *ws, *ms, *vs)
    return res[0], res[1:n + 1], res[n + 1:2 * n + 1], res[2 * n + 1:3 * n + 1], res[3 * n + 1:]


MATS = ("w_uq", "w_ukv", "w_branch_a", "w_branch_b", "w_out")
COL_SHARDED = dict(w_uq=False, w_ukv=True, w_branch_a=True, w_branch_b=True, w_out=False)
ORDER = ("g_pre", "w_in", "b_gate", "g_q", "w_uq", "g_kv", "w_ukv", "lb_logits", "g_hgrn",
         "w_branch_a", "w_branch_b", "w_out", "g_post")


def _to_slots(name, full):
    r, c = full.shape
    if COL_SHARDED[name]:
        return full.reshape(r, N_DEV, c // N_DEV).transpose(1, 0, 2)
    return full.reshape(N_DEV, r // N_DEV, c)


def _from_slots(name, slots):
    _, r, c = slots.shape
    if COL_SHARDED[name]:
        return slots.transpose(1, 0, 2).reshape(r, N_DEV * c)
    return slots.reshape(N_DEV * r, c)


def kernel(x, g_pre, w_in, b_gate, g_q, w_uq, g_kv, w_ukv, lb_logits, g_hgrn, w_branch_a, w_branch_b, w_out, g_post, loss_target, m_g_pre, m_w_in, m_b_gate, m_g_q, m_w_uq, m_g_kv, m_w_ukv, m_lb_logits, m_g_hgrn, m_w_branch_a, m_w_branch_b, m_w_out, m_g_post, v_g_pre, v_w_in, v_b_gate, v_g_q, v_w_uq, v_g_kv, v_w_ukv, v_lb_logits, v_g_hgrn, v_w_branch_a, v_w_branch_b, v_w_out, v_g_post):
    rows3 = lambda a: jnp.transpose(a, (2, 0, 1))
    w = dict(w_in=rows3(w_in), w_uq=w_uq[0], w_ukv=w_ukv[0], w_branch_a=w_branch_a[0], w_branch_b=w_branch_b[0],
             w_out=w_out[0], g_pre=g_pre, b_gate=b_gate, g_q=g_q, g_kv=g_kv, lb_logits=lb_logits, g_hgrn=g_hgrn,
             g_post=g_post)
    mom = dict(w_in=rows3(m_w_in), w_uq=m_w_uq[0], w_ukv=m_w_ukv[0], w_branch_a=m_w_branch_a[0],
               w_branch_b=m_w_branch_b[0], w_out=m_w_out[0], g_pre=m_g_pre, b_gate=m_b_gate, g_q=m_g_q, g_kv=m_g_kv,
               lb_logits=m_lb_logits, g_hgrn=m_g_hgrn, g_post=m_g_post)
    var = dict(w_in=rows3(v_w_in), w_uq=v_w_uq[0], w_ukv=v_w_ukv[0], w_branch_a=v_w_branch_a[0],
               w_branch_b=v_w_branch_b[0], w_out=v_w_out[0], g_pre=v_g_pre, b_gate=v_b_gate, g_q=v_g_q, g_kv=v_g_kv,
               lb_logits=v_lb_logits, g_hgrn=v_g_hgrn, g_post=v_g_post)

    w_blk = w["w_in"].reshape(W_IN_SHARD, D_MODEL).astype(BF16)
    dx, recv_in, recv, g_sum = _step(x[0], loss_target[0], w_blk, [w[n].astype(BF16) for n in MATS],
                                     g_pre, b_gate, g_q, g_kv, lb_logits, g_hgrn, g_post)

    g_in, d_in, m_in, v_in = _sum_adamw_w_in(recv_in, w["w_in"], mom["w_in"], var["w_in"])
    res = _sum_adamw_whole([recv[n] for n in MATS], *([t[n] for n in MATS] for t in (w, mom, var)))
    total, *vec = _vectors_adamw(g_sum, *([t[n] for n in SMALL] for t in (w, mom, var)))

    outs = []
    for mats, vecs, big in zip(res, vec, (g_in, d_in, m_in, v_in)):
        t = {**{n: a[None] for n, a in zip(MATS, mats)}, **dict(zip(SMALL, vecs)),
             "w_in": jnp.transpose(big, (1, 2, 0))}
        outs += [t[n] for n in ORDER]
    return (total.reshape(()), dx[None], *outs)
```

```python
import math

import jax
import jax.numpy as jnp
import numpy as np
from jax import lax
from jax.experimental import pallas as pl
from jax.experimental.pallas import tpu as pltpu

F32, BF16 = jnp.float32, jnp.bfloat16

D_MODEL = 1024
EPS = 1e-6
HEADS = 8
NOPE, ROPE, VDIM = 64, 32, 64
QK = NOPE + ROPE
Q_LORA, KV_LORA = 768, 256
ROPE_THETA = 10000.0
ATT_CHUNK_SHIFT = 6
HG_BLOCK = 32
HG_WIDTH = 512
D_IN = 5664
D_IN_PAD = 5760
W_IN_SHARD = D_IN // 8
N_DEV = 8
LANE = 128

ADAM_LR, ADAM_B1, ADAM_B2, ADAM_EPS, ADAM_WD, ADAM_STEP = 0.001, 0.9, 0.999, 1e-08, 0.01, 10

W_IN_SEGMENTS = ((3616, 5664, 0), (1056, 1568, 2048), (3104, 3616, 2560), (1568, 3104, 3072),
                 (0, 1024, 4608), (1024, 1056, 5696))
W_IN_ZERO = ((5632, 5696), (5728, 5760))

NT = (((1,), (1,)), ((), ()))
TN = (((0,), (0,)), ((), ()))
MESH_ID = pl.DeviceIdType.MESH


def _w_in_pieces():
    out = []
    for lo, hi, dst in W_IN_SEGMENTS:
        c = lo
        while c < hi:
            p = c // W_IN_SHARD
            e = min(hi, (p + 1) * W_IN_SHARD)
            out.append((p, c - p * W_IN_SHARD, e - p * W_IN_SHARD, dst + c - lo))
            c = e
    return out


def _params(sem, vmem_mb=48):
    return pltpu.CompilerParams(dimension_semantics=sem, vmem_limit_bytes=vmem_mb * 2**20)


def _dot(a, b):
    return jnp.dot(a, b, preferred_element_type=F32)


def _dotg(a, b, dims):
    return lax.dot_general(a, b, dims, preferred_element_type=F32)


def _split2(x):
    hi = x.astype(BF16)
    return hi, (x - hi.astype(F32)).astype(BF16)


def _sel_left(m01, x):
    hi, lo = _split2(x)
    return _dot(m01, hi) + _dot(m01, lo)


def _sel_right(x, m01):
    hi, lo = _split2(x)
    return _dot(hi, m01) + _dot(lo, m01)


def _hi_lo(x):
    hi = x.astype(BF16).astype(F32)
    return hi, x - hi


def _sigmoid(x):
    return 0.5 * jnp.tanh(0.5 * x) + 0.5


def _rope(x, c, s1, s2):
    return x * c + pltpu.roll(x, 112, 1) * s1 + pltpu.roll(x, 16, 1) * s2


def _unrope(d, c, s1, s2):
    return d * c + pltpu.roll(d * s1, 16, 1) + pltpu.roll(d * s2, 112, 1)


def _my_place():
    return lax.axis_index("x"), lax.axis_index("y"), lax.axis_index("c")


def _flip(k, x, y, c):
    fx, fy, fc = (k + 1) >> 2 & 1, (k + 1) >> 1 & 1, (k + 1) & 1
    return (1 - x if fx else x), (1 - y if fy else y), (1 - c if fc else c)


def _to_all_copies(s_refs, r_refs, sems, spread):
    send_sems, recv_sems, local_sems = sems
    x, y, c = _my_place()
    me = 4 * x + 2 * y + c
    src = (lambda a, p: s_refs[a]) if spread else (lambda a, p: s_refs[a].at[p])
    local = [pltpu.make_async_copy(src(a, me), r_refs[a].at[me], local_sems.at[a]) for a in range(len(s_refs))]
    remote = []
    for k in range(N_DEV - 1):
        px, py, pc = _flip(k, x, y, c)
        for a in range(len(s_refs)):
            remote.append(pltpu.make_async_remote_copy(
                src_ref=src(a, 4 * px + 2 * py + pc), dst_ref=r_refs[a].at[me],
                send_sem=send_sems.at[7 * a + k], recv_sem=recv_sems.at[7 * a + k],
                device_id=(px, py, pc), device_id_type=MESH_ID))
    return local, remote


def _to_chips_copies(s_refs, r_refs, sems):
    send_sems, recv_sems, local_sems = sems
    x, y, c = _my_place()
    me = 2 * x + y
    local = [pltpu.make_async_copy(s_refs[a].at[me], r_refs[a].at[me], local_sems.at[a]) for a in range(len(s_refs))]
    remote = []
    for k in range(3):
        px = 1 - x if (k + 1) >> 1 & 1 else x
        py = 1 - y if (k + 1) & 1 else y
        for a in range(len(s_refs)):
            remote.append(pltpu.make_async_remote_copy(
                src_ref=s_refs[a].at[2 * px + py], dst_ref=r_refs[a].at[me],
                send_sem=send_sems.at[3 * a + k], recv_sem=recv_sems.at[3 * a + k],
                device_id=(px, py, c), device_id_type=MESH_ID))
    return local, remote


def _chips_relay(s_refs, r_refs, t_refs, sems):
    send_sems, recv_sems, local_sems = sems
    x, y, c = _my_place()
    me = 2 * x + y
    n = len(s_refs)
    via_x = c == 0
    nbr = lambda fx, fy: (jnp.where(fx, 1 - x, x), jnp.where(fy, 1 - y, y), c)
    hop1_to = nbr(via_x, jnp.logical_not(via_x))
    hop2_to = nbr(jnp.logical_not(via_x), via_x)
    diag = 2 * (1 - x) + (1 - y)
    came_from = jnp.where(via_x, 2 * (1 - x) + y, 2 * x + (1 - y))

    def mk(a, k, src, dst, to):
        return pltpu.make_async_remote_copy(src_ref=src, dst_ref=dst, send_sem=send_sems.at[4 * a + k],
                                            recv_sem=recv_sems.at[4 * a + k], device_id=to, device_id_type=MESH_ID)

    local = [pltpu.make_async_copy(s_refs[a].at[me], r_refs[a].at[me], local_sems.at[a]) for a in range(n)]
    direct = [mk(a, 0, s_refs[a].at[2 * (1 - x) + y], r_refs[a].at[me], (1 - x, y, c)) for a in range(n)]
    direct += [mk(a, 1, s_refs[a].at[2 * x + (1 - y)], r_refs[a].at[me], (x, 1 - y, c)) for a in range(n)]
    hop1 = [mk(a, 2, s_refs[a].at[diag], t_refs[a], hop1_to) for a in range(n)]
    hop2 = [mk(a, 3, t_refs[a], r_refs[a].at[came_from], hop2_to) for a in range(n)]
    landed = [mk(a, 3, t_refs[a], r_refs[a].at[diag], hop2_to) for a in range(n)]

    def start():
        _start_all(local, hop1 + direct)

    def relay():
        for cp in hop1:
            cp.wait_recv()
        for cp in hop2:
            cp.start()

    def finish():
        for cp in direct + landed:
            cp.wait_recv()
        for cp in direct + hop1 + hop2:
            cp.wait_send()
        for cp in local:
            cp.wait()

    return start, relay, finish


def _start_all(local, remote):
    for cp in local + remote:
        cp.start()


def _wait_all(local, remote):
    for cp in remote:
        cp.wait_recv()
    for cp in remote:
        cp.wait_send()
    for cp in local:
        cp.wait()


def _copy_sems(n, peers):
    return [pltpu.SemaphoreType.DMA((peers * n,)), pltpu.SemaphoreType.DMA((peers * n,)),
            pltpu.SemaphoreType.DMA((n,))]


ANY = pl.BlockSpec(memory_space=pl.ANY)


def _assemble_w_in(slots):
    tc = 256
    pieces = _w_in_pieces()

    def body(s_ref, w_ref, wt_ref):
        for lo, hi in W_IN_ZERO:
            wt_ref[lo:hi, :] = jnp.zeros((hi - lo, tc), BF16)
        for p, lo, hi, dst in pieces:
            wt_ref[dst:dst + hi - lo, :] = s_ref[p, lo:hi, :]
        w_ref[...] = wt_ref[...].T

    return pl.pallas_call(
        body,
        grid=(D_MODEL // tc,),
        in_specs=[pl.BlockSpec((N_DEV, W_IN_SHARD, tc), lambda i: (0, 0, i))],
        out_specs=[pl.BlockSpec((tc, D_IN_PAD), lambda i: (i, 0)), pl.BlockSpec((D_IN_PAD, tc), lambda i: (0, i))],
        out_shape=[jax.ShapeDtypeStruct((D_MODEL, D_IN_PAD), BF16), jax.ShapeDtypeStruct((D_IN_PAD, D_MODEL), BF16)],
        compiler_params=_params(("arbitrary",)),
        name="assemble_w_in",
    )(slots)


def _scatter_w_in(dw):
    tc = 256
    pieces = _w_in_pieces()

    def body(d_ref, s_ref):
        dt = d_ref[...].T
        for p, lo, hi, dst in pieces:
            s_ref[p, lo:hi, :] = dt[dst:dst + hi - lo, :].astype(BF16)

    return pl.pallas_call(
        body,
        grid=(D_MODEL // tc,),
        in_specs=[pl.BlockSpec((tc, D_IN_PAD), lambda i: (i, 0))],
        out_specs=pl.BlockSpec((N_DEV, W_IN_SHARD, tc), lambda i: (0, 0, i)),
        out_shape=jax.ShapeDtypeStruct((N_DEV, W_IN_SHARD, D_MODEL), BF16),
        compiler_params=_params(("arbitrary",)),
        name="scatter_w_in",
    )(dw)


def _norm_proj(x, g_pre, w, shards):
    s, n = x.shape[0], w.shape[1]
    tm = 512
    ni, ns = s // tm, len(shards)

    def body(x_ref, g_ref, w_hbm, *rest):
        shard_refs, (proj_ref, ht_ref), got_refs = rest[:ns], rest[ns:ns + 2], rest[ns + 2:2 * ns + 2]
        w_ref, w_sem, sems = rest[2 * ns + 2], rest[2 * ns + 3], rest[2 * ns + 4:]
        i = pl.program_id(0)

        @pl.when(i == 0)
        def _():
            load = pltpu.make_async_copy(w_hbm, w_ref, w_sem)
            load.start()
            _start_all(*_to_all_copies(shard_refs, got_refs, sems, True))
            load.wait()

        xv = x_ref[...]
        r = lax.rsqrt(jnp.mean(xv * xv, axis=-1, keepdims=True) + EPS)
        h = (xv * r * g_ref[...]).astype(BF16)
        ht_ref[...] = h.T
        proj_ref[...] = _dot(h, w_ref[...])

        @pl.when(i == ni - 1)
        def _():
            _wait_all(*_to_all_copies(shard_refs, got_refs, sems, True))

    return pl.pallas_call(
        body,
        grid=(ni,),
        in_specs=[pl.BlockSpec((tm, D_MODEL), lambda i: (i, 0)), pl.BlockSpec((1, D_MODEL), lambda i: (0, 0)), ANY]
        + [ANY] * ns,
        out_specs=[pl.BlockSpec((tm, n), lambda i: (i, 0)), pl.BlockSpec((D_MODEL, tm), lambda i: (0, i))] + [ANY] * ns,
        out_shape=[jax.ShapeDtypeStruct((s, n), F32), jax.ShapeDtypeStruct((D_MODEL, s), BF16)]
        + [jax.ShapeDtypeStruct((N_DEV,) + b.shape, b.dtype) for b in shards],
        scratch_shapes=[pltpu.VMEM(w.shape, BF16), pltpu.SemaphoreType.DMA] + _copy_sems(ns, 7),
        compiler_params=_params(("arbitrary",), 56),
        name="norm_proj",
    )(x, g_pre, w, *shards)


GP_TN = 256
GP_COLS = 5888
GP_NT = GP_COLS // GP_TN


def _gp_tile_pieces():
    tiles = [[] for _ in range(GP_NT)]
    for p, lo, hi, dst in _w_in_pieces():
        while lo < hi:
            t = dst // GP_TN
            n = min(hi - lo, (t + 1) * GP_TN - dst)
            tiles[t].append((p, lo, lo + n, dst - t * GP_TN))
            lo, dst = lo + n, dst + n
    return tiles


def _gp_tables():
    pieces = _gp_tile_pieces()
    rank_of = {None: 0, 0: 1, 1: 2, 2: 2, 4: 3, 5: 3, 3: 4, 6: 5}
    order = np.zeros((N_DEV, GP_NT), np.int32)
    waits = np.zeros((N_DEV, GP_NT), np.int32)
    for me in range(N_DEV):
        x, y, c = me >> 2 & 1, me >> 1 & 1, me & 1
        chips = [(1 - x, y), (x, 1 - y), (1 - x, 1 - y)]

        def sem_of(p):
            px, py, pc = p >> 2 & 1, p >> 1 & 1, p & 1
            if (px, py) == (x, y):
                return None if pc == c else 0
            j = chips.index((px, py))
            return 1 + j if pc == c else 4 + j

        needs = [sorted({sem_of(p) for p, _, _, _ in tile} - {None}) for tile in pieces]
        ranks = [max([rank_of[k] for k in ks], default=0) for ks in needs]
        seq = sorted(range(GP_NT), key=lambda t: (ranks[t], t))
        seen = set()
        for step, t in enumerate(seq):
            order[me, step] = t
            new = [k for k in needs[t] if k not in seen]
            for k in new:
                waits[me, step] |= 1 << k
            seen.update(new)
        assert seen == set(range(7)), (me, seen)
    return order, waits


def _gather_proj(x, g_pre, w_blk, shards):
    s = x.shape[0]
    tx = 512
    ns = len(shards)
    tile_pieces = _gp_tile_pieces()
    order_np, waits_np = _gp_tables()
    xq, yq, cq = _my_place()
    me_out = 4 * xq + 2 * yq + cq
    order = lax.dynamic_index_in_dim(jnp.asarray(order_np), me_out, 0, keepdims=False)
    waits = lax.dynamic_index_in_dim(jnp.asarray(waits_np), me_out, 0, keepdims=False)

    def body(order_ref, waits_ref, x_hbm, g_ref, wblk_hbm, *rest):
        shard_refs, (proj_ref, wt_ref, ht_hbm), got_refs = rest[:ns], rest[ns:ns + 3], rest[ns + 3:2 * ns + 3]
        recv, h_ref, wtile, xbuf, htbuf = rest[2 * ns + 3:2 * ns + 8]
        send_sems, recv_sems, misc_sems = rest[2 * ns + 8:2 * ns + 11]
        sems = rest[2 * ns + 11:]
        t = pl.program_id(0)
        x_, y_, c = _my_place()
        sibling = (x_, y_, 1 - c)
        chips = [(1 - x_, y_), (x_, 1 - y_), (1 - x_, 1 - y_)]
        idx = lambda px, py, pc: 4 * px + 2 * py + pc
        me = idx(x_, y_, c)

        def copy(k, slot, to, src=None):
            return pltpu.make_async_remote_copy(
                src_ref=recv.at[slot] if src is None else src, dst_ref=recv.at[slot],
                send_sem=send_sems.at[k], recv_sem=recv_sems.at[k], device_id=to, device_id_type=MESH_ID)

        mine = pltpu.make_async_copy(wblk_hbm, recv.at[me], misc_sems.at[0])
        first = [copy(0, me, sibling, src=wblk_hbm)] + [copy(1 + j, me, (*chips[j], c), src=wblk_hbm) for j in range(2)]
        passed = [copy(4 + j, idx(*ch, c), sibling) for j, ch in enumerate(chips)]
        onward = [copy(3, idx(*chips[0], c), (*chips[1], c)), copy(3, idx(*chips[1], c), (*chips[0], c))]
        arrivals = ([copy(0, idx(x_, y_, 1 - c), sibling)] + [copy(1 + j, idx(*ch, c), sibling) for j, ch in enumerate(chips)]
                    + [copy(4 + j, idx(*ch, 1 - c), sibling) for j, ch in enumerate(chips)])

        @pl.when(t == 0)
        def _():
            mine.start()
            for cp in first:
                cp.start()
            _start_all(*_to_all_copies(shard_refs, got_refs, sems, True))

            def load(i):
                return pltpu.make_async_copy(x_hbm.at[pl.ds(i * tx, tx), :], xbuf.at[i & 1], misc_sems.at[1 + (i & 1)])

            def store(i):
                return pltpu.make_async_copy(htbuf.at[i & 1], ht_hbm.at[:, pl.ds(i * tx, tx)], misc_sems.at[3 + (i & 1)])

            load(0).start()
            for i in range(s // tx):
                if i + 1 < s // tx:
                    load(i + 1).start()
                load(i).wait()
                xv = xbuf[i & 1]
                r = lax.rsqrt(jnp.mean(xv * xv, axis=-1, keepdims=True) + EPS)
                h = (xv * r * g_ref[...]).astype(BF16)
                h_ref[i * tx:(i + 1) * tx, :] = h
                if i >= 2:
                    store(i - 2).wait()
                htbuf[i & 1] = h.T
                store(i).start()
            for i in range(max(s // tx - 2, 0), s // tx):
                store(i).wait()
            mine.wait()

        w = waits_ref[t]
        for k in range(7):
            @pl.when((w >> k) & 1 == 1)
            def _(k=k):
                arrivals[k].wait_recv()
                if 1 <= k <= 3:
                    passed[k - 1].start()
                if 1 <= k <= 2:
                    @pl.when(c == k - 1)
                    def _():
                        onward[k - 1].start()

        tile = order_ref[t]
        for tt in range(GP_NT):
            @pl.when(tile == tt)
            def _(tt=tt):
                covered = sorted((d, d + hi - lo) for _, lo, hi, d in tile_pieces[tt])
                at = 0
                for lo_z, hi_z in covered + [(GP_TN, GP_TN)]:
                    if lo_z > at:
                        wtile[at:lo_z, :] = jnp.zeros((lo_z - at, D_MODEL), BF16)
                    at = max(at, hi_z)
                for p, lo, hi, d in tile_pieces[tt]:
                    wtile[d:d + hi - lo, :] = recv[p, lo:hi, :]

        wt = wtile[...]
        wt_ref[...] = wt
        proj_ref[...] = _dotg(h_ref[...], wt, NT)

        @pl.when(t == GP_NT - 1)
        def _():
            for cp in first + passed + onward[:1]:
                cp.wait_send()
            _wait_all(*_to_all_copies(shard_refs, got_refs, sems, True))

    grid_spec = pltpu.PrefetchScalarGridSpec(
        num_scalar_prefetch=2,
        grid=(GP_NT,),
        in_specs=[ANY, pl.BlockSpec((1, D_MODEL), lambda t, o, w: (0, 0)), ANY] + [ANY] * ns,
        out_specs=[pl.BlockSpec((s, GP_TN), lambda t, o, w: (0, o[t])),
                   pl.BlockSpec((GP_TN, D_MODEL), lambda t, o, w: (o[t], 0)), ANY] + [ANY] * ns,
        scratch_shapes=[pltpu.VMEM((N_DEV, W_IN_SHARD, D_MODEL), BF16), pltpu.VMEM((s, D_MODEL), BF16),
                        pltpu.VMEM((GP_TN, D_MODEL), BF16), pltpu.VMEM((2, tx, D_MODEL), F32),
                        pltpu.VMEM((2, D_MODEL, tx), BF16),
                        pltpu.SemaphoreType.DMA((7,)), pltpu.SemaphoreType.DMA((7,)), pltpu.SemaphoreType.DMA((5,))]
        + _copy_sems(ns, 7),
    )
    return pl.pallas_call(
        body,
        grid_spec=grid_spec,
        out_shape=[jax.ShapeDtypeStruct((s, GP_COLS), F32), jax.ShapeDtypeStruct((GP_COLS, D_MODEL), BF16),
                   jax.ShapeDtypeStruct((D_MODEL, s), BF16)]
        + [jax.ShapeDtypeStruct((N_DEV,) + b.shape, b.dtype) for b in shards],
        compiler_params=_params(("arbitrary",), 56),
        name="gather_proj",
    )(order, waits, x, g_pre, w_blk, *shards)


def _mla_prep(proj, g_q, g_kv, w_uq_p, w_kv_p, rc, rs1, rs2):
    s = proj.shape[0]
    tm = 256
    scale = 1.0 / math.sqrt(QK)

    def body(cq_ref, ckv_ref, kpe_ref, gq_ref, gkv_ref, wuq_ref, wkv_ref, c_ref, s1_ref, s2_ref,
             qr_ref, kr_ref, v_ref, cqt_ref, ckvt_ref):
        cq = cq_ref[...]
        r = lax.rsqrt(jnp.mean(cq * cq, axis=-1, keepdims=True) + EPS)
        cqn = (cq * r * gq_ref[...]).astype(BF16)
        cqt_ref[...] = cqn.T
        q = _dot(cqn, wuq_ref[...])
        ckv = ckv_ref[...]
        r = lax.rsqrt(jnp.mean(ckv * ckv, axis=-1, keepdims=True) + EPS)
        ckvn = (ckv * r * gkv_ref[...]).astype(BF16)
        ckvt_ref[...] = ckvn.T
        kv = _dot(ckvn, wkv_ref[...])
        c, s1, s2 = c_ref[...], s1_ref[...], s2_ref[...]
        lane = lax.broadcasted_iota(jnp.int32, (tm, LANE), 1)
        kpe = _rope(kpe_ref[...], c, s1, s2) + jnp.where((lane == QK) | (lane == QK + 1), 1.0, 0.0)
        vone = jnp.where((lane == VDIM) | (lane == VDIM + 1), 1.0, 0.0)
        for h in range(HEADS):
            sl = slice(LANE * h, LANE * (h + 1))
            qr_ref[:, sl] = (_rope(q[:, sl], c, s1, s2) * scale).astype(BF16)
            kr_ref[:, sl] = (kv[:, sl] + kpe).astype(BF16)
            v_ref[:, sl] = (kv[:, HEADS * LANE + LANE * h:HEADS * LANE + LANE * (h + 1)] + vone).astype(BF16)

    row = lambda w, j: pl.BlockSpec((tm, w), lambda i: (i, j))
    col = lambda w: pl.BlockSpec((w, tm), lambda i: (0, i))
    full = lambda a: pl.BlockSpec(a.shape, lambda i: (0, 0))
    return pl.pallas_call(
        body,
        grid=(s // tm,),
        in_specs=[row(768, 6), row(256, 21), row(128, 44), full(g_q), full(g_kv), full(w_uq_p), full(w_kv_p),
                  row(128, 0), row(128, 0), row(128, 0)],
        out_specs=[row(1024, 0), row(1024, 0), row(1024, 0), col(768), col(256)],
        out_shape=[jax.ShapeDtypeStruct((s, 1024), BF16), jax.ShapeDtypeStruct((s, 1024), BF16),
                   jax.ShapeDtypeStruct((s, 1024), BF16), jax.ShapeDtypeStruct((768, s), BF16),
                   jax.ShapeDtypeStruct((256, s), BF16)],
        compiler_params=_params(("arbitrary",)),
        name="mla_prep",
    )(proj, proj, proj, g_q, g_kv, w_uq_p, w_kv_p, rc, rs1, rs2)


ATT_T = 512
ATT_FWD_HEADS = 4


def _chunk_mask(transposed):
    r = lax.broadcasted_iota(jnp.int32, (ATT_T, ATT_T), 0) >> ATT_CHUNK_SHIFT
    c = lax.broadcasted_iota(jnp.int32, (ATT_T, ATT_T), 1) >> ATT_CHUNK_SHIFT
    return (r <= c) if transposed else (c <= r)


def _attn_fwd(qr, kr, vp, shards):
    s = qr.shape[0]
    t = ATT_T
    g = ATT_FWD_HEADS
    ns = len(shards)

    def body(q_ref, k_ref, v_ref, *rest):
        shard_refs, (o_ref, qa_ref), got_refs = rest[:ns], rest[ns:ns + 2], rest[ns + 2:2 * ns + 2]
        sc_ref, sems = rest[2 * ns + 2], rest[2 * ns + 3:]
        qi = pl.program_id(1)

        @pl.when((pl.program_id(0) == 0) & (qi == 0))
        def _():
            _start_all(*_to_all_copies(shard_refs, got_refs, sems, True))
        lane = lax.broadcasted_iota(jnp.int32, (t, LANE), 1)
        sls = [slice(LANE * a, LANE * (a + 1)) for a in range(g)]
        qs = [q_ref[:, sl] for sl in sls]

        def scores(j):
            rows = pl.ds(pl.multiple_of(j * t, t), t)
            for a in range(g):
                sc_ref[j & 1, a] = _dotg(qs[a], k_ref[rows, sls[a]], NT)

        def step(j, carry, masked):
            rows = pl.ds(pl.multiple_of(j * t, t), t)
            out = []
            for a in range(g):
                m, acc = carry[a]
                sc = sc_ref[j & 1, a]
                if masked:
                    sc = jnp.where(_chunk_mask(False), sc, -1e30)
                m_new = jnp.maximum(m, jnp.max(sc, axis=-1, keepdims=True))
                p = jnp.exp(sc - m_new).astype(BF16)
                acc = jnp.exp(m - m_new) * acc + _dot(p, v_ref[rows, sls[a]])
                out.append((m_new, acc))
            return tuple(out)

        def loop(j, carry):
            carry = step(j, carry, False)
            scores(j + 1)
            return carry

        init = tuple((jnp.full((t, 1), -1e30, F32), jnp.zeros((t, LANE), F32)) for _ in range(g))
        scores(0)
        carry = lax.fori_loop(0, qi, loop, init)
        carry = step(qi, carry, True)
        outs = []
        for a in range(g):
            m, acc = carry[a]
            l = acc[:, VDIM:VDIM + 1]
            outs.append(acc / l)
            hi, lo_part = _hi_lo(-(m + jnp.log(l)))
            qa = jnp.where(lane == QK, hi, jnp.where(lane == QK + 1, lo_part, qs[a].astype(F32)))
            qa_ref[:, sls[a]] = qa.astype(BF16)
        for p in range(g // 2):
            o_ref[:, LANE * p:LANE * (p + 1)] = jnp.where(lane < VDIM, outs[2 * p], pltpu.roll(outs[2 * p + 1], VDIM, 1))

        @pl.when((pl.program_id(0) == HEADS // g - 1) & (qi == s // t - 1))
        def _():
            _wait_all(*_to_all_copies(shard_refs, got_refs, sems, True))

    return pl.pallas_call(
        body,
        grid=(HEADS // g, s // t),
        in_specs=[
            pl.BlockSpec((t, g * LANE), lambda h, i: (i, h)),
            pl.BlockSpec((s, g * LANE), lambda h, i: (0, h)),
            pl.BlockSpec((s, g * LANE), lambda h, i: (0, h)),
        ] + [ANY] * ns,
        out_specs=[
            pl.BlockSpec((t, g * VDIM), lambda h, i: (i, h)),
            pl.BlockSpec((t, g * LANE), lambda h, i: (i, h)),
        ] + [ANY] * ns,
        out_shape=[jax.ShapeDtypeStruct((s, 512), F32), jax.ShapeDtypeStruct((s, 1024), BF16)]
        + [jax.ShapeDtypeStruct((N_DEV,) + b.shape, b.dtype) for b in shards],
        scratch_shapes=[pltpu.VMEM((2, g, t, t), F32)] + _copy_sems(ns, 7),
        compiler_params=_params(("arbitrary", "arbitrary")),
        name="attn_fwd",
    )(qr, kr, vp, *shards)


def _attn_bwd(qa, kr, vp, dop, sends):
    s = qa.shape[0]
    t = ATT_T
    nq = s // t
    ns = len(sends)

    def body(q_ref, k_ref, v_ref, do_ref, *rest):
        send_refs, (dq_out, dk_out, dv_out) = rest[:ns], rest[ns:ns + 3]
        recv_refs = rest[ns + 3:2 * ns + 3]
        (dq_ref, dk_ref, dv_ref), sems = rest[2 * ns + 3:2 * ns + 6], rest[2 * ns + 6:]
        j = pl.program_id(1)
        sls = [slice(LANE * a, LANE * (a + 1)) for a in range(2)]

        @pl.when((pl.program_id(0) == 0) & (j == 0))
        def _():
            _start_all(*_to_all_copies(send_refs, recv_refs, sems, False))

        @pl.when(j == 0)
        def _():
            dq_ref[...] = jnp.zeros_like(dq_ref)

        dk_ref[...] = jnp.zeros_like(dk_ref)
        dv_ref[...] = jnp.zeros_like(dv_ref)
        ks = [k_ref[:, sl] for sl in sls]
        vs = [v_ref[:, sl] for sl in sls]

        def step(i, masked):
            rows = pl.ds(pl.multiple_of(i * t, t), t)
            for a in range(2):
                q = q_ref[rows, sls[a]]
                do = do_ref[rows, sls[a]]
                sc = _dotg(ks[a], q, NT)
                if masked:
                    sc = jnp.where(_chunk_mask(True), sc, -1e30)
                p = jnp.exp(sc)
                ds = (p * _dotg(vs[a], do, NT)).astype(BF16)
                dv_ref[:, sls[a]] += _dot(p.astype(BF16), do)
                dk_ref[:, sls[a]] += _dot(ds, q)
                dq_ref[rows, sls[a]] += _dotg(ds, ks[a], TN)

        step(j, True)

        def loop(i, c):
            step(i, False)
            return c

        lax.fori_loop(j + 1, nq, loop, 0)
        dk_out[...] = dk_ref[...].astype(BF16)
        dv_out[...] = dv_ref[...].astype(BF16)

        @pl.when(j == nq - 1)
        def _():
            dq_out[...] = dq_ref[...].astype(BF16)

        @pl.when((pl.program_id(0) == HEADS // 2 - 1) & (j == nq - 1))
        def _():
            _wait_all(*_to_all_copies(send_refs, recv_refs, sems, False))

    blk = pl.BlockSpec((t, 2 * LANE), lambda h, j: (j, h))
    whole = pl.BlockSpec((s, 2 * LANE), lambda h, j: (0, h))
    out = jax.ShapeDtypeStruct((s, 1024), BF16)
    return pl.pallas_call(
        body,
        grid=(HEADS // 2, nq),
        in_specs=[whole, blk, blk, whole] + [ANY] * ns,
        out_specs=[whole, blk, blk] + [ANY] * ns,
        out_shape=[out, out, out] + [jax.ShapeDtypeStruct(a.shape, a.dtype) for a in sends],
        scratch_shapes=[pltpu.VMEM((s, 2 * LANE), F32), pltpu.VMEM((t, 2 * LANE), F32),
                        pltpu.VMEM((t, 2 * LANE), F32)] + _copy_sems(ns, 7),
        compiler_params=_params(("arbitrary", "arbitrary")),
        name="attn_bwd",
    )(qa, kr, vp, dop, *sends)


HG_T = 256
HG_NC = HG_T // HG_BLOCK
HG_G = 4
GW = 64 * HG_G


def _hg_consts():
    r = jnp.arange(HG_T)[:, None]
    c = jnp.arange(HG_T)[None, :]
    same = (r // HG_BLOCK) == (c // HG_BLOCK)
    mcum = (same & (c <= r)).astype(BF16)
    mrev = (same & (c >= r)).astype(BF16)
    msum = same.astype(BF16)
    a = jnp.arange(GW) // 64
    bd = (a[:, None] == a[None, :]).astype(F32)
    return mcum, mrev, msum, bd


def _stack_heads(xg, head):
    return jnp.concatenate([jnp.where(head == h, xg, 0.0) for h in range(HG_G)], axis=0)


def _unstack_heads(r, head, t):
    out = r[(HG_G - 1) * t:]
    for h in range(HG_G - 2, -1, -1):
        out = jnp.where(head == h, r[h * t:(h + 1) * t], out)
    return out


def _compact_state(st):
    out = st[:64]
    for h in range(1, HG_G):
        out = out + st[64 * h:64 * (h + 1)]
    return out


def _expand_state(cs, head64):
    return jnp.concatenate([jnp.where(head64 == h, cs, 0.0) for h in range(HG_G)], axis=0)


def _hg_pre(hq, hf, lbl, mcum, msum):
    lb = _sigmoid(lbl[0:1, :] - lbl[1:2, :])
    sig = _sigmoid(hf)
    f = lb + (1.0 - lb) * sig
    lf = jnp.log(f)
    b = _sel_left(mcum, lf)
    big_l = _sel_left(msum, lf)
    k = 1.0 - f
    qd = hq * jnp.exp(b)
    ki = k * jnp.exp(-b)
    ke = k * jnp.exp(big_l - b)
    return lb, sig, f, b, big_l, qd, ki, ke


def _hgrn_fwd(proj, lbl):
    s = proj.shape[0]
    t = HG_T
    mcum, _, msum, bd = _hg_consts()

    def body(hq_ref, hf_ref, hi_ref, lbl_ref, mcum_ref, msum_ref, bd_ref, o_ref, sp_ref, st_ref):
        @pl.when(pl.program_id(0) == 0)
        def _():
            st_ref[...] = jnp.zeros_like(st_ref)

        mc = mcum_ref[...]
        _, _, _, _, big_l, qd, ki, ke = _hg_pre(hq_ref[...], hf_ref[...], lbl_ref[...], mc, msum_ref[...])
        el = jnp.exp(big_l)
        hi = hi_ref[...]
        head = lax.broadcasted_iota(jnp.int32, (t, GW), 1) >> 6
        mask = jnp.concatenate([mc] * HG_G, axis=0) > 0.5
        for p in range(HEADS // HG_G):
            sl = slice(GW * p, GW * (p + 1))
            vp = hi[:, sl].astype(BF16)
            qs = _stack_heads(qd[:, sl], head).astype(BF16)
            a = jnp.where(mask, _dotg(qs, ki[:, sl].astype(BF16), NT), 0.0)
            o_intra = _unstack_heads(_dot(a.astype(BF16), vp), head, t)
            qb = qd[:, sl].astype(BF16)
            kb = ke[:, sl].astype(BF16)
            st = st_ref[p]
            for c in range(HG_NC):
                rows = slice(HG_BLOCK * c, HG_BLOCK * (c + 1))
                sp_ref[c, :, sl] = _compact_state(st)
                o_ref[rows, sl] = o_intra[rows] + _dotg(qb[rows], st.astype(BF16), NT)
                u = _dotg(vp[rows], kb[rows], TN) * bd_ref[...]
                st = st * el[HG_BLOCK * c:HG_BLOCK * c + 1, sl] + u
            st_ref[p] = st

    row = lambda j: pl.BlockSpec((t, HG_WIDTH), lambda i: (i, j))
    full = lambda a: pl.BlockSpec(a.shape, lambda i: (0, 0))
    return pl.pallas_call(
        body,
        grid=(s // t,),
        in_specs=[row(6), row(7), row(8), full(lbl), full(mcum), full(msum), full(bd)],
        out_specs=[row(0), pl.BlockSpec((HG_NC, 64, HG_WIDTH), lambda i: (i, 0, 0))],
        out_shape=[jax.ShapeDtypeStruct((s, HG_WIDTH), F32),
                   jax.ShapeDtypeStruct((s // HG_BLOCK, 64, HG_WIDTH), F32)],
        scratch_shapes=[pltpu.VMEM((HEADS // HG_G, GW, GW), F32)],
        compiler_params=_params(("arbitrary",)),
        name="hgrn_fwd",
    )(proj, proj, proj, lbl, mcum, msum, bd)


def _hgrn_bwd(proj, lbl, do, sprev, dproj):
    s = proj.shape[0]
    t = HG_T
    nt = s // t
    mcum, mrev, msum, bd = _hg_consts()

    def body(hq_ref, hf_ref, hi_ref, lbl_ref, do_ref, sp_ref, mcum_ref, mrev_ref, msum_ref, bd_ref,
             dproj_in, dh_ref, dlbl_ref, g_ref):
        del dproj_in

        @pl.when(pl.program_id(0) == 0)
        def _():
            g_ref[...] = jnp.zeros_like(g_ref)
            dlbl_ref[...] = jnp.zeros_like(dlbl_ref)

        mc = mcum_ref[...]
        lb, sig, f, b, big_l, qd, ki, ke = _hg_pre(hq_ref[...], hf_ref[...], lbl_ref[...], mc, msum_ref[...])
        el = jnp.exp(big_l)
        hi = hi_ref[...]
        dov = do_ref[...]
        head = lax.broadcasted_iota(jnp.int32, (t, GW), 1) >> 6
        head64 = lax.broadcasted_iota(jnp.int32, (64, GW), 1) >> 6
        mask = jnp.concatenate([mc] * HG_G, axis=0) > 0.5
        dqd_parts, dke_parts, dv_parts, del_parts, dki_parts = [], [], [], [], []
        for p in range(HEADS // HG_G):
            sl = slice(GW * p, GW * (p + 1))
            vp = hi[:, sl].astype(BF16)
            qs = _stack_heads(qd[:, sl], head).astype(BF16)
            kip = ki[:, sl].astype(BF16)
            dos = _stack_heads(dov[:, sl], head).astype(BF16)
            a = jnp.where(mask, _dotg(qs, kip, NT), 0.0).astype(BF16)
            da = jnp.where(mask, _dotg(dos, vp, NT), 0.0).astype(BF16)
            r = _dot(da, kip)
            dki_parts.append(_dotg(da, qs, TN))
            qb = qd[:, sl].astype(BF16)
            kb = ke[:, sl].astype(BF16)
            dob = dov[:, sl].astype(BF16)
            g = g_ref[p]
            dqd_c, dv_c, dke_c, del_c = [], [], [], []
            for c in range(HG_NC - 1, -1, -1):
                rows = slice(HG_BLOCK * c, HG_BLOCK * (c + 1))
                gb = g.astype(BF16)
                st = _expand_state(sp_ref[c, :, sl], head64)
                dqd_c.append(_dot(dob[rows], st.astype(BF16)))
                dv_c.append(_dotg(kb[rows], gb, NT))
                dke_c.append(_dot(vp[rows], gb))
                del_c.append(jnp.broadcast_to(jnp.sum(g * st, axis=0, keepdims=True), (HG_BLOCK, GW)))
                g = g * el[HG_BLOCK * c:HG_BLOCK * c + 1, sl] + _dotg(dob[rows], qb[rows], TN) * bd_ref[...]
            g_ref[p] = g
            up = lambda parts: jnp.concatenate(parts[::-1], axis=0)
            dqd_parts.append(_unstack_heads(r, head, t) + up(dqd_c))
            dv_parts.append(_dotg(a, dos, TN) + up(dv_c))
            dke_parts.append(up(dke_c))
            del_parts.append(up(del_c))
        wide = lambda parts: jnp.concatenate(parts, axis=1)
        dqd, dke, dki, dvv, del_rows = wide(dqd_parts), wide(dke_parts), wide(dki_parts), wide(dv_parts), wide(del_parts)
        dh_ref[:, :HG_WIDTH] = (dqd * jnp.exp(b)).astype(BF16)
        dh_ref[:, 2 * HG_WIDTH:] = dvv.astype(BF16)
        dke_ke = dke * ke
        db = dqd * qd - dki * ki - dke_ke
        dl_rows = _sel_left(msum_ref[...], dke_ke) + del_rows * el
        is_last = (lax.broadcasted_iota(jnp.int32, (t, HG_WIDTH), 0) & (HG_BLOCK - 1)) == HG_BLOCK - 1
        db = db + jnp.where(is_last, dl_rows, 0.0)
        dlf = _sel_left(mrev_ref[...], db)
        dk = dki * jnp.exp(-b) + dke * jnp.exp(big_l - b)
        df = dlf / f - dk
        dh_ref[:, HG_WIDTH:2 * HG_WIDTH] = (df * (1.0 - lb) * sig * (1.0 - sig)).astype(BF16)
        dlb = jnp.sum(df * (1.0 - sig), axis=0, keepdims=True) * lb * (1.0 - lb)
        dlbl_ref[0:1, :] += dlb
        dlbl_ref[1:2, :] -= dlb

    rrow = lambda j: pl.BlockSpec((t, HG_WIDTH), lambda i: (nt - 1 - i, j))
    full = lambda a: pl.BlockSpec(a.shape, lambda i: (0, 0))
    return pl.pallas_call(
        body,
        grid=(nt,),
        in_specs=[rrow(6), rrow(7), rrow(8), full(lbl), rrow(0),
                  pl.BlockSpec((HG_NC, 64, HG_WIDTH), lambda i: (nt - 1 - i, 0, 0)),
                  full(mcum), full(mrev), full(msum), full(bd), pl.BlockSpec(memory_space=pl.ANY)],
        out_specs=[pl.BlockSpec((t, 3 * HG_WIDTH), lambda i: (nt - 1 - i, 2)),
                   pl.BlockSpec((2, HG_WIDTH), lambda i: (0, 0))],
        out_shape=[jax.ShapeDtypeStruct(dproj.shape, BF16), jax.ShapeDtypeStruct((2, HG_WIDTH), F32)],
        input_output_aliases={10: 0},
        scratch_shapes=[pltpu.VMEM((HEADS // HG_G, GW, GW), F32)],
        compiler_params=_params(("arbitrary",)),
        name="hgrn_bwd",
    )(proj, proj, proj, lbl, do, sprev, mcum, mrev, msum, bd, dproj)


def _tail(x, tgt, proj, attn, o, w_a, w_b, w_out, w_at, w_bt, w_outt, b_gate, g_post, gh):
    s = x.shape[0]
    tm = 256
    ones64 = (jnp.arange(HG_WIDTH)[:, None] // 64 == jnp.arange(HG_WIDTH)[None, :] // 64).astype(BF16)
    weights = (w_a, w_b, w_out, w_at, w_bt, w_outt)

    def body(x_ref, t_ref, ml_ref, ga_ref, gb_ref, at_ref, o_ref, *rest):
        w_hbm, (bg_ref, gp_ref, gh_ref, ones_ref) = rest[:6], rest[6:10]
        (dout_ref, dpj_ref, dop_ref, do_ref, mt_ref, dy_ref, yat_ref, dya_ref, ybt_ref, dyb_ref,
         loss_ref, dgp_ref, dbg_ref, dgh_ref) = rest[10:24]
        (wa_ref, wb_ref, wo_ref, wat_ref, wbt_ref, wot_ref), w_sem = rest[24:30], rest[30]

        @pl.when(pl.program_id(0) == 0)
        def _():
            loads = [pltpu.make_async_copy(src, dst, w_sem.at[k])
                     for k, (src, dst) in enumerate(zip(w_hbm, rest[24:30]))]
            _start_all(loads, [])
            loss_ref[...] = jnp.zeros_like(loss_ref)
            dgp_ref[...] = jnp.zeros_like(dgp_ref)
            dbg_ref[...] = jnp.zeros_like(dbg_ref)
            dgh_ref[...] = jnp.zeros_like(dgh_ref)
            _wait_all(loads, [])

        ones = ones_ref[...]
        gate_a = ga_ref[...]
        sa = _sigmoid(gate_a)
        silu_a = gate_a * sa
        attn_v = at_ref[...]
        ya_in = attn_v * silu_a
        ov = o_ref[...]
        ro = lax.rsqrt(_sel_right(ov * ov, ones) * (1.0 / 64.0) + EPS)
        ohat = ov * ro
        ghv = gh_ref[...]
        on = ohat * ghv
        gate_b = gb_ref[...]
        sb = _sigmoid(gate_b)
        silu_b = gate_b * sb
        yb_in = on * silu_b
        ya_bf = ya_in.astype(BF16)
        yb_bf = yb_in.astype(BF16)
        yat_ref[...] = ya_bf.T
        ybt_ref[...] = yb_bf.T
        y_a = _dot(ya_bf, wa_ref[...])
        y_b = _dot(yb_bf, wb_ref[...])
        gts = _sigmoid(ml_ref[...] + bg_ref[...])
        g_a = gts[:, :D_MODEL]
        g_b = gts[:, D_MODEL:]
        m_bf = (g_a * y_a + g_b * y_b).astype(BF16)
        mt_ref[...] = m_bf.T
        y = _dot(m_bf, wo_ref[...])
        r1 = lax.rsqrt(jnp.mean(y * y, axis=-1, keepdims=True) + EPS)
        yn = y * r1
        gp = gp_ref[...]
        e = x_ref[...] + yn * gp - t_ref[...]
        loss_ref[...] += jnp.sum(e * e, axis=0, keepdims=True)
        dout = e * (1.0 / D_MODEL)
        dout_ref[...] = dout
        dgp_ref[...] += jnp.sum(dout * yn, axis=0, keepdims=True)
        dyn = dout * gp
        dy = r1 * (dyn - yn * jnp.mean(dyn * yn, axis=-1, keepdims=True))
        dy_bf = dy.astype(BF16)
        dy_ref[...] = dy_bf
        dm = _dot(dy_bf, wot_ref[...])
        dml_a = dm * y_a * g_a * (1.0 - g_a)
        dml_b = dm * y_b * g_b * (1.0 - g_b)
        dpj_ref[:, :D_MODEL] = dml_a.astype(BF16)
        dpj_ref[:, D_MODEL:2 * D_MODEL] = dml_b.astype(BF16)
        dbg_ref[:, :D_MODEL] += jnp.sum(dml_a, axis=0, keepdims=True)
        dbg_ref[:, D_MODEL:] += jnp.sum(dml_b, axis=0, keepdims=True)
        dya_bf = (dm * g_a).astype(BF16)
        dyb_bf = (dm * g_b).astype(BF16)
        dya_ref[...] = dya_bf
        dyb_ref[...] = dyb_bf
        dya_in = _dot(dya_bf, wat_ref[...])
        dyb_in = _dot(dyb_bf, wbt_ref[...])
        dattn = dya_in * silu_a
        delta = _sel_right(dattn * attn_v, ones)
        lane = lax.broadcasted_iota(jnp.int32, (tm, LANE), 1)
        for p in range(HEADS // 2):
            sl = slice(LANE * p, LANE * (p + 1))
            xs = (dattn[:, sl], pltpu.roll(dattn[:, sl], VDIM, 1))
            nds = (-pltpu.roll(delta[:, sl], VDIM, 1), -delta[:, sl])
            for a in range(2):
                hi, lo_part = _hi_lo(nds[a])
                blk = jnp.where(lane < VDIM, xs[a], jnp.where(lane == VDIM, hi, jnp.where(lane == VDIM + 1, lo_part, 0.0)))
                dop_ref[:, LANE * (2 * p + a):LANE * (2 * p + a + 1)] = blk.astype(BF16)
        dpj_ref[:, 2 * D_MODEL:2 * D_MODEL + HG_WIDTH] = (
            dya_in * attn_v * (sa * (1.0 + gate_a * (1.0 - sa)))).astype(BF16)
        don = dyb_in * silu_b
        dpj_ref[:, 2 * D_MODEL + HG_WIDTH:] = (dyb_in * on * (sb * (1.0 + gate_b * (1.0 - sb)))).astype(BF16)
        dgh_ref[...] += jnp.sum(don * ohat, axis=0, keepdims=True)
        dohat = don * ghv
        do_ref[...] = ro * (dohat - ohat * (_sel_right(dohat * ohat, ones) * (1.0 / 64.0)))

    row = lambda w, j: pl.BlockSpec((tm, w), lambda i: (i, j))
    col = lambda w: pl.BlockSpec((w, tm), lambda i: (0, i))
    full = lambda a: pl.BlockSpec(a.shape, lambda i: (0, 0))
    acc = lambda w: pl.BlockSpec((1, w), lambda i: (0, 0))
    sds = lambda w, dt: jax.ShapeDtypeStruct((s, w), dt)
    sdt = lambda w: jax.ShapeDtypeStruct((w, s), BF16)
    return pl.pallas_call(
        body,
        grid=(s // tm,),
        in_specs=[row(1024, 0), row(1024, 0), row(2048, 0), row(512, 4), row(512, 5), row(512, 0), row(512, 0)]
        + [ANY] * 6 + [full(b_gate), full(g_post), full(gh), full(ones64)],
        out_specs=[row(1024, 0), row(3072, 0), row(1024, 0), row(512, 0),
                   col(1024), row(1024, 0), col(512), row(1024, 0), col(512), row(1024, 0),
                   acc(1024), acc(1024), acc(2048), acc(512)],
        out_shape=[sds(1024, F32), sds(D_IN_PAD, BF16), sds(1024, BF16), sds(512, F32),
                   sdt(1024), sds(1024, BF16), sdt(512), sds(1024, BF16), sdt(512), sds(1024, BF16),
                   jax.ShapeDtypeStruct((1, 1024), F32), jax.ShapeDtypeStruct((1, 1024), F32),
                   jax.ShapeDtypeStruct((1, 2048), F32), jax.ShapeDtypeStruct((1, 512), F32)],
        scratch_shapes=[pltpu.VMEM(a.shape, BF16) for a in weights] + [pltpu.SemaphoreType.DMA((6,))],
        compiler_params=_params(("arbitrary",), 56),
        name="tail",
    )(x, tgt, proj, proj, proj, attn, o, *weights, b_gate, g_post, gh, ones64)


def _mla_bwd(proj, dqr, dkr, dv, g_q, g_kv, w_uq_pt, w_kv_pt, rc, rs1, rs2, dproj):
    s = proj.shape[0]
    tm = 256
    scale = 1.0 / math.sqrt(QK)

    def body(cq_ref, ckv_ref, dqr_ref, dkr_ref, dv_ref, gq_ref, gkv_ref, wuqt_ref, wkvt_ref, c_ref, s1_ref, s2_ref,
             dproj_in, dqf_ref, dkvf_ref, dc_ref, dgq_ref, dgkv_ref):
        del dproj_in

        @pl.when(pl.program_id(0) == 0)
        def _():
            dgq_ref[...] = jnp.zeros_like(dgq_ref)
            dgkv_ref[...] = jnp.zeros_like(dgkv_ref)

        c, s1, s2 = c_ref[...], s1_ref[...], s2_ref[...]
        lane = lax.broadcasted_iota(jnp.int32, (tm, LANE), 1)
        ksum = jnp.zeros((tm, LANE), F32)
        for h in range(HEADS):
            sl = slice(LANE * h, LANE * (h + 1))
            dqf_ref[:, sl] = (_unrope(dqr_ref[:, sl], c, s1, s2) * scale).astype(BF16)
            dkh = dkr_ref[:, sl]
            ksum = ksum + dkh
            dkvf_ref[:, sl] = jnp.where(lane < NOPE, dkh, 0.0).astype(BF16)
            dkvf_ref[:, HEADS * LANE + LANE * h:HEADS * LANE + LANE * (h + 1)] = jnp.where(
                lane < VDIM, dv_ref[:, sl], 0.0).astype(BF16)
        dkpe = _unrope(ksum, c, s1, s2)
        dc_ref[:, Q_LORA + KV_LORA:] = jnp.where((lane >= NOPE) & (lane < QK), dkpe, 0.0).astype(BF16)
        dcqn = _dot(dqf_ref[...], wuqt_ref[...])
        dckvn = _dot(dkvf_ref[...], wkvt_ref[...])
        for x_ref, g_ref, dn, cols, dg_ref in ((cq_ref, gq_ref, dcqn, slice(0, Q_LORA), dgq_ref),
                                               (ckv_ref, gkv_ref, dckvn, slice(Q_LORA, Q_LORA + KV_LORA), dgkv_ref)):
            xv = x_ref[...]
            r = lax.rsqrt(jnp.mean(xv * xv, axis=-1, keepdims=True) + EPS)
            xh = xv * r
            dg_ref[...] += jnp.sum(dn * xh, axis=0, keepdims=True)
            dh = dn * g_ref[...]
            dc_ref[:, cols] = (r * (dh - xh * jnp.mean(dh * xh, axis=-1, keepdims=True))).astype(BF16)

    row = lambda w, j: pl.BlockSpec((tm, w), lambda i: (i, j))
    full = lambda a: pl.BlockSpec(a.shape, lambda i: (0, 0))
    acc = lambda w: pl.BlockSpec((1, w), lambda i: (0, 0))
    sds = lambda w, dt: jax.ShapeDtypeStruct((s, w), dt)
    return pl.pallas_call(
        body,
        grid=(s // tm,),
        in_specs=[row(768, 6), row(256, 21), row(1024, 0), row(1024, 0), row(1024, 0), full(g_q), full(g_kv),
                  full(w_uq_pt), full(w_kv_pt), row(128, 0), row(128, 0), row(128, 0),
                  pl.BlockSpec(memory_space=pl.ANY)],
        out_specs=[row(1024, 0), row(2048, 0), row(1152, 4), acc(768), acc(256)],
        out_shape=[sds(1024, BF16), sds(2048, BF16), jax.ShapeDtypeStruct(dproj.shape, BF16),
                   jax.ShapeDtypeStruct((1, 768), F32), jax.ShapeDtypeStruct((1, 256), F32)],
        input_output_aliases={12: 2},
        compiler_params=_params(("arbitrary",)),
        name="mla_bwd",
    )(proj, proj, dqr, dkr, dv, g_q, g_kv, w_uq_pt, w_kv_pt, rc, rs1, rs2, dproj)


def _pick(n, options):
    for o in options:
        if n % o == 0:
            return o
    raise ValueError(n)


def _matmul(a, b, name):
    m, k = a.shape
    n = b.shape[1]
    tm = _pick(m, (1024, 768, 512, 256))
    tn = _pick(n, (1152, 1024, 768, 512))
    tk = _pick(k, (1024, 512))
    nk = k // tk

    def body(a_ref, b_ref, o_ref):
        @pl.when(pl.program_id(2) == 0)
        def _():
            o_ref[...] = jnp.zeros_like(o_ref)

        o_ref[...] += _dot(a_ref[...], b_ref[...])

    return pl.pallas_call(
        body,
        grid=(m // tm, n // tn, nk),
        in_specs=[pl.BlockSpec((tm, tk), lambda i, j, l: (i, l)), pl.BlockSpec((tk, tn), lambda i, j, l: (l, j))],
        out_specs=pl.BlockSpec((tm, tn), lambda i, j, l: (i, j)),
        out_shape=jax.ShapeDtypeStruct((m, n), F32),
        compiler_params=_params(("arbitrary", "arbitrary", "arbitrary")),
        name=name,
    )(a, b)


def _dh_dx(dproj, w_in_pt, x, dout, g_pre, sends):
    s, k = dproj.shape
    tm = 256
    ns, ni = len(sends), s // tm

    def body(dp_ref, w_ref, x_ref, dout_ref, g_ref, *rest):
        send_refs, (dx_ref, dg_ref) = rest[:ns], rest[ns:ns + 2]
        recv_refs, transit_refs, sems = rest[ns + 2:2 * ns + 2], rest[2 * ns + 2:3 * ns + 2], rest[3 * ns + 2:]
        start, relay, finish = _chips_relay(send_refs, recv_refs, transit_refs, sems)

        @pl.when(pl.program_id(0) == 0)
        def _():
            start()
            dg_ref[...] = jnp.zeros_like(dg_ref)

        @pl.when(pl.program_id(0) == (9 * ni) // 16)
        def _():
            relay()

        dh = _dot(dp_ref[...], w_ref[...])
        xv = x_ref[...]
        r = lax.rsqrt(jnp.mean(xv * xv, axis=-1, keepdims=True) + EPS)
        xh = xv * r
        dg_ref[...] += jnp.sum(dh * xh, axis=0, keepdims=True)
        dxh = dh * g_ref[...]
        dx_ref[...] = dout_ref[...] + r * (dxh - xh * jnp.mean(dxh * xh, axis=-1, keepdims=True))

        @pl.when(pl.program_id(0) == ni - 1)
        def _():
            finish()

    row = lambda w: pl.BlockSpec((tm, w), lambda i: (i, 0))
    res = pl.pallas_call(
        body,
        grid=(ni,),
        in_specs=[row(k), pl.BlockSpec((k, D_MODEL), lambda i: (0, 0)), row(D_MODEL), row(D_MODEL),
                  pl.BlockSpec((1, D_MODEL), lambda i: (0, 0))] + [ANY] * ns,
        out_specs=[row(D_MODEL), pl.BlockSpec((1, D_MODEL), lambda i: (0, 0))] + [ANY] * (2 * ns),
        out_shape=[jax.ShapeDtypeStruct((s, D_MODEL), F32), jax.ShapeDtypeStruct((1, D_MODEL), F32)]
        + [jax.ShapeDtypeStruct(a.shape, a.dtype) for a in sends]
        + [jax.ShapeDtypeStruct(a.shape[1:], a.dtype) for a in sends],
        scratch_shapes=_copy_sems(ns, 4),
        compiler_params=_params(("arbitrary",)),
        name="dh_dx",
    )(dproj, w_in_pt, x, dout, g_pre, *sends)
    return res[:2 + ns]


def _pair_reduce(slots):
    n = len(slots)
    half = [(N_DEV // 2,) + a.shape[1:] for a in slots]

    def body(*refs):
        s_refs, o_refs = refs[:n], refs[n:2 * n]
        mine, got = refs[2 * n:3 * n], refs[3 * n:4 * n]
        send_sems, recv_sems, local_sems = refs[4 * n:]
        x, y, c = _my_place()
        copies, loads = [], []
        for a in range(n):
            for q in range(N_DEV // 2):
                copies.append(pltpu.make_async_remote_copy(
                    src_ref=s_refs[a].at[2 * q + 1 - c], dst_ref=got[a].at[q],
                    send_sem=send_sems.at[4 * a + q], recv_sem=recv_sems.at[4 * a + q],
                    device_id=(x, y, 1 - c), device_id_type=MESH_ID))
                loads.append(pltpu.make_async_copy(s_refs[a].at[2 * q + c], mine[a].at[q], local_sems.at[4 * a + q]))
        _start_all(loads, copies)
        _wait_all(loads, copies)
        for a in range(n):
            o_refs[a][...] = (mine[a][...].astype(F32) + got[a][...].astype(F32)).astype(o_refs[a].dtype)

    vm = lambda: [pltpu.VMEM(h, a.dtype) for h, a in zip(half, slots)]
    return pl.pallas_call(
        body,
        in_specs=[ANY] * n,
        out_shape=[jax.ShapeDtypeStruct(h, a.dtype) for h, a in zip(half, slots)],
        scratch_shapes=vm() + vm() + [pltpu.SemaphoreType.DMA((4 * n,)), pltpu.SemaphoreType.DMA((4 * n,)),
                                      pltpu.SemaphoreType.DMA((4 * n,))],
        compiler_params=pltpu.CompilerParams(vmem_limit_bytes=48 * 2**20),
        name="pair_reduce",
    )(*slots)


def _rope_tables(s):
    inv = (np.float32(ROPE_THETA) ** (-np.arange(0, ROPE, 2, dtype=np.float32) / np.float32(ROPE))).astype(np.float32)
    ang = (np.arange(s, dtype=np.float32)[:, None] * inv[None, :]).astype(np.float32)
    cos, sin = jnp.asarray(np.cos(ang.astype(np.float64)), F32), jnp.asarray(np.sin(ang.astype(np.float64)), F32)
    z = lambda w: jnp.zeros((s, w), F32)
    rc = jnp.concatenate([jnp.ones((s, NOPE), F32), cos, cos, z(32)], axis=1)
    rs1 = jnp.concatenate([z(NOPE), -sin, z(16), z(32)], axis=1)
    rs2 = jnp.concatenate([z(NOPE), z(16), sin, z(32)], axis=1)
    return rc, rs1, rs2


def _step(x, tgt, w_blk, shards, g_pre, b_gate, g_q, g_kv, lbl, g_hgrn, g_post):
    s = x.shape[0]
    rc, rs1, rs2 = _rope_tables(s)
    gh = jnp.tile(g_hgrn, (1, HEADS))

    proj, w_in_pt, ht, *got = _gather_proj(x, g_pre, w_blk, shards[:2])
    w_uq, w_ukv = (_from_slots(n, g) for n, g in zip(MATS[:2], got))
    w_uq_p = jnp.pad(w_uq.reshape(Q_LORA, HEADS, QK), ((0, 0), (0, 0), (0, LANE - QK))).reshape(Q_LORA, HEADS * LANE)
    kv3 = w_ukv.reshape(KV_LORA, HEADS, NOPE + VDIM)
    pad64 = lambda t: jnp.pad(t, ((0, 0), (0, 0), (0, LANE - 64))).reshape(KV_LORA, HEADS * LANE)
    w_kv_p = jnp.concatenate([pad64(kv3[:, :, :NOPE]), pad64(kv3[:, :, NOPE:])], axis=1)

    qr, kr, v, cqt, ckvt = _mla_prep(proj, g_q, g_kv, w_uq_p, w_kv_p, rc, rs1, rs2)
    attn, qa, *got = _attn_fwd(qr, kr, v, shards[2:])
    w_a, w_b, w_out = (_from_slots(n, g) for n, g in zip(MATS[2:], got))
    o, sprev = _hgrn_fwd(proj, lbl)
    (dout, dproj, dop, do, mt, dy_bf, yat, dya_bf, ybt, dyb_bf,
     loss_vec, dg_post, db_gate, dgh) = _tail(x, tgt, proj, attn, o, w_a, w_b, w_out, w_a.T, w_b.T, w_out.T,
                                               b_gate, g_post, gh)
    early = [_to_slots(n, _matmul(a, b, "d" + n)).astype(BF16)
             for n, a, b in (("w_branch_a", yat, dya_bf), ("w_branch_b", ybt, dyb_bf), ("w_out", mt, dy_bf))]
    dqr, dkr, dv, *early_recv = _attn_bwd(qa, kr, v, dop, early)
    dproj, dlbl = _hgrn_bwd(proj, lbl, do, sprev, dproj)
    dqf, dkvf, dproj, dg_q, dg_kv = _mla_bwd(proj, dqr, dkr, dv, g_q, g_kv, w_uq_p.T, w_kv_p.T, rc, rs1, rs2, dproj)

    dw_in_slots = _scatter_w_in(_matmul(ht, dproj, "dw_in"))
    dw_uq_p = _matmul(cqt, dqf, "dw_uq")
    dw_kv_p = _matmul(ckvt, dkvf, "dw_kv")
    dw_uq = dw_uq_p.reshape(Q_LORA, HEADS, LANE)[:, :, :QK].reshape(Q_LORA, HEADS * QK)
    dw_ukv = jnp.concatenate([dw_kv_p[:, :HEADS * LANE].reshape(KV_LORA, HEADS, LANE)[:, :, :NOPE],
                              dw_kv_p[:, HEADS * LANE:].reshape(KV_LORA, HEADS, LANE)[:, :, :VDIM]],
                             axis=2).reshape(KV_LORA, 1024)
    late = _pair_reduce([dw_in_slots, _to_slots("w_uq", dw_uq).astype(BF16), _to_slots("w_ukv", dw_ukv).astype(BF16)])
    dx, dg_pre, *late_recv = _dh_dx(dproj, w_in_pt, x, dout, g_pre, late)

    g_sum = _vectors_sum(dg_pre, db_gate, dg_q, dg_kv, dlbl, dgh, dg_post, loss_vec)
    return dx, late_recv[0], dict(zip(MATS, late_recv[1:] + early_recv)), g_sum


def _all_gather(blocks):
    n = len(blocks)

    def body(*refs):
        x_refs, out_refs = refs[:n], refs[n:2 * n]
        send_sems, recv_sems, local_sems = refs[2 * n:]
        x, y, c = _my_place()
        me, sibling = (x, y, c), (x, y, 1 - c)
        chips = [(1 - x, y), (x, 1 - y), (1 - x, 1 - y)]

        def slot(a, px, py, pc):
            return out_refs[a].at[4 * px + 2 * py + pc]

        def copy(a, k, blk, to, src=None):
            return pltpu.make_async_remote_copy(
                src_ref=slot(a, *blk) if src is None else src, dst_ref=slot(a, *blk),
                send_sem=send_sems.at[7 * a + k], recv_sem=recv_sems.at[7 * a + k],
                device_id=to, device_id_type=MESH_ID)

        mine = [pltpu.make_async_copy(x_refs[a], slot(a, *me), local_sems.at[a]) for a in range(n)]
        for cp in mine:
            cp.start()
        first = [copy(a, 0, me, sibling, src=x_refs[a]) for a in range(n)]
        first += [copy(a, 1 + j, me, (*chip, c), src=x_refs[a]) for a in range(n) for j, chip in enumerate(chips)]
        for cp in first:
            cp.start()
        passed = []
        for j, chip in enumerate(chips):
            for a in range(n):
                copy(a, 1 + j, (*chip, c), me).wait_recv()
                passed.append(copy(a, 4 + j, (*chip, c), sibling))
                passed[-1].start()
        for a in range(n):
            copy(a, 0, sibling, me).wait_recv()
        for j, chip in enumerate(chips):
            for a in range(n):
                copy(a, 4 + j, (*chip, 1 - c), me).wait_recv()
        for cp in first + passed:
            cp.wait_send()
        for cp in mine:
            cp.wait()

    return pl.pallas_call(
        body,
        out_shape=[jax.ShapeDtypeStruct((N_DEV,) + b.shape, b.dtype) for b in blocks],
        in_specs=[pl.BlockSpec(memory_space=pl.ANY)] * n,
        out_specs=[pl.BlockSpec(memory_space=pl.ANY)] * n,
        scratch_shapes=[pltpu.SemaphoreType.DMA((7 * n,)), pltpu.SemaphoreType.DMA((7 * n,)),
                        pltpu.SemaphoreType.DMA((n,))],
        name="gather_weights",
    )(*blocks)


def _adamw(g, w, m, v):
    c1 = 1.0 / (1.0 - ADAM_B1 ** ADAM_STEP)
    c2 = 1.0 / (1.0 - ADAM_B2 ** ADAM_STEP)
    nm = ADAM_B1 * m + (1.0 - ADAM_B1) * g
    nv = ADAM_B2 * v + (1.0 - ADAM_B2) * (g * g)
    d = -ADAM_LR * ((nm * c1) / (jnp.sqrt(nv * c2) + ADAM_EPS) + ADAM_WD * w)
    return d, nm, nv


def _sum8(r_ref):
    g = r_ref[0].astype(F32)
    for k in range(1, r_ref.shape[0]):
        g = g + r_ref[k].astype(F32)
    return g


def _sum_adamw_w_in(recv, w, m, v):
    rows, _, cols = w.shape
    tc = 256

    def body(r_ref, w_ref, m_ref, v_ref, g_ref, d_ref, nm_ref, nv_ref):
        g = _sum8(r_ref)
        dense = lambda ref: ref[...].reshape(rows, tc)
        d, nm, nv = _adamw(g, dense(w_ref), dense(m_ref), dense(v_ref))
        for ref, val in ((g_ref, g), (d_ref, d), (nm_ref, nm), (nv_ref, nv)):
            ref[...] = val.reshape(rows, 1, tc)

    blk = pl.BlockSpec((rows, 1, tc), lambda i: (0, 0, i))
    out = jax.ShapeDtypeStruct((rows, 1, cols), F32)
    return pl.pallas_call(
        body,
        grid=(cols // tc,),
        in_specs=[pl.BlockSpec((recv.shape[0], rows, tc), lambda i: (0, 0, i)), blk, blk, blk],
        out_specs=[blk, blk, blk, blk],
        out_shape=[out, out, out, out],
        compiler_params=_params(("arbitrary",)),
        name="sum_adamw_w_in",
    )(recv, w, m, v)


def _sum_adamw_whole(recvs, ws, ms, vs):
    n = len(ws)

    def body(*refs):
        r_refs, w_refs, m_refs, v_refs = refs[:n], refs[n:2 * n], refs[2 * n:3 * n], refs[3 * n:4 * n]
        outs = refs[4 * n:]
        for a in range(n):
            g = _sum8(r_refs[a])
            d, nm, nv = _adamw(g, w_refs[a][...], m_refs[a][...], v_refs[a][...])
            outs[a][...] = g
            outs[n + a][...] = d
            outs[2 * n + a][...] = nm
            outs[3 * n + a][...] = nv

    shapes = [jax.ShapeDtypeStruct(w.shape, F32) for w in ws]
    res = pl.pallas_call(
        body,
        out_shape=shapes * 4,
        compiler_params=pltpu.CompilerParams(vmem_limit_bytes=48 * 2**20),
        name="sum_adamw_mats",
    )(*recvs, *ws, *ms, *vs)
    return res[:n], res[n:2 * n], res[2 * n:3 * n], res[3 * n:]


SMALL = ("g_pre", "b_gate", "g_q", "g_kv", "lb_logits", "g_hgrn", "g_post")
SMALL_SHAPE = dict(g_pre=(1, 1024), b_gate=(1, 2048), g_q=(1, 768), g_kv=(1, 256), lb_logits=(2, 512),
                   g_hgrn=(1, 64), g_post=(1, 1024))


def _vectors_sum(dg_pre, db_gate, dg_q, dg_kv, dlbl, dgh, dg_post, loss_vec):
    def body(gpre_ref, bg_ref, gq_ref, gkv_ref, lbl_ref, gh_ref, gpost_ref, loss_ref, out_ref, mine, got,
             send_sems, recv_sems):
        mine[...] = jnp.zeros_like(mine)
        mine[0:1, :] = gpre_ref[...]
        mine[1:2, :] = bg_ref[:, :1024]
        mine[2:3, :] = bg_ref[:, 1024:]
        mine[3:4, :Q_LORA] = gq_ref[...]
        mine[4:5, :KV_LORA] = gkv_ref[...]
        loss = (0.5 / D_MODEL) * jnp.sum(loss_ref[...], axis=-1, keepdims=True)
        mine[4:5, KV_LORA:] = jnp.broadcast_to(loss, (1, 1024 - KV_LORA))
        mine[5:6, :HG_WIDTH] = lbl_ref[0:1, :]
        mine[5:6, HG_WIDTH:] = lbl_ref[1:2, :]
        gh = gh_ref[...]
        fold = gh[:, :VDIM]
        for h in range(1, HEADS):
            fold = fold + gh[:, VDIM * h:VDIM * (h + 1)]
        mine[6:7, :VDIM] = fold
        mine[7:8, :] = gpost_ref[...]
        x, y, c = _my_place()
        me = 4 * x + 2 * y + c
        got[me] = mine[...]
        copies = [pltpu.make_async_remote_copy(
            src_ref=mine, dst_ref=got.at[me], send_sem=send_sems.at[k], recv_sem=recv_sems.at[k],
            device_id=_flip(k, x, y, c), device_id_type=MESH_ID) for k in range(N_DEV - 1)]
        _start_all([], copies)
        _wait_all([], copies)
        out_ref[...] = _sum8(got)

    return pl.pallas_call(
        body,
        out_shape=jax.ShapeDtypeStruct((8, 1024), F32),
        scratch_shapes=[pltpu.VMEM((8, 1024), F32), pltpu.VMEM((N_DEV, 8, 1024), F32),
                        pltpu.SemaphoreType.DMA((7,)), pltpu.SemaphoreType.DMA((7,))],
        name="vectors_sum",
    )(dg_pre, db_gate, dg_q, dg_kv, dlbl, dgh, dg_post, loss_vec)


def _vectors_adamw(g_sum, ws, ms, vs):
    n = len(SMALL)

    def body(g_ref, *refs):
        w_refs, m_refs, v_refs = refs[:n], refs[n:2 * n], refs[2 * n:3 * n]
        loss_ref, outs = refs[3 * n], refs[3 * n + 1:]
        g = g_ref[...]
        loss_ref[...] = g[4:5, KV_LORA:KV_LORA + 1]
        grads = (g[0:1, :], jnp.concatenate([g[1:2, :], g[2:3, :]], axis=1), g[3:4, :Q_LORA], g[4:5, :KV_LORA],
                 jnp.concatenate([g[5:6, :HG_WIDTH], g[5:6, HG_WIDTH:]], axis=0), g[6:7, :VDIM], g[7:8, :])
        for a in range(n):
            d, nm, nv = _adamw(grads[a], w_refs[a][...], m_refs[a][...], v_refs[a][...])
            outs[a][...] = grads[a]
            outs[n + a][...] = d
            outs[2 * n + a][...] = nm
            outs[3 * n + a][...] = nv

    shapes = [jax.ShapeDtypeStruct(SMALL_SHAPE[k], F32) for k in SMALL]
    res = pl.pallas_call(
        body,
        out_shape=[jax.ShapeDtypeStruct((1, 1), F32)] + shapes * 4,
        name="vectors_adamw",
    )(g_sum, *ws, *ms, *vs)
    return res[0], res[1:n + 1], res[n + 1:2 * n + 1], res[2 * n + 1:3 * n + 1], res[3 * n + 1:]


MATS = ("w_uq", "w_ukv", "w_branch_a", "w_branch_b", "w_out")
COL_SHARDED = dict(w_uq=False, w_ukv=True, w_branch_a=True, w_branch_b=True, w_out=False)
ORDER = ("g_pre", "w_in", "b_gate", "g_q", "w_uq", "g_kv", "w_ukv", "lb_logits", "g_hgrn",
         "w_branch_a", "w_branch_b", "w_out", "g_post")


def _to_slots(name, full):
    r, c = full.shape
    if COL_SHARDED[name]:
        return full.reshape(r, N_DEV, c // N_DEV).transpose(1, 0, 2)
    return full.reshape(N_DEV, r // N_DEV, c)


def _from_slots(name, slots):
    _, r, c = slots.shape
    if COL_SHARDED[name]:
        return slots.transpose(1, 0, 2).reshape(r, N_DEV * c)
    return slots.reshape(N_DEV * r, c)


def kernel(x, g_pre, w_in, b_gate, g_q, w_uq, g_kv, w_ukv, lb_logits, g_hgrn, w_branch_a, w_branch_b, w_out, g_post, loss_target, m_g_pre, m_w_in, m_b_gate, m_g_q, m_w_uq, m_g_kv, m_w_ukv, m_lb_logits, m_g_hgrn, m_w_branch_a, m_w_branch_b, m_w_out, m_g_post, v_g_pre, v_w_in, v_b_gate, v_g_q, v_w_uq, v_g_kv, v_w_ukv, v_lb_logits, v_g_hgrn, v_w_branch_a, v_w_branch_b, v_w_out, v_g_post):
    rows3 = lambda a: jnp.transpose(a, (2, 0, 1))
    w = dict(w_in=rows3(w_in), w_uq=w_uq[0], w_ukv=w_ukv[0], w_branch_a=w_branch_a[0], w_branch_b=w_branch_b[0],
             w_out=w_out[0], g_pre=g_pre, b_gate=b_gate, g_q=g_q, g_kv=g_kv, lb_logits=lb_logits, g_hgrn=g_hgrn,
             g_post=g_post)
    mom = dict(w_in=rows3(m_w_in), w_uq=m_w_uq[0], w_ukv=m_w_ukv[0], w_branch_a=m_w_branch_a[0],
               w_branch_b=m_w_branch_b[0], w_out=m_w_out[0], g_pre=m_g_pre, b_gate=m_b_gate, g_q=m_g_q, g_kv=m_g_kv,
               lb_logits=m_lb_logits, g_hgrn=m_g_hgrn, g_post=m_g_post)
    var = dict(w_in=rows3(v_w_in), w_uq=v_w_uq[0], w_ukv=v_w_ukv[0], w_branch_a=v_w_branch_a[0],
               w_branch_b=v_w_branch_b[0], w_out=v_w_out[0], g_pre=v_g_pre, b_gate=v_b_gate, g_q=v_g_q, g_kv=v_g_kv,
               lb_logits=v_lb_logits, g_hgrn=v_g_hgrn, g_post=v_g_post)

    w_blk = w["w_in"].reshape(W_IN_SHARD, D_MODEL).astype(BF16)
    dx, recv_in, recv, g_sum = _step(x[0], loss_target[0], w_blk, [w[n].astype(BF16) for n in MATS],
                                     g_pre, b_gate, g_q, g_kv, lb_logits, g_hgrn, g_post)

    g_in, d_in, m_in, v_in = _sum_adamw_w_in(recv_in, w["w_in"], mom["w_in"], var["w_in"])
    res = _sum_adamw_whole([recv[n] for n in MATS], *([t[n] for n in MATS] for t in (w, mom, var)))
    total, *vec = _vectors_adamw(g_sum, *([t[n] for n in SMALL] for t in (w, mom, var)))

    outs = []
    for mats, vecs, big in zip(res, vec, (g_in, d_in, m_in, v_in)):
        t = {**{n: a[None] for n, a in zip(MATS, mats)}, **dict(zip(SMALL, vecs)),
             "w_in": jnp.transpose(big, (1, 2, 0))}
        outs += [t[n] for n in ORDER]
    return (total.reshape(()), dx[None], *outs)
```

```python
import math

import jax
import jax.numpy as jnp
import numpy as np
from jax import lax
from jax.experimental import pallas as pl
from jax.experimental.pallas import tpu as pltpu

F32, BF16 = jnp.float32, jnp.bfloat16

D_MODEL = 1024
EPS = 1e-6
HEADS = 8
NOPE, ROPE, VDIM = 64, 32, 64
QK = NOPE + ROPE
Q_LORA, KV_LORA = 768, 256
ROPE_THETA = 10000.0
ATT_CHUNK_SHIFT = 6
HG_BLOCK = 32
HG_WIDTH = 512
D_IN = 5664
D_IN_PAD = 5760
W_IN_SHARD = D_IN // 8
N_DEV = 8
LANE = 128

ADAM_LR, ADAM_B1, ADAM_B2, ADAM_EPS, ADAM_WD, ADAM_STEP = 0.001, 0.9, 0.999, 1e-08, 0.01, 10

W_IN_SEGMENTS = ((3616, 5664, 0), (1056, 1568, 2048), (3104, 3616, 2560), (1568, 3104, 3072),
                 (0, 1024, 4608), (1024, 1056, 5696))

NT = (((1,), (1,)), ((), ()))
TN = (((0,), (0,)), ((), ()))
MESH_ID = pl.DeviceIdType.MESH


def _w_in_pieces():
    out = []
    for lo, hi, dst in W_IN_SEGMENTS:
        c = lo
        while c < hi:
            p = c // W_IN_SHARD
            e = min(hi, (p + 1) * W_IN_SHARD)
            out.append((p, c - p * W_IN_SHARD, e - p * W_IN_SHARD, dst + c - lo))
            c = e
    return out


def _params(sem, vmem_mb=48):
    return pltpu.CompilerParams(dimension_semantics=sem, vmem_limit_bytes=vmem_mb * 2**20)


def _dot(a, b):
    return jnp.dot(a, b, preferred_element_type=F32)


def _dotg(a, b, dims):
    return lax.dot_general(a, b, dims, preferred_element_type=F32)


def _split2(x):
    hi = x.astype(BF16)
    return hi, (x - hi.astype(F32)).astype(BF16)


def _sel_left(m01, x):
    hi, lo = _split2(x)
    return _dot(m01, hi) + _dot(m01, lo)


def _sel_right(x, m01):
    hi, lo = _split2(x)
    return _dot(hi, m01) + _dot(lo, m01)


def _hi_lo(x):
    hi = x.astype(BF16).astype(F32)
    return hi, x - hi


def _sigmoid(x):
    return 0.5 * jnp.tanh(0.5 * x) + 0.5


def _rope(x, c, s1, s2):
    return x * c + pltpu.roll(x, 112, 1) * s1 + pltpu.roll(x, 16, 1) * s2


def _unrope(d, c, s1, s2):
    return d * c + pltpu.roll(d * s1, 16, 1) + pltpu.roll(d * s2, 112, 1)


def _my_place():
    return lax.axis_index("x"), lax.axis_index("y"), lax.axis_index("c")


def _flip(k, x, y, c):
    fx, fy, fc = (k + 1) >> 2 & 1, (k + 1) >> 1 & 1, (k + 1) & 1
    return (1 - x if fx else x), (1 - y if fy else y), (1 - c if fc else c)


def _to_all_copies(s_refs, r_refs, sems, spread):
    send_sems, recv_sems, local_sems = sems
    x, y, c = _my_place()
    me = 4 * x + 2 * y + c
    src = (lambda a, p: s_refs[a]) if spread else (lambda a, p: s_refs[a].at[p])
    local = [pltpu.make_async_copy(src(a, me), r_refs[a].at[me], local_sems.at[a]) for a in range(len(s_refs))]
    remote = []
    for k in range(N_DEV - 1):
        px, py, pc = _flip(k, x, y, c)
        for a in range(len(s_refs)):
            remote.append(pltpu.make_async_remote_copy(
                src_ref=src(a, 4 * px + 2 * py + pc), dst_ref=r_refs[a].at[me],
                send_sem=send_sems.at[7 * a + k], recv_sem=recv_sems.at[7 * a + k],
                device_id=(px, py, pc), device_id_type=MESH_ID))
    return local, remote


def _to_chips_copies(s_refs, r_refs, sems):
    send_sems, recv_sems, local_sems = sems
    x, y, c = _my_place()
    me = 2 * x + y
    local = [pltpu.make_async_copy(s_refs[a].at[me], r_refs[a].at[me], local_sems.at[a]) for a in range(len(s_refs))]
    remote = []
    for k in range(3):
        px = 1 - x if (k + 1) >> 1 & 1 else x
        py = 1 - y if (k + 1) & 1 else y
        for a in range(len(s_refs)):
            remote.append(pltpu.make_async_remote_copy(
                src_ref=s_refs[a].at[2 * px + py], dst_ref=r_refs[a].at[me],
                send_sem=send_sems.at[3 * a + k], recv_sem=recv_sems.at[3 * a + k],
                device_id=(px, py, c), device_id_type=MESH_ID))
    return local, remote


def _start_all(local, remote):
    for cp in local + remote:
        cp.start()


def _wait_all(local, remote):
    for cp in remote:
        cp.wait_recv()
    for cp in remote:
        cp.wait_send()
    for cp in local:
        cp.wait()


def _copy_sems(n, peers):
    return [pltpu.SemaphoreType.DMA((peers * n,)), pltpu.SemaphoreType.DMA((peers * n,)),
            pltpu.SemaphoreType.DMA((n,))]


ANY = pl.BlockSpec(memory_space=pl.ANY)


def _dw_in_slots(ht, dproj):
    m, k = ht.shape
    n = dproj.shape[1]
    tn, tk = 1152, 1024
    nj, nk = n // tn, k // tk
    by_tile = [[] for _ in range(nj)]
    for p, lo, hi, dst in _w_in_pieces():
        while lo < hi:
            j = dst // tn
            cnt = min(hi - lo, (j + 1) * tn - dst)
            by_tile[j].append((p, lo, lo + cnt, dst - j * tn))
            lo, dst = lo + cnt, dst + cnt

    def body(a_ref, b_ref, s_ref, acc_ref):
        j, l = pl.program_id(0), pl.program_id(1)

        @pl.when(l == 0)
        def _():
            acc_ref[...] = jnp.zeros_like(acc_ref)

        acc_ref[...] += _dot(a_ref[...], b_ref[...])

        @pl.when(l == nk - 1)
        def _():
            at = acc_ref[...].T
            for jj in range(nj):
                @pl.when(j == jj)
                def _(jj=jj):
                    for p, lo, hi, d in by_tile[jj]:
                        s_ref[p, lo:hi, :] = at[d:d + hi - lo, :].astype(BF16)

    return pl.pallas_call(
        body,
        grid=(nj, nk),
        in_specs=[pl.BlockSpec((m, tk), lambda j, l: (0, l)), pl.BlockSpec((tk, tn), lambda j, l: (l, j))],
        out_specs=pl.BlockSpec((N_DEV, W_IN_SHARD, m), lambda j, l: (0, 0, 0)),
        out_shape=jax.ShapeDtypeStruct((N_DEV, W_IN_SHARD, m), BF16),
        scratch_shapes=[pltpu.VMEM((m, tn), F32)],
        compiler_params=_params(("arbitrary", "arbitrary")),
        name="dw_in",
    )(ht, dproj)


GP_TN = 256
GP_COLS = 5888
GP_NT = GP_COLS // GP_TN


def _gp_tile_pieces():
    tiles = [[] for _ in range(GP_NT)]
    for p, lo, hi, dst in _w_in_pieces():
        while lo < hi:
            t = dst // GP_TN
            n = min(hi - lo, (t + 1) * GP_TN - dst)
            tiles[t].append((p, lo, lo + n, dst - t * GP_TN))
            lo, dst = lo + n, dst + n
    return tiles


def _gp_tables():
    pieces = _gp_tile_pieces()
    rank_of = {None: 0, 0: 1, 1: 2, 2: 2, 4: 3, 5: 3, 3: 4, 6: 5}
    order = np.zeros((N_DEV, GP_NT), np.int32)
    waits = np.zeros((N_DEV, GP_NT), np.int32)
    for me in range(N_DEV):
        x, y, c = me >> 2 & 1, me >> 1 & 1, me & 1
        chips = [(1 - x, y), (x, 1 - y), (1 - x, 1 - y)]

        def sem_of(p):
            px, py, pc = p >> 2 & 1, p >> 1 & 1, p & 1
            if (px, py) == (x, y):
                return None if pc == c else 0
            j = chips.index((px, py))
            return 1 + j if pc == c else 4 + j

        needs = [sorted({sem_of(p) for p, _, _, _ in tile} - {None}) for tile in pieces]
        ranks = [max([rank_of[k] for k in ks], default=0) for ks in needs]
        seq = sorted(range(GP_NT), key=lambda t: (ranks[t], t))
        seen = set()
        for step, t in enumerate(seq):
            order[me, step] = t
            new = [k for k in needs[t] if k not in seen]
            for k in new:
                waits[me, step] |= 1 << k
            seen.update(new)
        assert seen == set(range(7)), (me, seen)
    return order, waits


def _gather_proj(x, g_pre, w_blk, shards):
    s = x.shape[0]
    tx = 512
    ns = len(shards)
    tile_pieces = _gp_tile_pieces()
    order_np, waits_np = _gp_tables()
    xq, yq, cq = _my_place()
    me_out = 4 * xq + 2 * yq + cq
    order = lax.dynamic_index_in_dim(jnp.asarray(order_np), me_out, 0, keepdims=False)
    waits = lax.dynamic_index_in_dim(jnp.asarray(waits_np), me_out, 0, keepdims=False)

    def body(order_ref, waits_ref, x_hbm, g_ref, wblk_hbm, *rest):
        shard_refs, (proj_ref, wt_ref, ht_hbm), got_refs = rest[:ns], rest[ns:ns + 3], rest[ns + 3:2 * ns + 3]
        recv, h_ref, wtile, xbuf, htbuf = rest[2 * ns + 3:2 * ns + 8]
        send_sems, recv_sems, misc_sems = rest[2 * ns + 8:2 * ns + 11]
        sems = rest[2 * ns + 11:]
        t = pl.program_id(0)
        x_, y_, c = _my_place()
        sibling = (x_, y_, 1 - c)
        chips = [(1 - x_, y_), (x_, 1 - y_), (1 - x_, 1 - y_)]
        idx = lambda px, py, pc: 4 * px + 2 * py + pc
        me = idx(x_, y_, c)

        def copy(k, slot, to, src=None):
            return pltpu.make_async_remote_copy(
                src_ref=recv.at[slot] if src is None else src, dst_ref=recv.at[slot],
                send_sem=send_sems.at[k], recv_sem=recv_sems.at[k], device_id=to, device_id_type=MESH_ID)

        mine = pltpu.make_async_copy(wblk_hbm, recv.at[me], misc_sems.at[0])
        first = [copy(0, me, sibling, src=wblk_hbm)] + [copy(1 + j, me, (*chips[j], c), src=wblk_hbm) for j in range(2)]
        passed = [copy(4 + j, idx(*ch, c), sibling) for j, ch in enumerate(chips)]
        onward = [copy(3, idx(*chips[0], c), (*chips[1], c)), copy(3, idx(*chips[1], c), (*chips[0], c))]
        arrivals = ([copy(0, idx(x_, y_, 1 - c), sibling)] + [copy(1 + j, idx(*ch, c), sibling) for j, ch in enumerate(chips)]
                    + [copy(4 + j, idx(*ch, 1 - c), sibling) for j, ch in enumerate(chips)])

        @pl.when(t == 0)
        def _():
            mine.start()
            for cp in first:
                cp.start()
            _start_all(*_to_all_copies(shard_refs, got_refs, sems, True))

            def load(i):
                return pltpu.make_async_copy(x_hbm.at[pl.ds(i * tx, tx), :], xbuf.at[i & 1], misc_sems.at[1 + (i & 1)])

            def store(i):
                return pltpu.make_async_copy(htbuf.at[i & 1], ht_hbm.at[:, pl.ds(i * tx, tx)], misc_sems.at[3 + (i & 1)])

            load(0).start()
            for i in range(s // tx):
                if i + 1 < s // tx:
                    load(i + 1).start()
                load(i).wait()
                xv = xbuf[i & 1]
                r = lax.rsqrt(jnp.mean(xv * xv, axis=-1, keepdims=True) + EPS)
                h = (xv * r * g_ref[...]).astype(BF16)
                h_ref[i * tx:(i + 1) * tx, :] = h
                if i >= 2:
                    store(i - 2).wait()
                htbuf[i & 1] = h.T
                store(i).start()
            for i in range(max(s // tx - 2, 0), s // tx):
                store(i).wait()
            mine.wait()

        w = waits_ref[t]
        for k in range(7):
            @pl.when((w >> k) & 1 == 1)
            def _(k=k):
                arrivals[k].wait_recv()
                if 1 <= k <= 3:
                    passed[k - 1].start()
                if 1 <= k <= 2:
                    @pl.when(c == k - 1)
                    def _():
                        onward[k - 1].start()

        tile = order_ref[t]
        for tt in range(GP_NT):
            @pl.when(tile == tt)
            def _(tt=tt):
                covered = sorted((d, d + hi - lo) for _, lo, hi, d in tile_pieces[tt])
                at = 0
                for lo_z, hi_z in covered + [(GP_TN, GP_TN)]:
                    if lo_z > at:
                        wtile[at:lo_z, :] = jnp.zeros((lo_z - at, D_MODEL), BF16)
                    at = max(at, hi_z)
                for p, lo, hi, d in tile_pieces[tt]:
                    wtile[d:d + hi - lo, :] = recv[p, lo:hi, :]

        wt = wtile[...]
        wt_ref[...] = wt
        proj_ref[...] = _dotg(h_ref[...], wt, NT)

        @pl.when(t == GP_NT - 1)
        def _():
            for cp in first + passed + onward[:1]:
                cp.wait_send()
            _wait_all(*_to_all_copies(shard_refs, got_refs, sems, True))

    grid_spec = pltpu.PrefetchScalarGridSpec(
        num_scalar_prefetch=2,
        grid=(GP_NT,),
        in_specs=[ANY, pl.BlockSpec((1, D_MODEL), lambda t, o, w: (0, 0)), ANY] + [ANY] * ns,
        out_specs=[pl.BlockSpec((s, GP_TN), lambda t, o, w: (0, o[t])),
                   pl.BlockSpec((GP_TN, D_MODEL), lambda t, o, w: (o[t], 0)), ANY] + [ANY] * ns,
        scratch_shapes=[pltpu.VMEM((N_DEV, W_IN_SHARD, D_MODEL), BF16), pltpu.VMEM((s, D_MODEL), BF16),
                        pltpu.VMEM((GP_TN, D_MODEL), BF16), pltpu.VMEM((2, tx, D_MODEL), F32),
                        pltpu.VMEM((2, D_MODEL, tx), BF16),
                        pltpu.SemaphoreType.DMA((7,)), pltpu.SemaphoreType.DMA((7,)), pltpu.SemaphoreType.DMA((5,))]
        + _copy_sems(ns, 7),
    )
    return pl.pallas_call(
        body,
        grid_spec=grid_spec,
        out_shape=[jax.ShapeDtypeStruct((s, GP_COLS), F32), jax.ShapeDtypeStruct((GP_COLS, D_MODEL), BF16),
                   jax.ShapeDtypeStruct((D_MODEL, s), BF16)]
        + [jax.ShapeDtypeStruct((N_DEV,) + b.shape, b.dtype) for b in shards],
        compiler_params=_params(("arbitrary",), 56),
        name="gather_proj",
    )(order, waits, x, g_pre, w_blk, *shards)


def _mla_prep(proj, g_q, g_kv, w_uq_p, w_kv_p, rc, rs1, rs2):
    s = proj.shape[0]
    tm = 256
    scale = 1.0 / math.sqrt(QK)

    def body(cq_ref, ckv_ref, kpe_ref, gq_ref, gkv_ref, wuq_ref, wkv_ref, c_ref, s1_ref, s2_ref,
             qr_ref, kr_ref, v_ref, cqt_ref, ckvt_ref):
        cq = cq_ref[...]
        r = lax.rsqrt(jnp.mean(cq * cq, axis=-1, keepdims=True) + EPS)
        cqn = (cq * r * gq_ref[...]).astype(BF16)
        cqt_ref[...] = cqn.T
        q = _dot(cqn, wuq_ref[...])
        ckv = ckv_ref[...]
        r = lax.rsqrt(jnp.mean(ckv * ckv, axis=-1, keepdims=True) + EPS)
        ckvn = (ckv * r * gkv_ref[...]).astype(BF16)
        ckvt_ref[...] = ckvn.T
        kv = _dot(ckvn, wkv_ref[...])
        c, s1, s2 = c_ref[...], s1_ref[...], s2_ref[...]
        lane = lax.broadcasted_iota(jnp.int32, (tm, LANE), 1)
        kpe = _rope(kpe_ref[...], c, s1, s2) + jnp.where((lane == QK) | (lane == QK + 1), 1.0, 0.0)
        vone = jnp.where((lane == VDIM) | (lane == VDIM + 1), 1.0, 0.0)
        for h in range(HEADS):
            sl = slice(LANE * h, LANE * (h + 1))
            qr_ref[:, sl] = (_rope(q[:, sl], c, s1, s2) * scale).astype(BF16)
            kr_ref[:, sl] = (kv[:, sl] + kpe).astype(BF16)
            v_ref[:, sl] = (kv[:, HEADS * LANE + LANE * h:HEADS * LANE + LANE * (h + 1)] + vone).astype(BF16)

    row = lambda w, j: pl.BlockSpec((tm, w), lambda i: (i, j))
    col = lambda w: pl.BlockSpec((w, tm), lambda i: (0, i))
    full = lambda a: pl.BlockSpec(a.shape, lambda i: (0, 0))
    return pl.pallas_call(
        body,
        grid=(s // tm,),
        in_specs=[row(768, 6), row(256, 21), row(128, 44), full(g_q), full(g_kv), full(w_uq_p), full(w_kv_p),
                  row(128, 0), row(128, 0), row(128, 0)],
        out_specs=[row(1024, 0), row(1024, 0), row(1024, 0), col(768), col(256)],
        out_shape=[jax.ShapeDtypeStruct((s, 1024), BF16), jax.ShapeDtypeStruct((s, 1024), BF16),
                   jax.ShapeDtypeStruct((s, 1024), BF16), jax.ShapeDtypeStruct((768, s), BF16),
                   jax.ShapeDtypeStruct((256, s), BF16)],
        compiler_params=_params(("arbitrary",)),
        name="mla_prep",
    )(proj, proj, proj, g_q, g_kv, w_uq_p, w_kv_p, rc, rs1, rs2)


ATT_T = 512
ATT_FWD_HEADS = 4


def _chunk_mask(transposed):
    r = lax.broadcasted_iota(jnp.int32, (ATT_T, ATT_T), 0) >> ATT_CHUNK_SHIFT
    c = lax.broadcasted_iota(jnp.int32, (ATT_T, ATT_T), 1) >> ATT_CHUNK_SHIFT
    return (r <= c) if transposed else (c <= r)


def _attn_fwd(qr, kr, vp, shards):
    s = qr.shape[0]
    t = ATT_T
    g = ATT_FWD_HEADS
    ns = len(shards)

    def body(q_ref, k_ref, v_ref, *rest):
        shard_refs, (o_ref, qa_ref), got_refs = rest[:ns], rest[ns:ns + 2], rest[ns + 2:2 * ns + 2]
        sc_ref, sems = rest[2 * ns + 2], rest[2 * ns + 3:]
        qi = pl.program_id(1)

        @pl.when((pl.program_id(0) == 0) & (qi == 0))
        def _():
            _start_all(*_to_all_copies(shard_refs, got_refs, sems, True))
        lane = lax.broadcasted_iota(jnp.int32, (t, LANE), 1)
        sls = [slice(LANE * a, LANE * (a + 1)) for a in range(g)]
        qs = [q_ref[:, sl] for sl in sls]

        def scores(j):
            rows = pl.ds(pl.multiple_of(j * t, t), t)
            for a in range(g):
                sc_ref[j & 1, a] = _dotg(qs[a], k_ref[rows, sls[a]], NT)

        def step(j, carry, masked):
            rows = pl.ds(pl.multiple_of(j * t, t), t)
            out = []
            for a in range(g):
                m, acc = carry[a]
                sc = sc_ref[j & 1, a]
                if masked:
                    sc = jnp.where(_chunk_mask(False), sc, -1e30)
                m_new = jnp.maximum(m, jnp.max(sc, axis=-1, keepdims=True))
                p = jnp.exp(sc - m_new).astype(BF16)
                acc = jnp.exp(m - m_new) * acc + _dot(p, v_ref[rows, sls[a]])
                out.append((m_new, acc))
            return tuple(out)

        def loop(j, carry):
            carry = step(j, carry, False)
            scores(j + 1)
            return carry

        init = tuple((jnp.full((t, 1), -1e30, F32), jnp.zeros((t, LANE), F32)) for _ in range(g))
        scores(0)
        carry = lax.fori_loop(0, qi, loop, init)
        carry = step(qi, carry, True)
        outs = []
        for a in range(g):
            m, acc = carry[a]
            l = acc[:, VDIM:VDIM + 1]
            outs.append(acc / l)
            hi, lo_part = _hi_lo(-(m + jnp.log(l)))
            qa = jnp.where(lane == QK, hi, jnp.where(lane == QK + 1, lo_part, qs[a].astype(F32)))
            qa_ref[:, sls[a]] = qa.astype(BF16)
        for p in range(g // 2):
            o_ref[:, LANE * p:LANE * (p + 1)] = jnp.where(lane < VDIM, outs[2 * p], pltpu.roll(outs[2 * p + 1], VDIM, 1))

        @pl.when((pl.program_id(0) == HEADS // g - 1) & (qi == s // t - 1))
        def _():
            _wait_all(*_to_all_copies(shard_refs, got_refs, sems, True))

    return pl.pallas_call(
        body,
        grid=(HEADS // g, s // t),
        in_specs=[
            pl.BlockSpec((t, g * LANE), lambda h, i: (i, h)),
            pl.BlockSpec((s, g * LANE), lambda h, i: (0, h)),
            pl.BlockSpec((s, g * LANE), lambda h, i: (0, h)),
        ] + [ANY] * ns,
        out_specs=[
            pl.BlockSpec((t, g * VDIM), lambda h, i: (i, h)),
            pl.BlockSpec((t, g * LANE), lambda h, i: (i, h)),
        ] + [ANY] * ns,
        out_shape=[jax.ShapeDtypeStruct((s, 512), F32), jax.ShapeDtypeStruct((s, 1024), BF16)]
        + [jax.ShapeDtypeStruct((N_DEV,) + b.shape, b.dtype) for b in shards],
        scratch_shapes=[pltpu.VMEM((2, g, t, t), F32)] + _copy_sems(ns, 7),
        compiler_params=_params(("arbitrary", "arbitrary")),
        name="attn_fwd",
    )(qr, kr, vp, *shards)


def _attn_bwd(qa, kr, vp, dop, sends):
    s = qa.shape[0]
    t = ATT_T
    nq = s // t
    ns = len(sends)

    def body(q_ref, k_ref, v_ref, do_ref, *rest):
        send_refs, (dq_out, dk_out, dv_out) = rest[:ns], rest[ns:ns + 3]
        recv_refs = rest[ns + 3:2 * ns + 3]
        (dq_ref, dk_ref, dv_ref), sems = rest[2 * ns + 3:2 * ns + 6], rest[2 * ns + 6:]
        j = pl.program_id(1)
        sls = [slice(LANE * a, LANE * (a + 1)) for a in range(2)]

        @pl.when((pl.program_id(0) == 0) & (j == 0))
        def _():
            _start_all(*_to_all_copies(send_refs, recv_refs, sems, False))

        @pl.when(j == 0)
        def _():
            dq_ref[...] = jnp.zeros_like(dq_ref)

        dk_ref[...] = jnp.zeros_like(dk_ref)
        dv_ref[...] = jnp.zeros_like(dv_ref)
        ks = [k_ref[:, sl] for sl in sls]
        vs = [v_ref[:, sl] for sl in sls]

        def part(i, k_lo, k_n, q_lo, q_n, masked):
            rows = pl.ds(pl.multiple_of(i * t + q_lo, 256), q_n)
            keys = slice(k_lo, k_lo + k_n)
            for a in range(2):
                q = q_ref[rows, sls[a]]
                do = do_ref[rows, sls[a]]
                sc = _dotg(ks[a][keys], q, NT)
                if masked:
                    kc = lax.broadcasted_iota(jnp.int32, (k_n, q_n), 0) >> ATT_CHUNK_SHIFT
                    qc = lax.broadcasted_iota(jnp.int32, (k_n, q_n), 1) >> ATT_CHUNK_SHIFT
                    sc = jnp.where(kc <= qc, sc, -1e30)
                p = jnp.exp(sc)
                ds = (p * _dotg(vs[a][keys], do, NT)).astype(BF16)
                dv_ref[keys, sls[a]] += _dot(p.astype(BF16), do)
                dk_ref[keys, sls[a]] += _dot(ds, q)
                dq_ref[rows, sls[a]] += _dotg(ds, ks[a][keys], TN)

        half = t // 2
        part(j, 0, half, 0, t, True)
        part(j, half, half, half, half, True)

        def loop(i, c):
            part(i, 0, t, 0, t, False)
            return c

        lax.fori_loop(j + 1, nq, loop, 0)
        dk_out[...] = dk_ref[...].astype(BF16)
        dv_out[...] = dv_ref[...].astype(BF16)

        @pl.when(j == nq - 1)
        def _():
            dq_out[...] = dq_ref[...].astype(BF16)

        @pl.when((pl.program_id(0) == HEADS // 2 - 1) & (j == nq - 1))
        def _():
            _wait_all(*_to_all_copies(send_refs, recv_refs, sems, False))

    blk = pl.BlockSpec((t, 2 * LANE), lambda h, j: (j, h))
    whole = pl.BlockSpec((s, 2 * LANE), lambda h, j: (0, h))
    out = jax.ShapeDtypeStruct((s, 1024), BF16)
    return pl.pallas_call(
        body,
        grid=(HEADS // 2, nq),
        in_specs=[whole, blk, blk, whole] + [ANY] * ns,
        out_specs=[whole, blk, blk] + [ANY] * ns,
        out_shape=[out, out, out] + [jax.ShapeDtypeStruct(a.shape, a.dtype) for a in sends],
        scratch_shapes=[pltpu.VMEM((s, 2 * LANE), F32), pltpu.VMEM((t, 2 * LANE), F32),
                        pltpu.VMEM((t, 2 * LANE), F32)] + _copy_sems(ns, 7),
        compiler_params=_params(("arbitrary", "arbitrary")),
        name="attn_bwd",
    )(qa, kr, vp, dop, *sends)


HG_T = 256
HG_NC = HG_T // HG_BLOCK
HG_G = 4
GW = 64 * HG_G


def _hg_consts():
    r = jnp.arange(HG_T)[:, None]
    c = jnp.arange(HG_T)[None, :]
    same = (r // HG_BLOCK) == (c // HG_BLOCK)
    mcum = (same & (c <= r)).astype(BF16)
    mrev = (same & (c >= r)).astype(BF16)
    msum = same.astype(BF16)
    a = jnp.arange(GW) // 64
    bd = (a[:, None] == a[None, :]).astype(F32)
    return mcum, mrev, msum, bd


def _stack_heads(xg, head):
    return jnp.concatenate([jnp.where(head == h, xg, 0.0) for h in range(HG_G)], axis=0)


def _unstack_heads(r, head, t):
    out = r[(HG_G - 1) * t:]
    for h in range(HG_G - 2, -1, -1):
        out = jnp.where(head == h, r[h * t:(h + 1) * t], out)
    return out


def _compact_state(st):
    out = st[:64]
    for h in range(1, HG_G):
        out = out + st[64 * h:64 * (h + 1)]
    return out


def _expand_state(cs, head64):
    return jnp.concatenate([jnp.where(head64 == h, cs, 0.0) for h in range(HG_G)], axis=0)


def _hg_pre(hq, hf, lbl, mcum, msum):
    lb = _sigmoid(lbl[0:1, :] - lbl[1:2, :])
    sig = _sigmoid(hf)
    f = lb + (1.0 - lb) * sig
    lf = jnp.log(f)
    b = _sel_left(mcum, lf)
    big_l = _sel_left(msum, lf)
    k = 1.0 - f
    qd = hq * jnp.exp(b)
    ki = k * jnp.exp(-b)
    ke = k * jnp.exp(big_l - b)
    return lb, sig, f, b, big_l, qd, ki, ke


def _hgrn_fwd(proj, lbl):
    s = proj.shape[0]
    t = HG_T
    mcum, _, msum, bd = _hg_consts()

    def body(hq_ref, hf_ref, hi_ref, lbl_ref, mcum_ref, msum_ref, bd_ref, o_ref, sp_ref, st_ref):
        @pl.when(pl.program_id(0) == 0)
        def _():
            st_ref[...] = jnp.zeros_like(st_ref)

        mc = mcum_ref[...]
        _, _, _, _, big_l, qd, ki, ke = _hg_pre(hq_ref[...], hf_ref[...], lbl_ref[...], mc, msum_ref[...])
        el = jnp.exp(big_l)
        hi = hi_ref[...]
        head = lax.broadcasted_iota(jnp.int32, (t, GW), 1) >> 6
        mask = jnp.concatenate([mc] * HG_G, axis=0) > 0.5
        for p in range(HEADS // HG_G):
            sl = slice(GW * p, GW * (p + 1))
            vp = hi[:, sl].astype(BF16)
            qs = _stack_heads(qd[:, sl], head).astype(BF16)
            a = jnp.where(mask, _dotg(qs, ki[:, sl].astype(BF16), NT), 0.0)
            o_intra = _unstack_heads(_dot(a.astype(BF16), vp), head, t)
            qb = qd[:, sl].astype(BF16)
            kb = ke[:, sl].astype(BF16)
            st = st_ref[p]
            for c in range(HG_NC):
                rows = slice(HG_BLOCK * c, HG_BLOCK * (c + 1))
                sp_ref[c, :, sl] = _compact_state(st)
                o_ref[rows, sl] = o_intra[rows] + _dotg(qb[rows], st.astype(BF16), NT)
                u = _dotg(vp[rows], kb[rows], TN) * bd_ref[...]
                st = st * el[HG_BLOCK * c:HG_BLOCK * c + 1, sl] + u
            st_ref[p] = st

    row = lambda j: pl.BlockSpec((t, HG_WIDTH), lambda i: (i, j))
    full = lambda a: pl.BlockSpec(a.shape, lambda i: (0, 0))
    return pl.pallas_call(
        body,
        grid=(s // t,),
        in_specs=[row(6), row(7), row(8), full(lbl), full(mcum), full(msum), full(bd)],
        out_specs=[row(0), pl.BlockSpec((HG_NC, 64, HG_WIDTH), lambda i: (i, 0, 0))],
        out_shape=[jax.ShapeDtypeStruct((s, HG_WIDTH), F32),
                   jax.ShapeDtypeStruct((s // HG_BLOCK, 64, HG_WIDTH), F32)],
        scratch_shapes=[pltpu.VMEM((HEADS // HG_G, GW, GW), F32)],
        compiler_params=_params(("arbitrary",)),
        name="hgrn_fwd",
    )(proj, proj, proj, lbl, mcum, msum, bd)


def _hgrn_bwd(proj, lbl, do, sprev, dproj):
    s = proj.shape[0]
    t = HG_T
    nt = s // t
    mcum, mrev, msum, bd = _hg_consts()

    def body(hq_ref, hf_ref, hi_ref, lbl_ref, do_ref, sp_ref, mcum_ref, mrev_ref, msum_ref, bd_ref,
             dproj_in, dh_ref, dlbl_ref, g_ref):
        del dproj_in

        @pl.when(pl.program_id(0) == 0)
        def _():
            g_ref[...] = jnp.zeros_like(g_ref)
            dlbl_ref[...] = jnp.zeros_like(dlbl_ref)

        mc = mcum_ref[...]
        lb, sig, f, b, big_l, qd, ki, ke = _hg_pre(hq_ref[...], hf_ref[...], lbl_ref[...], mc, msum_ref[...])
        el = jnp.exp(big_l)
        hi = hi_ref[...]
        dov = do_ref[...]
        head = lax.broadcasted_iota(jnp.int32, (t, GW), 1) >> 6
        head64 = lax.broadcasted_iota(jnp.int32, (64, GW), 1) >> 6
        mask = jnp.concatenate([mc] * HG_G, axis=0) > 0.5
        dqd_parts, dke_parts, dv_parts, del_parts, dki_parts = [], [], [], [], []
        for p in range(HEADS // HG_G):
            sl = slice(GW * p, GW * (p + 1))
            vp = hi[:, sl].astype(BF16)
            qs = _stack_heads(qd[:, sl], head).astype(BF16)
            kip = ki[:, sl].astype(BF16)
            dos = _stack_heads(dov[:, sl], head).astype(BF16)
            a = jnp.where(mask, _dotg(qs, kip, NT), 0.0).astype(BF16)
            da = jnp.where(mask, _dotg(dos, vp, NT), 0.0).astype(BF16)
            r = _dot(da, kip)
            dki_parts.append(_dotg(da, qs, TN))
            qb = qd[:, sl].astype(BF16)
            kb = ke[:, sl].astype(BF16)
            dob = dov[:, sl].astype(BF16)
            g = g_ref[p]
            dqd_c, dv_c, dke_c, del_c = [], [], [], []
            for c in range(HG_NC - 1, -1, -1):
                rows = slice(HG_BLOCK * c, HG_BLOCK * (c + 1))
                gb = g.astype(BF16)
                st = _expand_state(sp_ref[c, :, sl], head64)
                dqd_c.append(_dot(dob[rows], st.astype(BF16)))
                dv_c.append(_dotg(kb[rows], gb, NT))
                dke_c.append(_dot(vp[rows], gb))
                del_c.append(jnp.broadcast_to(jnp.sum(g * st, axis=0, keepdims=True), (HG_BLOCK, GW)))
                g = g * el[HG_BLOCK * c:HG_BLOCK * c + 1, sl] + _dotg(dob[rows], qb[rows], TN) * bd_ref[...]
            g_ref[p] = g
            up = lambda parts: jnp.concatenate(parts[::-1], axis=0)
            dqd_parts.append(_unstack_heads(r, head, t) + up(dqd_c))
            dv_parts.append(_dotg(a, dos, TN) + up(dv_c))
            dke_parts.append(up(dke_c))
            del_parts.append(up(del_c))
        wide = lambda parts: jnp.concatenate(parts, axis=1)
        dqd, dke, dki, dvv, del_rows = wide(dqd_parts), wide(dke_parts), wide(dki_parts), wide(dv_parts), wide(del_parts)
        dh_ref[:, :HG_WIDTH] = (dqd * jnp.exp(b)).astype(BF16)
        dh_ref[:, 2 * HG_WIDTH:] = dvv.astype(BF16)
        dke_ke = dke * ke
        db = dqd * qd - dki * ki - dke_ke
        dl_rows = _sel_left(msum_ref[...], dke_ke) + del_rows * el
        is_last = (lax.broadcasted_iota(jnp.int32, (t, HG_WIDTH), 0) & (HG_BLOCK - 1)) == HG_BLOCK - 1
        db = db + jnp.where(is_last, dl_rows, 0.0)
        dlf = _sel_left(mrev_ref[...], db)
        dk = dki * jnp.exp(-b) + dke * jnp.exp(big_l - b)
        df = dlf / f - dk
        dh_ref[:, HG_WIDTH:2 * HG_WIDTH] = (df * (1.0 - lb) * sig * (1.0 - sig)).astype(BF16)
        dlb = jnp.sum(df * (1.0 - sig), axis=0, keepdims=True) * lb * (1.0 - lb)
        dlbl_ref[0:1, :] += dlb
        dlbl_ref[1:2, :] -= dlb

    rrow = lambda j: pl.BlockSpec((t, HG_WIDTH), lambda i: (nt - 1 - i, j))
    full = lambda a: pl.BlockSpec(a.shape, lambda i: (0, 0))
    return pl.pallas_call(
        body,
        grid=(nt,),
        in_specs=[rrow(6), rrow(7), rrow(8), full(lbl), rrow(0),
                  pl.BlockSpec((HG_NC, 64, HG_WIDTH), lambda i: (nt - 1 - i, 0, 0)),
                  full(mcum), full(mrev), full(msum), full(bd), pl.BlockSpec(memory_space=pl.ANY)],
        out_specs=[pl.BlockSpec((t, 3 * HG_WIDTH), lambda i: (nt - 1 - i, 2)),
                   pl.BlockSpec((2, HG_WIDTH), lambda i: (0, 0))],
        out_shape=[jax.ShapeDtypeStruct(dproj.shape, BF16), jax.ShapeDtypeStruct((2, HG_WIDTH), F32)],
        input_output_aliases={10: 0},
        scratch_shapes=[pltpu.VMEM((HEADS // HG_G, GW, GW), F32)],
        compiler_params=_params(("arbitrary",)),
        name="hgrn_bwd",
    )(proj, proj, proj, lbl, do, sprev, mcum, mrev, msum, bd, dproj)


def _tail(x, tgt, proj, attn, o, w_a, w_b, w_out, w_at, w_bt, w_outt, b_gate, g_post, gh):
    s = x.shape[0]
    tm = 256
    ones64 = (jnp.arange(HG_WIDTH)[:, None] // 64 == jnp.arange(HG_WIDTH)[None, :] // 64).astype(BF16)
    weights = (w_a, w_b, w_out, w_at, w_bt, w_outt)

    def body(x_ref, t_ref, ml_ref, ga_ref, gb_ref, at_ref, o_ref, *rest):
        w_hbm, (bg_ref, gp_ref, gh_ref, ones_ref) = rest[:6], rest[6:10]
        (dout_ref, dpj_ref, dop_ref, do_ref, mt_ref, dy_ref, yat_ref, dya_ref, ybt_ref, dyb_ref,
         loss_ref, dgp_ref, dbg_ref, dgh_ref) = rest[10:24]
        (wa_ref, wb_ref, wo_ref, wat_ref, wbt_ref, wot_ref), w_sem = rest[24:30], rest[30]

        @pl.when(pl.program_id(0) == 0)
        def _():
            loads = [pltpu.make_async_copy(src, dst, w_sem.at[k])
                     for k, (src, dst) in enumerate(zip(w_hbm, rest[24:30]))]
            _start_all(loads, [])
            loss_ref[...] = jnp.zeros_like(loss_ref)
            dgp_ref[...] = jnp.zeros_like(dgp_ref)
            dbg_ref[...] = jnp.zeros_like(dbg_ref)
            dgh_ref[...] = jnp.zeros_like(dgh_ref)
            _wait_all(loads, [])

        ones = ones_ref[...]
        gate_a = ga_ref[...]
        sa = _sigmoid(gate_a)
        silu_a = gate_a * sa
        attn_v = at_ref[...]
        ya_in = attn_v * silu_a
        ov = o_ref[...]
        ro = lax.rsqrt(_sel_right(ov * ov, ones) * (1.0 / 64.0) + EPS)
        ohat = ov * ro
        ghv = gh_ref[...]
        on = ohat * ghv
        gate_b = gb_ref[...]
        sb = _sigmoid(gate_b)
        silu_b = gate_b * sb
        yb_in = on * silu_b
        ya_bf = ya_in.astype(BF16)
        yb_bf = yb_in.astype(BF16)
        yat_ref[...] = ya_bf.T
        ybt_ref[...] = yb_bf.T
        y_a = _dot(ya_bf, wa_ref[...])
        y_b = _dot(yb_bf, wb_ref[...])
        gts = _sigmoid(ml_ref[...] + bg_ref[...])
        g_a = gts[:, :D_MODEL]
        g_b = gts[:, D_MODEL:]
        m_bf = (g_a * y_a + g_b * y_b).astype(BF16)
        mt_ref[...] = m_bf.T
        y = _dot(m_bf, wo_ref[...])
        r1 = lax.rsqrt(jnp.mean(y * y, axis=-1, keepdims=True) + EPS)
        yn = y * r1
        gp = gp_ref[...]
        e = x_ref[...] + yn * gp - t_ref[...]
        loss_ref[...] += jnp.sum(e * e, axis=0, keepdims=True)
        dout = e * (1.0 / D_MODEL)
        dout_ref[...] = dout
        dgp_ref[...] += jnp.sum(dout * yn, axis=0, keepdims=True)
        dyn = dout * gp
        dy = r1 * (dyn - yn * jnp.mean(dyn * yn, axis=-1, keepdims=True))
        dy_bf = dy.astype(BF16)
        dy_ref[...] = dy_bf
        dm = _dot(dy_bf, wot_ref[...])
        dml_a = dm * y_a * g_a * (1.0 - g_a)
        dml_b = dm * y_b * g_b * (1.0 - g_b)
        dpj_ref[:, :D_MODEL] = dml_a.astype(BF16)
        dpj_ref[:, D_MODEL:2 * D_MODEL] = dml_b.astype(BF16)
        dbg_ref[:, :D_MODEL] += jnp.sum(dml_a, axis=0, keepdims=True)
        dbg_ref[:, D_MODEL:] += jnp.sum(dml_b, axis=0, keepdims=True)
        dya_bf = (dm * g_a).astype(BF16)
        dyb_bf = (dm * g_b).astype(BF16)
        dya_ref[...] = dya_bf
        dyb_ref[...] = dyb_bf
        dya_in = _dot(dya_bf, wat_ref[...])
        dyb_in = _dot(dyb_bf, wbt_ref[...])
        dattn = dya_in * silu_a
        delta = _sel_right(dattn * attn_v, ones)
        lane = lax.broadcasted_iota(jnp.int32, (tm, LANE), 1)
        for p in range(HEADS // 2):
            sl = slice(LANE * p, LANE * (p + 1))
            xs = (dattn[:, sl], pltpu.roll(dattn[:, sl], VDIM, 1))
            nds = (-pltpu.roll(delta[:, sl], VDIM, 1), -delta[:, sl])
            for a in range(2):
                hi, lo_part = _hi_lo(nds[a])
                blk = jnp.where(lane < VDIM, xs[a], jnp.where(lane == VDIM, hi, jnp.where(lane == VDIM + 1, lo_part, 0.0)))
                dop_ref[:, LANE * (2 * p + a):LANE * (2 * p + a + 1)] = blk.astype(BF16)
        dpj_ref[:, 2 * D_MODEL:2 * D_MODEL + HG_WIDTH] = (
            dya_in * attn_v * (sa * (1.0 + gate_a * (1.0 - sa)))).astype(BF16)
        don = dyb_in * silu_b
        dpj_ref[:, 2 * D_MODEL + HG_WIDTH:] = (dyb_in * on * (sb * (1.0 + gate_b * (1.0 - sb)))).astype(BF16)
        dgh_ref[...] += jnp.sum(don * ohat, axis=0, keepdims=True)
        dohat = don * ghv
        do_ref[...] = ro * (dohat - ohat * (_sel_right(dohat * ohat, ones) * (1.0 / 64.0)))

    row = lambda w, j: pl.BlockSpec((tm, w), lambda i: (i, j))
    col = lambda w: pl.BlockSpec((w, tm), lambda i: (0, i))
    full = lambda a: pl.BlockSpec(a.shape, lambda i: (0, 0))
    acc = lambda w: pl.BlockSpec((1, w), lambda i: (0, 0))
    sds = lambda w, dt: jax.ShapeDtypeStruct((s, w), dt)
    sdt = lambda w: jax.ShapeDtypeStruct((w, s), BF16)
    return pl.pallas_call(
        body,
        grid=(s // tm,),
        in_specs=[row(1024, 0), row(1024, 0), row(2048, 0), row(512, 4), row(512, 5), row(512, 0), row(512, 0)]
        + [ANY] * 6 + [full(b_gate), full(g_post), full(gh), full(ones64)],
        out_specs=[row(1024, 0), row(3072, 0), row(1024, 0), row(512, 0),
                   col(1024), row(1024, 0), col(512), row(1024, 0), col(512), row(1024, 0),
                   acc(1024), acc(1024), acc(2048), acc(512)],
        out_shape=[sds(1024, F32), sds(D_IN_PAD, BF16), sds(1024, BF16), sds(512, F32),
                   sdt(1024), sds(1024, BF16), sdt(512), sds(1024, BF16), sdt(512), sds(1024, BF16),
                   jax.ShapeDtypeStruct((1, 1024), F32), jax.ShapeDtypeStruct((1, 1024), F32),
                   jax.ShapeDtypeStruct((1, 2048), F32), jax.ShapeDtypeStruct((1, 512), F32)],
        scratch_shapes=[pltpu.VMEM(a.shape, BF16) for a in weights] + [pltpu.SemaphoreType.DMA((6,))],
        compiler_params=_params(("arbitrary",), 56),
        name="tail",
    )(x, tgt, proj, proj, proj, attn, o, *weights, b_gate, g_post, gh, ones64)


def _mla_bwd(proj, dqr, dkr, dv, g_q, g_kv, w_uq_pt, w_kv_pt, rc, rs1, rs2, dproj):
    s = proj.shape[0]
    tm = 256
    scale = 1.0 / math.sqrt(QK)

    def body(cq_ref, ckv_ref, dqr_ref, dkr_ref, dv_ref, gq_ref, gkv_ref, wuqt_ref, wkvt_ref, c_ref, s1_ref, s2_ref,
             dproj_in, dqf_ref, dkvf_ref, dc_ref, dgq_ref, dgkv_ref):
        del dproj_in

        @pl.when(pl.program_id(0) == 0)
        def _():
            dgq_ref[...] = jnp.zeros_like(dgq_ref)
            dgkv_ref[...] = jnp.zeros_like(dgkv_ref)

        c, s1, s2 = c_ref[...], s1_ref[...], s2_ref[...]
        lane = lax.broadcasted_iota(jnp.int32, (tm, LANE), 1)
        ksum = jnp.zeros((tm, LANE), F32)
        for h in range(HEADS):
            sl = slice(LANE * h, LANE * (h + 1))
            dqf_ref[:, sl] = (_unrope(dqr_ref[:, sl], c, s1, s2) * scale).astype(BF16)
            dkh = dkr_ref[:, sl]
            ksum = ksum + dkh
            dkvf_ref[:, sl] = jnp.where(lane < NOPE, dkh, 0.0).astype(BF16)
            dkvf_ref[:, HEADS * LANE + LANE * h:HEADS * LANE + LANE * (h + 1)] = jnp.where(
                lane < VDIM, dv_ref[:, sl], 0.0).astype(BF16)
        dkpe = _unrope(ksum, c, s1, s2)
        dc_ref[:, Q_LORA + KV_LORA:] = jnp.where((lane >= NOPE) & (lane < QK), dkpe, 0.0).astype(BF16)
        dcqn = _dot(dqf_ref[...], wuqt_ref[...])
        dckvn = _dot(dkvf_ref[...], wkvt_ref[...])
        for x_ref, g_ref, dn, cols, dg_ref in ((cq_ref, gq_ref, dcqn, slice(0, Q_LORA), dgq_ref),
                                               (ckv_ref, gkv_ref, dckvn, slice(Q_LORA, Q_LORA + KV_LORA), dgkv_ref)):
            xv = x_ref[...]
            r = lax.rsqrt(jnp.mean(xv * xv, axis=-1, keepdims=True) + EPS)
            xh = xv * r
            dg_ref[...] += jnp.sum(dn * xh, axis=0, keepdims=True)
            dh = dn * g_ref[...]
            dc_ref[:, cols] = (r * (dh - xh * jnp.mean(dh * xh, axis=-1, keepdims=True))).astype(BF16)

    row = lambda w, j: pl.BlockSpec((tm, w), lambda i: (i, j))
    full = lambda a: pl.BlockSpec(a.shape, lambda i: (0, 0))
    acc = lambda w: pl.BlockSpec((1, w), lambda i: (0, 0))
    sds = lambda w, dt: jax.ShapeDtypeStruct((s, w), dt)
    return pl.pallas_call(
        body,
        grid=(s // tm,),
        in_specs=[row(768, 6), row(256, 21), row(1024, 0), row(1024, 0), row(1024, 0), full(g_q), full(g_kv),
                  full(w_uq_pt), full(w_kv_pt), row(128, 0), row(128, 0), row(128, 0),
                  pl.BlockSpec(memory_space=pl.ANY)],
        out_specs=[row(1024, 0), row(2048, 0), row(1152, 4), acc(768), acc(256)],
        out_shape=[sds(1024, BF16), sds(2048, BF16), jax.ShapeDtypeStruct(dproj.shape, BF16),
                   jax.ShapeDtypeStruct((1, 768), F32), jax.ShapeDtypeStruct((1, 256), F32)],
        input_output_aliases={12: 2},
        compiler_params=_params(("arbitrary",)),
        name="mla_bwd",
    )(proj, proj, dqr, dkr, dv, g_q, g_kv, w_uq_pt, w_kv_pt, rc, rs1, rs2, dproj)


def _pick(n, options):
    for o in options:
        if n % o == 0:
            return o
    raise ValueError(n)


def _matmul(a, b, name):
    m, k = a.shape
    n = b.shape[1]
    tm = _pick(m, (1024, 768, 512, 256))
    tn = _pick(n, (1152, 1024, 768, 512))
    tk = _pick(k, (1024, 512))
    nk = k // tk

    def body(a_ref, b_ref, o_ref):
        @pl.when(pl.program_id(2) == 0)
        def _():
            o_ref[...] = jnp.zeros_like(o_ref)

        o_ref[...] += _dot(a_ref[...], b_ref[...])

    return pl.pallas_call(
        body,
        grid=(m // tm, n // tn, nk),
        in_specs=[pl.BlockSpec((tm, tk), lambda i, j, l: (i, l)), pl.BlockSpec((tk, tn), lambda i, j, l: (l, j))],
        out_specs=pl.BlockSpec((tm, tn), lambda i, j, l: (i, j)),
        out_shape=jax.ShapeDtypeStruct((m, n), F32),
        compiler_params=_params(("arbitrary", "arbitrary", "arbitrary")),
        name=name,
    )(a, b)


def _dh_dx(dproj, w_in_pt, x, dout, g_pre, sends):
    s, k = dproj.shape
    tm = 256
    ns, ni = len(sends), s // tm

    def body(dp_ref, w_ref, x_ref, dout_ref, g_ref, *rest):
        send_refs, (dx_ref, dg_ref) = rest[:ns], rest[ns:ns + 2]
        recv_refs, sems = rest[ns + 2:2 * ns + 2], rest[2 * ns + 2:]

        @pl.when(pl.program_id(0) == 0)
        def _():
            _start_all(*_to_chips_copies(send_refs, recv_refs, sems))
            dg_ref[...] = jnp.zeros_like(dg_ref)

        dh = _dot(dp_ref[...], w_ref[...])
        xv = x_ref[...]
        r = lax.rsqrt(jnp.mean(xv * xv, axis=-1, keepdims=True) + EPS)
        xh = xv * r
        dg_ref[...] += jnp.sum(dh * xh, axis=0, keepdims=True)
        dxh = dh * g_ref[...]
        dx_ref[...] = dout_ref[...] + r * (dxh - xh * jnp.mean(dxh * xh, axis=-1, keepdims=True))

        @pl.when(pl.program_id(0) == ni - 1)
        def _():
            _wait_all(*_to_chips_copies(send_refs, recv_refs, sems))

    row = lambda w: pl.BlockSpec((tm, w), lambda i: (i, 0))
    return pl.pallas_call(
        body,
        grid=(ni,),
        in_specs=[row(k), pl.BlockSpec((k, D_MODEL), lambda i: (0, 0)), row(D_MODEL), row(D_MODEL),
                  pl.BlockSpec((1, D_MODEL), lambda i: (0, 0))] + [ANY] * ns,
        out_specs=[row(D_MODEL), pl.BlockSpec((1, D_MODEL), lambda i: (0, 0))] + [ANY] * ns,
        out_shape=[jax.ShapeDtypeStruct((s, D_MODEL), F32), jax.ShapeDtypeStruct((1, D_MODEL), F32)]
        + [jax.ShapeDtypeStruct(a.shape, a.dtype) for a in sends],
        scratch_shapes=_copy_sems(ns, 3),
        compiler_params=_params(("arbitrary",)),
        name="dh_dx",
    )(dproj, w_in_pt, x, dout, g_pre, *sends)


def _pair_reduce(slots):
    n = len(slots)
    half = [(N_DEV // 2,) + a.shape[1:] for a in slots]

    def body(*refs):
        s_refs, o_refs = refs[:n], refs[n:2 * n]
        mine, got = refs[2 * n:3 * n], refs[3 * n:4 * n]
        send_sems, recv_sems, local_sems = refs[4 * n:]
        x, y, c = _my_place()
        copies, loads = [], []
        for a in range(n):
            for q in range(N_DEV // 2):
                copies.append(pltpu.make_async_remote_copy(
                    src_ref=s_refs[a].at[2 * q + 1 - c], dst_ref=got[a].at[q],
                    send_sem=send_sems.at[4 * a + q], recv_sem=recv_sems.at[4 * a + q],
                    device_id=(x, y, 1 - c), device_id_type=MESH_ID))
                loads.append(pltpu.make_async_copy(s_refs[a].at[2 * q + c], mine[a].at[q], local_sems.at[4 * a + q]))
        _start_all(loads, copies)
        _wait_all(loads, copies)
        for a in range(n):
            o_refs[a][...] = (mine[a][...].astype(F32) + got[a][...].astype(F32)).astype(o_refs[a].dtype)

    vm = lambda: [pltpu.VMEM(h, a.dtype) for h, a in zip(half, slots)]
    return pl.pallas_call(
        body,
        in_specs=[ANY] * n,
        out_shape=[jax.ShapeDtypeStruct(h, a.dtype) for h, a in zip(half, slots)],
        scratch_shapes=vm() + vm() + [pltpu.SemaphoreType.DMA((4 * n,)), pltpu.SemaphoreType.DMA((4 * n,)),
                                      pltpu.SemaphoreType.DMA((4 * n,))],
        compiler_params=pltpu.CompilerParams(vmem_limit_bytes=48 * 2**20),
        name="pair_reduce",
    )(*slots)


def _rope_tables(s):
    inv = (np.float32(ROPE_THETA) ** (-np.arange(0, ROPE, 2, dtype=np.float32) / np.float32(ROPE))).astype(np.float32)
    ang = (np.arange(s, dtype=np.float32)[:, None] * inv[None, :]).astype(np.float32)
    cos, sin = jnp.asarray(np.cos(ang.astype(np.float64)), F32), jnp.asarray(np.sin(ang.astype(np.float64)), F32)
    z = lambda w: jnp.zeros((s, w), F32)
    rc = jnp.concatenate([jnp.ones((s, NOPE), F32), cos, cos, z(32)], axis=1)
    rs1 = jnp.concatenate([z(NOPE), -sin, z(16), z(32)], axis=1)
    rs2 = jnp.concatenate([z(NOPE), z(16), sin, z(32)], axis=1)
    return rc, rs1, rs2


def _step(x, tgt, w_blk, shards, g_pre, b_gate, g_q, g_kv, lbl, g_hgrn, g_post):
    s = x.shape[0]
    rc, rs1, rs2 = _rope_tables(s)
    gh = jnp.tile(g_hgrn, (1, HEADS))

    proj, w_in_pt, ht, *got = _gather_proj(x, g_pre, w_blk, shards[:2])
    w_uq, w_ukv = (_from_slots(n, g) for n, g in zip(MATS[:2], got))
    w_uq_p = jnp.pad(w_uq.reshape(Q_LORA, HEADS, QK), ((0, 0), (0, 0), (0, LANE - QK))).reshape(Q_LORA, HEADS * LANE)
    kv3 = w_ukv.reshape(KV_LORA, HEADS, NOPE + VDIM)
    pad64 = lambda t: jnp.pad(t, ((0, 0), (0, 0), (0, LANE - 64))).reshape(KV_LORA, HEADS * LANE)
    w_kv_p = jnp.concatenate([pad64(kv3[:, :, :NOPE]), pad64(kv3[:, :, NOPE:])], axis=1)

    qr, kr, v, cqt, ckvt = _mla_prep(proj, g_q, g_kv, w_uq_p, w_kv_p, rc, rs1, rs2)
    attn, qa, *got = _attn_fwd(qr, kr, v, shards[2:])
    w_a, w_b, w_out = (_from_slots(n, g) for n, g in zip(MATS[2:], got))
    o, sprev = _hgrn_fwd(proj, lbl)
    (dout, dproj, dop, do, mt, dy_bf, yat, dya_bf, ybt, dyb_bf,
     loss_vec, dg_post, db_gate, dgh) = _tail(x, tgt, proj, attn, o, w_a, w_b, w_out, w_a.T, w_b.T, w_out.T,
                                               b_gate, g_post, gh)
    early = [_to_slots(n, _matmul(a, b, "d" + n)).astype(BF16)
             for n, a, b in (("w_branch_a", yat, dya_bf), ("w_branch_b", ybt, dyb_bf), ("w_out", mt, dy_bf))]
    dqr, dkr, dv, *early_recv = _attn_bwd(qa, kr, v, dop, early)
    dproj, dlbl = _hgrn_bwd(proj, lbl, do, sprev, dproj)
    dqf, dkvf, dproj, dg_q, dg_kv = _mla_bwd(proj, dqr, dkr, dv, g_q, g_kv, w_uq_p.T, w_kv_p.T, rc, rs1, rs2, dproj)

    dw_in_slots = _dw_in_slots(ht, dproj)
    dw_uq_p = _matmul(cqt, dqf, "dw_uq")
    dw_kv_p = _matmul(ckvt, dkvf, "dw_kv")
    dw_uq = dw_uq_p.reshape(Q_LORA, HEADS, LANE)[:, :, :QK].reshape(Q_LORA, HEADS * QK)
    dw_ukv = jnp.concatenate([dw_kv_p[:, :HEADS * LANE].reshape(KV_LORA, HEADS, LANE)[:, :, :NOPE],
                              dw_kv_p[:, HEADS * LANE:].reshape(KV_LORA, HEADS, LANE)[:, :, :VDIM]],
                             axis=2).reshape(KV_LORA, 1024)
    late = _pair_reduce([dw_in_slots, _to_slots("w_uq", dw_uq).astype(BF16), _to_slots("w_ukv", dw_ukv).astype(BF16)])
    dx, dg_pre, *late_recv = _dh_dx(dproj, w_in_pt, x, dout, g_pre, late)

    g_sum = _vectors_sum(dg_pre, db_gate, dg_q, dg_kv, dlbl, dgh, dg_post, loss_vec)
    return dx, late_recv[0], dict(zip(MATS, late_recv[1:] + early_recv)), g_sum


def _adamw(g, w, m, v):
    c1 = 1.0 / (1.0 - ADAM_B1 ** ADAM_STEP)
    c2 = 1.0 / (1.0 - ADAM_B2 ** ADAM_STEP)
    nm = ADAM_B1 * m + (1.0 - ADAM_B1) * g
    nv = ADAM_B2 * v + (1.0 - ADAM_B2) * (g * g)
    d = -ADAM_LR * ((nm * c1) / (jnp.sqrt(nv * c2) + ADAM_EPS) + ADAM_WD * w)
    return d, nm, nv


def _sum8(r_ref):
    g = r_ref[0].astype(F32)
    for k in range(1, r_ref.shape[0]):
        g = g + r_ref[k].astype(F32)
    return g


def _sum_adamw_w_in(recv, w, m, v):
    rows, _, cols = w.shape
    tc = 256

    def body(r_ref, w_ref, m_ref, v_ref, g_ref, d_ref, nm_ref, nv_ref):
        g = _sum8(r_ref)
        dense = lambda ref: ref[...].reshape(rows, tc)
        d, nm, nv = _adamw(g, dense(w_ref), dense(m_ref), dense(v_ref))
        for ref, val in ((g_ref, g), (d_ref, d), (nm_ref, nm), (nv_ref, nv)):
            ref[...] = val.reshape(rows, 1, tc)

    blk = pl.BlockSpec((rows, 1, tc), lambda i: (0, 0, i))
    out = jax.ShapeDtypeStruct((rows, 1, cols), F32)
    return pl.pallas_call(
        body,
        grid=(cols // tc,),
        in_specs=[pl.BlockSpec((recv.shape[0], rows, tc), lambda i: (0, 0, i)), blk, blk, blk],
        out_specs=[blk, blk, blk, blk],
        out_shape=[out, out, out, out],
        compiler_params=_params(("arbitrary",)),
        name="sum_adamw_w_in",
    )(recv, w, m, v)


def _sum_adamw_whole(recvs, ws, ms, vs):
    n = len(ws)

    def body(*refs):
        r_refs, w_refs, m_refs, v_refs = refs[:n], refs[n:2 * n], refs[2 * n:3 * n], refs[3 * n:4 * n]
        outs = refs[4 * n:]
        for a in range(n):
            g = _sum8(r_refs[a])
            d, nm, nv = _adamw(g, w_refs[a][...], m_refs[a][...], v_refs[a][...])
            outs[a][...] = g
            outs[n + a][...] = d
            outs[2 * n + a][...] = nm
            outs[3 * n + a][...] = nv

    shapes = [jax.ShapeDtypeStruct(w.shape, F32) for w in ws]
    res = pl.pallas_call(
        body,
        out_shape=shapes * 4,
        compiler_params=pltpu.CompilerParams(vmem_limit_bytes=48 * 2**20),
        name="sum_adamw_mats",
    )(*recvs, *ws, *ms, *vs)
    return res[:n], res[n:2 * n], res[2 * n:3 * n], res[3 * n:]


SMALL = ("g_pre", "b_gate", "g_q", "g_kv", "lb_logits", "g_hgrn", "g_post")
SMALL_SHAPE = dict(g_pre=(1, 1024), b_gate=(1, 2048), g_q=(1, 768), g_kv=(1, 256), lb_logits=(2, 512),
                   g_hgrn=(1, 64), g_post=(1, 1024))


def _vectors_sum(dg_pre, db_gate, dg_q, dg_kv, dlbl, dgh, dg_post, loss_vec):
    def body(gpre_ref, bg_ref, gq_ref, gkv_ref, lbl_ref, gh_ref, gpost_ref, loss_ref, out_ref, mine, got,
             send_sems, recv_sems):
        mine[...] = jnp.zeros_like(mine)
        mine[0:1, :] = gpre_ref[...]
        mine[1:2, :] = bg_ref[:, :1024]
        mine[2:3, :] = bg_ref[:, 1024:]
        mine[3:4, :Q_LORA] = gq_ref[...]
        mine[4:5, :KV_LORA] = gkv_ref[...]
        loss = (0.5 / D_MODEL) * jnp.sum(loss_ref[...], axis=-1, keepdims=True)
        mine[4:5, KV_LORA:] = jnp.broadcast_to(loss, (1, 1024 - KV_LORA))
        mine[5:6, :HG_WIDTH] = lbl_ref[0:1, :]
        mine[5:6, HG_WIDTH:] = lbl_ref[1:2, :]
        gh = gh_ref[...]
        fold = gh[:, :VDIM]
        for h in range(1, HEADS):
            fold = fold + gh[:, VDIM * h:VDIM * (h + 1)]
        mine[6:7, :VDIM] = fold
        mine[7:8, :] = gpost_ref[...]
        x, y, c = _my_place()
        me = 4 * x + 2 * y + c
        got[me] = mine[...]
        copies = [pltpu.make_async_remote_copy(
            src_ref=mine, dst_ref=got.at[me], send_sem=send_sems.at[k], recv_sem=recv_sems.at[k],
            device_id=_flip(k, x, y, c), device_id_type=MESH_ID) for k in range(N_DEV - 1)]
        _start_all([], copies)
        _wait_all([], copies)
        out_ref[...] = _sum8(got)

    return pl.pallas_call(
        body,
        out_shape=jax.ShapeDtypeStruct((8, 1024), F32),
        scratch_shapes=[pltpu.VMEM((8, 1024), F32), pltpu.VMEM((N_DEV, 8, 1024), F32),
                        pltpu.SemaphoreType.DMA((7,)), pltpu.SemaphoreType.DMA((7,))],
        name="vectors_sum",
    )(dg_pre, db_gate, dg_q, dg_kv, dlbl, dgh, dg_post, loss_vec)


def _vectors_adamw(g_sum, ws, ms, vs):
    n = len(SMALL)

    def body(g_ref, *refs):
        w_refs, m_refs, v_refs = refs[:n], refs[n:2 * n], refs[2 * n:3 * n]
        loss_ref, outs = refs[3 * n], refs[3 * n + 1:]
        g = g_ref[...]
        loss_ref[...] = g[4:5, KV_LORA:KV_LORA + 1]
        grads = (g[0:1, :], jnp.concatenate([g[1:2, :], g[2:3, :]], axis=1), g[3:4, :Q_LORA], g[4:5, :KV_LORA],
                 jnp.concatenate([g[5:6, :HG_WIDTH], g[5:6, HG_WIDTH:]], axis=0), g[6:7, :VDIM], g[7:8, :])
        for a in range(n):
            d, nm, nv = _adamw(grads[a], w_refs[a][...], m_refs[a][...], v_refs[a][...])
            outs[a][...] = grads[a]
            outs[n + a][...] = d
            outs[2 * n + a][...] = nm
            outs[3 * n + a][...] = nv

    shapes = [jax.ShapeDtypeStruct(SMALL_SHAPE[k], F32) for k in SMALL]
    res = pl.pallas_call(
        body,
        out_shape=[jax.ShapeDtypeStruct((1, 1), F32)] + shapes * 4,
        name="vectors_adamw",
    )(g_sum, *ws, *ms, *vs)
    return res[0], res[1:n + 1], res[n + 1:2 * n + 1], res[2 * n + 1:3 * n + 1], res[3 * n + 1:]


MATS = ("w_uq", "w_ukv", "w_branch_a", "w_branch_b", "w_out")
COL_SHARDED = dict(w_uq=False, w_ukv=True, w_branch_a=True, w_branch_b=True, w_out=False)
ORDER = ("g_pre", "w_in", "b_gate", "g_q", "w_uq", "g_kv", "w_ukv", "lb_logits", "g_hgrn",
         "w_branch_a", "w_branch_b", "w_out", "g_post")


def _to_slots(name, full):
    r, c = full.shape
    if COL_SHARDED[name]:
        return full.reshape(r, N_DEV, c // N_DEV).transpose(1, 0, 2)
    return full.reshape(N_DEV, r // N_DEV, c)


def _from_slots(name, slots):
    _, r, c = slots.shape
    if COL_SHARDED[name]:
        return slots.transpose(1, 0, 2).reshape(r, N_DEV * c)
    return slots.reshape(N_DEV * r, c)


def kernel(x, g_pre, w_in, b_gate, g_q, w_uq, g_kv, w_ukv, lb_logits, g_hgrn, w_branch_a, w_branch_b, w_out, g_post, loss_target, m_g_pre, m_w_in, m_b_gate, m_g_q, m_w_uq, m_g_kv, m_w_ukv, m_lb_logits, m_g_hgrn, m_w_branch_a, m_w_branch_b, m_w_out, m_g_post, v_g_pre, v_w_in, v_b_gate, v_g_q, v_w_uq, v_g_kv, v_w_ukv, v_lb_logits, v_g_hgrn, v_w_branch_a, v_w_branch_b, v_w_out, v_g_post):
    rows3 = lambda a: jnp.transpose(a, (2, 0, 1))
    w = dict(w_in=rows3(w_in), w_uq=w_uq[0], w_ukv=w_ukv[0], w_branch_a=w_branch_a[0], w_branch_b=w_branch_b[0],
             w_out=w_out[0], g_pre=g_pre, b_gate=b_gate, g_q=g_q, g_kv=g_kv, lb_logits=lb_logits, g_hgrn=g_hgrn,
             g_post=g_post)
    mom = dict(w_in=rows3(m_w_in), w_uq=m_w_uq[0], w_ukv=m_w_ukv[0], w_branch_a=m_w_branch_a[0],
               w_branch_b=m_w_branch_b[0], w_out=m_w_out[0], g_pre=m_g_pre, b_gate=m_b_gate, g_q=m_g_q, g_kv=m_g_kv,
               lb_logits=m_lb_logits, g_hgrn=m_g_hgrn, g_post=m_g_post)
    var = dict(w_in=rows3(v_w_in), w_uq=v_w_uq[0], w_ukv=v_w_ukv[0], w_branch_a=v_w_branch_a[0],
               w_branch_b=v_w_branch_b[0], w_out=v_w_out[0], g_pre=v_g_pre, b_gate=v_b_gate, g_q=v_g_q, g_kv=v_g_kv,
               lb_logits=v_lb_logits, g_hgrn=v_g_hgrn, g_post=v_g_post)

    w_blk = w["w_in"].reshape(W_IN_SHARD, D_MODEL).astype(BF16)
    dx, recv_in, recv, g_sum = _step(x[0], loss_target[0], w_blk, [w[n].astype(BF16) for n in MATS],
                                     g_pre, b_gate, g_q, g_kv, lb_logits, g_hgrn, g_post)

    g_in, d_in, m_in, v_in = _sum_adamw_w_in(recv_in, w["w_in"], mom["w_in"], var["w_in"])
    res = _sum_adamw_whole([recv[n] for n in MATS], *([t[n] for n in MATS] for t in (w, mom, var)))
    total, *vec = _vectors_adamw(g_sum, *([t[n] for n in SMALL] for t in (w, mom, var)))

    outs = []
    for mats, vecs, big in zip(res, vec, (g_in, d_in, m_in, v_in)):
        t = {**{n: a[None] for n, a in zip(MATS, mats)}, **dict(zip(SMALL, vecs)),
             "w_in": jnp.transpose(big, (1, 2, 0))}
        outs += [t[n] for n in ORDER]
    return (total.reshape(()), dx[None], *outs)
```

```python
import math

import jax
import jax.numpy as jnp
import numpy as np
from jax import lax
from jax.experimental import pallas as pl
from jax.experimental.pallas import tpu as pltpu

F32, BF16 = jnp.float32, jnp.bfloat16

D_MODEL = 1024
EPS = 1e-6
HEADS = 8
NOPE, ROPE, VDIM = 64, 32, 64
QK = NOPE + ROPE
Q_LORA, KV_LORA = 768, 256
ROPE_THETA = 10000.0
ATT_CHUNK_SHIFT = 6
HG_BLOCK = 32
HG_WIDTH = 512
D_IN = 5664
D_IN_PAD = 5760
W_IN_SHARD = D_IN // 8
N_DEV = 8
LANE = 128

ADAM_LR, ADAM_B1, ADAM_B2, ADAM_EPS, ADAM_WD, ADAM_STEP = 0.001, 0.9, 0.999, 1e-08, 0.01, 10

W_IN_SEGMENTS = ((3616, 5664, 0), (1056, 1568, 2048), (3104, 3616, 2560), (1568, 3104, 3072),
                 (0, 1024, 4608), (1024, 1056, 5696))

NT = (((1,), (1,)), ((), ()))
TN = (((0,), (0,)), ((), ()))
MESH_ID = pl.DeviceIdType.MESH


def _w_in_pieces():
    out = []
    for lo, hi, dst in W_IN_SEGMENTS:
        c = lo
        while c < hi:
            p = c // W_IN_SHARD
            e = min(hi, (p + 1) * W_IN_SHARD)
            out.append((p, c - p * W_IN_SHARD, e - p * W_IN_SHARD, dst + c - lo))
            c = e
    return out


def _params(sem, vmem_mb=48):
    return pltpu.CompilerParams(dimension_semantics=sem, vmem_limit_bytes=vmem_mb * 2**20)


def _dot(a, b):
    return jnp.dot(a, b, preferred_element_type=F32)


def _dotg(a, b, dims):
    return lax.dot_general(a, b, dims, preferred_element_type=F32)


def _split2(x):
    hi = x.astype(BF16)
    return hi, (x - hi.astype(F32)).astype(BF16)


def _sel_left(m01, x):
    hi, lo = _split2(x)
    return _dot(m01, hi) + _dot(m01, lo)


def _sel_right(x, m01):
    hi, lo = _split2(x)
    return _dot(hi, m01) + _dot(lo, m01)


def _hi_lo(x):
    hi = x.astype(BF16).astype(F32)
    return hi, x - hi


def _sigmoid(x):
    return 0.5 * jnp.tanh(0.5 * x) + 0.5


def _rope(x, c, s1, s2):
    return x * c + pltpu.roll(x, 112, 1) * s1 + pltpu.roll(x, 16, 1) * s2


def _unrope(d, c, s1, s2):
    return d * c + pltpu.roll(d * s1, 16, 1) + pltpu.roll(d * s2, 112, 1)


def _my_place():
    return lax.axis_index("x"), lax.axis_index("y"), lax.axis_index("c")


def _flip(k, x, y, c):
    fx, fy, fc = (k + 1) >> 2 & 1, (k + 1) >> 1 & 1, (k + 1) & 1
    return (1 - x if fx else x), (1 - y if fy else y), (1 - c if fc else c)


def _to_all_copies(s_refs, r_refs, sems, spread):
    send_sems, recv_sems, local_sems = sems
    x, y, c = _my_place()
    me = 4 * x + 2 * y + c
    src = (lambda a, p: s_refs[a]) if spread else (lambda a, p: s_refs[a].at[p])
    local = [pltpu.make_async_copy(src(a, me), r_refs[a].at[me], local_sems.at[a]) for a in range(len(s_refs))]
    remote = []
    for k in range(N_DEV - 1):
        px, py, pc = _flip(k, x, y, c)
        for a in range(len(s_refs)):
            remote.append(pltpu.make_async_remote_copy(
                src_ref=src(a, 4 * px + 2 * py + pc), dst_ref=r_refs[a].at[me],
                send_sem=send_sems.at[7 * a + k], recv_sem=recv_sems.at[7 * a + k],
                device_id=(px, py, pc), device_id_type=MESH_ID))
    return local, remote


def _to_chips_copies(s_refs, r_refs, sems):
    send_sems, recv_sems, local_sems = sems
    x, y, c = _my_place()
    me = 2 * x + y
    local = [pltpu.make_async_copy(s_refs[a].at[me], r_refs[a].at[me], local_sems.at[a]) for a in range(len(s_refs))]
    remote = []
    for k in range(3):
        px = 1 - x if (k + 1) >> 1 & 1 else x
        py = 1 - y if (k + 1) & 1 else y
        for a in range(len(s_refs)):
            remote.append(pltpu.make_async_remote_copy(
                src_ref=s_refs[a].at[2 * px + py], dst_ref=r_refs[a].at[me],
                send_sem=send_sems.at[3 * a + k], recv_sem=recv_sems.at[3 * a + k],
                device_id=(px, py, c), device_id_type=MESH_ID))
    return local, remote


def _start_all(local, remote):
    for cp in local + remote:
        cp.start()


def _wait_all(local, remote):
    for cp in remote:
        cp.wait_recv()
    for cp in remote:
        cp.wait_send()
    for cp in local:
        cp.wait()


def _copy_sems(n, peers):
    return [pltpu.SemaphoreType.DMA((peers * n,)), pltpu.SemaphoreType.DMA((peers * n,)),
            pltpu.SemaphoreType.DMA((n,))]


ANY = pl.BlockSpec(memory_space=pl.ANY)


def _dw_in_chip_partials(ht, dproj):
    m, k = ht.shape
    n = dproj.shape[1]
    tn, tk = 1152, 1024
    nj, nk = n // tn, k // tk
    by_tile = [[] for _ in range(nj)]
    for p, lo, hi, dst in _w_in_pieces():
        while lo < hi:
            j = dst // tn
            cnt = min(hi - lo, (j + 1) * tn - dst)
            by_tile[j].append((p, lo, lo + cnt, dst - j * tn))
            lo, dst = lo + cnt, dst + cnt

    def body(a_ref, b_ref, o_ref, acc_ref, s_ref, got, send_sems, recv_sems):
        j, l = pl.program_id(0), pl.program_id(1)

        @pl.when(l == 0)
        def _():
            acc_ref[...] = jnp.zeros_like(acc_ref)

        acc_ref[...] += _dot(a_ref[...], b_ref[...])

        @pl.when(l == nk - 1)
        def _():
            at = acc_ref[...].T
            for jj in range(nj):
                @pl.when(j == jj)
                def _(jj=jj):
                    for p, lo, hi, d in by_tile[jj]:
                        s_ref[p, lo:hi, :] = at[d:d + hi - lo, :].astype(BF16)

        @pl.when((j == nj - 1) & (l == nk - 1))
        def _():
            x, y, c = _my_place()
            copies = [pltpu.make_async_remote_copy(
                src_ref=s_ref.at[2 * q + 1 - c], dst_ref=got.at[q], send_sem=send_sems.at[q], recv_sem=recv_sems.at[q],
                device_id=(x, y, 1 - c), device_id_type=MESH_ID) for q in range(N_DEV // 2)]
            _start_all([], copies)
            _wait_all([], copies)
            for q in range(N_DEV // 2):
                o_ref[q] = (s_ref[2 * q + c].astype(F32) + got[q].astype(F32)).astype(BF16)

    slot = (W_IN_SHARD, m)
    return pl.pallas_call(
        body,
        grid=(nj, nk),
        in_specs=[pl.BlockSpec((m, tk), lambda j, l: (0, l)), pl.BlockSpec((tk, tn), lambda j, l: (l, j))],
        out_specs=pl.BlockSpec((N_DEV // 2,) + slot, lambda j, l: (0, 0, 0)),
        out_shape=jax.ShapeDtypeStruct((N_DEV // 2,) + slot, BF16),
        scratch_shapes=[pltpu.VMEM((m, tn), F32), pltpu.VMEM((N_DEV,) + slot, BF16),
                        pltpu.VMEM((N_DEV // 2,) + slot, BF16),
                        pltpu.SemaphoreType.DMA((N_DEV // 2,)), pltpu.SemaphoreType.DMA((N_DEV // 2,))],
        compiler_params=_params(("arbitrary", "arbitrary"), 56),
        name="dw_in",
    )(ht, dproj)


PROJ_DT = F32
GP_TN = 256
GP_COLS = 5888
GP_NT = GP_COLS // GP_TN


def _gp_tile_pieces():
    tiles = [[] for _ in range(GP_NT)]
    for p, lo, hi, dst in _w_in_pieces():
        while lo < hi:
            t = dst // GP_TN
            n = min(hi - lo, (t + 1) * GP_TN - dst)
            tiles[t].append((p, lo, lo + n, dst - t * GP_TN))
            lo, dst = lo + n, dst + n
    return tiles


def _gp_tables():
    pieces = _gp_tile_pieces()
    rank_of = {None: 0, 0: 1, 1: 2, 2: 2, 4: 3, 5: 3, 3: 4, 6: 5}
    order = np.zeros((N_DEV, GP_NT), np.int32)
    waits = np.zeros((N_DEV, GP_NT), np.int32)
    for me in range(N_DEV):
        x, y, c = me >> 2 & 1, me >> 1 & 1, me & 1
        chips = [(1 - x, y), (x, 1 - y), (1 - x, 1 - y)]

        def sem_of(p):
            px, py, pc = p >> 2 & 1, p >> 1 & 1, p & 1
            if (px, py) == (x, y):
                return None if pc == c else 0
            j = chips.index((px, py))
            return 1 + j if pc == c else 4 + j

        needs = [sorted({sem_of(p) for p, _, _, _ in tile} - {None}) for tile in pieces]
        ranks = [max([rank_of[k] for k in ks], default=0) for ks in needs]
        seq = sorted(range(GP_NT), key=lambda t: (ranks[t], t))
        seen = set()
        for step, t in enumerate(seq):
            order[me, step] = t
            new = [k for k in needs[t] if k not in seen]
            for k in new:
                waits[me, step] |= 1 << k
            seen.update(new)
        assert seen == set(range(7)), (me, seen)
    return order, waits


def _gather_proj(x, g_pre, w_blk, shards):
    s = x.shape[0]
    tx = 512
    ns = len(shards)
    tile_pieces = _gp_tile_pieces()
    order_np, waits_np = _gp_tables()
    xq, yq, cq = _my_place()
    me_out = 4 * xq + 2 * yq + cq
    order = lax.dynamic_index_in_dim(jnp.asarray(order_np), me_out, 0, keepdims=False)
    waits = lax.dynamic_index_in_dim(jnp.asarray(waits_np), me_out, 0, keepdims=False)

    def body(order_ref, waits_ref, x_hbm, g_ref, wblk_hbm, *rest):
        shard_refs, (proj_ref, wt_ref, ht_hbm), got_refs = rest[:ns], rest[ns:ns + 3], rest[ns + 3:2 * ns + 3]
        recv, h_ref, wtile, xbuf, htbuf = rest[2 * ns + 3:2 * ns + 8]
        send_sems, recv_sems, misc_sems = rest[2 * ns + 8:2 * ns + 11]
        sems = rest[2 * ns + 11:]
        t = pl.program_id(0)
        x_, y_, c = _my_place()
        sibling = (x_, y_, 1 - c)
        chips = [(1 - x_, y_), (x_, 1 - y_), (1 - x_, 1 - y_)]
        idx = lambda px, py, pc: 4 * px + 2 * py + pc
        me = idx(x_, y_, c)

        def copy(k, slot, to, src=None):
            return pltpu.make_async_remote_copy(
                src_ref=recv.at[slot] if src is None else src, dst_ref=recv.at[slot],
                send_sem=send_sems.at[k], recv_sem=recv_sems.at[k], device_id=to, device_id_type=MESH_ID)

        mine = pltpu.make_async_copy(wblk_hbm, recv.at[me], misc_sems.at[0])
        first = [copy(0, me, sibling, src=wblk_hbm)] + [copy(1 + j, me, (*chips[j], c), src=wblk_hbm) for j in range(2)]
        passed = [copy(4 + j, idx(*ch, c), sibling) for j, ch in enumerate(chips)]
        onward = [copy(3, idx(*chips[0], c), (*chips[1], c)), copy(3, idx(*chips[1], c), (*chips[0], c))]
        arrivals = ([copy(0, idx(x_, y_, 1 - c), sibling)] + [copy(1 + j, idx(*ch, c), sibling) for j, ch in enumerate(chips)]
                    + [copy(4 + j, idx(*ch, 1 - c), sibling) for j, ch in enumerate(chips)])

        @pl.when(t == 0)
        def _():
            mine.start()
            for cp in first:
                cp.start()
            _start_all(*_to_all_copies(shard_refs, got_refs, sems, True))

            def load(i):
                return pltpu.make_async_copy(x_hbm.at[pl.ds(i * tx, tx), :], xbuf.at[i & 1], misc_sems.at[1 + (i & 1)])

            def store(i):
                return pltpu.make_async_copy(htbuf.at[i & 1], ht_hbm.at[:, pl.ds(i * tx, tx)], misc_sems.at[3 + (i & 1)])

            load(0).start()
            for i in range(s // tx):
                if i + 1 < s // tx:
                    load(i + 1).start()
                load(i).wait()
                xv = xbuf[i & 1]
                r = lax.rsqrt(jnp.mean(xv * xv, axis=-1, keepdims=True) + EPS)
                h = (xv * r * g_ref[...]).astype(BF16)
                h_ref[i * tx:(i + 1) * tx, :] = h
                if i >= 2:
                    store(i - 2).wait()
                htbuf[i & 1] = h.T
                store(i).start()
            for i in range(max(s // tx - 2, 0), s // tx):
                store(i).wait()
            mine.wait()

        w = waits_ref[t]
        for k in range(7):
            @pl.when((w >> k) & 1 == 1)
            def _(k=k):
                arrivals[k].wait_recv()
                if 1 <= k <= 3:
                    passed[k - 1].start()
                if 1 <= k <= 2:
                    @pl.when(c == k - 1)
                    def _():
                        onward[k - 1].start()

        tile = order_ref[t]
        for tt in range(GP_NT):
            @pl.when(tile == tt)
            def _(tt=tt):
                covered = sorted((d, d + hi - lo) for _, lo, hi, d in tile_pieces[tt])
                at = 0
                for lo_z, hi_z in covered + [(GP_TN, GP_TN)]:
                    if lo_z > at:
                        wtile[at:lo_z, :] = jnp.zeros((lo_z - at, D_MODEL), BF16)
                    at = max(at, hi_z)
                for p, lo, hi, d in tile_pieces[tt]:
                    wtile[d:d + hi - lo, :] = recv[p, lo:hi, :]

        wt = wtile[...]
        wt_ref[...] = wt
        proj_ref[...] = _dotg(h_ref[...], wt, NT).astype(PROJ_DT)

        @pl.when(t == GP_NT - 1)
        def _():
            for cp in first + passed + onward[:1]:
                cp.wait_send()
            _wait_all(*_to_all_copies(shard_refs, got_refs, sems, True))

    grid_spec = pltpu.PrefetchScalarGridSpec(
        num_scalar_prefetch=2,
        grid=(GP_NT,),
        in_specs=[ANY, pl.BlockSpec((1, D_MODEL), lambda t, o, w: (0, 0)), ANY] + [ANY] * ns,
        out_specs=[pl.BlockSpec((s, GP_TN), lambda t, o, w: (0, o[t])),
                   pl.BlockSpec((GP_TN, D_MODEL), lambda t, o, w: (o[t], 0)), ANY] + [ANY] * ns,
        scratch_shapes=[pltpu.VMEM((N_DEV, W_IN_SHARD, D_MODEL), BF16), pltpu.VMEM((s, D_MODEL), BF16),
                        pltpu.VMEM((GP_TN, D_MODEL), BF16), pltpu.VMEM((2, tx, D_MODEL), F32),
                        pltpu.VMEM((2, D_MODEL, tx), BF16),
                        pltpu.SemaphoreType.DMA((7,)), pltpu.SemaphoreType.DMA((7,)), pltpu.SemaphoreType.DMA((5,))]
        + _copy_sems(ns, 7),
    )
    return pl.pallas_call(
        body,
        grid_spec=grid_spec,
        out_shape=[jax.ShapeDtypeStruct((s, GP_COLS), PROJ_DT), jax.ShapeDtypeStruct((GP_COLS, D_MODEL), BF16),
                   jax.ShapeDtypeStruct((D_MODEL, s), BF16)]
        + [jax.ShapeDtypeStruct((N_DEV,) + b.shape, b.dtype) for b in shards],
        compiler_params=_params(("arbitrary",), 56),
        name="gather_proj",
    )(order, waits, x, g_pre, w_blk, *shards)


def _mla_prep(proj, g_q, g_kv, w_uq_p, w_kv_p, rc, rs1, rs2):
    s = proj.shape[0]
    tm = 256
    scale = 1.0 / math.sqrt(QK)

    def body(cq_ref, ckv_ref, kpe_ref, gq_ref, gkv_ref, wuq_ref, wkv_ref, c_ref, s1_ref, s2_ref,
             qr_ref, kr_ref, v_ref, cqt_ref, ckvt_ref):
        cq = cq_ref[...].astype(F32)
        r = lax.rsqrt(jnp.mean(cq * cq, axis=-1, keepdims=True) + EPS)
        cqn = (cq * r * gq_ref[...]).astype(BF16)
        cqt_ref[...] = cqn.T
        q = _dot(cqn, wuq_ref[...])
        ckv = ckv_ref[...].astype(F32)
        r = lax.rsqrt(jnp.mean(ckv * ckv, axis=-1, keepdims=True) + EPS)
        ckvn = (ckv * r * gkv_ref[...]).astype(BF16)
        ckvt_ref[...] = ckvn.T
        kv = _dot(ckvn, wkv_ref[...])
        c, s1, s2 = c_ref[...], s1_ref[...], s2_ref[...]
        lane = lax.broadcasted_iota(jnp.int32, (tm, LANE), 1)
        kpe = _rope(kpe_ref[...].astype(F32), c, s1, s2) + jnp.where((lane == QK) | (lane == QK + 1), 1.0, 0.0)
        vone = jnp.where((lane == VDIM) | (lane == VDIM + 1), 1.0, 0.0)
        for h in range(HEADS):
            sl = slice(LANE * h, LANE * (h + 1))
            qr_ref[:, sl] = (_rope(q[:, sl], c, s1, s2) * scale).astype(BF16)
            kr_ref[:, sl] = (kv[:, sl] + kpe).astype(BF16)
            v_ref[:, sl] = (kv[:, HEADS * LANE + LANE * h:HEADS * LANE + LANE * (h + 1)] + vone).astype(BF16)

    row = lambda w, j: pl.BlockSpec((tm, w), lambda i: (i, j))
    col = lambda w: pl.BlockSpec((w, tm), lambda i: (0, i))
    full = lambda a: pl.BlockSpec(a.shape, lambda i: (0, 0))
    return pl.pallas_call(
        body,
        grid=(s // tm,),
        in_specs=[row(768, 6), row(256, 21), row(128, 44), full(g_q), full(g_kv), full(w_uq_p), full(w_kv_p),
                  row(128, 0), row(128, 0), row(128, 0)],
        out_specs=[row(1024, 0), row(1024, 0), row(1024, 0), col(768), col(256)],
        out_shape=[jax.ShapeDtypeStruct((s, 1024), BF16), jax.ShapeDtypeStruct((s, 1024), BF16),
                   jax.ShapeDtypeStruct((s, 1024), BF16), jax.ShapeDtypeStruct((768, s), BF16),
                   jax.ShapeDtypeStruct((256, s), BF16)],
        compiler_params=_params(("arbitrary",)),
        name="mla_prep",
    )(proj, proj, proj, g_q, g_kv, w_uq_p, w_kv_p, rc, rs1, rs2)


ATT_T = 512
ATT_FWD_HEADS = 4


def _chunk_mask(transposed):
    r = lax.broadcasted_iota(jnp.int32, (ATT_T, ATT_T), 0) >> ATT_CHUNK_SHIFT
    c = lax.broadcasted_iota(jnp.int32, (ATT_T, ATT_T), 1) >> ATT_CHUNK_SHIFT
    return (r <= c) if transposed else (c <= r)


def _attn_fwd(qr, kr, vp, shards):
    s = qr.shape[0]
    t = ATT_T
    g = ATT_FWD_HEADS
    ns = len(shards)

    def body(q_ref, k_ref, v_ref, *rest):
        shard_refs, (o_ref, qa_ref), got_refs = rest[:ns], rest[ns:ns + 2], rest[ns + 2:2 * ns + 2]
        sc_ref, sems = rest[2 * ns + 2], rest[2 * ns + 3:]
        qi = pl.program_id(1)

        @pl.when((pl.program_id(0) == 0) & (qi == 0))
        def _():
            _start_all(*_to_all_copies(shard_refs, got_refs, sems, True))
        lane = lax.broadcasted_iota(jnp.int32, (t, LANE), 1)
        sls = [slice(LANE * a, LANE * (a + 1)) for a in range(g)]
        qs = [q_ref[:, sl] for sl in sls]

        def scores(j):
            rows = pl.ds(pl.multiple_of(j * t, t), t)
            for a in range(g):
                sc_ref[j & 1, a] = _dotg(qs[a], k_ref[rows, sls[a]], NT)

        def step(j, carry, masked):
            rows = pl.ds(pl.multiple_of(j * t, t), t)
            out = []
            for a in range(g):
                m, acc = carry[a]
                sc = sc_ref[j & 1, a]
                if masked:
                    sc = jnp.where(_chunk_mask(False), sc, -1e30)
                m_new = jnp.maximum(m, jnp.max(sc, axis=-1, keepdims=True))
                p = jnp.exp(sc - m_new).astype(BF16)
                acc = jnp.exp(m - m_new) * acc + _dot(p, v_ref[rows, sls[a]])
                out.append((m_new, acc))
            return tuple(out)

        def loop(j, carry):
            carry = step(j, carry, False)
            scores(j + 1)
            return carry

        init = tuple((jnp.full((t, 1), -1e30, F32), jnp.zeros((t, LANE), F32)) for _ in range(g))
        scores(0)
        carry = lax.fori_loop(0, qi, loop, init)
        carry = step(qi, carry, True)
        outs = []
        for a in range(g):
            m, acc = carry[a]
            l = acc[:, VDIM:VDIM + 1]
            outs.append(acc / l)
            hi, lo_part = _hi_lo(-(m + jnp.log(l)))
            qa = jnp.where(lane == QK, hi, jnp.where(lane == QK + 1, lo_part, qs[a].astype(F32)))
            qa_ref[:, sls[a]] = qa.astype(BF16)
        for p in range(g // 2):
            o_ref[:, LANE * p:LANE * (p + 1)] = jnp.where(lane < VDIM, outs[2 * p], pltpu.roll(outs[2 * p + 1], VDIM, 1))

        @pl.when((pl.program_id(0) == HEADS // g - 1) & (qi == s // t - 1))
        def _():
            _wait_all(*_to_all_copies(shard_refs, got_refs, sems, True))

    return pl.pallas_call(
        body,
        grid=(HEADS // g, s // t),
        in_specs=[
            pl.BlockSpec((t, g * LANE), lambda h, i: (i, h)),
            pl.BlockSpec((s, g * LANE), lambda h, i: (0, h)),
            pl.BlockSpec((s, g * LANE), lambda h, i: (0, h)),
        ] + [ANY] * ns,
        out_specs=[
            pl.BlockSpec((t, g * VDIM), lambda h, i: (i, h)),
            pl.BlockSpec((t, g * LANE), lambda h, i: (i, h)),
        ] + [ANY] * ns,
        out_shape=[jax.ShapeDtypeStruct((s, 512), F32), jax.ShapeDtypeStruct((s, 1024), BF16)]
        + [jax.ShapeDtypeStruct((N_DEV,) + b.shape, b.dtype) for b in shards],
        scratch_shapes=[pltpu.VMEM((2, g, t, t), F32)] + _copy_sems(ns, 7),
        compiler_params=_params(("arbitrary", "arbitrary")),
        name="attn_fwd",
    )(qr, kr, vp, *shards)


def _attn_bwd(qa, kr, vp, dop, sends):
    s = qa.shape[0]
    t = ATT_T
    nq = s // t
    ns = len(sends)

    def body(q_ref, k_ref, v_ref, do_ref, *rest):
        send_refs, (dq_out, dk_out, dv_out) = rest[:ns], rest[ns:ns + 3]
        recv_refs = rest[ns + 3:2 * ns + 3]
        (dq_ref, dk_ref, dv_ref), sems = rest[2 * ns + 3:2 * ns + 6], rest[2 * ns + 6:]
        j = pl.program_id(1)
        sls = [slice(LANE * a, LANE * (a + 1)) for a in range(2)]

        @pl.when((pl.program_id(0) == 0) & (j == 0))
        def _():
            _start_all(*_to_all_copies(send_refs, recv_refs, sems, False))

        @pl.when(j == 0)
        def _():
            dq_ref[...] = jnp.zeros_like(dq_ref)

        dk_ref[...] = jnp.zeros_like(dk_ref)
        dv_ref[...] = jnp.zeros_like(dv_ref)
        ks = [k_ref[:, sl] for sl in sls]
        vs = [v_ref[:, sl] for sl in sls]

        def part(i, k_lo, k_n, q_lo, q_n, masked):
            rows = pl.ds(pl.multiple_of(i * t + q_lo, 256), q_n)
            keys = slice(k_lo, k_lo + k_n)
            for a in range(2):
                q = q_ref[rows, sls[a]]
                do = do_ref[rows, sls[a]]
                sc = _dotg(ks[a][keys], q, NT)
                if masked:
                    kc = lax.broadcasted_iota(jnp.int32, (k_n, q_n), 0) >> ATT_CHUNK_SHIFT
                    qc = lax.broadcasted_iota(jnp.int32, (k_n, q_n), 1) >> ATT_CHUNK_SHIFT
                    sc = jnp.where(kc <= qc, sc, -1e30)
                p = jnp.exp(sc)
                ds = (p * _dotg(vs[a][keys], do, NT)).astype(BF16)
                dv_ref[keys, sls[a]] += _dot(p.astype(BF16), do)
                dk_ref[keys, sls[a]] += _dot(ds, q)
                dq_ref[rows, sls[a]] += _dotg(ds, ks[a][keys], TN)

        half = t // 2
        part(j, 0, half, 0, t, True)
        part(j, half, half, half, half, True)

        def loop(i, c):
            part(i, 0, t, 0, t, False)
            return c

        lax.fori_loop(j + 1, nq, loop, 0)
        dk_out[...] = dk_ref[...].astype(BF16)
        dv_out[...] = dv_ref[...].astype(BF16)

        @pl.when(j == nq - 1)
        def _():
            dq_out[...] = dq_ref[...].astype(BF16)

        @pl.when((pl.program_id(0) == HEADS // 2 - 1) & (j == nq - 1))
        def _():
            _wait_all(*_to_all_copies(send_refs, recv_refs, sems, False))

    blk = pl.BlockSpec((t, 2 * LANE), lambda h, j: (j, h))
    whole = pl.BlockSpec((s, 2 * LANE), lambda h, j: (0, h))
    out = jax.ShapeDtypeStruct((s, 1024), BF16)
    return pl.pallas_call(
        body,
        grid=(HEADS // 2, nq),
        in_specs=[whole, blk, blk, whole] + [ANY] * ns,
        out_specs=[whole, blk, blk] + [ANY] * ns,
        out_shape=[out, out, out] + [jax.ShapeDtypeStruct(a.shape, a.dtype) for a in sends],
        scratch_shapes=[pltpu.VMEM((s, 2 * LANE), F32), pltpu.VMEM((t, 2 * LANE), F32),
                        pltpu.VMEM((t, 2 * LANE), F32)] + _copy_sems(ns, 7),
        compiler_params=_params(("arbitrary", "arbitrary")),
        name="attn_bwd",
    )(qa, kr, vp, dop, *sends)


HG_T = 256
HG_NC = HG_T // HG_BLOCK
HG_G = 4
GW = 64 * HG_G


def _hg_consts():
    r = jnp.arange(HG_T)[:, None]
    c = jnp.arange(HG_T)[None, :]
    same = (r // HG_BLOCK) == (c // HG_BLOCK)
    mcum = (same & (c <= r)).astype(BF16)
    mrev = (same & (c >= r)).astype(BF16)
    msum = same.astype(BF16)
    a = jnp.arange(GW) // 64
    bd = (a[:, None] == a[None, :]).astype(F32)
    return mcum, mrev, msum, bd


def _stack_heads(xg, head):
    return jnp.concatenate([jnp.where(head == h, xg, 0.0) for h in range(HG_G)], axis=0)


def _unstack_heads(r, head, t):
    out = r[(HG_G - 1) * t:]
    for h in range(HG_G - 2, -1, -1):
        out = jnp.where(head == h, r[h * t:(h + 1) * t], out)
    return out


def _compact_state(st):
    out = st[:64]
    for h in range(1, HG_G):
        out = out + st[64 * h:64 * (h + 1)]
    return out


def _expand_state(cs, head64):
    return jnp.concatenate([jnp.where(head64 == h, cs, 0.0) for h in range(HG_G)], axis=0)


def _hg_pre(hq, hf, lbl, mcum, msum):
    lb = _sigmoid(lbl[0:1, :] - lbl[1:2, :])
    sig = _sigmoid(hf)
    f = lb + (1.0 - lb) * sig
    lf = jnp.log(f)
    b = _sel_left(mcum, lf)
    big_l = _sel_left(msum, lf)
    k = 1.0 - f
    qd = hq * jnp.exp(b)
    ki = k * jnp.exp(-b)
    ke = k * jnp.exp(big_l - b)
    return lb, sig, f, b, big_l, qd, ki, ke


def _hgrn_fwd(proj, lbl):
    s = proj.shape[0]
    t = HG_T
    mcum, _, msum, bd = _hg_consts()

    def body(hq_ref, hf_ref, hi_ref, lbl_ref, mcum_ref, msum_ref, bd_ref, o_ref, sp_ref, st_ref):
        @pl.when(pl.program_id(0) == 0)
        def _():
            st_ref[...] = jnp.zeros_like(st_ref)

        mc = mcum_ref[...]
        _, _, _, _, big_l, qd, ki, ke = _hg_pre(hq_ref[...].astype(F32), hf_ref[...].astype(F32), lbl_ref[...], mc,
                                                msum_ref[...])
        el = jnp.exp(big_l)
        hi = hi_ref[...]
        head = lax.broadcasted_iota(jnp.int32, (t, GW), 1) >> 6
        mask = jnp.concatenate([mc] * HG_G, axis=0) > 0.5
        for p in range(HEADS // HG_G):
            sl = slice(GW * p, GW * (p + 1))
            vp = hi[:, sl].astype(BF16)
            qs = _stack_heads(qd[:, sl], head).astype(BF16)
            a = jnp.where(mask, _dotg(qs, ki[:, sl].astype(BF16), NT), 0.0)
            o_intra = _unstack_heads(_dot(a.astype(BF16), vp), head, t)
            qb = qd[:, sl].astype(BF16)
            kb = ke[:, sl].astype(BF16)
            st = st_ref[p]
            for c in range(HG_NC):
                rows = slice(HG_BLOCK * c, HG_BLOCK * (c + 1))
                sp_ref[c, :, sl] = _compact_state(st)
                o_ref[rows, sl] = o_intra[rows] + _dotg(qb[rows], st.astype(BF16), NT)
                u = _dotg(vp[rows], kb[rows], TN) * bd_ref[...]
                st = st * el[HG_BLOCK * c:HG_BLOCK * c + 1, sl] + u
            st_ref[p] = st

    row = lambda j: pl.BlockSpec((t, HG_WIDTH), lambda i: (i, j))
    full = lambda a: pl.BlockSpec(a.shape, lambda i: (0, 0))
    return pl.pallas_call(
        body,
        grid=(s // t,),
        in_specs=[row(6), row(7), row(8), full(lbl), full(mcum), full(msum), full(bd)],
        out_specs=[row(0), pl.BlockSpec((HG_NC, 64, HG_WIDTH), lambda i: (i, 0, 0))],
        out_shape=[jax.ShapeDtypeStruct((s, HG_WIDTH), F32),
                   jax.ShapeDtypeStruct((s // HG_BLOCK, 64, HG_WIDTH), F32)],
        scratch_shapes=[pltpu.VMEM((HEADS // HG_G, GW, GW), F32)],
        compiler_params=_params(("arbitrary",)),
        name="hgrn_fwd",
    )(proj, proj, proj, lbl, mcum, msum, bd)


def _hgrn_bwd(proj, lbl, do, sprev, dproj):
    s = proj.shape[0]
    t = HG_T
    nt = s // t
    mcum, mrev, msum, bd = _hg_consts()

    def body(hq_ref, hf_ref, hi_ref, lbl_ref, do_ref, sp_ref, mcum_ref, mrev_ref, msum_ref, bd_ref,
             dproj_in, dh_ref, dlbl_ref, g_ref):
        del dproj_in

        @pl.when(pl.program_id(0) == 0)
        def _():
            g_ref[...] = jnp.zeros_like(g_ref)
            dlbl_ref[...] = jnp.zeros_like(dlbl_ref)

        mc = mcum_ref[...]
        lb, sig, f, b, big_l, qd, ki, ke = _hg_pre(hq_ref[...].astype(F32), hf_ref[...].astype(F32), lbl_ref[...], mc,
                                                   msum_ref[...])
        el = jnp.exp(big_l)
        hi = hi_ref[...]
        dov = do_ref[...]
        head = lax.broadcasted_iota(jnp.int32, (t, GW), 1) >> 6
        head64 = lax.broadcasted_iota(jnp.int32, (64, GW), 1) >> 6
        mask = jnp.concatenate([mc] * HG_G, axis=0) > 0.5
        dqd_parts, dke_parts, dv_parts, del_parts, dki_parts = [], [], [], [], []
        for p in range(HEADS // HG_G):
            sl = slice(GW * p, GW * (p + 1))
            vp = hi[:, sl].astype(BF16)
            qs = _stack_heads(qd[:, sl], head).astype(BF16)
            kip = ki[:, sl].astype(BF16)
            dos = _stack_heads(dov[:, sl], head).astype(BF16)
            a = jnp.where(mask, _dotg(qs, kip, NT), 0.0).astype(BF16)
            da = jnp.where(mask, _dotg(dos, vp, NT), 0.0).astype(BF16)
            r = _dot(da, kip)
            dki_parts.append(_dotg(da, qs, TN))
            qb = qd[:, sl].astype(BF16)
            kb = ke[:, sl].astype(BF16)
            dob = dov[:, sl].astype(BF16)
            g = g_ref[p]
            dqd_c, dv_c, dke_c, del_c = [], [], [], []
            for c in range(HG_NC - 1, -1, -1):
                rows = slice(HG_BLOCK * c, HG_BLOCK * (c + 1))
                gb = g.astype(BF16)
                st = _expand_state(sp_ref[c, :, sl], head64)
                dqd_c.append(_dot(dob[rows], st.astype(BF16)))
                dv_c.append(_dotg(kb[rows], gb, NT))
                dke_c.append(_dot(vp[rows], gb))
                del_c.append(jnp.broadcast_to(jnp.sum(g * st, axis=0, keepdims=True), (HG_BLOCK, GW)))
                g = g * el[HG_BLOCK * c:HG_BLOCK * c + 1, sl] + _dotg(dob[rows], qb[rows], TN) * bd_ref[...]
            g_ref[p] = g
            up = lambda parts: jnp.concatenate(parts[::-1], axis=0)
            dqd_parts.append(_unstack_heads(r, head, t) + up(dqd_c))
            dv_parts.append(_dotg(a, dos, TN) + up(dv_c))
            dke_parts.append(up(dke_c))
            del_parts.append(up(del_c))
        wide = lambda parts: jnp.concatenate(parts, axis=1)
        dqd, dke, dki, dvv, del_rows = wide(dqd_parts), wide(dke_parts), wide(dki_parts), wide(dv_parts), wide(del_parts)
        dh_ref[:, :HG_WIDTH] = (dqd * jnp.exp(b)).astype(BF16)
        dh_ref[:, 2 * HG_WIDTH:] = dvv.astype(BF16)
        dke_ke = dke * ke
        db = dqd * qd - dki * ki - dke_ke
        dl_rows = _sel_left(msum_ref[...], dke_ke) + del_rows * el
        is_last = (lax.broadcasted_iota(jnp.int32, (t, HG_WIDTH), 0) & (HG_BLOCK - 1)) == HG_BLOCK - 1
        db = db + jnp.where(is_last, dl_rows, 0.0)
        dlf = _sel_left(mrev_ref[...], db)
        dk = dki * jnp.exp(-b) + dke * jnp.exp(big_l - b)
        df = dlf / f - dk
        dh_ref[:, HG_WIDTH:2 * HG_WIDTH] = (df * (1.0 - lb) * sig * (1.0 - sig)).astype(BF16)
        dlb = jnp.sum(df * (1.0 - sig), axis=0, keepdims=True) * lb * (1.0 - lb)
        dlbl_ref[0:1, :] += dlb
        dlbl_ref[1:2, :] -= dlb

    rrow = lambda j: pl.BlockSpec((t, HG_WIDTH), lambda i: (nt - 1 - i, j))
    full = lambda a: pl.BlockSpec(a.shape, lambda i: (0, 0))
    return pl.pallas_call(
        body,
        grid=(nt,),
        in_specs=[rrow(6), rrow(7), rrow(8), full(lbl), rrow(0),
                  pl.BlockSpec((HG_NC, 64, HG_WIDTH), lambda i: (nt - 1 - i, 0, 0)),
                  full(mcum), full(mrev), full(msum), full(bd), pl.BlockSpec(memory_space=pl.ANY)],
        out_specs=[pl.BlockSpec((t, 3 * HG_WIDTH), lambda i: (nt - 1 - i, 2)),
                   pl.BlockSpec((2, HG_WIDTH), lambda i: (0, 0))],
        out_shape=[jax.ShapeDtypeStruct(dproj.shape, BF16), jax.ShapeDtypeStruct((2, HG_WIDTH), F32)],
        input_output_aliases={10: 0},
        scratch_shapes=[pltpu.VMEM((HEADS // HG_G, GW, GW), F32)],
        compiler_params=_params(("arbitrary",)),
        name="hgrn_bwd",
    )(proj, proj, proj, lbl, do, sprev, mcum, mrev, msum, bd, dproj)


def _tail(x, tgt, proj, attn, o, w_a, w_b, w_out, w_at, w_bt, w_outt, b_gate, g_post, gh):
    s = x.shape[0]
    tm = 256
    ones64 = (jnp.arange(HG_WIDTH)[:, None] // 64 == jnp.arange(HG_WIDTH)[None, :] // 64).astype(BF16)
    weights = (w_a, w_b, w_out, w_at, w_bt, w_outt)

    def body(x_ref, t_ref, ml_ref, ga_ref, gb_ref, at_ref, o_ref, *rest):
        w_hbm, (bg_ref, gp_ref, gh_ref, ones_ref) = rest[:6], rest[6:10]
        (dout_ref, dpj_ref, dop_ref, do_ref, mt_ref, dy_ref, yat_ref, dya_ref, ybt_ref, dyb_ref,
         loss_ref, dgp_ref, dbg_ref, dgh_ref) = rest[10:24]
        (wa_ref, wb_ref, wo_ref, wat_ref, wbt_ref, wot_ref), w_sem = rest[24:30], rest[30]

        @pl.when(pl.program_id(0) == 0)
        def _():
            loads = [pltpu.make_async_copy(src, dst, w_sem.at[k])
                     for k, (src, dst) in enumerate(zip(w_hbm, rest[24:30]))]
            _start_all(loads, [])
            loss_ref[...] = jnp.zeros_like(loss_ref)
            dgp_ref[...] = jnp.zeros_like(dgp_ref)
            dbg_ref[...] = jnp.zeros_like(dbg_ref)
            dgh_ref[...] = jnp.zeros_like(dgh_ref)
            _wait_all(loads, [])

        ones = ones_ref[...]
        gate_a = ga_ref[...].astype(F32)
        sa = _sigmoid(gate_a)
        silu_a = gate_a * sa
        attn_v = at_ref[...]
        ya_in = attn_v * silu_a
        ov = o_ref[...]
        ro = lax.rsqrt(_sel_right(ov * ov, ones) * (1.0 / 64.0) + EPS)
        ohat = ov * ro
        ghv = gh_ref[...]
        on = ohat * ghv
        gate_b = gb_ref[...].astype(F32)
        sb = _sigmoid(gate_b)
        silu_b = gate_b * sb
        yb_in = on * silu_b
        ya_bf = ya_in.astype(BF16)
        yb_bf = yb_in.astype(BF16)
        yat_ref[...] = ya_bf.T
        ybt_ref[...] = yb_bf.T
        y_a = _dot(ya_bf, wa_ref[...])
        y_b = _dot(yb_bf, wb_ref[...])
        gts = _sigmoid(ml_ref[...].astype(F32) + bg_ref[...])
        g_a = gts[:, :D_MODEL]
        g_b = gts[:, D_MODEL:]
        m_bf = (g_a * y_a + g_b * y_b).astype(BF16)
        mt_ref[...] = m_bf.T
        y = _dot(m_bf, wo_ref[...])
        r1 = lax.rsqrt(jnp.mean(y * y, axis=-1, keepdims=True) + EPS)
        yn = y * r1
        gp = gp_ref[...]
        e = x_ref[...] + yn * gp - t_ref[...]
        loss_ref[...] += jnp.sum(e * e, axis=0, keepdims=True)
        dout = e * (1.0 / D_MODEL)
        dout_ref[...] = dout
        dgp_ref[...] += jnp.sum(dout * yn, axis=0, keepdims=True)
        dyn = dout * gp
        dy = r1 * (dyn - yn * jnp.mean(dyn * yn, axis=-1, keepdims=True))
        dy_bf = dy.astype(BF16)
        dy_ref[...] = dy_bf
        dm = _dot(dy_bf, wot_ref[...])
        dml_a = dm * y_a * g_a * (1.0 - g_a)
        dml_b = dm * y_b * g_b * (1.0 - g_b)
        dpj_ref[:, :D_MODEL] = dml_a.astype(BF16)
        dpj_ref[:, D_MODEL:2 * D_MODEL] = dml_b.astype(BF16)
        dbg_ref[:, :D_MODEL] += jnp.sum(dml_a, axis=0, keepdims=True)
        dbg_ref[:, D_MODEL:] += jnp.sum(dml_b, axis=0, keepdims=True)
        dya_bf = (dm * g_a).astype(BF16)
        dyb_bf = (dm * g_b).astype(BF16)
        dya_ref[...] = dya_bf
        dyb_ref[...] = dyb_bf
        dya_in = _dot(dya_bf, wat_ref[...])
        dyb_in = _dot(dyb_bf, wbt_ref[...])
        dattn = dya_in * silu_a
        delta = _sel_right(dattn * attn_v, ones)
        lane = lax.broadcasted_iota(jnp.int32, (tm, LANE), 1)
        for p in range(HEADS // 2):
            sl = slice(LANE * p, LANE * (p + 1))
            xs = (dattn[:, sl], pltpu.roll(dattn[:, sl], VDIM, 1))
            nds = (-pltpu.roll(delta[:, sl], VDIM, 1), -delta[:, sl])
            for a in range(2):
                hi, lo_part = _hi_lo(nds[a])
                blk = jnp.where(lane < VDIM, xs[a], jnp.where(lane == VDIM, hi, jnp.where(lane == VDIM + 1, lo_part, 0.0)))
                dop_ref[:, LANE * (2 * p + a):LANE * (2 * p + a + 1)] = blk.astype(BF16)
        dpj_ref[:, 2 * D_MODEL:2 * D_MODEL + HG_WIDTH] = (
            dya_in * attn_v * (sa * (1.0 + gate_a * (1.0 - sa)))).astype(BF16)
        don = dyb_in * silu_b
        dpj_ref[:, 2 * D_MODEL + HG_WIDTH:] = (dyb_in * on * (sb * (1.0 + gate_b * (1.0 - sb)))).astype(BF16)
        dgh_ref[...] += jnp.sum(don * ohat, axis=0, keepdims=True)
        dohat = don * ghv
        do_ref[...] = ro * (dohat - ohat * (_sel_right(dohat * ohat, ones) * (1.0 / 64.0)))

    row = lambda w, j: pl.BlockSpec((tm, w), lambda i: (i, j))
    col = lambda w: pl.BlockSpec((w, tm), lambda i: (0, i))
    full = lambda a: pl.BlockSpec(a.shape, lambda i: (0, 0))
    acc = lambda w: pl.BlockSpec((1, w), lambda i: (0, 0))
    sds = lambda w, dt: jax.ShapeDtypeStruct((s, w), dt)
    sdt = lambda w: jax.ShapeDtypeStruct((w, s), BF16)
    return pl.pallas_call(
        body,
        grid=(s // tm,),
        in_specs=[row(1024, 0), row(1024, 0), row(2048, 0), row(512, 4), row(512, 5), row(512, 0), row(512, 0)]
        + [ANY] * 6 + [full(b_gate), full(g_post), full(gh), full(ones64)],
        out_specs=[row(1024, 0), row(3072, 0), row(1024, 0), row(512, 0),
                   col(1024), row(1024, 0), col(512), row(1024, 0), col(512), row(1024, 0),
                   acc(1024), acc(1024), acc(2048), acc(512)],
        out_shape=[sds(1024, F32), sds(D_IN_PAD, BF16), sds(1024, BF16), sds(512, F32),
                   sdt(1024), sds(1024, BF16), sdt(512), sds(1024, BF16), sdt(512), sds(1024, BF16),
                   jax.ShapeDtypeStruct((1, 1024), F32), jax.ShapeDtypeStruct((1, 1024), F32),
                   jax.ShapeDtypeStruct((1, 2048), F32), jax.ShapeDtypeStruct((1, 512), F32)],
        scratch_shapes=[pltpu.VMEM(a.shape, BF16) for a in weights] + [pltpu.SemaphoreType.DMA((6,))],
        compiler_params=_params(("arbitrary",), 56),
        name="tail",
    )(x, tgt, proj, proj, proj, attn, o, *weights, b_gate, g_post, gh, ones64)


def _mla_bwd(proj, dqr, dkr, dv, g_q, g_kv, w_uq_pt, w_kv_pt, rc, rs1, rs2, dproj):
    s = proj.shape[0]
    tm = 256
    scale = 1.0 / math.sqrt(QK)

    def body(cq_ref, ckv_ref, dqr_ref, dkr_ref, dv_ref, gq_ref, gkv_ref, wuqt_ref, wkvt_ref, c_ref, s1_ref, s2_ref,
             dproj_in, dqf_ref, dkvf_ref, dc_ref, dgq_ref, dgkv_ref):
        del dproj_in

        @pl.when(pl.program_id(0) == 0)
        def _():
            dgq_ref[...] = jnp.zeros_like(dgq_ref)
            dgkv_ref[...] = jnp.zeros_like(dgkv_ref)

        c, s1, s2 = c_ref[...], s1_ref[...], s2_ref[...]
        lane = lax.broadcasted_iota(jnp.int32, (tm, LANE), 1)
        ksum = jnp.zeros((tm, LANE), F32)
        for h in range(HEADS):
            sl = slice(LANE * h, LANE * (h + 1))
            dqf_ref[:, sl] = (_unrope(dqr_ref[:, sl], c, s1, s2) * scale).astype(BF16)
            dkh = dkr_ref[:, sl]
            ksum = ksum + dkh
            dkvf_ref[:, sl] = jnp.where(lane < NOPE, dkh, 0.0).astype(BF16)
            dkvf_ref[:, HEADS * LANE + LANE * h:HEADS * LANE + LANE * (h + 1)] = jnp.where(
                lane < VDIM, dv_ref[:, sl], 0.0).astype(BF16)
        dkpe = _unrope(ksum, c, s1, s2)
        dc_ref[:, Q_LORA + KV_LORA:] = jnp.where((lane >= NOPE) & (lane < QK), dkpe, 0.0).astype(BF16)
        dcqn = _dot(dqf_ref[...], wuqt_ref[...])
        dckvn = _dot(dkvf_ref[...], wkvt_ref[...])
        for x_ref, g_ref, dn, cols, dg_ref in ((cq_ref, gq_ref, dcqn, slice(0, Q_LORA), dgq_ref),
                                               (ckv_ref, gkv_ref, dckvn, slice(Q_LORA, Q_LORA + KV_LORA), dgkv_ref)):
            xv = x_ref[...].astype(F32)
            r = lax.rsqrt(jnp.mean(xv * xv, axis=-1, keepdims=True) + EPS)
            xh = xv * r
            dg_ref[...] += jnp.sum(dn * xh, axis=0, keepdims=True)
            dh = dn * g_ref[...]
            dc_ref[:, cols] = (r * (dh - xh * jnp.mean(dh * xh, axis=-1, keepdims=True))).astype(BF16)

    row = lambda w, j: pl.BlockSpec((tm, w), lambda i: (i, j))
    full = lambda a: pl.BlockSpec(a.shape, lambda i: (0, 0))
    acc = lambda w: pl.BlockSpec((1, w), lambda i: (0, 0))
    sds = lambda w, dt: jax.ShapeDtypeStruct((s, w), dt)
    return pl.pallas_call(
        body,
        grid=(s // tm,),
        in_specs=[row(768, 6), row(256, 21), row(1024, 0), row(1024, 0), row(1024, 0), full(g_q), full(g_kv),
                  full(w_uq_pt), full(w_kv_pt), row(128, 0), row(128, 0), row(128, 0),
                  pl.BlockSpec(memory_space=pl.ANY)],
        out_specs=[row(1024, 0), row(2048, 0), row(1152, 4), acc(768), acc(256)],
        out_shape=[sds(1024, BF16), sds(2048, BF16), jax.ShapeDtypeStruct(dproj.shape, BF16),
                   jax.ShapeDtypeStruct((1, 768), F32), jax.ShapeDtypeStruct((1, 256), F32)],
        input_output_aliases={12: 2},
        compiler_params=_params(("arbitrary",)),
        name="mla_bwd",
    )(proj, proj, dqr, dkr, dv, g_q, g_kv, w_uq_pt, w_kv_pt, rc, rs1, rs2, dproj)


def _pick(n, options):
    for o in options:
        if n % o == 0:
            return o
    raise ValueError(n)


def _matmul(a, b, name):
    m, k = a.shape
    n = b.shape[1]
    tm = _pick(m, (1024, 768, 512, 256))
    tn = _pick(n, (1152, 1024, 768, 512))
    tk = _pick(k, (1024, 512))
    nk = k // tk

    def body(a_ref, b_ref, o_ref):
        @pl.when(pl.program_id(2) == 0)
        def _():
            o_ref[...] = jnp.zeros_like(o_ref)

        o_ref[...] += _dot(a_ref[...], b_ref[...])

    return pl.pallas_call(
        body,
        grid=(m // tm, n // tn, nk),
        in_specs=[pl.BlockSpec((tm, tk), lambda i, j, l: (i, l)), pl.BlockSpec((tk, tn), lambda i, j, l: (l, j))],
        out_specs=pl.BlockSpec((tm, tn), lambda i, j, l: (i, j)),
        out_shape=jax.ShapeDtypeStruct((m, n), F32),
        compiler_params=_params(("arbitrary", "arbitrary", "arbitrary")),
        name=name,
    )(a, b)


def _dh_dx(dproj, w_in_pt, x, dout, g_pre, sends4, sends8):
    s, k = dproj.shape
    tm = 256
    n4, n8 = len(sends4), len(sends8)
    sends = list(sends4) + list(sends8)
    ns, ni = n4 + n8, s // tm

    def body(dp_ref, w_ref, x_ref, dout_ref, g_ref, *rest):
        send_refs, (dx_ref, dg_ref) = rest[:ns], rest[ns:ns + 2]
        recv_refs, sems = rest[ns + 2:2 * ns + 2], rest[2 * ns + 2:]

        def exchanges():
            local, remote = _to_chips_copies(send_refs[:n4], recv_refs[:n4], sems[:3])
            local8, remote8 = _to_all_copies(send_refs[n4:], recv_refs[n4:], sems[3:], False)
            return local + local8, remote + remote8

        @pl.when(pl.program_id(0) == 0)
        def _():
            _start_all(*exchanges())
            dg_ref[...] = jnp.zeros_like(dg_ref)

        dh = _dot(dp_ref[...], w_ref[...])
        xv = x_ref[...]
        r = lax.rsqrt(jnp.mean(xv * xv, axis=-1, keepdims=True) + EPS)
        xh = xv * r
        dg_ref[...] += jnp.sum(dh * xh, axis=0, keepdims=True)
        dxh = dh * g_ref[...]
        dx_ref[...] = dout_ref[...] + r * (dxh - xh * jnp.mean(dxh * xh, axis=-1, keepdims=True))

        @pl.when(pl.program_id(0) == ni - 1)
        def _():
            _wait_all(*exchanges())

    row = lambda w: pl.BlockSpec((tm, w), lambda i: (i, 0))
    return pl.pallas_call(
        body,
        grid=(ni,),
        in_specs=[row(k), pl.BlockSpec((k, D_MODEL), lambda i: (0, 0)), row(D_MODEL), row(D_MODEL),
                  pl.BlockSpec((1, D_MODEL), lambda i: (0, 0))] + [ANY] * ns,
        out_specs=[row(D_MODEL), pl.BlockSpec((1, D_MODEL), lambda i: (0, 0))] + [ANY] * ns,
        out_shape=[jax.ShapeDtypeStruct((s, D_MODEL), F32), jax.ShapeDtypeStruct((1, D_MODEL), F32)]
        + [jax.ShapeDtypeStruct(a.shape, a.dtype) for a in sends],
        scratch_shapes=_copy_sems(n4, 3) + _copy_sems(n8, 7),
        compiler_params=_params(("arbitrary",)),
        name="dh_dx",
    )(dproj, w_in_pt, x, dout, g_pre, *sends)


def _rope_tables(s):
    inv = (np.float32(ROPE_THETA) ** (-np.arange(0, ROPE, 2, dtype=np.float32) / np.float32(ROPE))).astype(np.float32)
    ang = (np.arange(s, dtype=np.float32)[:, None] * inv[None, :]).astype(np.float32)
    cos, sin = jnp.asarray(np.cos(ang.astype(np.float64)), F32), jnp.asarray(np.sin(ang.astype(np.float64)), F32)
    z = lambda w: jnp.zeros((s, w), F32)
    rc = jnp.concatenate([jnp.ones((s, NOPE), F32), cos, cos, z(32)], axis=1)
    rs1 = jnp.concatenate([z(NOPE), -sin, z(16), z(32)], axis=1)
    rs2 = jnp.concatenate([z(NOPE), z(16), sin, z(32)], axis=1)
    return rc, rs1, rs2


def _step(x, tgt, w_blk, shards, g_pre, b_gate, g_q, g_kv, lbl, g_hgrn, g_post):
    s = x.shape[0]
    rc, rs1, rs2 = _rope_tables(s)
    gh = jnp.tile(g_hgrn, (1, HEADS))

    proj, w_in_pt, ht, *got = _gather_proj(x, g_pre, w_blk, shards[:2])
    w_uq, w_ukv = (_from_slots(n, g) for n, g in zip(MATS[:2], got))
    w_uq_p = jnp.pad(w_uq.reshape(Q_LORA, HEADS, QK), ((0, 0), (0, 0), (0, LANE - QK))).reshape(Q_LORA, HEADS * LANE)
    kv3 = w_ukv.reshape(KV_LORA, HEADS, NOPE + VDIM)
    pad64 = lambda t: jnp.pad(t, ((0, 0), (0, 0), (0, LANE - 64))).reshape(KV_LORA, HEADS * LANE)
    w_kv_p = jnp.concatenate([pad64(kv3[:, :, :NOPE]), pad64(kv3[:, :, NOPE:])], axis=1)

    qr, kr, v, cqt, ckvt = _mla_prep(proj, g_q, g_kv, w_uq_p, w_kv_p, rc, rs1, rs2)
    attn, qa, *got = _attn_fwd(qr, kr, v, shards[2:])
    w_a, w_b, w_out = (_from_slots(n, g) for n, g in zip(MATS[2:], got))
    o, sprev = _hgrn_fwd(proj, lbl)
    (dout, dproj, dop, do, mt, dy_bf, yat, dya_bf, ybt, dyb_bf,
     loss_vec, dg_post, db_gate, dgh) = _tail(x, tgt, proj, attn, o, w_a, w_b, w_out, w_a.T, w_b.T, w_out.T,
                                               b_gate, g_post, gh)
    early = [_to_slots(n, _matmul(a, b, "d" + n)).astype(BF16)
             for n, a, b in (("w_branch_a", yat, dya_bf), ("w_branch_b", ybt, dyb_bf), ("w_out", mt, dy_bf))]
    dqr, dkr, dv, *early_recv = _attn_bwd(qa, kr, v, dop, early)
    dproj, dlbl = _hgrn_bwd(proj, lbl, do, sprev, dproj)
    dqf, dkvf, dproj, dg_q, dg_kv = _mla_bwd(proj, dqr, dkr, dv, g_q, g_kv, w_uq_p.T, w_kv_p.T, rc, rs1, rs2, dproj)

    dw_in_chip = _dw_in_chip_partials(ht, dproj)
    dw_uq_p = _matmul(cqt, dqf, "dw_uq")
    dw_kv_p = _matmul(ckvt, dkvf, "dw_kv")
    dw_uq = dw_uq_p.reshape(Q_LORA, HEADS, LANE)[:, :, :QK].reshape(Q_LORA, HEADS * QK)
    dw_ukv = jnp.concatenate([dw_kv_p[:, :HEADS * LANE].reshape(KV_LORA, HEADS, LANE)[:, :, :NOPE],
                              dw_kv_p[:, HEADS * LANE:].reshape(KV_LORA, HEADS, LANE)[:, :, :VDIM]],
                             axis=2).reshape(KV_LORA, 1024)
    dx, dg_pre, *late_recv = _dh_dx(dproj, w_in_pt, x, dout, g_pre, [dw_in_chip],
                                    [_to_slots("w_uq", dw_uq).astype(BF16), _to_slots("w_ukv", dw_ukv).astype(BF16)])

    g_sum = _vectors_sum(dg_pre, db_gate, dg_q, dg_kv, dlbl, dgh, dg_post, loss_vec)
    return dx, late_recv[0], dict(zip(MATS, late_recv[1:] + early_recv)), g_sum


def _adamw(g, w, m, v):
    c1 = 1.0 / (1.0 - ADAM_B1 ** ADAM_STEP)
    c2 = 1.0 / (1.0 - ADAM_B2 ** ADAM_STEP)
    nm = ADAM_B1 * m + (1.0 - ADAM_B1) * g
    nv = ADAM_B2 * v + (1.0 - ADAM_B2) * (g * g)
    d = -ADAM_LR * ((nm * c1) / (jnp.sqrt(nv * c2) + ADAM_EPS) + ADAM_WD * w)
    return d, nm, nv


def _sum8(r_ref):
    g = r_ref[0].astype(F32)
    for k in range(1, r_ref.shape[0]):
        g = g + r_ref[k].astype(F32)
    return g


def _sum_adamw_w_in(recv, w, m, v):
    rows, _, cols = w.shape
    tc = 256

    def body(r_ref, w_ref, m_ref, v_ref, g_ref, d_ref, nm_ref, nv_ref):
        g = _sum8(r_ref)
        dense = lambda ref: ref[...].reshape(rows, tc)
        d, nm, nv = _adamw(g, dense(w_ref), dense(m_ref), dense(v_ref))
        for ref, val in ((g_ref, g), (d_ref, d), (nm_ref, nm), (nv_ref, nv)):
            ref[...] = val.reshape(rows, 1, tc)

    blk = pl.BlockSpec((rows, 1, tc), lambda i: (0, 0, i))
    out = jax.ShapeDtypeStruct((rows, 1, cols), F32)
    return pl.pallas_call(
        body,
        grid=(cols // tc,),
        in_specs=[pl.BlockSpec((recv.shape[0], rows, tc), lambda i: (0, 0, i)), blk, blk, blk],
        out_specs=[blk, blk, blk, blk],
        out_shape=[out, out, out, out],
        compiler_params=_params(("arbitrary",)),
        name="sum_adamw_w_in",
    )(recv, w, m, v)


def _sum_adamw_whole(recvs, ws, ms, vs):
    n = len(ws)

    def body(*refs):
        r_refs, w_refs, m_refs, v_refs = refs[:n], refs[n:2 * n], refs[2 * n:3 * n], refs[3 * n:4 * n]
        outs = refs[4 * n:]
        for a in range(n):
            g = _sum8(r_refs[a])
            d, nm, nv = _adamw(g, w_refs[a][...], m_refs[a][...], v_refs[a][...])
            outs[a][...] = g
            outs[n + a][...] = d
            outs[2 * n + a][...] = nm
            outs[3 * n + a][...] = nv

    shapes = [jax.ShapeDtypeStruct(w.shape, F32) for w in ws]
    res = pl.pallas_call(
        body,
        out_shape=shapes * 4,
        compiler_params=pltpu.CompilerParams(vmem_limit_bytes=48 * 2**20),
        name="sum_adamw_mats",
    )(*recvs, *ws, *ms, *vs)
    return res[:n], res[n:2 * n], res[2 * n:3 * n], res[3 * n:]


SMALL = ("g_pre", "b_gate", "g_q", "g_kv", "lb_logits", "g_hgrn", "g_post")
SMALL_SHAPE = dict(g_pre=(1, 1024), b_gate=(1, 2048), g_q=(1, 768), g_kv=(1, 256), lb_logits=(2, 512),
                   g_hgrn=(1, 64), g_post=(1, 1024))


def _vectors_sum(dg_pre, db_gate, dg_q, dg_kv, dlbl, dgh, dg_post, loss_vec):
    def body(gpre_ref, bg_ref, gq_ref, gkv_ref, lbl_ref, gh_ref, gpost_ref, loss_ref, out_ref, mine, got,
             send_sems, recv_sems):
        mine[...] = jnp.zeros_like(mine)
        mine[0:1, :] = gpre_ref[...]
        mine[1:2, :] = bg_ref[:, :1024]
        mine[2:3, :] = bg_ref[:, 1024:]
        mine[3:4, :Q_LORA] = gq_ref[...]
        mine[4:5, :KV_LORA] = gkv_ref[...]
        loss = (0.5 / D_MODEL) * jnp.sum(loss_ref[...], axis=-1, keepdims=True)
        mine[4:5, KV_LORA:] = jnp.broadcast_to(loss, (1, 1024 - KV_LORA))
        mine[5:6, :HG_WIDTH] = lbl_ref[0:1, :]
        mine[5:6, HG_WIDTH:] = lbl_ref[1:2, :]
        gh = gh_ref[...]
        fold = gh[:, :VDIM]
        for h in range(1, HEADS):
            fold = fold + gh[:, VDIM * h:VDIM * (h + 1)]
        mine[6:7, :VDIM] = fold
        mine[7:8, :] = gpost_ref[...]
        x, y, c = _my_place()
        me = 4 * x + 2 * y + c
        got[me] = mine[...]
        copies = [pltpu.make_async_remote_copy(
            src_ref=mine, dst_ref=got.at[me], send_sem=send_sems.at[k], recv_sem=recv_sems.at[k],
            device_id=_flip(k, x, y, c), device_id_type=MESH_ID) for k in range(N_DEV - 1)]
        _start_all([], copies)
        _wait_all([], copies)
        out_ref[...] = _sum8(got)

    return pl.pallas_call(
        body,
        out_shape=jax.ShapeDtypeStruct((8, 1024), F32),
        scratch_shapes=[pltpu.VMEM((8, 1024), F32), pltpu.VMEM((N_DEV, 8, 1024), F32),
                        pltpu.SemaphoreType.DMA((7,)), pltpu.SemaphoreType.DMA((7,))],
        name="vectors_sum",
    )(dg_pre, db_gate, dg_q, dg_kv, dlbl, dgh, dg_post, loss_vec)


def _vectors_adamw(g_sum, ws, ms, vs):
    n = len(SMALL)

    def body(g_ref, *refs):
        w_refs, m_refs, v_refs = refs[:n], refs[n:2 * n], refs[2 * n:3 * n]
        loss_ref, outs = refs[3 * n], refs[3 * n + 1:]
        g = g_ref[...]
        loss_ref[...] = g[4:5, KV_LORA:KV_LORA + 1]
        grads = (g[0:1, :], jnp.concatenate([g[1:2, :], g[2:3, :]], axis=1), g[3:4, :Q_LORA], g[4:5, :KV_LORA],
                 jnp.concatenate([g[5:6, :HG_WIDTH], g[5:6, HG_WIDTH:]], axis=0), g[6:7, :VDIM], g[7:8, :])
        for a in range(n):
            d, nm, nv = _adamw(grads[a], w_refs[a][...], m_refs[a][...], v_refs[a][...])
            outs[a][...] = grads[a]
            outs[n + a][...] = d
            outs[2 * n + a][...] = nm
            outs[3 * n + a][...] = nv

    shapes = [jax.ShapeDtypeStruct(SMALL_SHAPE[k], F32) for k in SMALL]
    res = pl.pallas_call(
        body,
        out_shape=[jax.ShapeDtypeStruct((1, 1), F32)] + shapes * 4,
        name="vectors_adamw",
    )(g_sum, *ws, *ms, *vs)
    return res[0], res[1:n + 1], res[n + 1:2 * n + 1], res[2 * n + 1:3 * n + 1], res[3 * n + 1:]


MATS = ("w_uq", "w_ukv", "w_branch_a", "w_branch_b", "w_out")
COL_SHARDED = dict(w_uq=False, w_ukv=True, w_branch_a=True, w_branch_b=True, w_out=False)
ORDER = ("g_pre", "w_in", "b_gate", "g_q", "w_uq", "g_kv", "w_ukv", "lb_logits", "g_hgrn",
         "w_branch_a", "w_branch_b", "w_out", "g_post")


def _to_slots(name, full):
    r, c = full.shape
    if COL_SHARDED[name]:
        return full.reshape(r, N_DEV, c // N_DEV).transpose(1, 0, 2)
    return full.reshape(N_DEV, r // N_DEV, c)


def _from_slots(name, slots):
    _, r, c = slots.shape
    if COL_SHARDED[name]:
        return slots.transpose(1, 0, 2).reshape(r, N_DEV * c)
    return slots.reshape(N_DEV * r, c)


def kernel(x, g_pre, w_in, b_gate, g_q, w_uq, g_kv, w_ukv, lb_logits, g_hgrn, w_branch_a, w_branch_b, w_out, g_post, loss_target, m_g_pre, m_w_in, m_b_gate, m_g_q, m_w_uq, m_g_kv, m_w_ukv, m_lb_logits, m_g_hgrn, m_w_branch_a, m_w_branch_b, m_w_out, m_g_post, v_g_pre, v_w_in, v_b_gate, v_g_q, v_w_uq, v_g_kv, v_w_ukv, v_lb_logits, v_g_hgrn, v_w_branch_a, v_w_branch_b, v_w_out, v_g_post):
    rows3 = lambda a: jnp.transpose(a, (2, 0, 1))
    w = dict(w_in=rows3(w_in), w_uq=w_uq[0], w_ukv=w_ukv[0], w_branch_a=w_branch_a[0], w_branch_b=w_branch_b[0],
             w_out=w_out[0], g_pre=g_pre, b_gate=b_gate, g_q=g_q, g_kv=g_kv, lb_logits=lb_logits, g_hgrn=g_hgrn,
             g_post=g_post)
    mom = dict(w_in=rows3(m_w_in), w_uq=m_w_uq[0], w_ukv=m_w_ukv[0], w_branch_a=m_w_branch_a[0],
               w_branch_b=m_w_branch_b[0], w_out=m_w_out[0], g_pre=m_g_pre, b_gate=m_b_gate, g_q=m_g_q, g_kv=m_g_kv,
               lb_logits=m_lb_logits, g_hgrn=m_g_hgrn, g_post=m_g_post)
    var = dict(w_in=rows3(v_w_in), w_uq=v_w_uq[0], w_ukv=v_w_ukv[0], w_branch_a=v_w_branch_a[0],
               w_branch_b=v_w_branch_b[0], w_out=v_w_out[0], g_pre=v_g_pre, b_gate=v_b_gate, g_q=v_g_q, g_kv=v_g_kv,
               lb_logits=v_lb_logits, g_hgrn=v_g_hgrn, g_post=v_g_post)

    w_blk = w["w_in"].reshape(W_IN_SHARD, D_MODEL).astype(BF16)
    dx, recv_in, recv, g_sum = _step(x[0], loss_target[0], w_blk, [w[n].astype(BF16) for n in MATS],
                                     g_pre, b_gate, g_q, g_kv, lb_logits, g_hgrn, g_post)

    g_in, d_in, m_in, v_in = _sum_adamw_w_in(recv_in, w["w_in"], mom["w_in"], var["w_in"])
    res = _sum_adamw_whole([recv[n] for n in MATS], *([t[n] for n in MATS] for t in (w, mom, var)))
    total, *vec = _vectors_adamw(g_sum, *([t[n] for n in SMALL] for t in (w, mom, var)))

    outs = []
    for mats, vecs, big in zip(res, vec, (g_in, d_in, m_in, v_in)):
        t = {**{n: a[None] for n, a in zip(MATS, mats)}, **dict(zip(SMALL, vecs)),
             "w_in": jnp.transpose(big, (1, 2, 0))}
        outs += [t[n] for n in ORDER]
    return (total.reshape(()), dx[None], *outs)
```

```python
import math

import jax
import jax.numpy as jnp
import numpy as np
from jax import lax
from jax.experimental import pallas as pl
from jax.experimental.pallas import tpu as pltpu

F32, BF16 = jnp.float32, jnp.bfloat16

D_MODEL = 1024
EPS = 1e-6
HEADS = 8
NOPE, ROPE, VDIM = 64, 32, 64
QK = NOPE + ROPE
Q_LORA, KV_LORA = 768, 256
ROPE_THETA = 10000.0
ATT_CHUNK_SHIFT = 6
HG_BLOCK = 32
HG_WIDTH = 512
D_IN = 5664
D_IN_PAD = 5760
W_IN_SHARD = D_IN // 8
N_DEV = 8
LANE = 128

ADAM_LR, ADAM_B1, ADAM_B2, ADAM_EPS, ADAM_WD, ADAM_STEP = 0.001, 0.9, 0.999, 1e-08, 0.01, 10

W_IN_SEGMENTS = ((3616, 5664, 0), (1056, 1568, 2048), (3104, 3616, 2560), (1568, 3104, 3072),
                 (0, 1024, 4608), (1024, 1056, 5696))

NT = (((1,), (1,)), ((), ()))
TN = (((0,), (0,)), ((), ()))
MESH_ID = pl.DeviceIdType.MESH


def _w_in_pieces():
    out = []
    for lo, hi, dst in W_IN_SEGMENTS:
        c = lo
        while c < hi:
            p = c // W_IN_SHARD
            e = min(hi, (p + 1) * W_IN_SHARD)
            out.append((p, c - p * W_IN_SHARD, e - p * W_IN_SHARD, dst + c - lo))
            c = e
    return out


def _params(sem, vmem_mb=48):
    return pltpu.CompilerParams(dimension_semantics=sem, vmem_limit_bytes=vmem_mb * 2**20)


def _dot(a, b):
    return jnp.dot(a, b, preferred_element_type=F32)


def _dotg(a, b, dims):
    return lax.dot_general(a, b, dims, preferred_element_type=F32)


def _split2(x):
    hi = x.astype(BF16)
    return hi, (x - hi.astype(F32)).astype(BF16)


def _sel_left(m01, x):
    hi, lo = _split2(x)
    return _dot(m01, hi) + _dot(m01, lo)


def _sel_right(x, m01):
    hi, lo = _split2(x)
    return _dot(hi, m01) + _dot(lo, m01)


def _hi_lo(x):
    hi = x.astype(BF16).astype(F32)
    return hi, x - hi


def _sigmoid(x):
    return 0.5 * jnp.tanh(0.5 * x) + 0.5


def _rope(x, c, s1, s2):
    return x * c + pltpu.roll(x, 112, 1) * s1 + pltpu.roll(x, 16, 1) * s2


def _unrope(d, c, s1, s2):
    return d * c + pltpu.roll(d * s1, 16, 1) + pltpu.roll(d * s2, 112, 1)


def _my_place():
    return lax.axis_index("x"), lax.axis_index("y"), lax.axis_index("c")


def _flip(k, x, y, c):
    fx, fy, fc = (k + 1) >> 2 & 1, (k + 1) >> 1 & 1, (k + 1) & 1
    return (1 - x if fx else x), (1 - y if fy else y), (1 - c if fc else c)


def _to_all_copies(s_refs, r_refs, sems, spread):
    send_sems, recv_sems, local_sems = sems
    x, y, c = _my_place()
    me = 4 * x + 2 * y + c
    src = (lambda a, p: s_refs[a]) if spread else (lambda a, p: s_refs[a].at[p])
    local = [pltpu.make_async_copy(src(a, me), r_refs[a].at[me], local_sems.at[a]) for a in range(len(s_refs))]
    remote = []
    for k in range(N_DEV - 1):
        px, py, pc = _flip(k, x, y, c)
        for a in range(len(s_refs)):
            remote.append(pltpu.make_async_remote_copy(
                src_ref=src(a, 4 * px + 2 * py + pc), dst_ref=r_refs[a].at[me],
                send_sem=send_sems.at[7 * a + k], recv_sem=recv_sems.at[7 * a + k],
                device_id=(px, py, pc), device_id_type=MESH_ID))
    return local, remote


def _to_chips_copies(s_refs, r_refs, sems):
    send_sems, recv_sems, local_sems = sems
    x, y, c = _my_place()
    me = 2 * x + y
    local = [pltpu.make_async_copy(s_refs[a].at[me], r_refs[a].at[me], local_sems.at[a]) for a in range(len(s_refs))]
    remote = []
    for k in range(3):
        px = 1 - x if (k + 1) >> 1 & 1 else x
        py = 1 - y if (k + 1) & 1 else y
        for a in range(len(s_refs)):
            remote.append(pltpu.make_async_remote_copy(
                src_ref=s_refs[a].at[2 * px + py], dst_ref=r_refs[a].at[me],
                send_sem=send_sems.at[3 * a + k], recv_sem=recv_sems.at[3 * a + k],
                device_id=(px, py, c), device_id_type=MESH_ID))
    return local, remote


def _start_all(local, remote):
    for cp in local + remote:
        cp.start()


def _wait_all(local, remote):
    for cp in remote:
        cp.wait_recv()
    for cp in remote:
        cp.wait_send()
    for cp in local:
        cp.wait()


def _copy_sems(n, peers):
    return [pltpu.SemaphoreType.DMA((peers * n,)), pltpu.SemaphoreType.DMA((peers * n,)),
            pltpu.SemaphoreType.DMA((n,))]


ANY = pl.BlockSpec(memory_space=pl.ANY)


def _dw_in_slots(ht, dproj):
    m, k = ht.shape
    n = dproj.shape[1]
    tn, tk = 1152, 1024
    nj, nk = n // tn, k // tk
    by_tile = [[] for _ in range(nj)]
    for p, lo, hi, dst in _w_in_pieces():
        while lo < hi:
            j = dst // tn
            cnt = min(hi - lo, (j + 1) * tn - dst)
            by_tile[j].append((p, lo, lo + cnt, dst - j * tn))
            lo, dst = lo + cnt, dst + cnt

    def body(a_ref, b_ref, s_ref, acc_ref):
        j, l = pl.program_id(0), pl.program_id(1)

        @pl.when(l == 0)
        def _():
            acc_ref[...] = jnp.zeros_like(acc_ref)

        acc_ref[...] += _dot(a_ref[...], b_ref[...])

        @pl.when(l == nk - 1)
        def _():
            at = acc_ref[...].T
            for jj in range(nj):
                @pl.when(j == jj)
                def _(jj=jj):
                    for p, lo, hi, d in by_tile[jj]:
                        s_ref[p, lo:hi, :] = at[d:d + hi - lo, :].astype(BF16)

    return pl.pallas_call(
        body,
        grid=(nj, nk),
        in_specs=[pl.BlockSpec((m, tk), lambda j, l: (0, l)), pl.BlockSpec((tk, tn), lambda j, l: (l, j))],
        out_specs=pl.BlockSpec((N_DEV, W_IN_SHARD, m), lambda j, l: (0, 0, 0)),
        out_shape=jax.ShapeDtypeStruct((N_DEV, W_IN_SHARD, m), BF16),
        scratch_shapes=[pltpu.VMEM((m, tn), F32)],
        compiler_params=_params(("arbitrary", "arbitrary")),
        name="dw_in",
    )(ht, dproj)


PROJ_DT = F32
GP_TN = 256
GP_COLS = 5888
GP_NT = GP_COLS // GP_TN


def _gp_tile_pieces():
    tiles = [[] for _ in range(GP_NT)]
    for p, lo, hi, dst in _w_in_pieces():
        while lo < hi:
            t = dst // GP_TN
            n = min(hi - lo, (t + 1) * GP_TN - dst)
            tiles[t].append((p, lo, lo + n, dst - t * GP_TN))
            lo, dst = lo + n, dst + n
    return tiles


def _gp_tables():
    pieces = _gp_tile_pieces()
    rank_of = {None: 0, 0: 1, 1: 2, 2: 2, 4: 3, 5: 3, 3: 4, 6: 5}
    order = np.zeros((N_DEV, GP_NT), np.int32)
    waits = np.zeros((N_DEV, GP_NT), np.int32)
    for me in range(N_DEV):
        x, y, c = me >> 2 & 1, me >> 1 & 1, me & 1
        chips = [(1 - x, y), (x, 1 - y), (1 - x, 1 - y)]

        def sem_of(p):
            px, py, pc = p >> 2 & 1, p >> 1 & 1, p & 1
            if (px, py) == (x, y):
                return None if pc == c else 0
            j = chips.index((px, py))
            return 1 + j if pc == c else 4 + j

        needs = [sorted({sem_of(p) for p, _, _, _ in tile} - {None}) for tile in pieces]
        ranks = [max([rank_of[k] for k in ks], default=0) for ks in needs]
        seq = sorted(range(GP_NT), key=lambda t: (ranks[t], t))
        seen = set()
        for step, t in enumerate(seq):
            order[me, step] = t
            new = [k for k in needs[t] if k not in seen]
            for k in new:
                waits[me, step] |= 1 << k
            seen.update(new)
        assert seen == set(range(7)), (me, seen)
    return order, waits


def _gather_proj(x, g_pre, w_blk, shards):
    s = x.shape[0]
    tx = 512
    ns = len(shards)
    tile_pieces = _gp_tile_pieces()
    order_np, waits_np = _gp_tables()
    xq, yq, cq = _my_place()
    me_out = 4 * xq + 2 * yq + cq
    order = lax.dynamic_index_in_dim(jnp.asarray(order_np), me_out, 0, keepdims=False)
    waits = lax.dynamic_index_in_dim(jnp.asarray(waits_np), me_out, 0, keepdims=False)

    def body(order_ref, waits_ref, x_hbm, g_ref, wblk_hbm, *rest):
        shard_refs, (proj_ref, wt_ref, ht_hbm), got_refs = rest[:ns], rest[ns:ns + 3], rest[ns + 3:2 * ns + 3]
        recv, h_ref, wtile, xbuf, htbuf = rest[2 * ns + 3:2 * ns + 8]
        send_sems, recv_sems, misc_sems = rest[2 * ns + 8:2 * ns + 11]
        sems = rest[2 * ns + 11:]
        t = pl.program_id(0)
        x_, y_, c = _my_place()
        sibling = (x_, y_, 1 - c)
        chips = [(1 - x_, y_), (x_, 1 - y_), (1 - x_, 1 - y_)]
        idx = lambda px, py, pc: 4 * px + 2 * py + pc
        me = idx(x_, y_, c)

        def copy(k, slot, to, src=None):
            return pltpu.make_async_remote_copy(
                src_ref=recv.at[slot] if src is None else src, dst_ref=recv.at[slot],
                send_sem=send_sems.at[k], recv_sem=recv_sems.at[k], device_id=to, device_id_type=MESH_ID)

        mine = pltpu.make_async_copy(wblk_hbm, recv.at[me], misc_sems.at[0])
        first = [copy(0, me, sibling, src=wblk_hbm)] + [copy(1 + j, me, (*chips[j], c), src=wblk_hbm) for j in range(2)]
        passed = [copy(4 + j, idx(*ch, c), sibling) for j, ch in enumerate(chips)]
        onward = [copy(3, idx(*chips[0], c), (*chips[1], c)), copy(3, idx(*chips[1], c), (*chips[0], c))]
        arrivals = ([copy(0, idx(x_, y_, 1 - c), sibling)] + [copy(1 + j, idx(*ch, c), sibling) for j, ch in enumerate(chips)]
                    + [copy(4 + j, idx(*ch, 1 - c), sibling) for j, ch in enumerate(chips)])

        @pl.when(t == 0)
        def _():
            mine.start()
            for cp in first:
                cp.start()
            _start_all(*_to_all_copies(shard_refs, got_refs, sems, True))

            def load(i):
                return pltpu.make_async_copy(x_hbm.at[pl.ds(i * tx, tx), :], xbuf.at[i & 1], misc_sems.at[1 + (i & 1)])

            def store(i):
                return pltpu.make_async_copy(htbuf.at[i & 1], ht_hbm.at[:, pl.ds(i * tx, tx)], misc_sems.at[3 + (i & 1)])

            load(0).start()
            for i in range(s // tx):
                if i + 1 < s // tx:
                    load(i + 1).start()
                load(i).wait()
                xv = xbuf[i & 1]
                r = lax.rsqrt(jnp.mean(xv * xv, axis=-1, keepdims=True) + EPS)
                h = (xv * r * g_ref[...]).astype(BF16)
                h_ref[i * tx:(i + 1) * tx, :] = h
                if i >= 2:
                    store(i - 2).wait()
                htbuf[i & 1] = h.T
                store(i).start()
            for i in range(max(s // tx - 2, 0), s // tx):
                store(i).wait()
            mine.wait()

        w = waits_ref[t]
        for k in range(7):
            @pl.when((w >> k) & 1 == 1)
            def _(k=k):
                arrivals[k].wait_recv()
                if 1 <= k <= 3:
                    passed[k - 1].start()
                if 1 <= k <= 2:
                    @pl.when(c == k - 1)
                    def _():
                        onward[k - 1].start()

        tile = order_ref[t]
        for tt in range(GP_NT):
            @pl.when(tile == tt)
            def _(tt=tt):
                covered = sorted((d, d + hi - lo) for _, lo, hi, d in tile_pieces[tt])
                at = 0
                for lo_z, hi_z in covered + [(GP_TN, GP_TN)]:
                    if lo_z > at:
                        wtile[at:lo_z, :] = jnp.zeros((lo_z - at, D_MODEL), BF16)
                    at = max(at, hi_z)
                for p, lo, hi, d in tile_pieces[tt]:
                    wtile[d:d + hi - lo, :] = recv[p, lo:hi, :]

        wt = wtile[...]
        wt_ref[...] = wt
        proj_ref[...] = _dotg(h_ref[...], wt, NT).astype(PROJ_DT)

        @pl.when(t == GP_NT - 1)
        def _():
            for cp in first + passed + onward[:1]:
                cp.wait_send()
            _wait_all(*_to_all_copies(shard_refs, got_refs, sems, True))

    grid_spec = pltpu.PrefetchScalarGridSpec(
        num_scalar_prefetch=2,
        grid=(GP_NT,),
        in_specs=[ANY, pl.BlockSpec((1, D_MODEL), lambda t, o, w: (0, 0)), ANY] + [ANY] * ns,
        out_specs=[pl.BlockSpec((s, GP_TN), lambda t, o, w: (0, o[t])),
                   pl.BlockSpec((GP_TN, D_MODEL), lambda t, o, w: (o[t], 0)), ANY] + [ANY] * ns,
        scratch_shapes=[pltpu.VMEM((N_DEV, W_IN_SHARD, D_MODEL), BF16), pltpu.VMEM((s, D_MODEL), BF16),
                        pltpu.VMEM((GP_TN, D_MODEL), BF16), pltpu.VMEM((2, tx, D_MODEL), F32),
                        pltpu.VMEM((2, D_MODEL, tx), BF16),
                        pltpu.SemaphoreType.DMA((7,)), pltpu.SemaphoreType.DMA((7,)), pltpu.SemaphoreType.DMA((5,))]
        + _copy_sems(ns, 7),
    )
    return pl.pallas_call(
        body,
        grid_spec=grid_spec,
        out_shape=[jax.ShapeDtypeStruct((s, GP_COLS), PROJ_DT), jax.ShapeDtypeStruct((GP_COLS, D_MODEL), BF16),
                   jax.ShapeDtypeStruct((D_MODEL, s), BF16)]
        + [jax.ShapeDtypeStruct((N_DEV,) + b.shape, b.dtype) for b in shards],
        compiler_params=_params(("arbitrary",), 56),
        name="gather_proj",
    )(order, waits, x, g_pre, w_blk, *shards)


def _mla_prep(proj, g_q, g_kv, w_uq_p, w_kv_p, rc, rs1, rs2):
    s = proj.shape[0]
    tm = 256
    scale = 1.0 / math.sqrt(QK)

    def body(cq_ref, ckv_ref, kpe_ref, gq_ref, gkv_ref, wuq_ref, wkv_ref, c_ref, s1_ref, s2_ref,
             qr_ref, kr_ref, v_ref, cqt_ref, ckvt_ref):
        cq = cq_ref[...].astype(F32)
        r = lax.rsqrt(jnp.mean(cq * cq, axis=-1, keepdims=True) + EPS)
        cqn = (cq * r * gq_ref[...]).astype(BF16)
        cqt_ref[...] = cqn.T
        q = _dot(cqn, wuq_ref[...])
        ckv = ckv_ref[...].astype(F32)
        r = lax.rsqrt(jnp.mean(ckv * ckv, axis=-1, keepdims=True) + EPS)
        ckvn = (ckv * r * gkv_ref[...]).astype(BF16)
        ckvt_ref[...] = ckvn.T
        kv = _dot(ckvn, wkv_ref[...])
        c, s1, s2 = c_ref[...], s1_ref[...], s2_ref[...]
        lane = lax.broadcasted_iota(jnp.int32, (tm, LANE), 1)
        kpe = _rope(kpe_ref[...].astype(F32), c, s1, s2) + jnp.where((lane == QK) | (lane == QK + 1), 1.0, 0.0)
        vone = jnp.where((lane == VDIM) | (lane == VDIM + 1), 1.0, 0.0)
        for h in range(HEADS):
            sl = slice(LANE * h, LANE * (h + 1))
            qr_ref[:, sl] = (_rope(q[:, sl], c, s1, s2) * scale).astype(BF16)
            kr_ref[:, sl] = (kv[:, sl] + kpe).astype(BF16)
            v_ref[:, sl] = (kv[:, HEADS * LANE + LANE * h:HEADS * LANE + LANE * (h + 1)] + vone).astype(BF16)

    row = lambda w, j: pl.BlockSpec((tm, w), lambda i: (i, j))
    col = lambda w: pl.BlockSpec((w, tm), lambda i: (0, i))
    full = lambda a: pl.BlockSpec(a.shape, lambda i: (0, 0))
    return pl.pallas_call(
        body,
        grid=(s // tm,),
        in_specs=[row(768, 6), row(256, 21), row(128, 44), full(g_q), full(g_kv), full(w_uq_p), full(w_kv_p),
                  row(128, 0), row(128, 0), row(128, 0)],
        out_specs=[row(1024, 0), row(1024, 0), row(1024, 0), col(768), col(256)],
        out_shape=[jax.ShapeDtypeStruct((s, 1024), BF16), jax.ShapeDtypeStruct((s, 1024), BF16),
                   jax.ShapeDtypeStruct((s, 1024), BF16), jax.ShapeDtypeStruct((768, s), BF16),
                   jax.ShapeDtypeStruct((256, s), BF16)],
        compiler_params=_params(("arbitrary",)),
        name="mla_prep",
    )(proj, proj, proj, g_q, g_kv, w_uq_p, w_kv_p, rc, rs1, rs2)


ATT_T = 512
ATT_FWD_HEADS = 4


def _chunk_mask(transposed):
    r = lax.broadcasted_iota(jnp.int32, (ATT_T, ATT_T), 0) >> ATT_CHUNK_SHIFT
    c = lax.broadcasted_iota(jnp.int32, (ATT_T, ATT_T), 1) >> ATT_CHUNK_SHIFT
    return (r <= c) if transposed else (c <= r)


def _attn_fwd(qr, kr, vp, shards):
    s = qr.shape[0]
    t = ATT_T
    g = ATT_FWD_HEADS
    ns = len(shards)

    def body(q_ref, k_ref, v_ref, *rest):
        shard_refs, (o_ref, qa_ref), got_refs = rest[:ns], rest[ns:ns + 2], rest[ns + 2:2 * ns + 2]
        sc_ref, sems = rest[2 * ns + 2], rest[2 * ns + 3:]
        qi = pl.program_id(1)

        @pl.when((pl.program_id(0) == 0) & (qi == 0))
        def _():
            _start_all(*_to_all_copies(shard_refs, got_refs, sems, True))
        lane = lax.broadcasted_iota(jnp.int32, (t, LANE), 1)
        sls = [slice(LANE * a, LANE * (a + 1)) for a in range(g)]
        qs = [q_ref[:, sl] for sl in sls]

        def scores(j):
            rows = pl.ds(pl.multiple_of(j * t, t), t)
            for a in range(g):
                sc_ref[j & 1, a] = _dotg(qs[a], k_ref[rows, sls[a]], NT)

        def step(j, carry, masked):
            rows = pl.ds(pl.multiple_of(j * t, t), t)
            out = []
            for a in range(g):
                m, acc = carry[a]
                sc = sc_ref[j & 1, a]
                if masked:
                    sc = jnp.where(_chunk_mask(False), sc, -1e30)
                m_new = jnp.maximum(m, jnp.max(sc, axis=-1, keepdims=True))
                p = jnp.exp(sc - m_new).astype(BF16)
                acc = jnp.exp(m - m_new) * acc + _dot(p, v_ref[rows, sls[a]])
                out.append((m_new, acc))
            return tuple(out)

        def loop(j, carry):
            carry = step(j, carry, False)
            scores(j + 1)
            return carry

        init = tuple((jnp.full((t, 1), -1e30, F32), jnp.zeros((t, LANE), F32)) for _ in range(g))
        scores(0)
        carry = lax.fori_loop(0, qi, loop, init)
        carry = step(qi, carry, True)
        outs = []
        for a in range(g):
            m, acc = carry[a]
            l = acc[:, VDIM:VDIM + 1]
            outs.append(acc / l)
            hi, lo_part = _hi_lo(-(m + jnp.log(l)))
            qa = jnp.where(lane == QK, hi, jnp.where(lane == QK + 1, lo_part, qs[a].astype(F32)))
            qa_ref[:, sls[a]] = qa.astype(BF16)
        for p in range(g // 2):
            o_ref[:, LANE * p:LANE * (p + 1)] = jnp.where(lane < VDIM, outs[2 * p], pltpu.roll(outs[2 * p + 1], VDIM, 1))

        @pl.when((pl.program_id(0) == HEADS // g - 1) & (qi == s // t - 1))
        def _():
            _wait_all(*_to_all_copies(shard_refs, got_refs, sems, True))

    return pl.pallas_call(
        body,
        grid=(HEADS // g, s // t),
        in_specs=[
            pl.BlockSpec((t, g * LANE), lambda h, i: (i, h)),
            pl.BlockSpec((s, g * LANE), lambda h, i: (0, h)),
            pl.BlockSpec((s, g * LANE), lambda h, i: (0, h)),
        ] + [ANY] * ns,
        out_specs=[
            pl.BlockSpec((t, g * VDIM), lambda h, i: (i, h)),
            pl.BlockSpec((t, g * LANE), lambda h, i: (i, h)),
        ] + [ANY] * ns,
        out_shape=[jax.ShapeDtypeStruct((s, 512), F32), jax.ShapeDtypeStruct((s, 1024), BF16)]
        + [jax.ShapeDtypeStruct((N_DEV,) + b.shape, b.dtype) for b in shards],
        scratch_shapes=[pltpu.VMEM((2, g, t, t), F32)] + _copy_sems(ns, 7),
        compiler_params=_params(("arbitrary", "arbitrary")),
        name="attn_fwd",
    )(qr, kr, vp, *shards)


def _attn_bwd(qa, kr, vp, dop, sends):
    s = qa.shape[0]
    t = ATT_T
    nq = s // t
    ns = len(sends)

    def body(q_ref, k_ref, v_ref, do_ref, *rest):
        send_refs, (dq_out, dk_out, dv_out) = rest[:ns], rest[ns:ns + 3]
        recv_refs = rest[ns + 3:2 * ns + 3]
        (dq_ref, dk_ref, dv_ref), sems = rest[2 * ns + 3:2 * ns + 6], rest[2 * ns + 6:]
        j = pl.program_id(1)
        sls = [slice(LANE * a, LANE * (a + 1)) for a in range(2)]

        @pl.when((pl.program_id(0) == 0) & (j == 0))
        def _():
            _start_all(*_to_all_copies(send_refs, recv_refs, sems, False))

        @pl.when(j == 0)
        def _():
            dq_ref[...] = jnp.zeros_like(dq_ref)

        dk_ref[...] = jnp.zeros_like(dk_ref)
        dv_ref[...] = jnp.zeros_like(dv_ref)
        ks = [k_ref[:, sl] for sl in sls]
        vs = [v_ref[:, sl] for sl in sls]

        def part(i, k_lo, k_n, q_lo, q_n, masked):
            rows = pl.ds(pl.multiple_of(i * t + q_lo, 256), q_n)
            keys = slice(k_lo, k_lo + k_n)
            for a in range(2):
                q = q_ref[rows, sls[a]]
                do = do_ref[rows, sls[a]]
                sc = _dotg(ks[a][keys], q, NT)
                if masked:
                    kc = lax.broadcasted_iota(jnp.int32, (k_n, q_n), 0) >> ATT_CHUNK_SHIFT
                    qc = lax.broadcasted_iota(jnp.int32, (k_n, q_n), 1) >> ATT_CHUNK_SHIFT
                    sc = jnp.where(kc <= qc, sc, -1e30)
                p = jnp.exp(sc)
                ds = (p * _dotg(vs[a][keys], do, NT)).astype(BF16)
                dv_ref[keys, sls[a]] += _dot(p.astype(BF16), do)
                dk_ref[keys, sls[a]] += _dot(ds, q)
                dq_ref[rows, sls[a]] += _dotg(ds, ks[a][keys], TN)

        half = t // 2
        part(j, 0, half, 0, t, True)
        part(j, half, half, half, half, True)

        def loop(i, c):
            part(i, 0, t, 0, t, False)
            return c

        lax.fori_loop(j + 1, nq, loop, 0)
        dk_out[...] = dk_ref[...].astype(BF16)
        dv_out[...] = dv_ref[...].astype(BF16)

        @pl.when(j == nq - 1)
        def _():
            dq_out[...] = dq_ref[...].astype(BF16)

        @pl.when((pl.program_id(0) == HEADS // 2 - 1) & (j == nq - 1))
        def _():
            _wait_all(*_to_all_copies(send_refs, recv_refs, sems, False))

    blk = pl.BlockSpec((t, 2 * LANE), lambda h, j: (j, h))
    whole = pl.BlockSpec((s, 2 * LANE), lambda h, j: (0, h))
    out = jax.ShapeDtypeStruct((s, 1024), BF16)
    return pl.pallas_call(
        body,
        grid=(HEADS // 2, nq),
        in_specs=[whole, blk, blk, whole] + [ANY] * ns,
        out_specs=[whole, blk, blk] + [ANY] * ns,
        out_shape=[out, out, out] + [jax.ShapeDtypeStruct(a.shape, a.dtype) for a in sends],
        scratch_shapes=[pltpu.VMEM((s, 2 * LANE), F32), pltpu.VMEM((t, 2 * LANE), F32),
                        pltpu.VMEM((t, 2 * LANE), F32)] + _copy_sems(ns, 7),
        compiler_params=_params(("arbitrary", "arbitrary")),
        name="attn_bwd",
    )(qa, kr, vp, dop, *sends)


HG_T = 256
HG_NC = HG_T // HG_BLOCK
HG_G = 4
GW = 64 * HG_G


def _hg_consts():
    r = jnp.arange(HG_T)[:, None]
    c = jnp.arange(HG_T)[None, :]
    same = (r // HG_BLOCK) == (c // HG_BLOCK)
    mcum = (same & (c <= r)).astype(BF16)
    mrev = (same & (c >= r)).astype(BF16)
    msum = same.astype(BF16)
    a = jnp.arange(GW) // 64
    bd = (a[:, None] == a[None, :]).astype(F32)
    return mcum, mrev, msum, bd


def _stack_heads(xg, head):
    return jnp.concatenate([jnp.where(head == h, xg, 0.0) for h in range(HG_G)], axis=0)


def _unstack_heads(r, head, t):
    out = r[(HG_G - 1) * t:]
    for h in range(HG_G - 2, -1, -1):
        out = jnp.where(head == h, r[h * t:(h + 1) * t], out)
    return out


def _compact_state(st):
    out = st[:64]
    for h in range(1, HG_G):
        out = out + st[64 * h:64 * (h + 1)]
    return out


def _expand_state(cs, head64):
    return jnp.concatenate([jnp.where(head64 == h, cs, 0.0) for h in range(HG_G)], axis=0)


def _hg_pre(hq, hf, lbl, mcum, msum):
    lb = _sigmoid(lbl[0:1, :] - lbl[1:2, :])
    sig = _sigmoid(hf)
    f = lb + (1.0 - lb) * sig
    lf = jnp.log(f)
    b = _sel_left(mcum, lf)
    big_l = _sel_left(msum, lf)
    k = 1.0 - f
    qd = hq * jnp.exp(b)
    ki = k * jnp.exp(-b)
    ke = k * jnp.exp(big_l - b)
    return lb, sig, f, b, big_l, qd, ki, ke


def _hgrn_fwd(proj, lbl):
    s = proj.shape[0]
    t = HG_T
    mcum, _, msum, bd = _hg_consts()

    def body(hq_ref, hf_ref, hi_ref, lbl_ref, mcum_ref, msum_ref, bd_ref, o_ref, sp_ref, st_ref):
        @pl.when(pl.program_id(0) == 0)
        def _():
            st_ref[...] = jnp.zeros_like(st_ref)

        mc = mcum_ref[...]
        _, _, _, _, big_l, qd, ki, ke = _hg_pre(hq_ref[...].astype(F32), hf_ref[...].astype(F32), lbl_ref[...], mc,
                                                msum_ref[...])
        el = jnp.exp(big_l)
        hi = hi_ref[...]
        head = lax.broadcasted_iota(jnp.int32, (t, GW), 1) >> 6
        mask = jnp.concatenate([mc] * HG_G, axis=0) > 0.5
        for p in range(HEADS // HG_G):
            sl = slice(GW * p, GW * (p + 1))
            vp = hi[:, sl].astype(BF16)
            qs = _stack_heads(qd[:, sl], head).astype(BF16)
            a = jnp.where(mask, _dotg(qs, ki[:, sl].astype(BF16), NT), 0.0)
            o_intra = _unstack_heads(_dot(a.astype(BF16), vp), head, t)
            qb = qd[:, sl].astype(BF16)
            kb = ke[:, sl].astype(BF16)
            st = st_ref[p]
            for c in range(HG_NC):
                rows = slice(HG_BLOCK * c, HG_BLOCK * (c + 1))
                sp_ref[c, :, sl] = _compact_state(st)
                o_ref[rows, sl] = o_intra[rows] + _dotg(qb[rows], st.astype(BF16), NT)
                u = _dotg(vp[rows], kb[rows], TN) * bd_ref[...]
                st = st * el[HG_BLOCK * c:HG_BLOCK * c + 1, sl] + u
            st_ref[p] = st

    row = lambda j: pl.BlockSpec((t, HG_WIDTH), lambda i: (i, j))
    full = lambda a: pl.BlockSpec(a.shape, lambda i: (0, 0))
    return pl.pallas_call(
        body,
        grid=(s // t,),
        in_specs=[row(6), row(7), row(8), full(lbl), full(mcum), full(msum), full(bd)],
        out_specs=[row(0), pl.BlockSpec((HG_NC, 64, HG_WIDTH), lambda i: (i, 0, 0))],
        out_shape=[jax.ShapeDtypeStruct((s, HG_WIDTH), F32),
                   jax.ShapeDtypeStruct((s // HG_BLOCK, 64, HG_WIDTH), F32)],
        scratch_shapes=[pltpu.VMEM((HEADS // HG_G, GW, GW), F32)],
        compiler_params=_params(("arbitrary",)),
        name="hgrn_fwd",
    )(proj, proj, proj, lbl, mcum, msum, bd)


def _hgrn_bwd(proj, lbl, do, sprev, dproj):
    s = proj.shape[0]
    t = HG_T
    nt = s // t
    mcum, mrev, msum, bd = _hg_consts()

    def body(hq_ref, hf_ref, hi_ref, lbl_ref, do_ref, sp_ref, mcum_ref, mrev_ref, msum_ref, bd_ref,
             dproj_in, dh_ref, dlbl_ref, g_ref):
        del dproj_in

        @pl.when(pl.program_id(0) == 0)
        def _():
            g_ref[...] = jnp.zeros_like(g_ref)
            dlbl_ref[...] = jnp.zeros_like(dlbl_ref)

        mc = mcum_ref[...]
        lb, sig, f, b, big_l, qd, ki, ke = _hg_pre(hq_ref[...].astype(F32), hf_ref[...].astype(F32), lbl_ref[...], mc,
                                                   msum_ref[...])
        el = jnp.exp(big_l)
        hi = hi_ref[...]
        dov = do_ref[...]
        head = lax.broadcasted_iota(jnp.int32, (t, GW), 1) >> 6
        head64 = lax.broadcasted_iota(jnp.int32, (64, GW), 1) >> 6
        mask = jnp.concatenate([mc] * HG_G, axis=0) > 0.5
        dqd_parts, dke_parts, dv_parts, del_parts, dki_parts = [], [], [], [], []
        for p in range(HEADS // HG_G):
            sl = slice(GW * p, GW * (p + 1))
            vp = hi[:, sl].astype(BF16)
            qs = _stack_heads(qd[:, sl], head).astype(BF16)
            kip = ki[:, sl].astype(BF16)
            dos = _stack_heads(dov[:, sl], head).astype(BF16)
            a = jnp.where(mask, _dotg(qs, kip, NT), 0.0).astype(BF16)
            da = jnp.where(mask, _dotg(dos, vp, NT), 0.0).astype(BF16)
            r = _dot(da, kip)
            dki_parts.append(_dotg(da, qs, TN))
            qb = qd[:, sl].astype(BF16)
            kb = ke[:, sl].astype(BF16)
            dob = dov[:, sl].astype(BF16)
            g = g_ref[p]
            dqd_c, dv_c, dke_c, del_c = [], [], [], []
            for c in range(HG_NC - 1, -1, -1):
                rows = slice(HG_BLOCK * c, HG_BLOCK * (c + 1))
                gb = g.astype(BF16)
                st = _expand_state(sp_ref[c, :, sl], head64)
                dqd_c.append(_dot(dob[rows], st.astype(BF16)))
                dv_c.append(_dotg(kb[rows], gb, NT))
                dke_c.append(_dot(vp[rows], gb))
                del_c.append(jnp.broadcast_to(jnp.sum(g * st, axis=0, keepdims=True), (HG_BLOCK, GW)))
                g = g * el[HG_BLOCK * c:HG_BLOCK * c + 1, sl] + _dotg(dob[rows], qb[rows], TN) * bd_ref[...]
            g_ref[p] = g
            up = lambda parts: jnp.concatenate(parts[::-1], axis=0)
            dqd_parts.append(_unstack_heads(r, head, t) + up(dqd_c))
            dv_parts.append(_dotg(a, dos, TN) + up(dv_c))
            dke_parts.append(up(dke_c))
            del_parts.append(up(del_c))
        wide = lambda parts: jnp.concatenate(parts, axis=1)
        dqd, dke, dki, dvv, del_rows = wide(dqd_parts), wide(dke_parts), wide(dki_parts), wide(dv_parts), wide(del_parts)
        dh_ref[:, :HG_WIDTH] = (dqd * jnp.exp(b)).astype(BF16)
        dh_ref[:, 2 * HG_WIDTH:] = dvv.astype(BF16)
        dke_ke = dke * ke
        db = dqd * qd - dki * ki - dke_ke
        dl_rows = _sel_left(msum_ref[...], dke_ke) + del_rows * el
        is_last = (lax.broadcasted_iota(jnp.int32, (t, HG_WIDTH), 0) & (HG_BLOCK - 1)) == HG_BLOCK - 1
        db = db + jnp.where(is_last, dl_rows, 0.0)
        dlf = _sel_left(mrev_ref[...], db)
        dk = dki * jnp.exp(-b) + dke * jnp.exp(big_l - b)
        df = dlf / f - dk
        dh_ref[:, HG_WIDTH:2 * HG_WIDTH] = (df * (1.0 - lb) * sig * (1.0 - sig)).astype(BF16)
        dlb = jnp.sum(df * (1.0 - sig), axis=0, keepdims=True) * lb * (1.0 - lb)
        dlbl_ref[0:1, :] += dlb
        dlbl_ref[1:2, :] -= dlb

    rrow = lambda j: pl.BlockSpec((t, HG_WIDTH), lambda i: (nt - 1 - i, j))
    full = lambda a: pl.BlockSpec(a.shape, lambda i: (0, 0))
    return pl.pallas_call(
        body,
        grid=(nt,),
        in_specs=[rrow(6), rrow(7), rrow(8), full(lbl), rrow(0),
                  pl.BlockSpec((HG_NC, 64, HG_WIDTH), lambda i: (nt - 1 - i, 0, 0)),
                  full(mcum), full(mrev), full(msum), full(bd), pl.BlockSpec(memory_space=pl.ANY)],
        out_specs=[pl.BlockSpec((t, 3 * HG_WIDTH), lambda i: (nt - 1 - i, 2)),
                   pl.BlockSpec((2, HG_WIDTH), lambda i: (0, 0))],
        out_shape=[jax.ShapeDtypeStruct(dproj.shape, BF16), jax.ShapeDtypeStruct((2, HG_WIDTH), F32)],
        input_output_aliases={10: 0},
        scratch_shapes=[pltpu.VMEM((HEADS // HG_G, GW, GW), F32)],
        compiler_params=_params(("arbitrary",)),
        name="hgrn_bwd",
    )(proj, proj, proj, lbl, do, sprev, mcum, mrev, msum, bd, dproj)


def _tail(x, tgt, proj, attn, o, w_a, w_b, w_out, w_at, w_bt, w_outt, b_gate, g_post, gh):
    s = x.shape[0]
    tm = 256
    ones64 = (jnp.arange(HG_WIDTH)[:, None] // 64 == jnp.arange(HG_WIDTH)[None, :] // 64).astype(BF16)
    weights = (w_a, w_b, w_out, w_at, w_bt, w_outt)

    def body(x_ref, t_ref, ml_ref, ga_ref, gb_ref, at_ref, o_ref, *rest):
        w_hbm, (bg_ref, gp_ref, gh_ref, ones_ref) = rest[:6], rest[6:10]
        (dout_ref, dpj_ref, dop_ref, do_ref, mt_ref, dy_ref, yat_ref, dya_ref, ybt_ref, dyb_ref,
         loss_ref, dgp_ref, dbg_ref, dgh_ref) = rest[10:24]
        (wa_ref, wb_ref, wo_ref, wat_ref, wbt_ref, wot_ref), w_sem = rest[24:30], rest[30]

        @pl.when(pl.program_id(0) == 0)
        def _():
            loads = [pltpu.make_async_copy(src, dst, w_sem.at[k])
                     for k, (src, dst) in enumerate(zip(w_hbm, rest[24:30]))]
            _start_all(loads, [])
            loss_ref[...] = jnp.zeros_like(loss_ref)
            dgp_ref[...] = jnp.zeros_like(dgp_ref)
            dbg_ref[...] = jnp.zeros_like(dbg_ref)
            dgh_ref[...] = jnp.zeros_like(dgh_ref)
            _wait_all(loads, [])

        ones = ones_ref[...]
        gate_a = ga_ref[...].astype(F32)
        sa = _sigmoid(gate_a)
        silu_a = gate_a * sa
        attn_v = at_ref[...]
        ya_in = attn_v * silu_a
        ov = o_ref[...]
        ro = lax.rsqrt(_sel_right(ov * ov, ones) * (1.0 / 64.0) + EPS)
        ohat = ov * ro
        ghv = gh_ref[...]
        on = ohat * ghv
        gate_b = gb_ref[...].astype(F32)
        sb = _sigmoid(gate_b)
        silu_b = gate_b * sb
        yb_in = on * silu_b
        ya_bf = ya_in.astype(BF16)
        yb_bf = yb_in.astype(BF16)
        yat_ref[...] = ya_bf.T
        ybt_ref[...] = yb_bf.T
        y_a = _dot(ya_bf, wa_ref[...])
        y_b = _dot(yb_bf, wb_ref[...])
        gts = _sigmoid(ml_ref[...].astype(F32) + bg_ref[...])
        g_a = gts[:, :D_MODEL]
        g_b = gts[:, D_MODEL:]
        m_bf = (g_a * y_a + g_b * y_b).astype(BF16)
        mt_ref[...] = m_bf.T
        y = _dot(m_bf, wo_ref[...])
        r1 = lax.rsqrt(jnp.mean(y * y, axis=-1, keepdims=True) + EPS)
        yn = y * r1
        gp = gp_ref[...]
        e = x_ref[...] + yn * gp - t_ref[...]
        loss_ref[...] += jnp.sum(e * e, axis=0, keepdims=True)
        dout = e * (1.0 / D_MODEL)
        dout_ref[...] = dout
        dgp_ref[...] += jnp.sum(dout * yn, axis=0, keepdims=True)
        dyn = dout * gp
        dy = r1 * (dyn - yn * jnp.mean(dyn * yn, axis=-1, keepdims=True))
        dy_bf = dy.astype(BF16)
        dy_ref[...] = dy_bf
        dm = _dot(dy_bf, wot_ref[...])
        dml_a = dm * y_a * g_a * (1.0 - g_a)
        dml_b = dm * y_b * g_b * (1.0 - g_b)
        dpj_ref[:, :D_MODEL] = dml_a.astype(BF16)
        dpj_ref[:, D_MODEL:2 * D_MODEL] = dml_b.astype(BF16)
        dbg_ref[:, :D_MODEL] += jnp.sum(dml_a, axis=0, keepdims=True)
        dbg_ref[:, D_MODEL:] += jnp.sum(dml_b, axis=0, keepdims=True)
        dya_bf = (dm * g_a).astype(BF16)
        dyb_bf = (dm * g_b).astype(BF16)
        dya_ref[...] = dya_bf
        dyb_ref[...] = dyb_bf
        dya_in = _dot(dya_bf, wat_ref[...])
        dyb_in = _dot(dyb_bf, wbt_ref[...])
        dattn = dya_in * silu_a
        delta = _sel_right(dattn * attn_v, ones)
        lane = lax.broadcasted_iota(jnp.int32, (tm, LANE), 1)
        for p in range(HEADS // 2):
            sl = slice(LANE * p, LANE * (p + 1))
            xs = (dattn[:, sl], pltpu.roll(dattn[:, sl], VDIM, 1))
            nds = (-pltpu.roll(delta[:, sl], VDIM, 1), -delta[:, sl])
            for a in range(2):
                hi, lo_part = _hi_lo(nds[a])
                blk = jnp.where(lane < VDIM, xs[a], jnp.where(lane == VDIM, hi, jnp.where(lane == VDIM + 1, lo_part, 0.0)))
                dop_ref[:, LANE * (2 * p + a):LANE * (2 * p + a + 1)] = blk.astype(BF16)
        dpj_ref[:, 2 * D_MODEL:2 * D_MODEL + HG_WIDTH] = (
            dya_in * attn_v * (sa * (1.0 + gate_a * (1.0 - sa)))).astype(BF16)
        don = dyb_in * silu_b
        dpj_ref[:, 2 * D_MODEL + HG_WIDTH:] = (dyb_in * on * (sb * (1.0 + gate_b * (1.0 - sb)))).astype(BF16)
        dgh_ref[...] += jnp.sum(don * ohat, axis=0, keepdims=True)
        dohat = don * ghv
        do_ref[...] = (ro * (dohat - ohat * (_sel_right(dohat * ohat, ones) * (1.0 / 64.0)))).astype(BF16)

    row = lambda w, j: pl.BlockSpec((tm, w), lambda i: (i, j))
    col = lambda w: pl.BlockSpec((w, tm), lambda i: (0, i))
    full = lambda a: pl.BlockSpec(a.shape, lambda i: (0, 0))
    acc = lambda w: pl.BlockSpec((1, w), lambda i: (0, 0))
    sds = lambda w, dt: jax.ShapeDtypeStruct((s, w), dt)
    sdt = lambda w: jax.ShapeDtypeStruct((w, s), BF16)
    return pl.pallas_call(
        body,
        grid=(s // tm,),
        in_specs=[row(1024, 0), row(1024, 0), row(2048, 0), row(512, 4), row(512, 5), row(512, 0), row(512, 0)]
        + [ANY] * 6 + [full(b_gate), full(g_post), full(gh), full(ones64)],
        out_specs=[row(1024, 0), row(3072, 0), row(1024, 0), row(512, 0),
                   col(1024), row(1024, 0), col(512), row(1024, 0), col(512), row(1024, 0),
                   acc(1024), acc(1024), acc(2048), acc(512)],
        out_shape=[sds(1024, F32), sds(D_IN_PAD, BF16), sds(1024, BF16), sds(512, BF16),
                   sdt(1024), sds(1024, BF16), sdt(512), sds(1024, BF16), sdt(512), sds(1024, BF16),
                   jax.ShapeDtypeStruct((1, 1024), F32), jax.ShapeDtypeStruct((1, 1024), F32),
                   jax.ShapeDtypeStruct((1, 2048), F32), jax.ShapeDtypeStruct((1, 512), F32)],
        scratch_shapes=[pltpu.VMEM(a.shape, BF16) for a in weights] + [pltpu.SemaphoreType.DMA((6,))],
        compiler_params=_params(("arbitrary",), 56),
        name="tail",
    )(x, tgt, proj, proj, proj, attn, o, *weights, b_gate, g_post, gh, ones64)


def _mla_bwd(proj, dqr, dkr, dv, g_q, g_kv, w_uq_pt, w_kv_pt, rc, rs1, rs2, dproj):
    s = proj.shape[0]
    tm = 256
    scale = 1.0 / math.sqrt(QK)

    def body(cq_ref, ckv_ref, dqr_ref, dkr_ref, dv_ref, gq_ref, gkv_ref, wuqt_ref, wkvt_ref, c_ref, s1_ref, s2_ref,
             dproj_in, dqf_ref, dkvf_ref, dc_ref, dgq_ref, dgkv_ref):
        del dproj_in

        @pl.when(pl.program_id(0) == 0)
        def _():
            dgq_ref[...] = jnp.zeros_like(dgq_ref)
            dgkv_ref[...] = jnp.zeros_like(dgkv_ref)

        c, s1, s2 = c_ref[...], s1_ref[...], s2_ref[...]
        lane = lax.broadcasted_iota(jnp.int32, (tm, LANE), 1)
        ksum = jnp.zeros((tm, LANE), F32)
        for h in range(HEADS):
            sl = slice(LANE * h, LANE * (h + 1))
            dqf_ref[:, sl] = (_unrope(dqr_ref[:, sl], c, s1, s2) * scale).astype(BF16)
            dkh = dkr_ref[:, sl]
            ksum = ksum + dkh
            dkvf_ref[:, sl] = jnp.where(lane < NOPE, dkh, 0.0).astype(BF16)
            dkvf_ref[:, HEADS * LANE + LANE * h:HEADS * LANE + LANE * (h + 1)] = jnp.where(
                lane < VDIM, dv_ref[:, sl], 0.0).astype(BF16)
        dkpe = _unrope(ksum, c, s1, s2)
        dc_ref[:, Q_LORA + KV_LORA:] = jnp.where((lane >= NOPE) & (lane < QK), dkpe, 0.0).astype(BF16)
        dcqn = _dot(dqf_ref[...], wuqt_ref[...])
        dckvn = _dot(dkvf_ref[...], wkvt_ref[...])
        for x_ref, g_ref, dn, cols, dg_ref in ((cq_ref, gq_ref, dcqn, slice(0, Q_LORA), dgq_ref),
                                               (ckv_ref, gkv_ref, dckvn, slice(Q_LORA, Q_LORA + KV_LORA), dgkv_ref)):
            xv = x_ref[...].astype(F32)
            r = lax.rsqrt(jnp.mean(xv * xv, axis=-1, keepdims=True) + EPS)
            xh = xv * r
            dg_ref[...] += jnp.sum(dn * xh, axis=0, keepdims=True)
            dh = dn * g_ref[...]
            dc_ref[:, cols] = (r * (dh - xh * jnp.mean(dh * xh, axis=-1, keepdims=True))).astype(BF16)

    row = lambda w, j: pl.BlockSpec((tm, w), lambda i: (i, j))
    full = lambda a: pl.BlockSpec(a.shape, lambda i: (0, 0))
    acc = lambda w: pl.BlockSpec((1, w), lambda i: (0, 0))
    sds = lambda w, dt: jax.ShapeDtypeStruct((s, w), dt)
    return pl.pallas_call(
        body,
        grid=(s // tm,),
        in_specs=[row(768, 6), row(256, 21), row(1024, 0), row(1024, 0), row(1024, 0), full(g_q), full(g_kv),
                  full(w_uq_pt), full(w_kv_pt), row(128, 0), row(128, 0), row(128, 0),
                  pl.BlockSpec(memory_space=pl.ANY)],
        out_specs=[row(1024, 0), row(2048, 0), row(1152, 4), acc(768), acc(256)],
        out_shape=[sds(1024, BF16), sds(2048, BF16), jax.ShapeDtypeStruct(dproj.shape, BF16),
                   jax.ShapeDtypeStruct((1, 768), F32), jax.ShapeDtypeStruct((1, 256), F32)],
        input_output_aliases={12: 2},
        compiler_params=_params(("arbitrary",)),
        name="mla_bwd",
    )(proj, proj, dqr, dkr, dv, g_q, g_kv, w_uq_pt, w_kv_pt, rc, rs1, rs2, dproj)


def _pick(n, options):
    for o in options:
        if n % o == 0:
            return o
    raise ValueError(n)


def _matmul(a, b, name):
    m, k = a.shape
    n = b.shape[1]
    tm = _pick(m, (1024, 768, 512, 256))
    tn = _pick(n, (1152, 1024, 768, 512))
    tk = _pick(k, (1024, 512))
    nk = k // tk

    def body(a_ref, b_ref, o_ref):
        @pl.when(pl.program_id(2) == 0)
        def _():
            o_ref[...] = jnp.zeros_like(o_ref)

        o_ref[...] += _dot(a_ref[...], b_ref[...])

    return pl.pallas_call(
        body,
        grid=(m // tm, n // tn, nk),
        in_specs=[pl.BlockSpec((tm, tk), lambda i, j, l: (i, l)), pl.BlockSpec((tk, tn), lambda i, j, l: (l, j))],
        out_specs=pl.BlockSpec((tm, tn), lambda i, j, l: (i, j)),
        out_shape=jax.ShapeDtypeStruct((m, n), F32),
        compiler_params=_params(("arbitrary", "arbitrary", "arbitrary")),
        name=name,
    )(a, b)


def _dh_dx(dproj, w_in_pt, x, dout, g_pre, sends):
    s, k = dproj.shape
    tm = 256
    ns, ni = len(sends), s // tm

    def body(dp_ref, w_ref, x_ref, dout_ref, g_ref, *rest):
        send_refs, (dx_ref, dg_ref) = rest[:ns], rest[ns:ns + 2]
        recv_refs, sems = rest[ns + 2:2 * ns + 2], rest[2 * ns + 2:]

        @pl.when(pl.program_id(0) == 0)
        def _():
            _start_all(*_to_chips_copies(send_refs, recv_refs, sems))
            dg_ref[...] = jnp.zeros_like(dg_ref)

        dh = _dot(dp_ref[...], w_ref[...])
        xv = x_ref[...]
        r = lax.rsqrt(jnp.mean(xv * xv, axis=-1, keepdims=True) + EPS)
        xh = xv * r
        dg_ref[...] += jnp.sum(dh * xh, axis=0, keepdims=True)
        dxh = dh * g_ref[...]
        dx_ref[...] = dout_ref[...] + r * (dxh - xh * jnp.mean(dxh * xh, axis=-1, keepdims=True))

        @pl.when(pl.program_id(0) == ni - 1)
        def _():
            _wait_all(*_to_chips_copies(send_refs, recv_refs, sems))

    row = lambda w: pl.BlockSpec((tm, w), lambda i: (i, 0))
    return pl.pallas_call(
        body,
        grid=(ni,),
        in_specs=[row(k), pl.BlockSpec((k, D_MODEL), lambda i: (0, 0)), row(D_MODEL), row(D_MODEL),
                  pl.BlockSpec((1, D_MODEL), lambda i: (0, 0))] + [ANY] * ns,
        out_specs=[row(D_MODEL), pl.BlockSpec((1, D_MODEL), lambda i: (0, 0))] + [ANY] * ns,
        out_shape=[jax.ShapeDtypeStruct((s, D_MODEL), F32), jax.ShapeDtypeStruct((1, D_MODEL), F32)]
        + [jax.ShapeDtypeStruct(a.shape, a.dtype) for a in sends],
        scratch_shapes=_copy_sems(ns, 3),
        compiler_params=_params(("arbitrary",)),
        name="dh_dx",
    )(dproj, w_in_pt, x, dout, g_pre, *sends)


def _pair_reduce(slots):
    n = len(slots)
    half = [(N_DEV // 2,) + a.shape[1:] for a in slots]

    def body(*refs):
        s_refs, o_refs = refs[:n], refs[n:2 * n]
        mine, got = refs[2 * n:3 * n], refs[3 * n:4 * n]
        send_sems, recv_sems, local_sems = refs[4 * n:]
        x, y, c = _my_place()
        copies, loads = [], []
        for a in range(n):
            for q in range(N_DEV // 2):
                copies.append(pltpu.make_async_remote_copy(
                    src_ref=s_refs[a].at[2 * q + 1 - c], dst_ref=got[a].at[q],
                    send_sem=send_sems.at[4 * a + q], recv_sem=recv_sems.at[4 * a + q],
                    device_id=(x, y, 1 - c), device_id_type=MESH_ID))
                loads.append(pltpu.make_async_copy(s_refs[a].at[2 * q + c], mine[a].at[q], local_sems.at[4 * a + q]))
        _start_all(loads, copies)
        _wait_all(loads, copies)
        for a in range(n):
            o_refs[a][...] = (mine[a][...].astype(F32) + got[a][...].astype(F32)).astype(o_refs[a].dtype)

    vm = lambda: [pltpu.VMEM(h, a.dtype) for h, a in zip(half, slots)]
    return pl.pallas_call(
        body,
        in_specs=[ANY] * n,
        out_shape=[jax.ShapeDtypeStruct(h, a.dtype) for h, a in zip(half, slots)],
        scratch_shapes=vm() + vm() + [pltpu.SemaphoreType.DMA((4 * n,)), pltpu.SemaphoreType.DMA((4 * n,)),
                                      pltpu.SemaphoreType.DMA((4 * n,))],
        compiler_params=pltpu.CompilerParams(vmem_limit_bytes=48 * 2**20),
        name="pair_reduce",
    )(*slots)


def _rope_tables(s):
    inv = (np.float32(ROPE_THETA) ** (-np.arange(0, ROPE, 2, dtype=np.float32) / np.float32(ROPE))).astype(np.float32)
    ang = (np.arange(s, dtype=np.float32)[:, None] * inv[None, :]).astype(np.float32)
    cos, sin = jnp.asarray(np.cos(ang.astype(np.float64)), F32), jnp.asarray(np.sin(ang.astype(np.float64)), F32)
    z = lambda w: jnp.zeros((s, w), F32)
    rc = jnp.concatenate([jnp.ones((s, NOPE), F32), cos, cos, z(32)], axis=1)
    rs1 = jnp.concatenate([z(NOPE), -sin, z(16), z(32)], axis=1)
    rs2 = jnp.concatenate([z(NOPE), z(16), sin, z(32)], axis=1)
    return rc, rs1, rs2


def _step(x, tgt, w_blk, shards, g_pre, b_gate, g_q, g_kv, lbl, g_hgrn, g_post):
    s = x.shape[0]
    rc, rs1, rs2 = _rope_tables(s)
    gh = jnp.tile(g_hgrn, (1, HEADS))

    proj, w_in_pt, ht, *got = _gather_proj(x, g_pre, w_blk, shards[:2])
    w_uq, w_ukv = (_from_slots(n, g) for n, g in zip(MATS[:2], got))
    w_uq_p = jnp.pad(w_uq.reshape(Q_LORA, HEADS, QK), ((0, 0), (0, 0), (0, LANE - QK))).reshape(Q_LORA, HEADS * LANE)
    kv3 = w_ukv.reshape(KV_LORA, HEADS, NOPE + VDIM)
    pad64 = lambda t: jnp.pad(t, ((0, 0), (0, 0), (0, LANE - 64))).reshape(KV_LORA, HEADS * LANE)
    w_kv_p = jnp.concatenate([pad64(kv3[:, :, :NOPE]), pad64(kv3[:, :, NOPE:])], axis=1)

    qr, kr, v, cqt, ckvt = _mla_prep(proj, g_q, g_kv, w_uq_p, w_kv_p, rc, rs1, rs2)
    attn, qa, *got = _attn_fwd(qr, kr, v, shards[2:])
    w_a, w_b, w_out = (_from_slots(n, g) for n, g in zip(MATS[2:], got))
    o, sprev = _hgrn_fwd(proj, lbl)
    (dout, dproj, dop, do, mt, dy_bf, yat, dya_bf, ybt, dyb_bf,
     loss_vec, dg_post, db_gate, dgh) = _tail(x, tgt, proj, attn, o, w_a, w_b, w_out, w_a.T, w_b.T, w_out.T,
                                               b_gate, g_post, gh)
    early = [_to_slots(n, _matmul(a, b, "d" + n)).astype(BF16)
             for n, a, b in (("w_branch_a", yat, dya_bf), ("w_branch_b", ybt, dyb_bf), ("w_out", mt, dy_bf))]
    dqr, dkr, dv, *early_recv = _attn_bwd(qa, kr, v, dop, early)
    dproj, dlbl = _hgrn_bwd(proj, lbl, do, sprev, dproj)
    dqf, dkvf, dproj, dg_q, dg_kv = _mla_bwd(proj, dqr, dkr, dv, g_q, g_kv, w_uq_p.T, w_kv_p.T, rc, rs1, rs2, dproj)

    dw_in_slots = _dw_in_slots(ht, dproj)
    dw_uq_p = _matmul(cqt, dqf, "dw_uq")
    dw_kv_p = _matmul(ckvt, dkvf, "dw_kv")
    dw_uq = dw_uq_p.reshape(Q_LORA, HEADS, LANE)[:, :, :QK].reshape(Q_LORA, HEADS * QK)
    dw_ukv = jnp.concatenate([dw_kv_p[:, :HEADS * LANE].reshape(KV_LORA, HEADS, LANE)[:, :, :NOPE],
                              dw_kv_p[:, HEADS * LANE:].reshape(KV_LORA, HEADS, LANE)[:, :, :VDIM]],
                             axis=2).reshape(KV_LORA, 1024)
    late = _pair_reduce([dw_in_slots, _to_slots("w_uq", dw_uq).astype(BF16), _to_slots("w_ukv", dw_ukv).astype(BF16)])
    dx, dg_pre, *late_recv = _dh_dx(dproj, w_in_pt, x, dout, g_pre, late)

    g_sum = _vectors_sum(dg_pre, db_gate, dg_q, dg_kv, dlbl, dgh, dg_post, loss_vec)
    return dx, late_recv[0], dict(zip(MATS, late_recv[1:] + early_recv)), g_sum


def _adamw(g, w, m, v):
    c1 = 1.0 / (1.0 - ADAM_B1 ** ADAM_STEP)
    c2 = 1.0 / (1.0 - ADAM_B2 ** ADAM_STEP)
    nm = ADAM_B1 * m + (1.0 - ADAM_B1) * g
    nv = ADAM_B2 * v + (1.0 - ADAM_B2) * (g * g)
    d = -ADAM_LR * ((nm * c1) / (jnp.sqrt(nv * c2) + ADAM_EPS) + ADAM_WD * w)
    return d, nm, nv


def _sum8(r_ref):
    g = r_ref[0].astype(F32)
    for k in range(1, r_ref.shape[0]):
        g = g + r_ref[k].astype(F32)
    return g


def _sum_adamw_w_in(recv, w, m, v):
    rows, _, cols = w.shape
    tc = 256
    nc = cols // tc

    def body(r_ref, w_hbm, m_hbm, v_hbm, g_hbm, d_hbm, nm_hbm, nv_hbm, ins, outs, in_sems, out_sems):
        i = pl.program_id(0)
        slot = i & 1
        cols_of = lambda step: pl.ds(pl.multiple_of(step * tc, tc), tc)

        def load(k, step, sl):
            return pltpu.make_async_copy((w_hbm, m_hbm, v_hbm)[k].at[:, 0, cols_of(step)], ins.at[sl, k],
                                         in_sems.at[sl, k])

        def store(k, step, sl):
            return pltpu.make_async_copy(outs.at[sl, k], (g_hbm, d_hbm, nm_hbm, nv_hbm)[k].at[:, 0, cols_of(step)],
                                         out_sems.at[sl, k])

        @pl.when(i == 0)
        def _():
            for k in range(3):
                load(k, 0, 0).start()

        @pl.when(i + 1 < nc)
        def _():
            for k in range(3):
                load(k, i + 1, 1 - slot).start()

        @pl.when(i >= 2)
        def _():
            for k in range(4):
                store(k, i - 2, slot).wait()

        for k in range(3):
            load(k, i, slot).wait()
        g = _sum8(r_ref)
        d, nm, nv = _adamw(g, ins[slot, 0], ins[slot, 1], ins[slot, 2])
        for k, val in enumerate((g, d, nm, nv)):
            outs[slot, k] = val
        for k in range(4):
            store(k, i, slot).start()

        @pl.when(i == nc - 1)
        def _():
            for k in range(4):
                store(k, i, slot).wait()
            if nc >= 2:
                for k in range(4):
                    store(k, i - 1, 1 - slot).wait()

    out = jax.ShapeDtypeStruct((rows, 1, cols), F32)
    return pl.pallas_call(
        body,
        grid=(nc,),
        in_specs=[pl.BlockSpec((recv.shape[0], rows, tc), lambda i: (0, 0, i)), ANY, ANY, ANY],
        out_specs=[ANY, ANY, ANY, ANY],
        out_shape=[out, out, out, out],
        scratch_shapes=[pltpu.VMEM((2, 3, rows, tc), F32), pltpu.VMEM((2, 4, rows, tc), F32),
                        pltpu.SemaphoreType.DMA((2, 3)), pltpu.SemaphoreType.DMA((2, 4))],
        compiler_params=_params(("arbitrary",)),
        name="sum_adamw_w_in",
    )(recv, w, m, v)


def _sum_adamw_whole(recvs, ws, ms, vs):
    n = len(ws)

    def body(*refs):
        r_refs, w_refs, m_refs, v_refs = refs[:n], refs[n:2 * n], refs[2 * n:3 * n], refs[3 * n:4 * n]
        outs = refs[4 * n:]
        for a in range(n):
            g = _sum8(r_refs[a])
            d, nm, nv = _adamw(g, w_refs[a][...], m_refs[a][...], v_refs[a][...])
            outs[a][...] = g
            outs[n + a][...] = d
            outs[2 * n + a][...] = nm
            outs[3 * n + a][...] = nv

    shapes = [jax.ShapeDtypeStruct(w.shape, F32) for w in ws]
    res = pl.pallas_call(
        body,
        out_shape=shapes * 4,
        compiler_params=pltpu.CompilerParams(vmem_limit_bytes=48 * 2**20),
        name="sum_adamw_mats",
    )(*recvs, *ws, *ms, *vs)
    return res[:n], res[n:2 * n], res[2 * n:3 * n], res[3 * n:]


SMALL = ("g_pre", "b_gate", "g_q", "g_kv", "lb_logits", "g_hgrn", "g_post")
SMALL_SHAPE = dict(g_pre=(1, 1024), b_gate=(1, 2048), g_q=(1, 768), g_kv=(1, 256), lb_logits=(2, 512),
                   g_hgrn=(1, 64), g_post=(1, 1024))


def _vectors_sum(dg_pre, db_gate, dg_q, dg_kv, dlbl, dgh, dg_post, loss_vec):
    def body(gpre_ref, bg_ref, gq_ref, gkv_ref, lbl_ref, gh_ref, gpost_ref, loss_ref, out_ref, mine, got,
             send_sems, recv_sems):
        mine[...] = jnp.zeros_like(mine)
        mine[0:1, :] = gpre_ref[...]
        mine[1:2, :] = bg_ref[:, :1024]
        mine[2:3, :] = bg_ref[:, 1024:]
        mine[3:4, :Q_LORA] = gq_ref[...]
        mine[4:5, :KV_LORA] = gkv_ref[...]
        loss = (0.5 / D_MODEL) * jnp.sum(loss_ref[...], axis=-1, keepdims=True)
        mine[4:5, KV_LORA:] = jnp.broadcast_to(loss, (1, 1024 - KV_LORA))
        mine[5:6, :HG_WIDTH] = lbl_ref[0:1, :]
        mine[5:6, HG_WIDTH:] = lbl_ref[1:2, :]
        gh = gh_ref[...]
        fold = gh[:, :VDIM]
        for h in range(1, HEADS):
            fold = fold + gh[:, VDIM * h:VDIM * (h + 1)]
        mine[6:7, :VDIM] = fold
        mine[7:8, :] = gpost_ref[...]
        x, y, c = _my_place()
        me = 4 * x + 2 * y + c
        got[me] = mine[...]
        copies = [pltpu.make_async_remote_copy(
            src_ref=mine, dst_ref=got.at[me], send_sem=send_sems.at[k], recv_sem=recv_sems.at[k],
            device_id=_flip(k, x, y, c), device_id_type=MESH_ID) for k in range(N_DEV - 1)]
        _start_all([], copies)
        _wait_all([], copies)
        out_ref[...] = _sum8(got)

    return pl.pallas_call(
        body,
        out_shape=jax.ShapeDtypeStruct((8, 1024), F32),
        scratch_shapes=[pltpu.VMEM((8, 1024), F32), pltpu.VMEM((N_DEV, 8, 1024), F32),
                        pltpu.SemaphoreType.DMA((7,)), pltpu.SemaphoreType.DMA((7,))],
        name="vectors_sum",
    )(dg_pre, db_gate, dg_q, dg_kv, dlbl, dgh, dg_post, loss_vec)


def _vectors_adamw(g_sum, ws, ms, vs):
    n = len(SMALL)

    def body(g_ref, *refs):
        w_refs, m_refs, v_refs = refs[:n], refs[n:2 * n], refs[2 * n:3 * n]
        loss_ref, outs = refs[3 * n], refs[3 * n + 1:]
        g = g_ref[...]
        loss_ref[...] = g[4:5, KV_LORA:KV_LORA + 1]
        grads = (g[0:1, :], jnp.concatenate([g[1:2, :], g[2:3, :]], axis=1), g[3:4, :Q_LORA], g[4:5, :KV_LORA],
                 jnp.concatenate([g[5:6, :HG_WIDTH], g[5:6, HG_WIDTH:]], axis=0), g[6:7, :VDIM], g[7:8, :])
        for a in range(n):
            d, nm, nv = _adamw(grads[a], w_refs[a][...], m_refs[a][...], v_refs[a][...])
            outs[a][...] = grads[a]
            outs[n + a][...] = d
            outs[2 * n + a][...] = nm
            outs[3 * n + a][...] = nv

    shapes = [jax.ShapeDtypeStruct(SMALL_SHAPE[k], F32) for k in SMALL]
    res = pl.pallas_call(
        body,
        out_shape=[jax.ShapeDtypeStruct((1, 1), F32)] + shapes * 4,
        name="vectors_adamw",
    )(g_sum, *ws, *ms, *vs)
    return res[0], res[1:n + 1], res[n + 1:2 * n + 1], res[2 * n + 1:3 * n + 1], res[3 * n + 1:]


MATS = ("w_uq", "w_ukv", "w_branch_a", "w_branch_b", "w_out")
COL_SHARDED = dict(w_uq=False, w_ukv=True, w_branch_a=True, w_branch_b=True, w_out=False)
ORDER = ("g_pre", "w_in", "b_gate", "g_q", "w_uq", "g_kv", "w_ukv", "lb_logits", "g_hgrn",
         "w_branch_a", "w_branch_b", "w_out", "g_post")


def _to_slots(name, full):
    r, c = full.shape
    if COL_SHARDED[name]:
        return full.reshape(r, N_DEV, c // N_DEV).transpose(1, 0, 2)
    return full.reshape(N_DEV, r // N_DEV, c)


def _from_slots(name, slots):
    _, r, c = slots.shape
    if COL_SHARDED[name]:
        return slots.transpose(1, 0, 2).reshape(r, N_DEV * c)
    return slots.reshape(N_DEV * r, c)


def kernel(x, g_pre, w_in, b_gate, g_q, w_uq, g_kv, w_ukv, lb_logits, g_hgrn, w_branch_a, w_branch_b, w_out, g_post, loss_target, m_g_pre, m_w_in, m_b_gate, m_g_q, m_w_uq, m_g_kv, m_w_ukv, m_lb_logits, m_g_hgrn, m_w_branch_a, m_w_branch_b, m_w_out, m_g_post, v_g_pre, v_w_in, v_b_gate, v_g_q, v_w_uq, v_g_kv, v_w_ukv, v_lb_logits, v_g_hgrn, v_w_branch_a, v_w_branch_b, v_w_out, v_g_post):
    rows3 = lambda a: jnp.transpose(a, (2, 0, 1))
    w = dict(w_in=rows3(w_in), w_uq=w_uq[0], w_ukv=w_ukv[0], w_branch_a=w_branch_a[0], w_branch_b=w_branch_b[0],
             w_out=w_out[0], g_pre=g_pre, b_gate=b_gate, g_q=g_q, g_kv=g_kv, lb_logits=lb_logits, g_hgrn=g_hgrn,
             g_post=g_post)
    mom = dict(w_in=rows3(m_w_in), w_uq=m_w_uq[0], w_ukv=m_w_ukv[0], w_branch_a=m_w_branch_a[0],
               w_branch_b=m_w_branch_b[0], w_out=m_w_out[0], g_pre=m_g_pre, b_gate=m_b_gate, g_q=m_g_q, g_kv=m_g_kv,
               lb_logits=m_lb_logits, g_hgrn=m_g_hgrn, g_post=m_g_post)
    var = dict(w_in=rows3(v_w_in), w_uq=v_w_uq[0], w_ukv=v_w_ukv[0], w_branch_a=v_w_branch_a[0],
               w_branch_b=v_w_branch_b[0], w_out=v_w_out[0], g_pre=v_g_pre, b_gate=v_b_gate, g_q=v_g_q, g_kv=v_g_kv,
               lb_logits=v_lb_logits, g_hgrn=v_g_hgrn, g_post=v_g_post)

    w_blk = w["w_in"].reshape(W_IN_SHARD, D_MODEL).astype(BF16)
    dx, recv_in, recv, g_sum = _step(x[0], loss_target[0], w_blk, [w[n].astype(BF16) for n in MATS],
                                     g_pre, b_gate, g_q, g_kv, lb_logits, g_hgrn, g_post)

    g_in, d_in, m_in, v_in = _sum_adamw_w_in(recv_in, w["w_in"], mom["w_in"], var["w_in"])
    res = _sum_adamw_whole([recv[n] for n in MATS], *([t[n] for n in MATS] for t in (w, mom, var)))
    total, *vec = _vectors_adamw(g_sum, *([t[n] for n in SMALL] for t in (w, mom, var)))

    outs = []
    for mats, vecs, big in zip(res, vec, (g_in, d_in, m_in, v_in)):
        t = {**{n: a[None] for n, a in zip(MATS, mats)}, **dict(zip(SMALL, vecs)),
             "w_in": jnp.transpose(big, (1, 2, 0))}
        outs += [t[n] for n in ORDER]
    return (total.reshape(()), dx[None], *outs)
```

```python
import math

import jax
import jax.numpy as jnp
import numpy as np
from jax import lax
from jax.experimental import pallas as pl
from jax.experimental.pallas import tpu as pltpu

F32, BF16 = jnp.float32, jnp.bfloat16

D_MODEL = 1024
EPS = 1e-6
HEADS = 8
NOPE, ROPE, VDIM = 64, 32, 64
QK = NOPE + ROPE
Q_LORA, KV_LORA = 768, 256
ROPE_THETA = 10000.0
ATT_CHUNK_SHIFT = 6
HG_BLOCK = 32
HG_WIDTH = 512
D_IN = 5664
D_IN_PAD = 5760
W_IN_SHARD = D_IN // 8
N_DEV = 8
LANE = 128

ADAM_LR, ADAM_B1, ADAM_B2, ADAM_EPS, ADAM_WD, ADAM_STEP = 0.001, 0.9, 0.999, 1e-08, 0.01, 10

W_IN_SEGMENTS = ((3616, 5664, 0), (1056, 1568, 2048), (3104, 3616, 2560), (1568, 3104, 3072),
                 (0, 1024, 4608), (1024, 1056, 5696))

NT = (((1,), (1,)), ((), ()))
TN = (((0,), (0,)), ((), ()))
MESH_ID = pl.DeviceIdType.MESH


def _w_in_pieces():
    out = []
    for lo, hi, dst in W_IN_SEGMENTS:
        c = lo
        while c < hi:
            p = c // W_IN_SHARD
            e = min(hi, (p + 1) * W_IN_SHARD)
            out.append((p, c - p * W_IN_SHARD, e - p * W_IN_SHARD, dst + c - lo))
            c = e
    return out


def _params(sem, vmem_mb=48):
    return pltpu.CompilerParams(dimension_semantics=sem, vmem_limit_bytes=vmem_mb * 2**20)


def _dot(a, b):
    return jnp.dot(a, b, preferred_element_type=F32)


def _dotg(a, b, dims):
    return lax.dot_general(a, b, dims, preferred_element_type=F32)


def _split2(x):
    hi = x.astype(BF16)
    return hi, (x - hi.astype(F32)).astype(BF16)


def _sel_left(m01, x):
    hi, lo = _split2(x)
    return _dot(m01, hi) + _dot(m01, lo)


def _sel_right(x, m01):
    hi, lo = _split2(x)
    return _dot(hi, m01) + _dot(lo, m01)


def _hi_lo(x):
    hi = x.astype(BF16).astype(F32)
    return hi, x - hi


def _sigmoid(x):
    return 0.5 * jnp.tanh(0.5 * x) + 0.5


def _rope(x, c, s1, s2):
    return x * c + pltpu.roll(x, 112, 1) * s1 + pltpu.roll(x, 16, 1) * s2


def _unrope(d, c, s1, s2):
    return d * c + pltpu.roll(d * s1, 16, 1) + pltpu.roll(d * s2, 112, 1)


def _my_place():
    return lax.axis_index("x"), lax.axis_index("y"), lax.axis_index("c")


def _flip(k, x, y, c):
    fx, fy, fc = (k + 1) >> 2 & 1, (k + 1) >> 1 & 1, (k + 1) & 1
    return (1 - x if fx else x), (1 - y if fy else y), (1 - c if fc else c)


def _to_all_copies(s_refs, r_refs, sems, spread):
    send_sems, recv_sems, local_sems = sems
    x, y, c = _my_place()
    me = 4 * x + 2 * y + c
    src = (lambda a, p: s_refs[a]) if spread else (lambda a, p: s_refs[a].at[p])
    local = [pltpu.make_async_copy(src(a, me), r_refs[a].at[me], local_sems.at[a]) for a in range(len(s_refs))]
    remote = []
    for k in range(N_DEV - 1):
        px, py, pc = _flip(k, x, y, c)
        for a in range(len(s_refs)):
            remote.append(pltpu.make_async_remote_copy(
                src_ref=src(a, 4 * px + 2 * py + pc), dst_ref=r_refs[a].at[me],
                send_sem=send_sems.at[7 * a + k], recv_sem=recv_sems.at[7 * a + k],
                device_id=(px, py, pc), device_id_type=MESH_ID))
    return local, remote


def _to_chips_copies(s_refs, r_refs, sems):
    send_sems, recv_sems, local_sems = sems
    x, y, c = _my_place()
    me = 2 * x + y
    local = [pltpu.make_async_copy(s_refs[a].at[me], r_refs[a].at[me], local_sems.at[a]) for a in range(len(s_refs))]
    remote = []
    for k in range(3):
        px = 1 - x if (k + 1) >> 1 & 1 else x
        py = 1 - y if (k + 1) & 1 else y
        for a in range(len(s_refs)):
            remote.append(pltpu.make_async_remote_copy(
                src_ref=s_refs[a].at[2 * px + py], dst_ref=r_refs[a].at[me],
                send_sem=send_sems.at[3 * a + k], recv_sem=recv_sems.at[3 * a + k],
                device_id=(px, py, c), device_id_type=MESH_ID))
    return local, remote


def _start_all(local, remote):
    for cp in local + remote:
        cp.start()


def _wait_all(local, remote):
    for cp in remote:
        cp.wait_recv()
    for cp in remote:
        cp.wait_send()
    for cp in local:
        cp.wait()


def _copy_sems(n, peers):
    return [pltpu.SemaphoreType.DMA((peers * n,)), pltpu.SemaphoreType.DMA((peers * n,)),
            pltpu.SemaphoreType.DMA((n,))]


ANY = pl.BlockSpec(memory_space=pl.ANY)


def _dw_in_slots(ht, dproj):
    m, k = ht.shape
    n = dproj.shape[1]
    tn, tk = 1152, 1024
    nj, nk = n // tn, k // tk
    by_tile = [[] for _ in range(nj)]
    for p, lo, hi, dst in _w_in_pieces():
        while lo < hi:
            j = dst // tn
            cnt = min(hi - lo, (j + 1) * tn - dst)
            by_tile[j].append((p, lo, lo + cnt, dst - j * tn))
            lo, dst = lo + cnt, dst + cnt

    def body(a_ref, b_ref, s_ref, acc_ref):
        j, l = pl.program_id(0), pl.program_id(1)

        @pl.when(l == 0)
        def _():
            acc_ref[...] = jnp.zeros_like(acc_ref)

        acc_ref[...] += _dot(a_ref[...], b_ref[...])

        @pl.when(l == nk - 1)
        def _():
            at = acc_ref[...].T
            for jj in range(nj):
                @pl.when(j == jj)
                def _(jj=jj):
                    for p, lo, hi, d in by_tile[jj]:
                        s_ref[p, lo:hi, :] = at[d:d + hi - lo, :].astype(BF16)

    return pl.pallas_call(
        body,
        grid=(nj, nk),
        in_specs=[pl.BlockSpec((m, tk), lambda j, l: (0, l)), pl.BlockSpec((tk, tn), lambda j, l: (l, j))],
        out_specs=pl.BlockSpec((N_DEV, W_IN_SHARD, m), lambda j, l: (0, 0, 0)),
        out_shape=jax.ShapeDtypeStruct((N_DEV, W_IN_SHARD, m), BF16),
        scratch_shapes=[pltpu.VMEM((m, tn), F32)],
        compiler_params=_params(("arbitrary", "arbitrary")),
        name="dw_in",
    )(ht, dproj)


PROJ_DT = F32
GP_TN = 256
GP_COLS = 5888
GP_NT = GP_COLS // GP_TN


def _gp_tile_pieces():
    tiles = [[] for _ in range(GP_NT)]
    for p, lo, hi, dst in _w_in_pieces():
        while lo < hi:
            t = dst // GP_TN
            n = min(hi - lo, (t + 1) * GP_TN - dst)
            tiles[t].append((p, lo, lo + n, dst - t * GP_TN))
            lo, dst = lo + n, dst + n
    return tiles


def _gp_tables():
    pieces = _gp_tile_pieces()
    rank_of = {None: 0, 0: 1, 1: 2, 2: 2, 4: 3, 5: 3, 3: 4, 6: 5}
    order = np.zeros((N_DEV, GP_NT), np.int32)
    waits = np.zeros((N_DEV, GP_NT), np.int32)
    for me in range(N_DEV):
        x, y, c = me >> 2 & 1, me >> 1 & 1, me & 1
        chips = [(1 - x, y), (x, 1 - y), (1 - x, 1 - y)]

        def sem_of(p):
            px, py, pc = p >> 2 & 1, p >> 1 & 1, p & 1
            if (px, py) == (x, y):
                return None if pc == c else 0
            j = chips.index((px, py))
            return 1 + j if pc == c else 4 + j

        needs = [sorted({sem_of(p) for p, _, _, _ in tile} - {None}) for tile in pieces]
        ranks = [max([rank_of[k] for k in ks], default=0) for ks in needs]
        seq = sorted(range(GP_NT), key=lambda t: (ranks[t], t))
        seen = set()
        for step, t in enumerate(seq):
            order[me, step] = t
            new = [k for k in needs[t] if k not in seen]
            for k in new:
                waits[me, step] |= 1 << k
            seen.update(new)
        assert seen == set(range(7)), (me, seen)
    return order, waits


def _gather_proj(x, g_pre, w_blk, shards):
    s = x.shape[0]
    tx = 512
    ns = len(shards)
    tile_pieces = _gp_tile_pieces()
    order_np, waits_np = _gp_tables()
    xq, yq, cq = _my_place()
    me_out = 4 * xq + 2 * yq + cq
    order = lax.dynamic_index_in_dim(jnp.asarray(order_np), me_out, 0, keepdims=False)
    waits = lax.dynamic_index_in_dim(jnp.asarray(waits_np), me_out, 0, keepdims=False)

    def body(order_ref, waits_ref, x_hbm, g_ref, wblk_hbm, *rest):
        shard_refs, (proj_ref, wt_ref, ht_hbm), got_refs = rest[:ns], rest[ns:ns + 3], rest[ns + 3:2 * ns + 3]
        recv, h_ref, wtile, xbuf, htbuf = rest[2 * ns + 3:2 * ns + 8]
        send_sems, recv_sems, misc_sems = rest[2 * ns + 8:2 * ns + 11]
        sems = rest[2 * ns + 11:]
        t = pl.program_id(0)
        x_, y_, c = _my_place()
        sibling = (x_, y_, 1 - c)
        chips = [(1 - x_, y_), (x_, 1 - y_), (1 - x_, 1 - y_)]
        idx = lambda px, py, pc: 4 * px + 2 * py + pc
        me = idx(x_, y_, c)

        def copy(k, slot, to, src=None):
            return pltpu.make_async_remote_copy(
                src_ref=recv.at[slot] if src is None else src, dst_ref=recv.at[slot],
                send_sem=send_sems.at[k], recv_sem=recv_sems.at[k], device_id=to, device_id_type=MESH_ID)

        mine = pltpu.make_async_copy(wblk_hbm, recv.at[me], misc_sems.at[0])
        first = [copy(0, me, sibling, src=wblk_hbm)] + [copy(1 + j, me, (*chips[j], c), src=wblk_hbm) for j in range(2)]
        passed = [copy(4 + j, idx(*ch, c), sibling) for j, ch in enumerate(chips)]
        onward = [copy(3, idx(*chips[0], c), (*chips[1], c)), copy(3, idx(*chips[1], c), (*chips[0], c))]
        arrivals = ([copy(0, idx(x_, y_, 1 - c), sibling)] + [copy(1 + j, idx(*ch, c), sibling) for j, ch in enumerate(chips)]
                    + [copy(4 + j, idx(*ch, 1 - c), sibling) for j, ch in enumerate(chips)])

        @pl.when(t == 0)
        def _():
            mine.start()
            for cp in first:
                cp.start()
            _start_all(*_to_all_copies(shard_refs, got_refs, sems, True))

            def load(i):
                return pltpu.make_async_copy(x_hbm.at[pl.ds(i * tx, tx), :], xbuf.at[i & 1], misc_sems.at[1 + (i & 1)])

            def store(i):
                return pltpu.make_async_copy(htbuf.at[i & 1], ht_hbm.at[:, pl.ds(i * tx, tx)], misc_sems.at[3 + (i & 1)])

            load(0).start()
            for i in range(s // tx):
                if i + 1 < s // tx:
                    load(i + 1).start()
                load(i).wait()
                xv = xbuf[i & 1]
                r = lax.rsqrt(jnp.mean(xv * xv, axis=-1, keepdims=True) + EPS)
                h = (xv * r * g_ref[...]).astype(BF16)
                h_ref[i * tx:(i + 1) * tx, :] = h
                if i >= 2:
                    store(i - 2).wait()
                htbuf[i & 1] = h.T
                store(i).start()
            for i in range(max(s // tx - 2, 0), s // tx):
                store(i).wait()
            mine.wait()

        w = waits_ref[t]
        for k in range(7):
            @pl.when((w >> k) & 1 == 1)
            def _(k=k):
                arrivals[k].wait_recv()
                if 1 <= k <= 3:
                    passed[k - 1].start()
                if 1 <= k <= 2:
                    @pl.when(c == k - 1)
                    def _():
                        onward[k - 1].start()

        tile = order_ref[t]
        for tt in range(GP_NT):
            @pl.when(tile == tt)
            def _(tt=tt):
                covered = sorted((d, d + hi - lo) for _, lo, hi, d in tile_pieces[tt])
                at = 0
                for lo_z, hi_z in covered + [(GP_TN, GP_TN)]:
                    if lo_z > at:
                        wtile[at:lo_z, :] = jnp.zeros((lo_z - at, D_MODEL), BF16)
                    at = max(at, hi_z)
                for p, lo, hi, d in tile_pieces[tt]:
                    wtile[d:d + hi - lo, :] = recv[p, lo:hi, :]

        wt = wtile[...]
        wt_ref[...] = wt
        proj_ref[...] = _dotg(h_ref[...], wt, NT).astype(PROJ_DT)

        @pl.when(t == GP_NT - 1)
        def _():
            for cp in first + passed + onward[:1]:
                cp.wait_send()
            _wait_all(*_to_all_copies(shard_refs, got_refs, sems, True))

    grid_spec = pltpu.PrefetchScalarGridSpec(
        num_scalar_prefetch=2,
        grid=(GP_NT,),
        in_specs=[ANY, pl.BlockSpec((1, D_MODEL), lambda t, o, w: (0, 0)), ANY] + [ANY] * ns,
        out_specs=[pl.BlockSpec((s, GP_TN), lambda t, o, w: (0, o[t])),
                   pl.BlockSpec((GP_TN, D_MODEL), lambda t, o, w: (o[t], 0)), ANY] + [ANY] * ns,
        scratch_shapes=[pltpu.VMEM((N_DEV, W_IN_SHARD, D_MODEL), BF16), pltpu.VMEM((s, D_MODEL), BF16),
                        pltpu.VMEM((GP_TN, D_MODEL), BF16), pltpu.VMEM((2, tx, D_MODEL), F32),
                        pltpu.VMEM((2, D_MODEL, tx), BF16),
                        pltpu.SemaphoreType.DMA((7,)), pltpu.SemaphoreType.DMA((7,)), pltpu.SemaphoreType.DMA((5,))]
        + _copy_sems(ns, 7),
    )
    return pl.pallas_call(
        body,
        grid_spec=grid_spec,
        out_shape=[jax.ShapeDtypeStruct((s, GP_COLS), PROJ_DT), jax.ShapeDtypeStruct((GP_COLS, D_MODEL), BF16),
                   jax.ShapeDtypeStruct((D_MODEL, s), BF16)]
        + [jax.ShapeDtypeStruct((N_DEV,) + b.shape, b.dtype) for b in shards],
        compiler_params=_params(("arbitrary",), 56),
        name="gather_proj",
    )(order, waits, x, g_pre, w_blk, *shards)


def _mla_prep(proj, g_q, g_kv, w_uq_p, w_kv_p, rc, rs1, rs2):
    s = proj.shape[0]
    tm = 256
    scale = 1.0 / math.sqrt(QK)

    def body(cq_ref, ckv_ref, kpe_ref, gq_ref, gkv_ref, wuq_ref, wkv_ref, c_ref, s1_ref, s2_ref,
             qr_ref, kr_ref, v_ref, cqt_ref, ckvt_ref):
        cq = cq_ref[...].astype(F32)
        r = lax.rsqrt(jnp.mean(cq * cq, axis=-1, keepdims=True) + EPS)
        cqn = (cq * r * gq_ref[...]).astype(BF16)
        cqt_ref[...] = cqn.T
        q = _dot(cqn, wuq_ref[...])
        ckv = ckv_ref[...].astype(F32)
        r = lax.rsqrt(jnp.mean(ckv * ckv, axis=-1, keepdims=True) + EPS)
        ckvn = (ckv * r * gkv_ref[...]).astype(BF16)
        ckvt_ref[...] = ckvn.T
        kv = _dot(ckvn, wkv_ref[...])
        c, s1, s2 = c_ref[...], s1_ref[...], s2_ref[...]
        lane = lax.broadcasted_iota(jnp.int32, (tm, LANE), 1)
        kpe = _rope(kpe_ref[...].astype(F32), c, s1, s2) + jnp.where((lane == QK) | (lane == QK + 1), 1.0, 0.0)
        vone = jnp.where((lane == VDIM) | (lane == VDIM + 1), 1.0, 0.0)
        for h in range(HEADS):
            sl = slice(LANE * h, LANE * (h + 1))
            qr_ref[:, sl] = (_rope(q[:, sl], c, s1, s2) * scale).astype(BF16)
            kr_ref[:, sl] = (kv[:, sl] + kpe).astype(BF16)
            v_ref[:, sl] = (kv[:, HEADS * LANE + LANE * h:HEADS * LANE + LANE * (h + 1)] + vone).astype(BF16)

    row = lambda w, j: pl.BlockSpec((tm, w), lambda i: (i, j))
    col = lambda w: pl.BlockSpec((w, tm), lambda i: (0, i))
    full = lambda a: pl.BlockSpec(a.shape, lambda i: (0, 0))
    return pl.pallas_call(
        body,
        grid=(s // tm,),
        in_specs=[row(768, 6), row(256, 21), row(128, 44), full(g_q), full(g_kv), full(w_uq_p), full(w_kv_p),
                  row(128, 0), row(128, 0), row(128, 0)],
        out_specs=[row(1024, 0), row(1024, 0), row(1024, 0), col(768), col(256)],
        out_shape=[jax.ShapeDtypeStruct((s, 1024), BF16), jax.ShapeDtypeStruct((s, 1024), BF16),
                   jax.ShapeDtypeStruct((s, 1024), BF16), jax.ShapeDtypeStruct((768, s), BF16),
                   jax.ShapeDtypeStruct((256, s), BF16)],
        compiler_params=_params(("arbitrary",)),
        name="mla_prep",
    )(proj, proj, proj, g_q, g_kv, w_uq_p, w_kv_p, rc, rs1, rs2)


ATT_T = 512
ATT_FWD_HEADS = 8


def _chunk_mask(transposed):
    r = lax.broadcasted_iota(jnp.int32, (ATT_T, ATT_T), 0) >> ATT_CHUNK_SHIFT
    c = lax.broadcasted_iota(jnp.int32, (ATT_T, ATT_T), 1) >> ATT_CHUNK_SHIFT
    return (r <= c) if transposed else (c <= r)


def _attn_fwd(qr, kr, vp, shards):
    s = qr.shape[0]
    t = ATT_T
    g = ATT_FWD_HEADS
    ns = len(shards)

    def body(q_ref, k_ref, v_ref, *rest):
        shard_refs, (o_ref, qa_ref), got_refs = rest[:ns], rest[ns:ns + 2], rest[ns + 2:2 * ns + 2]
        sc_ref, sems = rest[2 * ns + 2], rest[2 * ns + 3:]
        qi = pl.program_id(1)

        @pl.when((pl.program_id(0) == 0) & (qi == 0))
        def _():
            _start_all(*_to_all_copies(shard_refs, got_refs, sems, True))
        lane = lax.broadcasted_iota(jnp.int32, (t, LANE), 1)
        sls = [slice(LANE * a, LANE * (a + 1)) for a in range(g)]
        qs = [q_ref[:, sl] for sl in sls]

        def scores(j):
            rows = pl.ds(pl.multiple_of(j * t, t), t)
            for a in range(g):
                sc_ref[j & 1, a] = _dotg(qs[a], k_ref[rows, sls[a]], NT)

        def step(j, carry, masked):
            rows = pl.ds(pl.multiple_of(j * t, t), t)
            out = []
            for a in range(g):
                m, acc = carry[a]
                sc = sc_ref[j & 1, a]
                if masked:
                    sc = jnp.where(_chunk_mask(False), sc, -1e30)
                m_new = jnp.maximum(m, jnp.max(sc, axis=-1, keepdims=True))
                p = jnp.exp(sc - m_new).astype(BF16)
                acc = jnp.exp(m - m_new) * acc + _dot(p, v_ref[rows, sls[a]])
                out.append((m_new, acc))
            return tuple(out)

        def loop(j, carry):
            carry = step(j, carry, False)
            scores(j + 1)
            return carry

        init = tuple((jnp.full((t, 1), -1e30, F32), jnp.zeros((t, LANE), F32)) for _ in range(g))
        scores(0)
        carry = lax.fori_loop(0, qi, loop, init)
        carry = step(qi, carry, True)
        outs = []
        for a in range(g):
            m, acc = carry[a]
            l = acc[:, VDIM:VDIM + 1]
            outs.append(acc / l)
            hi, lo_part = _hi_lo(-(m + jnp.log(l)))
            qa = jnp.where(lane == QK, hi, jnp.where(lane == QK + 1, lo_part, qs[a].astype(F32)))
            qa_ref[:, sls[a]] = qa.astype(BF16)
        for p in range(g // 2):
            o_ref[:, LANE * p:LANE * (p + 1)] = jnp.where(lane < VDIM, outs[2 * p], pltpu.roll(outs[2 * p + 1], VDIM, 1))

        @pl.when((pl.program_id(0) == HEADS // g - 1) & (qi == s // t - 1))
        def _():
            _wait_all(*_to_all_copies(shard_refs, got_refs, sems, True))

    return pl.pallas_call(
        body,
        grid=(HEADS // g, s // t),
        in_specs=[
            pl.BlockSpec((t, g * LANE), lambda h, i: (i, h)),
            pl.BlockSpec((s, g * LANE), lambda h, i: (0, h), pipeline_mode=pl.Buffered(1)),
            pl.BlockSpec((s, g * LANE), lambda h, i: (0, h), pipeline_mode=pl.Buffered(1)),
        ] + [ANY] * ns,
        out_specs=[
            pl.BlockSpec((t, g * VDIM), lambda h, i: (i, h)),
            pl.BlockSpec((t, g * LANE), lambda h, i: (i, h)),
        ] + [ANY] * ns,
        out_shape=[jax.ShapeDtypeStruct((s, 512), F32), jax.ShapeDtypeStruct((s, 1024), BF16)]
        + [jax.ShapeDtypeStruct((N_DEV,) + b.shape, b.dtype) for b in shards],
        scratch_shapes=[pltpu.VMEM((2, g, t, t), F32)] + _copy_sems(ns, 7),
        compiler_params=_params(("arbitrary", "arbitrary"), 56),
        name="attn_fwd",
    )(qr, kr, vp, *shards)


def _attn_bwd(qa, kr, vp, dop, sends):
    s = qa.shape[0]
    t = ATT_T
    nq = s // t
    ns = len(sends)

    def body(q_ref, k_ref, v_ref, do_ref, *rest):
        send_refs, (dq_out, dk_out, dv_out) = rest[:ns], rest[ns:ns + 3]
        recv_refs = rest[ns + 3:2 * ns + 3]
        (dq_ref, dk_ref, dv_ref), sems = rest[2 * ns + 3:2 * ns + 6], rest[2 * ns + 6:]
        j = pl.program_id(1)
        sls = [slice(LANE * a, LANE * (a + 1)) for a in range(2)]

        @pl.when((pl.program_id(0) == 0) & (j == 0))
        def _():
            _start_all(*_to_all_copies(send_refs, recv_refs, sems, False))

        @pl.when(j == 0)
        def _():
            dq_ref[...] = jnp.zeros_like(dq_ref)

        dk_ref[...] = jnp.zeros_like(dk_ref)
        dv_ref[...] = jnp.zeros_like(dv_ref)
        ks = [k_ref[:, sl] for sl in sls]
        vs = [v_ref[:, sl] for sl in sls]

        def part(i, k_lo, k_n, q_lo, q_n, masked):
            rows = pl.ds(pl.multiple_of(i * t + q_lo, 256), q_n)
            keys = slice(k_lo, k_lo + k_n)
            for a in range(2):
                q = q_ref[rows, sls[a]]
                do = do_ref[rows, sls[a]]
                sc = _dotg(ks[a][keys], q, NT)
                if masked:
                    kc = lax.broadcasted_iota(jnp.int32, (k_n, q_n), 0) >> ATT_CHUNK_SHIFT
                    qc = lax.broadcasted_iota(jnp.int32, (k_n, q_n), 1) >> ATT_CHUNK_SHIFT
                    sc = jnp.where(kc <= qc, sc, -1e30)
                p = jnp.exp(sc)
                ds = (p * _dotg(vs[a][keys], do, NT)).astype(BF16)
                dv_ref[keys, sls[a]] += _dot(p.astype(BF16), do)
                dk_ref[keys, sls[a]] += _dot(ds, q)
                dq_ref[rows, sls[a]] += _dotg(ds, ks[a][keys], TN)

        half = t // 2
        part(j, 0, half, 0, t, True)
        part(j, half, half, half, half, True)

        def loop(i, c):
            part(i, 0, t, 0, t, False)
            return c

        lax.fori_loop(j + 1, nq, loop, 0)
        dk_out[...] = dk_ref[...].astype(BF16)
        dv_out[...] = dv_ref[...].astype(BF16)

        @pl.when(j == nq - 1)
        def _():
            dq_out[...] = dq_ref[...].astype(BF16)

        @pl.when((pl.program_id(0) == HEADS // 2 - 1) & (j == nq - 1))
        def _():
            _wait_all(*_to_all_copies(send_refs, recv_refs, sems, False))

    blk = pl.BlockSpec((t, 2 * LANE), lambda h, j: (j, h))
    whole = pl.BlockSpec((s, 2 * LANE), lambda h, j: (0, h))
    out = jax.ShapeDtypeStruct((s, 1024), BF16)
    return pl.pallas_call(
        body,
        grid=(HEADS // 2, nq),
        in_specs=[whole, blk, blk, whole] + [ANY] * ns,
        out_specs=[whole, blk, blk] + [ANY] * ns,
        out_shape=[out, out, out] + [jax.ShapeDtypeStruct(a.shape, a.dtype) for a in sends],
        scratch_shapes=[pltpu.VMEM((s, 2 * LANE), F32), pltpu.VMEM((t, 2 * LANE), F32),
                        pltpu.VMEM((t, 2 * LANE), F32)] + _copy_sems(ns, 7),
        compiler_params=_params(("arbitrary", "arbitrary")),
        name="attn_bwd",
    )(qa, kr, vp, dop, *sends)


HG_T = 256
HG_NC = HG_T // HG_BLOCK
HG_G = 4
GW = 64 * HG_G


def _hg_consts():
    r = jnp.arange(HG_T)[:, None]
    c = jnp.arange(HG_T)[None, :]
    same = (r // HG_BLOCK) == (c // HG_BLOCK)
    mcum = (same & (c <= r)).astype(BF16)
    mrev = (same & (c >= r)).astype(BF16)
    msum = same.astype(BF16)
    a = jnp.arange(GW) // 64
    bd = (a[:, None] == a[None, :]).astype(F32)
    return mcum, mrev, msum, bd


def _stack_heads(xg, head):
    return jnp.concatenate([jnp.where(head == h, xg, 0.0) for h in range(HG_G)], axis=0)


def _unstack_heads(r, head, t):
    out = r[(HG_G - 1) * t:]
    for h in range(HG_G - 2, -1, -1):
        out = jnp.where(head == h, r[h * t:(h + 1) * t], out)
    return out


def _compact_state(st):
    out = st[:64]
    for h in range(1, HG_G):
        out = out + st[64 * h:64 * (h + 1)]
    return out


def _expand_state(cs, head64):
    return jnp.concatenate([jnp.where(head64 == h, cs, 0.0) for h in range(HG_G)], axis=0)


def _hg_pre(hq, hf, lbl, mcum, msum):
    lb = _sigmoid(lbl[0:1, :] - lbl[1:2, :])
    sig = _sigmoid(hf)
    f = lb + (1.0 - lb) * sig
    lf = jnp.log(f)
    b = _sel_left(mcum, lf)
    big_l = _sel_left(msum, lf)
    k = 1.0 - f
    qd = hq * jnp.exp(b)
    ki = k * jnp.exp(-b)
    ke = k * jnp.exp(big_l - b)
    return lb, sig, f, b, big_l, qd, ki, ke


def _hgrn_fwd(proj, lbl):
    s = proj.shape[0]
    t = HG_T
    mcum, _, msum, bd = _hg_consts()

    def body(hq_ref, hf_ref, hi_ref, lbl_ref, mcum_ref, msum_ref, bd_ref, o_ref, sp_ref, st_ref):
        @pl.when(pl.program_id(0) == 0)
        def _():
            st_ref[...] = jnp.zeros_like(st_ref)

        mc = mcum_ref[...]
        _, _, _, _, big_l, qd, ki, ke = _hg_pre(hq_ref[...].astype(F32), hf_ref[...].astype(F32), lbl_ref[...], mc,
                                                msum_ref[...])
        el = jnp.exp(big_l)
        hi = hi_ref[...]
        head = lax.broadcasted_iota(jnp.int32, (t, GW), 1) >> 6
        mask = jnp.concatenate([mc] * HG_G, axis=0) > 0.5
        for p in range(HEADS // HG_G):
            sl = slice(GW * p, GW * (p + 1))
            vp = hi[:, sl].astype(BF16)
            qs = _stack_heads(qd[:, sl], head).astype(BF16)
            a = jnp.where(mask, _dotg(qs, ki[:, sl].astype(BF16), NT), 0.0)
            o_intra = _unstack_heads(_dot(a.astype(BF16), vp), head, t)
            qb = qd[:, sl].astype(BF16)
            kb = ke[:, sl].astype(BF16)
            st = st_ref[p]
            for c in range(HG_NC):
                rows = slice(HG_BLOCK * c, HG_BLOCK * (c + 1))
                sp_ref[c, :, sl] = _compact_state(st)
                o_ref[rows, sl] = o_intra[rows] + _dotg(qb[rows], st.astype(BF16), NT)
                u = _dotg(vp[rows], kb[rows], TN) * bd_ref[...]
                st = st * el[HG_BLOCK * c:HG_BLOCK * c + 1, sl] + u
            st_ref[p] = st

    row = lambda j: pl.BlockSpec((t, HG_WIDTH), lambda i: (i, j))
    full = lambda a: pl.BlockSpec(a.shape, lambda i: (0, 0))
    return pl.pallas_call(
        body,
        grid=(s // t,),
        in_specs=[row(6), row(7), row(8), full(lbl), full(mcum), full(msum), full(bd)],
        out_specs=[row(0), pl.BlockSpec((HG_NC, 64, HG_WIDTH), lambda i: (i, 0, 0))],
        out_shape=[jax.ShapeDtypeStruct((s, HG_WIDTH), F32),
                   jax.ShapeDtypeStruct((s // HG_BLOCK, 64, HG_WIDTH), F32)],
        scratch_shapes=[pltpu.VMEM((HEADS // HG_G, GW, GW), F32)],
        compiler_params=_params(("arbitrary",)),
        name="hgrn_fwd",
    )(proj, proj, proj, lbl, mcum, msum, bd)


def _hgrn_bwd(proj, lbl, do, sprev, dproj):
    s = proj.shape[0]
    t = HG_T
    nt = s // t
    mcum, mrev, msum, bd = _hg_consts()

    def body(hq_ref, hf_ref, hi_ref, lbl_ref, do_ref, sp_ref, mcum_ref, mrev_ref, msum_ref, bd_ref,
             dproj_in, dh_ref, dlbl_ref, g_ref):
        del dproj_in

        @pl.when(pl.program_id(0) == 0)
        def _():
            g_ref[...] = jnp.zeros_like(g_ref)
            dlbl_ref[...] = jnp.zeros_like(dlbl_ref)

        mc = mcum_ref[...]
        lb, sig, f, b, big_l, qd, ki, ke = _hg_pre(hq_ref[...].astype(F32), hf_ref[...].astype(F32), lbl_ref[...], mc,
                                                   msum_ref[...])
        el = jnp.exp(big_l)
        hi = hi_ref[...]
        dov = do_ref[...]
        head = lax.broadcasted_iota(jnp.int32, (t, GW), 1) >> 6
        head64 = lax.broadcasted_iota(jnp.int32, (64, GW), 1) >> 6
        mask = jnp.concatenate([mc] * HG_G, axis=0) > 0.5
        dqd_parts, dke_parts, dv_parts, del_parts, dki_parts = [], [], [], [], []
        for p in range(HEADS // HG_G):
            sl = slice(GW * p, GW * (p + 1))
            vp = hi[:, sl].astype(BF16)
            qs = _stack_heads(qd[:, sl], head).astype(BF16)
            kip = ki[:, sl].astype(BF16)
            dos = _stack_heads(dov[:, sl], head).astype(BF16)
            a = jnp.where(mask, _dotg(qs, kip, NT), 0.0).astype(BF16)
            da = jnp.where(mask, _dotg(dos, vp, NT), 0.0).astype(BF16)
            r = _dot(da, kip)
            dki_parts.append(_dotg(da, qs, TN))
            qb = qd[:, sl].astype(BF16)
            kb = ke[:, sl].astype(BF16)
            dob = dov[:, sl].astype(BF16)
            g = g_ref[p]
            dqd_c, dv_c, dke_c, del_c = [], [], [], []
            for c in range(HG_NC - 1, -1, -1):
                rows = slice(HG_BLOCK * c, HG_BLOCK * (c + 1))
                gb = g.astype(BF16)
                st = _expand_state(sp_ref[c, :, sl], head64)
                dqd_c.append(_dot(dob[rows], st.astype(BF16)))
                dv_c.append(_dotg(kb[rows], gb, NT))
                dke_c.append(_dot(vp[rows], gb))
                del_c.append(jnp.broadcast_to(jnp.sum(g * st, axis=0, keepdims=True), (HG_BLOCK, GW)))
                g = g * el[HG_BLOCK * c:HG_BLOCK * c + 1, sl] + _dotg(dob[rows], qb[rows], TN) * bd_ref[...]
            g_ref[p] = g
            up = lambda parts: jnp.concatenate(parts[::-1], axis=0)
            dqd_parts.append(_unstack_heads(r, head, t) + up(dqd_c))
            dv_parts.append(_dotg(a, dos, TN) + up(dv_c))
            dke_parts.append(up(dke_c))
            del_parts.append(up(del_c))
        wide = lambda parts: jnp.concatenate(parts, axis=1)
        dqd, dke, dki, dvv, del_rows = wide(dqd_parts), wide(dke_parts), wide(dki_parts), wide(dv_parts), wide(del_parts)
        dh_ref[:, :HG_WIDTH] = (dqd * jnp.exp(b)).astype(BF16)
        dh_ref[:, 2 * HG_WIDTH:] = dvv.astype(BF16)
        dke_ke = dke * ke
        db = dqd * qd - dki * ki - dke_ke
        dl_rows = _sel_left(msum_ref[...], dke_ke) + del_rows * el
        is_last = (lax.broadcasted_iota(jnp.int32, (t, HG_WIDTH), 0) & (HG_BLOCK - 1)) == HG_BLOCK - 1
        db = db + jnp.where(is_last, dl_rows, 0.0)
        dlf = _sel_left(mrev_ref[...], db)
        dk = dki * jnp.exp(-b) + dke * jnp.exp(big_l - b)
        df = dlf / f - dk
        dh_ref[:, HG_WIDTH:2 * HG_WIDTH] = (df * (1.0 - lb) * sig * (1.0 - sig)).astype(BF16)
        dlb = jnp.sum(df * (1.0 - sig), axis=0, keepdims=True) * lb * (1.0 - lb)
        dlbl_ref[0:1, :] += dlb
        dlbl_ref[1:2, :] -= dlb

    rrow = lambda j: pl.BlockSpec((t, HG_WIDTH), lambda i: (nt - 1 - i, j))
    full = lambda a: pl.BlockSpec(a.shape, lambda i: (0, 0))
    return pl.pallas_call(
        body,
        grid=(nt,),
        in_specs=[rrow(6), rrow(7), rrow(8), full(lbl), rrow(0),
                  pl.BlockSpec((HG_NC, 64, HG_WIDTH), lambda i: (nt - 1 - i, 0, 0)),
                  full(mcum), full(mrev), full(msum), full(bd), pl.BlockSpec(memory_space=pl.ANY)],
        out_specs=[pl.BlockSpec((t, 3 * HG_WIDTH), lambda i: (nt - 1 - i, 2)),
                   pl.BlockSpec((2, HG_WIDTH), lambda i: (0, 0))],
        out_shape=[jax.ShapeDtypeStruct(dproj.shape, BF16), jax.ShapeDtypeStruct((2, HG_WIDTH), F32)],
        input_output_aliases={10: 0},
        scratch_shapes=[pltpu.VMEM((HEADS // HG_G, GW, GW), F32)],
        compiler_params=_params(("arbitrary",)),
        name="hgrn_bwd",
    )(proj, proj, proj, lbl, do, sprev, mcum, mrev, msum, bd, dproj)


def _tail(x, tgt, proj, attn, o, w_a, w_b, w_out, w_at, w_bt, w_outt, b_gate, g_post, gh):
    s = x.shape[0]
    tm = 256
    ones64 = (jnp.arange(HG_WIDTH)[:, None] // 64 == jnp.arange(HG_WIDTH)[None, :] // 64).astype(BF16)
    weights = (w_a, w_b, w_out, w_at, w_bt, w_outt)

    def body(x_ref, t_ref, ml_ref, ga_ref, gb_ref, at_ref, o_ref, *rest):
        w_hbm, (bg_ref, gp_ref, gh_ref, ones_ref) = rest[:6], rest[6:10]
        (dout_ref, dpj_ref, dop_ref, do_ref, mt_ref, dy_ref, yat_ref, dya_ref, ybt_ref, dyb_ref,
         loss_ref, dgp_ref, dbg_ref, dgh_ref) = rest[10:24]
        (wa_ref, wb_ref, wo_ref, wat_ref, wbt_ref, wot_ref), w_sem = rest[24:30], rest[30]

        @pl.when(pl.program_id(0) == 0)
        def _():
            loads = [pltpu.make_async_copy(src, dst, w_sem.at[k])
                     for k, (src, dst) in enumerate(zip(w_hbm, rest[24:30]))]
            _start_all(loads, [])
            loss_ref[...] = jnp.zeros_like(loss_ref)
            dgp_ref[...] = jnp.zeros_like(dgp_ref)
            dbg_ref[...] = jnp.zeros_like(dbg_ref)
            dgh_ref[...] = jnp.zeros_like(dgh_ref)
            _wait_all(loads, [])

        ones = ones_ref[...]
        gate_a = ga_ref[...].astype(F32)
        sa = _sigmoid(gate_a)
        silu_a = gate_a * sa
        attn_v = at_ref[...]
        ya_in = attn_v * silu_a
        ov = o_ref[...]
        ro = lax.rsqrt(_sel_right(ov * ov, ones) * (1.0 / 64.0) + EPS)
        ohat = ov * ro
        ghv = gh_ref[...]
        on = ohat * ghv
        gate_b = gb_ref[...].astype(F32)
        sb = _sigmoid(gate_b)
        silu_b = gate_b * sb
        yb_in = on * silu_b
        ya_bf = ya_in.astype(BF16)
        yb_bf = yb_in.astype(BF16)
        yat_ref[...] = ya_bf.T
        ybt_ref[...] = yb_bf.T
        y_a = _dot(ya_bf, wa_ref[...])
        y_b = _dot(yb_bf, wb_ref[...])
        gts = _sigmoid(ml_ref[...].astype(F32) + bg_ref[...])
        g_a = gts[:, :D_MODEL]
        g_b = gts[:, D_MODEL:]
        m_bf = (g_a * y_a + g_b * y_b).astype(BF16)
        mt_ref[...] = m_bf.T
        y = _dot(m_bf, wo_ref[...])
        r1 = lax.rsqrt(jnp.mean(y * y, axis=-1, keepdims=True) + EPS)
        yn = y * r1
        gp = gp_ref[...]
        e = x_ref[...] + yn * gp - t_ref[...]
        loss_ref[...] += jnp.sum(e * e, axis=0, keepdims=True)
        dout = e * (1.0 / D_MODEL)
        dout_ref[...] = dout
        dgp_ref[...] += jnp.sum(dout * yn, axis=0, keepdims=True)
        dyn = dout * gp
        dy = r1 * (dyn - yn * jnp.mean(dyn * yn, axis=-1, keepdims=True))
        dy_bf = dy.astype(BF16)
        dy_ref[...] = dy_bf
        dm = _dot(dy_bf, wot_ref[...])
        dml_a = dm * y_a * g_a * (1.0 - g_a)
        dml_b = dm * y_b * g_b * (1.0 - g_b)
        dpj_ref[:, :D_MODEL] = dml_a.astype(BF16)
        dpj_ref[:, D_MODEL:2 * D_MODEL] = dml_b.astype(BF16)
        dbg_ref[:, :D_MODEL] += jnp.sum(dml_a, axis=0, keepdims=True)
        dbg_ref[:, D_MODEL:] += jnp.sum(dml_b, axis=0, keepdims=True)
        dya_bf = (dm * g_a).astype(BF16)
        dyb_bf = (dm * g_b).astype(BF16)
        dya_ref[...] = dya_bf
        dyb_ref[...] = dyb_bf
        dya_in = _dot(dya_bf, wat_ref[...])
        dyb_in = _dot(dyb_bf, wbt_ref[...])
        dattn = dya_in * silu_a
        delta = _sel_right(dattn * attn_v, ones)
        lane = lax.broadcasted_iota(jnp.int32, (tm, LANE), 1)
        for p in range(HEADS // 2):
            sl = slice(LANE * p, LANE * (p + 1))
            xs = (dattn[:, sl], pltpu.roll(dattn[:, sl], VDIM, 1))
            nds = (-pltpu.roll(delta[:, sl], VDIM, 1), -delta[:, sl])
            for a in range(2):
                hi, lo_part = _hi_lo(nds[a])
                blk = jnp.where(lane < VDIM, xs[a], jnp.where(lane == VDIM, hi, jnp.where(lane == VDIM + 1, lo_part, 0.0)))
                dop_ref[:, LANE * (2 * p + a):LANE * (2 * p + a + 1)] = blk.astype(BF16)
        dpj_ref[:, 2 * D_MODEL:2 * D_MODEL + HG_WIDTH] = (
            dya_in * attn_v * (sa * (1.0 + gate_a * (1.0 - sa)))).astype(BF16)
        don = dyb_in * silu_b
        dpj_ref[:, 2 * D_MODEL + HG_WIDTH:] = (dyb_in * on * (sb * (1.0 + gate_b * (1.0 - sb)))).astype(BF16)
        dgh_ref[...] += jnp.sum(don * ohat, axis=0, keepdims=True)
        dohat = don * ghv
        do_ref[...] = (ro * (dohat - ohat * (_sel_right(dohat * ohat, ones) * (1.0 / 64.0)))).astype(BF16)

    row = lambda w, j: pl.BlockSpec((tm, w), lambda i: (i, j))
    col = lambda w: pl.BlockSpec((w, tm), lambda i: (0, i))
    full = lambda a: pl.BlockSpec(a.shape, lambda i: (0, 0))
    acc = lambda w: pl.BlockSpec((1, w), lambda i: (0, 0))
    sds = lambda w, dt: jax.ShapeDtypeStruct((s, w), dt)
    sdt = lambda w: jax.ShapeDtypeStruct((w, s), BF16)
    return pl.pallas_call(
        body,
        grid=(s // tm,),
        in_specs=[row(1024, 0), row(1024, 0), row(2048, 0), row(512, 4), row(512, 5), row(512, 0), row(512, 0)]
        + [ANY] * 6 + [full(b_gate), full(g_post), full(gh), full(ones64)],
        out_specs=[row(1024, 0), row(3072, 0), row(1024, 0), row(512, 0),
                   col(1024), row(1024, 0), col(512), row(1024, 0), col(512), row(1024, 0),
                   acc(1024), acc(1024), acc(2048), acc(512)],
        out_shape=[sds(1024, F32), sds(D_IN_PAD, BF16), sds(1024, BF16), sds(512, BF16),
                   sdt(1024), sds(1024, BF16), sdt(512), sds(1024, BF16), sdt(512), sds(1024, BF16),
                   jax.ShapeDtypeStruct((1, 1024), F32), jax.ShapeDtypeStruct((1, 1024), F32),
                   jax.ShapeDtypeStruct((1, 2048), F32), jax.ShapeDtypeStruct((1, 512), F32)],
        scratch_shapes=[pltpu.VMEM(a.shape, BF16) for a in weights] + [pltpu.SemaphoreType.DMA((6,))],
        compiler_params=_params(("arbitrary",), 56),
        name="tail",
    )(x, tgt, proj, proj, proj, attn, o, *weights, b_gate, g_post, gh, ones64)


def _mla_bwd(proj, dqr, dkr, dv, g_q, g_kv, w_uq_pt, w_kv_pt, rc, rs1, rs2, dproj):
    s = proj.shape[0]
    tm = 256
    scale = 1.0 / math.sqrt(QK)

    def body(cq_ref, ckv_ref, dqr_ref, dkr_ref, dv_ref, gq_ref, gkv_ref, wuqt_ref, wkvt_ref, c_ref, s1_ref, s2_ref,
             dproj_in, dqf_ref, dkvf_ref, dc_ref, dgq_ref, dgkv_ref):
        del dproj_in

        @pl.when(pl.program_id(0) == 0)
        def _():
            dgq_ref[...] = jnp.zeros_like(dgq_ref)
            dgkv_ref[...] = jnp.zeros_like(dgkv_ref)

        c, s1, s2 = c_ref[...], s1_ref[...], s2_ref[...]
        lane = lax.broadcasted_iota(jnp.int32, (tm, LANE), 1)
        ksum = jnp.zeros((tm, LANE), F32)
        for h in range(HEADS):
            sl = slice(LANE * h, LANE * (h + 1))
            dqf_ref[:, sl] = (_unrope(dqr_ref[:, sl], c, s1, s2) * scale).astype(BF16)
            dkh = dkr_ref[:, sl]
            ksum = ksum + dkh
            dkvf_ref[:, sl] = jnp.where(lane < NOPE, dkh, 0.0).astype(BF16)
            dkvf_ref[:, HEADS * LANE + LANE * h:HEADS * LANE + LANE * (h + 1)] = jnp.where(
                lane < VDIM, dv_ref[:, sl], 0.0).astype(BF16)
        dkpe = _unrope(ksum, c, s1, s2)
        dc_ref[:, Q_LORA + KV_LORA:] = jnp.where((lane >= NOPE) & (lane < QK), dkpe, 0.0).astype(BF16)
        dcqn = _dot(dqf_ref[...], wuqt_ref[...])
        dckvn = _dot(dkvf_ref[...], wkvt_ref[...])
        for x_ref, g_ref, dn, cols, dg_ref in ((cq_ref, gq_ref, dcqn, slice(0, Q_LORA), dgq_ref),
                                               (ckv_ref, gkv_ref, dckvn, slice(Q_LORA, Q_LORA + KV_LORA), dgkv_ref)):
            xv = x_ref[...].astype(F32)
            r = lax.rsqrt(jnp.mean(xv * xv, axis=-1, keepdims=True) + EPS)
            xh = xv * r
            dg_ref[...] += jnp.sum(dn * xh, axis=0, keepdims=True)
            dh = dn * g_ref[...]
            dc_ref[:, cols] = (r * (dh - xh * jnp.mean(dh * xh, axis=-1, keepdims=True))).astype(BF16)

    row = lambda w, j: pl.BlockSpec((tm, w), lambda i: (i, j))
    full = lambda a: pl.BlockSpec(a.shape, lambda i: (0, 0))
    acc = lambda w: pl.BlockSpec((1, w), lambda i: (0, 0))
    sds = lambda w, dt: jax.ShapeDtypeStruct((s, w), dt)
    return pl.pallas_call(
        body,
        grid=(s // tm,),
        in_specs=[row(768, 6), row(256, 21), row(1024, 0), row(1024, 0), row(1024, 0), full(g_q), full(g_kv),
                  full(w_uq_pt), full(w_kv_pt), row(128, 0), row(128, 0), row(128, 0),
                  pl.BlockSpec(memory_space=pl.ANY)],
        out_specs=[row(1024, 0), row(2048, 0), row(1152, 4), acc(768), acc(256)],
        out_shape=[sds(1024, BF16), sds(2048, BF16), jax.ShapeDtypeStruct(dproj.shape, BF16),
                   jax.ShapeDtypeStruct((1, 768), F32), jax.ShapeDtypeStruct((1, 256), F32)],
        input_output_aliases={12: 2},
        compiler_params=_params(("arbitrary",)),
        name="mla_bwd",
    )(proj, proj, dqr, dkr, dv, g_q, g_kv, w_uq_pt, w_kv_pt, rc, rs1, rs2, dproj)


def _pick(n, options):
    for o in options:
        if n % o == 0:
            return o
    raise ValueError(n)


def _matmul(a, b, name):
    m, k = a.shape
    n = b.shape[1]
    tm = _pick(m, (1024, 768, 512, 256))
    tn = _pick(n, (1152, 1024, 768, 512))
    tk = _pick(k, (1024, 512))
    nk = k // tk

    def body(a_ref, b_ref, o_ref):
        @pl.when(pl.program_id(2) == 0)
        def _():
            o_ref[...] = jnp.zeros_like(o_ref)

        o_ref[...] += _dot(a_ref[...], b_ref[...])

    return pl.pallas_call(
        body,
        grid=(m // tm, n // tn, nk),
        in_specs=[pl.BlockSpec((tm, tk), lambda i, j, l: (i, l)), pl.BlockSpec((tk, tn), lambda i, j, l: (l, j))],
        out_specs=pl.BlockSpec((tm, tn), lambda i, j, l: (i, j)),
        out_shape=jax.ShapeDtypeStruct((m, n), F32),
        compiler_params=_params(("arbitrary", "arbitrary", "arbitrary")),
        name=name,
    )(a, b)


def _dh_dx(dproj, w_in_pt, x, dout, g_pre, sends):
    s, k = dproj.shape
    tm = 256
    ns, ni = len(sends), s // tm

    def body(dp_ref, w_ref, x_ref, dout_ref, g_ref, *rest):
        send_refs, (dx_ref, dg_ref) = rest[:ns], rest[ns:ns + 2]
        recv_refs, sems = rest[ns + 2:2 * ns + 2], rest[2 * ns + 2:]

        @pl.when(pl.program_id(0) == 0)
        def _():
            _start_all(*_to_chips_copies(send_refs, recv_refs, sems))
            dg_ref[...] = jnp.zeros_like(dg_ref)

        dh = _dot(dp_ref[...], w_ref[...])
        xv = x_ref[...]
        r = lax.rsqrt(jnp.mean(xv * xv, axis=-1, keepdims=True) + EPS)
        xh = xv * r
        dg_ref[...] += jnp.sum(dh * xh, axis=0, keepdims=True)
        dxh = dh * g_ref[...]
        dx_ref[...] = dout_ref[...] + r * (dxh - xh * jnp.mean(dxh * xh, axis=-1, keepdims=True))

        @pl.when(pl.program_id(0) == ni - 1)
        def _():
            _wait_all(*_to_chips_copies(send_refs, recv_refs, sems))

    row = lambda w: pl.BlockSpec((tm, w), lambda i: (i, 0))
    return pl.pallas_call(
        body,
        grid=(ni,),
        in_specs=[row(k), pl.BlockSpec((k, D_MODEL), lambda i: (0, 0)), row(D_MODEL), row(D_MODEL),
                  pl.BlockSpec((1, D_MODEL), lambda i: (0, 0))] + [ANY] * ns,
        out_specs=[row(D_MODEL), pl.BlockSpec((1, D_MODEL), lambda i: (0, 0))] + [ANY] * ns,
        out_shape=[jax.ShapeDtypeStruct((s, D_MODEL), F32), jax.ShapeDtypeStruct((1, D_MODEL), F32)]
        + [jax.ShapeDtypeStruct(a.shape, a.dtype) for a in sends],
        scratch_shapes=_copy_sems(ns, 3),
        compiler_params=_params(("arbitrary",)),
        name="dh_dx",
    )(dproj, w_in_pt, x, dout, g_pre, *sends)


def _pair_reduce(slots):
    n = len(slots)
    half = [(N_DEV // 2,) + a.shape[1:] for a in slots]

    def body(*refs):
        s_refs, o_refs = refs[:n], refs[n:2 * n]
        mine, got = refs[2 * n:3 * n], refs[3 * n:4 * n]
        send_sems, recv_sems, local_sems = refs[4 * n:]
        x, y, c = _my_place()
        copies, loads = [], []
        for a in range(n):
            for q in range(N_DEV // 2):
                copies.append(pltpu.make_async_remote_copy(
                    src_ref=s_refs[a].at[2 * q + 1 - c], dst_ref=got[a].at[q],
                    send_sem=send_sems.at[4 * a + q], recv_sem=recv_sems.at[4 * a + q],
                    device_id=(x, y, 1 - c), device_id_type=MESH_ID))
                loads.append(pltpu.make_async_copy(s_refs[a].at[2 * q + c], mine[a].at[q], local_sems.at[4 * a + q]))
        _start_all(loads, copies)
        _wait_all(loads, copies)
        for a in range(n):
            o_refs[a][...] = (mine[a][...].astype(F32) + got[a][...].astype(F32)).astype(o_refs[a].dtype)

    vm = lambda: [pltpu.VMEM(h, a.dtype) for h, a in zip(half, slots)]
    return pl.pallas_call(
        body,
        in_specs=[ANY] * n,
        out_shape=[jax.ShapeDtypeStruct(h, a.dtype) for h, a in zip(half, slots)],
        scratch_shapes=vm() + vm() + [pltpu.SemaphoreType.DMA((4 * n,)), pltpu.SemaphoreType.DMA((4 * n,)),
                                      pltpu.SemaphoreType.DMA((4 * n,))],
        compiler_params=pltpu.CompilerParams(vmem_limit_bytes=48 * 2**20),
        name="pair_reduce",
    )(*slots)


def _rope_tables(s):
    inv = (np.float32(ROPE_THETA) ** (-np.arange(0, ROPE, 2, dtype=np.float32) / np.float32(ROPE))).astype(np.float32)
    ang = (np.arange(s, dtype=np.float32)[:, None] * inv[None, :]).astype(np.float32)
    cos, sin = jnp.asarray(np.cos(ang.astype(np.float64)), F32), jnp.asarray(np.sin(ang.astype(np.float64)), F32)
    z = lambda w: jnp.zeros((s, w), F32)
    rc = jnp.concatenate([jnp.ones((s, NOPE), F32), cos, cos, z(32)], axis=1)
    rs1 = jnp.concatenate([z(NOPE), -sin, z(16), z(32)], axis=1)
    rs2 = jnp.concatenate([z(NOPE), z(16), sin, z(32)], axis=1)
    return rc, rs1, rs2


def _step(x, tgt, w_blk, shards, g_pre, b_gate, g_q, g_kv, lbl, g_hgrn, g_post):
    s = x.shape[0]
    rc, rs1, rs2 = _rope_tables(s)
    gh = jnp.tile(g_hgrn, (1, HEADS))

    proj, w_in_pt, ht, *got = _gather_proj(x, g_pre, w_blk, shards[:2])
    w_uq, w_ukv = (_from_slots(n, g) for n, g in zip(MATS[:2], got))
    w_uq_p = jnp.pad(w_uq.reshape(Q_LORA, HEADS, QK), ((0, 0), (0, 0), (0, LANE - QK))).reshape(Q_LORA, HEADS * LANE)
    kv3 = w_ukv.reshape(KV_LORA, HEADS, NOPE + VDIM)
    pad64 = lambda t: jnp.pad(t, ((0, 0), (0, 0), (0, LANE - 64))).reshape(KV_LORA, HEADS * LANE)
    w_kv_p = jnp.concatenate([pad64(kv3[:, :, :NOPE]), pad64(kv3[:, :, NOPE:])], axis=1)

    qr, kr, v, cqt, ckvt = _mla_prep(proj, g_q, g_kv, w_uq_p, w_kv_p, rc, rs1, rs2)
    attn, qa, *got = _attn_fwd(qr, kr, v, shards[2:])
    w_a, w_b, w_out = (_from_slots(n, g) for n, g in zip(MATS[2:], got))
    o, sprev = _hgrn_fwd(proj, lbl)
    (dout, dproj, dop, do, mt, dy_bf, yat, dya_bf, ybt, dyb_bf,
     loss_vec, dg_post, db_gate, dgh) = _tail(x, tgt, proj, attn, o, w_a, w_b, w_out, w_a.T, w_b.T, w_out.T,
                                               b_gate, g_post, gh)
    early = [_to_slots(n, _matmul(a, b, "d" + n)).astype(BF16)
             for n, a, b in (("w_branch_a", yat, dya_bf), ("w_branch_b", ybt, dyb_bf), ("w_out", mt, dy_bf))]
    dqr, dkr, dv, *early_recv = _attn_bwd(qa, kr, v, dop, early)
    dproj, dlbl = _hgrn_bwd(proj, lbl, do, sprev, dproj)
    dqf, dkvf, dproj, dg_q, dg_kv = _mla_bwd(proj, dqr, dkr, dv, g_q, g_kv, w_uq_p.T, w_kv_p.T, rc, rs1, rs2, dproj)

    dw_in_slots = _dw_in_slots(ht, dproj)
    dw_uq_p = _matmul(cqt, dqf, "dw_uq")
    dw_kv_p = _matmul(ckvt, dkvf, "dw_kv")
    dw_uq = dw_uq_p.reshape(Q_LORA, HEADS, LANE)[:, :, :QK].reshape(Q_LORA, HEADS * QK)
    dw_ukv = jnp.concatenate([dw_kv_p[:, :HEADS * LANE].reshape(KV_LORA, HEADS, LANE)[:, :, :NOPE],
                              dw_kv_p[:, HEADS * LANE:].reshape(KV_LORA, HEADS, LANE)[:, :, :VDIM]],
                             axis=2).reshape(KV_LORA, 1024)
    late = _pair_reduce([dw_in_slots, _to_slots("w_uq", dw_uq).astype(BF16), _to_slots("w_ukv", dw_ukv).astype(BF16)])
    dx, dg_pre, *late_recv = _dh_dx(dproj, w_in_pt, x, dout, g_pre, late)

    g_sum = _vectors_sum(dg_pre, db_gate, dg_q, dg_kv, dlbl, dgh, dg_post, loss_vec)
    return dx, late_recv[0], dict(zip(MATS, late_recv[1:] + early_recv)), g_sum


def _adamw(g, w, m, v):
    c1 = 1.0 / (1.0 - ADAM_B1 ** ADAM_STEP)
    c2 = 1.0 / (1.0 - ADAM_B2 ** ADAM_STEP)
    nm = ADAM_B1 * m + (1.0 - ADAM_B1) * g
    nv = ADAM_B2 * v + (1.0 - ADAM_B2) * (g * g)
    d = -ADAM_LR * ((nm * c1) / (jnp.sqrt(nv * c2) + ADAM_EPS) + ADAM_WD * w)
    return d, nm, nv


def _sum8(r_ref):
    g = r_ref[0].astype(F32)
    for k in range(1, r_ref.shape[0]):
        g = g + r_ref[k].astype(F32)
    return g


def _sum_adamw_w_in(recv, w, m, v):
    rows, _, cols = w.shape
    tc = 256
    nc = cols // tc

    def body(r_ref, w_hbm, m_hbm, v_hbm, g_hbm, d_hbm, nm_hbm, nv_hbm, ins, outs, in_sems, out_sems):
        i = pl.program_id(0)
        slot = i & 1
        cols_of = lambda step: pl.ds(pl.multiple_of(step * tc, tc), tc)

        def load(k, step, sl):
            return pltpu.make_async_copy((w_hbm, m_hbm, v_hbm)[k].at[:, 0, cols_of(step)], ins.at[sl, k],
                                         in_sems.at[sl, k])

        def store(k, step, sl):
            return pltpu.make_async_copy(outs.at[sl, k], (g_hbm, d_hbm, nm_hbm, nv_hbm)[k].at[:, 0, cols_of(step)],
                                         out_sems.at[sl, k])

        @pl.when(i == 0)
        def _():
            for k in range(3):
                load(k, 0, 0).start()

        @pl.when(i + 1 < nc)
        def _():
            for k in range(3):
                load(k, i + 1, 1 - slot).start()

        @pl.when(i >= 2)
        def _():
            for k in range(4):
                store(k, i - 2, slot).wait()

        for k in range(3):
            load(k, i, slot).wait()
        g = _sum8(r_ref)
        d, nm, nv = _adamw(g, ins[slot, 0], ins[slot, 1], ins[slot, 2])
        for k, val in enumerate((g, d, nm, nv)):
            outs[slot, k] = val
        for k in range(4):
            store(k, i, slot).start()

        @pl.when(i == nc - 1)
        def _():
            for k in range(4):
                store(k, i, slot).wait()
            if nc >= 2:
                for k in range(4):
                    store(k, i - 1, 1 - slot).wait()

    out = jax.ShapeDtypeStruct((rows, 1, cols), F32)
    return pl.pallas_call(
        body,
        grid=(nc,),
        in_specs=[pl.BlockSpec((recv.shape[0], rows, tc), lambda i: (0, 0, i)), ANY, ANY, ANY],
        out_specs=[ANY, ANY, ANY, ANY],
        out_shape=[out, out, out, out],
        scratch_shapes=[pltpu.VMEM((2, 3, rows, tc), F32), pltpu.VMEM((2, 4, rows, tc), F32),
                        pltpu.SemaphoreType.DMA((2, 3)), pltpu.SemaphoreType.DMA((2, 4))],
        compiler_params=_params(("arbitrary",)),
        name="sum_adamw_w_in",
    )(recv, w, m, v)


def _sum_adamw_whole(recvs, ws, ms, vs):
    n = len(ws)

    def body(*refs):
        r_refs, w_refs, m_refs, v_refs = refs[:n], refs[n:2 * n], refs[2 * n:3 * n], refs[3 * n:4 * n]
        outs = refs[4 * n:]
        for a in range(n):
            g = _sum8(r_refs[a])
            d, nm, nv = _adamw(g, w_refs[a][...], m_refs[a][...], v_refs[a][...])
            outs[a][...] = g
            outs[n + a][...] = d
            outs[2 * n + a][...] = nm
            outs[3 * n + a][...] = nv

    shapes = [jax.ShapeDtypeStruct(w.shape, F32) for w in ws]
    res = pl.pallas_call(
        body,
        out_shape=shapes * 4,
        compiler_params=pltpu.CompilerParams(vmem_limit_bytes=48 * 2**20),
        name="sum_adamw_mats",
    )(*recvs, *ws, *ms, *vs)
    return res[:n], res[n:2 * n], res[2 * n:3 * n], res[3 * n:]


SMALL = ("g_pre", "b_gate", "g_q", "g_kv", "lb_logits", "g_hgrn", "g_post")
SMALL_SHAPE = dict(g_pre=(1, 1024), b_gate=(1, 2048), g_q=(1, 768), g_kv=(1, 256), lb_logits=(2, 512),
                   g_hgrn=(1, 64), g_post=(1, 1024))


def _vectors_sum(dg_pre, db_gate, dg_q, dg_kv, dlbl, dgh, dg_post, loss_vec):
    def body(gpre_ref, bg_ref, gq_ref, gkv_ref, lbl_ref, gh_ref, gpost_ref, loss_ref, out_ref, mine, got,
             send_sems, recv_sems):
        mine[...] = jnp.zeros_like(mine)
        mine[0:1, :] = gpre_ref[...]
        mine[1:2, :] = bg_ref[:, :1024]
        mine[2:3, :] = bg_ref[:, 1024:]
        mine[3:4, :Q_LORA] = gq_ref[...]
        mine[4:5, :KV_LORA] = gkv_ref[...]
        loss = (0.5 / D_MODEL) * jnp.sum(loss_ref[...], axis=-1, keepdims=True)
        mine[4:5, KV_LORA:] = jnp.broadcast_to(loss, (1, 1024 - KV_LORA))
        mine[5:6, :HG_WIDTH] = lbl_ref[0:1, :]
        mine[5:6, HG_WIDTH:] = lbl_ref[1:2, :]
        gh = gh_ref[...]
        fold = gh[:, :VDIM]
        for h in range(1, HEADS):
            fold = fold + gh[:, VDIM * h:VDIM * (h + 1)]
        mine[6:7, :VDIM] = fold
        mine[7:8, :] = gpost_ref[...]
        x, y, c = _my_place()
        me = 4 * x + 2 * y + c
        got[me] = mine[...]
        copies = [pltpu.make_async_remote_copy(
            src_ref=mine, dst_ref=got.at[me], send_sem=send_sems.at[k], recv_sem=recv_sems.at[k],
            device_id=_flip(k, x, y, c), device_id_type=MESH_ID) for k in range(N_DEV - 1)]
        _start_all([], copies)
        _wait_all([], copies)
        out_ref[...] = _sum8(got)

    return pl.pallas_call(
        body,
        out_shape=jax.ShapeDtypeStruct((8, 1024), F32),
        scratch_shapes=[pltpu.VMEM((8, 1024), F32), pltpu.VMEM((N_DEV, 8, 1024), F32),
                        pltpu.SemaphoreType.DMA((7,)), pltpu.SemaphoreType.DMA((7,))],
        name="vectors_sum",
    )(dg_pre, db_gate, dg_q, dg_kv, dlbl, dgh, dg_post, loss_vec)


def _vectors_adamw(g_sum, ws, ms, vs):
    n = len(SMALL)

    def body(g_ref, *refs):
        w_refs, m_refs, v_refs = refs[:n], refs[n:2 * n], refs[2 * n:3 * n]
        loss_ref, outs = refs[3 * n], refs[3 * n + 1:]
        g = g_ref[...]
        loss_ref[...] = g[4:5, KV_LORA:KV_LORA + 1]
        grads = (g[0:1, :], jnp.concatenate([g[1:2, :], g[2:3, :]], axis=1), g[3:4, :Q_LORA], g[4:5, :KV_LORA],
                 jnp.concatenate([g[5:6, :HG_WIDTH], g[5:6, HG_WIDTH:]], axis=0), g[6:7, :VDIM], g[7:8, :])
        for a in range(n):
            d, nm, nv = _adamw(grads[a], w_refs[a][...], m_refs[a][...], v_refs[a][...])
            outs[a][...] = grads[a]
            outs[n + a][...] = d
            outs[2 * n + a][...] = nm
            outs[3 * n + a][...] = nv

    shapes = [jax.ShapeDtypeStruct(SMALL_SHAPE[k], F32) for k in SMALL]
    res = pl.pallas_call(
        body,
        out_shape=[jax.ShapeDtypeStruct((1, 1), F32)] + shapes * 4,
        name="vectors_adamw",
    )(g_sum, *ws, *ms, *vs)
    return res[0], res[1:n + 1], res[n + 1:2 * n + 1], res[2 * n + 1:3 * n + 1], res[3 * n + 1:]


MATS = ("w_uq", "w_ukv", "w_branch_a", "w_branch_b", "w_out")
COL_SHARDED = dict(w_uq=False, w_ukv=True, w_branch_a=True, w_branch_b=True, w_out=False)
ORDER = ("g_pre", "w_in", "b_gate", "g_q", "w_uq", "g_kv", "w_ukv", "lb_logits", "g_hgrn",
         "w_branch_a", "w_branch_b", "w_out", "g_post")


def _to_slots(name, full):
    r, c = full.shape
    if COL_SHARDED[name]:
        return full.reshape(r, N_DEV, c // N_DEV).transpose(1, 0, 2)
    return full.reshape(N_DEV, r // N_DEV, c)


def _from_slots(name, slots):
    _, r, c = slots.shape
    if COL_SHARDED[name]:
        return slots.transpose(1, 0, 2).reshape(r, N_DEV * c)
    return slots.reshape(N_DEV * r, c)


def kernel(x, g_pre, w_in, b_gate, g_q, w_uq, g_kv, w_ukv, lb_logits, g_hgrn, w_branch_a, w_branch_b, w_out, g_post, loss_target, m_g_pre, m_w_in, m_b_gate, m_g_q, m_w_uq, m_g_kv, m_w_ukv, m_lb_logits, m_g_hgrn, m_w_branch_a, m_w_branch_b, m_w_out, m_g_post, v_g_pre, v_w_in, v_b_gate, v_g_q, v_w_uq, v_g_kv, v_w_ukv, v_lb_logits, v_g_hgrn, v_w_branch_a, v_w_branch_b, v_w_out, v_g_post):
    rows3 = lambda a: jnp.transpose(a, (2, 0, 1))
    w = dict(w_in=rows3(w_in), w_uq=w_uq[0], w_ukv=w_ukv[0], w_branch_a=w_branch_a[0], w_branch_b=w_branch_b[0],
             w_out=w_out[0], g_pre=g_pre, b_gate=b_gate, g_q=g_q, g_kv=g_kv, lb_logits=lb_logits, g_hgrn=g_hgrn,
             g_post=g_post)
    mom = dict(w_in=rows3(m_w_in), w_uq=m_w_uq[0], w_ukv=m_w_ukv[0], w_branch_a=m_w_branch_a[0],
               w_branch_b=m_w_branch_b[0], w_out=m_w_out[0], g_pre=m_g_pre, b_gate=m_b_gate, g_q=m_g_q, g_kv=m_g_kv,
               lb_logits=m_lb_logits, g_hgrn=m_g_hgrn, g_post=m_g_post)
    var = dict(w_in=rows3(v_w_in), w_uq=v_w_uq[0], w_ukv=v_w_ukv[0], w_branch_a=v_w_branch_a[0],
               w_branch_b=v_w_branch_b[0], w_out=v_w_out[0], g_pre=v_g_pre, b_gate=v_b_gate, g_q=v_g_q, g_kv=v_g_kv,
               lb_logits=v_lb_logits, g_hgrn=v_g_hgrn, g_post=v_g_post)

    w_blk = w["w_in"].reshape(W_IN_SHARD, D_MODEL).astype(BF16)
    dx, recv_in, recv, g_sum = _step(x[0], loss_target[0], w_blk, [w[n].astype(BF16) for n in MATS],
                                     g_pre, b_gate, g_q, g_kv, lb_logits, g_hgrn, g_post)

    g_in, d_in, m_in, v_in = _sum_adamw_w_in(recv_in, w["w_in"], mom["w_in"], var["w_in"])
    res = _sum_adamw_whole([recv[n] for n in MATS], *([t[n] for n in MATS] for t in (w, mom, var)))
    total, *vec = _vectors_adamw(g_sum, *([t[n] for n in SMALL] for t in (w, mom, var)))

    outs = []
    for mats, vecs, big in zip(res, vec, (g_in, d_in, m_in, v_in)):
        t = {**{n: a[None] for n, a in zip(MATS, mats)}, **dict(zip(SMALL, vecs)),
             "w_in": jnp.transpose(big, (1, 2, 0))}
        outs += [t[n] for n in ORDER]
    return (total.reshape(()), dx[None], *outs)
```

```python
import math

import jax
import jax.numpy as jnp
import numpy as np
from jax import lax
from jax.experimental import pallas as pl
from jax.experimental.pallas import tpu as pltpu

F32, BF16 = jnp.float32, jnp.bfloat16

D_MODEL = 1024
EPS = 1e-6
HEADS = 8
NOPE, ROPE, VDIM = 64, 32, 64
QK = NOPE + ROPE
Q_LORA, KV_LORA = 768, 256
ROPE_THETA = 10000.0
ATT_CHUNK_SHIFT = 6
HG_BLOCK = 32
HG_WIDTH = 512
D_IN = 5664
D_IN_PAD = 5760
W_IN_SHARD = D_IN // 8
N_DEV = 8
LANE = 128

ADAM_LR, ADAM_B1, ADAM_B2, ADAM_EPS, ADAM_WD, ADAM_STEP = 0.001, 0.9, 0.999, 1e-08, 0.01, 10

W_IN_SEGMENTS = ((3616, 5664, 0), (1056, 1568, 2048), (3104, 3616, 2560), (1568, 3104, 3072),
                 (0, 1024, 4608), (1024, 1056, 5696))

NT = (((1,), (1,)), ((), ()))
TN = (((0,), (0,)), ((), ()))
MESH_ID = pl.DeviceIdType.MESH


def _w_in_pieces():
    out = []
    for lo, hi, dst in W_IN_SEGMENTS:
        c = lo
        while c < hi:
            p = c // W_IN_SHARD
            e = min(hi, (p + 1) * W_IN_SHARD)
            out.append((p, c - p * W_IN_SHARD, e - p * W_IN_SHARD, dst + c - lo))
            c = e
    return out


def _params(sem, vmem_mb=48):
    return pltpu.CompilerParams(dimension_semantics=sem, vmem_limit_bytes=vmem_mb * 2**20)


def _dot(a, b):
    return jnp.dot(a, b, preferred_element_type=F32)


def _dotg(a, b, dims):
    return lax.dot_general(a, b, dims, preferred_element_type=F32)


def _split2(x):
    hi = x.astype(BF16)
    return hi, (x - hi.astype(F32)).astype(BF16)


def _sel_left(m01, x):
    hi, lo = _split2(x)
    return _dot(m01, hi) + _dot(m01, lo)


def _sel_right(x, m01):
    hi, lo = _split2(x)
    return _dot(hi, m01) + _dot(lo, m01)


def _hi_lo(x):
    hi = x.astype(BF16).astype(F32)
    return hi, x - hi


def _sigmoid(x):
    return 0.5 * jnp.tanh(0.5 * x) + 0.5


def _rope(x, c, s1, s2):
    return x * c + pltpu.roll(x, 112, 1) * s1 + pltpu.roll(x, 16, 1) * s2


def _unrope(d, c, s1, s2):
    return d * c + pltpu.roll(d * s1, 16, 1) + pltpu.roll(d * s2, 112, 1)


def _my_place():
    return lax.axis_index("x"), lax.axis_index("y"), lax.axis_index("c")


def _flip(k, x, y, c):
    fx, fy, fc = (k + 1) >> 2 & 1, (k + 1) >> 1 & 1, (k + 1) & 1
    return (1 - x if fx else x), (1 - y if fy else y), (1 - c if fc else c)


def _to_all_copies(s_refs, r_refs, sems, spread):
    send_sems, recv_sems, local_sems = sems
    x, y, c = _my_place()
    me = 4 * x + 2 * y + c
    src = (lambda a, p: s_refs[a]) if spread else (lambda a, p: s_refs[a].at[p])
    local = [pltpu.make_async_copy(src(a, me), r_refs[a].at[me], local_sems.at[a]) for a in range(len(s_refs))]
    remote = []
    for k in range(N_DEV - 1):
        px, py, pc = _flip(k, x, y, c)
        for a in range(len(s_refs)):
            remote.append(pltpu.make_async_remote_copy(
                src_ref=src(a, 4 * px + 2 * py + pc), dst_ref=r_refs[a].at[me],
                send_sem=send_sems.at[7 * a + k], recv_sem=recv_sems.at[7 * a + k],
                device_id=(px, py, pc), device_id_type=MESH_ID))
    return local, remote


def _to_chips_copies(s_refs, r_refs, sems):
    send_sems, recv_sems, local_sems = sems
    x, y, c = _my_place()
    me = 2 * x + y
    local = [pltpu.make_async_copy(s_refs[a].at[me], r_refs[a].at[me], local_sems.at[a]) for a in range(len(s_refs))]
    remote = []
    for k in range(3):
        px = 1 - x if (k + 1) >> 1 & 1 else x
        py = 1 - y if (k + 1) & 1 else y
        for a in range(len(s_refs)):
            remote.append(pltpu.make_async_remote_copy(
                src_ref=s_refs[a].at[2 * px + py], dst_ref=r_refs[a].at[me],
                send_sem=send_sems.at[3 * a + k], recv_sem=recv_sems.at[3 * a + k],
                device_id=(px, py, c), device_id_type=MESH_ID))
    return local, remote


def _start_all(local, remote):
    for cp in local + remote:
        cp.start()


def _wait_all(local, remote):
    for cp in remote:
        cp.wait_recv()
    for cp in remote:
        cp.wait_send()
    for cp in local:
        cp.wait()


def _copy_sems(n, peers):
    return [pltpu.SemaphoreType.DMA((peers * n,)), pltpu.SemaphoreType.DMA((peers * n,)),
            pltpu.SemaphoreType.DMA((n,))]


ANY = pl.BlockSpec(memory_space=pl.ANY)


def _dw_in_slots(ht, dproj):
    m, k = ht.shape
    n = dproj.shape[1]
    tn, tk = 1152, 1024
    nj, nk = n // tn, k // tk
    by_tile = [[] for _ in range(nj)]
    for p, lo, hi, dst in _w_in_pieces():
        while lo < hi:
            j = dst // tn
            cnt = min(hi - lo, (j + 1) * tn - dst)
            by_tile[j].append((p, lo, lo + cnt, dst - j * tn))
            lo, dst = lo + cnt, dst + cnt

    def body(a_ref, b_ref, s_ref, acc_ref):
        j, l = pl.program_id(0), pl.program_id(1)

        @pl.when(l == 0)
        def _():
            acc_ref[...] = jnp.zeros_like(acc_ref)

        acc_ref[...] += _dot(a_ref[...], b_ref[...])

        @pl.when(l == nk - 1)
        def _():
            at = acc_ref[...].T
            for jj in range(nj):
                @pl.when(j == jj)
                def _(jj=jj):
                    for p, lo, hi, d in by_tile[jj]:
                        s_ref[p, lo:hi, :] = at[d:d + hi - lo, :].astype(BF16)

    return pl.pallas_call(
        body,
        grid=(nj, nk),
        in_specs=[pl.BlockSpec((m, tk), lambda j, l: (0, l)), pl.BlockSpec((tk, tn), lambda j, l: (l, j))],
        out_specs=pl.BlockSpec((N_DEV, W_IN_SHARD, m), lambda j, l: (0, 0, 0)),
        out_shape=jax.ShapeDtypeStruct((N_DEV, W_IN_SHARD, m), BF16),
        scratch_shapes=[pltpu.VMEM((m, tn), F32)],
        compiler_params=_params(("arbitrary", "arbitrary")),
        name="dw_in",
    )(ht, dproj)


PROJ_DT = F32
GP_TN = 256
GP_COLS = 5888
GP_NT = GP_COLS // GP_TN


def _gp_tile_pieces():
    tiles = [[] for _ in range(GP_NT)]
    for p, lo, hi, dst in _w_in_pieces():
        while lo < hi:
            t = dst // GP_TN
            n = min(hi - lo, (t + 1) * GP_TN - dst)
            tiles[t].append((p, lo, lo + n, dst - t * GP_TN))
            lo, dst = lo + n, dst + n
    return tiles


def _gp_tables():
    pieces = _gp_tile_pieces()
    rank_of = {None: 0, 0: 1, 1: 2, 2: 2, 4: 3, 5: 3, 3: 4, 6: 5}
    order = np.zeros((N_DEV, GP_NT), np.int32)
    waits = np.zeros((N_DEV, GP_NT), np.int32)
    for me in range(N_DEV):
        x, y, c = me >> 2 & 1, me >> 1 & 1, me & 1
        chips = [(1 - x, y), (x, 1 - y), (1 - x, 1 - y)]

        def sem_of(p):
            px, py, pc = p >> 2 & 1, p >> 1 & 1, p & 1
            if (px, py) == (x, y):
                return None if pc == c else 0
            j = chips.index((px, py))
            return 1 + j if pc == c else 4 + j

        needs = [sorted({sem_of(p) for p, _, _, _ in tile} - {None}) for tile in pieces]
        ranks = [max([rank_of[k] for k in ks], default=0) for ks in needs]
        seq = sorted(range(GP_NT), key=lambda t: (ranks[t], t))
        seen = set()
        for step, t in enumerate(seq):
            order[me, step] = t
            new = [k for k in needs[t] if k not in seen]
            for k in new:
                waits[me, step] |= 1 << k
            seen.update(new)
        assert seen == set(range(7)), (me, seen)
    return order, waits


def _gather_proj(x, g_pre, w_blk, shards):
    s = x.shape[0]
    tx = 512
    ns = len(shards)
    tile_pieces = _gp_tile_pieces()
    order_np, waits_np = _gp_tables()
    xq, yq, cq = _my_place()
    me_out = 4 * xq + 2 * yq + cq
    order = lax.dynamic_index_in_dim(jnp.asarray(order_np), me_out, 0, keepdims=False)
    waits = lax.dynamic_index_in_dim(jnp.asarray(waits_np), me_out, 0, keepdims=False)

    def body(order_ref, waits_ref, x_hbm, g_ref, wblk_hbm, *rest):
        shard_refs, (proj_ref, wt_ref, ht_hbm), got_refs = rest[:ns], rest[ns:ns + 3], rest[ns + 3:2 * ns + 3]
        recv, h_ref, wtile, xbuf, htbuf = rest[2 * ns + 3:2 * ns + 8]
        send_sems, recv_sems, misc_sems = rest[2 * ns + 8:2 * ns + 11]
        sems = rest[2 * ns + 11:]
        t = pl.program_id(0)
        x_, y_, c = _my_place()
        sibling = (x_, y_, 1 - c)
        chips = [(1 - x_, y_), (x_, 1 - y_), (1 - x_, 1 - y_)]
        idx = lambda px, py, pc: 4 * px + 2 * py + pc
        me = idx(x_, y_, c)

        def copy(k, slot, to, src=None):
            return pltpu.make_async_remote_copy(
                src_ref=recv.at[slot] if src is None else src, dst_ref=recv.at[slot],
                send_sem=send_sems.at[k], recv_sem=recv_sems.at[k], device_id=to, device_id_type=MESH_ID)

        mine = pltpu.make_async_copy(wblk_hbm, recv.at[me], misc_sems.at[0])
        first = [copy(0, me, sibling, src=wblk_hbm)] + [copy(1 + j, me, (*chips[j], c), src=wblk_hbm) for j in range(2)]
        passed = [copy(4 + j, idx(*ch, c), sibling) for j, ch in enumerate(chips)]
        onward = [copy(3, idx(*chips[0], c), (*chips[1], c)), copy(3, idx(*chips[1], c), (*chips[0], c))]
        arrivals = ([copy(0, idx(x_, y_, 1 - c), sibling)] + [copy(1 + j, idx(*ch, c), sibling) for j, ch in enumerate(chips)]
                    + [copy(4 + j, idx(*ch, 1 - c), sibling) for j, ch in enumerate(chips)])

        @pl.when(t == 0)
        def _():
            mine.start()
            for cp in first:
                cp.start()
            _start_all(*_to_all_copies(shard_refs, got_refs, sems, True))

            def load(i):
                return pltpu.make_async_copy(x_hbm.at[pl.ds(i * tx, tx), :], xbuf.at[i & 1], misc_sems.at[1 + (i & 1)])

            def store(i):
                return pltpu.make_async_copy(htbuf.at[i & 1], ht_hbm.at[:, pl.ds(i * tx, tx)], misc_sems.at[3 + (i & 1)])

            load(0).start()
            for i in range(s // tx):
                if i + 1 < s // tx:
                    load(i + 1).start()
                load(i).wait()
                xv = xbuf[i & 1]
                r = lax.rsqrt(jnp.mean(xv * xv, axis=-1, keepdims=True) + EPS)
                h = (xv * r * g_ref[...]).astype(BF16)
                h_ref[i * tx:(i + 1) * tx, :] = h
                if i >= 2:
                    store(i - 2).wait()
                htbuf[i & 1] = h.T
                store(i).start()
            for i in range(max(s // tx - 2, 0), s // tx):
                store(i).wait()
            mine.wait()

        w = waits_ref[t]
        for k in range(7):
            @pl.when((w >> k) & 1 == 1)
            def _(k=k):
                arrivals[k].wait_recv()
                if 1 <= k <= 3:
                    passed[k - 1].start()
                if 1 <= k <= 2:
                    @pl.when(c == k - 1)
                    def _():
                        onward[k - 1].start()

        tile = order_ref[t]
        for tt in range(GP_NT):
            @pl.when(tile == tt)
            def _(tt=tt):
                covered = sorted((d, d + hi - lo) for _, lo, hi, d in tile_pieces[tt])
                at = 0
                for lo_z, hi_z in covered + [(GP_TN, GP_TN)]:
                    if lo_z > at:
                        wtile[at:lo_z, :] = jnp.zeros((lo_z - at, D_MODEL), BF16)
                    at = max(at, hi_z)
                for p, lo, hi, d in tile_pieces[tt]:
                    wtile[d:d + hi - lo, :] = recv[p, lo:hi, :]

        wt = wtile[...]
        wt_ref[...] = wt
        proj_ref[...] = _dotg(h_ref[...], wt, NT).astype(PROJ_DT)

        @pl.when(t == GP_NT - 1)
        def _():
            for cp in first + passed + onward[:1]:
                cp.wait_send()
            _wait_all(*_to_all_copies(shard_refs, got_refs, sems, True))

    grid_spec = pltpu.PrefetchScalarGridSpec(
        num_scalar_prefetch=2,
        grid=(GP_NT,),
        in_specs=[ANY, pl.BlockSpec((1, D_MODEL), lambda t, o, w: (0, 0)), ANY] + [ANY] * ns,
        out_specs=[pl.BlockSpec((s, GP_TN), lambda t, o, w: (0, o[t])),
                   pl.BlockSpec((GP_TN, D_MODEL), lambda t, o, w: (o[t], 0)), ANY] + [ANY] * ns,
        scratch_shapes=[pltpu.VMEM((N_DEV, W_IN_SHARD, D_MODEL), BF16), pltpu.VMEM((s, D_MODEL), BF16),
                        pltpu.VMEM((GP_TN, D_MODEL), BF16), pltpu.VMEM((2, tx, D_MODEL), F32),
                        pltpu.VMEM((2, D_MODEL, tx), BF16),
                        pltpu.SemaphoreType.DMA((7,)), pltpu.SemaphoreType.DMA((7,)), pltpu.SemaphoreType.DMA((5,))]
        + _copy_sems(ns, 7),
    )
    return pl.pallas_call(
        body,
        grid_spec=grid_spec,
        out_shape=[jax.ShapeDtypeStruct((s, GP_COLS), PROJ_DT), jax.ShapeDtypeStruct((GP_COLS, D_MODEL), BF16),
                   jax.ShapeDtypeStruct((D_MODEL, s), BF16)]
        + [jax.ShapeDtypeStruct((N_DEV,) + b.shape, b.dtype) for b in shards],
        compiler_params=_params(("arbitrary",), 56),
        name="gather_proj",
    )(order, waits, x, g_pre, w_blk, *shards)


def _mla_prep(proj, g_q, g_kv, w_uq_p, w_kv_p, rc, rs1, rs2):
    s = proj.shape[0]
    tm = 256
    scale = 1.0 / math.sqrt(QK)

    def body(cq_ref, ckv_ref, kpe_ref, gq_ref, gkv_ref, wuq_ref, wkv_ref, c_ref, s1_ref, s2_ref,
             qr_ref, kr_ref, v_ref, cqt_ref, ckvt_ref):
        cq = cq_ref[...].astype(F32)
        r = lax.rsqrt(jnp.mean(cq * cq, axis=-1, keepdims=True) + EPS)
        cqn = (cq * r * gq_ref[...]).astype(BF16)
        cqt_ref[...] = cqn.T
        q = _dot(cqn, wuq_ref[...])
        ckv = ckv_ref[...].astype(F32)
        r = lax.rsqrt(jnp.mean(ckv * ckv, axis=-1, keepdims=True) + EPS)
        ckvn = (ckv * r * gkv_ref[...]).astype(BF16)
        ckvt_ref[...] = ckvn.T
        kv = _dot(ckvn, wkv_ref[...])
        c, s1, s2 = c_ref[...], s1_ref[...], s2_ref[...]
        lane = lax.broadcasted_iota(jnp.int32, (tm, LANE), 1)
        kpe = _rope(kpe_ref[...].astype(F32), c, s1, s2) + jnp.where((lane == QK) | (lane == QK + 1), 1.0, 0.0)
        vone = jnp.where((lane == VDIM) | (lane == VDIM + 1), 1.0, 0.0)
        for h in range(HEADS):
            sl = slice(LANE * h, LANE * (h + 1))
            qr_ref[:, sl] = (_rope(q[:, sl], c, s1, s2) * scale).astype(BF16)
            kr_ref[:, sl] = (kv[:, sl] + kpe).astype(BF16)
            v_ref[:, sl] = (kv[:, HEADS * LANE + LANE * h:HEADS * LANE + LANE * (h + 1)] + vone).astype(BF16)

    row = lambda w, j: pl.BlockSpec((tm, w), lambda i: (i, j))
    col = lambda w: pl.BlockSpec((w, tm), lambda i: (0, i))
    full = lambda a: pl.BlockSpec(a.shape, lambda i: (0, 0))
    return pl.pallas_call(
        body,
        grid=(s // tm,),
        in_specs=[row(768, 6), row(256, 21), row(128, 44), full(g_q), full(g_kv), full(w_uq_p), full(w_kv_p),
                  row(128, 0), row(128, 0), row(128, 0)],
        out_specs=[row(1024, 0), row(1024, 0), row(1024, 0), col(768), col(256)],
        out_shape=[jax.ShapeDtypeStruct((s, 1024), BF16), jax.ShapeDtypeStruct((s, 1024), BF16),
                   jax.ShapeDtypeStruct((s, 1024), BF16), jax.ShapeDtypeStruct((768, s), BF16),
                   jax.ShapeDtypeStruct((256, s), BF16)],
        compiler_params=_params(("arbitrary",)),
        name="mla_prep",
    )(proj, proj, proj, g_q, g_kv, w_uq_p, w_kv_p, rc, rs1, rs2)


ATT_T = 512
ATT_FWD_HEADS = 4


def _chunk_mask(transposed):
    r = lax.broadcasted_iota(jnp.int32, (ATT_T, ATT_T), 0) >> ATT_CHUNK_SHIFT
    c = lax.broadcasted_iota(jnp.int32, (ATT_T, ATT_T), 1) >> ATT_CHUNK_SHIFT
    return (r <= c) if transposed else (c <= r)


def _attn_fwd(qr, kr, vp, shards):
    s = qr.shape[0]
    t = ATT_T
    g = ATT_FWD_HEADS
    ns = len(shards)

    def body(q_ref, k_ref, v_ref, *rest):
        shard_refs, (o_ref, qa_ref), got_refs = rest[:ns], rest[ns:ns + 2], rest[ns + 2:2 * ns + 2]
        sc_ref, sems = rest[2 * ns + 2], rest[2 * ns + 3:]
        qi = pl.program_id(1)

        @pl.when((pl.program_id(0) == 0) & (qi == 0))
        def _():
            _start_all(*_to_all_copies(shard_refs, got_refs, sems, True))
        lane = lax.broadcasted_iota(jnp.int32, (t, LANE), 1)
        sls = [slice(LANE * a, LANE * (a + 1)) for a in range(g)]
        qs = [q_ref[:, sl] for sl in sls]

        def scores(j):
            rows = pl.ds(pl.multiple_of(j * t, t), t)
            for a in range(g):
                sc_ref[j & 1, a] = _dotg(qs[a], k_ref[rows, sls[a]], NT)

        def step(j, carry, masked):
            rows = pl.ds(pl.multiple_of(j * t, t), t)
            out = []
            for a in range(g):
                m, acc = carry[a]
                sc = sc_ref[j & 1, a]
                if masked:
                    sc = jnp.where(_chunk_mask(False), sc, -1e30)
                m_new = jnp.maximum(m, jnp.max(sc, axis=-1, keepdims=True))
                p = jnp.exp(sc - m_new).astype(BF16)
                acc = jnp.exp(m - m_new) * acc + _dot(p, v_ref[rows, sls[a]])
                out.append((m_new, acc))
            return tuple(out)

        def loop(j, carry):
            carry = step(j, carry, False)
            scores(j + 1)
            return carry

        init = tuple((jnp.full((t, 1), -1e30, F32), jnp.zeros((t, LANE), F32)) for _ in range(g))
        scores(0)
        carry = lax.fori_loop(0, qi, loop, init)
        carry = step(qi, carry, True)
        outs = []
        for a in range(g):
            m, acc = carry[a]
            l = acc[:, VDIM:VDIM + 1]
            outs.append(acc / l)
            hi, lo_part = _hi_lo(-(m + jnp.log(l)))
            qa = jnp.where(lane == QK, hi, jnp.where(lane == QK + 1, lo_part, qs[a].astype(F32)))
            qa_ref[:, sls[a]] = qa.astype(BF16)
        for p in range(g // 2):
            o_ref[:, LANE * p:LANE * (p + 1)] = jnp.where(lane < VDIM, outs[2 * p], pltpu.roll(outs[2 * p + 1], VDIM, 1))

        @pl.when((pl.program_id(0) == HEADS // g - 1) & (qi == s // t - 1))
        def _():
            _wait_all(*_to_all_copies(shard_refs, got_refs, sems, True))

    return pl.pallas_call(
        body,
        grid=(HEADS // g, s // t),
        in_specs=[
            pl.BlockSpec((t, g * LANE), lambda h, i: (i, h)),
            pl.BlockSpec((s, g * LANE), lambda h, i: (0, h)),
            pl.BlockSpec((s, g * LANE), lambda h, i: (0, h)),
        ] + [ANY] * ns,
        out_specs=[
            pl.BlockSpec((t, g * VDIM), lambda h, i: (i, h)),
            pl.BlockSpec((t, g * LANE), lambda h, i: (i, h)),
        ] + [ANY] * ns,
        out_shape=[jax.ShapeDtypeStruct((s, 512), F32), jax.ShapeDtypeStruct((s, 1024), BF16)]
        + [jax.ShapeDtypeStruct((N_DEV,) + b.shape, b.dtype) for b in shards],
        scratch_shapes=[pltpu.VMEM((2, g, t, t), F32)] + _copy_sems(ns, 7),
        compiler_params=_params(("arbitrary", "arbitrary")),
        name="attn_fwd",
    )(qr, kr, vp, *shards)


def _attn_bwd(qa, kr, vp, dop, sends):
    s = qa.shape[0]
    t = ATT_T
    nq = s // t
    ns = len(sends)

    def body(q_ref, k_ref, v_ref, do_ref, *rest):
        send_refs, (dq_out, dk_out, dv_out) = rest[:ns], rest[ns:ns + 3]
        recv_refs = rest[ns + 3:2 * ns + 3]
        (dq_ref, dk_ref, dv_ref), sems = rest[2 * ns + 3:2 * ns + 6], rest[2 * ns + 6:]
        j = pl.program_id(1)
        sls = [slice(LANE * a, LANE * (a + 1)) for a in range(2)]

        @pl.when((pl.program_id(0) == 0) & (j == 0))
        def _():
            _start_all(*_to_all_copies(send_refs, recv_refs, sems, False))

        @pl.when(j == 0)
        def _():
            dq_ref[...] = jnp.zeros_like(dq_ref)

        dk_ref[...] = jnp.zeros_like(dk_ref)
        dv_ref[...] = jnp.zeros_like(dv_ref)
        ks = [k_ref[:, sl] for sl in sls]
        vs = [v_ref[:, sl] for sl in sls]

        def part(i, k_lo, k_n, q_lo, q_n, masked):
            rows = pl.ds(pl.multiple_of(i * t + q_lo, 256), q_n)
            keys = slice(k_lo, k_lo + k_n)
            for a in range(2):
                q = q_ref[rows, sls[a]]
                do = do_ref[rows, sls[a]]
                sc = _dotg(ks[a][keys], q, NT)
                if masked:
                    kc = lax.broadcasted_iota(jnp.int32, (k_n, q_n), 0) >> ATT_CHUNK_SHIFT
                    qc = lax.broadcasted_iota(jnp.int32, (k_n, q_n), 1) >> ATT_CHUNK_SHIFT
                    sc = jnp.where(kc <= qc, sc, -1e30)
                p = jnp.exp(sc)
                ds = (p * _dotg(vs[a][keys], do, NT)).astype(BF16)
                dv_ref[keys, sls[a]] += _dot(p.astype(BF16), do)
                dk_ref[keys, sls[a]] += _dot(ds, q)
                dq_ref[rows, sls[a]] += _dotg(ds, ks[a][keys], TN)

        half = t // 2
        part(j, 0, half, 0, t, True)
        part(j, half, half, half, half, True)

        def loop(i, c):
            part(i, 0, t, 0, t, False)
            return c

        lax.fori_loop(j + 1, nq, loop, 0)
        dk_out[...] = dk_ref[...].astype(BF16)
        dv_out[...] = dv_ref[...].astype(BF16)

        @pl.when(j == nq - 1)
        def _():
            dq_out[...] = dq_ref[...].astype(BF16)

        @pl.when((pl.program_id(0) == HEADS // 2 - 1) & (j == nq - 1))
        def _():
            _wait_all(*_to_all_copies(send_refs, recv_refs, sems, False))

    blk = pl.BlockSpec((t, 2 * LANE), lambda h, j: (j, h))
    whole = pl.BlockSpec((s, 2 * LANE), lambda h, j: (0, h))
    out = jax.ShapeDtypeStruct((s, 1024), BF16)
    return pl.pallas_call(
        body,
        grid=(HEADS // 2, nq),
        in_specs=[whole, blk, blk, whole] + [ANY] * ns,
        out_specs=[whole, blk, blk] + [ANY] * ns,
        out_shape=[out, out, out] + [jax.ShapeDtypeStruct(a.shape, a.dtype) for a in sends],
        scratch_shapes=[pltpu.VMEM((s, 2 * LANE), F32), pltpu.VMEM((t, 2 * LANE), F32),
                        pltpu.VMEM((t, 2 * LANE), F32)] + _copy_sems(ns, 7),
        compiler_params=_params(("arbitrary", "arbitrary")),
        name="attn_bwd",
    )(qa, kr, vp, dop, *sends)


HG_T = 256
HG_NC = HG_T // HG_BLOCK
HG_G = 4
GW = 64 * HG_G


def _hg_consts():
    r = jnp.arange(HG_T)[:, None]
    c = jnp.arange(HG_T)[None, :]
    same = (r // HG_BLOCK) == (c // HG_BLOCK)
    mcum = (same & (c <= r)).astype(BF16)
    mrev = (same & (c >= r)).astype(BF16)
    msum = same.astype(BF16)
    a = jnp.arange(GW) // 64
    bd = (a[:, None] == a[None, :]).astype(F32)
    return mcum, mrev, msum, bd


def _stack_heads(xg, head):
    return jnp.concatenate([jnp.where(head == h, xg, 0.0) for h in range(HG_G)], axis=0)


def _unstack_heads(r, head, t):
    out = r[(HG_G - 1) * t:]
    for h in range(HG_G - 2, -1, -1):
        out = jnp.where(head == h, r[h * t:(h + 1) * t], out)
    return out


def _compact_state(st):
    out = st[:64]
    for h in range(1, HG_G):
        out = out + st[64 * h:64 * (h + 1)]
    return out


def _expand_state(cs, head64):
    return jnp.concatenate([jnp.where(head64 == h, cs, 0.0) for h in range(HG_G)], axis=0)


def _hg_pre(hq, hf, lbl, mcum, msum):
    lb = _sigmoid(lbl[0:1, :] - lbl[1:2, :])
    sig = _sigmoid(hf)
    f = lb + (1.0 - lb) * sig
    lf = jnp.log(f)
    b = _sel_left(mcum, lf)
    big_l = _sel_left(msum, lf)
    k = 1.0 - f
    qd = hq * jnp.exp(b)
    ki = k * jnp.exp(-b)
    ke = k * jnp.exp(big_l - b)
    return lb, sig, f, b, big_l, qd, ki, ke


def _hgrn_fwd(proj, lbl):
    s = proj.shape[0]
    t = HG_T
    mcum, _, msum, bd = _hg_consts()

    def body(hq_ref, hf_ref, hi_ref, lbl_ref, mcum_ref, msum_ref, bd_ref, o_ref, sp_ref, st_ref):
        @pl.when(pl.program_id(0) == 0)
        def _():
            st_ref[...] = jnp.zeros_like(st_ref)

        mc = mcum_ref[...]
        _, _, _, _, big_l, qd, ki, ke = _hg_pre(hq_ref[...].astype(F32), hf_ref[...].astype(F32), lbl_ref[...], mc,
                                                msum_ref[...])
        el = jnp.exp(big_l)
        hi = hi_ref[...]
        head = lax.broadcasted_iota(jnp.int32, (t, GW), 1) >> 6
        mask = jnp.concatenate([mc] * HG_G, axis=0) > 0.5
        for p in range(HEADS // HG_G):
            sl = slice(GW * p, GW * (p + 1))
            vp = hi[:, sl].astype(BF16)
            qs = _stack_heads(qd[:, sl], head).astype(BF16)
            a = jnp.where(mask, _dotg(qs, ki[:, sl].astype(BF16), NT), 0.0)
            o_intra = _unstack_heads(_dot(a.astype(BF16), vp), head, t)
            qb = qd[:, sl].astype(BF16)
            kb = ke[:, sl].astype(BF16)
            st = st_ref[p]
            for c in range(HG_NC):
                rows = slice(HG_BLOCK * c, HG_BLOCK * (c + 1))
                sp_ref[c, :, sl] = _compact_state(st)
                o_ref[rows, sl] = o_intra[rows] + _dotg(qb[rows], st.astype(BF16), NT)
                u = _dotg(vp[rows], kb[rows], TN) * bd_ref[...]
                st = st * el[HG_BLOCK * c:HG_BLOCK * c + 1, sl] + u
            st_ref[p] = st

    row = lambda j: pl.BlockSpec((t, HG_WIDTH), lambda i: (i, j))
    full = lambda a: pl.BlockSpec(a.shape, lambda i: (0, 0))
    return pl.pallas_call(
        body,
        grid=(s // t,),
        in_specs=[row(6), row(7), row(8), full(lbl), full(mcum), full(msum), full(bd)],
        out_specs=[row(0), pl.BlockSpec((HG_NC, 64, HG_WIDTH), lambda i: (i, 0, 0))],
        out_shape=[jax.ShapeDtypeStruct((s, HG_WIDTH), F32),
                   jax.ShapeDtypeStruct((s // HG_BLOCK, 64, HG_WIDTH), F32)],
        scratch_shapes=[pltpu.VMEM((HEADS // HG_G, GW, GW), F32)],
        compiler_params=_params(("arbitrary",)),
        name="hgrn_fwd",
    )(proj, proj, proj, lbl, mcum, msum, bd)


def _hgrn_bwd(proj, lbl, do, sprev, dproj):
    s = proj.shape[0]
    t = HG_T
    nt = s // t
    mcum, mrev, msum, bd = _hg_consts()

    def body(hq_ref, hf_ref, hi_ref, lbl_ref, do_ref, sp_ref, mcum_ref, mrev_ref, msum_ref, bd_ref,
             dproj_in, dh_ref, dlbl_ref, g_ref):
        del dproj_in

        @pl.when(pl.program_id(0) == 0)
        def _():
            g_ref[...] = jnp.zeros_like(g_ref)
            dlbl_ref[...] = jnp.zeros_like(dlbl_ref)

        mc = mcum_ref[...]
        lb, sig, f, b, big_l, qd, ki, ke = _hg_pre(hq_ref[...].astype(F32), hf_ref[...].astype(F32), lbl_ref[...], mc,
                                                   msum_ref[...])
        el = jnp.exp(big_l)
        hi = hi_ref[...]
        dov = do_ref[...]
        head = lax.broadcasted_iota(jnp.int32, (t, GW), 1) >> 6
        head64 = lax.broadcasted_iota(jnp.int32, (64, GW), 1) >> 6
        mask = jnp.concatenate([mc] * HG_G, axis=0) > 0.5
        dqd_parts, dke_parts, dv_parts, del_parts, dki_parts = [], [], [], [], []
        for p in range(HEADS // HG_G):
            sl = slice(GW * p, GW * (p + 1))
            vp = hi[:, sl].astype(BF16)
            qs = _stack_heads(qd[:, sl], head).astype(BF16)
            kip = ki[:, sl].astype(BF16)
            dos = _stack_heads(dov[:, sl], head).astype(BF16)
            a = jnp.where(mask, _dotg(qs, kip, NT), 0.0).astype(BF16)
            da = jnp.where(mask, _dotg(dos, vp, NT), 0.0).astype(BF16)
            r = _dot(da, kip)
            dki_parts.append(_dotg(da, qs, TN))
            qb = qd[:, sl].astype(BF16)
            kb = ke[:, sl].astype(BF16)
            dob = dov[:, sl].astype(BF16)
            g = g_ref[p]
            dqd_c, dv_c, dke_c, del_c = [], [], [], []
            for c in range(HG_NC - 1, -1, -1):
                rows = slice(HG_BLOCK * c, HG_BLOCK * (c + 1))
                gb = g.astype(BF16)
                st = _expand_state(sp_ref[c, :, sl], head64)
                dqd_c.append(_dot(dob[rows], st.astype(BF16)))
                dv_c.append(_dotg(kb[rows], gb, NT))
                dke_c.append(_dot(vp[rows], gb))
                del_c.append(jnp.broadcast_to(jnp.sum(g * st, axis=0, keepdims=True), (HG_BLOCK, GW)))
                g = g * el[HG_BLOCK * c:HG_BLOCK * c + 1, sl] + _dotg(dob[rows], qb[rows], TN) * bd_ref[...]
            g_ref[p] = g
            up = lambda parts: jnp.concatenate(parts[::-1], axis=0)
            dqd_parts.append(_unstack_heads(r, head, t) + up(dqd_c))
            dv_parts.append(_dotg(a, dos, TN) + up(dv_c))
            dke_parts.append(up(dke_c))
            del_parts.append(up(del_c))
        wide = lambda parts: jnp.concatenate(parts, axis=1)
        dqd, dke, dki, dvv, del_rows = wide(dqd_parts), wide(dke_parts), wide(dki_parts), wide(dv_parts), wide(del_parts)
        dh_ref[:, :HG_WIDTH] = (dqd * jnp.exp(b)).astype(BF16)
        dh_ref[:, 2 * HG_WIDTH:] = dvv.astype(BF16)
        dke_ke = dke * ke
        db = dqd * qd - dki * ki - dke_ke
        dl_rows = _sel_left(msum_ref[...], dke_ke) + del_rows * el
        is_last = (lax.broadcasted_iota(jnp.int32, (t, HG_WIDTH), 0) & (HG_BLOCK - 1)) == HG_BLOCK - 1
        db = db + jnp.where(is_last, dl_rows, 0.0)
        dlf = _sel_left(mrev_ref[...], db)
        dk = dki * jnp.exp(-b) + dke * jnp.exp(big_l - b)
        df = dlf / f - dk
        dh_ref[:, HG_WIDTH:2 * HG_WIDTH] = (df * (1.0 - lb) * sig * (1.0 - sig)).astype(BF16)
        dlb = jnp.sum(df * (1.0 - sig), axis=0, keepdims=True) * lb * (1.0 - lb)
        dlbl_ref[0:1, :] += dlb
        dlbl_ref[1:2, :] -= dlb

    rrow = lambda j: pl.BlockSpec((t, HG_WIDTH), lambda i: (nt - 1 - i, j))
    full = lambda a: pl.BlockSpec(a.shape, lambda i: (0, 0))
    return pl.pallas_call(
        body,
        grid=(nt,),
        in_specs=[rrow(6), rrow(7), rrow(8), full(lbl), rrow(0),
                  pl.BlockSpec((HG_NC, 64, HG_WIDTH), lambda i: (nt - 1 - i, 0, 0)),
                  full(mcum), full(mrev), full(msum), full(bd), pl.BlockSpec(memory_space=pl.ANY)],
        out_specs=[pl.BlockSpec((t, 3 * HG_WIDTH), lambda i: (nt - 1 - i, 2)),
                   pl.BlockSpec((2, HG_WIDTH), lambda i: (0, 0))],
        out_shape=[jax.ShapeDtypeStruct(dproj.shape, BF16), jax.ShapeDtypeStruct((2, HG_WIDTH), F32)],
        input_output_aliases={10: 0},
        scratch_shapes=[pltpu.VMEM((HEADS // HG_G, GW, GW), F32)],
        compiler_params=_params(("arbitrary",)),
        name="hgrn_bwd",
    )(proj, proj, proj, lbl, do, sprev, mcum, mrev, msum, bd, dproj)


def _tail(x, tgt, proj, attn, o, w_a, w_b, w_out, w_at, w_bt, w_outt, b_gate, g_post, gh):
    s = x.shape[0]
    tm = 256
    ones64 = (jnp.arange(HG_WIDTH)[:, None] // 64 == jnp.arange(HG_WIDTH)[None, :] // 64).astype(BF16)
    weights = (w_a, w_b, w_out, w_at, w_bt, w_outt)

    def body(x_ref, t_ref, ml_ref, ga_ref, gb_ref, at_ref, o_ref, *rest):
        w_hbm, (bg_ref, gp_ref, gh_ref, ones_ref) = rest[:6], rest[6:10]
        (dout_ref, dpj_ref, dop_ref, do_ref, mt_ref, dy_ref, yat_ref, dya_ref, ybt_ref, dyb_ref,
         loss_ref, dgp_ref, dbg_ref, dgh_ref) = rest[10:24]
        (wa_ref, wb_ref, wo_ref, wat_ref, wbt_ref, wot_ref), w_sem = rest[24:30], rest[30]

        @pl.when(pl.program_id(0) == 0)
        def _():
            loads = [pltpu.make_async_copy(src, dst, w_sem.at[k])
                     for k, (src, dst) in enumerate(zip(w_hbm, rest[24:30]))]
            _start_all(loads, [])
            loss_ref[...] = jnp.zeros_like(loss_ref)
            dgp_ref[...] = jnp.zeros_like(dgp_ref)
            dbg_ref[...] = jnp.zeros_like(dbg_ref)
            dgh_ref[...] = jnp.zeros_like(dgh_ref)
            _wait_all(loads, [])

        ones = ones_ref[...]
        gate_a = ga_ref[...].astype(F32)
        sa = _sigmoid(gate_a)
        silu_a = gate_a * sa
        attn_v = at_ref[...]
        ya_in = attn_v * silu_a
        ov = o_ref[...]
        ro = lax.rsqrt(_sel_right(ov * ov, ones) * (1.0 / 64.0) + EPS)
        ohat = ov * ro
        ghv = gh_ref[...]
        on = ohat * ghv
        gate_b = gb_ref[...].astype(F32)
        sb = _sigmoid(gate_b)
        silu_b = gate_b * sb
        yb_in = on * silu_b
        ya_bf = ya_in.astype(BF16)
        yb_bf = yb_in.astype(BF16)
        yat_ref[...] = ya_bf.T
        ybt_ref[...] = yb_bf.T
        y_a = _dot(ya_bf, wa_ref[...])
        y_b = _dot(yb_bf, wb_ref[...])
        gts = _sigmoid(ml_ref[...].astype(F32) + bg_ref[...])
        g_a = gts[:, :D_MODEL]
        g_b = gts[:, D_MODEL:]
        m_bf = (g_a * y_a + g_b * y_b).astype(BF16)
        mt_ref[...] = m_bf.T
        y = _dot(m_bf, wo_ref[...])
        r1 = lax.rsqrt(jnp.mean(y * y, axis=-1, keepdims=True) + EPS)
        yn = y * r1
        gp = gp_ref[...]
        e = x_ref[...] + yn * gp - t_ref[...]
        loss_ref[...] += jnp.sum(e * e, axis=0, keepdims=True)
        dout = e * (1.0 / D_MODEL)
        dout_ref[...] = dout
        dgp_ref[...] += jnp.sum(dout * yn, axis=0, keepdims=True)
        dyn = dout * gp
        dy = r1 * (dyn - yn * jnp.mean(dyn * yn, axis=-1, keepdims=True))
        dy_bf = dy.astype(BF16)
        dy_ref[...] = dy_bf
        dm = _dot(dy_bf, wot_ref[...])
        dml_a = dm * y_a * g_a * (1.0 - g_a)
        dml_b = dm * y_b * g_b * (1.0 - g_b)
        dpj_ref[:, :D_MODEL] = dml_a.astype(BF16)
        dpj_ref[:, D_MODEL:2 * D_MODEL] = dml_b.astype(BF16)
        dbg_ref[:, :D_MODEL] += jnp.sum(dml_a, axis=0, keepdims=True)
        dbg_ref[:, D_MODEL:] += jnp.sum(dml_b, axis=0, keepdims=True)
        dya_bf = (dm * g_a).astype(BF16)
        dyb_bf = (dm * g_b).astype(BF16)
        dya_ref[...] = dya_bf
        dyb_ref[...] = dyb_bf
        dya_in = _dot(dya_bf, wat_ref[...])
        dyb_in = _dot(dyb_bf, wbt_ref[...])
        dattn = dya_in * silu_a
        delta = _sel_right(dattn * attn_v, ones)
        lane = lax.broadcasted_iota(jnp.int32, (tm, LANE), 1)
        for p in range(HEADS // 2):
            sl = slice(LANE * p, LANE * (p + 1))
            xs = (dattn[:, sl], pltpu.roll(dattn[:, sl], VDIM, 1))
            nds = (-pltpu.roll(delta[:, sl], VDIM, 1), -delta[:, sl])
            for a in range(2):
                hi, lo_part = _hi_lo(nds[a])
                blk = jnp.where(lane < VDIM, xs[a], jnp.where(lane == VDIM, hi, jnp.where(lane == VDIM + 1, lo_part, 0.0)))
                dop_ref[:, LANE * (2 * p + a):LANE * (2 * p + a + 1)] = blk.astype(BF16)
        dpj_ref[:, 2 * D_MODEL:2 * D_MODEL + HG_WIDTH] = (
            dya_in * attn_v * (sa * (1.0 + gate_a * (1.0 - sa)))).astype(BF16)
        don = dyb_in * silu_b
        dpj_ref[:, 2 * D_MODEL + HG_WIDTH:] = (dyb_in * on * (sb * (1.0 + gate_b * (1.0 - sb)))).astype(BF16)
        dgh_ref[...] += jnp.sum(don * ohat, axis=0, keepdims=True)
        dohat = don * ghv
        do_ref[...] = (ro * (dohat - ohat * (_sel_right(dohat * ohat, ones) * (1.0 / 64.0)))).astype(BF16)

    row = lambda w, j: pl.BlockSpec((tm, w), lambda i: (i, j))
    col = lambda w: pl.BlockSpec((w, tm), lambda i: (0, i))
    full = lambda a: pl.BlockSpec(a.shape, lambda i: (0, 0))
    acc = lambda w: pl.BlockSpec((1, w), lambda i: (0, 0))
    sds = lambda w, dt: jax.ShapeDtypeStruct((s, w), dt)
    sdt = lambda w: jax.ShapeDtypeStruct((w, s), BF16)
    return pl.pallas_call(
        body,
        grid=(s // tm,),
        in_specs=[row(1024, 0), row(1024, 0), row(2048, 0), row(512, 4), row(512, 5), row(512, 0), row(512, 0)]
        + [ANY] * 6 + [full(b_gate), full(g_post), full(gh), full(ones64)],
        out_specs=[row(1024, 0), row(3072, 0), row(1024, 0), row(512, 0),
                   col(1024), row(1024, 0), col(512), row(1024, 0), col(512), row(1024, 0),
                   acc(1024), acc(1024), acc(2048), acc(512)],
        out_shape=[sds(1024, F32), sds(D_IN_PAD, BF16), sds(1024, BF16), sds(512, BF16),
                   sdt(1024), sds(1024, BF16), sdt(512), sds(1024, BF16), sdt(512), sds(1024, BF16),
                   jax.ShapeDtypeStruct((1, 1024), F32), jax.ShapeDtypeStruct((1, 1024), F32),
                   jax.ShapeDtypeStruct((1, 2048), F32), jax.ShapeDtypeStruct((1, 512), F32)],
        scratch_shapes=[pltpu.VMEM(a.shape, BF16) for a in weights] + [pltpu.SemaphoreType.DMA((6,))],
        compiler_params=_params(("arbitrary",), 56),
        name="tail",
    )(x, tgt, proj, proj, proj, attn, o, *weights, b_gate, g_post, gh, ones64)


def _mla_bwd(proj, dqr, dkr, dv, g_q, g_kv, w_uq_pt, w_kv_pt, rc, rs1, rs2, cqt, ckvt, dproj):
    s = proj.shape[0]
    tm = 256
    scale = 1.0 / math.sqrt(QK)

    def body(cq_ref, ckv_ref, dqr_ref, dkr_ref, dv_ref, gq_ref, gkv_ref, wuqt_ref, wkvt_ref, c_ref, s1_ref, s2_ref,
             cqt_ref, ckvt_ref, dproj_in, dc_ref, dgq_ref, dgkv_ref, dwuq_ref, dwkv_ref, dqf_ref, dkvf_ref):
        del dproj_in

        @pl.when(pl.program_id(0) == 0)
        def _():
            dgq_ref[...] = jnp.zeros_like(dgq_ref)
            dgkv_ref[...] = jnp.zeros_like(dgkv_ref)
            dwuq_ref[...] = jnp.zeros_like(dwuq_ref)
            dwkv_ref[...] = jnp.zeros_like(dwkv_ref)

        c, s1, s2 = c_ref[...], s1_ref[...], s2_ref[...]
        lane = lax.broadcasted_iota(jnp.int32, (tm, LANE), 1)
        ksum = jnp.zeros((tm, LANE), F32)
        for h in range(HEADS):
            sl = slice(LANE * h, LANE * (h + 1))
            dqf_ref[:, sl] = (_unrope(dqr_ref[:, sl], c, s1, s2) * scale).astype(BF16)
            dkh = dkr_ref[:, sl]
            ksum = ksum + dkh
            dkvf_ref[:, sl] = jnp.where(lane < NOPE, dkh, 0.0).astype(BF16)
            dkvf_ref[:, HEADS * LANE + LANE * h:HEADS * LANE + LANE * (h + 1)] = jnp.where(
                lane < VDIM, dv_ref[:, sl], 0.0).astype(BF16)
        dkpe = _unrope(ksum, c, s1, s2)
        dc_ref[:, Q_LORA + KV_LORA:] = jnp.where((lane >= NOPE) & (lane < QK), dkpe, 0.0).astype(BF16)
        dqf, dkvf = dqf_ref[...], dkvf_ref[...]
        dwuq_ref[...] += _dot(cqt_ref[...], dqf)
        dwkv_ref[...] += _dot(ckvt_ref[...], dkvf)
        dcqn = _dot(dqf, wuqt_ref[...])
        dckvn = _dot(dkvf, wkvt_ref[...])
        for x_ref, g_ref, dn, cols, dg_ref in ((cq_ref, gq_ref, dcqn, slice(0, Q_LORA), dgq_ref),
                                               (ckv_ref, gkv_ref, dckvn, slice(Q_LORA, Q_LORA + KV_LORA), dgkv_ref)):
            xv = x_ref[...].astype(F32)
            r = lax.rsqrt(jnp.mean(xv * xv, axis=-1, keepdims=True) + EPS)
            xh = xv * r
            dg_ref[...] += jnp.sum(dn * xh, axis=0, keepdims=True)
            dh = dn * g_ref[...]
            dc_ref[:, cols] = (r * (dh - xh * jnp.mean(dh * xh, axis=-1, keepdims=True))).astype(BF16)

    row = lambda w, j: pl.BlockSpec((tm, w), lambda i: (i, j))
    full = lambda a: pl.BlockSpec(a.shape, lambda i: (0, 0))
    acc = lambda w: pl.BlockSpec((1, w), lambda i: (0, 0))
    col = lambda w: pl.BlockSpec((w, tm), lambda i: (0, i))
    whole = lambda shape: pl.BlockSpec(shape, lambda i: (0, 0))
    return pl.pallas_call(
        body,
        grid=(s // tm,),
        in_specs=[row(768, 6), row(256, 21), row(1024, 0), row(1024, 0), row(1024, 0), full(g_q), full(g_kv),
                  full(w_uq_pt), full(w_kv_pt), row(128, 0), row(128, 0), row(128, 0), col(Q_LORA), col(KV_LORA),
                  pl.BlockSpec(memory_space=pl.ANY)],
        out_specs=[row(1152, 4), acc(768), acc(256), whole((Q_LORA, HEADS * LANE)), whole((KV_LORA, 2 * HEADS * LANE))],
        out_shape=[jax.ShapeDtypeStruct(dproj.shape, BF16),
                   jax.ShapeDtypeStruct((1, 768), F32), jax.ShapeDtypeStruct((1, 256), F32),
                   jax.ShapeDtypeStruct((Q_LORA, HEADS * LANE), F32),
                   jax.ShapeDtypeStruct((KV_LORA, 2 * HEADS * LANE), F32)],
        input_output_aliases={14: 0},
        scratch_shapes=[pltpu.VMEM((tm, HEADS * LANE), BF16), pltpu.VMEM((tm, 2 * HEADS * LANE), BF16)],
        compiler_params=_params(("arbitrary",)),
        name="mla_bwd",
    )(proj, proj, dqr, dkr, dv, g_q, g_kv, w_uq_pt, w_kv_pt, rc, rs1, rs2, cqt, ckvt, dproj)


def _pick(n, options):
    for o in options:
        if n % o == 0:
            return o
    raise ValueError(n)


def _matmul(a, b, name):
    m, k = a.shape
    n = b.shape[1]
    tm = _pick(m, (1024, 768, 512, 256))
    tn = _pick(n, (1152, 1024, 768, 512))
    tk = _pick(k, (1024, 512))
    nk = k // tk

    def body(a_ref, b_ref, o_ref):
        @pl.when(pl.program_id(2) == 0)
        def _():
            o_ref[...] = jnp.zeros_like(o_ref)

        o_ref[...] += _dot(a_ref[...], b_ref[...])

    return pl.pallas_call(
        body,
        grid=(m // tm, n // tn, nk),
        in_specs=[pl.BlockSpec((tm, tk), lambda i, j, l: (i, l)), pl.BlockSpec((tk, tn), lambda i, j, l: (l, j))],
        out_specs=pl.BlockSpec((tm, tn), lambda i, j, l: (i, j)),
        out_shape=jax.ShapeDtypeStruct((m, n), F32),
        compiler_params=_params(("arbitrary", "arbitrary", "arbitrary")),
        name=name,
    )(a, b)


def _dh_dx(dproj, w_in_pt, x, dout, g_pre, sends):
    s, k = dproj.shape
    tm = 256
    ns, ni = len(sends), s // tm

    def body(dp_ref, w_ref, x_ref, dout_ref, g_ref, *rest):
        send_refs, (dx_ref, dg_ref) = rest[:ns], rest[ns:ns + 2]
        recv_refs, sems = rest[ns + 2:2 * ns + 2], rest[2 * ns + 2:]

        @pl.when(pl.program_id(0) == 0)
        def _():
            _start_all(*_to_chips_copies(send_refs, recv_refs, sems))
            dg_ref[...] = jnp.zeros_like(dg_ref)

        dh = _dot(dp_ref[...], w_ref[...])
        xv = x_ref[...]
        r = lax.rsqrt(jnp.mean(xv * xv, axis=-1, keepdims=True) + EPS)
        xh = xv * r
        dg_ref[...] += jnp.sum(dh * xh, axis=0, keepdims=True)
        dxh = dh * g_ref[...]
        dx_ref[...] = dout_ref[...] + r * (dxh - xh * jnp.mean(dxh * xh, axis=-1, keepdims=True))

        @pl.when(pl.program_id(0) == ni - 1)
        def _():
            _wait_all(*_to_chips_copies(send_refs, recv_refs, sems))

    row = lambda w: pl.BlockSpec((tm, w), lambda i: (i, 0))
    return pl.pallas_call(
        body,
        grid=(ni,),
        in_specs=[row(k), pl.BlockSpec((k, D_MODEL), lambda i: (0, 0)), row(D_MODEL), row(D_MODEL),
                  pl.BlockSpec((1, D_MODEL), lambda i: (0, 0))] + [ANY] * ns,
        out_specs=[row(D_MODEL), pl.BlockSpec((1, D_MODEL), lambda i: (0, 0))] + [ANY] * ns,
        out_shape=[jax.ShapeDtypeStruct((s, D_MODEL), F32), jax.ShapeDtypeStruct((1, D_MODEL), F32)]
        + [jax.ShapeDtypeStruct(a.shape, a.dtype) for a in sends],
        scratch_shapes=_copy_sems(ns, 3),
        compiler_params=_params(("arbitrary",)),
        name="dh_dx",
    )(dproj, w_in_pt, x, dout, g_pre, *sends)


def _pair_reduce(slots):
    n = len(slots)
    half = [(N_DEV // 2,) + a.shape[1:] for a in slots]

    def body(*refs):
        s_refs, o_refs = refs[:n], refs[n:2 * n]
        mine, got = refs[2 * n:3 * n], refs[3 * n:4 * n]
        send_sems, recv_sems, local_sems = refs[4 * n:]
        x, y, c = _my_place()
        copies, loads = [], []
        for a in range(n):
            for q in range(N_DEV // 2):
                copies.append(pltpu.make_async_remote_copy(
                    src_ref=s_refs[a].at[2 * q + 1 - c], dst_ref=got[a].at[q],
                    send_sem=send_sems.at[4 * a + q], recv_sem=recv_sems.at[4 * a + q],
                    device_id=(x, y, 1 - c), device_id_type=MESH_ID))
                loads.append(pltpu.make_async_copy(s_refs[a].at[2 * q + c], mine[a].at[q], local_sems.at[4 * a + q]))
        _start_all(loads, copies)
        _wait_all(loads, copies)
        for a in range(n):
            o_refs[a][...] = (mine[a][...].astype(F32) + got[a][...].astype(F32)).astype(o_refs[a].dtype)

    vm = lambda: [pltpu.VMEM(h, a.dtype) for h, a in zip(half, slots)]
    return pl.pallas_call(
        body,
        in_specs=[ANY] * n,
        out_shape=[jax.ShapeDtypeStruct(h, a.dtype) for h, a in zip(half, slots)],
        scratch_shapes=vm() + vm() + [pltpu.SemaphoreType.DMA((4 * n,)), pltpu.SemaphoreType.DMA((4 * n,)),
                                      pltpu.SemaphoreType.DMA((4 * n,))],
        compiler_params=pltpu.CompilerParams(vmem_limit_bytes=48 * 2**20),
        name="pair_reduce",
    )(*slots)


def _rope_tables(s):
    inv = (np.float32(ROPE_THETA) ** (-np.arange(0, ROPE, 2, dtype=np.float32) / np.float32(ROPE))).astype(np.float32)
    ang = (np.arange(s, dtype=np.float32)[:, None] * inv[None, :]).astype(np.float32)
    cos, sin = jnp.asarray(np.cos(ang.astype(np.float64)), F32), jnp.asarray(np.sin(ang.astype(np.float64)), F32)
    z = lambda w: jnp.zeros((s, w), F32)
    rc = jnp.concatenate([jnp.ones((s, NOPE), F32), cos, cos, z(32)], axis=1)
    rs1 = jnp.concatenate([z(NOPE), -sin, z(16), z(32)], axis=1)
    rs2 = jnp.concatenate([z(NOPE), z(16), sin, z(32)], axis=1)
    return rc, rs1, rs2


def _step(x, tgt, w_blk, shards, g_pre, b_gate, g_q, g_kv, lbl, g_hgrn, g_post):
    s = x.shape[0]
    rc, rs1, rs2 = _rope_tables(s)
    gh = jnp.tile(g_hgrn, (1, HEADS))

    proj, w_in_pt, ht, *got = _gather_proj(x, g_pre, w_blk, shards[:2])
    w_uq, w_ukv = (_from_slots(n, g) for n, g in zip(MATS[:2], got))
    w_uq_p = jnp.pad(w_uq.reshape(Q_LORA, HEADS, QK), ((0, 0), (0, 0), (0, LANE - QK))).reshape(Q_LORA, HEADS * LANE)
    kv3 = w_ukv.reshape(KV_LORA, HEADS, NOPE + VDIM)
    pad64 = lambda t: jnp.pad(t, ((0, 0), (0, 0), (0, LANE - 64))).reshape(KV_LORA, HEADS * LANE)
    w_kv_p = jnp.concatenate([pad64(kv3[:, :, :NOPE]), pad64(kv3[:, :, NOPE:])], axis=1)

    qr, kr, v, cqt, ckvt = _mla_prep(proj, g_q, g_kv, w_uq_p, w_kv_p, rc, rs1, rs2)
    attn, qa, *got = _attn_fwd(qr, kr, v, shards[2:])
    w_a, w_b, w_out = (_from_slots(n, g) for n, g in zip(MATS[2:], got))
    o, sprev = _hgrn_fwd(proj, lbl)
    (dout, dproj, dop, do, mt, dy_bf, yat, dya_bf, ybt, dyb_bf,
     loss_vec, dg_post, db_gate, dgh) = _tail(x, tgt, proj, attn, o, w_a, w_b, w_out, w_a.T, w_b.T, w_out.T,
                                               b_gate, g_post, gh)
    early = [_to_slots(n, _matmul(a, b, "d" + n)).astype(BF16)
             for n, a, b in (("w_branch_a", yat, dya_bf), ("w_branch_b", ybt, dyb_bf), ("w_out", mt, dy_bf))]
    dqr, dkr, dv, *early_recv = _attn_bwd(qa, kr, v, dop, early)
    dproj, dlbl = _hgrn_bwd(proj, lbl, do, sprev, dproj)
    dproj, dg_q, dg_kv, dw_uq_p, dw_kv_p = _mla_bwd(proj, dqr, dkr, dv, g_q, g_kv, w_uq_p.T, w_kv_p.T, rc, rs1, rs2,
                                                    cqt, ckvt, dproj)

    dw_in_slots = _dw_in_slots(ht, dproj)
    dw_uq = dw_uq_p.reshape(Q_LORA, HEADS, LANE)[:, :, :QK].reshape(Q_LORA, HEADS * QK)
    dw_ukv = jnp.concatenate([dw_kv_p[:, :HEADS * LANE].reshape(KV_LORA, HEADS, LANE)[:, :, :NOPE],
                              dw_kv_p[:, HEADS * LANE:].reshape(KV_LORA, HEADS, LANE)[:, :, :VDIM]],
                             axis=2).reshape(KV_LORA, 1024)
    late = _pair_reduce([dw_in_slots, _to_slots("w_uq", dw_uq).astype(BF16), _to_slots("w_ukv", dw_ukv).astype(BF16)])
    dx, dg_pre, *late_recv = _dh_dx(dproj, w_in_pt, x, dout, g_pre, late)

    g_sum = _vectors_sum(dg_pre, db_gate, dg_q, dg_kv, dlbl, dgh, dg_post, loss_vec)
    return dx, late_recv[0], dict(zip(MATS, late_recv[1:] + early_recv)), g_sum


def _adamw(g, w, m, v):
    c1 = 1.0 / (1.0 - ADAM_B1 ** ADAM_STEP)
    c2 = 1.0 / (1.0 - ADAM_B2 ** ADAM_STEP)
    nm = ADAM_B1 * m + (1.0 - ADAM_B1) * g
    nv = ADAM_B2 * v + (1.0 - ADAM_B2) * (g * g)
    d = -ADAM_LR * ((nm * c1) / (jnp.sqrt(nv * c2) + ADAM_EPS) + ADAM_WD * w)
    return d, nm, nv


def _sum8(r_ref):
    g = r_ref[0].astype(F32)
    for k in range(1, r_ref.shape[0]):
        g = g + r_ref[k].astype(F32)
    return g


def _sum_adamw_w_in(recv, w, m, v):
    rows, _, cols = w.shape
    tc = 256
    nc = cols // tc

    def body(r_ref, w_hbm, m_hbm, v_hbm, g_hbm, d_hbm, nm_hbm, nv_hbm, ins, outs, in_sems, out_sems):
        i = pl.program_id(0)
        slot = i & 1
        cols_of = lambda step: pl.ds(pl.multiple_of(step * tc, tc), tc)

        def load(k, step, sl):
            return pltpu.make_async_copy((w_hbm, m_hbm, v_hbm)[k].at[:, 0, cols_of(step)], ins.at[sl, k],
                                         in_sems.at[sl, k])

        def store(k, step, sl):
            return pltpu.make_async_copy(outs.at[sl, k], (g_hbm, d_hbm, nm_hbm, nv_hbm)[k].at[:, 0, cols_of(step)],
                                         out_sems.at[sl, k])

        @pl.when(i == 0)
        def _():
            for k in range(3):
                load(k, 0, 0).start()

        @pl.when(i + 1 < nc)
        def _():
            for k in range(3):
                load(k, i + 1, 1 - slot).start()

        @pl.when(i >= 2)
        def _():
            for k in range(4):
                store(k, i - 2, slot).wait()

        for k in range(3):
            load(k, i, slot).wait()
        g = _sum8(r_ref)
        d, nm, nv = _adamw(g, ins[slot, 0], ins[slot, 1], ins[slot, 2])
        for k, val in enumerate((g, d, nm, nv)):
            outs[slot, k] = val
        for k in range(4):
            store(k, i, slot).start()

        @pl.when(i == nc - 1)
        def _():
            for k in range(4):
                store(k, i, slot).wait()
            if nc >= 2:
                for k in range(4):
                    store(k, i - 1, 1 - slot).wait()

    out = jax.ShapeDtypeStruct((rows, 1, cols), F32)
    return pl.pallas_call(
        body,
        grid=(nc,),
        in_specs=[pl.BlockSpec((recv.shape[0], rows, tc), lambda i: (0, 0, i)), ANY, ANY, ANY],
        out_specs=[ANY, ANY, ANY, ANY],
        out_shape=[out, out, out, out],
        scratch_shapes=[pltpu.VMEM((2, 3, rows, tc), F32), pltpu.VMEM((2, 4, rows, tc), F32),
                        pltpu.SemaphoreType.DMA((2, 3)), pltpu.SemaphoreType.DMA((2, 4))],
        compiler_params=_params(("arbitrary",)),
        name="sum_adamw_w_in",
    )(recv, w, m, v)


def _sum_adamw_whole(recvs, ws, ms, vs):
    n = len(ws)

    def body(*refs):
        r_refs, w_refs, m_refs, v_refs = refs[:n], refs[n:2 * n], refs[2 * n:3 * n], refs[3 * n:4 * n]
        outs = refs[4 * n:]
        for a in range(n):
            g = _sum8(r_refs[a])
            d, nm, nv = _adamw(g, w_refs[a][...], m_refs[a][...], v_refs[a][...])
            outs[a][...] = g
            outs[n + a][...] = d
            outs[2 * n + a][...] = nm
            outs[3 * n + a][...] = nv

    shapes = [jax.ShapeDtypeStruct(w.shape, F32) for w in ws]
    res = pl.pallas_call(
        body,
        out_shape=shapes * 4,
        compiler_params=pltpu.CompilerParams(vmem_limit_bytes=48 * 2**20),
        name="sum_adamw_mats",
    )(*recvs, *ws, *ms, *vs)
    return res[:n], res[n:2 * n], res[2 * n:3 * n], res[3 * n:]


SMALL = ("g_pre", "b_gate", "g_q", "g_kv", "lb_logits", "g_hgrn", "g_post")
SMALL_SHAPE = dict(g_pre=(1, 1024), b_gate=(1, 2048), g_q=(1, 768), g_kv=(1, 256), lb_logits=(2, 512),
                   g_hgrn=(1, 64), g_post=(1, 1024))


def _vectors_sum(dg_pre, db_gate, dg_q, dg_kv, dlbl, dgh, dg_post, loss_vec):
    def body(gpre_ref, bg_ref, gq_ref, gkv_ref, lbl_ref, gh_ref, gpost_ref, loss_ref, out_ref, mine, got,
             send_sems, recv_sems):
        mine[...] = jnp.zeros_like(mine)
        mine[0:1, :] = gpre_ref[...]
        mine[1:2, :] = bg_ref[:, :1024]
        mine[2:3, :] = bg_ref[:, 1024:]
        mine[3:4, :Q_LORA] = gq_ref[...]
        mine[4:5, :KV_LORA] = gkv_ref[...]
        loss = (0.5 / D_MODEL) * jnp.sum(loss_ref[...], axis=-1, keepdims=True)
        mine[4:5, KV_LORA:] = jnp.broadcast_to(loss, (1, 1024 - KV_LORA))
        mine[5:6, :HG_WIDTH] = lbl_ref[0:1, :]
        mine[5:6, HG_WIDTH:] = lbl_ref[1:2, :]
        gh = gh_ref[...]
        fold = gh[:, :VDIM]
        for h in range(1, HEADS):
            fold = fold + gh[:, VDIM * h:VDIM * (h + 1)]
        mine[6:7, :VDIM] = fold
        mine[7:8, :] = gpost_ref[...]
        x, y, c = _my_place()
        me = 4 * x + 2 * y + c
        got[me] = mine[...]
        copies = [pltpu.make_async_remote_copy(
            src_ref=mine, dst_ref=got.at[me], send_sem=send_sems.at[k], recv_sem=recv_sems.at[k],
            device_id=_flip(k, x, y, c), device_id_type=MESH_ID) for k in range(N_DEV - 1)]
        _start_all([], copies)
        _wait_all([], copies)
        out_ref[...] = _sum8(got)

    return pl.pallas_call(
        body,
        out_shape=jax.ShapeDtypeStruct((8, 1024), F32),
        scratch_shapes=[pltpu.VMEM((8, 1024), F32), pltpu.VMEM((N_DEV, 8, 1024), F32),
                        pltpu.SemaphoreType.DMA((7,)), pltpu.SemaphoreType.DMA((7,))],
        name="vectors_sum",
    )(dg_pre, db_gate, dg_q, dg_kv, dlbl, dgh, dg_post, loss_vec)


def _vectors_adamw(g_sum, ws, ms, vs):
    n = len(SMALL)

    def body(g_ref, *refs):
        w_refs, m_refs, v_refs = refs[:n], refs[n:2 * n], refs[2 * n:3 * n]
        loss_ref, outs = refs[3 * n], refs[3 * n + 1:]
        g = g_ref[...]
        loss_ref[...] = g[4:5, KV_LORA:KV_LORA + 1]
        grads = (g[0:1, :], jnp.concatenate([g[1:2, :], g[2:3, :]], axis=1), g[3:4, :Q_LORA], g[4:5, :KV_LORA],
                 jnp.concatenate([g[5:6, :HG_WIDTH], g[5:6, HG_WIDTH:]], axis=0), g[6:7, :VDIM], g[7:8, :])
        for a in range(n):
            d, nm, nv = _adamw(grads[a], w_refs[a][...], m_refs[a][...], v_refs[a][...])
            outs[a][...] = grads[a]
            outs[n + a][...] = d
            outs[2 * n + a][...] = nm
            outs[3 * n + a][...] = nv

    shapes = [jax.ShapeDtypeStruct(SMALL_SHAPE[k], F32) for k in SMALL]
    res = pl.pallas_call(
        body,
        out_shape=[jax.ShapeDtypeStruct((1, 1), F32)] + shapes * 4,
        name="vectors_adamw",
    )(g_sum, *ws, *ms, *vs)
    return res[0], res[1:n + 1], res[n + 1:2 * n + 1], res[2 * n + 1:3 * n + 1], res[3 * n + 1:]


MATS = ("w_uq", "w_ukv", "w_branch_a", "w_branch_b", "w_out")
COL_SHARDED = dict(w_uq=False, w_ukv=True, w_branch_a=True, w_branch_b=True, w_out=False)
ORDER = ("g_pre", "w_in", "b_gate", "g_q", "w_uq", "g_kv", "w_ukv", "lb_logits", "g_hgrn",
         "w_branch_a", "w_branch_b", "w_out", "g_post")


def _to_slots(name, full):
    r, c = full.shape
    if COL_SHARDED[name]:
        return full.reshape(r, N_DEV, c // N_DEV).transpose(1, 0, 2)
    return full.reshape(N_DEV, r // N_DEV, c)


def _from_slots(name, slots):
    _, r, c = slots.shape
    if COL_SHARDED[name]:
        return slots.transpose(1, 0, 2).reshape(r, N_DEV * c)
    return slots.reshape(N_DEV * r, c)


def kernel(x, g_pre, w_in, b_gate, g_q, w_uq, g_kv, w_ukv, lb_logits, g_hgrn, w_branch_a, w_branch_b, w_out, g_post, loss_target, m_g_pre, m_w_in, m_b_gate, m_g_q, m_w_uq, m_g_kv, m_w_ukv, m_lb_logits, m_g_hgrn, m_w_branch_a, m_w_branch_b, m_w_out, m_g_post, v_g_pre, v_w_in, v_b_gate, v_g_q, v_w_uq, v_g_kv, v_w_ukv, v_lb_logits, v_g_hgrn, v_w_branch_a, v_w_branch_b, v_w_out, v_g_post):
    rows3 = lambda a: jnp.transpose(a, (2, 0, 1))
    w = dict(w_in=rows3(w_in), w_uq=w_uq[0], w_ukv=w_ukv[0], w_branch_a=w_branch_a[0], w_branch_b=w_branch_b[0],
             w_out=w_out[0], g_pre=g_pre, b_gate=b_gate, g_q=g_q, g_kv=g_kv, lb_logits=lb_logits, g_hgrn=g_hgrn,
             g_post=g_post)
    mom = dict(w_in=rows3(m_w_in), w_uq=m_w_uq[0], w_ukv=m_w_ukv[0], w_branch_a=m_w_branch_a[0],
               w_branch_b=m_w_branch_b[0], w_out=m_w_out[0], g_pre=m_g_pre, b_gate=m_b_gate, g_q=m_g_q, g_kv=m_g_kv,
               lb_logits=m_lb_logits, g_hgrn=m_g_hgrn, g_post=m_g_post)
    var = dict(w_in=rows3(v_w_in), w_uq=v_w_uq[0], w_ukv=v_w_ukv[0], w_branch_a=v_w_branch_a[0],
               w_branch_b=v_w_branch_b[0], w_out=v_w_out[0], g_pre=v_g_pre, b_gate=v_b_gate, g_q=v_g_q, g_kv=v_g_kv,
               lb_logits=v_lb_logits, g_hgrn=v_g_hgrn, g_post=v_g_post)

    w_blk = w["w_in"].reshape(W_IN_SHARD, D_MODEL).astype(BF16)
    dx, recv_in, recv, g_sum = _step(x[0], loss_target[0], w_blk, [w[n].astype(BF16) for n in MATS],
                                     g_pre, b_gate, g_q, g_kv, lb_logits, g_hgrn, g_post)

    g_in, d_in, m_in, v_in = _sum_adamw_w_in(recv_in, w["w_in"], mom["w_in"], var["w_in"])
    res = _sum_adamw_whole([recv[n] for n in MATS], *([t[n] for n in MATS] for t in (w, mom, var)))
    total, *vec = _vectors_adamw(g_sum, *([t[n] for n in SMALL] for t in (w, mom, var)))

    outs = []
    for mats, vecs, big in zip(res, vec, (g_in, d_in, m_in, v_in)):
        t = {**{n: a[None] for n, a in zip(MATS, mats)}, **dict(zip(SMALL, vecs)),
             "w_in": jnp.transpose(big, (1, 2, 0))}
        outs += [t[n] for n in ORDER]
    return (total.reshape(()), dx[None], *outs)
```

```python
import math

import jax
import jax.numpy as jnp
import numpy as np
from jax import lax
from jax.experimental import pallas as pl
from jax.experimental.pallas import tpu as pltpu

F32, BF16 = jnp.float32, jnp.bfloat16

D_MODEL = 1024
EPS = 1e-6
HEADS = 8
NOPE, ROPE, VDIM = 64, 32, 64
QK = NOPE + ROPE
Q_LORA, KV_LORA = 768, 256
ROPE_THETA = 10000.0
ATT_CHUNK_SHIFT = 6
HG_BLOCK = 32
HG_WIDTH = 512
D_IN = 5664
D_IN_PAD = 5760
W_IN_SHARD = D_IN // 8
N_DEV = 8
LANE = 128

ADAM_LR, ADAM_B1, ADAM_B2, ADAM_EPS, ADAM_WD, ADAM_STEP = 0.001, 0.9, 0.999, 1e-08, 0.01, 10

W_IN_SEGMENTS = ((3616, 5664, 0), (1056, 1568, 2048), (3104, 3616, 2560), (1568, 3104, 3072),
                 (0, 1024, 4608), (1024, 1056, 5696))

NT = (((1,), (1,)), ((), ()))
TN = (((0,), (0,)), ((), ()))
MESH_ID = pl.DeviceIdType.MESH


def _w_in_pieces():
    out = []
    for lo, hi, dst in W_IN_SEGMENTS:
        c = lo
        while c < hi:
            p = c // W_IN_SHARD
            e = min(hi, (p + 1) * W_IN_SHARD)
            out.append((p, c - p * W_IN_SHARD, e - p * W_IN_SHARD, dst + c - lo))
            c = e
    return out


def _params(sem, vmem_mb=48):
    return pltpu.CompilerParams(dimension_semantics=sem, vmem_limit_bytes=vmem_mb * 2**20)


def _dot(a, b):
    return jnp.dot(a, b, preferred_element_type=F32)


def _dotg(a, b, dims):
    return lax.dot_general(a, b, dims, preferred_element_type=F32)


def _split2(x):
    hi = x.astype(BF16)
    return hi, (x - hi.astype(F32)).astype(BF16)


def _sel_left(m01, x):
    hi, lo = _split2(x)
    return _dot(m01, hi) + _dot(m01, lo)


def _sel_right(x, m01):
    hi, lo = _split2(x)
    return _dot(hi, m01) + _dot(lo, m01)


def _hi_lo(x):
    hi = x.astype(BF16).astype(F32)
    return hi, x - hi


def _sigmoid(x):
    return 0.5 * jnp.tanh(0.5 * x) + 0.5


def _rope(x, c, s1, s2):
    return x * c + pltpu.roll(x, 112, 1) * s1 + pltpu.roll(x, 16, 1) * s2


def _unrope(d, c, s1, s2):
    return d * c + pltpu.roll(d * s1, 16, 1) + pltpu.roll(d * s2, 112, 1)


def _my_place():
    return lax.axis_index("x"), lax.axis_index("y"), lax.axis_index("c")


def _flip(k, x, y, c):
    fx, fy, fc = (k + 1) >> 2 & 1, (k + 1) >> 1 & 1, (k + 1) & 1
    return (1 - x if fx else x), (1 - y if fy else y), (1 - c if fc else c)


def _to_all_copies(s_refs, r_refs, sems, spread):
    send_sems, recv_sems, local_sems = sems
    x, y, c = _my_place()
    me = 4 * x + 2 * y + c
    src = (lambda a, p: s_refs[a]) if spread else (lambda a, p: s_refs[a].at[p])
    local = [pltpu.make_async_copy(src(a, me), r_refs[a].at[me], local_sems.at[a]) for a in range(len(s_refs))]
    remote = []
    for k in range(N_DEV - 1):
        px, py, pc = _flip(k, x, y, c)
        for a in range(len(s_refs)):
            remote.append(pltpu.make_async_remote_copy(
                src_ref=src(a, 4 * px + 2 * py + pc), dst_ref=r_refs[a].at[me],
                send_sem=send_sems.at[7 * a + k], recv_sem=recv_sems.at[7 * a + k],
                device_id=(px, py, pc), device_id_type=MESH_ID))
    return local, remote


def _to_chips_copies(s_refs, r_refs, sems):
    send_sems, recv_sems, local_sems = sems
    x, y, c = _my_place()
    me = 2 * x + y
    local = [pltpu.make_async_copy(s_refs[a].at[me], r_refs[a].at[me], local_sems.at[a]) for a in range(len(s_refs))]
    remote = []
    for k in range(3):
        px = 1 - x if (k + 1) >> 1 & 1 else x
        py = 1 - y if (k + 1) & 1 else y
        for a in range(len(s_refs)):
            remote.append(pltpu.make_async_remote_copy(
                src_ref=s_refs[a].at[2 * px + py], dst_ref=r_refs[a].at[me],
                send_sem=send_sems.at[3 * a + k], recv_sem=recv_sems.at[3 * a + k],
                device_id=(px, py, c), device_id_type=MESH_ID))
    return local, remote


def _start_all(local, remote):
    for cp in local + remote:
        cp.start()


def _wait_all(local, remote):
    for cp in remote:
        cp.wait_recv()
    for cp in remote:
        cp.wait_send()
    for cp in local:
        cp.wait()


def _copy_sems(n, peers):
    return [pltpu.SemaphoreType.DMA((peers * n,)), pltpu.SemaphoreType.DMA((peers * n,)),
            pltpu.SemaphoreType.DMA((n,))]


ANY = pl.BlockSpec(memory_space=pl.ANY)


def _dw_in_slots(ht, dproj):
    m, k = ht.shape
    n = dproj.shape[1]
    tn, tk = 1152, 1024
    nj, nk = n // tn, k // tk
    by_tile = [[] for _ in range(nj)]
    for p, lo, hi, dst in _w_in_pieces():
        while lo < hi:
            j = dst // tn
            cnt = min(hi - lo, (j + 1) * tn - dst)
            by_tile[j].append((p, lo, lo + cnt, dst - j * tn))
            lo, dst = lo + cnt, dst + cnt

    def body(a_ref, b_ref, s_ref, acc_ref):
        j, l = pl.program_id(0), pl.program_id(1)

        @pl.when(l == 0)
        def _():
            acc_ref[...] = jnp.zeros_like(acc_ref)

        acc_ref[...] += _dot(a_ref[...], b_ref[...])

        @pl.when(l == nk - 1)
        def _():
            at = acc_ref[...].T
            for jj in range(nj):
                @pl.when(j == jj)
                def _(jj=jj):
                    for p, lo, hi, d in by_tile[jj]:
                        s_ref[p, lo:hi, :] = at[d:d + hi - lo, :].astype(BF16)

    return pl.pallas_call(
        body,
        grid=(nj, nk),
        in_specs=[pl.BlockSpec((m, tk), lambda j, l: (0, l)), pl.BlockSpec((tk, tn), lambda j, l: (l, j))],
        out_specs=pl.BlockSpec((N_DEV, W_IN_SHARD, m), lambda j, l: (0, 0, 0)),
        out_shape=jax.ShapeDtypeStruct((N_DEV, W_IN_SHARD, m), BF16),
        scratch_shapes=[pltpu.VMEM((m, tn), F32)],
        compiler_params=_params(("arbitrary", "arbitrary")),
        name="dw_in",
    )(ht, dproj)


PROJ_DT = F32
GP_TN = 256
GP_COLS = 5888
GP_NT = GP_COLS // GP_TN


def _gp_tile_pieces():
    tiles = [[] for _ in range(GP_NT)]
    for p, lo, hi, dst in _w_in_pieces():
        while lo < hi:
            t = dst // GP_TN
            n = min(hi - lo, (t + 1) * GP_TN - dst)
            tiles[t].append((p, lo, lo + n, dst - t * GP_TN))
            lo, dst = lo + n, dst + n
    return tiles


def _gp_tables():
    pieces = _gp_tile_pieces()
    rank_of = {None: 0, 0: 1, 1: 2, 2: 2, 4: 3, 5: 3, 3: 4, 6: 5}
    order = np.zeros((N_DEV, GP_NT), np.int32)
    waits = np.zeros((N_DEV, GP_NT), np.int32)
    for me in range(N_DEV):
        x, y, c = me >> 2 & 1, me >> 1 & 1, me & 1
        chips = [(1 - x, y), (x, 1 - y), (1 - x, 1 - y)]

        def sem_of(p):
            px, py, pc = p >> 2 & 1, p >> 1 & 1, p & 1
            if (px, py) == (x, y):
                return None if pc == c else 0
            j = chips.index((px, py))
            return 1 + j if pc == c else 4 + j

        needs = [sorted({sem_of(p) for p, _, _, _ in tile} - {None}) for tile in pieces]
        ranks = [max([rank_of[k] for k in ks], default=0) for ks in needs]
        seq = sorted(range(GP_NT), key=lambda t: (ranks[t], t))
        seen = set()
        for step, t in enumerate(seq):
            order[me, step] = t
            new = [k for k in needs[t] if k not in seen]
            for k in new:
                waits[me, step] |= 1 << k
            seen.update(new)
        assert seen == set(range(7)), (me, seen)
    return order, waits


def _gather_proj(x, g_pre, w_blk, shards):
    s = x.shape[0]
    tx = 512
    ns = len(shards)
    tile_pieces = _gp_tile_pieces()
    order_np, waits_np = _gp_tables()
    xq, yq, cq = _my_place()
    me_out = 4 * xq + 2 * yq + cq
    order = lax.dynamic_index_in_dim(jnp.asarray(order_np), me_out, 0, keepdims=False)
    waits = lax.dynamic_index_in_dim(jnp.asarray(waits_np), me_out, 0, keepdims=False)

    def body(order_ref, waits_ref, x_hbm, g_ref, wblk_hbm, *rest):
        shard_refs, (proj_ref, wt_ref, ht_hbm), got_refs = rest[:ns], rest[ns:ns + 3], rest[ns + 3:2 * ns + 3]
        recv, h_ref, wtile, xbuf, htbuf = rest[2 * ns + 3:2 * ns + 8]
        send_sems, recv_sems, misc_sems = rest[2 * ns + 8:2 * ns + 11]
        sems = rest[2 * ns + 11:]
        t = pl.program_id(0)
        x_, y_, c = _my_place()
        sibling = (x_, y_, 1 - c)
        chips = [(1 - x_, y_), (x_, 1 - y_), (1 - x_, 1 - y_)]
        idx = lambda px, py, pc: 4 * px + 2 * py + pc
        me = idx(x_, y_, c)

        def copy(k, slot, to, src=None):
            return pltpu.make_async_remote_copy(
                src_ref=recv.at[slot] if src is None else src, dst_ref=recv.at[slot],
                send_sem=send_sems.at[k], recv_sem=recv_sems.at[k], device_id=to, device_id_type=MESH_ID)

        mine = pltpu.make_async_copy(wblk_hbm, recv.at[me], misc_sems.at[0])
        first = [copy(0, me, sibling, src=wblk_hbm)] + [copy(1 + j, me, (*chips[j], c), src=wblk_hbm) for j in range(2)]
        passed = [copy(4 + j, idx(*ch, c), sibling) for j, ch in enumerate(chips)]
        onward = [copy(3, idx(*chips[0], c), (*chips[1], c)), copy(3, idx(*chips[1], c), (*chips[0], c))]
        arrivals = ([copy(0, idx(x_, y_, 1 - c), sibling)] + [copy(1 + j, idx(*ch, c), sibling) for j, ch in enumerate(chips)]
                    + [copy(4 + j, idx(*ch, 1 - c), sibling) for j, ch in enumerate(chips)])

        @pl.when(t == 0)
        def _():
            mine.start()
            for cp in first:
                cp.start()
            _start_all(*_to_all_copies(shard_refs, got_refs, sems, True))

            def load(i):
                return pltpu.make_async_copy(x_hbm.at[pl.ds(i * tx, tx), :], xbuf.at[i & 1], misc_sems.at[1 + (i & 1)])

            def store(i):
                return pltpu.make_async_copy(htbuf.at[i & 1], ht_hbm.at[:, pl.ds(i * tx, tx)], misc_sems.at[3 + (i & 1)])

            load(0).start()
            for i in range(s // tx):
                if i + 1 < s // tx:
                    load(i + 1).start()
                load(i).wait()
                xv = xbuf[i & 1]
                r = lax.rsqrt(jnp.mean(xv * xv, axis=-1, keepdims=True) + EPS)
                h = (xv * r * g_ref[...]).astype(BF16)
                h_ref[i * tx:(i + 1) * tx, :] = h
                if i >= 2:
                    store(i - 2).wait()
                htbuf[i & 1] = h.T
                store(i).start()
            for i in range(max(s // tx - 2, 0), s // tx):
                store(i).wait()
            mine.wait()

        w = waits_ref[t]
        for k in range(7):
            @pl.when((w >> k) & 1 == 1)
            def _(k=k):
                arrivals[k].wait_recv()
                if 1 <= k <= 3:
                    passed[k - 1].start()
                if 1 <= k <= 2:
                    @pl.when(c == k - 1)
                    def _():
                        onward[k - 1].start()

        tile = order_ref[t]
        for tt in range(GP_NT):
            @pl.when(tile == tt)
            def _(tt=tt):
                covered = sorted((d, d + hi - lo) for _, lo, hi, d in tile_pieces[tt])
                at = 0
                for lo_z, hi_z in covered + [(GP_TN, GP_TN)]:
                    if lo_z > at:
                        wtile[at:lo_z, :] = jnp.zeros((lo_z - at, D_MODEL), BF16)
                    at = max(at, hi_z)
                for p, lo, hi, d in tile_pieces[tt]:
                    wtile[d:d + hi - lo, :] = recv[p, lo:hi, :]

        wt = wtile[...]
        wt_ref[...] = wt
        proj_ref[...] = _dotg(h_ref[...], wt, NT).astype(PROJ_DT)

        @pl.when(t == GP_NT - 1)
        def _():
            for cp in first + passed + onward[:1]:
                cp.wait_send()
            _wait_all(*_to_all_copies(shard_refs, got_refs, sems, True))

    grid_spec = pltpu.PrefetchScalarGridSpec(
        num_scalar_prefetch=2,
        grid=(GP_NT,),
        in_specs=[ANY, pl.BlockSpec((1, D_MODEL), lambda t, o, w: (0, 0)), ANY] + [ANY] * ns,
        out_specs=[pl.BlockSpec((s, GP_TN), lambda t, o, w: (0, o[t])),
                   pl.BlockSpec((GP_TN, D_MODEL), lambda t, o, w: (o[t], 0)), ANY] + [ANY] * ns,
        scratch_shapes=[pltpu.VMEM((N_DEV, W_IN_SHARD, D_MODEL), BF16), pltpu.VMEM((s, D_MODEL), BF16),
                        pltpu.VMEM((GP_TN, D_MODEL), BF16), pltpu.VMEM((2, tx, D_MODEL), F32),
                        pltpu.VMEM((2, D_MODEL, tx), BF16),
                        pltpu.SemaphoreType.DMA((7,)), pltpu.SemaphoreType.DMA((7,)), pltpu.SemaphoreType.DMA((5,))]
        + _copy_sems(ns, 7),
    )
    return pl.pallas_call(
        body,
        grid_spec=grid_spec,
        out_shape=[jax.ShapeDtypeStruct((s, GP_COLS), PROJ_DT), jax.ShapeDtypeStruct((GP_COLS, D_MODEL), BF16),
                   jax.ShapeDtypeStruct((D_MODEL, s), BF16)]
        + [jax.ShapeDtypeStruct((N_DEV,) + b.shape, b.dtype) for b in shards],
        compiler_params=_params(("arbitrary",), 56),
        name="gather_proj",
    )(order, waits, x, g_pre, w_blk, *shards)


def _mla_prep(proj, g_q, g_kv, w_uq_p, w_kv_p, rc, rs1, rs2):
    s = proj.shape[0]
    tm = 256
    scale = 1.0 / math.sqrt(QK)

    def body(cq_ref, ckv_ref, kpe_ref, gq_ref, gkv_ref, wuq_ref, wkv_ref, c_ref, s1_ref, s2_ref,
             qr_ref, kr_ref, v_ref, cqt_ref, ckvt_ref):
        cq = cq_ref[...].astype(F32)
        r = lax.rsqrt(jnp.mean(cq * cq, axis=-1, keepdims=True) + EPS)
        cqn = (cq * r * gq_ref[...]).astype(BF16)
        cqt_ref[...] = cqn.T
        q = _dot(cqn, wuq_ref[...])
        ckv = ckv_ref[...].astype(F32)
        r = lax.rsqrt(jnp.mean(ckv * ckv, axis=-1, keepdims=True) + EPS)
        ckvn = (ckv * r * gkv_ref[...]).astype(BF16)
        ckvt_ref[...] = ckvn.T
        kv = _dot(ckvn, wkv_ref[...])
        c, s1, s2 = c_ref[...], s1_ref[...], s2_ref[...]
        lane = lax.broadcasted_iota(jnp.int32, (tm, LANE), 1)
        kpe = _rope(kpe_ref[...].astype(F32), c, s1, s2) + jnp.where((lane == QK) | (lane == QK + 1), 1.0, 0.0)
        vone = jnp.where((lane == VDIM) | (lane == VDIM + 1), 1.0, 0.0)
        for h in range(HEADS):
            sl = slice(LANE * h, LANE * (h + 1))
            qr_ref[:, sl] = (_rope(q[:, sl], c, s1, s2) * scale).astype(BF16)
            kr_ref[:, sl] = (kv[:, sl] + kpe).astype(BF16)
            v_ref[:, sl] = (kv[:, HEADS * LANE + LANE * h:HEADS * LANE + LANE * (h + 1)] + vone).astype(BF16)

    row = lambda w, j: pl.BlockSpec((tm, w), lambda i: (i, j))
    col = lambda w: pl.BlockSpec((w, tm), lambda i: (0, i))
    full = lambda a: pl.BlockSpec(a.shape, lambda i: (0, 0))
    return pl.pallas_call(
        body,
        grid=(s // tm,),
        in_specs=[row(768, 6), row(256, 21), row(128, 44), full(g_q), full(g_kv), full(w_uq_p), full(w_kv_p),
                  row(128, 0), row(128, 0), row(128, 0)],
        out_specs=[row(1024, 0), row(1024, 0), row(1024, 0), col(768), col(256)],
        out_shape=[jax.ShapeDtypeStruct((s, 1024), BF16), jax.ShapeDtypeStruct((s, 1024), BF16),
                   jax.ShapeDtypeStruct((s, 1024), BF16), jax.ShapeDtypeStruct((768, s), BF16),
                   jax.ShapeDtypeStruct((256, s), BF16)],
        compiler_params=_params(("arbitrary",)),
        name="mla_prep",
    )(proj, proj, proj, g_q, g_kv, w_uq_p, w_kv_p, rc, rs1, rs2)


ATT_T = 512
ATT_FWD_HEADS = 4


def _chunk_mask(transposed):
    r = lax.broadcasted_iota(jnp.int32, (ATT_T, ATT_T), 0) >> ATT_CHUNK_SHIFT
    c = lax.broadcasted_iota(jnp.int32, (ATT_T, ATT_T), 1) >> ATT_CHUNK_SHIFT
    return (r <= c) if transposed else (c <= r)


def _attn_fwd(qr, kr, vp, shards):
    s = qr.shape[0]
    t = ATT_T
    g = ATT_FWD_HEADS
    ns = len(shards)

    def body(q_ref, k_ref, v_ref, *rest):
        shard_refs, (o_ref, qa_ref), got_refs = rest[:ns], rest[ns:ns + 2], rest[ns + 2:2 * ns + 2]
        sc_ref, sems = rest[2 * ns + 2], rest[2 * ns + 3:]
        qi = pl.program_id(1)

        @pl.when((pl.program_id(0) == 0) & (qi == 0))
        def _():
            _start_all(*_to_all_copies(shard_refs, got_refs, sems, True))
        lane = lax.broadcasted_iota(jnp.int32, (t, LANE), 1)
        sls = [slice(LANE * a, LANE * (a + 1)) for a in range(g)]
        qs = [q_ref[:, sl] for sl in sls]

        def scores(j):
            rows = pl.ds(pl.multiple_of(j * t, t), t)
            for a in range(g):
                sc_ref[j & 1, a] = _dotg(qs[a], k_ref[rows, sls[a]], NT)

        def step(j, carry, masked):
            rows = pl.ds(pl.multiple_of(j * t, t), t)
            out = []
            for a in range(g):
                m, acc = carry[a]
                sc = sc_ref[j & 1, a]
                if masked:
                    sc = jnp.where(_chunk_mask(False), sc, -1e30)
                m_new = jnp.maximum(m, jnp.max(sc, axis=-1, keepdims=True))
                p = jnp.exp(sc - m_new).astype(BF16)
                acc = jnp.exp(m - m_new) * acc + _dot(p, v_ref[rows, sls[a]])
                out.append((m_new, acc))
            return tuple(out)

        def loop(j, carry):
            carry = step(j, carry, False)
            scores(j + 1)
            return carry

        init = tuple((jnp.full((t, 1), -1e30, F32), jnp.zeros((t, LANE), F32)) for _ in range(g))
        scores(0)
        carry = lax.fori_loop(0, qi, loop, init)
        carry = step(qi, carry, True)
        outs = []
        for a in range(g):
            m, acc = carry[a]
            l = acc[:, VDIM:VDIM + 1]
            outs.append(acc / l)
            hi, lo_part = _hi_lo(-(m + jnp.log(l)))
            qa = jnp.where(lane == QK, hi, jnp.where(lane == QK + 1, lo_part, qs[a].astype(F32)))
            qa_ref[:, sls[a]] = qa.astype(BF16)
        for p in range(g // 2):
            o_ref[:, LANE * p:LANE * (p + 1)] = jnp.where(lane < VDIM, outs[2 * p], pltpu.roll(outs[2 * p + 1], VDIM, 1))

        @pl.when((pl.program_id(0) == HEADS // g - 1) & (qi == s // t - 1))
        def _():
            _wait_all(*_to_all_copies(shard_refs, got_refs, sems, True))

    return pl.pallas_call(
        body,
        grid=(HEADS // g, s // t),
        in_specs=[
            pl.BlockSpec((t, g * LANE), lambda h, i: (i, h)),
            pl.BlockSpec((s, g * LANE), lambda h, i: (0, h)),
            pl.BlockSpec((s, g * LANE), lambda h, i: (0, h)),
        ] + [ANY] * ns,
        out_specs=[
            pl.BlockSpec((t, g * VDIM), lambda h, i: (i, h)),
            pl.BlockSpec((t, g * LANE), lambda h, i: (i, h)),
        ] + [ANY] * ns,
        out_shape=[jax.ShapeDtypeStruct((s, 512), F32), jax.ShapeDtypeStruct((s, 1024), BF16)]
        + [jax.ShapeDtypeStruct((N_DEV,) + b.shape, b.dtype) for b in shards],
        scratch_shapes=[pltpu.VMEM((2, g, t, t), F32)] + _copy_sems(ns, 7),
        compiler_params=_params(("arbitrary", "arbitrary")),
        name="attn_fwd",
    )(qr, kr, vp, *shards)


def _attn_bwd(qa, kr, vp, dop, sends):
    s = qa.shape[0]
    t = ATT_T
    nq = s // t
    ns = len(sends)

    def body(q_ref, k_ref, v_ref, do_ref, *rest):
        send_refs, (dq_out, dk_out, dv_out) = rest[:ns], rest[ns:ns + 3]
        recv_refs = rest[ns + 3:2 * ns + 3]
        (dq_ref, dk_ref, dv_ref), sems = rest[2 * ns + 3:2 * ns + 6], rest[2 * ns + 6:]
        j = pl.program_id(1)
        sls = [slice(LANE * a, LANE * (a + 1)) for a in range(2)]

        @pl.when((pl.program_id(0) == 0) & (j == 0))
        def _():
            _start_all(*_to_all_copies(send_refs, recv_refs, sems, False))

        @pl.when(j == 0)
        def _():
            dq_ref[...] = jnp.zeros_like(dq_ref)

        dk_ref[...] = jnp.zeros_like(dk_ref)
        dv_ref[...] = jnp.zeros_like(dv_ref)
        ks = [k_ref[:, sl] for sl in sls]
        vs = [v_ref[:, sl] for sl in sls]

        def part(i, k_lo, k_n, q_lo, q_n, masked):
            rows = pl.ds(pl.multiple_of(i * t + q_lo, 256), q_n)
            keys = slice(k_lo, k_lo + k_n)
            for a in range(2):
                q = q_ref[rows, sls[a]]
                do = do_ref[rows, sls[a]]
                sc = _dotg(ks[a][keys], q, NT)
                if masked:
                    kc = lax.broadcasted_iota(jnp.int32, (k_n, q_n), 0) >> ATT_CHUNK_SHIFT
                    qc = lax.broadcasted_iota(jnp.int32, (k_n, q_n), 1) >> ATT_CHUNK_SHIFT
                    sc = jnp.where(kc <= qc, sc, -1e30)
                p = jnp.exp(sc)
                ds = (p * _dotg(vs[a][keys], do, NT)).astype(BF16)
                dv_ref[keys, sls[a]] += _dot(p.astype(BF16), do)
                dk_ref[keys, sls[a]] += _dot(ds, q)
                dq_ref[rows, sls[a]] += _dotg(ds, ks[a][keys], TN)

        half = t // 2
        part(j, 0, half, 0, t, True)
        part(j, half, half, half, half, True)

        def loop(i, c):
            part(i, 0, t, 0, t, False)
            return c

        lax.fori_loop(j + 1, nq, loop, 0)
        dk_out[...] = dk_ref[...].astype(BF16)
        dv_out[...] = dv_ref[...].astype(BF16)

        @pl.when(j == nq - 1)
        def _():
            dq_out[...] = dq_ref[...].astype(BF16)

        @pl.when((pl.program_id(0) == HEADS // 2 - 1) & (j == nq - 1))
        def _():
            _wait_all(*_to_all_copies(send_refs, recv_refs, sems, False))

    blk = pl.BlockSpec((t, 2 * LANE), lambda h, j: (j, h))
    whole = pl.BlockSpec((s, 2 * LANE), lambda h, j: (0, h))
    out = jax.ShapeDtypeStruct((s, 1024), BF16)
    return pl.pallas_call(
        body,
        grid=(HEADS // 2, nq),
        in_specs=[whole, blk, blk, whole] + [ANY] * ns,
        out_specs=[whole, blk, blk] + [ANY] * ns,
        out_shape=[out, out, out] + [jax.ShapeDtypeStruct(a.shape, a.dtype) for a in sends],
        scratch_shapes=[pltpu.VMEM((s, 2 * LANE), F32), pltpu.VMEM((t, 2 * LANE), F32),
                        pltpu.VMEM((t, 2 * LANE), F32)] + _copy_sems(ns, 7),
        compiler_params=_params(("arbitrary", "arbitrary")),
        name="attn_bwd",
    )(qa, kr, vp, dop, *sends)


HG_T = 256
HG_NC = HG_T // HG_BLOCK
HG_G = 4
GW = 64 * HG_G


def _hg_consts():
    r = jnp.arange(HG_T)[:, None]
    c = jnp.arange(HG_T)[None, :]
    same = (r // HG_BLOCK) == (c // HG_BLOCK)
    mcum = (same & (c <= r)).astype(BF16)
    mrev = (same & (c >= r)).astype(BF16)
    msum = same.astype(BF16)
    a = jnp.arange(GW) // 64
    bd = (a[:, None] == a[None, :]).astype(F32)
    return mcum, mrev, msum, bd


def _stack_heads(xg, head):
    return jnp.concatenate([jnp.where(head == h, xg, 0.0) for h in range(HG_G)], axis=0)


def _unstack_heads(r, head, t):
    out = r[(HG_G - 1) * t:]
    for h in range(HG_G - 2, -1, -1):
        out = jnp.where(head == h, r[h * t:(h + 1) * t], out)
    return out


def _compact_state(st):
    out = st[:64]
    for h in range(1, HG_G):
        out = out + st[64 * h:64 * (h + 1)]
    return out


def _expand_state(cs, head64):
    return jnp.concatenate([jnp.where(head64 == h, cs, 0.0) for h in range(HG_G)], axis=0)


def _hg_pre(hq, hf, lbl, mcum, msum):
    lb = _sigmoid(lbl[0:1, :] - lbl[1:2, :])
    sig = _sigmoid(hf)
    f = lb + (1.0 - lb) * sig
    lf = jnp.log(f)
    b = _sel_left(mcum, lf)
    big_l = _sel_left(msum, lf)
    k = 1.0 - f
    qd = hq * jnp.exp(b)
    ki = k * jnp.exp(-b)
    ke = k * jnp.exp(big_l - b)
    return lb, sig, f, b, big_l, qd, ki, ke


def _hgrn_fwd(proj, lbl):
    s = proj.shape[0]
    t = HG_T
    mcum, _, msum, bd = _hg_consts()

    def body(hq_ref, hf_ref, hi_ref, lbl_ref, mcum_ref, msum_ref, bd_ref, o_ref, sp_ref, st_ref):
        @pl.when(pl.program_id(0) == 0)
        def _():
            st_ref[...] = jnp.zeros_like(st_ref)

        mc = mcum_ref[...]
        _, _, _, _, big_l, qd, ki, ke = _hg_pre(hq_ref[...].astype(F32), hf_ref[...].astype(F32), lbl_ref[...], mc,
                                                msum_ref[...])
        el = jnp.exp(big_l)
        hi = hi_ref[...]
        head = lax.broadcasted_iota(jnp.int32, (t, GW), 1) >> 6
        mask = jnp.concatenate([mc] * HG_G, axis=0) > 0.5
        for p in range(HEADS // HG_G):
            sl = slice(GW * p, GW * (p + 1))
            vp = hi[:, sl].astype(BF16)
            qs = _stack_heads(qd[:, sl], head).astype(BF16)
            a = jnp.where(mask, _dotg(qs, ki[:, sl].astype(BF16), NT), 0.0)
            o_intra = _unstack_heads(_dot(a.astype(BF16), vp), head, t)
            qb = qd[:, sl].astype(BF16)
            kb = ke[:, sl].astype(BF16)
            st = st_ref[p]
            for c in range(HG_NC):
                rows = slice(HG_BLOCK * c, HG_BLOCK * (c + 1))
                sp_ref[c, :, sl] = _compact_state(st)
                o_ref[rows, sl] = o_intra[rows] + _dotg(qb[rows], st.astype(BF16), NT)
                u = _dotg(vp[rows], kb[rows], TN) * bd_ref[...]
                st = st * el[HG_BLOCK * c:HG_BLOCK * c + 1, sl] + u
            st_ref[p] = st

    row = lambda j: pl.BlockSpec((t, HG_WIDTH), lambda i: (i, j))
    full = lambda a: pl.BlockSpec(a.shape, lambda i: (0, 0))
    return pl.pallas_call(
        body,
        grid=(s // t,),
        in_specs=[row(6), row(7), row(8), full(lbl), full(mcum), full(msum), full(bd)],
        out_specs=[row(0), pl.BlockSpec((HG_NC, 64, HG_WIDTH), lambda i: (i, 0, 0))],
        out_shape=[jax.ShapeDtypeStruct((s, HG_WIDTH), F32),
                   jax.ShapeDtypeStruct((s // HG_BLOCK, 64, HG_WIDTH), F32)],
        scratch_shapes=[pltpu.VMEM((HEADS // HG_G, GW, GW), F32)],
        compiler_params=_params(("arbitrary",)),
        name="hgrn_fwd",
    )(proj, proj, proj, lbl, mcum, msum, bd)


def _hgrn_bwd(proj, lbl, do, sprev, dproj, pairs):
    s = proj.shape[0]
    t = HG_T
    nt = s // t
    npair = len(pairs)
    mcum, mrev, msum, bd = _hg_consts()

    def body(hq_ref, hf_ref, hi_ref, lbl_ref, do_ref, sp_ref, mcum_ref, mrev_ref, msum_ref, bd_ref,
             dproj_in, *rest):
        del dproj_in
        pair_refs, (dh_ref, dlbl_ref) = rest[:2 * npair], rest[2 * npair:2 * npair + 2]
        dw_refs, g_ref = rest[2 * npair + 2:3 * npair + 2], rest[3 * npair + 2]

        @pl.when(pl.program_id(0) == 0)
        def _():
            g_ref[...] = jnp.zeros_like(g_ref)
            dlbl_ref[...] = jnp.zeros_like(dlbl_ref)
            for dw_ref in dw_refs:
                dw_ref[...] = jnp.zeros_like(dw_ref)

        for n, dw_ref in enumerate(dw_refs):
            dw_ref[...] += _dot(pair_refs[2 * n][...], pair_refs[2 * n + 1][...])

        mc = mcum_ref[...]
        lb, sig, f, b, big_l, qd, ki, ke = _hg_pre(hq_ref[...].astype(F32), hf_ref[...].astype(F32), lbl_ref[...], mc,
                                                   msum_ref[...])
        el = jnp.exp(big_l)
        hi = hi_ref[...]
        dov = do_ref[...]
        head = lax.broadcasted_iota(jnp.int32, (t, GW), 1) >> 6
        head64 = lax.broadcasted_iota(jnp.int32, (64, GW), 1) >> 6
        mask = jnp.concatenate([mc] * HG_G, axis=0) > 0.5
        dqd_parts, dke_parts, dv_parts, del_parts, dki_parts = [], [], [], [], []
        for p in range(HEADS // HG_G):
            sl = slice(GW * p, GW * (p + 1))
            vp = hi[:, sl].astype(BF16)
            qs = _stack_heads(qd[:, sl], head).astype(BF16)
            kip = ki[:, sl].astype(BF16)
            dos = _stack_heads(dov[:, sl], head).astype(BF16)
            a = jnp.where(mask, _dotg(qs, kip, NT), 0.0).astype(BF16)
            da = jnp.where(mask, _dotg(dos, vp, NT), 0.0).astype(BF16)
            r = _dot(da, kip)
            dki_parts.append(_dotg(da, qs, TN))
            qb = qd[:, sl].astype(BF16)
            kb = ke[:, sl].astype(BF16)
            dob = dov[:, sl].astype(BF16)
            g = g_ref[p]
            dqd_c, dv_c, dke_c, del_c = [], [], [], []
            for c in range(HG_NC - 1, -1, -1):
                rows = slice(HG_BLOCK * c, HG_BLOCK * (c + 1))
                gb = g.astype(BF16)
                st = _expand_state(sp_ref[c, :, sl], head64)
                dqd_c.append(_dot(dob[rows], st.astype(BF16)))
                dv_c.append(_dotg(kb[rows], gb, NT))
                dke_c.append(_dot(vp[rows], gb))
                del_c.append(jnp.broadcast_to(jnp.sum(g * st, axis=0, keepdims=True), (HG_BLOCK, GW)))
                g = g * el[HG_BLOCK * c:HG_BLOCK * c + 1, sl] + _dotg(dob[rows], qb[rows], TN) * bd_ref[...]
            g_ref[p] = g
            up = lambda parts: jnp.concatenate(parts[::-1], axis=0)
            dqd_parts.append(_unstack_heads(r, head, t) + up(dqd_c))
            dv_parts.append(_dotg(a, dos, TN) + up(dv_c))
            dke_parts.append(up(dke_c))
            del_parts.append(up(del_c))
        wide = lambda parts: jnp.concatenate(parts, axis=1)
        dqd, dke, dki, dvv, del_rows = wide(dqd_parts), wide(dke_parts), wide(dki_parts), wide(dv_parts), wide(del_parts)
        dh_ref[:, :HG_WIDTH] = (dqd * jnp.exp(b)).astype(BF16)
        dh_ref[:, 2 * HG_WIDTH:] = dvv.astype(BF16)
        dke_ke = dke * ke
        db = dqd * qd - dki * ki - dke_ke
        dl_rows = _sel_left(msum_ref[...], dke_ke) + del_rows * el
        is_last = (lax.broadcasted_iota(jnp.int32, (t, HG_WIDTH), 0) & (HG_BLOCK - 1)) == HG_BLOCK - 1
        db = db + jnp.where(is_last, dl_rows, 0.0)
        dlf = _sel_left(mrev_ref[...], db)
        dk = dki * jnp.exp(-b) + dke * jnp.exp(big_l - b)
        df = dlf / f - dk
        dh_ref[:, HG_WIDTH:2 * HG_WIDTH] = (df * (1.0 - lb) * sig * (1.0 - sig)).astype(BF16)
        dlb = jnp.sum(df * (1.0 - sig), axis=0, keepdims=True) * lb * (1.0 - lb)
        dlbl_ref[0:1, :] += dlb
        dlbl_ref[1:2, :] -= dlb

    rrow = lambda j: pl.BlockSpec((t, HG_WIDTH), lambda i: (nt - 1 - i, j))
    full = lambda a: pl.BlockSpec(a.shape, lambda i: (0, 0))
    pair_specs, dw_specs, dw_shapes = [], [], []
    for at, b in pairs:
        pair_specs += [pl.BlockSpec((at.shape[0], t), lambda i: (0, i)), pl.BlockSpec((t, b.shape[1]), lambda i: (i, 0))]
        dw_specs.append(pl.BlockSpec((at.shape[0], b.shape[1]), lambda i: (0, 0)))
        dw_shapes.append(jax.ShapeDtypeStruct((at.shape[0], b.shape[1]), F32))
    return pl.pallas_call(
        body,
        grid=(nt,),
        in_specs=[rrow(6), rrow(7), rrow(8), full(lbl), rrow(0),
                  pl.BlockSpec((HG_NC, 64, HG_WIDTH), lambda i: (nt - 1 - i, 0, 0)),
                  full(mcum), full(mrev), full(msum), full(bd), pl.BlockSpec(memory_space=pl.ANY)] + pair_specs,
        out_specs=[pl.BlockSpec((t, 3 * HG_WIDTH), lambda i: (nt - 1 - i, 2)),
                   pl.BlockSpec((2, HG_WIDTH), lambda i: (0, 0))] + dw_specs,
        out_shape=[jax.ShapeDtypeStruct(dproj.shape, BF16), jax.ShapeDtypeStruct((2, HG_WIDTH), F32)] + dw_shapes,
        input_output_aliases={10: 0},
        scratch_shapes=[pltpu.VMEM((HEADS // HG_G, GW, GW), F32)],
        compiler_params=_params(("arbitrary",)),
        name="hgrn_bwd",
    )(proj, proj, proj, lbl, do, sprev, mcum, mrev, msum, bd, dproj, *[a for pair in pairs for a in pair])


def _tail(x, tgt, proj, attn, o, w_a, w_b, w_out, w_at, w_bt, w_outt, b_gate, g_post, gh):
    s = x.shape[0]
    tm = 256
    ones64 = (jnp.arange(HG_WIDTH)[:, None] // 64 == jnp.arange(HG_WIDTH)[None, :] // 64).astype(BF16)
    weights = (w_a, w_b, w_out, w_at, w_bt, w_outt)

    def body(x_ref, t_ref, ml_ref, ga_ref, gb_ref, at_ref, o_ref, *rest):
        w_hbm, (bg_ref, gp_ref, gh_ref, ones_ref) = rest[:6], rest[6:10]
        (dout_ref, dpj_ref, dop_ref, do_ref, mt_ref, dy_ref, yat_ref, dya_ref, ybt_ref, dyb_ref,
         loss_ref, dgp_ref, dbg_ref, dgh_ref) = rest[10:24]
        (wa_ref, wb_ref, wo_ref, wat_ref, wbt_ref, wot_ref), w_sem = rest[24:30], rest[30]

        @pl.when(pl.program_id(0) == 0)
        def _():
            loads = [pltpu.make_async_copy(src, dst, w_sem.at[k])
                     for k, (src, dst) in enumerate(zip(w_hbm, rest[24:30]))]
            _start_all(loads, [])
            loss_ref[...] = jnp.zeros_like(loss_ref)
            dgp_ref[...] = jnp.zeros_like(dgp_ref)
            dbg_ref[...] = jnp.zeros_like(dbg_ref)
            dgh_ref[...] = jnp.zeros_like(dgh_ref)
            _wait_all(loads, [])

        ones = ones_ref[...]
        gate_a = ga_ref[...].astype(F32)
        sa = _sigmoid(gate_a)
        silu_a = gate_a * sa
        attn_v = at_ref[...]
        ya_in = attn_v * silu_a
        ov = o_ref[...]
        ro = lax.rsqrt(_sel_right(ov * ov, ones) * (1.0 / 64.0) + EPS)
        ohat = ov * ro
        ghv = gh_ref[...]
        on = ohat * ghv
        gate_b = gb_ref[...].astype(F32)
        sb = _sigmoid(gate_b)
        silu_b = gate_b * sb
        yb_in = on * silu_b
        ya_bf = ya_in.astype(BF16)
        yb_bf = yb_in.astype(BF16)
        yat_ref[...] = ya_bf.T
        ybt_ref[...] = yb_bf.T
        y_a = _dot(ya_bf, wa_ref[...])
        y_b = _dot(yb_bf, wb_ref[...])
        gts = _sigmoid(ml_ref[...].astype(F32) + bg_ref[...])
        g_a = gts[:, :D_MODEL]
        g_b = gts[:, D_MODEL:]
        m_bf = (g_a * y_a + g_b * y_b).astype(BF16)
        mt_ref[...] = m_bf.T
        y = _dot(m_bf, wo_ref[...])
        r1 = lax.rsqrt(jnp.mean(y * y, axis=-1, keepdims=True) + EPS)
        yn = y * r1
        gp = gp_ref[...]
        e = x_ref[...] + yn * gp - t_ref[...]
        loss_ref[...] += jnp.sum(e * e, axis=0, keepdims=True)
        dout = e * (1.0 / D_MODEL)
        dout_ref[...] = dout
        dgp_ref[...] += jnp.sum(dout * yn, axis=0, keepdims=True)
        dyn = dout * gp
        dy = r1 * (dyn - yn * jnp.mean(dyn * yn, axis=-1, keepdims=True))
        dy_bf = dy.astype(BF16)
        dy_ref[...] = dy_bf
        dm = _dot(dy_bf, wot_ref[...])
        dml_a = dm * y_a * g_a * (1.0 - g_a)
        dml_b = dm * y_b * g_b * (1.0 - g_b)
        dpj_ref[:, :D_MODEL] = dml_a.astype(BF16)
        dpj_ref[:, D_MODEL:2 * D_MODEL] = dml_b.astype(BF16)
        dbg_ref[:, :D_MODEL] += jnp.sum(dml_a, axis=0, keepdims=True)
        dbg_ref[:, D_MODEL:] += jnp.sum(dml_b, axis=0, keepdims=True)
        dya_bf = (dm * g_a).astype(BF16)
        dyb_bf = (dm * g_b).astype(BF16)
        dya_ref[...] = dya_bf
        dyb_ref[...] = dyb_bf
        dya_in = _dot(dya_bf, wat_ref[...])
        dyb_in = _dot(dyb_bf, wbt_ref[...])
        dattn = dya_in * silu_a
        delta = _sel_right(dattn * attn_v, ones)
        lane = lax.broadcasted_iota(jnp.int32, (tm, LANE), 1)
        for p in range(HEADS // 2):
            sl = slice(LANE * p, LANE * (p + 1))
            xs = (dattn[:, sl], pltpu.roll(dattn[:, sl], VDIM, 1))
            nds = (-pltpu.roll(delta[:, sl], VDIM, 1), -delta[:, sl])
            for a in range(2):
                hi, lo_part = _hi_lo(nds[a])
                blk = jnp.where(lane < VDIM, xs[a], jnp.where(lane == VDIM, hi, jnp.where(lane == VDIM + 1, lo_part, 0.0)))
                dop_ref[:, LANE * (2 * p + a):LANE * (2 * p + a + 1)] = blk.astype(BF16)
        dpj_ref[:, 2 * D_MODEL:2 * D_MODEL + HG_WIDTH] = (
            dya_in * attn_v * (sa * (1.0 + gate_a * (1.0 - sa)))).astype(BF16)
        don = dyb_in * silu_b
        dpj_ref[:, 2 * D_MODEL + HG_WIDTH:] = (dyb_in * on * (sb * (1.0 + gate_b * (1.0 - sb)))).astype(BF16)
        dgh_ref[...] += jnp.sum(don * ohat, axis=0, keepdims=True)
        dohat = don * ghv
        do_ref[...] = (ro * (dohat - ohat * (_sel_right(dohat * ohat, ones) * (1.0 / 64.0)))).astype(BF16)

    row = lambda w, j: pl.BlockSpec((tm, w), lambda i: (i, j))
    col = lambda w: pl.BlockSpec((w, tm), lambda i: (0, i))
    full = lambda a: pl.BlockSpec(a.shape, lambda i: (0, 0))
    acc = lambda w: pl.BlockSpec((1, w), lambda i: (0, 0))
    sds = lambda w, dt: jax.ShapeDtypeStruct((s, w), dt)
    sdt = lambda w: jax.ShapeDtypeStruct((w, s), BF16)
    return pl.pallas_call(
        body,
        grid=(s // tm,),
        in_specs=[row(1024, 0), row(1024, 0), row(2048, 0), row(512, 4), row(512, 5), row(512, 0), row(512, 0)]
        + [ANY] * 6 + [full(b_gate), full(g_post), full(gh), full(ones64)],
        out_specs=[row(1024, 0), row(3072, 0), row(1024, 0), row(512, 0),
                   col(1024), row(1024, 0), col(512), row(1024, 0), col(512), row(1024, 0),
                   acc(1024), acc(1024), acc(2048), acc(512)],
        out_shape=[sds(1024, F32), sds(D_IN_PAD, BF16), sds(1024, BF16), sds(512, BF16),
                   sdt(1024), sds(1024, BF16), sdt(512), sds(1024, BF16), sdt(512), sds(1024, BF16),
                   jax.ShapeDtypeStruct((1, 1024), F32), jax.ShapeDtypeStruct((1, 1024), F32),
                   jax.ShapeDtypeStruct((1, 2048), F32), jax.ShapeDtypeStruct((1, 512), F32)],
        scratch_shapes=[pltpu.VMEM(a.shape, BF16) for a in weights] + [pltpu.SemaphoreType.DMA((6,))],
        compiler_params=_params(("arbitrary",), 56),
        name="tail",
    )(x, tgt, proj, proj, proj, attn, o, *weights, b_gate, g_post, gh, ones64)


def _mla_bwd(proj, dqr, dkr, dv, g_q, g_kv, w_uq_pt, w_kv_pt, rc, rs1, rs2, cqt, ckvt, dproj):
    s = proj.shape[0]
    tm = 256
    scale = 1.0 / math.sqrt(QK)

    def body(cq_ref, ckv_ref, dqr_ref, dkr_ref, dv_ref, gq_ref, gkv_ref, wuqt_ref, wkvt_ref, c_ref, s1_ref, s2_ref,
             cqt_ref, ckvt_ref, dproj_in, dc_ref, dgq_ref, dgkv_ref, dwuq_ref, dwkv_ref, dqf_ref, dkvf_ref):
        del dproj_in

        @pl.when(pl.program_id(0) == 0)
        def _():
            dgq_ref[...] = jnp.zeros_like(dgq_ref)
            dgkv_ref[...] = jnp.zeros_like(dgkv_ref)
            dwuq_ref[...] = jnp.zeros_like(dwuq_ref)
            dwkv_ref[...] = jnp.zeros_like(dwkv_ref)

        c, s1, s2 = c_ref[...], s1_ref[...], s2_ref[...]
        lane = lax.broadcasted_iota(jnp.int32, (tm, LANE), 1)
        ksum = jnp.zeros((tm, LANE), F32)
        for h in range(HEADS):
            sl = slice(LANE * h, LANE * (h + 1))
            dqf_ref[:, sl] = (_unrope(dqr_ref[:, sl], c, s1, s2) * scale).astype(BF16)
            dkh = dkr_ref[:, sl]
            ksum = ksum + dkh
            dkvf_ref[:, sl] = jnp.where(lane < NOPE, dkh, 0.0).astype(BF16)
            dkvf_ref[:, HEADS * LANE + LANE * h:HEADS * LANE + LANE * (h + 1)] = jnp.where(
                lane < VDIM, dv_ref[:, sl], 0.0).astype(BF16)
        dkpe = _unrope(ksum, c, s1, s2)
        dc_ref[:, Q_LORA + KV_LORA:] = jnp.where((lane >= NOPE) & (lane < QK), dkpe, 0.0).astype(BF16)
        dqf, dkvf = dqf_ref[...], dkvf_ref[...]
        dwuq_ref[...] += _dot(cqt_ref[...], dqf)
        dwkv_ref[...] += _dot(ckvt_ref[...], dkvf)
        dcqn = _dot(dqf, wuqt_ref[...])
        dckvn = _dot(dkvf, wkvt_ref[...])
        for x_ref, g_ref, dn, cols, dg_ref in ((cq_ref, gq_ref, dcqn, slice(0, Q_LORA), dgq_ref),
                                               (ckv_ref, gkv_ref, dckvn, slice(Q_LORA, Q_LORA + KV_LORA), dgkv_ref)):
            xv = x_ref[...].astype(F32)
            r = lax.rsqrt(jnp.mean(xv * xv, axis=-1, keepdims=True) + EPS)
            xh = xv * r
            dg_ref[...] += jnp.sum(dn * xh, axis=0, keepdims=True)
            dh = dn * g_ref[...]
            dc_ref[:, cols] = (r * (dh - xh * jnp.mean(dh * xh, axis=-1, keepdims=True))).astype(BF16)

    row = lambda w, j: pl.BlockSpec((tm, w), lambda i: (i, j))
    full = lambda a: pl.BlockSpec(a.shape, lambda i: (0, 0))
    acc = lambda w: pl.BlockSpec((1, w), lambda i: (0, 0))
    col = lambda w: pl.BlockSpec((w, tm), lambda i: (0, i))
    whole = lambda shape: pl.BlockSpec(shape, lambda i: (0, 0))
    return pl.pallas_call(
        body,
        grid=(s // tm,),
        in_specs=[row(768, 6), row(256, 21), row(1024, 0), row(1024, 0), row(1024, 0), full(g_q), full(g_kv),
                  full(w_uq_pt), full(w_kv_pt), row(128, 0), row(128, 0), row(128, 0), col(Q_LORA), col(KV_LORA),
                  pl.BlockSpec(memory_space=pl.ANY)],
        out_specs=[row(1152, 4), acc(768), acc(256), whole((Q_LORA, HEADS * LANE)), whole((KV_LORA, 2 * HEADS * LANE))],
        out_shape=[jax.ShapeDtypeStruct(dproj.shape, BF16),
                   jax.ShapeDtypeStruct((1, 768), F32), jax.ShapeDtypeStruct((1, 256), F32),
                   jax.ShapeDtypeStruct((Q_LORA, HEADS * LANE), F32),
                   jax.ShapeDtypeStruct((KV_LORA, 2 * HEADS * LANE), F32)],
        input_output_aliases={14: 0},
        scratch_shapes=[pltpu.VMEM((tm, HEADS * LANE), BF16), pltpu.VMEM((tm, 2 * HEADS * LANE), BF16)],
        compiler_params=_params(("arbitrary",)),
        name="mla_bwd",
    )(proj, proj, dqr, dkr, dv, g_q, g_kv, w_uq_pt, w_kv_pt, rc, rs1, rs2, cqt, ckvt, dproj)


def _pick(n, options):
    for o in options:
        if n % o == 0:
            return o
    raise ValueError(n)


def _matmul(a, b, name):
    m, k = a.shape
    n = b.shape[1]
    tm = _pick(m, (1024, 768, 512, 256))
    tn = _pick(n, (1152, 1024, 768, 512))
    tk = _pick(k, (1024, 512))
    nk = k // tk

    def body(a_ref, b_ref, o_ref):
        @pl.when(pl.program_id(2) == 0)
        def _():
            o_ref[...] = jnp.zeros_like(o_ref)

        o_ref[...] += _dot(a_ref[...], b_ref[...])

    return pl.pallas_call(
        body,
        grid=(m // tm, n // tn, nk),
        in_specs=[pl.BlockSpec((tm, tk), lambda i, j, l: (i, l)), pl.BlockSpec((tk, tn), lambda i, j, l: (l, j))],
        out_specs=pl.BlockSpec((tm, tn), lambda i, j, l: (i, j)),
        out_shape=jax.ShapeDtypeStruct((m, n), F32),
        compiler_params=_params(("arbitrary", "arbitrary", "arbitrary")),
        name=name,
    )(a, b)


def _dh_dx(dproj, w_in_pt, x, dout, g_pre, sends):
    s, k = dproj.shape
    tm = 256
    ns, ni = len(sends), s // tm

    def body(dp_ref, w_ref, x_ref, dout_ref, g_ref, *rest):
        send_refs, (dx_ref, dg_ref) = rest[:ns], rest[ns:ns + 2]
        recv_refs, sems = rest[ns + 2:2 * ns + 2], rest[2 * ns + 2:]

        @pl.when(pl.program_id(0) == 0)
        def _():
            _start_all(*_to_chips_copies(send_refs, recv_refs, sems))
            dg_ref[...] = jnp.zeros_like(dg_ref)

        dh = _dot(dp_ref[...], w_ref[...])
        xv = x_ref[...]
        r = lax.rsqrt(jnp.mean(xv * xv, axis=-1, keepdims=True) + EPS)
        xh = xv * r
        dg_ref[...] += jnp.sum(dh * xh, axis=0, keepdims=True)
        dxh = dh * g_ref[...]
        dx_ref[...] = dout_ref[...] + r * (dxh - xh * jnp.mean(dxh * xh, axis=-1, keepdims=True))

        @pl.when(pl.program_id(0) == ni - 1)
        def _():
            _wait_all(*_to_chips_copies(send_refs, recv_refs, sems))

    row = lambda w: pl.BlockSpec((tm, w), lambda i: (i, 0))
    return pl.pallas_call(
        body,
        grid=(ni,),
        in_specs=[row(k), pl.BlockSpec((k, D_MODEL), lambda i: (0, 0)), row(D_MODEL), row(D_MODEL),
                  pl.BlockSpec((1, D_MODEL), lambda i: (0, 0))] + [ANY] * ns,
        out_specs=[row(D_MODEL), pl.BlockSpec((1, D_MODEL), lambda i: (0, 0))] + [ANY] * ns,
        out_shape=[jax.ShapeDtypeStruct((s, D_MODEL), F32), jax.ShapeDtypeStruct((1, D_MODEL), F32)]
        + [jax.ShapeDtypeStruct(a.shape, a.dtype) for a in sends],
        scratch_shapes=_copy_sems(ns, 3),
        compiler_params=_params(("arbitrary",)),
        name="dh_dx",
    )(dproj, w_in_pt, x, dout, g_pre, *sends)


def _pair_reduce(slots):
    n = len(slots)
    half = [(N_DEV // 2,) + a.shape[1:] for a in slots]

    def body(*refs):
        s_refs, o_refs = refs[:n], refs[n:2 * n]
        mine, got = refs[2 * n:3 * n], refs[3 * n:4 * n]
        send_sems, recv_sems, local_sems = refs[4 * n:]
        x, y, c = _my_place()
        copies, loads = [], []
        for a in range(n):
            for q in range(N_DEV // 2):
                copies.append(pltpu.make_async_remote_copy(
                    src_ref=s_refs[a].at[2 * q + 1 - c], dst_ref=got[a].at[q],
                    send_sem=send_sems.at[4 * a + q], recv_sem=recv_sems.at[4 * a + q],
                    device_id=(x, y, 1 - c), device_id_type=MESH_ID))
                loads.append(pltpu.make_async_copy(s_refs[a].at[2 * q + c], mine[a].at[q], local_sems.at[4 * a + q]))
        _start_all(loads, copies)
        _wait_all(loads, copies)
        for a in range(n):
            o_refs[a][...] = (mine[a][...].astype(F32) + got[a][...].astype(F32)).astype(o_refs[a].dtype)

    vm = lambda: [pltpu.VMEM(h, a.dtype) for h, a in zip(half, slots)]
    return pl.pallas_call(
        body,
        in_specs=[ANY] * n,
        out_shape=[jax.ShapeDtypeStruct(h, a.dtype) for h, a in zip(half, slots)],
        scratch_shapes=vm() + vm() + [pltpu.SemaphoreType.DMA((4 * n,)), pltpu.SemaphoreType.DMA((4 * n,)),
                                      pltpu.SemaphoreType.DMA((4 * n,))],
        compiler_params=pltpu.CompilerParams(vmem_limit_bytes=48 * 2**20),
        name="pair_reduce",
    )(*slots)


def _rope_tables(s):
    inv = (np.float32(ROPE_THETA) ** (-np.arange(0, ROPE, 2, dtype=np.float32) / np.float32(ROPE))).astype(np.float32)
    ang = (np.arange(s, dtype=np.float32)[:, None] * inv[None, :]).astype(np.float32)
    cos, sin = jnp.asarray(np.cos(ang.astype(np.float64)), F32), jnp.asarray(np.sin(ang.astype(np.float64)), F32)
    z = lambda w: jnp.zeros((s, w), F32)
    rc = jnp.concatenate([jnp.ones((s, NOPE), F32), cos, cos, z(32)], axis=1)
    rs1 = jnp.concatenate([z(NOPE), -sin, z(16), z(32)], axis=1)
    rs2 = jnp.concatenate([z(NOPE), z(16), sin, z(32)], axis=1)
    return rc, rs1, rs2


def _step(x, tgt, w_blk, shards, g_pre, b_gate, g_q, g_kv, lbl, g_hgrn, g_post):
    s = x.shape[0]
    rc, rs1, rs2 = _rope_tables(s)
    gh = jnp.tile(g_hgrn, (1, HEADS))

    proj, w_in_pt, ht, *got = _gather_proj(x, g_pre, w_blk, shards[:2])
    w_uq, w_ukv = (_from_slots(n, g) for n, g in zip(MATS[:2], got))
    w_uq_p = jnp.pad(w_uq.reshape(Q_LORA, HEADS, QK), ((0, 0), (0, 0), (0, LANE - QK))).reshape(Q_LORA, HEADS * LANE)
    kv3 = w_ukv.reshape(KV_LORA, HEADS, NOPE + VDIM)
    pad64 = lambda t: jnp.pad(t, ((0, 0), (0, 0), (0, LANE - 64))).reshape(KV_LORA, HEADS * LANE)
    w_kv_p = jnp.concatenate([pad64(kv3[:, :, :NOPE]), pad64(kv3[:, :, NOPE:])], axis=1)

    qr, kr, v, cqt, ckvt = _mla_prep(proj, g_q, g_kv, w_uq_p, w_kv_p, rc, rs1, rs2)
    attn, qa, *got = _attn_fwd(qr, kr, v, shards[2:])
    w_a, w_b, w_out = (_from_slots(n, g) for n, g in zip(MATS[2:], got))
    o, sprev = _hgrn_fwd(proj, lbl)
    (dout, dproj, dop, do, mt, dy_bf, yat, dya_bf, ybt, dyb_bf,
     loss_vec, dg_post, db_gate, dgh) = _tail(x, tgt, proj, attn, o, w_a, w_b, w_out, w_a.T, w_b.T, w_out.T,
                                               b_gate, g_post, gh)
    dproj, dlbl, *dw_early = _hgrn_bwd(proj, lbl, do, sprev, dproj, [(yat, dya_bf), (ybt, dyb_bf), (mt, dy_bf)])
    early = [_to_slots(n, dw).astype(BF16) for n, dw in zip(MATS[2:], dw_early)]
    dqr, dkr, dv, *early_recv = _attn_bwd(qa, kr, v, dop, early)
    dproj, dg_q, dg_kv, dw_uq_p, dw_kv_p = _mla_bwd(proj, dqr, dkr, dv, g_q, g_kv, w_uq_p.T, w_kv_p.T, rc, rs1, rs2,
                                                    cqt, ckvt, dproj)

    dw_in_slots = _dw_in_slots(ht, dproj)
    dw_uq = dw_uq_p.reshape(Q_LORA, HEADS, LANE)[:, :, :QK].reshape(Q_LORA, HEADS * QK)
    dw_ukv = jnp.concatenate([dw_kv_p[:, :HEADS * LANE].reshape(KV_LORA, HEADS, LANE)[:, :, :NOPE],
                              dw_kv_p[:, HEADS * LANE:].reshape(KV_LORA, HEADS, LANE)[:, :, :VDIM]],
                             axis=2).reshape(KV_LORA, 1024)
    late = _pair_reduce([dw_in_slots, _to_slots("w_uq", dw_uq).astype(BF16), _to_slots("w_ukv", dw_ukv).astype(BF16)])
    dx, dg_pre, *late_recv = _dh_dx(dproj, w_in_pt, x, dout, g_pre, late)

    g_sum = _vectors_sum(dg_pre, db_gate, dg_q, dg_kv, dlbl, dgh, dg_post, loss_vec)
    return dx, late_recv[0], dict(zip(MATS, late_recv[1:] + early_recv)), g_sum


def _adamw(g, w, m, v):
    c1 = 1.0 / (1.0 - ADAM_B1 ** ADAM_STEP)
    c2 = 1.0 / (1.0 - ADAM_B2 ** ADAM_STEP)
    nm = ADAM_B1 * m + (1.0 - ADAM_B1) * g
    nv = ADAM_B2 * v + (1.0 - ADAM_B2) * (g * g)
    d = -ADAM_LR * ((nm * c1) / (jnp.sqrt(nv * c2) + ADAM_EPS) + ADAM_WD * w)
    return d, nm, nv


def _sum8(r_ref):
    g = r_ref[0].astype(F32)
    for k in range(1, r_ref.shape[0]):
        g = g + r_ref[k].astype(F32)
    return g


def _sum_adamw_w_in(recv, w, m, v):
    rows, _, cols = w.shape
    tc = 256
    nc = cols // tc

    def body(r_ref, w_hbm, m_hbm, v_hbm, g_hbm, d_hbm, nm_hbm, nv_hbm, ins, outs, in_sems, out_sems):
        i = pl.program_id(0)
        slot = i & 1
        cols_of = lambda step: pl.ds(pl.multiple_of(step * tc, tc), tc)

        def load(k, step, sl):
            return pltpu.make_async_copy((w_hbm, m_hbm, v_hbm)[k].at[:, 0, cols_of(step)], ins.at[sl, k],
                                         in_sems.at[sl, k])

        def store(k, step, sl):
            return pltpu.make_async_copy(outs.at[sl, k], (g_hbm, d_hbm, nm_hbm, nv_hbm)[k].at[:, 0, cols_of(step)],
                                         out_sems.at[sl, k])

        @pl.when(i == 0)
        def _():
            for k in range(3):
                load(k, 0, 0).start()

        @pl.when(i + 1 < nc)
        def _():
            for k in range(3):
                load(k, i + 1, 1 - slot).start()

        @pl.when(i >= 2)
        def _():
            for k in range(4):
                store(k, i - 2, slot).wait()

        for k in range(3):
            load(k, i, slot).wait()
        g = _sum8(r_ref)
        d, nm, nv = _adamw(g, ins[slot, 0], ins[slot, 1], ins[slot, 2])
        for k, val in enumerate((g, d, nm, nv)):
            outs[slot, k] = val
        for k in range(4):
            store(k, i, slot).start()

        @pl.when(i == nc - 1)
        def _():
            for k in range(4):
                store(k, i, slot).wait()
            if nc >= 2:
                for k in range(4):
                    store(k, i - 1, 1 - slot).wait()

    out = jax.ShapeDtypeStruct((rows, 1, cols), F32)
    return pl.pallas_call(
        body,
        grid=(nc,),
        in_specs=[pl.BlockSpec((recv.shape[0], rows, tc), lambda i: (0, 0, i)), ANY, ANY, ANY],
        out_specs=[ANY, ANY, ANY, ANY],
        out_shape=[out, out, out, out],
        scratch_shapes=[pltpu.VMEM((2, 3, rows, tc), F32), pltpu.VMEM((2, 4, rows, tc), F32),
                        pltpu.SemaphoreType.DMA((2, 3)), pltpu.SemaphoreType.DMA((2, 4))],
        compiler_params=_params(("arbitrary",)),
        name="sum_adamw_w_in",
    )(recv, w, m, v)


def _sum_adamw_whole(recvs, ws, ms, vs):
    n = len(ws)

    def body(*refs):
        r_refs, w_refs, m_refs, v_refs = refs[:n], refs[n:2 * n], refs[2 * n:3 * n], refs[3 * n:4 * n]
        outs = refs[4 * n:]
        for a in range(n):
            g = _sum8(r_refs[a])
            d, nm, nv = _adamw(g, w_refs[a][...], m_refs[a][...], v_refs[a][...])
            outs[a][...] = g
            outs[n + a][...] = d
            outs[2 * n + a][...] = nm
            outs[3 * n + a][...] = nv

    shapes = [jax.ShapeDtypeStruct(w.shape, F32) for w in ws]
    res = pl.pallas_call(
        body,
        out_shape=shapes * 4,
        compiler_params=pltpu.CompilerParams(vmem_limit_bytes=48 * 2**20),
        name="sum_adamw_mats",
    )(*recvs, *ws, *ms, *vs)
    return res[:n], res[n:2 * n], res[2 * n:3 * n], res[3 * n:]


SMALL = ("g_pre", "b_gate", "g_q", "g_kv", "lb_logits", "g_hgrn", "g_post")
SMALL_SHAPE = dict(g_pre=(1, 1024), b_gate=(1, 2048), g_q=(1, 768), g_kv=(1, 256), lb_logits=(2, 512),
                   g_hgrn=(1, 64), g_post=(1, 1024))


def _vectors_sum(dg_pre, db_gate, dg_q, dg_kv, dlbl, dgh, dg_post, loss_vec):
    def body(gpre_ref, bg_ref, gq_ref, gkv_ref, lbl_ref, gh_ref, gpost_ref, loss_ref, out_ref, mine, got,
             send_sems, recv_sems):
        mine[...] = jnp.zeros_like(mine)
        mine[0:1, :] = gpre_ref[...]
        mine[1:2, :] = bg_ref[:, :1024]
        mine[2:3, :] = bg_ref[:, 1024:]
        mine[3:4, :Q_LORA] = gq_ref[...]
        mine[4:5, :KV_LORA] = gkv_ref[...]
        loss = (0.5 / D_MODEL) * jnp.sum(loss_ref[...], axis=-1, keepdims=True)
        mine[4:5, KV_LORA:] = jnp.broadcast_to(loss, (1, 1024 - KV_LORA))
        mine[5:6, :HG_WIDTH] = lbl_ref[0:1, :]
        mine[5:6, HG_WIDTH:] = lbl_ref[1:2, :]
        gh = gh_ref[...]
        fold = gh[:, :VDIM]
        for h in range(1, HEADS):
            fold = fold + gh[:, VDIM * h:VDIM * (h + 1)]
        mine[6:7, :VDIM] = fold
        mine[7:8, :] = gpost_ref[...]
        x, y, c = _my_place()
        me = 4 * x + 2 * y + c
        got[me] = mine[...]
        copies = [pltpu.make_async_remote_copy(
            src_ref=mine, dst_ref=got.at[me], send_sem=send_sems.at[k], recv_sem=recv_sems.at[k],
            device_id=_flip(k, x, y, c), device_id_type=MESH_ID) for k in range(N_DEV - 1)]
        _start_all([], copies)
        _wait_all([], copies)
        out_ref[...] = _sum8(got)

    return pl.pallas_call(
        body,
        out_shape=jax.ShapeDtypeStruct((8, 1024), F32),
        scratch_shapes=[pltpu.VMEM((8, 1024), F32), pltpu.VMEM((N_DEV, 8, 1024), F32),
                        pltpu.SemaphoreType.DMA((7,)), pltpu.SemaphoreType.DMA((7,))],
        name="vectors_sum",
    )(dg_pre, db_gate, dg_q, dg_kv, dlbl, dgh, dg_post, loss_vec)


def _vectors_adamw(g_sum, ws, ms, vs):
    n = len(SMALL)

    def body(g_ref, *refs):
        w_refs, m_refs, v_refs = refs[:n], refs[n:2 * n], refs[2 * n:3 * n]
        loss_ref, outs = refs[3 * n], refs[3 * n + 1:]
        g = g_ref[...]
        loss_ref[...] = g[4:5, KV_LORA:KV_LORA + 1]
        grads = (g[0:1, :], jnp.concatenate([g[1:2, :], g[2:3, :]], axis=1), g[3:4, :Q_LORA], g[4:5, :KV_LORA],
                 jnp.concatenate([g[5:6, :HG_WIDTH], g[5:6, HG_WIDTH:]], axis=0), g[6:7, :VDIM], g[7:8, :])
        for a in range(n):
            d, nm, nv = _adamw(grads[a], w_refs[a][...], m_refs[a][...], v_refs[a][...])
            outs[a][...] = grads[a]
            outs[n + a][...] = d
            outs[2 * n + a][...] = nm
            outs[3 * n + a][...] = nv

    shapes = [jax.ShapeDtypeStruct(SMALL_SHAPE[k], F32) for k in SMALL]
    res = pl.pallas_call(
        body,
        out_shape=[jax.ShapeDtypeStruct((1, 1), F32)] + shapes * 4,
        name="vectors_adamw",
    )(g_sum, *ws, *ms, *vs)
    return res[0], res[1:n + 1], res[n + 1:2 * n + 1], res[2 * n + 1:3 * n + 1], res[3 * n + 1:]


MATS = ("w_uq", "w_ukv", "w_branch_a", "w_branch_b", "w_out")
COL_SHARDED = dict(w_uq=False, w_ukv=True, w_branch_a=True, w_branch_b=True, w_out=False)
ORDER = ("g_pre", "w_in", "b_gate", "g_q", "w_uq", "g_kv", "w_ukv", "lb_logits", "g_hgrn",
         "w_branch_a", "w_branch_b", "w_out", "g_post")


def _to_slots(name, full):
    r, c = full.shape
    if COL_SHARDED[name]:
        return full.reshape(r, N_DEV, c // N_DEV).transpose(1, 0, 2)
    return full.reshape(N_DEV, r // N_DEV, c)


def _from_slots(name, slots):
    _, r, c = slots.shape
    if COL_SHARDED[name]:
        return slots.transpose(1, 0, 2).reshape(r, N_DEV * c)
    return slots.reshape(N_DEV * r, c)


def kernel(x, g_pre, w_in, b_gate, g_q, w_uq, g_kv, w_ukv, lb_logits, g_hgrn, w_branch_a, w_branch_b, w_out, g_post, loss_target, m_g_pre, m_w_in, m_b_gate, m_g_q, m_w_uq, m_g_kv, m_w_ukv, m_lb_logits, m_g_hgrn, m_w_branch_a, m_w_branch_b, m_w_out, m_g_post, v_g_pre, v_w_in, v_b_gate, v_g_q, v_w_uq, v_g_kv, v_w_ukv, v_lb_logits, v_g_hgrn, v_w_branch_a, v_w_branch_b, v_w_out, v_g_post):
    rows3 = lambda a: jnp.transpose(a, (2, 0, 1))
    w = dict(w_in=rows3(w_in), w_uq=w_uq[0], w_ukv=w_ukv[0], w_branch_a=w_branch_a[0], w_branch_b=w_branch_b[0],
             w_out=w_out[0], g_pre=g_pre, b_gate=b_gate, g_q=g_q, g_kv=g_kv, lb_logits=lb_logits, g_hgrn=g_hgrn,
             g_post=g_post)
    mom = dict(w_in=rows3(m_w_in), w_uq=m_w_uq[0], w_ukv=m_w_ukv[0], w_branch_a=m_w_branch_a[0],
               w_branch_b=m_w_branch_b[0], w_out=m_w_out[0], g_pre=m_g_pre, b_gate=m_b_gate, g_q=m_g_q, g_kv=m_g_kv,
               lb_logits=m_lb_logits, g_hgrn=m_g_hgrn, g_post=m_g_post)
    var = dict(w_in=rows3(v_w_in), w_uq=v_w_uq[0], w_ukv=v_w_ukv[0], w_branch_a=v_w_branch_a[0],
               w_branch_b=v_w_branch_b[0], w_out=v_w_out[0], g_pre=v_g_pre, b_gate=v_b_gate, g_q=v_g_q, g_kv=v_g_kv,
               lb_logits=v_lb_logits, g_hgrn=v_g_hgrn, g_post=v_g_post)

    w_blk = w["w_in"].reshape(W_IN_SHARD, D_MODEL).astype(BF16)
    dx, recv_in, recv, g_sum = _step(x[0], loss_target[0], w_blk, [w[n].astype(BF16) for n in MATS],
                                     g_pre, b_gate, g_q, g_kv, lb_logits, g_hgrn, g_post)

    g_in, d_in, m_in, v_in = _sum_adamw_w_in(recv_in, w["w_in"], mom["w_in"], var["w_in"])
    res = _sum_adamw_whole([recv[n] for n in MATS], *([t[n] for n in MATS] for t in (w, mom, var)))
    total, *vec = _vectors_adamw(g_sum, *([t[n] for n in SMALL] for t in (w, mom, var)))

    outs = []
    for mats, vecs, big in zip(res, vec, (g_in, d_in, m_in, v_in)):
        t = {**{n: a[None] for n, a in zip(MATS, mats)}, **dict(zip(SMALL, vecs)),
             "w_in": jnp.transpose(big, (1, 2, 0))}
        outs += [t[n] for n in ORDER]
    return (total.reshape(()), dx[None], *outs)
```

```python
import math

import jax
import jax.numpy as jnp
import numpy as np
from jax import lax
from jax.experimental import pallas as pl
from jax.experimental.pallas import tpu as pltpu

F32, BF16 = jnp.float32, jnp.bfloat16

D_MODEL = 1024
EPS = 1e-6
HEADS = 8
NOPE, ROPE, VDIM = 64, 32, 64
QK = NOPE + ROPE
Q_LORA, KV_LORA = 768, 256
ROPE_THETA = 10000.0
ATT_CHUNK_SHIFT = 6
HG_BLOCK = 32
HG_WIDTH = 512
D_IN = 5664
D_IN_PAD = 5760
W_IN_SHARD = D_IN // 8
N_DEV = 8
LANE = 128

ADAM_LR, ADAM_B1, ADAM_B2, ADAM_EPS, ADAM_WD, ADAM_STEP = 0.001, 0.9, 0.999, 1e-08, 0.01, 10

W_IN_SEGMENTS = ((3616, 5664, 0), (1056, 1568, 2048), (3104, 3616, 2560), (1568, 3104, 3072),
                 (0, 1024, 4608), (1024, 1056, 5696))

NT = (((1,), (1,)), ((), ()))
TN = (((0,), (0,)), ((), ()))
MESH_ID = pl.DeviceIdType.MESH


def _w_in_pieces():
    out = []
    for lo, hi, dst in W_IN_SEGMENTS:
        c = lo
        while c < hi:
            p = c // W_IN_SHARD
            e = min(hi, (p + 1) * W_IN_SHARD)
            out.append((p, c - p * W_IN_SHARD, e - p * W_IN_SHARD, dst + c - lo))
            c = e
    return out


def _params(sem, vmem_mb=48):
    return pltpu.CompilerParams(dimension_semantics=sem, vmem_limit_bytes=vmem_mb * 2**20)


def _dot(a, b):
    return jnp.dot(a, b, preferred_element_type=F32)


def _dotg(a, b, dims):
    return lax.dot_general(a, b, dims, preferred_element_type=F32)


def _split2(x):
    hi = x.astype(BF16)
    return hi, (x - hi.astype(F32)).astype(BF16)


def _sel_left(m01, x):
    hi, lo = _split2(x)
    return _dot(m01, hi) + _dot(m01, lo)


def _sel_right(x, m01):
    hi, lo = _split2(x)
    return _dot(hi, m01) + _dot(lo, m01)


def _hi_lo(x):
    hi = x.astype(BF16).astype(F32)
    return hi, x - hi


def _sigmoid(x):
    return 0.5 * jnp.tanh(0.5 * x) + 0.5


def _rope(x, c, s1, s2):
    return x * c + pltpu.roll(x, 112, 1) * s1 + pltpu.roll(x, 16, 1) * s2


def _unrope(d, c, s1, s2):
    return d * c + pltpu.roll(d * s1, 16, 1) + pltpu.roll(d * s2, 112, 1)


def _my_place():
    return lax.axis_index("x"), lax.axis_index("y"), lax.axis_index("c")


def _flip(k, x, y, c):
    fx, fy, fc = (k + 1) >> 2 & 1, (k + 1) >> 1 & 1, (k + 1) & 1
    return (1 - x if fx else x), (1 - y if fy else y), (1 - c if fc else c)


def _to_all_copies(s_refs, r_refs, sems, spread):
    send_sems, recv_sems, local_sems = sems
    x, y, c = _my_place()
    me = 4 * x + 2 * y + c
    src = (lambda a, p: s_refs[a]) if spread else (lambda a, p: s_refs[a].at[p])
    local = [pltpu.make_async_copy(src(a, me), r_refs[a].at[me], local_sems.at[a]) for a in range(len(s_refs))]
    remote = []
    for k in range(N_DEV - 1):
        px, py, pc = _flip(k, x, y, c)
        for a in range(len(s_refs)):
            remote.append(pltpu.make_async_remote_copy(
                src_ref=src(a, 4 * px + 2 * py + pc), dst_ref=r_refs[a].at[me],
                send_sem=send_sems.at[7 * a + k], recv_sem=recv_sems.at[7 * a + k],
                device_id=(px, py, pc), device_id_type=MESH_ID))
    return local, remote


def _to_chips_copies(s_refs, r_refs, sems):
    send_sems, recv_sems, local_sems = sems
    x, y, c = _my_place()
    me = 2 * x + y
    local = [pltpu.make_async_copy(s_refs[a].at[me], r_refs[a].at[me], local_sems.at[a]) for a in range(len(s_refs))]
    remote = []
    for k in range(3):
        px = 1 - x if (k + 1) >> 1 & 1 else x
        py = 1 - y if (k + 1) & 1 else y
        for a in range(len(s_refs)):
            remote.append(pltpu.make_async_remote_copy(
                src_ref=s_refs[a].at[2 * px + py], dst_ref=r_refs[a].at[me],
                send_sem=send_sems.at[3 * a + k], recv_sem=recv_sems.at[3 * a + k],
                device_id=(px, py, c), device_id_type=MESH_ID))
    return local, remote


def _start_all(local, remote):
    for cp in local + remote:
        cp.start()


def _wait_all(local, remote):
    for cp in remote:
        cp.wait_recv()
    for cp in remote:
        cp.wait_send()
    for cp in local:
        cp.wait()


def _copy_sems(n, peers):
    return [pltpu.SemaphoreType.DMA((peers * n,)), pltpu.SemaphoreType.DMA((peers * n,)),
            pltpu.SemaphoreType.DMA((n,))]


ANY = pl.BlockSpec(memory_space=pl.ANY)


def _dw_in_slots(ht, dproj):
    m, k = ht.shape
    n = dproj.shape[1]
    tn, tk = 1152, 1024
    nj, nk = n // tn, k // tk
    by_tile = [[] for _ in range(nj)]
    for p, lo, hi, dst in _w_in_pieces():
        while lo < hi:
            j = dst // tn
            cnt = min(hi - lo, (j + 1) * tn - dst)
            by_tile[j].append((p, lo, lo + cnt, dst - j * tn))
            lo, dst = lo + cnt, dst + cnt

    def body(a_ref, b_ref, s_ref, acc_ref):
        j, l = pl.program_id(0), pl.program_id(1)

        @pl.when(l == 0)
        def _():
            acc_ref[...] = jnp.zeros_like(acc_ref)

        acc_ref[...] += _dot(a_ref[...], b_ref[...])

        @pl.when(l == nk - 1)
        def _():
            at = acc_ref[...].T
            for jj in range(nj):
                @pl.when(j == jj)
                def _(jj=jj):
                    for p, lo, hi, d in by_tile[jj]:
                        s_ref[p, lo:hi, :] = at[d:d + hi - lo, :].astype(BF16)

    return pl.pallas_call(
        body,
        grid=(nj, nk),
        in_specs=[pl.BlockSpec((m, tk), lambda j, l: (0, l)), pl.BlockSpec((tk, tn), lambda j, l: (l, j))],
        out_specs=pl.BlockSpec((N_DEV, W_IN_SHARD, m), lambda j, l: (0, 0, 0)),
        out_shape=jax.ShapeDtypeStruct((N_DEV, W_IN_SHARD, m), BF16),
        scratch_shapes=[pltpu.VMEM((m, tn), F32)],
        compiler_params=_params(("arbitrary", "arbitrary")),
        name="dw_in",
    )(ht, dproj)


PROJ_DT = F32
GP_TN = 256
GP_COLS = 5888
GP_NT = GP_COLS // GP_TN


def _gp_tile_pieces():
    tiles = [[] for _ in range(GP_NT)]
    for p, lo, hi, dst in _w_in_pieces():
        while lo < hi:
            t = dst // GP_TN
            n = min(hi - lo, (t + 1) * GP_TN - dst)
            tiles[t].append((p, lo, lo + n, dst - t * GP_TN))
            lo, dst = lo + n, dst + n
    return tiles


def _gp_tables():
    pieces = _gp_tile_pieces()
    rank_of = {None: 0, 0: 1, 1: 2, 2: 2, 4: 3, 5: 3, 3: 4, 6: 5}
    order = np.zeros((N_DEV, GP_NT), np.int32)
    waits = np.zeros((N_DEV, GP_NT), np.int32)
    for me in range(N_DEV):
        x, y, c = me >> 2 & 1, me >> 1 & 1, me & 1
        chips = [(1 - x, y), (x, 1 - y), (1 - x, 1 - y)]

        def sem_of(p):
            px, py, pc = p >> 2 & 1, p >> 1 & 1, p & 1
            if (px, py) == (x, y):
                return None if pc == c else 0
            j = chips.index((px, py))
            return 1 + j if pc == c else 4 + j

        needs = [sorted({sem_of(p) for p, _, _, _ in tile} - {None}) for tile in pieces]
        ranks = [max([rank_of[k] for k in ks], default=0) for ks in needs]
        seq = sorted(range(GP_NT), key=lambda t: (ranks[t], t))
        seen = set()
        for step, t in enumerate(seq):
            order[me, step] = t
            new = [k for k in needs[t] if k not in seen]
            for k in new:
                waits[me, step] |= 1 << k
            seen.update(new)
        assert seen == set(range(7)), (me, seen)
    return order, waits


def _gather_proj(x, g_pre, w_blk, shards):
    s = x.shape[0]
    tx = 512
    ns = len(shards)
    tile_pieces = _gp_tile_pieces()
    order_np, waits_np = _gp_tables()
    xq, yq, cq = _my_place()
    me_out = 4 * xq + 2 * yq + cq
    order = lax.dynamic_index_in_dim(jnp.asarray(order_np), me_out, 0, keepdims=False)
    waits = lax.dynamic_index_in_dim(jnp.asarray(waits_np), me_out, 0, keepdims=False)

    def body(order_ref, waits_ref, x_hbm, g_ref, wblk_hbm, *rest):
        shard_refs, (proj_ref, wt_ref, ht_hbm), got_refs = rest[:ns], rest[ns:ns + 3], rest[ns + 3:2 * ns + 3]
        recv, h_ref, wtile, xbuf, htbuf = rest[2 * ns + 3:2 * ns + 8]
        send_sems, recv_sems, misc_sems = rest[2 * ns + 8:2 * ns + 11]
        sems = rest[2 * ns + 11:]
        t = pl.program_id(0)
        x_, y_, c = _my_place()
        sibling = (x_, y_, 1 - c)
        chips = [(1 - x_, y_), (x_, 1 - y_), (1 - x_, 1 - y_)]
        idx = lambda px, py, pc: 4 * px + 2 * py + pc
        me = idx(x_, y_, c)

        def copy(k, slot, to, src=None):
            return pltpu.make_async_remote_copy(
                src_ref=recv.at[slot] if src is None else src, dst_ref=recv.at[slot],
                send_sem=send_sems.at[k], recv_sem=recv_sems.at[k], device_id=to, device_id_type=MESH_ID)

        mine = pltpu.make_async_copy(wblk_hbm, recv.at[me], misc_sems.at[0])
        first = [copy(0, me, sibling, src=wblk_hbm)] + [copy(1 + j, me, (*chips[j], c), src=wblk_hbm) for j in range(2)]
        passed = [copy(4 + j, idx(*ch, c), sibling) for j, ch in enumerate(chips)]
        onward = [copy(3, idx(*chips[0], c), (*chips[1], c)), copy(3, idx(*chips[1], c), (*chips[0], c))]
        arrivals = ([copy(0, idx(x_, y_, 1 - c), sibling)] + [copy(1 + j, idx(*ch, c), sibling) for j, ch in enumerate(chips)]
                    + [copy(4 + j, idx(*ch, 1 - c), sibling) for j, ch in enumerate(chips)])

        @pl.when(t == 0)
        def _():
            mine.start()
            for cp in first:
                cp.start()
            _start_all(*_to_all_copies(shard_refs, got_refs, sems, True))

            def load(i):
                return pltpu.make_async_copy(x_hbm.at[pl.ds(i * tx, tx), :], xbuf.at[i & 1], misc_sems.at[1 + (i & 1)])

            def store(i):
                return pltpu.make_async_copy(htbuf.at[i & 1], ht_hbm.at[:, pl.ds(i * tx, tx)], misc_sems.at[3 + (i & 1)])

            load(0).start()
            for i in range(s // tx):
                if i + 1 < s // tx:
                    load(i + 1).start()
                load(i).wait()
                xv = xbuf[i & 1]
                r = lax.rsqrt(jnp.mean(xv * xv, axis=-1, keepdims=True) + EPS)
                h = (xv * r * g_ref[...]).astype(BF16)
                h_ref[i * tx:(i + 1) * tx, :] = h
                if i >= 2:
                    store(i - 2).wait()
                htbuf[i & 1] = h.T
                store(i).start()
            for i in range(max(s // tx - 2, 0), s // tx):
                store(i).wait()
            mine.wait()

        w = waits_ref[t]
        for k in range(7):
            @pl.when((w >> k) & 1 == 1)
            def _(k=k):
                arrivals[k].wait_recv()
                if 1 <= k <= 3:
                    passed[k - 1].start()
                if 1 <= k <= 2:
                    @pl.when(c == k - 1)
                    def _():
                        onward[k - 1].start()

        tile = order_ref[t]
        for tt in range(GP_NT):
            @pl.when(tile == tt)
            def _(tt=tt):
                covered = sorted((d, d + hi - lo) for _, lo, hi, d in tile_pieces[tt])
                at = 0
                for lo_z, hi_z in covered + [(GP_TN, GP_TN)]:
                    if lo_z > at:
                        wtile[at:lo_z, :] = jnp.zeros((lo_z - at, D_MODEL), BF16)
                    at = max(at, hi_z)
                for p, lo, hi, d in tile_pieces[tt]:
                    wtile[d:d + hi - lo, :] = recv[p, lo:hi, :]

        wt = wtile[...]
        wt_ref[...] = wt
        proj_ref[...] = _dotg(h_ref[...], wt, NT).astype(PROJ_DT)

        @pl.when(t == GP_NT - 1)
        def _():
            for cp in first + passed + onward[:1]:
                cp.wait_send()
            _wait_all(*_to_all_copies(shard_refs, got_refs, sems, True))

    grid_spec = pltpu.PrefetchScalarGridSpec(
        num_scalar_prefetch=2,
        grid=(GP_NT,),
        in_specs=[ANY, pl.BlockSpec((1, D_MODEL), lambda t, o, w: (0, 0)), ANY] + [ANY] * ns,
        out_specs=[pl.BlockSpec((s, GP_TN), lambda t, o, w: (0, o[t])),
                   pl.BlockSpec((GP_TN, D_MODEL), lambda t, o, w: (o[t], 0)), ANY] + [ANY] * ns,
        scratch_shapes=[pltpu.VMEM((N_DEV, W_IN_SHARD, D_MODEL), BF16), pltpu.VMEM((s, D_MODEL), BF16),
                        pltpu.VMEM((GP_TN, D_MODEL), BF16), pltpu.VMEM((2, tx, D_MODEL), F32),
                        pltpu.VMEM((2, D_MODEL, tx), BF16),
                        pltpu.SemaphoreType.DMA((7,)), pltpu.SemaphoreType.DMA((7,)), pltpu.SemaphoreType.DMA((5,))]
        + _copy_sems(ns, 7),
    )
    return pl.pallas_call(
        body,
        grid_spec=grid_spec,
        out_shape=[jax.ShapeDtypeStruct((s, GP_COLS), PROJ_DT), jax.ShapeDtypeStruct((GP_COLS, D_MODEL), BF16),
                   jax.ShapeDtypeStruct((D_MODEL, s), BF16)]
        + [jax.ShapeDtypeStruct((N_DEV,) + b.shape, b.dtype) for b in shards],
        compiler_params=_params(("arbitrary",), 56),
        name="gather_proj",
    )(order, waits, x, g_pre, w_blk, *shards)


def _mla_prep(proj, g_q, g_kv, w_uq_p, w_kv_p, rc, rs1, rs2):
    s = proj.shape[0]
    tm = 256
    scale = 1.0 / math.sqrt(QK)

    def body(cq_ref, ckv_ref, kpe_ref, gq_ref, gkv_ref, wuq_ref, wkv_ref, c_ref, s1_ref, s2_ref,
             qr_ref, kr_ref, v_ref, cqt_ref, ckvt_ref):
        cq = cq_ref[...].astype(F32)
        r = lax.rsqrt(jnp.mean(cq * cq, axis=-1, keepdims=True) + EPS)
        cqn = (cq * r * gq_ref[...]).astype(BF16)
        cqt_ref[...] = cqn.T
        q = _dot(cqn, wuq_ref[...])
        ckv = ckv_ref[...].astype(F32)
        r = lax.rsqrt(jnp.mean(ckv * ckv, axis=-1, keepdims=True) + EPS)
        ckvn = (ckv * r * gkv_ref[...]).astype(BF16)
        ckvt_ref[...] = ckvn.T
        kv = _dot(ckvn, wkv_ref[...])
        c, s1, s2 = c_ref[...], s1_ref[...], s2_ref[...]
        lane = lax.broadcasted_iota(jnp.int32, (tm, LANE), 1)
        kpe = _rope(kpe_ref[...].astype(F32), c, s1, s2) + jnp.where((lane == QK) | (lane == QK + 1), 1.0, 0.0)
        vone = jnp.where((lane == VDIM) | (lane == VDIM + 1), 1.0, 0.0)
        for h in range(HEADS):
            sl = slice(LANE * h, LANE * (h + 1))
            qr_ref[:, sl] = (_rope(q[:, sl], c, s1, s2) * scale).astype(BF16)
            kr_ref[:, sl] = (kv[:, sl] + kpe).astype(BF16)
            v_ref[:, sl] = (kv[:, HEADS * LANE + LANE * h:HEADS * LANE + LANE * (h + 1)] + vone).astype(BF16)

    row = lambda w, j: pl.BlockSpec((tm, w), lambda i: (i, j))
    col = lambda w: pl.BlockSpec((w, tm), lambda i: (0, i))
    full = lambda a: pl.BlockSpec(a.shape, lambda i: (0, 0))
    return pl.pallas_call(
        body,
        grid=(s // tm,),
        in_specs=[row(768, 6), row(256, 21), row(128, 44), full(g_q), full(g_kv), full(w_uq_p), full(w_kv_p),
                  row(128, 0), row(128, 0), row(128, 0)],
        out_specs=[row(1024, 0), row(1024, 0), row(1024, 0), col(768), col(256)],
        out_shape=[jax.ShapeDtypeStruct((s, 1024), BF16), jax.ShapeDtypeStruct((s, 1024), BF16),
                   jax.ShapeDtypeStruct((s, 1024), BF16), jax.ShapeDtypeStruct((768, s), BF16),
                   jax.ShapeDtypeStruct((256, s), BF16)],
        compiler_params=_params(("arbitrary",)),
        name="mla_prep",
    )(proj, proj, proj, g_q, g_kv, w_uq_p, w_kv_p, rc, rs1, rs2)


ATT_T = 512
ATT_FWD_HEADS = 4


def _chunk_mask(transposed):
    r = lax.broadcasted_iota(jnp.int32, (ATT_T, ATT_T), 0) >> ATT_CHUNK_SHIFT
    c = lax.broadcasted_iota(jnp.int32, (ATT_T, ATT_T), 1) >> ATT_CHUNK_SHIFT
    return (r <= c) if transposed else (c <= r)


def _attn_fwd(qr, kr, vp, shards):
    s = qr.shape[0]
    t = ATT_T
    g = ATT_FWD_HEADS
    ns = len(shards)

    def body(q_ref, k_ref, v_ref, *rest):
        shard_refs, (o_ref, qa_ref), got_refs = rest[:ns], rest[ns:ns + 2], rest[ns + 2:2 * ns + 2]
        sc_ref, sems = rest[2 * ns + 2], rest[2 * ns + 3:]
        qi = pl.program_id(1)

        @pl.when((pl.program_id(0) == 0) & (qi == 0))
        def _():
            _start_all(*_to_all_copies(shard_refs, got_refs, sems, True))
        lane = lax.broadcasted_iota(jnp.int32, (t, LANE), 1)
        sls = [slice(LANE * a, LANE * (a + 1)) for a in range(g)]
        qs = [q_ref[:, sl] for sl in sls]

        def scores(j):
            rows = pl.ds(pl.multiple_of(j * t, t), t)
            for a in range(g):
                sc_ref[j & 1, a] = _dotg(qs[a], k_ref[rows, sls[a]], NT)

        def step(j, carry, masked):
            rows = pl.ds(pl.multiple_of(j * t, t), t)
            out = []
            for a in range(g):
                m, acc = carry[a]
                sc = sc_ref[j & 1, a]
                if masked:
                    sc = jnp.where(_chunk_mask(False), sc, -1e30)
                m_new = jnp.maximum(m, jnp.max(sc, axis=-1, keepdims=True))
                p = jnp.exp(sc - m_new).astype(BF16)
                acc = jnp.exp(m - m_new) * acc + _dot(p, v_ref[rows, sls[a]])
                out.append((m_new, acc))
            return tuple(out)

        def loop(j, carry):
            carry = step(j, carry, False)
            scores(j + 1)
            return carry

        init = tuple((jnp.full((t, 1), -1e30, F32), jnp.zeros((t, LANE), F32)) for _ in range(g))
        scores(0)
        carry = lax.fori_loop(0, qi, loop, init)
        carry = step(qi, carry, True)
        outs = []
        for a in range(g):
            m, acc = carry[a]
            l = acc[:, VDIM:VDIM + 1]
            outs.append(acc / l)
            hi, lo_part = _hi_lo(-(m + jnp.log(l)))
            qa = jnp.where(lane == QK, hi, jnp.where(lane == QK + 1, lo_part, qs[a].astype(F32)))
            qa_ref[:, sls[a]] = qa.astype(BF16)
        for p in range(g // 2):
            o_ref[:, LANE * p:LANE * (p + 1)] = jnp.where(lane < VDIM, outs[2 * p], pltpu.roll(outs[2 * p + 1], VDIM, 1))

        @pl.when((pl.program_id(0) == HEADS // g - 1) & (qi == s // t - 1))
        def _():
            _wait_all(*_to_all_copies(shard_refs, got_refs, sems, True))

    return pl.pallas_call(
        body,
        grid=(HEADS // g, s // t),
        in_specs=[
            pl.BlockSpec((t, g * LANE), lambda h, i: (i, h)),
            pl.BlockSpec((s, g * LANE), lambda h, i: (0, h)),
            pl.BlockSpec((s, g * LANE), lambda h, i: (0, h)),
        ] + [ANY] * ns,
        out_specs=[
            pl.BlockSpec((t, g * VDIM), lambda h, i: (i, h)),
            pl.BlockSpec((t, g * LANE), lambda h, i: (i, h)),
        ] + [ANY] * ns,
        out_shape=[jax.ShapeDtypeStruct((s, 512), F32), jax.ShapeDtypeStruct((s, 1024), BF16)]
        + [jax.ShapeDtypeStruct((N_DEV,) + b.shape, b.dtype) for b in shards],
        scratch_shapes=[pltpu.VMEM((2, g, t, t), F32)] + _copy_sems(ns, 7),
        compiler_params=_params(("arbitrary", "arbitrary")),
        name="attn_fwd",
    )(qr, kr, vp, *shards)


def _attn_bwd(qa, kr, vp, dop, sends):
    s = qa.shape[0]
    t = ATT_T
    nq = s // t
    ns = len(sends)

    def body(q_ref, k_ref, v_ref, do_ref, *rest):
        send_refs, (dq_out, dk_out, dv_out) = rest[:ns], rest[ns:ns + 3]
        recv_refs = rest[ns + 3:2 * ns + 3]
        (dq_ref, dk_ref, dv_ref), sems = rest[2 * ns + 3:2 * ns + 6], rest[2 * ns + 6:]
        j = pl.program_id(1)
        sls = [slice(LANE * a, LANE * (a + 1)) for a in range(2)]

        @pl.when((pl.program_id(0) == 0) & (j == 0))
        def _():
            _start_all(*_to_all_copies(send_refs, recv_refs, sems, False))

        @pl.when(j == 0)
        def _():
            dq_ref[...] = jnp.zeros_like(dq_ref)

        dk_ref[...] = jnp.zeros_like(dk_ref)
        dv_ref[...] = jnp.zeros_like(dv_ref)
        ks = [k_ref[:, sl] for sl in sls]
        vs = [v_ref[:, sl] for sl in sls]

        def part(i, k_lo, k_n, q_lo, q_n, masked):
            rows = pl.ds(pl.multiple_of(i * t + q_lo, 256), q_n)
            keys = slice(k_lo, k_lo + k_n)
            for a in range(2):
                q = q_ref[rows, sls[a]]
                do = do_ref[rows, sls[a]]
                sc = _dotg(ks[a][keys], q, NT)
                if masked:
                    kc = lax.broadcasted_iota(jnp.int32, (k_n, q_n), 0) >> ATT_CHUNK_SHIFT
                    qc = lax.broadcasted_iota(jnp.int32, (k_n, q_n), 1) >> ATT_CHUNK_SHIFT
                    sc = jnp.where(kc <= qc, sc, -1e30)
                p = jnp.exp(sc)
                ds = (p * _dotg(vs[a][keys], do, NT)).astype(BF16)
                dv_ref[keys, sls[a]] += _dot(p.astype(BF16), do)
                dk_ref[keys, sls[a]] += _dot(ds, q)
                dq_ref[rows, sls[a]] += _dotg(ds, ks[a][keys], TN)

        half = t // 2
        part(j, 0, half, 0, t, True)
        part(j, half, half, half, half, True)

        def loop(i, c):
            part(i, 0, t, 0, t, False)
            return c

        lax.fori_loop(j + 1, nq, loop, 0)
        dk_out[...] = dk_ref[...].astype(BF16)
        dv_out[...] = dv_ref[...].astype(BF16)

        @pl.when(j == nq - 1)
        def _():
            dq_out[...] = dq_ref[...].astype(BF16)

        @pl.when((pl.program_id(0) == HEADS // 2 - 1) & (j == nq - 1))
        def _():
            _wait_all(*_to_all_copies(send_refs, recv_refs, sems, False))

    blk = pl.BlockSpec((t, 2 * LANE), lambda h, j: (j, h))
    whole = pl.BlockSpec((s, 2 * LANE), lambda h, j: (0, h))
    out = jax.ShapeDtypeStruct((s, 1024), BF16)
    return pl.pallas_call(
        body,
        grid=(HEADS // 2, nq),
        in_specs=[whole, blk, blk, whole] + [ANY] * ns,
        out_specs=[whole, blk, blk] + [ANY] * ns,
        out_shape=[out, out, out] + [jax.ShapeDtypeStruct(a.shape, a.dtype) for a in sends],
        scratch_shapes=[pltpu.VMEM((s, 2 * LANE), F32), pltpu.VMEM((t, 2 * LANE), F32),
                        pltpu.VMEM((t, 2 * LANE), F32)] + _copy_sems(ns, 7),
        compiler_params=_params(("arbitrary", "arbitrary")),
        name="attn_bwd",
    )(qa, kr, vp, dop, *sends)


HG_T = 256
HG_NC = HG_T // HG_BLOCK
HG_G = 4
GW = 64 * HG_G


def _hg_consts():
    r = jnp.arange(HG_T)[:, None]
    c = jnp.arange(HG_T)[None, :]
    same = (r // HG_BLOCK) == (c // HG_BLOCK)
    mcum = (same & (c <= r)).astype(BF16)
    mrev = (same & (c >= r)).astype(BF16)
    msum = same.astype(BF16)
    a = jnp.arange(GW) // 64
    bd = (a[:, None] == a[None, :]).astype(F32)
    return mcum, mrev, msum, bd


def _stack_heads(xg, head):
    return jnp.concatenate([jnp.where(head == h, xg, 0.0) for h in range(HG_G)], axis=0)


def _unstack_heads(r, head, t):
    out = r[(HG_G - 1) * t:]
    for h in range(HG_G - 2, -1, -1):
        out = jnp.where(head == h, r[h * t:(h + 1) * t], out)
    return out


def _compact_state(st):
    out = st[:64]
    for h in range(1, HG_G):
        out = out + st[64 * h:64 * (h + 1)]
    return out


def _expand_state(cs, head64):
    return jnp.concatenate([jnp.where(head64 == h, cs, 0.0) for h in range(HG_G)], axis=0)


def _hg_pre(hq, hf, lbl, mcum, msum):
    lb = _sigmoid(lbl[0:1, :] - lbl[1:2, :])
    sig = _sigmoid(hf)
    f = lb + (1.0 - lb) * sig
    lf = jnp.log(f)
    b = _sel_left(mcum, lf)
    big_l = _sel_left(msum, lf)
    k = 1.0 - f
    qd = hq * jnp.exp(b)
    ki = k * jnp.exp(-b)
    ke = k * jnp.exp(big_l - b)
    return lb, sig, f, b, big_l, qd, ki, ke


def _hgrn_fwd(proj, lbl):
    s = proj.shape[0]
    t = HG_T
    mcum, _, msum, bd = _hg_consts()

    def body(hq_ref, hf_ref, hi_ref, lbl_ref, mcum_ref, msum_ref, bd_ref, o_ref, sp_ref, st_ref):
        @pl.when(pl.program_id(0) == 0)
        def _():
            st_ref[...] = jnp.zeros_like(st_ref)

        mc = mcum_ref[...]
        _, _, _, _, big_l, qd, ki, ke = _hg_pre(hq_ref[...].astype(F32), hf_ref[...].astype(F32), lbl_ref[...], mc,
                                                msum_ref[...])
        el = jnp.exp(big_l)
        hi = hi_ref[...]
        head = lax.broadcasted_iota(jnp.int32, (t, GW), 1) >> 6
        mask = jnp.concatenate([mc] * HG_G, axis=0) > 0.5
        for p in range(HEADS // HG_G):
            sl = slice(GW * p, GW * (p + 1))
            vp = hi[:, sl].astype(BF16)
            qs = _stack_heads(qd[:, sl], head).astype(BF16)
            a = jnp.where(mask, _dotg(qs, ki[:, sl].astype(BF16), NT), 0.0)
            o_intra = _unstack_heads(_dot(a.astype(BF16), vp), head, t)
            qb = qd[:, sl].astype(BF16)
            kb = ke[:, sl].astype(BF16)
            st = st_ref[p]
            for c in range(HG_NC):
                rows = slice(HG_BLOCK * c, HG_BLOCK * (c + 1))
                sp_ref[c, :, sl] = _compact_state(st)
                o_ref[rows, sl] = o_intra[rows] + _dotg(qb[rows], st.astype(BF16), NT)
                u = _dotg(vp[rows], kb[rows], TN) * bd_ref[...]
                st = st * el[HG_BLOCK * c:HG_BLOCK * c + 1, sl] + u
            st_ref[p] = st

    row = lambda j: pl.BlockSpec((t, HG_WIDTH), lambda i: (i, j))
    full = lambda a: pl.BlockSpec(a.shape, lambda i: (0, 0))
    return pl.pallas_call(
        body,
        grid=(s // t,),
        in_specs=[row(6), row(7), row(8), full(lbl), full(mcum), full(msum), full(bd)],
        out_specs=[row(0), pl.BlockSpec((HG_NC, 64, HG_WIDTH), lambda i: (i, 0, 0))],
        out_shape=[jax.ShapeDtypeStruct((s, HG_WIDTH), F32),
                   jax.ShapeDtypeStruct((s // HG_BLOCK, 64, HG_WIDTH), F32)],
        scratch_shapes=[pltpu.VMEM((HEADS // HG_G, GW, GW), F32)],
        compiler_params=_params(("arbitrary",)),
        name="hgrn_fwd",
    )(proj, proj, proj, lbl, mcum, msum, bd)


def _slot_shape(name, r, c):
    return (N_DEV, r, c // N_DEV) if COL_SHARDED[name] else (N_DEV, r // N_DEV, c)


def _emit_slots(name, acc_ref, out_ref):
    r, c = acc_ref.shape
    for p in range(N_DEV):
        if COL_SHARDED[name]:
            out_ref[p] = acc_ref[:, c // N_DEV * p:c // N_DEV * (p + 1)].astype(BF16)
        else:
            out_ref[p] = acc_ref[r // N_DEV * p:r // N_DEV * (p + 1), :].astype(BF16)


def _hgrn_bwd(proj, lbl, do, sprev, dproj, pairs):
    s = proj.shape[0]
    t = HG_T
    nt = s // t
    npair = len(pairs)
    mcum, mrev, msum, bd = _hg_consts()

    def body(hq_ref, hf_ref, hi_ref, lbl_ref, do_ref, sp_ref, mcum_ref, mrev_ref, msum_ref, bd_ref,
             dproj_in, *rest):
        del dproj_in
        pair_refs, (dh_ref, dlbl_ref) = rest[:2 * npair], rest[2 * npair:2 * npair + 2]
        dw_refs, g_ref, acc_refs = rest[2 * npair + 2:3 * npair + 2], rest[3 * npair + 2], rest[3 * npair + 3:]

        @pl.when(pl.program_id(0) == 0)
        def _():
            g_ref[...] = jnp.zeros_like(g_ref)
            dlbl_ref[...] = jnp.zeros_like(dlbl_ref)
            for acc_ref in acc_refs:
                acc_ref[...] = jnp.zeros_like(acc_ref)

        for n, acc_ref in enumerate(acc_refs):
            acc_ref[...] += _dot(pair_refs[2 * n][...], pair_refs[2 * n + 1][...])

        @pl.when(pl.program_id(0) == nt - 1)
        def _():
            for (name, _, _), acc_ref, dw_ref in zip(pairs, acc_refs, dw_refs):
                _emit_slots(name, acc_ref, dw_ref)

        mc = mcum_ref[...]
        lb, sig, f, b, big_l, qd, ki, ke = _hg_pre(hq_ref[...].astype(F32), hf_ref[...].astype(F32), lbl_ref[...], mc,
                                                   msum_ref[...])
        el = jnp.exp(big_l)
        hi = hi_ref[...]
        dov = do_ref[...]
        head = lax.broadcasted_iota(jnp.int32, (t, GW), 1) >> 6
        head64 = lax.broadcasted_iota(jnp.int32, (64, GW), 1) >> 6
        mask = jnp.concatenate([mc] * HG_G, axis=0) > 0.5
        dqd_parts, dke_parts, dv_parts, del_parts, dki_parts = [], [], [], [], []
        for p in range(HEADS // HG_G):
            sl = slice(GW * p, GW * (p + 1))
            vp = hi[:, sl].astype(BF16)
            qs = _stack_heads(qd[:, sl], head).astype(BF16)
            kip = ki[:, sl].astype(BF16)
            dos = _stack_heads(dov[:, sl], head).astype(BF16)
            a = jnp.where(mask, _dotg(qs, kip, NT), 0.0).astype(BF16)
            da = jnp.where(mask, _dotg(dos, vp, NT), 0.0).astype(BF16)
            r = _dot(da, kip)
            dki_parts.append(_dotg(da, qs, TN))
            qb = qd[:, sl].astype(BF16)
            kb = ke[:, sl].astype(BF16)
            dob = dov[:, sl].astype(BF16)
            g = g_ref[p]
            dqd_c, dv_c, dke_c, del_c = [], [], [], []
            for c in range(HG_NC - 1, -1, -1):
                rows = slice(HG_BLOCK * c, HG_BLOCK * (c + 1))
                gb = g.astype(BF16)
                st = _expand_state(sp_ref[c, :, sl], head64)
                dqd_c.append(_dot(dob[rows], st.astype(BF16)))
                dv_c.append(_dotg(kb[rows], gb, NT))
                dke_c.append(_dot(vp[rows], gb))
                del_c.append(jnp.broadcast_to(jnp.sum(g * st, axis=0, keepdims=True), (HG_BLOCK, GW)))
                g = g * el[HG_BLOCK * c:HG_BLOCK * c + 1, sl] + _dotg(dob[rows], qb[rows], TN) * bd_ref[...]
            g_ref[p] = g
            up = lambda parts: jnp.concatenate(parts[::-1], axis=0)
            dqd_parts.append(_unstack_heads(r, head, t) + up(dqd_c))
            dv_parts.append(_dotg(a, dos, TN) + up(dv_c))
            dke_parts.append(up(dke_c))
            del_parts.append(up(del_c))
        wide = lambda parts: jnp.concatenate(parts, axis=1)
        dqd, dke, dki, dvv, del_rows = wide(dqd_parts), wide(dke_parts), wide(dki_parts), wide(dv_parts), wide(del_parts)
        dh_ref[:, :HG_WIDTH] = (dqd * jnp.exp(b)).astype(BF16)
        dh_ref[:, 2 * HG_WIDTH:] = dvv.astype(BF16)
        dke_ke = dke * ke
        db = dqd * qd - dki * ki - dke_ke
        dl_rows = _sel_left(msum_ref[...], dke_ke) + del_rows * el
        is_last = (lax.broadcasted_iota(jnp.int32, (t, HG_WIDTH), 0) & (HG_BLOCK - 1)) == HG_BLOCK - 1
        db = db + jnp.where(is_last, dl_rows, 0.0)
        dlf = _sel_left(mrev_ref[...], db)
        dk = dki * jnp.exp(-b) + dke * jnp.exp(big_l - b)
        df = dlf / f - dk
        dh_ref[:, HG_WIDTH:2 * HG_WIDTH] = (df * (1.0 - lb) * sig * (1.0 - sig)).astype(BF16)
        dlb = jnp.sum(df * (1.0 - sig), axis=0, keepdims=True) * lb * (1.0 - lb)
        dlbl_ref[0:1, :] += dlb
        dlbl_ref[1:2, :] -= dlb

    rrow = lambda j: pl.BlockSpec((t, HG_WIDTH), lambda i: (nt - 1 - i, j))
    full = lambda a: pl.BlockSpec(a.shape, lambda i: (0, 0))
    pair_specs, dw_specs, dw_shapes, accs = [], [], [], []
    for name, at, b in pairs:
        pair_specs += [pl.BlockSpec((at.shape[0], t), lambda i: (0, i)), pl.BlockSpec((t, b.shape[1]), lambda i: (i, 0))]
        shape = _slot_shape(name, at.shape[0], b.shape[1])
        dw_specs.append(pl.BlockSpec(shape, lambda i: (0, 0, 0)))
        dw_shapes.append(jax.ShapeDtypeStruct(shape, BF16))
        accs.append(pltpu.VMEM((at.shape[0], b.shape[1]), F32))
    return pl.pallas_call(
        body,
        grid=(nt,),
        in_specs=[rrow(6), rrow(7), rrow(8), full(lbl), rrow(0),
                  pl.BlockSpec((HG_NC, 64, HG_WIDTH), lambda i: (nt - 1 - i, 0, 0)),
                  full(mcum), full(mrev), full(msum), full(bd), pl.BlockSpec(memory_space=pl.ANY)] + pair_specs,
        out_specs=[pl.BlockSpec((t, 3 * HG_WIDTH), lambda i: (nt - 1 - i, 2)),
                   pl.BlockSpec((2, HG_WIDTH), lambda i: (0, 0))] + dw_specs,
        out_shape=[jax.ShapeDtypeStruct(dproj.shape, BF16), jax.ShapeDtypeStruct((2, HG_WIDTH), F32)] + dw_shapes,
        input_output_aliases={10: 0},
        scratch_shapes=[pltpu.VMEM((HEADS // HG_G, GW, GW), F32)] + accs,
        compiler_params=_params(("arbitrary",)),
        name="hgrn_bwd",
    )(proj, proj, proj, lbl, do, sprev, mcum, mrev, msum, bd, dproj, *[a for pair in pairs for a in pair[1:]])


def _tail(x, tgt, proj, attn, o, w_a, w_b, w_out, w_at, w_bt, w_outt, b_gate, g_post, gh):
    s = x.shape[0]
    tm = 256
    ones64 = (jnp.arange(HG_WIDTH)[:, None] // 64 == jnp.arange(HG_WIDTH)[None, :] // 64).astype(BF16)
    weights = (w_a, w_b, w_out, w_at, w_bt, w_outt)

    def body(x_ref, t_ref, ml_ref, ga_ref, gb_ref, at_ref, o_ref, *rest):
        w_hbm, (bg_ref, gp_ref, gh_ref, ones_ref) = rest[:6], rest[6:10]
        (dout_ref, dpj_ref, dop_ref, do_ref, mt_ref, dy_ref, yat_ref, dya_ref, ybt_ref, dyb_ref,
         loss_ref, dgp_ref, dbg_ref, dgh_ref) = rest[10:24]
        (wa_ref, wb_ref, wo_ref, wat_ref, wbt_ref, wot_ref), w_sem = rest[24:30], rest[30]

        @pl.when(pl.program_id(0) == 0)
        def _():
            loads = [pltpu.make_async_copy(src, dst, w_sem.at[k])
                     for k, (src, dst) in enumerate(zip(w_hbm, rest[24:30]))]
            _start_all(loads, [])
            loss_ref[...] = jnp.zeros_like(loss_ref)
            dgp_ref[...] = jnp.zeros_like(dgp_ref)
            dbg_ref[...] = jnp.zeros_like(dbg_ref)
            dgh_ref[...] = jnp.zeros_like(dgh_ref)
            _wait_all(loads, [])

        ones = ones_ref[...]
        gate_a = ga_ref[...].astype(F32)
        sa = _sigmoid(gate_a)
        silu_a = gate_a * sa
        attn_v = at_ref[...]
        ya_in = attn_v * silu_a
        ov = o_ref[...]
        ro = lax.rsqrt(_sel_right(ov * ov, ones) * (1.0 / 64.0) + EPS)
        ohat = ov * ro
        ghv = gh_ref[...]
        on = ohat * ghv
        gate_b = gb_ref[...].astype(F32)
        sb = _sigmoid(gate_b)
        silu_b = gate_b * sb
        yb_in = on * silu_b
        ya_bf = ya_in.astype(BF16)
        yb_bf = yb_in.astype(BF16)
        yat_ref[...] = ya_bf.T
        ybt_ref[...] = yb_bf.T
        y_a = _dot(ya_bf, wa_ref[...])
        y_b = _dot(yb_bf, wb_ref[...])
        gts = _sigmoid(ml_ref[...].astype(F32) + bg_ref[...])
        g_a = gts[:, :D_MODEL]
        g_b = gts[:, D_MODEL:]
        m_bf = (g_a * y_a + g_b * y_b).astype(BF16)
        mt_ref[...] = m_bf.T
        y = _dot(m_bf, wo_ref[...])
        r1 = lax.rsqrt(jnp.mean(y * y, axis=-1, keepdims=True) + EPS)
        yn = y * r1
        gp = gp_ref[...]
        e = x_ref[...] + yn * gp - t_ref[...]
        loss_ref[...] += jnp.sum(e * e, axis=0, keepdims=True)
        dout = e * (1.0 / D_MODEL)
        dout_ref[...] = dout
        dgp_ref[...] += jnp.sum(dout * yn, axis=0, keepdims=True)
        dyn = dout * gp
        dy = r1 * (dyn - yn * jnp.mean(dyn * yn, axis=-1, keepdims=True))
        dy_bf = dy.astype(BF16)
        dy_ref[...] = dy_bf
        dm = _dot(dy_bf, wot_ref[...])
        dml_a = dm * y_a * g_a * (1.0 - g_a)
        dml_b = dm * y_b * g_b * (1.0 - g_b)
        dpj_ref[:, :D_MODEL] = dml_a.astype(BF16)
        dpj_ref[:, D_MODEL:2 * D_MODEL] = dml_b.astype(BF16)
        dbg_ref[:, :D_MODEL] += jnp.sum(dml_a, axis=0, keepdims=True)
        dbg_ref[:, D_MODEL:] += jnp.sum(dml_b, axis=0, keepdims=True)
        dya_bf = (dm * g_a).astype(BF16)
        dyb_bf = (dm * g_b).astype(BF16)
        dya_ref[...] = dya_bf
        dyb_ref[...] = dyb_bf
        dya_in = _dot(dya_bf, wat_ref[...])
        dyb_in = _dot(dyb_bf, wbt_ref[...])
        dattn = dya_in * silu_a
        delta = _sel_right(dattn * attn_v, ones)
        lane = lax.broadcasted_iota(jnp.int32, (tm, LANE), 1)
        for p in range(HEADS // 2):
            sl = slice(LANE * p, LANE * (p + 1))
            xs = (dattn[:, sl], pltpu.roll(dattn[:, sl], VDIM, 1))
            nds = (-pltpu.roll(delta[:, sl], VDIM, 1), -delta[:, sl])
            for a in range(2):
                hi, lo_part = _hi_lo(nds[a])
                blk = jnp.where(lane < VDIM, xs[a], jnp.where(lane == VDIM, hi, jnp.where(lane == VDIM + 1, lo_part, 0.0)))
                dop_ref[:, LANE * (2 * p + a):LANE * (2 * p + a + 1)] = blk.astype(BF16)
        dpj_ref[:, 2 * D_MODEL:2 * D_MODEL + HG_WIDTH] = (
            dya_in * attn_v * (sa * (1.0 + gate_a * (1.0 - sa)))).astype(BF16)
        don = dyb_in * silu_b
        dpj_ref[:, 2 * D_MODEL + HG_WIDTH:] = (dyb_in * on * (sb * (1.0 + gate_b * (1.0 - sb)))).astype(BF16)
        dgh_ref[...] += jnp.sum(don * ohat, axis=0, keepdims=True)
        dohat = don * ghv
        do_ref[...] = (ro * (dohat - ohat * (_sel_right(dohat * ohat, ones) * (1.0 / 64.0)))).astype(BF16)

    row = lambda w, j: pl.BlockSpec((tm, w), lambda i: (i, j))
    col = lambda w: pl.BlockSpec((w, tm), lambda i: (0, i))
    full = lambda a: pl.BlockSpec(a.shape, lambda i: (0, 0))
    acc = lambda w: pl.BlockSpec((1, w), lambda i: (0, 0))
    sds = lambda w, dt: jax.ShapeDtypeStruct((s, w), dt)
    sdt = lambda w: jax.ShapeDtypeStruct((w, s), BF16)
    return pl.pallas_call(
        body,
        grid=(s // tm,),
        in_specs=[row(1024, 0), row(1024, 0), row(2048, 0), row(512, 4), row(512, 5), row(512, 0), row(512, 0)]
        + [ANY] * 6 + [full(b_gate), full(g_post), full(gh), full(ones64)],
        out_specs=[row(1024, 0), row(3072, 0), row(1024, 0), row(512, 0),
                   col(1024), row(1024, 0), col(512), row(1024, 0), col(512), row(1024, 0),
                   acc(1024), acc(1024), acc(2048), acc(512)],
        out_shape=[sds(1024, F32), sds(D_IN_PAD, BF16), sds(1024, BF16), sds(512, BF16),
                   sdt(1024), sds(1024, BF16), sdt(512), sds(1024, BF16), sdt(512), sds(1024, BF16),
                   jax.ShapeDtypeStruct((1, 1024), F32), jax.ShapeDtypeStruct((1, 1024), F32),
                   jax.ShapeDtypeStruct((1, 2048), F32), jax.ShapeDtypeStruct((1, 512), F32)],
        scratch_shapes=[pltpu.VMEM(a.shape, BF16) for a in weights] + [pltpu.SemaphoreType.DMA((6,))],
        compiler_params=_params(("arbitrary",), 56),
        name="tail",
    )(x, tgt, proj, proj, proj, attn, o, *weights, b_gate, g_post, gh, ones64)


def _mla_bwd(proj, dqr, dkr, dv, g_q, g_kv, w_uq_pt, w_kv_pt, rc, rs1, rs2, cqt, ckvt, dproj):
    assert HEADS == N_DEV
    s = proj.shape[0]
    tm = 256
    scale = 1.0 / math.sqrt(QK)

    def body(cq_ref, ckv_ref, dqr_ref, dkr_ref, dv_ref, gq_ref, gkv_ref, wuqt_ref, wkvt_ref, c_ref, s1_ref, s2_ref,
             cqt_ref, ckvt_ref, dproj_in, dc_ref, dgq_ref, dgkv_ref, uq_slots, ukv_slots,
             dqf_ref, dkvf_ref, dwuq_ref, dwkv_ref):
        del dproj_in

        @pl.when(pl.program_id(0) == 0)
        def _():
            dgq_ref[...] = jnp.zeros_like(dgq_ref)
            dgkv_ref[...] = jnp.zeros_like(dgkv_ref)
            dwuq_ref[...] = jnp.zeros_like(dwuq_ref)
            dwkv_ref[...] = jnp.zeros_like(dwkv_ref)

        c, s1, s2 = c_ref[...], s1_ref[...], s2_ref[...]
        lane = lax.broadcasted_iota(jnp.int32, (tm, LANE), 1)
        ksum = jnp.zeros((tm, LANE), F32)
        for h in range(HEADS):
            sl = slice(LANE * h, LANE * (h + 1))
            dqf_ref[:, sl] = (_unrope(dqr_ref[:, sl], c, s1, s2) * scale).astype(BF16)
            dkh = dkr_ref[:, sl]
            ksum = ksum + dkh
            dkvf_ref[:, sl] = jnp.where(lane < NOPE, dkh, 0.0).astype(BF16)
            dkvf_ref[:, HEADS * LANE + LANE * h:HEADS * LANE + LANE * (h + 1)] = jnp.where(
                lane < VDIM, dv_ref[:, sl], 0.0).astype(BF16)
        dkpe = _unrope(ksum, c, s1, s2)
        dc_ref[:, Q_LORA + KV_LORA:] = jnp.where((lane >= NOPE) & (lane < QK), dkpe, 0.0).astype(BF16)
        dqf, dkvf = dqf_ref[...], dkvf_ref[...]
        dwuq_ref[...] += _dot(cqt_ref[...], dqf)
        dwkv_ref[...] += _dot(ckvt_ref[...], dkvf)
        dcqn = _dot(dqf, wuqt_ref[...])
        dckvn = _dot(dkvf, wkvt_ref[...])
        for x_ref, g_ref, dn, cols, dg_ref in ((cq_ref, gq_ref, dcqn, slice(0, Q_LORA), dgq_ref),
                                               (ckv_ref, gkv_ref, dckvn, slice(Q_LORA, Q_LORA + KV_LORA), dgkv_ref)):
            xv = x_ref[...].astype(F32)
            r = lax.rsqrt(jnp.mean(xv * xv, axis=-1, keepdims=True) + EPS)
            xh = xv * r
            dg_ref[...] += jnp.sum(dn * xh, axis=0, keepdims=True)
            dh = dn * g_ref[...]
            dc_ref[:, cols] = (r * (dh - xh * jnp.mean(dh * xh, axis=-1, keepdims=True))).astype(BF16)

        @pl.when(pl.program_id(0) == s // tm - 1)
        def _():
            ur = Q_LORA // N_DEV
            for p in range(N_DEV):
                uq_slots[p] = jnp.concatenate(
                    [dwuq_ref[ur * p:ur * (p + 1), LANE * h:LANE * h + QK] for h in range(HEADS)], axis=1).astype(BF16)
                ukv_slots[p] = jnp.concatenate(
                    [dwkv_ref[:, LANE * p:LANE * p + NOPE],
                     dwkv_ref[:, LANE * (HEADS + p):LANE * (HEADS + p) + VDIM]], axis=1).astype(BF16)

    row = lambda w, j: pl.BlockSpec((tm, w), lambda i: (i, j))
    full = lambda a: pl.BlockSpec(a.shape, lambda i: (0, 0))
    acc = lambda w: pl.BlockSpec((1, w), lambda i: (0, 0))
    col = lambda w: pl.BlockSpec((w, tm), lambda i: (0, i))
    whole = lambda shape: pl.BlockSpec(shape, lambda i: (0, 0, 0))
    uq_shape = (N_DEV, Q_LORA // N_DEV, HEADS * QK)
    ukv_shape = (N_DEV, KV_LORA, NOPE + VDIM)
    return pl.pallas_call(
        body,
        grid=(s // tm,),
        in_specs=[row(768, 6), row(256, 21), row(1024, 0), row(1024, 0), row(1024, 0), full(g_q), full(g_kv),
                  full(w_uq_pt), full(w_kv_pt), row(128, 0), row(128, 0), row(128, 0), col(Q_LORA), col(KV_LORA),
                  pl.BlockSpec(memory_space=pl.ANY)],
        out_specs=[row(1152, 4), acc(768), acc(256), whole(uq_shape), whole(ukv_shape)],
        out_shape=[jax.ShapeDtypeStruct(dproj.shape, BF16),
                   jax.ShapeDtypeStruct((1, 768), F32), jax.ShapeDtypeStruct((1, 256), F32),
                   jax.ShapeDtypeStruct(uq_shape, BF16), jax.ShapeDtypeStruct(ukv_shape, BF16)],
        input_output_aliases={14: 0},
        scratch_shapes=[pltpu.VMEM((tm, HEADS * LANE), BF16), pltpu.VMEM((tm, 2 * HEADS * LANE), BF16),
                        pltpu.VMEM((Q_LORA, HEADS * LANE), F32), pltpu.VMEM((KV_LORA, 2 * HEADS * LANE), F32)],
        compiler_params=_params(("arbitrary",)),
        name="mla_bwd",
    )(proj, proj, dqr, dkr, dv, g_q, g_kv, w_uq_pt, w_kv_pt, rc, rs1, rs2, cqt, ckvt, dproj)


def _dh_dx(dproj, w_in_pt, x, dout, g_pre, sends):
    s, k = dproj.shape
    tm = 256
    ns, ni = len(sends), s // tm

    def body(dp_ref, w_ref, x_ref, dout_ref, g_ref, *rest):
        send_refs, (dx_ref, dg_ref) = rest[:ns], rest[ns:ns + 2]
        recv_refs, sems = rest[ns + 2:2 * ns + 2], rest[2 * ns + 2:]

        @pl.when(pl.program_id(0) == 0)
        def _():
            _start_all(*_to_chips_copies(send_refs, recv_refs, sems))
            dg_ref[...] = jnp.zeros_like(dg_ref)

        dh = _dot(dp_ref[...], w_ref[...])
        xv = x_ref[...]
        r = lax.rsqrt(jnp.mean(xv * xv, axis=-1, keepdims=True) + EPS)
        xh = xv * r
        dg_ref[...] += jnp.sum(dh * xh, axis=0, keepdims=True)
        dxh = dh * g_ref[...]
        dx_ref[...] = dout_ref[...] + r * (dxh - xh * jnp.mean(dxh * xh, axis=-1, keepdims=True))

        @pl.when(pl.program_id(0) == ni - 1)
        def _():
            _wait_all(*_to_chips_copies(send_refs, recv_refs, sems))

    row = lambda w: pl.BlockSpec((tm, w), lambda i: (i, 0))
    return pl.pallas_call(
        body,
        grid=(ni,),
        in_specs=[row(k), pl.BlockSpec((k, D_MODEL), lambda i: (0, 0)), row(D_MODEL), row(D_MODEL),
                  pl.BlockSpec((1, D_MODEL), lambda i: (0, 0))] + [ANY] * ns,
        out_specs=[row(D_MODEL), pl.BlockSpec((1, D_MODEL), lambda i: (0, 0))] + [ANY] * ns,
        out_shape=[jax.ShapeDtypeStruct((s, D_MODEL), F32), jax.ShapeDtypeStruct((1, D_MODEL), F32)]
        + [jax.ShapeDtypeStruct(a.shape, a.dtype) for a in sends],
        scratch_shapes=_copy_sems(ns, 3),
        compiler_params=_params(("arbitrary",)),
        name="dh_dx",
    )(dproj, w_in_pt, x, dout, g_pre, *sends)


def _pair_reduce(slots):
    n = len(slots)
    half = [(N_DEV // 2,) + a.shape[1:] for a in slots]

    def body(*refs):
        s_refs, o_refs = refs[:n], refs[n:2 * n]
        mine, got = refs[2 * n:3 * n], refs[3 * n:4 * n]
        send_sems, recv_sems, local_sems = refs[4 * n:]
        x, y, c = _my_place()
        copies, loads = [], []
        for a in range(n):
            for q in range(N_DEV // 2):
                copies.append(pltpu.make_async_remote_copy(
                    src_ref=s_refs[a].at[2 * q + 1 - c], dst_ref=got[a].at[q],
                    send_sem=send_sems.at[4 * a + q], recv_sem=recv_sems.at[4 * a + q],
                    device_id=(x, y, 1 - c), device_id_type=MESH_ID))
                loads.append(pltpu.make_async_copy(s_refs[a].at[2 * q + c], mine[a].at[q], local_sems.at[4 * a + q]))
        _start_all(loads, copies)
        _wait_all(loads, copies)
        for a in range(n):
            o_refs[a][...] = (mine[a][...].astype(F32) + got[a][...].astype(F32)).astype(o_refs[a].dtype)

    vm = lambda: [pltpu.VMEM(h, a.dtype) for h, a in zip(half, slots)]
    return pl.pallas_call(
        body,
        in_specs=[ANY] * n,
        out_shape=[jax.ShapeDtypeStruct(h, a.dtype) for h, a in zip(half, slots)],
        scratch_shapes=vm() + vm() + [pltpu.SemaphoreType.DMA((4 * n,)), pltpu.SemaphoreType.DMA((4 * n,)),
                                      pltpu.SemaphoreType.DMA((4 * n,))],
        compiler_params=pltpu.CompilerParams(vmem_limit_bytes=48 * 2**20),
        name="pair_reduce",
    )(*slots)


def _rope_tables(s):
    inv = (np.float32(ROPE_THETA) ** (-np.arange(0, ROPE, 2, dtype=np.float32) / np.float32(ROPE))).astype(np.float32)
    ang = (np.arange(s, dtype=np.float32)[:, None] * inv[None, :]).astype(np.float32)
    cos, sin = jnp.asarray(np.cos(ang.astype(np.float64)), F32), jnp.asarray(np.sin(ang.astype(np.float64)), F32)
    z = lambda w: jnp.zeros((s, w), F32)
    rc = jnp.concatenate([jnp.ones((s, NOPE), F32), cos, cos, z(32)], axis=1)
    rs1 = jnp.concatenate([z(NOPE), -sin, z(16), z(32)], axis=1)
    rs2 = jnp.concatenate([z(NOPE), z(16), sin, z(32)], axis=1)
    return rc, rs1, rs2


def _step(x, tgt, w_blk, shards, g_pre, b_gate, g_q, g_kv, lbl, g_hgrn, g_post):
    s = x.shape[0]
    rc, rs1, rs2 = _rope_tables(s)
    gh = jnp.tile(g_hgrn, (1, HEADS))

    proj, w_in_pt, ht, *got = _gather_proj(x, g_pre, w_blk, shards[:2])
    w_uq, w_ukv = (_from_slots(n, g) for n, g in zip(MATS[:2], got))
    w_uq_p = jnp.pad(w_uq.reshape(Q_LORA, HEADS, QK), ((0, 0), (0, 0), (0, LANE - QK))).reshape(Q_LORA, HEADS * LANE)
    kv3 = w_ukv.reshape(KV_LORA, HEADS, NOPE + VDIM)
    pad64 = lambda t: jnp.pad(t, ((0, 0), (0, 0), (0, LANE - 64))).reshape(KV_LORA, HEADS * LANE)
    w_kv_p = jnp.concatenate([pad64(kv3[:, :, :NOPE]), pad64(kv3[:, :, NOPE:])], axis=1)

    qr, kr, v, cqt, ckvt = _mla_prep(proj, g_q, g_kv, w_uq_p, w_kv_p, rc, rs1, rs2)
    attn, qa, *got = _attn_fwd(qr, kr, v, shards[2:])
    w_a, w_b, w_out = (_from_slots(n, g) for n, g in zip(MATS[2:], got))
    o, sprev = _hgrn_fwd(proj, lbl)
    (dout, dproj, dop, do, mt, dy_bf, yat, dya_bf, ybt, dyb_bf,
     loss_vec, dg_post, db_gate, dgh) = _tail(x, tgt, proj, attn, o, w_a, w_b, w_out, w_a.T, w_b.T, w_out.T,
                                               b_gate, g_post, gh)
    dproj, dlbl, *early = _hgrn_bwd(proj, lbl, do, sprev, dproj,
                                    [("w_branch_a", yat, dya_bf), ("w_branch_b", ybt, dyb_bf), ("w_out", mt, dy_bf)])
    dqr, dkr, dv, *early_recv = _attn_bwd(qa, kr, v, dop, early)
    dproj, dg_q, dg_kv, dw_uq_slots, dw_ukv_slots = _mla_bwd(proj, dqr, dkr, dv, g_q, g_kv, w_uq_p.T, w_kv_p.T,
                                                             rc, rs1, rs2, cqt, ckvt, dproj)

    dw_in_slots = _dw_in_slots(ht, dproj)
    late = _pair_reduce([dw_in_slots, dw_uq_slots, dw_ukv_slots])
    dx, dg_pre, *late_recv = _dh_dx(dproj, w_in_pt, x, dout, g_pre, late)

    g_sum = _vectors_sum(dg_pre, db_gate, dg_q, dg_kv, dlbl, dgh, dg_post, loss_vec)
    return dx, late_recv[0], dict(zip(MATS, late_recv[1:] + early_recv)), g_sum


def _adamw(g, w, m, v):
    c1 = 1.0 / (1.0 - ADAM_B1 ** ADAM_STEP)
    c2 = 1.0 / (1.0 - ADAM_B2 ** ADAM_STEP)
    nm = ADAM_B1 * m + (1.0 - ADAM_B1) * g
    nv = ADAM_B2 * v + (1.0 - ADAM_B2) * (g * g)
    d = -ADAM_LR * ((nm * c1) / (jnp.sqrt(nv * c2) + ADAM_EPS) + ADAM_WD * w)
    return d, nm, nv


def _sum8(r_ref):
    g = r_ref[0].astype(F32)
    for k in range(1, r_ref.shape[0]):
        g = g + r_ref[k].astype(F32)
    return g


def _sum_adamw_w_in(recv, w, m, v):
    rows, _, cols = w.shape
    tc = 256
    nc = cols // tc

    def body(r_ref, w_hbm, m_hbm, v_hbm, g_hbm, d_hbm, nm_hbm, nv_hbm, ins, outs, in_sems, out_sems):
        i = pl.program_id(0)
        slot = i & 1
        cols_of = lambda step: pl.ds(pl.multiple_of(step * tc, tc), tc)

        def load(k, step, sl):
            return pltpu.make_async_copy((w_hbm, m_hbm, v_hbm)[k].at[:, 0, cols_of(step)], ins.at[sl, k],
                                         in_sems.at[sl, k])

        def store(k, step, sl):
            return pltpu.make_async_copy(outs.at[sl, k], (g_hbm, d_hbm, nm_hbm, nv_hbm)[k].at[:, 0, cols_of(step)],
                                         out_sems.at[sl, k])

        @pl.when(i == 0)
        def _():
            for k in range(3):
                load(k, 0, 0).start()

        @pl.when(i + 1 < nc)
        def _():
            for k in range(3):
                load(k, i + 1, 1 - slot).start()

        @pl.when(i >= 2)
        def _():
            for k in range(4):
                store(k, i - 2, slot).wait()

        for k in range(3):
            load(k, i, slot).wait()
        g = _sum8(r_ref)
        d, nm, nv = _adamw(g, ins[slot, 0], ins[slot, 1], ins[slot, 2])
        for k, val in enumerate((g, d, nm, nv)):
            outs[slot, k] = val
        for k in range(4):
            store(k, i, slot).start()

        @pl.when(i == nc - 1)
        def _():
            for k in range(4):
                store(k, i, slot).wait()
            if nc >= 2:
                for k in range(4):
                    store(k, i - 1, 1 - slot).wait()

    out = jax.ShapeDtypeStruct((rows, 1, cols), F32)
    return pl.pallas_call(
        body,
        grid=(nc,),
        in_specs=[pl.BlockSpec((recv.shape[0], rows, tc), lambda i: (0, 0, i)), ANY, ANY, ANY],
        out_specs=[ANY, ANY, ANY, ANY],
        out_shape=[out, out, out, out],
        scratch_shapes=[pltpu.VMEM((2, 3, rows, tc), F32), pltpu.VMEM((2, 4, rows, tc), F32),
                        pltpu.SemaphoreType.DMA((2, 3)), pltpu.SemaphoreType.DMA((2, 4))],
        compiler_params=_params(("arbitrary",)),
        name="sum_adamw_w_in",
    )(recv, w, m, v)


def _sum_adamw_whole(recvs, ws, ms, vs):
    n = len(ws)

    def body(*refs):
        r_refs, w_refs, m_refs, v_refs = refs[:n], refs[n:2 * n], refs[2 * n:3 * n], refs[3 * n:4 * n]
        outs = refs[4 * n:]
        for a in range(n):
            g = _sum8(r_refs[a])
            d, nm, nv = _adamw(g, w_refs[a][...], m_refs[a][...], v_refs[a][...])
            outs[a][...] = g
            outs[n + a][...] = d
            outs[2 * n + a][...] = nm
            outs[3 * n + a][...] = nv

    shapes = [jax.ShapeDtypeStruct(w.shape, F32) for w in ws]
    res = pl.pallas_call(
        body,
        out_shape=shapes * 4,
        compiler_params=pltpu.CompilerParams(vmem_limit_bytes=48 * 2**20),
        name="sum_adamw_mats",
    )(*recvs, *ws, *ms, *vs)
    return res[:n], res[n:2 * n], res[2 * n:3 * n], res[3 * n:]


SMALL = ("g_pre", "b_gate", "g_q", "g_kv", "lb_logits", "g_hgrn", "g_post")
SMALL_SHAPE = dict(g_pre=(1, 1024), b_gate=(1, 2048), g_q=(1, 768), g_kv=(1, 256), lb_logits=(2, 512),
                   g_hgrn=(1, 64), g_post=(1, 1024))


def _vectors_sum(dg_pre, db_gate, dg_q, dg_kv, dlbl, dgh, dg_post, loss_vec):
    def body(gpre_ref, bg_ref, gq_ref, gkv_ref, lbl_ref, gh_ref, gpost_ref, loss_ref, out_ref, mine, got,
             send_sems, recv_sems):
        mine[...] = jnp.zeros_like(mine)
        mine[0:1, :] = gpre_ref[...]
        mine[1:2, :] = bg_ref[:, :1024]
        mine[2:3, :] = bg_ref[:, 1024:]
        mine[3:4, :Q_LORA] = gq_ref[...]
        mine[4:5, :KV_LORA] = gkv_ref[...]
        loss = (0.5 / D_MODEL) * jnp.sum(loss_ref[...], axis=-1, keepdims=True)
        mine[4:5, KV_LORA:] = jnp.broadcast_to(loss, (1, 1024 - KV_LORA))
        mine[5:6, :HG_WIDTH] = lbl_ref[0:1, :]
        mine[5:6, HG_WIDTH:] = lbl_ref[1:2, :]
        gh = gh_ref[...]
        fold = gh[:, :VDIM]
        for h in range(1, HEADS):
            fold = fold + gh[:, VDIM * h:VDIM * (h + 1)]
        mine[6:7, :VDIM] = fold
        mine[7:8, :] = gpost_ref[...]
        x, y, c = _my_place()
        me = 4 * x + 2 * y + c
        got[me] = mine[...]
        copies = [pltpu.make_async_remote_copy(
            src_ref=mine, dst_ref=got.at[me], send_sem=send_sems.at[k], recv_sem=recv_sems.at[k],
            device_id=_flip(k, x, y, c), device_id_type=MESH_ID) for k in range(N_DEV - 1)]
        _start_all([], copies)
        _wait_all([], copies)
        out_ref[...] = _sum8(got)

    return pl.pallas_call(
        body,
        out_shape=jax.ShapeDtypeStruct((8, 1024), F32),
        scratch_shapes=[pltpu.VMEM((8, 1024), F32), pltpu.VMEM((N_DEV, 8, 1024), F32),
                        pltpu.SemaphoreType.DMA((7,)), pltpu.SemaphoreType.DMA((7,))],
        name="vectors_sum",
    )(dg_pre, db_gate, dg_q, dg_kv, dlbl, dgh, dg_post, loss_vec)


def _vectors_adamw(g_sum, ws, ms, vs):
    n = len(SMALL)

    def body(g_ref, *refs):
        w_refs, m_refs, v_refs = refs[:n], refs[n:2 * n], refs[2 * n:3 * n]
        loss_ref, outs = refs[3 * n], refs[3 * n + 1:]
        g = g_ref[...]
        loss_ref[...] = g[4:5, KV_LORA:KV_LORA + 1]
        grads = (g[0:1, :], jnp.concatenate([g[1:2, :], g[2:3, :]], axis=1), g[3:4, :Q_LORA], g[4:5, :KV_LORA],
                 jnp.concatenate([g[5:6, :HG_WIDTH], g[5:6, HG_WIDTH:]], axis=0), g[6:7, :VDIM], g[7:8, :])
        for a in range(n):
            d, nm, nv = _adamw(grads[a], w_refs[a][...], m_refs[a][...], v_refs[a][...])
            outs[a][...] = grads[a]
            outs[n + a][...] = d
            outs[2 * n + a][...] = nm
            outs[3 * n + a][...] = nv

    shapes = [jax.ShapeDtypeStruct(SMALL_SHAPE[k], F32) for k in SMALL]
    res = pl.pallas_call(
        body,
        out_shape=[jax.ShapeDtypeStruct((1, 1), F32)] + shapes * 4,
        name="vectors_adamw",
    )(g_sum, *ws, *ms, *vs)
    return res[0], res[1:n + 1], res[n + 1:2 * n + 1], res[2 * n + 1:3 * n + 1], res[3 * n + 1:]


MATS = ("w_uq", "w_ukv", "w_branch_a", "w_branch_b", "w_out")
COL_SHARDED = dict(w_uq=False, w_ukv=True, w_branch_a=True, w_branch_b=True, w_out=False)
ORDER = ("g_pre", "w_in", "b_gate", "g_q", "w_uq", "g_kv", "w_ukv", "lb_logits", "g_hgrn",
         "w_branch_a", "w_branch_b", "w_out", "g_post")


def _from_slots(name, slots):
    _, r, c = slots.shape
    if COL_SHARDED[name]:
        return slots.transpose(1, 0, 2).reshape(r, N_DEV * c)
    return slots.reshape(N_DEV * r, c)


def kernel(x, g_pre, w_in, b_gate, g_q, w_uq, g_kv, w_ukv, lb_logits, g_hgrn, w_branch_a, w_branch_b, w_out, g_post, loss_target, m_g_pre, m_w_in, m_b_gate, m_g_q, m_w_uq, m_g_kv, m_w_ukv, m_lb_logits, m_g_hgrn, m_w_branch_a, m_w_branch_b, m_w_out, m_g_post, v_g_pre, v_w_in, v_b_gate, v_g_q, v_w_uq, v_g_kv, v_w_ukv, v_lb_logits, v_g_hgrn, v_w_branch_a, v_w_branch_b, v_w_out, v_g_post):
    rows3 = lambda a: jnp.transpose(a, (2, 0, 1))
    w = dict(w_in=rows3(w_in), w_uq=w_uq[0], w_ukv=w_ukv[0], w_branch_a=w_branch_a[0], w_branch_b=w_branch_b[0],
             w_out=w_out[0], g_pre=g_pre, b_gate=b_gate, g_q=g_q, g_kv=g_kv, lb_logits=lb_logits, g_hgrn=g_hgrn,
             g_post=g_post)
    mom = dict(w_in=rows3(m_w_in), w_uq=m_w_uq[0], w_ukv=m_w_ukv[0], w_branch_a=m_w_branch_a[0],
               w_branch_b=m_w_branch_b[0], w_out=m_w_out[0], g_pre=m_g_pre, b_gate=m_b_gate, g_q=m_g_q, g_kv=m_g_kv,
               lb_logits=m_lb_logits, g_hgrn=m_g_hgrn, g_post=m_g_post)
    var = dict(w_in=rows3(v_w_in), w_uq=v_w_uq[0], w_ukv=v_w_ukv[0], w_branch_a=v_w_branch_a[0],
               w_branch_b=v_w_branch_b[0], w_out=v_w_out[0], g_pre=v_g_pre, b_gate=v_b_gate, g_q=v_g_q, g_kv=v_g_kv,
               lb_logits=v_lb_logits, g_hgrn=v_g_hgrn, g_post=v_g_post)

    w_blk = w["w_in"].reshape(W_IN_SHARD, D_MODEL).astype(BF16)
    dx, recv_in, recv, g_sum = _step(x[0], loss_target[0], w_blk, [w[n].astype(BF16) for n in MATS],
                                     g_pre, b_gate, g_q, g_kv, lb_logits, g_hgrn, g_post)

    g_in, d_in, m_in, v_in = _sum_adamw_w_in(recv_in, w["w_in"], mom["w_in"], var["w_in"])
    res = _sum_adamw_whole([recv[n] for n in MATS], *([t[n] for n in MATS] for t in (w, mom, var)))
    total, *vec = _vectors_adamw(g_sum, *([t[n] for n in SMALL] for t in (w, mom, var)))

    outs = []
    for mats, vecs, big in zip(res, vec, (g_in, d_in, m_in, v_in)):
        t = {**{n: a[None] for n, a in zip(MATS, mats)}, **dict(zip(SMALL, vecs)),
             "w_in": jnp.transpose(big, (1, 2, 0))}
        outs += [t[n] for n in ORDER]
    return (total.reshape(()), dx[None], *outs)
```

```python
import math

import jax
import jax.numpy as jnp
import numpy as np
from jax import lax
from jax.experimental import pallas as pl
from jax.experimental.pallas import tpu as pltpu

F32, BF16 = jnp.float32, jnp.bfloat16

D_MODEL = 1024
EPS = 1e-6
HEADS = 8
NOPE, ROPE, VDIM = 64, 32, 64
QK = NOPE + ROPE
Q_LORA, KV_LORA = 768, 256
ROPE_THETA = 10000.0
ATT_CHUNK_SHIFT = 6
HG_BLOCK = 32
HG_WIDTH = 512
D_IN = 5664
D_IN_PAD = 5760
W_IN_SHARD = D_IN // 8
N_DEV = 8
LANE = 128

ADAM_LR, ADAM_B1, ADAM_B2, ADAM_EPS, ADAM_WD, ADAM_STEP = 0.001, 0.9, 0.999, 1e-08, 0.01, 10

W_IN_SEGMENTS = ((3616, 5664, 0), (1056, 1568, 2048), (3104, 3616, 2560), (1568, 3104, 3072),
                 (0, 1024, 4608), (1024, 1056, 5696))

NT = (((1,), (1,)), ((), ()))
TN = (((0,), (0,)), ((), ()))
MESH_ID = pl.DeviceIdType.MESH


def _w_in_pieces():
    out = []
    for lo, hi, dst in W_IN_SEGMENTS:
        c = lo
        while c < hi:
            p = c // W_IN_SHARD
            e = min(hi, (p + 1) * W_IN_SHARD)
            out.append((p, c - p * W_IN_SHARD, e - p * W_IN_SHARD, dst + c - lo))
            c = e
    return out


def _params(sem, vmem_mb=48):
    return pltpu.CompilerParams(dimension_semantics=sem, vmem_limit_bytes=vmem_mb * 2**20)


def _dot(a, b):
    return jnp.dot(a, b, preferred_element_type=F32)


def _dotg(a, b, dims):
    return lax.dot_general(a, b, dims, preferred_element_type=F32)


def _split2(x):
    hi = x.astype(BF16)
    return hi, (x - hi.astype(F32)).astype(BF16)


def _sel_left(m01, x):
    hi, lo = _split2(x)
    return _dot(m01, hi) + _dot(m01, lo)


def _sel_right(x, m01):
    hi, lo = _split2(x)
    return _dot(hi, m01) + _dot(lo, m01)


def _hi_lo(x):
    hi = x.astype(BF16).astype(F32)
    return hi, x - hi


def _sigmoid(x):
    return 0.5 * jnp.tanh(0.5 * x) + 0.5


def _rope(x, c, s1, s2):
    return x * c + pltpu.roll(x, 112, 1) * s1 + pltpu.roll(x, 16, 1) * s2


def _unrope(d, c, s1, s2):
    return d * c + pltpu.roll(d * s1, 16, 1) + pltpu.roll(d * s2, 112, 1)


def _my_place():
    return lax.axis_index("x"), lax.axis_index("y"), lax.axis_index("c")


def _flip(k, x, y, c):
    fx, fy, fc = (k + 1) >> 2 & 1, (k + 1) >> 1 & 1, (k + 1) & 1
    return (1 - x if fx else x), (1 - y if fy else y), (1 - c if fc else c)


def _to_all_copies(s_refs, r_refs, sems, spread):
    send_sems, recv_sems, local_sems = sems
    x, y, c = _my_place()
    me = 4 * x + 2 * y + c
    src = (lambda a, p: s_refs[a]) if spread else (lambda a, p: s_refs[a].at[p])
    local = [pltpu.make_async_copy(src(a, me), r_refs[a].at[me], local_sems.at[a]) for a in range(len(s_refs))]
    remote = []
    for k in range(N_DEV - 1):
        px, py, pc = _flip(k, x, y, c)
        for a in range(len(s_refs)):
            remote.append(pltpu.make_async_remote_copy(
                src_ref=src(a, 4 * px + 2 * py + pc), dst_ref=r_refs[a].at[me],
                send_sem=send_sems.at[7 * a + k], recv_sem=recv_sems.at[7 * a + k],
                device_id=(px, py, pc), device_id_type=MESH_ID))
    return local, remote


def _to_chips_copies(s_refs, r_refs, sems):
    send_sems, recv_sems, local_sems = sems
    x, y, c = _my_place()
    me = 2 * x + y
    local = [pltpu.make_async_copy(s_refs[a].at[me], r_refs[a].at[me], local_sems.at[a]) for a in range(len(s_refs))]
    remote = []
    for k in range(3):
        px = 1 - x if (k + 1) >> 1 & 1 else x
        py = 1 - y if (k + 1) & 1 else y
        for a in range(len(s_refs)):
            remote.append(pltpu.make_async_remote_copy(
                src_ref=s_refs[a].at[2 * px + py], dst_ref=r_refs[a].at[me],
                send_sem=send_sems.at[3 * a + k], recv_sem=recv_sems.at[3 * a + k],
                device_id=(px, py, c), device_id_type=MESH_ID))
    return local, remote


def _start_all(local, remote):
    for cp in local + remote:
        cp.start()


def _wait_all(local, remote):
    for cp in remote:
        cp.wait_recv()
    for cp in remote:
        cp.wait_send()
    for cp in local:
        cp.wait()


def _copy_sems(n, peers):
    return [pltpu.SemaphoreType.DMA((peers * n,)), pltpu.SemaphoreType.DMA((peers * n,)),
            pltpu.SemaphoreType.DMA((n,))]


ANY = pl.BlockSpec(memory_space=pl.ANY)


def _dw_in_slots(ht, dproj):
    m, k = ht.shape
    n = dproj.shape[1]
    tn, tk = 1152, 1024
    nj, nk = n // tn, k // tk
    by_tile = [[] for _ in range(nj)]
    for p, lo, hi, dst in _w_in_pieces():
        while lo < hi:
            j = dst // tn
            cnt = min(hi - lo, (j + 1) * tn - dst)
            by_tile[j].append((p, lo, lo + cnt, dst - j * tn))
            lo, dst = lo + cnt, dst + cnt

    def body(a_ref, b_ref, s_ref, acc_ref):
        j, l = pl.program_id(0), pl.program_id(1)

        @pl.when(l == 0)
        def _():
            acc_ref[...] = jnp.zeros_like(acc_ref)

        acc_ref[...] += _dot(a_ref[...], b_ref[...])

        @pl.when(l == nk - 1)
        def _():
            at = acc_ref[...].T
            for jj in range(nj):
                @pl.when(j == jj)
                def _(jj=jj):
                    for p, lo, hi, d in by_tile[jj]:
                        s_ref[p, lo:hi, :] = at[d:d + hi - lo, :].astype(BF16)

    return pl.pallas_call(
        body,
        grid=(nj, nk),
        in_specs=[pl.BlockSpec((m, tk), lambda j, l: (0, l)), pl.BlockSpec((tk, tn), lambda j, l: (l, j))],
        out_specs=pl.BlockSpec((N_DEV, W_IN_SHARD, m), lambda j, l: (0, 0, 0)),
        out_shape=jax.ShapeDtypeStruct((N_DEV, W_IN_SHARD, m), BF16),
        scratch_shapes=[pltpu.VMEM((m, tn), F32)],
        compiler_params=_params(("arbitrary", "arbitrary")),
        name="dw_in",
    )(ht, dproj)


PROJ_DT = F32
GP_TN = 256
GP_COLS = 5888
GP_NT = GP_COLS // GP_TN


def _gp_tile_pieces():
    tiles = [[] for _ in range(GP_NT)]
    for p, lo, hi, dst in _w_in_pieces():
        while lo < hi:
            t = dst // GP_TN
            n = min(hi - lo, (t + 1) * GP_TN - dst)
            tiles[t].append((p, lo, lo + n, dst - t * GP_TN))
            lo, dst = lo + n, dst + n
    return tiles


def _gp_tables():
    pieces = _gp_tile_pieces()
    rank_of = {None: 0, 0: 1, 1: 2, 2: 2, 4: 3, 5: 3, 3: 4, 6: 5}
    order = np.zeros((N_DEV, GP_NT), np.int32)
    waits = np.zeros((N_DEV, GP_NT), np.int32)
    for me in range(N_DEV):
        x, y, c = me >> 2 & 1, me >> 1 & 1, me & 1
        chips = [(1 - x, y), (x, 1 - y), (1 - x, 1 - y)]

        def sem_of(p):
            px, py, pc = p >> 2 & 1, p >> 1 & 1, p & 1
            if (px, py) == (x, y):
                return None if pc == c else 0
            j = chips.index((px, py))
            return 1 + j if pc == c else 4 + j

        needs = [sorted({sem_of(p) for p, _, _, _ in tile} - {None}) for tile in pieces]
        ranks = [max([rank_of[k] for k in ks], default=0) for ks in needs]
        seq = sorted(range(GP_NT), key=lambda t: (ranks[t], t))
        seen = set()
        for step, t in enumerate(seq):
            order[me, step] = t
            new = [k for k in needs[t] if k not in seen]
            for k in new:
                waits[me, step] |= 1 << k
            seen.update(new)
        assert seen == set(range(7)), (me, seen)
    return order, waits


def _gather_proj(x, g_pre, w_blk, shards):
    s = x.shape[0]
    tx = 512
    ns = len(shards)
    tile_pieces = _gp_tile_pieces()
    order_np, waits_np = _gp_tables()
    xq, yq, cq = _my_place()
    me_out = 4 * xq + 2 * yq + cq
    order = lax.dynamic_index_in_dim(jnp.asarray(order_np), me_out, 0, keepdims=False)
    waits = lax.dynamic_index_in_dim(jnp.asarray(waits_np), me_out, 0, keepdims=False)

    def body(order_ref, waits_ref, x_hbm, g_ref, wblk_hbm, *rest):
        shard_refs, (proj_ref, wt_ref, ht_hbm), got_refs = rest[:ns], rest[ns:ns + 3], rest[ns + 3:2 * ns + 3]
        recv, h_ref, wtile, xbuf, htbuf = rest[2 * ns + 3:2 * ns + 8]
        send_sems, recv_sems, misc_sems = rest[2 * ns + 8:2 * ns + 11]
        sems = rest[2 * ns + 11:]
        t = pl.program_id(0)
        x_, y_, c = _my_place()
        sibling = (x_, y_, 1 - c)
        chips = [(1 - x_, y_), (x_, 1 - y_), (1 - x_, 1 - y_)]
        idx = lambda px, py, pc: 4 * px + 2 * py + pc
        me = idx(x_, y_, c)

        def copy(k, slot, to, src=None):
            return pltpu.make_async_remote_copy(
                src_ref=recv.at[slot] if src is None else src, dst_ref=recv.at[slot],
                send_sem=send_sems.at[k], recv_sem=recv_sems.at[k], device_id=to, device_id_type=MESH_ID)

        mine = pltpu.make_async_copy(wblk_hbm, recv.at[me], misc_sems.at[0])
        first = [copy(0, me, sibling, src=wblk_hbm)] + [copy(1 + j, me, (*chips[j], c), src=wblk_hbm) for j in range(2)]
        passed = [copy(4 + j, idx(*ch, c), sibling) for j, ch in enumerate(chips)]
        onward = [copy(3, idx(*chips[0], c), (*chips[1], c)), copy(3, idx(*chips[1], c), (*chips[0], c))]
        arrivals = ([copy(0, idx(x_, y_, 1 - c), sibling)] + [copy(1 + j, idx(*ch, c), sibling) for j, ch in enumerate(chips)]
                    + [copy(4 + j, idx(*ch, 1 - c), sibling) for j, ch in enumerate(chips)])

        @pl.when(t == 0)
        def _():
            mine.start()
            for cp in first:
                cp.start()
            _start_all(*_to_all_copies(shard_refs, got_refs, sems, True))

            def load(i):
                return pltpu.make_async_copy(x_hbm.at[pl.ds(i * tx, tx), :], xbuf.at[i & 1], misc_sems.at[1 + (i & 1)])

            def store(i):
                return pltpu.make_async_copy(htbuf.at[i & 1], ht_hbm.at[:, pl.ds(i * tx, tx)], misc_sems.at[3 + (i & 1)])

            load(0).start()
            for i in range(s // tx):
                if i + 1 < s // tx:
                    load(i + 1).start()
                load(i).wait()
                xv = xbuf[i & 1]
                r = lax.rsqrt(jnp.mean(xv * xv, axis=-1, keepdims=True) + EPS)
                h = (xv * r * g_ref[...]).astype(BF16)
                h_ref[i * tx:(i + 1) * tx, :] = h
                if i >= 2:
                    store(i - 2).wait()
                htbuf[i & 1] = h.T
                store(i).start()
            for i in range(max(s // tx - 2, 0), s // tx):
                store(i).wait()
            mine.wait()

        w = waits_ref[t]
        for k in range(7):
            @pl.when((w >> k) & 1 == 1)
            def _(k=k):
                arrivals[k].wait_recv()
                if 1 <= k <= 3:
                    passed[k - 1].start()
                if 1 <= k <= 2:
                    @pl.when(c == k - 1)
                    def _():
                        onward[k - 1].start()

        tile = order_ref[t]
        for tt in range(GP_NT):
            @pl.when(tile == tt)
            def _(tt=tt):
                covered = sorted((d, d + hi - lo) for _, lo, hi, d in tile_pieces[tt])
                at = 0
                for lo_z, hi_z in covered + [(GP_TN, GP_TN)]:
                    if lo_z > at:
                        wtile[at:lo_z, :] = jnp.zeros((lo_z - at, D_MODEL), BF16)
                    at = max(at, hi_z)
                for p, lo, hi, d in tile_pieces[tt]:
                    wtile[d:d + hi - lo, :] = recv[p, lo:hi, :]

        wt = wtile[...]
        wt_ref[...] = wt
        proj_ref[...] = _dotg(h_ref[...], wt, NT).astype(PROJ_DT)

        @pl.when(t == GP_NT - 1)
        def _():
            for cp in first + passed + onward[:1]:
                cp.wait_send()
            _wait_all(*_to_all_copies(shard_refs, got_refs, sems, True))

    grid_spec = pltpu.PrefetchScalarGridSpec(
        num_scalar_prefetch=2,
        grid=(GP_NT,),
        in_specs=[ANY, pl.BlockSpec((1, D_MODEL), lambda t, o, w: (0, 0)), ANY] + [ANY] * ns,
        out_specs=[pl.BlockSpec((s, GP_TN), lambda t, o, w: (0, o[t])),
                   pl.BlockSpec((GP_TN, D_MODEL), lambda t, o, w: (o[t], 0)), ANY] + [ANY] * ns,
        scratch_shapes=[pltpu.VMEM((N_DEV, W_IN_SHARD, D_MODEL), BF16), pltpu.VMEM((s, D_MODEL), BF16),
                        pltpu.VMEM((GP_TN, D_MODEL), BF16), pltpu.VMEM((2, tx, D_MODEL), F32),
                        pltpu.VMEM((2, D_MODEL, tx), BF16),
                        pltpu.SemaphoreType.DMA((7,)), pltpu.SemaphoreType.DMA((7,)), pltpu.SemaphoreType.DMA((5,))]
        + _copy_sems(ns, 7),
    )
    return pl.pallas_call(
        body,
        grid_spec=grid_spec,
        out_shape=[jax.ShapeDtypeStruct((s, GP_COLS), PROJ_DT), jax.ShapeDtypeStruct((GP_COLS, D_MODEL), BF16),
                   jax.ShapeDtypeStruct((D_MODEL, s), BF16)]
        + [jax.ShapeDtypeStruct((N_DEV,) + b.shape, b.dtype) for b in shards],
        compiler_params=_params(("arbitrary",), 56),
        name="gather_proj",
    )(order, waits, x, g_pre, w_blk, *shards)


def _mla_prep(proj, g_q, g_kv, w_uq_p, w_kv_p, rc, rs1, rs2):
    s = proj.shape[0]
    tm = 256
    scale = 1.0 / math.sqrt(QK)

    def body(cq_ref, ckv_ref, kpe_ref, gq_ref, gkv_ref, wuq_ref, wkv_ref, c_ref, s1_ref, s2_ref,
             qr_ref, kr_ref, v_ref, cqt_ref, ckvt_ref):
        cq = cq_ref[...].astype(F32)
        r = lax.rsqrt(jnp.mean(cq * cq, axis=-1, keepdims=True) + EPS)
        cqn = (cq * r * gq_ref[...]).astype(BF16)
        cqt_ref[...] = cqn.T
        q = _dot(cqn, wuq_ref[...])
        ckv = ckv_ref[...].astype(F32)
        r = lax.rsqrt(jnp.mean(ckv * ckv, axis=-1, keepdims=True) + EPS)
        ckvn = (ckv * r * gkv_ref[...]).astype(BF16)
        ckvt_ref[...] = ckvn.T
        kv = _dot(ckvn, wkv_ref[...])
        c, s1, s2 = c_ref[...], s1_ref[...], s2_ref[...]
        lane = lax.broadcasted_iota(jnp.int32, (tm, LANE), 1)
        kpe = _rope(kpe_ref[...].astype(F32), c, s1, s2) + jnp.where((lane == QK) | (lane == QK + 1), 1.0, 0.0)
        vone = jnp.where((lane == VDIM) | (lane == VDIM + 1), 1.0, 0.0)
        for h in range(HEADS):
            sl = slice(LANE * h, LANE * (h + 1))
            qr_ref[:, sl] = (_rope(q[:, sl], c, s1, s2) * scale).astype(BF16)
            kr_ref[:, sl] = (kv[:, sl] + kpe).astype(BF16)
            v_ref[:, sl] = (kv[:, HEADS * LANE + LANE * h:HEADS * LANE + LANE * (h + 1)] + vone).astype(BF16)

    row = lambda w, j: pl.BlockSpec((tm, w), lambda i: (i, j))
    col = lambda w: pl.BlockSpec((w, tm), lambda i: (0, i))
    full = lambda a: pl.BlockSpec(a.shape, lambda i: (0, 0))
    return pl.pallas_call(
        body,
        grid=(s // tm,),
        in_specs=[row(768, 6), row(256, 21), row(128, 44), full(g_q), full(g_kv), full(w_uq_p), full(w_kv_p),
                  row(128, 0), row(128, 0), row(128, 0)],
        out_specs=[row(1024, 0), row(1024, 0), row(1024, 0), col(768), col(256)],
        out_shape=[jax.ShapeDtypeStruct((s, 1024), BF16), jax.ShapeDtypeStruct((s, 1024), BF16),
                   jax.ShapeDtypeStruct((s, 1024), BF16), jax.ShapeDtypeStruct((768, s), BF16),
                   jax.ShapeDtypeStruct((256, s), BF16)],
        compiler_params=_params(("arbitrary",)),
        name="mla_prep",
    )(proj, proj, proj, g_q, g_kv, w_uq_p, w_kv_p, rc, rs1, rs2)


ATT_T = 512
ATT_FWD_HEADS = 4


def _chunk_mask(transposed):
    r = lax.broadcasted_iota(jnp.int32, (ATT_T, ATT_T), 0) >> ATT_CHUNK_SHIFT
    c = lax.broadcasted_iota(jnp.int32, (ATT_T, ATT_T), 1) >> ATT_CHUNK_SHIFT
    return (r <= c) if transposed else (c <= r)


def _attn_fwd(qr, kr, vp, shards):
    s = qr.shape[0]
    t = ATT_T
    g = ATT_FWD_HEADS
    ns = len(shards)

    def body(q_ref, k_ref, v_ref, *rest):
        shard_refs, (o_ref, qa_ref), got_refs = rest[:ns], rest[ns:ns + 2], rest[ns + 2:2 * ns + 2]
        sc_ref, sems = rest[2 * ns + 2], rest[2 * ns + 3:]
        qi = pl.program_id(1)

        @pl.when((pl.program_id(0) == 0) & (qi == 0))
        def _():
            _start_all(*_to_all_copies(shard_refs, got_refs, sems, True))
        lane = lax.broadcasted_iota(jnp.int32, (t, LANE), 1)
        sls = [slice(LANE * a, LANE * (a + 1)) for a in range(g)]
        qs = [q_ref[:, sl] for sl in sls]

        def scores(j):
            rows = pl.ds(pl.multiple_of(j * t, t), t)
            for a in range(g):
                sc_ref[j & 1, a] = _dotg(qs[a], k_ref[rows, sls[a]], NT)

        def step(j, carry, masked):
            rows = pl.ds(pl.multiple_of(j * t, t), t)
            out = []
            for a in range(g):
                m, acc = carry[a]
                sc = sc_ref[j & 1, a]
                if masked:
                    sc = jnp.where(_chunk_mask(False), sc, -1e30)
                m_new = jnp.maximum(m, jnp.max(sc, axis=-1, keepdims=True))
                p = jnp.exp(sc - m_new).astype(BF16)
                acc = jnp.exp(m - m_new) * acc + _dot(p, v_ref[rows, sls[a]])
                out.append((m_new, acc))
            return tuple(out)

        def loop(j, carry):
            carry = step(j, carry, False)
            scores(j + 1)
            return carry

        init = tuple((jnp.full((t, 1), -1e30, F32), jnp.zeros((t, LANE), F32)) for _ in range(g))
        scores(0)
        carry = lax.fori_loop(0, qi, loop, init)
        carry = step(qi, carry, True)
        outs = []
        for a in range(g):
            m, acc = carry[a]
            l = acc[:, VDIM:VDIM + 1]
            outs.append(acc / l)
            hi, lo_part = _hi_lo(-(m + jnp.log(l)))
            qa = jnp.where(lane == QK, hi, jnp.where(lane == QK + 1, lo_part, qs[a].astype(F32)))
            qa_ref[:, sls[a]] = qa.astype(BF16)
        for p in range(g // 2):
            o_ref[:, LANE * p:LANE * (p + 1)] = jnp.where(lane < VDIM, outs[2 * p], pltpu.roll(outs[2 * p + 1], VDIM, 1))

        @pl.when((pl.program_id(0) == HEADS // g - 1) & (qi == s // t - 1))
        def _():
            _wait_all(*_to_all_copies(shard_refs, got_refs, sems, True))

    return pl.pallas_call(
        body,
        grid=(HEADS // g, s // t),
        in_specs=[
            pl.BlockSpec((t, g * LANE), lambda h, i: (i, h)),
            pl.BlockSpec((s, g * LANE), lambda h, i: (0, h)),
            pl.BlockSpec((s, g * LANE), lambda h, i: (0, h)),
        ] + [ANY] * ns,
        out_specs=[
            pl.BlockSpec((t, g * VDIM), lambda h, i: (i, h)),
            pl.BlockSpec((t, g * LANE), lambda h, i: (i, h)),
        ] + [ANY] * ns,
        out_shape=[jax.ShapeDtypeStruct((s, 512), F32), jax.ShapeDtypeStruct((s, 1024), BF16)]
        + [jax.ShapeDtypeStruct((N_DEV,) + b.shape, b.dtype) for b in shards],
        scratch_shapes=[pltpu.VMEM((2, g, t, t), F32)] + _copy_sems(ns, 7),
        compiler_params=_params(("arbitrary", "arbitrary")),
        name="attn_fwd",
    )(qr, kr, vp, *shards)


def _attn_bwd(qa, kr, vp, dop, sends):
    s = qa.shape[0]
    t = ATT_T
    nq = s // t
    ns = len(sends)

    def body(q_ref, k_ref, v_ref, do_ref, *rest):
        send_refs, (dq_out, dk_out, dv_out) = rest[:ns], rest[ns:ns + 3]
        recv_refs = rest[ns + 3:2 * ns + 3]
        (dq_ref, dk_ref, dv_ref), sems = rest[2 * ns + 3:2 * ns + 6], rest[2 * ns + 6:]
        j = pl.program_id(1)
        sls = [slice(LANE * a, LANE * (a + 1)) for a in range(2)]

        @pl.when((pl.program_id(0) == 0) & (j == 0))
        def _():
            _start_all(*_to_all_copies(send_refs, recv_refs, sems, False))

        @pl.when(j == 0)
        def _():
            dq_ref[...] = jnp.zeros_like(dq_ref)

        dk_ref[...] = jnp.zeros_like(dk_ref)
        dv_ref[...] = jnp.zeros_like(dv_ref)
        ks = [k_ref[:, sl] for sl in sls]
        vs = [v_ref[:, sl] for sl in sls]

        def part(i, k_lo, k_n, q_lo, q_n, masked):
            rows = pl.ds(pl.multiple_of(i * t + q_lo, 256), q_n)
            keys = slice(k_lo, k_lo + k_n)
            for a in range(2):
                q = q_ref[rows, sls[a]]
                do = do_ref[rows, sls[a]]
                sc = _dotg(ks[a][keys], q, NT)
                if masked:
                    kc = lax.broadcasted_iota(jnp.int32, (k_n, q_n), 0) >> ATT_CHUNK_SHIFT
                    qc = lax.broadcasted_iota(jnp.int32, (k_n, q_n), 1) >> ATT_CHUNK_SHIFT
                    sc = jnp.where(kc <= qc, sc, -1e30)
                p = jnp.exp(sc)
                ds = (p * _dotg(vs[a][keys], do, NT)).astype(BF16)
                dv_ref[keys, sls[a]] += _dot(p.astype(BF16), do)
                dk_ref[keys, sls[a]] += _dot(ds, q)
                dq_ref[rows, sls[a]] += _dotg(ds, ks[a][keys], TN)

        half = t // 2
        part(j, 0, half, 0, t, True)
        part(j, half, half, half, half, True)

        def loop(i, c):
            part(i, 0, t, 0, t, False)
            return c

        lax.fori_loop(j + 1, nq, loop, 0)
        dk_out[...] = dk_ref[...].astype(BF16)
        dv_out[...] = dv_ref[...].astype(BF16)

        @pl.when(j == nq - 1)
        def _():
            dq_out[...] = dq_ref[...].astype(BF16)

        @pl.when((pl.program_id(0) == HEADS // 2 - 1) & (j == nq - 1))
        def _():
            _wait_all(*_to_all_copies(send_refs, recv_refs, sems, False))

    blk = pl.BlockSpec((t, 2 * LANE), lambda h, j: (j, h))
    whole = pl.BlockSpec((s, 2 * LANE), lambda h, j: (0, h))
    out = jax.ShapeDtypeStruct((s, 1024), BF16)
    return pl.pallas_call(
        body,
        grid=(HEADS // 2, nq),
        in_specs=[whole, blk, blk, whole] + [ANY] * ns,
        out_specs=[whole, blk, blk] + [ANY] * ns,
        out_shape=[out, out, out] + [jax.ShapeDtypeStruct(a.shape, a.dtype) for a in sends],
        scratch_shapes=[pltpu.VMEM((s, 2 * LANE), F32), pltpu.VMEM((t, 2 * LANE), F32),
                        pltpu.VMEM((t, 2 * LANE), F32)] + _copy_sems(ns, 7),
        compiler_params=_params(("arbitrary", "arbitrary")),
        name="attn_bwd",
    )(qa, kr, vp, dop, *sends)


HG_T = 256
HG_NC = HG_T // HG_BLOCK
HG_G = 4
GW = 64 * HG_G


def _hg_consts():
    r = jnp.arange(HG_T)[:, None]
    c = jnp.arange(HG_T)[None, :]
    same = (r // HG_BLOCK) == (c // HG_BLOCK)
    mcum = (same & (c <= r)).astype(BF16)
    mrev = (same & (c >= r)).astype(BF16)
    msum = same.astype(BF16)
    a = jnp.arange(GW) // 64
    bd = (a[:, None] == a[None, :]).astype(F32)
    return mcum, mrev, msum, bd


def _stack_heads(xg, head):
    return jnp.concatenate([jnp.where(head == h, xg, 0.0) for h in range(HG_G)], axis=0)


def _unstack_heads(r, head, t):
    out = r[(HG_G - 1) * t:]
    for h in range(HG_G - 2, -1, -1):
        out = jnp.where(head == h, r[h * t:(h + 1) * t], out)
    return out


def _compact_state(st):
    out = st[:64]
    for h in range(1, HG_G):
        out = out + st[64 * h:64 * (h + 1)]
    return out


def _expand_state(cs, head64):
    return jnp.concatenate([jnp.where(head64 == h, cs, 0.0) for h in range(HG_G)], axis=0)


def _hg_pre(hq, hf, lbl, mcum, msum):
    lb = _sigmoid(lbl[0:1, :] - lbl[1:2, :])
    sig = _sigmoid(hf)
    f = lb + (1.0 - lb) * sig
    lf = jnp.log(f)
    b = _sel_left(mcum, lf)
    big_l = _sel_left(msum, lf)
    k = 1.0 - f
    qd = hq * jnp.exp(b)
    ki = k * jnp.exp(-b)
    ke = k * jnp.exp(big_l - b)
    return lb, sig, f, b, big_l, qd, ki, ke


def _hgrn_fwd(proj, lbl):
    s = proj.shape[0]
    t = HG_T
    mcum, _, msum, bd = _hg_consts()

    def body(hq_ref, hf_ref, hi_ref, lbl_ref, mcum_ref, msum_ref, bd_ref, o_ref, sp_ref, st_ref):
        @pl.when(pl.program_id(0) == 0)
        def _():
            st_ref[...] = jnp.zeros_like(st_ref)

        mc = mcum_ref[...]
        _, _, _, _, big_l, qd, ki, ke = _hg_pre(hq_ref[...].astype(F32), hf_ref[...].astype(F32), lbl_ref[...], mc,
                                                msum_ref[...])
        el = jnp.exp(big_l)
        hi = hi_ref[...]
        head = lax.broadcasted_iota(jnp.int32, (t, GW), 1) >> 6
        mask = jnp.concatenate([mc] * HG_G, axis=0) > 0.5
        for p in range(HEADS // HG_G):
            sl = slice(GW * p, GW * (p + 1))
            vp = hi[:, sl].astype(BF16)
            qs = _stack_heads(qd[:, sl], head).astype(BF16)
            a = jnp.where(mask, _dotg(qs, ki[:, sl].astype(BF16), NT), 0.0)
            o_intra = _unstack_heads(_dot(a.astype(BF16), vp), head, t)
            qb = qd[:, sl].astype(BF16)
            kb = ke[:, sl].astype(BF16)
            st = st_ref[p]
            for c in range(HG_NC):
                rows = slice(HG_BLOCK * c, HG_BLOCK * (c + 1))
                sp_ref[c, :, sl] = _compact_state(st)
                o_ref[rows, sl] = o_intra[rows] + _dotg(qb[rows], st.astype(BF16), NT)
                u = _dotg(vp[rows], kb[rows], TN) * bd_ref[...]
                st = st * el[HG_BLOCK * c:HG_BLOCK * c + 1, sl] + u
            st_ref[p] = st

    row = lambda j: pl.BlockSpec((t, HG_WIDTH), lambda i: (i, j))
    full = lambda a: pl.BlockSpec(a.shape, lambda i: (0, 0))
    return pl.pallas_call(
        body,
        grid=(s // t,),
        in_specs=[row(6), row(7), row(8), full(lbl), full(mcum), full(msum), full(bd)],
        out_specs=[row(0), pl.BlockSpec((HG_NC, 64, HG_WIDTH), lambda i: (i, 0, 0))],
        out_shape=[jax.ShapeDtypeStruct((s, HG_WIDTH), F32),
                   jax.ShapeDtypeStruct((s // HG_BLOCK, 64, HG_WIDTH), F32)],
        scratch_shapes=[pltpu.VMEM((HEADS // HG_G, GW, GW), F32)],
        compiler_params=_params(("arbitrary",)),
        name="hgrn_fwd",
    )(proj, proj, proj, lbl, mcum, msum, bd)


def _slot_shape(name, r, c):
    return (N_DEV, r, c // N_DEV) if COL_SHARDED[name] else (N_DEV, r // N_DEV, c)


def _emit_slots(name, acc_ref, out_ref):
    r, c = acc_ref.shape
    for p in range(N_DEV):
        if COL_SHARDED[name]:
            out_ref[p] = acc_ref[:, c // N_DEV * p:c // N_DEV * (p + 1)].astype(BF16)
        else:
            out_ref[p] = acc_ref[r // N_DEV * p:r // N_DEV * (p + 1), :].astype(BF16)


def _hgrn_bwd(proj, lbl, do, sprev, dproj, pairs):
    s = proj.shape[0]
    t = HG_T
    nt = s // t
    npair = len(pairs)
    mcum, mrev, msum, bd = _hg_consts()

    def body(hq_ref, hf_ref, hi_ref, lbl_ref, do_ref, sp_ref, mcum_ref, mrev_ref, msum_ref, bd_ref,
             dproj_in, *rest):
        del dproj_in
        pair_refs, (dh_ref, dlbl_ref) = rest[:2 * npair], rest[2 * npair:2 * npair + 2]
        dw_refs, g_ref, acc_refs = rest[2 * npair + 2:3 * npair + 2], rest[3 * npair + 2], rest[3 * npair + 3:]

        @pl.when(pl.program_id(0) == 0)
        def _():
            g_ref[...] = jnp.zeros_like(g_ref)
            dlbl_ref[...] = jnp.zeros_like(dlbl_ref)
            for acc_ref in acc_refs:
                acc_ref[...] = jnp.zeros_like(acc_ref)

        for n, acc_ref in enumerate(acc_refs):
            acc_ref[...] += _dot(pair_refs[2 * n][...], pair_refs[2 * n + 1][...])

        @pl.when(pl.program_id(0) == nt - 1)
        def _():
            for (name, _, _), acc_ref, dw_ref in zip(pairs, acc_refs, dw_refs):
                _emit_slots(name, acc_ref, dw_ref)

        mc = mcum_ref[...]
        lb, sig, f, b, big_l, qd, ki, ke = _hg_pre(hq_ref[...].astype(F32), hf_ref[...].astype(F32), lbl_ref[...], mc,
                                                   msum_ref[...])
        el = jnp.exp(big_l)
        hi = hi_ref[...]
        dov = do_ref[...]
        head = lax.broadcasted_iota(jnp.int32, (t, GW), 1) >> 6
        head64 = lax.broadcasted_iota(jnp.int32, (64, GW), 1) >> 6
        mask = jnp.concatenate([mc] * HG_G, axis=0) > 0.5
        dqd_parts, dke_parts, dv_parts, del_parts, dki_parts = [], [], [], [], []
        for p in range(HEADS // HG_G):
            sl = slice(GW * p, GW * (p + 1))
            vp = hi[:, sl].astype(BF16)
            qs = _stack_heads(qd[:, sl], head).astype(BF16)
            kip = ki[:, sl].astype(BF16)
            dos = _stack_heads(dov[:, sl], head).astype(BF16)
            a = jnp.where(mask, _dotg(qs, kip, NT), 0.0).astype(BF16)
            da = jnp.where(mask, _dotg(dos, vp, NT), 0.0).astype(BF16)
            r = _dot(da, kip)
            dki_parts.append(_dotg(da, qs, TN))
            qb = qd[:, sl].astype(BF16)
            kb = ke[:, sl].astype(BF16)
            dob = dov[:, sl].astype(BF16)
            g = g_ref[p]
            dqd_c, dv_c, dke_c, del_c = [], [], [], []
            for c in range(HG_NC - 1, -1, -1):
                rows = slice(HG_BLOCK * c, HG_BLOCK * (c + 1))
                gb = g.astype(BF16)
                st = _expand_state(sp_ref[c, :, sl], head64)
                dqd_c.append(_dot(dob[rows], st.astype(BF16)))
                dv_c.append(_dotg(kb[rows], gb, NT))
                dke_c.append(_dot(vp[rows], gb))
                del_c.append(jnp.broadcast_to(jnp.sum(g * st, axis=0, keepdims=True), (HG_BLOCK, GW)))
                g = g * el[HG_BLOCK * c:HG_BLOCK * c + 1, sl] + _dotg(dob[rows], qb[rows], TN) * bd_ref[...]
            g_ref[p] = g
            up = lambda parts: jnp.concatenate(parts[::-1], axis=0)
            dqd_parts.append(_unstack_heads(r, head, t) + up(dqd_c))
            dv_parts.append(_dotg(a, dos, TN) + up(dv_c))
            dke_parts.append(up(dke_c))
            del_parts.append(up(del_c))
        wide = lambda parts: jnp.concatenate(parts, axis=1)
        dqd, dke, dki, dvv, del_rows = wide(dqd_parts), wide(dke_parts), wide(dki_parts), wide(dv_parts), wide(del_parts)
        dh_ref[:, :HG_WIDTH] = (dqd * jnp.exp(b)).astype(BF16)
        dh_ref[:, 2 * HG_WIDTH:] = dvv.astype(BF16)
        dke_ke = dke * ke
        db = dqd * qd - dki * ki - dke_ke
        dl_rows = _sel_left(msum_ref[...], dke_ke) + del_rows * el
        is_last = (lax.broadcasted_iota(jnp.int32, (t, HG_WIDTH), 0) & (HG_BLOCK - 1)) == HG_BLOCK - 1
        db = db + jnp.where(is_last, dl_rows, 0.0)
        dlf = _sel_left(mrev_ref[...], db)
        dk = dki * jnp.exp(-b) + dke * jnp.exp(big_l - b)
        df = dlf / f - dk
        dh_ref[:, HG_WIDTH:2 * HG_WIDTH] = (df * (1.0 - lb) * sig * (1.0 - sig)).astype(BF16)
        dlb = jnp.sum(df * (1.0 - sig), axis=0, keepdims=True) * lb * (1.0 - lb)
        dlbl_ref[0:1, :] += dlb
        dlbl_ref[1:2, :] -= dlb

    rrow = lambda j: pl.BlockSpec((t, HG_WIDTH), lambda i: (nt - 1 - i, j))
    full = lambda a: pl.BlockSpec(a.shape, lambda i: (0, 0))
    pair_specs, dw_specs, dw_shapes, accs = [], [], [], []
    for name, at, b in pairs:
        pair_specs += [pl.BlockSpec((at.shape[0], t), lambda i: (0, i)), pl.BlockSpec((t, b.shape[1]), lambda i: (i, 0))]
        shape = _slot_shape(name, at.shape[0], b.shape[1])
        dw_specs.append(pl.BlockSpec(shape, lambda i: (0, 0, 0)))
        dw_shapes.append(jax.ShapeDtypeStruct(shape, BF16))
        accs.append(pltpu.VMEM((at.shape[0], b.shape[1]), F32))
    return pl.pallas_call(
        body,
        grid=(nt,),
        in_specs=[rrow(6), rrow(7), rrow(8), full(lbl), rrow(0),
                  pl.BlockSpec((HG_NC, 64, HG_WIDTH), lambda i: (nt - 1 - i, 0, 0)),
                  full(mcum), full(mrev), full(msum), full(bd), pl.BlockSpec(memory_space=pl.ANY)] + pair_specs,
        out_specs=[pl.BlockSpec((t, 3 * HG_WIDTH), lambda i: (nt - 1 - i, 2)),
                   pl.BlockSpec((2, HG_WIDTH), lambda i: (0, 0))] + dw_specs,
        out_shape=[jax.ShapeDtypeStruct(dproj.shape, BF16), jax.ShapeDtypeStruct((2, HG_WIDTH), F32)] + dw_shapes,
        input_output_aliases={10: 0},
        scratch_shapes=[pltpu.VMEM((HEADS // HG_G, GW, GW), F32)] + accs,
        compiler_params=_params(("arbitrary",)),
        name="hgrn_bwd",
    )(proj, proj, proj, lbl, do, sprev, mcum, mrev, msum, bd, dproj, *[a for pair in pairs for a in pair[1:]])


def _tail(x, tgt, proj, attn, o, w_a, w_b, w_out, w_at, w_bt, w_outt, b_gate, g_post, gh):
    s = x.shape[0]
    tm = 256
    ones64 = (jnp.arange(HG_WIDTH)[:, None] // 64 == jnp.arange(HG_WIDTH)[None, :] // 64).astype(BF16)
    weights = (w_a, w_b, w_out, w_at, w_bt, w_outt)

    def body(x_ref, t_ref, ml_ref, ga_ref, gb_ref, at_ref, o_ref, *rest):
        w_hbm, (bg_ref, gp_ref, gh_ref, ones_ref) = rest[:6], rest[6:10]
        (dout_ref, dpj_ref, dop_ref, do_ref, mt_ref, dy_ref, yat_ref, dya_ref, ybt_ref, dyb_ref,
         loss_ref, dgp_ref, dbg_ref, dgh_ref) = rest[10:24]
        (wa_ref, wb_ref, wo_ref, wat_ref, wbt_ref, wot_ref), w_sem = rest[24:30], rest[30]

        @pl.when(pl.program_id(0) == 0)
        def _():
            loads = [pltpu.make_async_copy(src, dst, w_sem.at[k])
                     for k, (src, dst) in enumerate(zip(w_hbm, rest[24:30]))]
            _start_all(loads, [])
            loss_ref[...] = jnp.zeros_like(loss_ref)
            dgp_ref[...] = jnp.zeros_like(dgp_ref)
            dbg_ref[...] = jnp.zeros_like(dbg_ref)
            dgh_ref[...] = jnp.zeros_like(dgh_ref)
            _wait_all(loads, [])

        ones = ones_ref[...]
        gate_a = ga_ref[...].astype(F32)
        sa = _sigmoid(gate_a)
        silu_a = gate_a * sa
        attn_v = at_ref[...]
        ya_in = attn_v * silu_a
        ov = o_ref[...]
        ro = lax.rsqrt(_sel_right(ov * ov, ones) * (1.0 / 64.0) + EPS)
        ohat = ov * ro
        ghv = gh_ref[...]
        on = ohat * ghv
        gate_b = gb_ref[...].astype(F32)
        sb = _sigmoid(gate_b)
        silu_b = gate_b * sb
        yb_in = on * silu_b
        ya_bf = ya_in.astype(BF16)
        yb_bf = yb_in.astype(BF16)
        yat_ref[...] = ya_bf.T
        ybt_ref[...] = yb_bf.T
        y_a = _dot(ya_bf, wa_ref[...])
        y_b = _dot(yb_bf, wb_ref[...])
        gts = _sigmoid(ml_ref[...].astype(F32) + bg_ref[...])
        g_a = gts[:, :D_MODEL]
        g_b = gts[:, D_MODEL:]
        m_bf = (g_a * y_a + g_b * y_b).astype(BF16)
        mt_ref[...] = m_bf.T
        y = _dot(m_bf, wo_ref[...])
        r1 = lax.rsqrt(jnp.mean(y * y, axis=-1, keepdims=True) + EPS)
        yn = y * r1
        gp = gp_ref[...]
        e = x_ref[...] + yn * gp - t_ref[...]
        loss_ref[...] += jnp.sum(e * e, axis=0, keepdims=True)
        dout = e * (1.0 / D_MODEL)
        dout_ref[...] = dout
        dgp_ref[...] += jnp.sum(dout * yn, axis=0, keepdims=True)
        dyn = dout * gp
        dy = r1 * (dyn - yn * jnp.mean(dyn * yn, axis=-1, keepdims=True))
        dy_bf = dy.astype(BF16)
        dy_ref[...] = dy_bf
        dm = _dot(dy_bf, wot_ref[...])
        dml_a = dm * y_a * g_a * (1.0 - g_a)
        dml_b = dm * y_b * g_b * (1.0 - g_b)
        dpj_ref[:, :D_MODEL] = dml_a.astype(BF16)
        dpj_ref[:, D_MODEL:2 * D_MODEL] = dml_b.astype(BF16)
        dbg_ref[:, :D_MODEL] += jnp.sum(dml_a, axis=0, keepdims=True)
        dbg_ref[:, D_MODEL:] += jnp.sum(dml_b, axis=0, keepdims=True)
        dya_bf = (dm * g_a).astype(BF16)
        dyb_bf = (dm * g_b).astype(BF16)
        dya_ref[...] = dya_bf
        dyb_ref[...] = dyb_bf
        dya_in = _dot(dya_bf, wat_ref[...])
        dyb_in = _dot(dyb_bf, wbt_ref[...])
        dattn = dya_in * silu_a
        delta = _sel_right(dattn * attn_v, ones)
        lane = lax.broadcasted_iota(jnp.int32, (tm, LANE), 1)
        for p in range(HEADS // 2):
            sl = slice(LANE * p, LANE * (p + 1))
            xs = (dattn[:, sl], pltpu.roll(dattn[:, sl], VDIM, 1))
            nds = (-pltpu.roll(delta[:, sl], VDIM, 1), -delta[:, sl])
            for a in range(2):
                hi, lo_part = _hi_lo(nds[a])
                blk = jnp.where(lane < VDIM, xs[a], jnp.where(lane == VDIM, hi, jnp.where(lane == VDIM + 1, lo_part, 0.0)))
                dop_ref[:, LANE * (2 * p + a):LANE * (2 * p + a + 1)] = blk.astype(BF16)
        dpj_ref[:, 2 * D_MODEL:2 * D_MODEL + HG_WIDTH] = (
            dya_in * attn_v * (sa * (1.0 + gate_a * (1.0 - sa)))).astype(BF16)
        don = dyb_in * silu_b
        dpj_ref[:, 2 * D_MODEL + HG_WIDTH:] = (dyb_in * on * (sb * (1.0 + gate_b * (1.0 - sb)))).astype(BF16)
        dgh_ref[...] += jnp.sum(don * ohat, axis=0, keepdims=True)
        dohat = don * ghv
        do_ref[...] = (ro * (dohat - ohat * (_sel_right(dohat * ohat, ones) * (1.0 / 64.0)))).astype(BF16)

    row = lambda w, j: pl.BlockSpec((tm, w), lambda i: (i, j))
    col = lambda w: pl.BlockSpec((w, tm), lambda i: (0, i))
    full = lambda a: pl.BlockSpec(a.shape, lambda i: (0, 0))
    acc = lambda w: pl.BlockSpec((1, w), lambda i: (0, 0))
    sds = lambda w, dt: jax.ShapeDtypeStruct((s, w), dt)
    sdt = lambda w: jax.ShapeDtypeStruct((w, s), BF16)
    return pl.pallas_call(
        body,
        grid=(s // tm,),
        in_specs=[row(1024, 0), row(1024, 0), row(2048, 0), row(512, 4), row(512, 5), row(512, 0), row(512, 0)]
        + [ANY] * 6 + [full(b_gate), full(g_post), full(gh), full(ones64)],
        out_specs=[row(1024, 0), row(3072, 0), row(1024, 0), row(512, 0),
                   col(1024), row(1024, 0), col(512), row(1024, 0), col(512), row(1024, 0),
                   acc(1024), acc(1024), acc(2048), acc(512)],
        out_shape=[sds(1024, F32), sds(D_IN_PAD, BF16), sds(1024, BF16), sds(512, BF16),
                   sdt(1024), sds(1024, BF16), sdt(512), sds(1024, BF16), sdt(512), sds(1024, BF16),
                   jax.ShapeDtypeStruct((1, 1024), F32), jax.ShapeDtypeStruct((1, 1024), F32),
                   jax.ShapeDtypeStruct((1, 2048), F32), jax.ShapeDtypeStruct((1, 512), F32)],
        scratch_shapes=[pltpu.VMEM(a.shape, BF16) for a in weights] + [pltpu.SemaphoreType.DMA((6,))],
        compiler_params=_params(("arbitrary",), 56),
        name="tail",
    )(x, tgt, proj, proj, proj, attn, o, *weights, b_gate, g_post, gh, ones64)


def _mla_bwd(proj, dqr, dkr, dv, g_q, g_kv, w_uq_pt, w_kv_pt, rc, rs1, rs2, cqt, ckvt, dproj):
    assert HEADS == N_DEV
    s = proj.shape[0]
    tm = 256
    scale = 1.0 / math.sqrt(QK)

    def body(cq_ref, ckv_ref, dqr_ref, dkr_ref, dv_ref, gq_ref, gkv_ref, wuqt_ref, wkvt_ref, c_ref, s1_ref, s2_ref,
             cqt_ref, ckvt_ref, dproj_in, dc_ref, dgq_ref, dgkv_ref, uq_slots, ukv_slots,
             dqf_ref, dkvf_ref, dwuq_ref, dwkv_ref):
        del dproj_in

        @pl.when(pl.program_id(0) == 0)
        def _():
            dgq_ref[...] = jnp.zeros_like(dgq_ref)
            dgkv_ref[...] = jnp.zeros_like(dgkv_ref)
            dwuq_ref[...] = jnp.zeros_like(dwuq_ref)
            dwkv_ref[...] = jnp.zeros_like(dwkv_ref)

        c, s1, s2 = c_ref[...], s1_ref[...], s2_ref[...]
        lane = lax.broadcasted_iota(jnp.int32, (tm, LANE), 1)
        ksum = jnp.zeros((tm, LANE), F32)
        for h in range(HEADS):
            sl = slice(LANE * h, LANE * (h + 1))
            dqf_ref[:, sl] = (_unrope(dqr_ref[:, sl], c, s1, s2) * scale).astype(BF16)
            dkh = dkr_ref[:, sl]
            ksum = ksum + dkh
            dkvf_ref[:, sl] = jnp.where(lane < NOPE, dkh, 0.0).astype(BF16)
            dkvf_ref[:, HEADS * LANE + LANE * h:HEADS * LANE + LANE * (h + 1)] = jnp.where(
                lane < VDIM, dv_ref[:, sl], 0.0).astype(BF16)
        dkpe = _unrope(ksum, c, s1, s2)
        dc_ref[:, Q_LORA + KV_LORA:] = jnp.where((lane >= NOPE) & (lane < QK), dkpe, 0.0).astype(BF16)
        dqf, dkvf = dqf_ref[...], dkvf_ref[...]
        dwuq_ref[...] += _dot(cqt_ref[...], dqf)
        dwkv_ref[...] += _dot(ckvt_ref[...], dkvf)
        dcqn = _dot(dqf, wuqt_ref[...])
        dckvn = _dot(dkvf, wkvt_ref[...])
        for x_ref, g_ref, dn, cols, dg_ref in ((cq_ref, gq_ref, dcqn, slice(0, Q_LORA), dgq_ref),
                                               (ckv_ref, gkv_ref, dckvn, slice(Q_LORA, Q_LORA + KV_LORA), dgkv_ref)):
            xv = x_ref[...].astype(F32)
            r = lax.rsqrt(jnp.mean(xv * xv, axis=-1, keepdims=True) + EPS)
            xh = xv * r
            dg_ref[...] += jnp.sum(dn * xh, axis=0, keepdims=True)
            dh = dn * g_ref[...]
            dc_ref[:, cols] = (r * (dh - xh * jnp.mean(dh * xh, axis=-1, keepdims=True))).astype(BF16)

        @pl.when(pl.program_id(0) == s // tm - 1)
        def _():
            ur = Q_LORA // N_DEV
            for p in range(N_DEV):
                uq_slots[p] = jnp.concatenate(
                    [dwuq_ref[ur * p:ur * (p + 1), LANE * h:LANE * h + QK] for h in range(HEADS)], axis=1).astype(BF16)
                ukv_slots[p] = jnp.concatenate(
                    [dwkv_ref[:, LANE * p:LANE * p + NOPE],
                     dwkv_ref[:, LANE * (HEADS + p):LANE * (HEADS + p) + VDIM]], axis=1).astype(BF16)

    row = lambda w, j: pl.BlockSpec((tm, w), lambda i: (i, j))
    full = lambda a: pl.BlockSpec(a.shape, lambda i: (0, 0))
    acc = lambda w: pl.BlockSpec((1, w), lambda i: (0, 0))
    col = lambda w: pl.BlockSpec((w, tm), lambda i: (0, i))
    whole = lambda shape: pl.BlockSpec(shape, lambda i: (0, 0, 0))
    uq_shape = (N_DEV, Q_LORA // N_DEV, HEADS * QK)
    ukv_shape = (N_DEV, KV_LORA, NOPE + VDIM)
    return pl.pallas_call(
        body,
        grid=(s // tm,),
        in_specs=[row(768, 6), row(256, 21), row(1024, 0), row(1024, 0), row(1024, 0), full(g_q), full(g_kv),
                  full(w_uq_pt), full(w_kv_pt), row(128, 0), row(128, 0), row(128, 0), col(Q_LORA), col(KV_LORA),
                  pl.BlockSpec(memory_space=pl.ANY)],
        out_specs=[row(1152, 4), acc(768), acc(256), whole(uq_shape), whole(ukv_shape)],
        out_shape=[jax.ShapeDtypeStruct(dproj.shape, BF16),
                   jax.ShapeDtypeStruct((1, 768), F32), jax.ShapeDtypeStruct((1, 256), F32),
                   jax.ShapeDtypeStruct(uq_shape, BF16), jax.ShapeDtypeStruct(ukv_shape, BF16)],
        input_output_aliases={14: 0},
        scratch_shapes=[pltpu.VMEM((tm, HEADS * LANE), BF16), pltpu.VMEM((tm, 2 * HEADS * LANE), BF16),
                        pltpu.VMEM((Q_LORA, HEADS * LANE), F32), pltpu.VMEM((KV_LORA, 2 * HEADS * LANE), F32)],
        compiler_params=_params(("arbitrary",)),
        name="mla_bwd",
    )(proj, proj, dqr, dkr, dv, g_q, g_kv, w_uq_pt, w_kv_pt, rc, rs1, rs2, cqt, ckvt, dproj)


def _dh_dx(dproj, w_in_pt, x, dout, g_pre, sends):
    s, k = dproj.shape
    tm = 256
    ns, ni = len(sends), s // tm

    def body(dp_ref, w_ref, x_ref, dout_ref, g_ref, *rest):
        send_refs, (dx_ref, dg_ref) = rest[:ns], rest[ns:ns + 2]
        recv_refs, sems = rest[ns + 2:2 * ns + 2], rest[2 * ns + 2:]

        @pl.when(pl.program_id(0) == 0)
        def _():
            _start_all(*_to_chips_copies(send_refs, recv_refs, sems))
            dg_ref[...] = jnp.zeros_like(dg_ref)

        dh = _dot(dp_ref[...], w_ref[...])
        xv = x_ref[...]
        r = lax.rsqrt(jnp.mean(xv * xv, axis=-1, keepdims=True) + EPS)
        xh = xv * r
        dg_ref[...] += jnp.sum(dh * xh, axis=0, keepdims=True)
        dxh = dh * g_ref[...]
        dx_ref[...] = dout_ref[...] + r * (dxh - xh * jnp.mean(dxh * xh, axis=-1, keepdims=True))

        @pl.when(pl.program_id(0) == ni - 1)
        def _():
            _wait_all(*_to_chips_copies(send_refs, recv_refs, sems))

    row = lambda w: pl.BlockSpec((tm, w), lambda i: (i, 0))
    return pl.pallas_call(
        body,
        grid=(ni,),
        in_specs=[row(k), pl.BlockSpec((k, D_MODEL), lambda i: (0, 0)), row(D_MODEL), row(D_MODEL),
                  pl.BlockSpec((1, D_MODEL), lambda i: (0, 0))] + [ANY] * ns,
        out_specs=[row(D_MODEL), pl.BlockSpec((1, D_MODEL), lambda i: (0, 0))] + [ANY] * ns,
        out_shape=[jax.ShapeDtypeStruct((s, D_MODEL), F32), jax.ShapeDtypeStruct((1, D_MODEL), F32)]
        + [jax.ShapeDtypeStruct(a.shape, a.dtype) for a in sends],
        scratch_shapes=_copy_sems(ns, 3),
        compiler_params=_params(("arbitrary",)),
        name="dh_dx",
    )(dproj, w_in_pt, x, dout, g_pre, *sends)


def _pair_reduce(slots):
    n = len(slots)
    half = [(N_DEV // 2,) + a.shape[1:] for a in slots]

    def body(*refs):
        s_refs, o_refs = refs[:n], refs[n:2 * n]
        mine, got = refs[2 * n:3 * n], refs[3 * n:4 * n]
        send_sems, recv_sems, local_sems = refs[4 * n:]
        x, y, c = _my_place()
        copies, loads = [], []
        for a in range(n):
            for q in range(N_DEV // 2):
                copies.append(pltpu.make_async_remote_copy(
                    src_ref=s_refs[a].at[2 * q + 1 - c], dst_ref=got[a].at[q],
                    send_sem=send_sems.at[4 * a + q], recv_sem=recv_sems.at[4 * a + q],
                    device_id=(x, y, 1 - c), device_id_type=MESH_ID))
                loads.append(pltpu.make_async_copy(s_refs[a].at[2 * q + c], mine[a].at[q], local_sems.at[4 * a + q]))
        _start_all(loads, copies)
        _wait_all(loads, copies)
        for a in range(n):
            o_refs[a][...] = (mine[a][...].astype(F32) + got[a][...].astype(F32)).astype(o_refs[a].dtype)

    vm = lambda: [pltpu.VMEM(h, a.dtype) for h, a in zip(half, slots)]
    return pl.pallas_call(
        body,
        in_specs=[ANY] * n,
        out_shape=[jax.ShapeDtypeStruct(h, a.dtype) for h, a in zip(half, slots)],
        scratch_shapes=vm() + vm() + [pltpu.SemaphoreType.DMA((4 * n,)), pltpu.SemaphoreType.DMA((4 * n,)),
                                      pltpu.SemaphoreType.DMA((4 * n,))],
        compiler_params=pltpu.CompilerParams(vmem_limit_bytes=48 * 2**20),
        name="pair_reduce",
    )(*slots)


def _rope_tables(s):
    inv = (np.float32(ROPE_THETA) ** (-np.arange(0, ROPE, 2, dtype=np.float32) / np.float32(ROPE))).astype(np.float32)
    ang = (np.arange(s, dtype=np.float32)[:, None] * inv[None, :]).astype(np.float32)
    cos, sin = jnp.asarray(np.cos(ang.astype(np.float64)), F32), jnp.asarray(np.sin(ang.astype(np.float64)), F32)
    z = lambda w: jnp.zeros((s, w), F32)
    rc = jnp.concatenate([jnp.ones((s, NOPE), F32), cos, cos, z(32)], axis=1)
    rs1 = jnp.concatenate([z(NOPE), -sin, z(16), z(32)], axis=1)
    rs2 = jnp.concatenate([z(NOPE), z(16), sin, z(32)], axis=1)
    return rc, rs1, rs2


def _step(x, tgt, w_blk, shards, g_pre, b_gate, g_q, g_kv, lbl, g_hgrn, g_post):
    s = x.shape[0]
    rc, rs1, rs2 = _rope_tables(s)
    gh = jnp.tile(g_hgrn, (1, HEADS))

    proj, w_in_pt, ht, *got = _gather_proj(x, g_pre, w_blk, shards[:2])
    w_uq, w_ukv = (_from_slots(n, g) for n, g in zip(MATS[:2], got))
    w_uq_p = jnp.pad(w_uq.reshape(Q_LORA, HEADS, QK), ((0, 0), (0, 0), (0, LANE - QK))).reshape(Q_LORA, HEADS * LANE)
    kv3 = w_ukv.reshape(KV_LORA, HEADS, NOPE + VDIM)
    pad64 = lambda t: jnp.pad(t, ((0, 0), (0, 0), (0, LANE - 64))).reshape(KV_LORA, HEADS * LANE)
    w_kv_p = jnp.concatenate([pad64(kv3[:, :, :NOPE]), pad64(kv3[:, :, NOPE:])], axis=1)

    qr, kr, v, cqt, ckvt = _mla_prep(proj, g_q, g_kv, w_uq_p, w_kv_p, rc, rs1, rs2)
    attn, qa, *got = _attn_fwd(qr, kr, v, shards[2:])
    w_a, w_b, w_out = (_from_slots(n, g) for n, g in zip(MATS[2:], got))
    o, sprev = _hgrn_fwd(proj, lbl)
    (dout, dproj, dop, do, mt, dy_bf, yat, dya_bf, ybt, dyb_bf,
     loss_vec, dg_post, db_gate, dgh) = _tail(x, tgt, proj, attn, o, w_a, w_b, w_out, w_a.T, w_b.T, w_out.T,
                                               b_gate, g_post, gh)
    dproj, dlbl, *early = _hgrn_bwd(proj, lbl, do, sprev, dproj,
                                    [("w_branch_a", yat, dya_bf), ("w_branch_b", ybt, dyb_bf), ("w_out", mt, dy_bf)])
    dqr, dkr, dv, *early_recv = _attn_bwd(qa, kr, v, dop, early)
    dproj, dg_q, dg_kv, dw_uq_slots, dw_ukv_slots = _mla_bwd(proj, dqr, dkr, dv, g_q, g_kv, w_uq_p.T, w_kv_p.T,
                                                             rc, rs1, rs2, cqt, ckvt, dproj)

    dw_in_slots = _dw_in_slots(ht, dproj)
    late = _pair_reduce([dw_in_slots, dw_uq_slots, dw_ukv_slots])
    dx, dg_pre, *late_recv = _dh_dx(dproj, w_in_pt, x, dout, g_pre, late)

    g_sum = _vectors_sum(dg_pre, db_gate, dg_q, dg_kv, dlbl, dgh, dg_post, loss_vec)
    return dx, late_recv[0], dict(zip(MATS, late_recv[1:] + early_recv)), g_sum


def _adamw(g, w, m, v):
    c1 = 1.0 / (1.0 - ADAM_B1 ** ADAM_STEP)
    c2 = 1.0 / (1.0 - ADAM_B2 ** ADAM_STEP)
    nm = ADAM_B1 * m + (1.0 - ADAM_B1) * g
    nv = ADAM_B2 * v + (1.0 - ADAM_B2) * (g * g)
    d = -ADAM_LR * ((nm * c1) / (jnp.sqrt(nv * c2) + ADAM_EPS) + ADAM_WD * w)
    return d, nm, nv


def _sum8(r_ref):
    g = r_ref[0].astype(F32)
    for k in range(1, r_ref.shape[0]):
        g = g + r_ref[k].astype(F32)
    return g


def _sum_adamw_w_in(recv, w, m, v):
    rows, _, cols = w.shape
    tc = 256
    nc = cols // tc

    def body(r_ref, w_hbm, m_hbm, v_hbm, g_hbm, d_hbm, nm_hbm, nv_hbm, ins, outs, in_sems, out_sems):
        i = pl.program_id(0)
        slot = i & 1
        cols_of = lambda step: pl.ds(pl.multiple_of(step * tc, tc), tc)

        def load(k, step, sl):
            return pltpu.make_async_copy((w_hbm, m_hbm, v_hbm)[k].at[:, 0, cols_of(step)], ins.at[sl, k],
                                         in_sems.at[sl, k])

        def store(k, step, sl):
            return pltpu.make_async_copy(outs.at[sl, k], (g_hbm, d_hbm, nm_hbm, nv_hbm)[k].at[:, 0, cols_of(step)],
                                         out_sems.at[sl, k])

        @pl.when(i == 0)
        def _():
            for k in range(3):
                load(k, 0, 0).start()

        @pl.when(i + 1 < nc)
        def _():
            for k in range(3):
                load(k, i + 1, 1 - slot).start()

        @pl.when(i >= 2)
        def _():
            for k in range(4):
                store(k, i - 2, slot).wait()

        for k in range(3):
            load(k, i, slot).wait()
        g = _sum8(r_ref)
        d, nm, nv = _adamw(g, ins[slot, 0], ins[slot, 1], ins[slot, 2])
        for k, val in enumerate((g, d, nm, nv)):
            outs[slot, k] = val
        for k in range(4):
            store(k, i, slot).start()

        @pl.when(i == nc - 1)
        def _():
            for k in range(4):
                store(k, i, slot).wait()
            if nc >= 2:
                for k in range(4):
                    store(k, i - 1, 1 - slot).wait()

    out = jax.ShapeDtypeStruct((rows, 1, cols), F32)
    return pl.pallas_call(
        body,
        grid=(nc,),
        in_specs=[pl.BlockSpec((recv.shape[0], rows, tc), lambda i: (0, 0, i)), ANY, ANY, ANY],
        out_specs=[ANY, ANY, ANY, ANY],
        out_shape=[out, out, out, out],
        scratch_shapes=[pltpu.VMEM((2, 3, rows, tc), F32), pltpu.VMEM((2, 4, rows, tc), F32),
                        pltpu.SemaphoreType.DMA((2, 3)), pltpu.SemaphoreType.DMA((2, 4))],
        compiler_params=_params(("arbitrary",)),
        name="sum_adamw_w_in",
    )(recv, w, m, v)


def _sum_adamw_whole(recvs, ws, ms, vs):
    n = len(ws)

    def body(*refs):
        r_refs, w_refs, m_refs, v_refs = refs[:n], refs[n:2 * n], refs[2 * n:3 * n], refs[3 * n:4 * n]
        outs = refs[4 * n:]
        for a in range(n):
            g = _sum8(r_refs[a])
            d, nm, nv = _adamw(g, w_refs[a][...], m_refs[a][...], v_refs[a][...])
            outs[a][...] = g
            outs[n + a][...] = d
            outs[2 * n + a][...] = nm
            outs[3 * n + a][...] = nv

    shapes = [jax.ShapeDtypeStruct(w.shape, F32) for w in ws]
    res = pl.pallas_call(
        body,
        out_shape=shapes * 4,
        compiler_params=pltpu.CompilerParams(vmem_limit_bytes=48 * 2**20),
        name="sum_adamw_mats",
    )(*recvs, *ws, *ms, *vs)
    return res[:n], res[n:2 * n], res[2 * n:3 * n], res[3 * n:]


SMALL = ("g_pre", "b_gate", "g_q", "g_kv", "lb_logits", "g_hgrn", "g_post")
SMALL_SHAPE = dict(g_pre=(1, 1024), b_gate=(1, 2048), g_q=(1, 768), g_kv=(1, 256), lb_logits=(2, 512),
                   g_hgrn=(1, 64), g_post=(1, 1024))


def _vectors_sum(dg_pre, db_gate, dg_q, dg_kv, dlbl, dgh, dg_post, loss_vec):
    def body(gpre_ref, bg_ref, gq_ref, gkv_ref, lbl_ref, gh_ref, gpost_ref, loss_ref, out_ref, mine, got,
             send_sems, recv_sems):
        mine[...] = jnp.zeros_like(mine)
        mine[0:1, :] = gpre_ref[...]
        mine[1:2, :] = bg_ref[:, :1024]
        mine[2:3, :] = bg_ref[:, 1024:]
        mine[3:4, :Q_LORA] = gq_ref[...]
        mine[4:5, :KV_LORA] = gkv_ref[...]
        loss = (0.5 / D_MODEL) * jnp.sum(loss_ref[...], axis=-1, keepdims=True)
        mine[4:5, KV_LORA:] = jnp.broadcast_to(loss, (1, 1024 - KV_LORA))
        mine[5:6, :HG_WIDTH] = lbl_ref[0:1, :]
        mine[5:6, HG_WIDTH:] = lbl_ref[1:2, :]
        gh = gh_ref[...]
        fold = gh[:, :VDIM]
        for h in range(1, HEADS):
            fold = fold + gh[:, VDIM * h:VDIM * (h + 1)]
        mine[6:7, :VDIM] = fold
        mine[7:8, :] = gpost_ref[...]
        x, y, c = _my_place()
        me = 4 * x + 2 * y + c
        got[me] = mine[...]
        copies = [pltpu.make_async_remote_copy(
            src_ref=mine, dst_ref=got.at[me], send_sem=send_sems.at[k], recv_sem=recv_sems.at[k],
            device_id=_flip(k, x, y, c), device_id_type=MESH_ID) for k in range(N_DEV - 1)]
        _start_all([], copies)
        _wait_all([], copies)
        out_ref[...] = _sum8(got)

    return pl.pallas_call(
        body,
        out_shape=jax.ShapeDtypeStruct((8, 1024), F32),
        scratch_shapes=[pltpu.VMEM((8, 1024), F32), pltpu.VMEM((N_DEV, 8, 1024), F32),
                        pltpu.SemaphoreType.DMA((7,)), pltpu.SemaphoreType.DMA((7,))],
        name="vectors_sum",
    )(dg_pre, db_gate, dg_q, dg_kv, dlbl, dgh, dg_post, loss_vec)


def _vectors_adamw(g_sum, ws, ms, vs):
    n = len(SMALL)

    def body(g_ref, *refs):
        w_refs, m_refs, v_refs = refs[:n], refs[n:2 * n], refs[2 * n:3 * n]
        loss_ref, outs = refs[3 * n], refs[3 * n + 1:]
        g = g_ref[...]
        loss_ref[...] = g[4:5, KV_LORA:KV_LORA + 1]
        grads = (g[0:1, :], jnp.concatenate([g[1:2, :], g[2:3, :]], axis=1), g[3:4, :Q_LORA], g[4:5, :KV_LORA],
                 jnp.concatenate([g[5:6, :HG_WIDTH], g[5:6, HG_WIDTH:]], axis=0), g[6:7, :VDIM], g[7:8, :])
        for a in range(n):
            d, nm, nv = _adamw(grads[a], w_refs[a][...], m_refs[a][...], v_refs[a][...])
            outs[a][...] = grads[a]
            outs[n + a][...] = d
            outs[2 * n + a][...] = nm
            outs[3 * n + a][...] = nv

    shapes = [jax.ShapeDtypeStruct(SMALL_SHAPE[k], F32) for k in SMALL]
    res = pl.pallas_call(
        body,
        out_shape=[jax.ShapeDtypeStruct((1, 1), F32)] + shapes * 4,
        name="vectors_adamw",
    )(g_sum, *ws, *ms, *vs)
    return res[0], res[1:n + 1], res[n + 1:2 * n + 1], res[2 * n + 1:3 * n + 1], res[3 * n + 1:]


MATS = ("w_uq", "w_ukv", "w_branch_a", "w_branch_b", "w_out")
COL_SHARDED = dict(w_uq=False, w_ukv=True, w_branch_a=True, w_branch_b=True, w_out=False)
ORDER = ("g_pre", "w_in", "b_gate", "g_q", "w_uq", "g_kv", "w_ukv", "lb_logits", "g_hgrn",
         "w_branch_a", "w_branch_b", "w_out", "g_post")


def _from_slots(name, slots):
    _, r, c = slots.shape
    if COL_SHARDED[name]:
        return slots.transpose(1, 0, 2).reshape(r, N_DEV * c)
    return slots.reshape(N_DEV * r, c)


def _shards_bf16(w_in3, mats):
    n = len(mats)

    def body(win_hbm, *rest):
        m_refs, blk_ref, o_refs, (buf, sem) = rest[:n], rest[n], rest[n + 1:2 * n + 1], rest[2 * n + 1:]
        load = pltpu.make_async_copy(win_hbm.at[:, 0, pl.ds(0, D_MODEL)], buf, sem)
        load.start()
        for m_ref, o_ref in zip(m_refs, o_refs):
            o_ref[...] = m_ref[...].astype(BF16)
        load.wait()
        blk_ref[...] = buf[...].astype(BF16)

    vm = pl.BlockSpec(memory_space=pltpu.VMEM)
    blk, *out = pl.pallas_call(
        body,
        in_specs=[ANY] + [vm] * n,
        out_specs=[vm] * (n + 1),
        out_shape=[jax.ShapeDtypeStruct((W_IN_SHARD, D_MODEL), BF16)]
        + [jax.ShapeDtypeStruct(a.shape, BF16) for a in mats],
        scratch_shapes=[pltpu.VMEM((W_IN_SHARD, D_MODEL), F32), pltpu.SemaphoreType.DMA],
        name="shards_bf16",
    )(w_in3, *mats)
    return blk, out


def kernel(x, g_pre, w_in, b_gate, g_q, w_uq, g_kv, w_ukv, lb_logits, g_hgrn, w_branch_a, w_branch_b, w_out, g_post, loss_target, m_g_pre, m_w_in, m_b_gate, m_g_q, m_w_uq, m_g_kv, m_w_ukv, m_lb_logits, m_g_hgrn, m_w_branch_a, m_w_branch_b, m_w_out, m_g_post, v_g_pre, v_w_in, v_b_gate, v_g_q, v_w_uq, v_g_kv, v_w_ukv, v_lb_logits, v_g_hgrn, v_w_branch_a, v_w_branch_b, v_w_out, v_g_post):
    rows3 = lambda a: jnp.transpose(a, (2, 0, 1))
    w = dict(w_in=rows3(w_in), w_uq=w_uq[0], w_ukv=w_ukv[0], w_branch_a=w_branch_a[0], w_branch_b=w_branch_b[0],
             w_out=w_out[0], g_pre=g_pre, b_gate=b_gate, g_q=g_q, g_kv=g_kv, lb_logits=lb_logits, g_hgrn=g_hgrn,
             g_post=g_post)
    mom = dict(w_in=rows3(m_w_in), w_uq=m_w_uq[0], w_ukv=m_w_ukv[0], w_branch_a=m_w_branch_a[0],
               w_branch_b=m_w_branch_b[0], w_out=m_w_out[0], g_pre=m_g_pre, b_gate=m_b_gate, g_q=m_g_q, g_kv=m_g_kv,
               lb_logits=m_lb_logits, g_hgrn=m_g_hgrn, g_post=m_g_post)
    var = dict(w_in=rows3(v_w_in), w_uq=v_w_uq[0], w_ukv=v_w_ukv[0], w_branch_a=v_w_branch_a[0],
               w_branch_b=v_w_branch_b[0], w_out=v_w_out[0], g_pre=v_g_pre, b_gate=v_b_gate, g_q=v_g_q, g_kv=v_g_kv,
               lb_logits=v_lb_logits, g_hgrn=v_g_hgrn, g_post=v_g_post)

    w_blk, shards = _shards_bf16(w["w_in"], [w[n] for n in MATS])
    dx, recv_in, recv, g_sum = _step(x[0], loss_target[0], w_blk, shards,
                                     g_pre, b_gate, g_q, g_kv, lb_logits, g_hgrn, g_post)

    g_in, d_in, m_in, v_in = _sum_adamw_w_in(recv_in, w["w_in"], mom["w_in"], var["w_in"])
    res = _sum_adamw_whole([recv[n] for n in MATS], *([t[n] for n in MATS] for t in (w, mom, var)))
    total, *vec = _vectors_adamw(g_sum, *([t[n] for n in SMALL] for t in (w, mom, var)))

    outs = []
    for mats, vecs, big in zip(res, vec, (g_in, d_in, m_in, v_in)):
        t = {**{n: a[None] for n, a in zip(MATS, mats)}, **dict(zip(SMALL, vecs)),
             "w_in": jnp.transpose(big, (1, 2, 0))}
        outs += [t[n] for n in ORDER]
    return (total.reshape(()), dx[None], *outs)
```

```python
import math

import jax
import jax.numpy as jnp
import numpy as np
from jax import lax
from jax.experimental import pallas as pl
from jax.experimental.pallas import tpu as pltpu

F32, BF16 = jnp.float32, jnp.bfloat16

D_MODEL = 1024
EPS = 1e-6
HEADS = 8
NOPE, ROPE, VDIM = 64, 32, 64
QK = NOPE + ROPE
Q_LORA, KV_LORA = 768, 256
ROPE_THETA = 10000.0
ATT_CHUNK_SHIFT = 6
HG_BLOCK = 32
HG_WIDTH = 512
D_IN = 5664
D_IN_PAD = 5760
W_IN_SHARD = D_IN // 8
N_DEV = 8
LANE = 128

ADAM_LR, ADAM_B1, ADAM_B2, ADAM_EPS, ADAM_WD, ADAM_STEP = 0.001, 0.9, 0.999, 1e-08, 0.01, 10

W_IN_SEGMENTS = ((3616, 5664, 0), (1056, 1568, 2048), (3104, 3616, 2560), (1568, 3104, 3072),
                 (0, 1024, 4608), (1024, 1056, 5696))

NT = (((1,), (1,)), ((), ()))
TN = (((0,), (0,)), ((), ()))
MESH_ID = pl.DeviceIdType.MESH


def _w_in_pieces():
    out = []
    for lo, hi, dst in W_IN_SEGMENTS:
        c = lo
        while c < hi:
            p = c // W_IN_SHARD
            e = min(hi, (p + 1) * W_IN_SHARD)
            out.append((p, c - p * W_IN_SHARD, e - p * W_IN_SHARD, dst + c - lo))
            c = e
    return out


def _params(sem, vmem_mb=48):
    return pltpu.CompilerParams(dimension_semantics=sem, vmem_limit_bytes=vmem_mb * 2**20)


def _dot(a, b):
    return jnp.dot(a, b, preferred_element_type=F32)


def _dotg(a, b, dims):
    return lax.dot_general(a, b, dims, preferred_element_type=F32)


def _split2(x):
    hi = x.astype(BF16)
    return hi, (x - hi.astype(F32)).astype(BF16)


def _sel_left(m01, x):
    hi, lo = _split2(x)
    return _dot(m01, hi) + _dot(m01, lo)


def _sel_right(x, m01):
    hi, lo = _split2(x)
    return _dot(hi, m01) + _dot(lo, m01)


def _hi_lo(x):
    hi = x.astype(BF16).astype(F32)
    return hi, x - hi


def _sigmoid(x):
    return 0.5 * jnp.tanh(0.5 * x) + 0.5


def _rope(x, c, s1, s2):
    return x * c + pltpu.roll(x, 112, 1) * s1 + pltpu.roll(x, 16, 1) * s2


def _unrope(d, c, s1, s2):
    return d * c + pltpu.roll(d * s1, 16, 1) + pltpu.roll(d * s2, 112, 1)


def _my_place():
    return lax.axis_index("x"), lax.axis_index("y"), lax.axis_index("c")


def _flip(k, x, y, c):
    fx, fy, fc = (k + 1) >> 2 & 1, (k + 1) >> 1 & 1, (k + 1) & 1
    return (1 - x if fx else x), (1 - y if fy else y), (1 - c if fc else c)


def _to_all_copies(s_refs, r_refs, sems, spread):
    send_sems, recv_sems, local_sems = sems
    x, y, c = _my_place()
    me = 4 * x + 2 * y + c
    src = (lambda a, p: s_refs[a]) if spread else (lambda a, p: s_refs[a].at[p])
    local = [pltpu.make_async_copy(src(a, me), r_refs[a].at[me], local_sems.at[a]) for a in range(len(s_refs))]
    remote = []
    for k in range(N_DEV - 1):
        px, py, pc = _flip(k, x, y, c)
        for a in range(len(s_refs)):
            remote.append(pltpu.make_async_remote_copy(
                src_ref=src(a, 4 * px + 2 * py + pc), dst_ref=r_refs[a].at[me],
                send_sem=send_sems.at[7 * a + k], recv_sem=recv_sems.at[7 * a + k],
                device_id=(px, py, pc), device_id_type=MESH_ID))
    return local, remote


def _to_chips_copies(s_refs, r_refs, sems):
    send_sems, recv_sems, local_sems = sems
    x, y, c = _my_place()
    me = 2 * x + y
    local = [pltpu.make_async_copy(s_refs[a].at[me], r_refs[a].at[me], local_sems.at[a]) for a in range(len(s_refs))]
    remote = []
    for k in range(3):
        px = 1 - x if (k + 1) >> 1 & 1 else x
        py = 1 - y if (k + 1) & 1 else y
        for a in range(len(s_refs)):
            remote.append(pltpu.make_async_remote_copy(
                src_ref=s_refs[a].at[2 * px + py], dst_ref=r_refs[a].at[me],
                send_sem=send_sems.at[3 * a + k], recv_sem=recv_sems.at[3 * a + k],
                device_id=(px, py, c), device_id_type=MESH_ID))
    return local, remote


def _start_all(local, remote):
    for cp in local + remote:
        cp.start()


def _wait_all(local, remote):
    for cp in remote:
        cp.wait_recv()
    for cp in remote:
        cp.wait_send()
    for cp in local:
        cp.wait()


def _copy_sems(n, peers):
    return [pltpu.SemaphoreType.DMA((peers * n,)), pltpu.SemaphoreType.DMA((peers * n,)),
            pltpu.SemaphoreType.DMA((n,))]


ANY = pl.BlockSpec(memory_space=pl.ANY)


def _dw_in_slots(ht, dproj):
    m, k = ht.shape
    n = dproj.shape[1]
    tn, tk = 1152, 1024
    nj, nk = n // tn, k // tk
    by_tile = [[] for _ in range(nj)]
    for p, lo, hi, dst in _w_in_pieces():
        while lo < hi:
            j = dst // tn
            cnt = min(hi - lo, (j + 1) * tn - dst)
            by_tile[j].append((p, lo, lo + cnt, dst - j * tn))
            lo, dst = lo + cnt, dst + cnt

    def body(a_ref, b_ref, s_ref, acc_ref):
        j, l = pl.program_id(0), pl.program_id(1)

        @pl.when(l == 0)
        def _():
            acc_ref[...] = jnp.zeros_like(acc_ref)

        acc_ref[...] += _dot(a_ref[...], b_ref[...])

        @pl.when(l == nk - 1)
        def _():
            at = acc_ref[...].T
            for jj in range(nj):
                @pl.when(j == jj)
                def _(jj=jj):
                    for p, lo, hi, d in by_tile[jj]:
                        s_ref[p, lo:hi, :] = at[d:d + hi - lo, :].astype(BF16)

    return pl.pallas_call(
        body,
        grid=(nj, nk),
        in_specs=[pl.BlockSpec((m, tk), lambda j, l: (0, l)), pl.BlockSpec((tk, tn), lambda j, l: (l, j))],
        out_specs=pl.BlockSpec((N_DEV, W_IN_SHARD, m), lambda j, l: (0, 0, 0)),
        out_shape=jax.ShapeDtypeStruct((N_DEV, W_IN_SHARD, m), BF16),
        scratch_shapes=[pltpu.VMEM((m, tn), F32)],
        compiler_params=_params(("arbitrary", "arbitrary")),
        name="dw_in",
    )(ht, dproj)


PROJ_DT = F32
GP_TN = 256
GP_COLS = 5888
GP_NT = GP_COLS // GP_TN


def _gp_tile_pieces():
    tiles = [[] for _ in range(GP_NT)]
    for p, lo, hi, dst in _w_in_pieces():
        while lo < hi:
            t = dst // GP_TN
            n = min(hi - lo, (t + 1) * GP_TN - dst)
            tiles[t].append((p, lo, lo + n, dst - t * GP_TN))
            lo, dst = lo + n, dst + n
    return tiles


def _gp_tables():
    pieces = _gp_tile_pieces()
    rank_of = {None: 0, 0: 1, 1: 2, 2: 2, 4: 3, 5: 3, 3: 4, 6: 5}
    order = np.zeros((N_DEV, GP_NT), np.int32)
    waits = np.zeros((N_DEV, GP_NT), np.int32)
    for me in range(N_DEV):
        x, y, c = me >> 2 & 1, me >> 1 & 1, me & 1
        chips = [(1 - x, y), (x, 1 - y), (1 - x, 1 - y)]

        def sem_of(p):
            px, py, pc = p >> 2 & 1, p >> 1 & 1, p & 1
            if (px, py) == (x, y):
                return None if pc == c else 0
            j = chips.index((px, py))
            return 1 + j if pc == c else 4 + j

        needs = [sorted({sem_of(p) for p, _, _, _ in tile} - {None}) for tile in pieces]
        ranks = [max([rank_of[k] for k in ks], default=0) for ks in needs]
        seq = sorted(range(GP_NT), key=lambda t: (ranks[t], t))
        seen = set()
        for step, t in enumerate(seq):
            order[me, step] = t
            new = [k for k in needs[t] if k not in seen]
            for k in new:
                waits[me, step] |= 1 << k
            seen.update(new)
        assert seen == set(range(7)), (me, seen)
    return order, waits


def _gather_proj(x, g_pre, w_blk, shards):
    s = x.shape[0]
    tx = 512
    ns = len(shards)
    tile_pieces = _gp_tile_pieces()
    order_np, waits_np = _gp_tables()
    xq, yq, cq = _my_place()
    me_out = 4 * xq + 2 * yq + cq
    order = lax.dynamic_index_in_dim(jnp.asarray(order_np), me_out, 0, keepdims=False)
    waits = lax.dynamic_index_in_dim(jnp.asarray(waits_np), me_out, 0, keepdims=False)

    def body(order_ref, waits_ref, x_hbm, g_ref, wblk_hbm, *rest):
        shard_refs, (proj_ref, wt_ref, ht_hbm), got_refs = rest[:ns], rest[ns:ns + 3], rest[ns + 3:2 * ns + 3]
        recv, h_ref, wtile, xbuf, htbuf = rest[2 * ns + 3:2 * ns + 8]
        send_sems, recv_sems, misc_sems = rest[2 * ns + 8:2 * ns + 11]
        sems = rest[2 * ns + 11:]
        t = pl.program_id(0)
        x_, y_, c = _my_place()
        sibling = (x_, y_, 1 - c)
        chips = [(1 - x_, y_), (x_, 1 - y_), (1 - x_, 1 - y_)]
        idx = lambda px, py, pc: 4 * px + 2 * py + pc
        me = idx(x_, y_, c)

        def copy(k, slot, to, src=None):
            return pltpu.make_async_remote_copy(
                src_ref=recv.at[slot] if src is None else src, dst_ref=recv.at[slot],
                send_sem=send_sems.at[k], recv_sem=recv_sems.at[k], device_id=to, device_id_type=MESH_ID)

        mine = pltpu.make_async_copy(wblk_hbm, recv.at[me], misc_sems.at[0])
        first = [copy(0, me, sibling, src=wblk_hbm)] + [copy(1 + j, me, (*chips[j], c), src=wblk_hbm) for j in range(2)]
        passed = [copy(4 + j, idx(*ch, c), sibling) for j, ch in enumerate(chips)]
        onward = [copy(3, idx(*chips[0], c), (*chips[1], c)), copy(3, idx(*chips[1], c), (*chips[0], c))]
        arrivals = ([copy(0, idx(x_, y_, 1 - c), sibling)] + [copy(1 + j, idx(*ch, c), sibling) for j, ch in enumerate(chips)]
                    + [copy(4 + j, idx(*ch, 1 - c), sibling) for j, ch in enumerate(chips)])

        @pl.when(t == 0)
        def _():
            mine.start()
            for cp in first:
                cp.start()
            _start_all(*_to_all_copies(shard_refs, got_refs, sems, True))

            def load(i):
                return pltpu.make_async_copy(x_hbm.at[pl.ds(i * tx, tx), :], xbuf.at[i & 1], misc_sems.at[1 + (i & 1)])

            def store(i):
                return pltpu.make_async_copy(htbuf.at[i & 1], ht_hbm.at[:, pl.ds(i * tx, tx)], misc_sems.at[3 + (i & 1)])

            load(0).start()
            for i in range(s // tx):
                if i + 1 < s // tx:
                    load(i + 1).start()
                load(i).wait()
                xv = xbuf[i & 1]
                r = lax.rsqrt(jnp.mean(xv * xv, axis=-1, keepdims=True) + EPS)
                h = (xv * r * g_ref[...]).astype(BF16)
                h_ref[i * tx:(i + 1) * tx, :] = h
                if i >= 2:
                    store(i - 2).wait()
                htbuf[i & 1] = h.T
                store(i).start()
            for i in range(max(s // tx - 2, 0), s // tx):
                store(i).wait()
            mine.wait()

        w = waits_ref[t]
        for k in range(7):
            @pl.when((w >> k) & 1 == 1)
            def _(k=k):
                arrivals[k].wait_recv()
                if 1 <= k <= 3:
                    passed[k - 1].start()
                if 1 <= k <= 2:
                    @pl.when(c == k - 1)
                    def _():
                        onward[k - 1].start()

        tile = order_ref[t]
        for tt in range(GP_NT):
            @pl.when(tile == tt)
            def _(tt=tt):
                covered = sorted((d, d + hi - lo) for _, lo, hi, d in tile_pieces[tt])
                at = 0
                for lo_z, hi_z in covered + [(GP_TN, GP_TN)]:
                    if lo_z > at:
                        wtile[at:lo_z, :] = jnp.zeros((lo_z - at, D_MODEL), BF16)
                    at = max(at, hi_z)
                for p, lo, hi, d in tile_pieces[tt]:
                    wtile[d:d + hi - lo, :] = recv[p, lo:hi, :]

        wt = wtile[...]
        wt_ref[...] = wt
        proj_ref[...] = _dotg(h_ref[...], wt, NT).astype(PROJ_DT)

        @pl.when(t == GP_NT - 1)
        def _():
            for cp in first + passed + onward[:1]:
                cp.wait_send()
            _wait_all(*_to_all_copies(shard_refs, got_refs, sems, True))

    grid_spec = pltpu.PrefetchScalarGridSpec(
        num_scalar_prefetch=2,
        grid=(GP_NT,),
        in_specs=[ANY, pl.BlockSpec((1, D_MODEL), lambda t, o, w: (0, 0)), ANY] + [ANY] * ns,
        out_specs=[pl.BlockSpec((s, GP_TN), lambda t, o, w: (0, o[t])),
                   pl.BlockSpec((GP_TN, D_MODEL), lambda t, o, w: (o[t], 0)), ANY] + [ANY] * ns,
        scratch_shapes=[pltpu.VMEM((N_DEV, W_IN_SHARD, D_MODEL), BF16), pltpu.VMEM((s, D_MODEL), BF16),
                        pltpu.VMEM((GP_TN, D_MODEL), BF16), pltpu.VMEM((2, tx, D_MODEL), F32),
                        pltpu.VMEM((2, D_MODEL, tx), BF16),
                        pltpu.SemaphoreType.DMA((7,)), pltpu.SemaphoreType.DMA((7,)), pltpu.SemaphoreType.DMA((5,))]
        + _copy_sems(ns, 7),
    )
    return pl.pallas_call(
        body,
        grid_spec=grid_spec,
        out_shape=[jax.ShapeDtypeStruct((s, GP_COLS), PROJ_DT), jax.ShapeDtypeStruct((GP_COLS, D_MODEL), BF16),
                   jax.ShapeDtypeStruct((D_MODEL, s), BF16)]
        + [jax.ShapeDtypeStruct((N_DEV,) + b.shape, b.dtype) for b in shards],
        compiler_params=_params(("arbitrary",), 56),
        name="gather_proj",
    )(order, waits, x, g_pre, w_blk, *shards)


def _mla_prep(proj, g_q, g_kv, w_uq_p, w_kv_p, rc, rs1, rs2):
    s = proj.shape[0]
    tm = 256
    scale = 1.0 / math.sqrt(QK)

    def body(cq_ref, ckv_ref, kpe_ref, gq_ref, gkv_ref, wuq_ref, wkv_ref, c_ref, s1_ref, s2_ref,
             qr_ref, kr_ref, v_ref, cqt_ref, ckvt_ref):
        cq = cq_ref[...].astype(F32)
        r = lax.rsqrt(jnp.mean(cq * cq, axis=-1, keepdims=True) + EPS)
        cqn = (cq * r * gq_ref[...]).astype(BF16)
        cqt_ref[...] = cqn.T
        q = _dot(cqn, wuq_ref[...])
        ckv = ckv_ref[...].astype(F32)
        r = lax.rsqrt(jnp.mean(ckv * ckv, axis=-1, keepdims=True) + EPS)
        ckvn = (ckv * r * gkv_ref[...]).astype(BF16)
        ckvt_ref[...] = ckvn.T
        kv = _dot(ckvn, wkv_ref[...])
        c, s1, s2 = c_ref[...], s1_ref[...], s2_ref[...]
        lane = lax.broadcasted_iota(jnp.int32, (tm, LANE), 1)
        kpe = _rope(kpe_ref[...].astype(F32), c, s1, s2) + jnp.where((lane == QK) | (lane == QK + 1), 1.0, 0.0)
        vone = jnp.where((lane == VDIM) | (lane == VDIM + 1), 1.0, 0.0)
        for h in range(HEADS):
            sl = slice(LANE * h, LANE * (h + 1))
            qr_ref[:, sl] = (_rope(q[:, sl], c, s1, s2) * scale).astype(BF16)
            kr_ref[:, sl] = (kv[:, sl] + kpe).astype(BF16)
            v_ref[:, sl] = (kv[:, HEADS * LANE + LANE * h:HEADS * LANE + LANE * (h + 1)] + vone).astype(BF16)

    row = lambda w, j: pl.BlockSpec((tm, w), lambda i: (i, j))
    col = lambda w: pl.BlockSpec((w, tm), lambda i: (0, i))
    full = lambda a: pl.BlockSpec(a.shape, lambda i: (0, 0))
    return pl.pallas_call(
        body,
        grid=(s // tm,),
        in_specs=[row(768, 6), row(256, 21), row(128, 44), full(g_q), full(g_kv), full(w_uq_p), full(w_kv_p),
                  row(128, 0), row(128, 0), row(128, 0)],
        out_specs=[row(1024, 0), row(1024, 0), row(1024, 0), col(768), col(256)],
        out_shape=[jax.ShapeDtypeStruct((s, 1024), BF16), jax.ShapeDtypeStruct((s, 1024), BF16),
                   jax.ShapeDtypeStruct((s, 1024), BF16), jax.ShapeDtypeStruct((768, s), BF16),
                   jax.ShapeDtypeStruct((256, s), BF16)],
        compiler_params=_params(("arbitrary",)),
        name="mla_prep",
    )(proj, proj, proj, g_q, g_kv, w_uq_p, w_kv_p, rc, rs1, rs2)


ATT_T = 512
ATT_FWD_HEADS = 4


def _chunk_mask(transposed):
    r = lax.broadcasted_iota(jnp.int32, (ATT_T, ATT_T), 0) >> ATT_CHUNK_SHIFT
    c = lax.broadcasted_iota(jnp.int32, (ATT_T, ATT_T), 1) >> ATT_CHUNK_SHIFT
    return (r <= c) if transposed else (c <= r)


def _attn_fwd(qr, kr, vp, shards):
    s = qr.shape[0]
    t = ATT_T
    g = ATT_FWD_HEADS
    ns = len(shards)

    def body(q_ref, k_ref, v_ref, *rest):
        shard_refs, (o_ref, qa_ref), got_refs = rest[:ns], rest[ns:ns + 2], rest[ns + 2:2 * ns + 2]
        sc_ref, sems = rest[2 * ns + 2], rest[2 * ns + 3:]
        qi = pl.program_id(1)

        @pl.when((pl.program_id(0) == 0) & (qi == 0))
        def _():
            _start_all(*_to_all_copies(shard_refs, got_refs, sems, True))
        lane = lax.broadcasted_iota(jnp.int32, (t, LANE), 1)
        sls = [slice(LANE * a, LANE * (a + 1)) for a in range(g)]
        qs = [q_ref[:, sl] for sl in sls]

        def scores(j):
            rows = pl.ds(pl.multiple_of(j * t, t), t)
            for a in range(g):
                sc_ref[j & 1, a] = _dotg(qs[a], k_ref[rows, sls[a]], NT)

        def step(j, carry, masked):
            rows = pl.ds(pl.multiple_of(j * t, t), t)
            out = []
            for a in range(g):
                m, acc = carry[a]
                sc = sc_ref[j & 1, a]
                if masked:
                    sc = jnp.where(_chunk_mask(False), sc, -1e30)
                m_new = jnp.maximum(m, jnp.max(sc, axis=-1, keepdims=True))
                p = jnp.exp(sc - m_new).astype(BF16)
                acc = jnp.exp(m - m_new) * acc + _dot(p, v_ref[rows, sls[a]])
                out.append((m_new, acc))
            return tuple(out)

        def loop(j, carry):
            carry = step(j, carry, False)
            scores(j + 1)
            return carry

        init = tuple((jnp.full((t, 1), -1e30, F32), jnp.zeros((t, LANE), F32)) for _ in range(g))
        scores(0)
        carry = lax.fori_loop(0, qi, loop, init)
        carry = step(qi, carry, True)
        outs = []
        for a in range(g):
            m, acc = carry[a]
            l = acc[:, VDIM:VDIM + 1]
            outs.append(acc / l)
            hi, lo_part = _hi_lo(-(m + jnp.log(l)))
            qa = jnp.where(lane == QK, hi, jnp.where(lane == QK + 1, lo_part, qs[a].astype(F32)))
            qa_ref[:, sls[a]] = qa.astype(BF16)
        for p in range(g // 2):
            o_ref[:, LANE * p:LANE * (p + 1)] = jnp.where(lane < VDIM, outs[2 * p], pltpu.roll(outs[2 * p + 1], VDIM, 1))

        @pl.when((pl.program_id(0) == HEADS // g - 1) & (qi == s // t - 1))
        def _():
            _wait_all(*_to_all_copies(shard_refs, got_refs, sems, True))

    return pl.pallas_call(
        body,
        grid=(HEADS // g, s // t),
        in_specs=[
            pl.BlockSpec((t, g * LANE), lambda h, i: (i, h)),
            pl.BlockSpec((s, g * LANE), lambda h, i: (0, h)),
            pl.BlockSpec((s, g * LANE), lambda h, i: (0, h)),
        ] + [ANY] * ns,
        out_specs=[
            pl.BlockSpec((t, g * VDIM), lambda h, i: (i, h)),
            pl.BlockSpec((t, g * LANE), lambda h, i: (i, h)),
        ] + [ANY] * ns,
        out_shape=[jax.ShapeDtypeStruct((s, 512), F32), jax.ShapeDtypeStruct((s, 1024), BF16)]
        + [jax.ShapeDtypeStruct((N_DEV,) + b.shape, b.dtype) for b in shards],
        scratch_shapes=[pltpu.VMEM((2, g, t, t), F32)] + _copy_sems(ns, 7),
        compiler_params=_params(("arbitrary", "arbitrary")),
        name="attn_fwd",
    )(qr, kr, vp, *shards)


def _attn_bwd(qa, kr, vp, dop, sends):
    s = qa.shape[0]
    t = ATT_T
    nq = s // t
    ns = len(sends)

    def body(q_ref, k_ref, v_ref, do_ref, *rest):
        send_refs, (dq_out, dk_out, dv_out) = rest[:ns], rest[ns:ns + 3]
        recv_refs = rest[ns + 3:2 * ns + 3]
        (dq_ref, dk_ref, dv_ref), sems = rest[2 * ns + 3:2 * ns + 6], rest[2 * ns + 6:]
        j = pl.program_id(1)
        sls = [slice(LANE * a, LANE * (a + 1)) for a in range(2)]

        @pl.when((pl.program_id(0) == 0) & (j == 0))
        def _():
            _start_all(*_to_all_copies(send_refs, recv_refs, sems, False))

        @pl.when(j == 0)
        def _():
            dq_ref[...] = jnp.zeros_like(dq_ref)

        dk_ref[...] = jnp.zeros_like(dk_ref)
        dv_ref[...] = jnp.zeros_like(dv_ref)
        ks = [k_ref[:, sl] for sl in sls]
        vs = [v_ref[:, sl] for sl in sls]

        def part(i, k_lo, k_n, q_lo, q_n, masked):
            rows = pl.ds(pl.multiple_of(i * t + q_lo, 256), q_n)
            keys = slice(k_lo, k_lo + k_n)
            for a in range(2):
                q = q_ref[rows, sls[a]]
                do = do_ref[rows, sls[a]]
                sc = _dotg(ks[a][keys], q, NT)
                if masked:
                    kc = lax.broadcasted_iota(jnp.int32, (k_n, q_n), 0) >> ATT_CHUNK_SHIFT
                    qc = lax.broadcasted_iota(jnp.int32, (k_n, q_n), 1) >> ATT_CHUNK_SHIFT
                    sc = jnp.where(kc <= qc, sc, -1e30)
                p = jnp.exp(sc)
                ds = (p * _dotg(vs[a][keys], do, NT)).astype(BF16)
                dv_ref[keys, sls[a]] += _dot(p.astype(BF16), do)
                dk_ref[keys, sls[a]] += _dot(ds, q)
                dq_ref[rows, sls[a]] += _dotg(ds, ks[a][keys], TN)

        half = t // 2
        part(j, 0, half, 0, t, True)
        part(j, half, half, half, half, True)

        def loop(i, c):
            part(i, 0, t, 0, t, False)
            return c

        lax.fori_loop(j + 1, nq, loop, 0)
        dk_out[...] = dk_ref[...].astype(BF16)
        dv_out[...] = dv_ref[...].astype(BF16)

        @pl.when(j == nq - 1)
        def _():
            dq_out[...] = dq_ref[...].astype(BF16)

        @pl.when((pl.program_id(0) == HEADS // 2 - 1) & (j == nq - 1))
        def _():
            _wait_all(*_to_all_copies(send_refs, recv_refs, sems, False))

    blk = pl.BlockSpec((t, 2 * LANE), lambda h, j: (j, h))
    whole = pl.BlockSpec((s, 2 * LANE), lambda h, j: (0, h))
    out = jax.ShapeDtypeStruct((s, 1024), BF16)
    return pl.pallas_call(
        body,
        grid=(HEADS // 2, nq),
        in_specs=[whole, blk, blk, whole] + [ANY] * ns,
        out_specs=[whole, blk, blk] + [ANY] * ns,
        out_shape=[out, out, out] + [jax.ShapeDtypeStruct(a.shape, a.dtype) for a in sends],
        scratch_shapes=[pltpu.VMEM((s, 2 * LANE), F32), pltpu.VMEM((t, 2 * LANE), F32),
                        pltpu.VMEM((t, 2 * LANE), F32)] + _copy_sems(ns, 7),
        compiler_params=_params(("arbitrary", "arbitrary")),
        name="attn_bwd",
    )(qa, kr, vp, dop, *sends)


HG_T = 256
HG_NC = HG_T // HG_BLOCK
HG_G = 4
GW = 64 * HG_G


def _hg_consts():
    r = jnp.arange(HG_T)[:, None]
    c = jnp.arange(HG_T)[None, :]
    same = (r // HG_BLOCK) == (c // HG_BLOCK)
    mcum = (same & (c <= r)).astype(BF16)
    mrev = (same & (c >= r)).astype(BF16)
    msum = same.astype(BF16)
    a = jnp.arange(GW) // 64
    bd = (a[:, None] == a[None, :]).astype(F32)
    return mcum, mrev, msum, bd


def _stack_heads(xg, head):
    return jnp.concatenate([jnp.where(head == h, xg, 0.0) for h in range(HG_G)], axis=0)


def _unstack_heads(r, head, t):
    out = r[(HG_G - 1) * t:]
    for h in range(HG_G - 2, -1, -1):
        out = jnp.where(head == h, r[h * t:(h + 1) * t], out)
    return out


def _compact_state(st):
    out = st[:64]
    for h in range(1, HG_G):
        out = out + st[64 * h:64 * (h + 1)]
    return out


def _expand_state(cs, head64):
    return jnp.concatenate([jnp.where(head64 == h, cs, 0.0) for h in range(HG_G)], axis=0)


def _hg_pre(hq, hf, lbl, mcum, msum):
    lb = _sigmoid(lbl[0:1, :] - lbl[1:2, :])
    sig = _sigmoid(hf)
    f = lb + (1.0 - lb) * sig
    lf = jnp.log(f)
    b = _sel_left(mcum, lf)
    big_l = _sel_left(msum, lf)
    k = 1.0 - f
    qd = hq * jnp.exp(b)
    ki = k * jnp.exp(-b)
    ke = k * jnp.exp(big_l - b)
    return lb, sig, f, b, big_l, qd, ki, ke


def _hgrn_fwd(proj, lbl):
    s = proj.shape[0]
    t = HG_T
    mcum, _, msum, bd = _hg_consts()

    def body(hq_ref, hf_ref, hi_ref, lbl_ref, mcum_ref, msum_ref, bd_ref, o_ref, sp_ref, st_ref):
        @pl.when(pl.program_id(0) == 0)
        def _():
            st_ref[...] = jnp.zeros_like(st_ref)

        mc = mcum_ref[...]
        _, _, _, _, big_l, qd, ki, ke = _hg_pre(hq_ref[...].astype(F32), hf_ref[...].astype(F32), lbl_ref[...], mc,
                                                msum_ref[...])
        el = jnp.exp(big_l)
        hi = hi_ref[...]
        head = lax.broadcasted_iota(jnp.int32, (t, GW), 1) >> 6
        mask = jnp.concatenate([mc] * HG_G, axis=0) > 0.5
        for p in range(HEADS // HG_G):
            sl = slice(GW * p, GW * (p + 1))
            vp = hi[:, sl].astype(BF16)
            qs = _stack_heads(qd[:, sl], head).astype(BF16)
            a = jnp.where(mask, _dotg(qs, ki[:, sl].astype(BF16), NT), 0.0)
            o_intra = _unstack_heads(_dot(a.astype(BF16), vp), head, t)
            qb = qd[:, sl].astype(BF16)
            kb = ke[:, sl].astype(BF16)
            st = st_ref[p]
            for c in range(HG_NC):
                rows = slice(HG_BLOCK * c, HG_BLOCK * (c + 1))
                sp_ref[c, :, sl] = _compact_state(st)
                o_ref[rows, sl] = o_intra[rows] + _dotg(qb[rows], st.astype(BF16), NT)
                u = _dotg(vp[rows], kb[rows], TN) * bd_ref[...]
                st = st * el[HG_BLOCK * c:HG_BLOCK * c + 1, sl] + u
            st_ref[p] = st

    row = lambda j: pl.BlockSpec((t, HG_WIDTH), lambda i: (i, j))
    full = lambda a: pl.BlockSpec(a.shape, lambda i: (0, 0))
    return pl.pallas_call(
        body,
        grid=(s // t,),
        in_specs=[row(6), row(7), row(8), full(lbl), full(mcum), full(msum), full(bd)],
        out_specs=[row(0), pl.BlockSpec((HG_NC, 64, HG_WIDTH), lambda i: (i, 0, 0))],
        out_shape=[jax.ShapeDtypeStruct((s, HG_WIDTH), F32),
                   jax.ShapeDtypeStruct((s // HG_BLOCK, 64, HG_WIDTH), F32)],
        scratch_shapes=[pltpu.VMEM((HEADS // HG_G, GW, GW), F32)],
        compiler_params=_params(("arbitrary",)),
        name="hgrn_fwd",
    )(proj, proj, proj, lbl, mcum, msum, bd)


def _slot_shape(name, r, c):
    return (N_DEV, r, c // N_DEV) if COL_SHARDED[name] else (N_DEV, r // N_DEV, c)


def _emit_slots(name, acc_ref, out_ref):
    r, c = acc_ref.shape
    for p in range(N_DEV):
        if COL_SHARDED[name]:
            out_ref[p] = acc_ref[:, c // N_DEV * p:c // N_DEV * (p + 1)].astype(BF16)
        else:
            out_ref[p] = acc_ref[r // N_DEV * p:r // N_DEV * (p + 1), :].astype(BF16)


def _hgrn_bwd(proj, lbl, do, sprev, dproj, pairs):
    s = proj.shape[0]
    t = HG_T
    nt = s // t
    npair = len(pairs)
    mcum, mrev, msum, bd = _hg_consts()

    def body(hq_ref, hf_ref, hi_ref, lbl_ref, do_ref, sp_ref, mcum_ref, mrev_ref, msum_ref, bd_ref,
             dproj_in, *rest):
        del dproj_in
        pair_refs, (dh_ref, dlbl_ref) = rest[:2 * npair], rest[2 * npair:2 * npair + 2]
        dw_refs, g_ref, acc_refs = rest[2 * npair + 2:3 * npair + 2], rest[3 * npair + 2], rest[3 * npair + 3:]

        @pl.when(pl.program_id(0) == 0)
        def _():
            g_ref[...] = jnp.zeros_like(g_ref)
            dlbl_ref[...] = jnp.zeros_like(dlbl_ref)
            for acc_ref in acc_refs:
                acc_ref[...] = jnp.zeros_like(acc_ref)

        for n, acc_ref in enumerate(acc_refs):
            acc_ref[...] += _dot(pair_refs[2 * n][...], pair_refs[2 * n + 1][...])

        @pl.when(pl.program_id(0) == nt - 1)
        def _():
            for (name, _, _), acc_ref, dw_ref in zip(pairs, acc_refs, dw_refs):
                _emit_slots(name, acc_ref, dw_ref)

        mc = mcum_ref[...]
        lb, sig, f, b, big_l, qd, ki, ke = _hg_pre(hq_ref[...].astype(F32), hf_ref[...].astype(F32), lbl_ref[...], mc,
                                                   msum_ref[...])
        el = jnp.exp(big_l)
        hi = hi_ref[...]
        dov = do_ref[...]
        head = lax.broadcasted_iota(jnp.int32, (t, GW), 1) >> 6
        head64 = lax.broadcasted_iota(jnp.int32, (64, GW), 1) >> 6
        mask = jnp.concatenate([mc] * HG_G, axis=0) > 0.5
        dqd_parts, dke_parts, dv_parts, del_parts, dki_parts = [], [], [], [], []
        for p in range(HEADS // HG_G):
            sl = slice(GW * p, GW * (p + 1))
            vp = hi[:, sl].astype(BF16)
            qs = _stack_heads(qd[:, sl], head).astype(BF16)
            kip = ki[:, sl].astype(BF16)
            dos = _stack_heads(dov[:, sl], head).astype(BF16)
            a = jnp.where(mask, _dotg(qs, kip, NT), 0.0).astype(BF16)
            da = jnp.where(mask, _dotg(dos, vp, NT), 0.0).astype(BF16)
            r = _dot(da, kip)
            dki_parts.append(_dotg(da, qs, TN))
            qb = qd[:, sl].astype(BF16)
            kb = ke[:, sl].astype(BF16)
            dob = dov[:, sl].astype(BF16)
            g = g_ref[p]
            dqd_c, dv_c, dke_c, del_c = [], [], [], []
            for c in range(HG_NC - 1, -1, -1):
                rows = slice(HG_BLOCK * c, HG_BLOCK * (c + 1))
                gb = g.astype(BF16)
                st = _expand_state(sp_ref[c, :, sl], head64)
                dqd_c.append(_dot(dob[rows], st.astype(BF16)))
                dv_c.append(_dotg(kb[rows], gb, NT))
                dke_c.append(_dot(vp[rows], gb))
                del_c.append(jnp.broadcast_to(jnp.sum(g * st, axis=0, keepdims=True), (HG_BLOCK, GW)))
                g = g * el[HG_BLOCK * c:HG_BLOCK * c + 1, sl] + _dotg(dob[rows], qb[rows], TN) * bd_ref[...]
            g_ref[p] = g
            up = lambda parts: jnp.concatenate(parts[::-1], axis=0)
            dqd_parts.append(_unstack_heads(r, head, t) + up(dqd_c))
            dv_parts.append(_dotg(a, dos, TN) + up(dv_c))
            dke_parts.append(up(dke_c))
            del_parts.append(up(del_c))
        wide = lambda parts: jnp.concatenate(parts, axis=1)
        dqd, dke, dki, dvv, del_rows = wide(dqd_parts), wide(dke_parts), wide(dki_parts), wide(dv_parts), wide(del_parts)
        dh_ref[:, :HG_WIDTH] = (dqd * jnp.exp(b)).astype(BF16)
        dh_ref[:, 2 * HG_WIDTH:] = dvv.astype(BF16)
        dke_ke = dke * ke
        db = dqd * qd - dki * ki - dke_ke
        dl_rows = _sel_left(msum_ref[...], dke_ke) + del_rows * el
        is_last = (lax.broadcasted_iota(jnp.int32, (t, HG_WIDTH), 0) & (HG_BLOCK - 1)) == HG_BLOCK - 1
        db = db + jnp.where(is_last, dl_rows, 0.0)
        dlf = _sel_left(mrev_ref[...], db)
        dk = dki * jnp.exp(-b) + dke * jnp.exp(big_l - b)
        df = dlf / f - dk
        dh_ref[:, HG_WIDTH:2 * HG_WIDTH] = (df * (1.0 - lb) * sig * (1.0 - sig)).astype(BF16)
        dlb = jnp.sum(df * (1.0 - sig), axis=0, keepdims=True) * lb * (1.0 - lb)
        dlbl_ref[0:1, :] += dlb
        dlbl_ref[1:2, :] -= dlb

    rrow = lambda j: pl.BlockSpec((t, HG_WIDTH), lambda i: (nt - 1 - i, j))
    full = lambda a: pl.BlockSpec(a.shape, lambda i: (0, 0))
    pair_specs, dw_specs, dw_shapes, accs = [], [], [], []
    for name, at, b in pairs:
        pair_specs += [pl.BlockSpec((at.shape[0], t), lambda i: (0, i)), pl.BlockSpec((t, b.shape[1]), lambda i: (i, 0))]
        shape = _slot_shape(name, at.shape[0], b.shape[1])
        dw_specs.append(pl.BlockSpec(shape, lambda i: (0, 0, 0)))
        dw_shapes.append(jax.ShapeDtypeStruct(shape, BF16))
        accs.append(pltpu.VMEM((at.shape[0], b.shape[1]), F32))
    return pl.pallas_call(
        body,
        grid=(nt,),
        in_specs=[rrow(6), rrow(7), rrow(8), full(lbl), rrow(0),
                  pl.BlockSpec((HG_NC, 64, HG_WIDTH), lambda i: (nt - 1 - i, 0, 0)),
                  full(mcum), full(mrev), full(msum), full(bd), pl.BlockSpec(memory_space=pl.ANY)] + pair_specs,
        out_specs=[pl.BlockSpec((t, 3 * HG_WIDTH), lambda i: (nt - 1 - i, 2)),
                   pl.BlockSpec((2, HG_WIDTH), lambda i: (0, 0))] + dw_specs,
        out_shape=[jax.ShapeDtypeStruct(dproj.shape, BF16), jax.ShapeDtypeStruct((2, HG_WIDTH), F32)] + dw_shapes,
        input_output_aliases={10: 0},
        scratch_shapes=[pltpu.VMEM((HEADS // HG_G, GW, GW), F32)] + accs,
        compiler_params=_params(("arbitrary",)),
        name="hgrn_bwd",
    )(proj, proj, proj, lbl, do, sprev, mcum, mrev, msum, bd, dproj, *[a for pair in pairs for a in pair[1:]])


def _tail(x, tgt, proj, attn, o, w_a, w_b, w_out, w_at, w_bt, w_outt, b_gate, g_post, gh):
    s = x.shape[0]
    tm = 256
    ones64 = (jnp.arange(HG_WIDTH)[:, None] // 64 == jnp.arange(HG_WIDTH)[None, :] // 64).astype(BF16)
    weights = (w_a, w_b, w_out, w_at, w_bt, w_outt)

    def body(x_ref, t_ref, ml_ref, ga_ref, gb_ref, at_ref, o_ref, *rest):
        w_hbm, (bg_ref, gp_ref, gh_ref, ones_ref) = rest[:6], rest[6:10]
        (dout_ref, dpj_ref, dop_ref, do_ref, mt_ref, dy_ref, yat_ref, dya_ref, ybt_ref, dyb_ref,
         loss_ref, dgp_ref, dbg_ref, dgh_ref) = rest[10:24]
        (wa_ref, wb_ref, wo_ref, wat_ref, wbt_ref, wot_ref), w_sem = rest[24:30], rest[30]

        @pl.when(pl.program_id(0) == 0)
        def _():
            loads = [pltpu.make_async_copy(src, dst, w_sem.at[k])
                     for k, (src, dst) in enumerate(zip(w_hbm, rest[24:30]))]
            _start_all(loads, [])
            loss_ref[...] = jnp.zeros_like(loss_ref)
            dgp_ref[...] = jnp.zeros_like(dgp_ref)
            dbg_ref[...] = jnp.zeros_like(dbg_ref)
            dgh_ref[...] = jnp.zeros_like(dgh_ref)
            _wait_all(loads, [])

        ones = ones_ref[...]
        gate_a = ga_ref[...].astype(F32)
        sa = _sigmoid(gate_a)
        silu_a = gate_a * sa
        attn_v = at_ref[...]
        ya_in = attn_v * silu_a
        ov = o_ref[...]
        ro = lax.rsqrt(_sel_right(ov * ov, ones) * (1.0 / 64.0) + EPS)
        ohat = ov * ro
        ghv = gh_ref[...]
        on = ohat * ghv
        gate_b = gb_ref[...].astype(F32)
        sb = _sigmoid(gate_b)
        silu_b = gate_b * sb
        yb_in = on * silu_b
        ya_bf = ya_in.astype(BF16)
        yb_bf = yb_in.astype(BF16)
        yat_ref[...] = ya_bf.T
        ybt_ref[...] = yb_bf.T
        y_a = _dot(ya_bf, wa_ref[...])
        y_b = _dot(yb_bf, wb_ref[...])
        gts = _sigmoid(ml_ref[...].astype(F32) + bg_ref[...])
        g_a = gts[:, :D_MODEL]
        g_b = gts[:, D_MODEL:]
        m_bf = (g_a * y_a + g_b * y_b).astype(BF16)
        mt_ref[...] = m_bf.T
        y = _dot(m_bf, wo_ref[...])
        r1 = lax.rsqrt(jnp.mean(y * y, axis=-1, keepdims=True) + EPS)
        yn = y * r1
        gp = gp_ref[...]
        e = x_ref[...] + yn * gp - t_ref[...]
        loss_ref[...] += jnp.sum(e * e, axis=0, keepdims=True)
        dout = e * (1.0 / D_MODEL)
        dout_ref[...] = dout
        dgp_ref[...] += jnp.sum(dout * yn, axis=0, keepdims=True)
        dyn = dout * gp
        dy = r1 * (dyn - yn * jnp.mean(dyn * yn, axis=-1, keepdims=True))
        dy_bf = dy.astype(BF16)
        dy_ref[...] = dy_bf
        dm = _dot(dy_bf, wot_ref[...])
        dml_a = dm * y_a * g_a * (1.0 - g_a)
        dml_b = dm * y_b * g_b * (1.0 - g_b)
        dpj_ref[:, :D_MODEL] = dml_a.astype(BF16)
        dpj_ref[:, D_MODEL:2 * D_MODEL] = dml_b.astype(BF16)
        dbg_ref[:, :D_MODEL] += jnp.sum(dml_a, axis=0, keepdims=True)
        dbg_ref[:, D_MODEL:] += jnp.sum(dml_b, axis=0, keepdims=True)
        dya_bf = (dm * g_a).astype(BF16)
        dyb_bf = (dm * g_b).astype(BF16)
        dya_ref[...] = dya_bf
        dyb_ref[...] = dyb_bf
        dya_in = _dot(dya_bf, wat_ref[...])
        dyb_in = _dot(dyb_bf, wbt_ref[...])
        dattn = dya_in * silu_a
        delta = _sel_right(dattn * attn_v, ones)
        lane = lax.broadcasted_iota(jnp.int32, (tm, LANE), 1)
        for p in range(HEADS // 2):
            sl = slice(LANE * p, LANE * (p + 1))
            xs = (dattn[:, sl], pltpu.roll(dattn[:, sl], VDIM, 1))
            nds = (-pltpu.roll(delta[:, sl], VDIM, 1), -delta[:, sl])
            for a in range(2):
                hi, lo_part = _hi_lo(nds[a])
                blk = jnp.where(lane < VDIM, xs[a], jnp.where(lane == VDIM, hi, jnp.where(lane == VDIM + 1, lo_part, 0.0)))
                dop_ref[:, LANE * (2 * p + a):LANE * (2 * p + a + 1)] = blk.astype(BF16)
        dpj_ref[:, 2 * D_MODEL:2 * D_MODEL + HG_WIDTH] = (
            dya_in * attn_v * (sa * (1.0 + gate_a * (1.0 - sa)))).astype(BF16)
        don = dyb_in * silu_b
        dpj_ref[:, 2 * D_MODEL + HG_WIDTH:] = (dyb_in * on * (sb * (1.0 + gate_b * (1.0 - sb)))).astype(BF16)
        dgh_ref[...] += jnp.sum(don * ohat, axis=0, keepdims=True)
        dohat = don * ghv
        do_ref[...] = (ro * (dohat - ohat * (_sel_right(dohat * ohat, ones) * (1.0 / 64.0)))).astype(BF16)

    row = lambda w, j: pl.BlockSpec((tm, w), lambda i: (i, j))
    col = lambda w: pl.BlockSpec((w, tm), lambda i: (0, i))
    full = lambda a: pl.BlockSpec(a.shape, lambda i: (0, 0))
    acc = lambda w: pl.BlockSpec((1, w), lambda i: (0, 0))
    sds = lambda w, dt: jax.ShapeDtypeStruct((s, w), dt)
    sdt = lambda w: jax.ShapeDtypeStruct((w, s), BF16)
    return pl.pallas_call(
        body,
        grid=(s // tm,),
        in_specs=[row(1024, 0), row(1024, 0), row(2048, 0), row(512, 4), row(512, 5), row(512, 0), row(512, 0)]
        + [ANY] * 6 + [full(b_gate), full(g_post), full(gh), full(ones64)],
        out_specs=[row(1024, 0), row(3072, 0), row(1024, 0), row(512, 0),
                   col(1024), row(1024, 0), col(512), row(1024, 0), col(512), row(1024, 0),
                   acc(1024), acc(1024), acc(2048), acc(512)],
        out_shape=[sds(1024, F32), sds(D_IN_PAD, BF16), sds(1024, BF16), sds(512, BF16),
                   sdt(1024), sds(1024, BF16), sdt(512), sds(1024, BF16), sdt(512), sds(1024, BF16),
                   jax.ShapeDtypeStruct((1, 1024), F32), jax.ShapeDtypeStruct((1, 1024), F32),
                   jax.ShapeDtypeStruct((1, 2048), F32), jax.ShapeDtypeStruct((1, 512), F32)],
        scratch_shapes=[pltpu.VMEM(a.shape, BF16) for a in weights] + [pltpu.SemaphoreType.DMA((6,))],
        compiler_params=_params(("arbitrary",), 56),
        name="tail",
    )(x, tgt, proj, proj, proj, attn, o, *weights, b_gate, g_post, gh, ones64)


def _mla_bwd(proj, dqr, dkr, dv, g_q, g_kv, w_uq_pt, w_kv_pt, rc, rs1, rs2, cqt, ckvt, dproj):
    assert HEADS == N_DEV
    s = proj.shape[0]
    tm = 256
    scale = 1.0 / math.sqrt(QK)

    def body(cq_ref, ckv_ref, dqr_ref, dkr_ref, dv_ref, gq_ref, gkv_ref, wuqt_ref, wkvt_ref, c_ref, s1_ref, s2_ref,
             cqt_ref, ckvt_ref, dproj_in, dc_ref, dgq_ref, dgkv_ref, uq_slots, ukv_slots,
             dqf_ref, dkvf_ref, dwuq_ref, dwkv_ref):
        del dproj_in

        @pl.when(pl.program_id(0) == 0)
        def _():
            dgq_ref[...] = jnp.zeros_like(dgq_ref)
            dgkv_ref[...] = jnp.zeros_like(dgkv_ref)
            dwuq_ref[...] = jnp.zeros_like(dwuq_ref)
            dwkv_ref[...] = jnp.zeros_like(dwkv_ref)

        c, s1, s2 = c_ref[...], s1_ref[...], s2_ref[...]
        lane = lax.broadcasted_iota(jnp.int32, (tm, LANE), 1)
        ksum = jnp.zeros((tm, LANE), F32)
        for h in range(HEADS):
            sl = slice(LANE * h, LANE * (h + 1))
            dqf_ref[:, sl] = (_unrope(dqr_ref[:, sl], c, s1, s2) * scale).astype(BF16)
            dkh = dkr_ref[:, sl]
            ksum = ksum + dkh
            dkvf_ref[:, sl] = jnp.where(lane < NOPE, dkh, 0.0).astype(BF16)
            dkvf_ref[:, HEADS * LANE + LANE * h:HEADS * LANE + LANE * (h + 1)] = jnp.where(
                lane < VDIM, dv_ref[:, sl], 0.0).astype(BF16)
        dkpe = _unrope(ksum, c, s1, s2)
        dc_ref[:, Q_LORA + KV_LORA:] = jnp.where((lane >= NOPE) & (lane < QK), dkpe, 0.0).astype(BF16)
        dqf, dkvf = dqf_ref[...], dkvf_ref[...]
        dwuq_ref[...] += _dot(cqt_ref[...], dqf)
        dwkv_ref[...] += _dot(ckvt_ref[...], dkvf)
        dcqn = _dot(dqf, wuqt_ref[...])
        dckvn = _dot(dkvf, wkvt_ref[...])
        for x_ref, g_ref, dn, cols, dg_ref in ((cq_ref, gq_ref, dcqn, slice(0, Q_LORA), dgq_ref),
                                               (ckv_ref, gkv_ref, dckvn, slice(Q_LORA, Q_LORA + KV_LORA), dgkv_ref)):
            xv = x_ref[...].astype(F32)
            r = lax.rsqrt(jnp.mean(xv * xv, axis=-1, keepdims=True) + EPS)
            xh = xv * r
            dg_ref[...] += jnp.sum(dn * xh, axis=0, keepdims=True)
            dh = dn * g_ref[...]
            dc_ref[:, cols] = (r * (dh - xh * jnp.mean(dh * xh, axis=-1, keepdims=True))).astype(BF16)

        @pl.when(pl.program_id(0) == s // tm - 1)
        def _():
            ur = Q_LORA // N_DEV
            for p in range(N_DEV):
                uq_slots[p] = jnp.concatenate(
                    [dwuq_ref[ur * p:ur * (p + 1), LANE * h:LANE * h + QK] for h in range(HEADS)], axis=1).astype(BF16)
                ukv_slots[p] = jnp.concatenate(
                    [dwkv_ref[:, LANE * p:LANE * p + NOPE],
                     dwkv_ref[:, LANE * (HEADS + p):LANE * (HEADS + p) + VDIM]], axis=1).astype(BF16)

    row = lambda w, j: pl.BlockSpec((tm, w), lambda i: (i, j))
    full = lambda a: pl.BlockSpec(a.shape, lambda i: (0, 0))
    acc = lambda w: pl.BlockSpec((1, w), lambda i: (0, 0))
    col = lambda w: pl.BlockSpec((w, tm), lambda i: (0, i))
    whole = lambda shape: pl.BlockSpec(shape, lambda i: (0, 0, 0))
    uq_shape = (N_DEV, Q_LORA // N_DEV, HEADS * QK)
    ukv_shape = (N_DEV, KV_LORA, NOPE + VDIM)
    return pl.pallas_call(
        body,
        grid=(s // tm,),
        in_specs=[row(768, 6), row(256, 21), row(1024, 0), row(1024, 0), row(1024, 0), full(g_q), full(g_kv),
                  full(w_uq_pt), full(w_kv_pt), row(128, 0), row(128, 0), row(128, 0), col(Q_LORA), col(KV_LORA),
                  pl.BlockSpec(memory_space=pl.ANY)],
        out_specs=[row(1152, 4), acc(768), acc(256), whole(uq_shape), whole(ukv_shape)],
        out_shape=[jax.ShapeDtypeStruct(dproj.shape, BF16),
                   jax.ShapeDtypeStruct((1, 768), F32), jax.ShapeDtypeStruct((1, 256), F32),
                   jax.ShapeDtypeStruct(uq_shape, BF16), jax.ShapeDtypeStruct(ukv_shape, BF16)],
        input_output_aliases={14: 0},
        scratch_shapes=[pltpu.VMEM((tm, HEADS * LANE), BF16), pltpu.VMEM((tm, 2 * HEADS * LANE), BF16),
                        pltpu.VMEM((Q_LORA, HEADS * LANE), F32), pltpu.VMEM((KV_LORA, 2 * HEADS * LANE), F32)],
        compiler_params=_params(("arbitrary",)),
        name="mla_bwd",
    )(proj, proj, dqr, dkr, dv, g_q, g_kv, w_uq_pt, w_kv_pt, rc, rs1, rs2, cqt, ckvt, dproj)


def _dh_dx(dproj, w_in_pt, x, dout, g_pre, sends):
    s, k = dproj.shape
    tm = 256
    ns, ni = len(sends), s // tm

    def body(dp_ref, w_ref, x_ref, dout_ref, g_ref, *rest):
        send_refs, (dx_ref, dg_ref) = rest[:ns], rest[ns:ns + 2]
        recv_refs, sems = rest[ns + 2:2 * ns + 2], rest[2 * ns + 2:]

        @pl.when(pl.program_id(0) == 0)
        def _():
            _start_all(*_to_chips_copies(send_refs, recv_refs, sems))
            dg_ref[...] = jnp.zeros_like(dg_ref)

        dh = _dot(dp_ref[...], w_ref[...])
        xv = x_ref[...]
        r = lax.rsqrt(jnp.mean(xv * xv, axis=-1, keepdims=True) + EPS)
        xh = xv * r
        dg_ref[...] += jnp.sum(dh * xh, axis=0, keepdims=True)
        dxh = dh * g_ref[...]
        dx_ref[...] = dout_ref[...] + r * (dxh - xh * jnp.mean(dxh * xh, axis=-1, keepdims=True))

        @pl.when(pl.program_id(0) == ni - 1)
        def _():
            _wait_all(*_to_chips_copies(send_refs, recv_refs, sems))

    row = lambda w: pl.BlockSpec((tm, w), lambda i: (i, 0))
    return pl.pallas_call(
        body,
        grid=(ni,),
        in_specs=[row(k), pl.BlockSpec((k, D_MODEL), lambda i: (0, 0)), row(D_MODEL), row(D_MODEL),
                  pl.BlockSpec((1, D_MODEL), lambda i: (0, 0))] + [ANY] * ns,
        out_specs=[row(D_MODEL), pl.BlockSpec((1, D_MODEL), lambda i: (0, 0))] + [ANY] * ns,
        out_shape=[jax.ShapeDtypeStruct((s, D_MODEL), F32), jax.ShapeDtypeStruct((1, D_MODEL), F32)]
        + [jax.ShapeDtypeStruct(a.shape, a.dtype) for a in sends],
        scratch_shapes=_copy_sems(ns, 3),
        compiler_params=_params(("arbitrary",)),
        name="dh_dx",
    )(dproj, w_in_pt, x, dout, g_pre, *sends)


def _pair_reduce(slots):
    n = len(slots)
    half = [(N_DEV // 2,) + a.shape[1:] for a in slots]

    def body(*refs):
        s_refs, o_refs = refs[:n], refs[n:2 * n]
        mine, got = refs[2 * n:3 * n], refs[3 * n:4 * n]
        send_sems, recv_sems, local_sems = refs[4 * n:]
        x, y, c = _my_place()
        copies, loads = [], []
        for a in range(n):
            for q in range(N_DEV // 2):
                copies.append(pltpu.make_async_remote_copy(
                    src_ref=s_refs[a].at[2 * q + 1 - c], dst_ref=got[a].at[q],
                    send_sem=send_sems.at[4 * a + q], recv_sem=recv_sems.at[4 * a + q],
                    device_id=(x, y, 1 - c), device_id_type=MESH_ID))
                loads.append(pltpu.make_async_copy(s_refs[a].at[2 * q + c], mine[a].at[q], local_sems.at[4 * a + q]))
        _start_all(loads, copies)
        _wait_all(loads, copies)
        for a in range(n):
            o_refs[a][...] = (mine[a][...].astype(F32) + got[a][...].astype(F32)).astype(o_refs[a].dtype)

    vm = lambda: [pltpu.VMEM(h, a.dtype) for h, a in zip(half, slots)]
    return pl.pallas_call(
        body,
        in_specs=[ANY] * n,
        out_shape=[jax.ShapeDtypeStruct(h, a.dtype) for h, a in zip(half, slots)],
        scratch_shapes=vm() + vm() + [pltpu.SemaphoreType.DMA((4 * n,)), pltpu.SemaphoreType.DMA((4 * n,)),
                                      pltpu.SemaphoreType.DMA((4 * n,))],
        compiler_params=pltpu.CompilerParams(vmem_limit_bytes=48 * 2**20),
        name="pair_reduce",
    )(*slots)


def _rope_tables(s):
    inv = (np.float32(ROPE_THETA) ** (-np.arange(0, ROPE, 2, dtype=np.float32) / np.float32(ROPE))).astype(np.float32)
    ang = (np.arange(s, dtype=np.float32)[:, None] * inv[None, :]).astype(np.float32)
    cos, sin = jnp.asarray(np.cos(ang.astype(np.float64)), F32), jnp.asarray(np.sin(ang.astype(np.float64)), F32)
    z = lambda w: jnp.zeros((s, w), F32)
    rc = jnp.concatenate([jnp.ones((s, NOPE), F32), cos, cos, z(32)], axis=1)
    rs1 = jnp.concatenate([z(NOPE), -sin, z(16), z(32)], axis=1)
    rs2 = jnp.concatenate([z(NOPE), z(16), sin, z(32)], axis=1)
    return rc, rs1, rs2


def _step(x, tgt, w_blk, shards, g_pre, b_gate, g_q, g_kv, lbl, g_hgrn, g_post):
    s = x.shape[0]
    rc, rs1, rs2 = _rope_tables(s)
    gh = jnp.tile(g_hgrn, (1, HEADS))

    proj, w_in_pt, ht, *got = _gather_proj(x, g_pre, w_blk, shards[:2])
    w_uq, w_ukv = (_from_slots(n, g) for n, g in zip(MATS[:2], got))
    w_uq_p = jnp.pad(w_uq.reshape(Q_LORA, HEADS, QK), ((0, 0), (0, 0), (0, LANE - QK))).reshape(Q_LORA, HEADS * LANE)
    kv3 = w_ukv.reshape(KV_LORA, HEADS, NOPE + VDIM)
    pad64 = lambda t: jnp.pad(t, ((0, 0), (0, 0), (0, LANE - 64))).reshape(KV_LORA, HEADS * LANE)
    w_kv_p = jnp.concatenate([pad64(kv3[:, :, :NOPE]), pad64(kv3[:, :, NOPE:])], axis=1)

    qr, kr, v, cqt, ckvt = _mla_prep(proj, g_q, g_kv, w_uq_p, w_kv_p, rc, rs1, rs2)
    attn, qa, *got = _attn_fwd(qr, kr, v, shards[2:])
    w_a, w_b, w_out = (_from_slots(n, g) for n, g in zip(MATS[2:], got))
    o, sprev = _hgrn_fwd(proj, lbl)
    (dout, dproj, dop, do, mt, dy_bf, yat, dya_bf, ybt, dyb_bf,
     loss_vec, dg_post, db_gate, dgh) = _tail(x, tgt, proj, attn, o, w_a, w_b, w_out, w_a.T, w_b.T, w_out.T,
                                               b_gate, g_post, gh)
    dproj, dlbl, *early = _hgrn_bwd(proj, lbl, do, sprev, dproj,
                                    [("w_branch_a", yat, dya_bf), ("w_branch_b", ybt, dyb_bf), ("w_out", mt, dy_bf)])
    dqr, dkr, dv, *early_recv = _attn_bwd(qa, kr, v, dop, early)
    dproj, dg_q, dg_kv, dw_uq_slots, dw_ukv_slots = _mla_bwd(proj, dqr, dkr, dv, g_q, g_kv, w_uq_p.T, w_kv_p.T,
                                                             rc, rs1, rs2, cqt, ckvt, dproj)

    dw_in_slots = _dw_in_slots(ht, dproj)
    late = _pair_reduce([dw_in_slots, dw_uq_slots, dw_ukv_slots])
    dx, dg_pre, *late_recv = _dh_dx(dproj, w_in_pt, x, dout, g_pre, late)

    g_sum = _vectors_sum(dg_pre, db_gate, dg_q, dg_kv, dlbl, dgh, dg_post, loss_vec)
    return dx, late_recv[0], dict(zip(MATS, late_recv[1:] + early_recv)), g_sum


def _adamw(g, w, m, v):
    c1 = 1.0 / (1.0 - ADAM_B1 ** ADAM_STEP)
    c2 = 1.0 / (1.0 - ADAM_B2 ** ADAM_STEP)
    nm = ADAM_B1 * m + (1.0 - ADAM_B1) * g
    nv = ADAM_B2 * v + (1.0 - ADAM_B2) * (g * g)
    d = -ADAM_LR * ((nm * c1) / (jnp.sqrt(nv * c2) + ADAM_EPS) + ADAM_WD * w)
    return d, nm, nv


def _sum8(r_ref):
    g = r_ref[0].astype(F32)
    for k in range(1, r_ref.shape[0]):
        g = g + r_ref[k].astype(F32)
    return g


def _sum_adamw_w_in(recv, w, m, v):
    rows, _, cols = w.shape
    tc = 256
    nc = cols // tc

    def body(r_ref, w_hbm, m_hbm, v_hbm, g_hbm, d_hbm, nm_hbm, nv_hbm, ins, outs, in_sems, out_sems):
        i = pl.program_id(0)
        slot = i & 1
        cols_of = lambda step: pl.ds(pl.multiple_of(step * tc, tc), tc)

        def load(k, step, sl):
            return pltpu.make_async_copy((w_hbm, m_hbm, v_hbm)[k].at[:, 0, cols_of(step)], ins.at[sl, k],
                                         in_sems.at[sl, k])

        def store(k, step, sl):
            return pltpu.make_async_copy(outs.at[sl, k], (g_hbm, d_hbm, nm_hbm, nv_hbm)[k].at[:, 0, cols_of(step)],
                                         out_sems.at[sl, k])

        @pl.when(i == 0)
        def _():
            for k in range(3):
                load(k, 0, 0).start()

        @pl.when(i + 1 < nc)
        def _():
            for k in range(3):
                load(k, i + 1, 1 - slot).start()

        @pl.when(i >= 2)
        def _():
            for k in range(4):
                store(k, i - 2, slot).wait()

        for k in range(3):
            load(k, i, slot).wait()
        g = _sum8(r_ref)
        d, nm, nv = _adamw(g, ins[slot, 0], ins[slot, 1], ins[slot, 2])
        for k, val in enumerate((g, d, nm, nv)):
            outs[slot, k] = val
        for k in range(4):
            store(k, i, slot).start()

        @pl.when(i == nc - 1)
        def _():
            for k in range(4):
                store(k, i, slot).wait()
            if nc >= 2:
                for k in range(4):
                    store(k, i - 1, 1 - slot).wait()

    out = jax.ShapeDtypeStruct((rows, 1, cols), F32)
    return pl.pallas_call(
        body,
        grid=(nc,),
        in_specs=[pl.BlockSpec((recv.shape[0], rows, tc), lambda i: (0, 0, i)), ANY, ANY, ANY],
        out_specs=[ANY, ANY, ANY, ANY],
        out_shape=[out, out, out, out],
        scratch_shapes=[pltpu.VMEM((2, 3, rows, tc), F32), pltpu.VMEM((2, 4, rows, tc), F32),
                        pltpu.SemaphoreType.DMA((2, 3)), pltpu.SemaphoreType.DMA((2, 4))],
        compiler_params=_params(("arbitrary",)),
        name="sum_adamw_w_in",
    )(recv, w, m, v)


def _sum_adamw_whole(recvs, ws, ms, vs):
    n = len(ws)

    def body(*refs):
        r_refs, w_refs, m_refs, v_refs = refs[:n], refs[n:2 * n], refs[2 * n:3 * n], refs[3 * n:4 * n]
        outs = refs[4 * n:]
        for a in range(n):
            g = _sum8(r_refs[a])
            d, nm, nv = _adamw(g, w_refs[a][...], m_refs[a][...], v_refs[a][...])
            outs[a][...] = g
            outs[n + a][...] = d
            outs[2 * n + a][...] = nm
            outs[3 * n + a][...] = nv

    shapes = [jax.ShapeDtypeStruct(w.shape, F32) for w in ws]
    res = pl.pallas_call(
        body,
        out_shape=shapes * 4,
        compiler_params=pltpu.CompilerParams(vmem_limit_bytes=48 * 2**20),
        name="sum_adamw_mats",
    )(*recvs, *ws, *ms, *vs)
    return res[:n], res[n:2 * n], res[2 * n:3 * n], res[3 * n:]


SMALL = ("g_pre", "b_gate", "g_q", "g_kv", "lb_logits", "g_hgrn", "g_post")
SMALL_SHAPE = dict(g_pre=(1, 1024), b_gate=(1, 2048), g_q=(1, 768), g_kv=(1, 256), lb_logits=(2, 512),
                   g_hgrn=(1, 64), g_post=(1, 1024))


def _vectors_sum(dg_pre, db_gate, dg_q, dg_kv, dlbl, dgh, dg_post, loss_vec):
    def body(gpre_ref, bg_ref, gq_ref, gkv_ref, lbl_ref, gh_ref, gpost_ref, loss_ref, out_ref, mine, got,
             send_sems, recv_sems):
        mine[...] = jnp.zeros_like(mine)
        mine[0:1, :] = gpre_ref[...]
        mine[1:2, :] = bg_ref[:, :1024]
        mine[2:3, :] = bg_ref[:, 1024:]
        mine[3:4, :Q_LORA] = gq_ref[...]
        mine[4:5, :KV_LORA] = gkv_ref[...]
        loss = (0.5 / D_MODEL) * jnp.sum(loss_ref[...], axis=-1, keepdims=True)
        mine[4:5, KV_LORA:] = jnp.broadcast_to(loss, (1, 1024 - KV_LORA))
        mine[5:6, :HG_WIDTH] = lbl_ref[0:1, :]
        mine[5:6, HG_WIDTH:] = lbl_ref[1:2, :]
        gh = gh_ref[...]
        fold = gh[:, :VDIM]
        for h in range(1, HEADS):
            fold = fold + gh[:, VDIM * h:VDIM * (h + 1)]
        mine[6:7, :VDIM] = fold
        mine[7:8, :] = gpost_ref[...]
        x, y, c = _my_place()
        me = 4 * x + 2 * y + c
        got[me] = mine[...]
        copies = [pltpu.make_async_remote_copy(
            src_ref=mine, dst_ref=got.at[me], send_sem=send_sems.at[k], recv_sem=recv_sems.at[k],
            device_id=_flip(k, x, y, c), device_id_type=MESH_ID) for k in range(N_DEV - 1)]
        _start_all([], copies)
        _wait_all([], copies)
        out_ref[...] = _sum8(got)

    return pl.pallas_call(
        body,
        out_shape=jax.ShapeDtypeStruct((8, 1024), F32),
        scratch_shapes=[pltpu.VMEM((8, 1024), F32), pltpu.VMEM((N_DEV, 8, 1024), F32),
                        pltpu.SemaphoreType.DMA((7,)), pltpu.SemaphoreType.DMA((7,))],
        name="vectors_sum",
    )(dg_pre, db_gate, dg_q, dg_kv, dlbl, dgh, dg_post, loss_vec)


def _vectors_adamw(g_sum, ws, ms, vs):
    n = len(SMALL)

    def body(g_ref, *refs):
        w_refs, m_refs, v_refs = refs[:n], refs[n:2 * n], refs[2 * n:3 * n]
        loss_ref, outs = refs[3 * n], refs[3 * n + 1:]
        g = g_ref[...]
        loss_ref[...] = g[4:5, KV_LORA:KV_LORA + 1]
        grads = (g[0:1, :], jnp.concatenate([g[1:2, :], g[2:3, :]], axis=1), g[3:4, :Q_LORA], g[4:5, :KV_LORA],
                 jnp.concatenate([g[5:6, :HG_WIDTH], g[5:6, HG_WIDTH:]], axis=0), g[6:7, :VDIM], g[7:8, :])
        for a in range(n):
            d, nm, nv = _adamw(grads[a], w_refs[a][...], m_refs[a][...], v_refs[a][...])
            outs[a][...] = grads[a]
            outs[n + a][...] = d
            outs[2 * n + a][...] = nm
            outs[3 * n + a][...] = nv

    shapes = [jax.ShapeDtypeStruct(SMALL_SHAPE[k], F32) for k in SMALL]
    res = pl.pallas_call(
        body,
        out_shape=[jax.ShapeDtypeStruct((1, 1), F32)] + shapes * 4,
        name="vectors_adamw",
    )(g_sum, *ws, *ms, *vs)
    return res[0], res[1:n + 1], res[n + 1:2 * n + 1], res[2 * n + 1:3 * n + 1], res[3 * n + 1:]


MATS = ("w_uq", "w_ukv", "w_branch_a", "w_branch_b", "w_out")
COL_SHARDED = dict(w_uq=False, w_ukv=True, w_branch_a=True, w_branch_b=True, w_out=False)
ORDER = ("g_pre", "w_in", "b_gate", "g_q", "w_uq", "g_kv", "w_ukv", "lb_logits", "g_hgrn",
         "w_branch_a", "w_branch_b", "w_out", "g_post")


def _from_slots(name, slots):
    _, r, c = slots.shape
    if COL_SHARDED[name]:
        return slots.transpose(1, 0, 2).reshape(r, N_DEV * c)
    return slots.reshape(N_DEV * r, c)


def _shards_bf16(w_in3, mats):
    n = len(mats)

    def body(win_hbm, *rest):
        m_refs, blk_ref, o_refs, (buf, sem) = rest[:n], rest[n], rest[n + 1:2 * n + 1], rest[2 * n + 1:]
        load = pltpu.make_async_copy(win_hbm.at[:, 0, pl.ds(0, D_MODEL)], buf, sem)
        load.start()
        for m_ref, o_ref in zip(m_refs, o_refs):
            o_ref[...] = m_ref[...].astype(BF16)
        load.wait()
        blk_ref[...] = buf[...].astype(BF16)

    whole = lambda shape: pl.BlockSpec(shape, lambda i: (0, 0))
    blk, *out = pl.pallas_call(
        body,
        grid=(1,),
        in_specs=[ANY] + [whole(a.shape) for a in mats],
        out_specs=[whole((W_IN_SHARD, D_MODEL))] + [whole(a.shape) for a in mats],
        out_shape=[jax.ShapeDtypeStruct((W_IN_SHARD, D_MODEL), BF16)]
        + [jax.ShapeDtypeStruct(a.shape, BF16) for a in mats],
        scratch_shapes=[pltpu.VMEM((W_IN_SHARD, D_MODEL), F32), pltpu.SemaphoreType.DMA],
        name="shards_bf16",
    )(w_in3, *mats)
    return blk, out


def kernel(x, g_pre, w_in, b_gate, g_q, w_uq, g_kv, w_ukv, lb_logits, g_hgrn, w_branch_a, w_branch_b, w_out, g_post, loss_target, m_g_pre, m_w_in, m_b_gate, m_g_q, m_w_uq, m_g_kv, m_w_ukv, m_lb_logits, m_g_hgrn, m_w_branch_a, m_w_branch_b, m_w_out, m_g_post, v_g_pre, v_w_in, v_b_gate, v_g_q, v_w_uq, v_g_kv, v_w_ukv, v_lb_logits, v_g_hgrn, v_w_branch_a, v_w_branch_b, v_w_out, v_g_post):
    rows3 = lambda a: jnp.transpose(a, (2, 0, 1))
    w = dict(w_in=rows3(w_in), w_uq=w_uq[0], w_ukv=w_ukv[0], w_branch_a=w_branch_a[0], w_branch_b=w_branch_b[0],
             w_out=w_out[0], g_pre=g_pre, b_gate=b_gate, g_q=g_q, g_kv=g_kv, lb_logits=lb_logits, g_hgrn=g_hgrn,
             g_post=g_post)
    mom = dict(w_in=rows3(m_w_in), w_uq=m_w_uq[0], w_ukv=m_w_ukv[0], w_branch_a=m_w_branch_a[0],
               w_branch_b=m_w_branch_b[0], w_out=m_w_out[0], g_pre=m_g_pre, b_gate=m_b_gate, g_q=m_g_q, g_kv=m_g_kv,
               lb_logits=m_lb_logits, g_hgrn=m_g_hgrn, g_post=m_g_post)
    var = dict(w_in=rows3(v_w_in), w_uq=v_w_uq[0], w_ukv=v_w_ukv[0], w_branch_a=v_w_branch_a[0],
               w_branch_b=v_w_branch_b[0], w_out=v_w_out[0], g_pre=v_g_pre, b_gate=v_b_gate, g_q=v_g_q, g_kv=v_g_kv,
               lb_logits=v_lb_logits, g_hgrn=v_g_hgrn, g_post=v_g_post)

    w_blk, shards = _shards_bf16(w["w_in"], [w[n] for n in MATS])
    dx, recv_in, recv, g_sum = _step(x[0], loss_target[0], w_blk, shards,
                                     g_pre, b_gate, g_q, g_kv, lb_logits, g_hgrn, g_post)

    g_in, d_in, m_in, v_in = _sum_adamw_w_in(recv_in, w["w_in"], mom["w_in"], var["w_in"])
    res = _sum_adamw_whole([recv[n] for n in MATS], *([t[n] for n in MATS] for t in (w, mom, var)))
    total, *vec = _vectors_adamw(g_sum, *([t[n] for n in SMALL] for t in (w, mom, var)))

    outs = []
    for mats, vecs, big in zip(res, vec, (g_in, d_in, m_in, v_in)):
        t = {**{n: a[None] for n, a in zip(MATS, mats)}, **dict(zip(SMALL, vecs)),
             "w_in": jnp.transpose(big, (1, 2, 0))}
        outs += [t[n] for n in ORDER]
    return (total.reshape(()), dx[None], *outs)
```

```python
import math

import jax
import jax.numpy as jnp
import numpy as np
from jax import lax
from jax.experimental import pallas as pl
from jax.experimental.pallas import tpu as pltpu

F32, BF16 = jnp.float32, jnp.bfloat16

D_MODEL = 1024
EPS = 1e-6
HEADS = 8
NOPE, ROPE, VDIM = 64, 32, 64
QK = NOPE + ROPE
Q_LORA, KV_LORA = 768, 256
ROPE_THETA = 10000.0
ATT_CHUNK_SHIFT = 6
HG_BLOCK = 32
HG_WIDTH = 512
D_IN = 5664
D_IN_PAD = 5760
W_IN_SHARD = D_IN // 8
N_DEV = 8
LANE = 128

ADAM_LR, ADAM_B1, ADAM_B2, ADAM_EPS, ADAM_WD, ADAM_STEP = 0.001, 0.9, 0.999, 1e-08, 0.01, 10

W_IN_SEGMENTS = ((3616, 5664, 0), (1056, 1568, 2048), (3104, 3616, 2560), (1568, 3104, 3072),
                 (0, 1024, 4608), (1024, 1056, 5696))

NT = (((1,), (1,)), ((), ()))
TN = (((0,), (0,)), ((), ()))
MESH_ID = pl.DeviceIdType.MESH


def _w_in_pieces():
    out = []
    for lo, hi, dst in W_IN_SEGMENTS:
        c = lo
        while c < hi:
            p = c // W_IN_SHARD
            e = min(hi, (p + 1) * W_IN_SHARD)
            out.append((p, c - p * W_IN_SHARD, e - p * W_IN_SHARD, dst + c - lo))
            c = e
    return out


def _params(sem, vmem_mb=48):
    return pltpu.CompilerParams(dimension_semantics=sem, vmem_limit_bytes=vmem_mb * 2**20)


def _dot(a, b):
    return jnp.dot(a, b, preferred_element_type=F32)


def _dotg(a, b, dims):
    return lax.dot_general(a, b, dims, preferred_element_type=F32)


def _split2(x):
    hi = x.astype(BF16)
    return hi, (x - hi.astype(F32)).astype(BF16)


def _sel_left(m01, x):
    hi, lo = _split2(x)
    return _dot(m01, hi) + _dot(m01, lo)


def _sel_right(x, m01):
    hi, lo = _split2(x)
    return _dot(hi, m01) + _dot(lo, m01)


def _hi_lo(x):
    hi = x.astype(BF16).astype(F32)
    return hi, x - hi


def _sigmoid(x):
    return 0.5 * jnp.tanh(0.5 * x) + 0.5


def _rope(x, c, s1, s2):
    return x * c + pltpu.roll(x, 112, 1) * s1 + pltpu.roll(x, 16, 1) * s2


def _unrope(d, c, s1, s2):
    return d * c + pltpu.roll(d * s1, 16, 1) + pltpu.roll(d * s2, 112, 1)


def _my_place():
    return lax.axis_index("x"), lax.axis_index("y"), lax.axis_index("c")


def _flip(k, x, y, c):
    fx, fy, fc = (k + 1) >> 2 & 1, (k + 1) >> 1 & 1, (k + 1) & 1
    return (1 - x if fx else x), (1 - y if fy else y), (1 - c if fc else c)


def _to_all_copies(s_refs, r_refs, sems, spread):
    send_sems, recv_sems, local_sems = sems
    x, y, c = _my_place()
    me = 4 * x + 2 * y + c
    src = (lambda a, p: s_refs[a]) if spread else (lambda a, p: s_refs[a].at[p])
    local = [pltpu.make_async_copy(src(a, me), r_refs[a].at[me], local_sems.at[a]) for a in range(len(s_refs))]
    remote = []
    for k in range(N_DEV - 1):
        px, py, pc = _flip(k, x, y, c)
        for a in range(len(s_refs)):
            remote.append(pltpu.make_async_remote_copy(
                src_ref=src(a, 4 * px + 2 * py + pc), dst_ref=r_refs[a].at[me],
                send_sem=send_sems.at[7 * a + k], recv_sem=recv_sems.at[7 * a + k],
                device_id=(px, py, pc), device_id_type=MESH_ID))
    return local, remote


def _to_chips_copies(s_refs, r_refs, sems):
    send_sems, recv_sems, local_sems = sems
    x, y, c = _my_place()
    me = 2 * x + y
    local = [pltpu.make_async_copy(s_refs[a].at[me], r_refs[a].at[me], local_sems.at[a]) for a in range(len(s_refs))]
    remote = []
    for k in range(3):
        px = 1 - x if (k + 1) >> 1 & 1 else x
        py = 1 - y if (k + 1) & 1 else y
        for a in range(len(s_refs)):
            remote.append(pltpu.make_async_remote_copy(
                src_ref=s_refs[a].at[2 * px + py], dst_ref=r_refs[a].at[me],
                send_sem=send_sems.at[3 * a + k], recv_sem=recv_sems.at[3 * a + k],
                device_id=(px, py, c), device_id_type=MESH_ID))
    return local, remote


def _start_all(local, remote):
    for cp in local + remote:
        cp.start()


def _wait_all(local, remote):
    for cp in remote:
        cp.wait_recv()
    for cp in remote:
        cp.wait_send()
    for cp in local:
        cp.wait()


def _copy_sems(n, peers):
    return [pltpu.SemaphoreType.DMA((peers * n,)), pltpu.SemaphoreType.DMA((peers * n,)),
            pltpu.SemaphoreType.DMA((n,))]


ANY = pl.BlockSpec(memory_space=pl.ANY)


def _dw_in_slots(ht, dproj):
    m, k = ht.shape
    n = dproj.shape[1]
    tn, tk = 1152, 1024
    nj, nk = n // tn, k // tk
    by_tile = [[] for _ in range(nj)]
    for p, lo, hi, dst in _w_in_pieces():
        while lo < hi:
            j = dst // tn
            cnt = min(hi - lo, (j + 1) * tn - dst)
            by_tile[j].append((p, lo, lo + cnt, dst - j * tn))
            lo, dst = lo + cnt, dst + cnt

    def body(a_ref, b_ref, s_ref, acc_ref):
        j, l = pl.program_id(0), pl.program_id(1)

        @pl.when(l == 0)
        def _():
            acc_ref[...] = jnp.zeros_like(acc_ref)

        acc_ref[...] += _dot(a_ref[...], b_ref[...])

        @pl.when(l == nk - 1)
        def _():
            at = acc_ref[...].T
            for jj in range(nj):
                @pl.when(j == jj)
                def _(jj=jj):
                    for p, lo, hi, d in by_tile[jj]:
                        s_ref[p, lo:hi, :] = at[d:d + hi - lo, :].astype(BF16)

    return pl.pallas_call(
        body,
        grid=(nj, nk),
        in_specs=[pl.BlockSpec((m, tk), lambda j, l: (0, l)), pl.BlockSpec((tk, tn), lambda j, l: (l, j))],
        out_specs=pl.BlockSpec((N_DEV, W_IN_SHARD, m), lambda j, l: (0, 0, 0)),
        out_shape=jax.ShapeDtypeStruct((N_DEV, W_IN_SHARD, m), BF16),
        scratch_shapes=[pltpu.VMEM((m, tn), F32)],
        compiler_params=_params(("arbitrary", "arbitrary")),
        name="dw_in",
    )(ht, dproj)


PROJ_DT = F32
GP_TN = 256
GP_COLS = 5888
GP_NT = GP_COLS // GP_TN


def _gp_tile_pieces():
    tiles = [[] for _ in range(GP_NT)]
    for p, lo, hi, dst in _w_in_pieces():
        while lo < hi:
            t = dst // GP_TN
            n = min(hi - lo, (t + 1) * GP_TN - dst)
            tiles[t].append((p, lo, lo + n, dst - t * GP_TN))
            lo, dst = lo + n, dst + n
    return tiles


def _gp_tables():
    pieces = _gp_tile_pieces()
    rank_of = {None: 0, 0: 1, 1: 2, 2: 2, 4: 3, 5: 3, 3: 4, 6: 5}
    order = np.zeros((N_DEV, GP_NT), np.int32)
    waits = np.zeros((N_DEV, GP_NT), np.int32)
    for me in range(N_DEV):
        x, y, c = me >> 2 & 1, me >> 1 & 1, me & 1
        chips = [(1 - x, y), (x, 1 - y), (1 - x, 1 - y)]

        def sem_of(p):
            px, py, pc = p >> 2 & 1, p >> 1 & 1, p & 1
            if (px, py) == (x, y):
                return None if pc == c else 0
            j = chips.index((px, py))
            return 1 + j if pc == c else 4 + j

        needs = [sorted({sem_of(p) for p, _, _, _ in tile} - {None}) for tile in pieces]
        ranks = [max([rank_of[k] for k in ks], default=0) for ks in needs]
        seq = sorted(range(GP_NT), key=lambda t: (ranks[t], t))
        seen = set()
        for step, t in enumerate(seq):
            order[me, step] = t
            new = [k for k in needs[t] if k not in seen]
            for k in new:
                waits[me, step] |= 1 << k
            seen.update(new)
        assert seen == set(range(7)), (me, seen)
    return order, waits


def _gather_proj(x, g_pre, w_blk, shards):
    s = x.shape[0]
    tx = 512
    ns = len(shards)
    tile_pieces = _gp_tile_pieces()
    order_np, waits_np = _gp_tables()
    xq, yq, cq = _my_place()
    me_out = 4 * xq + 2 * yq + cq
    order = lax.dynamic_index_in_dim(jnp.asarray(order_np), me_out, 0, keepdims=False)
    waits = lax.dynamic_index_in_dim(jnp.asarray(waits_np), me_out, 0, keepdims=False)

    def body(order_ref, waits_ref, x_hbm, g_ref, wblk_hbm, *rest):
        shard_refs, (proj_ref, wt_ref, ht_hbm), got_refs = rest[:ns], rest[ns:ns + 3], rest[ns + 3:2 * ns + 3]
        recv, h_ref, wtile, xbuf, htbuf = rest[2 * ns + 3:2 * ns + 8]
        send_sems, recv_sems, misc_sems = rest[2 * ns + 8:2 * ns + 11]
        sems = rest[2 * ns + 11:]
        t = pl.program_id(0)
        x_, y_, c = _my_place()
        sibling = (x_, y_, 1 - c)
        chips = [(1 - x_, y_), (x_, 1 - y_), (1 - x_, 1 - y_)]
        idx = lambda px, py, pc: 4 * px + 2 * py + pc
        me = idx(x_, y_, c)

        def copy(k, slot, to, src=None):
            return pltpu.make_async_remote_copy(
                src_ref=recv.at[slot] if src is None else src, dst_ref=recv.at[slot],
                send_sem=send_sems.at[k], recv_sem=recv_sems.at[k], device_id=to, device_id_type=MESH_ID)

        mine = pltpu.make_async_copy(wblk_hbm, recv.at[me], misc_sems.at[0])
        first = [copy(0, me, sibling, src=wblk_hbm)] + [copy(1 + j, me, (*chips[j], c), src=wblk_hbm) for j in range(2)]
        passed = [copy(4 + j, idx(*ch, c), sibling) for j, ch in enumerate(chips)]
        onward = [copy(3, idx(*chips[0], c), (*chips[1], c)), copy(3, idx(*chips[1], c), (*chips[0], c))]
        arrivals = ([copy(0, idx(x_, y_, 1 - c), sibling)] + [copy(1 + j, idx(*ch, c), sibling) for j, ch in enumerate(chips)]
                    + [copy(4 + j, idx(*ch, 1 - c), sibling) for j, ch in enumerate(chips)])

        @pl.when(t == 0)
        def _():
            mine.start()
            for cp in first:
                cp.start()
            _start_all(*_to_all_copies(shard_refs, got_refs, sems, True))

            def load(i):
                return pltpu.make_async_copy(x_hbm.at[pl.ds(i * tx, tx), :], xbuf.at[i & 1], misc_sems.at[1 + (i & 1)])

            def store(i):
                return pltpu.make_async_copy(htbuf.at[i & 1], ht_hbm.at[:, pl.ds(i * tx, tx)], misc_sems.at[3 + (i & 1)])

            load(0).start()
            for i in range(s // tx):
                if i + 1 < s // tx:
                    load(i + 1).start()
                load(i).wait()
                xv = xbuf[i & 1]
                r = lax.rsqrt(jnp.mean(xv * xv, axis=-1, keepdims=True) + EPS)
                h = (xv * r * g_ref[...]).astype(BF16)
                h_ref[i * tx:(i + 1) * tx, :] = h
                if i >= 2:
                    store(i - 2).wait()
                htbuf[i & 1] = h.T
                store(i).start()
            for i in range(max(s // tx - 2, 0), s // tx):
                store(i).wait()
            mine.wait()

        w = waits_ref[t]
        for k in range(7):
            @pl.when((w >> k) & 1 == 1)
            def _(k=k):
                arrivals[k].wait_recv()
                if 1 <= k <= 3:
                    passed[k - 1].start()
                if 1 <= k <= 2:
                    @pl.when(c == k - 1)
                    def _():
                        onward[k - 1].start()

        tile = order_ref[t]
        for tt in range(GP_NT):
            @pl.when(tile == tt)
            def _(tt=tt):
                covered = sorted((d, d + hi - lo) for _, lo, hi, d in tile_pieces[tt])
                at = 0
                for lo_z, hi_z in covered + [(GP_TN, GP_TN)]:
                    if lo_z > at:
                        wtile[at:lo_z, :] = jnp.zeros((lo_z - at, D_MODEL), BF16)
                    at = max(at, hi_z)
                for p, lo, hi, d in tile_pieces[tt]:
                    wtile[d:d + hi - lo, :] = recv[p, lo:hi, :]

        wt = wtile[...]
        wt_ref[...] = wt
        proj_ref[...] = _dotg(h_ref[...], wt, NT).astype(PROJ_DT)

        @pl.when(t == GP_NT - 1)
        def _():
            for cp in first + passed + onward[:1]:
                cp.wait_send()
            _wait_all(*_to_all_copies(shard_refs, got_refs, sems, True))

    grid_spec = pltpu.PrefetchScalarGridSpec(
        num_scalar_prefetch=2,
        grid=(GP_NT,),
        in_specs=[ANY, pl.BlockSpec((1, D_MODEL), lambda t, o, w: (0, 0)), ANY] + [ANY] * ns,
        out_specs=[pl.BlockSpec((s, GP_TN), lambda t, o, w: (0, o[t])),
                   pl.BlockSpec((GP_TN, D_MODEL), lambda t, o, w: (o[t], 0)), ANY] + [ANY] * ns,
        scratch_shapes=[pltpu.VMEM((N_DEV, W_IN_SHARD, D_MODEL), BF16), pltpu.VMEM((s, D_MODEL), BF16),
                        pltpu.VMEM((GP_TN, D_MODEL), BF16), pltpu.VMEM((2, tx, D_MODEL), F32),
                        pltpu.VMEM((2, D_MODEL, tx), BF16),
                        pltpu.SemaphoreType.DMA((7,)), pltpu.SemaphoreType.DMA((7,)), pltpu.SemaphoreType.DMA((5,))]
        + _copy_sems(ns, 7),
    )
    return pl.pallas_call(
        body,
        grid_spec=grid_spec,
        out_shape=[jax.ShapeDtypeStruct((s, GP_COLS), PROJ_DT), jax.ShapeDtypeStruct((GP_COLS, D_MODEL), BF16),
                   jax.ShapeDtypeStruct((D_MODEL, s), BF16)]
        + [jax.ShapeDtypeStruct((N_DEV,) + b.shape, b.dtype) for b in shards],
        compiler_params=_params(("arbitrary",), 56),
        name="gather_proj",
    )(order, waits, x, g_pre, w_blk, *shards)


def _mla_prep(proj, g_q, g_kv, w_uq_p, w_kv_p, rc, rs1, rs2):
    s = proj.shape[0]
    tm = 512
    scale = 1.0 / math.sqrt(QK)

    def body(cq_ref, ckv_ref, kpe_ref, gq_ref, gkv_ref, wuq_ref, wkv_ref, c_ref, s1_ref, s2_ref,
             qr_ref, kr_ref, v_ref, cqt_ref, ckvt_ref):
        cq = cq_ref[...].astype(F32)
        r = lax.rsqrt(jnp.mean(cq * cq, axis=-1, keepdims=True) + EPS)
        cqn = (cq * r * gq_ref[...]).astype(BF16)
        cqt_ref[...] = cqn.T
        q = _dot(cqn, wuq_ref[...])
        ckv = ckv_ref[...].astype(F32)
        r = lax.rsqrt(jnp.mean(ckv * ckv, axis=-1, keepdims=True) + EPS)
        ckvn = (ckv * r * gkv_ref[...]).astype(BF16)
        ckvt_ref[...] = ckvn.T
        kv = _dot(ckvn, wkv_ref[...])
        c, s1, s2 = c_ref[...], s1_ref[...], s2_ref[...]
        lane = lax.broadcasted_iota(jnp.int32, (tm, LANE), 1)
        kpe = _rope(kpe_ref[...].astype(F32), c, s1, s2) + jnp.where((lane == QK) | (lane == QK + 1), 1.0, 0.0)
        vone = jnp.where((lane == VDIM) | (lane == VDIM + 1), 1.0, 0.0)
        for h in range(HEADS):
            sl = slice(LANE * h, LANE * (h + 1))
            qr_ref[:, sl] = (_rope(q[:, sl], c, s1, s2) * scale).astype(BF16)
            kr_ref[:, sl] = (kv[:, sl] + kpe).astype(BF16)
            v_ref[:, sl] = (kv[:, HEADS * LANE + LANE * h:HEADS * LANE + LANE * (h + 1)] + vone).astype(BF16)

    row = lambda w, j: pl.BlockSpec((tm, w), lambda i: (i, j))
    col = lambda w: pl.BlockSpec((w, tm), lambda i: (0, i))
    full = lambda a: pl.BlockSpec(a.shape, lambda i: (0, 0))
    return pl.pallas_call(
        body,
        grid=(s // tm,),
        in_specs=[row(768, 6), row(256, 21), row(128, 44), full(g_q), full(g_kv), full(w_uq_p), full(w_kv_p),
                  row(128, 0), row(128, 0), row(128, 0)],
        out_specs=[row(1024, 0), row(1024, 0), row(1024, 0), col(768), col(256)],
        out_shape=[jax.ShapeDtypeStruct((s, 1024), BF16), jax.ShapeDtypeStruct((s, 1024), BF16),
                   jax.ShapeDtypeStruct((s, 1024), BF16), jax.ShapeDtypeStruct((768, s), BF16),
                   jax.ShapeDtypeStruct((256, s), BF16)],
        compiler_params=_params(("arbitrary",)),
        name="mla_prep",
    )(proj, proj, proj, g_q, g_kv, w_uq_p, w_kv_p, rc, rs1, rs2)


ATT_T = 512
ATT_FWD_HEADS = 4


def _chunk_mask(transposed):
    r = lax.broadcasted_iota(jnp.int32, (ATT_T, ATT_T), 0) >> ATT_CHUNK_SHIFT
    c = lax.broadcasted_iota(jnp.int32, (ATT_T, ATT_T), 1) >> ATT_CHUNK_SHIFT
    return (r <= c) if transposed else (c <= r)


def _attn_fwd(qr, kr, vp, shards):
    s = qr.shape[0]
    t = ATT_T
    g = ATT_FWD_HEADS
    ns = len(shards)

    def body(q_ref, k_ref, v_ref, *rest):
        shard_refs, (o_ref, qa_ref), got_refs = rest[:ns], rest[ns:ns + 2], rest[ns + 2:2 * ns + 2]
        sc_ref, sems = rest[2 * ns + 2], rest[2 * ns + 3:]
        qi = pl.program_id(1)

        @pl.when((pl.program_id(0) == 0) & (qi == 0))
        def _():
            _start_all(*_to_all_copies(shard_refs, got_refs, sems, True))
        lane = lax.broadcasted_iota(jnp.int32, (t, LANE), 1)
        sls = [slice(LANE * a, LANE * (a + 1)) for a in range(g)]
        qs = [q_ref[:, sl] for sl in sls]

        def scores(j):
            rows = pl.ds(pl.multiple_of(j * t, t), t)
            for a in range(g):
                sc_ref[j & 1, a] = _dotg(qs[a], k_ref[rows, sls[a]], NT)

        def step(j, carry, masked):
            rows = pl.ds(pl.multiple_of(j * t, t), t)
            out = []
            for a in range(g):
                m, acc = carry[a]
                sc = sc_ref[j & 1, a]
                if masked:
                    sc = jnp.where(_chunk_mask(False), sc, -1e30)
                m_new = jnp.maximum(m, jnp.max(sc, axis=-1, keepdims=True))
                p = jnp.exp(sc - m_new).astype(BF16)
                acc = jnp.exp(m - m_new) * acc + _dot(p, v_ref[rows, sls[a]])
                out.append((m_new, acc))
            return tuple(out)

        def loop(j, carry):
            carry = step(j, carry, False)
            scores(j + 1)
            return carry

        init = tuple((jnp.full((t, 1), -1e30, F32), jnp.zeros((t, LANE), F32)) for _ in range(g))
        scores(0)
        carry = lax.fori_loop(0, qi, loop, init)
        carry = step(qi, carry, True)
        outs = []
        for a in range(g):
            m, acc = carry[a]
            l = acc[:, VDIM:VDIM + 1]
            outs.append(acc / l)
            hi, lo_part = _hi_lo(-(m + jnp.log(l)))
            qa = jnp.where(lane == QK, hi, jnp.where(lane == QK + 1, lo_part, qs[a].astype(F32)))
            qa_ref[:, sls[a]] = qa.astype(BF16)
        for p in range(g // 2):
            o_ref[:, LANE * p:LANE * (p + 1)] = jnp.where(lane < VDIM, outs[2 * p], pltpu.roll(outs[2 * p + 1], VDIM, 1))

        @pl.when((pl.program_id(0) == HEADS // g - 1) & (qi == s // t - 1))
        def _():
            _wait_all(*_to_all_copies(shard_refs, got_refs, sems, True))

    return pl.pallas_call(
        body,
        grid=(HEADS // g, s // t),
        in_specs=[
            pl.BlockSpec((t, g * LANE), lambda h, i: (i, h)),
            pl.BlockSpec((s, g * LANE), lambda h, i: (0, h)),
            pl.BlockSpec((s, g * LANE), lambda h, i: (0, h)),
        ] + [ANY] * ns,
        out_specs=[
            pl.BlockSpec((t, g * VDIM), lambda h, i: (i, h)),
            pl.BlockSpec((t, g * LANE), lambda h, i: (i, h)),
        ] + [ANY] * ns,
        out_shape=[jax.ShapeDtypeStruct((s, 512), F32), jax.ShapeDtypeStruct((s, 1024), BF16)]
        + [jax.ShapeDtypeStruct((N_DEV,) + b.shape, b.dtype) for b in shards],
        scratch_shapes=[pltpu.VMEM((2, g, t, t), F32)] + _copy_sems(ns, 7),
        compiler_params=_params(("arbitrary", "arbitrary")),
        name="attn_fwd",
    )(qr, kr, vp, *shards)


def _attn_bwd(qa, kr, vp, dop, sends):
    s = qa.shape[0]
    t = ATT_T
    nq = s // t
    ns = len(sends)

    def body(q_ref, k_ref, v_ref, do_ref, *rest):
        send_refs, (dq_out, dk_out, dv_out) = rest[:ns], rest[ns:ns + 3]
        recv_refs = rest[ns + 3:2 * ns + 3]
        (dq_ref, dk_ref, dv_ref), sems = rest[2 * ns + 3:2 * ns + 6], rest[2 * ns + 6:]
        j = pl.program_id(1)
        sls = [slice(LANE * a, LANE * (a + 1)) for a in range(2)]

        @pl.when((pl.program_id(0) == 0) & (j == 0))
        def _():
            _start_all(*_to_all_copies(send_refs, recv_refs, sems, False))

        @pl.when(j == 0)
        def _():
            dq_ref[...] = jnp.zeros_like(dq_ref)

        dk_ref[...] = jnp.zeros_like(dk_ref)
        dv_ref[...] = jnp.zeros_like(dv_ref)
        ks = [k_ref[:, sl] for sl in sls]
        vs = [v_ref[:, sl] for sl in sls]

        def part(i, k_lo, k_n, q_lo, q_n, masked):
            rows = pl.ds(pl.multiple_of(i * t + q_lo, 256), q_n)
            keys = slice(k_lo, k_lo + k_n)
            for a in range(2):
                q = q_ref[rows, sls[a]]
                do = do_ref[rows, sls[a]]
                sc = _dotg(ks[a][keys], q, NT)
                if masked:
                    kc = lax.broadcasted_iota(jnp.int32, (k_n, q_n), 0) >> ATT_CHUNK_SHIFT
                    qc = lax.broadcasted_iota(jnp.int32, (k_n, q_n), 1) >> ATT_CHUNK_SHIFT
                    sc = jnp.where(kc <= qc, sc, -1e30)
                p = jnp.exp(sc)
                ds = (p * _dotg(vs[a][keys], do, NT)).astype(BF16)
                dv_ref[keys, sls[a]] += _dot(p.astype(BF16), do)
                dk_ref[keys, sls[a]] += _dot(ds, q)
                dq_ref[rows, sls[a]] += _dotg(ds, ks[a][keys], TN)

        half = t // 2
        part(j, 0, half, 0, t, True)
        part(j, half, half, half, half, True)

        def loop(i, c):
            part(i, 0, t, 0, t, False)
            return c

        lax.fori_loop(j + 1, nq, loop, 0)
        dk_out[...] = dk_ref[...].astype(BF16)
        dv_out[...] = dv_ref[...].astype(BF16)

        @pl.when(j == nq - 1)
        def _():
            dq_out[...] = dq_ref[...].astype(BF16)

        @pl.when((pl.program_id(0) == HEADS // 2 - 1) & (j == nq - 1))
        def _():
            _wait_all(*_to_all_copies(send_refs, recv_refs, sems, False))

    blk = pl.BlockSpec((t, 2 * LANE), lambda h, j: (j, h))
    whole = pl.BlockSpec((s, 2 * LANE), lambda h, j: (0, h))
    out = jax.ShapeDtypeStruct((s, 1024), BF16)
    return pl.pallas_call(
        body,
        grid=(HEADS // 2, nq),
        in_specs=[whole, blk, blk, whole] + [ANY] * ns,
        out_specs=[whole, blk, blk] + [ANY] * ns,
        out_shape=[out, out, out] + [jax.ShapeDtypeStruct(a.shape, a.dtype) for a in sends],
        scratch_shapes=[pltpu.VMEM((s, 2 * LANE), F32), pltpu.VMEM((t, 2 * LANE), F32),
                        pltpu.VMEM((t, 2 * LANE), F32)] + _copy_sems(ns, 7),
        compiler_params=_params(("arbitrary", "arbitrary")),
        name="attn_bwd",
    )(qa, kr, vp, dop, *sends)


HG_T = 256
HG_NC = HG_T // HG_BLOCK
HG_G = 4
GW = 64 * HG_G


def _hg_consts():
    r = jnp.arange(HG_T)[:, None]
    c = jnp.arange(HG_T)[None, :]
    same = (r // HG_BLOCK) == (c // HG_BLOCK)
    mcum = (same & (c <= r)).astype(BF16)
    mrev = (same & (c >= r)).astype(BF16)
    msum = same.astype(BF16)
    a = jnp.arange(GW) // 64
    bd = (a[:, None] == a[None, :]).astype(F32)
    return mcum, mrev, msum, bd


def _stack_heads(xg, head):
    return jnp.concatenate([jnp.where(head == h, xg, 0.0) for h in range(HG_G)], axis=0)


def _unstack_heads(r, head, t):
    out = r[(HG_G - 1) * t:]
    for h in range(HG_G - 2, -1, -1):
        out = jnp.where(head == h, r[h * t:(h + 1) * t], out)
    return out


def _compact_state(st):
    out = st[:64]
    for h in range(1, HG_G):
        out = out + st[64 * h:64 * (h + 1)]
    return out


def _expand_state(cs, head64):
    return jnp.concatenate([jnp.where(head64 == h, cs, 0.0) for h in range(HG_G)], axis=0)


def _hg_pre(hq, hf, lbl, mcum, msum):
    lb = _sigmoid(lbl[0:1, :] - lbl[1:2, :])
    sig = _sigmoid(hf)
    f = lb + (1.0 - lb) * sig
    lf = jnp.log(f)
    b = _sel_left(mcum, lf)
    big_l = _sel_left(msum, lf)
    k = 1.0 - f
    qd = hq * jnp.exp(b)
    ki = k * jnp.exp(-b)
    ke = k * jnp.exp(big_l - b)
    return lb, sig, f, b, big_l, qd, ki, ke


def _hgrn_fwd(proj, lbl):
    s = proj.shape[0]
    t = HG_T
    mcum, _, msum, bd = _hg_consts()

    def body(hq_ref, hf_ref, hi_ref, lbl_ref, mcum_ref, msum_ref, bd_ref, o_ref, sp_ref, st_ref):
        @pl.when(pl.program_id(0) == 0)
        def _():
            st_ref[...] = jnp.zeros_like(st_ref)

        mc = mcum_ref[...]
        _, _, _, _, big_l, qd, ki, ke = _hg_pre(hq_ref[...].astype(F32), hf_ref[...].astype(F32), lbl_ref[...], mc,
                                                msum_ref[...])
        el = jnp.exp(big_l)
        hi = hi_ref[...]
        head = lax.broadcasted_iota(jnp.int32, (t, GW), 1) >> 6
        mask = jnp.concatenate([mc] * HG_G, axis=0) > 0.5
        for p in range(HEADS // HG_G):
            sl = slice(GW * p, GW * (p + 1))
            vp = hi[:, sl].astype(BF16)
            qs = _stack_heads(qd[:, sl], head).astype(BF16)
            a = jnp.where(mask, _dotg(qs, ki[:, sl].astype(BF16), NT), 0.0)
            o_intra = _unstack_heads(_dot(a.astype(BF16), vp), head, t)
            qb = qd[:, sl].astype(BF16)
            kb = ke[:, sl].astype(BF16)
            st = st_ref[p]
            for c in range(HG_NC):
                rows = slice(HG_BLOCK * c, HG_BLOCK * (c + 1))
                sp_ref[c, :, sl] = _compact_state(st)
                o_ref[rows, sl] = o_intra[rows] + _dotg(qb[rows], st.astype(BF16), NT)
                u = _dotg(vp[rows], kb[rows], TN) * bd_ref[...]
                st = st * el[HG_BLOCK * c:HG_BLOCK * c + 1, sl] + u
            st_ref[p] = st

    row = lambda j: pl.BlockSpec((t, HG_WIDTH), lambda i: (i, j))
    full = lambda a: pl.BlockSpec(a.shape, lambda i: (0, 0))
    return pl.pallas_call(
        body,
        grid=(s // t,),
        in_specs=[row(6), row(7), row(8), full(lbl), full(mcum), full(msum), full(bd)],
        out_specs=[row(0), pl.BlockSpec((HG_NC, 64, HG_WIDTH), lambda i: (i, 0, 0))],
        out_shape=[jax.ShapeDtypeStruct((s, HG_WIDTH), F32),
                   jax.ShapeDtypeStruct((s // HG_BLOCK, 64, HG_WIDTH), F32)],
        scratch_shapes=[pltpu.VMEM((HEADS // HG_G, GW, GW), F32)],
        compiler_params=_params(("arbitrary",)),
        name="hgrn_fwd",
    )(proj, proj, proj, lbl, mcum, msum, bd)


def _slot_shape(name, r, c):
    return (N_DEV, r, c // N_DEV) if COL_SHARDED[name] else (N_DEV, r // N_DEV, c)


def _emit_slots(name, acc_ref, out_ref):
    r, c = acc_ref.shape
    for p in range(N_DEV):
        if COL_SHARDED[name]:
            out_ref[p] = acc_ref[:, c // N_DEV * p:c // N_DEV * (p + 1)].astype(BF16)
        else:
            out_ref[p] = acc_ref[r // N_DEV * p:r // N_DEV * (p + 1), :].astype(BF16)


def _hgrn_bwd(proj, lbl, do, sprev, dproj, pairs):
    s = proj.shape[0]
    t = HG_T
    nt = s // t
    npair = len(pairs)
    mcum, mrev, msum, bd = _hg_consts()

    def body(hq_ref, hf_ref, hi_ref, lbl_ref, do_ref, sp_ref, mcum_ref, mrev_ref, msum_ref, bd_ref,
             dproj_in, *rest):
        del dproj_in
        pair_refs, (dh_ref, dlbl_ref) = rest[:2 * npair], rest[2 * npair:2 * npair + 2]
        dw_refs, g_ref, acc_refs = rest[2 * npair + 2:3 * npair + 2], rest[3 * npair + 2], rest[3 * npair + 3:]

        @pl.when(pl.program_id(0) == 0)
        def _():
            g_ref[...] = jnp.zeros_like(g_ref)
            dlbl_ref[...] = jnp.zeros_like(dlbl_ref)
            for acc_ref in acc_refs:
                acc_ref[...] = jnp.zeros_like(acc_ref)

        for n, acc_ref in enumerate(acc_refs):
            acc_ref[...] += _dot(pair_refs[2 * n][...], pair_refs[2 * n + 1][...])

        @pl.when(pl.program_id(0) == nt - 1)
        def _():
            for (name, _, _), acc_ref, dw_ref in zip(pairs, acc_refs, dw_refs):
                _emit_slots(name, acc_ref, dw_ref)

        mc = mcum_ref[...]
        lb, sig, f, b, big_l, qd, ki, ke = _hg_pre(hq_ref[...].astype(F32), hf_ref[...].astype(F32), lbl_ref[...], mc,
                                                   msum_ref[...])
        el = jnp.exp(big_l)
        hi = hi_ref[...]
        dov = do_ref[...]
        head = lax.broadcasted_iota(jnp.int32, (t, GW), 1) >> 6
        head64 = lax.broadcasted_iota(jnp.int32, (64, GW), 1) >> 6
        mask = jnp.concatenate([mc] * HG_G, axis=0) > 0.5
        dqd_parts, dke_parts, dv_parts, del_parts, dki_parts = [], [], [], [], []
        for p in range(HEADS // HG_G):
            sl = slice(GW * p, GW * (p + 1))
            vp = hi[:, sl].astype(BF16)
            qs = _stack_heads(qd[:, sl], head).astype(BF16)
            kip = ki[:, sl].astype(BF16)
            dos = _stack_heads(dov[:, sl], head).astype(BF16)
            a = jnp.where(mask, _dotg(qs, kip, NT), 0.0).astype(BF16)
            da = jnp.where(mask, _dotg(dos, vp, NT), 0.0).astype(BF16)
            r = _dot(da, kip)
            dki_parts.append(_dotg(da, qs, TN))
            qb = qd[:, sl].astype(BF16)
            kb = ke[:, sl].astype(BF16)
            dob = dov[:, sl].astype(BF16)
            g = g_ref[p]
            dqd_c, dv_c, dke_c, del_c = [], [], [], []
            for c in range(HG_NC - 1, -1, -1):
                rows = slice(HG_BLOCK * c, HG_BLOCK * (c + 1))
                gb = g.astype(BF16)
                st = _expand_state(sp_ref[c, :, sl], head64)
                dqd_c.append(_dot(dob[rows], st.astype(BF16)))
                dv_c.append(_dotg(kb[rows], gb, NT))
                dke_c.append(_dot(vp[rows], gb))
                del_c.append(jnp.broadcast_to(jnp.sum(g * st, axis=0, keepdims=True), (HG_BLOCK, GW)))
                g = g * el[HG_BLOCK * c:HG_BLOCK * c + 1, sl] + _dotg(dob[rows], qb[rows], TN) * bd_ref[...]
            g_ref[p] = g
            up = lambda parts: jnp.concatenate(parts[::-1], axis=0)
            dqd_parts.append(_unstack_heads(r, head, t) + up(dqd_c))
            dv_parts.append(_dotg(a, dos, TN) + up(dv_c))
            dke_parts.append(up(dke_c))
            del_parts.append(up(del_c))
        wide = lambda parts: jnp.concatenate(parts, axis=1)
        dqd, dke, dki, dvv, del_rows = wide(dqd_parts), wide(dke_parts), wide(dki_parts), wide(dv_parts), wide(del_parts)
        dh_ref[:, :HG_WIDTH] = (dqd * jnp.exp(b)).astype(BF16)
        dh_ref[:, 2 * HG_WIDTH:] = dvv.astype(BF16)
        dke_ke = dke * ke
        db = dqd * qd - dki * ki - dke_ke
        dl_rows = _sel_left(msum_ref[...], dke_ke) + del_rows * el
        is_last = (lax.broadcasted_iota(jnp.int32, (t, HG_WIDTH), 0) & (HG_BLOCK - 1)) == HG_BLOCK - 1
        db = db + jnp.where(is_last, dl_rows, 0.0)
        dlf = _sel_left(mrev_ref[...], db)
        dk = dki * jnp.exp(-b) + dke * jnp.exp(big_l - b)
        df = dlf / f - dk
        dh_ref[:, HG_WIDTH:2 * HG_WIDTH] = (df * (1.0 - lb) * sig * (1.0 - sig)).astype(BF16)
        dlb = jnp.sum(df * (1.0 - sig), axis=0, keepdims=True) * lb * (1.0 - lb)
        dlbl_ref[0:1, :] += dlb
        dlbl_ref[1:2, :] -= dlb

    rrow = lambda j: pl.BlockSpec((t, HG_WIDTH), lambda i: (nt - 1 - i, j))
    full = lambda a: pl.BlockSpec(a.shape, lambda i: (0, 0))
    pair_specs, dw_specs, dw_shapes, accs = [], [], [], []
    for name, at, b in pairs:
        pair_specs += [pl.BlockSpec((at.shape[0], t), lambda i: (0, i)), pl.BlockSpec((t, b.shape[1]), lambda i: (i, 0))]
        shape = _slot_shape(name, at.shape[0], b.shape[1])
        dw_specs.append(pl.BlockSpec(shape, lambda i: (0, 0, 0)))
        dw_shapes.append(jax.ShapeDtypeStruct(shape, BF16))
        accs.append(pltpu.VMEM((at.shape[0], b.shape[1]), F32))
    return pl.pallas_call(
        body,
        grid=(nt,),
        in_specs=[rrow(6), rrow(7), rrow(8), full(lbl), rrow(0),
                  pl.BlockSpec((HG_NC, 64, HG_WIDTH), lambda i: (nt - 1 - i, 0, 0)),
                  full(mcum), full(mrev), full(msum), full(bd), pl.BlockSpec(memory_space=pl.ANY)] + pair_specs,
        out_specs=[pl.BlockSpec((t, 3 * HG_WIDTH), lambda i: (nt - 1 - i, 2)),
                   pl.BlockSpec((2, HG_WIDTH), lambda i: (0, 0))] + dw_specs,
        out_shape=[jax.ShapeDtypeStruct(dproj.shape, BF16), jax.ShapeDtypeStruct((2, HG_WIDTH), F32)] + dw_shapes,
        input_output_aliases={10: 0},
        scratch_shapes=[pltpu.VMEM((HEADS // HG_G, GW, GW), F32)] + accs,
        compiler_params=_params(("arbitrary",)),
        name="hgrn_bwd",
    )(proj, proj, proj, lbl, do, sprev, mcum, mrev, msum, bd, dproj, *[a for pair in pairs for a in pair[1:]])


def _tail(x, tgt, proj, attn, o, w_a, w_b, w_out, w_at, w_bt, w_outt, b_gate, g_post, gh):
    s = x.shape[0]
    tm = 256
    ones64 = (jnp.arange(HG_WIDTH)[:, None] // 64 == jnp.arange(HG_WIDTH)[None, :] // 64).astype(BF16)
    weights = (w_a, w_b, w_out, w_at, w_bt, w_outt)

    def body(x_ref, t_ref, ml_ref, ga_ref, gb_ref, at_ref, o_ref, *rest):
        w_hbm, (bg_ref, gp_ref, gh_ref, ones_ref) = rest[:6], rest[6:10]
        (dout_ref, dpj_ref, dop_ref, do_ref, mt_ref, dy_ref, yat_ref, dya_ref, ybt_ref, dyb_ref,
         loss_ref, dgp_ref, dbg_ref, dgh_ref) = rest[10:24]
        (wa_ref, wb_ref, wo_ref, wat_ref, wbt_ref, wot_ref), w_sem = rest[24:30], rest[30]

        @pl.when(pl.program_id(0) == 0)
        def _():
            loads = [pltpu.make_async_copy(src, dst, w_sem.at[k])
                     for k, (src, dst) in enumerate(zip(w_hbm, rest[24:30]))]
            _start_all(loads, [])
            loss_ref[...] = jnp.zeros_like(loss_ref)
            dgp_ref[...] = jnp.zeros_like(dgp_ref)
            dbg_ref[...] = jnp.zeros_like(dbg_ref)
            dgh_ref[...] = jnp.zeros_like(dgh_ref)
            _wait_all(loads, [])

        ones = ones_ref[...]
        gate_a = ga_ref[...].astype(F32)
        sa = _sigmoid(gate_a)
        silu_a = gate_a * sa
        attn_v = at_ref[...]
        ya_in = attn_v * silu_a
        ov = o_ref[...]
        ro = lax.rsqrt(_sel_right(ov * ov, ones) * (1.0 / 64.0) + EPS)
        ohat = ov * ro
        ghv = gh_ref[...]
        on = ohat * ghv
        gate_b = gb_ref[...].astype(F32)
        sb = _sigmoid(gate_b)
        silu_b = gate_b * sb
        yb_in = on * silu_b
        ya_bf = ya_in.astype(BF16)
        yb_bf = yb_in.astype(BF16)
        yat_ref[...] = ya_bf.T
        ybt_ref[...] = yb_bf.T
        y_a = _dot(ya_bf, wa_ref[...])
        y_b = _dot(yb_bf, wb_ref[...])
        gts = _sigmoid(ml_ref[...].astype(F32) + bg_ref[...])
        g_a = gts[:, :D_MODEL]
        g_b = gts[:, D_MODEL:]
        m_bf = (g_a * y_a + g_b * y_b).astype(BF16)
        mt_ref[...] = m_bf.T
        y = _dot(m_bf, wo_ref[...])
        r1 = lax.rsqrt(jnp.mean(y * y, axis=-1, keepdims=True) + EPS)
        yn = y * r1
        gp = gp_ref[...]
        e = x_ref[...] + yn * gp - t_ref[...]
        loss_ref[...] += jnp.sum(e * e, axis=0, keepdims=True)
        dout = e * (1.0 / D_MODEL)
        dout_ref[...] = dout
        dgp_ref[...] += jnp.sum(dout * yn, axis=0, keepdims=True)
        dyn = dout * gp
        dy = r1 * (dyn - yn * jnp.mean(dyn * yn, axis=-1, keepdims=True))
        dy_bf = dy.astype(BF16)
        dy_ref[...] = dy_bf
        dm = _dot(dy_bf, wot_ref[...])
        dml_a = dm * y_a * g_a * (1.0 - g_a)
        dml_b = dm * y_b * g_b * (1.0 - g_b)
        dpj_ref[:, :D_MODEL] = dml_a.astype(BF16)
        dpj_ref[:, D_MODEL:2 * D_MODEL] = dml_b.astype(BF16)
        dbg_ref[:, :D_MODEL] += jnp.sum(dml_a, axis=0, keepdims=True)
        dbg_ref[:, D_MODEL:] += jnp.sum(dml_b, axis=0, keepdims=True)
        dya_bf = (dm * g_a).astype(BF16)
        dyb_bf = (dm * g_b).astype(BF16)
        dya_ref[...] = dya_bf
        dyb_ref[...] = dyb_bf
        dya_in = _dot(dya_bf, wat_ref[...])
        dyb_in = _dot(dyb_bf, wbt_ref[...])
        dattn = dya_in * silu_a
        delta = _sel_right(dattn * attn_v, ones)
        lane = lax.broadcasted_iota(jnp.int32, (tm, LANE), 1)
        for p in range(HEADS // 2):
            sl = slice(LANE * p, LANE * (p + 1))
            xs = (dattn[:, sl], pltpu.roll(dattn[:, sl], VDIM, 1))
            nds = (-pltpu.roll(delta[:, sl], VDIM, 1), -delta[:, sl])
            for a in range(2):
                hi, lo_part = _hi_lo(nds[a])
                blk = jnp.where(lane < VDIM, xs[a], jnp.where(lane == VDIM, hi, jnp.where(lane == VDIM + 1, lo_part, 0.0)))
                dop_ref[:, LANE * (2 * p + a):LANE * (2 * p + a + 1)] = blk.astype(BF16)
        dpj_ref[:, 2 * D_MODEL:2 * D_MODEL + HG_WIDTH] = (
            dya_in * attn_v * (sa * (1.0 + gate_a * (1.0 - sa)))).astype(BF16)
        don = dyb_in * silu_b
        dpj_ref[:, 2 * D_MODEL + HG_WIDTH:] = (dyb_in * on * (sb * (1.0 + gate_b * (1.0 - sb)))).astype(BF16)
        dgh_ref[...] += jnp.sum(don * ohat, axis=0, keepdims=True)
        dohat = don * ghv
        do_ref[...] = (ro * (dohat - ohat * (_sel_right(dohat * ohat, ones) * (1.0 / 64.0)))).astype(BF16)

    row = lambda w, j: pl.BlockSpec((tm, w), lambda i: (i, j))
    col = lambda w: pl.BlockSpec((w, tm), lambda i: (0, i))
    full = lambda a: pl.BlockSpec(a.shape, lambda i: (0, 0))
    acc = lambda w: pl.BlockSpec((1, w), lambda i: (0, 0))
    sds = lambda w, dt: jax.ShapeDtypeStruct((s, w), dt)
    sdt = lambda w: jax.ShapeDtypeStruct((w, s), BF16)
    return pl.pallas_call(
        body,
        grid=(s // tm,),
        in_specs=[row(1024, 0), row(1024, 0), row(2048, 0), row(512, 4), row(512, 5), row(512, 0), row(512, 0)]
        + [ANY] * 6 + [full(b_gate), full(g_post), full(gh), full(ones64)],
        out_specs=[row(1024, 0), row(3072, 0), row(1024, 0), row(512, 0),
                   col(1024), row(1024, 0), col(512), row(1024, 0), col(512), row(1024, 0),
                   acc(1024), acc(1024), acc(2048), acc(512)],
        out_shape=[sds(1024, F32), sds(D_IN_PAD, BF16), sds(1024, BF16), sds(512, BF16),
                   sdt(1024), sds(1024, BF16), sdt(512), sds(1024, BF16), sdt(512), sds(1024, BF16),
                   jax.ShapeDtypeStruct((1, 1024), F32), jax.ShapeDtypeStruct((1, 1024), F32),
                   jax.ShapeDtypeStruct((1, 2048), F32), jax.ShapeDtypeStruct((1, 512), F32)],
        scratch_shapes=[pltpu.VMEM(a.shape, BF16) for a in weights] + [pltpu.SemaphoreType.DMA((6,))],
        compiler_params=_params(("arbitrary",), 56),
        name="tail",
    )(x, tgt, proj, proj, proj, attn, o, *weights, b_gate, g_post, gh, ones64)


def _mla_bwd(proj, dqr, dkr, dv, g_q, g_kv, w_uq_pt, w_kv_pt, rc, rs1, rs2, cqt, ckvt, dproj):
    assert HEADS == N_DEV
    s = proj.shape[0]
    tm = 512
    scale = 1.0 / math.sqrt(QK)

    def body(cq_ref, ckv_ref, dqr_ref, dkr_ref, dv_ref, gq_ref, gkv_ref, wuqt_ref, wkvt_ref, c_ref, s1_ref, s2_ref,
             cqt_ref, ckvt_ref, dproj_in, dc_ref, dgq_ref, dgkv_ref, uq_slots, ukv_slots,
             dqf_ref, dkvf_ref, dwuq_ref, dwkv_ref):
        del dproj_in

        @pl.when(pl.program_id(0) == 0)
        def _():
            dgq_ref[...] = jnp.zeros_like(dgq_ref)
            dgkv_ref[...] = jnp.zeros_like(dgkv_ref)
            dwuq_ref[...] = jnp.zeros_like(dwuq_ref)
            dwkv_ref[...] = jnp.zeros_like(dwkv_ref)

        c, s1, s2 = c_ref[...], s1_ref[...], s2_ref[...]
        lane = lax.broadcasted_iota(jnp.int32, (tm, LANE), 1)
        ksum = jnp.zeros((tm, LANE), F32)
        for h in range(HEADS):
            sl = slice(LANE * h, LANE * (h + 1))
            dqf_ref[:, sl] = (_unrope(dqr_ref[:, sl], c, s1, s2) * scale).astype(BF16)
            dkh = dkr_ref[:, sl]
            ksum = ksum + dkh
            dkvf_ref[:, sl] = jnp.where(lane < NOPE, dkh, 0.0).astype(BF16)
            dkvf_ref[:, HEADS * LANE + LANE * h:HEADS * LANE + LANE * (h + 1)] = jnp.where(
                lane < VDIM, dv_ref[:, sl], 0.0).astype(BF16)
        dkpe = _unrope(ksum, c, s1, s2)
        dc_ref[:, Q_LORA + KV_LORA:] = jnp.where((lane >= NOPE) & (lane < QK), dkpe, 0.0).astype(BF16)
        dqf, dkvf = dqf_ref[...], dkvf_ref[...]
        dwuq_ref[...] += _dot(cqt_ref[...], dqf)
        dwkv_ref[...] += _dot(ckvt_ref[...], dkvf)
        dcqn = _dot(dqf, wuqt_ref[...])
        dckvn = _dot(dkvf, wkvt_ref[...])
        for x_ref, g_ref, dn, cols, dg_ref in ((cq_ref, gq_ref, dcqn, slice(0, Q_LORA), dgq_ref),
                                               (ckv_ref, gkv_ref, dckvn, slice(Q_LORA, Q_LORA + KV_LORA), dgkv_ref)):
            xv = x_ref[...].astype(F32)
            r = lax.rsqrt(jnp.mean(xv * xv, axis=-1, keepdims=True) + EPS)
            xh = xv * r
            dg_ref[...] += jnp.sum(dn * xh, axis=0, keepdims=True)
            dh = dn * g_ref[...]
            dc_ref[:, cols] = (r * (dh - xh * jnp.mean(dh * xh, axis=-1, keepdims=True))).astype(BF16)

        @pl.when(pl.program_id(0) == s // tm - 1)
        def _():
            ur = Q_LORA // N_DEV
            for p in range(N_DEV):
                uq_slots[p] = jnp.concatenate(
                    [dwuq_ref[ur * p:ur * (p + 1), LANE * h:LANE * h + QK] for h in range(HEADS)], axis=1).astype(BF16)
                ukv_slots[p] = jnp.concatenate(
                    [dwkv_ref[:, LANE * p:LANE * p + NOPE],
                     dwkv_ref[:, LANE * (HEADS + p):LANE * (HEADS + p) + VDIM]], axis=1).astype(BF16)

    row = lambda w, j: pl.BlockSpec((tm, w), lambda i: (i, j))
    full = lambda a: pl.BlockSpec(a.shape, lambda i: (0, 0))
    acc = lambda w: pl.BlockSpec((1, w), lambda i: (0, 0))
    col = lambda w: pl.BlockSpec((w, tm), lambda i: (0, i))
    whole = lambda shape: pl.BlockSpec(shape, lambda i: (0, 0, 0))
    uq_shape = (N_DEV, Q_LORA // N_DEV, HEADS * QK)
    ukv_shape = (N_DEV, KV_LORA, NOPE + VDIM)
    return pl.pallas_call(
        body,
        grid=(s // tm,),
        in_specs=[row(768, 6), row(256, 21), row(1024, 0), row(1024, 0), row(1024, 0), full(g_q), full(g_kv),
                  full(w_uq_pt), full(w_kv_pt), row(128, 0), row(128, 0), row(128, 0), col(Q_LORA), col(KV_LORA),
                  pl.BlockSpec(memory_space=pl.ANY)],
        out_specs=[row(1152, 4), acc(768), acc(256), whole(uq_shape), whole(ukv_shape)],
        out_shape=[jax.ShapeDtypeStruct(dproj.shape, BF16),
                   jax.ShapeDtypeStruct((1, 768), F32), jax.ShapeDtypeStruct((1, 256), F32),
                   jax.ShapeDtypeStruct(uq_shape, BF16), jax.ShapeDtypeStruct(ukv_shape, BF16)],
        input_output_aliases={14: 0},
        scratch_shapes=[pltpu.VMEM((tm, HEADS * LANE), BF16), pltpu.VMEM((tm, 2 * HEADS * LANE), BF16),
                        pltpu.VMEM((Q_LORA, HEADS * LANE), F32), pltpu.VMEM((KV_LORA, 2 * HEADS * LANE), F32)],
        compiler_params=_params(("arbitrary",)),
        name="mla_bwd",
    )(proj, proj, dqr, dkr, dv, g_q, g_kv, w_uq_pt, w_kv_pt, rc, rs1, rs2, cqt, ckvt, dproj)


def _dh_dx(dproj, w_in_pt, x, dout, g_pre, sends):
    s, k = dproj.shape
    tm = 256
    ns, ni = len(sends), s // tm

    def body(dp_ref, w_ref, x_ref, dout_ref, g_ref, *rest):
        send_refs, (dx_ref, dg_ref) = rest[:ns], rest[ns:ns + 2]
        recv_refs, sems = rest[ns + 2:2 * ns + 2], rest[2 * ns + 2:]

        @pl.when(pl.program_id(0) == 0)
        def _():
            _start_all(*_to_chips_copies(send_refs, recv_refs, sems))
            dg_ref[...] = jnp.zeros_like(dg_ref)

        dh = _dot(dp_ref[...], w_ref[...])
        xv = x_ref[...]
        r = lax.rsqrt(jnp.mean(xv * xv, axis=-1, keepdims=True) + EPS)
        xh = xv * r
        dg_ref[...] += jnp.sum(dh * xh, axis=0, keepdims=True)
        dxh = dh * g_ref[...]
        dx_ref[...] = dout_ref[...] + r * (dxh - xh * jnp.mean(dxh * xh, axis=-1, keepdims=True))

        @pl.when(pl.program_id(0) == ni - 1)
        def _():
            _wait_all(*_to_chips_copies(send_refs, recv_refs, sems))

    row = lambda w: pl.BlockSpec((tm, w), lambda i: (i, 0))
    return pl.pallas_call(
        body,
        grid=(ni,),
        in_specs=[row(k), pl.BlockSpec((k, D_MODEL), lambda i: (0, 0)), row(D_MODEL), row(D_MODEL),
                  pl.BlockSpec((1, D_MODEL), lambda i: (0, 0))] + [ANY] * ns,
        out_specs=[row(D_MODEL), pl.BlockSpec((1, D_MODEL), lambda i: (0, 0))] + [ANY] * ns,
        out_shape=[jax.ShapeDtypeStruct((s, D_MODEL), F32), jax.ShapeDtypeStruct((1, D_MODEL), F32)]
        + [jax.ShapeDtypeStruct(a.shape, a.dtype) for a in sends],
        scratch_shapes=_copy_sems(ns, 3),
        compiler_params=_params(("arbitrary",)),
        name="dh_dx",
    )(dproj, w_in_pt, x, dout, g_pre, *sends)


def _pair_reduce(slots):
    n = len(slots)
    half = [(N_DEV // 2,) + a.shape[1:] for a in slots]

    def body(*refs):
        s_refs, o_refs = refs[:n], refs[n:2 * n]
        mine, got = refs[2 * n:3 * n], refs[3 * n:4 * n]
        send_sems, recv_sems, local_sems = refs[4 * n:]
        x, y, c = _my_place()
        copies, loads = [], []
        for a in range(n):
            for q in range(N_DEV // 2):
                copies.append(pltpu.make_async_remote_copy(
                    src_ref=s_refs[a].at[2 * q + 1 - c], dst_ref=got[a].at[q],
                    send_sem=send_sems.at[4 * a + q], recv_sem=recv_sems.at[4 * a + q],
                    device_id=(x, y, 1 - c), device_id_type=MESH_ID))
                loads.append(pltpu.make_async_copy(s_refs[a].at[2 * q + c], mine[a].at[q], local_sems.at[4 * a + q]))
        _start_all(loads, copies)
        _wait_all(loads, copies)
        for a in range(n):
            o_refs[a][...] = (mine[a][...].astype(F32) + got[a][...].astype(F32)).astype(o_refs[a].dtype)

    vm = lambda: [pltpu.VMEM(h, a.dtype) for h, a in zip(half, slots)]
    return pl.pallas_call(
        body,
        in_specs=[ANY] * n,
        out_shape=[jax.ShapeDtypeStruct(h, a.dtype) for h, a in zip(half, slots)],
        scratch_shapes=vm() + vm() + [pltpu.SemaphoreType.DMA((4 * n,)), pltpu.SemaphoreType.DMA((4 * n,)),
                                      pltpu.SemaphoreType.DMA((4 * n,))],
        compiler_params=pltpu.CompilerParams(vmem_limit_bytes=48 * 2**20),
        name="pair_reduce",
    )(*slots)


def _rope_tables(s):
    inv = (np.float32(ROPE_THETA) ** (-np.arange(0, ROPE, 2, dtype=np.float32) / np.float32(ROPE))).astype(np.float32)
    ang = (np.arange(s, dtype=np.float32)[:, None] * inv[None, :]).astype(np.float32)
    cos, sin = jnp.asarray(np.cos(ang.astype(np.float64)), F32), jnp.asarray(np.sin(ang.astype(np.float64)), F32)
    z = lambda w: jnp.zeros((s, w), F32)
    rc = jnp.concatenate([jnp.ones((s, NOPE), F32), cos, cos, z(32)], axis=1)
    rs1 = jnp.concatenate([z(NOPE), -sin, z(16), z(32)], axis=1)
    rs2 = jnp.concatenate([z(NOPE), z(16), sin, z(32)], axis=1)
    return rc, rs1, rs2


def _step(x, tgt, w_blk, shards, g_pre, b_gate, g_q, g_kv, lbl, g_hgrn, g_post):
    s = x.shape[0]
    rc, rs1, rs2 = _rope_tables(s)
    gh = jnp.tile(g_hgrn, (1, HEADS))

    proj, w_in_pt, ht, *got = _gather_proj(x, g_pre, w_blk, shards[:2])
    w_uq, w_ukv = (_from_slots(n, g) for n, g in zip(MATS[:2], got))
    w_uq_p = jnp.pad(w_uq.reshape(Q_LORA, HEADS, QK), ((0, 0), (0, 0), (0, LANE - QK))).reshape(Q_LORA, HEADS * LANE)
    kv3 = w_ukv.reshape(KV_LORA, HEADS, NOPE + VDIM)
    pad64 = lambda t: jnp.pad(t, ((0, 0), (0, 0), (0, LANE - 64))).reshape(KV_LORA, HEADS * LANE)
    w_kv_p = jnp.concatenate([pad64(kv3[:, :, :NOPE]), pad64(kv3[:, :, NOPE:])], axis=1)

    qr, kr, v, cqt, ckvt = _mla_prep(proj, g_q, g_kv, w_uq_p, w_kv_p, rc, rs1, rs2)
    attn, qa, *got = _attn_fwd(qr, kr, v, shards[2:])
    w_a, w_b, w_out = (_from_slots(n, g) for n, g in zip(MATS[2:], got))
    o, sprev = _hgrn_fwd(proj, lbl)
    (dout, dproj, dop, do, mt, dy_bf, yat, dya_bf, ybt, dyb_bf,
     loss_vec, dg_post, db_gate, dgh) = _tail(x, tgt, proj, attn, o, w_a, w_b, w_out, w_a.T, w_b.T, w_out.T,
                                               b_gate, g_post, gh)
    dproj, dlbl, *early = _hgrn_bwd(proj, lbl, do, sprev, dproj,
                                    [("w_branch_a", yat, dya_bf), ("w_branch_b", ybt, dyb_bf), ("w_out", mt, dy_bf)])
    dqr, dkr, dv, *early_recv = _attn_bwd(qa, kr, v, dop, early)
    dproj, dg_q, dg_kv, dw_uq_slots, dw_ukv_slots = _mla_bwd(proj, dqr, dkr, dv, g_q, g_kv, w_uq_p.T, w_kv_p.T,
                                                             rc, rs1, rs2, cqt, ckvt, dproj)

    dw_in_slots = _dw_in_slots(ht, dproj)
    late = _pair_reduce([dw_in_slots, dw_uq_slots, dw_ukv_slots])
    dx, dg_pre, *late_recv = _dh_dx(dproj, w_in_pt, x, dout, g_pre, late)

    g_sum = _vectors_sum(dg_pre, db_gate, dg_q, dg_kv, dlbl, dgh, dg_post, loss_vec)
    return dx, late_recv[0], dict(zip(MATS, late_recv[1:] + early_recv)), g_sum


def _adamw(g, w, m, v):
    c1 = 1.0 / (1.0 - ADAM_B1 ** ADAM_STEP)
    c2 = 1.0 / (1.0 - ADAM_B2 ** ADAM_STEP)
    nm = ADAM_B1 * m + (1.0 - ADAM_B1) * g
    nv = ADAM_B2 * v + (1.0 - ADAM_B2) * (g * g)
    d = -ADAM_LR * ((nm * c1) / (jnp.sqrt(nv * c2) + ADAM_EPS) + ADAM_WD * w)
    return d, nm, nv


def _sum8(r_ref):
    g = r_ref[0].astype(F32)
    for k in range(1, r_ref.shape[0]):
        g = g + r_ref[k].astype(F32)
    return g


def _sum_adamw_w_in(recv, w, m, v):
    rows, _, cols = w.shape
    tc = 256
    nc = cols // tc

    def body(r_ref, w_hbm, m_hbm, v_hbm, g_hbm, d_hbm, nm_hbm, nv_hbm, ins, outs, in_sems, out_sems):
        i = pl.program_id(0)
        slot = i & 1
        cols_of = lambda step: pl.ds(pl.multiple_of(step * tc, tc), tc)

        def load(k, step, sl):
            return pltpu.make_async_copy((w_hbm, m_hbm, v_hbm)[k].at[:, 0, cols_of(step)], ins.at[sl, k],
                                         in_sems.at[sl, k])

        def store(k, step, sl):
            return pltpu.make_async_copy(outs.at[sl, k], (g_hbm, d_hbm, nm_hbm, nv_hbm)[k].at[:, 0, cols_of(step)],
                                         out_sems.at[sl, k])

        @pl.when(i == 0)
        def _():
            for k in range(3):
                load(k, 0, 0).start()

        @pl.when(i + 1 < nc)
        def _():
            for k in range(3):
                load(k, i + 1, 1 - slot).start()

        @pl.when(i >= 2)
        def _():
            for k in range(4):
                store(k, i - 2, slot).wait()

        for k in range(3):
            load(k, i, slot).wait()
        g = _sum8(r_ref)
        d, nm, nv = _adamw(g, ins[slot, 0], ins[slot, 1], ins[slot, 2])
        for k, val in enumerate((g, d, nm, nv)):
            outs[slot, k] = val
        for k in range(4):
            store(k, i, slot).start()

        @pl.when(i == nc - 1)
        def _():
            for k in range(4):
                store(k, i, slot).wait()
            if nc >= 2:
                for k in range(4):
                    store(k, i - 1, 1 - slot).wait()

    out = jax.ShapeDtypeStruct((rows, 1, cols), F32)
    return pl.pallas_call(
        body,
        grid=(nc,),
        in_specs=[pl.BlockSpec((recv.shape[0], rows, tc), lambda i: (0, 0, i)), ANY, ANY, ANY],
        out_specs=[ANY, ANY, ANY, ANY],
        out_shape=[out, out, out, out],
        scratch_shapes=[pltpu.VMEM((2, 3, rows, tc), F32), pltpu.VMEM((2, 4, rows, tc), F32),
                        pltpu.SemaphoreType.DMA((2, 3)), pltpu.SemaphoreType.DMA((2, 4))],
        compiler_params=_params(("arbitrary",)),
        name="sum_adamw_w_in",
    )(recv, w, m, v)


def _sum_adamw_whole(recvs, ws, ms, vs):
    n = len(ws)

    def body(*refs):
        r_refs, w_refs, m_refs, v_refs = refs[:n], refs[n:2 * n], refs[2 * n:3 * n], refs[3 * n:4 * n]
        outs = refs[4 * n:]
        for a in range(n):
            g = _sum8(r_refs[a])
            d, nm, nv = _adamw(g, w_refs[a][...], m_refs[a][...], v_refs[a][...])
            outs[a][...] = g
            outs[n + a][...] = d
            outs[2 * n + a][...] = nm
            outs[3 * n + a][...] = nv

    shapes = [jax.ShapeDtypeStruct(w.shape, F32) for w in ws]
    res = pl.pallas_call(
        body,
        out_shape=shapes * 4,
        compiler_params=pltpu.CompilerParams(vmem_limit_bytes=48 * 2**20),
        name="sum_adamw_mats",
    )(*recvs, *ws, *ms, *vs)
    return res[:n], res[n:2 * n], res[2 * n:3 * n], res[3 * n:]


SMALL = ("g_pre", "b_gate", "g_q", "g_kv", "lb_logits", "g_hgrn", "g_post")
SMALL_SHAPE = dict(g_pre=(1, 1024), b_gate=(1, 2048), g_q=(1, 768), g_kv=(1, 256), lb_logits=(2, 512),
                   g_hgrn=(1, 64), g_post=(1, 1024))


def _vectors_sum(dg_pre, db_gate, dg_q, dg_kv, dlbl, dgh, dg_post, loss_vec):
    def body(gpre_ref, bg_ref, gq_ref, gkv_ref, lbl_ref, gh_ref, gpost_ref, loss_ref, out_ref, mine, got,
             send_sems, recv_sems):
        mine[...] = jnp.zeros_like(mine)
        mine[0:1, :] = gpre_ref[...]
        mine[1:2, :] = bg_ref[:, :1024]
        mine[2:3, :] = bg_ref[:, 1024:]
        mine[3:4, :Q_LORA] = gq_ref[...]
        mine[4:5, :KV_LORA] = gkv_ref[...]
        loss = (0.5 / D_MODEL) * jnp.sum(loss_ref[...], axis=-1, keepdims=True)
        mine[4:5, KV_LORA:] = jnp.broadcast_to(loss, (1, 1024 - KV_LORA))
        mine[5:6, :HG_WIDTH] = lbl_ref[0:1, :]
        mine[5:6, HG_WIDTH:] = lbl_ref[1:2, :]
        gh = gh_ref[...]
        fold = gh[:, :VDIM]
        for h in range(1, HEADS):
            fold = fold + gh[:, VDIM * h:VDIM * (h + 1)]
        mine[6:7, :VDIM] = fold
        mine[7:8, :] = gpost_ref[...]
        x, y, c = _my_place()
        me = 4 * x + 2 * y + c
        got[me] = mine[...]
        copies = [pltpu.make_async_remote_copy(
            src_ref=mine, dst_ref=got.at[me], send_sem=send_sems.at[k], recv_sem=recv_sems.at[k],
            device_id=_flip(k, x, y, c), device_id_type=MESH_ID) for k in range(N_DEV - 1)]
        _start_all([], copies)
        _wait_all([], copies)
        out_ref[...] = _sum8(got)

    return pl.pallas_call(
        body,
        out_shape=jax.ShapeDtypeStruct((8, 1024), F32),
        scratch_shapes=[pltpu.VMEM((8, 1024), F32), pltpu.VMEM((N_DEV, 8, 1024), F32),
                        pltpu.SemaphoreType.DMA((7,)), pltpu.SemaphoreType.DMA((7,))],
        name="vectors_sum",
    )(dg_pre, db_gate, dg_q, dg_kv, dlbl, dgh, dg_post, loss_vec)


def _vectors_adamw(g_sum, ws, ms, vs):
    n = len(SMALL)

    def body(g_ref, *refs):
        w_refs, m_refs, v_refs = refs[:n], refs[n:2 * n], refs[2 * n:3 * n]
        loss_ref, outs = refs[3 * n], refs[3 * n + 1:]
        g = g_ref[...]
        loss_ref[...] = g[4:5, KV_LORA:KV_LORA + 1]
        grads = (g[0:1, :], jnp.concatenate([g[1:2, :], g[2:3, :]], axis=1), g[3:4, :Q_LORA], g[4:5, :KV_LORA],
                 jnp.concatenate([g[5:6, :HG_WIDTH], g[5:6, HG_WIDTH:]], axis=0), g[6:7, :VDIM], g[7:8, :])
        for a in range(n):
            d, nm, nv = _adamw(grads[a], w_refs[a][...], m_refs[a][...], v_refs[a][...])
            outs[a][...] = grads[a]
            outs[n + a][...] = d
            outs[2 * n + a][...] = nm
            outs[3 * n + a][...] = nv

    shapes = [jax.ShapeDtypeStruct(SMALL_SHAPE[k], F32) for k in SMALL]
    res = pl.pallas_call(
        body,
        out_shape=[jax.ShapeDtypeStruct((1, 1), F32)] + shapes * 4,
        name="vectors_adamw",
    )(g_sum, *ws, *ms, *vs)
    return res[0], res[1:n + 1], res[n + 1:2 * n + 1], res[2 * n + 1:3 * n + 1], res[3 * n + 1:]


MATS = ("w_uq", "w_ukv", "w_branch_a", "w_branch_b", "w_out")
COL_SHARDED = dict(w_uq=False, w_ukv=True, w_branch_a=True, w_branch_b=True, w_out=False)
ORDER = ("g_pre", "w_in", "b_gate", "g_q", "w_uq", "g_kv", "w_ukv", "lb_logits", "g_hgrn",
         "w_branch_a", "w_branch_b", "w_out", "g_post")


def _from_slots(name, slots):
    _, r, c = slots.shape
    if COL_SHARDED[name]:
        return slots.transpose(1, 0, 2).reshape(r, N_DEV * c)
    return slots.reshape(N_DEV * r, c)


def kernel(x, g_pre, w_in, b_gate, g_q, w_uq, g_kv, w_ukv, lb_logits, g_hgrn, w_branch_a, w_branch_b, w_out, g_post, loss_target, m_g_pre, m_w_in, m_b_gate, m_g_q, m_w_uq, m_g_kv, m_w_ukv, m_lb_logits, m_g_hgrn, m_w_branch_a, m_w_branch_b, m_w_out, m_g_post, v_g_pre, v_w_in, v_b_gate, v_g_q, v_w_uq, v_g_kv, v_w_ukv, v_lb_logits, v_g_hgrn, v_w_branch_a, v_w_branch_b, v_w_out, v_g_post):
    rows3 = lambda a: jnp.transpose(a, (2, 0, 1))
    w = dict(w_in=rows3(w_in), w_uq=w_uq[0], w_ukv=w_ukv[0], w_branch_a=w_branch_a[0], w_branch_b=w_branch_b[0],
             w_out=w_out[0], g_pre=g_pre, b_gate=b_gate, g_q=g_q, g_kv=g_kv, lb_logits=lb_logits, g_hgrn=g_hgrn,
             g_post=g_post)
    mom = dict(w_in=rows3(m_w_in), w_uq=m_w_uq[0], w_ukv=m_w_ukv[0], w_branch_a=m_w_branch_a[0],
               w_branch_b=m_w_branch_b[0], w_out=m_w_out[0], g_pre=m_g_pre, b_gate=m_b_gate, g_q=m_g_q, g_kv=m_g_kv,
               lb_logits=m_lb_logits, g_hgrn=m_g_hgrn, g_post=m_g_post)
    var = dict(w_in=rows3(v_w_in), w_uq=v_w_uq[0], w_ukv=v_w_ukv[0], w_branch_a=v_w_branch_a[0],
               w_branch_b=v_w_branch_b[0], w_out=v_w_out[0], g_pre=v_g_pre, b_gate=v_b_gate, g_q=v_g_q, g_kv=v_g_kv,
               lb_logits=v_lb_logits, g_hgrn=v_g_hgrn, g_post=v_g_post)

    w_blk = w["w_in"].reshape(W_IN_SHARD, D_MODEL).astype(BF16)
    dx, recv_in, recv, g_sum = _step(x[0], loss_target[0], w_blk, [w[n].astype(BF16) for n in MATS],
                                     g_pre, b_gate, g_q, g_kv, lb_logits, g_hgrn, g_post)

    g_in, d_in, m_in, v_in = _sum_adamw_w_in(recv_in, w["w_in"], mom["w_in"], var["w_in"])
    res = _sum_adamw_whole([recv[n] for n in MATS], *([t[n] for n in MATS] for t in (w, mom, var)))
    total, *vec = _vectors_adamw(g_sum, *([t[n] for n in SMALL] for t in (w, mom, var)))

    outs = []
    for mats, vecs, big in zip(res, vec, (g_in, d_in, m_in, v_in)):
        t = {**{n: a[None] for n, a in zip(MATS, mats)}, **dict(zip(SMALL, vecs)),
             "w_in": jnp.transpose(big, (1, 2, 0))}
        outs += [t[n] for n in ORDER]
    return (total.reshape(()), dx[None], *outs)
```

```python
import math

import jax
import jax.numpy as jnp
import numpy as np
from jax import lax
from jax.experimental import pallas as pl
from jax.experimental.pallas import tpu as pltpu

F32, BF16 = jnp.float32, jnp.bfloat16

D_MODEL = 1024
EPS = 1e-6
HEADS = 8
NOPE, ROPE, VDIM = 64, 32, 64
QK = NOPE + ROPE
Q_LORA, KV_LORA = 768, 256
ROPE_THETA = 10000.0
ATT_CHUNK_SHIFT = 6
HG_BLOCK = 32
HG_WIDTH = 512
D_IN = 5664
D_IN_PAD = 5760
W_IN_SHARD = D_IN // 8
N_DEV = 8
LANE = 128

ADAM_LR, ADAM_B1, ADAM_B2, ADAM_EPS, ADAM_WD, ADAM_STEP = 0.001, 0.9, 0.999, 1e-08, 0.01, 10

W_IN_SEGMENTS = ((3616, 5664, 0), (1056, 1568, 2048), (3104, 3616, 2560), (1568, 3104, 3072),
                 (0, 1024, 4608), (1024, 1056, 5696))

NT = (((1,), (1,)), ((), ()))
TN = (((0,), (0,)), ((), ()))
MESH_ID = pl.DeviceIdType.MESH


def _w_in_pieces():
    out = []
    for lo, hi, dst in W_IN_SEGMENTS:
        c = lo
        while c < hi:
            p = c // W_IN_SHARD
            e = min(hi, (p + 1) * W_IN_SHARD)
            out.append((p, c - p * W_IN_SHARD, e - p * W_IN_SHARD, dst + c - lo))
            c = e
    return out


def _params(sem, vmem_mb=48):
    return pltpu.CompilerParams(dimension_semantics=sem, vmem_limit_bytes=vmem_mb * 2**20)


def _dot(a, b):
    return jnp.dot(a, b, preferred_element_type=F32)


def _dotg(a, b, dims):
    return lax.dot_general(a, b, dims, preferred_element_type=F32)


def _split2(x):
    hi = x.astype(BF16)
    return hi, (x - hi.astype(F32)).astype(BF16)


def _sel_left(m01, x):
    hi, lo = _split2(x)
    return _dot(m01, hi) + _dot(m01, lo)


def _sel_right(x, m01):
    hi, lo = _split2(x)
    return _dot(hi, m01) + _dot(lo, m01)


def _hi_lo(x):
    hi = x.astype(BF16).astype(F32)
    return hi, x - hi


def _sigmoid(x):
    return 0.5 * jnp.tanh(0.5 * x) + 0.5


def _rope(x, c, s1, s2):
    return x * c + pltpu.roll(x, 112, 1) * s1 + pltpu.roll(x, 16, 1) * s2


def _unrope(d, c, s1, s2):
    return d * c + pltpu.roll(d * s1, 16, 1) + pltpu.roll(d * s2, 112, 1)


def _my_place():
    return lax.axis_index("x"), lax.axis_index("y"), lax.axis_index("c")


def _flip(k, x, y, c):
    fx, fy, fc = (k + 1) >> 2 & 1, (k + 1) >> 1 & 1, (k + 1) & 1
    return (1 - x if fx else x), (1 - y if fy else y), (1 - c if fc else c)


def _to_all_copies(s_refs, r_refs, sems, spread):
    send_sems, recv_sems, local_sems = sems
    x, y, c = _my_place()
    me = 4 * x + 2 * y + c
    src = (lambda a, p: s_refs[a]) if spread else (lambda a, p: s_refs[a].at[p])
    local = [pltpu.make_async_copy(src(a, me), r_refs[a].at[me], local_sems.at[a]) for a in range(len(s_refs))]
    remote = []
    for k in range(N_DEV - 1):
        px, py, pc = _flip(k, x, y, c)
        for a in range(len(s_refs)):
            remote.append(pltpu.make_async_remote_copy(
                src_ref=src(a, 4 * px + 2 * py + pc), dst_ref=r_refs[a].at[me],
                send_sem=send_sems.at[7 * a + k], recv_sem=recv_sems.at[7 * a + k],
                device_id=(px, py, pc), device_id_type=MESH_ID))
    return local, remote


def _to_chips_copies(s_refs, r_refs, sems):
    send_sems, recv_sems, local_sems = sems
    x, y, c = _my_place()
    me = 2 * x + y
    local = [pltpu.make_async_copy(s_refs[a].at[me], r_refs[a].at[me], local_sems.at[a]) for a in range(len(s_refs))]
    remote = []
    for k in range(3):
        px = 1 - x if (k + 1) >> 1 & 1 else x
        py = 1 - y if (k + 1) & 1 else y
        for a in range(len(s_refs)):
            remote.append(pltpu.make_async_remote_copy(
                src_ref=s_refs[a].at[2 * px + py], dst_ref=r_refs[a].at[me],
                send_sem=send_sems.at[3 * a + k], recv_sem=recv_sems.at[3 * a + k],
                device_id=(px, py, c), device_id_type=MESH_ID))
    return local, remote


def _start_all(local, remote):
    for cp in local + remote:
        cp.start()


def _wait_all(local, remote):
    for cp in remote:
        cp.wait_recv()
    for cp in remote:
        cp.wait_send()
    for cp in local:
        cp.wait()


def _copy_sems(n, peers):
    return [pltpu.SemaphoreType.DMA((peers * n,)), pltpu.SemaphoreType.DMA((peers * n,)),
            pltpu.SemaphoreType.DMA((n,))]


ANY = pl.BlockSpec(memory_space=pl.ANY)


def _dw_in_slots(ht, dproj):
    m, k = ht.shape
    n = dproj.shape[1]
    tn, tk = 1152, 1024
    nj, nk = n // tn, k // tk
    by_tile = [[] for _ in range(nj)]
    for p, lo, hi, dst in _w_in_pieces():
        while lo < hi:
            j = dst // tn
            cnt = min(hi - lo, (j + 1) * tn - dst)
            by_tile[j].append((p, lo, lo + cnt, dst - j * tn))
            lo, dst = lo + cnt, dst + cnt

    def body(a_ref, b_ref, s_ref, acc_ref):
        j, l = pl.program_id(0), pl.program_id(1)

        @pl.when(l == 0)
        def _():
            acc_ref[...] = jnp.zeros_like(acc_ref)

        acc_ref[...] += _dot(a_ref[...], b_ref[...])

        @pl.when(l == nk - 1)
        def _():
            at = acc_ref[...].T
            for jj in range(nj):
                @pl.when(j == jj)
                def _(jj=jj):
                    for p, lo, hi, d in by_tile[jj]:
                        s_ref[p, lo:hi, :] = at[d:d + hi - lo, :].astype(BF16)

    return pl.pallas_call(
        body,
        grid=(nj, nk),
        in_specs=[pl.BlockSpec((m, tk), lambda j, l: (0, l)), pl.BlockSpec((tk, tn), lambda j, l: (l, j))],
        out_specs=pl.BlockSpec((N_DEV, W_IN_SHARD, m), lambda j, l: (0, 0, 0)),
        out_shape=jax.ShapeDtypeStruct((N_DEV, W_IN_SHARD, m), BF16),
        scratch_shapes=[pltpu.VMEM((m, tn), F32)],
        compiler_params=_params(("arbitrary", "arbitrary")),
        name="dw_in",
    )(ht, dproj)


PROJ_DT = F32
GP_TN = 256
GP_COLS = 5888
GP_NT = GP_COLS // GP_TN


def _gp_tile_pieces():
    tiles = [[] for _ in range(GP_NT)]
    for p, lo, hi, dst in _w_in_pieces():
        while lo < hi:
            t = dst // GP_TN
            n = min(hi - lo, (t + 1) * GP_TN - dst)
            tiles[t].append((p, lo, lo + n, dst - t * GP_TN))
            lo, dst = lo + n, dst + n
    return tiles


def _gp_tables():
    pieces = _gp_tile_pieces()
    rank_of = {None: 0, 0: 1, 1: 2, 2: 2, 4: 3, 5: 3, 3: 4, 6: 5}
    order = np.zeros((N_DEV, GP_NT), np.int32)
    waits = np.zeros((N_DEV, GP_NT), np.int32)
    for me in range(N_DEV):
        x, y, c = me >> 2 & 1, me >> 1 & 1, me & 1
        chips = [(1 - x, y), (x, 1 - y), (1 - x, 1 - y)]

        def sem_of(p):
            px, py, pc = p >> 2 & 1, p >> 1 & 1, p & 1
            if (px, py) == (x, y):
                return None if pc == c else 0
            j = chips.index((px, py))
            return 1 + j if pc == c else 4 + j

        needs = [sorted({sem_of(p) for p, _, _, _ in tile} - {None}) for tile in pieces]
        ranks = [max([rank_of[k] for k in ks], default=0) for ks in needs]
        seq = sorted(range(GP_NT), key=lambda t: (ranks[t], t))
        seen = set()
        for step, t in enumerate(seq):
            order[me, step] = t
            new = [k for k in needs[t] if k not in seen]
            for k in new:
                waits[me, step] |= 1 << k
            seen.update(new)
        assert seen == set(range(7)), (me, seen)
    return order, waits


def _gather_proj(x, g_pre, w_blk, shards):
    s = x.shape[0]
    tx = 512
    ns = len(shards)
    tile_pieces = _gp_tile_pieces()
    order_np, waits_np = _gp_tables()
    xq, yq, cq = _my_place()
    me_out = 4 * xq + 2 * yq + cq
    order = lax.dynamic_index_in_dim(jnp.asarray(order_np), me_out, 0, keepdims=False)
    waits = lax.dynamic_index_in_dim(jnp.asarray(waits_np), me_out, 0, keepdims=False)

    def body(order_ref, waits_ref, x_hbm, g_ref, wblk_hbm, *rest):
        shard_refs, (proj_ref, wt_ref, ht_hbm), got_refs = rest[:ns], rest[ns:ns + 3], rest[ns + 3:2 * ns + 3]
        recv, h_ref, wtile, xbuf, htbuf = rest[2 * ns + 3:2 * ns + 8]
        send_sems, recv_sems, misc_sems = rest[2 * ns + 8:2 * ns + 11]
        sems = rest[2 * ns + 11:]
        t = pl.program_id(0)
        x_, y_, c = _my_place()
        sibling = (x_, y_, 1 - c)
        chips = [(1 - x_, y_), (x_, 1 - y_), (1 - x_, 1 - y_)]
        idx = lambda px, py, pc: 4 * px + 2 * py + pc
        me = idx(x_, y_, c)

        def copy(k, slot, to, src=None):
            return pltpu.make_async_remote_copy(
                src_ref=recv.at[slot] if src is None else src, dst_ref=recv.at[slot],
                send_sem=send_sems.at[k], recv_sem=recv_sems.at[k], device_id=to, device_id_type=MESH_ID)

        mine = pltpu.make_async_copy(wblk_hbm, recv.at[me], misc_sems.at[0])
        first = [copy(0, me, sibling, src=wblk_hbm)] + [copy(1 + j, me, (*chips[j], c), src=wblk_hbm) for j in range(2)]
        passed = [copy(4 + j, idx(*ch, c), sibling) for j, ch in enumerate(chips)]
        onward = [copy(3, idx(*chips[0], c), (*chips[1], c)), copy(3, idx(*chips[1], c), (*chips[0], c))]
        arrivals = ([copy(0, idx(x_, y_, 1 - c), sibling)] + [copy(1 + j, idx(*ch, c), sibling) for j, ch in enumerate(chips)]
                    + [copy(4 + j, idx(*ch, 1 - c), sibling) for j, ch in enumerate(chips)])

        @pl.when(t == 0)
        def _():
            mine.start()
            for cp in first:
                cp.start()
            _start_all(*_to_all_copies(shard_refs, got_refs, sems, True))

            def load(i):
                return pltpu.make_async_copy(x_hbm.at[pl.ds(i * tx, tx), :], xbuf.at[i & 1], misc_sems.at[1 + (i & 1)])

            def store(i):
                return pltpu.make_async_copy(htbuf.at[i & 1], ht_hbm.at[:, pl.ds(i * tx, tx)], misc_sems.at[3 + (i & 1)])

            load(0).start()
            for i in range(s // tx):
                if i + 1 < s // tx:
                    load(i + 1).start()
                load(i).wait()
                xv = xbuf[i & 1]
                r = lax.rsqrt(jnp.mean(xv * xv, axis=-1, keepdims=True) + EPS)
                h = (xv * r * g_ref[...]).astype(BF16)
                h_ref[i * tx:(i + 1) * tx, :] = h
                if i >= 2:
                    store(i - 2).wait()
                htbuf[i & 1] = h.T
                store(i).start()
            for i in range(max(s // tx - 2, 0), s // tx):
                store(i).wait()
            mine.wait()

        w = waits_ref[t]
        for k in range(7):
            @pl.when((w >> k) & 1 == 1)
            def _(k=k):
                arrivals[k].wait_recv()
                if 1 <= k <= 3:
                    passed[k - 1].start()
                if 1 <= k <= 2:
                    @pl.when(c == k - 1)
                    def _():
                        onward[k - 1].start()

        tile = order_ref[t]
        for tt in range(GP_NT):
            @pl.when(tile == tt)
            def _(tt=tt):
                covered = sorted((d, d + hi - lo) for _, lo, hi, d in tile_pieces[tt])
                at = 0
                for lo_z, hi_z in covered + [(GP_TN, GP_TN)]:
                    if lo_z > at:
                        wtile[at:lo_z, :] = jnp.zeros((lo_z - at, D_MODEL), BF16)
                    at = max(at, hi_z)
                for p, lo, hi, d in tile_pieces[tt]:
                    wtile[d:d + hi - lo, :] = recv[p, lo:hi, :]

        wt = wtile[...]
        wt_ref[...] = wt
        proj_ref[...] = _dotg(h_ref[...], wt, NT).astype(PROJ_DT)

        @pl.when(t == GP_NT - 1)
        def _():
            for cp in first + passed + onward[:1]:
                cp.wait_send()
            _wait_all(*_to_all_copies(shard_refs, got_refs, sems, True))

    grid_spec = pltpu.PrefetchScalarGridSpec(
        num_scalar_prefetch=2,
        grid=(GP_NT,),
        in_specs=[ANY, pl.BlockSpec((1, D_MODEL), lambda t, o, w: (0, 0)), ANY] + [ANY] * ns,
        out_specs=[pl.BlockSpec((s, GP_TN), lambda t, o, w: (0, o[t])),
                   pl.BlockSpec((GP_TN, D_MODEL), lambda t, o, w: (o[t], 0)), ANY] + [ANY] * ns,
        scratch_shapes=[pltpu.VMEM((N_DEV, W_IN_SHARD, D_MODEL), BF16), pltpu.VMEM((s, D_MODEL), BF16),
                        pltpu.VMEM((GP_TN, D_MODEL), BF16), pltpu.VMEM((2, tx, D_MODEL), F32),
                        pltpu.VMEM((2, D_MODEL, tx), BF16),
                        pltpu.SemaphoreType.DMA((7,)), pltpu.SemaphoreType.DMA((7,)), pltpu.SemaphoreType.DMA((5,))]
        + _copy_sems(ns, 7),
    )
    return pl.pallas_call(
        body,
        grid_spec=grid_spec,
        out_shape=[jax.ShapeDtypeStruct((s, GP_COLS), PROJ_DT), jax.ShapeDtypeStruct((GP_COLS, D_MODEL), BF16),
                   jax.ShapeDtypeStruct((D_MODEL, s), BF16)]
        + [jax.ShapeDtypeStruct((N_DEV,) + b.shape, b.dtype) for b in shards],
        compiler_params=_params(("arbitrary",), 56),
        name="gather_proj",
    )(order, waits, x, g_pre, w_blk, *shards)


def _mla_prep(proj, g_q, g_kv, w_uq_p, w_kv_p, rc, rs1, rs2):
    s = proj.shape[0]
    tm = 512
    scale = 1.0 / math.sqrt(QK)

    def body(cq_ref, ckv_ref, kpe_ref, gq_ref, gkv_ref, wuq_ref, wkv_ref, c_ref, s1_ref, s2_ref,
             qr_ref, kr_ref, v_ref, cqt_ref, ckvt_ref):
        cq = cq_ref[...].astype(F32)
        r = lax.rsqrt(jnp.mean(cq * cq, axis=-1, keepdims=True) + EPS)
        cqn = (cq * r * gq_ref[...]).astype(BF16)
        cqt_ref[...] = cqn.T
        q = _dot(cqn, wuq_ref[...])
        ckv = ckv_ref[...].astype(F32)
        r = lax.rsqrt(jnp.mean(ckv * ckv, axis=-1, keepdims=True) + EPS)
        ckvn = (ckv * r * gkv_ref[...]).astype(BF16)
        ckvt_ref[...] = ckvn.T
        kv = _dot(ckvn, wkv_ref[...])
        c, s1, s2 = c_ref[...], s1_ref[...], s2_ref[...]
        lane = lax.broadcasted_iota(jnp.int32, (tm, LANE), 1)
        kpe = _rope(kpe_ref[...].astype(F32), c, s1, s2) + jnp.where((lane == QK) | (lane == QK + 1), 1.0, 0.0)
        vone = jnp.where((lane == VDIM) | (lane == VDIM + 1), 1.0, 0.0)
        for h in range(HEADS):
            sl = slice(LANE * h, LANE * (h + 1))
            qr_ref[:, sl] = (_rope(q[:, sl], c, s1, s2) * scale).astype(BF16)
            kr_ref[:, sl] = (kv[:, sl] + kpe).astype(BF16)
            v_ref[:, sl] = (kv[:, HEADS * LANE + LANE * h:HEADS * LANE + LANE * (h + 1)] + vone).astype(BF16)

    row = lambda w, j: pl.BlockSpec((tm, w), lambda i: (i, j))
    col = lambda w: pl.BlockSpec((w, tm), lambda i: (0, i))
    full = lambda a: pl.BlockSpec(a.shape, lambda i: (0, 0))
    return pl.pallas_call(
        body,
        grid=(s // tm,),
        in_specs=[row(768, 6), row(256, 21), row(128, 44), full(g_q), full(g_kv), full(w_uq_p), full(w_kv_p),
                  row(128, 0), row(128, 0), row(128, 0)],
        out_specs=[row(1024, 0), row(1024, 0), row(1024, 0), col(768), col(256)],
        out_shape=[jax.ShapeDtypeStruct((s, 1024), BF16), jax.ShapeDtypeStruct((s, 1024), BF16),
                   jax.ShapeDtypeStruct((s, 1024), BF16), jax.ShapeDtypeStruct((768, s), BF16),
                   jax.ShapeDtypeStruct((256, s), BF16)],
        compiler_params=_params(("arbitrary",)),
        name="mla_prep",
    )(proj, proj, proj, g_q, g_kv, w_uq_p, w_kv_p, rc, rs1, rs2)


ATT_T = 512
ATT_FWD_HEADS = 4


def _chunk_mask(transposed):
    r = lax.broadcasted_iota(jnp.int32, (ATT_T, ATT_T), 0) >> ATT_CHUNK_SHIFT
    c = lax.broadcasted_iota(jnp.int32, (ATT_T, ATT_T), 1) >> ATT_CHUNK_SHIFT
    return (r <= c) if transposed else (c <= r)


def _attn_fwd(qr, kr, vp, shards):
    s = qr.shape[0]
    t = ATT_T
    g = ATT_FWD_HEADS
    ns = len(shards)

    def body(q_ref, k_ref, v_ref, *rest):
        shard_refs, (o_ref, qa_ref), got_refs = rest[:ns], rest[ns:ns + 2], rest[ns + 2:2 * ns + 2]
        sc_ref, sems = rest[2 * ns + 2], rest[2 * ns + 3:]
        qi = pl.program_id(1)

        @pl.when((pl.program_id(0) == 0) & (qi == 0))
        def _():
            _start_all(*_to_all_copies(shard_refs, got_refs, sems, True))
        lane = lax.broadcasted_iota(jnp.int32, (t, LANE), 1)
        sls = [slice(LANE * a, LANE * (a + 1)) for a in range(g)]
        qs = [q_ref[:, sl] for sl in sls]

        def scores(j):
            rows = pl.ds(pl.multiple_of(j * t, t), t)
            for a in range(g):
                sc_ref[j & 1, a] = _dotg(qs[a], k_ref[rows, sls[a]], NT)

        def step(j, carry, masked):
            rows = pl.ds(pl.multiple_of(j * t, t), t)
            out = []
            for a in range(g):
                m, acc = carry[a]
                sc = sc_ref[j & 1, a]
                if masked:
                    sc = jnp.where(_chunk_mask(False), sc, -1e30)
                m_new = jnp.maximum(m, jnp.max(sc, axis=-1, keepdims=True))
                p = jnp.exp(sc - m_new).astype(BF16)
                acc = jnp.exp(m - m_new) * acc + _dot(p, v_ref[rows, sls[a]])
                out.append((m_new, acc))
            return tuple(out)

        def loop(j, carry):
            carry = step(j, carry, False)
            scores(j + 1)
            return carry

        init = tuple((jnp.full((t, 1), -1e30, F32), jnp.zeros((t, LANE), F32)) for _ in range(g))
        scores(0)
        carry = lax.fori_loop(0, qi, loop, init)
        carry = step(qi, carry, True)
        outs = []
        for a in range(g):
            m, acc = carry[a]
            l = acc[:, VDIM:VDIM + 1]
            outs.append(acc / l)
            hi, lo_part = _hi_lo(-(m + jnp.log(l)))
            qa = jnp.where(lane == QK, hi, jnp.where(lane == QK + 1, lo_part, qs[a].astype(F32)))
            qa_ref[:, sls[a]] = qa.astype(BF16)
        for p in range(g // 2):
            o_ref[:, LANE * p:LANE * (p + 1)] = jnp.where(lane < VDIM, outs[2 * p], pltpu.roll(outs[2 * p + 1], VDIM, 1))

        @pl.when((pl.program_id(0) == HEADS // g - 1) & (qi == s // t - 1))
        def _():
            _wait_all(*_to_all_copies(shard_refs, got_refs, sems, True))

    return pl.pallas_call(
        body,
        grid=(HEADS // g, s // t),
        in_specs=[
            pl.BlockSpec((t, g * LANE), lambda h, i: (i, h)),
            pl.BlockSpec((s, g * LANE), lambda h, i: (0, h)),
            pl.BlockSpec((s, g * LANE), lambda h, i: (0, h)),
        ] + [ANY] * ns,
        out_specs=[
            pl.BlockSpec((t, g * VDIM), lambda h, i: (i, h)),
            pl.BlockSpec((t, g * LANE), lambda h, i: (i, h)),
        ] + [ANY] * ns,
        out_shape=[jax.ShapeDtypeStruct((s, 512), F32), jax.ShapeDtypeStruct((s, 1024), BF16)]
        + [jax.ShapeDtypeStruct((N_DEV,) + b.shape, b.dtype) for b in shards],
        scratch_shapes=[pltpu.VMEM((2, g, t, t), F32)] + _copy_sems(ns, 7),
        compiler_params=_params(("arbitrary", "arbitrary")),
        name="attn_fwd",
    )(qr, kr, vp, *shards)


def _attn_bwd(qa, kr, vp, dop, sends):
    s = qa.shape[0]
    t = ATT_T
    nq = s // t
    ns = len(sends)

    def body(q_ref, k_ref, v_ref, do_ref, *rest):
        send_refs, (dq_out, dk_out, dv_out) = rest[:ns], rest[ns:ns + 3]
        recv_refs = rest[ns + 3:2 * ns + 3]
        (dq_ref, dk_ref, dv_ref), sems = rest[2 * ns + 3:2 * ns + 6], rest[2 * ns + 6:]
        j = pl.program_id(1)
        sls = [slice(LANE * a, LANE * (a + 1)) for a in range(2)]

        @pl.when((pl.program_id(0) == 0) & (j == 0))
        def _():
            _start_all(*_to_all_copies(send_refs, recv_refs, sems, False))

        @pl.when(j == 0)
        def _():
            dq_ref[...] = jnp.zeros_like(dq_ref)

        dk_ref[...] = jnp.zeros_like(dk_ref)
        dv_ref[...] = jnp.zeros_like(dv_ref)
        ks = [k_ref[:, sl] for sl in sls]
        vs = [v_ref[:, sl] for sl in sls]

        def part(i, k_lo, k_n, q_lo, q_n, masked):
            rows = pl.ds(pl.multiple_of(i * t + q_lo, 256), q_n)
            keys = slice(k_lo, k_lo + k_n)
            for a in range(2):
                q = q_ref[rows, sls[a]]
                do = do_ref[rows, sls[a]]
                sc = _dotg(ks[a][keys], q, NT)
                if masked:
                    kc = lax.broadcasted_iota(jnp.int32, (k_n, q_n), 0) >> ATT_CHUNK_SHIFT
                    qc = lax.broadcasted_iota(jnp.int32, (k_n, q_n), 1) >> ATT_CHUNK_SHIFT
                    sc = jnp.where(kc <= qc, sc, -1e30)
                p = jnp.exp(sc)
                ds = (p * _dotg(vs[a][keys], do, NT)).astype(BF16)
                dv_ref[keys, sls[a]] += _dot(p.astype(BF16), do)
                dk_ref[keys, sls[a]] += _dot(ds, q)
                dq_ref[rows, sls[a]] += _dotg(ds, ks[a][keys], TN)

        half = t // 2
        part(j, 0, half, 0, t, True)
        part(j, half, half, half, half, True)

        def loop(i, c):
            part(i, 0, t, 0, t, False)
            return c

        lax.fori_loop(j + 1, nq, loop, 0)
        dk_out[...] = dk_ref[...].astype(BF16)
        dv_out[...] = dv_ref[...].astype(BF16)

        @pl.when(j == nq - 1)
        def _():
            dq_out[...] = dq_ref[...].astype(BF16)

        @pl.when((pl.program_id(0) == HEADS // 2 - 1) & (j == nq - 1))
        def _():
            _wait_all(*_to_all_copies(send_refs, recv_refs, sems, False))

    blk = pl.BlockSpec((t, 2 * LANE), lambda h, j: (j, h))
    whole = pl.BlockSpec((s, 2 * LANE), lambda h, j: (0, h))
    out = jax.ShapeDtypeStruct((s, 1024), BF16)
    return pl.pallas_call(
        body,
        grid=(HEADS // 2, nq),
        in_specs=[whole, blk, blk, whole] + [ANY] * ns,
        out_specs=[whole, blk, blk] + [ANY] * ns,
        out_shape=[out, out, out] + [jax.ShapeDtypeStruct(a.shape, a.dtype) for a in sends],
        scratch_shapes=[pltpu.VMEM((s, 2 * LANE), F32), pltpu.VMEM((t, 2 * LANE), F32),
                        pltpu.VMEM((t, 2 * LANE), F32)] + _copy_sems(ns, 7),
        compiler_params=_params(("arbitrary", "arbitrary")),
        name="attn_bwd",
    )(qa, kr, vp, dop, *sends)


HG_T = 256
HG_NC = HG_T // HG_BLOCK
HG_G = 4
GW = 64 * HG_G


def _hg_consts():
    r = jnp.arange(HG_T)[:, None]
    c = jnp.arange(HG_T)[None, :]
    same = (r // HG_BLOCK) == (c // HG_BLOCK)
    mcum = (same & (c <= r)).astype(BF16)
    mrev = (same & (c >= r)).astype(BF16)
    msum = same.astype(BF16)
    a = jnp.arange(GW) // 64
    bd = (a[:, None] == a[None, :]).astype(F32)
    return mcum, mrev, msum, bd


def _stack_heads(xg, head):
    return jnp.concatenate([jnp.where(head == h, xg, 0.0) for h in range(HG_G)], axis=0)


def _unstack_heads(r, head, t):
    out = r[(HG_G - 1) * t:]
    for h in range(HG_G - 2, -1, -1):
        out = jnp.where(head == h, r[h * t:(h + 1) * t], out)
    return out


def _compact_state(st):
    out = st[:64]
    for h in range(1, HG_G):
        out = out + st[64 * h:64 * (h + 1)]
    return out


def _expand_state(cs, head64):
    return jnp.concatenate([jnp.where(head64 == h, cs, 0.0) for h in range(HG_G)], axis=0)


def _hg_pre(hq, hf, lbl, mcum, msum):
    lb = _sigmoid(lbl[0:1, :] - lbl[1:2, :])
    sig = _sigmoid(hf)
    f = lb + (1.0 - lb) * sig
    lf = jnp.log(f)
    b = _sel_left(mcum, lf)
    big_l = _sel_left(msum, lf)
    k = 1.0 - f
    qd = hq * jnp.exp(b)
    ki = k * jnp.exp(-b)
    ke = k * jnp.exp(big_l - b)
    return lb, sig, f, b, big_l, qd, ki, ke


def _hgrn_fwd(proj, lbl):
    s = proj.shape[0]
    t = HG_T
    mcum, _, msum, bd = _hg_consts()

    def body(hq_ref, hf_ref, hi_ref, lbl_ref, mcum_ref, msum_ref, bd_ref, o_ref, sp_ref, st_ref):
        @pl.when(pl.program_id(0) == 0)
        def _():
            st_ref[...] = jnp.zeros_like(st_ref)

        mc = mcum_ref[...]
        _, _, _, _, big_l, qd, ki, ke = _hg_pre(hq_ref[...].astype(F32), hf_ref[...].astype(F32), lbl_ref[...], mc,
                                                msum_ref[...])
        el = jnp.exp(big_l)
        hi = hi_ref[...]
        head = lax.broadcasted_iota(jnp.int32, (t, GW), 1) >> 6
        mask = jnp.concatenate([mc] * HG_G, axis=0) > 0.5
        for p in range(HEADS // HG_G):
            sl = slice(GW * p, GW * (p + 1))
            vp = hi[:, sl].astype(BF16)
            qs = _stack_heads(qd[:, sl], head).astype(BF16)
            a = jnp.where(mask, _dotg(qs, ki[:, sl].astype(BF16), NT), 0.0)
            o_intra = _unstack_heads(_dot(a.astype(BF16), vp), head, t)
            qb = qd[:, sl].astype(BF16)
            kb = ke[:, sl].astype(BF16)
            st = st_ref[p]
            for c in range(HG_NC):
                rows = slice(HG_BLOCK * c, HG_BLOCK * (c + 1))
                sp_ref[c, :, sl] = _compact_state(st)
                o_ref[rows, sl] = o_intra[rows] + _dotg(qb[rows], st.astype(BF16), NT)
                u = _dotg(vp[rows], kb[rows], TN) * bd_ref[...]
                st = st * el[HG_BLOCK * c:HG_BLOCK * c + 1, sl] + u
            st_ref[p] = st

    row = lambda j: pl.BlockSpec((t, HG_WIDTH), lambda i: (i, j))
    full = lambda a: pl.BlockSpec(a.shape, lambda i: (0, 0))
    return pl.pallas_call(
        body,
        grid=(s // t,),
        in_specs=[row(6), row(7), row(8), full(lbl), full(mcum), full(msum), full(bd)],
        out_specs=[row(0), pl.BlockSpec((HG_NC, 64, HG_WIDTH), lambda i: (i, 0, 0))],
        out_shape=[jax.ShapeDtypeStruct((s, HG_WIDTH), F32),
                   jax.ShapeDtypeStruct((s // HG_BLOCK, 64, HG_WIDTH), F32)],
        scratch_shapes=[pltpu.VMEM((HEADS // HG_G, GW, GW), F32)],
        compiler_params=_params(("arbitrary",)),
        name="hgrn_fwd",
    )(proj, proj, proj, lbl, mcum, msum, bd)


def _slot_shape(name, r, c):
    return (N_DEV, r, c // N_DEV) if COL_SHARDED[name] else (N_DEV, r // N_DEV, c)


def _emit_slots(name, acc_ref, out_ref):
    r, c = acc_ref.shape
    for p in range(N_DEV):
        if COL_SHARDED[name]:
            out_ref[p] = acc_ref[:, c // N_DEV * p:c // N_DEV * (p + 1)].astype(BF16)
        else:
            out_ref[p] = acc_ref[r // N_DEV * p:r // N_DEV * (p + 1), :].astype(BF16)


def _hgrn_bwd(proj, lbl, do, sprev, dproj, pairs):
    s = proj.shape[0]
    t = HG_T
    nt = s // t
    npair = len(pairs)
    mcum, mrev, msum, bd = _hg_consts()

    def body(hq_ref, hf_ref, hi_ref, lbl_ref, do_ref, sp_ref, mcum_ref, mrev_ref, msum_ref, bd_ref,
             dproj_in, *rest):
        del dproj_in
        pair_refs, (dh_ref, dlbl_ref) = rest[:2 * npair], rest[2 * npair:2 * npair + 2]
        dw_refs, g_ref, acc_refs = rest[2 * npair + 2:3 * npair + 2], rest[3 * npair + 2], rest[3 * npair + 3:]

        @pl.when(pl.program_id(0) == 0)
        def _():
            g_ref[...] = jnp.zeros_like(g_ref)
            dlbl_ref[...] = jnp.zeros_like(dlbl_ref)
            for acc_ref in acc_refs:
                acc_ref[...] = jnp.zeros_like(acc_ref)

        for n, acc_ref in enumerate(acc_refs):
            acc_ref[...] += _dot(pair_refs[2 * n][...], pair_refs[2 * n + 1][...])

        @pl.when(pl.program_id(0) == nt - 1)
        def _():
            for (name, _, _), acc_ref, dw_ref in zip(pairs, acc_refs, dw_refs):
                _emit_slots(name, acc_ref, dw_ref)

        mc = mcum_ref[...]
        lb, sig, f, b, big_l, qd, ki, ke = _hg_pre(hq_ref[...].astype(F32), hf_ref[...].astype(F32), lbl_ref[...], mc,
                                                   msum_ref[...])
        el = jnp.exp(big_l)
        hi = hi_ref[...]
        dov = do_ref[...]
        head = lax.broadcasted_iota(jnp.int32, (t, GW), 1) >> 6
        head64 = lax.broadcasted_iota(jnp.int32, (64, GW), 1) >> 6
        mask = jnp.concatenate([mc] * HG_G, axis=0) > 0.5
        dqd_parts, dke_parts, dv_parts, del_parts, dki_parts = [], [], [], [], []
        for p in range(HEADS // HG_G):
            sl = slice(GW * p, GW * (p + 1))
            vp = hi[:, sl].astype(BF16)
            qs = _stack_heads(qd[:, sl], head).astype(BF16)
            kip = ki[:, sl].astype(BF16)
            dos = _stack_heads(dov[:, sl], head).astype(BF16)
            a = jnp.where(mask, _dotg(qs, kip, NT), 0.0).astype(BF16)
            da = jnp.where(mask, _dotg(dos, vp, NT), 0.0).astype(BF16)
            r = _dot(da, kip)
            dki_parts.append(_dotg(da, qs, TN))
            qb = qd[:, sl].astype(BF16)
            kb = ke[:, sl].astype(BF16)
            dob = dov[:, sl].astype(BF16)
            g = g_ref[p]
            dqd_c, dv_c, dke_c, del_c = [], [], [], []
            for c in range(HG_NC - 1, -1, -1):
                rows = slice(HG_BLOCK * c, HG_BLOCK * (c + 1))
                gb = g.astype(BF16)
                st = _expand_state(sp_ref[c, :, sl], head64)
                dqd_c.append(_dot(dob[rows], st.astype(BF16)))
                dv_c.append(_dotg(kb[rows], gb, NT))
                dke_c.append(_dot(vp[rows], gb))
                del_c.append(jnp.broadcast_to(jnp.sum(g * st, axis=0, keepdims=True), (HG_BLOCK, GW)))
                g = g * el[HG_BLOCK * c:HG_BLOCK * c + 1, sl] + _dotg(dob[rows], qb[rows], TN) * bd_ref[...]
            g_ref[p] = g
            up = lambda parts: jnp.concatenate(parts[::-1], axis=0)
            dqd_parts.append(_unstack_heads(r, head, t) + up(dqd_c))
            dv_parts.append(_dotg(a, dos, TN) + up(dv_c))
            dke_parts.append(up(dke_c))
            del_parts.append(up(del_c))
        wide = lambda parts: jnp.concatenate(parts, axis=1)
        dqd, dke, dki, dvv, del_rows = wide(dqd_parts), wide(dke_parts), wide(dki_parts), wide(dv_parts), wide(del_parts)
        dh_ref[:, :HG_WIDTH] = (dqd * jnp.exp(b)).astype(BF16)
        dh_ref[:, 2 * HG_WIDTH:] = dvv.astype(BF16)
        dke_ke = dke * ke
        db = dqd * qd - dki * ki - dke_ke
        dl_rows = _sel_left(msum_ref[...], dke_ke) + del_rows * el
        is_last = (lax.broadcasted_iota(jnp.int32, (t, HG_WIDTH), 0) & (HG_BLOCK - 1)) == HG_BLOCK - 1
        db = db + jnp.where(is_last, dl_rows, 0.0)
        dlf = _sel_left(mrev_ref[...], db)
        dk = dki * jnp.exp(-b) + dke * jnp.exp(big_l - b)
        df = dlf / f - dk
        dh_ref[:, HG_WIDTH:2 * HG_WIDTH] = (df * (1.0 - lb) * sig * (1.0 - sig)).astype(BF16)
        dlb = jnp.sum(df * (1.0 - sig), axis=0, keepdims=True) * lb * (1.0 - lb)
        dlbl_ref[0:1, :] += dlb
        dlbl_ref[1:2, :] -= dlb

    rrow = lambda j: pl.BlockSpec((t, HG_WIDTH), lambda i: (nt - 1 - i, j))
    full = lambda a: pl.BlockSpec(a.shape, lambda i: (0, 0))
    pair_specs, dw_specs, dw_shapes, accs = [], [], [], []
    for name, at, b in pairs:
        pair_specs += [pl.BlockSpec((at.shape[0], t), lambda i: (0, i)), pl.BlockSpec((t, b.shape[1]), lambda i: (i, 0))]
        shape = _slot_shape(name, at.shape[0], b.shape[1])
        dw_specs.append(pl.BlockSpec(shape, lambda i: (0, 0, 0)))
        dw_shapes.append(jax.ShapeDtypeStruct(shape, BF16))
        accs.append(pltpu.VMEM((at.shape[0], b.shape[1]), F32))
    return pl.pallas_call(
        body,
        grid=(nt,),
        in_specs=[rrow(6), rrow(7), rrow(8), full(lbl), rrow(0),
                  pl.BlockSpec((HG_NC, 64, HG_WIDTH), lambda i: (nt - 1 - i, 0, 0)),
                  full(mcum), full(mrev), full(msum), full(bd), pl.BlockSpec(memory_space=pl.ANY)] + pair_specs,
        out_specs=[pl.BlockSpec((t, 3 * HG_WIDTH), lambda i: (nt - 1 - i, 2)),
                   pl.BlockSpec((2, HG_WIDTH), lambda i: (0, 0))] + dw_specs,
        out_shape=[jax.ShapeDtypeStruct(dproj.shape, BF16), jax.ShapeDtypeStruct((2, HG_WIDTH), F32)] + dw_shapes,
        input_output_aliases={10: 0},
        scratch_shapes=[pltpu.VMEM((HEADS // HG_G, GW, GW), F32)] + accs,
        compiler_params=_params(("arbitrary",)),
        name="hgrn_bwd",
    )(proj, proj, proj, lbl, do, sprev, mcum, mrev, msum, bd, dproj, *[a for pair in pairs for a in pair[1:]])


def _tail(x, tgt, proj, attn, o, w_a, w_b, w_out, w_at, w_bt, w_outt, b_gate, g_post, gh):
    s = x.shape[0]
    tm = 256
    ones64 = (jnp.arange(HG_WIDTH)[:, None] // 64 == jnp.arange(HG_WIDTH)[None, :] // 64).astype(BF16)
    weights = (w_a, w_b, w_out, w_at, w_bt, w_outt)

    def body(x_ref, t_ref, ml_ref, ga_ref, gb_ref, at_ref, o_ref, *rest):
        w_hbm, (bg_ref, gp_ref, gh_ref, ones_ref) = rest[:6], rest[6:10]
        (dout_ref, dpj_ref, dop_ref, do_ref, mt_ref, dy_ref, yat_ref, dya_ref, ybt_ref, dyb_ref,
         loss_ref, dgp_ref, dbg_ref, dgh_ref) = rest[10:24]
        (wa_ref, wb_ref, wo_ref, wat_ref, wbt_ref, wot_ref), w_sem = rest[24:30], rest[30]

        @pl.when(pl.program_id(0) == 0)
        def _():
            loads = [pltpu.make_async_copy(src, dst, w_sem.at[k])
                     for k, (src, dst) in enumerate(zip(w_hbm, rest[24:30]))]
            _start_all(loads, [])
            loss_ref[...] = jnp.zeros_like(loss_ref)
            dgp_ref[...] = jnp.zeros_like(dgp_ref)
            dbg_ref[...] = jnp.zeros_like(dbg_ref)
            dgh_ref[...] = jnp.zeros_like(dgh_ref)
            _wait_all(loads, [])

        ones = ones_ref[...]
        gate_a = ga_ref[...].astype(F32)
        sa = _sigmoid(gate_a)
        silu_a = gate_a * sa
        attn_v = at_ref[...]
        ya_in = attn_v * silu_a
        ov = o_ref[...]
        ro = lax.rsqrt(_sel_right(ov * ov, ones) * (1.0 / 64.0) + EPS)
        ohat = ov * ro
        ghv = gh_ref[...]
        on = ohat * ghv
        gate_b = gb_ref[...].astype(F32)
        sb = _sigmoid(gate_b)
        silu_b = gate_b * sb
        yb_in = on * silu_b
        ya_bf = ya_in.astype(BF16)
        yb_bf = yb_in.astype(BF16)
        yat_ref[...] = ya_bf.T
        ybt_ref[...] = yb_bf.T
        y_a = _dot(ya_bf, wa_ref[...])
        y_b = _dot(yb_bf, wb_ref[...])
        gts = _sigmoid(ml_ref[...].astype(F32) + bg_ref[...])
        g_a = gts[:, :D_MODEL]
        g_b = gts[:, D_MODEL:]
        m_bf = (g_a * y_a + g_b * y_b).astype(BF16)
        mt_ref[...] = m_bf.T
        y = _dot(m_bf, wo_ref[...])
        r1 = lax.rsqrt(jnp.mean(y * y, axis=-1, keepdims=True) + EPS)
        yn = y * r1
        gp = gp_ref[...]
        e = x_ref[...] + yn * gp - t_ref[...]
        loss_ref[...] += jnp.sum(e * e, axis=0, keepdims=True)
        dout = e * (1.0 / D_MODEL)
        dout_ref[...] = dout
        dgp_ref[...] += jnp.sum(dout * yn, axis=0, keepdims=True)
        dyn = dout * gp
        dy = r1 * (dyn - yn * jnp.mean(dyn * yn, axis=-1, keepdims=True))
        dy_bf = dy.astype(BF16)
        dy_ref[...] = dy_bf
        dm = _dot(dy_bf, wot_ref[...])
        dml_a = dm * y_a * g_a * (1.0 - g_a)
        dml_b = dm * y_b * g_b * (1.0 - g_b)
        dpj_ref[:, :D_MODEL] = dml_a.astype(BF16)
        dpj_ref[:, D_MODEL:2 * D_MODEL] = dml_b.astype(BF16)
        dbg_ref[:, :D_MODEL] += jnp.sum(dml_a, axis=0, keepdims=True)
        dbg_ref[:, D_MODEL:] += jnp.sum(dml_b, axis=0, keepdims=True)
        dya_bf = (dm * g_a).astype(BF16)
        dyb_bf = (dm * g_b).astype(BF16)
        dya_ref[...] = dya_bf
        dyb_ref[...] = dyb_bf
        dya_in = _dot(dya_bf, wat_ref[...])
        dyb_in = _dot(dyb_bf, wbt_ref[...])
        dattn = dya_in * silu_a
        delta = _sel_right(dattn * attn_v, ones)
        lane = lax.broadcasted_iota(jnp.int32, (tm, LANE), 1)
        for p in range(HEADS // 2):
            sl = slice(LANE * p, LANE * (p + 1))
            xs = (dattn[:, sl], pltpu.roll(dattn[:, sl], VDIM, 1))
            nds = (-pltpu.roll(delta[:, sl], VDIM, 1), -delta[:, sl])
            for a in range(2):
                hi, lo_part = _hi_lo(nds[a])
                blk = jnp.where(lane < VDIM, xs[a], jnp.where(lane == VDIM, hi, jnp.where(lane == VDIM + 1, lo_part, 0.0)))
                dop_ref[:, LANE * (2 * p + a):LANE * (2 * p + a + 1)] = blk.astype(BF16)
        dpj_ref[:, 2 * D_MODEL:2 * D_MODEL + HG_WIDTH] = (
            dya_in * attn_v * (sa * (1.0 + gate_a * (1.0 - sa)))).astype(BF16)
        don = dyb_in * silu_b
        dpj_ref[:, 2 * D_MODEL + HG_WIDTH:] = (dyb_in * on * (sb * (1.0 + gate_b * (1.0 - sb)))).astype(BF16)
        dgh_ref[...] += jnp.sum(don * ohat, axis=0, keepdims=True)
        dohat = don * ghv
        do_ref[...] = (ro * (dohat - ohat * (_sel_right(dohat * ohat, ones) * (1.0 / 64.0)))).astype(BF16)

    row = lambda w, j: pl.BlockSpec((tm, w), lambda i: (i, j))
    col = lambda w: pl.BlockSpec((w, tm), lambda i: (0, i))
    full = lambda a: pl.BlockSpec(a.shape, lambda i: (0, 0))
    acc = lambda w: pl.BlockSpec((1, w), lambda i: (0, 0))
    sds = lambda w, dt: jax.ShapeDtypeStruct((s, w), dt)
    sdt = lambda w: jax.ShapeDtypeStruct((w, s), BF16)
    return pl.pallas_call(
        body,
        grid=(s // tm,),
        in_specs=[row(1024, 0), row(1024, 0), row(2048, 0), row(512, 4), row(512, 5), row(512, 0), row(512, 0)]
        + [ANY] * 6 + [full(b_gate), full(g_post), full(gh), full(ones64)],
        out_specs=[row(1024, 0), row(3072, 0), row(1024, 0), row(512, 0),
                   col(1024), row(1024, 0), col(512), row(1024, 0), col(512), row(1024, 0),
                   acc(1024), acc(1024), acc(2048), acc(512)],
        out_shape=[sds(1024, F32), sds(D_IN_PAD, BF16), sds(1024, BF16), sds(512, BF16),
                   sdt(1024), sds(1024, BF16), sdt(512), sds(1024, BF16), sdt(512), sds(1024, BF16),
                   jax.ShapeDtypeStruct((1, 1024), F32), jax.ShapeDtypeStruct((1, 1024), F32),
                   jax.ShapeDtypeStruct((1, 2048), F32), jax.ShapeDtypeStruct((1, 512), F32)],
        scratch_shapes=[pltpu.VMEM(a.shape, BF16) for a in weights] + [pltpu.SemaphoreType.DMA((6,))],
        compiler_params=_params(("arbitrary",), 56),
        name="tail",
    )(x, tgt, proj, proj, proj, attn, o, *weights, b_gate, g_post, gh, ones64)


def _mla_bwd(proj, dqr, dkr, dv, g_q, g_kv, w_uq_pt, w_kv_pt, rc, rs1, rs2, cqt, ckvt, dproj):
    assert HEADS == N_DEV
    s = proj.shape[0]
    tm = 512
    scale = 1.0 / math.sqrt(QK)

    def body(cq_ref, ckv_ref, dqr_ref, dkr_ref, dv_ref, gq_ref, gkv_ref, wuqt_ref, wkvt_ref, c_ref, s1_ref, s2_ref,
             cqt_ref, ckvt_ref, dproj_in, dc_ref, dgq_ref, dgkv_ref, uq_slots, ukv_slots,
             dqf_ref, dkvf_ref, dwuq_ref, dwkv_ref):
        del dproj_in

        @pl.when(pl.program_id(0) == 0)
        def _():
            dgq_ref[...] = jnp.zeros_like(dgq_ref)
            dgkv_ref[...] = jnp.zeros_like(dgkv_ref)
            dwuq_ref[...] = jnp.zeros_like(dwuq_ref)
            dwkv_ref[...] = jnp.zeros_like(dwkv_ref)

        c, s1, s2 = c_ref[...], s1_ref[...], s2_ref[...]
        lane = lax.broadcasted_iota(jnp.int32, (tm, LANE), 1)
        ksum = jnp.zeros((tm, LANE), F32)
        for h in range(HEADS):
            sl = slice(LANE * h, LANE * (h + 1))
            dqf_ref[:, sl] = (_unrope(dqr_ref[:, sl], c, s1, s2) * scale).astype(BF16)
            dkh = dkr_ref[:, sl]
            ksum = ksum + dkh
            dkvf_ref[:, sl] = jnp.where(lane < NOPE, dkh, 0.0).astype(BF16)
            dkvf_ref[:, HEADS * LANE + LANE * h:HEADS * LANE + LANE * (h + 1)] = jnp.where(
                lane < VDIM, dv_ref[:, sl], 0.0).astype(BF16)
        dkpe = _unrope(ksum, c, s1, s2)
        dc_ref[:, Q_LORA + KV_LORA:] = jnp.where((lane >= NOPE) & (lane < QK), dkpe, 0.0).astype(BF16)
        dqf, dkvf = dqf_ref[...], dkvf_ref[...]
        dwuq_ref[...] += _dot(cqt_ref[...], dqf)
        dwkv_ref[...] += _dot(ckvt_ref[...], dkvf)
        dcqn = _dot(dqf, wuqt_ref[...])
        dckvn = _dot(dkvf, wkvt_ref[...])
        for x_ref, g_ref, dn, cols, dg_ref in ((cq_ref, gq_ref, dcqn, slice(0, Q_LORA), dgq_ref),
                                               (ckv_ref, gkv_ref, dckvn, slice(Q_LORA, Q_LORA + KV_LORA), dgkv_ref)):
            xv = x_ref[...].astype(F32)
            r = lax.rsqrt(jnp.mean(xv * xv, axis=-1, keepdims=True) + EPS)
            xh = xv * r
            dg_ref[...] += jnp.sum(dn * xh, axis=0, keepdims=True)
            dh = dn * g_ref[...]
            dc_ref[:, cols] = (r * (dh - xh * jnp.mean(dh * xh, axis=-1, keepdims=True))).astype(BF16)

        @pl.when(pl.program_id(0) == s // tm - 1)
        def _():
            ur = Q_LORA // N_DEV
            for p in range(N_DEV):
                uq_slots[p] = jnp.concatenate(
                    [dwuq_ref[ur * p:ur * (p + 1), LANE * h:LANE * h + QK] for h in range(HEADS)], axis=1).astype(BF16)
                ukv_slots[p] = jnp.concatenate(
                    [dwkv_ref[:, LANE * p:LANE * p + NOPE],
                     dwkv_ref[:, LANE * (HEADS + p):LANE * (HEADS + p) + VDIM]], axis=1).astype(BF16)

    row = lambda w, j: pl.BlockSpec((tm, w), lambda i: (i, j))
    full = lambda a: pl.BlockSpec(a.shape, lambda i: (0, 0))
    acc = lambda w: pl.BlockSpec((1, w), lambda i: (0, 0))
    col = lambda w: pl.BlockSpec((w, tm), lambda i: (0, i))
    whole = lambda shape: pl.BlockSpec(shape, lambda i: (0, 0, 0))
    uq_shape = (N_DEV, Q_LORA // N_DEV, HEADS * QK)
    ukv_shape = (N_DEV, KV_LORA, NOPE + VDIM)
    return pl.pallas_call(
        body,
        grid=(s // tm,),
        in_specs=[row(768, 6), row(256, 21), row(1024, 0), row(1024, 0), row(1024, 0), full(g_q), full(g_kv),
                  full(w_uq_pt), full(w_kv_pt), row(128, 0), row(128, 0), row(128, 0), col(Q_LORA), col(KV_LORA),
                  pl.BlockSpec(memory_space=pl.ANY)],
        out_specs=[row(1152, 4), acc(768), acc(256), whole(uq_shape), whole(ukv_shape)],
        out_shape=[jax.ShapeDtypeStruct(dproj.shape, BF16),
                   jax.ShapeDtypeStruct((1, 768), F32), jax.ShapeDtypeStruct((1, 256), F32),
                   jax.ShapeDtypeStruct(uq_shape, BF16), jax.ShapeDtypeStruct(ukv_shape, BF16)],
        input_output_aliases={14: 0},
        scratch_shapes=[pltpu.VMEM((tm, HEADS * LANE), BF16), pltpu.VMEM((tm, 2 * HEADS * LANE), BF16),
                        pltpu.VMEM((Q_LORA, HEADS * LANE), F32), pltpu.VMEM((KV_LORA, 2 * HEADS * LANE), F32)],
        compiler_params=_params(("arbitrary",)),
        name="mla_bwd",
    )(proj, proj, dqr, dkr, dv, g_q, g_kv, w_uq_pt, w_kv_pt, rc, rs1, rs2, cqt, ckvt, dproj)


def _dh_dx(dproj, w_in_pt, x, dout, g_pre, sends):
    s, k = dproj.shape
    tm = 512
    ns, ni = len(sends), s // tm

    def body(dp_ref, w_ref, x_ref, dout_ref, g_ref, *rest):
        send_refs, (dx_ref, dg_ref) = rest[:ns], rest[ns:ns + 2]
        recv_refs, sems = rest[ns + 2:2 * ns + 2], rest[2 * ns + 2:]

        @pl.when(pl.program_id(0) == 0)
        def _():
            _start_all(*_to_chips_copies(send_refs, recv_refs, sems))
            dg_ref[...] = jnp.zeros_like(dg_ref)

        dh = _dot(dp_ref[...], w_ref[...])
        xv = x_ref[...]
        r = lax.rsqrt(jnp.mean(xv * xv, axis=-1, keepdims=True) + EPS)
        xh = xv * r
        dg_ref[...] += jnp.sum(dh * xh, axis=0, keepdims=True)
        dxh = dh * g_ref[...]
        dx_ref[...] = dout_ref[...] + r * (dxh - xh * jnp.mean(dxh * xh, axis=-1, keepdims=True))

        @pl.when(pl.program_id(0) == ni - 1)
        def _():
            _wait_all(*_to_chips_copies(send_refs, recv_refs, sems))

    row = lambda w: pl.BlockSpec((tm, w), lambda i: (i, 0))
    return pl.pallas_call(
        body,
        grid=(ni,),
        in_specs=[row(k), pl.BlockSpec((k, D_MODEL), lambda i: (0, 0), pipeline_mode=pl.Buffered(1)),
                  row(D_MODEL), row(D_MODEL), pl.BlockSpec((1, D_MODEL), lambda i: (0, 0))] + [ANY] * ns,
        out_specs=[row(D_MODEL), pl.BlockSpec((1, D_MODEL), lambda i: (0, 0))] + [ANY] * ns,
        out_shape=[jax.ShapeDtypeStruct((s, D_MODEL), F32), jax.ShapeDtypeStruct((1, D_MODEL), F32)]
        + [jax.ShapeDtypeStruct(a.shape, a.dtype) for a in sends],
        scratch_shapes=_copy_sems(ns, 3),
        compiler_params=_params(("arbitrary",), 56),
        name="dh_dx",
    )(dproj, w_in_pt, x, dout, g_pre, *sends)


def _pair_reduce(slots):
    n = len(slots)
    half = [(N_DEV // 2,) + a.shape[1:] for a in slots]

    def body(*refs):
        s_refs, o_refs = refs[:n], refs[n:2 * n]
        mine, got = refs[2 * n:3 * n], refs[3 * n:4 * n]
        send_sems, recv_sems, local_sems = refs[4 * n:]
        x, y, c = _my_place()
        copies, loads = [], []
        for a in range(n):
            for q in range(N_DEV // 2):
                copies.append(pltpu.make_async_remote_copy(
                    src_ref=s_refs[a].at[2 * q + 1 - c], dst_ref=got[a].at[q],
                    send_sem=send_sems.at[4 * a + q], recv_sem=recv_sems.at[4 * a + q],
                    device_id=(x, y, 1 - c), device_id_type=MESH_ID))
                loads.append(pltpu.make_async_copy(s_refs[a].at[2 * q + c], mine[a].at[q], local_sems.at[4 * a + q]))
        _start_all(loads, copies)
        _wait_all(loads, copies)
        for a in range(n):
            o_refs[a][...] = (mine[a][...].astype(F32) + got[a][...].astype(F32)).astype(o_refs[a].dtype)

    vm = lambda: [pltpu.VMEM(h, a.dtype) for h, a in zip(half, slots)]
    return pl.pallas_call(
        body,
        in_specs=[ANY] * n,
        out_shape=[jax.ShapeDtypeStruct(h, a.dtype) for h, a in zip(half, slots)],
        scratch_shapes=vm() + vm() + [pltpu.SemaphoreType.DMA((4 * n,)), pltpu.SemaphoreType.DMA((4 * n,)),
                                      pltpu.SemaphoreType.DMA((4 * n,))],
        compiler_params=pltpu.CompilerParams(vmem_limit_bytes=48 * 2**20),
        name="pair_reduce",
    )(*slots)


def _rope_tables(s):
    inv = (np.float32(ROPE_THETA) ** (-np.arange(0, ROPE, 2, dtype=np.float32) / np.float32(ROPE))).astype(np.float32)
    ang = (np.arange(s, dtype=np.float32)[:, None] * inv[None, :]).astype(np.float32)
    cos, sin = jnp.asarray(np.cos(ang.astype(np.float64)), F32), jnp.asarray(np.sin(ang.astype(np.float64)), F32)
    z = lambda w: jnp.zeros((s, w), F32)
    rc = jnp.concatenate([jnp.ones((s, NOPE), F32), cos, cos, z(32)], axis=1)
    rs1 = jnp.concatenate([z(NOPE), -sin, z(16), z(32)], axis=1)
    rs2 = jnp.concatenate([z(NOPE), z(16), sin, z(32)], axis=1)
    return rc, rs1, rs2


def _step(x, tgt, w_blk, shards, g_pre, b_gate, g_q, g_kv, lbl, g_hgrn, g_post):
    s = x.shape[0]
    rc, rs1, rs2 = _rope_tables(s)
    gh = jnp.tile(g_hgrn, (1, HEADS))

    proj, w_in_pt, ht, *got = _gather_proj(x, g_pre, w_blk, shards[:2])
    w_uq, w_ukv = (_from_slots(n, g) for n, g in zip(MATS[:2], got))
    w_uq_p = jnp.pad(w_uq.reshape(Q_LORA, HEADS, QK), ((0, 0), (0, 0), (0, LANE - QK))).reshape(Q_LORA, HEADS * LANE)
    kv3 = w_ukv.reshape(KV_LORA, HEADS, NOPE + VDIM)
    pad64 = lambda t: jnp.pad(t, ((0, 0), (0, 0), (0, LANE - 64))).reshape(KV_LORA, HEADS * LANE)
    w_kv_p = jnp.concatenate([pad64(kv3[:, :, :NOPE]), pad64(kv3[:, :, NOPE:])], axis=1)

    qr, kr, v, cqt, ckvt = _mla_prep(proj, g_q, g_kv, w_uq_p, w_kv_p, rc, rs1, rs2)
    attn, qa, *got = _attn_fwd(qr, kr, v, shards[2:])
    w_a, w_b, w_out = (_from_slots(n, g) for n, g in zip(MATS[2:], got))
    o, sprev = _hgrn_fwd(proj, lbl)
    (dout, dproj, dop, do, mt, dy_bf, yat, dya_bf, ybt, dyb_bf,
     loss_vec, dg_post, db_gate, dgh) = _tail(x, tgt, proj, attn, o, w_a, w_b, w_out, w_a.T, w_b.T, w_out.T,
                                               b_gate, g_post, gh)
    dproj, dlbl, *early = _hgrn_bwd(proj, lbl, do, sprev, dproj,
                                    [("w_branch_a", yat, dya_bf), ("w_branch_b", ybt, dyb_bf), ("w_out", mt, dy_bf)])
    dqr, dkr, dv, *early_recv = _attn_bwd(qa, kr, v, dop, early)
    dproj, dg_q, dg_kv, dw_uq_slots, dw_ukv_slots = _mla_bwd(proj, dqr, dkr, dv, g_q, g_kv, w_uq_p.T, w_kv_p.T,
                                                             rc, rs1, rs2, cqt, ckvt, dproj)

    dw_in_slots = _dw_in_slots(ht, dproj)
    late = _pair_reduce([dw_in_slots, dw_uq_slots, dw_ukv_slots])
    dx, dg_pre, *late_recv = _dh_dx(dproj, w_in_pt, x, dout, g_pre, late)

    g_sum = _vectors_sum(dg_pre, db_gate, dg_q, dg_kv, dlbl, dgh, dg_post, loss_vec)
    return dx, late_recv[0], dict(zip(MATS, late_recv[1:] + early_recv)), g_sum


def _adamw(g, w, m, v):
    c1 = 1.0 / (1.0 - ADAM_B1 ** ADAM_STEP)
    c2 = 1.0 / (1.0 - ADAM_B2 ** ADAM_STEP)
    nm = ADAM_B1 * m + (1.0 - ADAM_B1) * g
    nv = ADAM_B2 * v + (1.0 - ADAM_B2) * (g * g)
    d = -ADAM_LR * ((nm * c1) / (jnp.sqrt(nv * c2) + ADAM_EPS) + ADAM_WD * w)
    return d, nm, nv


def _sum8(r_ref):
    g = r_ref[0].astype(F32)
    for k in range(1, r_ref.shape[0]):
        g = g + r_ref[k].astype(F32)
    return g


def _sum_adamw_w_in(recv, w, m, v):
    rows, _, cols = w.shape
    tc = 256
    nc = cols // tc

    def body(r_ref, w_hbm, m_hbm, v_hbm, g_hbm, d_hbm, nm_hbm, nv_hbm, ins, outs, in_sems, out_sems):
        i = pl.program_id(0)
        slot = i & 1
        cols_of = lambda step: pl.ds(pl.multiple_of(step * tc, tc), tc)

        def load(k, step, sl):
            return pltpu.make_async_copy((w_hbm, m_hbm, v_hbm)[k].at[:, 0, cols_of(step)], ins.at[sl, k],
                                         in_sems.at[sl, k])

        def store(k, step, sl):
            return pltpu.make_async_copy(outs.at[sl, k], (g_hbm, d_hbm, nm_hbm, nv_hbm)[k].at[:, 0, cols_of(step)],
                                         out_sems.at[sl, k])

        @pl.when(i == 0)
        def _():
            for k in range(3):
                load(k, 0, 0).start()

        @pl.when(i + 1 < nc)
        def _():
            for k in range(3):
                load(k, i + 1, 1 - slot).start()

        @pl.when(i >= 2)
        def _():
            for k in range(4):
                store(k, i - 2, slot).wait()

        for k in range(3):
            load(k, i, slot).wait()
        g = _sum8(r_ref)
        d, nm, nv = _adamw(g, ins[slot, 0], ins[slot, 1], ins[slot, 2])
        for k, val in enumerate((g, d, nm, nv)):
            outs[slot, k] = val
        for k in range(4):
            store(k, i, slot).start()

        @pl.when(i == nc - 1)
        def _():
            for k in range(4):
                store(k, i, slot).wait()
            if nc >= 2:
                for k in range(4):
                    store(k, i - 1, 1 - slot).wait()

    out = jax.ShapeDtypeStruct((rows, 1, cols), F32)
    return pl.pallas_call(
        body,
        grid=(nc,),
        in_specs=[pl.BlockSpec((recv.shape[0], rows, tc), lambda i: (0, 0, i)), ANY, ANY, ANY],
        out_specs=[ANY, ANY, ANY, ANY],
        out_shape=[out, out, out, out],
        scratch_shapes=[pltpu.VMEM((2, 3, rows, tc), F32), pltpu.VMEM((2, 4, rows, tc), F32),
                        pltpu.SemaphoreType.DMA((2, 3)), pltpu.SemaphoreType.DMA((2, 4))],
        compiler_params=_params(("arbitrary",)),
        name="sum_adamw_w_in",
    )(recv, w, m, v)


def _sum_adamw_whole(recvs, ws, ms, vs):
    n = len(ws)

    def body(*refs):
        r_refs, w_refs, m_refs, v_refs = refs[:n], refs[n:2 * n], refs[2 * n:3 * n], refs[3 * n:4 * n]
        outs = refs[4 * n:]
        for a in range(n):
            g = _sum8(r_refs[a])
            d, nm, nv = _adamw(g, w_refs[a][...], m_refs[a][...], v_refs[a][...])
            outs[a][...] = g
            outs[n + a][...] = d
            outs[2 * n + a][...] = nm
            outs[3 * n + a][...] = nv

    shapes = [jax.ShapeDtypeStruct(w.shape, F32) for w in ws]
    res = pl.pallas_call(
        body,
        out_shape=shapes * 4,
        compiler_params=pltpu.CompilerParams(vmem_limit_bytes=48 * 2**20),
        name="sum_adamw_mats",
    )(*recvs, *ws, *ms, *vs)
    return res[:n], res[n:2 * n], res[2 * n:3 * n], res[3 * n:]


SMALL = ("g_pre", "b_gate", "g_q", "g_kv", "lb_logits", "g_hgrn", "g_post")
SMALL_SHAPE = dict(g_pre=(1, 1024), b_gate=(1, 2048), g_q=(1, 768), g_kv=(1, 256), lb_logits=(2, 512),
                   g_hgrn=(1, 64), g_post=(1, 1024))


def _vectors_sum(dg_pre, db_gate, dg_q, dg_kv, dlbl, dgh, dg_post, loss_vec):
    def body(gpre_ref, bg_ref, gq_ref, gkv_ref, lbl_ref, gh_ref, gpost_ref, loss_ref, out_ref, mine, got,
             send_sems, recv_sems):
        mine[...] = jnp.zeros_like(mine)
        mine[0:1, :] = gpre_ref[...]
        mine[1:2, :] = bg_ref[:, :1024]
        mine[2:3, :] = bg_ref[:, 1024:]
        mine[3:4, :Q_LORA] = gq_ref[...]
        mine[4:5, :KV_LORA] = gkv_ref[...]
        loss = (0.5 / D_MODEL) * jnp.sum(loss_ref[...], axis=-1, keepdims=True)
        mine[4:5, KV_LORA:] = jnp.broadcast_to(loss, (1, 1024 - KV_LORA))
        mine[5:6, :HG_WIDTH] = lbl_ref[0:1, :]
        mine[5:6, HG_WIDTH:] = lbl_ref[1:2, :]
        gh = gh_ref[...]
        fold = gh[:, :VDIM]
        for h in range(1, HEADS):
            fold = fold + gh[:, VDIM * h:VDIM * (h + 1)]
        mine[6:7, :VDIM] = fold
        mine[7:8, :] = gpost_ref[...]
        x, y, c = _my_place()
        me = 4 * x + 2 * y + c
        got[me] = mine[...]
        copies = [pltpu.make_async_remote_copy(
            src_ref=mine, dst_ref=got.at[me], send_sem=send_sems.at[k], recv_sem=recv_sems.at[k],
            device_id=_flip(k, x, y, c), device_id_type=MESH_ID) for k in range(N_DEV - 1)]
        _start_all([], copies)
        _wait_all([], copies)
        out_ref[...] = _sum8(got)

    return pl.pallas_call(
        body,
        out_shape=jax.ShapeDtypeStruct((8, 1024), F32),
        scratch_shapes=[pltpu.VMEM((8, 1024), F32), pltpu.VMEM((N_DEV, 8, 1024), F32),
                        pltpu.SemaphoreType.DMA((7,)), pltpu.SemaphoreType.DMA((7,))],
        name="vectors_sum",
    )(dg_pre, db_gate, dg_q, dg_kv, dlbl, dgh, dg_post, loss_vec)


def _vectors_adamw(g_sum, ws, ms, vs):
    n = len(SMALL)

    def body(g_ref, *refs):
        w_refs, m_refs, v_refs = refs[:n], refs[n:2 * n], refs[2 * n:3 * n]
        loss_ref, outs = refs[3 * n], refs[3 * n + 1:]
        g = g_ref[...]
        loss_ref[...] = g[4:5, KV_LORA:KV_LORA + 1]
        grads = (g[0:1, :], jnp.concatenate([g[1:2, :], g[2:3, :]], axis=1), g[3:4, :Q_LORA], g[4:5, :KV_LORA],
                 jnp.concatenate([g[5:6, :HG_WIDTH], g[5:6, HG_WIDTH:]], axis=0), g[6:7, :VDIM], g[7:8, :])
        for a in range(n):
            d, nm, nv = _adamw(grads[a], w_refs[a][...], m_refs[a][...], v_refs[a][...])
            outs[a][...] = grads[a]
            outs[n + a][...] = d
            outs[2 * n + a][...] = nm
            outs[3 * n + a][...] = nv

    shapes = [jax.ShapeDtypeStruct(SMALL_SHAPE[k], F32) for k in SMALL]
    res = pl.pallas_call(
        body,
        out_shape=[jax.ShapeDtypeStruct((1, 1), F32)] + shapes * 4,
        name="vectors_adamw",
    )(g_sum, *ws, *ms, *vs)
    return res[0], res[1:n + 1], res[n + 1:2 * n + 1], res[2 * n + 1:3 * n + 1], res[3 * n + 1:]


MATS = ("w_uq", "w_ukv", "w_branch_a", "w_branch_b", "w_out")
COL_SHARDED = dict(w_uq=False, w_ukv=True, w_branch_a=True, w_branch_b=True, w_out=False)
ORDER = ("g_pre", "w_in", "b_gate", "g_q", "w_uq", "g_kv", "w_ukv", "lb_logits", "g_hgrn",
         "w_branch_a", "w_branch_b", "w_out", "g_post")


def _from_slots(name, slots):
    _, r, c = slots.shape
    if COL_SHARDED[name]:
        return slots.transpose(1, 0, 2).reshape(r, N_DEV * c)
    return slots.reshape(N_DEV * r, c)


def kernel(x, g_pre, w_in, b_gate, g_q, w_uq, g_kv, w_ukv, lb_logits, g_hgrn, w_branch_a, w_branch_b, w_out, g_post, loss_target, m_g_pre, m_w_in, m_b_gate, m_g_q, m_w_uq, m_g_kv, m_w_ukv, m_lb_logits, m_g_hgrn, m_w_branch_a, m_w_branch_b, m_w_out, m_g_post, v_g_pre, v_w_in, v_b_gate, v_g_q, v_w_uq, v_g_kv, v_w_ukv, v_lb_logits, v_g_hgrn, v_w_branch_a, v_w_branch_b, v_w_out, v_g_post):
    rows3 = lambda a: jnp.transpose(a, (2, 0, 1))
    w = dict(w_in=rows3(w_in), w_uq=w_uq[0], w_ukv=w_ukv[0], w_branch_a=w_branch_a[0], w_branch_b=w_branch_b[0],
             w_out=w_out[0], g_pre=g_pre, b_gate=b_gate, g_q=g_q, g_kv=g_kv, lb_logits=lb_logits, g_hgrn=g_hgrn,
             g_post=g_post)
    mom = dict(w_in=rows3(m_w_in), w_uq=m_w_uq[0], w_ukv=m_w_ukv[0], w_branch_a=m_w_branch_a[0],
               w_branch_b=m_w_branch_b[0], w_out=m_w_out[0], g_pre=m_g_pre, b_gate=m_b_gate, g_q=m_g_q, g_kv=m_g_kv,
               lb_logits=m_lb_logits, g_hgrn=m_g_hgrn, g_post=m_g_post)
    var = dict(w_in=rows3(v_w_in), w_uq=v_w_uq[0], w_ukv=v_w_ukv[0], w_branch_a=v_w_branch_a[0],
               w_branch_b=v_w_branch_b[0], w_out=v_w_out[0], g_pre=v_g_pre, b_gate=v_b_gate, g_q=v_g_q, g_kv=v_g_kv,
               lb_logits=v_lb_logits, g_hgrn=v_g_hgrn, g_post=v_g_post)

    w_blk = w["w_in"].reshape(W_IN_SHARD, D_MODEL).astype(BF16)
    dx, recv_in, recv, g_sum = _step(x[0], loss_target[0], w_blk, [w[n].astype(BF16) for n in MATS],
                                     g_pre, b_gate, g_q, g_kv, lb_logits, g_hgrn, g_post)

    g_in, d_in, m_in, v_in = _sum_adamw_w_in(recv_in, w["w_in"], mom["w_in"], var["w_in"])
    res = _sum_adamw_whole([recv[n] for n in MATS], *([t[n] for n in MATS] for t in (w, mom, var)))
    total, *vec = _vectors_adamw(g_sum, *([t[n] for n in SMALL] for t in (w, mom, var)))

    outs = []
    for mats, vecs, big in zip(res, vec, (g_in, d_in, m_in, v_in)):
        t = {**{n: a[None] for n, a in zip(MATS, mats)}, **dict(zip(SMALL, vecs)),
             "w_in": jnp.transpose(big, (1, 2, 0))}
        outs += [t[n] for n in ORDER]
    return (total.reshape(()), dx[None], *outs)
```

```python
import math

import jax
import jax.numpy as jnp
import numpy as np
from jax import lax
from jax.experimental import pallas as pl
from jax.experimental.pallas import tpu as pltpu

F32, BF16 = jnp.float32, jnp.bfloat16

D_MODEL = 1024
EPS = 1e-6
HEADS = 8
NOPE, ROPE, VDIM = 64, 32, 64
QK = NOPE + ROPE
Q_LORA, KV_LORA = 768, 256
ROPE_THETA = 10000.0
ATT_CHUNK_SHIFT = 6
HG_BLOCK = 32
HG_WIDTH = 512
D_IN = 5664
D_IN_PAD = 5760
W_IN_SHARD = D_IN // 8
N_DEV = 8
LANE = 128

ADAM_LR, ADAM_B1, ADAM_B2, ADAM_EPS, ADAM_WD, ADAM_STEP = 0.001, 0.9, 0.999, 1e-08, 0.01, 10

W_IN_SEGMENTS = ((3616, 5664, 0), (1056, 1568, 2048), (3104, 3616, 2560), (1568, 3104, 3072),
                 (0, 1024, 4608), (1024, 1056, 5696))

NT = (((1,), (1,)), ((), ()))
TN = (((0,), (0,)), ((), ()))
MESH_ID = pl.DeviceIdType.MESH


def _w_in_pieces():
    out = []
    for lo, hi, dst in W_IN_SEGMENTS:
        c = lo
        while c < hi:
            p = c // W_IN_SHARD
            e = min(hi, (p + 1) * W_IN_SHARD)
            out.append((p, c - p * W_IN_SHARD, e - p * W_IN_SHARD, dst + c - lo))
            c = e
    return out


def _params(sem, vmem_mb=48):
    return pltpu.CompilerParams(dimension_semantics=sem, vmem_limit_bytes=vmem_mb * 2**20)


def _dot(a, b):
    return jnp.dot(a, b, preferred_element_type=F32)


def _dotg(a, b, dims):
    return lax.dot_general(a, b, dims, preferred_element_type=F32)


def _split2(x):
    hi = x.astype(BF16)
    return hi, (x - hi.astype(F32)).astype(BF16)


def _sel_left(m01, x):
    hi, lo = _split2(x)
    return _dot(m01, hi) + _dot(m01, lo)


def _sel_right(x, m01):
    hi, lo = _split2(x)
    return _dot(hi, m01) + _dot(lo, m01)


def _hi_lo(x):
    hi = x.astype(BF16).astype(F32)
    return hi, x - hi


def _sigmoid(x):
    return 0.5 * jnp.tanh(0.5 * x) + 0.5


def _rope(x, c, s1, s2):
    return x * c + pltpu.roll(x, 112, 1) * s1 + pltpu.roll(x, 16, 1) * s2


def _unrope(d, c, s1, s2):
    return d * c + pltpu.roll(d * s1, 16, 1) + pltpu.roll(d * s2, 112, 1)


def _my_place():
    return lax.axis_index("x"), lax.axis_index("y"), lax.axis_index("c")


def _flip(k, x, y, c):
    fx, fy, fc = (k + 1) >> 2 & 1, (k + 1) >> 1 & 1, (k + 1) & 1
    return (1 - x if fx else x), (1 - y if fy else y), (1 - c if fc else c)


def _to_all_copies(s_refs, r_refs, sems, spread):
    send_sems, recv_sems, local_sems = sems
    x, y, c = _my_place()
    me = 4 * x + 2 * y + c
    src = (lambda a, p: s_refs[a]) if spread else (lambda a, p: s_refs[a].at[p])
    local = [pltpu.make_async_copy(src(a, me), r_refs[a].at[me], local_sems.at[a]) for a in range(len(s_refs))]
    remote = []
    for k in range(N_DEV - 1):
        px, py, pc = _flip(k, x, y, c)
        for a in range(len(s_refs)):
            remote.append(pltpu.make_async_remote_copy(
                src_ref=src(a, 4 * px + 2 * py + pc), dst_ref=r_refs[a].at[me],
                send_sem=send_sems.at[7 * a + k], recv_sem=recv_sems.at[7 * a + k],
                device_id=(px, py, pc), device_id_type=MESH_ID))
    return local, remote


def _to_chips_copies(s_refs, r_refs, sems):
    send_sems, recv_sems, local_sems = sems
    x, y, c = _my_place()
    me = 2 * x + y
    local = [pltpu.make_async_copy(s_refs[a].at[me], r_refs[a].at[me], local_sems.at[a]) for a in range(len(s_refs))]
    remote = []
    for k in range(3):
        px = 1 - x if (k + 1) >> 1 & 1 else x
        py = 1 - y if (k + 1) & 1 else y
        for a in range(len(s_refs)):
            remote.append(pltpu.make_async_remote_copy(
                src_ref=s_refs[a].at[2 * px + py], dst_ref=r_refs[a].at[me],
                send_sem=send_sems.at[3 * a + k], recv_sem=recv_sems.at[3 * a + k],
                device_id=(px, py, c), device_id_type=MESH_ID))
    return local, remote


def _start_all(local, remote):
    for cp in local + remote:
        cp.start()


def _wait_all(local, remote):
    for cp in remote:
        cp.wait_recv()
    for cp in remote:
        cp.wait_send()
    for cp in local:
        cp.wait()


def _copy_sems(n, peers):
    return [pltpu.SemaphoreType.DMA((peers * n,)), pltpu.SemaphoreType.DMA((peers * n,)),
            pltpu.SemaphoreType.DMA((n,))]


ANY = pl.BlockSpec(memory_space=pl.ANY)


def _dw_in_slots(ht, dproj):
    m, k = ht.shape
    n = dproj.shape[1]
    tn, tk = 1152, 1024
    nj, nk = n // tn, k // tk
    by_tile = [[] for _ in range(nj)]
    for p, lo, hi, dst in _w_in_pieces():
        while lo < hi:
            j = dst // tn
            cnt = min(hi - lo, (j + 1) * tn - dst)
            by_tile[j].append((p, lo, lo + cnt, dst - j * tn))
            lo, dst = lo + cnt, dst + cnt

    def body(a_ref, b_ref, s_ref, acc_ref):
        j, l = pl.program_id(0), pl.program_id(1)

        @pl.when(l == 0)
        def _():
            acc_ref[...] = jnp.zeros_like(acc_ref)

        acc_ref[...] += _dot(a_ref[...], b_ref[...])

        @pl.when(l == nk - 1)
        def _():
            at = acc_ref[...].T
            for jj in range(nj):
                @pl.when(j == jj)
                def _(jj=jj):
                    for p, lo, hi, d in by_tile[jj]:
                        s_ref[p, lo:hi, :] = at[d:d + hi - lo, :].astype(BF16)

    return pl.pallas_call(
        body,
        grid=(nj, nk),
        in_specs=[pl.BlockSpec((m, tk), lambda j, l: (0, l)), pl.BlockSpec((tk, tn), lambda j, l: (l, j))],
        out_specs=pl.BlockSpec((N_DEV, W_IN_SHARD, m), lambda j, l: (0, 0, 0)),
        out_shape=jax.ShapeDtypeStruct((N_DEV, W_IN_SHARD, m), BF16),
        scratch_shapes=[pltpu.VMEM((m, tn), F32)],
        compiler_params=_params(("arbitrary", "arbitrary")),
        name="dw_in",
    )(ht, dproj)


PROJ_DT = F32
GP_TN = 256
GP_COLS = 5888
GP_NT = GP_COLS // GP_TN


def _gp_tile_pieces():
    tiles = [[] for _ in range(GP_NT)]
    for p, lo, hi, dst in _w_in_pieces():
        while lo < hi:
            t = dst // GP_TN
            n = min(hi - lo, (t + 1) * GP_TN - dst)
            tiles[t].append((p, lo, lo + n, dst - t * GP_TN))
            lo, dst = lo + n, dst + n
    return tiles


def _gp_tables():
    pieces = _gp_tile_pieces()
    rank_of = {None: 0, 0: 1, 1: 2, 2: 2, 4: 3, 5: 3, 3: 4, 6: 5}
    order = np.zeros((N_DEV, GP_NT), np.int32)
    waits = np.zeros((N_DEV, GP_NT), np.int32)
    for me in range(N_DEV):
        x, y, c = me >> 2 & 1, me >> 1 & 1, me & 1
        chips = [(1 - x, y), (x, 1 - y), (1 - x, 1 - y)]

        def sem_of(p):
            px, py, pc = p >> 2 & 1, p >> 1 & 1, p & 1
            if (px, py) == (x, y):
                return None if pc == c else 0
            j = chips.index((px, py))
            return 1 + j if pc == c else 4 + j

        needs = [sorted({sem_of(p) for p, _, _, _ in tile} - {None}) for tile in pieces]
        ranks = [max([rank_of[k] for k in ks], default=0) for ks in needs]
        seq = sorted(range(GP_NT), key=lambda t: (ranks[t], t))
        seen = set()
        for step, t in enumerate(seq):
            order[me, step] = t
            new = [k for k in needs[t] if k not in seen]
            for k in new:
                waits[me, step] |= 1 << k
            seen.update(new)
        assert seen == set(range(7)), (me, seen)
    return order, waits


def _gather_proj(x, g_pre, w_blk, shards):
    s = x.shape[0]
    tx = 512
    ns = len(shards)
    tile_pieces = _gp_tile_pieces()
    order_np, waits_np = _gp_tables()
    xq, yq, cq = _my_place()
    me_out = 4 * xq + 2 * yq + cq
    order = lax.dynamic_index_in_dim(jnp.asarray(order_np), me_out, 0, keepdims=False)
    waits = lax.dynamic_index_in_dim(jnp.asarray(waits_np), me_out, 0, keepdims=False)

    def body(order_ref, waits_ref, x_hbm, g_ref, wblk_hbm, *rest):
        shard_refs, (proj_ref, wt_ref, ht_hbm), got_refs = rest[:ns], rest[ns:ns + 3], rest[ns + 3:2 * ns + 3]
        recv, h_ref, wtile, xbuf, htbuf = rest[2 * ns + 3:2 * ns + 8]
        send_sems, recv_sems, misc_sems = rest[2 * ns + 8:2 * ns + 11]
        sems = rest[2 * ns + 11:]
        t = pl.program_id(0)
        x_, y_, c = _my_place()
        sibling = (x_, y_, 1 - c)
        chips = [(1 - x_, y_), (x_, 1 - y_), (1 - x_, 1 - y_)]
        idx = lambda px, py, pc: 4 * px + 2 * py + pc
        me = idx(x_, y_, c)

        def copy(k, slot, to, src=None):
            return pltpu.make_async_remote_copy(
                src_ref=recv.at[slot] if src is None else src, dst_ref=recv.at[slot],
                send_sem=send_sems.at[k], recv_sem=recv_sems.at[k], device_id=to, device_id_type=MESH_ID)

        mine = pltpu.make_async_copy(wblk_hbm, recv.at[me], misc_sems.at[0])
        first = [copy(0, me, sibling, src=wblk_hbm)] + [copy(1 + j, me, (*chips[j], c), src=wblk_hbm) for j in range(2)]
        passed = [copy(4 + j, idx(*ch, c), sibling) for j, ch in enumerate(chips)]
        onward = [copy(3, idx(*chips[0], c), (*chips[1], c)), copy(3, idx(*chips[1], c), (*chips[0], c))]
        arrivals = ([copy(0, idx(x_, y_, 1 - c), sibling)] + [copy(1 + j, idx(*ch, c), sibling) for j, ch in enumerate(chips)]
                    + [copy(4 + j, idx(*ch, 1 - c), sibling) for j, ch in enumerate(chips)])

        @pl.when(t == 0)
        def _():
            mine.start()
            for cp in first:
                cp.start()
            _start_all(*_to_all_copies(shard_refs, got_refs, sems, True))

            def load(i):
                return pltpu.make_async_copy(x_hbm.at[pl.ds(i * tx, tx), :], xbuf.at[i & 1], misc_sems.at[1 + (i & 1)])

            def store(i):
                return pltpu.make_async_copy(htbuf.at[i & 1], ht_hbm.at[:, pl.ds(i * tx, tx)], misc_sems.at[3 + (i & 1)])

            load(0).start()
            for i in range(s // tx):
                if i + 1 < s // tx:
                    load(i + 1).start()
                load(i).wait()
                xv = xbuf[i & 1]
                r = lax.rsqrt(jnp.mean(xv * xv, axis=-1, keepdims=True) + EPS)
                h = (xv * r * g_ref[...]).astype(BF16)
                h_ref[i * tx:(i + 1) * tx, :] = h
                if i >= 2:
                    store(i - 2).wait()
                htbuf[i & 1] = h.T
                store(i).start()
            for i in range(max(s // tx - 2, 0), s // tx):
                store(i).wait()
            mine.wait()

        w = waits_ref[t]
        for k in range(7):
            @pl.when((w >> k) & 1 == 1)
            def _(k=k):
                arrivals[k].wait_recv()
                if 1 <= k <= 3:
                    passed[k - 1].start()
                if 1 <= k <= 2:
                    @pl.when(c == k - 1)
                    def _():
                        onward[k - 1].start()

        tile = order_ref[t]
        for tt in range(GP_NT):
            @pl.when(tile == tt)
            def _(tt=tt):
                covered = sorted((d, d + hi - lo) for _, lo, hi, d in tile_pieces[tt])
                at = 0
                for lo_z, hi_z in covered + [(GP_TN, GP_TN)]:
                    if lo_z > at:
                        wtile[at:lo_z, :] = jnp.zeros((lo_z - at, D_MODEL), BF16)
                    at = max(at, hi_z)
                for p, lo, hi, d in tile_pieces[tt]:
                    wtile[d:d + hi - lo, :] = recv[p, lo:hi, :]

        wt = wtile[...]
        wt_ref[...] = wt
        proj_ref[...] = _dotg(h_ref[...], wt, NT).astype(PROJ_DT)

        @pl.when(t == GP_NT - 1)
        def _():
            for cp in first + passed + onward[:1]:
                cp.wait_send()
            _wait_all(*_to_all_copies(shard_refs, got_refs, sems, True))

    grid_spec = pltpu.PrefetchScalarGridSpec(
        num_scalar_prefetch=2,
        grid=(GP_NT,),
        in_specs=[ANY, pl.BlockSpec((1, D_MODEL), lambda t, o, w: (0, 0)), ANY] + [ANY] * ns,
        out_specs=[pl.BlockSpec((s, GP_TN), lambda t, o, w: (0, o[t])),
                   pl.BlockSpec((GP_TN, D_MODEL), lambda t, o, w: (o[t], 0)), ANY] + [ANY] * ns,
        scratch_shapes=[pltpu.VMEM((N_DEV, W_IN_SHARD, D_MODEL), BF16), pltpu.VMEM((s, D_MODEL), BF16),
                        pltpu.VMEM((GP_TN, D_MODEL), BF16), pltpu.VMEM((2, tx, D_MODEL), F32),
                        pltpu.VMEM((2, D_MODEL, tx), BF16),
                        pltpu.SemaphoreType.DMA((7,)), pltpu.SemaphoreType.DMA((7,)), pltpu.SemaphoreType.DMA((5,))]
        + _copy_sems(ns, 7),
    )
    return pl.pallas_call(
        body,
        grid_spec=grid_spec,
        out_shape=[jax.ShapeDtypeStruct((s, GP_COLS), PROJ_DT), jax.ShapeDtypeStruct((GP_COLS, D_MODEL), BF16),
                   jax.ShapeDtypeStruct((D_MODEL, s), BF16)]
        + [jax.ShapeDtypeStruct((N_DEV,) + b.shape, b.dtype) for b in shards],
        compiler_params=_params(("arbitrary",), 56),
        name="gather_proj",
    )(order, waits, x, g_pre, w_blk, *shards)


def _mla_prep(proj, g_q, g_kv, w_uq_p, w_kv_p, rc, rs1, rs2):
    s = proj.shape[0]
    tm = 512
    scale = 1.0 / math.sqrt(QK)

    def body(cq_ref, ckv_ref, kpe_ref, gq_ref, gkv_ref, wuq_ref, wkv_ref, c_ref, s1_ref, s2_ref,
             qr_ref, kr_ref, v_ref, cqt_ref, ckvt_ref):
        cq = cq_ref[...].astype(F32)
        r = lax.rsqrt(jnp.mean(cq * cq, axis=-1, keepdims=True) + EPS)
        cqn = (cq * r * gq_ref[...]).astype(BF16)
        cqt_ref[...] = cqn.T
        q = _dot(cqn, wuq_ref[...])
        ckv = ckv_ref[...].astype(F32)
        r = lax.rsqrt(jnp.mean(ckv * ckv, axis=-1, keepdims=True) + EPS)
        ckvn = (ckv * r * gkv_ref[...]).astype(BF16)
        ckvt_ref[...] = ckvn.T
        kv = _dot(ckvn, wkv_ref[...])
        c, s1, s2 = c_ref[...], s1_ref[...], s2_ref[...]
        lane = lax.broadcasted_iota(jnp.int32, (tm, LANE), 1)
        kpe = _rope(kpe_ref[...].astype(F32), c, s1, s2) + jnp.where((lane == QK) | (lane == QK + 1), 1.0, 0.0)
        vone = jnp.where((lane == VDIM) | (lane == VDIM + 1), 1.0, 0.0)
        for h in range(HEADS):
            sl = slice(LANE * h, LANE * (h + 1))
            qr_ref[:, sl] = (_rope(q[:, sl], c, s1, s2) * scale).astype(BF16)
            kr_ref[:, sl] = (kv[:, sl] + kpe).astype(BF16)
            v_ref[:, sl] = (kv[:, HEADS * LANE + LANE * h:HEADS * LANE + LANE * (h + 1)] + vone).astype(BF16)

    row = lambda w, j: pl.BlockSpec((tm, w), lambda i: (i, j))
    col = lambda w: pl.BlockSpec((w, tm), lambda i: (0, i))
    full = lambda a: pl.BlockSpec(a.shape, lambda i: (0, 0))
    return pl.pallas_call(
        body,
        grid=(s // tm,),
        in_specs=[row(768, 6), row(256, 21), row(128, 44), full(g_q), full(g_kv), full(w_uq_p), full(w_kv_p),
                  row(128, 0), row(128, 0), row(128, 0)],
        out_specs=[row(1024, 0), row(1024, 0), row(1024, 0), col(768), col(256)],
        out_shape=[jax.ShapeDtypeStruct((s, 1024), BF16), jax.ShapeDtypeStruct((s, 1024), BF16),
                   jax.ShapeDtypeStruct((s, 1024), BF16), jax.ShapeDtypeStruct((768, s), BF16),
                   jax.ShapeDtypeStruct((256, s), BF16)],
        compiler_params=_params(("arbitrary",)),
        name="mla_prep",
    )(proj, proj, proj, g_q, g_kv, w_uq_p, w_kv_p, rc, rs1, rs2)


ATT_T = 512
ATT_FWD_HEADS = 4


def _chunk_mask(transposed):
    r = lax.broadcasted_iota(jnp.int32, (ATT_T, ATT_T), 0) >> ATT_CHUNK_SHIFT
    c = lax.broadcasted_iota(jnp.int32, (ATT_T, ATT_T), 1) >> ATT_CHUNK_SHIFT
    return (r <= c) if transposed else (c <= r)


def _attn_fwd(qr, kr, vp, shards):
    s = qr.shape[0]
    t = ATT_T
    g = ATT_FWD_HEADS
    ns = len(shards)

    def body(q_ref, k_ref, v_ref, *rest):
        shard_refs, (o_ref, qa_ref), got_refs = rest[:ns], rest[ns:ns + 2], rest[ns + 2:2 * ns + 2]
        sc_ref, sems = rest[2 * ns + 2], rest[2 * ns + 3:]
        qi = pl.program_id(1)

        @pl.when((pl.program_id(0) == 0) & (qi == 0))
        def _():
            _start_all(*_to_all_copies(shard_refs, got_refs, sems, True))
        lane = lax.broadcasted_iota(jnp.int32, (t, LANE), 1)
        sls = [slice(LANE * a, LANE * (a + 1)) for a in range(g)]
        qs = [q_ref[:, sl] for sl in sls]

        def scores(j):
            rows = pl.ds(pl.multiple_of(j * t, t), t)
            for a in range(g):
                sc_ref[j & 1, a] = _dotg(qs[a], k_ref[rows, sls[a]], NT)

        def step(j, carry, masked):
            rows = pl.ds(pl.multiple_of(j * t, t), t)
            out = []
            for a in range(g):
                m, acc = carry[a]
                sc = sc_ref[j & 1, a]
                if masked:
                    sc = jnp.where(_chunk_mask(False), sc, -1e30)
                m_new = jnp.maximum(m, jnp.max(sc, axis=-1, keepdims=True))
                p = jnp.exp(sc - m_new).astype(BF16)
                acc = jnp.exp(m - m_new) * acc + _dot(p, v_ref[rows, sls[a]])
                out.append((m_new, acc))
            return tuple(out)

        def loop(j, carry):
            carry = step(j, carry, False)
            scores(j + 1)
            return carry

        init = tuple((jnp.full((t, 1), -1e30, F32), jnp.zeros((t, LANE), F32)) for _ in range(g))
        scores(0)
        carry = lax.fori_loop(0, qi, loop, init)
        carry = step(qi, carry, True)
        outs = []
        for a in range(g):
            m, acc = carry[a]
            l = acc[:, VDIM:VDIM + 1]
            outs.append(acc / l)
            hi, lo_part = _hi_lo(-(m + jnp.log(l)))
            qa = jnp.where(lane == QK, hi, jnp.where(lane == QK + 1, lo_part, qs[a].astype(F32)))
            qa_ref[:, sls[a]] = qa.astype(BF16)
        for p in range(g // 2):
            o_ref[:, LANE * p:LANE * (p + 1)] = jnp.where(lane < VDIM, outs[2 * p], pltpu.roll(outs[2 * p + 1], VDIM, 1))

        @pl.when((pl.program_id(0) == HEADS // g - 1) & (qi == s // t - 1))
        def _():
            _wait_all(*_to_all_copies(shard_refs, got_refs, sems, True))

    return pl.pallas_call(
        body,
        grid=(HEADS // g, s // t),
        in_specs=[
            pl.BlockSpec((t, g * LANE), lambda h, i: (i, h)),
            pl.BlockSpec((s, g * LANE), lambda h, i: (0, h)),
            pl.BlockSpec((s, g * LANE), lambda h, i: (0, h)),
        ] + [ANY] * ns,
        out_specs=[
            pl.BlockSpec((t, g * VDIM), lambda h, i: (i, h)),
            pl.BlockSpec((t, g * LANE), lambda h, i: (i, h)),
        ] + [ANY] * ns,
        out_shape=[jax.ShapeDtypeStruct((s, 512), F32), jax.ShapeDtypeStruct((s, 1024), BF16)]
        + [jax.ShapeDtypeStruct((N_DEV,) + b.shape, b.dtype) for b in shards],
        scratch_shapes=[pltpu.VMEM((2, g, t, t), F32)] + _copy_sems(ns, 7),
        compiler_params=_params(("arbitrary", "arbitrary")),
        name="attn_fwd",
    )(qr, kr, vp, *shards)


def _attn_bwd(qa, kr, vp, dop, sends):
    s = qa.shape[0]
    t = ATT_T
    nq = s // t
    ns = len(sends)

    def body(q_ref, k_ref, v_ref, do_ref, *rest):
        send_refs, (dq_out, dk_out, dv_out) = rest[:ns], rest[ns:ns + 3]
        recv_refs = rest[ns + 3:2 * ns + 3]
        (dq_ref, dk_ref, dv_ref), sems = rest[2 * ns + 3:2 * ns + 6], rest[2 * ns + 6:]
        j = pl.program_id(1)
        sls = [slice(LANE * a, LANE * (a + 1)) for a in range(2)]

        @pl.when((pl.program_id(0) == 0) & (j == 0))
        def _():
            _start_all(*_to_all_copies(send_refs, recv_refs, sems, False))

        @pl.when(j == 0)
        def _():
            dq_ref[...] = jnp.zeros_like(dq_ref)

        dk_ref[...] = jnp.zeros_like(dk_ref)
        dv_ref[...] = jnp.zeros_like(dv_ref)
        ks = [k_ref[:, sl] for sl in sls]
        vs = [v_ref[:, sl] for sl in sls]

        def part(i, k_lo, k_n, q_lo, q_n, masked):
            rows = pl.ds(pl.multiple_of(i * t + q_lo, 256), q_n)
            keys = slice(k_lo, k_lo + k_n)
            for a in range(2):
                q = q_ref[rows, sls[a]]
                do = do_ref[rows, sls[a]]
                sc = _dotg(ks[a][keys], q, NT)
                if masked:
                    kc = lax.broadcasted_iota(jnp.int32, (k_n, q_n), 0) >> ATT_CHUNK_SHIFT
                    qc = lax.broadcasted_iota(jnp.int32, (k_n, q_n), 1) >> ATT_CHUNK_SHIFT
                    sc = jnp.where(kc <= qc, sc, -1e30)
                p = jnp.exp(sc)
                ds = (p * _dotg(vs[a][keys], do, NT)).astype(BF16)
                dv_ref[keys, sls[a]] += _dot(p.astype(BF16), do)
                dk_ref[keys, sls[a]] += _dot(ds, q)
                dq_ref[rows, sls[a]] += _dotg(ds, ks[a][keys], TN)

        half = t // 2
        part(j, 0, half, 0, t, True)
        part(j, half, half, half, half, True)

        def loop(i, c):
            part(i, 0, t, 0, t, False)
            return c

        lax.fori_loop(j + 1, nq, loop, 0)
        dk_out[...] = dk_ref[...].astype(BF16)
        dv_out[...] = dv_ref[...].astype(BF16)

        @pl.when(j == nq - 1)
        def _():
            dq_out[...] = dq_ref[...].astype(BF16)

        @pl.when((pl.program_id(0) == HEADS // 2 - 1) & (j == nq - 1))
        def _():
            _wait_all(*_to_all_copies(send_refs, recv_refs, sems, False))

    blk = pl.BlockSpec((t, 2 * LANE), lambda h, j: (j, h))
    whole = pl.BlockSpec((s, 2 * LANE), lambda h, j: (0, h))
    out = jax.ShapeDtypeStruct((s, 1024), BF16)
    return pl.pallas_call(
        body,
        grid=(HEADS // 2, nq),
        in_specs=[whole, blk, blk, whole] + [ANY] * ns,
        out_specs=[whole, blk, blk] + [ANY] * ns,
        out_shape=[out, out, out] + [jax.ShapeDtypeStruct(a.shape, a.dtype) for a in sends],
        scratch_shapes=[pltpu.VMEM((s, 2 * LANE), F32), pltpu.VMEM((t, 2 * LANE), F32),
                        pltpu.VMEM((t, 2 * LANE), F32)] + _copy_sems(ns, 7),
        compiler_params=_params(("arbitrary", "arbitrary")),
        name="attn_bwd",
    )(qa, kr, vp, dop, *sends)


HG_T = 256
HG_NC = HG_T // HG_BLOCK
HG_G = 4
GW = 64 * HG_G


def _hg_consts():
    r = jnp.arange(HG_T)[:, None]
    c = jnp.arange(HG_T)[None, :]
    same = (r // HG_BLOCK) == (c // HG_BLOCK)
    mcum = (same & (c <= r)).astype(BF16)
    mrev = (same & (c >= r)).astype(BF16)
    msum = same.astype(BF16)
    a = jnp.arange(GW) // 64
    bd = (a[:, None] == a[None, :]).astype(F32)
    return mcum, mrev, msum, bd


def _stack_heads(xg, head):
    return jnp.concatenate([jnp.where(head == h, xg, 0.0) for h in range(HG_G)], axis=0)


def _unstack_heads(r, head, t):
    out = r[(HG_G - 1) * t:]
    for h in range(HG_G - 2, -1, -1):
        out = jnp.where(head == h, r[h * t:(h + 1) * t], out)
    return out


def _compact_state(st):
    out = st[:64]
    for h in range(1, HG_G):
        out = out + st[64 * h:64 * (h + 1)]
    return out


def _expand_state(cs, head64):
    return jnp.concatenate([jnp.where(head64 == h, cs, 0.0) for h in range(HG_G)], axis=0)


def _hg_pre(hq, hf, lbl, mcum, msum):
    lb = _sigmoid(lbl[0:1, :] - lbl[1:2, :])
    sig = _sigmoid(hf)
    f = lb + (1.0 - lb) * sig
    lf = jnp.log(f)
    b = _sel_left(mcum, lf)
    big_l = _sel_left(msum, lf)
    k = 1.0 - f
    qd = hq * jnp.exp(b)
    ki = k * jnp.exp(-b)
    ke = k * jnp.exp(big_l - b)
    return lb, sig, f, b, big_l, qd, ki, ke


def _hgrn_fwd(proj, lbl):
    s = proj.shape[0]
    t = HG_T
    mcum, _, msum, bd = _hg_consts()

    def body(hq_ref, hf_ref, hi_ref, lbl_ref, mcum_ref, msum_ref, bd_ref, o_ref, sp_ref, st_ref):
        @pl.when(pl.program_id(0) == 0)
        def _():
            st_ref[...] = jnp.zeros_like(st_ref)

        mc = mcum_ref[...]
        _, _, _, _, big_l, qd, ki, ke = _hg_pre(hq_ref[...].astype(F32), hf_ref[...].astype(F32), lbl_ref[...], mc,
                                                msum_ref[...])
        el = jnp.exp(big_l)
        hi = hi_ref[...]
        head = lax.broadcasted_iota(jnp.int32, (t, GW), 1) >> 6
        mask = jnp.concatenate([mc] * HG_G, axis=0) > 0.5
        for p in range(HEADS // HG_G):
            sl = slice(GW * p, GW * (p + 1))
            vp = hi[:, sl].astype(BF16)
            qs = _stack_heads(qd[:, sl], head).astype(BF16)
            a = jnp.where(mask, _dotg(qs, ki[:, sl].astype(BF16), NT), 0.0)
            o_intra = _unstack_heads(_dot(a.astype(BF16), vp), head, t)
            qb = qd[:, sl].astype(BF16)
            kb = ke[:, sl].astype(BF16)
            st = st_ref[p]
            for c in range(HG_NC):
                rows = slice(HG_BLOCK * c, HG_BLOCK * (c + 1))
                sp_ref[c, :, sl] = _compact_state(st)
                o_ref[rows, sl] = o_intra[rows] + _dotg(qb[rows], st.astype(BF16), NT)
                u = _dotg(vp[rows], kb[rows], TN) * bd_ref[...]
                st = st * el[HG_BLOCK * c:HG_BLOCK * c + 1, sl] + u
            st_ref[p] = st

    row = lambda j: pl.BlockSpec((t, HG_WIDTH), lambda i: (i, j))
    full = lambda a: pl.BlockSpec(a.shape, lambda i: (0, 0))
    return pl.pallas_call(
        body,
        grid=(s // t,),
        in_specs=[row(6), row(7), row(8), full(lbl), full(mcum), full(msum), full(bd)],
        out_specs=[row(0), pl.BlockSpec((HG_NC, 64, HG_WIDTH), lambda i: (i, 0, 0))],
        out_shape=[jax.ShapeDtypeStruct((s, HG_WIDTH), F32),
                   jax.ShapeDtypeStruct((s // HG_BLOCK, 64, HG_WIDTH), F32)],
        scratch_shapes=[pltpu.VMEM((HEADS // HG_G, GW, GW), F32)],
        compiler_params=_params(("arbitrary",)),
        name="hgrn_fwd",
    )(proj, proj, proj, lbl, mcum, msum, bd)


def _slot_shape(name, r, c):
    return (N_DEV, r, c // N_DEV) if COL_SHARDED[name] else (N_DEV, r // N_DEV, c)


def _emit_slots(name, acc_ref, out_ref):
    r, c = acc_ref.shape
    for p in range(N_DEV):
        if COL_SHARDED[name]:
            out_ref[p] = acc_ref[:, c // N_DEV * p:c // N_DEV * (p + 1)].astype(BF16)
        else:
            out_ref[p] = acc_ref[r // N_DEV * p:r // N_DEV * (p + 1), :].astype(BF16)


def _hgrn_bwd(proj, lbl, do, sprev, dproj, pairs):
    s = proj.shape[0]
    t = HG_T
    nt = s // t
    npair = len(pairs)
    mcum, mrev, msum, bd = _hg_consts()

    def body(hq_ref, hf_ref, hi_ref, lbl_ref, do_ref, sp_ref, mcum_ref, mrev_ref, msum_ref, bd_ref,
             dproj_in, *rest):
        del dproj_in
        pair_refs, (dh_ref, dlbl_ref) = rest[:2 * npair], rest[2 * npair:2 * npair + 2]
        dw_refs, g_ref, acc_refs = rest[2 * npair + 2:3 * npair + 2], rest[3 * npair + 2], rest[3 * npair + 3:]

        @pl.when(pl.program_id(0) == 0)
        def _():
            g_ref[...] = jnp.zeros_like(g_ref)
            dlbl_ref[...] = jnp.zeros_like(dlbl_ref)
            for acc_ref in acc_refs:
                acc_ref[...] = jnp.zeros_like(acc_ref)

        for n, acc_ref in enumerate(acc_refs):
            acc_ref[...] += _dot(pair_refs[2 * n][...], pair_refs[2 * n + 1][...])

        @pl.when(pl.program_id(0) == nt - 1)
        def _():
            for (name, _, _), acc_ref, dw_ref in zip(pairs, acc_refs, dw_refs):
                _emit_slots(name, acc_ref, dw_ref)

        mc = mcum_ref[...]
        lb, sig, f, b, big_l, qd, ki, ke = _hg_pre(hq_ref[...].astype(F32), hf_ref[...].astype(F32), lbl_ref[...], mc,
                                                   msum_ref[...])
        el = jnp.exp(big_l)
        hi = hi_ref[...]
        dov = do_ref[...]
        head = lax.broadcasted_iota(jnp.int32, (t, GW), 1) >> 6
        head64 = lax.broadcasted_iota(jnp.int32, (64, GW), 1) >> 6
        mask = jnp.concatenate([mc] * HG_G, axis=0) > 0.5
        dqd_parts, dke_parts, dv_parts, del_parts, dki_parts = [], [], [], [], []
        for p in range(HEADS // HG_G):
            sl = slice(GW * p, GW * (p + 1))
            vp = hi[:, sl].astype(BF16)
            qs = _stack_heads(qd[:, sl], head).astype(BF16)
            kip = ki[:, sl].astype(BF16)
            dos = _stack_heads(dov[:, sl], head).astype(BF16)
            a = jnp.where(mask, _dotg(qs, kip, NT), 0.0).astype(BF16)
            da = jnp.where(mask, _dotg(dos, vp, NT), 0.0).astype(BF16)
            r = _dot(da, kip)
            dki_parts.append(_dotg(da, qs, TN))
            qb = qd[:, sl].astype(BF16)
            kb = ke[:, sl].astype(BF16)
            dob = dov[:, sl].astype(BF16)
            g = g_ref[p]
            dqd_c, dv_c, dke_c, del_c = [], [], [], []
            for c in range(HG_NC - 1, -1, -1):
                rows = slice(HG_BLOCK * c, HG_BLOCK * (c + 1))
                gb = g.astype(BF16)
                st = _expand_state(sp_ref[c, :, sl], head64)
                dqd_c.append(_dot(dob[rows], st.astype(BF16)))
                dv_c.append(_dotg(kb[rows], gb, NT))
                dke_c.append(_dot(vp[rows], gb))
                del_c.append(jnp.broadcast_to(jnp.sum(g * st, axis=0, keepdims=True), (HG_BLOCK, GW)))
                g = g * el[HG_BLOCK * c:HG_BLOCK * c + 1, sl] + _dotg(dob[rows], qb[rows], TN) * bd_ref[...]
            g_ref[p] = g
            up = lambda parts: jnp.concatenate(parts[::-1], axis=0)
            dqd_parts.append(_unstack_heads(r, head, t) + up(dqd_c))
            dv_parts.append(_dotg(a, dos, TN) + up(dv_c))
            dke_parts.append(up(dke_c))
            del_parts.append(up(del_c))
        wide = lambda parts: jnp.concatenate(parts, axis=1)
        dqd, dke, dki, dvv, del_rows = wide(dqd_parts), wide(dke_parts), wide(dki_parts), wide(dv_parts), wide(del_parts)
        dh_ref[:, :HG_WIDTH] = (dqd * jnp.exp(b)).astype(BF16)
        dh_ref[:, 2 * HG_WIDTH:] = dvv.astype(BF16)
        dke_ke = dke * ke
        db = dqd * qd - dki * ki - dke_ke
        dl_rows = _sel_left(msum_ref[...], dke_ke) + del_rows * el
        is_last = (lax.broadcasted_iota(jnp.int32, (t, HG_WIDTH), 0) & (HG_BLOCK - 1)) == HG_BLOCK - 1
        db = db + jnp.where(is_last, dl_rows, 0.0)
        dlf = _sel_left(mrev_ref[...], db)
        dk = dki * jnp.exp(-b) + dke * jnp.exp(big_l - b)
        df = dlf / f - dk
        dh_ref[:, HG_WIDTH:2 * HG_WIDTH] = (df * (1.0 - lb) * sig * (1.0 - sig)).astype(BF16)
        dlb = jnp.sum(df * (1.0 - sig), axis=0, keepdims=True) * lb * (1.0 - lb)
        dlbl_ref[0:1, :] += dlb
        dlbl_ref[1:2, :] -= dlb

    rrow = lambda j: pl.BlockSpec((t, HG_WIDTH), lambda i: (nt - 1 - i, j))
    full = lambda a: pl.BlockSpec(a.shape, lambda i: (0, 0))
    pair_specs, dw_specs, dw_shapes, accs = [], [], [], []
    for name, at, b in pairs:
        pair_specs += [pl.BlockSpec((at.shape[0], t), lambda i: (0, i)), pl.BlockSpec((t, b.shape[1]), lambda i: (i, 0))]
        shape = _slot_shape(name, at.shape[0], b.shape[1])
        dw_specs.append(pl.BlockSpec(shape, lambda i: (0, 0, 0)))
        dw_shapes.append(jax.ShapeDtypeStruct(shape, BF16))
        accs.append(pltpu.VMEM((at.shape[0], b.shape[1]), F32))
    return pl.pallas_call(
        body,
        grid=(nt,),
        in_specs=[rrow(6), rrow(7), rrow(8), full(lbl), rrow(0),
                  pl.BlockSpec((HG_NC, 64, HG_WIDTH), lambda i: (nt - 1 - i, 0, 0)),
                  full(mcum), full(mrev), full(msum), full(bd), pl.BlockSpec(memory_space=pl.ANY)] + pair_specs,
        out_specs=[pl.BlockSpec((t, 3 * HG_WIDTH), lambda i: (nt - 1 - i, 2)),
                   pl.BlockSpec((2, HG_WIDTH), lambda i: (0, 0))] + dw_specs,
        out_shape=[jax.ShapeDtypeStruct(dproj.shape, BF16), jax.ShapeDtypeStruct((2, HG_WIDTH), F32)] + dw_shapes,
        input_output_aliases={10: 0},
        scratch_shapes=[pltpu.VMEM((HEADS // HG_G, GW, GW), F32)] + accs,
        compiler_params=_params(("arbitrary",)),
        name="hgrn_bwd",
    )(proj, proj, proj, lbl, do, sprev, mcum, mrev, msum, bd, dproj, *[a for pair in pairs for a in pair[1:]])


def _tail(x, tgt, proj, attn, o, w_a, w_b, w_out, w_at, w_bt, w_outt, b_gate, g_post, gh):
    s = x.shape[0]
    tm = 256
    ones64 = (jnp.arange(HG_WIDTH)[:, None] // 64 == jnp.arange(HG_WIDTH)[None, :] // 64).astype(BF16)
    weights = (w_a, w_b, w_out, w_at, w_bt, w_outt)

    def body(x_ref, t_ref, ml_ref, ga_ref, gb_ref, at_ref, o_ref, *rest):
        w_hbm, (bg_ref, gp_ref, gh_ref, ones_ref) = rest[:6], rest[6:10]
        (dout_ref, dpj_ref, dop_ref, do_ref, mt_ref, dy_ref, yat_ref, dya_ref, ybt_ref, dyb_ref,
         loss_ref, dgp_ref, dbg_ref, dgh_ref) = rest[10:24]
        (wa_ref, wb_ref, wo_ref, wat_ref, wbt_ref, wot_ref), w_sem = rest[24:30], rest[30]

        @pl.when(pl.program_id(0) == 0)
        def _():
            loads = [pltpu.make_async_copy(src, dst, w_sem.at[k])
                     for k, (src, dst) in enumerate(zip(w_hbm, rest[24:30]))]
            _start_all(loads, [])
            loss_ref[...] = jnp.zeros_like(loss_ref)
            dgp_ref[...] = jnp.zeros_like(dgp_ref)
            dbg_ref[...] = jnp.zeros_like(dbg_ref)
            dgh_ref[...] = jnp.zeros_like(dgh_ref)
            _wait_all(loads, [])

        ones = ones_ref[...]
        gate_a = ga_ref[...].astype(F32)
        sa = _sigmoid(gate_a)
        silu_a = gate_a * sa
        attn_v = at_ref[...]
        ya_in = attn_v * silu_a
        ov = o_ref[...]
        ro = lax.rsqrt(_sel_right(ov * ov, ones) * (1.0 / 64.0) + EPS)
        ohat = ov * ro
        ghv = gh_ref[...]
        on = ohat * ghv
        gate_b = gb_ref[...].astype(F32)
        sb = _sigmoid(gate_b)
        silu_b = gate_b * sb
        yb_in = on * silu_b
        ya_bf = ya_in.astype(BF16)
        yb_bf = yb_in.astype(BF16)
        yat_ref[...] = ya_bf.T
        ybt_ref[...] = yb_bf.T
        y_a = _dot(ya_bf, wa_ref[...])
        y_b = _dot(yb_bf, wb_ref[...])
        gts = _sigmoid(ml_ref[...].astype(F32) + bg_ref[...])
        g_a = gts[:, :D_MODEL]
        g_b = gts[:, D_MODEL:]
        m_bf = (g_a * y_a + g_b * y_b).astype(BF16)
        mt_ref[...] = m_bf.T
        y = _dot(m_bf, wo_ref[...])
        r1 = lax.rsqrt(jnp.mean(y * y, axis=-1, keepdims=True) + EPS)
        yn = y * r1
        gp = gp_ref[...]
        e = x_ref[...] + yn * gp - t_ref[...]
        loss_ref[...] += jnp.sum(e * e, axis=0, keepdims=True)
        dout = e * (1.0 / D_MODEL)
        dout_ref[...] = dout
        dgp_ref[...] += jnp.sum(dout * yn, axis=0, keepdims=True)
        dyn = dout * gp
        dy = r1 * (dyn - yn * jnp.mean(dyn * yn, axis=-1, keepdims=True))
        dy_bf = dy.astype(BF16)
        dy_ref[...] = dy_bf
        dm = _dot(dy_bf, wot_ref[...])
        dml_a = dm * y_a * g_a * (1.0 - g_a)
        dml_b = dm * y_b * g_b * (1.0 - g_b)
        dpj_ref[:, :D_MODEL] = dml_a.astype(BF16)
        dpj_ref[:, D_MODEL:2 * D_MODEL] = dml_b.astype(BF16)
        dbg_ref[:, :D_MODEL] += jnp.sum(dml_a, axis=0, keepdims=True)
        dbg_ref[:, D_MODEL:] += jnp.sum(dml_b, axis=0, keepdims=True)
        dya_bf = (dm * g_a).astype(BF16)
        dyb_bf = (dm * g_b).astype(BF16)
        dya_ref[...] = dya_bf
        dyb_ref[...] = dyb_bf
        dya_in = _dot(dya_bf, wat_ref[...])
        dyb_in = _dot(dyb_bf, wbt_ref[...])
        dattn = dya_in * silu_a
        delta = _sel_right(dattn * attn_v, ones)
        lane = lax.broadcasted_iota(jnp.int32, (tm, LANE), 1)
        for p in range(HEADS // 2):
            sl = slice(LANE * p, LANE * (p + 1))
            xs = (dattn[:, sl], pltpu.roll(dattn[:, sl], VDIM, 1))
            nds = (-pltpu.roll(delta[:, sl], VDIM, 1), -delta[:, sl])
            for a in range(2):
                hi, lo_part = _hi_lo(nds[a])
                blk = jnp.where(lane < VDIM, xs[a], jnp.where(lane == VDIM, hi, jnp.where(lane == VDIM + 1, lo_part, 0.0)))
                dop_ref[:, LANE * (2 * p + a):LANE * (2 * p + a + 1)] = blk.astype(BF16)
        dpj_ref[:, 2 * D_MODEL:2 * D_MODEL + HG_WIDTH] = (
            dya_in * attn_v * (sa * (1.0 + gate_a * (1.0 - sa)))).astype(BF16)
        don = dyb_in * silu_b
        dpj_ref[:, 2 * D_MODEL + HG_WIDTH:] = (dyb_in * on * (sb * (1.0 + gate_b * (1.0 - sb)))).astype(BF16)
        dgh_ref[...] += jnp.sum(don * ohat, axis=0, keepdims=True)
        dohat = don * ghv
        do_ref[...] = (ro * (dohat - ohat * (_sel_right(dohat * ohat, ones) * (1.0 / 64.0)))).astype(BF16)

    row = lambda w, j: pl.BlockSpec((tm, w), lambda i: (i, j))
    col = lambda w: pl.BlockSpec((w, tm), lambda i: (0, i))
    full = lambda a: pl.BlockSpec(a.shape, lambda i: (0, 0))
    acc = lambda w: pl.BlockSpec((1, w), lambda i: (0, 0))
    sds = lambda w, dt: jax.ShapeDtypeStruct((s, w), dt)
    sdt = lambda w: jax.ShapeDtypeStruct((w, s), BF16)
    return pl.pallas_call(
        body,
        grid=(s // tm,),
        in_specs=[row(1024, 0), row(1024, 0), row(2048, 0), row(512, 4), row(512, 5), row(512, 0), row(512, 0)]
        + [ANY] * 6 + [full(b_gate), full(g_post), full(gh), full(ones64)],
        out_specs=[row(1024, 0), row(3072, 0), row(1024, 0), row(512, 0),
                   col(1024), row(1024, 0), col(512), row(1024, 0), col(512), row(1024, 0),
                   acc(1024), acc(1024), acc(2048), acc(512)],
        out_shape=[sds(1024, F32), sds(D_IN_PAD, BF16), sds(1024, BF16), sds(512, BF16),
                   sdt(1024), sds(1024, BF16), sdt(512), sds(1024, BF16), sdt(512), sds(1024, BF16),
                   jax.ShapeDtypeStruct((1, 1024), F32), jax.ShapeDtypeStruct((1, 1024), F32),
                   jax.ShapeDtypeStruct((1, 2048), F32), jax.ShapeDtypeStruct((1, 512), F32)],
        scratch_shapes=[pltpu.VMEM(a.shape, BF16) for a in weights] + [pltpu.SemaphoreType.DMA((6,))],
        compiler_params=_params(("arbitrary",), 56),
        name="tail",
    )(x, tgt, proj, proj, proj, attn, o, *weights, b_gate, g_post, gh, ones64)


def _mla_bwd(proj, dqr, dkr, dv, g_q, g_kv, w_uq_pt, w_kv_pt, rc, rs1, rs2, cqt, ckvt, dproj):
    assert HEADS == N_DEV
    s = proj.shape[0]
    tm = 512
    scale = 1.0 / math.sqrt(QK)

    def body(cq_ref, ckv_ref, dqr_ref, dkr_ref, dv_ref, gq_ref, gkv_ref, wuqt_ref, wkvt_ref, c_ref, s1_ref, s2_ref,
             cqt_ref, ckvt_ref, dproj_in, dc_ref, dgq_ref, dgkv_ref, uq_slots, ukv_slots,
             dqf_ref, dkvf_ref, dwuq_ref, dwkv_ref):
        del dproj_in

        @pl.when(pl.program_id(0) == 0)
        def _():
            dgq_ref[...] = jnp.zeros_like(dgq_ref)
            dgkv_ref[...] = jnp.zeros_like(dgkv_ref)
            dwuq_ref[...] = jnp.zeros_like(dwuq_ref)
            dwkv_ref[...] = jnp.zeros_like(dwkv_ref)

        c, s1, s2 = c_ref[...], s1_ref[...], s2_ref[...]
        lane = lax.broadcasted_iota(jnp.int32, (tm, LANE), 1)
        ksum = jnp.zeros((tm, LANE), F32)
        for h in range(HEADS):
            sl = slice(LANE * h, LANE * (h + 1))
            dqf_ref[:, sl] = (_unrope(dqr_ref[:, sl], c, s1, s2) * scale).astype(BF16)
            dkh = dkr_ref[:, sl]
            ksum = ksum + dkh
            dkvf_ref[:, sl] = jnp.where(lane < NOPE, dkh, 0.0).astype(BF16)
            dkvf_ref[:, HEADS * LANE + LANE * h:HEADS * LANE + LANE * (h + 1)] = jnp.where(
                lane < VDIM, dv_ref[:, sl], 0.0).astype(BF16)
        dkpe = _unrope(ksum, c, s1, s2)
        dc_ref[:, Q_LORA + KV_LORA:] = jnp.where((lane >= NOPE) & (lane < QK), dkpe, 0.0).astype(BF16)
        dqf, dkvf = dqf_ref[...], dkvf_ref[...]
        dwuq_ref[...] += _dot(cqt_ref[...], dqf)
        dwkv_ref[...] += _dot(ckvt_ref[...], dkvf)
        dcqn = _dot(dqf, wuqt_ref[...])
        dckvn = _dot(dkvf, wkvt_ref[...])
        for x_ref, g_ref, dn, cols, dg_ref in ((cq_ref, gq_ref, dcqn, slice(0, Q_LORA), dgq_ref),
                                               (ckv_ref, gkv_ref, dckvn, slice(Q_LORA, Q_LORA + KV_LORA), dgkv_ref)):
            xv = x_ref[...].astype(F32)
            r = lax.rsqrt(jnp.mean(xv * xv, axis=-1, keepdims=True) + EPS)
            xh = xv * r
            dg_ref[...] += jnp.sum(dn * xh, axis=0, keepdims=True)
            dh = dn * g_ref[...]
            dc_ref[:, cols] = (r * (dh - xh * jnp.mean(dh * xh, axis=-1, keepdims=True))).astype(BF16)

        @pl.when(pl.program_id(0) == s // tm - 1)
        def _():
            ur = Q_LORA // N_DEV
            for p in range(N_DEV):
                uq_slots[p] = jnp.concatenate(
                    [dwuq_ref[ur * p:ur * (p + 1), LANE * h:LANE * h + QK] for h in range(HEADS)], axis=1).astype(BF16)
                ukv_slots[p] = jnp.concatenate(
                    [dwkv_ref[:, LANE * p:LANE * p + NOPE],
                     dwkv_ref[:, LANE * (HEADS + p):LANE * (HEADS + p) + VDIM]], axis=1).astype(BF16)

    row = lambda w, j: pl.BlockSpec((tm, w), lambda i: (i, j))
    full = lambda a: pl.BlockSpec(a.shape, lambda i: (0, 0))
    acc = lambda w: pl.BlockSpec((1, w), lambda i: (0, 0))
    col = lambda w: pl.BlockSpec((w, tm), lambda i: (0, i))
    whole = lambda shape: pl.BlockSpec(shape, lambda i: (0, 0, 0))
    uq_shape = (N_DEV, Q_LORA // N_DEV, HEADS * QK)
    ukv_shape = (N_DEV, KV_LORA, NOPE + VDIM)
    return pl.pallas_call(
        body,
        grid=(s // tm,),
        in_specs=[row(768, 6), row(256, 21), row(1024, 0), row(1024, 0), row(1024, 0), full(g_q), full(g_kv),
                  full(w_uq_pt), full(w_kv_pt), row(128, 0), row(128, 0), row(128, 0), col(Q_LORA), col(KV_LORA),
                  pl.BlockSpec(memory_space=pl.ANY)],
        out_specs=[row(1152, 4), acc(768), acc(256), whole(uq_shape), whole(ukv_shape)],
        out_shape=[jax.ShapeDtypeStruct(dproj.shape, BF16),
                   jax.ShapeDtypeStruct((1, 768), F32), jax.ShapeDtypeStruct((1, 256), F32),
                   jax.ShapeDtypeStruct(uq_shape, BF16), jax.ShapeDtypeStruct(ukv_shape, BF16)],
        input_output_aliases={14: 0},
        scratch_shapes=[pltpu.VMEM((tm, HEADS * LANE), BF16), pltpu.VMEM((tm, 2 * HEADS * LANE), BF16),
                        pltpu.VMEM((Q_LORA, HEADS * LANE), F32), pltpu.VMEM((KV_LORA, 2 * HEADS * LANE), F32)],
        compiler_params=_params(("arbitrary",)),
        name="mla_bwd",
    )(proj, proj, dqr, dkr, dv, g_q, g_kv, w_uq_pt, w_kv_pt, rc, rs1, rs2, cqt, ckvt, dproj)


def _dh_dx(dproj, w_in_pt, x, dout, g_pre, sends):
    s, k = dproj.shape
    tm = 256
    ns, ni = len(sends), s // tm

    def body(dp_ref, w_ref, x_ref, dout_ref, g_ref, *rest):
        send_refs, (dx_ref, dg_ref) = rest[:ns], rest[ns:ns + 2]
        recv_refs, sems = rest[ns + 2:2 * ns + 2], rest[2 * ns + 2:]

        @pl.when(pl.program_id(0) == 0)
        def _():
            _start_all(*_to_chips_copies(send_refs, recv_refs, sems))
            dg_ref[...] = jnp.zeros_like(dg_ref)

        dh = _dot(dp_ref[...], w_ref[...])
        xv = x_ref[...]
        r = lax.rsqrt(jnp.mean(xv * xv, axis=-1, keepdims=True) + EPS)
        xh = xv * r
        dg_ref[...] += jnp.sum(dh * xh, axis=0, keepdims=True)
        dxh = dh * g_ref[...]
        dx_ref[...] = dout_ref[...] + r * (dxh - xh * jnp.mean(dxh * xh, axis=-1, keepdims=True))

        @pl.when(pl.program_id(0) == ni - 1)
        def _():
            _wait_all(*_to_chips_copies(send_refs, recv_refs, sems))

    row = lambda w: pl.BlockSpec((tm, w), lambda i: (i, 0))
    return pl.pallas_call(
        body,
        grid=(ni,),
        in_specs=[row(k), pl.BlockSpec((k, D_MODEL), lambda i: (0, 0)), row(D_MODEL), row(D_MODEL),
                  pl.BlockSpec((1, D_MODEL), lambda i: (0, 0))] + [ANY] * ns,
        out_specs=[row(D_MODEL), pl.BlockSpec((1, D_MODEL), lambda i: (0, 0))] + [ANY] * ns,
        out_shape=[jax.ShapeDtypeStruct((s, D_MODEL), F32), jax.ShapeDtypeStruct((1, D_MODEL), F32)]
        + [jax.ShapeDtypeStruct(a.shape, a.dtype) for a in sends],
        scratch_shapes=_copy_sems(ns, 3),
        compiler_params=_params(("arbitrary",)),
        name="dh_dx",
    )(dproj, w_in_pt, x, dout, g_pre, *sends)


def _pair_reduce(slots):
    n = len(slots)
    half = [(N_DEV // 2,) + a.shape[1:] for a in slots]

    def body(*refs):
        s_refs, o_refs = refs[:n], refs[n:2 * n]
        mine, got = refs[2 * n:3 * n], refs[3 * n:4 * n]
        send_sems, recv_sems, local_sems, out_sems = refs[4 * n:]
        x, y, c = _my_place()
        copies, loads, stores = [], [], []
        for q in range(N_DEV // 2):
            for a in range(n):
                copies.append(pltpu.make_async_remote_copy(
                    src_ref=s_refs[a].at[2 * q + 1 - c], dst_ref=got[a].at[q],
                    send_sem=send_sems.at[4 * a + q], recv_sem=recv_sems.at[4 * a + q],
                    device_id=(x, y, 1 - c), device_id_type=MESH_ID))
                loads.append(pltpu.make_async_copy(s_refs[a].at[2 * q + c], mine[a].at[q], local_sems.at[4 * a + q]))
                stores.append(pltpu.make_async_copy(mine[a].at[q], o_refs[a].at[q], out_sems.at[4 * a + q]))
        _start_all(loads, copies)
        k = 0
        for q in range(N_DEV // 2):
            for a in range(n):
                loads[k].wait()
                copies[k].wait_recv()
                mine[a][q] = (mine[a][q].astype(F32) + got[a][q].astype(F32)).astype(mine[a].dtype)
                stores[k].start()
                k += 1
        for cp in copies:
            cp.wait_send()
        for cp in stores:
            cp.wait()

    vm = lambda: [pltpu.VMEM(h, a.dtype) for h, a in zip(half, slots)]
    return pl.pallas_call(
        body,
        in_specs=[ANY] * n,
        out_specs=[ANY] * n,
        out_shape=[jax.ShapeDtypeStruct(h, a.dtype) for h, a in zip(half, slots)],
        scratch_shapes=vm() + vm() + [pltpu.SemaphoreType.DMA((4 * n,)), pltpu.SemaphoreType.DMA((4 * n,)),
                                      pltpu.SemaphoreType.DMA((4 * n,)), pltpu.SemaphoreType.DMA((4 * n,))],
        compiler_params=pltpu.CompilerParams(vmem_limit_bytes=48 * 2**20),
        name="pair_reduce",
    )(*slots)


def _rope_tables(s):
    inv = (np.float32(ROPE_THETA) ** (-np.arange(0, ROPE, 2, dtype=np.float32) / np.float32(ROPE))).astype(np.float32)
    ang = (np.arange(s, dtype=np.float32)[:, None] * inv[None, :]).astype(np.float32)
    cos, sin = jnp.asarray(np.cos(ang.astype(np.float64)), F32), jnp.asarray(np.sin(ang.astype(np.float64)), F32)
    z = lambda w: jnp.zeros((s, w), F32)
    rc = jnp.concatenate([jnp.ones((s, NOPE), F32), cos, cos, z(32)], axis=1)
    rs1 = jnp.concatenate([z(NOPE), -sin, z(16), z(32)], axis=1)
    rs2 = jnp.concatenate([z(NOPE), z(16), sin, z(32)], axis=1)
    return rc, rs1, rs2


def _step(x, tgt, w_blk, shards, g_pre, b_gate, g_q, g_kv, lbl, g_hgrn, g_post):
    s = x.shape[0]
    rc, rs1, rs2 = _rope_tables(s)
    gh = jnp.tile(g_hgrn, (1, HEADS))

    proj, w_in_pt, ht, *got = _gather_proj(x, g_pre, w_blk, shards[:2])
    w_uq, w_ukv = (_from_slots(n, g) for n, g in zip(MATS[:2], got))
    w_uq_p = jnp.pad(w_uq.reshape(Q_LORA, HEADS, QK), ((0, 0), (0, 0), (0, LANE - QK))).reshape(Q_LORA, HEADS * LANE)
    kv3 = w_ukv.reshape(KV_LORA, HEADS, NOPE + VDIM)
    pad64 = lambda t: jnp.pad(t, ((0, 0), (0, 0), (0, LANE - 64))).reshape(KV_LORA, HEADS * LANE)
    w_kv_p = jnp.concatenate([pad64(kv3[:, :, :NOPE]), pad64(kv3[:, :, NOPE:])], axis=1)

    qr, kr, v, cqt, ckvt = _mla_prep(proj, g_q, g_kv, w_uq_p, w_kv_p, rc, rs1, rs2)
    attn, qa, *got = _attn_fwd(qr, kr, v, shards[2:])
    w_a, w_b, w_out = (_from_slots(n, g) for n, g in zip(MATS[2:], got))
    o, sprev = _hgrn_fwd(proj, lbl)
    (dout, dproj, dop, do, mt, dy_bf, yat, dya_bf, ybt, dyb_bf,
     loss_vec, dg_post, db_gate, dgh) = _tail(x, tgt, proj, attn, o, w_a, w_b, w_out, w_a.T, w_b.T, w_out.T,
                                               b_gate, g_post, gh)
    dproj, dlbl, *early = _hgrn_bwd(proj, lbl, do, sprev, dproj,
                                    [("w_branch_a", yat, dya_bf), ("w_branch_b", ybt, dyb_bf), ("w_out", mt, dy_bf)])
    dqr, dkr, dv, *early_recv = _attn_bwd(qa, kr, v, dop, early)
    dproj, dg_q, dg_kv, dw_uq_slots, dw_ukv_slots = _mla_bwd(proj, dqr, dkr, dv, g_q, g_kv, w_uq_p.T, w_kv_p.T,
                                                             rc, rs1, rs2, cqt, ckvt, dproj)

    dw_in_slots = _dw_in_slots(ht, dproj)
    late = _pair_reduce([dw_in_slots, dw_uq_slots, dw_ukv_slots])
    dx, dg_pre, *late_recv = _dh_dx(dproj, w_in_pt, x, dout, g_pre, late)

    g_sum = _vectors_sum(dg_pre, db_gate, dg_q, dg_kv, dlbl, dgh, dg_post, loss_vec)
    return dx, late_recv[0], dict(zip(MATS, late_recv[1:] + early_recv)), g_sum


def _adamw(g, w, m, v):
    c1 = 1.0 / (1.0 - ADAM_B1 ** ADAM_STEP)
    c2 = 1.0 / (1.0 - ADAM_B2 ** ADAM_STEP)
    nm = ADAM_B1 * m + (1.0 - ADAM_B1) * g
    nv = ADAM_B2 * v + (1.0 - ADAM_B2) * (g * g)
    d = -ADAM_LR * ((nm * c1) / (jnp.sqrt(nv * c2) + ADAM_EPS) + ADAM_WD * w)
    return d, nm, nv


def _sum8(r_ref):
    g = r_ref[0].astype(F32)
    for k in range(1, r_ref.shape[0]):
        g = g + r_ref[k].astype(F32)
    return g


def _sum_adamw_w_in(recv, w, m, v):
    rows, _, cols = w.shape
    tc = 256
    nc = cols // tc

    def body(r_ref, w_hbm, m_hbm, v_hbm, g_hbm, d_hbm, nm_hbm, nv_hbm, ins, outs, in_sems, out_sems):
        i = pl.program_id(0)
        slot = i & 1
        cols_of = lambda step: pl.ds(pl.multiple_of(step * tc, tc), tc)

        def load(k, step, sl):
            return pltpu.make_async_copy((w_hbm, m_hbm, v_hbm)[k].at[:, 0, cols_of(step)], ins.at[sl, k],
                                         in_sems.at[sl, k])

        def store(k, step, sl):
            return pltpu.make_async_copy(outs.at[sl, k], (g_hbm, d_hbm, nm_hbm, nv_hbm)[k].at[:, 0, cols_of(step)],
                                         out_sems.at[sl, k])

        @pl.when(i == 0)
        def _():
            for k in range(3):
                load(k, 0, 0).start()

        @pl.when(i + 1 < nc)
        def _():
            for k in range(3):
                load(k, i + 1, 1 - slot).start()

        @pl.when(i >= 2)
        def _():
            for k in range(4):
                store(k, i - 2, slot).wait()

        for k in range(3):
            load(k, i, slot).wait()
        g = _sum8(r_ref)
        d, nm, nv = _adamw(g, ins[slot, 0], ins[slot, 1], ins[slot, 2])
        for k, val in enumerate((g, d, nm, nv)):
            outs[slot, k] = val
        for k in range(4):
            store(k, i, slot).start()

        @pl.when(i == nc - 1)
        def _():
            for k in range(4):
                store(k, i, slot).wait()
            if nc >= 2:
                for k in range(4):
                    store(k, i - 1, 1 - slot).wait()

    out = jax.ShapeDtypeStruct((rows, 1, cols), F32)
    return pl.pallas_call(
        body,
        grid=(nc,),
        in_specs=[pl.BlockSpec((recv.shape[0], rows, tc), lambda i: (0, 0, i)), ANY, ANY, ANY],
        out_specs=[ANY, ANY, ANY, ANY],
        out_shape=[out, out, out, out],
        scratch_shapes=[pltpu.VMEM((2, 3, rows, tc), F32), pltpu.VMEM((2, 4, rows, tc), F32),
                        pltpu.SemaphoreType.DMA((2, 3)), pltpu.SemaphoreType.DMA((2, 4))],
        compiler_params=_params(("arbitrary",)),
        name="sum_adamw_w_in",
    )(recv, w, m, v)


def _sum_adamw_whole(recvs, ws, ms, vs):
    n = len(ws)

    def body(*refs):
        r_refs, w_refs, m_refs, v_refs = refs[:n], refs[n:2 * n], refs[2 * n:3 * n], refs[3 * n:4 * n]
        outs = refs[4 * n:]
        for a in range(n):
            g = _sum8(r_refs[a])
            d, nm, nv = _adamw(g, w_refs[a][...], m_refs[a][...], v_refs[a][...])
            outs[a][...] = g
            outs[n + a][...] = d
            outs[2 * n + a][...] = nm
            outs[3 * n + a][...] = nv

    shapes = [jax.ShapeDtypeStruct(w.shape, F32) for w in ws]
    res = pl.pallas_call(
        body,
        out_shape=shapes * 4,
        compiler_params=pltpu.CompilerParams(vmem_limit_bytes=48 * 2**20),
        name="sum_adamw_mats",
    )(*recvs, *ws, *ms, *vs)
    return res[:n], res[n:2 * n], res[2 * n:3 * n], res[3 * n:]


SMALL = ("g_pre", "b_gate", "g_q", "g_kv", "lb_logits", "g_hgrn", "g_post")
SMALL_SHAPE = dict(g_pre=(1, 1024), b_gate=(1, 2048), g_q=(1, 768), g_kv=(1, 256), lb_logits=(2, 512),
                   g_hgrn=(1, 64), g_post=(1, 1024))


def _vectors_sum(dg_pre, db_gate, dg_q, dg_kv, dlbl, dgh, dg_post, loss_vec):
    def body(gpre_ref, bg_ref, gq_ref, gkv_ref, lbl_ref, gh_ref, gpost_ref, loss_ref, out_ref, mine, got,
             send_sems, recv_sems):
        mine[...] = jnp.zeros_like(mine)
        mine[0:1, :] = gpre_ref[...]
        mine[1:2, :] = bg_ref[:, :1024]
        mine[2:3, :] = bg_ref[:, 1024:]
        mine[3:4, :Q_LORA] = gq_ref[...]
        mine[4:5, :KV_LORA] = gkv_ref[...]
        loss = (0.5 / D_MODEL) * jnp.sum(loss_ref[...], axis=-1, keepdims=True)
        mine[4:5, KV_LORA:] = jnp.broadcast_to(loss, (1, 1024 - KV_LORA))
        mine[5:6, :HG_WIDTH] = lbl_ref[0:1, :]
        mine[5:6, HG_WIDTH:] = lbl_ref[1:2, :]
        gh = gh_ref[...]
        fold = gh[:, :VDIM]
        for h in range(1, HEADS):
            fold = fold + gh[:, VDIM * h:VDIM * (h + 1)]
        mine[6:7, :VDIM] = fold
        mine[7:8, :] = gpost_ref[...]
        x, y, c = _my_place()
        me = 4 * x + 2 * y + c
        got[me] = mine[...]
        copies = [pltpu.make_async_remote_copy(
            src_ref=mine, dst_ref=got.at[me], send_sem=send_sems.at[k], recv_sem=recv_sems.at[k],
            device_id=_flip(k, x, y, c), device_id_type=MESH_ID) for k in range(N_DEV - 1)]
        _start_all([], copies)
        _wait_all([], copies)
        out_ref[...] = _sum8(got)

    return pl.pallas_call(
        body,
        out_shape=jax.ShapeDtypeStruct((8, 1024), F32),
        scratch_shapes=[pltpu.VMEM((8, 1024), F32), pltpu.VMEM((N_DEV, 8, 1024), F32),
                        pltpu.SemaphoreType.DMA((7,)), pltpu.SemaphoreType.DMA((7,))],
        name="vectors_sum",
    )(dg_pre, db_gate, dg_q, dg_kv, dlbl, dgh, dg_post, loss_vec)


def _vectors_adamw(g_sum, ws, ms, vs):
    n = len(SMALL)

    def body(g_ref, *refs):
        w_refs, m_refs, v_refs = refs[:n], refs[n:2 * n], refs[2 * n:3 * n]
        loss_ref, outs = refs[3 * n], refs[3 * n + 1:]
        g = g_ref[...]
        loss_ref[...] = g[4:5, KV_LORA:KV_LORA + 1]
        grads = (g[0:1, :], jnp.concatenate([g[1:2, :], g[2:3, :]], axis=1), g[3:4, :Q_LORA], g[4:5, :KV_LORA],
                 jnp.concatenate([g[5:6, :HG_WIDTH], g[5:6, HG_WIDTH:]], axis=0), g[6:7, :VDIM], g[7:8, :])
        for a in range(n):
            d, nm, nv = _adamw(grads[a], w_refs[a][...], m_refs[a][...], v_refs[a][...])
            outs[a][...] = grads[a]
            outs[n + a][...] = d
            outs[2 * n + a][...] = nm
            outs[3 * n + a][...] = nv

    shapes = [jax.ShapeDtypeStruct(SMALL_SHAPE[k], F32) for k in SMALL]
    res = pl.pallas_call(
        body,
        out_shape=[jax.ShapeDtypeStruct((1, 1), F32)] + shapes * 4,
        name="vectors_adamw",
    )(g_sum, *ws, *ms, *vs)
    return res[0], res[1:n + 1], res[n + 1:2 * n + 1], res[2 * n + 1:3 * n + 1], res[3 * n + 1:]


MATS = ("w_uq", "w_ukv", "w_branch_a", "w_branch_b", "w_out")
COL_SHARDED = dict(w_uq=False, w_ukv=True, w_branch_a=True, w_branch_b=True, w_out=False)
ORDER = ("g_pre", "w_in", "b_gate", "g_q", "w_uq", "g_kv", "w_ukv", "lb_logits", "g_hgrn",
         "w_branch_a", "w_branch_b", "w_out", "g_post")


def _from_slots(name, slots):
    _, r, c = slots.shape
    if COL_SHARDED[name]:
        return slots.transpose(1, 0, 2).reshape(r, N_DEV * c)
    return slots.reshape(N_DEV * r, c)


def kernel(x, g_pre, w_in, b_gate, g_q, w_uq, g_kv, w_ukv, lb_logits, g_hgrn, w_branch_a, w_branch_b, w_out, g_post, loss_target, m_g_pre, m_w_in, m_b_gate, m_g_q, m_w_uq, m_g_kv, m_w_ukv, m_lb_logits, m_g_hgrn, m_w_branch_a, m_w_branch_b, m_w_out, m_g_post, v_g_pre, v_w_in, v_b_gate, v_g_q, v_w_uq, v_g_kv, v_w_ukv, v_lb_logits, v_g_hgrn, v_w_branch_a, v_w_branch_b, v_w_out, v_g_post):
    rows3 = lambda a: jnp.transpose(a, (2, 0, 1))
    w = dict(w_in=rows3(w_in), w_uq=w_uq[0], w_ukv=w_ukv[0], w_branch_a=w_branch_a[0], w_branch_b=w_branch_b[0],
             w_out=w_out[0], g_pre=g_pre, b_gate=b_gate, g_q=g_q, g_kv=g_kv, lb_logits=lb_logits, g_hgrn=g_hgrn,
             g_post=g_post)
    mom = dict(w_in=rows3(m_w_in), w_uq=m_w_uq[0], w_ukv=m_w_ukv[0], w_branch_a=m_w_branch_a[0],
               w_branch_b=m_w_branch_b[0], w_out=m_w_out[0], g_pre=m_g_pre, b_gate=m_b_gate, g_q=m_g_q, g_kv=m_g_kv,
               lb_logits=m_lb_logits, g_hgrn=m_g_hgrn, g_post=m_g_post)
    var = dict(w_in=rows3(v_w_in), w_uq=v_w_uq[0], w_ukv=v_w_ukv[0], w_branch_a=v_w_branch_a[0],
               w_branch_b=v_w_branch_b[0], w_out=v_w_out[0], g_pre=v_g_pre, b_gate=v_b_gate, g_q=v_g_q, g_kv=v_g_kv,
               lb_logits=v_lb_logits, g_hgrn=v_g_hgrn, g_post=v_g_post)

    w_blk = w["w_in"].reshape(W_IN_SHARD, D_MODEL).astype(BF16)
    dx, recv_in, recv, g_sum = _step(x[0], loss_target[0], w_blk, [w[n].astype(BF16) for n in MATS],
                                     g_pre, b_gate, g_q, g_kv, lb_logits, g_hgrn, g_post)

    g_in, d_in, m_in, v_in = _sum_adamw_w_in(recv_in, w["w_in"], mom["w_in"], var["w_in"])
    res = _sum_adamw_whole([recv[n] for n in MATS], *([t[n] for n in MATS] for t in (w, mom, var)))
    total, *vec = _vectors_adamw(g_sum, *([t[n] for n in SMALL] for t in (w, mom, var)))

    outs = []
    for mats, vecs, big in zip(res, vec, (g_in, d_in, m_in, v_in)):
        t = {**{n: a[None] for n, a in zip(MATS, mats)}, **dict(zip(SMALL, vecs)),
             "w_in": jnp.transpose(big, (1, 2, 0))}
        outs += [t[n] for n in ORDER]
    return (total.reshape(()), dx[None], *outs)
```

```python
import math

import jax
import jax.numpy as jnp
import numpy as np
from jax import lax
from jax.experimental import pallas as pl
from jax.experimental.pallas import tpu as pltpu

F32, BF16 = jnp.float32, jnp.bfloat16

D_MODEL = 1024
EPS = 1e-6
HEADS = 8
NOPE, ROPE, VDIM = 64, 32, 64
QK = NOPE + ROPE
Q_LORA, KV_LORA = 768, 256
ROPE_THETA = 10000.0
ATT_CHUNK_SHIFT = 6
HG_BLOCK = 32
HG_WIDTH = 512
D_IN = 5664
D_IN_PAD = 5760
W_IN_SHARD = D_IN // 8
N_DEV = 8
LANE = 128

ADAM_LR, ADAM_B1, ADAM_B2, ADAM_EPS, ADAM_WD, ADAM_STEP = 0.001, 0.9, 0.999, 1e-08, 0.01, 10

W_IN_SEGMENTS = ((3616, 5664, 0), (1056, 1568, 2048), (3104, 3616, 2560), (1568, 3104, 3072),
                 (0, 1024, 4608), (1024, 1056, 5696))

NT = (((1,), (1,)), ((), ()))
TN = (((0,), (0,)), ((), ()))
MESH_ID = pl.DeviceIdType.MESH


def _w_in_pieces():
    out = []
    for lo, hi, dst in W_IN_SEGMENTS:
        c = lo
        while c < hi:
            p = c // W_IN_SHARD
            e = min(hi, (p + 1) * W_IN_SHARD)
            out.append((p, c - p * W_IN_SHARD, e - p * W_IN_SHARD, dst + c - lo))
            c = e
    return out


def _params(sem, vmem_mb=48):
    return pltpu.CompilerParams(dimension_semantics=sem, vmem_limit_bytes=vmem_mb * 2**20)


def _dot(a, b):
    return jnp.dot(a, b, preferred_element_type=F32)


def _dotg(a, b, dims):
    return lax.dot_general(a, b, dims, preferred_element_type=F32)


def _split2(x):
    hi = x.astype(BF16)
    return hi, (x - hi.astype(F32)).astype(BF16)


def _sel_left(m01, x):
    hi, lo = _split2(x)
    return _dot(m01, hi) + _dot(m01, lo)


def _sel_right(x, m01):
    hi, lo = _split2(x)
    return _dot(hi, m01) + _dot(lo, m01)


def _hi_lo(x):
    hi = x.astype(BF16).astype(F32)
    return hi, x - hi


def _sigmoid(x):
    return 0.5 * jnp.tanh(0.5 * x) + 0.5


def _rope(x, c, s1, s2):
    return x * c + pltpu.roll(x, 112, 1) * s1 + pltpu.roll(x, 16, 1) * s2


def _unrope(d, c, s1, s2):
    return d * c + pltpu.roll(d * s1, 16, 1) + pltpu.roll(d * s2, 112, 1)


def _my_place():
    return lax.axis_index("x"), lax.axis_index("y"), lax.axis_index("c")


def _flip(k, x, y, c):
    fx, fy, fc = (k + 1) >> 2 & 1, (k + 1) >> 1 & 1, (k + 1) & 1
    return (1 - x if fx else x), (1 - y if fy else y), (1 - c if fc else c)


def _to_all_copies(s_refs, r_refs, sems, spread):
    send_sems, recv_sems, local_sems = sems
    x, y, c = _my_place()
    me = 4 * x + 2 * y + c
    src = (lambda a, p: s_refs[a]) if spread else (lambda a, p: s_refs[a].at[p])
    local = [pltpu.make_async_copy(src(a, me), r_refs[a].at[me], local_sems.at[a]) for a in range(len(s_refs))]
    remote = []
    for k in range(N_DEV - 1):
        px, py, pc = _flip(k, x, y, c)
        for a in range(len(s_refs)):
            remote.append(pltpu.make_async_remote_copy(
                src_ref=src(a, 4 * px + 2 * py + pc), dst_ref=r_refs[a].at[me],
                send_sem=send_sems.at[7 * a + k], recv_sem=recv_sems.at[7 * a + k],
                device_id=(px, py, pc), device_id_type=MESH_ID))
    return local, remote


def _to_chips_copies(s_refs, r_refs, sems):
    send_sems, recv_sems, local_sems = sems
    x, y, c = _my_place()
    me = 2 * x + y
    local = [pltpu.make_async_copy(s_refs[a].at[me], r_refs[a].at[me], local_sems.at[a]) for a in range(len(s_refs))]
    remote = []
    for k in range(3):
        px = 1 - x if (k + 1) >> 1 & 1 else x
        py = 1 - y if (k + 1) & 1 else y
        for a in range(len(s_refs)):
            remote.append(pltpu.make_async_remote_copy(
                src_ref=s_refs[a].at[2 * px + py], dst_ref=r_refs[a].at[me],
                send_sem=send_sems.at[3 * a + k], recv_sem=recv_sems.at[3 * a + k],
                device_id=(px, py, c), device_id_type=MESH_ID))
    return local, remote


def _start_all(local, remote):
    for cp in local + remote:
        cp.start()


def _wait_all(local, remote):
    for cp in remote:
        cp.wait_recv()
    for cp in remote:
        cp.wait_send()
    for cp in local:
        cp.wait()


def _copy_sems(n, peers):
    return [pltpu.SemaphoreType.DMA((peers * n,)), pltpu.SemaphoreType.DMA((peers * n,)),
            pltpu.SemaphoreType.DMA((n,))]


ANY = pl.BlockSpec(memory_space=pl.ANY)


def _dw_in_slots(ht, dproj):
    m, k = ht.shape
    n = dproj.shape[1]
    tn, tk = 1152, 1024
    nj, nk = n // tn, k // tk
    by_tile = [[] for _ in range(nj)]
    for p, lo, hi, dst in _w_in_pieces():
        while lo < hi:
            j = dst // tn
            cnt = min(hi - lo, (j + 1) * tn - dst)
            by_tile[j].append((p, lo, lo + cnt, dst - j * tn))
            lo, dst = lo + cnt, dst + cnt

    def body(a_ref, b_ref, s_ref, acc_ref):
        j, l = pl.program_id(0), pl.program_id(1)

        @pl.when(l == 0)
        def _():
            acc_ref[...] = jnp.zeros_like(acc_ref)

        acc_ref[...] += _dot(a_ref[...], b_ref[...])

        @pl.when(l == nk - 1)
        def _():
            at = acc_ref[...].T
            for jj in range(nj):
                @pl.when(j == jj)
                def _(jj=jj):
                    for p, lo, hi, d in by_tile[jj]:
                        s_ref[p, lo:hi, :] = at[d:d + hi - lo, :].astype(BF16)

    return pl.pallas_call(
        body,
        grid=(nj, nk),
        in_specs=[pl.BlockSpec((m, tk), lambda j, l: (0, l)), pl.BlockSpec((tk, tn), lambda j, l: (l, j))],
        out_specs=pl.BlockSpec((N_DEV, W_IN_SHARD, m), lambda j, l: (0, 0, 0)),
        out_shape=jax.ShapeDtypeStruct((N_DEV, W_IN_SHARD, m), BF16),
        scratch_shapes=[pltpu.VMEM((m, tn), F32)],
        compiler_params=_params(("arbitrary", "arbitrary")),
        name="dw_in",
    )(ht, dproj)


PROJ_DT = F32
GP_TN = 256
GP_COLS = 5888
GP_NT = GP_COLS // GP_TN


def _gp_tile_pieces():
    tiles = [[] for _ in range(GP_NT)]
    for p, lo, hi, dst in _w_in_pieces():
        while lo < hi:
            t = dst // GP_TN
            n = min(hi - lo, (t + 1) * GP_TN - dst)
            tiles[t].append((p, lo, lo + n, dst - t * GP_TN))
            lo, dst = lo + n, dst + n
    return tiles


def _gp_tables():
    pieces = _gp_tile_pieces()
    rank_of = {None: 0, 0: 1, 1: 2, 2: 2, 4: 3, 5: 3, 3: 4, 6: 5}
    order = np.zeros((N_DEV, GP_NT), np.int32)
    waits = np.zeros((N_DEV, GP_NT), np.int32)
    for me in range(N_DEV):
        x, y, c = me >> 2 & 1, me >> 1 & 1, me & 1
        chips = [(1 - x, y), (x, 1 - y), (1 - x, 1 - y)]

        def sem_of(p):
            px, py, pc = p >> 2 & 1, p >> 1 & 1, p & 1
            if (px, py) == (x, y):
                return None if pc == c else 0
            j = chips.index((px, py))
            return 1 + j if pc == c else 4 + j

        needs = [sorted({sem_of(p) for p, _, _, _ in tile} - {None}) for tile in pieces]
        ranks = [max([rank_of[k] for k in ks], default=0) for ks in needs]
        seq = sorted(range(GP_NT), key=lambda t: (ranks[t], t))
        seen = set()
        for step, t in enumerate(seq):
            order[me, step] = t
            new = [k for k in needs[t] if k not in seen]
            for k in new:
                waits[me, step] |= 1 << k
            seen.update(new)
        assert seen == set(range(7)), (me, seen)
    return order, waits


def _gather_proj(x, g_pre, w_blk, shards):
    s = x.shape[0]
    tx = 512
    ns = len(shards)
    tile_pieces = _gp_tile_pieces()
    order_np, waits_np = _gp_tables()
    xq, yq, cq = _my_place()
    me_out = 4 * xq + 2 * yq + cq
    order = lax.dynamic_index_in_dim(jnp.asarray(order_np), me_out, 0, keepdims=False)
    waits = lax.dynamic_index_in_dim(jnp.asarray(waits_np), me_out, 0, keepdims=False)

    def body(order_ref, waits_ref, x_hbm, g_ref, wblk_hbm, *rest):
        shard_refs, (proj_ref, wt_ref, ht_hbm), got_refs = rest[:ns], rest[ns:ns + 3], rest[ns + 3:2 * ns + 3]
        recv, h_ref, wtile, xbuf, htbuf = rest[2 * ns + 3:2 * ns + 8]
        send_sems, recv_sems, misc_sems = rest[2 * ns + 8:2 * ns + 11]
        sems = rest[2 * ns + 11:]
        t = pl.program_id(0)
        x_, y_, c = _my_place()
        sibling = (x_, y_, 1 - c)
        chips = [(1 - x_, y_), (x_, 1 - y_), (1 - x_, 1 - y_)]
        idx = lambda px, py, pc: 4 * px + 2 * py + pc
        me = idx(x_, y_, c)

        def copy(k, slot, to, src=None):
            return pltpu.make_async_remote_copy(
                src_ref=recv.at[slot] if src is None else src, dst_ref=recv.at[slot],
                send_sem=send_sems.at[k], recv_sem=recv_sems.at[k], device_id=to, device_id_type=MESH_ID)

        mine = pltpu.make_async_copy(wblk_hbm, recv.at[me], misc_sems.at[0])
        first = [copy(0, me, sibling, src=wblk_hbm)] + [copy(1 + j, me, (*chips[j], c), src=wblk_hbm) for j in range(2)]
        passed = [copy(4 + j, idx(*ch, c), sibling) for j, ch in enumerate(chips)]
        onward = [copy(3, idx(*chips[0], c), (*chips[1], c)), copy(3, idx(*chips[1], c), (*chips[0], c))]
        arrivals = ([copy(0, idx(x_, y_, 1 - c), sibling)] + [copy(1 + j, idx(*ch, c), sibling) for j, ch in enumerate(chips)]
                    + [copy(4 + j, idx(*ch, 1 - c), sibling) for j, ch in enumerate(chips)])

        @pl.when(t == 0)
        def _():
            mine.start()
            for cp in first:
                cp.start()
            _start_all(*_to_all_copies(shard_refs, got_refs, sems, True))

            def load(i):
                return pltpu.make_async_copy(x_hbm.at[pl.ds(i * tx, tx), :], xbuf.at[i & 1], misc_sems.at[1 + (i & 1)])

            def store(i):
                return pltpu.make_async_copy(htbuf.at[i & 1], ht_hbm.at[:, pl.ds(i * tx, tx)], misc_sems.at[3 + (i & 1)])

            load(0).start()
            for i in range(s // tx):
                if i + 1 < s // tx:
                    load(i + 1).start()
                load(i).wait()
                xv = xbuf[i & 1]
                r = lax.rsqrt(jnp.mean(xv * xv, axis=-1, keepdims=True) + EPS)
                h = (xv * r * g_ref[...]).astype(BF16)
                h_ref[i * tx:(i + 1) * tx, :] = h
                if i >= 2:
                    store(i - 2).wait()
                htbuf[i & 1] = h.T
                store(i).start()
            for i in range(max(s // tx - 2, 0), s // tx):
                store(i).wait()
            mine.wait()

        w = waits_ref[t]
        for k in range(7):
            @pl.when((w >> k) & 1 == 1)
            def _(k=k):
                arrivals[k].wait_recv()
                if 1 <= k <= 3:
                    passed[k - 1].start()
                if 1 <= k <= 2:
                    @pl.when(c == k - 1)
                    def _():
                        onward[k - 1].start()

        tile = order_ref[t]
        for tt in range(GP_NT):
            @pl.when(tile == tt)
            def _(tt=tt):
                covered = sorted((d, d + hi - lo) for _, lo, hi, d in tile_pieces[tt])
                at = 0
                for lo_z, hi_z in covered + [(GP_TN, GP_TN)]:
                    if lo_z > at:
                        wtile[at:lo_z, :] = jnp.zeros((lo_z - at, D_MODEL), BF16)
                    at = max(at, hi_z)
                for p, lo, hi, d in tile_pieces[tt]:
                    wtile[d:d + hi - lo, :] = recv[p, lo:hi, :]

        wt = wtile[...]
        wt_ref[...] = wt
        proj_ref[...] = _dotg(h_ref[...], wt, NT).astype(PROJ_DT)

        @pl.when(t == GP_NT - 1)
        def _():
            for cp in first + passed + onward[:1]:
                cp.wait_send()
            _wait_all(*_to_all_copies(shard_refs, got_refs, sems, True))

    grid_spec = pltpu.PrefetchScalarGridSpec(
        num_scalar_prefetch=2,
        grid=(GP_NT,),
        in_specs=[ANY, pl.BlockSpec((1, D_MODEL), lambda t, o, w: (0, 0)), ANY] + [ANY] * ns,
        out_specs=[pl.BlockSpec((s, GP_TN), lambda t, o, w: (0, o[t])),
                   pl.BlockSpec((GP_TN, D_MODEL), lambda t, o, w: (o[t], 0)), ANY] + [ANY] * ns,
        scratch_shapes=[pltpu.VMEM((N_DEV, W_IN_SHARD, D_MODEL), BF16), pltpu.VMEM((s, D_MODEL), BF16),
                        pltpu.VMEM((GP_TN, D_MODEL), BF16), pltpu.VMEM((2, tx, D_MODEL), F32),
                        pltpu.VMEM((2, D_MODEL, tx), BF16),
                        pltpu.SemaphoreType.DMA((7,)), pltpu.SemaphoreType.DMA((7,)), pltpu.SemaphoreType.DMA((5,))]
        + _copy_sems(ns, 7),
    )
    return pl.pallas_call(
        body,
        grid_spec=grid_spec,
        out_shape=[jax.ShapeDtypeStruct((s, GP_COLS), PROJ_DT), jax.ShapeDtypeStruct((GP_COLS, D_MODEL), BF16),
                   jax.ShapeDtypeStruct((D_MODEL, s), BF16)]
        + [jax.ShapeDtypeStruct((N_DEV,) + b.shape, b.dtype) for b in shards],
        compiler_params=_params(("arbitrary",), 56),
        name="gather_proj",
    )(order, waits, x, g_pre, w_blk, *shards)


def _mla_prep(proj, g_q, g_kv, w_uq_p, w_kv_p, rc, rs1, rs2):
    s = proj.shape[0]
    tm = 512
    scale = 1.0 / math.sqrt(QK)

    def body(cq_ref, ckv_ref, kpe_ref, gq_ref, gkv_ref, wuq_ref, wkv_ref, c_ref, s1_ref, s2_ref,
             qr_ref, kr_ref, v_ref, cqt_ref, ckvt_ref):
        cq = cq_ref[...].astype(F32)
        r = lax.rsqrt(jnp.mean(cq * cq, axis=-1, keepdims=True) + EPS)
        cqn = (cq * r * gq_ref[...]).astype(BF16)
        cqt_ref[...] = cqn.T
        q = _dot(cqn, wuq_ref[...])
        ckv = ckv_ref[...].astype(F32)
        r = lax.rsqrt(jnp.mean(ckv * ckv, axis=-1, keepdims=True) + EPS)
        ckvn = (ckv * r * gkv_ref[...]).astype(BF16)
        ckvt_ref[...] = ckvn.T
        kv = _dot(ckvn, wkv_ref[...])
        c, s1, s2 = c_ref[...], s1_ref[...], s2_ref[...]
        lane = lax.broadcasted_iota(jnp.int32, (tm, LANE), 1)
        kpe = _rope(kpe_ref[...].astype(F32), c, s1, s2) + jnp.where((lane == QK) | (lane == QK + 1), 1.0, 0.0)
        vone = jnp.where((lane == VDIM) | (lane == VDIM + 1), 1.0, 0.0)
        for h in range(HEADS):
            sl = slice(LANE * h, LANE * (h + 1))
            qr_ref[:, sl] = (_rope(q[:, sl], c, s1, s2) * scale).astype(BF16)
            kr_ref[:, sl] = (kv[:, sl] + kpe).astype(BF16)
            v_ref[:, sl] = (kv[:, HEADS * LANE + LANE * h:HEADS * LANE + LANE * (h + 1)] + vone).astype(BF16)

    row = lambda w, j: pl.BlockSpec((tm, w), lambda i: (i, j))
    col = lambda w: pl.BlockSpec((w, tm), lambda i: (0, i))
    full = lambda a: pl.BlockSpec(a.shape, lambda i: (0, 0))
    return pl.pallas_call(
        body,
        grid=(s // tm,),
        in_specs=[row(768, 6), row(256, 21), row(128, 44), full(g_q), full(g_kv), full(w_uq_p), full(w_kv_p),
                  row(128, 0), row(128, 0), row(128, 0)],
        out_specs=[row(1024, 0), row(1024, 0), row(1024, 0), col(768), col(256)],
        out_shape=[jax.ShapeDtypeStruct((s, 1024), BF16), jax.ShapeDtypeStruct((s, 1024), BF16),
                   jax.ShapeDtypeStruct((s, 1024), BF16), jax.ShapeDtypeStruct((768, s), BF16),
                   jax.ShapeDtypeStruct((256, s), BF16)],
        compiler_params=_params(("arbitrary",)),
        name="mla_prep",
    )(proj, proj, proj, g_q, g_kv, w_uq_p, w_kv_p, rc, rs1, rs2)


ATT_T = 512
ATT_FWD_HEADS = 4


def _chunk_mask(transposed):
    r = lax.broadcasted_iota(jnp.int32, (ATT_T, ATT_T), 0) >> ATT_CHUNK_SHIFT
    c = lax.broadcasted_iota(jnp.int32, (ATT_T, ATT_T), 1) >> ATT_CHUNK_SHIFT
    return (r <= c) if transposed else (c <= r)


def _attn_fwd(qr, kr, vp, shards):
    s = qr.shape[0]
    t = ATT_T
    g = ATT_FWD_HEADS
    ns = len(shards)

    def body(q_ref, k_ref, v_ref, *rest):
        shard_refs, (o_ref, qa_ref), got_refs = rest[:ns], rest[ns:ns + 2], rest[ns + 2:2 * ns + 2]
        sc_ref, sems = rest[2 * ns + 2], rest[2 * ns + 3:]
        qi = pl.program_id(1)

        @pl.when((pl.program_id(0) == 0) & (qi == 0))
        def _():
            _start_all(*_to_all_copies(shard_refs, got_refs, sems, True))
        lane = lax.broadcasted_iota(jnp.int32, (t, LANE), 1)
        sls = [slice(LANE * a, LANE * (a + 1)) for a in range(g)]
        qs = [q_ref[:, sl] for sl in sls]

        def scores(j):
            rows = pl.ds(pl.multiple_of(j * t, t), t)
            for a in range(g):
                sc_ref[j & 1, a] = _dotg(qs[a], k_ref[rows, sls[a]], NT)

        def step(j, carry, masked):
            rows = pl.ds(pl.multiple_of(j * t, t), t)
            out = []
            for a in range(g):
                m, acc = carry[a]
                sc = sc_ref[j & 1, a]
                if masked:
                    sc = jnp.where(_chunk_mask(False), sc, -1e30)
                m_new = jnp.maximum(m, jnp.max(sc, axis=-1, keepdims=True))
                p = jnp.exp((sc - m_new).astype(BF16))
                acc = jnp.exp(m - m_new) * acc + _dot(p, v_ref[rows, sls[a]])
                out.append((m_new, acc))
            return tuple(out)

        def loop(j, carry):
            carry = step(j, carry, False)
            scores(j + 1)
            return carry

        init = tuple((jnp.full((t, 1), -1e30, F32), jnp.zeros((t, LANE), F32)) for _ in range(g))
        scores(0)
        carry = lax.fori_loop(0, qi, loop, init)
        carry = step(qi, carry, True)
        outs = []
        for a in range(g):
            m, acc = carry[a]
            l = acc[:, VDIM:VDIM + 1]
            outs.append(acc / l)
            hi, lo_part = _hi_lo(-(m + jnp.log(l)))
            qa = jnp.where(lane == QK, hi, jnp.where(lane == QK + 1, lo_part, qs[a].astype(F32)))
            qa_ref[:, sls[a]] = qa.astype(BF16)
        for p in range(g // 2):
            o_ref[:, LANE * p:LANE * (p + 1)] = jnp.where(lane < VDIM, outs[2 * p], pltpu.roll(outs[2 * p + 1], VDIM, 1))

        @pl.when((pl.program_id(0) == HEADS // g - 1) & (qi == s // t - 1))
        def _():
            _wait_all(*_to_all_copies(shard_refs, got_refs, sems, True))

    return pl.pallas_call(
        body,
        grid=(HEADS // g, s // t),
        in_specs=[
            pl.BlockSpec((t, g * LANE), lambda h, i: (i, h)),
            pl.BlockSpec((s, g * LANE), lambda h, i: (0, h)),
            pl.BlockSpec((s, g * LANE), lambda h, i: (0, h)),
        ] + [ANY] * ns,
        out_specs=[
            pl.BlockSpec((t, g * VDIM), lambda h, i: (i, h)),
            pl.BlockSpec((t, g * LANE), lambda h, i: (i, h)),
        ] + [ANY] * ns,
        out_shape=[jax.ShapeDtypeStruct((s, 512), F32), jax.ShapeDtypeStruct((s, 1024), BF16)]
        + [jax.ShapeDtypeStruct((N_DEV,) + b.shape, b.dtype) for b in shards],
        scratch_shapes=[pltpu.VMEM((2, g, t, t), F32)] + _copy_sems(ns, 7),
        compiler_params=_params(("arbitrary", "arbitrary")),
        name="attn_fwd",
    )(qr, kr, vp, *shards)


def _attn_bwd(qa, kr, vp, dop, sends):
    s = qa.shape[0]
    t = ATT_T
    nq = s // t
    ns = len(sends)

    def body(q_ref, k_ref, v_ref, do_ref, *rest):
        send_refs, (dq_out, dk_out, dv_out) = rest[:ns], rest[ns:ns + 3]
        recv_refs = rest[ns + 3:2 * ns + 3]
        (dq_ref, dk_ref, dv_ref), sems = rest[2 * ns + 3:2 * ns + 6], rest[2 * ns + 6:]
        j = pl.program_id(1)
        sls = [slice(LANE * a, LANE * (a + 1)) for a in range(2)]

        @pl.when((pl.program_id(0) == 0) & (j == 0))
        def _():
            _start_all(*_to_all_copies(send_refs, recv_refs, sems, False))

        @pl.when(j == 0)
        def _():
            dq_ref[...] = jnp.zeros_like(dq_ref)

        dk_ref[...] = jnp.zeros_like(dk_ref)
        dv_ref[...] = jnp.zeros_like(dv_ref)
        ks = [k_ref[:, sl] for sl in sls]
        vs = [v_ref[:, sl] for sl in sls]

        def part(i, k_lo, k_n, q_lo, q_n, masked):
            rows = pl.ds(pl.multiple_of(i * t + q_lo, 256), q_n)
            keys = slice(k_lo, k_lo + k_n)
            for a in range(2):
                q = q_ref[rows, sls[a]]
                do = do_ref[rows, sls[a]]
                sc = _dotg(ks[a][keys], q, NT)
                if masked:
                    kc = lax.broadcasted_iota(jnp.int32, (k_n, q_n), 0) >> ATT_CHUNK_SHIFT
                    qc = lax.broadcasted_iota(jnp.int32, (k_n, q_n), 1) >> ATT_CHUNK_SHIFT
                    sc = jnp.where(kc <= qc, sc, -1e30)
                p = jnp.exp(sc)
                ds = (p * _dotg(vs[a][keys], do, NT)).astype(BF16)
                dv_ref[keys, sls[a]] += _dot(p.astype(BF16), do)
                dk_ref[keys, sls[a]] += _dot(ds, q)
                dq_ref[rows, sls[a]] += _dotg(ds, ks[a][keys], TN)

        half = t // 2
        part(j, 0, half, 0, t, True)
        part(j, half, half, half, half, True)

        def loop(i, c):
            part(i, 0, t, 0, t, False)
            return c

        lax.fori_loop(j + 1, nq, loop, 0)
        dk_out[...] = dk_ref[...].astype(BF16)
        dv_out[...] = dv_ref[...].astype(BF16)

        @pl.when(j == nq - 1)
        def _():
            dq_out[...] = dq_ref[...].astype(BF16)

        @pl.when((pl.program_id(0) == HEADS // 2 - 1) & (j == nq - 1))
        def _():
            _wait_all(*_to_all_copies(send_refs, recv_refs, sems, False))

    blk = pl.BlockSpec((t, 2 * LANE), lambda h, j: (j, h))
    whole = pl.BlockSpec((s, 2 * LANE), lambda h, j: (0, h))
    out = jax.ShapeDtypeStruct((s, 1024), BF16)
    return pl.pallas_call(
        body,
        grid=(HEADS // 2, nq),
        in_specs=[whole, blk, blk, whole] + [ANY] * ns,
        out_specs=[whole, blk, blk] + [ANY] * ns,
        out_shape=[out, out, out] + [jax.ShapeDtypeStruct(a.shape, a.dtype) for a in sends],
        scratch_shapes=[pltpu.VMEM((s, 2 * LANE), F32), pltpu.VMEM((t, 2 * LANE), F32),
                        pltpu.VMEM((t, 2 * LANE), F32)] + _copy_sems(ns, 7),
        compiler_params=_params(("arbitrary", "arbitrary")),
        name="attn_bwd",
    )(qa, kr, vp, dop, *sends)


HG_T = 256
HG_NC = HG_T // HG_BLOCK
HG_G = 4
GW = 64 * HG_G


def _hg_consts():
    r = jnp.arange(HG_T)[:, None]
    c = jnp.arange(HG_T)[None, :]
    same = (r // HG_BLOCK) == (c // HG_BLOCK)
    mcum = (same & (c <= r)).astype(BF16)
    mrev = (same & (c >= r)).astype(BF16)
    msum = same.astype(BF16)
    a = jnp.arange(GW) // 64
    bd = (a[:, None] == a[None, :]).astype(F32)
    return mcum, mrev, msum, bd


def _stack_heads(xg, head):
    return jnp.concatenate([jnp.where(head == h, xg, 0.0) for h in range(HG_G)], axis=0)


def _unstack_heads(r, head, t):
    out = r[(HG_G - 1) * t:]
    for h in range(HG_G - 2, -1, -1):
        out = jnp.where(head == h, r[h * t:(h + 1) * t], out)
    return out


def _compact_state(st):
    out = st[:64]
    for h in range(1, HG_G):
        out = out + st[64 * h:64 * (h + 1)]
    return out


def _expand_state(cs, head64):
    return jnp.concatenate([jnp.where(head64 == h, cs, 0.0) for h in range(HG_G)], axis=0)


def _hg_pre(hq, hf, lbl, mcum, msum):
    lb = _sigmoid(lbl[0:1, :] - lbl[1:2, :])
    sig = _sigmoid(hf)
    f = lb + (1.0 - lb) * sig
    lf = jnp.log(f)
    b = _sel_left(mcum, lf)
    big_l = _sel_left(msum, lf)
    k = 1.0 - f
    qd = hq * jnp.exp(b)
    ki = k * jnp.exp(-b)
    ke = k * jnp.exp(big_l - b)
    return lb, sig, f, b, big_l, qd, ki, ke


def _hgrn_fwd(proj, lbl):
    s = proj.shape[0]
    t = HG_T
    mcum, _, msum, bd = _hg_consts()

    def body(hq_ref, hf_ref, hi_ref, lbl_ref, mcum_ref, msum_ref, bd_ref, o_ref, sp_ref, st_ref):
        @pl.when(pl.program_id(0) == 0)
        def _():
            st_ref[...] = jnp.zeros_like(st_ref)

        mc = mcum_ref[...]
        _, _, _, _, big_l, qd, ki, ke = _hg_pre(hq_ref[...].astype(F32), hf_ref[...].astype(F32), lbl_ref[...], mc,
                                                msum_ref[...])
        el = jnp.exp(big_l)
        hi = hi_ref[...]
        head = lax.broadcasted_iota(jnp.int32, (t, GW), 1) >> 6
        mask = jnp.concatenate([mc] * HG_G, axis=0) > 0.5
        for p in range(HEADS // HG_G):
            sl = slice(GW * p, GW * (p + 1))
            vp = hi[:, sl].astype(BF16)
            qs = _stack_heads(qd[:, sl], head).astype(BF16)
            a = jnp.where(mask, _dotg(qs, ki[:, sl].astype(BF16), NT), 0.0)
            o_intra = _unstack_heads(_dot(a.astype(BF16), vp), head, t)
            qb = qd[:, sl].astype(BF16)
            kb = ke[:, sl].astype(BF16)
            st = st_ref[p]
            for c in range(HG_NC):
                rows = slice(HG_BLOCK * c, HG_BLOCK * (c + 1))
                sp_ref[c, :, sl] = _compact_state(st)
                o_ref[rows, sl] = o_intra[rows] + _dotg(qb[rows], st.astype(BF16), NT)
                u = _dotg(vp[rows], kb[rows], TN) * bd_ref[...]
                st = st * el[HG_BLOCK * c:HG_BLOCK * c + 1, sl] + u
            st_ref[p] = st

    row = lambda j: pl.BlockSpec((t, HG_WIDTH), lambda i: (i, j))
    full = lambda a: pl.BlockSpec(a.shape, lambda i: (0, 0))
    return pl.pallas_call(
        body,
        grid=(s // t,),
        in_specs=[row(6), row(7), row(8), full(lbl), full(mcum), full(msum), full(bd)],
        out_specs=[row(0), pl.BlockSpec((HG_NC, 64, HG_WIDTH), lambda i: (i, 0, 0))],
        out_shape=[jax.ShapeDtypeStruct((s, HG_WIDTH), F32),
                   jax.ShapeDtypeStruct((s // HG_BLOCK, 64, HG_WIDTH), F32)],
        scratch_shapes=[pltpu.VMEM((HEADS // HG_G, GW, GW), F32)],
        compiler_params=_params(("arbitrary",)),
        name="hgrn_fwd",
    )(proj, proj, proj, lbl, mcum, msum, bd)


def _slot_shape(name, r, c):
    return (N_DEV, r, c // N_DEV) if COL_SHARDED[name] else (N_DEV, r // N_DEV, c)


def _emit_slots(name, acc_ref, out_ref):
    r, c = acc_ref.shape
    for p in range(N_DEV):
        if COL_SHARDED[name]:
            out_ref[p] = acc_ref[:, c // N_DEV * p:c // N_DEV * (p + 1)].astype(BF16)
        else:
            out_ref[p] = acc_ref[r // N_DEV * p:r // N_DEV * (p + 1), :].astype(BF16)


def _hgrn_bwd(proj, lbl, do, sprev, dproj, pairs):
    s = proj.shape[0]
    t = HG_T
    nt = s // t
    npair = len(pairs)
    mcum, mrev, msum, bd = _hg_consts()

    def body(hq_ref, hf_ref, hi_ref, lbl_ref, do_ref, sp_ref, mcum_ref, mrev_ref, msum_ref, bd_ref,
             dproj_in, *rest):
        del dproj_in
        pair_refs, (dh_ref, dlbl_ref) = rest[:2 * npair], rest[2 * npair:2 * npair + 2]
        dw_refs, g_ref, acc_refs = rest[2 * npair + 2:3 * npair + 2], rest[3 * npair + 2], rest[3 * npair + 3:]

        @pl.when(pl.program_id(0) == 0)
        def _():
            g_ref[...] = jnp.zeros_like(g_ref)
            dlbl_ref[...] = jnp.zeros_like(dlbl_ref)
            for acc_ref in acc_refs:
                acc_ref[...] = jnp.zeros_like(acc_ref)

        for n, acc_ref in enumerate(acc_refs):
            acc_ref[...] += _dot(pair_refs[2 * n][...], pair_refs[2 * n + 1][...])

        @pl.when(pl.program_id(0) == nt - 1)
        def _():
            for (name, _, _), acc_ref, dw_ref in zip(pairs, acc_refs, dw_refs):
                _emit_slots(name, acc_ref, dw_ref)

        mc = mcum_ref[...]
        lb, sig, f, b, big_l, qd, ki, ke = _hg_pre(hq_ref[...].astype(F32), hf_ref[...].astype(F32), lbl_ref[...], mc,
                                                   msum_ref[...])
        el = jnp.exp(big_l)
        hi = hi_ref[...]
        dov = do_ref[...]
        head = lax.broadcasted_iota(jnp.int32, (t, GW), 1) >> 6
        head64 = lax.broadcasted_iota(jnp.int32, (64, GW), 1) >> 6
        mask = jnp.concatenate([mc] * HG_G, axis=0) > 0.5
        dqd_parts, dke_parts, dv_parts, del_parts, dki_parts = [], [], [], [], []
        for p in range(HEADS // HG_G):
            sl = slice(GW * p, GW * (p + 1))
            vp = hi[:, sl].astype(BF16)
            qs = _stack_heads(qd[:, sl], head).astype(BF16)
            kip = ki[:, sl].astype(BF16)
            dos = _stack_heads(dov[:, sl], head).astype(BF16)
            a = jnp.where(mask, _dotg(qs, kip, NT), 0.0).astype(BF16)
            da = jnp.where(mask, _dotg(dos, vp, NT), 0.0).astype(BF16)
            r = _dot(da, kip)
            dki_parts.append(_dotg(da, qs, TN))
            qb = qd[:, sl].astype(BF16)
            kb = ke[:, sl].astype(BF16)
            dob = dov[:, sl].astype(BF16)
            g = g_ref[p]
            dqd_c, dv_c, dke_c, del_c = [], [], [], []
            for c in range(HG_NC - 1, -1, -1):
                rows = slice(HG_BLOCK * c, HG_BLOCK * (c + 1))
                gb = g.astype(BF16)
                st = _expand_state(sp_ref[c, :, sl], head64)
                dqd_c.append(_dot(dob[rows], st.astype(BF16)))
                dv_c.append(_dotg(kb[rows], gb, NT))
                dke_c.append(_dot(vp[rows], gb))
                del_c.append(jnp.broadcast_to(jnp.sum(g * st, axis=0, keepdims=True), (HG_BLOCK, GW)))
                g = g * el[HG_BLOCK * c:HG_BLOCK * c + 1, sl] + _dotg(dob[rows], qb[rows], TN) * bd_ref[...]
            g_ref[p] = g
            up = lambda parts: jnp.concatenate(parts[::-1], axis=0)
            dqd_parts.append(_unstack_heads(r, head, t) + up(dqd_c))
            dv_parts.append(_dotg(a, dos, TN) + up(dv_c))
            dke_parts.append(up(dke_c))
            del_parts.append(up(del_c))
        wide = lambda parts: jnp.concatenate(parts, axis=1)
        dqd, dke, dki, dvv, del_rows = wide(dqd_parts), wide(dke_parts), wide(dki_parts), wide(dv_parts), wide(del_parts)
        dh_ref[:, :HG_WIDTH] = (dqd * jnp.exp(b)).astype(BF16)
        dh_ref[:, 2 * HG_WIDTH:] = dvv.astype(BF16)
        dke_ke = dke * ke
        db = dqd * qd - dki * ki - dke_ke
        dl_rows = _sel_left(msum_ref[...], dke_ke) + del_rows * el
        is_last = (lax.broadcasted_iota(jnp.int32, (t, HG_WIDTH), 0) & (HG_BLOCK - 1)) == HG_BLOCK - 1
        db = db + jnp.where(is_last, dl_rows, 0.0)
        dlf = _sel_left(mrev_ref[...], db)
        dk = dki * jnp.exp(-b) + dke * jnp.exp(big_l - b)
        df = dlf / f - dk
        dh_ref[:, HG_WIDTH:2 * HG_WIDTH] = (df * (1.0 - lb) * sig * (1.0 - sig)).astype(BF16)
        dlb = jnp.sum(df * (1.0 - sig), axis=0, keepdims=True) * lb * (1.0 - lb)
        dlbl_ref[0:1, :] += dlb
        dlbl_ref[1:2, :] -= dlb

    rrow = lambda j: pl.BlockSpec((t, HG_WIDTH), lambda i: (nt - 1 - i, j))
    full = lambda a: pl.BlockSpec(a.shape, lambda i: (0, 0))
    pair_specs, dw_specs, dw_shapes, accs = [], [], [], []
    for name, at, b in pairs:
        pair_specs += [pl.BlockSpec((at.shape[0], t), lambda i: (0, i)), pl.BlockSpec((t, b.shape[1]), lambda i: (i, 0))]
        shape = _slot_shape(name, at.shape[0], b.shape[1])
        dw_specs.append(pl.BlockSpec(shape, lambda i: (0, 0, 0)))
        dw_shapes.append(jax.ShapeDtypeStruct(shape, BF16))
        accs.append(pltpu.VMEM((at.shape[0], b.shape[1]), F32))
    return pl.pallas_call(
        body,
        grid=(nt,),
        in_specs=[rrow(6), rrow(7), rrow(8), full(lbl), rrow(0),
                  pl.BlockSpec((HG_NC, 64, HG_WIDTH), lambda i: (nt - 1 - i, 0, 0)),
                  full(mcum), full(mrev), full(msum), full(bd), pl.BlockSpec(memory_space=pl.ANY)] + pair_specs,
        out_specs=[pl.BlockSpec((t, 3 * HG_WIDTH), lambda i: (nt - 1 - i, 2)),
                   pl.BlockSpec((2, HG_WIDTH), lambda i: (0, 0))] + dw_specs,
        out_shape=[jax.ShapeDtypeStruct(dproj.shape, BF16), jax.ShapeDtypeStruct((2, HG_WIDTH), F32)] + dw_shapes,
        input_output_aliases={10: 0},
        scratch_shapes=[pltpu.VMEM((HEADS // HG_G, GW, GW), F32)] + accs,
        compiler_params=_params(("arbitrary",)),
        name="hgrn_bwd",
    )(proj, proj, proj, lbl, do, sprev, mcum, mrev, msum, bd, dproj, *[a for pair in pairs for a in pair[1:]])


def _tail(x, tgt, proj, attn, o, w_a, w_b, w_out, w_at, w_bt, w_outt, b_gate, g_post, gh):
    s = x.shape[0]
    tm = 256
    ones64 = (jnp.arange(HG_WIDTH)[:, None] // 64 == jnp.arange(HG_WIDTH)[None, :] // 64).astype(BF16)
    weights = (w_a, w_b, w_out, w_at, w_bt, w_outt)

    def body(x_ref, t_ref, ml_ref, ga_ref, gb_ref, at_ref, o_ref, *rest):
        w_hbm, (bg_ref, gp_ref, gh_ref, ones_ref) = rest[:6], rest[6:10]
        (dout_ref, dpj_ref, dop_ref, do_ref, mt_ref, dy_ref, yat_ref, dya_ref, ybt_ref, dyb_ref,
         loss_ref, dgp_ref, dbg_ref, dgh_ref) = rest[10:24]
        (wa_ref, wb_ref, wo_ref, wat_ref, wbt_ref, wot_ref), w_sem = rest[24:30], rest[30]

        @pl.when(pl.program_id(0) == 0)
        def _():
            loads = [pltpu.make_async_copy(src, dst, w_sem.at[k])
                     for k, (src, dst) in enumerate(zip(w_hbm, rest[24:30]))]
            _start_all(loads, [])
            loss_ref[...] = jnp.zeros_like(loss_ref)
            dgp_ref[...] = jnp.zeros_like(dgp_ref)
            dbg_ref[...] = jnp.zeros_like(dbg_ref)
            dgh_ref[...] = jnp.zeros_like(dgh_ref)
            _wait_all(loads, [])

        ones = ones_ref[...]
        gate_a = ga_ref[...].astype(F32)
        sa = _sigmoid(gate_a)
        silu_a = gate_a * sa
        attn_v = at_ref[...]
        ya_in = attn_v * silu_a
        ov = o_ref[...]
        ro = lax.rsqrt(_sel_right(ov * ov, ones) * (1.0 / 64.0) + EPS)
        ohat = ov * ro
        ghv = gh_ref[...]
        on = ohat * ghv
        gate_b = gb_ref[...].astype(F32)
        sb = _sigmoid(gate_b)
        silu_b = gate_b * sb
        yb_in = on * silu_b
        ya_bf = ya_in.astype(BF16)
        yb_bf = yb_in.astype(BF16)
        yat_ref[...] = ya_bf.T
        ybt_ref[...] = yb_bf.T
        y_a = _dot(ya_bf, wa_ref[...])
        y_b = _dot(yb_bf, wb_ref[...])
        gts = _sigmoid(ml_ref[...].astype(F32) + bg_ref[...])
        g_a = gts[:, :D_MODEL]
        g_b = gts[:, D_MODEL:]
        m_bf = (g_a * y_a + g_b * y_b).astype(BF16)
        mt_ref[...] = m_bf.T
        y = _dot(m_bf, wo_ref[...])
        r1 = lax.rsqrt(jnp.mean(y * y, axis=-1, keepdims=True) + EPS)
        yn = y * r1
        gp = gp_ref[...]
        e = x_ref[...] + yn * gp - t_ref[...]
        loss_ref[...] += jnp.sum(e * e, axis=0, keepdims=True)
        dout = e * (1.0 / D_MODEL)
        dout_ref[...] = dout
        dgp_ref[...] += jnp.sum(dout * yn, axis=0, keepdims=True)
        dyn = dout * gp
        dy = r1 * (dyn - yn * jnp.mean(dyn * yn, axis=-1, keepdims=True))
        dy_bf = dy.astype(BF16)
        dy_ref[...] = dy_bf
        dm = _dot(dy_bf, wot_ref[...])
        dml_a = dm * y_a * g_a * (1.0 - g_a)
        dml_b = dm * y_b * g_b * (1.0 - g_b)
        dpj_ref[:, :D_MODEL] = dml_a.astype(BF16)
        dpj_ref[:, D_MODEL:2 * D_MODEL] = dml_b.astype(BF16)
        dbg_ref[:, :D_MODEL] += jnp.sum(dml_a, axis=0, keepdims=True)
        dbg_ref[:, D_MODEL:] += jnp.sum(dml_b, axis=0, keepdims=True)
        dya_bf = (dm * g_a).astype(BF16)
        dyb_bf = (dm * g_b).astype(BF16)
        dya_ref[...] = dya_bf
        dyb_ref[...] = dyb_bf
        dya_in = _dot(dya_bf, wat_ref[...])
        dyb_in = _dot(dyb_bf, wbt_ref[...])
        dattn = dya_in * silu_a
        delta = _sel_right(dattn * attn_v, ones)
        lane = lax.broadcasted_iota(jnp.int32, (tm, LANE), 1)
        for p in range(HEADS // 2):
            sl = slice(LANE * p, LANE * (p + 1))
            xs = (dattn[:, sl], pltpu.roll(dattn[:, sl], VDIM, 1))
            nds = (-pltpu.roll(delta[:, sl], VDIM, 1), -delta[:, sl])
            for a in range(2):
                hi, lo_part = _hi_lo(nds[a])
                blk = jnp.where(lane < VDIM, xs[a], jnp.where(lane == VDIM, hi, jnp.where(lane == VDIM + 1, lo_part, 0.0)))
                dop_ref[:, LANE * (2 * p + a):LANE * (2 * p + a + 1)] = blk.astype(BF16)
        dpj_ref[:, 2 * D_MODEL:2 * D_MODEL + HG_WIDTH] = (
            dya_in * attn_v * (sa * (1.0 + gate_a * (1.0 - sa)))).astype(BF16)
        don = dyb_in * silu_b
        dpj_ref[:, 2 * D_MODEL + HG_WIDTH:] = (dyb_in * on * (sb * (1.0 + gate_b * (1.0 - sb)))).astype(BF16)
        dgh_ref[...] += jnp.sum(don * ohat, axis=0, keepdims=True)
        dohat = don * ghv
        do_ref[...] = (ro * (dohat - ohat * (_sel_right(dohat * ohat, ones) * (1.0 / 64.0)))).astype(BF16)

    row = lambda w, j: pl.BlockSpec((tm, w), lambda i: (i, j))
    col = lambda w: pl.BlockSpec((w, tm), lambda i: (0, i))
    full = lambda a: pl.BlockSpec(a.shape, lambda i: (0, 0))
    acc = lambda w: pl.BlockSpec((1, w), lambda i: (0, 0))
    sds = lambda w, dt: jax.ShapeDtypeStruct((s, w), dt)
    sdt = lambda w: jax.ShapeDtypeStruct((w, s), BF16)
    return pl.pallas_call(
        body,
        grid=(s // tm,),
        in_specs=[row(1024, 0), row(1024, 0), row(2048, 0), row(512, 4), row(512, 5), row(512, 0), row(512, 0)]
        + [ANY] * 6 + [full(b_gate), full(g_post), full(gh), full(ones64)],
        out_specs=[row(1024, 0), row(3072, 0), row(1024, 0), row(512, 0),
                   col(1024), row(1024, 0), col(512), row(1024, 0), col(512), row(1024, 0),
                   acc(1024), acc(1024), acc(2048), acc(512)],
        out_shape=[sds(1024, F32), sds(D_IN_PAD, BF16), sds(1024, BF16), sds(512, BF16),
                   sdt(1024), sds(1024, BF16), sdt(512), sds(1024, BF16), sdt(512), sds(1024, BF16),
                   jax.ShapeDtypeStruct((1, 1024), F32), jax.ShapeDtypeStruct((1, 1024), F32),
                   jax.ShapeDtypeStruct((1, 2048), F32), jax.ShapeDtypeStruct((1, 512), F32)],
        scratch_shapes=[pltpu.VMEM(a.shape, BF16) for a in weights] + [pltpu.SemaphoreType.DMA((6,))],
        compiler_params=_params(("arbitrary",), 56),
        name="tail",
    )(x, tgt, proj, proj, proj, attn, o, *weights, b_gate, g_post, gh, ones64)


def _mla_bwd(proj, dqr, dkr, dv, g_q, g_kv, w_uq_pt, w_kv_pt, rc, rs1, rs2, cqt, ckvt, dproj):
    assert HEADS == N_DEV
    s = proj.shape[0]
    tm = 512
    scale = 1.0 / math.sqrt(QK)

    def body(cq_ref, ckv_ref, dqr_ref, dkr_ref, dv_ref, gq_ref, gkv_ref, wuqt_ref, wkvt_ref, c_ref, s1_ref, s2_ref,
             cqt_ref, ckvt_ref, dproj_in, dc_ref, dgq_ref, dgkv_ref, uq_slots, ukv_slots,
             dqf_ref, dkvf_ref, dwuq_ref, dwkv_ref):
        del dproj_in

        @pl.when(pl.program_id(0) == 0)
        def _():
            dgq_ref[...] = jnp.zeros_like(dgq_ref)
            dgkv_ref[...] = jnp.zeros_like(dgkv_ref)
            dwuq_ref[...] = jnp.zeros_like(dwuq_ref)
            dwkv_ref[...] = jnp.zeros_like(dwkv_ref)

        c, s1, s2 = c_ref[...], s1_ref[...], s2_ref[...]
        lane = lax.broadcasted_iota(jnp.int32, (tm, LANE), 1)
        ksum = jnp.zeros((tm, LANE), F32)
        for h in range(HEADS):
            sl = slice(LANE * h, LANE * (h + 1))
            dqf_ref[:, sl] = (_unrope(dqr_ref[:, sl], c, s1, s2) * scale).astype(BF16)
            dkh = dkr_ref[:, sl]
            ksum = ksum + dkh
            dkvf_ref[:, sl] = jnp.where(lane < NOPE, dkh, 0.0).astype(BF16)
            dkvf_ref[:, HEADS * LANE + LANE * h:HEADS * LANE + LANE * (h + 1)] = jnp.where(
                lane < VDIM, dv_ref[:, sl], 0.0).astype(BF16)
        dkpe = _unrope(ksum, c, s1, s2)
        dc_ref[:, Q_LORA + KV_LORA:] = jnp.where((lane >= NOPE) & (lane < QK), dkpe, 0.0).astype(BF16)
        dqf, dkvf = dqf_ref[...], dkvf_ref[...]
        dwuq_ref[...] += _dot(cqt_ref[...], dqf)
        dwkv_ref[...] += _dot(ckvt_ref[...], dkvf)
        dcqn = _dot(dqf, wuqt_ref[...])
        dckvn = _dot(dkvf, wkvt_ref[...])
        for x_ref, g_ref, dn, cols, dg_ref in ((cq_ref, gq_ref, dcqn, slice(0, Q_LORA), dgq_ref),
                                               (ckv_ref, gkv_ref, dckvn, slice(Q_LORA, Q_LORA + KV_LORA), dgkv_ref)):
            xv = x_ref[...].astype(F32)
            r = lax.rsqrt(jnp.mean(xv * xv, axis=-1, keepdims=True) + EPS)
            xh = xv * r
            dg_ref[...] += jnp.sum(dn * xh, axis=0, keepdims=True)
            dh = dn * g_ref[...]
            dc_ref[:, cols] = (r * (dh - xh * jnp.mean(dh * xh, axis=-1, keepdims=True))).astype(BF16)

        @pl.when(pl.program_id(0) == s // tm - 1)
        def _():
            ur = Q_LORA // N_DEV
            for p in range(N_DEV):
                uq_slots[p] = jnp.concatenate(
                    [dwuq_ref[ur * p:ur * (p + 1), LANE * h:LANE * h + QK] for h in range(HEADS)], axis=1).astype(BF16)
                ukv_slots[p] = jnp.concatenate(
                    [dwkv_ref[:, LANE * p:LANE * p + NOPE],
                     dwkv_ref[:, LANE * (HEADS + p):LANE * (HEADS + p) + VDIM]], axis=1).astype(BF16)

    row = lambda w, j: pl.BlockSpec((tm, w), lambda i: (i, j))
    full = lambda a: pl.BlockSpec(a.shape, lambda i: (0, 0))
    acc = lambda w: pl.BlockSpec((1, w), lambda i: (0, 0))
    col = lambda w: pl.BlockSpec((w, tm), lambda i: (0, i))
    whole = lambda shape: pl.BlockSpec(shape, lambda i: (0, 0, 0))
    uq_shape = (N_DEV, Q_LORA // N_DEV, HEADS * QK)
    ukv_shape = (N_DEV, KV_LORA, NOPE + VDIM)
    return pl.pallas_call(
        body,
        grid=(s // tm,),
        in_specs=[row(768, 6), row(256, 21), row(1024, 0), row(1024, 0), row(1024, 0), full(g_q), full(g_kv),
                  full(w_uq_pt), full(w_kv_pt), row(128, 0), row(128, 0), row(128, 0), col(Q_LORA), col(KV_LORA),
                  pl.BlockSpec(memory_space=pl.ANY)],
        out_specs=[row(1152, 4), acc(768), acc(256), whole(uq_shape), whole(ukv_shape)],
        out_shape=[jax.ShapeDtypeStruct(dproj.shape, BF16),
                   jax.ShapeDtypeStruct((1, 768), F32), jax.ShapeDtypeStruct((1, 256), F32),
                   jax.ShapeDtypeStruct(uq_shape, BF16), jax.ShapeDtypeStruct(ukv_shape, BF16)],
        input_output_aliases={14: 0},
        scratch_shapes=[pltpu.VMEM((tm, HEADS * LANE), BF16), pltpu.VMEM((tm, 2 * HEADS * LANE), BF16),
                        pltpu.VMEM((Q_LORA, HEADS * LANE), F32), pltpu.VMEM((KV_LORA, 2 * HEADS * LANE), F32)],
        compiler_params=_params(("arbitrary",)),
        name="mla_bwd",
    )(proj, proj, dqr, dkr, dv, g_q, g_kv, w_uq_pt, w_kv_pt, rc, rs1, rs2, cqt, ckvt, dproj)


def _dh_dx(dproj, w_in_pt, x, dout, g_pre, sends):
    s, k = dproj.shape
    tm = 256
    ns, ni = len(sends), s // tm

    def body(dp_ref, w_ref, x_ref, dout_ref, g_ref, *rest):
        send_refs, (dx_ref, dg_ref) = rest[:ns], rest[ns:ns + 2]
        recv_refs, sems = rest[ns + 2:2 * ns + 2], rest[2 * ns + 2:]

        @pl.when(pl.program_id(0) == 0)
        def _():
            _start_all(*_to_chips_copies(send_refs, recv_refs, sems))
            dg_ref[...] = jnp.zeros_like(dg_ref)

        dh = _dot(dp_ref[...], w_ref[...])
        xv = x_ref[...]
        r = lax.rsqrt(jnp.mean(xv * xv, axis=-1, keepdims=True) + EPS)
        xh = xv * r
        dg_ref[...] += jnp.sum(dh * xh, axis=0, keepdims=True)
        dxh = dh * g_ref[...]
        dx_ref[...] = dout_ref[...] + r * (dxh - xh * jnp.mean(dxh * xh, axis=-1, keepdims=True))

        @pl.when(pl.program_id(0) == ni - 1)
        def _():
            _wait_all(*_to_chips_copies(send_refs, recv_refs, sems))

    row = lambda w: pl.BlockSpec((tm, w), lambda i: (i, 0))
    return pl.pallas_call(
        body,
        grid=(ni,),
        in_specs=[row(k), pl.BlockSpec((k, D_MODEL), lambda i: (0, 0)), row(D_MODEL), row(D_MODEL),
                  pl.BlockSpec((1, D_MODEL), lambda i: (0, 0))] + [ANY] * ns,
        out_specs=[row(D_MODEL), pl.BlockSpec((1, D_MODEL), lambda i: (0, 0))] + [ANY] * ns,
        out_shape=[jax.ShapeDtypeStruct((s, D_MODEL), F32), jax.ShapeDtypeStruct((1, D_MODEL), F32)]
        + [jax.ShapeDtypeStruct(a.shape, a.dtype) for a in sends],
        scratch_shapes=_copy_sems(ns, 3),
        compiler_params=_params(("arbitrary",)),
        name="dh_dx",
    )(dproj, w_in_pt, x, dout, g_pre, *sends)


def _pair_reduce(slots):
    n = len(slots)
    half = [(N_DEV // 2,) + a.shape[1:] for a in slots]

    def body(*refs):
        s_refs, o_refs = refs[:n], refs[n:2 * n]
        mine, got = refs[2 * n:3 * n], refs[3 * n:4 * n]
        send_sems, recv_sems, local_sems, out_sems = refs[4 * n:]
        x, y, c = _my_place()
        copies, loads, stores = [], [], []
        for q in range(N_DEV // 2):
            for a in range(n):
                copies.append(pltpu.make_async_remote_copy(
                    src_ref=s_refs[a].at[2 * q + 1 - c], dst_ref=got[a].at[q],
                    send_sem=send_sems.at[4 * a + q], recv_sem=recv_sems.at[4 * a + q],
                    device_id=(x, y, 1 - c), device_id_type=MESH_ID))
                loads.append(pltpu.make_async_copy(s_refs[a].at[2 * q + c], mine[a].at[q], local_sems.at[4 * a + q]))
                stores.append(pltpu.make_async_copy(mine[a].at[q], o_refs[a].at[q], out_sems.at[4 * a + q]))
        _start_all(loads, copies)
        k = 0
        for q in range(N_DEV // 2):
            for a in range(n):
                loads[k].wait()
                copies[k].wait_recv()
                mine[a][q] = (mine[a][q].astype(F32) + got[a][q].astype(F32)).astype(mine[a].dtype)
                stores[k].start()
                k += 1
        for cp in copies:
            cp.wait_send()
        for cp in stores:
            cp.wait()

    vm = lambda: [pltpu.VMEM(h, a.dtype) for h, a in zip(half, slots)]
    return pl.pallas_call(
        body,
        in_specs=[ANY] * n,
        out_specs=[ANY] * n,
        out_shape=[jax.ShapeDtypeStruct(h, a.dtype) for h, a in zip(half, slots)],
        scratch_shapes=vm() + vm() + [pltpu.SemaphoreType.DMA((4 * n,)), pltpu.SemaphoreType.DMA((4 * n,)),
                                      pltpu.SemaphoreType.DMA((4 * n,)), pltpu.SemaphoreType.DMA((4 * n,))],
        compiler_params=pltpu.CompilerParams(vmem_limit_bytes=48 * 2**20),
        name="pair_reduce",
    )(*slots)


def _rope_tables(s):
    inv = (np.float32(ROPE_THETA) ** (-np.arange(0, ROPE, 2, dtype=np.float32) / np.float32(ROPE))).astype(np.float32)
    ang = (np.arange(s, dtype=np.float32)[:, None] * inv[None, :]).astype(np.float32)
    cos, sin = jnp.asarray(np.cos(ang.astype(np.float64)), F32), jnp.asarray(np.sin(ang.astype(np.float64)), F32)
    z = lambda w: jnp.zeros((s, w), F32)
    rc = jnp.concatenate([jnp.ones((s, NOPE), F32), cos, cos, z(32)], axis=1)
    rs1 = jnp.concatenate([z(NOPE), -sin, z(16), z(32)], axis=1)
    rs2 = jnp.concatenate([z(NOPE), z(16), sin, z(32)], axis=1)
    return rc, rs1, rs2


def _step(x, tgt, w_blk, shards, g_pre, b_gate, g_q, g_kv, lbl, g_hgrn, g_post):
    s = x.shape[0]
    rc, rs1, rs2 = _rope_tables(s)
    gh = jnp.tile(g_hgrn, (1, HEADS))

    proj, w_in_pt, ht, *got = _gather_proj(x, g_pre, w_blk, shards[:2])
    w_uq, w_ukv = (_from_slots(n, g) for n, g in zip(MATS[:2], got))
    w_uq_p = jnp.pad(w_uq.reshape(Q_LORA, HEADS, QK), ((0, 0), (0, 0), (0, LANE - QK))).reshape(Q_LORA, HEADS * LANE)
    kv3 = w_ukv.reshape(KV_LORA, HEADS, NOPE + VDIM)
    pad64 = lambda t: jnp.pad(t, ((0, 0), (0, 0), (0, LANE - 64))).reshape(KV_LORA, HEADS * LANE)
    w_kv_p = jnp.concatenate([pad64(kv3[:, :, :NOPE]), pad64(kv3[:, :, NOPE:])], axis=1)

    qr, kr, v, cqt, ckvt = _mla_prep(proj, g_q, g_kv, w_uq_p, w_kv_p, rc, rs1, rs2)
    attn, qa, *got = _attn_fwd(qr, kr, v, shards[2:])
    w_a, w_b, w_out = (_from_slots(n, g) for n, g in zip(MATS[2:], got))
    o, sprev = _hgrn_fwd(proj, lbl)
    (dout, dproj, dop, do, mt, dy_bf, yat, dya_bf, ybt, dyb_bf,
     loss_vec, dg_post, db_gate, dgh) = _tail(x, tgt, proj, attn, o, w_a, w_b, w_out, w_a.T, w_b.T, w_out.T,
                                               b_gate, g_post, gh)
    dproj, dlbl, *early = _hgrn_bwd(proj, lbl, do, sprev, dproj,
                                    [("w_branch_a", yat, dya_bf), ("w_branch_b", ybt, dyb_bf), ("w_out", mt, dy_bf)])
    dqr, dkr, dv, *early_recv = _attn_bwd(qa, kr, v, dop, early)
    dproj, dg_q, dg_kv, dw_uq_slots, dw_ukv_slots = _mla_bwd(proj, dqr, dkr, dv, g_q, g_kv, w_uq_p.T, w_kv_p.T,
                                                             rc, rs1, rs2, cqt, ckvt, dproj)

    dw_in_slots = _dw_in_slots(ht, dproj)
    late = _pair_reduce([dw_in_slots, dw_uq_slots, dw_ukv_slots])
    dx, dg_pre, *late_recv = _dh_dx(dproj, w_in_pt, x, dout, g_pre, late)

    g_sum = _vectors_sum(dg_pre, db_gate, dg_q, dg_kv, dlbl, dgh, dg_post, loss_vec)
    return dx, late_recv[0], dict(zip(MATS, late_recv[1:] + early_recv)), g_sum


def _adamw(g, w, m, v):
    c1 = 1.0 / (1.0 - ADAM_B1 ** ADAM_STEP)
    c2 = 1.0 / (1.0 - ADAM_B2 ** ADAM_STEP)
    nm = ADAM_B1 * m + (1.0 - ADAM_B1) * g
    nv = ADAM_B2 * v + (1.0 - ADAM_B2) * (g * g)
    d = -ADAM_LR * ((nm * c1) / (jnp.sqrt(nv * c2) + ADAM_EPS) + ADAM_WD * w)
    return d, nm, nv


def _sum8(r_ref):
    g = r_ref[0].astype(F32)
    for k in range(1, r_ref.shape[0]):
        g = g + r_ref[k].astype(F32)
    return g


def _sum_adamw_w_in(recv, w, m, v):
    rows, _, cols = w.shape
    tc = 256
    nc = cols // tc

    def body(r_ref, w_hbm, m_hbm, v_hbm, g_hbm, d_hbm, nm_hbm, nv_hbm, ins, outs, in_sems, out_sems):
        i = pl.program_id(0)
        slot = i & 1
        cols_of = lambda step: pl.ds(pl.multiple_of(step * tc, tc), tc)

        def load(k, step, sl):
            return pltpu.make_async_copy((w_hbm, m_hbm, v_hbm)[k].at[:, 0, cols_of(step)], ins.at[sl, k],
                                         in_sems.at[sl, k])

        def store(k, step, sl):
            return pltpu.make_async_copy(outs.at[sl, k], (g_hbm, d_hbm, nm_hbm, nv_hbm)[k].at[:, 0, cols_of(step)],
                                         out_sems.at[sl, k])

        @pl.when(i == 0)
        def _():
            for k in range(3):
                load(k, 0, 0).start()

        @pl.when(i + 1 < nc)
        def _():
            for k in range(3):
                load(k, i + 1, 1 - slot).start()

        @pl.when(i >= 2)
        def _():
            for k in range(4):
                store(k, i - 2, slot).wait()

        for k in range(3):
            load(k, i, slot).wait()
        g = _sum8(r_ref)
        d, nm, nv = _adamw(g, ins[slot, 0], ins[slot, 1], ins[slot, 2])
        for k, val in enumerate((g, d, nm, nv)):
            outs[slot, k] = val
        for k in range(4):
            store(k, i, slot).start()

        @pl.when(i == nc - 1)
        def _():
            for k in range(4):
                store(k, i, slot).wait()
            if nc >= 2:
                for k in range(4):
                    store(k, i - 1, 1 - slot).wait()

    out = jax.ShapeDtypeStruct((rows, 1, cols), F32)
    return pl.pallas_call(
        body,
        grid=(nc,),
        in_specs=[pl.BlockSpec((recv.shape[0], rows, tc), lambda i: (0, 0, i)), ANY, ANY, ANY],
        out_specs=[ANY, ANY, ANY, ANY],
        out_shape=[out, out, out, out],
        scratch_shapes=[pltpu.VMEM((2, 3, rows, tc), F32), pltpu.VMEM((2, 4, rows, tc), F32),
                        pltpu.SemaphoreType.DMA((2, 3)), pltpu.SemaphoreType.DMA((2, 4))],
        compiler_params=_params(("arbitrary",)),
        name="sum_adamw_w_in",
    )(recv, w, m, v)


def _sum_adamw_whole(recvs, ws, ms, vs):
    n = len(ws)

    def body(*refs):
        r_refs, w_refs, m_refs, v_refs = refs[:n], refs[n:2 * n], refs[2 * n:3 * n], refs[3 * n:4 * n]
        outs = refs[4 * n:]
        for a in range(n):
            g = _sum8(r_refs[a])
            d, nm, nv = _adamw(g, w_refs[a][...], m_refs[a][...], v_refs[a][...])
            outs[a][...] = g
            outs[n + a][...] = d
            outs[2 * n + a][...] = nm
            outs[3 * n + a][...] = nv

    shapes = [jax.ShapeDtypeStruct(w.shape, F32) for w in ws]
    res = pl.pallas_call(
        body,
        out_shape=shapes * 4,
        compiler_params=pltpu.CompilerParams(vmem_limit_bytes=48 * 2**20),
        name="sum_adamw_mats",
    )(*recvs, *ws, *ms, *vs)
    return res[:n], res[n:2 * n], res[2 * n:3 * n], res[3 * n:]


SMALL = ("g_pre", "b_gate", "g_q", "g_kv", "lb_logits", "g_hgrn", "g_post")
SMALL_SHAPE = dict(g_pre=(1, 1024), b_gate=(1, 2048), g_q=(1, 768), g_kv=(1, 256), lb_logits=(2, 512),
                   g_hgrn=(1, 64), g_post=(1, 1024))


def _vectors_sum(dg_pre, db_gate, dg_q, dg_kv, dlbl, dgh, dg_post, loss_vec):
    def body(gpre_ref, bg_ref, gq_ref, gkv_ref, lbl_ref, gh_ref, gpost_ref, loss_ref, out_ref, mine, got,
             send_sems, recv_sems):
        mine[...] = jnp.zeros_like(mine)
        mine[0:1, :] = gpre_ref[...]
        mine[1:2, :] = bg_ref[:, :1024]
        mine[2:3, :] = bg_ref[:, 1024:]
        mine[3:4, :Q_LORA] = gq_ref[...]
        mine[4:5, :KV_LORA] = gkv_ref[...]
        loss = (0.5 / D_MODEL) * jnp.sum(loss_ref[...], axis=-1, keepdims=True)
        mine[4:5, KV_LORA:] = jnp.broadcast_to(loss, (1, 1024 - KV_LORA))
        mine[5:6, :HG_WIDTH] = lbl_ref[0:1, :]
        mine[5:6, HG_WIDTH:] = lbl_ref[1:2, :]
        gh = gh_ref[...]
        fold = gh[:, :VDIM]
        for h in range(1, HEADS):
            fold = fold + gh[:, VDIM * h:VDIM * (h + 1)]
        mine[6:7, :VDIM] = fold
        mine[7:8, :] = gpost_ref[...]
        x, y, c = _my_place()
        me = 4 * x + 2 * y + c
        got[me] = mine[...]
        copies = [pltpu.make_async_remote_copy(
            src_ref=mine, dst_ref=got.at[me], send_sem=send_sems.at[k], recv_sem=recv_sems.at[k],
            device_id=_flip(k, x, y, c), device_id_type=MESH_ID) for k in range(N_DEV - 1)]
        _start_all([], copies)
        _wait_all([], copies)
        out_ref[...] = _sum8(got)

    return pl.pallas_call(
        body,
        out_shape=jax.ShapeDtypeStruct((8, 1024), F32),
        scratch_shapes=[pltpu.VMEM((8, 1024), F32), pltpu.VMEM((N_DEV, 8, 1024), F32),
                        pltpu.SemaphoreType.DMA((7,)), pltpu.SemaphoreType.DMA((7,))],
        name="vectors_sum",
    )(dg_pre, db_gate, dg_q, dg_kv, dlbl, dgh, dg_post, loss_vec)


def _vectors_adamw(g_sum, ws, ms, vs):
    n = len(SMALL)

    def body(g_ref, *refs):
        w_refs, m_refs, v_refs = refs[:n], refs[n:2 * n], refs[2 * n:3 * n]
        loss_ref, outs = refs[3 * n], refs[3 * n + 1:]
        g = g_ref[...]
        loss_ref[...] = g[4:5, KV_LORA:KV_LORA + 1]
        grads = (g[0:1, :], jnp.concatenate([g[1:2, :], g[2:3, :]], axis=1), g[3:4, :Q_LORA], g[4:5, :KV_LORA],
                 jnp.concatenate([g[5:6, :HG_WIDTH], g[5:6, HG_WIDTH:]], axis=0), g[6:7, :VDIM], g[7:8, :])
        for a in range(n):
            d, nm, nv = _adamw(grads[a], w_refs[a][...], m_refs[a][...], v_refs[a][...])
            outs[a][...] = grads[a]
            outs[n + a][...] = d
            outs[2 * n + a][...] = nm
            outs[3 * n + a][...] = nv

    shapes = [jax.ShapeDtypeStruct(SMALL_SHAPE[k], F32) for k in SMALL]
    res = pl.pallas_call(
        body,
        out_shape=[jax.ShapeDtypeStruct((1, 1), F32)] + shapes * 4,
        name="vectors_adamw",
    )(g_sum, *ws, *ms, *vs)
    return res[0], res[1:n + 1], res[n + 1:2 * n + 1], res[2 * n + 1:3 * n + 1], res[3 * n + 1:]


MATS = ("w_uq", "w_ukv", "w_branch_a", "w_branch_b", "w_out")
COL_SHARDED = dict(w_uq=False, w_ukv=True, w_branch_a=True, w_branch_b=True, w_out=False)
ORDER = ("g_pre", "w_in", "b_gate", "g_q", "w_uq", "g_kv", "w_ukv", "lb_logits", "g_hgrn",
         "w_branch_a", "w_branch_b", "w_out", "g_post")


def _from_slots(name, slots):
    _, r, c = slots.shape
    if COL_SHARDED[name]:
        return slots.transpose(1, 0, 2).reshape(r, N_DEV * c)
    return slots.reshape(N_DEV * r, c)


def kernel(x, g_pre, w_in, b_gate, g_q, w_uq, g_kv, w_ukv, lb_logits, g_hgrn, w_branch_a, w_branch_b, w_out, g_post, loss_target, m_g_pre, m_w_in, m_b_gate, m_g_q, m_w_uq, m_g_kv, m_w_ukv, m_lb_logits, m_g_hgrn, m_w_branch_a, m_w_branch_b, m_w_out, m_g_post, v_g_pre, v_w_in, v_b_gate, v_g_q, v_w_uq, v_g_kv, v_w_ukv, v_lb_logits, v_g_hgrn, v_w_branch_a, v_w_branch_b, v_w_out, v_g_post):
    rows3 = lambda a: jnp.transpose(a, (2, 0, 1))
    w = dict(w_in=rows3(w_in), w_uq=w_uq[0], w_ukv=w_ukv[0], w_branch_a=w_branch_a[0], w_branch_b=w_branch_b[0],
             w_out=w_out[0], g_pre=g_pre, b_gate=b_gate, g_q=g_q, g_kv=g_kv, lb_logits=lb_logits, g_hgrn=g_hgrn,
             g_post=g_post)
    mom = dict(w_in=rows3(m_w_in), w_uq=m_w_uq[0], w_ukv=m_w_ukv[0], w_branch_a=m_w_branch_a[0],
               w_branch_b=m_w_branch_b[0], w_out=m_w_out[0], g_pre=m_g_pre, b_gate=m_b_gate, g_q=m_g_q, g_kv=m_g_kv,
               lb_logits=m_lb_logits, g_hgrn=m_g_hgrn, g_post=m_g_post)
    var = dict(w_in=rows3(v_w_in), w_uq=v_w_uq[0], w_ukv=v_w_ukv[0], w_branch_a=v_w_branch_a[0],
               w_branch_b=v_w_branch_b[0], w_out=v_w_out[0], g_pre=v_g_pre, b_gate=v_b_gate, g_q=v_g_q, g_kv=v_g_kv,
               lb_logits=v_lb_logits, g_hgrn=v_g_hgrn, g_post=v_g_post)

    w_blk = w["w_in"].reshape(W_IN_SHARD, D_MODEL).astype(BF16)
    dx, recv_in, recv, g_sum = _step(x[0], loss_target[0], w_blk, [w[n].astype(BF16) for n in MATS],
                                     g_pre, b_gate, g_q, g_kv, lb_logits, g_hgrn, g_post)

    g_in, d_in, m_in, v_in = _sum_adamw_w_in(recv_in, w["w_in"], mom["w_in"], var["w_in"])
    res = _sum_adamw_whole([recv[n] for n in MATS], *([t[n] for n in MATS] for t in (w, mom, var)))
    total, *vec = _vectors_adamw(g_sum, *([t[n] for n in SMALL] for t in (w, mom, var)))

    outs = []
    for mats, vecs, big in zip(res, vec, (g_in, d_in, m_in, v_in)):
        t = {**{n: a[None] for n, a in zip(MATS, mats)}, **dict(zip(SMALL, vecs)),
             "w_in": jnp.transpose(big, (1, 2, 0))}
        outs += [t[n] for n in ORDER]
    return (total.reshape(()), dx[None], *outs)
```

```python
import math

import jax
import jax.numpy as jnp
import numpy as np
from jax import lax
from jax.experimental import pallas as pl
from jax.experimental.pallas import tpu as pltpu

F32, BF16 = jnp.float32, jnp.bfloat16

D_MODEL = 1024
EPS = 1e-6
HEADS = 8
NOPE, ROPE, VDIM = 64, 32, 64
QK = NOPE + ROPE
Q_LORA, KV_LORA = 768, 256
ROPE_THETA = 10000.0
ATT_CHUNK_SHIFT = 6
HG_BLOCK = 32
HG_WIDTH = 512
D_IN = 5664
D_IN_PAD = 5760
W_IN_SHARD = D_IN // 8
N_DEV = 8
LANE = 128

ADAM_LR, ADAM_B1, ADAM_B2, ADAM_EPS, ADAM_WD, ADAM_STEP = 0.001, 0.9, 0.999, 1e-08, 0.01, 10

W_IN_SEGMENTS = ((3616, 5664, 0), (1056, 1568, 2048), (3104, 3616, 2560), (1568, 3104, 3072),
                 (0, 1024, 4608), (1024, 1056, 5696))

NT = (((1,), (1,)), ((), ()))
TN = (((0,), (0,)), ((), ()))
MESH_ID = pl.DeviceIdType.MESH


def _w_in_pieces():
    out = []
    for lo, hi, dst in W_IN_SEGMENTS:
        c = lo
        while c < hi:
            p = c // W_IN_SHARD
            e = min(hi, (p + 1) * W_IN_SHARD)
            out.append((p, c - p * W_IN_SHARD, e - p * W_IN_SHARD, dst + c - lo))
            c = e
    return out


def _params(sem, vmem_mb=48):
    return pltpu.CompilerParams(dimension_semantics=sem, vmem_limit_bytes=vmem_mb * 2**20)


def _dot(a, b):
    return jnp.dot(a, b, preferred_element_type=F32)


def _dotg(a, b, dims):
    return lax.dot_general(a, b, dims, preferred_element_type=F32)


def _split2(x):
    hi = x.astype(BF16)
    return hi, (x - hi.astype(F32)).astype(BF16)


def _sel_left(m01, x):
    hi, lo = _split2(x)
    return _dot(m01, hi) + _dot(m01, lo)


def _sel_right(x, m01):
    hi, lo = _split2(x)
    return _dot(hi, m01) + _dot(lo, m01)


def _hi_lo(x):
    hi = x.astype(BF16).astype(F32)
    return hi, x - hi


def _sigmoid(x):
    return 0.5 * jnp.tanh(0.5 * x) + 0.5


def _rope(x, c, s1, s2):
    return x * c + pltpu.roll(x, 112, 1) * s1 + pltpu.roll(x, 16, 1) * s2


def _unrope(d, c, s1, s2):
    return d * c + pltpu.roll(d * s1, 16, 1) + pltpu.roll(d * s2, 112, 1)


def _my_place():
    return lax.axis_index("x"), lax.axis_index("y"), lax.axis_index("c")


def _flip(k, x, y, c):
    fx, fy, fc = (k + 1) >> 2 & 1, (k + 1) >> 1 & 1, (k + 1) & 1
    return (1 - x if fx else x), (1 - y if fy else y), (1 - c if fc else c)


def _to_all_copies(s_refs, r_refs, sems, spread):
    send_sems, recv_sems, local_sems = sems
    x, y, c = _my_place()
    me = 4 * x + 2 * y + c
    src = (lambda a, p: s_refs[a]) if spread else (lambda a, p: s_refs[a].at[p])
    local = [pltpu.make_async_copy(src(a, me), r_refs[a].at[me], local_sems.at[a]) for a in range(len(s_refs))]
    remote = []
    for k in range(N_DEV - 1):
        px, py, pc = _flip(k, x, y, c)
        for a in range(len(s_refs)):
            remote.append(pltpu.make_async_remote_copy(
                src_ref=src(a, 4 * px + 2 * py + pc), dst_ref=r_refs[a].at[me],
                send_sem=send_sems.at[7 * a + k], recv_sem=recv_sems.at[7 * a + k],
                device_id=(px, py, pc), device_id_type=MESH_ID))
    return local, remote


def _to_chips_copies(s_refs, r_refs, sems):
    send_sems, recv_sems, local_sems = sems
    x, y, c = _my_place()
    me = 2 * x + y
    local = [pltpu.make_async_copy(s_refs[a].at[me], r_refs[a].at[me], local_sems.at[a]) for a in range(len(s_refs))]
    remote = []
    for k in range(3):
        px = 1 - x if (k + 1) >> 1 & 1 else x
        py = 1 - y if (k + 1) & 1 else y
        for a in range(len(s_refs)):
            remote.append(pltpu.make_async_remote_copy(
                src_ref=s_refs[a].at[2 * px + py], dst_ref=r_refs[a].at[me],
                send_sem=send_sems.at[3 * a + k], recv_sem=recv_sems.at[3 * a + k],
                device_id=(px, py, c), device_id_type=MESH_ID))
    return local, remote


def _start_all(local, remote):
    for cp in local + remote:
        cp.start()


def _wait_all(local, remote):
    for cp in remote:
        cp.wait_recv()
    for cp in remote:
        cp.wait_send()
    for cp in local:
        cp.wait()


def _copy_sems(n, peers):
    return [pltpu.SemaphoreType.DMA((peers * n,)), pltpu.SemaphoreType.DMA((peers * n,)),
            pltpu.SemaphoreType.DMA((n,))]


ANY = pl.BlockSpec(memory_space=pl.ANY)


def _dw_in_slots(ht, dproj):
    m, k = ht.shape
    n = dproj.shape[1]
    tn, tk = 1152, 2048
    nj, nk = n // tn, k // tk
    by_tile = [[] for _ in range(nj)]
    for p, lo, hi, dst in _w_in_pieces():
        while lo < hi:
            j = dst // tn
            cnt = min(hi - lo, (j + 1) * tn - dst)
            by_tile[j].append((p, lo, lo + cnt, dst - j * tn))
            lo, dst = lo + cnt, dst + cnt

    def body(a_ref, b_ref, s_ref, acc_ref):
        j, l = pl.program_id(0), pl.program_id(1)

        @pl.when(l == 0)
        def _():
            acc_ref[...] = jnp.zeros_like(acc_ref)

        acc_ref[...] += _dot(a_ref[...], b_ref[...])

        @pl.when(l == nk - 1)
        def _():
            at = acc_ref[...].T
            for jj in range(nj):
                @pl.when(j == jj)
                def _(jj=jj):
                    for p, lo, hi, d in by_tile[jj]:
                        s_ref[p, lo:hi, :] = at[d:d + hi - lo, :].astype(BF16)

    return pl.pallas_call(
        body,
        grid=(nj, nk),
        in_specs=[pl.BlockSpec((m, tk), lambda j, l: (0, l)), pl.BlockSpec((tk, tn), lambda j, l: (l, j))],
        out_specs=pl.BlockSpec((N_DEV, W_IN_SHARD, m), lambda j, l: (0, 0, 0)),
        out_shape=jax.ShapeDtypeStruct((N_DEV, W_IN_SHARD, m), BF16),
        scratch_shapes=[pltpu.VMEM((m, tn), F32)],
        compiler_params=_params(("arbitrary", "arbitrary"), 56),
        name="dw_in",
    )(ht, dproj)


GP_TN = 256
GP_COLS = 5888
GP_NT = GP_COLS // GP_TN


def _gp_tile_pieces():
    tiles = [[] for _ in range(GP_NT)]
    for p, lo, hi, dst in _w_in_pieces():
        while lo < hi:
            t = dst // GP_TN
            n = min(hi - lo, (t + 1) * GP_TN - dst)
            tiles[t].append((p, lo, lo + n, dst - t * GP_TN))
            lo, dst = lo + n, dst + n
    return tiles


def _gp_tables():
    pieces = _gp_tile_pieces()
    rank_of = {None: 0, 0: 1, 1: 2, 2: 2, 4: 3, 5: 3, 3: 4, 6: 5}
    order = np.zeros((N_DEV, GP_NT), np.int32)
    waits = np.zeros((N_DEV, GP_NT), np.int32)
    for me in range(N_DEV):
        x, y, c = me >> 2 & 1, me >> 1 & 1, me & 1
        chips = [(1 - x, y), (x, 1 - y), (1 - x, 1 - y)]

        def sem_of(p):
            px, py, pc = p >> 2 & 1, p >> 1 & 1, p & 1
            if (px, py) == (x, y):
                return None if pc == c else 0
            j = chips.index((px, py))
            return 1 + j if pc == c else 4 + j

        needs = [sorted({sem_of(p) for p, _, _, _ in tile} - {None}) for tile in pieces]
        ranks = [max([rank_of[k] for k in ks], default=0) for ks in needs]
        seq = sorted(range(GP_NT), key=lambda t: (ranks[t], t))
        seen = set()
        for step, t in enumerate(seq):
            order[me, step] = t
            new = [k for k in needs[t] if k not in seen]
            for k in new:
                waits[me, step] |= 1 << k
            seen.update(new)
        assert seen == set(range(7)), (me, seen)
    return order, waits


def _gather_proj(x, g_pre, w_blk, shards):
    s = x.shape[0]
    tx = 512
    ns = len(shards)
    tile_pieces = _gp_tile_pieces()
    order_np, waits_np = _gp_tables()
    xq, yq, cq = _my_place()
    me_out = 4 * xq + 2 * yq + cq
    order = lax.dynamic_index_in_dim(jnp.asarray(order_np), me_out, 0, keepdims=False)
    waits = lax.dynamic_index_in_dim(jnp.asarray(waits_np), me_out, 0, keepdims=False)

    def body(order_ref, waits_ref, x_hbm, g_ref, wblk_hbm, *rest):
        shard_refs, (proj_ref, wt_ref, ht_hbm), got_refs = rest[:ns], rest[ns:ns + 3], rest[ns + 3:2 * ns + 3]
        recv, h_ref, wtile, xbuf, htbuf = rest[2 * ns + 3:2 * ns + 8]
        send_sems, recv_sems, misc_sems = rest[2 * ns + 8:2 * ns + 11]
        sems = rest[2 * ns + 11:]
        t = pl.program_id(0)
        x_, y_, c = _my_place()
        sibling = (x_, y_, 1 - c)
        chips = [(1 - x_, y_), (x_, 1 - y_), (1 - x_, 1 - y_)]
        idx = lambda px, py, pc: 4 * px + 2 * py + pc
        me = idx(x_, y_, c)

        def copy(k, slot, to, src=None):
            return pltpu.make_async_remote_copy(
                src_ref=recv.at[slot] if src is None else src, dst_ref=recv.at[slot],
                send_sem=send_sems.at[k], recv_sem=recv_sems.at[k], device_id=to, device_id_type=MESH_ID)

        mine = pltpu.make_async_copy(wblk_hbm, recv.at[me], misc_sems.at[0])
        first = [copy(0, me, sibling, src=wblk_hbm)] + [copy(1 + j, me, (*chips[j], c), src=wblk_hbm) for j in range(2)]
        passed = [copy(4 + j, idx(*ch, c), sibling) for j, ch in enumerate(chips)]
        onward = [copy(3, idx(*chips[0], c), (*chips[1], c)), copy(3, idx(*chips[1], c), (*chips[0], c))]
        arrivals = ([copy(0, idx(x_, y_, 1 - c), sibling)] + [copy(1 + j, idx(*ch, c), sibling) for j, ch in enumerate(chips)]
                    + [copy(4 + j, idx(*ch, 1 - c), sibling) for j, ch in enumerate(chips)])

        @pl.when(t == 0)
        def _():
            mine.start()
            for cp in first:
                cp.start()
            _start_all(*_to_all_copies(shard_refs, got_refs, sems, True))

            def load(i):
                return pltpu.make_async_copy(x_hbm.at[pl.ds(i * tx, tx), :], xbuf.at[i & 1], misc_sems.at[1 + (i & 1)])

            def store(i):
                return pltpu.make_async_copy(htbuf.at[i & 1], ht_hbm.at[:, pl.ds(i * tx, tx)], misc_sems.at[3 + (i & 1)])

            load(0).start()
            for i in range(s // tx):
                if i + 1 < s // tx:
                    load(i + 1).start()
                load(i).wait()
                xv = xbuf[i & 1]
                r = lax.rsqrt(jnp.mean(xv * xv, axis=-1, keepdims=True) + EPS)
                h = (xv * r * g_ref[...]).astype(BF16)
                h_ref[i * tx:(i + 1) * tx, :] = h
                if i >= 2:
                    store(i - 2).wait()
                htbuf[i & 1] = h.T
                store(i).start()
            for i in range(max(s // tx - 2, 0), s // tx):
                store(i).wait()
            mine.wait()

        w = waits_ref[t]
        for k in range(7):
            @pl.when((w >> k) & 1 == 1)
            def _(k=k):
                arrivals[k].wait_recv()
                if 1 <= k <= 3:
                    passed[k - 1].start()
                if 1 <= k <= 2:
                    @pl.when(c == k - 1)
                    def _():
                        onward[k - 1].start()

        tile = order_ref[t]
        for tt in range(GP_NT):
            @pl.when(tile == tt)
            def _(tt=tt):
                covered = sorted((d, d + hi - lo) for _, lo, hi, d in tile_pieces[tt])
                at = 0
                for lo_z, hi_z in covered + [(GP_TN, GP_TN)]:
                    if lo_z > at:
                        wtile[at:lo_z, :] = jnp.zeros((lo_z - at, D_MODEL), BF16)
                    at = max(at, hi_z)
                for p, lo, hi, d in tile_pieces[tt]:
                    wtile[d:d + hi - lo, :] = recv[p, lo:hi, :]

        wt = wtile[...]
        wt_ref[...] = wt
        proj_ref[...] = _dotg(h_ref[...], wt, NT)

        @pl.when(t == GP_NT - 1)
        def _():
            for cp in first + passed + onward[:1]:
                cp.wait_send()
            _wait_all(*_to_all_copies(shard_refs, got_refs, sems, True))

    grid_spec = pltpu.PrefetchScalarGridSpec(
        num_scalar_prefetch=2,
        grid=(GP_NT,),
        in_specs=[ANY, pl.BlockSpec((1, D_MODEL), lambda t, o, w: (0, 0)), ANY] + [ANY] * ns,
        out_specs=[pl.BlockSpec((s, GP_TN), lambda t, o, w: (0, o[t])),
                   pl.BlockSpec((GP_TN, D_MODEL), lambda t, o, w: (o[t], 0)), ANY] + [ANY] * ns,
        scratch_shapes=[pltpu.VMEM((N_DEV, W_IN_SHARD, D_MODEL), BF16), pltpu.VMEM((s, D_MODEL), BF16),
                        pltpu.VMEM((GP_TN, D_MODEL), BF16), pltpu.VMEM((2, tx, D_MODEL), F32),
                        pltpu.VMEM((2, D_MODEL, tx), BF16),
                        pltpu.SemaphoreType.DMA((7,)), pltpu.SemaphoreType.DMA((7,)), pltpu.SemaphoreType.DMA((5,))]
        + _copy_sems(ns, 7),
    )
    return pl.pallas_call(
        body,
        grid_spec=grid_spec,
        out_shape=[jax.ShapeDtypeStruct((s, GP_COLS), F32),jax.ShapeDtypeStruct((GP_COLS, D_MODEL), BF16),
                   jax.ShapeDtypeStruct((D_MODEL, s), BF16)]
        + [jax.ShapeDtypeStruct((N_DEV,) + b.shape, b.dtype) for b in shards],
        compiler_params=_params(("arbitrary",), 56),
        name="gather_proj",
    )(order, waits, x, g_pre, w_blk, *shards)


def _mla_prep(proj, g_q, g_kv, w_uq_p, w_kv_p, rc, rs1, rs2):
    s = proj.shape[0]
    tm = 512
    scale = 1.0 / math.sqrt(QK)

    def body(cq_ref, ckv_ref, kpe_ref, gq_ref, gkv_ref, wuq_ref, wkv_ref, c_ref, s1_ref, s2_ref,
             qr_ref, kr_ref, v_ref, cqt_ref, ckvt_ref):
        cq = cq_ref[...]
        r = lax.rsqrt(jnp.mean(cq * cq, axis=-1, keepdims=True) + EPS)
        cqn = (cq * r * gq_ref[...]).astype(BF16)
        cqt_ref[...] = cqn.T
        q = _dot(cqn, wuq_ref[...])
        ckv = ckv_ref[...]
        r = lax.rsqrt(jnp.mean(ckv * ckv, axis=-1, keepdims=True) + EPS)
        ckvn = (ckv * r * gkv_ref[...]).astype(BF16)
        ckvt_ref[...] = ckvn.T
        kv = _dot(ckvn, wkv_ref[...])
        c, s1, s2 = c_ref[...], s1_ref[...], s2_ref[...]
        lane = lax.broadcasted_iota(jnp.int32, (tm, LANE), 1)
        kpe = _rope(kpe_ref[...], c, s1, s2) + jnp.where((lane == QK) | (lane == QK + 1), 1.0, 0.0)
        vone = jnp.where((lane == VDIM) | (lane == VDIM + 1), 1.0, 0.0)
        for h in range(HEADS):
            sl = slice(LANE * h, LANE * (h + 1))
            qr_ref[:, sl] = (_rope(q[:, sl], c, s1, s2) * scale).astype(BF16)
            kr_ref[:, sl] = (kv[:, sl] + kpe).astype(BF16)
            v_ref[:, sl] = (kv[:, HEADS * LANE + LANE * h:HEADS * LANE + LANE * (h + 1)] + vone).astype(BF16)

    row = lambda w, j: pl.BlockSpec((tm, w), lambda i: (i, j))
    col = lambda w: pl.BlockSpec((w, tm), lambda i: (0, i))
    full = lambda a: pl.BlockSpec(a.shape, lambda i: (0, 0))
    return pl.pallas_call(
        body,
        grid=(s // tm,),
        in_specs=[row(768, 6), row(256, 21), row(128, 44), full(g_q), full(g_kv), full(w_uq_p), full(w_kv_p),
                  row(128, 0), row(128, 0), row(128, 0)],
        out_specs=[row(1024, 0), row(1024, 0), row(1024, 0), col(768), col(256)],
        out_shape=[jax.ShapeDtypeStruct((s, 1024), BF16), jax.ShapeDtypeStruct((s, 1024), BF16),
                   jax.ShapeDtypeStruct((s, 1024), BF16), jax.ShapeDtypeStruct((768, s), BF16),
                   jax.ShapeDtypeStruct((256, s), BF16)],
        compiler_params=_params(("arbitrary",)),
        name="mla_prep",
    )(proj, proj, proj, g_q, g_kv, w_uq_p, w_kv_p, rc, rs1, rs2)


ATT_T = 512
ATT_FWD_HEADS = 4


def _chunk_mask(transposed):
    r = lax.broadcasted_iota(jnp.int32, (ATT_T, ATT_T), 0) >> ATT_CHUNK_SHIFT
    c = lax.broadcasted_iota(jnp.int32, (ATT_T, ATT_T), 1) >> ATT_CHUNK_SHIFT
    return (r <= c) if transposed else (c <= r)


def _attn_fwd(qr, kr, vp, shards):
    s = qr.shape[0]
    t = ATT_T
    g = ATT_FWD_HEADS
    ns = len(shards)

    def body(q_ref, k_ref, v_ref, *rest):
        shard_refs, (o_ref, qa_ref), got_refs = rest[:ns], rest[ns:ns + 2], rest[ns + 2:2 * ns + 2]
        sc_ref, sems = rest[2 * ns + 2], rest[2 * ns + 3:]
        qi = pl.program_id(1)

        @pl.when((pl.program_id(0) == 0) & (qi == 0))
        def _():
            _start_all(*_to_all_copies(shard_refs, got_refs, sems, True))
        lane = lax.broadcasted_iota(jnp.int32, (t, LANE), 1)
        sls = [slice(LANE * a, LANE * (a + 1)) for a in range(g)]
        qs = [q_ref[:, sl] for sl in sls]

        def scores(j):
            rows = pl.ds(pl.multiple_of(j * t, t), t)
            for a in range(g):
                sc_ref[j & 1, a] = _dotg(qs[a], k_ref[rows, sls[a]], NT)

        def step(j, carry, masked):
            rows = pl.ds(pl.multiple_of(j * t, t), t)
            out = []
            for a in range(g):
                m, acc = carry[a]
                sc = sc_ref[j & 1, a]
                if masked:
                    sc = jnp.where(_chunk_mask(False), sc, -1e30)
                m_new = jnp.maximum(m, jnp.max(sc, axis=-1, keepdims=True))
                p = jnp.exp(sc - m_new).astype(BF16)
                acc = jnp.exp(m - m_new) * acc + _dot(p, v_ref[rows, sls[a]])
                out.append((m_new, acc))
            return tuple(out)

        def loop(j, carry):
            carry = step(j, carry, False)
            scores(j + 1)
            return carry

        init = tuple((jnp.full((t, 1), -1e30, F32), jnp.zeros((t, LANE), F32)) for _ in range(g))
        scores(0)
        carry = lax.fori_loop(0, qi, loop, init)
        carry = step(qi, carry, True)
        outs = []
        for a in range(g):
            m, acc = carry[a]
            l = acc[:, VDIM:VDIM + 1]
            outs.append(acc / l)
            hi, lo_part = _hi_lo(-(m + jnp.log(l)))
            qa = jnp.where(lane == QK, hi, jnp.where(lane == QK + 1, lo_part, qs[a].astype(F32)))
            qa_ref[:, sls[a]] = qa.astype(BF16)
        for p in range(g // 2):
            o_ref[:, LANE * p:LANE * (p + 1)] = jnp.where(lane < VDIM, outs[2 * p], pltpu.roll(outs[2 * p + 1], VDIM, 1))

        @pl.when((pl.program_id(0) == HEADS // g - 1) & (qi == s // t - 1))
        def _():
            _wait_all(*_to_all_copies(shard_refs, got_refs, sems, True))

    return pl.pallas_call(
        body,
        grid=(HEADS // g, s // t),
        in_specs=[
            pl.BlockSpec((t, g * LANE), lambda h, i: (i, h)),
            pl.BlockSpec((s, g * LANE), lambda h, i: (0, h)),
            pl.BlockSpec((s, g * LANE), lambda h, i: (0, h)),
        ] + [ANY] * ns,
        out_specs=[
            pl.BlockSpec((t, g * VDIM), lambda h, i: (i, h)),
            pl.BlockSpec((t, g * LANE), lambda h, i: (i, h)),
        ] + [ANY] * ns,
        out_shape=[jax.ShapeDtypeStruct((s, 512), F32), jax.ShapeDtypeStruct((s, 1024), BF16)]
        + [jax.ShapeDtypeStruct((N_DEV,) + b.shape, b.dtype) for b in shards],
        scratch_shapes=[pltpu.VMEM((2, g, t, t), F32)] + _copy_sems(ns, 7),
        compiler_params=_params(("arbitrary", "arbitrary")),
        name="attn_fwd",
    )(qr, kr, vp, *shards)


def _attn_bwd(qa, kr, vp, dop, sends):
    s = qa.shape[0]
    t = ATT_T
    nq = s // t
    ns = len(sends)

    def body(q_ref, k_ref, v_ref, do_ref, *rest):
        send_refs, (dq_out, dk_out, dv_out) = rest[:ns], rest[ns:ns + 3]
        recv_refs = rest[ns + 3:2 * ns + 3]
        (dq_ref, dk_ref, dv_ref), sems = rest[2 * ns + 3:2 * ns + 6], rest[2 * ns + 6:]
        j = pl.program_id(1)
        sls = [slice(LANE * a, LANE * (a + 1)) for a in range(2)]

        @pl.when((pl.program_id(0) == 0) & (j == 0))
        def _():
            _start_all(*_to_all_copies(send_refs, recv_refs, sems, False))

        @pl.when(j == 0)
        def _():
            dq_ref[...] = jnp.zeros_like(dq_ref)

        dk_ref[...] = jnp.zeros_like(dk_ref)
        dv_ref[...] = jnp.zeros_like(dv_ref)
        ks = [k_ref[:, sl] for sl in sls]
        vs = [v_ref[:, sl] for sl in sls]

        def part(i, k_lo, k_n, q_lo, q_n, masked):
            rows = pl.ds(pl.multiple_of(i * t + q_lo, 256), q_n)
            keys = slice(k_lo, k_lo + k_n)
            for a in range(2):
                q = q_ref[rows, sls[a]]
                do = do_ref[rows, sls[a]]
                sc = _dotg(ks[a][keys], q, NT)
                if masked:
                    kc = lax.broadcasted_iota(jnp.int32, (k_n, q_n), 0) >> ATT_CHUNK_SHIFT
                    qc = lax.broadcasted_iota(jnp.int32, (k_n, q_n), 1) >> ATT_CHUNK_SHIFT
                    sc = jnp.where(kc <= qc, sc, -1e30)
                p = jnp.exp(sc)
                ds = (p * _dotg(vs[a][keys], do, NT)).astype(BF16)
                dv_ref[keys, sls[a]] += _dot(p.astype(BF16), do)
                dk_ref[keys, sls[a]] += _dot(ds, q)
                dq_ref[rows, sls[a]] += _dotg(ds, ks[a][keys], TN)

        half = t // 2
        part(j, 0, half, 0, t, True)
        part(j, half, half, half, half, True)

        def loop(i, c):
            part(i, 0, t, 0, t, False)
            return c

        lax.fori_loop(j + 1, nq, loop, 0)
        dk_out[...] = dk_ref[...].astype(BF16)
        dv_out[...] = dv_ref[...].astype(BF16)

        @pl.when(j == nq - 1)
        def _():
            dq_out[...] = dq_ref[...].astype(BF16)

        @pl.when((pl.program_id(0) == HEADS // 2 - 1) & (j == nq - 1))
        def _():
            _wait_all(*_to_all_copies(send_refs, recv_refs, sems, False))

    blk = pl.BlockSpec((t, 2 * LANE), lambda h, j: (j, h))
    whole = pl.BlockSpec((s, 2 * LANE), lambda h, j: (0, h))
    out = jax.ShapeDtypeStruct((s, 1024), BF16)
    return pl.pallas_call(
        body,
        grid=(HEADS // 2, nq),
        in_specs=[whole, blk, blk, whole] + [ANY] * ns,
        out_specs=[whole, blk, blk] + [ANY] * ns,
        out_shape=[out, out, out] + [jax.ShapeDtypeStruct(a.shape, a.dtype) for a in sends],
        scratch_shapes=[pltpu.VMEM((s, 2 * LANE), F32), pltpu.VMEM((t, 2 * LANE), F32),
                        pltpu.VMEM((t, 2 * LANE), F32)] + _copy_sems(ns, 7),
        compiler_params=_params(("arbitrary", "arbitrary")),
        name="attn_bwd",
    )(qa, kr, vp, dop, *sends)


HG_T = 256
HG_NC = HG_T // HG_BLOCK
HG_G = 4
GW = 64 * HG_G


def _hg_consts():
    r = jnp.arange(HG_T)[:, None]
    c = jnp.arange(HG_T)[None, :]
    same = (r // HG_BLOCK) == (c // HG_BLOCK)
    mcum = (same & (c <= r)).astype(BF16)
    mrev = (same & (c >= r)).astype(BF16)
    msum = same.astype(BF16)
    a = jnp.arange(GW) // 64
    bd = (a[:, None] == a[None, :]).astype(F32)
    return mcum, mrev, msum, bd


def _stack_heads(xg, head):
    return jnp.concatenate([jnp.where(head == h, xg, 0.0) for h in range(HG_G)], axis=0)


def _unstack_heads(r, head, t):
    out = r[(HG_G - 1) * t:]
    for h in range(HG_G - 2, -1, -1):
        out = jnp.where(head == h, r[h * t:(h + 1) * t], out)
    return out


def _compact_state(st):
    out = st[:64]
    for h in range(1, HG_G):
        out = out + st[64 * h:64 * (h + 1)]
    return out


def _expand_state(cs, head64):
    return jnp.concatenate([jnp.where(head64 == h, cs, 0.0) for h in range(HG_G)], axis=0)


def _hg_pre(hq, hf, lbl, mcum, msum):
    lb = _sigmoid(lbl[0:1, :] - lbl[1:2, :])
    sig = _sigmoid(hf)
    f = lb + (1.0 - lb) * sig
    lf = jnp.log(f)
    b = _sel_left(mcum, lf)
    big_l = _sel_left(msum, lf)
    k = 1.0 - f
    qd = hq * jnp.exp(b)
    ki = k * jnp.exp(-b)
    ke = k * jnp.exp(big_l - b)
    return lb, sig, f, b, big_l, qd, ki, ke


def _hgrn_fwd(proj, lbl):
    s = proj.shape[0]
    t = HG_T
    mcum, _, msum, bd = _hg_consts()

    def body(hq_ref, hf_ref, hi_ref, lbl_ref, mcum_ref, msum_ref, bd_ref, o_ref, sp_ref, st_ref):
        @pl.when(pl.program_id(0) == 0)
        def _():
            st_ref[...] = jnp.zeros_like(st_ref)

        mc = mcum_ref[...]
        _, _, _, _, big_l, qd, ki, ke = _hg_pre(hq_ref[...], hf_ref[...], lbl_ref[...], mc, msum_ref[...])
        el = jnp.exp(big_l)
        hi = hi_ref[...]
        head = lax.broadcasted_iota(jnp.int32, (t, GW), 1) >> 6
        mask = jnp.concatenate([mc] * HG_G, axis=0) > 0.5
        for p in range(HEADS // HG_G):
            sl = slice(GW * p, GW * (p + 1))
            vp = hi[:, sl].astype(BF16)
            qs = _stack_heads(qd[:, sl], head).astype(BF16)
            a = jnp.where(mask, _dotg(qs, ki[:, sl].astype(BF16), NT), 0.0)
            o_intra = _unstack_heads(_dot(a.astype(BF16), vp), head, t)
            qb = qd[:, sl].astype(BF16)
            kb = ke[:, sl].astype(BF16)
            st = st_ref[p]
            for c in range(HG_NC):
                rows = slice(HG_BLOCK * c, HG_BLOCK * (c + 1))
                sp_ref[c, :, sl] = _compact_state(st)
                o_ref[rows, sl] = o_intra[rows] + _dotg(qb[rows], st.astype(BF16), NT)
                u = _dotg(vp[rows], kb[rows], TN) * bd_ref[...]
                st = st * el[HG_BLOCK * c:HG_BLOCK * c + 1, sl] + u
            st_ref[p] = st

    row = lambda j: pl.BlockSpec((t, HG_WIDTH), lambda i: (i, j))
    full = lambda a: pl.BlockSpec(a.shape, lambda i: (0, 0))
    return pl.pallas_call(
        body,
        grid=(s // t,),
        in_specs=[row(6), row(7), row(8), full(lbl), full(mcum), full(msum), full(bd)],
        out_specs=[row(0), pl.BlockSpec((HG_NC, 64, HG_WIDTH), lambda i: (i, 0, 0))],
        out_shape=[jax.ShapeDtypeStruct((s, HG_WIDTH), F32),
                   jax.ShapeDtypeStruct((s // HG_BLOCK, 64, HG_WIDTH), F32)],
        scratch_shapes=[pltpu.VMEM((HEADS // HG_G, GW, GW), F32)],
        compiler_params=_params(("arbitrary",)),
        name="hgrn_fwd",
    )(proj, proj, proj, lbl, mcum, msum, bd)


def _slot_shape(name, r, c):
    return (N_DEV, r, c // N_DEV) if COL_SHARDED[name] else (N_DEV, r // N_DEV, c)


def _emit_slots(name, acc_ref, out_ref):
    r, c = acc_ref.shape
    for p in range(N_DEV):
        if COL_SHARDED[name]:
            out_ref[p] = acc_ref[:, c // N_DEV * p:c // N_DEV * (p + 1)].astype(BF16)
        else:
            out_ref[p] = acc_ref[r // N_DEV * p:r // N_DEV * (p + 1), :].astype(BF16)


def _hgrn_bwd(proj, lbl, do, sprev, dproj, pairs):
    s = proj.shape[0]
    t = HG_T
    nt = s // t
    npair = len(pairs)
    mcum, mrev, msum, bd = _hg_consts()

    def body(hq_ref, hf_ref, hi_ref, lbl_ref, do_ref, sp_ref, mcum_ref, mrev_ref, msum_ref, bd_ref,
             dproj_in, *rest):
        del dproj_in
        pair_refs, (dh_ref, dlbl_ref) = rest[:2 * npair], rest[2 * npair:2 * npair + 2]
        dw_refs, g_ref, acc_refs = rest[2 * npair + 2:3 * npair + 2], rest[3 * npair + 2], rest[3 * npair + 3:]

        @pl.when(pl.program_id(0) == 0)
        def _():
            g_ref[...] = jnp.zeros_like(g_ref)
            dlbl_ref[...] = jnp.zeros_like(dlbl_ref)
            for acc_ref in acc_refs:
                acc_ref[...] = jnp.zeros_like(acc_ref)

        for n, acc_ref in enumerate(acc_refs):
            acc_ref[...] += _dot(pair_refs[2 * n][...], pair_refs[2 * n + 1][...])

        @pl.when(pl.program_id(0) == nt - 1)
        def _():
            for (name, _, _), acc_ref, dw_ref in zip(pairs, acc_refs, dw_refs):
                _emit_slots(name, acc_ref, dw_ref)

        mc = mcum_ref[...]
        lb, sig, f, b, big_l, qd, ki, ke = _hg_pre(hq_ref[...], hf_ref[...], lbl_ref[...], mc, msum_ref[...])
        el = jnp.exp(big_l)
        hi = hi_ref[...]
        dov = do_ref[...]
        head = lax.broadcasted_iota(jnp.int32, (t, GW), 1) >> 6
        head64 = lax.broadcasted_iota(jnp.int32, (64, GW), 1) >> 6
        mask = jnp.concatenate([mc] * HG_G, axis=0) > 0.5
        dqd_parts, dke_parts, dv_parts, del_parts, dki_parts = [], [], [], [], []
        for p in range(HEADS // HG_G):
            sl = slice(GW * p, GW * (p + 1))
            vp = hi[:, sl].astype(BF16)
            qs = _stack_heads(qd[:, sl], head).astype(BF16)
            kip = ki[:, sl].astype(BF16)
            dos = _stack_heads(dov[:, sl], head).astype(BF16)
            a = jnp.where(mask, _dotg(qs, kip, NT), 0.0).astype(BF16)
            da = jnp.where(mask, _dotg(dos, vp, NT), 0.0).astype(BF16)
            r = _dot(da, kip)
            dki_parts.append(_dotg(da, qs, TN))
            qb = qd[:, sl].astype(BF16)
            kb = ke[:, sl].astype(BF16)
            dob = dov[:, sl].astype(BF16)
            g = g_ref[p]
            dqd_c, dv_c, dke_c, del_c = [], [], [], []
            for c in range(HG_NC - 1, -1, -1):
                rows = slice(HG_BLOCK * c, HG_BLOCK * (c + 1))
                gb = g.astype(BF16)
                st = _expand_state(sp_ref[c, :, sl], head64)
                dqd_c.append(_dot(dob[rows], st.astype(BF16)))
                dv_c.append(_dotg(kb[rows], gb, NT))
                dke_c.append(_dot(vp[rows], gb))
                del_c.append(jnp.broadcast_to(jnp.sum(g * st, axis=0, keepdims=True), (HG_BLOCK, GW)))
                g = g * el[HG_BLOCK * c:HG_BLOCK * c + 1, sl] + _dotg(dob[rows], qb[rows], TN) * bd_ref[...]
            g_ref[p] = g
            up = lambda parts: jnp.concatenate(parts[::-1], axis=0)
            dqd_parts.append(_unstack_heads(r, head, t) + up(dqd_c))
            dv_parts.append(_dotg(a, dos, TN) + up(dv_c))
            dke_parts.append(up(dke_c))
            del_parts.append(up(del_c))
        wide = lambda parts: jnp.concatenate(parts, axis=1)
        dqd, dke, dki, dvv, del_rows = wide(dqd_parts), wide(dke_parts), wide(dki_parts), wide(dv_parts), wide(del_parts)
        dh_ref[:, :HG_WIDTH] = (dqd * jnp.exp(b)).astype(BF16)
        dh_ref[:, 2 * HG_WIDTH:] = dvv.astype(BF16)
        dke_ke = dke * ke
        db = dqd * qd - dki * ki - dke_ke
        dl_rows = _sel_left(msum_ref[...], dke_ke) + del_rows * el
        is_last = (lax.broadcasted_iota(jnp.int32, (t, HG_WIDTH), 0) & (HG_BLOCK - 1)) == HG_BLOCK - 1
        db = db + jnp.where(is_last, dl_rows, 0.0)
        dlf = _sel_left(mrev_ref[...], db)
        dk = dki * jnp.exp(-b) + dke * jnp.exp(big_l - b)
        df = dlf / f - dk
        dh_ref[:, HG_WIDTH:2 * HG_WIDTH] = (df * (1.0 - lb) * sig * (1.0 - sig)).astype(BF16)
        dlb = jnp.sum(df * (1.0 - sig), axis=0, keepdims=True) * lb * (1.0 - lb)
        dlbl_ref[0:1, :] += dlb
        dlbl_ref[1:2, :] -= dlb

    rrow = lambda j: pl.BlockSpec((t, HG_WIDTH), lambda i: (nt - 1 - i, j))
    full = lambda a: pl.BlockSpec(a.shape, lambda i: (0, 0))
    pair_specs, dw_specs, dw_shapes, accs = [], [], [], []
    for name, at, b in pairs:
        pair_specs += [pl.BlockSpec((at.shape[0], t), lambda i: (0, i)), pl.BlockSpec((t, b.shape[1]), lambda i: (i, 0))]
        shape = _slot_shape(name, at.shape[0], b.shape[1])
        dw_specs.append(pl.BlockSpec(shape, lambda i: (0, 0, 0)))
        dw_shapes.append(jax.ShapeDtypeStruct(shape, BF16))
        accs.append(pltpu.VMEM((at.shape[0], b.shape[1]), F32))
    return pl.pallas_call(
        body,
        grid=(nt,),
        in_specs=[rrow(6), rrow(7), rrow(8), full(lbl), rrow(0),
                  pl.BlockSpec((HG_NC, 64, HG_WIDTH), lambda i: (nt - 1 - i, 0, 0)),
                  full(mcum), full(mrev), full(msum), full(bd), pl.BlockSpec(memory_space=pl.ANY)] + pair_specs,
        out_specs=[pl.BlockSpec((t, 3 * HG_WIDTH), lambda i: (nt - 1 - i, 2)),
                   pl.BlockSpec((2, HG_WIDTH), lambda i: (0, 0))] + dw_specs,
        out_shape=[jax.ShapeDtypeStruct(dproj.shape, BF16), jax.ShapeDtypeStruct((2, HG_WIDTH), F32)] + dw_shapes,
        input_output_aliases={10: 0},
        scratch_shapes=[pltpu.VMEM((HEADS // HG_G, GW, GW), F32)] + accs,
        compiler_params=_params(("arbitrary",)),
        name="hgrn_bwd",
    )(proj, proj, proj, lbl, do, sprev, mcum, mrev, msum, bd, dproj, *[a for pair in pairs for a in pair[1:]])


def _tail(x, tgt, proj, attn, o, w_a, w_b, w_out, w_at, w_bt, w_outt, b_gate, g_post, gh):
    s = x.shape[0]
    tm = 256
    ones64 = (jnp.arange(HG_WIDTH)[:, None] // 64 == jnp.arange(HG_WIDTH)[None, :] // 64).astype(BF16)
    weights = (w_a, w_b, w_out, w_at, w_bt, w_outt)

    def body(x_ref, t_ref, ml_ref, ga_ref, gb_ref, at_ref, o_ref, *rest):
        w_hbm, (bg_ref, gp_ref, gh_ref, ones_ref) = rest[:6], rest[6:10]
        (dout_ref, dpj_ref, dop_ref, do_ref, mt_ref, dy_ref, yat_ref, dya_ref, ybt_ref, dyb_ref,
         loss_ref, dgp_ref, dbg_ref, dgh_ref) = rest[10:24]
        (wa_ref, wb_ref, wo_ref, wat_ref, wbt_ref, wot_ref), w_sem = rest[24:30], rest[30]

        @pl.when(pl.program_id(0) == 0)
        def _():
            loads = [pltpu.make_async_copy(src, dst, w_sem.at[k])
                     for k, (src, dst) in enumerate(zip(w_hbm, rest[24:30]))]
            _start_all(loads, [])
            loss_ref[...] = jnp.zeros_like(loss_ref)
            dgp_ref[...] = jnp.zeros_like(dgp_ref)
            dbg_ref[...] = jnp.zeros_like(dbg_ref)
            dgh_ref[...] = jnp.zeros_like(dgh_ref)
            _wait_all(loads, [])

        ones = ones_ref[...]
        gate_a = ga_ref[...]
        sa = _sigmoid(gate_a)
        silu_a = gate_a * sa
        attn_v = at_ref[...]
        ya_in = attn_v * silu_a
        ov = o_ref[...]
        ro = lax.rsqrt(_sel_right(ov * ov, ones) * (1.0 / 64.0) + EPS)
        ohat = ov * ro
        ghv = gh_ref[...]
        on = ohat * ghv
        gate_b = gb_ref[...]
        sb = _sigmoid(gate_b)
        silu_b = gate_b * sb
        yb_in = on * silu_b
        ya_bf = ya_in.astype(BF16)
        yb_bf = yb_in.astype(BF16)
        yat_ref[...] = ya_bf.T
        ybt_ref[...] = yb_bf.T
        y_a = _dot(ya_bf, wa_ref[...])
        y_b = _dot(yb_bf, wb_ref[...])
        gts = _sigmoid(ml_ref[...] + bg_ref[...])
        g_a = gts[:, :D_MODEL]
        g_b = gts[:, D_MODEL:]
        m_bf = (g_a * y_a + g_b * y_b).astype(BF16)
        mt_ref[...] = m_bf.T
        y = _dot(m_bf, wo_ref[...])
        r1 = lax.rsqrt(jnp.mean(y * y, axis=-1, keepdims=True) + EPS)
        yn = y * r1
        gp = gp_ref[...]
        e = x_ref[...] + yn * gp - t_ref[...]
        loss_ref[...] += jnp.sum(e * e, axis=0, keepdims=True)
        dout = e * (1.0 / D_MODEL)
        dout_ref[...] = dout
        dgp_ref[...] += jnp.sum(dout * yn, axis=0, keepdims=True)
        dyn = dout * gp
        dy = r1 * (dyn - yn * jnp.mean(dyn * yn, axis=-1, keepdims=True))
        dy_bf = dy.astype(BF16)
        dy_ref[...] = dy_bf
        dm = _dot(dy_bf, wot_ref[...])
        dml_a = dm * y_a * g_a * (1.0 - g_a)
        dml_b = dm * y_b * g_b * (1.0 - g_b)
        dpj_ref[:, :D_MODEL] = dml_a.astype(BF16)
        dpj_ref[:, D_MODEL:2 * D_MODEL] = dml_b.astype(BF16)
        dbg_ref[:, :D_MODEL] += jnp.sum(dml_a, axis=0, keepdims=True)
        dbg_ref[:, D_MODEL:] += jnp.sum(dml_b, axis=0, keepdims=True)
        dya_bf = (dm * g_a).astype(BF16)
        dyb_bf = (dm * g_b).astype(BF16)
        dya_ref[...] = dya_bf
        dyb_ref[...] = dyb_bf
        dya_in = _dot(dya_bf, wat_ref[...])
        dyb_in = _dot(dyb_bf, wbt_ref[...])
        dattn = dya_in * silu_a
        delta = _sel_right(dattn * attn_v, ones)
        lane = lax.broadcasted_iota(jnp.int32, (tm, LANE), 1)
        for p in range(HEADS // 2):
            sl = slice(LANE * p, LANE * (p + 1))
            xs = (dattn[:, sl], pltpu.roll(dattn[:, sl], VDIM, 1))
            nds = (-pltpu.roll(delta[:, sl], VDIM, 1), -delta[:, sl])
            for a in range(2):
                hi, lo_part = _hi_lo(nds[a])
                blk = jnp.where(lane < VDIM, xs[a], jnp.where(lane == VDIM, hi, jnp.where(lane == VDIM + 1, lo_part, 0.0)))
                dop_ref[:, LANE * (2 * p + a):LANE * (2 * p + a + 1)] = blk.astype(BF16)
        dpj_ref[:, 2 * D_MODEL:2 * D_MODEL + HG_WIDTH] = (
            dya_in * attn_v * (sa * (1.0 + gate_a * (1.0 - sa)))).astype(BF16)
        don = dyb_in * silu_b
        dpj_ref[:, 2 * D_MODEL + HG_WIDTH:] = (dyb_in * on * (sb * (1.0 + gate_b * (1.0 - sb)))).astype(BF16)
        dgh_ref[...] += jnp.sum(don * ohat, axis=0, keepdims=True)
        dohat = don * ghv
        do_ref[...] = (ro * (dohat - ohat * (_sel_right(dohat * ohat, ones) * (1.0 / 64.0)))).astype(BF16)

    row = lambda w, j: pl.BlockSpec((tm, w), lambda i: (i, j))
    col = lambda w: pl.BlockSpec((w, tm), lambda i: (0, i))
    full = lambda a: pl.BlockSpec(a.shape, lambda i: (0, 0))
    acc = lambda w: pl.BlockSpec((1, w), lambda i: (0, 0))
    sds = lambda w, dt: jax.ShapeDtypeStruct((s, w), dt)
    sdt = lambda w: jax.ShapeDtypeStruct((w, s), BF16)
    return pl.pallas_call(
        body,
        grid=(s // tm,),
        in_specs=[row(1024, 0), row(1024, 0), row(2048, 0), row(512, 4), row(512, 5), row(512, 0), row(512, 0)]
        + [ANY] * 6 + [full(b_gate), full(g_post), full(gh), full(ones64)],
        out_specs=[row(1024, 0), row(3072, 0), row(1024, 0), row(512, 0),
                   col(1024), row(1024, 0), col(512), row(1024, 0), col(512), row(1024, 0),
                   acc(1024), acc(1024), acc(2048), acc(512)],
        out_shape=[sds(1024, F32), sds(D_IN_PAD, BF16), sds(1024, BF16), sds(512, BF16),
                   sdt(1024), sds(1024, BF16), sdt(512), sds(1024, BF16), sdt(512), sds(1024, BF16),
                   jax.ShapeDtypeStruct((1, 1024), F32), jax.ShapeDtypeStruct((1, 1024), F32),
                   jax.ShapeDtypeStruct((1, 2048), F32), jax.ShapeDtypeStruct((1, 512), F32)],
        scratch_shapes=[pltpu.VMEM(a.shape, BF16) for a in weights] + [pltpu.SemaphoreType.DMA((6,))],
        compiler_params=_params(("arbitrary",), 56),
        name="tail",
    )(x, tgt, proj, proj, proj, attn, o, *weights, b_gate, g_post, gh, ones64)


def _mla_bwd(proj, dqr, dkr, dv, g_q, g_kv, w_uq_pt, w_kv_pt, rc, rs1, rs2, cqt, ckvt, dproj):
    assert HEADS == N_DEV
    s = proj.shape[0]
    tm = 512
    scale = 1.0 / math.sqrt(QK)

    def body(cq_ref, ckv_ref, dqr_ref, dkr_ref, dv_ref, gq_ref, gkv_ref, wuqt_ref, wkvt_ref, c_ref, s1_ref, s2_ref,
             cqt_ref, ckvt_ref, dproj_in, dc_ref, dgq_ref, dgkv_ref, uq_slots, ukv_slots,
             dqf_ref, dkvf_ref, dwuq_ref, dwkv_ref):
        del dproj_in

        @pl.when(pl.program_id(0) == 0)
        def _():
            dgq_ref[...] = jnp.zeros_like(dgq_ref)
            dgkv_ref[...] = jnp.zeros_like(dgkv_ref)
            dwuq_ref[...] = jnp.zeros_like(dwuq_ref)
            dwkv_ref[...] = jnp.zeros_like(dwkv_ref)

        c, s1, s2 = c_ref[...], s1_ref[...], s2_ref[...]
        lane = lax.broadcasted_iota(jnp.int32, (tm, LANE), 1)
        ksum = jnp.zeros((tm, LANE), F32)
        for h in range(HEADS):
            sl = slice(LANE * h, LANE * (h + 1))
            dqf_ref[:, sl] = (_unrope(dqr_ref[:, sl], c, s1, s2) * scale).astype(BF16)
            dkh = dkr_ref[:, sl]
            ksum = ksum + dkh
            dkvf_ref[:, sl] = jnp.where(lane < NOPE, dkh, 0.0).astype(BF16)
            dkvf_ref[:, HEADS * LANE + LANE * h:HEADS * LANE + LANE * (h + 1)] = jnp.where(
                lane < VDIM, dv_ref[:, sl], 0.0).astype(BF16)
        dkpe = _unrope(ksum, c, s1, s2)
        dc_ref[:, Q_LORA + KV_LORA:] = jnp.where((lane >= NOPE) & (lane < QK), dkpe, 0.0).astype(BF16)
        dqf, dkvf = dqf_ref[...], dkvf_ref[...]
        dwuq_ref[...] += _dot(cqt_ref[...], dqf)
        dwkv_ref[...] += _dot(ckvt_ref[...], dkvf)
        dcqn = _dot(dqf, wuqt_ref[...])
        dckvn = _dot(dkvf, wkvt_ref[...])
        for x_ref, g_ref, dn, cols, dg_ref in ((cq_ref, gq_ref, dcqn, slice(0, Q_LORA), dgq_ref),
                                               (ckv_ref, gkv_ref, dckvn, slice(Q_LORA, Q_LORA + KV_LORA), dgkv_ref)):
            xv = x_ref[...]
            r = lax.rsqrt(jnp.mean(xv * xv, axis=-1, keepdims=True) + EPS)
            xh = xv * r
            dg_ref[...] += jnp.sum(dn * xh, axis=0, keepdims=True)
            dh = dn * g_ref[...]
            dc_ref[:, cols] = (r * (dh - xh * jnp.mean(dh * xh, axis=-1, keepdims=True))).astype(BF16)

        @pl.when(pl.program_id(0) == s // tm - 1)
        def _():
            ur = Q_LORA // N_DEV
            for p in range(N_DEV):
                uq_slots[p] = jnp.concatenate(
                    [dwuq_ref[ur * p:ur * (p + 1), LANE * h:LANE * h + QK] for h in range(HEADS)], axis=1).astype(BF16)
                ukv_slots[p] = jnp.concatenate(
                    [dwkv_ref[:, LANE * p:LANE * p + NOPE],
                     dwkv_ref[:, LANE * (HEADS + p):LANE * (HEADS + p) + VDIM]], axis=1).astype(BF16)

    row = lambda w, j: pl.BlockSpec((tm, w), lambda i: (i, j))
    full = lambda a: pl.BlockSpec(a.shape, lambda i: (0, 0))
    acc = lambda w: pl.BlockSpec((1, w), lambda i: (0, 0))
    col = lambda w: pl.BlockSpec((w, tm), lambda i: (0, i))
    whole = lambda shape: pl.BlockSpec(shape, lambda i: (0, 0, 0))
    uq_shape = (N_DEV, Q_LORA // N_DEV, HEADS * QK)
    ukv_shape = (N_DEV, KV_LORA, NOPE + VDIM)
    return pl.pallas_call(
        body,
        grid=(s // tm,),
        in_specs=[row(768, 6), row(256, 21), row(1024, 0), row(1024, 0), row(1024, 0), full(g_q), full(g_kv),
                  full(w_uq_pt), full(w_kv_pt), row(128, 0), row(128, 0), row(128, 0), col(Q_LORA), col(KV_LORA),
                  pl.BlockSpec(memory_space=pl.ANY)],
        out_specs=[row(1152, 4), acc(768), acc(256), whole(uq_shape), whole(ukv_shape)],
        out_shape=[jax.ShapeDtypeStruct(dproj.shape, BF16),
                   jax.ShapeDtypeStruct((1, 768), F32), jax.ShapeDtypeStruct((1, 256), F32),
                   jax.ShapeDtypeStruct(uq_shape, BF16), jax.ShapeDtypeStruct(ukv_shape, BF16)],
        input_output_aliases={14: 0},
        scratch_shapes=[pltpu.VMEM((tm, HEADS * LANE), BF16), pltpu.VMEM((tm, 2 * HEADS * LANE), BF16),
                        pltpu.VMEM((Q_LORA, HEADS * LANE), F32), pltpu.VMEM((KV_LORA, 2 * HEADS * LANE), F32)],
        compiler_params=_params(("arbitrary",)),
        name="mla_bwd",
    )(proj, proj, dqr, dkr, dv, g_q, g_kv, w_uq_pt, w_kv_pt, rc, rs1, rs2, cqt, ckvt, dproj)


def _dh_dx(dproj, w_in_pt, x, dout, g_pre, sends):
    s, k = dproj.shape
    tm = 256
    ns, ni = len(sends), s // tm

    def body(dp_ref, w_ref, x_ref, dout_ref, g_ref, *rest):
        send_refs, (dx_ref, dg_ref) = rest[:ns], rest[ns:ns + 2]
        recv_refs, sems = rest[ns + 2:2 * ns + 2], rest[2 * ns + 2:]

        @pl.when(pl.program_id(0) == 0)
        def _():
            _start_all(*_to_chips_copies(send_refs, recv_refs, sems))
            dg_ref[...] = jnp.zeros_like(dg_ref)

        dh = _dot(dp_ref[...], w_ref[...])
        xv = x_ref[...]
        r = lax.rsqrt(jnp.mean(xv * xv, axis=-1, keepdims=True) + EPS)
        xh = xv * r
        dg_ref[...] += jnp.sum(dh * xh, axis=0, keepdims=True)
        dxh = dh * g_ref[...]
        dx_ref[...] = dout_ref[...] + r * (dxh - xh * jnp.mean(dxh * xh, axis=-1, keepdims=True))

        @pl.when(pl.program_id(0) == ni - 1)
        def _():
            _wait_all(*_to_chips_copies(send_refs, recv_refs, sems))

    row = lambda w: pl.BlockSpec((tm, w), lambda i: (i, 0))
    return pl.pallas_call(
        body,
        grid=(ni,),
        in_specs=[row(k), pl.BlockSpec((k, D_MODEL), lambda i: (0, 0)), row(D_MODEL), row(D_MODEL),
                  pl.BlockSpec((1, D_MODEL), lambda i: (0, 0))] + [ANY] * ns,
        out_specs=[row(D_MODEL), pl.BlockSpec((1, D_MODEL), lambda i: (0, 0))] + [ANY] * ns,
        out_shape=[jax.ShapeDtypeStruct((s, D_MODEL), F32), jax.ShapeDtypeStruct((1, D_MODEL), F32)]
        + [jax.ShapeDtypeStruct(a.shape, a.dtype) for a in sends],
        scratch_shapes=_copy_sems(ns, 3),
        compiler_params=_params(("arbitrary",)),
        name="dh_dx",
    )(dproj, w_in_pt, x, dout, g_pre, *sends)


def _pair_reduce(slots):
    n = len(slots)
    half = [(N_DEV // 2,) + a.shape[1:] for a in slots]

    def body(*refs):
        s_refs, o_refs = refs[:n], refs[n:2 * n]
        mine, got = refs[2 * n:3 * n], refs[3 * n:4 * n]
        send_sems, recv_sems, local_sems, out_sems = refs[4 * n:]
        x, y, c = _my_place()
        copies, loads, stores = [], [], []
        for q in range(N_DEV // 2):
            for a in range(n):
                copies.append(pltpu.make_async_remote_copy(
                    src_ref=s_refs[a].at[2 * q + 1 - c], dst_ref=got[a].at[q],
                    send_sem=send_sems.at[4 * a + q], recv_sem=recv_sems.at[4 * a + q],
                    device_id=(x, y, 1 - c), device_id_type=MESH_ID))
                loads.append(pltpu.make_async_copy(s_refs[a].at[2 * q + c], mine[a].at[q], local_sems.at[4 * a + q]))
                stores.append(pltpu.make_async_copy(mine[a].at[q], o_refs[a].at[q], out_sems.at[4 * a + q]))
        _start_all(loads, copies)
        k = 0
        for q in range(N_DEV // 2):
            for a in range(n):
                loads[k].wait()
                copies[k].wait_recv()
                mine[a][q] = (mine[a][q].astype(F32) + got[a][q].astype(F32)).astype(mine[a].dtype)
                stores[k].start()
                k += 1
        for cp in copies:
            cp.wait_send()
        for cp in stores:
            cp.wait()

    vm = lambda: [pltpu.VMEM(h, a.dtype) for h, a in zip(half, slots)]
    return pl.pallas_call(
        body,
        in_specs=[ANY] * n,
        out_specs=[ANY] * n,
        out_shape=[jax.ShapeDtypeStruct(h, a.dtype) for h, a in zip(half, slots)],
        scratch_shapes=vm() + vm() + [pltpu.SemaphoreType.DMA((4 * n,)), pltpu.SemaphoreType.DMA((4 * n,)),
                                      pltpu.SemaphoreType.DMA((4 * n,)), pltpu.SemaphoreType.DMA((4 * n,))],
        compiler_params=pltpu.CompilerParams(vmem_limit_bytes=48 * 2**20),
        name="pair_reduce",
    )(*slots)


def _rope_tables(s):
    inv = (np.float32(ROPE_THETA) ** (-np.arange(0, ROPE, 2, dtype=np.float32) / np.float32(ROPE))).astype(np.float32)
    ang = (np.arange(s, dtype=np.float32)[:, None] * inv[None, :]).astype(np.float32)
    cos, sin = jnp.asarray(np.cos(ang.astype(np.float64)), F32), jnp.asarray(np.sin(ang.astype(np.float64)), F32)
    z = lambda w: jnp.zeros((s, w), F32)
    rc = jnp.concatenate([jnp.ones((s, NOPE), F32), cos, cos, z(32)], axis=1)
    rs1 = jnp.concatenate([z(NOPE), -sin, z(16), z(32)], axis=1)
    rs2 = jnp.concatenate([z(NOPE), z(16), sin, z(32)], axis=1)
    return rc, rs1, rs2


def _step(x, tgt, w_blk, shards, g_pre, b_gate, g_q, g_kv, lbl, g_hgrn, g_post):
    s = x.shape[0]
    rc, rs1, rs2 = _rope_tables(s)
    gh = jnp.tile(g_hgrn, (1, HEADS))

    proj, w_in_pt, ht, *got = _gather_proj(x, g_pre, w_blk, shards[:2])
    w_uq, w_ukv = (_from_slots(n, g) for n, g in zip(MATS[:2], got))
    w_uq_p = jnp.pad(w_uq.reshape(Q_LORA, HEADS, QK), ((0, 0), (0, 0), (0, LANE - QK))).reshape(Q_LORA, HEADS * LANE)
    kv3 = w_ukv.reshape(KV_LORA, HEADS, NOPE + VDIM)
    pad64 = lambda t: jnp.pad(t, ((0, 0), (0, 0), (0, LANE - 64))).reshape(KV_LORA, HEADS * LANE)
    w_kv_p = jnp.concatenate([pad64(kv3[:, :, :NOPE]), pad64(kv3[:, :, NOPE:])], axis=1)

    qr, kr, v, cqt, ckvt = _mla_prep(proj, g_q, g_kv, w_uq_p, w_kv_p, rc, rs1, rs2)
    attn, qa, *got = _attn_fwd(qr, kr, v, shards[2:])
    w_a, w_b, w_out = (_from_slots(n, g) for n, g in zip(MATS[2:], got))
    o, sprev = _hgrn_fwd(proj, lbl)
    (dout, dproj, dop, do, mt, dy_bf, yat, dya_bf, ybt, dyb_bf,
     loss_vec, dg_post, db_gate, dgh) = _tail(x, tgt, proj, attn, o, w_a, w_b, w_out, w_a.T, w_b.T, w_out.T,
                                               b_gate, g_post, gh)
    dproj, dlbl, *early = _hgrn_bwd(proj, lbl, do, sprev, dproj,
                                    [("w_branch_a", yat, dya_bf), ("w_branch_b", ybt, dyb_bf), ("w_out", mt, dy_bf)])
    dqr, dkr, dv, *early_recv = _attn_bwd(qa, kr, v, dop, early)
    dproj, dg_q, dg_kv, dw_uq_slots, dw_ukv_slots = _mla_bwd(proj, dqr, dkr, dv, g_q, g_kv, w_uq_p.T, w_kv_p.T,
                                                             rc, rs1, rs2, cqt, ckvt, dproj)

    dw_in_slots = _dw_in_slots(ht, dproj)
    late = _pair_reduce([dw_in_slots, dw_uq_slots, dw_ukv_slots])
    dx, dg_pre, *late_recv = _dh_dx(dproj, w_in_pt, x, dout, g_pre, late)

    g_sum = _vectors_sum(dg_pre, db_gate, dg_q, dg_kv, dlbl, dgh, dg_post, loss_vec)
    return dx, late_recv[0], dict(zip(MATS, late_recv[1:] + early_recv)), g_sum


def _adamw(g, w, m, v):
    c1 = 1.0 / (1.0 - ADAM_B1 ** ADAM_STEP)
    c2 = 1.0 / (1.0 - ADAM_B2 ** ADAM_STEP)
    nm = ADAM_B1 * m + (1.0 - ADAM_B1) * g
    nv = ADAM_B2 * v + (1.0 - ADAM_B2) * (g * g)
    d = -ADAM_LR * ((nm * c1) / (jnp.sqrt(nv * c2) + ADAM_EPS) + ADAM_WD * w)
    return d, nm, nv


def _sum8(r_ref):
    g = r_ref[0].astype(F32)
    for k in range(1, r_ref.shape[0]):
        g = g + r_ref[k].astype(F32)
    return g


def _sum_adamw_w_in(recv, w, m, v):
    rows, _, cols = w.shape
    tc = 256
    nc = cols // tc

    def body(r_ref, w_hbm, m_hbm, v_hbm, g_hbm, d_hbm, nm_hbm, nv_hbm, ins, outs, in_sems, out_sems):
        i = pl.program_id(0)
        slot = i & 1
        cols_of = lambda step: pl.ds(pl.multiple_of(step * tc, tc), tc)

        def load(k, step, sl):
            return pltpu.make_async_copy((w_hbm, m_hbm, v_hbm)[k].at[:, 0, cols_of(step)], ins.at[sl, k],
                                         in_sems.at[sl, k])

        def store(k, step, sl):
            return pltpu.make_async_copy(outs.at[sl, k], (g_hbm, d_hbm, nm_hbm, nv_hbm)[k].at[:, 0, cols_of(step)],
                                         out_sems.at[sl, k])

        @pl.when(i == 0)
        def _():
            for k in range(3):
                load(k, 0, 0).start()

        @pl.when(i + 1 < nc)
        def _():
            for k in range(3):
                load(k, i + 1, 1 - slot).start()

        @pl.when(i >= 2)
        def _():
            for k in range(4):
                store(k, i - 2, slot).wait()

        for k in range(3):
            load(k, i, slot).wait()
        g = _sum8(r_ref)
        d, nm, nv = _adamw(g, ins[slot, 0], ins[slot, 1], ins[slot, 2])
        for k, val in enumerate((g, d, nm, nv)):
            outs[slot, k] = val
        for k in range(4):
            store(k, i, slot).start()

        @pl.when(i == nc - 1)
        def _():
            for k in range(4):
                store(k, i, slot).wait()
            if nc >= 2:
                for k in range(4):
                    store(k, i - 1, 1 - slot).wait()

    out = jax.ShapeDtypeStruct((rows, 1, cols), F32)
    return pl.pallas_call(
        body,
        grid=(nc,),
        in_specs=[pl.BlockSpec((recv.shape[0], rows, tc), lambda i: (0, 0, i)), ANY, ANY, ANY],
        out_specs=[ANY, ANY, ANY, ANY],
        out_shape=[out, out, out, out],
        scratch_shapes=[pltpu.VMEM((2, 3, rows, tc), F32), pltpu.VMEM((2, 4, rows, tc), F32),
                        pltpu.SemaphoreType.DMA((2, 3)), pltpu.SemaphoreType.DMA((2, 4))],
        compiler_params=_params(("arbitrary",)),
        name="sum_adamw_w_in",
    )(recv, w, m, v)


def _sum_adamw_whole(recvs, ws, ms, vs):
    n = len(ws)

    def body(*refs):
        r_refs, w_refs, m_refs, v_refs = refs[:n], refs[n:2 * n], refs[2 * n:3 * n], refs[3 * n:4 * n]
        outs = refs[4 * n:]
        for a in range(n):
            g = _sum8(r_refs[a])
            d, nm, nv = _adamw(g, w_refs[a][...], m_refs[a][...], v_refs[a][...])
            outs[a][...] = g
            outs[n + a][...] = d
            outs[2 * n + a][...] = nm
            outs[3 * n + a][...] = nv

    shapes = [jax.ShapeDtypeStruct(w.shape, F32) for w in ws]
    res = pl.pallas_call(
        body,
        out_shape=shapes * 4,
        compiler_params=pltpu.CompilerParams(vmem_limit_bytes=48 * 2**20),
        name="sum_adamw_mats",
    )(*recvs, *ws, *ms, *vs)
    return res[:n], res[n:2 * n], res[2 * n:3 * n], res[3 * n:]


SMALL = ("g_pre", "b_gate", "g_q", "g_kv", "lb_logits", "g_hgrn", "g_post")
SMALL_SHAPE = dict(g_pre=(1, 1024), b_gate=(1, 2048), g_q=(1, 768), g_kv=(1, 256), lb_logits=(2, 512),
                   g_hgrn=(1, 64), g_post=(1, 1024))


def _vectors_sum(dg_pre, db_gate, dg_q, dg_kv, dlbl, dgh, dg_post, loss_vec):
    def body(gpre_ref, bg_ref, gq_ref, gkv_ref, lbl_ref, gh_ref, gpost_ref, loss_ref, out_ref, mine, got,
             send_sems, recv_sems):
        mine[...] = jnp.zeros_like(mine)
        mine[0:1, :] = gpre_ref[...]
        mine[1:2, :] = bg_ref[:, :1024]
        mine[2:3, :] = bg_ref[:, 1024:]
        mine[3:4, :Q_LORA] = gq_ref[...]
        mine[4:5, :KV_LORA] = gkv_ref[...]
        loss = (0.5 / D_MODEL) * jnp.sum(loss_ref[...], axis=-1, keepdims=True)
        mine[4:5, KV_LORA:] = jnp.broadcast_to(loss, (1, 1024 - KV_LORA))
        mine[5:6, :HG_WIDTH] = lbl_ref[0:1, :]
        mine[5:6, HG_WIDTH:] = lbl_ref[1:2, :]
        gh = gh_ref[...]
        fold = gh[:, :VDIM]
        for h in range(1, HEADS):
            fold = fold + gh[:, VDIM * h:VDIM * (h + 1)]
        mine[6:7, :VDIM] = fold
        mine[7:8, :] = gpost_ref[...]
        x, y, c = _my_place()
        me = 4 * x + 2 * y + c
        got[me] = mine[...]
        copies = [pltpu.make_async_remote_copy(
            src_ref=mine, dst_ref=got.at[me], send_sem=send_sems.at[k], recv_sem=recv_sems.at[k],
            device_id=_flip(k, x, y, c), device_id_type=MESH_ID) for k in range(N_DEV - 1)]
        _start_all([], copies)
        _wait_all([], copies)
        out_ref[...] = _sum8(got)

    return pl.pallas_call(
        body,
        out_shape=jax.ShapeDtypeStruct((8, 1024), F32),
        scratch_shapes=[pltpu.VMEM((8, 1024), F32), pltpu.VMEM((N_DEV, 8, 1024), F32),
                        pltpu.SemaphoreType.DMA((7,)), pltpu.SemaphoreType.DMA((7,))],
        name="vectors_sum",
    )(dg_pre, db_gate, dg_q, dg_kv, dlbl, dgh, dg_post, loss_vec)


def _vectors_adamw(g_sum, ws, ms, vs):
    n = len(SMALL)

    def body(g_ref, *refs):
        w_refs, m_refs, v_refs = refs[:n], refs[n:2 * n], refs[2 * n:3 * n]
        loss_ref, outs = refs[3 * n], refs[3 * n + 1:]
        g = g_ref[...]
        loss_ref[...] = g[4:5, KV_LORA:KV_LORA + 1]
        grads = (g[0:1, :], jnp.concatenate([g[1:2, :], g[2:3, :]], axis=1), g[3:4, :Q_LORA], g[4:5, :KV_LORA],
                 jnp.concatenate([g[5:6, :HG_WIDTH], g[5:6, HG_WIDTH:]], axis=0), g[6:7, :VDIM], g[7:8, :])
        for a in range(n):
            d, nm, nv = _adamw(grads[a], w_refs[a][...], m_refs[a][...], v_refs[a][...])
            outs[a][...] = grads[a]
            outs[n + a][...] = d
            outs[2 * n + a][...] = nm
            outs[3 * n + a][...] = nv

    shapes = [jax.ShapeDtypeStruct(SMALL_SHAPE[k], F32) for k in SMALL]
    res = pl.pallas_call(
        body,
        out_shape=[jax.ShapeDtypeStruct((1, 1), F32)] + shapes * 4,
        name="vectors_adamw",
    )(g_sum, *ws, *ms, *vs)
    return res[0], res[1:n + 1], res[n + 1:2 * n + 1], res[2 * n + 1:3 * n + 1], res[3 * n + 1:]


MATS = ("w_uq", "w_ukv", "w_branch_a", "w_branch_b", "w_out")
COL_SHARDED = dict(w_uq=False, w_ukv=True, w_branch_a=True, w_branch_b=True, w_out=False)
ORDER = ("g_pre", "w_in", "b_gate", "g_q", "w_uq", "g_kv", "w_ukv", "lb_logits", "g_hgrn",
         "w_branch_a", "w_branch_b", "w_out", "g_post")


def _from_slots(name, slots):
    _, r, c = slots.shape
    if COL_SHARDED[name]:
        return slots.transpose(1, 0, 2).reshape(r, N_DEV * c)
    return slots.reshape(N_DEV * r, c)


def kernel(x, g_pre, w_in, b_gate, g_q, w_uq, g_kv, w_ukv, lb_logits, g_hgrn, w_branch_a, w_branch_b, w_out, g_post, loss_target, m_g_pre, m_w_in, m_b_gate, m_g_q, m_w_uq, m_g_kv, m_w_ukv, m_lb_logits, m_g_hgrn, m_w_branch_a, m_w_branch_b, m_w_out, m_g_post, v_g_pre, v_w_in, v_b_gate, v_g_q, v_w_uq, v_g_kv, v_w_ukv, v_lb_logits, v_g_hgrn, v_w_branch_a, v_w_branch_b, v_w_out, v_g_post):
    rows3 = lambda a: jnp.transpose(a, (2, 0, 1))
    w = dict(w_in=rows3(w_in), w_uq=w_uq[0], w_ukv=w_ukv[0], w_branch_a=w_branch_a[0], w_branch_b=w_branch_b[0],
             w_out=w_out[0], g_pre=g_pre, b_gate=b_gate, g_q=g_q, g_kv=g_kv, lb_logits=lb_logits, g_hgrn=g_hgrn,
             g_post=g_post)
    mom = dict(w_in=rows3(m_w_in), w_uq=m_w_uq[0], w_ukv=m_w_ukv[0], w_branch_a=m_w_branch_a[0],
               w_branch_b=m_w_branch_b[0], w_out=m_w_out[0], g_pre=m_g_pre, b_gate=m_b_gate, g_q=m_g_q, g_kv=m_g_kv,
               lb_logits=m_lb_logits, g_hgrn=m_g_hgrn, g_post=m_g_post)
    var = dict(w_in=rows3(v_w_in), w_uq=v_w_uq[0], w_ukv=v_w_ukv[0], w_branch_a=v_w_branch_a[0],
               w_branch_b=v_w_branch_b[0], w_out=v_w_out[0], g_pre=v_g_pre, b_gate=v_b_gate, g_q=v_g_q, g_kv=v_g_kv,
               lb_logits=v_lb_logits, g_hgrn=v_g_hgrn, g_post=v_g_post)

    w_blk = w["w_in"].reshape(W_IN_SHARD, D_MODEL).astype(BF16)
    dx, recv_in, recv, g_sum = _step(x[0], loss_target[0], w_blk, [w[n].astype(BF16) for n in MATS],
                                     g_pre, b_gate, g_q, g_kv, lb_logits, g_hgrn, g_post)

    g_in, d_in, m_in, v_in = _sum_adamw_w_in(recv_in, w["w_in"], mom["w_in"], var["w_in"])
    res = _sum_adamw_whole([recv[n] for n in MATS], *([t[n] for n in MATS] for t in (w, mom, var)))
    total, *vec = _vectors_adamw(g_sum, *([t[n] for n in SMALL] for t in (w, mom, var)))

    outs = []
    for mats, vecs, big in zip(res, vec, (g_in, d_in, m_in, v_in)):
        t = {**{n: a[None] for n, a in zip(MATS, mats)}, **dict(zip(SMALL, vecs)),
             "w_in": jnp.transpose(big, (1, 2, 0))}
        outs += [t[n] for n in ORDER]
    return (total.reshape(()), dx[None], *outs)
```

```python
import math

import jax
import jax.numpy as jnp
import numpy as np
from jax import lax
from jax.experimental import pallas as pl
from jax.experimental.pallas import tpu as pltpu

F32, BF16 = jnp.float32, jnp.bfloat16

D_MODEL = 1024
EPS = 1e-6
HEADS = 8
NOPE, ROPE, VDIM = 64, 32, 64
QK = NOPE + ROPE
Q_LORA, KV_LORA = 768, 256
ROPE_THETA = 10000.0
ATT_CHUNK_SHIFT = 6
HG_BLOCK = 32
HG_WIDTH = 512
D_IN = 5664
D_IN_PAD = 5760
W_IN_SHARD = D_IN // 8
N_DEV = 8
LANE = 128

ADAM_LR, ADAM_B1, ADAM_B2, ADAM_EPS, ADAM_WD, ADAM_STEP = 0.001, 0.9, 0.999, 1e-08, 0.01, 10

W_IN_SEGMENTS = ((3616, 5664, 0), (1056, 1568, 2048), (3104, 3616, 2560), (1568, 3104, 3072),
                 (0, 1024, 4608), (1024, 1056, 5696))

NT = (((1,), (1,)), ((), ()))
TN = (((0,), (0,)), ((), ()))
MESH_ID = pl.DeviceIdType.MESH


def _w_in_pieces():
    out = []
    for lo, hi, dst in W_IN_SEGMENTS:
        c = lo
        while c < hi:
            p = c // W_IN_SHARD
            e = min(hi, (p + 1) * W_IN_SHARD)
            out.append((p, c - p * W_IN_SHARD, e - p * W_IN_SHARD, dst + c - lo))
            c = e
    return out


def _params(sem, vmem_mb=48):
    return pltpu.CompilerParams(dimension_semantics=sem, vmem_limit_bytes=vmem_mb * 2**20)


def _dot(a, b):
    return jnp.dot(a, b, preferred_element_type=F32)


def _dotg(a, b, dims):
    return lax.dot_general(a, b, dims, preferred_element_type=F32)


def _split2(x):
    hi = x.astype(BF16)
    return hi, (x - hi.astype(F32)).astype(BF16)


def _sel_left(m01, x):
    hi, lo = _split2(x)
    return _dot(m01, hi) + _dot(m01, lo)


def _sel_right(x, m01):
    hi, lo = _split2(x)
    return _dot(hi, m01) + _dot(lo, m01)


def _hi_lo(x):
    hi = x.astype(BF16).astype(F32)
    return hi, x - hi


def _sigmoid(x):
    return 0.5 * jnp.tanh(0.5 * x) + 0.5


def _rope(x, c, s1, s2):
    return x * c + pltpu.roll(x, 112, 1) * s1 + pltpu.roll(x, 16, 1) * s2


def _unrope(d, c, s1, s2):
    return d * c + pltpu.roll(d * s1, 16, 1) + pltpu.roll(d * s2, 112, 1)


def _my_place():
    return lax.axis_index("x"), lax.axis_index("y"), lax.axis_index("c")


def _flip(k, x, y, c):
    fx, fy, fc = (k + 1) >> 2 & 1, (k + 1) >> 1 & 1, (k + 1) & 1
    return (1 - x if fx else x), (1 - y if fy else y), (1 - c if fc else c)


def _to_all_copies(s_refs, r_refs, sems, spread):
    send_sems, recv_sems, local_sems = sems
    x, y, c = _my_place()
    me = 4 * x + 2 * y + c
    src = (lambda a, p: s_refs[a]) if spread else (lambda a, p: s_refs[a].at[p])
    local = [pltpu.make_async_copy(src(a, me), r_refs[a].at[me], local_sems.at[a]) for a in range(len(s_refs))]
    remote = []
    for k in range(N_DEV - 1):
        px, py, pc = _flip(k, x, y, c)
        for a in range(len(s_refs)):
            remote.append(pltpu.make_async_remote_copy(
                src_ref=src(a, 4 * px + 2 * py + pc), dst_ref=r_refs[a].at[me],
                send_sem=send_sems.at[7 * a + k], recv_sem=recv_sems.at[7 * a + k],
                device_id=(px, py, pc), device_id_type=MESH_ID))
    return local, remote


def _to_chips_copies(s_refs, r_refs, sems):
    send_sems, recv_sems, local_sems = sems
    x, y, c = _my_place()
    me = 2 * x + y
    local = [pltpu.make_async_copy(s_refs[a].at[me], r_refs[a].at[me], local_sems.at[a]) for a in range(len(s_refs))]
    remote = []
    for k in range(3):
        px = 1 - x if (k + 1) >> 1 & 1 else x
        py = 1 - y if (k + 1) & 1 else y
        for a in range(len(s_refs)):
            remote.append(pltpu.make_async_remote_copy(
                src_ref=s_refs[a].at[2 * px + py], dst_ref=r_refs[a].at[me],
                send_sem=send_sems.at[3 * a + k], recv_sem=recv_sems.at[3 * a + k],
                device_id=(px, py, c), device_id_type=MESH_ID))
    return local, remote


def _start_all(local, remote):
    for cp in local + remote:
        cp.start()


def _wait_all(local, remote):
    for cp in remote:
        cp.wait_recv()
    for cp in remote:
        cp.wait_send()
    for cp in local:
        cp.wait()


def _copy_sems(n, peers):
    return [pltpu.SemaphoreType.DMA((peers * n,)), pltpu.SemaphoreType.DMA((peers * n,)),
            pltpu.SemaphoreType.DMA((n,))]


ANY = pl.BlockSpec(memory_space=pl.ANY)


def _dw_in_slots(ht, dproj):
    m, k = ht.shape
    n = dproj.shape[1]
    tn, tk = 1152, 2048
    nj, nk = n // tn, k // tk
    by_tile = [[] for _ in range(nj)]
    for p, lo, hi, dst in _w_in_pieces():
        while lo < hi:
            j = dst // tn
            cnt = min(hi - lo, (j + 1) * tn - dst)
            by_tile[j].append((p, lo, lo + cnt, dst - j * tn))
            lo, dst = lo + cnt, dst + cnt

    def body(a_ref, b_ref, s_ref, acc_ref):
        j, l = pl.program_id(0), pl.program_id(1)

        @pl.when(l == 0)
        def _():
            acc_ref[...] = jnp.zeros_like(acc_ref)

        acc_ref[...] += _dot(a_ref[...], b_ref[...])

        @pl.when(l == nk - 1)
        def _():
            at = acc_ref[...].T
            for jj in range(nj):
                @pl.when(j == jj)
                def _(jj=jj):
                    for p, lo, hi, d in by_tile[jj]:
                        s_ref[p, lo:hi, :] = at[d:d + hi - lo, :].astype(BF16)

    return pl.pallas_call(
        body,
        grid=(nj, nk),
        in_specs=[pl.BlockSpec((m, tk), lambda j, l: (0, l)), pl.BlockSpec((tk, tn), lambda j, l: (l, j))],
        out_specs=pl.BlockSpec((N_DEV, W_IN_SHARD, m), lambda j, l: (0, 0, 0)),
        out_shape=jax.ShapeDtypeStruct((N_DEV, W_IN_SHARD, m), BF16),
        scratch_shapes=[pltpu.VMEM((m, tn), F32)],
        compiler_params=_params(("arbitrary", "arbitrary"), 56),
        name="dw_in",
    )(ht, dproj)


GP_TN = 256
GP_COLS = 5888
GP_NT = GP_COLS // GP_TN


def _gp_tile_pieces():
    tiles = [[] for _ in range(GP_NT)]
    for p, lo, hi, dst in _w_in_pieces():
        while lo < hi:
            t = dst // GP_TN
            n = min(hi - lo, (t + 1) * GP_TN - dst)
            tiles[t].append((p, lo, lo + n, dst - t * GP_TN))
            lo, dst = lo + n, dst + n
    return tiles


def _gp_tables():
    pieces = _gp_tile_pieces()
    rank_of = {None: 0, 0: 1, 1: 2, 2: 2, 4: 3, 5: 3, 3: 4, 6: 5}
    order = np.zeros((N_DEV, GP_NT), np.int32)
    waits = np.zeros((N_DEV, GP_NT), np.int32)
    for me in range(N_DEV):
        x, y, c = me >> 2 & 1, me >> 1 & 1, me & 1
        chips = [(1 - x, y), (x, 1 - y), (1 - x, 1 - y)]

        def sem_of(p):
            px, py, pc = p >> 2 & 1, p >> 1 & 1, p & 1
            if (px, py) == (x, y):
                return None if pc == c else 0
            j = chips.index((px, py))
            return 1 + j if pc == c else 4 + j

        needs = [sorted({sem_of(p) for p, _, _, _ in tile} - {None}) for tile in pieces]
        ranks = [max([rank_of[k] for k in ks], default=0) for ks in needs]
        seq = sorted(range(GP_NT), key=lambda t: (ranks[t], t))
        seen = set()
        for step, t in enumerate(seq):
            order[me, step] = t
            new = [k for k in needs[t] if k not in seen]
            for k in new:
                waits[me, step] |= 1 << k
            seen.update(new)
        assert seen == set(range(7)), (me, seen)
    return order, waits


def _gather_proj(x, g_pre, w_blk, shards):
    s = x.shape[0]
    tx = 512
    ns = len(shards)
    tile_pieces = _gp_tile_pieces()
    order_np, waits_np = _gp_tables()
    xq, yq, cq = _my_place()
    me_out = 4 * xq + 2 * yq + cq
    order = lax.dynamic_index_in_dim(jnp.asarray(order_np), me_out, 0, keepdims=False)
    waits = lax.dynamic_index_in_dim(jnp.asarray(waits_np), me_out, 0, keepdims=False)

    def body(order_ref, waits_ref, x_hbm, g_ref, wblk_hbm, *rest):
        shard_refs, (proj_ref, wt_ref, ht_hbm), got_refs = rest[:ns], rest[ns:ns + 3], rest[ns + 3:2 * ns + 3]
        recv, h_ref, wtile, xbuf, htbuf = rest[2 * ns + 3:2 * ns + 8]
        send_sems, recv_sems, misc_sems = rest[2 * ns + 8:2 * ns + 11]
        sems = rest[2 * ns + 11:]
        t = pl.program_id(0)
        x_, y_, c = _my_place()
        sibling = (x_, y_, 1 - c)
        chips = [(1 - x_, y_), (x_, 1 - y_), (1 - x_, 1 - y_)]
        idx = lambda px, py, pc: 4 * px + 2 * py + pc
        me = idx(x_, y_, c)

        def copy(k, slot, to, src=None):
            return pltpu.make_async_remote_copy(
                src_ref=recv.at[slot] if src is None else src, dst_ref=recv.at[slot],
                send_sem=send_sems.at[k], recv_sem=recv_sems.at[k], device_id=to, device_id_type=MESH_ID)

        mine = pltpu.make_async_copy(wblk_hbm, recv.at[me], misc_sems.at[0])
        first = [copy(0, me, sibling, src=wblk_hbm)] + [copy(1 + j, me, (*chips[j], c), src=wblk_hbm) for j in range(2)]
        passed = [copy(4 + j, idx(*ch, c), sibling) for j, ch in enumerate(chips)]
        onward = [copy(3, idx(*chips[0], c), (*chips[1], c)), copy(3, idx(*chips[1], c), (*chips[0], c))]
        arrivals = ([copy(0, idx(x_, y_, 1 - c), sibling)] + [copy(1 + j, idx(*ch, c), sibling) for j, ch in enumerate(chips)]
                    + [copy(4 + j, idx(*ch, 1 - c), sibling) for j, ch in enumerate(chips)])

        @pl.when(t == 0)
        def _():
            mine.start()
            for cp in first:
                cp.start()
            _start_all(*_to_all_copies(shard_refs, got_refs, sems, True))

            def load(i):
                return pltpu.make_async_copy(x_hbm.at[pl.ds(i * tx, tx), :], xbuf.at[i & 1], misc_sems.at[1 + (i & 1)])

            def store(i):
                return pltpu.make_async_copy(htbuf.at[i & 1], ht_hbm.at[:, pl.ds(i * tx, tx)], misc_sems.at[3 + (i & 1)])

            load(0).start()
            for i in range(s // tx):
                if i + 1 < s // tx:
                    load(i + 1).start()
                load(i).wait()
                xv = xbuf[i & 1]
                r = lax.rsqrt(jnp.mean(xv * xv, axis=-1, keepdims=True) + EPS)
                h = (xv * r * g_ref[...]).astype(BF16)
                h_ref[i * tx:(i + 1) * tx, :] = h
                if i >= 2:
                    store(i - 2).wait()
                htbuf[i & 1] = h.T
                store(i).start()
            for i in range(max(s // tx - 2, 0), s // tx):
                store(i).wait()
            mine.wait()

        w = waits_ref[t]
        for k in range(7):
            @pl.when((w >> k) & 1 == 1)
            def _(k=k):
                arrivals[k].wait_recv()
                if 1 <= k <= 3:
                    passed[k - 1].start()
                if 1 <= k <= 2:
                    @pl.when(c == k - 1)
                    def _():
                        onward[k - 1].start()

        tile = order_ref[t]
        for tt in range(GP_NT):
            @pl.when(tile == tt)
            def _(tt=tt):
                covered = sorted((d, d + hi - lo) for _, lo, hi, d in tile_pieces[tt])
                at = 0
                for lo_z, hi_z in covered + [(GP_TN, GP_TN)]:
                    if lo_z > at:
                        wtile[at:lo_z, :] = jnp.zeros((lo_z - at, D_MODEL), BF16)
                    at = max(at, hi_z)
                for p, lo, hi, d in tile_pieces[tt]:
                    wtile[d:d + hi - lo, :] = recv[p, lo:hi, :]

        wt = wtile[...]
        wt_ref[...] = wt
        proj_ref[...] = _dotg(h_ref[...], wt, NT)

        @pl.when(t == GP_NT - 1)
        def _():
            for cp in first + passed + onward[:1]:
                cp.wait_send()
            _wait_all(*_to_all_copies(shard_refs, got_refs, sems, True))

    grid_spec = pltpu.PrefetchScalarGridSpec(
        num_scalar_prefetch=2,
        grid=(GP_NT,),
        in_specs=[ANY, pl.BlockSpec((1, D_MODEL), lambda t, o, w: (0, 0)), ANY] + [ANY] * ns,
        out_specs=[pl.BlockSpec((s, GP_TN), lambda t, o, w: (0, o[t])),
                   pl.BlockSpec((GP_TN, D_MODEL), lambda t, o, w: (o[t], 0)), ANY] + [ANY] * ns,
        scratch_shapes=[pltpu.VMEM((N_DEV, W_IN_SHARD, D_MODEL), BF16), pltpu.VMEM((s, D_MODEL), BF16),
                        pltpu.VMEM((GP_TN, D_MODEL), BF16), pltpu.VMEM((2, tx, D_MODEL), F32),
                        pltpu.VMEM((2, D_MODEL, tx), BF16),
                        pltpu.SemaphoreType.DMA((7,)), pltpu.SemaphoreType.DMA((7,)), pltpu.SemaphoreType.DMA((5,))]
        + _copy_sems(ns, 7),
    )
    return pl.pallas_call(
        body,
        grid_spec=grid_spec,
        out_shape=[jax.ShapeDtypeStruct((s, GP_COLS), F32),jax.ShapeDtypeStruct((GP_COLS, D_MODEL), BF16),
                   jax.ShapeDtypeStruct((D_MODEL, s), BF16)]
        + [jax.ShapeDtypeStruct((N_DEV,) + b.shape, b.dtype) for b in shards],
        compiler_params=_params(("arbitrary",), 56),
        name="gather_proj",
    )(order, waits, x, g_pre, w_blk, *shards)


def _mla_prep(proj, g_q, g_kv, w_uq_p, w_kv_p, rc, rs1, rs2):
    s = proj.shape[0]
    tm = 1024
    scale = 1.0 / math.sqrt(QK)

    def body(cq_ref, ckv_ref, kpe_ref, gq_ref, gkv_ref, wuq_ref, wkv_ref, c_ref, s1_ref, s2_ref,
             qr_ref, kr_ref, v_ref, cqt_ref, ckvt_ref):
        cq = cq_ref[...]
        r = lax.rsqrt(jnp.mean(cq * cq, axis=-1, keepdims=True) + EPS)
        cqn = (cq * r * gq_ref[...]).astype(BF16)
        cqt_ref[...] = cqn.T
        q = _dot(cqn, wuq_ref[...])
        ckv = ckv_ref[...]
        r = lax.rsqrt(jnp.mean(ckv * ckv, axis=-1, keepdims=True) + EPS)
        ckvn = (ckv * r * gkv_ref[...]).astype(BF16)
        ckvt_ref[...] = ckvn.T
        kv = _dot(ckvn, wkv_ref[...])
        c, s1, s2 = c_ref[...], s1_ref[...], s2_ref[...]
        lane = lax.broadcasted_iota(jnp.int32, (tm, LANE), 1)
        kpe = _rope(kpe_ref[...], c, s1, s2) + jnp.where((lane == QK) | (lane == QK + 1), 1.0, 0.0)
        vone = jnp.where((lane == VDIM) | (lane == VDIM + 1), 1.0, 0.0)
        for h in range(HEADS):
            sl = slice(LANE * h, LANE * (h + 1))
            qr_ref[:, sl] = (_rope(q[:, sl], c, s1, s2) * scale).astype(BF16)
            kr_ref[:, sl] = (kv[:, sl] + kpe).astype(BF16)
            v_ref[:, sl] = (kv[:, HEADS * LANE + LANE * h:HEADS * LANE + LANE * (h + 1)] + vone).astype(BF16)

    row = lambda w, j: pl.BlockSpec((tm, w), lambda i: (i, j))
    col = lambda w: pl.BlockSpec((w, tm), lambda i: (0, i))
    full = lambda a: pl.BlockSpec(a.shape, lambda i: (0, 0))
    return pl.pallas_call(
        body,
        grid=(s // tm,),
        in_specs=[row(768, 6), row(256, 21), row(128, 44), full(g_q), full(g_kv), full(w_uq_p), full(w_kv_p),
                  row(128, 0), row(128, 0), row(128, 0)],
        out_specs=[row(1024, 0), row(1024, 0), row(1024, 0), col(768), col(256)],
        out_shape=[jax.ShapeDtypeStruct((s, 1024), BF16), jax.ShapeDtypeStruct((s, 1024), BF16),
                   jax.ShapeDtypeStruct((s, 1024), BF16), jax.ShapeDtypeStruct((768, s), BF16),
                   jax.ShapeDtypeStruct((256, s), BF16)],
        compiler_params=_params(("arbitrary",), 56),
        name="mla_prep",
    )(proj, proj, proj, g_q, g_kv, w_uq_p, w_kv_p, rc, rs1, rs2)


ATT_T = 512
ATT_FWD_HEADS = 4


def _chunk_mask(transposed):
    r = lax.broadcasted_iota(jnp.int32, (ATT_T, ATT_T), 0) >> ATT_CHUNK_SHIFT
    c = lax.broadcasted_iota(jnp.int32, (ATT_T, ATT_T), 1) >> ATT_CHUNK_SHIFT
    return (r <= c) if transposed else (c <= r)


def _attn_fwd(qr, kr, vp, shards):
    s = qr.shape[0]
    t = ATT_T
    g = ATT_FWD_HEADS
    ns = len(shards)

    def body(q_ref, k_ref, v_ref, *rest):
        shard_refs, (o_ref, qa_ref), got_refs = rest[:ns], rest[ns:ns + 2], rest[ns + 2:2 * ns + 2]
        sc_ref, sems = rest[2 * ns + 2], rest[2 * ns + 3:]
        qi = pl.program_id(1)

        @pl.when((pl.program_id(0) == 0) & (qi == 0))
        def _():
            _start_all(*_to_all_copies(shard_refs, got_refs, sems, True))
        lane = lax.broadcasted_iota(jnp.int32, (t, LANE), 1)
        sls = [slice(LANE * a, LANE * (a + 1)) for a in range(g)]
        qs = [q_ref[:, sl] for sl in sls]

        def scores(j):
            rows = pl.ds(pl.multiple_of(j * t, t), t)
            for a in range(g):
                sc_ref[j & 1, a] = _dotg(qs[a], k_ref[rows, sls[a]], NT)

        def step(j, carry, masked):
            rows = pl.ds(pl.multiple_of(j * t, t), t)
            out = []
            for a in range(g):
                m, acc = carry[a]
                sc = sc_ref[j & 1, a]
                if masked:
                    sc = jnp.where(_chunk_mask(False), sc, -1e30)
                m_new = jnp.maximum(m, jnp.max(sc, axis=-1, keepdims=True))
                p = jnp.exp(sc - m_new).astype(BF16)
                acc = jnp.exp(m - m_new) * acc + _dot(p, v_ref[rows, sls[a]])
                out.append((m_new, acc))
            return tuple(out)

        def loop(j, carry):
            carry = step(j, carry, False)
            scores(j + 1)
            return carry

        init = tuple((jnp.full((t, 1), -1e30, F32), jnp.zeros((t, LANE), F32)) for _ in range(g))
        scores(0)
        carry = lax.fori_loop(0, qi, loop, init)
        carry = step(qi, carry, True)
        outs = []
        for a in range(g):
            m, acc = carry[a]
            l = acc[:, VDIM:VDIM + 1]
            outs.append(acc / l)
            hi, lo_part = _hi_lo(-(m + jnp.log(l)))
            qa = jnp.where(lane == QK, hi, jnp.where(lane == QK + 1, lo_part, qs[a].astype(F32)))
            qa_ref[:, sls[a]] = qa.astype(BF16)
        for p in range(g // 2):
            o_ref[:, LANE * p:LANE * (p + 1)] = jnp.where(lane < VDIM, outs[2 * p], pltpu.roll(outs[2 * p + 1], VDIM, 1))

        @pl.when((pl.program_id(0) == HEADS // g - 1) & (qi == s // t - 1))
        def _():
            _wait_all(*_to_all_copies(shard_refs, got_refs, sems, True))

    return pl.pallas_call(
        body,
        grid=(HEADS // g, s // t),
        in_specs=[
            pl.BlockSpec((t, g * LANE), lambda h, i: (i, h)),
            pl.BlockSpec((s, g * LANE), lambda h, i: (0, h)),
            pl.BlockSpec((s, g * LANE), lambda h, i: (0, h)),
        ] + [ANY] * ns,
        out_specs=[
            pl.BlockSpec((t, g * VDIM), lambda h, i: (i, h)),
            pl.BlockSpec((t, g * LANE), lambda h, i: (i, h)),
        ] + [ANY] * ns,
        out_shape=[jax.ShapeDtypeStruct((s, 512), F32), jax.ShapeDtypeStruct((s, 1024), BF16)]
        + [jax.ShapeDtypeStruct((N_DEV,) + b.shape, b.dtype) for b in shards],
        scratch_shapes=[pltpu.VMEM((2, g, t, t), F32)] + _copy_sems(ns, 7),
        compiler_params=_params(("arbitrary", "arbitrary")),
        name="attn_fwd",
    )(qr, kr, vp, *shards)


def _attn_bwd(qa, kr, vp, dop, sends):
    s = qa.shape[0]
    t = ATT_T
    nq = s // t
    ns = len(sends)

    def body(q_ref, k_ref, v_ref, do_ref, *rest):
        send_refs, (dq_out, dk_out, dv_out) = rest[:ns], rest[ns:ns + 3]
        recv_refs = rest[ns + 3:2 * ns + 3]
        (dq_ref, dk_ref, dv_ref), sems = rest[2 * ns + 3:2 * ns + 6], rest[2 * ns + 6:]
        j = pl.program_id(1)
        sls = [slice(LANE * a, LANE * (a + 1)) for a in range(2)]

        @pl.when((pl.program_id(0) == 0) & (j == 0))
        def _():
            _start_all(*_to_all_copies(send_refs, recv_refs, sems, False))

        @pl.when(j == 0)
        def _():
            dq_ref[...] = jnp.zeros_like(dq_ref)

        dk_ref[...] = jnp.zeros_like(dk_ref)
        dv_ref[...] = jnp.zeros_like(dv_ref)
        ks = [k_ref[:, sl] for sl in sls]
        vs = [v_ref[:, sl] for sl in sls]

        def part(i, k_lo, k_n, q_lo, q_n, masked):
            rows = pl.ds(pl.multiple_of(i * t + q_lo, 256), q_n)
            keys = slice(k_lo, k_lo + k_n)
            for a in range(2):
                q = q_ref[rows, sls[a]]
                do = do_ref[rows, sls[a]]
                sc = _dotg(ks[a][keys], q, NT)
                if masked:
                    kc = lax.broadcasted_iota(jnp.int32, (k_n, q_n), 0) >> ATT_CHUNK_SHIFT
                    qc = lax.broadcasted_iota(jnp.int32, (k_n, q_n), 1) >> ATT_CHUNK_SHIFT
                    sc = jnp.where(kc <= qc, sc, -1e30)
                p = jnp.exp(sc)
                ds = (p * _dotg(vs[a][keys], do, NT)).astype(BF16)
                dv_ref[keys, sls[a]] += _dot(p.astype(BF16), do)
                dk_ref[keys, sls[a]] += _dot(ds, q)
                dq_ref[rows, sls[a]] += _dotg(ds, ks[a][keys], TN)

        half = t // 2
        part(j, 0, half, 0, t, True)
        part(j, half, half, half, half, True)

        def loop(i, c):
            part(i, 0, t, 0, t, False)
            return c

        lax.fori_loop(j + 1, nq, loop, 0)
        dk_out[...] = dk_ref[...].astype(BF16)
        dv_out[...] = dv_ref[...].astype(BF16)

        @pl.when(j == nq - 1)
        def _():
            dq_out[...] = dq_ref[...].astype(BF16)

        @pl.when((pl.program_id(0) == HEADS // 2 - 1) & (j == nq - 1))
        def _():
            _wait_all(*_to_all_copies(send_refs, recv_refs, sems, False))

    blk = pl.BlockSpec((t, 2 * LANE), lambda h, j: (j, h))
    whole = pl.BlockSpec((s, 2 * LANE), lambda h, j: (0, h))
    out = jax.ShapeDtypeStruct((s, 1024), BF16)
    return pl.pallas_call(
        body,
        grid=(HEADS // 2, nq),
        in_specs=[whole, blk, blk, whole] + [ANY] * ns,
        out_specs=[whole, blk, blk] + [ANY] * ns,
        out_shape=[out, out, out] + [jax.ShapeDtypeStruct(a.shape, a.dtype) for a in sends],
        scratch_shapes=[pltpu.VMEM((s, 2 * LANE), F32), pltpu.VMEM((t, 2 * LANE), F32),
                        pltpu.VMEM((t, 2 * LANE), F32)] + _copy_sems(ns, 7),
        compiler_params=_params(("arbitrary", "arbitrary")),
        name="attn_bwd",
    )(qa, kr, vp, dop, *sends)


HG_T = 256
HG_NC = HG_T // HG_BLOCK
HG_G = 4
GW = 64 * HG_G


def _hg_consts():
    r = jnp.arange(HG_T)[:, None]
    c = jnp.arange(HG_T)[None, :]
    same = (r // HG_BLOCK) == (c // HG_BLOCK)
    mcum = (same & (c <= r)).astype(BF16)
    mrev = (same & (c >= r)).astype(BF16)
    msum = same.astype(BF16)
    a = jnp.arange(GW) // 64
    bd = (a[:, None] == a[None, :]).astype(F32)
    return mcum, mrev, msum, bd


def _stack_heads(xg, head):
    return jnp.concatenate([jnp.where(head == h, xg, 0.0) for h in range(HG_G)], axis=0)


def _unstack_heads(r, head, t):
    out = r[(HG_G - 1) * t:]
    for h in range(HG_G - 2, -1, -1):
        out = jnp.where(head == h, r[h * t:(h + 1) * t], out)
    return out


def _compact_state(st):
    out = st[:64]
    for h in range(1, HG_G):
        out = out + st[64 * h:64 * (h + 1)]
    return out


def _expand_state(cs, head64):
    return jnp.concatenate([jnp.where(head64 == h, cs, 0.0) for h in range(HG_G)], axis=0)


def _hg_pre(hq, hf, lbl, mcum, msum):
    lb = _sigmoid(lbl[0:1, :] - lbl[1:2, :])
    sig = _sigmoid(hf)
    f = lb + (1.0 - lb) * sig
    lf = jnp.log(f)
    b = _sel_left(mcum, lf)
    big_l = _sel_left(msum, lf)
    k = 1.0 - f
    qd = hq * jnp.exp(b)
    ki = k * jnp.exp(-b)
    ke = k * jnp.exp(big_l - b)
    return lb, sig, f, b, big_l, qd, ki, ke


def _hgrn_fwd(proj, lbl):
    s = proj.shape[0]
    t = HG_T
    mcum, _, msum, bd = _hg_consts()

    def body(hq_ref, hf_ref, hi_ref, lbl_ref, mcum_ref, msum_ref, bd_ref, o_ref, sp_ref, st_ref):
        @pl.when(pl.program_id(0) == 0)
        def _():
            st_ref[...] = jnp.zeros_like(st_ref)

        mc = mcum_ref[...]
        _, _, _, _, big_l, qd, ki, ke = _hg_pre(hq_ref[...], hf_ref[...], lbl_ref[...], mc, msum_ref[...])
        el = jnp.exp(big_l)
        hi = hi_ref[...]
        head = lax.broadcasted_iota(jnp.int32, (t, GW), 1) >> 6
        mask = jnp.concatenate([mc] * HG_G, axis=0) > 0.5
        for p in range(HEADS // HG_G):
            sl = slice(GW * p, GW * (p + 1))
            vp = hi[:, sl].astype(BF16)
            qs = _stack_heads(qd[:, sl], head).astype(BF16)
            a = jnp.where(mask, _dotg(qs, ki[:, sl].astype(BF16), NT), 0.0)
            o_intra = _unstack_heads(_dot(a.astype(BF16), vp), head, t)
            qb = qd[:, sl].astype(BF16)
            kb = ke[:, sl].astype(BF16)
            st = st_ref[p]
            for c in range(HG_NC):
                rows = slice(HG_BLOCK * c, HG_BLOCK * (c + 1))
                sp_ref[c, :, sl] = _compact_state(st)
                o_ref[rows, sl] = o_intra[rows] + _dotg(qb[rows], st.astype(BF16), NT)
                u = _dotg(vp[rows], kb[rows], TN) * bd_ref[...]
                st = st * el[HG_BLOCK * c:HG_BLOCK * c + 1, sl] + u
            st_ref[p] = st

    row = lambda j: pl.BlockSpec((t, HG_WIDTH), lambda i: (i, j))
    full = lambda a: pl.BlockSpec(a.shape, lambda i: (0, 0))
    return pl.pallas_call(
        body,
        grid=(s // t,),
        in_specs=[row(6), row(7), row(8), full(lbl), full(mcum), full(msum), full(bd)],
        out_specs=[row(0), pl.BlockSpec((HG_NC, 64, HG_WIDTH), lambda i: (i, 0, 0))],
        out_shape=[jax.ShapeDtypeStruct((s, HG_WIDTH), F32),
                   jax.ShapeDtypeStruct((s // HG_BLOCK, 64, HG_WIDTH), F32)],
        scratch_shapes=[pltpu.VMEM((HEADS // HG_G, GW, GW), F32)],
        compiler_params=_params(("arbitrary",)),
        name="hgrn_fwd",
    )(proj, proj, proj, lbl, mcum, msum, bd)


def _slot_shape(name, r, c):
    return (N_DEV, r, c // N_DEV) if COL_SHARDED[name] else (N_DEV, r // N_DEV, c)


def _emit_slots(name, acc_ref, out_ref):
    r, c = acc_ref.shape
    for p in range(N_DEV):
        if COL_SHARDED[name]:
            out_ref[p] = acc_ref[:, c // N_DEV * p:c // N_DEV * (p + 1)].astype(BF16)
        else:
            out_ref[p] = acc_ref[r // N_DEV * p:r // N_DEV * (p + 1), :].astype(BF16)


def _hgrn_bwd(proj, lbl, do, sprev, dproj, pairs):
    s = proj.shape[0]
    t = HG_T
    nt = s // t
    npair = len(pairs)
    mcum, mrev, msum, bd = _hg_consts()

    def body(hq_ref, hf_ref, hi_ref, lbl_ref, do_ref, sp_ref, mcum_ref, mrev_ref, msum_ref, bd_ref,
             dproj_in, *rest):
        del dproj_in
        pair_refs, (dh_ref, dlbl_ref) = rest[:2 * npair], rest[2 * npair:2 * npair + 2]
        dw_refs, g_ref, acc_refs = rest[2 * npair + 2:3 * npair + 2], rest[3 * npair + 2], rest[3 * npair + 3:]

        @pl.when(pl.program_id(0) == 0)
        def _():
            g_ref[...] = jnp.zeros_like(g_ref)
            dlbl_ref[...] = jnp.zeros_like(dlbl_ref)
            for acc_ref in acc_refs:
                acc_ref[...] = jnp.zeros_like(acc_ref)

        for n, acc_ref in enumerate(acc_refs):
            acc_ref[...] += _dot(pair_refs[2 * n][...], pair_refs[2 * n + 1][...])

        @pl.when(pl.program_id(0) == nt - 1)
        def _():
            for (name, _, _), acc_ref, dw_ref in zip(pairs, acc_refs, dw_refs):
                _emit_slots(name, acc_ref, dw_ref)

        mc = mcum_ref[...]
        lb, sig, f, b, big_l, qd, ki, ke = _hg_pre(hq_ref[...], hf_ref[...], lbl_ref[...], mc, msum_ref[...])
        el = jnp.exp(big_l)
        hi = hi_ref[...]
        dov = do_ref[...]
        head = lax.broadcasted_iota(jnp.int32, (t, GW), 1) >> 6
        head64 = lax.broadcasted_iota(jnp.int32, (64, GW), 1) >> 6
        mask = jnp.concatenate([mc] * HG_G, axis=0) > 0.5
        dqd_parts, dke_parts, dv_parts, del_parts, dki_parts = [], [], [], [], []
        for p in range(HEADS // HG_G):
            sl = slice(GW * p, GW * (p + 1))
            vp = hi[:, sl].astype(BF16)
            qs = _stack_heads(qd[:, sl], head).astype(BF16)
            kip = ki[:, sl].astype(BF16)
            dos = _stack_heads(dov[:, sl], head).astype(BF16)
            a = jnp.where(mask, _dotg(qs, kip, NT), 0.0).astype(BF16)
            da = jnp.where(mask, _dotg(dos, vp, NT), 0.0).astype(BF16)
            r = _dot(da, kip)
            dki_parts.append(_dotg(da, qs, TN))
            qb = qd[:, sl].astype(BF16)
            kb = ke[:, sl].astype(BF16)
            dob = dov[:, sl].astype(BF16)
            g = g_ref[p]
            dqd_c, dv_c, dke_c, del_c = [], [], [], []
            for c in range(HG_NC - 1, -1, -1):
                rows = slice(HG_BLOCK * c, HG_BLOCK * (c + 1))
                gb = g.astype(BF16)
                st = _expand_state(sp_ref[c, :, sl], head64)
                dqd_c.append(_dot(dob[rows], st.astype(BF16)))
                dv_c.append(_dotg(kb[rows], gb, NT))
                dke_c.append(_dot(vp[rows], gb))
                del_c.append(jnp.broadcast_to(jnp.sum(g * st, axis=0, keepdims=True), (HG_BLOCK, GW)))
                g = g * el[HG_BLOCK * c:HG_BLOCK * c + 1, sl] + _dotg(dob[rows], qb[rows], TN) * bd_ref[...]
            g_ref[p] = g
            up = lambda parts: jnp.concatenate(parts[::-1], axis=0)
            dqd_parts.append(_unstack_heads(r, head, t) + up(dqd_c))
            dv_parts.append(_dotg(a, dos, TN) + up(dv_c))
            dke_parts.append(up(dke_c))
            del_parts.append(up(del_c))
        wide = lambda parts: jnp.concatenate(parts, axis=1)
        dqd, dke, dki, dvv, del_rows = wide(dqd_parts), wide(dke_parts), wide(dki_parts), wide(dv_parts), wide(del_parts)
        dh_ref[:, :HG_WIDTH] = (dqd * jnp.exp(b)).astype(BF16)
        dh_ref[:, 2 * HG_WIDTH:] = dvv.astype(BF16)
        dke_ke = dke * ke
        db = dqd * qd - dki * ki - dke_ke
        dl_rows = _sel_left(msum_ref[...], dke_ke) + del_rows * el
        is_last = (lax.broadcasted_iota(jnp.int32, (t, HG_WIDTH), 0) & (HG_BLOCK - 1)) == HG_BLOCK - 1
        db = db + jnp.where(is_last, dl_rows, 0.0)
        dlf = _sel_left(mrev_ref[...], db)
        dk = dki * jnp.exp(-b) + dke * jnp.exp(big_l - b)
        df = dlf / f - dk
        dh_ref[:, HG_WIDTH:2 * HG_WIDTH] = (df * (1.0 - lb) * sig * (1.0 - sig)).astype(BF16)
        dlb = jnp.sum(df * (1.0 - sig), axis=0, keepdims=True) * lb * (1.0 - lb)
        dlbl_ref[0:1, :] += dlb
        dlbl_ref[1:2, :] -= dlb

    rrow = lambda j: pl.BlockSpec((t, HG_WIDTH), lambda i: (nt - 1 - i, j))
    full = lambda a: pl.BlockSpec(a.shape, lambda i: (0, 0))
    pair_specs, dw_specs, dw_shapes, accs = [], [], [], []
    for name, at, b in pairs:
        pair_specs += [pl.BlockSpec((at.shape[0], t), lambda i: (0, i)), pl.BlockSpec((t, b.shape[1]), lambda i: (i, 0))]
        shape = _slot_shape(name, at.shape[0], b.shape[1])
        dw_specs.append(pl.BlockSpec(shape, lambda i: (0, 0, 0)))
        dw_shapes.append(jax.ShapeDtypeStruct(shape, BF16))
        accs.append(pltpu.VMEM((at.shape[0], b.shape[1]), F32))
    return pl.pallas_call(
        body,
        grid=(nt,),
        in_specs=[rrow(6), rrow(7), rrow(8), full(lbl), rrow(0),
                  pl.BlockSpec((HG_NC, 64, HG_WIDTH), lambda i: (nt - 1 - i, 0, 0)),
                  full(mcum), full(mrev), full(msum), full(bd), pl.BlockSpec(memory_space=pl.ANY)] + pair_specs,
        out_specs=[pl.BlockSpec((t, 3 * HG_WIDTH), lambda i: (nt - 1 - i, 2)),
                   pl.BlockSpec((2, HG_WIDTH), lambda i: (0, 0))] + dw_specs,
        out_shape=[jax.ShapeDtypeStruct(dproj.shape, BF16), jax.ShapeDtypeStruct((2, HG_WIDTH), F32)] + dw_shapes,
        input_output_aliases={10: 0},
        scratch_shapes=[pltpu.VMEM((HEADS // HG_G, GW, GW), F32)] + accs,
        compiler_params=_params(("arbitrary",)),
        name="hgrn_bwd",
    )(proj, proj, proj, lbl, do, sprev, mcum, mrev, msum, bd, dproj, *[a for pair in pairs for a in pair[1:]])


def _tail(x, tgt, proj, attn, o, w_a, w_b, w_out, w_at, w_bt, w_outt, b_gate, g_post, gh):
    s = x.shape[0]
    tm = 256
    ones64 = (jnp.arange(HG_WIDTH)[:, None] // 64 == jnp.arange(HG_WIDTH)[None, :] // 64).astype(BF16)
    weights = (w_a, w_b, w_out, w_at, w_bt, w_outt)

    def body(x_ref, t_ref, ml_ref, ga_ref, gb_ref, at_ref, o_ref, *rest):
        w_hbm, (bg_ref, gp_ref, gh_ref, ones_ref) = rest[:6], rest[6:10]
        (dout_ref, dpj_ref, dop_ref, do_ref, mt_ref, dy_ref, yat_ref, dya_ref, ybt_ref, dyb_ref,
         loss_ref, dgp_ref, dbg_ref, dgh_ref) = rest[10:24]
        (wa_ref, wb_ref, wo_ref, wat_ref, wbt_ref, wot_ref), w_sem = rest[24:30], rest[30]

        @pl.when(pl.program_id(0) == 0)
        def _():
            loads = [pltpu.make_async_copy(src, dst, w_sem.at[k])
                     for k, (src, dst) in enumerate(zip(w_hbm, rest[24:30]))]
            _start_all(loads, [])
            loss_ref[...] = jnp.zeros_like(loss_ref)
            dgp_ref[...] = jnp.zeros_like(dgp_ref)
            dbg_ref[...] = jnp.zeros_like(dbg_ref)
            dgh_ref[...] = jnp.zeros_like(dgh_ref)
            _wait_all(loads, [])

        ones = ones_ref[...]
        gate_a = ga_ref[...]
        sa = _sigmoid(gate_a)
        silu_a = gate_a * sa
        attn_v = at_ref[...]
        ya_in = attn_v * silu_a
        ov = o_ref[...]
        ro = lax.rsqrt(_sel_right(ov * ov, ones) * (1.0 / 64.0) + EPS)
        ohat = ov * ro
        ghv = gh_ref[...]
        on = ohat * ghv
        gate_b = gb_ref[...]
        sb = _sigmoid(gate_b)
        silu_b = gate_b * sb
        yb_in = on * silu_b
        ya_bf = ya_in.astype(BF16)
        yb_bf = yb_in.astype(BF16)
        yat_ref[...] = ya_bf.T
        ybt_ref[...] = yb_bf.T
        y_a = _dot(ya_bf, wa_ref[...])
        y_b = _dot(yb_bf, wb_ref[...])
        gts = _sigmoid(ml_ref[...] + bg_ref[...])
        g_a = gts[:, :D_MODEL]
        g_b = gts[:, D_MODEL:]
        m_bf = (g_a * y_a + g_b * y_b).astype(BF16)
        mt_ref[...] = m_bf.T
        y = _dot(m_bf, wo_ref[...])
        r1 = lax.rsqrt(jnp.mean(y * y, axis=-1, keepdims=True) + EPS)
        yn = y * r1
        gp = gp_ref[...]
        e = x_ref[...] + yn * gp - t_ref[...]
        loss_ref[...] += jnp.sum(e * e, axis=0, keepdims=True)
        dout = e * (1.0 / D_MODEL)
        dout_ref[...] = dout
        dgp_ref[...] += jnp.sum(dout * yn, axis=0, keepdims=True)
        dyn = dout * gp
        dy = r1 * (dyn - yn * jnp.mean(dyn * yn, axis=-1, keepdims=True))
        dy_bf = dy.astype(BF16)
        dy_ref[...] = dy_bf
        dm = _dot(dy_bf, wot_ref[...])
        dml_a = dm * y_a * g_a * (1.0 - g_a)
        dml_b = dm * y_b * g_b * (1.0 - g_b)
        dpj_ref[:, :D_MODEL] = dml_a.astype(BF16)
        dpj_ref[:, D_MODEL:2 * D_MODEL] = dml_b.astype(BF16)
        dbg_ref[:, :D_MODEL] += jnp.sum(dml_a, axis=0, keepdims=True)
        dbg_ref[:, D_MODEL:] += jnp.sum(dml_b, axis=0, keepdims=True)
        dya_bf = (dm * g_a).astype(BF16)
        dyb_bf = (dm * g_b).astype(BF16)
        dya_ref[...] = dya_bf
        dyb_ref[...] = dyb_bf
        dya_in = _dot(dya_bf, wat_ref[...])
        dyb_in = _dot(dyb_bf, wbt_ref[...])
        dattn = dya_in * silu_a
        delta = _sel_right(dattn * attn_v, ones)
        lane = lax.broadcasted_iota(jnp.int32, (tm, LANE), 1)
        for p in range(HEADS // 2):
            sl = slice(LANE * p, LANE * (p + 1))
            xs = (dattn[:, sl], pltpu.roll(dattn[:, sl], VDIM, 1))
            nds = (-pltpu.roll(delta[:, sl], VDIM, 1), -delta[:, sl])
            for a in range(2):
                hi, lo_part = _hi_lo(nds[a])
                blk = jnp.where(lane < VDIM, xs[a], jnp.where(lane == VDIM, hi, jnp.where(lane == VDIM + 1, lo_part, 0.0)))
                dop_ref[:, LANE * (2 * p + a):LANE * (2 * p + a + 1)] = blk.astype(BF16)
        dpj_ref[:, 2 * D_MODEL:2 * D_MODEL + HG_WIDTH] = (
            dya_in * attn_v * (sa * (1.0 + gate_a * (1.0 - sa)))).astype(BF16)
        don = dyb_in * silu_b
        dpj_ref[:, 2 * D_MODEL + HG_WIDTH:] = (dyb_in * on * (sb * (1.0 + gate_b * (1.0 - sb)))).astype(BF16)
        dgh_ref[...] += jnp.sum(don * ohat, axis=0, keepdims=True)
        dohat = don * ghv
        do_ref[...] = (ro * (dohat - ohat * (_sel_right(dohat * ohat, ones) * (1.0 / 64.0)))).astype(BF16)

    row = lambda w, j: pl.BlockSpec((tm, w), lambda i: (i, j))
    col = lambda w: pl.BlockSpec((w, tm), lambda i: (0, i))
    full = lambda a: pl.BlockSpec(a.shape, lambda i: (0, 0))
    acc = lambda w: pl.BlockSpec((1, w), lambda i: (0, 0))
    sds = lambda w, dt: jax.ShapeDtypeStruct((s, w), dt)
    sdt = lambda w: jax.ShapeDtypeStruct((w, s), BF16)
    return pl.pallas_call(
        body,
        grid=(s // tm,),
        in_specs=[row(1024, 0), row(1024, 0), row(2048, 0), row(512, 4), row(512, 5), row(512, 0), row(512, 0)]
        + [ANY] * 6 + [full(b_gate), full(g_post), full(gh), full(ones64)],
        out_specs=[row(1024, 0), row(3072, 0), row(1024, 0), row(512, 0),
                   col(1024), row(1024, 0), col(512), row(1024, 0), col(512), row(1024, 0),
                   acc(1024), acc(1024), acc(2048), acc(512)],
        out_shape=[sds(1024, F32), sds(D_IN_PAD, BF16), sds(1024, BF16), sds(512, BF16),
                   sdt(1024), sds(1024, BF16), sdt(512), sds(1024, BF16), sdt(512), sds(1024, BF16),
                   jax.ShapeDtypeStruct((1, 1024), F32), jax.ShapeDtypeStruct((1, 1024), F32),
                   jax.ShapeDtypeStruct((1, 2048), F32), jax.ShapeDtypeStruct((1, 512), F32)],
        scratch_shapes=[pltpu.VMEM(a.shape, BF16) for a in weights] + [pltpu.SemaphoreType.DMA((6,))],
        compiler_params=_params(("arbitrary",), 56),
        name="tail",
    )(x, tgt, proj, proj, proj, attn, o, *weights, b_gate, g_post, gh, ones64)


def _mla_bwd(proj, dqr, dkr, dv, g_q, g_kv, w_uq_pt, w_kv_pt, rc, rs1, rs2, cqt, ckvt, dproj):
    assert HEADS == N_DEV
    s = proj.shape[0]
    tm = 512
    scale = 1.0 / math.sqrt(QK)

    def body(cq_ref, ckv_ref, dqr_ref, dkr_ref, dv_ref, gq_ref, gkv_ref, wuqt_ref, wkvt_ref, c_ref, s1_ref, s2_ref,
             cqt_ref, ckvt_ref, dproj_in, dc_ref, dgq_ref, dgkv_ref, uq_slots, ukv_slots,
             dqf_ref, dkvf_ref, dwuq_ref, dwkv_ref):
        del dproj_in

        @pl.when(pl.program_id(0) == 0)
        def _():
            dgq_ref[...] = jnp.zeros_like(dgq_ref)
            dgkv_ref[...] = jnp.zeros_like(dgkv_ref)
            dwuq_ref[...] = jnp.zeros_like(dwuq_ref)
            dwkv_ref[...] = jnp.zeros_like(dwkv_ref)

        c, s1, s2 = c_ref[...], s1_ref[...], s2_ref[...]
        lane = lax.broadcasted_iota(jnp.int32, (tm, LANE), 1)
        ksum = jnp.zeros((tm, LANE), F32)
        for h in range(HEADS):
            sl = slice(LANE * h, LANE * (h + 1))
            dqf_ref[:, sl] = (_unrope(dqr_ref[:, sl], c, s1, s2) * scale).astype(BF16)
            dkh = dkr_ref[:, sl]
            ksum = ksum + dkh
            dkvf_ref[:, sl] = jnp.where(lane < NOPE, dkh, 0.0).astype(BF16)
            dkvf_ref[:, HEADS * LANE + LANE * h:HEADS * LANE + LANE * (h + 1)] = jnp.where(
                lane < VDIM, dv_ref[:, sl], 0.0).astype(BF16)
        dkpe = _unrope(ksum, c, s1, s2)
        dc_ref[:, Q_LORA + KV_LORA:] = jnp.where((lane >= NOPE) & (lane < QK), dkpe, 0.0).astype(BF16)
        dqf, dkvf = dqf_ref[...], dkvf_ref[...]
        dwuq_ref[...] += _dot(cqt_ref[...], dqf)
        dwkv_ref[...] += _dot(ckvt_ref[...], dkvf)
        dcqn = _dot(dqf, wuqt_ref[...])
        dckvn = _dot(dkvf, wkvt_ref[...])
        for x_ref, g_ref, dn, cols, dg_ref in ((cq_ref, gq_ref, dcqn, slice(0, Q_LORA), dgq_ref),
                                               (ckv_ref, gkv_ref, dckvn, slice(Q_LORA, Q_LORA + KV_LORA), dgkv_ref)):
            xv = x_ref[...]
            r = lax.rsqrt(jnp.mean(xv * xv, axis=-1, keepdims=True) + EPS)
            xh = xv * r
            dg_ref[...] += jnp.sum(dn * xh, axis=0, keepdims=True)
            dh = dn * g_ref[...]
            dc_ref[:, cols] = (r * (dh - xh * jnp.mean(dh * xh, axis=-1, keepdims=True))).astype(BF16)

        @pl.when(pl.program_id(0) == s // tm - 1)
        def _():
            ur = Q_LORA // N_DEV
            for p in range(N_DEV):
                uq_slots[p] = jnp.concatenate(
                    [dwuq_ref[ur * p:ur * (p + 1), LANE * h:LANE * h + QK] for h in range(HEADS)], axis=1).astype(BF16)
                ukv_slots[p] = jnp.concatenate(
                    [dwkv_ref[:, LANE * p:LANE * p + NOPE],
                     dwkv_ref[:, LANE * (HEADS + p):LANE * (HEADS + p) + VDIM]], axis=1).astype(BF16)

    row = lambda w, j: pl.BlockSpec((tm, w), lambda i: (i, j))
    full = lambda a: pl.BlockSpec(a.shape, lambda i: (0, 0))
    acc = lambda w: pl.BlockSpec((1, w), lambda i: (0, 0))
    col = lambda w: pl.BlockSpec((w, tm), lambda i: (0, i))
    whole = lambda shape: pl.BlockSpec(shape, lambda i: (0, 0, 0))
    uq_shape = (N_DEV, Q_LORA // N_DEV, HEADS * QK)
    ukv_shape = (N_DEV, KV_LORA, NOPE + VDIM)
    return pl.pallas_call(
        body,
        grid=(s // tm,),
        in_specs=[row(768, 6), row(256, 21), row(1024, 0), row(1024, 0), row(1024, 0), full(g_q), full(g_kv),
                  full(w_uq_pt), full(w_kv_pt), row(128, 0), row(128, 0), row(128, 0), col(Q_LORA), col(KV_LORA),
                  pl.BlockSpec(memory_space=pl.ANY)],
        out_specs=[row(1152, 4), acc(768), acc(256), whole(uq_shape), whole(ukv_shape)],
        out_shape=[jax.ShapeDtypeStruct(dproj.shape, BF16),
                   jax.ShapeDtypeStruct((1, 768), F32), jax.ShapeDtypeStruct((1, 256), F32),
                   jax.ShapeDtypeStruct(uq_shape, BF16), jax.ShapeDtypeStruct(ukv_shape, BF16)],
        input_output_aliases={14: 0},
        scratch_shapes=[pltpu.VMEM((tm, HEADS * LANE), BF16), pltpu.VMEM((tm, 2 * HEADS * LANE), BF16),
                        pltpu.VMEM((Q_LORA, HEADS * LANE), F32), pltpu.VMEM((KV_LORA, 2 * HEADS * LANE), F32)],
        compiler_params=_params(("arbitrary",)),
        name="mla_bwd",
    )(proj, proj, dqr, dkr, dv, g_q, g_kv, w_uq_pt, w_kv_pt, rc, rs1, rs2, cqt, ckvt, dproj)


def _dh_dx(dproj, w_in_pt, x, dout, g_pre, sends):
    s, k = dproj.shape
    tm = 256
    ns, ni = len(sends), s // tm

    def body(dp_ref, w_ref, x_ref, dout_ref, g_ref, *rest):
        send_refs, (dx_ref, dg_ref) = rest[:ns], rest[ns:ns + 2]
        recv_refs, sems = rest[ns + 2:2 * ns + 2], rest[2 * ns + 2:]

        @pl.when(pl.program_id(0) == 0)
        def _():
            _start_all(*_to_chips_copies(send_refs, recv_refs, sems))
            dg_ref[...] = jnp.zeros_like(dg_ref)

        dh = _dot(dp_ref[...], w_ref[...])
        xv = x_ref[...]
        r = lax.rsqrt(jnp.mean(xv * xv, axis=-1, keepdims=True) + EPS)
        xh = xv * r
        dg_ref[...] += jnp.sum(dh * xh, axis=0, keepdims=True)
        dxh = dh * g_ref[...]
        dx_ref[...] = dout_ref[...] + r * (dxh - xh * jnp.mean(dxh * xh, axis=-1, keepdims=True))

        @pl.when(pl.program_id(0) == ni - 1)
        def _():
            _wait_all(*_to_chips_copies(send_refs, recv_refs, sems))

    row = lambda w: pl.BlockSpec((tm, w), lambda i: (i, 0))
    return pl.pallas_call(
        body,
        grid=(ni,),
        in_specs=[row(k), pl.BlockSpec((k, D_MODEL), lambda i: (0, 0)), row(D_MODEL), row(D_MODEL),
                  pl.BlockSpec((1, D_MODEL), lambda i: (0, 0))] + [ANY] * ns,
        out_specs=[row(D_MODEL), pl.BlockSpec((1, D_MODEL), lambda i: (0, 0))] + [ANY] * ns,
        out_shape=[jax.ShapeDtypeStruct((s, D_MODEL), F32), jax.ShapeDtypeStruct((1, D_MODEL), F32)]
        + [jax.ShapeDtypeStruct(a.shape, a.dtype) for a in sends],
        scratch_shapes=_copy_sems(ns, 3),
        compiler_params=_params(("arbitrary",)),
        name="dh_dx",
    )(dproj, w_in_pt, x, dout, g_pre, *sends)


def _pair_reduce(slots):
    n = len(slots)
    half = [(N_DEV // 2,) + a.shape[1:] for a in slots]

    def body(*refs):
        s_refs, o_refs = refs[:n], refs[n:2 * n]
        mine, got = refs[2 * n:3 * n], refs[3 * n:4 * n]
        send_sems, recv_sems, local_sems, out_sems = refs[4 * n:]
        x, y, c = _my_place()
        copies, loads, stores = [], [], []
        for q in range(N_DEV // 2):
            for a in range(n):
                copies.append(pltpu.make_async_remote_copy(
                    src_ref=s_refs[a].at[2 * q + 1 - c], dst_ref=got[a].at[q],
                    send_sem=send_sems.at[4 * a + q], recv_sem=recv_sems.at[4 * a + q],
                    device_id=(x, y, 1 - c), device_id_type=MESH_ID))
                loads.append(pltpu.make_async_copy(s_refs[a].at[2 * q + c], mine[a].at[q], local_sems.at[4 * a + q]))
                stores.append(pltpu.make_async_copy(mine[a].at[q], o_refs[a].at[q], out_sems.at[4 * a + q]))
        _start_all(loads, copies)
        k = 0
        for q in range(N_DEV // 2):
            for a in range(n):
                loads[k].wait()
                copies[k].wait_recv()
                mine[a][q] = (mine[a][q].astype(F32) + got[a][q].astype(F32)).astype(mine[a].dtype)
                stores[k].start()
                k += 1
        for cp in copies:
            cp.wait_send()
        for cp in stores:
            cp.wait()

    vm = lambda: [pltpu.VMEM(h, a.dtype) for h, a in zip(half, slots)]
    return pl.pallas_call(
        body,
        in_specs=[ANY] * n,
        out_specs=[ANY] * n,
        out_shape=[jax.ShapeDtypeStruct(h, a.dtype) for h, a in zip(half, slots)],
        scratch_shapes=vm() + vm() + [pltpu.SemaphoreType.DMA((4 * n,)), pltpu.SemaphoreType.DMA((4 * n,)),
                                      pltpu.SemaphoreType.DMA((4 * n,)), pltpu.SemaphoreType.DMA((4 * n,))],
        compiler_params=pltpu.CompilerParams(vmem_limit_bytes=48 * 2**20),
        name="pair_reduce",
    )(*slots)


def _rope_tables(s):
    inv = (np.float32(ROPE_THETA) ** (-np.arange(0, ROPE, 2, dtype=np.float32) / np.float32(ROPE))).astype(np.float32)
    ang = (np.arange(s, dtype=np.float32)[:, None] * inv[None, :]).astype(np.float32)
    cos, sin = jnp.asarray(np.cos(ang.astype(np.float64)), F32), jnp.asarray(np.sin(ang.astype(np.float64)), F32)
    z = lambda w: jnp.zeros((s, w), F32)
    rc = jnp.concatenate([jnp.ones((s, NOPE), F32), cos, cos, z(32)], axis=1)
    rs1 = jnp.concatenate([z(NOPE), -sin, z(16), z(32)], axis=1)
    rs2 = jnp.concatenate([z(NOPE), z(16), sin, z(32)], axis=1)
    return rc, rs1, rs2


def _step(x, tgt, w_blk, shards, g_pre, b_gate, g_q, g_kv, lbl, g_hgrn, g_post):
    s = x.shape[0]
    rc, rs1, rs2 = _rope_tables(s)
    gh = jnp.tile(g_hgrn, (1, HEADS))

    proj, w_in_pt, ht, *got = _gather_proj(x, g_pre, w_blk, shards[:2])
    w_uq, w_ukv = (_from_slots(n, g) for n, g in zip(MATS[:2], got))
    w_uq_p = jnp.pad(w_uq.reshape(Q_LORA, HEADS, QK), ((0, 0), (0, 0), (0, LANE - QK))).reshape(Q_LORA, HEADS * LANE)
    kv3 = w_ukv.reshape(KV_LORA, HEADS, NOPE + VDIM)
    pad64 = lambda t: jnp.pad(t, ((0, 0), (0, 0), (0, LANE - 64))).reshape(KV_LORA, HEADS * LANE)
    w_kv_p = jnp.concatenate([pad64(kv3[:, :, :NOPE]), pad64(kv3[:, :, NOPE:])], axis=1)

    qr, kr, v, cqt, ckvt = _mla_prep(proj, g_q, g_kv, w_uq_p, w_kv_p, rc, rs1, rs2)
    attn, qa, *got = _attn_fwd(qr, kr, v, shards[2:])
    w_a, w_b, w_out = (_from_slots(n, g) for n, g in zip(MATS[2:], got))
    o, sprev = _hgrn_fwd(proj, lbl)
    (dout, dproj, dop, do, mt, dy_bf, yat, dya_bf, ybt, dyb_bf,
     loss_vec, dg_post, db_gate, dgh) = _tail(x, tgt, proj, attn, o, w_a, w_b, w_out, w_a.T, w_b.T, w_out.T,
                                               b_gate, g_post, gh)
    dproj, dlbl, *early = _hgrn_bwd(proj, lbl, do, sprev, dproj,
                                    [("w_branch_a", yat, dya_bf), ("w_branch_b", ybt, dyb_bf), ("w_out", mt, dy_bf)])
    dqr, dkr, dv, *early_recv = _attn_bwd(qa, kr, v, dop, early)
    dproj, dg_q, dg_kv, dw_uq_slots, dw_ukv_slots = _mla_bwd(proj, dqr, dkr, dv, g_q, g_kv, w_uq_p.T, w_kv_p.T,
                                                             rc, rs1, rs2, cqt, ckvt, dproj)

    dw_in_slots = _dw_in_slots(ht, dproj)
    late = _pair_reduce([dw_in_slots, dw_uq_slots, dw_ukv_slots])
    dx, dg_pre, *late_recv = _dh_dx(dproj, w_in_pt, x, dout, g_pre, late)

    g_sum = _vectors_sum(dg_pre, db_gate, dg_q, dg_kv, dlbl, dgh, dg_post, loss_vec)
    return dx, late_recv[0], dict(zip(MATS, late_recv[1:] + early_recv)), g_sum


def _adamw(g, w, m, v):
    c1 = 1.0 / (1.0 - ADAM_B1 ** ADAM_STEP)
    c2 = 1.0 / (1.0 - ADAM_B2 ** ADAM_STEP)
    nm = ADAM_B1 * m + (1.0 - ADAM_B1) * g
    nv = ADAM_B2 * v + (1.0 - ADAM_B2) * (g * g)
    d = -ADAM_LR * ((nm * c1) / (jnp.sqrt(nv * c2) + ADAM_EPS) + ADAM_WD * w)
    return d, nm, nv


def _sum8(r_ref):
    g = r_ref[0].astype(F32)
    for k in range(1, r_ref.shape[0]):
        g = g + r_ref[k].astype(F32)
    return g


def _sum_adamw_w_in(recv, w, m, v):
    rows, _, cols = w.shape
    tc = 256
    nc = cols // tc

    def body(r_ref, w_hbm, m_hbm, v_hbm, g_hbm, d_hbm, nm_hbm, nv_hbm, ins, outs, in_sems, out_sems):
        i = pl.program_id(0)
        slot = i & 1
        cols_of = lambda step: pl.ds(pl.multiple_of(step * tc, tc), tc)

        def load(k, step, sl):
            return pltpu.make_async_copy((w_hbm, m_hbm, v_hbm)[k].at[:, 0, cols_of(step)], ins.at[sl, k],
                                         in_sems.at[sl, k])

        def store(k, step, sl):
            return pltpu.make_async_copy(outs.at[sl, k], (g_hbm, d_hbm, nm_hbm, nv_hbm)[k].at[:, 0, cols_of(step)],
                                         out_sems.at[sl, k])

        @pl.when(i == 0)
        def _():
            for k in range(3):
                load(k, 0, 0).start()

        @pl.when(i + 1 < nc)
        def _():
            for k in range(3):
                load(k, i + 1, 1 - slot).start()

        @pl.when(i >= 2)
        def _():
            for k in range(4):
                store(k, i - 2, slot).wait()

        for k in range(3):
            load(k, i, slot).wait()
        g = _sum8(r_ref)
        d, nm, nv = _adamw(g, ins[slot, 0], ins[slot, 1], ins[slot, 2])
        for k, val in enumerate((g, d, nm, nv)):
            outs[slot, k] = val
        for k in range(4):
            store(k, i, slot).start()

        @pl.when(i == nc - 1)
        def _():
            for k in range(4):
                store(k, i, slot).wait()
            if nc >= 2:
                for k in range(4):
                    store(k, i - 1, 1 - slot).wait()

    out = jax.ShapeDtypeStruct((rows, 1, cols), F32)
    return pl.pallas_call(
        body,
        grid=(nc,),
        in_specs=[pl.BlockSpec((recv.shape[0], rows, tc), lambda i: (0, 0, i)), ANY, ANY, ANY],
        out_specs=[ANY, ANY, ANY, ANY],
        out_shape=[out, out, out, out],
        scratch_shapes=[pltpu.VMEM((2, 3, rows, tc), F32), pltpu.VMEM((2, 4, rows, tc), F32),
                        pltpu.SemaphoreType.DMA((2, 3)), pltpu.SemaphoreType.DMA((2, 4))],
        compiler_params=_params(("arbitrary",)),
        name="sum_adamw_w_in",
    )(recv, w, m, v)


def _sum_adamw_whole(recvs, ws, ms, vs):
    n = len(ws)

    def body(*refs):
        r_refs, w_refs, m_refs, v_refs = refs[:n], refs[n:2 * n], refs[2 * n:3 * n], refs[3 * n:4 * n]
        outs = refs[4 * n:]
        for a in range(n):
            g = _sum8(r_refs[a])
            d, nm, nv = _adamw(g, w_refs[a][...], m_refs[a][...], v_refs[a][...])
            outs[a][...] = g
            outs[n + a][...] = d
            outs[2 * n + a][...] = nm
            outs[3 * n + a][...] = nv

    shapes = [jax.ShapeDtypeStruct(w.shape, F32) for w in ws]
    res = pl.pallas_call(
        body,
        out_shape=shapes * 4,
        compiler_params=pltpu.CompilerParams(vmem_limit_bytes=48 * 2**20),
        name="sum_adamw_mats",
    )(*recvs, *ws, *ms, *vs)
    return res[:n], res[n:2 * n], res[2 * n:3 * n], res[3 * n:]


SMALL = ("g_pre", "b_gate", "g_q", "g_kv", "lb_logits", "g_hgrn", "g_post")
SMALL_SHAPE = dict(g_pre=(1, 1024), b_gate=(1, 2048), g_q=(1, 768), g_kv=(1, 256), lb_logits=(2, 512),
                   g_hgrn=(1, 64), g_post=(1, 1024))


def _vectors_sum(dg_pre, db_gate, dg_q, dg_kv, dlbl, dgh, dg_post, loss_vec):
    def body(gpre_ref, bg_ref, gq_ref, gkv_ref, lbl_ref, gh_ref, gpost_ref, loss_ref, out_ref, mine, got,
             send_sems, recv_sems):
        mine[...] = jnp.zeros_like(mine)
        mine[0:1, :] = gpre_ref[...]
        mine[1:2, :] = bg_ref[:, :1024]
        mine[2:3, :] = bg_ref[:, 1024:]
        mine[3:4, :Q_LORA] = gq_ref[...]
        mine[4:5, :KV_LORA] = gkv_ref[...]
        loss = (0.5 / D_MODEL) * jnp.sum(loss_ref[...], axis=-1, keepdims=True)
        mine[4:5, KV_LORA:] = jnp.broadcast_to(loss, (1, 1024 - KV_LORA))
        mine[5:6, :HG_WIDTH] = lbl_ref[0:1, :]
        mine[5:6, HG_WIDTH:] = lbl_ref[1:2, :]
        gh = gh_ref[...]
        fold = gh[:, :VDIM]
        for h in range(1, HEADS):
            fold = fold + gh[:, VDIM * h:VDIM * (h + 1)]
        mine[6:7, :VDIM] = fold
        mine[7:8, :] = gpost_ref[...]
        x, y, c = _my_place()
        me = 4 * x + 2 * y + c
        got[me] = mine[...]
        copies = [pltpu.make_async_remote_copy(
            src_ref=mine, dst_ref=got.at[me], send_sem=send_sems.at[k], recv_sem=recv_sems.at[k],
            device_id=_flip(k, x, y, c), device_id_type=MESH_ID) for k in range(N_DEV - 1)]
        _start_all([], copies)
        _wait_all([], copies)
        out_ref[...] = _sum8(got)

    return pl.pallas_call(
        body,
        out_shape=jax.ShapeDtypeStruct((8, 1024), F32),
        scratch_shapes=[pltpu.VMEM((8, 1024), F32), pltpu.VMEM((N_DEV, 8, 1024), F32),
                        pltpu.SemaphoreType.DMA((7,)), pltpu.SemaphoreType.DMA((7,))],
        name="vectors_sum",
    )(dg_pre, db_gate, dg_q, dg_kv, dlbl, dgh, dg_post, loss_vec)


def _vectors_adamw(g_sum, ws, ms, vs):
    n = len(SMALL)

    def body(g_ref, *refs):
        w_refs, m_refs, v_refs = refs[:n], refs[n:2 * n], refs[2 * n:3 * n]
        loss_ref, outs = refs[3 * n], refs[3 * n + 1:]
        g = g_ref[...]
        loss_ref[...] = g[4:5, KV_LORA:KV_LORA + 1]
        grads = (g[0:1, :], jnp.concatenate([g[1:2, :], g[2:3, :]], axis=1), g[3:4, :Q_LORA], g[4:5, :KV_LORA],
                 jnp.concatenate([g[5:6, :HG_WIDTH], g[5:6, HG_WIDTH:]], axis=0), g[6:7, :VDIM], g[7:8, :])
        for a in range(n):
            d, nm, nv = _adamw(grads[a], w_refs[a][...], m_refs[a][...], v_refs[a][...])
            outs[a][...] = grads[a]
            outs[n + a][...] = d
            outs[2 * n + a][...] = nm
            outs[3 * n + a][...] = nv

    shapes = [jax.ShapeDtypeStruct(SMALL_SHAPE[k], F32) for k in SMALL]
    res = pl.pallas_call(
        body,
        out_shape=[jax.ShapeDtypeStruct((1, 1), F32)] + shapes * 4,
        name="vectors_adamw",
    )(g_sum, *ws, *ms, *vs)
    return res[0], res[1:n + 1], res[n + 1:2 * n + 1], res[2 * n + 1:3 * n + 1], res[3 * n + 1:]


MATS = ("w_uq", "w_ukv", "w_branch_a", "w_branch_b", "w_out")
COL_SHARDED = dict(w_uq=False, w_ukv=True, w_branch_a=True, w_branch_b=True, w_out=False)
ORDER = ("g_pre", "w_in", "b_gate", "g_q", "w_uq", "g_kv", "w_ukv", "lb_logits", "g_hgrn",
         "w_branch_a", "w_branch_b", "w_out", "g_post")


def _from_slots(name, slots):
    _, r, c = slots.shape
    if COL_SHARDED[name]:
        return slots.transpose(1, 0, 2).reshape(r, N_DEV * c)
    return slots.reshape(N_DEV * r, c)


def kernel(x, g_pre, w_in, b_gate, g_q, w_uq, g_kv, w_ukv, lb_logits, g_hgrn, w_branch_a, w_branch_b, w_out, g_post, loss_target, m_g_pre, m_w_in, m_b_gate, m_g_q, m_w_uq, m_g_kv, m_w_ukv, m_lb_logits, m_g_hgrn, m_w_branch_a, m_w_branch_b, m_w_out, m_g_post, v_g_pre, v_w_in, v_b_gate, v_g_q, v_w_uq, v_g_kv, v_w_ukv, v_lb_logits, v_g_hgrn, v_w_branch_a, v_w_branch_b, v_w_out, v_g_post):
    rows3 = lambda a: jnp.transpose(a, (2, 0, 1))
    w = dict(w_in=rows3(w_in), w_uq=w_uq[0], w_ukv=w_ukv[0], w_branch_a=w_branch_a[0], w_branch_b=w_branch_b[0],
             w_out=w_out[0], g_pre=g_pre, b_gate=b_gate, g_q=g_q, g_kv=g_kv, lb_logits=lb_logits, g_hgrn=g_hgrn,
             g_post=g_post)
    mom = dict(w_in=rows3(m_w_in), w_uq=m_w_uq[0], w_ukv=m_w_ukv[0], w_branch_a=m_w_branch_a[0],
               w_branch_b=m_w_branch_b[0], w_out=m_w_out[0], g_pre=m_g_pre, b_gate=m_b_gate, g_q=m_g_q, g_kv=m_g_kv,
               lb_logits=m_lb_logits, g_hgrn=m_g_hgrn, g_post=m_g_post)
    var = dict(w_in=rows3(v_w_in), w_uq=v_w_uq[0], w_ukv=v_w_ukv[0], w_branch_a=v_w_branch_a[0],
               w_branch_b=v_w_branch_b[0], w_out=v_w_out[0], g_pre=v_g_pre, b_gate=v_b_gate, g_q=v_g_q, g_kv=v_g_kv,
               lb_logits=v_lb_logits, g_hgrn=v_g_hgrn, g_post=v_g_post)

    w_blk = w["w_in"].reshape(W_IN_SHARD, D_MODEL).astype(BF16)
    dx, recv_in, recv, g_sum = _step(x[0], loss_target[0], w_blk, [w[n].astype(BF16) for n in MATS],
                                     g_pre, b_gate, g_q, g_kv, lb_logits, g_hgrn, g_post)

    g_in, d_in, m_in, v_in = _sum_adamw_w_in(recv_in, w["w_in"], mom["w_in"], var["w_in"])
    res = _sum_adamw_whole([recv[n] for n in MATS], *([t[n] for n in MATS] for t in (w, mom, var)))
    total, *vec = _vectors_adamw(g_sum, *([t[n] for n in SMALL] for t in (w, mom, var)))

    outs = []
    for mats, vecs, big in zip(res, vec, (g_in, d_in, m_in, v_in)):
        t = {**{n: a[None] for n, a in zip(MATS, mats)}, **dict(zip(SMALL, vecs)),
             "w_in": jnp.transpose(big, (1, 2, 0))}
        outs += [t[n] for n in ORDER]
    return (total.reshape(()), dx[None], *outs)
```

```python
import math

import jax
import jax.numpy as jnp
import numpy as np
from jax import lax
from jax.experimental import pallas as pl
from jax.experimental.pallas import tpu as pltpu

F32, BF16 = jnp.float32, jnp.bfloat16

D_MODEL = 1024
EPS = 1e-6
HEADS = 8
NOPE, ROPE, VDIM = 64, 32, 64
QK = NOPE + ROPE
Q_LORA, KV_LORA = 768, 256
ROPE_THETA = 10000.0
ATT_CHUNK_SHIFT = 6
HG_BLOCK = 32
HG_WIDTH = 512
D_IN = 5664
D_IN_PAD = 5760
W_IN_SHARD = D_IN // 8
N_DEV = 8
LANE = 128

ADAM_LR, ADAM_B1, ADAM_B2, ADAM_EPS, ADAM_WD, ADAM_STEP = 0.001, 0.9, 0.999, 1e-08, 0.01, 10

W_IN_SEGMENTS = ((3616, 5664, 0), (1056, 1568, 2048), (3104, 3616, 2560), (1568, 3104, 3072),
                 (0, 1024, 4608), (1024, 1056, 5696))

NT = (((1,), (1,)), ((), ()))
TN = (((0,), (0,)), ((), ()))
MESH_ID = pl.DeviceIdType.MESH


def _w_in_pieces():
    out = []
    for lo, hi, dst in W_IN_SEGMENTS:
        c = lo
        while c < hi:
            p = c // W_IN_SHARD
            e = min(hi, (p + 1) * W_IN_SHARD)
            out.append((p, c - p * W_IN_SHARD, e - p * W_IN_SHARD, dst + c - lo))
            c = e
    return out


def _params(sem, vmem_mb=48):
    return pltpu.CompilerParams(dimension_semantics=sem, vmem_limit_bytes=vmem_mb * 2**20)


def _dot(a, b):
    return jnp.dot(a, b, preferred_element_type=F32)


def _dotg(a, b, dims):
    return lax.dot_general(a, b, dims, preferred_element_type=F32)


def _split2(x):
    hi = x.astype(BF16)
    return hi, (x - hi.astype(F32)).astype(BF16)


def _sel_left(m01, x):
    hi, lo = _split2(x)
    return _dot(m01, hi) + _dot(m01, lo)


def _sel_right(x, m01):
    hi, lo = _split2(x)
    return _dot(hi, m01) + _dot(lo, m01)


def _hi_lo(x):
    hi = x.astype(BF16).astype(F32)
    return hi, x - hi


def _sigmoid(x):
    return 0.5 * jnp.tanh(0.5 * x) + 0.5


def _rope(x, c, s1, s2):
    return x * c + pltpu.roll(x, 112, 1) * s1 + pltpu.roll(x, 16, 1) * s2


def _unrope(d, c, s1, s2):
    return d * c + pltpu.roll(d * s1, 16, 1) + pltpu.roll(d * s2, 112, 1)


def _my_place():
    return lax.axis_index("x"), lax.axis_index("y"), lax.axis_index("c")


def _flip(k, x, y, c):
    fx, fy, fc = (k + 1) >> 2 & 1, (k + 1) >> 1 & 1, (k + 1) & 1
    return (1 - x if fx else x), (1 - y if fy else y), (1 - c if fc else c)


def _to_all_copies(s_refs, r_refs, sems, spread):
    send_sems, recv_sems, local_sems = sems
    x, y, c = _my_place()
    me = 4 * x + 2 * y + c
    src = (lambda a, p: s_refs[a]) if spread else (lambda a, p: s_refs[a].at[p])
    local = [pltpu.make_async_copy(src(a, me), r_refs[a].at[me], local_sems.at[a]) for a in range(len(s_refs))]
    remote = []
    for k in range(N_DEV - 1):
        px, py, pc = _flip(k, x, y, c)
        for a in range(len(s_refs)):
            remote.append(pltpu.make_async_remote_copy(
                src_ref=src(a, 4 * px + 2 * py + pc), dst_ref=r_refs[a].at[me],
                send_sem=send_sems.at[7 * a + k], recv_sem=recv_sems.at[7 * a + k],
                device_id=(px, py, pc), device_id_type=MESH_ID))
    return local, remote


def _to_chips_copies(s_refs, r_refs, sems):
    send_sems, recv_sems, local_sems = sems
    x, y, c = _my_place()
    me = 2 * x + y
    local = [pltpu.make_async_copy(s_refs[a].at[me], r_refs[a].at[me], local_sems.at[a]) for a in range(len(s_refs))]
    remote = []
    for k in range(3):
        px = 1 - x if (k + 1) >> 1 & 1 else x
        py = 1 - y if (k + 1) & 1 else y
        for a in range(len(s_refs)):
            remote.append(pltpu.make_async_remote_copy(
                src_ref=s_refs[a].at[2 * px + py], dst_ref=r_refs[a].at[me],
                send_sem=send_sems.at[3 * a + k], recv_sem=recv_sems.at[3 * a + k],
                device_id=(px, py, c), device_id_type=MESH_ID))
    return local, remote


def _start_all(local, remote):
    for cp in local + remote:
        cp.start()


def _wait_all(local, remote):
    for cp in remote:
        cp.wait_recv()
    for cp in remote:
        cp.wait_send()
    for cp in local:
        cp.wait()


def _copy_sems(n, peers):
    return [pltpu.SemaphoreType.DMA((peers * n,)), pltpu.SemaphoreType.DMA((peers * n,)),
            pltpu.SemaphoreType.DMA((n,))]


ANY = pl.BlockSpec(memory_space=pl.ANY)


def _dw_in_slots(ht, dproj):
    m, k = ht.shape
    n = dproj.shape[1]
    tn, tk = 1920, 2048
    nj, nk = n // tn, k // tk
    by_tile = [[] for _ in range(nj)]
    for p, lo, hi, dst in _w_in_pieces():
        while lo < hi:
            j = dst // tn
            cnt = min(hi - lo, (j + 1) * tn - dst)
            by_tile[j].append((p, lo, lo + cnt, dst - j * tn))
            lo, dst = lo + cnt, dst + cnt

    def body(a_ref, b_ref, s_ref, acc_ref):
        j, l = pl.program_id(0), pl.program_id(1)

        @pl.when(l == 0)
        def _():
            acc_ref[...] = jnp.zeros_like(acc_ref)

        acc_ref[...] += _dot(a_ref[...], b_ref[...])

        @pl.when(l == nk - 1)
        def _():
            at = acc_ref[...].T
            for jj in range(nj):
                @pl.when(j == jj)
                def _(jj=jj):
                    for p, lo, hi, d in by_tile[jj]:
                        s_ref[p, lo:hi, :] = at[d:d + hi - lo, :].astype(BF16)

    return pl.pallas_call(
        body,
        grid=(nj, nk),
        in_specs=[pl.BlockSpec((m, tk), lambda j, l: (0, l)), pl.BlockSpec((tk, tn), lambda j, l: (l, j))],
        out_specs=pl.BlockSpec((N_DEV, W_IN_SHARD, m), lambda j, l: (0, 0, 0), pipeline_mode=pl.Buffered(1)),
        out_shape=jax.ShapeDtypeStruct((N_DEV, W_IN_SHARD, m), BF16),
        scratch_shapes=[pltpu.VMEM((m, tn), F32)],
        compiler_params=_params(("arbitrary", "arbitrary"), 56),
        name="dw_in",
    )(ht, dproj)


GP_TN = 256
GP_COLS = 5888
GP_NT = GP_COLS // GP_TN


def _gp_tile_pieces():
    tiles = [[] for _ in range(GP_NT)]
    for p, lo, hi, dst in _w_in_pieces():
        while lo < hi:
            t = dst // GP_TN
            n = min(hi - lo, (t + 1) * GP_TN - dst)
            tiles[t].append((p, lo, lo + n, dst - t * GP_TN))
            lo, dst = lo + n, dst + n
    return tiles


def _gp_tables():
    pieces = _gp_tile_pieces()
    rank_of = {None: 0, 0: 1, 1: 2, 2: 2, 4: 3, 5: 3, 3: 4, 6: 5}
    order = np.zeros((N_DEV, GP_NT), np.int32)
    waits = np.zeros((N_DEV, GP_NT), np.int32)
    for me in range(N_DEV):
        x, y, c = me >> 2 & 1, me >> 1 & 1, me & 1
        chips = [(1 - x, y), (x, 1 - y), (1 - x, 1 - y)]

        def sem_of(p):
            px, py, pc = p >> 2 & 1, p >> 1 & 1, p & 1
            if (px, py) == (x, y):
                return None if pc == c else 0
            j = chips.index((px, py))
            return 1 + j if pc == c else 4 + j

        needs = [sorted({sem_of(p) for p, _, _, _ in tile} - {None}) for tile in pieces]
        ranks = [max([rank_of[k] for k in ks], default=0) for ks in needs]
        seq = sorted(range(GP_NT), key=lambda t: (ranks[t], t))
        seen = set()
        for step, t in enumerate(seq):
            order[me, step] = t
            new = [k for k in needs[t] if k not in seen]
            for k in new:
                waits[me, step] |= 1 << k
            seen.update(new)
        assert seen == set(range(7)), (me, seen)
    return order, waits


def _gather_proj(x, g_pre, w_blk, shards):
    s = x.shape[0]
    tx = 512
    ns = len(shards)
    tile_pieces = _gp_tile_pieces()
    order_np, waits_np = _gp_tables()
    xq, yq, cq = _my_place()
    me_out = 4 * xq + 2 * yq + cq
    order = lax.dynamic_index_in_dim(jnp.asarray(order_np), me_out, 0, keepdims=False)
    waits = lax.dynamic_index_in_dim(jnp.asarray(waits_np), me_out, 0, keepdims=False)

    def body(order_ref, waits_ref, x_hbm, g_ref, wblk_hbm, *rest):
        shard_refs, (proj_ref, wt_ref, ht_hbm), got_refs = rest[:ns], rest[ns:ns + 3], rest[ns + 3:2 * ns + 3]
        recv, h_ref, wtile, xbuf, htbuf = rest[2 * ns + 3:2 * ns + 8]
        send_sems, recv_sems, misc_sems = rest[2 * ns + 8:2 * ns + 11]
        sems = rest[2 * ns + 11:]
        t = pl.program_id(0)
        x_, y_, c = _my_place()
        sibling = (x_, y_, 1 - c)
        chips = [(1 - x_, y_), (x_, 1 - y_), (1 - x_, 1 - y_)]
        idx = lambda px, py, pc: 4 * px + 2 * py + pc
        me = idx(x_, y_, c)

        def copy(k, slot, to, src=None):
            return pltpu.make_async_remote_copy(
                src_ref=recv.at[slot] if src is None else src, dst_ref=recv.at[slot],
                send_sem=send_sems.at[k], recv_sem=recv_sems.at[k], device_id=to, device_id_type=MESH_ID)

        mine = pltpu.make_async_copy(wblk_hbm, recv.at[me], misc_sems.at[0])
        first = [copy(0, me, sibling, src=wblk_hbm)] + [copy(1 + j, me, (*chips[j], c), src=wblk_hbm) for j in range(2)]
        passed = [copy(4 + j, idx(*ch, c), sibling) for j, ch in enumerate(chips)]
        onward = [copy(3, idx(*chips[0], c), (*chips[1], c)), copy(3, idx(*chips[1], c), (*chips[0], c))]
        arrivals = ([copy(0, idx(x_, y_, 1 - c), sibling)] + [copy(1 + j, idx(*ch, c), sibling) for j, ch in enumerate(chips)]
                    + [copy(4 + j, idx(*ch, 1 - c), sibling) for j, ch in enumerate(chips)])

        @pl.when(t == 0)
        def _():
            mine.start()
            for cp in first:
                cp.start()
            _start_all(*_to_all_copies(shard_refs, got_refs, sems, True))

            def load(i):
                return pltpu.make_async_copy(x_hbm.at[pl.ds(i * tx, tx), :], xbuf.at[i & 1], misc_sems.at[1 + (i & 1)])

            def store(i):
                return pltpu.make_async_copy(htbuf.at[i & 1], ht_hbm.at[:, pl.ds(i * tx, tx)], misc_sems.at[3 + (i & 1)])

            load(0).start()
            for i in range(s // tx):
                if i + 1 < s // tx:
                    load(i + 1).start()
                load(i).wait()
                xv = xbuf[i & 1]
                r = lax.rsqrt(jnp.mean(xv * xv, axis=-1, keepdims=True) + EPS)
                h = (xv * r * g_ref[...]).astype(BF16)
                h_ref[i * tx:(i + 1) * tx, :] = h
                if i >= 2:
                    store(i - 2).wait()
                htbuf[i & 1] = h.T
                store(i).start()
            for i in range(max(s // tx - 2, 0), s // tx):
                store(i).wait()
            mine.wait()

        w = waits_ref[t]
        for k in range(7):
            @pl.when((w >> k) & 1 == 1)
            def _(k=k):
                arrivals[k].wait_recv()
                if 1 <= k <= 3:
                    passed[k - 1].start()
                if 1 <= k <= 2:
                    @pl.when(c == k - 1)
                    def _():
                        onward[k - 1].start()

        tile = order_ref[t]
        for tt in range(GP_NT):
            @pl.when(tile == tt)
            def _(tt=tt):
                covered = sorted((d, d + hi - lo) for _, lo, hi, d in tile_pieces[tt])
                at = 0
                for lo_z, hi_z in covered + [(GP_TN, GP_TN)]:
                    if lo_z > at:
                        wtile[at:lo_z, :] = jnp.zeros((lo_z - at, D_MODEL), BF16)
                    at = max(at, hi_z)
                for p, lo, hi, d in tile_pieces[tt]:
                    wtile[d:d + hi - lo, :] = recv[p, lo:hi, :]

        wt = wtile[...]
        wt_ref[...] = wt
        proj_ref[...] = _dotg(h_ref[...], wt, NT)

        @pl.when(t == GP_NT - 1)
        def _():
            for cp in first + passed + onward[:1]:
                cp.wait_send()
            _wait_all(*_to_all_copies(shard_refs, got_refs, sems, True))

    grid_spec = pltpu.PrefetchScalarGridSpec(
        num_scalar_prefetch=2,
        grid=(GP_NT,),
        in_specs=[ANY, pl.BlockSpec((1, D_MODEL), lambda t, o, w: (0, 0)), ANY] + [ANY] * ns,
        out_specs=[pl.BlockSpec((s, GP_TN), lambda t, o, w: (0, o[t])),
                   pl.BlockSpec((GP_TN, D_MODEL), lambda t, o, w: (o[t], 0)), ANY] + [ANY] * ns,
        scratch_shapes=[pltpu.VMEM((N_DEV, W_IN_SHARD, D_MODEL), BF16), pltpu.VMEM((s, D_MODEL), BF16),
                        pltpu.VMEM((GP_TN, D_MODEL), BF16), pltpu.VMEM((2, tx, D_MODEL), F32),
                        pltpu.VMEM((2, D_MODEL, tx), BF16),
                        pltpu.SemaphoreType.DMA((7,)), pltpu.SemaphoreType.DMA((7,)), pltpu.SemaphoreType.DMA((5,))]
        + _copy_sems(ns, 7),
    )
    return pl.pallas_call(
        body,
        grid_spec=grid_spec,
        out_shape=[jax.ShapeDtypeStruct((s, GP_COLS), F32),jax.ShapeDtypeStruct((GP_COLS, D_MODEL), BF16),
                   jax.ShapeDtypeStruct((D_MODEL, s), BF16)]
        + [jax.ShapeDtypeStruct((N_DEV,) + b.shape, b.dtype) for b in shards],
        compiler_params=_params(("arbitrary",), 56),
        name="gather_proj",
    )(order, waits, x, g_pre, w_blk, *shards)


def _mla_prep(proj, g_q, g_kv, w_uq_p, w_kv_p, rc, rs1, rs2):
    s = proj.shape[0]
    tm = 512
    scale = 1.0 / math.sqrt(QK)

    def body(cq_ref, ckv_ref, kpe_ref, gq_ref, gkv_ref, wuq_ref, wkv_ref, c_ref, s1_ref, s2_ref,
             qr_ref, kr_ref, v_ref, cqt_ref, ckvt_ref):
        cq = cq_ref[...]
        r = lax.rsqrt(jnp.mean(cq * cq, axis=-1, keepdims=True) + EPS)
        cqn = (cq * r * gq_ref[...]).astype(BF16)
        cqt_ref[...] = cqn.T
        q = _dot(cqn, wuq_ref[...])
        ckv = ckv_ref[...]
        r = lax.rsqrt(jnp.mean(ckv * ckv, axis=-1, keepdims=True) + EPS)
        ckvn = (ckv * r * gkv_ref[...]).astype(BF16)
        ckvt_ref[...] = ckvn.T
        kv = _dot(ckvn, wkv_ref[...])
        c, s1, s2 = c_ref[...], s1_ref[...], s2_ref[...]
        lane = lax.broadcasted_iota(jnp.int32, (tm, LANE), 1)
        kpe = _rope(kpe_ref[...], c, s1, s2) + jnp.where((lane == QK) | (lane == QK + 1), 1.0, 0.0)
        vone = jnp.where((lane == VDIM) | (lane == VDIM + 1), 1.0, 0.0)
        for h in range(HEADS):
            sl = slice(LANE * h, LANE * (h + 1))
            qr_ref[:, sl] = (_rope(q[:, sl], c, s1, s2) * scale).astype(BF16)
            kr_ref[:, sl] = (kv[:, sl] + kpe).astype(BF16)
            v_ref[:, sl] = (kv[:, HEADS * LANE + LANE * h:HEADS * LANE + LANE * (h + 1)] + vone).astype(BF16)

    row = lambda w, j: pl.BlockSpec((tm, w), lambda i: (i, j))
    col = lambda w: pl.BlockSpec((w, tm), lambda i: (0, i))
    full = lambda a: pl.BlockSpec(a.shape, lambda i: (0, 0))
    return pl.pallas_call(
        body,
        grid=(s // tm,),
        in_specs=[row(768, 6), row(256, 21), row(128, 44), full(g_q), full(g_kv), full(w_uq_p), full(w_kv_p),
                  row(128, 0), row(128, 0), row(128, 0)],
        out_specs=[row(1024, 0), row(1024, 0), row(1024, 0), col(768), col(256)],
        out_shape=[jax.ShapeDtypeStruct((s, 1024), BF16), jax.ShapeDtypeStruct((s, 1024), BF16),
                   jax.ShapeDtypeStruct((s, 1024), BF16), jax.ShapeDtypeStruct((768, s), BF16),
                   jax.ShapeDtypeStruct((256, s), BF16)],
        compiler_params=_params(("arbitrary",)),
        name="mla_prep",
    )(proj, proj, proj, g_q, g_kv, w_uq_p, w_kv_p, rc, rs1, rs2)


ATT_T = 512
ATT_FWD_HEADS = 4


def _chunk_mask(transposed):
    r = lax.broadcasted_iota(jnp.int32, (ATT_T, ATT_T), 0) >> ATT_CHUNK_SHIFT
    c = lax.broadcasted_iota(jnp.int32, (ATT_T, ATT_T), 1) >> ATT_CHUNK_SHIFT
    return (r <= c) if transposed else (c <= r)


def _attn_fwd(qr, kr, vp, shards):
    s = qr.shape[0]
    t = ATT_T
    g = ATT_FWD_HEADS
    ns = len(shards)

    def body(q_ref, k_ref, v_ref, *rest):
        shard_refs, (o_ref, qa_ref), got_refs = rest[:ns], rest[ns:ns + 2], rest[ns + 2:2 * ns + 2]
        sc_ref, sems = rest[2 * ns + 2], rest[2 * ns + 3:]
        qi = pl.program_id(1)

        @pl.when((pl.program_id(0) == 0) & (qi == 0))
        def _():
            _start_all(*_to_all_copies(shard_refs, got_refs, sems, True))
        lane = lax.broadcasted_iota(jnp.int32, (t, LANE), 1)
        sls = [slice(LANE * a, LANE * (a + 1)) for a in range(g)]
        qs = [q_ref[:, sl] for sl in sls]

        def scores(j):
            rows = pl.ds(pl.multiple_of(j * t, t), t)
            for a in range(g):
                sc_ref[j & 1, a] = _dotg(qs[a], k_ref[rows, sls[a]], NT)

        def step(j, carry, masked):
            rows = pl.ds(pl.multiple_of(j * t, t), t)
            out = []
            for a in range(g):
                m, acc = carry[a]
                sc = sc_ref[j & 1, a]
                if masked:
                    sc = jnp.where(_chunk_mask(False), sc, -1e30)
                m_new = jnp.maximum(m, jnp.max(sc, axis=-1, keepdims=True))
                p = jnp.exp(sc - m_new).astype(BF16)
                acc = jnp.exp(m - m_new) * acc + _dot(p, v_ref[rows, sls[a]])
                out.append((m_new, acc))
            return tuple(out)

        def loop(j, carry):
            carry = step(j, carry, False)
            scores(j + 1)
            return carry

        init = tuple((jnp.full((t, 1), -1e30, F32), jnp.zeros((t, LANE), F32)) for _ in range(g))
        scores(0)
        carry = lax.fori_loop(0, qi, loop, init)
        carry = step(qi, carry, True)
        outs = []
        for a in range(g):
            m, acc = carry[a]
            l = acc[:, VDIM:VDIM + 1]
            outs.append(acc / l)
            hi, lo_part = _hi_lo(-(m + jnp.log(l)))
            qa = jnp.where(lane == QK, hi, jnp.where(lane == QK + 1, lo_part, qs[a].astype(F32)))
            qa_ref[:, sls[a]] = qa.astype(BF16)
        for p in range(g // 2):
            o_ref[:, LANE * p:LANE * (p + 1)] = jnp.where(lane < VDIM, outs[2 * p], pltpu.roll(outs[2 * p + 1], VDIM, 1))

        @pl.when((pl.program_id(0) == HEADS // g - 1) & (qi == s // t - 1))
        def _():
            _wait_all(*_to_all_copies(shard_refs, got_refs, sems, True))

    return pl.pallas_call(
        body,
        grid=(HEADS // g, s // t),
        in_specs=[
            pl.BlockSpec((t, g * LANE), lambda h, i: (i, h)),
            pl.BlockSpec((s, g * LANE), lambda h, i: (0, h)),
            pl.BlockSpec((s, g * LANE), lambda h, i: (0, h)),
        ] + [ANY] * ns,
        out_specs=[
            pl.BlockSpec((t, g * VDIM), lambda h, i: (i, h)),
            pl.BlockSpec((t, g * LANE), lambda h, i: (i, h)),
        ] + [ANY] * ns,
        out_shape=[jax.ShapeDtypeStruct((s, 512), F32), jax.ShapeDtypeStruct((s, 1024), BF16)]
        + [jax.ShapeDtypeStruct((N_DEV,) + b.shape, b.dtype) for b in shards],
        scratch_shapes=[pltpu.VMEM((2, g, t, t), F32)] + _copy_sems(ns, 7),
        compiler_params=_params(("arbitrary", "arbitrary")),
        name="attn_fwd",
    )(qr, kr, vp, *shards)


def _attn_bwd(qa, kr, vp, dop, sends):
    s = qa.shape[0]
    t = ATT_T
    nq = s // t
    ns = len(sends)

    def body(q_ref, k_ref, v_ref, do_ref, *rest):
        send_refs, (dq_out, dk_out, dv_out) = rest[:ns], rest[ns:ns + 3]
        recv_refs = rest[ns + 3:2 * ns + 3]
        (dq_ref, dk_ref, dv_ref), sems = rest[2 * ns + 3:2 * ns + 6], rest[2 * ns + 6:]
        j = pl.program_id(1)
        sls = [slice(LANE * a, LANE * (a + 1)) for a in range(2)]

        @pl.when((pl.program_id(0) == 0) & (j == 0))
        def _():
            _start_all(*_to_all_copies(send_refs, recv_refs, sems, False))

        @pl.when(j == 0)
        def _():
            dq_ref[...] = jnp.zeros_like(dq_ref)

        dk_ref[...] = jnp.zeros_like(dk_ref)
        dv_ref[...] = jnp.zeros_like(dv_ref)
        ks = [k_ref[:, sl] for sl in sls]
        vs = [v_ref[:, sl] for sl in sls]

        def part(i, k_lo, k_n, q_lo, q_n, masked):
            rows = pl.ds(pl.multiple_of(i * t + q_lo, 256), q_n)
            keys = slice(k_lo, k_lo + k_n)
            for a in range(2):
                q = q_ref[rows, sls[a]]
                do = do_ref[rows, sls[a]]
                sc = _dotg(ks[a][keys], q, NT)
                if masked:
                    kc = lax.broadcasted_iota(jnp.int32, (k_n, q_n), 0) >> ATT_CHUNK_SHIFT
                    qc = lax.broadcasted_iota(jnp.int32, (k_n, q_n), 1) >> ATT_CHUNK_SHIFT
                    sc = jnp.where(kc <= qc, sc, -1e30)
                p = jnp.exp(sc)
                ds = (p * _dotg(vs[a][keys], do, NT)).astype(BF16)
                dv_ref[keys, sls[a]] += _dot(p.astype(BF16), do)
                dk_ref[keys, sls[a]] += _dot(ds, q)
                dq_ref[rows, sls[a]] += _dotg(ds, ks[a][keys], TN)

        half = t // 2
        part(j, 0, half, 0, t, True)
        part(j, half, half, half, half, True)

        def loop(i, c):
            part(i, 0, t, 0, t, False)
            return c

        lax.fori_loop(j + 1, nq, loop, 0)
        dk_out[...] = dk_ref[...].astype(BF16)
        dv_out[...] = dv_ref[...].astype(BF16)

        @pl.when(j == nq - 1)
        def _():
            dq_out[...] = dq_ref[...].astype(BF16)

        @pl.when((pl.program_id(0) == HEADS // 2 - 1) & (j == nq - 1))
        def _():
            _wait_all(*_to_all_copies(send_refs, recv_refs, sems, False))

    blk = pl.BlockSpec((t, 2 * LANE), lambda h, j: (j, h))
    whole = pl.BlockSpec((s, 2 * LANE), lambda h, j: (0, h))
    out = jax.ShapeDtypeStruct((s, 1024), BF16)
    return pl.pallas_call(
        body,
        grid=(HEADS // 2, nq),
        in_specs=[whole, blk, blk, whole] + [ANY] * ns,
        out_specs=[whole, blk, blk] + [ANY] * ns,
        out_shape=[out, out, out] + [jax.ShapeDtypeStruct(a.shape, a.dtype) for a in sends],
        scratch_shapes=[pltpu.VMEM((s, 2 * LANE), F32), pltpu.VMEM((t, 2 * LANE), F32),
                        pltpu.VMEM((t, 2 * LANE), F32)] + _copy_sems(ns, 7),
        compiler_params=_params(("arbitrary", "arbitrary")),
        name="attn_bwd",
    )(qa, kr, vp, dop, *sends)


HG_T = 256
HG_NC = HG_T // HG_BLOCK
HG_G = 4
GW = 64 * HG_G


def _hg_consts():
    r = jnp.arange(HG_T)[:, None]
    c = jnp.arange(HG_T)[None, :]
    same = (r // HG_BLOCK) == (c // HG_BLOCK)
    mcum = (same & (c <= r)).astype(BF16)
    mrev = (same & (c >= r)).astype(BF16)
    msum = same.astype(BF16)
    a = jnp.arange(GW) // 64
    bd = (a[:, None] == a[None, :]).astype(F32)
    return mcum, mrev, msum, bd


def _stack_heads(xg, head):
    return jnp.concatenate([jnp.where(head == h, xg, 0.0) for h in range(HG_G)], axis=0)


def _unstack_heads(r, head, t):
    out = r[(HG_G - 1) * t:]
    for h in range(HG_G - 2, -1, -1):
        out = jnp.where(head == h, r[h * t:(h + 1) * t], out)
    return out


def _compact_state(st):
    out = st[:64]
    for h in range(1, HG_G):
        out = out + st[64 * h:64 * (h + 1)]
    return out


def _expand_state(cs, head64):
    return jnp.concatenate([jnp.where(head64 == h, cs, 0.0) for h in range(HG_G)], axis=0)


def _hg_pre(hq, hf, lbl, mcum, msum):
    lb = _sigmoid(lbl[0:1, :] - lbl[1:2, :])
    sig = _sigmoid(hf)
    f = lb + (1.0 - lb) * sig
    lf = jnp.log(f)
    b = _sel_left(mcum, lf)
    big_l = _sel_left(msum, lf)
    k = 1.0 - f
    qd = hq * jnp.exp(b)
    ki = k * jnp.exp(-b)
    ke = k * jnp.exp(big_l - b)
    return lb, sig, f, b, big_l, qd, ki, ke


def _hgrn_fwd(proj, lbl):
    s = proj.shape[0]
    t = HG_T
    mcum, _, msum, bd = _hg_consts()

    def body(hq_ref, hf_ref, hi_ref, lbl_ref, mcum_ref, msum_ref, bd_ref, o_ref, sp_ref, st_ref):
        @pl.when(pl.program_id(0) == 0)
        def _():
            st_ref[...] = jnp.zeros_like(st_ref)

        mc = mcum_ref[...]
        _, _, _, _, big_l, qd, ki, ke = _hg_pre(hq_ref[...], hf_ref[...], lbl_ref[...], mc, msum_ref[...])
        el = jnp.exp(big_l)
        hi = hi_ref[...]
        head = lax.broadcasted_iota(jnp.int32, (t, GW), 1) >> 6
        mask = jnp.concatenate([mc] * HG_G, axis=0) > 0.5
        for p in range(HEADS // HG_G):
            sl = slice(GW * p, GW * (p + 1))
            vp = hi[:, sl].astype(BF16)
            qs = _stack_heads(qd[:, sl], head).astype(BF16)
            a = jnp.where(mask, _dotg(qs, ki[:, sl].astype(BF16), NT), 0.0)
            o_intra = _unstack_heads(_dot(a.astype(BF16), vp), head, t)
            qb = qd[:, sl].astype(BF16)
            kb = ke[:, sl].astype(BF16)
            st = st_ref[p]
            for c in range(HG_NC):
                rows = slice(HG_BLOCK * c, HG_BLOCK * (c + 1))
                sp_ref[c, :, sl] = _compact_state(st)
                o_ref[rows, sl] = o_intra[rows] + _dotg(qb[rows], st.astype(BF16), NT)
                u = _dotg(vp[rows], kb[rows], TN) * bd_ref[...]
                st = st * el[HG_BLOCK * c:HG_BLOCK * c + 1, sl] + u
            st_ref[p] = st

    row = lambda j: pl.BlockSpec((t, HG_WIDTH), lambda i: (i, j))
    full = lambda a: pl.BlockSpec(a.shape, lambda i: (0, 0))
    return pl.pallas_call(
        body,
        grid=(s // t,),
        in_specs=[row(6), row(7), row(8), full(lbl), full(mcum), full(msum), full(bd)],
        out_specs=[row(0), pl.BlockSpec((HG_NC, 64, HG_WIDTH), lambda i: (i, 0, 0))],
        out_shape=[jax.ShapeDtypeStruct((s, HG_WIDTH), F32),
                   jax.ShapeDtypeStruct((s // HG_BLOCK, 64, HG_WIDTH), F32)],
        scratch_shapes=[pltpu.VMEM((HEADS // HG_G, GW, GW), F32)],
        compiler_params=_params(("arbitrary",)),
        name="hgrn_fwd",
    )(proj, proj, proj, lbl, mcum, msum, bd)


def _slot_shape(name, r, c):
    return (N_DEV, r, c // N_DEV) if COL_SHARDED[name] else (N_DEV, r // N_DEV, c)


def _emit_slots(name, acc_ref, out_ref):
    r, c = acc_ref.shape
    for p in range(N_DEV):
        if COL_SHARDED[name]:
            out_ref[p] = acc_ref[:, c // N_DEV * p:c // N_DEV * (p + 1)].astype(BF16)
        else:
            out_ref[p] = acc_ref[r // N_DEV * p:r // N_DEV * (p + 1), :].astype(BF16)


def _hgrn_bwd(proj, lbl, do, sprev, dproj, pairs):
    s = proj.shape[0]
    t = HG_T
    nt = s // t
    npair = len(pairs)
    mcum, mrev, msum, bd = _hg_consts()

    def body(hq_ref, hf_ref, hi_ref, lbl_ref, do_ref, sp_ref, mcum_ref, mrev_ref, msum_ref, bd_ref,
             dproj_in, *rest):
        del dproj_in
        pair_refs, (dh_ref, dlbl_ref) = rest[:2 * npair], rest[2 * npair:2 * npair + 2]
        dw_refs, g_ref, acc_refs = rest[2 * npair + 2:3 * npair + 2], rest[3 * npair + 2], rest[3 * npair + 3:]

        @pl.when(pl.program_id(0) == 0)
        def _():
            g_ref[...] = jnp.zeros_like(g_ref)
            dlbl_ref[...] = jnp.zeros_like(dlbl_ref)
            for acc_ref in acc_refs:
                acc_ref[...] = jnp.zeros_like(acc_ref)

        for n, acc_ref in enumerate(acc_refs):
            acc_ref[...] += _dot(pair_refs[2 * n][...], pair_refs[2 * n + 1][...])

        @pl.when(pl.program_id(0) == nt - 1)
        def _():
            for (name, _, _), acc_ref, dw_ref in zip(pairs, acc_refs, dw_refs):
                _emit_slots(name, acc_ref, dw_ref)

        mc = mcum_ref[...]
        lb, sig, f, b, big_l, qd, ki, ke = _hg_pre(hq_ref[...], hf_ref[...], lbl_ref[...], mc, msum_ref[...])
        el = jnp.exp(big_l)
        hi = hi_ref[...]
        dov = do_ref[...]
        head = lax.broadcasted_iota(jnp.int32, (t, GW), 1) >> 6
        head64 = lax.broadcasted_iota(jnp.int32, (64, GW), 1) >> 6
        mask = jnp.concatenate([mc] * HG_G, axis=0) > 0.5
        dqd_parts, dke_parts, dv_parts, del_parts, dki_parts = [], [], [], [], []
        for p in range(HEADS // HG_G):
            sl = slice(GW * p, GW * (p + 1))
            vp = hi[:, sl].astype(BF16)
            qs = _stack_heads(qd[:, sl], head).astype(BF16)
            kip = ki[:, sl].astype(BF16)
            dos = _stack_heads(dov[:, sl], head).astype(BF16)
            a = jnp.where(mask, _dotg(qs, kip, NT), 0.0).astype(BF16)
            da = jnp.where(mask, _dotg(dos, vp, NT), 0.0).astype(BF16)
            r = _dot(da, kip)
            dki_parts.append(_dotg(da, qs, TN))
            qb = qd[:, sl].astype(BF16)
            kb = ke[:, sl].astype(BF16)
            dob = dov[:, sl].astype(BF16)
            g = g_ref[p]
            dqd_c, dv_c, dke_c, del_c = [], [], [], []
            for c in range(HG_NC - 1, -1, -1):
                rows = slice(HG_BLOCK * c, HG_BLOCK * (c + 1))
                gb = g.astype(BF16)
                st = _expand_state(sp_ref[c, :, sl], head64)
                dqd_c.append(_dot(dob[rows], st.astype(BF16)))
                dv_c.append(_dotg(kb[rows], gb, NT))
                dke_c.append(_dot(vp[rows], gb))
                del_c.append(jnp.broadcast_to(jnp.sum(g * st, axis=0, keepdims=True), (HG_BLOCK, GW)))
                g = g * el[HG_BLOCK * c:HG_BLOCK * c + 1, sl] + _dotg(dob[rows], qb[rows], TN) * bd_ref[...]
            g_ref[p] = g
            up = lambda parts: jnp.concatenate(parts[::-1], axis=0)
            dqd_parts.append(_unstack_heads(r, head, t) + up(dqd_c))
            dv_parts.append(_dotg(a, dos, TN) + up(dv_c))
            dke_parts.append(up(dke_c))
            del_parts.append(up(del_c))
        wide = lambda parts: jnp.concatenate(parts, axis=1)
        dqd, dke, dki, dvv, del_rows = wide(dqd_parts), wide(dke_parts), wide(dki_parts), wide(dv_parts), wide(del_parts)
        dh_ref[:, :HG_WIDTH] = (dqd * jnp.exp(b)).astype(BF16)
        dh_ref[:, 2 * HG_WIDTH:] = dvv.astype(BF16)
        dke_ke = dke * ke
        db = dqd * qd - dki * ki - dke_ke
        dl_rows = _sel_left(msum_ref[...], dke_ke) + del_rows * el
        is_last = (lax.broadcasted_iota(jnp.int32, (t, HG_WIDTH), 0) & (HG_BLOCK - 1)) == HG_BLOCK - 1
        db = db + jnp.where(is_last, dl_rows, 0.0)
        dlf = _sel_left(mrev_ref[...], db)
        dk = dki * jnp.exp(-b) + dke * jnp.exp(big_l - b)
        df = dlf / f - dk
        dh_ref[:, HG_WIDTH:2 * HG_WIDTH] = (df * (1.0 - lb) * sig * (1.0 - sig)).astype(BF16)
        dlb = jnp.sum(df * (1.0 - sig), axis=0, keepdims=True) * lb * (1.0 - lb)
        dlbl_ref[0:1, :] += dlb
        dlbl_ref[1:2, :] -= dlb

    rrow = lambda j: pl.BlockSpec((t, HG_WIDTH), lambda i: (nt - 1 - i, j))
    full = lambda a: pl.BlockSpec(a.shape, lambda i: (0, 0))
    pair_specs, dw_specs, dw_shapes, accs = [], [], [], []
    for name, at, b in pairs:
        pair_specs += [pl.BlockSpec((at.shape[0], t), lambda i: (0, i)), pl.BlockSpec((t, b.shape[1]), lambda i: (i, 0))]
        shape = _slot_shape(name, at.shape[0], b.shape[1])
        dw_specs.append(pl.BlockSpec(shape, lambda i: (0, 0, 0)))
        dw_shapes.append(jax.ShapeDtypeStruct(shape, BF16))
        accs.append(pltpu.VMEM((at.shape[0], b.shape[1]), F32))
    return pl.pallas_call(
        body,
        grid=(nt,),
        in_specs=[rrow(6), rrow(7), rrow(8), full(lbl), rrow(0),
                  pl.BlockSpec((HG_NC, 64, HG_WIDTH), lambda i: (nt - 1 - i, 0, 0)),
                  full(mcum), full(mrev), full(msum), full(bd), pl.BlockSpec(memory_space=pl.ANY)] + pair_specs,
        out_specs=[pl.BlockSpec((t, 3 * HG_WIDTH), lambda i: (nt - 1 - i, 2)),
                   pl.BlockSpec((2, HG_WIDTH), lambda i: (0, 0))] + dw_specs,
        out_shape=[jax.ShapeDtypeStruct(dproj.shape, BF16), jax.ShapeDtypeStruct((2, HG_WIDTH), F32)] + dw_shapes,
        input_output_aliases={10: 0},
        scratch_shapes=[pltpu.VMEM((HEADS // HG_G, GW, GW), F32)] + accs,
        compiler_params=_params(("arbitrary",)),
        name="hgrn_bwd",
    )(proj, proj, proj, lbl, do, sprev, mcum, mrev, msum, bd, dproj, *[a for pair in pairs for a in pair[1:]])


def _tail(x, tgt, proj, attn, o, w_a, w_b, w_out, w_at, w_bt, w_outt, b_gate, g_post, gh):
    s = x.shape[0]
    tm = 256
    ones64 = (jnp.arange(HG_WIDTH)[:, None] // 64 == jnp.arange(HG_WIDTH)[None, :] // 64).astype(BF16)
    weights = (w_a, w_b, w_out, w_at, w_bt, w_outt)

    def body(x_ref, t_ref, ml_ref, ga_ref, gb_ref, at_ref, o_ref, *rest):
        w_hbm, (bg_ref, gp_ref, gh_ref, ones_ref) = rest[:6], rest[6:10]
        (dout_ref, dpj_ref, dop_ref, do_ref, mt_ref, dy_ref, yat_ref, dya_ref, ybt_ref, dyb_ref,
         loss_ref, dgp_ref, dbg_ref, dgh_ref) = rest[10:24]
        (wa_ref, wb_ref, wo_ref, wat_ref, wbt_ref, wot_ref), w_sem = rest[24:30], rest[30]

        @pl.when(pl.program_id(0) == 0)
        def _():
            loads = [pltpu.make_async_copy(src, dst, w_sem.at[k])
                     for k, (src, dst) in enumerate(zip(w_hbm, rest[24:30]))]
            _start_all(loads, [])
            loss_ref[...] = jnp.zeros_like(loss_ref)
            dgp_ref[...] = jnp.zeros_like(dgp_ref)
            dbg_ref[...] = jnp.zeros_like(dbg_ref)
            dgh_ref[...] = jnp.zeros_like(dgh_ref)
            _wait_all(loads, [])

        ones = ones_ref[...]
        gate_a = ga_ref[...]
        sa = _sigmoid(gate_a)
        silu_a = gate_a * sa
        attn_v = at_ref[...]
        ya_in = attn_v * silu_a
        ov = o_ref[...]
        ro = lax.rsqrt(_sel_right(ov * ov, ones) * (1.0 / 64.0) + EPS)
        ohat = ov * ro
        ghv = gh_ref[...]
        on = ohat * ghv
        gate_b = gb_ref[...]
        sb = _sigmoid(gate_b)
        silu_b = gate_b * sb
        yb_in = on * silu_b
        ya_bf = ya_in.astype(BF16)
        yb_bf = yb_in.astype(BF16)
        yat_ref[...] = ya_bf.T
        ybt_ref[...] = yb_bf.T
        y_a = _dot(ya_bf, wa_ref[...])
        y_b = _dot(yb_bf, wb_ref[...])
        gts = _sigmoid(ml_ref[...] + bg_ref[...])
        g_a = gts[:, :D_MODEL]
        g_b = gts[:, D_MODEL:]
        m_bf = (g_a * y_a + g_b * y_b).astype(BF16)
        mt_ref[...] = m_bf.T
        y = _dot(m_bf, wo_ref[...])
        r1 = lax.rsqrt(jnp.mean(y * y, axis=-1, keepdims=True) + EPS)
        yn = y * r1
        gp = gp_ref[...]
        e = x_ref[...] + yn * gp - t_ref[...]
        loss_ref[...] += jnp.sum(e * e, axis=0, keepdims=True)
        dout = e * (1.0 / D_MODEL)
        dout_ref[...] = dout
        dgp_ref[...] += jnp.sum(dout * yn, axis=0, keepdims=True)
        dyn = dout * gp
        dy = r1 * (dyn - yn * jnp.mean(dyn * yn, axis=-1, keepdims=True))
        dy_bf = dy.astype(BF16)
        dy_ref[...] = dy_bf
        dm = _dot(dy_bf, wot_ref[...])
        dml_a = dm * y_a * g_a * (1.0 - g_a)
        dml_b = dm * y_b * g_b * (1.0 - g_b)
        dpj_ref[:, :D_MODEL] = dml_a.astype(BF16)
        dpj_ref[:, D_MODEL:2 * D_MODEL] = dml_b.astype(BF16)
        dbg_ref[:, :D_MODEL] += jnp.sum(dml_a, axis=0, keepdims=True)
        dbg_ref[:, D_MODEL:] += jnp.sum(dml_b, axis=0, keepdims=True)
        dya_bf = (dm * g_a).astype(BF16)
        dyb_bf = (dm * g_b).astype(BF16)
        dya_ref[...] = dya_bf
        dyb_ref[...] = dyb_bf
        dya_in = _dot(dya_bf, wat_ref[...])
        dyb_in = _dot(dyb_bf, wbt_ref[...])
        dattn = dya_in * silu_a
        delta = _sel_right(dattn * attn_v, ones)
        lane = lax.broadcasted_iota(jnp.int32, (tm, LANE), 1)
        for p in range(HEADS // 2):
            sl = slice(LANE * p, LANE * (p + 1))
            xs = (dattn[:, sl], pltpu.roll(dattn[:, sl], VDIM, 1))
            nds = (-pltpu.roll(delta[:, sl], VDIM, 1), -delta[:, sl])
            for a in range(2):
                hi, lo_part = _hi_lo(nds[a])
                blk = jnp.where(lane < VDIM, xs[a], jnp.where(lane == VDIM, hi, jnp.where(lane == VDIM + 1, lo_part, 0.0)))
                dop_ref[:, LANE * (2 * p + a):LANE * (2 * p + a + 1)] = blk.astype(BF16)
        dpj_ref[:, 2 * D_MODEL:2 * D_MODEL + HG_WIDTH] = (
            dya_in * attn_v * (sa * (1.0 + gate_a * (1.0 - sa)))).astype(BF16)
        don = dyb_in * silu_b
        dpj_ref[:, 2 * D_MODEL + HG_WIDTH:] = (dyb_in * on * (sb * (1.0 + gate_b * (1.0 - sb)))).astype(BF16)
        dgh_ref[...] += jnp.sum(don * ohat, axis=0, keepdims=True)
        dohat = don * ghv
        do_ref[...] = (ro * (dohat - ohat * (_sel_right(dohat * ohat, ones) * (1.0 / 64.0)))).astype(BF16)

    row = lambda w, j: pl.BlockSpec((tm, w), lambda i: (i, j))
    col = lambda w: pl.BlockSpec((w, tm), lambda i: (0, i))
    full = lambda a: pl.BlockSpec(a.shape, lambda i: (0, 0))
    acc = lambda w: pl.BlockSpec((1, w), lambda i: (0, 0))
    sds = lambda w, dt: jax.ShapeDtypeStruct((s, w), dt)
    sdt = lambda w: jax.ShapeDtypeStruct((w, s), BF16)
    return pl.pallas_call(
        body,
        grid=(s // tm,),
        in_specs=[row(1024, 0), row(1024, 0), row(2048, 0), row(512, 4), row(512, 5), row(512, 0), row(512, 0)]
        + [ANY] * 6 + [full(b_gate), full(g_post), full(gh), full(ones64)],
        out_specs=[row(1024, 0), row(3072, 0), row(1024, 0), row(512, 0),
                   col(1024), row(1024, 0), col(512), row(1024, 0), col(512), row(1024, 0),
                   acc(1024), acc(1024), acc(2048), acc(512)],
        out_shape=[sds(1024, F32), sds(D_IN_PAD, BF16), sds(1024, BF16), sds(512, BF16),
                   sdt(1024), sds(1024, BF16), sdt(512), sds(1024, BF16), sdt(512), sds(1024, BF16),
                   jax.ShapeDtypeStruct((1, 1024), F32), jax.ShapeDtypeStruct((1, 1024), F32),
                   jax.ShapeDtypeStruct((1, 2048), F32), jax.ShapeDtypeStruct((1, 512), F32)],
        scratch_shapes=[pltpu.VMEM(a.shape, BF16) for a in weights] + [pltpu.SemaphoreType.DMA((6,))],
        compiler_params=_params(("arbitrary",), 56),
        name="tail",
    )(x, tgt, proj, proj, proj, attn, o, *weights, b_gate, g_post, gh, ones64)


def _mla_bwd(proj, dqr, dkr, dv, g_q, g_kv, w_uq_pt, w_kv_pt, rc, rs1, rs2, cqt, ckvt, dproj):
    assert HEADS == N_DEV
    s = proj.shape[0]
    tm = 512
    scale = 1.0 / math.sqrt(QK)

    def body(cq_ref, ckv_ref, dqr_ref, dkr_ref, dv_ref, gq_ref, gkv_ref, wuqt_ref, wkvt_ref, c_ref, s1_ref, s2_ref,
             cqt_ref, ckvt_ref, dproj_in, dc_ref, dgq_ref, dgkv_ref, uq_slots, ukv_slots,
             dqf_ref, dkvf_ref, dwuq_ref, dwkv_ref):
        del dproj_in

        @pl.when(pl.program_id(0) == 0)
        def _():
            dgq_ref[...] = jnp.zeros_like(dgq_ref)
            dgkv_ref[...] = jnp.zeros_like(dgkv_ref)
            dwuq_ref[...] = jnp.zeros_like(dwuq_ref)
            dwkv_ref[...] = jnp.zeros_like(dwkv_ref)

        c, s1, s2 = c_ref[...], s1_ref[...], s2_ref[...]
        lane = lax.broadcasted_iota(jnp.int32, (tm, LANE), 1)
        ksum = jnp.zeros((tm, LANE), F32)
        for h in range(HEADS):
            sl = slice(LANE * h, LANE * (h + 1))
            dqf_ref[:, sl] = (_unrope(dqr_ref[:, sl], c, s1, s2) * scale).astype(BF16)
            dkh = dkr_ref[:, sl]
            ksum = ksum + dkh
            dkvf_ref[:, sl] = jnp.where(lane < NOPE, dkh, 0.0).astype(BF16)
            dkvf_ref[:, HEADS * LANE + LANE * h:HEADS * LANE + LANE * (h + 1)] = jnp.where(
                lane < VDIM, dv_ref[:, sl], 0.0).astype(BF16)
        dkpe = _unrope(ksum, c, s1, s2)
        dc_ref[:, Q_LORA + KV_LORA:] = jnp.where((lane >= NOPE) & (lane < QK), dkpe, 0.0).astype(BF16)
        dqf, dkvf = dqf_ref[...], dkvf_ref[...]
        dwuq_ref[...] += _dot(cqt_ref[...], dqf)
        dwkv_ref[...] += _dot(ckvt_ref[...], dkvf)
        dcqn = _dot(dqf, wuqt_ref[...])
        dckvn = _dot(dkvf, wkvt_ref[...])
        for x_ref, g_ref, dn, cols, dg_ref in ((cq_ref, gq_ref, dcqn, slice(0, Q_LORA), dgq_ref),
                                               (ckv_ref, gkv_ref, dckvn, slice(Q_LORA, Q_LORA + KV_LORA), dgkv_ref)):
            xv = x_ref[...]
            r = lax.rsqrt(jnp.mean(xv * xv, axis=-1, keepdims=True) + EPS)
            xh = xv * r
            dg_ref[...] += jnp.sum(dn * xh, axis=0, keepdims=True)
            dh = dn * g_ref[...]
            dc_ref[:, cols] = (r * (dh - xh * jnp.mean(dh * xh, axis=-1, keepdims=True))).astype(BF16)

        @pl.when(pl.program_id(0) == s // tm - 1)
        def _():
            ur = Q_LORA // N_DEV
            for p in range(N_DEV):
                uq_slots[p] = jnp.concatenate(
                    [dwuq_ref[ur * p:ur * (p + 1), LANE * h:LANE * h + QK] for h in range(HEADS)], axis=1).astype(BF16)
                ukv_slots[p] = jnp.concatenate(
                    [dwkv_ref[:, LANE * p:LANE * p + NOPE],
                     dwkv_ref[:, LANE * (HEADS + p):LANE * (HEADS + p) + VDIM]], axis=1).astype(BF16)

    row = lambda w, j: pl.BlockSpec((tm, w), lambda i: (i, j))
    full = lambda a: pl.BlockSpec(a.shape, lambda i: (0, 0))
    acc = lambda w: pl.BlockSpec((1, w), lambda i: (0, 0))
    col = lambda w: pl.BlockSpec((w, tm), lambda i: (0, i))
    whole = lambda shape: pl.BlockSpec(shape, lambda i: (0, 0, 0))
    uq_shape = (N_DEV, Q_LORA // N_DEV, HEADS * QK)
    ukv_shape = (N_DEV, KV_LORA, NOPE + VDIM)
    return pl.pallas_call(
        body,
        grid=(s // tm,),
        in_specs=[row(768, 6), row(256, 21), row(1024, 0), row(1024, 0), row(1024, 0), full(g_q), full(g_kv),
                  full(w_uq_pt), full(w_kv_pt), row(128, 0), row(128, 0), row(128, 0), col(Q_LORA), col(KV_LORA),
                  pl.BlockSpec(memory_space=pl.ANY)],
        out_specs=[row(1152, 4), acc(768), acc(256), whole(uq_shape), whole(ukv_shape)],
        out_shape=[jax.ShapeDtypeStruct(dproj.shape, BF16),
                   jax.ShapeDtypeStruct((1, 768), F32), jax.ShapeDtypeStruct((1, 256), F32),
                   jax.ShapeDtypeStruct(uq_shape, BF16), jax.ShapeDtypeStruct(ukv_shape, BF16)],
        input_output_aliases={14: 0},
        scratch_shapes=[pltpu.VMEM((tm, HEADS * LANE), BF16), pltpu.VMEM((tm, 2 * HEADS * LANE), BF16),
                        pltpu.VMEM((Q_LORA, HEADS * LANE), F32), pltpu.VMEM((KV_LORA, 2 * HEADS * LANE), F32)],
        compiler_params=_params(("arbitrary",)),
        name="mla_bwd",
    )(proj, proj, dqr, dkr, dv, g_q, g_kv, w_uq_pt, w_kv_pt, rc, rs1, rs2, cqt, ckvt, dproj)


def _dh_dx(dproj, w_in_pt, x, dout, g_pre, sends):
    s, k = dproj.shape
    tm = 256
    ns, ni = len(sends), s // tm

    def body(dp_ref, w_ref, x_ref, dout_ref, g_ref, *rest):
        send_refs, (dx_ref, dg_ref) = rest[:ns], rest[ns:ns + 2]
        recv_refs, sems = rest[ns + 2:2 * ns + 2], rest[2 * ns + 2:]

        @pl.when(pl.program_id(0) == 0)
        def _():
            _start_all(*_to_chips_copies(send_refs, recv_refs, sems))
            dg_ref[...] = jnp.zeros_like(dg_ref)

        dh = _dot(dp_ref[...], w_ref[...])
        xv = x_ref[...]
        r = lax.rsqrt(jnp.mean(xv * xv, axis=-1, keepdims=True) + EPS)
        xh = xv * r
        dg_ref[...] += jnp.sum(dh * xh, axis=0, keepdims=True)
        dxh = dh * g_ref[...]
        dx_ref[...] = dout_ref[...] + r * (dxh - xh * jnp.mean(dxh * xh, axis=-1, keepdims=True))

        @pl.when(pl.program_id(0) == ni - 1)
        def _():
            _wait_all(*_to_chips_copies(send_refs, recv_refs, sems))

    row = lambda w: pl.BlockSpec((tm, w), lambda i: (i, 0))
    return pl.pallas_call(
        body,
        grid=(ni,),
        in_specs=[row(k), pl.BlockSpec((k, D_MODEL), lambda i: (0, 0)), row(D_MODEL), row(D_MODEL),
                  pl.BlockSpec((1, D_MODEL), lambda i: (0, 0))] + [ANY] * ns,
        out_specs=[row(D_MODEL), pl.BlockSpec((1, D_MODEL), lambda i: (0, 0))] + [ANY] * ns,
        out_shape=[jax.ShapeDtypeStruct((s, D_MODEL), F32), jax.ShapeDtypeStruct((1, D_MODEL), F32)]
        + [jax.ShapeDtypeStruct(a.shape, a.dtype) for a in sends],
        scratch_shapes=_copy_sems(ns, 3),
        compiler_params=_params(("arbitrary",)),
        name="dh_dx",
    )(dproj, w_in_pt, x, dout, g_pre, *sends)


def _pair_reduce(slots):
    n = len(slots)
    half = [(N_DEV // 2,) + a.shape[1:] for a in slots]

    def body(*refs):
        s_refs, o_refs = refs[:n], refs[n:2 * n]
        mine, got = refs[2 * n:3 * n], refs[3 * n:4 * n]
        send_sems, recv_sems, local_sems, out_sems = refs[4 * n:]
        x, y, c = _my_place()
        copies, loads, stores = [], [], []
        for q in range(N_DEV // 2):
            for a in range(n):
                copies.append(pltpu.make_async_remote_copy(
                    src_ref=s_refs[a].at[2 * q + 1 - c], dst_ref=got[a].at[q],
                    send_sem=send_sems.at[4 * a + q], recv_sem=recv_sems.at[4 * a + q],
                    device_id=(x, y, 1 - c), device_id_type=MESH_ID))
                loads.append(pltpu.make_async_copy(s_refs[a].at[2 * q + c], mine[a].at[q], local_sems.at[4 * a + q]))
                stores.append(pltpu.make_async_copy(mine[a].at[q], o_refs[a].at[q], out_sems.at[4 * a + q]))
        _start_all(loads, copies)
        k = 0
        for q in range(N_DEV // 2):
            for a in range(n):
                loads[k].wait()
                copies[k].wait_recv()
                mine[a][q] = (mine[a][q].astype(F32) + got[a][q].astype(F32)).astype(mine[a].dtype)
                stores[k].start()
                k += 1
        for cp in copies:
            cp.wait_send()
        for cp in stores:
            cp.wait()

    vm = lambda: [pltpu.VMEM(h, a.dtype) for h, a in zip(half, slots)]
    return pl.pallas_call(
        body,
        in_specs=[ANY] * n,
        out_specs=[ANY] * n,
        out_shape=[jax.ShapeDtypeStruct(h, a.dtype) for h, a in zip(half, slots)],
        scratch_shapes=vm() + vm() + [pltpu.SemaphoreType.DMA((4 * n,)), pltpu.SemaphoreType.DMA((4 * n,)),
                                      pltpu.SemaphoreType.DMA((4 * n,)), pltpu.SemaphoreType.DMA((4 * n,))],
        compiler_params=pltpu.CompilerParams(vmem_limit_bytes=48 * 2**20),
        name="pair_reduce",
    )(*slots)


def _rope_tables(s):
    inv = (np.float32(ROPE_THETA) ** (-np.arange(0, ROPE, 2, dtype=np.float32) / np.float32(ROPE))).astype(np.float32)
    ang = (np.arange(s, dtype=np.float32)[:, None] * inv[None, :]).astype(np.float32)
    cos, sin = jnp.asarray(np.cos(ang.astype(np.float64)), F32), jnp.asarray(np.sin(ang.astype(np.float64)), F32)
    z = lambda w: jnp.zeros((s, w), F32)
    rc = jnp.concatenate([jnp.ones((s, NOPE), F32), cos, cos, z(32)], axis=1)
    rs1 = jnp.concatenate([z(NOPE), -sin, z(16), z(32)], axis=1)
    rs2 = jnp.concatenate([z(NOPE), z(16), sin, z(32)], axis=1)
    return rc, rs1, rs2


def _step(x, tgt, w_blk, shards, g_pre, b_gate, g_q, g_kv, lbl, g_hgrn, g_post):
    s = x.shape[0]
    rc, rs1, rs2 = _rope_tables(s)
    gh = jnp.tile(g_hgrn, (1, HEADS))

    proj, w_in_pt, ht, *got = _gather_proj(x, g_pre, w_blk, shards[:2])
    w_uq, w_ukv = (_from_slots(n, g) for n, g in zip(MATS[:2], got))
    w_uq_p = jnp.pad(w_uq.reshape(Q_LORA, HEADS, QK), ((0, 0), (0, 0), (0, LANE - QK))).reshape(Q_LORA, HEADS * LANE)
    kv3 = w_ukv.reshape(KV_LORA, HEADS, NOPE + VDIM)
    pad64 = lambda t: jnp.pad(t, ((0, 0), (0, 0), (0, LANE - 64))).reshape(KV_LORA, HEADS * LANE)
    w_kv_p = jnp.concatenate([pad64(kv3[:, :, :NOPE]), pad64(kv3[:, :, NOPE:])], axis=1)

    qr, kr, v, cqt, ckvt = _mla_prep(proj, g_q, g_kv, w_uq_p, w_kv_p, rc, rs1, rs2)
    attn, qa, *got = _attn_fwd(qr, kr, v, shards[2:])
    w_a, w_b, w_out = (_from_slots(n, g) for n, g in zip(MATS[2:], got))
    o, sprev = _hgrn_fwd(proj, lbl)
    (dout, dproj, dop, do, mt, dy_bf, yat, dya_bf, ybt, dyb_bf,
     loss_vec, dg_post, db_gate, dgh) = _tail(x, tgt, proj, attn, o, w_a, w_b, w_out, w_a.T, w_b.T, w_out.T,
                                               b_gate, g_post, gh)
    dproj, dlbl, *early = _hgrn_bwd(proj, lbl, do, sprev, dproj,
                                    [("w_branch_a", yat, dya_bf), ("w_branch_b", ybt, dyb_bf), ("w_out", mt, dy_bf)])
    dqr, dkr, dv, *early_recv = _attn_bwd(qa, kr, v, dop, early)
    dproj, dg_q, dg_kv, dw_uq_slots, dw_ukv_slots = _mla_bwd(proj, dqr, dkr, dv, g_q, g_kv, w_uq_p.T, w_kv_p.T,
                                                             rc, rs1, rs2, cqt, ckvt, dproj)

    dw_in_slots = _dw_in_slots(ht, dproj)
    late = _pair_reduce([dw_in_slots, dw_uq_slots, dw_ukv_slots])
    dx, dg_pre, *late_recv = _dh_dx(dproj, w_in_pt, x, dout, g_pre, late)

    g_sum = _vectors_sum(dg_pre, db_gate, dg_q, dg_kv, dlbl, dgh, dg_post, loss_vec)
    return dx, late_recv[0], dict(zip(MATS, late_recv[1:] + early_recv)), g_sum


def _adamw(g, w, m, v):
    c1 = 1.0 / (1.0 - ADAM_B1 ** ADAM_STEP)
    c2 = 1.0 / (1.0 - ADAM_B2 ** ADAM_STEP)
    nm = ADAM_B1 * m + (1.0 - ADAM_B1) * g
    nv = ADAM_B2 * v + (1.0 - ADAM_B2) * (g * g)
    d = -ADAM_LR * ((nm * c1) / (jnp.sqrt(nv * c2) + ADAM_EPS) + ADAM_WD * w)
    return d, nm, nv


def _sum8(r_ref):
    g = r_ref[0].astype(F32)
    for k in range(1, r_ref.shape[0]):
        g = g + r_ref[k].astype(F32)
    return g


def _sum_adamw_w_in(recv, w, m, v):
    rows, _, cols = w.shape
    tc = 256
    nc = cols // tc

    def body(r_ref, w_hbm, m_hbm, v_hbm, g_hbm, d_hbm, nm_hbm, nv_hbm, ins, outs, in_sems, out_sems):
        i = pl.program_id(0)
        slot = i & 1
        cols_of = lambda step: pl.ds(pl.multiple_of(step * tc, tc), tc)

        def load(k, step, sl):
            return pltpu.make_async_copy((w_hbm, m_hbm, v_hbm)[k].at[:, 0, cols_of(step)], ins.at[sl, k],
                                         in_sems.at[sl, k])

        def store(k, step, sl):
            return pltpu.make_async_copy(outs.at[sl, k], (g_hbm, d_hbm, nm_hbm, nv_hbm)[k].at[:, 0, cols_of(step)],
                                         out_sems.at[sl, k])

        @pl.when(i == 0)
        def _():
            for k in range(3):
                load(k, 0, 0).start()

        @pl.when(i + 1 < nc)
        def _():
            for k in range(3):
                load(k, i + 1, 1 - slot).start()

        @pl.when(i >= 2)
        def _():
            for k in range(4):
                store(k, i - 2, slot).wait()

        for k in range(3):
            load(k, i, slot).wait()
        g = _sum8(r_ref)
        d, nm, nv = _adamw(g, ins[slot, 0], ins[slot, 1], ins[slot, 2])
        for k, val in enumerate((g, d, nm, nv)):
            outs[slot, k] = val
        for k in range(4):
            store(k, i, slot).start()

        @pl.when(i == nc - 1)
        def _():
            for k in range(4):
                store(k, i, slot).wait()
            if nc >= 2:
                for k in range(4):
                    store(k, i - 1, 1 - slot).wait()

    out = jax.ShapeDtypeStruct((rows, 1, cols), F32)
    return pl.pallas_call(
        body,
        grid=(nc,),
        in_specs=[pl.BlockSpec((recv.shape[0], rows, tc), lambda i: (0, 0, i)), ANY, ANY, ANY],
        out_specs=[ANY, ANY, ANY, ANY],
        out_shape=[out, out, out, out],
        scratch_shapes=[pltpu.VMEM((2, 3, rows, tc), F32), pltpu.VMEM((2, 4, rows, tc), F32),
                        pltpu.SemaphoreType.DMA((2, 3)), pltpu.SemaphoreType.DMA((2, 4))],
        compiler_params=_params(("arbitrary",)),
        name="sum_adamw_w_in",
    )(recv, w, m, v)


def _sum_adamw_whole(recvs, ws, ms, vs):
    n = len(ws)

    def body(*refs):
        r_refs, w_refs, m_refs, v_refs = refs[:n], refs[n:2 * n], refs[2 * n:3 * n], refs[3 * n:4 * n]
        outs = refs[4 * n:]
        for a in range(n):
            g = _sum8(r_refs[a])
            d, nm, nv = _adamw(g, w_refs[a][...], m_refs[a][...], v_refs[a][...])
            outs[a][...] = g
            outs[n + a][...] = d
            outs[2 * n + a][...] = nm
            outs[3 * n + a][...] = nv

    shapes = [jax.ShapeDtypeStruct(w.shape, F32) for w in ws]
    res = pl.pallas_call(
        body,
        out_shape=shapes * 4,
        compiler_params=pltpu.CompilerParams(vmem_limit_bytes=48 * 2**20),
        name="sum_adamw_mats",
    )(*recvs, *ws, *ms, *vs)
    return res[:n], res[n:2 * n], res[2 * n:3 * n], res[3 * n:]


SMALL = ("g_pre", "b_gate", "g_q", "g_kv", "lb_logits", "g_hgrn", "g_post")
SMALL_SHAPE = dict(g_pre=(1, 1024), b_gate=(1, 2048), g_q=(1, 768), g_kv=(1, 256), lb_logits=(2, 512),
                   g_hgrn=(1, 64), g_post=(1, 1024))


def _vectors_sum(dg_pre, db_gate, dg_q, dg_kv, dlbl, dgh, dg_post, loss_vec):
    def body(gpre_ref, bg_ref, gq_ref, gkv_ref, lbl_ref, gh_ref, gpost_ref, loss_ref, out_ref, mine, got,
             send_sems, recv_sems):
        mine[...] = jnp.zeros_like(mine)
        mine[0:1, :] = gpre_ref[...]
        mine[1:2, :] = bg_ref[:, :1024]
        mine[2:3, :] = bg_ref[:, 1024:]
        mine[3:4, :Q_LORA] = gq_ref[...]
        mine[4:5, :KV_LORA] = gkv_ref[...]
        loss = (0.5 / D_MODEL) * jnp.sum(loss_ref[...], axis=-1, keepdims=True)
        mine[4:5, KV_LORA:] = jnp.broadcast_to(loss, (1, 1024 - KV_LORA))
        mine[5:6, :HG_WIDTH] = lbl_ref[0:1, :]
        mine[5:6, HG_WIDTH:] = lbl_ref[1:2, :]
        gh = gh_ref[...]
        fold = gh[:, :VDIM]
        for h in range(1, HEADS):
            fold = fold + gh[:, VDIM * h:VDIM * (h + 1)]
        mine[6:7, :VDIM] = fold
        mine[7:8, :] = gpost_ref[...]
        x, y, c = _my_place()
        me = 4 * x + 2 * y + c
        got[me] = mine[...]
        copies = [pltpu.make_async_remote_copy(
            src_ref=mine, dst_ref=got.at[me], send_sem=send_sems.at[k], recv_sem=recv_sems.at[k],
            device_id=_flip(k, x, y, c), device_id_type=MESH_ID) for k in range(N_DEV - 1)]
        _start_all([], copies)
        _wait_all([], copies)
        out_ref[...] = _sum8(got)

    return pl.pallas_call(
        body,
        out_shape=jax.ShapeDtypeStruct((8, 1024), F32),
        scratch_shapes=[pltpu.VMEM((8, 1024), F32), pltpu.VMEM((N_DEV, 8, 1024), F32),
                        pltpu.SemaphoreType.DMA((7,)), pltpu.SemaphoreType.DMA((7,))],
        name="vectors_sum",
    )(dg_pre, db_gate, dg_q, dg_kv, dlbl, dgh, dg_post, loss_vec)


def _vectors_adamw(g_sum, ws, ms, vs):
    n = len(SMALL)

    def body(g_ref, *refs):
        w_refs, m_refs, v_refs = refs[:n], refs[n:2 * n], refs[2 * n:3 * n]
        loss_ref, outs = refs[3 * n], refs[3 * n + 1:]
        g = g_ref[...]
        loss_ref[...] = g[4:5, KV_LORA:KV_LORA + 1]
        grads = (g[0:1, :], jnp.concatenate([g[1:2, :], g[2:3, :]], axis=1), g[3:4, :Q_LORA], g[4:5, :KV_LORA],
                 jnp.concatenate([g[5:6, :HG_WIDTH], g[5:6, HG_WIDTH:]], axis=0), g[6:7, :VDIM], g[7:8, :])
        for a in range(n):
            d, nm, nv = _adamw(grads[a], w_refs[a][...], m_refs[a][...], v_refs[a][...])
            outs[a][...] = grads[a]
            outs[n + a][...] = d
            outs[2 * n + a][...] = nm
            outs[3 * n + a][...] = nv

    shapes = [jax.ShapeDtypeStruct(SMALL_SHAPE[k], F32) for k in SMALL]
    res = pl.pallas_call(
        body,
        out_shape=[jax.ShapeDtypeStruct((1, 1), F32)] + shapes * 4,
        name="vectors_adamw",
    )(g_sum, *ws, *ms, *vs)
    return res[0], res[1:n + 1], res[n + 1:2 * n + 1], res[2 * n + 1:3 * n + 1], res[3 * n + 1:]


MATS = ("w_uq", "w_ukv", "w_branch_a", "w_branch_b", "w_out")
COL_SHARDED = dict(w_uq=False, w_ukv=True, w_branch_a=True, w_branch_b=True, w_out=False)
ORDER = ("g_pre", "w_in", "b_gate", "g_q", "w_uq", "g_kv", "w_ukv", "lb_logits", "g_hgrn",
         "w_branch_a", "w_branch_b", "w_out", "g_post")


def _from_slots(name, slots):
    _, r, c = slots.shape
    if COL_SHARDED[name]:
        return slots.transpose(1, 0, 2).reshape(r, N_DEV * c)
    return slots.reshape(N_DEV * r, c)


def kernel(x, g_pre, w_in, b_gate, g_q, w_uq, g_kv, w_ukv, lb_logits, g_hgrn, w_branch_a, w_branch_b, w_out, g_post, loss_target, m_g_pre, m_w_in, m_b_gate, m_g_q, m_w_uq, m_g_kv, m_w_ukv, m_lb_logits, m_g_hgrn, m_w_branch_a, m_w_branch_b, m_w_out, m_g_post, v_g_pre, v_w_in, v_b_gate, v_g_q, v_w_uq, v_g_kv, v_w_ukv, v_lb_logits, v_g_hgrn, v_w_branch_a, v_w_branch_b, v_w_out, v_g_post):
    rows3 = lambda a: jnp.transpose(a, (2, 0, 1))
    w = dict(w_in=rows3(w_in), w_uq=w_uq[0], w_ukv=w_ukv[0], w_branch_a=w_branch_a[0], w_branch_b=w_branch_b[0],
             w_out=w_out[0], g_pre=g_pre, b_gate=b_gate, g_q=g_q, g_kv=g_kv, lb_logits=lb_logits, g_hgrn=g_hgrn,
             g_post=g_post)
    mom = dict(w_in=rows3(m_w_in), w_uq=m_w_uq[0], w_ukv=m_w_ukv[0], w_branch_a=m_w_branch_a[0],
               w_branch_b=m_w_branch_b[0], w_out=m_w_out[0], g_pre=m_g_pre, b_gate=m_b_gate, g_q=m_g_q, g_kv=m_g_kv,
               lb_logits=m_lb_logits, g_hgrn=m_g_hgrn, g_post=m_g_post)
    var = dict(w_in=rows3(v_w_in), w_uq=v_w_uq[0], w_ukv=v_w_ukv[0], w_branch_a=v_w_branch_a[0],
               w_branch_b=v_w_branch_b[0], w_out=v_w_out[0], g_pre=v_g_pre, b_gate=v_b_gate, g_q=v_g_q, g_kv=v_g_kv,
               lb_logits=v_lb_logits, g_hgrn=v_g_hgrn, g_post=v_g_post)

    w_blk = w["w_in"].reshape(W_IN_SHARD, D_MODEL).astype(BF16)
    dx, recv_in, recv, g_sum = _step(x[0], loss_target[0], w_blk, [w[n].astype(BF16) for n in MATS],
                                     g_pre, b_gate, g_q, g_kv, lb_logits, g_hgrn, g_post)

    g_in, d_in, m_in, v_in = _sum_adamw_w_in(recv_in, w["w_in"], mom["w_in"], var["w_in"])
    res = _sum_adamw_whole([recv[n] for n in MATS], *([t[n] for n in MATS] for t in (w, mom, var)))
    total, *vec = _vectors_adamw(g_sum, *([t[n] for n in SMALL] for t in (w, mom, var)))

    outs = []
    for mats, vecs, big in zip(res, vec, (g_in, d_in, m_in, v_in)):
        t = {**{n: a[None] for n, a in zip(MATS, mats)}, **dict(zip(SMALL, vecs)),
             "w_in": jnp.transpose(big, (1, 2, 0))}
        outs += [t[n] for n in ORDER]
    return (total.reshape(()), dx[None], *outs)
```

```python
import math

import jax
import jax.numpy as jnp
import numpy as np
from jax import lax
from jax.experimental import pallas as pl
from jax.experimental.pallas import tpu as pltpu

F32, BF16 = jnp.float32, jnp.bfloat16

D_MODEL = 1024
EPS = 1e-6
HEADS = 8
NOPE, ROPE, VDIM = 64, 32, 64
QK = NOPE + ROPE
Q_LORA, KV_LORA = 768, 256
ROPE_THETA = 10000.0
ATT_CHUNK_SHIFT = 6
HG_BLOCK = 32
HG_WIDTH = 512
D_IN = 5664
D_IN_PAD = 5760
W_IN_SHARD = D_IN // 8
N_DEV = 8
LANE = 128

ADAM_LR, ADAM_B1, ADAM_B2, ADAM_EPS, ADAM_WD, ADAM_STEP = 0.001, 0.9, 0.999, 1e-08, 0.01, 10

W_IN_SEGMENTS = ((3616, 5664, 0), (1056, 1568, 2048), (3104, 3616, 2560), (1568, 3104, 3072),
                 (0, 1024, 4608), (1024, 1056, 5696))

NT = (((1,), (1,)), ((), ()))
TN = (((0,), (0,)), ((), ()))
MESH_ID = pl.DeviceIdType.MESH


def _w_in_pieces():
    out = []
    for lo, hi, dst in W_IN_SEGMENTS:
        c = lo
        while c < hi:
            p = c // W_IN_SHARD
            e = min(hi, (p + 1) * W_IN_SHARD)
            out.append((p, c - p * W_IN_SHARD, e - p * W_IN_SHARD, dst + c - lo))
            c = e
    return out


def _params(sem, vmem_mb=48):
    return pltpu.CompilerParams(dimension_semantics=sem, vmem_limit_bytes=vmem_mb * 2**20)


def _dot(a, b):
    return jnp.dot(a, b, preferred_element_type=F32)


def _dotg(a, b, dims):
    return lax.dot_general(a, b, dims, preferred_element_type=F32)


def _split2(x):
    hi = x.astype(BF16)
    return hi, (x - hi.astype(F32)).astype(BF16)


def _sel_left(m01, x):
    hi, lo = _split2(x)
    return _dot(m01, hi) + _dot(m01, lo)


def _sel_right(x, m01):
    hi, lo = _split2(x)
    return _dot(hi, m01) + _dot(lo, m01)


def _hi_lo(x):
    hi = x.astype(BF16).astype(F32)
    return hi, x - hi


def _sigmoid(x):
    return 0.5 * jnp.tanh(0.5 * x) + 0.5


def _rope(x, c, s1, s2):
    return x * c + pltpu.roll(x, 112, 1) * s1 + pltpu.roll(x, 16, 1) * s2


def _unrope(d, c, s1, s2):
    return d * c + pltpu.roll(d * s1, 16, 1) + pltpu.roll(d * s2, 112, 1)


def _my_place():
    return lax.axis_index("x"), lax.axis_index("y"), lax.axis_index("c")


def _flip(k, x, y, c):
    fx, fy, fc = (k + 1) >> 2 & 1, (k + 1) >> 1 & 1, (k + 1) & 1
    return (1 - x if fx else x), (1 - y if fy else y), (1 - c if fc else c)


def _to_all_copies(s_refs, r_refs, sems, spread):
    send_sems, recv_sems, local_sems = sems
    x, y, c = _my_place()
    me = 4 * x + 2 * y + c
    src = (lambda a, p: s_refs[a]) if spread else (lambda a, p: s_refs[a].at[p])
    local = [pltpu.make_async_copy(src(a, me), r_refs[a].at[me], local_sems.at[a]) for a in range(len(s_refs))]
    remote = []
    for k in range(N_DEV - 1):
        px, py, pc = _flip(k, x, y, c)
        for a in range(len(s_refs)):
            remote.append(pltpu.make_async_remote_copy(
                src_ref=src(a, 4 * px + 2 * py + pc), dst_ref=r_refs[a].at[me],
                send_sem=send_sems.at[7 * a + k], recv_sem=recv_sems.at[7 * a + k],
                device_id=(px, py, pc), device_id_type=MESH_ID))
    return local, remote


def _to_chips_copies(s_refs, r_refs, sems):
    send_sems, recv_sems, local_sems = sems
    x, y, c = _my_place()
    me = 2 * x + y
    local = [pltpu.make_async_copy(s_refs[a].at[me], r_refs[a].at[me], local_sems.at[a]) for a in range(len(s_refs))]
    remote = []
    for k in range(3):
        px = 1 - x if (k + 1) >> 1 & 1 else x
        py = 1 - y if (k + 1) & 1 else y
        for a in range(len(s_refs)):
            remote.append(pltpu.make_async_remote_copy(
                src_ref=s_refs[a].at[2 * px + py], dst_ref=r_refs[a].at[me],
                send_sem=send_sems.at[3 * a + k], recv_sem=recv_sems.at[3 * a + k],
                device_id=(px, py, c), device_id_type=MESH_ID))
    return local, remote


def _start_all(local, remote):
    for cp in local + remote:
        cp.start()


def _wait_all(local, remote):
    for cp in remote:
        cp.wait_recv()
    for cp in remote:
        cp.wait_send()
    for cp in local:
        cp.wait()


def _copy_sems(n, peers):
    return [pltpu.SemaphoreType.DMA((peers * n,)), pltpu.SemaphoreType.DMA((peers * n,)),
            pltpu.SemaphoreType.DMA((n,))]


ANY = pl.BlockSpec(memory_space=pl.ANY)


def _dw_in_slots(ht, dproj):
    m, k = ht.shape
    n = dproj.shape[1]
    tn, tk = 1920, 2048
    nj, nk = n // tn, k // tk
    by_tile = [[] for _ in range(nj)]
    for p, lo, hi, dst in _w_in_pieces():
        while lo < hi:
            j = dst // tn
            cnt = min(hi - lo, (j + 1) * tn - dst)
            by_tile[j].append((p, lo, lo + cnt, dst - j * tn))
            lo, dst = lo + cnt, dst + cnt

    def body(a_ref, b_ref, s_ref, acc_ref):
        j, l = pl.program_id(0), pl.program_id(1)

        @pl.when(l == 0)
        def _():
            acc_ref[...] = jnp.zeros_like(acc_ref)

        acc_ref[...] += _dot(a_ref[...], b_ref[...])

        @pl.when(l == nk - 1)
        def _():
            at = acc_ref[...].T
            for jj in range(nj):
                @pl.when(j == jj)
                def _(jj=jj):
                    for p, lo, hi, d in by_tile[jj]:
                        s_ref[p, lo:hi, :] = at[d:d + hi - lo, :].astype(BF16)

    return pl.pallas_call(
        body,
        grid=(nj, nk),
        in_specs=[pl.BlockSpec((m, tk), lambda j, l: (0, l)), pl.BlockSpec((tk, tn), lambda j, l: (l, j))],
        out_specs=pl.BlockSpec((N_DEV, W_IN_SHARD, m), lambda j, l: (0, 0, 0), pipeline_mode=pl.Buffered(1)),
        out_shape=jax.ShapeDtypeStruct((N_DEV, W_IN_SHARD, m), BF16),
        scratch_shapes=[pltpu.VMEM((m, tn), F32)],
        compiler_params=_params(("arbitrary", "arbitrary"), 56),
        name="dw_in",
    )(ht, dproj)


GP_TN = 256
GP_COLS = 5888
GP_NT = GP_COLS // GP_TN


def _gp_tile_pieces():
    tiles = [[] for _ in range(GP_NT)]
    for p, lo, hi, dst in _w_in_pieces():
        while lo < hi:
            t = dst // GP_TN
            n = min(hi - lo, (t + 1) * GP_TN - dst)
            tiles[t].append((p, lo, lo + n, dst - t * GP_TN))
            lo, dst = lo + n, dst + n
    return tiles


def _gp_tables():
    pieces = _gp_tile_pieces()
    rank_of = {None: 0, 0: 1, 1: 2, 2: 2, 4: 3, 5: 3, 3: 4, 6: 5}
    order = np.zeros((N_DEV, GP_NT), np.int32)
    waits = np.zeros((N_DEV, GP_NT), np.int32)
    for me in range(N_DEV):
        x, y, c = me >> 2 & 1, me >> 1 & 1, me & 1
        chips = [(1 - x, y), (x, 1 - y), (1 - x, 1 - y)]

        def sem_of(p):
            px, py, pc = p >> 2 & 1, p >> 1 & 1, p & 1
            if (px, py) == (x, y):
                return None if pc == c else 0
            j = chips.index((px, py))
            return 1 + j if pc == c else 4 + j

        needs = [sorted({sem_of(p) for p, _, _, _ in tile} - {None}) for tile in pieces]
        ranks = [max([rank_of[k] for k in ks], default=0) for ks in needs]
        seq = sorted(range(GP_NT), key=lambda t: (ranks[t], t))
        seen = set()
        for step, t in enumerate(seq):
            order[me, step] = t
            new = [k for k in needs[t] if k not in seen]
            for k in new:
                waits[me, step] |= 1 << k
            seen.update(new)
        assert seen == set(range(7)), (me, seen)
    return order, waits


def _gather_proj(x, g_pre, w_blk, shards):
    s = x.shape[0]
    tx = 512
    ns = len(shards)
    tile_pieces = _gp_tile_pieces()
    order_np, waits_np = _gp_tables()
    xq, yq, cq = _my_place()
    me_out = 4 * xq + 2 * yq + cq
    order = lax.dynamic_index_in_dim(jnp.asarray(order_np), me_out, 0, keepdims=False)
    waits = lax.dynamic_index_in_dim(jnp.asarray(waits_np), me_out, 0, keepdims=False)

    def body(order_ref, waits_ref, x_hbm, g_ref, wblk_hbm, *rest):
        shard_refs, (proj_ref, wt_ref, ht_hbm), got_refs = rest[:ns], rest[ns:ns + 3], rest[ns + 3:2 * ns + 3]
        recv, h_ref, wtile, xbuf, htbuf = rest[2 * ns + 3:2 * ns + 8]
        send_sems, recv_sems, misc_sems = rest[2 * ns + 8:2 * ns + 11]
        sems = rest[2 * ns + 11:]
        t = pl.program_id(0)
        x_, y_, c = _my_place()
        sibling = (x_, y_, 1 - c)
        chips = [(1 - x_, y_), (x_, 1 - y_), (1 - x_, 1 - y_)]
        idx = lambda px, py, pc: 4 * px + 2 * py + pc
        me = idx(x_, y_, c)

        def copy(k, slot, to, src=None):
            return pltpu.make_async_remote_copy(
                src_ref=recv.at[slot] if src is None else src, dst_ref=recv.at[slot],
                send_sem=send_sems.at[k], recv_sem=recv_sems.at[k], device_id=to, device_id_type=MESH_ID)

        mine = pltpu.make_async_copy(wblk_hbm, recv.at[me], misc_sems.at[0])
        first = [copy(0, me, sibling, src=wblk_hbm)] + [copy(1 + j, me, (*chips[j], c), src=wblk_hbm) for j in range(2)]
        passed = [copy(4 + j, idx(*ch, c), sibling) for j, ch in enumerate(chips)]
        onward = [copy(3, idx(*chips[0], c), (*chips[1], c)), copy(3, idx(*chips[1], c), (*chips[0], c))]
        arrivals = ([copy(0, idx(x_, y_, 1 - c), sibling)] + [copy(1 + j, idx(*ch, c), sibling) for j, ch in enumerate(chips)]
                    + [copy(4 + j, idx(*ch, 1 - c), sibling) for j, ch in enumerate(chips)])

        @pl.when(t == 0)
        def _():
            mine.start()
            for cp in first:
                cp.start()
            _start_all(*_to_all_copies(shard_refs, got_refs, sems, True))

            def load(i):
                return pltpu.make_async_copy(x_hbm.at[pl.ds(i * tx, tx), :], xbuf.at[i & 1], misc_sems.at[1 + (i & 1)])

            def store(i):
                return pltpu.make_async_copy(htbuf.at[i & 1], ht_hbm.at[:, pl.ds(i * tx, tx)], misc_sems.at[3 + (i & 1)])

            load(0).start()
            for i in range(s // tx):
                if i + 1 < s // tx:
                    load(i + 1).start()
                load(i).wait()
                xv = xbuf[i & 1]
                r = lax.rsqrt(jnp.mean(xv * xv, axis=-1, keepdims=True) + EPS)
                h = (xv * r * g_ref[...]).astype(BF16)
                h_ref[i * tx:(i + 1) * tx, :] = h
                if i >= 2:
                    store(i - 2).wait()
                htbuf[i & 1] = h.T
                store(i).start()
            for i in range(max(s // tx - 2, 0), s // tx):
                store(i).wait()
            mine.wait()

        w = waits_ref[t]
        for k in range(7):
            @pl.when((w >> k) & 1 == 1)
            def _(k=k):
                arrivals[k].wait_recv()
                if 1 <= k <= 3:
                    passed[k - 1].start()
                if 1 <= k <= 2:
                    @pl.when(c == k - 1)
                    def _():
                        onward[k - 1].start()

        tile = order_ref[t]
        for tt in range(GP_NT):
            @pl.when(tile == tt)
            def _(tt=tt):
                covered = sorted((d, d + hi - lo) for _, lo, hi, d in tile_pieces[tt])
                at = 0
                for lo_z, hi_z in covered + [(GP_TN, GP_TN)]:
                    if lo_z > at:
                        wtile[at:lo_z, :] = jnp.zeros((lo_z - at, D_MODEL), BF16)
                    at = max(at, hi_z)
                for p, lo, hi, d in tile_pieces[tt]:
                    wtile[d:d + hi - lo, :] = recv[p, lo:hi, :]

        wt = wtile[...]
        wt_ref[...] = wt
        proj_ref[...] = _dotg(h_ref[...], wt, NT)

        @pl.when(t == GP_NT - 1)
        def _():
            for cp in first + passed + onward[:1]:
                cp.wait_send()
            _wait_all(*_to_all_copies(shard_refs, got_refs, sems, True))

    grid_spec = pltpu.PrefetchScalarGridSpec(
        num_scalar_prefetch=2,
        grid=(GP_NT,),
        in_specs=[ANY, pl.BlockSpec((1, D_MODEL), lambda t, o, w: (0, 0)), ANY] + [ANY] * ns,
        out_specs=[pl.BlockSpec((s, GP_TN), lambda t, o, w: (0, o[t])),
                   pl.BlockSpec((GP_TN, D_MODEL), lambda t, o, w: (o[t], 0)), ANY] + [ANY] * ns,
        scratch_shapes=[pltpu.VMEM((N_DEV, W_IN_SHARD, D_MODEL), BF16), pltpu.VMEM((s, D_MODEL), BF16),
                        pltpu.VMEM((GP_TN, D_MODEL), BF16), pltpu.VMEM((2, tx, D_MODEL), F32),
                        pltpu.VMEM((2, D_MODEL, tx), BF16),
                        pltpu.SemaphoreType.DMA((7,)), pltpu.SemaphoreType.DMA((7,)), pltpu.SemaphoreType.DMA((5,))]
        + _copy_sems(ns, 7),
    )
    return pl.pallas_call(
        body,
        grid_spec=grid_spec,
        out_shape=[jax.ShapeDtypeStruct((s, GP_COLS), F32),jax.ShapeDtypeStruct((GP_COLS, D_MODEL), BF16),
                   jax.ShapeDtypeStruct((D_MODEL, s), BF16)]
        + [jax.ShapeDtypeStruct((N_DEV,) + b.shape, b.dtype) for b in shards],
        compiler_params=_params(("arbitrary",), 56),
        name="gather_proj",
    )(order, waits, x, g_pre, w_blk, *shards)


def _mla_prep(proj, g_q, g_kv, w_uq_p, w_kv_p, rc, rs1, rs2, casts):
    s = proj.shape[0]
    tm = 512
    nc = len(casts)
    scale = 1.0 / math.sqrt(QK)

    def body(cq_ref, ckv_ref, kpe_ref, gq_ref, gkv_ref, wuq_ref, wkv_ref, c_ref, s1_ref, s2_ref, *rest):
        cast_in, (qr_ref, kr_ref, v_ref, cqt_ref, ckvt_ref), cast_out = rest[:nc], rest[nc:nc + 5], rest[nc + 5:]

        @pl.when(pl.program_id(0) == 0)
        def _():
            for src, dst in zip(cast_in, cast_out):
                dst[...] = src[...].astype(BF16)

        cq = cq_ref[...]
        r = lax.rsqrt(jnp.mean(cq * cq, axis=-1, keepdims=True) + EPS)
        cqn = (cq * r * gq_ref[...]).astype(BF16)
        cqt_ref[...] = cqn.T
        q = _dot(cqn, wuq_ref[...])
        ckv = ckv_ref[...]
        r = lax.rsqrt(jnp.mean(ckv * ckv, axis=-1, keepdims=True) + EPS)
        ckvn = (ckv * r * gkv_ref[...]).astype(BF16)
        ckvt_ref[...] = ckvn.T
        kv = _dot(ckvn, wkv_ref[...])
        c, s1, s2 = c_ref[...], s1_ref[...], s2_ref[...]
        lane = lax.broadcasted_iota(jnp.int32, (tm, LANE), 1)
        kpe = _rope(kpe_ref[...], c, s1, s2) + jnp.where((lane == QK) | (lane == QK + 1), 1.0, 0.0)
        vone = jnp.where((lane == VDIM) | (lane == VDIM + 1), 1.0, 0.0)
        for h in range(HEADS):
            sl = slice(LANE * h, LANE * (h + 1))
            qr_ref[:, sl] = (_rope(q[:, sl], c, s1, s2) * scale).astype(BF16)
            kr_ref[:, sl] = (kv[:, sl] + kpe).astype(BF16)
            v_ref[:, sl] = (kv[:, HEADS * LANE + LANE * h:HEADS * LANE + LANE * (h + 1)] + vone).astype(BF16)

    row = lambda w, j: pl.BlockSpec((tm, w), lambda i: (i, j))
    col = lambda w: pl.BlockSpec((w, tm), lambda i: (0, i))
    full = lambda a: pl.BlockSpec(a.shape, lambda i: (0, 0))
    return pl.pallas_call(
        body,
        grid=(s // tm,),
        in_specs=[row(768, 6), row(256, 21), row(128, 44), full(g_q), full(g_kv), full(w_uq_p), full(w_kv_p),
                  row(128, 0), row(128, 0), row(128, 0)] + [full(a) for a in casts],
        out_specs=[row(1024, 0), row(1024, 0), row(1024, 0), col(768), col(256)] + [full(a) for a in casts],
        out_shape=[jax.ShapeDtypeStruct((s, 1024), BF16), jax.ShapeDtypeStruct((s, 1024), BF16),
                   jax.ShapeDtypeStruct((s, 1024), BF16), jax.ShapeDtypeStruct((768, s), BF16),
                   jax.ShapeDtypeStruct((256, s), BF16)] + [jax.ShapeDtypeStruct(a.shape, BF16) for a in casts],
        compiler_params=_params(("arbitrary",)),
        name="mla_prep",
    )(proj, proj, proj, g_q, g_kv, w_uq_p, w_kv_p, rc, rs1, rs2, *casts)


ATT_T = 512
ATT_FWD_HEADS = 4


def _chunk_mask(transposed):
    r = lax.broadcasted_iota(jnp.int32, (ATT_T, ATT_T), 0) >> ATT_CHUNK_SHIFT
    c = lax.broadcasted_iota(jnp.int32, (ATT_T, ATT_T), 1) >> ATT_CHUNK_SHIFT
    return (r <= c) if transposed else (c <= r)


def _attn_fwd(qr, kr, vp, shards):
    s = qr.shape[0]
    t = ATT_T
    g = ATT_FWD_HEADS
    ns = len(shards)

    def body(q_ref, k_ref, v_ref, *rest):
        shard_refs, (o_ref, qa_ref), got_refs = rest[:ns], rest[ns:ns + 2], rest[ns + 2:2 * ns + 2]
        sc_ref, sems = rest[2 * ns + 2], rest[2 * ns + 3:]
        qi = pl.program_id(1)

        @pl.when((pl.program_id(0) == 0) & (qi == 0))
        def _():
            _start_all(*_to_all_copies(shard_refs, got_refs, sems, True))
        lane = lax.broadcasted_iota(jnp.int32, (t, LANE), 1)
        sls = [slice(LANE * a, LANE * (a + 1)) for a in range(g)]
        qs = [q_ref[:, sl] for sl in sls]

        def scores(j):
            rows = pl.ds(pl.multiple_of(j * t, t), t)
            for a in range(g):
                sc_ref[j & 1, a] = _dotg(qs[a], k_ref[rows, sls[a]], NT)

        def step(j, carry, masked):
            rows = pl.ds(pl.multiple_of(j * t, t), t)
            out = []
            for a in range(g):
                m, acc = carry[a]
                sc = sc_ref[j & 1, a]
                if masked:
                    sc = jnp.where(_chunk_mask(False), sc, -1e30)
                m_new = jnp.maximum(m, jnp.max(sc, axis=-1, keepdims=True))
                p = jnp.exp(sc - m_new).astype(BF16)
                acc = jnp.exp(m - m_new) * acc + _dot(p, v_ref[rows, sls[a]])
                out.append((m_new, acc))
            return tuple(out)

        def loop(j, carry):
            carry = step(j, carry, False)
            scores(j + 1)
            return carry

        init = tuple((jnp.full((t, 1), -1e30, F32), jnp.zeros((t, LANE), F32)) for _ in range(g))
        scores(0)
        carry = lax.fori_loop(0, qi, loop, init)
        carry = step(qi, carry, True)
        outs = []
        for a in range(g):
            m, acc = carry[a]
            l = acc[:, VDIM:VDIM + 1]
            outs.append(acc / l)
            hi, lo_part = _hi_lo(-(m + jnp.log(l)))
            qa = jnp.where(lane == QK, hi, jnp.where(lane == QK + 1, lo_part, qs[a].astype(F32)))
            qa_ref[:, sls[a]] = qa.astype(BF16)
        for p in range(g // 2):
            o_ref[:, LANE * p:LANE * (p + 1)] = jnp.where(lane < VDIM, outs[2 * p], pltpu.roll(outs[2 * p + 1], VDIM, 1))

        @pl.when((pl.program_id(0) == HEADS // g - 1) & (qi == s // t - 1))
        def _():
            _wait_all(*_to_all_copies(shard_refs, got_refs, sems, True))

    return pl.pallas_call(
        body,
        grid=(HEADS // g, s // t),
        in_specs=[
            pl.BlockSpec((t, g * LANE), lambda h, i: (i, h)),
            pl.BlockSpec((s, g * LANE), lambda h, i: (0, h)),
            pl.BlockSpec((s, g * LANE), lambda h, i: (0, h)),
        ] + [ANY] * ns,
        out_specs=[
            pl.BlockSpec((t, g * VDIM), lambda h, i: (i, h)),
            pl.BlockSpec((t, g * LANE), lambda h, i: (i, h)),
        ] + [ANY] * ns,
        out_shape=[jax.ShapeDtypeStruct((s, 512), F32), jax.ShapeDtypeStruct((s, 1024), BF16)]
        + [jax.ShapeDtypeStruct((N_DEV,) + b.shape, b.dtype) for b in shards],
        scratch_shapes=[pltpu.VMEM((2, g, t, t), F32)] + _copy_sems(ns, 7),
        compiler_params=_params(("arbitrary", "arbitrary")),
        name="attn_fwd",
    )(qr, kr, vp, *shards)


def _attn_bwd(qa, kr, vp, dop, sends):
    s = qa.shape[0]
    t = ATT_T
    nq = s // t
    ns = len(sends)

    def body(q_ref, k_ref, v_ref, do_ref, *rest):
        send_refs, (dq_out, dk_out, dv_out) = rest[:ns], rest[ns:ns + 3]
        recv_refs = rest[ns + 3:2 * ns + 3]
        (dq_ref, dk_ref, dv_ref), sems = rest[2 * ns + 3:2 * ns + 6], rest[2 * ns + 6:]
        j = pl.program_id(1)
        sls = [slice(LANE * a, LANE * (a + 1)) for a in range(2)]

        @pl.when((pl.program_id(0) == 0) & (j == 0))
        def _():
            _start_all(*_to_all_copies(send_refs, recv_refs, sems, False))

        @pl.when(j == 0)
        def _():
            dq_ref[...] = jnp.zeros_like(dq_ref)

        dk_ref[...] = jnp.zeros_like(dk_ref)
        dv_ref[...] = jnp.zeros_like(dv_ref)
        ks = [k_ref[:, sl] for sl in sls]
        vs = [v_ref[:, sl] for sl in sls]

        def part(i, k_lo, k_n, q_lo, q_n, masked):
            rows = pl.ds(pl.multiple_of(i * t + q_lo, 256), q_n)
            keys = slice(k_lo, k_lo + k_n)
            for a in range(2):
                q = q_ref[rows, sls[a]]
                do = do_ref[rows, sls[a]]
                sc = _dotg(ks[a][keys], q, NT)
                if masked:
                    kc = lax.broadcasted_iota(jnp.int32, (k_n, q_n), 0) >> ATT_CHUNK_SHIFT
                    qc = lax.broadcasted_iota(jnp.int32, (k_n, q_n), 1) >> ATT_CHUNK_SHIFT
                    sc = jnp.where(kc <= qc, sc, -1e30)
                p = jnp.exp(sc)
                ds = (p * _dotg(vs[a][keys], do, NT)).astype(BF16)
                dv_ref[keys, sls[a]] += _dot(p.astype(BF16), do)
                dk_ref[keys, sls[a]] += _dot(ds, q)
                dq_ref[rows, sls[a]] += _dotg(ds, ks[a][keys], TN)

        half = t // 2
        part(j, 0, half, 0, t, True)
        part(j, half, half, half, half, True)

        def loop(i, c):
            part(i, 0, t, 0, t, False)
            return c

        lax.fori_loop(j + 1, nq, loop, 0)
        dk_out[...] = dk_ref[...].astype(BF16)
        dv_out[...] = dv_ref[...].astype(BF16)

        @pl.when(j == nq - 1)
        def _():
            dq_out[...] = dq_ref[...].astype(BF16)

        @pl.when((pl.program_id(0) == HEADS // 2 - 1) & (j == nq - 1))
        def _():
            _wait_all(*_to_all_copies(send_refs, recv_refs, sems, False))

    blk = pl.BlockSpec((t, 2 * LANE), lambda h, j: (j, h))
    whole = pl.BlockSpec((s, 2 * LANE), lambda h, j: (0, h))
    out = jax.ShapeDtypeStruct((s, 1024), BF16)
    return pl.pallas_call(
        body,
        grid=(HEADS // 2, nq),
        in_specs=[whole, blk, blk, whole] + [ANY] * ns,
        out_specs=[whole, blk, blk] + [ANY] * ns,
        out_shape=[out, out, out] + [jax.ShapeDtypeStruct(a.shape, a.dtype) for a in sends],
        scratch_shapes=[pltpu.VMEM((s, 2 * LANE), F32), pltpu.VMEM((t, 2 * LANE), F32),
                        pltpu.VMEM((t, 2 * LANE), F32)] + _copy_sems(ns, 7),
        compiler_params=_params(("arbitrary", "arbitrary")),
        name="attn_bwd",
    )(qa, kr, vp, dop, *sends)


HG_T = 256
HG_NC = HG_T // HG_BLOCK
HG_G = 4
GW = 64 * HG_G


def _hg_consts():
    r = jnp.arange(HG_T)[:, None]
    c = jnp.arange(HG_T)[None, :]
    same = (r // HG_BLOCK) == (c // HG_BLOCK)
    mcum = (same & (c <= r)).astype(BF16)
    mrev = (same & (c >= r)).astype(BF16)
    msum = same.astype(BF16)
    a = jnp.arange(GW) // 64
    bd = (a[:, None] == a[None, :]).astype(F32)
    return mcum, mrev, msum, bd


def _stack_heads(xg, head):
    return jnp.concatenate([jnp.where(head == h, xg, 0.0) for h in range(HG_G)], axis=0)


def _unstack_heads(r, head, t):
    out = r[(HG_G - 1) * t:]
    for h in range(HG_G - 2, -1, -1):
        out = jnp.where(head == h, r[h * t:(h + 1) * t], out)
    return out


def _compact_state(st):
    out = st[:64]
    for h in range(1, HG_G):
        out = out + st[64 * h:64 * (h + 1)]
    return out


def _expand_state(cs, head64):
    return jnp.concatenate([jnp.where(head64 == h, cs, 0.0) for h in range(HG_G)], axis=0)


def _hg_pre(hq, hf, lbl, mcum, msum):
    lb = _sigmoid(lbl[0:1, :] - lbl[1:2, :])
    sig = _sigmoid(hf)
    f = lb + (1.0 - lb) * sig
    lf = jnp.log(f)
    b = _sel_left(mcum, lf)
    big_l = _sel_left(msum, lf)
    k = 1.0 - f
    qd = hq * jnp.exp(b)
    ki = k * jnp.exp(-b)
    ke = k * jnp.exp(big_l - b)
    return lb, sig, f, b, big_l, qd, ki, ke


def _hgrn_fwd(proj, lbl):
    s = proj.shape[0]
    t = HG_T
    mcum, _, msum, bd = _hg_consts()

    def body(hq_ref, hf_ref, hi_ref, lbl_ref, mcum_ref, msum_ref, bd_ref, o_ref, sp_ref, st_ref):
        @pl.when(pl.program_id(0) == 0)
        def _():
            st_ref[...] = jnp.zeros_like(st_ref)

        mc = mcum_ref[...]
        _, _, _, _, big_l, qd, ki, ke = _hg_pre(hq_ref[...], hf_ref[...], lbl_ref[...], mc, msum_ref[...])
        el = jnp.exp(big_l)
        hi = hi_ref[...]
        head = lax.broadcasted_iota(jnp.int32, (t, GW), 1) >> 6
        mask = jnp.concatenate([mc] * HG_G, axis=0) > 0.5
        for p in range(HEADS // HG_G):
            sl = slice(GW * p, GW * (p + 1))
            vp = hi[:, sl].astype(BF16)
            qs = _stack_heads(qd[:, sl], head).astype(BF16)
            a = jnp.where(mask, _dotg(qs, ki[:, sl].astype(BF16), NT), 0.0)
            o_intra = _unstack_heads(_dot(a.astype(BF16), vp), head, t)
            qb = qd[:, sl].astype(BF16)
            kb = ke[:, sl].astype(BF16)
            st = st_ref[p]
            for c in range(HG_NC):
                rows = slice(HG_BLOCK * c, HG_BLOCK * (c + 1))
                sp_ref[c, :, sl] = _compact_state(st)
                o_ref[rows, sl] = o_intra[rows] + _dotg(qb[rows], st.astype(BF16), NT)
                u = _dotg(vp[rows], kb[rows], TN) * bd_ref[...]
                st = st * el[HG_BLOCK * c:HG_BLOCK * c + 1, sl] + u
            st_ref[p] = st

    row = lambda j: pl.BlockSpec((t, HG_WIDTH), lambda i: (i, j))
    full = lambda a: pl.BlockSpec(a.shape, lambda i: (0, 0))
    return pl.pallas_call(
        body,
        grid=(s // t,),
        in_specs=[row(6), row(7), row(8), full(lbl), full(mcum), full(msum), full(bd)],
        out_specs=[row(0), pl.BlockSpec((HG_NC, 64, HG_WIDTH), lambda i: (i, 0, 0))],
        out_shape=[jax.ShapeDtypeStruct((s, HG_WIDTH), F32),
                   jax.ShapeDtypeStruct((s // HG_BLOCK, 64, HG_WIDTH), F32)],
        scratch_shapes=[pltpu.VMEM((HEADS // HG_G, GW, GW), F32)],
        compiler_params=_params(("arbitrary",)),
        name="hgrn_fwd",
    )(proj, proj, proj, lbl, mcum, msum, bd)


def _slot_shape(name, r, c):
    return (N_DEV, r, c // N_DEV) if COL_SHARDED[name] else (N_DEV, r // N_DEV, c)


def _emit_slots(name, acc_ref, out_ref):
    r, c = acc_ref.shape
    for p in range(N_DEV):
        if COL_SHARDED[name]:
            out_ref[p] = acc_ref[:, c // N_DEV * p:c // N_DEV * (p + 1)].astype(BF16)
        else:
            out_ref[p] = acc_ref[r // N_DEV * p:r // N_DEV * (p + 1), :].astype(BF16)


def _hgrn_bwd(proj, lbl, do, sprev, dproj, pairs):
    s = proj.shape[0]
    t = HG_T
    nt = s // t
    npair = len(pairs)
    mcum, mrev, msum, bd = _hg_consts()

    def body(hq_ref, hf_ref, hi_ref, lbl_ref, do_ref, sp_ref, mcum_ref, mrev_ref, msum_ref, bd_ref,
             dproj_in, *rest):
        del dproj_in
        pair_refs, (dh_ref, dlbl_ref) = rest[:2 * npair], rest[2 * npair:2 * npair + 2]
        dw_refs, g_ref, acc_refs = rest[2 * npair + 2:3 * npair + 2], rest[3 * npair + 2], rest[3 * npair + 3:]

        @pl.when(pl.program_id(0) == 0)
        def _():
            g_ref[...] = jnp.zeros_like(g_ref)
            dlbl_ref[...] = jnp.zeros_like(dlbl_ref)
            for acc_ref in acc_refs:
                acc_ref[...] = jnp.zeros_like(acc_ref)

        for n, acc_ref in enumerate(acc_refs):
            acc_ref[...] += _dot(pair_refs[2 * n][...], pair_refs[2 * n + 1][...])

        @pl.when(pl.program_id(0) == nt - 1)
        def _():
            for (name, _, _), acc_ref, dw_ref in zip(pairs, acc_refs, dw_refs):
                _emit_slots(name, acc_ref, dw_ref)

        mc = mcum_ref[...]
        lb, sig, f, b, big_l, qd, ki, ke = _hg_pre(hq_ref[...], hf_ref[...], lbl_ref[...], mc, msum_ref[...])
        el = jnp.exp(big_l)
        hi = hi_ref[...]
        dov = do_ref[...]
        head = lax.broadcasted_iota(jnp.int32, (t, GW), 1) >> 6
        head64 = lax.broadcasted_iota(jnp.int32, (64, GW), 1) >> 6
        mask = jnp.concatenate([mc] * HG_G, axis=0) > 0.5
        dqd_parts, dke_parts, dv_parts, del_parts, dki_parts = [], [], [], [], []
        for p in range(HEADS // HG_G):
            sl = slice(GW * p, GW * (p + 1))
            vp = hi[:, sl].astype(BF16)
            qs = _stack_heads(qd[:, sl], head).astype(BF16)
            kip = ki[:, sl].astype(BF16)
            dos = _stack_heads(dov[:, sl], head).astype(BF16)
            a = jnp.where(mask, _dotg(qs, kip, NT), 0.0).astype(BF16)
            da = jnp.where(mask, _dotg(dos, vp, NT), 0.0).astype(BF16)
            r = _dot(da, kip)
            dki_parts.append(_dotg(da, qs, TN))
            qb = qd[:, sl].astype(BF16)
            kb = ke[:, sl].astype(BF16)
            dob = dov[:, sl].astype(BF16)
            g = g_ref[p]
            dqd_c, dv_c, dke_c, del_c = [], [], [], []
            for c in range(HG_NC - 1, -1, -1):
                rows = slice(HG_BLOCK * c, HG_BLOCK * (c + 1))
                gb = g.astype(BF16)
                st = _expand_state(sp_ref[c, :, sl], head64)
                dqd_c.append(_dot(dob[rows], st.astype(BF16)))
                dv_c.append(_dotg(kb[rows], gb, NT))
                dke_c.append(_dot(vp[rows], gb))
                del_c.append(jnp.broadcast_to(jnp.sum(g * st, axis=0, keepdims=True), (HG_BLOCK, GW)))
                g = g * el[HG_BLOCK * c:HG_BLOCK * c + 1, sl] + _dotg(dob[rows], qb[rows], TN) * bd_ref[...]
            g_ref[p] = g
            up = lambda parts: jnp.concatenate(parts[::-1], axis=0)
            dqd_parts.append(_unstack_heads(r, head, t) + up(dqd_c))
            dv_parts.append(_dotg(a, dos, TN) + up(dv_c))
            dke_parts.append(up(dke_c))
            del_parts.append(up(del_c))
        wide = lambda parts: jnp.concatenate(parts, axis=1)
        dqd, dke, dki, dvv, del_rows = wide(dqd_parts), wide(dke_parts), wide(dki_parts), wide(dv_parts), wide(del_parts)
        dh_ref[:, :HG_WIDTH] = (dqd * jnp.exp(b)).astype(BF16)
        dh_ref[:, 2 * HG_WIDTH:] = dvv.astype(BF16)
        dke_ke = dke * ke
        db = dqd * qd - dki * ki - dke_ke
        dl_rows = _sel_left(msum_ref[...], dke_ke) + del_rows * el
        is_last = (lax.broadcasted_iota(jnp.int32, (t, HG_WIDTH), 0) & (HG_BLOCK - 1)) == HG_BLOCK - 1
        db = db + jnp.where(is_last, dl_rows, 0.0)
        dlf = _sel_left(mrev_ref[...], db)
        dk = dki * jnp.exp(-b) + dke * jnp.exp(big_l - b)
        df = dlf / f - dk
        dh_ref[:, HG_WIDTH:2 * HG_WIDTH] = (df * (1.0 - lb) * sig * (1.0 - sig)).astype(BF16)
        dlb = jnp.sum(df * (1.0 - sig), axis=0, keepdims=True) * lb * (1.0 - lb)
        dlbl_ref[0:1, :] += dlb
        dlbl_ref[1:2, :] -= dlb

    rrow = lambda j: pl.BlockSpec((t, HG_WIDTH), lambda i: (nt - 1 - i, j))
    full = lambda a: pl.BlockSpec(a.shape, lambda i: (0, 0))
    pair_specs, dw_specs, dw_shapes, accs = [], [], [], []
    for name, at, b in pairs:
        pair_specs += [pl.BlockSpec((at.shape[0], t), lambda i: (0, i)), pl.BlockSpec((t, b.shape[1]), lambda i: (i, 0))]
        shape = _slot_shape(name, at.shape[0], b.shape[1])
        dw_specs.append(pl.BlockSpec(shape, lambda i: (0, 0, 0)))
        dw_shapes.append(jax.ShapeDtypeStruct(shape, BF16))
        accs.append(pltpu.VMEM((at.shape[0], b.shape[1]), F32))
    return pl.pallas_call(
        body,
        grid=(nt,),
        in_specs=[rrow(6), rrow(7), rrow(8), full(lbl), rrow(0),
                  pl.BlockSpec((HG_NC, 64, HG_WIDTH), lambda i: (nt - 1 - i, 0, 0)),
                  full(mcum), full(mrev), full(msum), full(bd), pl.BlockSpec(memory_space=pl.ANY)] + pair_specs,
        out_specs=[pl.BlockSpec((t, 3 * HG_WIDTH), lambda i: (nt - 1 - i, 2)),
                   pl.BlockSpec((2, HG_WIDTH), lambda i: (0, 0))] + dw_specs,
        out_shape=[jax.ShapeDtypeStruct(dproj.shape, BF16), jax.ShapeDtypeStruct((2, HG_WIDTH), F32)] + dw_shapes,
        input_output_aliases={10: 0},
        scratch_shapes=[pltpu.VMEM((HEADS // HG_G, GW, GW), F32)] + accs,
        compiler_params=_params(("arbitrary",)),
        name="hgrn_bwd",
    )(proj, proj, proj, lbl, do, sprev, mcum, mrev, msum, bd, dproj, *[a for pair in pairs for a in pair[1:]])


def _tail(x, tgt, proj, attn, o, w_a, w_b, w_out, w_at, w_bt, w_outt, b_gate, g_post, gh):
    s = x.shape[0]
    tm = 256
    ones64 = (jnp.arange(HG_WIDTH)[:, None] // 64 == jnp.arange(HG_WIDTH)[None, :] // 64).astype(BF16)
    weights = (w_a, w_b, w_out, w_at, w_bt, w_outt)

    def body(x_ref, t_ref, ml_ref, ga_ref, gb_ref, at_ref, o_ref, *rest):
        w_hbm, (bg_ref, gp_ref, gh_ref, ones_ref) = rest[:6], rest[6:10]
        (dout_ref, dpj_ref, dop_ref, do_ref, mt_ref, dy_ref, yat_ref, dya_ref, ybt_ref, dyb_ref,
         loss_ref, dgp_ref, dbg_ref, dgh_ref) = rest[10:24]
        (wa_ref, wb_ref, wo_ref, wat_ref, wbt_ref, wot_ref), w_sem = rest[24:30], rest[30]

        @pl.when(pl.program_id(0) == 0)
        def _():
            loads = [pltpu.make_async_copy(src, dst, w_sem.at[k])
                     for k, (src, dst) in enumerate(zip(w_hbm, rest[24:30]))]
            _start_all(loads, [])
            loss_ref[...] = jnp.zeros_like(loss_ref)
            dgp_ref[...] = jnp.zeros_like(dgp_ref)
            dbg_ref[...] = jnp.zeros_like(dbg_ref)
            dgh_ref[...] = jnp.zeros_like(dgh_ref)
            _wait_all(loads, [])

        ones = ones_ref[...]
        gate_a = ga_ref[...]
        sa = _sigmoid(gate_a)
        silu_a = gate_a * sa
        attn_v = at_ref[...]
        ya_in = attn_v * silu_a
        ov = o_ref[...]
        ro = lax.rsqrt(_sel_right(ov * ov, ones) * (1.0 / 64.0) + EPS)
        ohat = ov * ro
        ghv = gh_ref[...]
        on = ohat * ghv
        gate_b = gb_ref[...]
        sb = _sigmoid(gate_b)
        silu_b = gate_b * sb
        yb_in = on * silu_b
        ya_bf = ya_in.astype(BF16)
        yb_bf = yb_in.astype(BF16)
        yat_ref[...] = ya_bf.T
        ybt_ref[...] = yb_bf.T
        y_a = _dot(ya_bf, wa_ref[...])
        y_b = _dot(yb_bf, wb_ref[...])
        gts = _sigmoid(ml_ref[...] + bg_ref[...])
        g_a = gts[:, :D_MODEL]
        g_b = gts[:, D_MODEL:]
        m_bf = (g_a * y_a + g_b * y_b).astype(BF16)
        mt_ref[...] = m_bf.T
        y = _dot(m_bf, wo_ref[...])
        r1 = lax.rsqrt(jnp.mean(y * y, axis=-1, keepdims=True) + EPS)
        yn = y * r1
        gp = gp_ref[...]
        e = x_ref[...] + yn * gp - t_ref[...]
        loss_ref[...] += jnp.sum(e * e, axis=0, keepdims=True)
        dout = e * (1.0 / D_MODEL)
        dout_ref[...] = dout
        dgp_ref[...] += jnp.sum(dout * yn, axis=0, keepdims=True)
        dyn = dout * gp
        dy = r1 * (dyn - yn * jnp.mean(dyn * yn, axis=-1, keepdims=True))
        dy_bf = dy.astype(BF16)
        dy_ref[...] = dy_bf
        dm = _dot(dy_bf, wot_ref[...])
        dml_a = dm * y_a * g_a * (1.0 - g_a)
        dml_b = dm * y_b * g_b * (1.0 - g_b)
        dpj_ref[:, :D_MODEL] = dml_a.astype(BF16)
        dpj_ref[:, D_MODEL:2 * D_MODEL] = dml_b.astype(BF16)
        dbg_ref[:, :D_MODEL] += jnp.sum(dml_a, axis=0, keepdims=True)
        dbg_ref[:, D_MODEL:] += jnp.sum(dml_b, axis=0, keepdims=True)
        dya_bf = (dm * g_a).astype(BF16)
        dyb_bf = (dm * g_b).astype(BF16)
        dya_ref[...] = dya_bf
        dyb_ref[...] = dyb_bf
        dya_in = _dot(dya_bf, wat_ref[...])
        dyb_in = _dot(dyb_bf, wbt_ref[...])
        dattn = dya_in * silu_a
        delta = _sel_right(dattn * attn_v, ones)
        lane = lax.broadcasted_iota(jnp.int32, (tm, LANE), 1)
        for p in range(HEADS // 2):
            sl = slice(LANE * p, LANE * (p + 1))
            xs = (dattn[:, sl], pltpu.roll(dattn[:, sl], VDIM, 1))
            nds = (-pltpu.roll(delta[:, sl], VDIM, 1), -delta[:, sl])
            for a in range(2):
                hi, lo_part = _hi_lo(nds[a])
                blk = jnp.where(lane < VDIM, xs[a], jnp.where(lane == VDIM, hi, jnp.where(lane == VDIM + 1, lo_part, 0.0)))
                dop_ref[:, LANE * (2 * p + a):LANE * (2 * p + a + 1)] = blk.astype(BF16)
        dpj_ref[:, 2 * D_MODEL:2 * D_MODEL + HG_WIDTH] = (
            dya_in * attn_v * (sa * (1.0 + gate_a * (1.0 - sa)))).astype(BF16)
        don = dyb_in * silu_b
        dpj_ref[:, 2 * D_MODEL + HG_WIDTH:] = (dyb_in * on * (sb * (1.0 + gate_b * (1.0 - sb)))).astype(BF16)
        dgh_ref[...] += jnp.sum(don * ohat, axis=0, keepdims=True)
        dohat = don * ghv
        do_ref[...] = (ro * (dohat - ohat * (_sel_right(dohat * ohat, ones) * (1.0 / 64.0)))).astype(BF16)

    row = lambda w, j: pl.BlockSpec((tm, w), lambda i: (i, j))
    col = lambda w: pl.BlockSpec((w, tm), lambda i: (0, i))
    full = lambda a: pl.BlockSpec(a.shape, lambda i: (0, 0))
    acc = lambda w: pl.BlockSpec((1, w), lambda i: (0, 0))
    sds = lambda w, dt: jax.ShapeDtypeStruct((s, w), dt)
    sdt = lambda w: jax.ShapeDtypeStruct((w, s), BF16)
    return pl.pallas_call(
        body,
        grid=(s // tm,),
        in_specs=[row(1024, 0), row(1024, 0), row(2048, 0), row(512, 4), row(512, 5), row(512, 0), row(512, 0)]
        + [ANY] * 6 + [full(b_gate), full(g_post), full(gh), full(ones64)],
        out_specs=[row(1024, 0), row(3072, 0), row(1024, 0), row(512, 0),
                   col(1024), row(1024, 0), col(512), row(1024, 0), col(512), row(1024, 0),
                   acc(1024), acc(1024), acc(2048), acc(512)],
        out_shape=[sds(1024, F32), sds(D_IN_PAD, BF16), sds(1024, BF16), sds(512, BF16),
                   sdt(1024), sds(1024, BF16), sdt(512), sds(1024, BF16), sdt(512), sds(1024, BF16),
                   jax.ShapeDtypeStruct((1, 1024), F32), jax.ShapeDtypeStruct((1, 1024), F32),
                   jax.ShapeDtypeStruct((1, 2048), F32), jax.ShapeDtypeStruct((1, 512), F32)],
        scratch_shapes=[pltpu.VMEM(a.shape, BF16) for a in weights] + [pltpu.SemaphoreType.DMA((6,))],
        compiler_params=_params(("arbitrary",), 56),
        name="tail",
    )(x, tgt, proj, proj, proj, attn, o, *weights, b_gate, g_post, gh, ones64)


def _mla_bwd(proj, dqr, dkr, dv, g_q, g_kv, w_uq_pt, w_kv_pt, rc, rs1, rs2, cqt, ckvt, dproj):
    assert HEADS == N_DEV
    s = proj.shape[0]
    tm = 512
    scale = 1.0 / math.sqrt(QK)

    def body(cq_ref, ckv_ref, dqr_ref, dkr_ref, dv_ref, gq_ref, gkv_ref, wuqt_ref, wkvt_ref, c_ref, s1_ref, s2_ref,
             cqt_ref, ckvt_ref, dproj_in, dc_ref, dgq_ref, dgkv_ref, uq_slots, ukv_slots,
             dqf_ref, dkvf_ref, dwuq_ref, dwkv_ref):
        del dproj_in

        @pl.when(pl.program_id(0) == 0)
        def _():
            dgq_ref[...] = jnp.zeros_like(dgq_ref)
            dgkv_ref[...] = jnp.zeros_like(dgkv_ref)
            dwuq_ref[...] = jnp.zeros_like(dwuq_ref)
            dwkv_ref[...] = jnp.zeros_like(dwkv_ref)

        c, s1, s2 = c_ref[...], s1_ref[...], s2_ref[...]
        lane = lax.broadcasted_iota(jnp.int32, (tm, LANE), 1)
        ksum = jnp.zeros((tm, LANE), F32)
        for h in range(HEADS):
            sl = slice(LANE * h, LANE * (h + 1))
            dqf_ref[:, sl] = (_unrope(dqr_ref[:, sl], c, s1, s2) * scale).astype(BF16)
            dkh = dkr_ref[:, sl]
            ksum = ksum + dkh
            dkvf_ref[:, sl] = jnp.where(lane < NOPE, dkh, 0.0).astype(BF16)
            dkvf_ref[:, HEADS * LANE + LANE * h:HEADS * LANE + LANE * (h + 1)] = jnp.where(
                lane < VDIM, dv_ref[:, sl], 0.0).astype(BF16)
        dkpe = _unrope(ksum, c, s1, s2)
        dc_ref[:, Q_LORA + KV_LORA:] = jnp.where((lane >= NOPE) & (lane < QK), dkpe, 0.0).astype(BF16)
        dqf, dkvf = dqf_ref[...], dkvf_ref[...]
        dwuq_ref[...] += _dot(cqt_ref[...], dqf)
        dwkv_ref[...] += _dot(ckvt_ref[...], dkvf)
        dcqn = _dot(dqf, wuqt_ref[...])
        dckvn = _dot(dkvf, wkvt_ref[...])
        for x_ref, g_ref, dn, cols, dg_ref in ((cq_ref, gq_ref, dcqn, slice(0, Q_LORA), dgq_ref),
                                               (ckv_ref, gkv_ref, dckvn, slice(Q_LORA, Q_LORA + KV_LORA), dgkv_ref)):
            xv = x_ref[...]
            r = lax.rsqrt(jnp.mean(xv * xv, axis=-1, keepdims=True) + EPS)
            xh = xv * r
            dg_ref[...] += jnp.sum(dn * xh, axis=0, keepdims=True)
            dh = dn * g_ref[...]
            dc_ref[:, cols] = (r * (dh - xh * jnp.mean(dh * xh, axis=-1, keepdims=True))).astype(BF16)

        @pl.when(pl.program_id(0) == s // tm - 1)
        def _():
            ur = Q_LORA // N_DEV
            for p in range(N_DEV):
                uq_slots[p] = jnp.concatenate(
                    [dwuq_ref[ur * p:ur * (p + 1), LANE * h:LANE * h + QK] for h in range(HEADS)], axis=1).astype(BF16)
                ukv_slots[p] = jnp.concatenate(
                    [dwkv_ref[:, LANE * p:LANE * p + NOPE],
                     dwkv_ref[:, LANE * (HEADS + p):LANE * (HEADS + p) + VDIM]], axis=1).astype(BF16)

    row = lambda w, j: pl.BlockSpec((tm, w), lambda i: (i, j))
    full = lambda a: pl.BlockSpec(a.shape, lambda i: (0, 0))
    acc = lambda w: pl.BlockSpec((1, w), lambda i: (0, 0))
    col = lambda w: pl.BlockSpec((w, tm), lambda i: (0, i))
    whole = lambda shape: pl.BlockSpec(shape, lambda i: (0, 0, 0))
    uq_shape = (N_DEV, Q_LORA // N_DEV, HEADS * QK)
    ukv_shape = (N_DEV, KV_LORA, NOPE + VDIM)
    return pl.pallas_call(
        body,
        grid=(s // tm,),
        in_specs=[row(768, 6), row(256, 21), row(1024, 0), row(1024, 0), row(1024, 0), full(g_q), full(g_kv),
                  full(w_uq_pt), full(w_kv_pt), row(128, 0), row(128, 0), row(128, 0), col(Q_LORA), col(KV_LORA),
                  pl.BlockSpec(memory_space=pl.ANY)],
        out_specs=[row(1152, 4), acc(768), acc(256), whole(uq_shape), whole(ukv_shape)],
        out_shape=[jax.ShapeDtypeStruct(dproj.shape, BF16),
                   jax.ShapeDtypeStruct((1, 768), F32), jax.ShapeDtypeStruct((1, 256), F32),
                   jax.ShapeDtypeStruct(uq_shape, BF16), jax.ShapeDtypeStruct(ukv_shape, BF16)],
        input_output_aliases={14: 0},
        scratch_shapes=[pltpu.VMEM((tm, HEADS * LANE), BF16), pltpu.VMEM((tm, 2 * HEADS * LANE), BF16),
                        pltpu.VMEM((Q_LORA, HEADS * LANE), F32), pltpu.VMEM((KV_LORA, 2 * HEADS * LANE), F32)],
        compiler_params=_params(("arbitrary",)),
        name="mla_bwd",
    )(proj, proj, dqr, dkr, dv, g_q, g_kv, w_uq_pt, w_kv_pt, rc, rs1, rs2, cqt, ckvt, dproj)


def _dh_dx(dproj, w_in_pt, x, dout, g_pre, sends):
    s, k = dproj.shape
    tm = 256
    ns, ni = len(sends), s // tm

    def body(dp_ref, w_ref, x_ref, dout_ref, g_ref, *rest):
        send_refs, (dx_ref, dg_ref) = rest[:ns], rest[ns:ns + 2]
        recv_refs, sems = rest[ns + 2:2 * ns + 2], rest[2 * ns + 2:]

        @pl.when(pl.program_id(0) == 0)
        def _():
            _start_all(*_to_chips_copies(send_refs, recv_refs, sems))
            dg_ref[...] = jnp.zeros_like(dg_ref)

        dh = _dot(dp_ref[...], w_ref[...])
        xv = x_ref[...]
        r = lax.rsqrt(jnp.mean(xv * xv, axis=-1, keepdims=True) + EPS)
        xh = xv * r
        dg_ref[...] += jnp.sum(dh * xh, axis=0, keepdims=True)
        dxh = dh * g_ref[...]
        dx_ref[...] = dout_ref[...] + r * (dxh - xh * jnp.mean(dxh * xh, axis=-1, keepdims=True))

        @pl.when(pl.program_id(0) == ni - 1)
        def _():
            _wait_all(*_to_chips_copies(send_refs, recv_refs, sems))

    row = lambda w: pl.BlockSpec((tm, w), lambda i: (i, 0))
    return pl.pallas_call(
        body,
        grid=(ni,),
        in_specs=[row(k), pl.BlockSpec((k, D_MODEL), lambda i: (0, 0)), row(D_MODEL), row(D_MODEL),
                  pl.BlockSpec((1, D_MODEL), lambda i: (0, 0))] + [ANY] * ns,
        out_specs=[row(D_MODEL), pl.BlockSpec((1, D_MODEL), lambda i: (0, 0))] + [ANY] * ns,
        out_shape=[jax.ShapeDtypeStruct((s, D_MODEL), F32), jax.ShapeDtypeStruct((1, D_MODEL), F32)]
        + [jax.ShapeDtypeStruct(a.shape, a.dtype) for a in sends],
        scratch_shapes=_copy_sems(ns, 3),
        compiler_params=_params(("arbitrary",)),
        name="dh_dx",
    )(dproj, w_in_pt, x, dout, g_pre, *sends)


def _pair_reduce(slots):
    n = len(slots)
    half = [(N_DEV // 2,) + a.shape[1:] for a in slots]

    def body(*refs):
        s_refs, o_refs = refs[:n], refs[n:2 * n]
        mine, got = refs[2 * n:3 * n], refs[3 * n:4 * n]
        send_sems, recv_sems, local_sems, out_sems = refs[4 * n:]
        x, y, c = _my_place()
        copies, loads, stores = [], [], []
        for q in range(N_DEV // 2):
            for a in range(n):
                copies.append(pltpu.make_async_remote_copy(
                    src_ref=s_refs[a].at[2 * q + 1 - c], dst_ref=got[a].at[q],
                    send_sem=send_sems.at[4 * a + q], recv_sem=recv_sems.at[4 * a + q],
                    device_id=(x, y, 1 - c), device_id_type=MESH_ID))
                loads.append(pltpu.make_async_copy(s_refs[a].at[2 * q + c], mine[a].at[q], local_sems.at[4 * a + q]))
                stores.append(pltpu.make_async_copy(mine[a].at[q], o_refs[a].at[q], out_sems.at[4 * a + q]))
        _start_all(loads, copies)
        k = 0
        for q in range(N_DEV // 2):
            for a in range(n):
                loads[k].wait()
                copies[k].wait_recv()
                mine[a][q] = (mine[a][q].astype(F32) + got[a][q].astype(F32)).astype(mine[a].dtype)
                stores[k].start()
                k += 1
        for cp in copies:
            cp.wait_send()
        for cp in stores:
            cp.wait()

    vm = lambda: [pltpu.VMEM(h, a.dtype) for h, a in zip(half, slots)]
    return pl.pallas_call(
        body,
        in_specs=[ANY] * n,
        out_specs=[ANY] * n,
        out_shape=[jax.ShapeDtypeStruct(h, a.dtype) for h, a in zip(half, slots)],
        scratch_shapes=vm() + vm() + [pltpu.SemaphoreType.DMA((4 * n,)), pltpu.SemaphoreType.DMA((4 * n,)),
                                      pltpu.SemaphoreType.DMA((4 * n,)), pltpu.SemaphoreType.DMA((4 * n,))],
        compiler_params=pltpu.CompilerParams(vmem_limit_bytes=48 * 2**20),
        name="pair_reduce",
    )(*slots)


def _rope_tables(s):
    inv = (np.float32(ROPE_THETA) ** (-np.arange(0, ROPE, 2, dtype=np.float32) / np.float32(ROPE))).astype(np.float32)
    ang = (np.arange(s, dtype=np.float32)[:, None] * inv[None, :]).astype(np.float32)
    cos, sin = jnp.asarray(np.cos(ang.astype(np.float64)), F32), jnp.asarray(np.sin(ang.astype(np.float64)), F32)
    z = lambda w: jnp.zeros((s, w), F32)
    rc = jnp.concatenate([jnp.ones((s, NOPE), F32), cos, cos, z(32)], axis=1)
    rs1 = jnp.concatenate([z(NOPE), -sin, z(16), z(32)], axis=1)
    rs2 = jnp.concatenate([z(NOPE), z(16), sin, z(32)], axis=1)
    return rc, rs1, rs2


def _step(x, tgt, w_blk, shards, g_pre, b_gate, g_q, g_kv, lbl, g_hgrn, g_post):
    s = x.shape[0]
    rc, rs1, rs2 = _rope_tables(s)
    gh = jnp.tile(g_hgrn, (1, HEADS))

    proj, w_in_pt, ht, *got = _gather_proj(x, g_pre, w_blk, shards[:2])
    w_uq, w_ukv = (_from_slots(n, g) for n, g in zip(MATS[:2], got))
    w_uq_p = jnp.pad(w_uq.reshape(Q_LORA, HEADS, QK), ((0, 0), (0, 0), (0, LANE - QK))).reshape(Q_LORA, HEADS * LANE)
    kv3 = w_ukv.reshape(KV_LORA, HEADS, NOPE + VDIM)
    pad64 = lambda t: jnp.pad(t, ((0, 0), (0, 0), (0, LANE - 64))).reshape(KV_LORA, HEADS * LANE)
    w_kv_p = jnp.concatenate([pad64(kv3[:, :, :NOPE]), pad64(kv3[:, :, NOPE:])], axis=1)

    qr, kr, v, cqt, ckvt, *later_shards = _mla_prep(proj, g_q, g_kv, w_uq_p, w_kv_p, rc, rs1, rs2, shards[2:])
    attn, qa, *got = _attn_fwd(qr, kr, v, later_shards)
    w_a, w_b, w_out = (_from_slots(n, g) for n, g in zip(MATS[2:], got))
    o, sprev = _hgrn_fwd(proj, lbl)
    (dout, dproj, dop, do, mt, dy_bf, yat, dya_bf, ybt, dyb_bf,
     loss_vec, dg_post, db_gate, dgh) = _tail(x, tgt, proj, attn, o, w_a, w_b, w_out, w_a.T, w_b.T, w_out.T,
                                               b_gate, g_post, gh)
    dproj, dlbl, *early = _hgrn_bwd(proj, lbl, do, sprev, dproj,
                                    [("w_branch_a", yat, dya_bf), ("w_branch_b", ybt, dyb_bf), ("w_out", mt, dy_bf)])
    dqr, dkr, dv, *early_recv = _attn_bwd(qa, kr, v, dop, early)
    dproj, dg_q, dg_kv, dw_uq_slots, dw_ukv_slots = _mla_bwd(proj, dqr, dkr, dv, g_q, g_kv, w_uq_p.T, w_kv_p.T,
                                                             rc, rs1, rs2, cqt, ckvt, dproj)

    dw_in_slots = _dw_in_slots(ht, dproj)
    late = _pair_reduce([dw_in_slots, dw_uq_slots, dw_ukv_slots])
    dx, dg_pre, *late_recv = _dh_dx(dproj, w_in_pt, x, dout, g_pre, late)

    g_sum = _vectors_sum(dg_pre, db_gate, dg_q, dg_kv, dlbl, dgh, dg_post, loss_vec)
    return dx, late_recv[0], dict(zip(MATS, late_recv[1:] + early_recv)), g_sum


def _adamw(g, w, m, v):
    c1 = 1.0 / (1.0 - ADAM_B1 ** ADAM_STEP)
    c2 = 1.0 / (1.0 - ADAM_B2 ** ADAM_STEP)
    nm = ADAM_B1 * m + (1.0 - ADAM_B1) * g
    nv = ADAM_B2 * v + (1.0 - ADAM_B2) * (g * g)
    d = -ADAM_LR * ((nm * c1) / (jnp.sqrt(nv * c2) + ADAM_EPS) + ADAM_WD * w)
    return d, nm, nv


def _sum8(r_ref):
    g = r_ref[0].astype(F32)
    for k in range(1, r_ref.shape[0]):
        g = g + r_ref[k].astype(F32)
    return g


def _sum_adamw_w_in(recv, w, m, v):
    rows, _, cols = w.shape
    tc = 256
    nc = cols // tc

    def body(r_ref, w_hbm, m_hbm, v_hbm, g_hbm, d_hbm, nm_hbm, nv_hbm, ins, outs, in_sems, out_sems):
        i = pl.program_id(0)
        slot = i & 1
        cols_of = lambda step: pl.ds(pl.multiple_of(step * tc, tc), tc)

        def load(k, step, sl):
            return pltpu.make_async_copy((w_hbm, m_hbm, v_hbm)[k].at[:, 0, cols_of(step)], ins.at[sl, k],
                                         in_sems.at[sl, k])

        def store(k, step, sl):
            return pltpu.make_async_copy(outs.at[sl, k], (g_hbm, d_hbm, nm_hbm, nv_hbm)[k].at[:, 0, cols_of(step)],
                                         out_sems.at[sl, k])

        @pl.when(i == 0)
        def _():
            for k in range(3):
                load(k, 0, 0).start()

        @pl.when(i + 1 < nc)
        def _():
            for k in range(3):
                load(k, i + 1, 1 - slot).start()

        @pl.when(i >= 2)
        def _():
            for k in range(4):
                store(k, i - 2, slot).wait()

        for k in range(3):
            load(k, i, slot).wait()
        g = _sum8(r_ref)
        d, nm, nv = _adamw(g, ins[slot, 0], ins[slot, 1], ins[slot, 2])
        for k, val in enumerate((g, d, nm, nv)):
            outs[slot, k] = val
        for k in range(4):
            store(k, i, slot).start()

        @pl.when(i == nc - 1)
        def _():
            for k in range(4):
                store(k, i, slot).wait()
            if nc >= 2:
                for k in range(4):
                    store(k, i - 1, 1 - slot).wait()

    out = jax.ShapeDtypeStruct((rows, 1, cols), F32)
    return pl.pallas_call(
        body,
        grid=(nc,),
        in_specs=[pl.BlockSpec((recv.shape[0], rows, tc), lambda i: (0, 0, i)), ANY, ANY, ANY],
        out_specs=[ANY, ANY, ANY, ANY],
        out_shape=[out, out, out, out],
        scratch_shapes=[pltpu.VMEM((2, 3, rows, tc), F32), pltpu.VMEM((2, 4, rows, tc), F32),
                        pltpu.SemaphoreType.DMA((2, 3)), pltpu.SemaphoreType.DMA((2, 4))],
        compiler_params=_params(("arbitrary",)),
        name="sum_adamw_w_in",
    )(recv, w, m, v)


def _sum_adamw_whole(recvs, ws, ms, vs):
    n = len(ws)

    def body(*refs):
        r_refs, w_refs, m_refs, v_refs = refs[:n], refs[n:2 * n], refs[2 * n:3 * n], refs[3 * n:4 * n]
        outs = refs[4 * n:]
        for a in range(n):
            g = _sum8(r_refs[a])
            d, nm, nv = _adamw(g, w_refs[a][...], m_refs[a][...], v_refs[a][...])
            outs[a][...] = g
            outs[n + a][...] = d
            outs[2 * n + a][...] = nm
            outs[3 * n + a][...] = nv

    shapes = [jax.ShapeDtypeStruct(w.shape, F32) for w in ws]
    res = pl.pallas_call(
        body,
        out_shape=shapes * 4,
        compiler_params=pltpu.CompilerParams(vmem_limit_bytes=48 * 2**20),
        name="sum_adamw_mats",
    )(*recvs, *ws, *ms, *vs)
    return res[:n], res[n:2 * n], res[2 * n:3 * n], res[3 * n:]


SMALL = ("g_pre", "b_gate", "g_q", "g_kv", "lb_logits", "g_hgrn", "g_post")
SMALL_SHAPE = dict(g_pre=(1, 1024), b_gate=(1, 2048), g_q=(1, 768), g_kv=(1, 256), lb_logits=(2, 512),
                   g_hgrn=(1, 64), g_post=(1, 1024))


def _vectors_sum(dg_pre, db_gate, dg_q, dg_kv, dlbl, dgh, dg_post, loss_vec):
    def body(gpre_ref, bg_ref, gq_ref, gkv_ref, lbl_ref, gh_ref, gpost_ref, loss_ref, out_ref, mine, got,
             send_sems, recv_sems):
        mine[...] = jnp.zeros_like(mine)
        mine[0:1, :] = gpre_ref[...]
        mine[1:2, :] = bg_ref[:, :1024]
        mine[2:3, :] = bg_ref[:, 1024:]
        mine[3:4, :Q_LORA] = gq_ref[...]
        mine[4:5, :KV_LORA] = gkv_ref[...]
        loss = (0.5 / D_MODEL) * jnp.sum(loss_ref[...], axis=-1, keepdims=True)
        mine[4:5, KV_LORA:] = jnp.broadcast_to(loss, (1, 1024 - KV_LORA))
        mine[5:6, :HG_WIDTH] = lbl_ref[0:1, :]
        mine[5:6, HG_WIDTH:] = lbl_ref[1:2, :]
        gh = gh_ref[...]
        fold = gh[:, :VDIM]
        for h in range(1, HEADS):
            fold = fold + gh[:, VDIM * h:VDIM * (h + 1)]
        mine[6:7, :VDIM] = fold
        mine[7:8, :] = gpost_ref[...]
        x, y, c = _my_place()
        me = 4 * x + 2 * y + c
        got[me] = mine[...]
        copies = [pltpu.make_async_remote_copy(
            src_ref=mine, dst_ref=got.at[me], send_sem=send_sems.at[k], recv_sem=recv_sems.at[k],
            device_id=_flip(k, x, y, c), device_id_type=MESH_ID) for k in range(N_DEV - 1)]
        _start_all([], copies)
        _wait_all([], copies)
        out_ref[...] = _sum8(got)

    return pl.pallas_call(
        body,
        out_shape=jax.ShapeDtypeStruct((8, 1024), F32),
        scratch_shapes=[pltpu.VMEM((8, 1024), F32), pltpu.VMEM((N_DEV, 8, 1024), F32),
                        pltpu.SemaphoreType.DMA((7,)), pltpu.SemaphoreType.DMA((7,))],
        name="vectors_sum",
    )(dg_pre, db_gate, dg_q, dg_kv, dlbl, dgh, dg_post, loss_vec)


def _vectors_adamw(g_sum, ws, ms, vs):
    n = len(SMALL)

    def body(g_ref, *refs):
        w_refs, m_refs, v_refs = refs[:n], refs[n:2 * n], refs[2 * n:3 * n]
        loss_ref, outs = refs[3 * n], refs[3 * n + 1:]
        g = g_ref[...]
        loss_ref[...] = g[4:5, KV_LORA:KV_LORA + 1]
        grads = (g[0:1, :], jnp.concatenate([g[1:2, :], g[2:3, :]], axis=1), g[3:4, :Q_LORA], g[4:5, :KV_LORA],
                 jnp.concatenate([g[5:6, :HG_WIDTH], g[5:6, HG_WIDTH:]], axis=0), g[6:7, :VDIM], g[7:8, :])
        for a in range(n):
            d, nm, nv = _adamw(grads[a], w_refs[a][...], m_refs[a][...], v_refs[a][...])
            outs[a][...] = grads[a]
            outs[n + a][...] = d
            outs[2 * n + a][...] = nm
            outs[3 * n + a][...] = nv

    shapes = [jax.ShapeDtypeStruct(SMALL_SHAPE[k], F32) for k in SMALL]
    res = pl.pallas_call(
        body,
        out_shape=[jax.ShapeDtypeStruct((1, 1), F32)] + shapes * 4,
        name="vectors_adamw",
    )(g_sum, *ws, *ms, *vs)
    return res[0], res[1:n + 1], res[n + 1:2 * n + 1], res[2 * n + 1:3 * n + 1], res[3 * n + 1:]


MATS = ("w_uq", "w_ukv", "w_branch_a", "w_branch_b", "w_out")
COL_SHARDED = dict(w_uq=False, w_ukv=True, w_branch_a=True, w_branch_b=True, w_out=False)
ORDER = ("g_pre", "w_in", "b_gate", "g_q", "w_uq", "g_kv", "w_ukv", "lb_logits", "g_hgrn",
         "w_branch_a", "w_branch_b", "w_out", "g_post")


def _from_slots(name, slots):
    _, r, c = slots.shape
    if COL_SHARDED[name]:
        return slots.transpose(1, 0, 2).reshape(r, N_DEV * c)
    return slots.reshape(N_DEV * r, c)


def kernel(x, g_pre, w_in, b_gate, g_q, w_uq, g_kv, w_ukv, lb_logits, g_hgrn, w_branch_a, w_branch_b, w_out, g_post, loss_target, m_g_pre, m_w_in, m_b_gate, m_g_q, m_w_uq, m_g_kv, m_w_ukv, m_lb_logits, m_g_hgrn, m_w_branch_a, m_w_branch_b, m_w_out, m_g_post, v_g_pre, v_w_in, v_b_gate, v_g_q, v_w_uq, v_g_kv, v_w_ukv, v_lb_logits, v_g_hgrn, v_w_branch_a, v_w_branch_b, v_w_out, v_g_post):
    rows3 = lambda a: jnp.transpose(a, (2, 0, 1))
    w = dict(w_in=rows3(w_in), w_uq=w_uq[0], w_ukv=w_ukv[0], w_branch_a=w_branch_a[0], w_branch_b=w_branch_b[0],
             w_out=w_out[0], g_pre=g_pre, b_gate=b_gate, g_q=g_q, g_kv=g_kv, lb_logits=lb_logits, g_hgrn=g_hgrn,
             g_post=g_post)
    mom = dict(w_in=rows3(m_w_in), w_uq=m_w_uq[0], w_ukv=m_w_ukv[0], w_branch_a=m_w_branch_a[0],
               w_branch_b=m_w_branch_b[0], w_out=m_w_out[0], g_pre=m_g_pre, b_gate=m_b_gate, g_q=m_g_q, g_kv=m_g_kv,
               lb_logits=m_lb_logits, g_hgrn=m_g_hgrn, g_post=m_g_post)
    var = dict(w_in=rows3(v_w_in), w_uq=v_w_uq[0], w_ukv=v_w_ukv[0], w_branch_a=v_w_branch_a[0],
               w_branch_b=v_w_branch_b[0], w_out=v_w_out[0], g_pre=v_g_pre, b_gate=v_b_gate, g_q=v_g_q, g_kv=v_g_kv,
               lb_logits=v_lb_logits, g_hgrn=v_g_hgrn, g_post=v_g_post)

    w_blk = w["w_in"].reshape(W_IN_SHARD, D_MODEL).astype(BF16)
    shards = [w[n].astype(BF16) for n in MATS[:2]] + [w[n] for n in MATS[2:]]
    dx, recv_in, recv, g_sum = _step(x[0], loss_target[0], w_blk, shards,
                                     g_pre, b_gate, g_q, g_kv, lb_logits, g_hgrn, g_post)

    g_in, d_in, m_in, v_in = _sum_adamw_w_in(recv_in, w["w_in"], mom["w_in"], var["w_in"])
    res = _sum_adamw_whole([recv[n] for n in MATS], *([t[n] for n in MATS] for t in (w, mom, var)))
    total, *vec = _vectors_adamw(g_sum, *([t[n] for n in SMALL] for t in (w, mom, var)))

    outs = []
    for mats, vecs, big in zip(res, vec, (g_in, d_in, m_in, v_in)):
        t = {**{n: a[None] for n, a in zip(MATS, mats)}, **dict(zip(SMALL, vecs)),
             "w_in": jnp.transpose(big, (1, 2, 0))}
        outs += [t[n] for n in ORDER]
    return (total.reshape(()), dx[None], *outs)
```

```python
import math

import jax
import jax.numpy as jnp
import numpy as np
from jax import lax
from jax.experimental import pallas as pl
from jax.experimental.pallas import tpu as pltpu

F32, BF16 = jnp.float32, jnp.bfloat16

D_MODEL = 1024
EPS = 1e-6
HEADS = 8
NOPE, ROPE, VDIM = 64, 32, 64
QK = NOPE + ROPE
Q_LORA, KV_LORA = 768, 256
ROPE_THETA = 10000.0
ATT_CHUNK_SHIFT = 6
HG_BLOCK = 32
HG_WIDTH = 512
D_IN = 5664
D_IN_PAD = 5760
W_IN_SHARD = D_IN // 8
N_DEV = 8
LANE = 128

ADAM_LR, ADAM_B1, ADAM_B2, ADAM_EPS, ADAM_WD, ADAM_STEP = 0.001, 0.9, 0.999, 1e-08, 0.01, 10

W_IN_SEGMENTS = ((3616, 5664, 0), (1056, 1568, 2048), (3104, 3616, 2560), (1568, 3104, 3072),
                 (0, 1024, 4608), (1024, 1056, 5696))

NT = (((1,), (1,)), ((), ()))
TN = (((0,), (0,)), ((), ()))
MESH_ID = pl.DeviceIdType.MESH


def _w_in_pieces():
    out = []
    for lo, hi, dst in W_IN_SEGMENTS:
        c = lo
        while c < hi:
            p = c // W_IN_SHARD
            e = min(hi, (p + 1) * W_IN_SHARD)
            out.append((p, c - p * W_IN_SHARD, e - p * W_IN_SHARD, dst + c - lo))
            c = e
    return out


def _params(sem, vmem_mb=48):
    return pltpu.CompilerParams(dimension_semantics=sem, vmem_limit_bytes=vmem_mb * 2**20)


def _dot(a, b):
    return jnp.dot(a, b, preferred_element_type=F32)


def _dotg(a, b, dims):
    return lax.dot_general(a, b, dims, preferred_element_type=F32)


def _split2(x):
    hi = x.astype(BF16)
    return hi, (x - hi.astype(F32)).astype(BF16)


def _sel_left(m01, x):
    hi, lo = _split2(x)
    return _dot(m01, hi) + _dot(m01, lo)


def _sel_right(x, m01):
    hi, lo = _split2(x)
    return _dot(hi, m01) + _dot(lo, m01)


def _hi_lo(x):
    hi = x.astype(BF16).astype(F32)
    return hi, x - hi


def _sigmoid(x):
    return 0.5 * jnp.tanh(0.5 * x) + 0.5


def _rope(x, c, s1, s2):
    return x * c + pltpu.roll(x, 112, 1) * s1 + pltpu.roll(x, 16, 1) * s2


def _unrope(d, c, s1, s2):
    return d * c + pltpu.roll(d * s1, 16, 1) + pltpu.roll(d * s2, 112, 1)


def _my_place():
    return lax.axis_index("x"), lax.axis_index("y"), lax.axis_index("c")


def _flip(k, x, y, c):
    fx, fy, fc = (k + 1) >> 2 & 1, (k + 1) >> 1 & 1, (k + 1) & 1
    return (1 - x if fx else x), (1 - y if fy else y), (1 - c if fc else c)


def _to_all_copies(s_refs, r_refs, sems, spread):
    send_sems, recv_sems, local_sems = sems
    x, y, c = _my_place()
    me = 4 * x + 2 * y + c
    src = (lambda a, p: s_refs[a]) if spread else (lambda a, p: s_refs[a].at[p])
    local = [pltpu.make_async_copy(src(a, me), r_refs[a].at[me], local_sems.at[a]) for a in range(len(s_refs))]
    remote = []
    for k in range(N_DEV - 1):
        px, py, pc = _flip(k, x, y, c)
        for a in range(len(s_refs)):
            remote.append(pltpu.make_async_remote_copy(
                src_ref=src(a, 4 * px + 2 * py + pc), dst_ref=r_refs[a].at[me],
                send_sem=send_sems.at[7 * a + k], recv_sem=recv_sems.at[7 * a + k],
                device_id=(px, py, pc), device_id_type=MESH_ID))
    return local, remote


def _to_chips_copies(s_refs, r_refs, sems):
    send_sems, recv_sems, local_sems = sems
    x, y, c = _my_place()
    me = 2 * x + y
    local = [pltpu.make_async_copy(s_refs[a].at[me], r_refs[a].at[me], local_sems.at[a]) for a in range(len(s_refs))]
    remote = []
    for k in range(3):
        px = 1 - x if (k + 1) >> 1 & 1 else x
        py = 1 - y if (k + 1) & 1 else y
        for a in range(len(s_refs)):
            remote.append(pltpu.make_async_remote_copy(
                src_ref=s_refs[a].at[2 * px + py], dst_ref=r_refs[a].at[me],
                send_sem=send_sems.at[3 * a + k], recv_sem=recv_sems.at[3 * a + k],
                device_id=(px, py, c), device_id_type=MESH_ID))
    return local, remote


def _start_all(local, remote):
    for cp in local + remote:
        cp.start()


def _wait_all(local, remote):
    for cp in remote:
        cp.wait_recv()
    for cp in remote:
        cp.wait_send()
    for cp in local:
        cp.wait()


def _copy_sems(n, peers):
    return [pltpu.SemaphoreType.DMA((peers * n,)), pltpu.SemaphoreType.DMA((peers * n,)),
            pltpu.SemaphoreType.DMA((n,))]


ANY = pl.BlockSpec(memory_space=pl.ANY)


def _dw_in_slots(ht, dproj):
    m, k = ht.shape
    n = dproj.shape[1]
    tn, tk = 1920, 2048
    nj, nk = n // tn, k // tk
    by_tile = [[] for _ in range(nj)]
    for p, lo, hi, dst in _w_in_pieces():
        while lo < hi:
            j = dst // tn
            cnt = min(hi - lo, (j + 1) * tn - dst)
            by_tile[j].append((p, lo, lo + cnt, dst - j * tn))
            lo, dst = lo + cnt, dst + cnt

    def body(a_ref, b_ref, s_ref, acc_ref):
        j, l = pl.program_id(0), pl.program_id(1)

        @pl.when(l == 0)
        def _():
            acc_ref[...] = jnp.zeros_like(acc_ref)

        acc_ref[...] += _dot(a_ref[...], b_ref[...])

        @pl.when(l == nk - 1)
        def _():
            at = acc_ref[...].T
            for jj in range(nj):
                @pl.when(j == jj)
                def _(jj=jj):
                    for p, lo, hi, d in by_tile[jj]:
                        s_ref[p, lo:hi, :] = at[d:d + hi - lo, :].astype(BF16)

    return pl.pallas_call(
        body,
        grid=(nj, nk),
        in_specs=[pl.BlockSpec((m, tk), lambda j, l: (0, l)), pl.BlockSpec((tk, tn), lambda j, l: (l, j))],
        out_specs=pl.BlockSpec((N_DEV, W_IN_SHARD, m), lambda j, l: (0, 0, 0), pipeline_mode=pl.Buffered(1)),
        out_shape=jax.ShapeDtypeStruct((N_DEV, W_IN_SHARD, m), BF16),
        scratch_shapes=[pltpu.VMEM((m, tn), F32)],
        compiler_params=_params(("arbitrary", "arbitrary"), 56),
        name="dw_in",
    )(ht, dproj)


GP_TN = 256
GP_COLS = 5888
GP_NT = GP_COLS // GP_TN


def _gp_tile_pieces():
    tiles = [[] for _ in range(GP_NT)]
    for p, lo, hi, dst in _w_in_pieces():
        while lo < hi:
            t = dst // GP_TN
            n = min(hi - lo, (t + 1) * GP_TN - dst)
            tiles[t].append((p, lo, lo + n, dst - t * GP_TN))
            lo, dst = lo + n, dst + n
    return tiles


def _gp_tables():
    pieces = _gp_tile_pieces()
    rank_of = {None: 0, 0: 1, 1: 2, 2: 2, 4: 3, 5: 3, 3: 4, 6: 5}
    order = np.zeros((N_DEV, GP_NT), np.int32)
    waits = np.zeros((N_DEV, GP_NT), np.int32)
    for me in range(N_DEV):
        x, y, c = me >> 2 & 1, me >> 1 & 1, me & 1
        chips = [(1 - x, y), (x, 1 - y), (1 - x, 1 - y)]

        def sem_of(p):
            px, py, pc = p >> 2 & 1, p >> 1 & 1, p & 1
            if (px, py) == (x, y):
                return None if pc == c else 0
            j = chips.index((px, py))
            return 1 + j if pc == c else 4 + j

        needs = [sorted({sem_of(p) for p, _, _, _ in tile} - {None}) for tile in pieces]
        ranks = [max([rank_of[k] for k in ks], default=0) for ks in needs]
        seq = sorted(range(GP_NT), key=lambda t: (ranks[t], t))
        seen = set()
        for step, t in enumerate(seq):
            order[me, step] = t
            new = [k for k in needs[t] if k not in seen]
            for k in new:
                waits[me, step] |= 1 << k
            seen.update(new)
        assert seen == set(range(7)), (me, seen)
    return order, waits


def _gather_proj(x, g_pre, w_blk, shards):
    s = x.shape[0]
    tx = 512
    ns = len(shards)
    tile_pieces = _gp_tile_pieces()
    order_np, waits_np = _gp_tables()
    xq, yq, cq = _my_place()
    me_out = 4 * xq + 2 * yq + cq
    order = lax.dynamic_index_in_dim(jnp.asarray(order_np), me_out, 0, keepdims=False)
    waits = lax.dynamic_index_in_dim(jnp.asarray(waits_np), me_out, 0, keepdims=False)

    def body(order_ref, waits_ref, x_hbm, g_ref, wblk_hbm, *rest):
        shard_refs, (proj_ref, wt_ref, ht_hbm), got_refs = rest[:ns], rest[ns:ns + 3], rest[ns + 3:2 * ns + 3]
        recv, h_ref, wtile, xbuf, htbuf, wbuf = rest[2 * ns + 3:2 * ns + 9]
        send_sems, recv_sems, misc_sems = rest[2 * ns + 9:2 * ns + 12]
        sems = rest[2 * ns + 12:]
        t = pl.program_id(0)
        x_, y_, c = _my_place()
        sibling = (x_, y_, 1 - c)
        chips = [(1 - x_, y_), (x_, 1 - y_), (1 - x_, 1 - y_)]
        idx = lambda px, py, pc: 4 * px + 2 * py + pc
        me = idx(x_, y_, c)

        def copy(k, slot, to, src=None):
            return pltpu.make_async_remote_copy(
                src_ref=recv.at[slot] if src is None else src, dst_ref=recv.at[slot],
                send_sem=send_sems.at[k], recv_sem=recv_sems.at[k], device_id=to, device_id_type=MESH_ID)

        mine = pltpu.make_async_copy(wblk_hbm.at[:, 0, pl.ds(0, D_MODEL)], wbuf, misc_sems.at[0])
        first = [copy(0, me, sibling)] + [copy(1 + j, me, (*chips[j], c)) for j in range(2)]
        passed = [copy(4 + j, idx(*ch, c), sibling) for j, ch in enumerate(chips)]
        onward = [copy(3, idx(*chips[0], c), (*chips[1], c)), copy(3, idx(*chips[1], c), (*chips[0], c))]
        arrivals = ([copy(0, idx(x_, y_, 1 - c), sibling)] + [copy(1 + j, idx(*ch, c), sibling) for j, ch in enumerate(chips)]
                    + [copy(4 + j, idx(*ch, 1 - c), sibling) for j, ch in enumerate(chips)])

        @pl.when(t == 0)
        def _():
            mine.start()
            _start_all(*_to_all_copies(shard_refs, got_refs, sems, True))
            mine.wait()
            recv[me] = wbuf[...].astype(BF16)
            for cp in first:
                cp.start()

            def load(i):
                return pltpu.make_async_copy(x_hbm.at[pl.ds(i * tx, tx), :], xbuf.at[i & 1], misc_sems.at[1 + (i & 1)])

            def store(i):
                return pltpu.make_async_copy(htbuf.at[i & 1], ht_hbm.at[:, pl.ds(i * tx, tx)], misc_sems.at[3 + (i & 1)])

            load(0).start()
            for i in range(s // tx):
                if i + 1 < s // tx:
                    load(i + 1).start()
                load(i).wait()
                xv = xbuf[i & 1]
                r = lax.rsqrt(jnp.mean(xv * xv, axis=-1, keepdims=True) + EPS)
                h = (xv * r * g_ref[...]).astype(BF16)
                h_ref[i * tx:(i + 1) * tx, :] = h
                if i >= 2:
                    store(i - 2).wait()
                htbuf[i & 1] = h.T
                store(i).start()
            for i in range(max(s // tx - 2, 0), s // tx):
                store(i).wait()

        w = waits_ref[t]
        for k in range(7):
            @pl.when((w >> k) & 1 == 1)
            def _(k=k):
                arrivals[k].wait_recv()
                if 1 <= k <= 3:
                    passed[k - 1].start()
                if 1 <= k <= 2:
                    @pl.when(c == k - 1)
                    def _():
                        onward[k - 1].start()

        tile = order_ref[t]
        for tt in range(GP_NT):
            @pl.when(tile == tt)
            def _(tt=tt):
                covered = sorted((d, d + hi - lo) for _, lo, hi, d in tile_pieces[tt])
                at = 0
                for lo_z, hi_z in covered + [(GP_TN, GP_TN)]:
                    if lo_z > at:
                        wtile[at:lo_z, :] = jnp.zeros((lo_z - at, D_MODEL), BF16)
                    at = max(at, hi_z)
                for p, lo, hi, d in tile_pieces[tt]:
                    wtile[d:d + hi - lo, :] = recv[p, lo:hi, :]

        wt = wtile[...]
        wt_ref[...] = wt
        proj_ref[...] = _dotg(h_ref[...], wt, NT)

        @pl.when(t == GP_NT - 1)
        def _():
            for cp in first + passed + onward[:1]:
                cp.wait_send()
            _wait_all(*_to_all_copies(shard_refs, got_refs, sems, True))

    grid_spec = pltpu.PrefetchScalarGridSpec(
        num_scalar_prefetch=2,
        grid=(GP_NT,),
        in_specs=[ANY, pl.BlockSpec((1, D_MODEL), lambda t, o, w: (0, 0)), ANY] + [ANY] * ns,
        out_specs=[pl.BlockSpec((s, GP_TN), lambda t, o, w: (0, o[t])),
                   pl.BlockSpec((GP_TN, D_MODEL), lambda t, o, w: (o[t], 0)), ANY] + [ANY] * ns,
        scratch_shapes=[pltpu.VMEM((N_DEV, W_IN_SHARD, D_MODEL), BF16), pltpu.VMEM((s, D_MODEL), BF16),
                        pltpu.VMEM((GP_TN, D_MODEL), BF16), pltpu.VMEM((2, tx, D_MODEL), F32),
                        pltpu.VMEM((2, D_MODEL, tx), BF16), pltpu.VMEM((W_IN_SHARD, D_MODEL), F32),
                        pltpu.SemaphoreType.DMA((7,)), pltpu.SemaphoreType.DMA((7,)), pltpu.SemaphoreType.DMA((5,))]
        + _copy_sems(ns, 7),
    )
    return pl.pallas_call(
        body,
        grid_spec=grid_spec,
        out_shape=[jax.ShapeDtypeStruct((s, GP_COLS), F32), jax.ShapeDtypeStruct((GP_COLS, D_MODEL), BF16),
                   jax.ShapeDtypeStruct((D_MODEL, s), BF16)]
        + [jax.ShapeDtypeStruct((N_DEV,) + b.shape, b.dtype) for b in shards],
        compiler_params=_params(("arbitrary",), 56),
        name="gather_proj",
    )(order, waits, x, g_pre, w_blk, *shards)


def _mla_prep(proj, g_q, g_kv, w_uq_p, w_kv_p, rc, rs1, rs2, casts):
    s = proj.shape[0]
    tm = 512
    nc = len(casts)
    scale = 1.0 / math.sqrt(QK)

    def body(cq_ref, ckv_ref, kpe_ref, gq_ref, gkv_ref, wuq_ref, wkv_ref, c_ref, s1_ref, s2_ref, *rest):
        cast_in, (qr_ref, kr_ref, v_ref, cqt_ref, ckvt_ref), cast_out = rest[:nc], rest[nc:nc + 5], rest[nc + 5:]

        @pl.when(pl.program_id(0) == 0)
        def _():
            for src, dst in zip(cast_in, cast_out):
                dst[...] = src[...].astype(BF16)

        cq = cq_ref[...]
        r = lax.rsqrt(jnp.mean(cq * cq, axis=-1, keepdims=True) + EPS)
        cqn = (cq * r * gq_ref[...]).astype(BF16)
        cqt_ref[...] = cqn.T
        q = _dot(cqn, wuq_ref[...])
        ckv = ckv_ref[...]
        r = lax.rsqrt(jnp.mean(ckv * ckv, axis=-1, keepdims=True) + EPS)
        ckvn = (ckv * r * gkv_ref[...]).astype(BF16)
        ckvt_ref[...] = ckvn.T
        kv = _dot(ckvn, wkv_ref[...])
        c, s1, s2 = c_ref[...], s1_ref[...], s2_ref[...]
        lane = lax.broadcasted_iota(jnp.int32, (tm, LANE), 1)
        kpe = _rope(kpe_ref[...], c, s1, s2) + jnp.where((lane == QK) | (lane == QK + 1), 1.0, 0.0)
        vone = jnp.where((lane == VDIM) | (lane == VDIM + 1), 1.0, 0.0)
        for h in range(HEADS):
            sl = slice(LANE * h, LANE * (h + 1))
            qr_ref[:, sl] = (_rope(q[:, sl], c, s1, s2) * scale).astype(BF16)
            kr_ref[:, sl] = (kv[:, sl] + kpe).astype(BF16)
            v_ref[:, sl] = (kv[:, HEADS * LANE + LANE * h:HEADS * LANE + LANE * (h + 1)] + vone).astype(BF16)

    row = lambda w, j: pl.BlockSpec((tm, w), lambda i: (i, j))
    col = lambda w: pl.BlockSpec((w, tm), lambda i: (0, i))
    full = lambda a: pl.BlockSpec(a.shape, lambda i: (0, 0))
    return pl.pallas_call(
        body,
        grid=(s // tm,),
        in_specs=[row(768, 6), row(256, 21), row(128, 44), full(g_q), full(g_kv), full(w_uq_p), full(w_kv_p),
                  row(128, 0), row(128, 0), row(128, 0)] + [full(a) for a in casts],
        out_specs=[row(1024, 0), row(1024, 0), row(1024, 0), col(768), col(256)] + [full(a) for a in casts],
        out_shape=[jax.ShapeDtypeStruct((s, 1024), BF16), jax.ShapeDtypeStruct((s, 1024), BF16),
                   jax.ShapeDtypeStruct((s, 1024), BF16), jax.ShapeDtypeStruct((768, s), BF16),
                   jax.ShapeDtypeStruct((256, s), BF16)] + [jax.ShapeDtypeStruct(a.shape, BF16) for a in casts],
        compiler_params=_params(("arbitrary",)),
        name="mla_prep",
    )(proj, proj, proj, g_q, g_kv, w_uq_p, w_kv_p, rc, rs1, rs2, *casts)


ATT_T = 512
ATT_FWD_HEADS = 4


def _chunk_mask(transposed):
    r = lax.broadcasted_iota(jnp.int32, (ATT_T, ATT_T), 0) >> ATT_CHUNK_SHIFT
    c = lax.broadcasted_iota(jnp.int32, (ATT_T, ATT_T), 1) >> ATT_CHUNK_SHIFT
    return (r <= c) if transposed else (c <= r)


def _attn_fwd(qr, kr, vp, shards):
    s = qr.shape[0]
    t = ATT_T
    g = ATT_FWD_HEADS
    ns = len(shards)

    def body(q_ref, k_ref, v_ref, *rest):
        shard_refs, (o_ref, qa_ref), got_refs = rest[:ns], rest[ns:ns + 2], rest[ns + 2:2 * ns + 2]
        sc_ref, sems = rest[2 * ns + 2], rest[2 * ns + 3:]
        qi = pl.program_id(1)

        @pl.when((pl.program_id(0) == 0) & (qi == 0))
        def _():
            _start_all(*_to_all_copies(shard_refs, got_refs, sems, True))
        lane = lax.broadcasted_iota(jnp.int32, (t, LANE), 1)
        sls = [slice(LANE * a, LANE * (a + 1)) for a in range(g)]
        qs = [q_ref[:, sl] for sl in sls]

        def scores(j):
            rows = pl.ds(pl.multiple_of(j * t, t), t)
            for a in range(g):
                sc_ref[j & 1, a] = _dotg(qs[a], k_ref[rows, sls[a]], NT)

        def step(j, carry, masked):
            rows = pl.ds(pl.multiple_of(j * t, t), t)
            out = []
            for a in range(g):
                m, acc = carry[a]
                sc = sc_ref[j & 1, a]
                if masked:
                    sc = jnp.where(_chunk_mask(False), sc, -1e30)
                m_new = jnp.maximum(m, jnp.max(sc, axis=-1, keepdims=True))
                p = jnp.exp(sc - m_new).astype(BF16)
                acc = jnp.exp(m - m_new) * acc + _dot(p, v_ref[rows, sls[a]])
                out.append((m_new, acc))
            return tuple(out)

        def loop(j, carry):
            carry = step(j, carry, False)
            scores(j + 1)
            return carry

        init = tuple((jnp.full((t, 1), -1e30, F32), jnp.zeros((t, LANE), F32)) for _ in range(g))
        scores(0)
        carry = lax.fori_loop(0, qi, loop, init)
        carry = step(qi, carry, True)
        outs = []
        for a in range(g):
            m, acc = carry[a]
            l = acc[:, VDIM:VDIM + 1]
            outs.append(acc / l)
            hi, lo_part = _hi_lo(-(m + jnp.log(l)))
            qa = jnp.where(lane == QK, hi, jnp.where(lane == QK + 1, lo_part, qs[a].astype(F32)))
            qa_ref[:, sls[a]] = qa.astype(BF16)
        for p in range(g // 2):
            o_ref[:, LANE * p:LANE * (p + 1)] = jnp.where(lane < VDIM, outs[2 * p], pltpu.roll(outs[2 * p + 1], VDIM, 1))

        @pl.when((pl.program_id(0) == HEADS // g - 1) & (qi == s // t - 1))
        def _():
            _wait_all(*_to_all_copies(shard_refs, got_refs, sems, True))

    return pl.pallas_call(
        body,
        grid=(HEADS // g, s // t),
        in_specs=[
            pl.BlockSpec((t, g * LANE), lambda h, i: (i, h)),
            pl.BlockSpec((s, g * LANE), lambda h, i: (0, h)),
            pl.BlockSpec((s, g * LANE), lambda h, i: (0, h)),
        ] + [ANY] * ns,
        out_specs=[
            pl.BlockSpec((t, g * VDIM), lambda h, i: (i, h)),
            pl.BlockSpec((t, g * LANE), lambda h, i: (i, h)),
        ] + [ANY] * ns,
        out_shape=[jax.ShapeDtypeStruct((s, 512), F32), jax.ShapeDtypeStruct((s, 1024), BF16)]
        + [jax.ShapeDtypeStruct((N_DEV,) + b.shape, b.dtype) for b in shards],
        scratch_shapes=[pltpu.VMEM((2, g, t, t), F32)] + _copy_sems(ns, 7),
        compiler_params=_params(("arbitrary", "arbitrary")),
        name="attn_fwd",
    )(qr, kr, vp, *shards)


def _attn_bwd(qa, kr, vp, dop, sends):
    s = qa.shape[0]
    t = ATT_T
    nq = s // t
    ns = len(sends)

    def body(q_ref, k_ref, v_ref, do_ref, *rest):
        send_refs, (dq_out, dk_out, dv_out) = rest[:ns], rest[ns:ns + 3]
        recv_refs = rest[ns + 3:2 * ns + 3]
        (dq_ref, dk_ref, dv_ref), sems = rest[2 * ns + 3:2 * ns + 6], rest[2 * ns + 6:]
        j = pl.program_id(1)
        sls = [slice(LANE * a, LANE * (a + 1)) for a in range(2)]

        @pl.when((pl.program_id(0) == 0) & (j == 0))
        def _():
            _start_all(*_to_all_copies(send_refs, recv_refs, sems, False))

        @pl.when(j == 0)
        def _():
            dq_ref[...] = jnp.zeros_like(dq_ref)

        dk_ref[...] = jnp.zeros_like(dk_ref)
        dv_ref[...] = jnp.zeros_like(dv_ref)
        ks = [k_ref[:, sl] for sl in sls]
        vs = [v_ref[:, sl] for sl in sls]

        def part(i, k_lo, k_n, q_lo, q_n, masked):
            rows = pl.ds(pl.multiple_of(i * t + q_lo, 256), q_n)
            keys = slice(k_lo, k_lo + k_n)
            for a in range(2):
                q = q_ref[rows, sls[a]]
                do = do_ref[rows, sls[a]]
                sc = _dotg(ks[a][keys], q, NT)
                if masked:
                    kc = lax.broadcasted_iota(jnp.int32, (k_n, q_n), 0) >> ATT_CHUNK_SHIFT
                    qc = lax.broadcasted_iota(jnp.int32, (k_n, q_n), 1) >> ATT_CHUNK_SHIFT
                    sc = jnp.where(kc <= qc, sc, -1e30)
                p = jnp.exp(sc)
                ds = (p * _dotg(vs[a][keys], do, NT)).astype(BF16)
                dv_ref[keys, sls[a]] += _dot(p.astype(BF16), do)
                dk_ref[keys, sls[a]] += _dot(ds, q)
                dq_ref[rows, sls[a]] += _dotg(ds, ks[a][keys], TN)

        half = t // 2
        part(j, 0, half, 0, t, True)
        part(j, half, half, half, half, True)

        def loop(i, c):
            part(i, 0, t, 0, t, False)
            return c

        lax.fori_loop(j + 1, nq, loop, 0)
        dk_out[...] = dk_ref[...].astype(BF16)
        dv_out[...] = dv_ref[...].astype(BF16)

        @pl.when(j == nq - 1)
        def _():
            dq_out[...] = dq_ref[...].astype(BF16)

        @pl.when((pl.program_id(0) == HEADS // 2 - 1) & (j == nq - 1))
        def _():
            _wait_all(*_to_all_copies(send_refs, recv_refs, sems, False))

    blk = pl.BlockSpec((t, 2 * LANE), lambda h, j: (j, h))
    whole = pl.BlockSpec((s, 2 * LANE), lambda h, j: (0, h))
    out = jax.ShapeDtypeStruct((s, 1024), BF16)
    return pl.pallas_call(
        body,
        grid=(HEADS // 2, nq),
        in_specs=[whole, blk, blk, whole] + [ANY] * ns,
        out_specs=[whole, blk, blk] + [ANY] * ns,
        out_shape=[out, out, out] + [jax.ShapeDtypeStruct(a.shape, a.dtype) for a in sends],
        scratch_shapes=[pltpu.VMEM((s, 2 * LANE), F32), pltpu.VMEM((t, 2 * LANE), F32),
                        pltpu.VMEM((t, 2 * LANE), F32)] + _copy_sems(ns, 7),
        compiler_params=_params(("arbitrary", "arbitrary")),
        name="attn_bwd",
    )(qa, kr, vp, dop, *sends)


HG_T = 256
HG_NC = HG_T // HG_BLOCK
HG_G = 4
GW = 64 * HG_G


def _hg_consts():
    r = jnp.arange(HG_T)[:, None]
    c = jnp.arange(HG_T)[None, :]
    same = (r // HG_BLOCK) == (c // HG_BLOCK)
    mcum = (same & (c <= r)).astype(BF16)
    mrev = (same & (c >= r)).astype(BF16)
    msum = same.astype(BF16)
    a = jnp.arange(GW) // 64
    bd = (a[:, None] == a[None, :]).astype(F32)
    return mcum, mrev, msum, bd


def _stack_heads(xg, head):
    return jnp.concatenate([jnp.where(head == h, xg, 0.0) for h in range(HG_G)], axis=0)


def _unstack_heads(r, head, t):
    out = r[(HG_G - 1) * t:]
    for h in range(HG_G - 2, -1, -1):
        out = jnp.where(head == h, r[h * t:(h + 1) * t], out)
    return out


def _compact_state(st):
    out = st[:64]
    for h in range(1, HG_G):
        out = out + st[64 * h:64 * (h + 1)]
    return out


def _expand_state(cs, head64):
    return jnp.concatenate([jnp.where(head64 == h, cs, 0.0) for h in range(HG_G)], axis=0)


def _hg_pre(hq, hf, lbl, mcum, msum):
    lb = _sigmoid(lbl[0:1, :] - lbl[1:2, :])
    sig = _sigmoid(hf)
    f = lb + (1.0 - lb) * sig
    lf = jnp.log(f)
    b = _sel_left(mcum, lf)
    big_l = _sel_left(msum, lf)
    k = 1.0 - f
    qd = hq * jnp.exp(b)
    ki = k * jnp.exp(-b)
    ke = k * jnp.exp(big_l - b)
    return lb, sig, f, b, big_l, qd, ki, ke


def _hgrn_fwd(proj, lbl):
    s = proj.shape[0]
    t = HG_T
    mcum, _, msum, bd = _hg_consts()

    def body(hq_ref, hf_ref, hi_ref, lbl_ref, mcum_ref, msum_ref, bd_ref, o_ref, sp_ref, st_ref):
        @pl.when(pl.program_id(0) == 0)
        def _():
            st_ref[...] = jnp.zeros_like(st_ref)

        mc = mcum_ref[...]
        _, _, _, _, big_l, qd, ki, ke = _hg_pre(hq_ref[...], hf_ref[...], lbl_ref[...], mc, msum_ref[...])
        el = jnp.exp(big_l)
        hi = hi_ref[...]
        head = lax.broadcasted_iota(jnp.int32, (t, GW), 1) >> 6
        mask = jnp.concatenate([mc] * HG_G, axis=0) > 0.5
        for p in range(HEADS // HG_G):
            sl = slice(GW * p, GW * (p + 1))
            vp = hi[:, sl].astype(BF16)
            qs = _stack_heads(qd[:, sl], head).astype(BF16)
            a = jnp.where(mask, _dotg(qs, ki[:, sl].astype(BF16), NT), 0.0)
            o_intra = _unstack_heads(_dot(a.astype(BF16), vp), head, t)
            qb = qd[:, sl].astype(BF16)
            kb = ke[:, sl].astype(BF16)
            st = st_ref[p]
            for c in range(HG_NC):
                rows = slice(HG_BLOCK * c, HG_BLOCK * (c + 1))
                sp_ref[c, :, sl] = _compact_state(st)
                o_ref[rows, sl] = o_intra[rows] + _dotg(qb[rows], st.astype(BF16), NT)
                u = _dotg(vp[rows], kb[rows], TN) * bd_ref[...]
                st = st * el[HG_BLOCK * c:HG_BLOCK * c + 1, sl] + u
            st_ref[p] = st

    row = lambda j: pl.BlockSpec((t, HG_WIDTH), lambda i: (i, j))
    full = lambda a: pl.BlockSpec(a.shape, lambda i: (0, 0))
    return pl.pallas_call(
        body,
        grid=(s // t,),
        in_specs=[row(6), row(7), row(8), full(lbl), full(mcum), full(msum), full(bd)],
        out_specs=[row(0), pl.BlockSpec((HG_NC, 64, HG_WIDTH), lambda i: (i, 0, 0))],
        out_shape=[jax.ShapeDtypeStruct((s, HG_WIDTH), F32),
                   jax.ShapeDtypeStruct((s // HG_BLOCK, 64, HG_WIDTH), F32)],
        scratch_shapes=[pltpu.VMEM((HEADS // HG_G, GW, GW), F32)],
        compiler_params=_params(("arbitrary",)),
        name="hgrn_fwd",
    )(proj, proj, proj, lbl, mcum, msum, bd)


def _slot_shape(name, r, c):
    return (N_DEV, r, c // N_DEV) if COL_SHARDED[name] else (N_DEV, r // N_DEV, c)


def _emit_slots(name, acc_ref, out_ref):
    r, c = acc_ref.shape
    for p in range(N_DEV):
        if COL_SHARDED[name]:
            out_ref[p] = acc_ref[:, c // N_DEV * p:c // N_DEV * (p + 1)].astype(BF16)
        else:
            out_ref[p] = acc_ref[r // N_DEV * p:r // N_DEV * (p + 1), :].astype(BF16)


def _hgrn_bwd(proj, lbl, do, sprev, dproj, pairs):
    s = proj.shape[0]
    t = HG_T
    nt = s // t
    npair = len(pairs)
    mcum, mrev, msum, bd = _hg_consts()

    def body(hq_ref, hf_ref, hi_ref, lbl_ref, do_ref, sp_ref, mcum_ref, mrev_ref, msum_ref, bd_ref,
             dproj_in, *rest):
        del dproj_in
        pair_refs, (dh_ref, dlbl_ref) = rest[:2 * npair], rest[2 * npair:2 * npair + 2]
        dw_refs, g_ref, acc_refs = rest[2 * npair + 2:3 * npair + 2], rest[3 * npair + 2], rest[3 * npair + 3:]

        @pl.when(pl.program_id(0) == 0)
        def _():
            g_ref[...] = jnp.zeros_like(g_ref)
            dlbl_ref[...] = jnp.zeros_like(dlbl_ref)
            for acc_ref in acc_refs:
                acc_ref[...] = jnp.zeros_like(acc_ref)

        for n, acc_ref in enumerate(acc_refs):
            acc_ref[...] += _dot(pair_refs[2 * n][...], pair_refs[2 * n + 1][...])

        @pl.when(pl.program_id(0) == nt - 1)
        def _():
            for (name, _, _), acc_ref, dw_ref in zip(pairs, acc_refs, dw_refs):
                _emit_slots(name, acc_ref, dw_ref)

        mc = mcum_ref[...]
        lb, sig, f, b, big_l, qd, ki, ke = _hg_pre(hq_ref[...], hf_ref[...], lbl_ref[...], mc, msum_ref[...])
        el = jnp.exp(big_l)
        hi = hi_ref[...]
        dov = do_ref[...]
        head = lax.broadcasted_iota(jnp.int32, (t, GW), 1) >> 6
        head64 = lax.broadcasted_iota(jnp.int32, (64, GW), 1) >> 6
        mask = jnp.concatenate([mc] * HG_G, axis=0) > 0.5
        dqd_parts, dke_parts, dv_parts, del_parts, dki_parts = [], [], [], [], []
        for p in range(HEADS // HG_G):
            sl = slice(GW * p, GW * (p + 1))
            vp = hi[:, sl].astype(BF16)
            qs = _stack_heads(qd[:, sl], head).astype(BF16)
            kip = ki[:, sl].astype(BF16)
            dos = _stack_heads(dov[:, sl], head).astype(BF16)
            a = jnp.where(mask, _dotg(qs, kip, NT), 0.0).astype(BF16)
            da = jnp.where(mask, _dotg(dos, vp, NT), 0.0).astype(BF16)
            r = _dot(da, kip)
            dki_parts.append(_dotg(da, qs, TN))
            qb = qd[:, sl].astype(BF16)
            kb = ke[:, sl].astype(BF16)
            dob = dov[:, sl].astype(BF16)
            g = g_ref[p]
            dqd_c, dv_c, dke_c, del_c = [], [], [], []
            for c in range(HG_NC - 1, -1, -1):
                rows = slice(HG_BLOCK * c, HG_BLOCK * (c + 1))
                gb = g.astype(BF16)
                st = _expand_state(sp_ref[c, :, sl], head64)
                dqd_c.append(_dot(dob[rows], st.astype(BF16)))
                dv_c.append(_dotg(kb[rows], gb, NT))
                dke_c.append(_dot(vp[rows], gb))
                del_c.append(jnp.broadcast_to(jnp.sum(g * st, axis=0, keepdims=True), (HG_BLOCK, GW)))
                g = g * el[HG_BLOCK * c:HG_BLOCK * c + 1, sl] + _dotg(dob[rows], qb[rows], TN) * bd_ref[...]
            g_ref[p] = g
            up = lambda parts: jnp.concatenate(parts[::-1], axis=0)
            dqd_parts.append(_unstack_heads(r, head, t) + up(dqd_c))
            dv_parts.append(_dotg(a, dos, TN) + up(dv_c))
            dke_parts.append(up(dke_c))
            del_parts.append(up(del_c))
        wide = lambda parts: jnp.concatenate(parts, axis=1)
        dqd, dke, dki, dvv, del_rows = wide(dqd_parts), wide(dke_parts), wide(dki_parts), wide(dv_parts), wide(del_parts)
        dh_ref[:, :HG_WIDTH] = (dqd * jnp.exp(b)).astype(BF16)
        dh_ref[:, 2 * HG_WIDTH:] = dvv.astype(BF16)
        dke_ke = dke * ke
        db = dqd * qd - dki * ki - dke_ke
        dl_rows = _sel_left(msum_ref[...], dke_ke) + del_rows * el
        is_last = (lax.broadcasted_iota(jnp.int32, (t, HG_WIDTH), 0) & (HG_BLOCK - 1)) == HG_BLOCK - 1
        db = db + jnp.where(is_last, dl_rows, 0.0)
        dlf = _sel_left(mrev_ref[...], db)
        dk = dki * jnp.exp(-b) + dke * jnp.exp(big_l - b)
        df = dlf / f - dk
        dh_ref[:, HG_WIDTH:2 * HG_WIDTH] = (df * (1.0 - lb) * sig * (1.0 - sig)).astype(BF16)
        dlb = jnp.sum(df * (1.0 - sig), axis=0, keepdims=True) * lb * (1.0 - lb)
        dlbl_ref[0:1, :] += dlb
        dlbl_ref[1:2, :] -= dlb

    rrow = lambda j: pl.BlockSpec((t, HG_WIDTH), lambda i: (nt - 1 - i, j))
    full = lambda a: pl.BlockSpec(a.shape, lambda i: (0, 0))
    pair_specs, dw_specs, dw_shapes, accs = [], [], [], []
    for name, at, b in pairs:
        pair_specs += [pl.BlockSpec((at.shape[0], t), lambda i: (0, i)), pl.BlockSpec((t, b.shape[1]), lambda i: (i, 0))]
        shape = _slot_shape(name, at.shape[0], b.shape[1])
        dw_specs.append(pl.BlockSpec(shape, lambda i: (0, 0, 0)))
        dw_shapes.append(jax.ShapeDtypeStruct(shape, BF16))
        accs.append(pltpu.VMEM((at.shape[0], b.shape[1]), F32))
    return pl.pallas_call(
        body,
        grid=(nt,),
        in_specs=[rrow(6), rrow(7), rrow(8), full(lbl), rrow(0),
                  pl.BlockSpec((HG_NC, 64, HG_WIDTH), lambda i: (nt - 1 - i, 0, 0)),
                  full(mcum), full(mrev), full(msum), full(bd), pl.BlockSpec(memory_space=pl.ANY)] + pair_specs,
        out_specs=[pl.BlockSpec((t, 3 * HG_WIDTH), lambda i: (nt - 1 - i, 2)),
                   pl.BlockSpec((2, HG_WIDTH), lambda i: (0, 0))] + dw_specs,
        out_shape=[jax.ShapeDtypeStruct(dproj.shape, BF16), jax.ShapeDtypeStruct((2, HG_WIDTH), F32)] + dw_shapes,
        input_output_aliases={10: 0},
        scratch_shapes=[pltpu.VMEM((HEADS // HG_G, GW, GW), F32)] + accs,
        compiler_params=_params(("arbitrary",)),
        name="hgrn_bwd",
    )(proj, proj, proj, lbl, do, sprev, mcum, mrev, msum, bd, dproj, *[a for pair in pairs for a in pair[1:]])


def _tail(x, tgt, proj, attn, o, w_a, w_b, w_out, w_at, w_bt, w_outt, b_gate, g_post, gh):
    s = x.shape[0]
    tm = 256
    ones64 = (jnp.arange(HG_WIDTH)[:, None] // 64 == jnp.arange(HG_WIDTH)[None, :] // 64).astype(BF16)
    weights = (w_a, w_b, w_out, w_at, w_bt, w_outt)

    def body(x_ref, t_ref, ml_ref, ga_ref, gb_ref, at_ref, o_ref, *rest):
        w_hbm, (bg_ref, gp_ref, gh_ref, ones_ref) = rest[:6], rest[6:10]
        (dout_ref, dpj_ref, dop_ref, do_ref, mt_ref, dy_ref, yat_ref, dya_ref, ybt_ref, dyb_ref,
         loss_ref, dgp_ref, dbg_ref, dgh_ref) = rest[10:24]
        (wa_ref, wb_ref, wo_ref, wat_ref, wbt_ref, wot_ref), w_sem = rest[24:30], rest[30]

        @pl.when(pl.program_id(0) == 0)
        def _():
            loads = [pltpu.make_async_copy(src, dst, w_sem.at[k])
                     for k, (src, dst) in enumerate(zip(w_hbm, rest[24:30]))]
            _start_all(loads, [])
            loss_ref[...] = jnp.zeros_like(loss_ref)
            dgp_ref[...] = jnp.zeros_like(dgp_ref)
            dbg_ref[...] = jnp.zeros_like(dbg_ref)
            dgh_ref[...] = jnp.zeros_like(dgh_ref)
            _wait_all(loads, [])

        ones = ones_ref[...]
        gate_a = ga_ref[...]
        sa = _sigmoid(gate_a)
        silu_a = gate_a * sa
        attn_v = at_ref[...]
        ya_in = attn_v * silu_a
        ov = o_ref[...]
        ro = lax.rsqrt(_sel_right(ov * ov, ones) * (1.0 / 64.0) + EPS)
        ohat = ov * ro
        ghv = gh_ref[...]
        on = ohat * ghv
        gate_b = gb_ref[...]
        sb = _sigmoid(gate_b)
        silu_b = gate_b * sb
        yb_in = on * silu_b
        ya_bf = ya_in.astype(BF16)
        yb_bf = yb_in.astype(BF16)
        yat_ref[...] = ya_bf.T
        ybt_ref[...] = yb_bf.T
        y_a = _dot(ya_bf, wa_ref[...])
        y_b = _dot(yb_bf, wb_ref[...])
        gts = _sigmoid(ml_ref[...] + bg_ref[...])
        g_a = gts[:, :D_MODEL]
        g_b = gts[:, D_MODEL:]
        m_bf = (g_a * y_a + g_b * y_b).astype(BF16)
        mt_ref[...] = m_bf.T
        y = _dot(m_bf, wo_ref[...])
        r1 = lax.rsqrt(jnp.mean(y * y, axis=-1, keepdims=True) + EPS)
        yn = y * r1
        gp = gp_ref[...]
        e = x_ref[...] + yn * gp - t_ref[...]
        loss_ref[...] += jnp.sum(e * e, axis=0, keepdims=True)
        dout = e * (1.0 / D_MODEL)
        dout_ref[...] = dout
        dgp_ref[...] += jnp.sum(dout * yn, axis=0, keepdims=True)
        dyn = dout * gp
        dy = r1 * (dyn - yn * jnp.mean(dyn * yn, axis=-1, keepdims=True))
        dy_bf = dy.astype(BF16)
        dy_ref[...] = dy_bf
        dm = _dot(dy_bf, wot_ref[...])
        dml_a = dm * y_a * g_a * (1.0 - g_a)
        dml_b = dm * y_b * g_b * (1.0 - g_b)
        dpj_ref[:, :D_MODEL] = dml_a.astype(BF16)
        dpj_ref[:, D_MODEL:2 * D_MODEL] = dml_b.astype(BF16)
        dbg_ref[:, :D_MODEL] += jnp.sum(dml_a, axis=0, keepdims=True)
        dbg_ref[:, D_MODEL:] += jnp.sum(dml_b, axis=0, keepdims=True)
        dya_bf = (dm * g_a).astype(BF16)
        dyb_bf = (dm * g_b).astype(BF16)
        dya_ref[...] = dya_bf
        dyb_ref[...] = dyb_bf
        dya_in = _dot(dya_bf, wat_ref[...])
        dyb_in = _dot(dyb_bf, wbt_ref[...])
        dattn = dya_in * silu_a
        delta = _sel_right(dattn * attn_v, ones)
        lane = lax.broadcasted_iota(jnp.int32, (tm, LANE), 1)
        for p in range(HEADS // 2):
            sl = slice(LANE * p, LANE * (p + 1))
            xs = (dattn[:, sl], pltpu.roll(dattn[:, sl], VDIM, 1))
            nds = (-pltpu.roll(delta[:, sl], VDIM, 1), -delta[:, sl])
            for a in range(2):
                hi, lo_part = _hi_lo(nds[a])
                blk = jnp.where(lane < VDIM, xs[a], jnp.where(lane == VDIM, hi, jnp.where(lane == VDIM + 1, lo_part, 0.0)))
                dop_ref[:, LANE * (2 * p + a):LANE * (2 * p + a + 1)] = blk.astype(BF16)
        dpj_ref[:, 2 * D_MODEL:2 * D_MODEL + HG_WIDTH] = (
            dya_in * attn_v * (sa * (1.0 + gate_a * (1.0 - sa)))).astype(BF16)
        don = dyb_in * silu_b
        dpj_ref[:, 2 * D_MODEL + HG_WIDTH:] = (dyb_in * on * (sb * (1.0 + gate_b * (1.0 - sb)))).astype(BF16)
        dgh_ref[...] += jnp.sum(don * ohat, axis=0, keepdims=True)
        dohat = don * ghv
        do_ref[...] = (ro * (dohat - ohat * (_sel_right(dohat * ohat, ones) * (1.0 / 64.0)))).astype(BF16)

    row = lambda w, j: pl.BlockSpec((tm, w), lambda i: (i, j))
    col = lambda w: pl.BlockSpec((w, tm), lambda i: (0, i))
    full = lambda a: pl.BlockSpec(a.shape, lambda i: (0, 0))
    acc = lambda w: pl.BlockSpec((1, w), lambda i: (0, 0))
    sds = lambda w, dt: jax.ShapeDtypeStruct((s, w), dt)
    sdt = lambda w: jax.ShapeDtypeStruct((w, s), BF16)
    return pl.pallas_call(
        body,
        grid=(s // tm,),
        in_specs=[row(1024, 0), row(1024, 0), row(2048, 0), row(512, 4), row(512, 5), row(512, 0), row(512, 0)]
        + [ANY] * 6 + [full(b_gate), full(g_post), full(gh), full(ones64)],
        out_specs=[row(1024, 0), row(3072, 0), row(1024, 0), row(512, 0),
                   col(1024), row(1024, 0), col(512), row(1024, 0), col(512), row(1024, 0),
                   acc(1024), acc(1024), acc(2048), acc(512)],
        out_shape=[sds(1024, F32), sds(D_IN_PAD, BF16), sds(1024, BF16), sds(512, BF16),
                   sdt(1024), sds(1024, BF16), sdt(512), sds(1024, BF16), sdt(512), sds(1024, BF16),
                   jax.ShapeDtypeStruct((1, 1024), F32), jax.ShapeDtypeStruct((1, 1024), F32),
                   jax.ShapeDtypeStruct((1, 2048), F32), jax.ShapeDtypeStruct((1, 512), F32)],
        scratch_shapes=[pltpu.VMEM(a.shape, BF16) for a in weights] + [pltpu.SemaphoreType.DMA((6,))],
        compiler_params=_params(("arbitrary",), 56),
        name="tail",
    )(x, tgt, proj, proj, proj, attn, o, *weights, b_gate, g_post, gh, ones64)


def _mla_bwd(proj, dqr, dkr, dv, g_q, g_kv, w_uq_pt, w_kv_pt, rc, rs1, rs2, cqt, ckvt, dproj):
    assert HEADS == N_DEV
    s = proj.shape[0]
    tm = 512
    scale = 1.0 / math.sqrt(QK)

    def body(cq_ref, ckv_ref, dqr_ref, dkr_ref, dv_ref, gq_ref, gkv_ref, wuqt_ref, wkvt_ref, c_ref, s1_ref, s2_ref,
             cqt_ref, ckvt_ref, dproj_in, dc_ref, dgq_ref, dgkv_ref, uq_slots, ukv_slots,
             dqf_ref, dkvf_ref, dwuq_ref, dwkv_ref):
        del dproj_in

        @pl.when(pl.program_id(0) == 0)
        def _():
            dgq_ref[...] = jnp.zeros_like(dgq_ref)
            dgkv_ref[...] = jnp.zeros_like(dgkv_ref)
            dwuq_ref[...] = jnp.zeros_like(dwuq_ref)
            dwkv_ref[...] = jnp.zeros_like(dwkv_ref)

        c, s1, s2 = c_ref[...], s1_ref[...], s2_ref[...]
        lane = lax.broadcasted_iota(jnp.int32, (tm, LANE), 1)
        ksum = jnp.zeros((tm, LANE), F32)
        for h in range(HEADS):
            sl = slice(LANE * h, LANE * (h + 1))
            dqf_ref[:, sl] = (_unrope(dqr_ref[:, sl], c, s1, s2) * scale).astype(BF16)
            dkh = dkr_ref[:, sl]
            ksum = ksum + dkh
            dkvf_ref[:, sl] = jnp.where(lane < NOPE, dkh, 0.0).astype(BF16)
            dkvf_ref[:, HEADS * LANE + LANE * h:HEADS * LANE + LANE * (h + 1)] = jnp.where(
                lane < VDIM, dv_ref[:, sl], 0.0).astype(BF16)
        dkpe = _unrope(ksum, c, s1, s2)
        dc_ref[:, Q_LORA + KV_LORA:] = jnp.where((lane >= NOPE) & (lane < QK), dkpe, 0.0).astype(BF16)
        dqf, dkvf = dqf_ref[...], dkvf_ref[...]
        dwuq_ref[...] += _dot(cqt_ref[...], dqf)
        dwkv_ref[...] += _dot(ckvt_ref[...], dkvf)
        dcqn = _dot(dqf, wuqt_ref[...])
        dckvn = _dot(dkvf, wkvt_ref[...])
        for x_ref, g_ref, dn, cols, dg_ref in ((cq_ref, gq_ref, dcqn, slice(0, Q_LORA), dgq_ref),
                                               (ckv_ref, gkv_ref, dckvn, slice(Q_LORA, Q_LORA + KV_LORA), dgkv_ref)):
            xv = x_ref[...]
            r = lax.rsqrt(jnp.mean(xv * xv, axis=-1, keepdims=True) + EPS)
            xh = xv * r
            dg_ref[...] += jnp.sum(dn * xh, axis=0, keepdims=True)
            dh = dn * g_ref[...]
            dc_ref[:, cols] = (r * (dh - xh * jnp.mean(dh * xh, axis=-1, keepdims=True))).astype(BF16)

        @pl.when(pl.program_id(0) == s // tm - 1)
        def _():
            ur = Q_LORA // N_DEV
            for p in range(N_DEV):
                uq_slots[p] = jnp.concatenate(
                    [dwuq_ref[ur * p:ur * (p + 1), LANE * h:LANE * h + QK] for h in range(HEADS)], axis=1).astype(BF16)
                ukv_slots[p] = jnp.concatenate(
                    [dwkv_ref[:, LANE * p:LANE * p + NOPE],
                     dwkv_ref[:, LANE * (HEADS + p):LANE * (HEADS + p) + VDIM]], axis=1).astype(BF16)

    row = lambda w, j: pl.BlockSpec((tm, w), lambda i: (i, j))
    full = lambda a: pl.BlockSpec(a.shape, lambda i: (0, 0))
    acc = lambda w: pl.BlockSpec((1, w), lambda i: (0, 0))
    col = lambda w: pl.BlockSpec((w, tm), lambda i: (0, i))
    whole = lambda shape: pl.BlockSpec(shape, lambda i: (0, 0, 0))
    uq_shape = (N_DEV, Q_LORA // N_DEV, HEADS * QK)
    ukv_shape = (N_DEV, KV_LORA, NOPE + VDIM)
    return pl.pallas_call(
        body,
        grid=(s // tm,),
        in_specs=[row(768, 6), row(256, 21), row(1024, 0), row(1024, 0), row(1024, 0), full(g_q), full(g_kv),
                  full(w_uq_pt), full(w_kv_pt), row(128, 0), row(128, 0), row(128, 0), col(Q_LORA), col(KV_LORA),
                  pl.BlockSpec(memory_space=pl.ANY)],
        out_specs=[row(1152, 4), acc(768), acc(256), whole(uq_shape), whole(ukv_shape)],
        out_shape=[jax.ShapeDtypeStruct(dproj.shape, BF16),
                   jax.ShapeDtypeStruct((1, 768), F32), jax.ShapeDtypeStruct((1, 256), F32),
                   jax.ShapeDtypeStruct(uq_shape, BF16), jax.ShapeDtypeStruct(ukv_shape, BF16)],
        input_output_aliases={14: 0},
        scratch_shapes=[pltpu.VMEM((tm, HEADS * LANE), BF16), pltpu.VMEM((tm, 2 * HEADS * LANE), BF16),
                        pltpu.VMEM((Q_LORA, HEADS * LANE), F32), pltpu.VMEM((KV_LORA, 2 * HEADS * LANE), F32)],
        compiler_params=_params(("arbitrary",)),
        name="mla_bwd",
    )(proj, proj, dqr, dkr, dv, g_q, g_kv, w_uq_pt, w_kv_pt, rc, rs1, rs2, cqt, ckvt, dproj)


def _dh_dx(dproj, w_in_pt, x, dout, g_pre, sends):
    s, k = dproj.shape
    tm = 256
    ns, ni = len(sends), s // tm

    def body(dp_ref, w_ref, x_ref, dout_ref, g_ref, *rest):
        send_refs, (dx_ref, dg_ref) = rest[:ns], rest[ns:ns + 2]
        recv_refs, sems = rest[ns + 2:2 * ns + 2], rest[2 * ns + 2:]

        @pl.when(pl.program_id(0) == 0)
        def _():
            _start_all(*_to_chips_copies(send_refs, recv_refs, sems))
            dg_ref[...] = jnp.zeros_like(dg_ref)

        dh = _dot(dp_ref[...], w_ref[...])
        xv = x_ref[...]
        r = lax.rsqrt(jnp.mean(xv * xv, axis=-1, keepdims=True) + EPS)
        xh = xv * r
        dg_ref[...] += jnp.sum(dh * xh, axis=0, keepdims=True)
        dxh = dh * g_ref[...]
        dx_ref[...] = dout_ref[...] + r * (dxh - xh * jnp.mean(dxh * xh, axis=-1, keepdims=True))

        @pl.when(pl.program_id(0) == ni - 1)
        def _():
            _wait_all(*_to_chips_copies(send_refs, recv_refs, sems))

    row = lambda w: pl.BlockSpec((tm, w), lambda i: (i, 0))
    return pl.pallas_call(
        body,
        grid=(ni,),
        in_specs=[row(k), pl.BlockSpec((k, D_MODEL), lambda i: (0, 0)), row(D_MODEL), row(D_MODEL),
                  pl.BlockSpec((1, D_MODEL), lambda i: (0, 0))] + [ANY] * ns,
        out_specs=[row(D_MODEL), pl.BlockSpec((1, D_MODEL), lambda i: (0, 0))] + [ANY] * ns,
        out_shape=[jax.ShapeDtypeStruct((s, D_MODEL), F32), jax.ShapeDtypeStruct((1, D_MODEL), F32)]
        + [jax.ShapeDtypeStruct(a.shape, a.dtype) for a in sends],
        scratch_shapes=_copy_sems(ns, 3),
        compiler_params=_params(("arbitrary",)),
        name="dh_dx",
    )(dproj, w_in_pt, x, dout, g_pre, *sends)


def _pair_reduce(slots):
    n = len(slots)
    half = [(N_DEV // 2,) + a.shape[1:] for a in slots]

    def body(*refs):
        s_refs, o_refs = refs[:n], refs[n:2 * n]
        mine, got = refs[2 * n:3 * n], refs[3 * n:4 * n]
        send_sems, recv_sems, local_sems, out_sems = refs[4 * n:]
        x, y, c = _my_place()
        copies, loads, stores = [], [], []
        for q in range(N_DEV // 2):
            for a in range(n):
                copies.append(pltpu.make_async_remote_copy(
                    src_ref=s_refs[a].at[2 * q + 1 - c], dst_ref=got[a].at[q],
                    send_sem=send_sems.at[4 * a + q], recv_sem=recv_sems.at[4 * a + q],
                    device_id=(x, y, 1 - c), device_id_type=MESH_ID))
                loads.append(pltpu.make_async_copy(s_refs[a].at[2 * q + c], mine[a].at[q], local_sems.at[4 * a + q]))
                stores.append(pltpu.make_async_copy(mine[a].at[q], o_refs[a].at[q], out_sems.at[4 * a + q]))
        _start_all(loads, copies)
        k = 0
        for q in range(N_DEV // 2):
            for a in range(n):
                loads[k].wait()
                copies[k].wait_recv()
                mine[a][q] = (mine[a][q].astype(F32) + got[a][q].astype(F32)).astype(mine[a].dtype)
                stores[k].start()
                k += 1
        for cp in copies:
            cp.wait_send()
        for cp in stores:
            cp.wait()

    vm = lambda: [pltpu.VMEM(h, a.dtype) for h, a in zip(half, slots)]
    return pl.pallas_call(
        body,
        in_specs=[ANY] * n,
        out_specs=[ANY] * n,
        out_shape=[jax.ShapeDtypeStruct(h, a.dtype) for h, a in zip(half, slots)],
        scratch_shapes=vm() + vm() + [pltpu.SemaphoreType.DMA((4 * n,)), pltpu.SemaphoreType.DMA((4 * n,)),
                                      pltpu.SemaphoreType.DMA((4 * n,)), pltpu.SemaphoreType.DMA((4 * n,))],
        compiler_params=pltpu.CompilerParams(vmem_limit_bytes=48 * 2**20),
        name="pair_reduce",
    )(*slots)


def _rope_tables(s):
    inv = (np.float32(ROPE_THETA) ** (-np.arange(0, ROPE, 2, dtype=np.float32) / np.float32(ROPE))).astype(np.float32)
    ang = (np.arange(s, dtype=np.float32)[:, None] * inv[None, :]).astype(np.float32)
    cos, sin = jnp.asarray(np.cos(ang.astype(np.float64)), F32), jnp.asarray(np.sin(ang.astype(np.float64)), F32)
    z = lambda w: jnp.zeros((s, w), F32)
    rc = jnp.concatenate([jnp.ones((s, NOPE), F32), cos, cos, z(32)], axis=1)
    rs1 = jnp.concatenate([z(NOPE), -sin, z(16), z(32)], axis=1)
    rs2 = jnp.concatenate([z(NOPE), z(16), sin, z(32)], axis=1)
    return rc, rs1, rs2


def _step(x, tgt, w_blk, shards, g_pre, b_gate, g_q, g_kv, lbl, g_hgrn, g_post):
    s = x.shape[0]
    rc, rs1, rs2 = _rope_tables(s)
    gh = jnp.tile(g_hgrn, (1, HEADS))

    proj, w_in_pt, ht, *got = _gather_proj(x, g_pre, w_blk, shards[:2])
    w_uq, w_ukv = (_from_slots(n, g) for n, g in zip(MATS[:2], got))
    w_uq_p = jnp.pad(w_uq.reshape(Q_LORA, HEADS, QK), ((0, 0), (0, 0), (0, LANE - QK))).reshape(Q_LORA, HEADS * LANE)
    kv3 = w_ukv.reshape(KV_LORA, HEADS, NOPE + VDIM)
    pad64 = lambda t: jnp.pad(t, ((0, 0), (0, 0), (0, LANE - 64))).reshape(KV_LORA, HEADS * LANE)
    w_kv_p = jnp.concatenate([pad64(kv3[:, :, :NOPE]), pad64(kv3[:, :, NOPE:])], axis=1)

    qr, kr, v, cqt, ckvt, *later_shards = _mla_prep(proj, g_q, g_kv, w_uq_p, w_kv_p, rc, rs1, rs2, shards[2:])
    attn, qa, *got = _attn_fwd(qr, kr, v, later_shards)
    w_a, w_b, w_out = (_from_slots(n, g) for n, g in zip(MATS[2:], got))
    o, sprev = _hgrn_fwd(proj, lbl)
    (dout, dproj, dop, do, mt, dy_bf, yat, dya_bf, ybt, dyb_bf,
     loss_vec, dg_post, db_gate, dgh) = _tail(x, tgt, proj, attn, o, w_a, w_b, w_out, w_a.T, w_b.T, w_out.T,
                                               b_gate, g_post, gh)
    dproj, dlbl, *early = _hgrn_bwd(proj, lbl, do, sprev, dproj,
                                    [("w_branch_a", yat, dya_bf), ("w_branch_b", ybt, dyb_bf), ("w_out", mt, dy_bf)])
    dqr, dkr, dv, *early_recv = _attn_bwd(qa, kr, v, dop, early)
    dproj, dg_q, dg_kv, dw_uq_slots, dw_ukv_slots = _mla_bwd(proj, dqr, dkr, dv, g_q, g_kv, w_uq_p.T, w_kv_p.T,
                                                             rc, rs1, rs2, cqt, ckvt, dproj)

    dw_in_slots = _dw_in_slots(ht, dproj)
    late = _pair_reduce([dw_in_slots, dw_uq_slots, dw_ukv_slots])
    dx, dg_pre, *late_recv = _dh_dx(dproj, w_in_pt, x, dout, g_pre, late)

    g_sum = _vectors_sum(dg_pre, db_gate, dg_q, dg_kv, dlbl, dgh, dg_post, loss_vec)
    return dx, late_recv[0], dict(zip(MATS, late_recv[1:] + early_recv)), g_sum


def _adamw(g, w, m, v):
    c1 = 1.0 / (1.0 - ADAM_B1 ** ADAM_STEP)
    c2 = 1.0 / (1.0 - ADAM_B2 ** ADAM_STEP)
    nm = ADAM_B1 * m + (1.0 - ADAM_B1) * g
    nv = ADAM_B2 * v + (1.0 - ADAM_B2) * (g * g)
    d = -ADAM_LR * ((nm * c1) / (jnp.sqrt(nv * c2) + ADAM_EPS) + ADAM_WD * w)
    return d, nm, nv


def _sum8(r_ref):
    g = r_ref[0].astype(F32)
    for k in range(1, r_ref.shape[0]):
        g = g + r_ref[k].astype(F32)
    return g


def _sum_adamw_w_in(recv, w, m, v):
    rows, _, cols = w.shape
    tc = 256
    nc = cols // tc

    def body(r_ref, w_hbm, m_hbm, v_hbm, g_hbm, d_hbm, nm_hbm, nv_hbm, ins, outs, in_sems, out_sems):
        i = pl.program_id(0)
        slot = i & 1
        cols_of = lambda step: pl.ds(pl.multiple_of(step * tc, tc), tc)

        def load(k, step, sl):
            return pltpu.make_async_copy((w_hbm, m_hbm, v_hbm)[k].at[:, 0, cols_of(step)], ins.at[sl, k],
                                         in_sems.at[sl, k])

        def store(k, step, sl):
            return pltpu.make_async_copy(outs.at[sl, k], (g_hbm, d_hbm, nm_hbm, nv_hbm)[k].at[:, 0, cols_of(step)],
                                         out_sems.at[sl, k])

        @pl.when(i == 0)
        def _():
            for k in range(3):
                load(k, 0, 0).start()

        @pl.when(i + 1 < nc)
        def _():
            for k in range(3):
                load(k, i + 1, 1 - slot).start()

        @pl.when(i >= 2)
        def _():
            for k in range(4):
                store(k, i - 2, slot).wait()

        for k in range(3):
            load(k, i, slot).wait()
        g = _sum8(r_ref)
        d, nm, nv = _adamw(g, ins[slot, 0], ins[slot, 1], ins[slot, 2])
        for k, val in enumerate((g, d, nm, nv)):
            outs[slot, k] = val
        for k in range(4):
            store(k, i, slot).start()

        @pl.when(i == nc - 1)
        def _():
            for k in range(4):
                store(k, i, slot).wait()
            if nc >= 2:
                for k in range(4):
                    store(k, i - 1, 1 - slot).wait()

    out = jax.ShapeDtypeStruct((rows, 1, cols), F32)
    return pl.pallas_call(
        body,
        grid=(nc,),
        in_specs=[pl.BlockSpec((recv.shape[0], rows, tc), lambda i: (0, 0, i)), ANY, ANY, ANY],
        out_specs=[ANY, ANY, ANY, ANY],
        out_shape=[out, out, out, out],
        scratch_shapes=[pltpu.VMEM((2, 3, rows, tc), F32), pltpu.VMEM((2, 4, rows, tc), F32),
                        pltpu.SemaphoreType.DMA((2, 3)), pltpu.SemaphoreType.DMA((2, 4))],
        compiler_params=_params(("arbitrary",)),
        name="sum_adamw_w_in",
    )(recv, w, m, v)


def _sum_adamw_whole(recvs, ws, ms, vs):
    n = len(ws)

    def body(*refs):
        r_refs, w_refs, m_refs, v_refs = refs[:n], refs[n:2 * n], refs[2 * n:3 * n], refs[3 * n:4 * n]
        outs = refs[4 * n:]
        for a in range(n):
            g = _sum8(r_refs[a])
            d, nm, nv = _adamw(g, w_refs[a][...], m_refs[a][...], v_refs[a][...])
            outs[a][...] = g
            outs[n + a][...] = d
            outs[2 * n + a][...] = nm
            outs[3 * n + a][...] = nv

    shapes = [jax.ShapeDtypeStruct(w.shape, F32) for w in ws]
    res = pl.pallas_call(
        body,
        out_shape=shapes * 4,
        compiler_params=pltpu.CompilerParams(vmem_limit_bytes=48 * 2**20),
        name="sum_adamw_mats",
    )(*recvs, *ws, *ms, *vs)
    return res[:n], res[n:2 * n], res[2 * n:3 * n], res[3 * n:]


SMALL = ("g_pre", "b_gate", "g_q", "g_kv", "lb_logits", "g_hgrn", "g_post")
SMALL_SHAPE = dict(g_pre=(1, 1024), b_gate=(1, 2048), g_q=(1, 768), g_kv=(1, 256), lb_logits=(2, 512),
                   g_hgrn=(1, 64), g_post=(1, 1024))


def _vectors_sum(dg_pre, db_gate, dg_q, dg_kv, dlbl, dgh, dg_post, loss_vec):
    def body(gpre_ref, bg_ref, gq_ref, gkv_ref, lbl_ref, gh_ref, gpost_ref, loss_ref, out_ref, mine, got,
             send_sems, recv_sems):
        mine[...] = jnp.zeros_like(mine)
        mine[0:1, :] = gpre_ref[...]
        mine[1:2, :] = bg_ref[:, :1024]
        mine[2:3, :] = bg_ref[:, 1024:]
        mine[3:4, :Q_LORA] = gq_ref[...]
        mine[4:5, :KV_LORA] = gkv_ref[...]
        loss = (0.5 / D_MODEL) * jnp.sum(loss_ref[...], axis=-1, keepdims=True)
        mine[4:5, KV_LORA:] = jnp.broadcast_to(loss, (1, 1024 - KV_LORA))
        mine[5:6, :HG_WIDTH] = lbl_ref[0:1, :]
        mine[5:6, HG_WIDTH:] = lbl_ref[1:2, :]
        gh = gh_ref[...]
        fold = gh[:, :VDIM]
        for h in range(1, HEADS):
            fold = fold + gh[:, VDIM * h:VDIM * (h + 1)]
        mine[6:7, :VDIM] = fold
        mine[7:8, :] = gpost_ref[...]
        x, y, c = _my_place()
        me = 4 * x + 2 * y + c
        got[me] = mine[...]
        copies = [pltpu.make_async_remote_copy(
            src_ref=mine, dst_ref=got.at[me], send_sem=send_sems.at[k], recv_sem=recv_sems.at[k],
            device_id=_flip(k, x, y, c), device_id_type=MESH_ID) for k in range(N_DEV - 1)]
        _start_all([], copies)
        _wait_all([], copies)
        out_ref[...] = _sum8(got)

    return pl.pallas_call(
        body,
        out_shape=jax.ShapeDtypeStruct((8, 1024), F32),
        scratch_shapes=[pltpu.VMEM((8, 1024), F32), pltpu.VMEM((N_DEV, 8, 1024), F32),
                        pltpu.SemaphoreType.DMA((7,)), pltpu.SemaphoreType.DMA((7,))],
        name="vectors_sum",
    )(dg_pre, db_gate, dg_q, dg_kv, dlbl, dgh, dg_post, loss_vec)


def _vectors_adamw(g_sum, ws, ms, vs):
    n = len(SMALL)

    def body(g_ref, *refs):
        w_refs, m_refs, v_refs = refs[:n], refs[n:2 * n], refs[2 * n:3 * n]
        loss_ref, outs = refs[3 * n], refs[3 * n + 1:]
        g = g_ref[...]
        loss_ref[...] = g[4:5, KV_LORA:KV_LORA + 1]
        grads = (g[0:1, :], jnp.concatenate([g[1:2, :], g[2:3, :]], axis=1), g[3:4, :Q_LORA], g[4:5, :KV_LORA],
                 jnp.concatenate([g[5:6, :HG_WIDTH], g[5:6, HG_WIDTH:]], axis=0), g[6:7, :VDIM], g[7:8, :])
        for a in range(n):
            d, nm, nv = _adamw(grads[a], w_refs[a][...], m_refs[a][...], v_refs[a][...])
            outs[a][...] = grads[a]
            outs[n + a][...] = d
            outs[2 * n + a][...] = nm
            outs[3 * n + a][...] = nv

    shapes = [jax.ShapeDtypeStruct(SMALL_SHAPE[k], F32) for k in SMALL]
    res = pl.pallas_call(
        body,
        out_shape=[jax.ShapeDtypeStruct((1, 1), F32)] + shapes * 4,
        name="vectors_adamw",
    )(g_sum, *ws, *ms, *vs)
    return res[0], res[1:n + 1], res[n + 1:2 * n + 1], res[2 * n + 1:3 * n + 1], res[3 * n + 1:]


MATS = ("w_uq", "w_ukv", "w_branch_a", "w_branch_b", "w_out")
COL_SHARDED = dict(w_uq=False, w_ukv=True, w_branch_a=True, w_branch_b=True, w_out=False)
ORDER = ("g_pre", "w_in", "b_gate", "g_q", "w_uq", "g_kv", "w_ukv", "lb_logits", "g_hgrn",
         "w_branch_a", "w_branch_b", "w_out", "g_post")


def _from_slots(name, slots):
    _, r, c = slots.shape
    if COL_SHARDED[name]:
        return slots.transpose(1, 0, 2).reshape(r, N_DEV * c)
    return slots.reshape(N_DEV * r, c)


def kernel(x, g_pre, w_in, b_gate, g_q, w_uq, g_kv, w_ukv, lb_logits, g_hgrn, w_branch_a, w_branch_b, w_out, g_post, loss_target, m_g_pre, m_w_in, m_b_gate, m_g_q, m_w_uq, m_g_kv, m_w_ukv, m_lb_logits, m_g_hgrn, m_w_branch_a, m_w_branch_b, m_w_out, m_g_post, v_g_pre, v_w_in, v_b_gate, v_g_q, v_w_uq, v_g_kv, v_w_ukv, v_lb_logits, v_g_hgrn, v_w_branch_a, v_w_branch_b, v_w_out, v_g_post):
    rows3 = lambda a: jnp.transpose(a, (2, 0, 1))
    w = dict(w_in=rows3(w_in), w_uq=w_uq[0], w_ukv=w_ukv[0], w_branch_a=w_branch_a[0], w_branch_b=w_branch_b[0],
             w_out=w_out[0], g_pre=g_pre, b_gate=b_gate, g_q=g_q, g_kv=g_kv, lb_logits=lb_logits, g_hgrn=g_hgrn,
             g_post=g_post)
    mom = dict(w_in=rows3(m_w_in), w_uq=m_w_uq[0], w_ukv=m_w_ukv[0], w_branch_a=m_w_branch_a[0],
               w_branch_b=m_w_branch_b[0], w_out=m_w_out[0], g_pre=m_g_pre, b_gate=m_b_gate, g_q=m_g_q, g_kv=m_g_kv,
               lb_logits=m_lb_logits, g_hgrn=m_g_hgrn, g_post=m_g_post)
    var = dict(w_in=rows3(v_w_in), w_uq=v_w_uq[0], w_ukv=v_w_ukv[0], w_branch_a=v_w_branch_a[0],
               w_branch_b=v_w_branch_b[0], w_out=v_w_out[0], g_pre=v_g_pre, b_gate=v_b_gate, g_q=v_g_q, g_kv=v_g_kv,
               lb_logits=v_lb_logits, g_hgrn=v_g_hgrn, g_post=v_g_post)

    shards = [w[n].astype(BF16) for n in MATS[:2]] + [w[n] for n in MATS[2:]]
    dx, recv_in, recv, g_sum = _step(x[0], loss_target[0], w["w_in"], shards,
                                     g_pre, b_gate, g_q, g_kv, lb_logits, g_hgrn, g_post)

    g_in, d_in, m_in, v_in = _sum_adamw_w_in(recv_in, w["w_in"], mom["w_in"], var["w_in"])
    res = _sum_adamw_whole([recv[n] for n in MATS], *([t[n] for n in MATS] for t in (w, mom, var)))
    total, *vec = _vectors_adamw(g_sum, *([t[n] for n in SMALL] for t in (w, mom, var)))

    outs = []
    for mats, vecs, big in zip(res, vec, (g_in, d_in, m_in, v_in)):
        t = {**{n: a[None] for n, a in zip(MATS, mats)}, **dict(zip(SMALL, vecs)),
             "w_in": jnp.transpose(big, (1, 2, 0))}
        outs += [t[n] for n in ORDER]
    return (total.reshape(()), dx[None], *outs)
```

```python
import math

import jax
import jax.numpy as jnp
import numpy as np
from jax import lax
from jax.experimental import pallas as pl
from jax.experimental.pallas import tpu as pltpu

F32, BF16 = jnp.float32, jnp.bfloat16

D_MODEL = 1024
EPS = 1e-6
HEADS = 8
NOPE, ROPE, VDIM = 64, 32, 64
QK = NOPE + ROPE
Q_LORA, KV_LORA = 768, 256
ROPE_THETA = 10000.0
ATT_CHUNK_SHIFT = 6
HG_BLOCK = 32
HG_WIDTH = 512
D_IN = 5664
D_IN_PAD = 5760
W_IN_SHARD = D_IN // 8
N_DEV = 8
LANE = 128

ADAM_LR, ADAM_B1, ADAM_B2, ADAM_EPS, ADAM_WD, ADAM_STEP = 0.001, 0.9, 0.999, 1e-08, 0.01, 10

W_IN_SEGMENTS = ((3616, 5664, 0), (1056, 1568, 2048), (3104, 3616, 2560), (1568, 3104, 3072),
                 (0, 1024, 4608), (1024, 1056, 5696))

NT = (((1,), (1,)), ((), ()))
TN = (((0,), (0,)), ((), ()))
MESH_ID = pl.DeviceIdType.MESH


def _w_in_pieces():
    out = []
    for lo, hi, dst in W_IN_SEGMENTS:
        c = lo
        while c < hi:
            p = c // W_IN_SHARD
            e = min(hi, (p + 1) * W_IN_SHARD)
            out.append((p, c - p * W_IN_SHARD, e - p * W_IN_SHARD, dst + c - lo))
            c = e
    return out


def _params(sem, vmem_mb=48):
    return pltpu.CompilerParams(dimension_semantics=sem, vmem_limit_bytes=vmem_mb * 2**20)


def _dot(a, b):
    return jnp.dot(a, b, preferred_element_type=F32)


def _dotg(a, b, dims):
    return lax.dot_general(a, b, dims, preferred_element_type=F32)


def _split2(x):
    hi = x.astype(BF16)
    return hi, (x - hi.astype(F32)).astype(BF16)


def _sel_left(m01, x):
    hi, lo = _split2(x)
    return _dot(m01, hi) + _dot(m01, lo)


def _sel_right(x, m01):
    hi, lo = _split2(x)
    return _dot(hi, m01) + _dot(lo, m01)


def _hi_lo(x):
    hi = x.astype(BF16).astype(F32)
    return hi, x - hi


def _sigmoid(x):
    return 0.5 * jnp.tanh(0.5 * x) + 0.5


def _rope(x, c, s1, s2):
    return x * c + pltpu.roll(x, 112, 1) * s1 + pltpu.roll(x, 16, 1) * s2


def _unrope(d, c, s1, s2):
    return d * c + pltpu.roll(d * s1, 16, 1) + pltpu.roll(d * s2, 112, 1)


def _my_place():
    return lax.axis_index("x"), lax.axis_index("y"), lax.axis_index("c")


def _flip(k, x, y, c):
    fx, fy, fc = (k + 1) >> 2 & 1, (k + 1) >> 1 & 1, (k + 1) & 1
    return (1 - x if fx else x), (1 - y if fy else y), (1 - c if fc else c)


def _to_all_copies(s_refs, r_refs, sems, spread):
    send_sems, recv_sems, local_sems = sems
    x, y, c = _my_place()
    me = 4 * x + 2 * y + c
    src = (lambda a, p: s_refs[a]) if spread else (lambda a, p: s_refs[a].at[p])
    local = [pltpu.make_async_copy(src(a, me), r_refs[a].at[me], local_sems.at[a]) for a in range(len(s_refs))]
    remote = []
    for k in range(N_DEV - 1):
        px, py, pc = _flip(k, x, y, c)
        for a in range(len(s_refs)):
            remote.append(pltpu.make_async_remote_copy(
                src_ref=src(a, 4 * px + 2 * py + pc), dst_ref=r_refs[a].at[me],
                send_sem=send_sems.at[7 * a + k], recv_sem=recv_sems.at[7 * a + k],
                device_id=(px, py, pc), device_id_type=MESH_ID))
    return local, remote


def _to_chips_copies(s_refs, r_refs, sems):
    send_sems, recv_sems, local_sems = sems
    x, y, c = _my_place()
    me = 2 * x + y
    local = [pltpu.make_async_copy(s_refs[a].at[me], r_refs[a].at[me], local_sems.at[a]) for a in range(len(s_refs))]
    remote = []
    for k in range(3):
        px = 1 - x if (k + 1) >> 1 & 1 else x
        py = 1 - y if (k + 1) & 1 else y
        for a in range(len(s_refs)):
            remote.append(pltpu.make_async_remote_copy(
                src_ref=s_refs[a].at[2 * px + py], dst_ref=r_refs[a].at[me],
                send_sem=send_sems.at[3 * a + k], recv_sem=recv_sems.at[3 * a + k],
                device_id=(px, py, c), device_id_type=MESH_ID))
    return local, remote


def _start_all(local, remote):
    for cp in local + remote:
        cp.start()


def _wait_all(local, remote):
    for cp in remote:
        cp.wait_recv()
    for cp in remote:
        cp.wait_send()
    for cp in local:
        cp.wait()


def _copy_sems(n, peers):
    return [pltpu.SemaphoreType.DMA((peers * n,)), pltpu.SemaphoreType.DMA((peers * n,)),
            pltpu.SemaphoreType.DMA((n,))]


ANY = pl.BlockSpec(memory_space=pl.ANY)


def _dw_in_slots(ht, dproj):
    m, k = ht.shape
    n = dproj.shape[1]
    tn, tk = 1920, 2048
    nj, nk = n // tn, k // tk
    by_tile = [[] for _ in range(nj)]
    for p, lo, hi, dst in _w_in_pieces():
        while lo < hi:
            j = dst // tn
            cnt = min(hi - lo, (j + 1) * tn - dst)
            by_tile[j].append((p, lo, lo + cnt, dst - j * tn))
            lo, dst = lo + cnt, dst + cnt

    def body(a_ref, b_ref, s_ref, acc_ref):
        j, l = pl.program_id(0), pl.program_id(1)

        @pl.when(l == 0)
        def _():
            acc_ref[...] = jnp.zeros_like(acc_ref)

        acc_ref[...] += _dot(a_ref[...], b_ref[...])

        @pl.when(l == nk - 1)
        def _():
            at = acc_ref[...].T
            for jj in range(nj):
                @pl.when(j == jj)
                def _(jj=jj):
                    for p, lo, hi, d in by_tile[jj]:
                        s_ref[p, lo:hi, :] = at[d:d + hi - lo, :].astype(BF16)

    return pl.pallas_call(
        body,
        grid=(nj, nk),
        in_specs=[pl.BlockSpec((m, tk), lambda j, l: (0, l)), pl.BlockSpec((tk, tn), lambda j, l: (l, j))],
        out_specs=pl.BlockSpec((N_DEV, W_IN_SHARD, m), lambda j, l: (0, 0, 0), pipeline_mode=pl.Buffered(1)),
        out_shape=jax.ShapeDtypeStruct((N_DEV, W_IN_SHARD, m), BF16),
        scratch_shapes=[pltpu.VMEM((m, tn), F32)],
        compiler_params=_params(("arbitrary", "arbitrary"), 56),
        name="dw_in",
    )(ht, dproj)


GP_TN = 256
GP_COLS = 5888
GP_NT = GP_COLS // GP_TN


def _gp_tile_pieces():
    tiles = [[] for _ in range(GP_NT)]
    for p, lo, hi, dst in _w_in_pieces():
        while lo < hi:
            t = dst // GP_TN
            n = min(hi - lo, (t + 1) * GP_TN - dst)
            tiles[t].append((p, lo, lo + n, dst - t * GP_TN))
            lo, dst = lo + n, dst + n
    return tiles


def _gp_tables():
    pieces = _gp_tile_pieces()
    rank_of = {None: 0, 0: 1, 1: 2, 2: 2, 4: 3, 5: 3, 3: 4, 6: 5}
    order = np.zeros((N_DEV, GP_NT), np.int32)
    waits = np.zeros((N_DEV, GP_NT), np.int32)
    for me in range(N_DEV):
        x, y, c = me >> 2 & 1, me >> 1 & 1, me & 1
        chips = [(1 - x, y), (x, 1 - y), (1 - x, 1 - y)]

        def sem_of(p):
            px, py, pc = p >> 2 & 1, p >> 1 & 1, p & 1
            if (px, py) == (x, y):
                return None if pc == c else 0
            j = chips.index((px, py))
            return 1 + j if pc == c else 4 + j

        needs = [sorted({sem_of(p) for p, _, _, _ in tile} - {None}) for tile in pieces]
        ranks = [max([rank_of[k] for k in ks], default=0) for ks in needs]
        seq = sorted(range(GP_NT), key=lambda t: (ranks[t], t))
        seen = set()
        for step, t in enumerate(seq):
            order[me, step] = t
            new = [k for k in needs[t] if k not in seen]
            for k in new:
                waits[me, step] |= 1 << k
            seen.update(new)
        assert seen == set(range(7)), (me, seen)
    return order, waits


def _gather_proj(x, g_pre, w_blk, shards):
    s = x.shape[0]
    tx = 512
    ns = len(shards)
    tile_pieces = _gp_tile_pieces()
    order_np, waits_np = _gp_tables()
    xq, yq, cq = _my_place()
    me_out = 4 * xq + 2 * yq + cq
    order = lax.dynamic_index_in_dim(jnp.asarray(order_np), me_out, 0, keepdims=False)
    waits = lax.dynamic_index_in_dim(jnp.asarray(waits_np), me_out, 0, keepdims=False)

    def body(order_ref, waits_ref, x_hbm, g_ref, wblk_hbm, *rest):
        shard_refs, (proj_ref, wt_ref, ht_hbm), got_refs = rest[:ns], rest[ns:ns + 3], rest[ns + 3:2 * ns + 3]
        recv, h_ref, wtile, xbuf, htbuf = rest[2 * ns + 3:2 * ns + 8]
        send_sems, recv_sems, misc_sems = rest[2 * ns + 8:2 * ns + 11]
        sems = rest[2 * ns + 11:]
        t = pl.program_id(0)
        x_, y_, c = _my_place()
        sibling = (x_, y_, 1 - c)
        chips = [(1 - x_, y_), (x_, 1 - y_), (1 - x_, 1 - y_)]
        idx = lambda px, py, pc: 4 * px + 2 * py + pc
        me = idx(x_, y_, c)

        def copy(k, slot, to, src=None):
            return pltpu.make_async_remote_copy(
                src_ref=recv.at[slot] if src is None else src, dst_ref=recv.at[slot],
                send_sem=send_sems.at[k], recv_sem=recv_sems.at[k], device_id=to, device_id_type=MESH_ID)

        mine = pltpu.make_async_copy(wblk_hbm, recv.at[me], misc_sems.at[0])
        first = [copy(0, me, sibling, src=wblk_hbm)] + [copy(1 + j, me, (*chips[j], c), src=wblk_hbm) for j in range(2)]
        passed = [copy(4 + j, idx(*ch, c), sibling) for j, ch in enumerate(chips)]
        onward = [copy(3, idx(*chips[0], c), (*chips[1], c)), copy(3, idx(*chips[1], c), (*chips[0], c))]
        arrivals = ([copy(0, idx(x_, y_, 1 - c), sibling)] + [copy(1 + j, idx(*ch, c), sibling) for j, ch in enumerate(chips)]
                    + [copy(4 + j, idx(*ch, 1 - c), sibling) for j, ch in enumerate(chips)])

        @pl.when(t == 0)
        def _():
            mine.start()
            for cp in first:
                cp.start()
            _start_all(*_to_all_copies(shard_refs, got_refs, sems, True))

            def load(i):
                return pltpu.make_async_copy(x_hbm.at[pl.ds(i * tx, tx), :], xbuf.at[i & 1], misc_sems.at[1 + (i & 1)])

            def store(i):
                return pltpu.make_async_copy(htbuf.at[i & 1], ht_hbm.at[:, pl.ds(i * tx, tx)], misc_sems.at[3 + (i & 1)])

            load(0).start()
            for i in range(s // tx):
                if i + 1 < s // tx:
                    load(i + 1).start()
                load(i).wait()
                xv = xbuf[i & 1]
                r = lax.rsqrt(jnp.mean(xv * xv, axis=-1, keepdims=True) + EPS)
                h = (xv * r * g_ref[...]).astype(BF16)
                h_ref[i * tx:(i + 1) * tx, :] = h
                if i >= 2:
                    store(i - 2).wait()
                htbuf[i & 1] = h.T
                store(i).start()
            for i in range(max(s // tx - 2, 0), s // tx):
                store(i).wait()
            mine.wait()

        w = waits_ref[t]
        for k in range(7):
            @pl.when((w >> k) & 1 == 1)
            def _(k=k):
                arrivals[k].wait_recv()
                if 1 <= k <= 3:
                    passed[k - 1].start()
                if 1 <= k <= 2:
                    @pl.when(c == k - 1)
                    def _():
                        onward[k - 1].start()

        tile = order_ref[t]
        for tt in range(GP_NT):
            @pl.when(tile == tt)
            def _(tt=tt):
                covered = sorted((d, d + hi - lo) for _, lo, hi, d in tile_pieces[tt])
                at = 0
                for lo_z, hi_z in covered + [(GP_TN, GP_TN)]:
                    if lo_z > at:
                        wtile[at:lo_z, :] = jnp.zeros((lo_z - at, D_MODEL), BF16)
                    at = max(at, hi_z)
                for p, lo, hi, d in tile_pieces[tt]:
                    wtile[d:d + hi - lo, :] = recv[p, lo:hi, :]

        wt = wtile[...]
        wt_ref[...] = wt
        proj_ref[...] = _dotg(h_ref[...], wt, NT)

        @pl.when(t == GP_NT - 1)
        def _():
            for cp in first + passed + onward[:1]:
                cp.wait_send()
            _wait_all(*_to_all_copies(shard_refs, got_refs, sems, True))

    grid_spec = pltpu.PrefetchScalarGridSpec(
        num_scalar_prefetch=2,
        grid=(GP_NT,),
        in_specs=[ANY, pl.BlockSpec((1, D_MODEL), lambda t, o, w: (0, 0)), ANY] + [ANY] * ns,
        out_specs=[pl.BlockSpec((s, GP_TN), lambda t, o, w: (0, o[t])),
                   pl.BlockSpec((GP_TN, D_MODEL), lambda t, o, w: (o[t], 0)), ANY] + [ANY] * ns,
        scratch_shapes=[pltpu.VMEM((N_DEV, W_IN_SHARD, D_MODEL), BF16), pltpu.VMEM((s, D_MODEL), BF16),
                        pltpu.VMEM((GP_TN, D_MODEL), BF16), pltpu.VMEM((2, tx, D_MODEL), F32),
                        pltpu.VMEM((2, D_MODEL, tx), BF16),
                        pltpu.SemaphoreType.DMA((7,)), pltpu.SemaphoreType.DMA((7,)), pltpu.SemaphoreType.DMA((5,))]
        + _copy_sems(ns, 7),
    )
    return pl.pallas_call(
        body,
        grid_spec=grid_spec,
        out_shape=[jax.ShapeDtypeStruct((s, GP_COLS), F32),jax.ShapeDtypeStruct((GP_COLS, D_MODEL), BF16),
                   jax.ShapeDtypeStruct((D_MODEL, s), BF16)]
        + [jax.ShapeDtypeStruct((N_DEV,) + b.shape, b.dtype) for b in shards],
        compiler_params=_params(("arbitrary",), 56),
        name="gather_proj",
    )(order, waits, x, g_pre, w_blk, *shards)


def _mla_prep(proj, g_q, g_kv, w_uq_p, w_kv_p, rc, rs1, rs2, casts):
    s = proj.shape[0]
    tm = 512
    nc = len(casts)
    scale = 1.0 / math.sqrt(QK)

    def body(cq_ref, ckv_ref, kpe_ref, gq_ref, gkv_ref, wuq_ref, wkv_ref, c_ref, s1_ref, s2_ref, *rest):
        cast_in, (qr_ref, kr_ref, v_ref, cqt_ref, ckvt_ref), cast_out = rest[:nc], rest[nc:nc + 5], rest[nc + 5:]

        @pl.when(pl.program_id(0) == 0)
        def _():
            for src, dst in zip(cast_in, cast_out):
                dst[...] = src[...].astype(BF16)

        cq = cq_ref[...]
        r = lax.rsqrt(jnp.mean(cq * cq, axis=-1, keepdims=True) + EPS)
        cqn = (cq * r * gq_ref[...]).astype(BF16)
        cqt_ref[...] = cqn.T
        q = _dot(cqn, wuq_ref[...])
        ckv = ckv_ref[...]
        r = lax.rsqrt(jnp.mean(ckv * ckv, axis=-1, keepdims=True) + EPS)
        ckvn = (ckv * r * gkv_ref[...]).astype(BF16)
        ckvt_ref[...] = ckvn.T
        kv = _dot(ckvn, wkv_ref[...])
        c, s1, s2 = c_ref[...], s1_ref[...], s2_ref[...]
        lane = lax.broadcasted_iota(jnp.int32, (tm, LANE), 1)
        kpe = _rope(kpe_ref[...], c, s1, s2) + jnp.where((lane == QK) | (lane == QK + 1), 1.0, 0.0)
        vone = jnp.where((lane == VDIM) | (lane == VDIM + 1), 1.0, 0.0)
        for h in range(HEADS):
            sl = slice(LANE * h, LANE * (h + 1))
            qr_ref[:, sl] = (_rope(q[:, sl], c, s1, s2) * scale).astype(BF16)
            kr_ref[:, sl] = (kv[:, sl] + kpe).astype(BF16)
            v_ref[:, sl] = (kv[:, HEADS * LANE + LANE * h:HEADS * LANE + LANE * (h + 1)] + vone).astype(BF16)

    row = lambda w, j: pl.BlockSpec((tm, w), lambda i: (i, j))
    col = lambda w: pl.BlockSpec((w, tm), lambda i: (0, i))
    full = lambda a: pl.BlockSpec(a.shape, lambda i: (0, 0))
    return pl.pallas_call(
        body,
        grid=(s // tm,),
        in_specs=[row(768, 6), row(256, 21), row(128, 44), full(g_q), full(g_kv), full(w_uq_p), full(w_kv_p),
                  row(128, 0), row(128, 0), row(128, 0)] + [full(a) for a in casts],
        out_specs=[row(1024, 0), row(1024, 0), row(1024, 0), col(768), col(256)] + [full(a) for a in casts],
        out_shape=[jax.ShapeDtypeStruct((s, 1024), BF16), jax.ShapeDtypeStruct((s, 1024), BF16),
                   jax.ShapeDtypeStruct((s, 1024), BF16), jax.ShapeDtypeStruct((768, s), BF16),
                   jax.ShapeDtypeStruct((256, s), BF16)] + [jax.ShapeDtypeStruct(a.shape, BF16) for a in casts],
        compiler_params=_params(("arbitrary",)),
        name="mla_prep",
    )(proj, proj, proj, g_q, g_kv, w_uq_p, w_kv_p, rc, rs1, rs2, *casts)


ATT_T = 512
ATT_FWD_HEADS = 4


def _chunk_mask(transposed):
    r = lax.broadcasted_iota(jnp.int32, (ATT_T, ATT_T), 0) >> ATT_CHUNK_SHIFT
    c = lax.broadcasted_iota(jnp.int32, (ATT_T, ATT_T), 1) >> ATT_CHUNK_SHIFT
    return (r <= c) if transposed else (c <= r)


def _attn_fwd(qr, kr, vp, shards):
    s = qr.shape[0]
    t = ATT_T
    g = ATT_FWD_HEADS
    ns = len(shards)

    def body(q_ref, k_ref, v_ref, *rest):
        shard_refs, (o_ref, qa_ref), got_refs = rest[:ns], rest[ns:ns + 2], rest[ns + 2:2 * ns + 2]
        sc_ref, sems = rest[2 * ns + 2], rest[2 * ns + 3:]
        qi = pl.program_id(1)

        @pl.when((pl.program_id(0) == 0) & (qi == 0))
        def _():
            _start_all(*_to_all_copies(shard_refs, got_refs, sems, True))
        lane = lax.broadcasted_iota(jnp.int32, (t, LANE), 1)
        sls = [slice(LANE * a, LANE * (a + 1)) for a in range(g)]
        qs = [q_ref[:, sl] for sl in sls]

        def scores(j):
            rows = pl.ds(pl.multiple_of(j * t, t), t)
            for a in range(g):
                sc_ref[j & 1, a] = _dotg(qs[a], k_ref[rows, sls[a]], NT)

        def step(j, carry, masked):
            rows = pl.ds(pl.multiple_of(j * t, t), t)
            out = []
            for a in range(g):
                m, acc = carry[a]
                sc = sc_ref[j & 1, a]
                if masked:
                    sc = jnp.where(_chunk_mask(False), sc, -1e30)
                m_new = jnp.maximum(m, jnp.max(sc, axis=-1, keepdims=True))
                p = jnp.exp(sc - m_new).astype(BF16)
                acc = jnp.exp(m - m_new) * acc + _dot(p, v_ref[rows, sls[a]])
                out.append((m_new, acc))
            return tuple(out)

        def loop(j, carry):
            carry = step(j, carry, False)
            scores(j + 1)
            return carry

        init = tuple((jnp.full((t, 1), -1e30, F32), jnp.zeros((t, LANE), F32)) for _ in range(g))
        scores(0)
        carry = lax.fori_loop(0, qi, loop, init)
        carry = step(qi, carry, True)
        outs = []
        for a in range(g):
            m, acc = carry[a]
            l = acc[:, VDIM:VDIM + 1]
            outs.append(acc / l)
            hi, lo_part = _hi_lo(-(m + jnp.log(l)))
            qa = jnp.where(lane == QK, hi, jnp.where(lane == QK + 1, lo_part, qs[a].astype(F32)))
            qa_ref[:, sls[a]] = qa.astype(BF16)
        for p in range(g // 2):
            o_ref[:, LANE * p:LANE * (p + 1)] = jnp.where(lane < VDIM, outs[2 * p], pltpu.roll(outs[2 * p + 1], VDIM, 1))

        @pl.when((pl.program_id(0) == HEADS // g - 1) & (qi == s // t - 1))
        def _():
            _wait_all(*_to_all_copies(shard_refs, got_refs, sems, True))

    return pl.pallas_call(
        body,
        grid=(HEADS // g, s // t),
        in_specs=[
            pl.BlockSpec((t, g * LANE), lambda h, i: (i, h)),
            pl.BlockSpec((s, g * LANE), lambda h, i: (0, h)),
            pl.BlockSpec((s, g * LANE), lambda h, i: (0, h)),
        ] + [ANY] * ns,
        out_specs=[
            pl.BlockSpec((t, g * VDIM), lambda h, i: (i, h)),
            pl.BlockSpec((t, g * LANE), lambda h, i: (i, h)),
        ] + [ANY] * ns,
        out_shape=[jax.ShapeDtypeStruct((s, 512), F32), jax.ShapeDtypeStruct((s, 1024), BF16)]
        + [jax.ShapeDtypeStruct((N_DEV,) + b.shape, b.dtype) for b in shards],
        scratch_shapes=[pltpu.VMEM((2, g, t, t), F32)] + _copy_sems(ns, 7),
        compiler_params=_params(("arbitrary", "arbitrary")),
        name="attn_fwd",
    )(qr, kr, vp, *shards)


def _attn_bwd(qa, kr, vp, dop, sends):
    s = qa.shape[0]
    t = ATT_T
    nq = s // t
    ns = len(sends)

    def body(q_ref, k_ref, v_ref, do_ref, *rest):
        send_refs, (dq_out, dk_out, dv_out) = rest[:ns], rest[ns:ns + 3]
        recv_refs = rest[ns + 3:2 * ns + 3]
        (dq_ref, dk_ref, dv_ref), sems = rest[2 * ns + 3:2 * ns + 6], rest[2 * ns + 6:]
        j = pl.program_id(1)
        sls = [slice(LANE * a, LANE * (a + 1)) for a in range(2)]

        @pl.when((pl.program_id(0) == 0) & (j == 0))
        def _():
            _start_all(*_to_all_copies(send_refs, recv_refs, sems, False))

        @pl.when(j == 0)
        def _():
            dq_ref[...] = jnp.zeros_like(dq_ref)

        dk_ref[...] = jnp.zeros_like(dk_ref)
        dv_ref[...] = jnp.zeros_like(dv_ref)
        ks = [k_ref[:, sl] for sl in sls]
        vs = [v_ref[:, sl] for sl in sls]

        def part(i, k_lo, k_n, q_lo, q_n, masked):
            rows = pl.ds(pl.multiple_of(i * t + q_lo, 256), q_n)
            keys = slice(k_lo, k_lo + k_n)
            for a in range(2):
                q = q_ref[rows, sls[a]]
                do = do_ref[rows, sls[a]]
                sc = _dotg(ks[a][keys], q, NT)
                if masked:
                    kc = lax.broadcasted_iota(jnp.int32, (k_n, q_n), 0) >> ATT_CHUNK_SHIFT
                    qc = lax.broadcasted_iota(jnp.int32, (k_n, q_n), 1) >> ATT_CHUNK_SHIFT
                    sc = jnp.where(kc <= qc, sc, -1e30)
                p = jnp.exp(sc)
                ds = (p * _dotg(vs[a][keys], do, NT)).astype(BF16)
                dv_ref[keys, sls[a]] += _dot(p.astype(BF16), do)
                dk_ref[keys, sls[a]] += _dot(ds, q)
                dq_ref[rows, sls[a]] += _dotg(ds, ks[a][keys], TN)

        half = t // 2
        part(j, 0, half, 0, t, True)
        part(j, half, half, half, half, True)

        def loop(i, c):
            part(i, 0, t, 0, t, False)
            return c

        lax.fori_loop(j + 1, nq, loop, 0)
        dk_out[...] = dk_ref[...].astype(BF16)
        dv_out[...] = dv_ref[...].astype(BF16)

        @pl.when(j == nq - 1)
        def _():
            dq_out[...] = dq_ref[...].astype(BF16)

        @pl.when((pl.program_id(0) == HEADS // 2 - 1) & (j == nq - 1))
        def _():
            _wait_all(*_to_all_copies(send_refs, recv_refs, sems, False))

    blk = pl.BlockSpec((t, 2 * LANE), lambda h, j: (j, h))
    whole = pl.BlockSpec((s, 2 * LANE), lambda h, j: (0, h))
    out = jax.ShapeDtypeStruct((s, 1024), BF16)
    return pl.pallas_call(
        body,
        grid=(HEADS // 2, nq),
        in_specs=[whole, blk, blk, whole] + [ANY] * ns,
        out_specs=[whole, blk, blk] + [ANY] * ns,
        out_shape=[out, out, out] + [jax.ShapeDtypeStruct(a.shape, a.dtype) for a in sends],
        scratch_shapes=[pltpu.VMEM((s, 2 * LANE), F32), pltpu.VMEM((t, 2 * LANE), F32),
                        pltpu.VMEM((t, 2 * LANE), F32)] + _copy_sems(ns, 7),
        compiler_params=_params(("arbitrary", "arbitrary")),
        name="attn_bwd",
    )(qa, kr, vp, dop, *sends)


HG_T = 256
HG_NC = HG_T // HG_BLOCK
HG_G = 4
GW = 64 * HG_G


def _hg_consts():
    r = jnp.arange(HG_T)[:, None]
    c = jnp.arange(HG_T)[None, :]
    same = (r // HG_BLOCK) == (c // HG_BLOCK)
    mcum = (same & (c <= r)).astype(BF16)
    mrev = (same & (c >= r)).astype(BF16)
    msum = same.astype(BF16)
    a = jnp.arange(GW) // 64
    bd = (a[:, None] == a[None, :]).astype(F32)
    return mcum, mrev, msum, bd


def _stack_heads(xg, head):
    return jnp.concatenate([jnp.where(head == h, xg, 0.0) for h in range(HG_G)], axis=0)


def _unstack_heads(r, head, t):
    out = r[(HG_G - 1) * t:]
    for h in range(HG_G - 2, -1, -1):
        out = jnp.where(head == h, r[h * t:(h + 1) * t], out)
    return out


def _compact_state(st):
    out = st[:64]
    for h in range(1, HG_G):
        out = out + st[64 * h:64 * (h + 1)]
    return out


def _expand_state(cs, head64):
    return jnp.concatenate([jnp.where(head64 == h, cs, 0.0) for h in range(HG_G)], axis=0)


def _hg_pre(hq, hf, lbl, mcum, msum):
    lb = _sigmoid(lbl[0:1, :] - lbl[1:2, :])
    sig = _sigmoid(hf)
    f = lb + (1.0 - lb) * sig
    lf = jnp.log(f)
    b = _sel_left(mcum, lf)
    big_l = _sel_left(msum, lf)
    k = 1.0 - f
    qd = hq * jnp.exp(b)
    ki = k * jnp.exp(-b)
    ke = k * jnp.exp(big_l - b)
    return lb, sig, f, b, big_l, qd, ki, ke


def _hgrn_fwd(proj, lbl):
    s = proj.shape[0]
    t = HG_T
    mcum, _, msum, bd = _hg_consts()

    def body(hq_ref, hf_ref, hi_ref, lbl_ref, mcum_ref, msum_ref, bd_ref, o_ref, sp_ref, st_ref):
        @pl.when(pl.program_id(0) == 0)
        def _():
            st_ref[...] = jnp.zeros_like(st_ref)

        mc = mcum_ref[...]
        _, _, _, _, big_l, qd, ki, ke = _hg_pre(hq_ref[...], hf_ref[...], lbl_ref[...], mc, msum_ref[...])
        el = jnp.exp(big_l)
        hi = hi_ref[...]
        head = lax.broadcasted_iota(jnp.int32, (t, GW), 1) >> 6
        mask = jnp.concatenate([mc] * HG_G, axis=0) > 0.5
        for p in range(HEADS // HG_G):
            sl = slice(GW * p, GW * (p + 1))
            vp = hi[:, sl].astype(BF16)
            qs = _stack_heads(qd[:, sl], head).astype(BF16)
            a = jnp.where(mask, _dotg(qs, ki[:, sl].astype(BF16), NT), 0.0)
            o_intra = _unstack_heads(_dot(a.astype(BF16), vp), head, t)
            qb = qd[:, sl].astype(BF16)
            kb = ke[:, sl].astype(BF16)
            st = st_ref[p]
            for c in range(HG_NC):
                rows = slice(HG_BLOCK * c, HG_BLOCK * (c + 1))
                sp_ref[c, :, sl] = _compact_state(st)
                o_ref[rows, sl] = o_intra[rows] + _dotg(qb[rows], st.astype(BF16), NT)
                u = _dotg(vp[rows], kb[rows], TN) * bd_ref[...]
                st = st * el[HG_BLOCK * c:HG_BLOCK * c + 1, sl] + u
            st_ref[p] = st

    row = lambda j: pl.BlockSpec((t, HG_WIDTH), lambda i: (i, j))
    full = lambda a: pl.BlockSpec(a.shape, lambda i: (0, 0))
    return pl.pallas_call(
        body,
        grid=(s // t,),
        in_specs=[row(6), row(7), row(8), full(lbl), full(mcum), full(msum), full(bd)],
        out_specs=[row(0), pl.BlockSpec((HG_NC, 64, HG_WIDTH), lambda i: (i, 0, 0))],
        out_shape=[jax.ShapeDtypeStruct((s, HG_WIDTH), F32),
                   jax.ShapeDtypeStruct((s // HG_BLOCK, 64, HG_WIDTH), F32)],
        scratch_shapes=[pltpu.VMEM((HEADS // HG_G, GW, GW), F32)],
        compiler_params=_params(("arbitrary",)),
        name="hgrn_fwd",
    )(proj, proj, proj, lbl, mcum, msum, bd)


def _slot_shape(name, r, c):
    return (N_DEV, r, c // N_DEV) if COL_SHARDED[name] else (N_DEV, r // N_DEV, c)


def _emit_slots(name, acc_ref, out_ref):
    r, c = acc_ref.shape
    for p in range(N_DEV):
        if COL_SHARDED[name]:
            out_ref[p] = acc_ref[:, c // N_DEV * p:c // N_DEV * (p + 1)].astype(BF16)
        else:
            out_ref[p] = acc_ref[r // N_DEV * p:r // N_DEV * (p + 1), :].astype(BF16)


def _hgrn_bwd(proj, lbl, do, sprev, dproj, pairs):
    s = proj.shape[0]
    t = HG_T
    nt = s // t
    npair = len(pairs)
    mcum, mrev, msum, bd = _hg_consts()

    def body(hq_ref, hf_ref, hi_ref, lbl_ref, do_ref, sp_ref, mcum_ref, mrev_ref, msum_ref, bd_ref,
             dproj_in, *rest):
        del dproj_in
        pair_refs, (dh_ref, dlbl_ref) = rest[:2 * npair], rest[2 * npair:2 * npair + 2]
        dw_refs, g_ref, acc_refs = rest[2 * npair + 2:3 * npair + 2], rest[3 * npair + 2], rest[3 * npair + 3:]

        @pl.when(pl.program_id(0) == 0)
        def _():
            g_ref[...] = jnp.zeros_like(g_ref)
            dlbl_ref[...] = jnp.zeros_like(dlbl_ref)
            for acc_ref in acc_refs:
                acc_ref[...] = jnp.zeros_like(acc_ref)

        for n, acc_ref in enumerate(acc_refs):
            acc_ref[...] += _dot(pair_refs[2 * n][...], pair_refs[2 * n + 1][...])

        @pl.when(pl.program_id(0) == nt - 1)
        def _():
            for (name, _, _), acc_ref, dw_ref in zip(pairs, acc_refs, dw_refs):
                _emit_slots(name, acc_ref, dw_ref)

        mc = mcum_ref[...]
        lb, sig, f, b, big_l, qd, ki, ke = _hg_pre(hq_ref[...], hf_ref[...], lbl_ref[...], mc, msum_ref[...])
        el = jnp.exp(big_l)
        hi = hi_ref[...]
        dov = do_ref[...]
        head = lax.broadcasted_iota(jnp.int32, (t, GW), 1) >> 6
        head64 = lax.broadcasted_iota(jnp.int32, (64, GW), 1) >> 6
        mask = jnp.concatenate([mc] * HG_G, axis=0) > 0.5
        dqd_parts, dke_parts, dv_parts, del_parts, dki_parts = [], [], [], [], []
        for p in range(HEADS // HG_G):
            sl = slice(GW * p, GW * (p + 1))
            vp = hi[:, sl].astype(BF16)
            qs = _stack_heads(qd[:, sl], head).astype(BF16)
            kip = ki[:, sl].astype(BF16)
            dos = _stack_heads(dov[:, sl], head).astype(BF16)
            a = jnp.where(mask, _dotg(qs, kip, NT), 0.0).astype(BF16)
            da = jnp.where(mask, _dotg(dos, vp, NT), 0.0).astype(BF16)
            r = _dot(da, kip)
            dki_parts.append(_dotg(da, qs, TN))
            qb = qd[:, sl].astype(BF16)
            kb = ke[:, sl].astype(BF16)
            dob = dov[:, sl].astype(BF16)
            g = g_ref[p]
            dqd_c, dv_c, dke_c, del_c = [], [], [], []
            for c in range(HG_NC - 1, -1, -1):
                rows = slice(HG_BLOCK * c, HG_BLOCK * (c + 1))
                gb = g.astype(BF16)
                st = _expand_state(sp_ref[c, :, sl], head64)
                dqd_c.append(_dot(dob[rows], st.astype(BF16)))
                dv_c.append(_dotg(kb[rows], gb, NT))
                dke_c.append(_dot(vp[rows], gb))
                del_c.append(jnp.broadcast_to(jnp.sum(g * st, axis=0, keepdims=True), (HG_BLOCK, GW)))
                g = g * el[HG_BLOCK * c:HG_BLOCK * c + 1, sl] + _dotg(dob[rows], qb[rows], TN) * bd_ref[...]
            g_ref[p] = g
            up = lambda parts: jnp.concatenate(parts[::-1], axis=0)
            dqd_parts.append(_unstack_heads(r, head, t) + up(dqd_c))
            dv_parts.append(_dotg(a, dos, TN) + up(dv_c))
            dke_parts.append(up(dke_c))
            del_parts.append(up(del_c))
        wide = lambda parts: jnp.concatenate(parts, axis=1)
        dqd, dke, dki, dvv, del_rows = wide(dqd_parts), wide(dke_parts), wide(dki_parts), wide(dv_parts), wide(del_parts)
        dh_ref[:, :HG_WIDTH] = (dqd * jnp.exp(b)).astype(BF16)
        dh_ref[:, 2 * HG_WIDTH:] = dvv.astype(BF16)
        dke_ke = dke * ke
        db = dqd * qd - dki * ki - dke_ke
        dl_rows = _sel_left(msum_ref[...], dke_ke) + del_rows * el
        is_last = (lax.broadcasted_iota(jnp.int32, (t, HG_WIDTH), 0) & (HG_BLOCK - 1)) == HG_BLOCK - 1
        db = db + jnp.where(is_last, dl_rows, 0.0)
        dlf = _sel_left(mrev_ref[...], db)
        dk = dki * jnp.exp(-b) + dke * jnp.exp(big_l - b)
        df = dlf / f - dk
        dh_ref[:, HG_WIDTH:2 * HG_WIDTH] = (df * (1.0 - lb) * sig * (1.0 - sig)).astype(BF16)
        dlb = jnp.sum(df * (1.0 - sig), axis=0, keepdims=True) * lb * (1.0 - lb)
        dlbl_ref[0:1, :] += dlb
        dlbl_ref[1:2, :] -= dlb

    rrow = lambda j: pl.BlockSpec((t, HG_WIDTH), lambda i: (nt - 1 - i, j))
    full = lambda a: pl.BlockSpec(a.shape, lambda i: (0, 0))
    pair_specs, dw_specs, dw_shapes, accs = [], [], [], []
    for name, at, b in pairs:
        pair_specs += [pl.BlockSpec((at.shape[0], t), lambda i: (0, i)), pl.BlockSpec((t, b.shape[1]), lambda i: (i, 0))]
        shape = _slot_shape(name, at.shape[0], b.shape[1])
        dw_specs.append(pl.BlockSpec(shape, lambda i: (0, 0, 0)))
        dw_shapes.append(jax.ShapeDtypeStruct(shape, BF16))
        accs.append(pltpu.VMEM((at.shape[0], b.shape[1]), F32))
    return pl.pallas_call(
        body,
        grid=(nt,),
        in_specs=[rrow(6), rrow(7), rrow(8), full(lbl), rrow(0),
                  pl.BlockSpec((HG_NC, 64, HG_WIDTH), lambda i: (nt - 1 - i, 0, 0)),
                  full(mcum), full(mrev), full(msum), full(bd), pl.BlockSpec(memory_space=pl.ANY)] + pair_specs,
        out_specs=[pl.BlockSpec((t, 3 * HG_WIDTH), lambda i: (nt - 1 - i, 2)),
                   pl.BlockSpec((2, HG_WIDTH), lambda i: (0, 0))] + dw_specs,
        out_shape=[jax.ShapeDtypeStruct(dproj.shape, BF16), jax.ShapeDtypeStruct((2, HG_WIDTH), F32)] + dw_shapes,
        input_output_aliases={10: 0},
        scratch_shapes=[pltpu.VMEM((HEADS // HG_G, GW, GW), F32)] + accs,
        compiler_params=_params(("arbitrary",)),
        name="hgrn_bwd",
    )(proj, proj, proj, lbl, do, sprev, mcum, mrev, msum, bd, dproj, *[a for pair in pairs for a in pair[1:]])


def _tail(x, tgt, proj, attn, o, w_a, w_b, w_out, w_at, w_bt, w_outt, b_gate, g_post, gh):
    s = x.shape[0]
    tm = 256
    ones64 = (jnp.arange(HG_WIDTH)[:, None] // 64 == jnp.arange(HG_WIDTH)[None, :] // 64).astype(BF16)
    weights = (w_a, w_b, w_out, w_at, w_bt, w_outt)

    def body(x_ref, t_ref, ml_ref, ga_ref, gb_ref, at_ref, o_ref, *rest):
        w_hbm, (bg_ref, gp_ref, gh_ref, ones_ref) = rest[:6], rest[6:10]
        (dout_ref, dpj_ref, dop_ref, do_ref, mt_ref, dy_ref, yat_ref, dya_ref, ybt_ref, dyb_ref,
         loss_ref, dgp_ref, dbg_ref, dgh_ref) = rest[10:24]
        (wa_ref, wb_ref, wo_ref, wat_ref, wbt_ref, wot_ref), w_sem = rest[24:30], rest[30]

        @pl.when(pl.program_id(0) == 0)
        def _():
            loads = [pltpu.make_async_copy(src, dst, w_sem.at[k])
                     for k, (src, dst) in enumerate(zip(w_hbm, rest[24:30]))]
            _start_all(loads, [])
            loss_ref[...] = jnp.zeros_like(loss_ref)
            dgp_ref[...] = jnp.zeros_like(dgp_ref)
            dbg_ref[...] = jnp.zeros_like(dbg_ref)
            dgh_ref[...] = jnp.zeros_like(dgh_ref)
            _wait_all(loads, [])

        ones = ones_ref[...]
        gate_a = ga_ref[...]
        sa = _sigmoid(gate_a)
        silu_a = gate_a * sa
        attn_v = at_ref[...]
        ya_in = attn_v * silu_a
        ov = o_ref[...]
        ro = lax.rsqrt(_sel_right(ov * ov, ones) * (1.0 / 64.0) + EPS)
        ohat = ov * ro
        ghv = gh_ref[...]
        on = ohat * ghv
        gate_b = gb_ref[...]
        sb = _sigmoid(gate_b)
        silu_b = gate_b * sb
        yb_in = on * silu_b
        ya_bf = ya_in.astype(BF16)
        yb_bf = yb_in.astype(BF16)
        yat_ref[...] = ya_bf.T
        ybt_ref[...] = yb_bf.T
        y_a = _dot(ya_bf, wa_ref[...])
        y_b = _dot(yb_bf, wb_ref[...])
        gts = _sigmoid(ml_ref[...] + bg_ref[...])
        g_a = gts[:, :D_MODEL]
        g_b = gts[:, D_MODEL:]
        m_bf = (g_a * y_a + g_b * y_b).astype(BF16)
        mt_ref[...] = m_bf.T
        y = _dot(m_bf, wo_ref[...])
        r1 = lax.rsqrt(jnp.mean(y * y, axis=-1, keepdims=True) + EPS)
        yn = y * r1
        gp = gp_ref[...]
        e = x_ref[...] + yn * gp - t_ref[...]
        loss_ref[...] += jnp.sum(e * e, axis=0, keepdims=True)
        dout = e * (1.0 / D_MODEL)
        dout_ref[...] = dout
        dgp_ref[...] += jnp.sum(dout * yn, axis=0, keepdims=True)
        dyn = dout * gp
        dy = r1 * (dyn - yn * jnp.mean(dyn * yn, axis=-1, keepdims=True))
        dy_bf = dy.astype(BF16)
        dy_ref[...] = dy_bf
        dm = _dot(dy_bf, wot_ref[...])
        dml_a = dm * y_a * g_a * (1.0 - g_a)
        dml_b = dm * y_b * g_b * (1.0 - g_b)
        dpj_ref[:, :D_MODEL] = dml_a.astype(BF16)
        dpj_ref[:, D_MODEL:2 * D_MODEL] = dml_b.astype(BF16)
        dbg_ref[:, :D_MODEL] += jnp.sum(dml_a, axis=0, keepdims=True)
        dbg_ref[:, D_MODEL:] += jnp.sum(dml_b, axis=0, keepdims=True)
        dya_bf = (dm * g_a).astype(BF16)
        dyb_bf = (dm * g_b).astype(BF16)
        dya_ref[...] = dya_bf
        dyb_ref[...] = dyb_bf
        dya_in = _dot(dya_bf, wat_ref[...])
        dyb_in = _dot(dyb_bf, wbt_ref[...])
        dattn = dya_in * silu_a
        delta = _sel_right(dattn * attn_v, ones)
        lane = lax.broadcasted_iota(jnp.int32, (tm, LANE), 1)
        for p in range(HEADS // 2):
            sl = slice(LANE * p, LANE * (p + 1))
            xs = (dattn[:, sl], pltpu.roll(dattn[:, sl], VDIM, 1))
            nds = (-pltpu.roll(delta[:, sl], VDIM, 1), -delta[:, sl])
            for a in range(2):
                hi, lo_part = _hi_lo(nds[a])
                blk = jnp.where(lane < VDIM, xs[a], jnp.where(lane == VDIM, hi, jnp.where(lane == VDIM + 1, lo_part, 0.0)))
                dop_ref[:, LANE * (2 * p + a):LANE * (2 * p + a + 1)] = blk.astype(BF16)
        dpj_ref[:, 2 * D_MODEL:2 * D_MODEL + HG_WIDTH] = (
            dya_in * attn_v * (sa * (1.0 + gate_a * (1.0 - sa)))).astype(BF16)
        don = dyb_in * silu_b
        dpj_ref[:, 2 * D_MODEL + HG_WIDTH:] = (dyb_in * on * (sb * (1.0 + gate_b * (1.0 - sb)))).astype(BF16)
        dgh_ref[...] += jnp.sum(don * ohat, axis=0, keepdims=True)
        dohat = don * ghv
        do_ref[...] = (ro * (dohat - ohat * (_sel_right(dohat * ohat, ones) * (1.0 / 64.0)))).astype(BF16)

    row = lambda w, j: pl.BlockSpec((tm, w), lambda i: (i, j))
    col = lambda w: pl.BlockSpec((w, tm), lambda i: (0, i))
    full = lambda a: pl.BlockSpec(a.shape, lambda i: (0, 0))
    acc = lambda w: pl.BlockSpec((1, w), lambda i: (0, 0))
    sds = lambda w, dt: jax.ShapeDtypeStruct((s, w), dt)
    sdt = lambda w: jax.ShapeDtypeStruct((w, s), BF16)
    return pl.pallas_call(
        body,
        grid=(s // tm,),
        in_specs=[row(1024, 0), row(1024, 0), row(2048, 0), row(512, 4), row(512, 5), row(512, 0), row(512, 0)]
        + [ANY] * 6 + [full(b_gate), full(g_post), full(gh), full(ones64)],
        out_specs=[row(1024, 0), row(3072, 0), row(1024, 0), row(512, 0),
                   col(1024), row(1024, 0), col(512), row(1024, 0), col(512), row(1024, 0),
                   acc(1024), acc(1024), acc(2048), acc(512)],
        out_shape=[sds(1024, F32), sds(D_IN_PAD, BF16), sds(1024, BF16), sds(512, BF16),
                   sdt(1024), sds(1024, BF16), sdt(512), sds(1024, BF16), sdt(512), sds(1024, BF16),
                   jax.ShapeDtypeStruct((1, 1024), F32), jax.ShapeDtypeStruct((1, 1024), F32),
                   jax.ShapeDtypeStruct((1, 2048), F32), jax.ShapeDtypeStruct((1, 512), F32)],
        scratch_shapes=[pltpu.VMEM(a.shape, BF16) for a in weights] + [pltpu.SemaphoreType.DMA((6,))],
        compiler_params=_params(("arbitrary",), 56),
        name="tail",
    )(x, tgt, proj, proj, proj, attn, o, *weights, b_gate, g_post, gh, ones64)


def _mla_bwd(proj, dqr, dkr, dv, g_q, g_kv, w_uq_pad, w_kv_pad, rc, rs1, rs2, cqt, ckvt, dproj):
    assert HEADS == N_DEV
    s = proj.shape[0]
    tm = 512
    scale = 1.0 / math.sqrt(QK)

    def body(cq_ref, ckv_ref, dqr_ref, dkr_ref, dv_ref, gq_ref, gkv_ref, wuqp_ref, wkvp_ref, c_ref, s1_ref, s2_ref,
             cqt_ref, ckvt_ref, dproj_in, dc_ref, dgq_ref, dgkv_ref, uq_slots, ukv_slots,
             dqf_ref, dkvf_ref, dwuq_ref, dwkv_ref):
        del dproj_in

        @pl.when(pl.program_id(0) == 0)
        def _():
            dgq_ref[...] = jnp.zeros_like(dgq_ref)
            dgkv_ref[...] = jnp.zeros_like(dgkv_ref)
            dwuq_ref[...] = jnp.zeros_like(dwuq_ref)
            dwkv_ref[...] = jnp.zeros_like(dwkv_ref)

        c, s1, s2 = c_ref[...], s1_ref[...], s2_ref[...]
        lane = lax.broadcasted_iota(jnp.int32, (tm, LANE), 1)
        ksum = jnp.zeros((tm, LANE), F32)
        for h in range(HEADS):
            sl = slice(LANE * h, LANE * (h + 1))
            dqf_ref[:, sl] = (_unrope(dqr_ref[:, sl], c, s1, s2) * scale).astype(BF16)
            dkh = dkr_ref[:, sl]
            ksum = ksum + dkh
            dkvf_ref[:, sl] = jnp.where(lane < NOPE, dkh, 0.0).astype(BF16)
            dkvf_ref[:, HEADS * LANE + LANE * h:HEADS * LANE + LANE * (h + 1)] = jnp.where(
                lane < VDIM, dv_ref[:, sl], 0.0).astype(BF16)
        dkpe = _unrope(ksum, c, s1, s2)
        dc_ref[:, Q_LORA + KV_LORA:] = jnp.where((lane >= NOPE) & (lane < QK), dkpe, 0.0).astype(BF16)
        dqf, dkvf = dqf_ref[...], dkvf_ref[...]
        dwuq_ref[...] += _dot(cqt_ref[...], dqf)
        dwkv_ref[...] += _dot(ckvt_ref[...], dkvf)
        dcqn = _dotg(dqf, wuqp_ref[...], NT)
        dckvn = _dotg(dkvf, wkvp_ref[...], NT)
        for x_ref, g_ref, dn, cols, dg_ref in ((cq_ref, gq_ref, dcqn, slice(0, Q_LORA), dgq_ref),
                                               (ckv_ref, gkv_ref, dckvn, slice(Q_LORA, Q_LORA + KV_LORA), dgkv_ref)):
            xv = x_ref[...]
            r = lax.rsqrt(jnp.mean(xv * xv, axis=-1, keepdims=True) + EPS)
            xh = xv * r
            dg_ref[...] += jnp.sum(dn * xh, axis=0, keepdims=True)
            dh = dn * g_ref[...]
            dc_ref[:, cols] = (r * (dh - xh * jnp.mean(dh * xh, axis=-1, keepdims=True))).astype(BF16)

        @pl.when(pl.program_id(0) == s // tm - 1)
        def _():
            ur = Q_LORA // N_DEV
            for p in range(N_DEV):
                uq_slots[p] = jnp.concatenate(
                    [dwuq_ref[ur * p:ur * (p + 1), LANE * h:LANE * h + QK] for h in range(HEADS)], axis=1).astype(BF16)
                ukv_slots[p] = jnp.concatenate(
                    [dwkv_ref[:, LANE * p:LANE * p + NOPE],
                     dwkv_ref[:, LANE * (HEADS + p):LANE * (HEADS + p) + VDIM]], axis=1).astype(BF16)

    row = lambda w, j: pl.BlockSpec((tm, w), lambda i: (i, j))
    full = lambda a: pl.BlockSpec(a.shape, lambda i: (0, 0))
    acc = lambda w: pl.BlockSpec((1, w), lambda i: (0, 0))
    col = lambda w: pl.BlockSpec((w, tm), lambda i: (0, i))
    whole = lambda shape: pl.BlockSpec(shape, lambda i: (0, 0, 0))
    uq_shape = (N_DEV, Q_LORA // N_DEV, HEADS * QK)
    ukv_shape = (N_DEV, KV_LORA, NOPE + VDIM)
    return pl.pallas_call(
        body,
        grid=(s // tm,),
        in_specs=[row(768, 6), row(256, 21), row(1024, 0), row(1024, 0), row(1024, 0), full(g_q), full(g_kv),
                  full(w_uq_pad), full(w_kv_pad), row(128, 0), row(128, 0), row(128, 0), col(Q_LORA), col(KV_LORA),
                  pl.BlockSpec(memory_space=pl.ANY)],
        out_specs=[row(1152, 4), acc(768), acc(256), whole(uq_shape), whole(ukv_shape)],
        out_shape=[jax.ShapeDtypeStruct(dproj.shape, BF16),
                   jax.ShapeDtypeStruct((1, 768), F32), jax.ShapeDtypeStruct((1, 256), F32),
                   jax.ShapeDtypeStruct(uq_shape, BF16), jax.ShapeDtypeStruct(ukv_shape, BF16)],
        input_output_aliases={14: 0},
        scratch_shapes=[pltpu.VMEM((tm, HEADS * LANE), BF16), pltpu.VMEM((tm, 2 * HEADS * LANE), BF16),
                        pltpu.VMEM((Q_LORA, HEADS * LANE), F32), pltpu.VMEM((KV_LORA, 2 * HEADS * LANE), F32)],
        compiler_params=_params(("arbitrary",)),
        name="mla_bwd",
    )(proj, proj, dqr, dkr, dv, g_q, g_kv, w_uq_pad, w_kv_pad, rc, rs1, rs2, cqt, ckvt, dproj)


def _dh_dx(dproj, w_in_pt, x, dout, g_pre, sends):
    s, k = dproj.shape
    tm = 256
    ns, ni = len(sends), s // tm

    def body(dp_ref, w_ref, x_ref, dout_ref, g_ref, *rest):
        send_refs, (dx_ref, dg_ref) = rest[:ns], rest[ns:ns + 2]
        recv_refs, sems = rest[ns + 2:2 * ns + 2], rest[2 * ns + 2:]

        @pl.when(pl.program_id(0) == 0)
        def _():
            _start_all(*_to_chips_copies(send_refs, recv_refs, sems))
            dg_ref[...] = jnp.zeros_like(dg_ref)

        dh = _dot(dp_ref[...], w_ref[...])
        xv = x_ref[...]
        r = lax.rsqrt(jnp.mean(xv * xv, axis=-1, keepdims=True) + EPS)
        xh = xv * r
        dg_ref[...] += jnp.sum(dh * xh, axis=0, keepdims=True)
        dxh = dh * g_ref[...]
        dx_ref[...] = dout_ref[...] + r * (dxh - xh * jnp.mean(dxh * xh, axis=-1, keepdims=True))

        @pl.when(pl.program_id(0) == ni - 1)
        def _():
            _wait_all(*_to_chips_copies(send_refs, recv_refs, sems))

    row = lambda w: pl.BlockSpec((tm, w), lambda i: (i, 0))
    return pl.pallas_call(
        body,
        grid=(ni,),
        in_specs=[row(k), pl.BlockSpec((k, D_MODEL), lambda i: (0, 0)), row(D_MODEL), row(D_MODEL),
                  pl.BlockSpec((1, D_MODEL), lambda i: (0, 0))] + [ANY] * ns,
        out_specs=[row(D_MODEL), pl.BlockSpec((1, D_MODEL), lambda i: (0, 0))] + [ANY] * ns,
        out_shape=[jax.ShapeDtypeStruct((s, D_MODEL), F32), jax.ShapeDtypeStruct((1, D_MODEL), F32)]
        + [jax.ShapeDtypeStruct(a.shape, a.dtype) for a in sends],
        scratch_shapes=_copy_sems(ns, 3),
        compiler_params=_params(("arbitrary",)),
        name="dh_dx",
    )(dproj, w_in_pt, x, dout, g_pre, *sends)


def _pair_reduce(slots):
    n = len(slots)
    half = [(N_DEV // 2,) + a.shape[1:] for a in slots]

    def body(*refs):
        s_refs, o_refs = refs[:n], refs[n:2 * n]
        mine, got = refs[2 * n:3 * n], refs[3 * n:4 * n]
        send_sems, recv_sems, local_sems, out_sems = refs[4 * n:]
        x, y, c = _my_place()
        copies, loads, stores = [], [], []
        for q in range(N_DEV // 2):
            for a in range(n):
                copies.append(pltpu.make_async_remote_copy(
                    src_ref=s_refs[a].at[2 * q + 1 - c], dst_ref=got[a].at[q],
                    send_sem=send_sems.at[4 * a + q], recv_sem=recv_sems.at[4 * a + q],
                    device_id=(x, y, 1 - c), device_id_type=MESH_ID))
                loads.append(pltpu.make_async_copy(s_refs[a].at[2 * q + c], mine[a].at[q], local_sems.at[4 * a + q]))
                stores.append(pltpu.make_async_copy(mine[a].at[q], o_refs[a].at[q], out_sems.at[4 * a + q]))
        _start_all(loads, copies)
        k = 0
        for q in range(N_DEV // 2):
            for a in range(n):
                loads[k].wait()
                copies[k].wait_recv()
                mine[a][q] = (mine[a][q].astype(F32) + got[a][q].astype(F32)).astype(mine[a].dtype)
                stores[k].start()
                k += 1
        for cp in copies:
            cp.wait_send()
        for cp in stores:
            cp.wait()

    vm = lambda: [pltpu.VMEM(h, a.dtype) for h, a in zip(half, slots)]
    return pl.pallas_call(
        body,
        in_specs=[ANY] * n,
        out_specs=[ANY] * n,
        out_shape=[jax.ShapeDtypeStruct(h, a.dtype) for h, a in zip(half, slots)],
        scratch_shapes=vm() + vm() + [pltpu.SemaphoreType.DMA((4 * n,)), pltpu.SemaphoreType.DMA((4 * n,)),
                                      pltpu.SemaphoreType.DMA((4 * n,)), pltpu.SemaphoreType.DMA((4 * n,))],
        compiler_params=pltpu.CompilerParams(vmem_limit_bytes=48 * 2**20),
        name="pair_reduce",
    )(*slots)


def _rope_tables(s):
    inv = (np.float32(ROPE_THETA) ** (-np.arange(0, ROPE, 2, dtype=np.float32) / np.float32(ROPE))).astype(np.float32)
    ang = (np.arange(s, dtype=np.float32)[:, None] * inv[None, :]).astype(np.float32)
    cos, sin = jnp.asarray(np.cos(ang.astype(np.float64)), F32), jnp.asarray(np.sin(ang.astype(np.float64)), F32)
    z = lambda w: jnp.zeros((s, w), F32)
    rc = jnp.concatenate([jnp.ones((s, NOPE), F32), cos, cos, z(32)], axis=1)
    rs1 = jnp.concatenate([z(NOPE), -sin, z(16), z(32)], axis=1)
    rs2 = jnp.concatenate([z(NOPE), z(16), sin, z(32)], axis=1)
    return rc, rs1, rs2


def _step(x, tgt, w_blk, shards, g_pre, b_gate, g_q, g_kv, lbl, g_hgrn, g_post):
    s = x.shape[0]
    rc, rs1, rs2 = _rope_tables(s)
    gh = jnp.tile(g_hgrn, (1, HEADS))

    proj, w_in_pt, ht, *got = _gather_proj(x, g_pre, w_blk, shards[:2])
    w_uq, w_ukv = (_from_slots(n, g) for n, g in zip(MATS[:2], got))
    w_uq_p = jnp.pad(w_uq.reshape(Q_LORA, HEADS, QK), ((0, 0), (0, 0), (0, LANE - QK))).reshape(Q_LORA, HEADS * LANE)
    kv3 = w_ukv.reshape(KV_LORA, HEADS, NOPE + VDIM)
    pad64 = lambda t: jnp.pad(t, ((0, 0), (0, 0), (0, LANE - 64))).reshape(KV_LORA, HEADS * LANE)
    w_kv_p = jnp.concatenate([pad64(kv3[:, :, :NOPE]), pad64(kv3[:, :, NOPE:])], axis=1)

    qr, kr, v, cqt, ckvt, *later_shards = _mla_prep(proj, g_q, g_kv, w_uq_p, w_kv_p, rc, rs1, rs2, shards[2:])
    attn, qa, *got = _attn_fwd(qr, kr, v, later_shards)
    w_a, w_b, w_out = (_from_slots(n, g) for n, g in zip(MATS[2:], got))
    o, sprev = _hgrn_fwd(proj, lbl)
    (dout, dproj, dop, do, mt, dy_bf, yat, dya_bf, ybt, dyb_bf,
     loss_vec, dg_post, db_gate, dgh) = _tail(x, tgt, proj, attn, o, w_a, w_b, w_out, w_a.T, w_b.T, w_out.T,
                                               b_gate, g_post, gh)
    dproj, dlbl, *early = _hgrn_bwd(proj, lbl, do, sprev, dproj,
                                    [("w_branch_a", yat, dya_bf), ("w_branch_b", ybt, dyb_bf), ("w_out", mt, dy_bf)])
    dqr, dkr, dv, *early_recv = _attn_bwd(qa, kr, v, dop, early)
    dproj, dg_q, dg_kv, dw_uq_slots, dw_ukv_slots = _mla_bwd(proj, dqr, dkr, dv, g_q, g_kv, w_uq_p, w_kv_p,
                                                             rc, rs1, rs2, cqt, ckvt, dproj)

    dw_in_slots = _dw_in_slots(ht, dproj)
    late = _pair_reduce([dw_in_slots, dw_uq_slots, dw_ukv_slots])
    dx, dg_pre, *late_recv = _dh_dx(dproj, w_in_pt, x, dout, g_pre, late)

    g_sum = _vectors_sum(dg_pre, db_gate, dg_q, dg_kv, dlbl, dgh, dg_post, loss_vec)
    return dx, late_recv[0], dict(zip(MATS, late_recv[1:] + early_recv)), g_sum


def _adamw(g, w, m, v):
    c1 = 1.0 / (1.0 - ADAM_B1 ** ADAM_STEP)
    c2 = 1.0 / (1.0 - ADAM_B2 ** ADAM_STEP)
    nm = ADAM_B1 * m + (1.0 - ADAM_B1) * g
    nv = ADAM_B2 * v + (1.0 - ADAM_B2) * (g * g)
    d = -ADAM_LR * ((nm * c1) / (jnp.sqrt(nv * c2) + ADAM_EPS) + ADAM_WD * w)
    return d, nm, nv


def _sum8(r_ref):
    g = r_ref[0].astype(F32)
    for k in range(1, r_ref.shape[0]):
        g = g + r_ref[k].astype(F32)
    return g


def _sum_adamw_w_in(recv, w, m, v):
    rows, _, cols = w.shape
    tc = 256
    nc = cols // tc

    def body(r_ref, w_hbm, m_hbm, v_hbm, g_hbm, d_hbm, nm_hbm, nv_hbm, ins, outs, in_sems, out_sems):
        i = pl.program_id(0)
        slot = i & 1
        cols_of = lambda step: pl.ds(pl.multiple_of(step * tc, tc), tc)

        def load(k, step, sl):
            return pltpu.make_async_copy((w_hbm, m_hbm, v_hbm)[k].at[:, 0, cols_of(step)], ins.at[sl, k],
                                         in_sems.at[sl, k])

        def store(k, step, sl):
            return pltpu.make_async_copy(outs.at[sl, k], (g_hbm, d_hbm, nm_hbm, nv_hbm)[k].at[:, 0, cols_of(step)],
                                         out_sems.at[sl, k])

        @pl.when(i == 0)
        def _():
            for k in range(3):
                load(k, 0, 0).start()

        @pl.when(i + 1 < nc)
        def _():
            for k in range(3):
                load(k, i + 1, 1 - slot).start()

        @pl.when(i >= 2)
        def _():
            for k in range(4):
                store(k, i - 2, slot).wait()

        for k in range(3):
            load(k, i, slot).wait()
        g = _sum8(r_ref)
        d, nm, nv = _adamw(g, ins[slot, 0], ins[slot, 1], ins[slot, 2])
        for k, val in enumerate((g, d, nm, nv)):
            outs[slot, k] = val
        for k in range(4):
            store(k, i, slot).start()

        @pl.when(i == nc - 1)
        def _():
            for k in range(4):
                store(k, i, slot).wait()
            if nc >= 2:
                for k in range(4):
                    store(k, i - 1, 1 - slot).wait()

    out = jax.ShapeDtypeStruct((rows, 1, cols), F32)
    return pl.pallas_call(
        body,
        grid=(nc,),
        in_specs=[pl.BlockSpec((recv.shape[0], rows, tc), lambda i: (0, 0, i)), ANY, ANY, ANY],
        out_specs=[ANY, ANY, ANY, ANY],
        out_shape=[out, out, out, out],
        scratch_shapes=[pltpu.VMEM((2, 3, rows, tc), F32), pltpu.VMEM((2, 4, rows, tc), F32),
                        pltpu.SemaphoreType.DMA((2, 3)), pltpu.SemaphoreType.DMA((2, 4))],
        compiler_params=_params(("arbitrary",)),
        name="sum_adamw_w_in",
    )(recv, w, m, v)


def _sum_adamw_whole(recvs, ws, ms, vs):
    n = len(ws)

    def body(*refs):
        r_refs, w_refs, m_refs, v_refs = refs[:n], refs[n:2 * n], refs[2 * n:3 * n], refs[3 * n:4 * n]
        outs = refs[4 * n:]
        for a in range(n):
            g = _sum8(r_refs[a])
            d, nm, nv = _adamw(g, w_refs[a][...], m_refs[a][...], v_refs[a][...])
            outs[a][...] = g
            outs[n + a][...] = d
            outs[2 * n + a][...] = nm
            outs[3 * n + a][...] = nv

    shapes = [jax.ShapeDtypeStruct(w.shape, F32) for w in ws]
    res = pl.pallas_call(
        body,
        out_shape=shapes * 4,
        compiler_params=pltpu.CompilerParams(vmem_limit_bytes=48 * 2**20),
        name="sum_adamw_mats",
    )(*recvs, *ws, *ms, *vs)
    return res[:n], res[n:2 * n], res[2 * n:3 * n], res[3 * n:]


SMALL = ("g_pre", "b_gate", "g_q", "g_kv", "lb_logits", "g_hgrn", "g_post")
SMALL_SHAPE = dict(g_pre=(1, 1024), b_gate=(1, 2048), g_q=(1, 768), g_kv=(1, 256), lb_logits=(2, 512),
                   g_hgrn=(1, 64), g_post=(1, 1024))


def _vectors_sum(dg_pre, db_gate, dg_q, dg_kv, dlbl, dgh, dg_post, loss_vec):
    def body(gpre_ref, bg_ref, gq_ref, gkv_ref, lbl_ref, gh_ref, gpost_ref, loss_ref, out_ref, mine, got,
             send_sems, recv_sems):
        mine[...] = jnp.zeros_like(mine)
        mine[0:1, :] = gpre_ref[...]
        mine[1:2, :] = bg_ref[:, :1024]
        mine[2:3, :] = bg_ref[:, 1024:]
        mine[3:4, :Q_LORA] = gq_ref[...]
        mine[4:5, :KV_LORA] = gkv_ref[...]
        loss = (0.5 / D_MODEL) * jnp.sum(loss_ref[...], axis=-1, keepdims=True)
        mine[4:5, KV_LORA:] = jnp.broadcast_to(loss, (1, 1024 - KV_LORA))
        mine[5:6, :HG_WIDTH] = lbl_ref[0:1, :]
        mine[5:6, HG_WIDTH:] = lbl_ref[1:2, :]
        gh = gh_ref[...]
        fold = gh[:, :VDIM]
        for h in range(1, HEADS):
            fold = fold + gh[:, VDIM * h:VDIM * (h + 1)]
        mine[6:7, :VDIM] = fold
        mine[7:8, :] = gpost_ref[...]
        x, y, c = _my_place()
        me = 4 * x + 2 * y + c
        got[me] = mine[...]
        copies = [pltpu.make_async_remote_copy(
            src_ref=mine, dst_ref=got.at[me], send_sem=send_sems.at[k], recv_sem=recv_sems.at[k],
            device_id=_flip(k, x, y, c), device_id_type=MESH_ID) for k in range(N_DEV - 1)]
        _start_all([], copies)
        _wait_all([], copies)
        out_ref[...] = _sum8(got)

    return pl.pallas_call(
        body,
        out_shape=jax.ShapeDtypeStruct((8, 1024), F32),
        scratch_shapes=[pltpu.VMEM((8, 1024), F32), pltpu.VMEM((N_DEV, 8, 1024), F32),
                        pltpu.SemaphoreType.DMA((7,)), pltpu.SemaphoreType.DMA((7,))],
        name="vectors_sum",
    )(dg_pre, db_gate, dg_q, dg_kv, dlbl, dgh, dg_post, loss_vec)


def _vectors_adamw(g_sum, ws, ms, vs):
    n = len(SMALL)

    def body(g_ref, *refs):
        w_refs, m_refs, v_refs = refs[:n], refs[n:2 * n], refs[2 * n:3 * n]
        loss_ref, outs = refs[3 * n], refs[3 * n + 1:]
        g = g_ref[...]
        loss_ref[...] = g[4:5, KV_LORA:KV_LORA + 1]
        grads = (g[0:1, :], jnp.concatenate([g[1:2, :], g[2:3, :]], axis=1), g[3:4, :Q_LORA], g[4:5, :KV_LORA],
                 jnp.concatenate([g[5:6, :HG_WIDTH], g[5:6, HG_WIDTH:]], axis=0), g[6:7, :VDIM], g[7:8, :])
        for a in range(n):
            d, nm, nv = _adamw(grads[a], w_refs[a][...], m_refs[a][...], v_refs[a][...])
            outs[a][...] = grads[a]
            outs[n + a][...] = d
            outs[2 * n + a][...] = nm
            outs[3 * n + a][...] = nv

    shapes = [jax.ShapeDtypeStruct(SMALL_SHAPE[k], F32) for k in SMALL]
    res = pl.pallas_call(
        body,
        out_shape=[jax.ShapeDtypeStruct((1, 1), F32)] + shapes * 4,
        name="vectors_adamw",
    )(g_sum, *ws, *ms, *vs)
    return res[0], res[1:n + 1], res[n + 1:2 * n + 1], res[2 * n + 1:3 * n + 1], res[3 * n + 1:]


MATS = ("w_uq", "w_ukv", "w_branch_a", "w_branch_b", "w_out")
COL_SHARDED = dict(w_uq=False, w_ukv=True, w_branch_a=True, w_branch_b=True, w_out=False)
ORDER = ("g_pre", "w_in", "b_gate", "g_q", "w_uq", "g_kv", "w_ukv", "lb_logits", "g_hgrn",
         "w_branch_a", "w_branch_b", "w_out", "g_post")


def _from_slots(name, slots):
    _, r, c = slots.shape
    if COL_SHARDED[name]:
        return slots.transpose(1, 0, 2).reshape(r, N_DEV * c)
    return slots.reshape(N_DEV * r, c)


def kernel(x, g_pre, w_in, b_gate, g_q, w_uq, g_kv, w_ukv, lb_logits, g_hgrn, w_branch_a, w_branch_b, w_out, g_post, loss_target, m_g_pre, m_w_in, m_b_gate, m_g_q, m_w_uq, m_g_kv, m_w_ukv, m_lb_logits, m_g_hgrn, m_w_branch_a, m_w_branch_b, m_w_out, m_g_post, v_g_pre, v_w_in, v_b_gate, v_g_q, v_w_uq, v_g_kv, v_w_ukv, v_lb_logits, v_g_hgrn, v_w_branch_a, v_w_branch_b, v_w_out, v_g_post):
    rows3 = lambda a: jnp.transpose(a, (2, 0, 1))
    w = dict(w_in=rows3(w_in), w_uq=w_uq[0], w_ukv=w_ukv[0], w_branch_a=w_branch_a[0], w_branch_b=w_branch_b[0],
             w_out=w_out[0], g_pre=g_pre, b_gate=b_gate, g_q=g_q, g_kv=g_kv, lb_logits=lb_logits, g_hgrn=g_hgrn,
             g_post=g_post)
    mom = dict(w_in=rows3(m_w_in), w_uq=m_w_uq[0], w_ukv=m_w_ukv[0], w_branch_a=m_w_branch_a[0],
               w_branch_b=m_w_branch_b[0], w_out=m_w_out[0], g_pre=m_g_pre, b_gate=m_b_gate, g_q=m_g_q, g_kv=m_g_kv,
               lb_logits=m_lb_logits, g_hgrn=m_g_hgrn, g_post=m_g_post)
    var = dict(w_in=rows3(v_w_in), w_uq=v_w_uq[0], w_ukv=v_w_ukv[0], w_branch_a=v_w_branch_a[0],
               w_branch_b=v_w_branch_b[0], w_out=v_w_out[0], g_pre=v_g_pre, b_gate=v_b_gate, g_q=v_g_q, g_kv=v_g_kv,
               lb_logits=v_lb_logits, g_hgrn=v_g_hgrn, g_post=v_g_post)

    w_blk = w["w_in"].reshape(W_IN_SHARD, D_MODEL).astype(BF16)
    shards = [w[n].astype(BF16) for n in MATS[:2]] + [w[n] for n in MATS[2:]]
    dx, recv_in, recv, g_sum = _step(x[0], loss_target[0], w_blk, shards,
                                     g_pre, b_gate, g_q, g_kv, lb_logits, g_hgrn, g_post)

    g_in, d_in, m_in, v_in = _sum_adamw_w_in(recv_in, w["w_in"], mom["w_in"], var["w_in"])
    res = _sum_adamw_whole([recv[n] for n in MATS], *([t[n] for n in MATS] for t in (w, mom, var)))
    total, *vec = _vectors_adamw(g_sum, *([t[n] for n in SMALL] for t in (w, mom, var)))

    outs = []
    for mats, vecs, big in zip(res, vec, (g_in, d_in, m_in, v_in)):
        t = {**{n: a[None] for n, a in zip(MATS, mats)}, **dict(zip(SMALL, vecs)),
             "w_in": jnp.transpose(big, (1, 2, 0))}
        outs += [t[n] for n in ORDER]
    return (total.reshape(()), dx[None], *outs)
```

```python
import math

import jax
import jax.numpy as jnp
import numpy as np
from jax import lax
from jax.experimental import pallas as pl
from jax.experimental.pallas import tpu as pltpu

F32, BF16 = jnp.float32, jnp.bfloat16

D_MODEL = 1024
EPS = 1e-6
HEADS = 8
NOPE, ROPE, VDIM = 64, 32, 64
QK = NOPE + ROPE
Q_LORA, KV_LORA = 768, 256
ROPE_THETA = 10000.0
ATT_CHUNK_SHIFT = 6
HG_BLOCK = 32
HG_WIDTH = 512
D_IN = 5664
D_IN_PAD = 5760
W_IN_SHARD = D_IN // 8
N_DEV = 8
LANE = 128

ADAM_LR, ADAM_B1, ADAM_B2, ADAM_EPS, ADAM_WD, ADAM_STEP = 0.001, 0.9, 0.999, 1e-08, 0.01, 10

W_IN_SEGMENTS = ((3616, 5664, 0), (1056, 1568, 2048), (3104, 3616, 2560), (1568, 3104, 3072),
                 (0, 1024, 4608), (1024, 1056, 5696))

NT = (((1,), (1,)), ((), ()))
TN = (((0,), (0,)), ((), ()))
MESH_ID = pl.DeviceIdType.MESH


def _w_in_pieces():
    out = []
    for lo, hi, dst in W_IN_SEGMENTS:
        c = lo
        while c < hi:
            p = c // W_IN_SHARD
            e = min(hi, (p + 1) * W_IN_SHARD)
            out.append((p, c - p * W_IN_SHARD, e - p * W_IN_SHARD, dst + c - lo))
            c = e
    return out


def _params(sem, vmem_mb=48):
    return pltpu.CompilerParams(dimension_semantics=sem, vmem_limit_bytes=vmem_mb * 2**20)


def _dot(a, b):
    return jnp.dot(a, b, preferred_element_type=F32)


def _dotg(a, b, dims):
    return lax.dot_general(a, b, dims, preferred_element_type=F32)


def _split2(x):
    hi = x.astype(BF16)
    return hi, (x - hi.astype(F32)).astype(BF16)


def _sel_left(m01, x):
    hi, lo = _split2(x)
    return _dot(m01, hi) + _dot(m01, lo)


def _sel_right(x, m01):
    hi, lo = _split2(x)
    return _dot(hi, m01) + _dot(lo, m01)


def _hi_lo(x):
    hi = x.astype(BF16).astype(F32)
    return hi, x - hi


def _sigmoid(x):
    return 0.5 * jnp.tanh(0.5 * x) + 0.5


def _rope(x, c, s1, s2):
    return x * c + pltpu.roll(x, 112, 1) * s1 + pltpu.roll(x, 16, 1) * s2


def _unrope(d, c, s1, s2):
    return d * c + pltpu.roll(d * s1, 16, 1) + pltpu.roll(d * s2, 112, 1)


def _my_place():
    return lax.axis_index("x"), lax.axis_index("y"), lax.axis_index("c")


def _flip(k, x, y, c):
    fx, fy, fc = (k + 1) >> 2 & 1, (k + 1) >> 1 & 1, (k + 1) & 1
    return (1 - x if fx else x), (1 - y if fy else y), (1 - c if fc else c)


def _to_all_copies(s_refs, r_refs, sems, spread):
    send_sems, recv_sems, local_sems = sems
    x, y, c = _my_place()
    me = 4 * x + 2 * y + c
    src = (lambda a, p: s_refs[a]) if spread else (lambda a, p: s_refs[a].at[p])
    local = [pltpu.make_async_copy(src(a, me), r_refs[a].at[me], local_sems.at[a]) for a in range(len(s_refs))]
    remote = []
    for k in range(N_DEV - 1):
        px, py, pc = _flip(k, x, y, c)
        for a in range(len(s_refs)):
            remote.append(pltpu.make_async_remote_copy(
                src_ref=src(a, 4 * px + 2 * py + pc), dst_ref=r_refs[a].at[me],
                send_sem=send_sems.at[7 * a + k], recv_sem=recv_sems.at[7 * a + k],
                device_id=(px, py, pc), device_id_type=MESH_ID))
    return local, remote


def _to_chips_copies(s_refs, r_refs, sems):
    send_sems, recv_sems, local_sems = sems
    x, y, c = _my_place()
    me = 2 * x + y
    local = [pltpu.make_async_copy(s_refs[a].at[me], r_refs[a].at[me], local_sems.at[a]) for a in range(len(s_refs))]
    remote = []
    for k in range(3):
        px = 1 - x if (k + 1) >> 1 & 1 else x
        py = 1 - y if (k + 1) & 1 else y
        for a in range(len(s_refs)):
            remote.append(pltpu.make_async_remote_copy(
                src_ref=s_refs[a].at[2 * px + py], dst_ref=r_refs[a].at[me],
                send_sem=send_sems.at[3 * a + k], recv_sem=recv_sems.at[3 * a + k],
                device_id=(px, py, c), device_id_type=MESH_ID))
    return local, remote


def _start_all(local, remote):
    for cp in local + remote:
        cp.start()


def _wait_all(local, remote):
    for cp in remote:
        cp.wait_recv()
    for cp in remote:
        cp.wait_send()
    for cp in local:
        cp.wait()


def _copy_sems(n, peers):
    return [pltpu.SemaphoreType.DMA((peers * n,)), pltpu.SemaphoreType.DMA((peers * n,)),
            pltpu.SemaphoreType.DMA((n,))]


ANY = pl.BlockSpec(memory_space=pl.ANY)


def _dw_in_slots(ht, dproj):
    m, k = ht.shape
    n = dproj.shape[1]
    tn, tk = 1920, 2048
    nj, nk = n // tn, k // tk
    by_tile = [[] for _ in range(nj)]
    for p, lo, hi, dst in _w_in_pieces():
        while lo < hi:
            j = dst // tn
            cnt = min(hi - lo, (j + 1) * tn - dst)
            by_tile[j].append((p, lo, lo + cnt, dst - j * tn))
            lo, dst = lo + cnt, dst + cnt

    def body(a_ref, b_ref, s_ref, acc_ref):
        j, l = pl.program_id(0), pl.program_id(1)

        @pl.when(l == 0)
        def _():
            acc_ref[...] = jnp.zeros_like(acc_ref)

        acc_ref[...] += _dot(a_ref[...], b_ref[...])

        @pl.when(l == nk - 1)
        def _():
            at = acc_ref[...].T
            for jj in range(nj):
                @pl.when(j == jj)
                def _(jj=jj):
                    for p, lo, hi, d in by_tile[jj]:
                        s_ref[p, lo:hi, :] = at[d:d + hi - lo, :].astype(BF16)

    return pl.pallas_call(
        body,
        grid=(nj, nk),
        in_specs=[pl.BlockSpec((m, tk), lambda j, l: (0, l)), pl.BlockSpec((tk, tn), lambda j, l: (l, j))],
        out_specs=pl.BlockSpec((N_DEV, W_IN_SHARD, m), lambda j, l: (0, 0, 0), pipeline_mode=pl.Buffered(1)),
        out_shape=jax.ShapeDtypeStruct((N_DEV, W_IN_SHARD, m), BF16),
        scratch_shapes=[pltpu.VMEM((m, tn), F32)],
        compiler_params=_params(("arbitrary", "arbitrary"), 56),
        name="dw_in",
    )(ht, dproj)


GP_TN = 256
GP_COLS = 5888
GP_NT = GP_COLS // GP_TN


def _gp_tile_pieces():
    tiles = [[] for _ in range(GP_NT)]
    for p, lo, hi, dst in _w_in_pieces():
        while lo < hi:
            t = dst // GP_TN
            n = min(hi - lo, (t + 1) * GP_TN - dst)
            tiles[t].append((p, lo, lo + n, dst - t * GP_TN))
            lo, dst = lo + n, dst + n
    return tiles


def _gp_tables():
    pieces = _gp_tile_pieces()
    rank_of = {None: 0, 0: 1, 1: 2, 2: 2, 4: 3, 5: 3, 3: 4, 6: 5}
    order = np.zeros((N_DEV, GP_NT), np.int32)
    waits = np.zeros((N_DEV, GP_NT), np.int32)
    for me in range(N_DEV):
        x, y, c = me >> 2 & 1, me >> 1 & 1, me & 1
        chips = [(1 - x, y), (x, 1 - y), (1 - x, 1 - y)]

        def sem_of(p):
            px, py, pc = p >> 2 & 1, p >> 1 & 1, p & 1
            if (px, py) == (x, y):
                return None if pc == c else 0
            j = chips.index((px, py))
            return 1 + j if pc == c else 4 + j

        needs = [sorted({sem_of(p) for p, _, _, _ in tile} - {None}) for tile in pieces]
        ranks = [max([rank_of[k] for k in ks], default=0) for ks in needs]
        seq = sorted(range(GP_NT), key=lambda t: (ranks[t], t))
        seen = set()
        for step, t in enumerate(seq):
            order[me, step] = t
            new = [k for k in needs[t] if k not in seen]
            for k in new:
                waits[me, step] |= 1 << k
            seen.update(new)
        assert seen == set(range(7)), (me, seen)
    return order, waits


def _gather_proj(x, g_pre, w_blk, shards):
    s = x.shape[0]
    tx = 512
    ns = len(shards)
    tile_pieces = _gp_tile_pieces()
    order_np, waits_np = _gp_tables()
    xq, yq, cq = _my_place()
    me_out = 4 * xq + 2 * yq + cq
    order = lax.dynamic_index_in_dim(jnp.asarray(order_np), me_out, 0, keepdims=False)
    waits = lax.dynamic_index_in_dim(jnp.asarray(waits_np), me_out, 0, keepdims=False)

    def body(order_ref, waits_ref, x_hbm, g_ref, wblk_hbm, *rest):
        shard_refs, (proj_ref, wt_ref, ht_hbm), got_refs = rest[:ns], rest[ns:ns + 3], rest[ns + 3:2 * ns + 3]
        recv, h_ref, wtile, xbuf, htbuf = rest[2 * ns + 3:2 * ns + 8]
        send_sems, recv_sems, misc_sems = rest[2 * ns + 8:2 * ns + 11]
        sems = rest[2 * ns + 11:]
        t = pl.program_id(0)
        x_, y_, c = _my_place()
        sibling = (x_, y_, 1 - c)
        chips = [(1 - x_, y_), (x_, 1 - y_), (1 - x_, 1 - y_)]
        idx = lambda px, py, pc: 4 * px + 2 * py + pc
        me = idx(x_, y_, c)

        def copy(k, slot, to, src=None):
            return pltpu.make_async_remote_copy(
                src_ref=recv.at[slot] if src is None else src, dst_ref=recv.at[slot],
                send_sem=send_sems.at[k], recv_sem=recv_sems.at[k], device_id=to, device_id_type=MESH_ID)

        mine = pltpu.make_async_copy(wblk_hbm, recv.at[me], misc_sems.at[0])
        first = [copy(0, me, sibling, src=wblk_hbm)] + [copy(1 + j, me, (*chips[j], c), src=wblk_hbm) for j in range(2)]
        passed = [copy(4 + j, idx(*ch, c), sibling) for j, ch in enumerate(chips)]
        onward = [copy(3, idx(*chips[0], c), (*chips[1], c)), copy(3, idx(*chips[1], c), (*chips[0], c))]
        arrivals = ([copy(0, idx(x_, y_, 1 - c), sibling)] + [copy(1 + j, idx(*ch, c), sibling) for j, ch in enumerate(chips)]
                    + [copy(4 + j, idx(*ch, 1 - c), sibling) for j, ch in enumerate(chips)])

        @pl.when(t == 0)
        def _():
            mine.start()
            for cp in first:
                cp.start()
            _start_all(*_to_all_copies(shard_refs, got_refs, sems, True))

            def load(i):
                return pltpu.make_async_copy(x_hbm.at[pl.ds(i * tx, tx), :], xbuf.at[i & 1], misc_sems.at[1 + (i & 1)])

            def store(i):
                return pltpu.make_async_copy(htbuf.at[i & 1], ht_hbm.at[:, pl.ds(i * tx, tx)], misc_sems.at[3 + (i & 1)])

            load(0).start()
            for i in range(s // tx):
                if i + 1 < s // tx:
                    load(i + 1).start()
                load(i).wait()
                xv = xbuf[i & 1]
                r = lax.rsqrt(jnp.mean(xv * xv, axis=-1, keepdims=True) + EPS)
                h = (xv * r * g_ref[...]).astype(BF16)
                h_ref[i * tx:(i + 1) * tx, :] = h
                if i >= 2:
                    store(i - 2).wait()
                htbuf[i & 1] = h.T
                store(i).start()
            for i in range(max(s // tx - 2, 0), s // tx):
                store(i).wait()
            mine.wait()

        w = waits_ref[t]
        for k in range(7):
            @pl.when((w >> k) & 1 == 1)
            def _(k=k):
                arrivals[k].wait_recv()
                if 1 <= k <= 3:
                    passed[k - 1].start()
                if 1 <= k <= 2:
                    @pl.when(c == k - 1)
                    def _():
                        onward[k - 1].start()

        tile = order_ref[t]
        for tt in range(GP_NT):
            @pl.when(tile == tt)
            def _(tt=tt):
                covered = sorted((d, d + hi - lo) for _, lo, hi, d in tile_pieces[tt])
                at = 0
                for lo_z, hi_z in covered + [(GP_TN, GP_TN)]:
                    if lo_z > at:
                        wtile[at:lo_z, :] = jnp.zeros((lo_z - at, D_MODEL), BF16)
                    at = max(at, hi_z)
                for p, lo, hi, d in tile_pieces[tt]:
                    wtile[d:d + hi - lo, :] = recv[p, lo:hi, :]

        wt = wtile[...]
        wt_ref[...] = wt
        proj_ref[...] = _dotg(h_ref[...], wt, NT)

        @pl.when(t == GP_NT - 1)
        def _():
            for cp in first + passed + onward[:1]:
                cp.wait_send()
            _wait_all(*_to_all_copies(shard_refs, got_refs, sems, True))

    grid_spec = pltpu.PrefetchScalarGridSpec(
        num_scalar_prefetch=2,
        grid=(GP_NT,),
        in_specs=[ANY, pl.BlockSpec((1, D_MODEL), lambda t, o, w: (0, 0)), ANY] + [ANY] * ns,
        out_specs=[pl.BlockSpec((s, GP_TN), lambda t, o, w: (0, o[t])),
                   pl.BlockSpec((GP_TN, D_MODEL), lambda t, o, w: (o[t], 0)), ANY] + [ANY] * ns,
        scratch_shapes=[pltpu.VMEM((N_DEV, W_IN_SHARD, D_MODEL), BF16), pltpu.VMEM((s, D_MODEL), BF16),
                        pltpu.VMEM((GP_TN, D_MODEL), BF16), pltpu.VMEM((2, tx, D_MODEL), F32),
                        pltpu.VMEM((2, D_MODEL, tx), BF16),
                        pltpu.SemaphoreType.DMA((7,)), pltpu.SemaphoreType.DMA((7,)), pltpu.SemaphoreType.DMA((5,))]
        + _copy_sems(ns, 7),
    )
    return pl.pallas_call(
        body,
        grid_spec=grid_spec,
        out_shape=[jax.ShapeDtypeStruct((s, GP_COLS), F32),jax.ShapeDtypeStruct((GP_COLS, D_MODEL), BF16),
                   jax.ShapeDtypeStruct((D_MODEL, s), BF16)]
        + [jax.ShapeDtypeStruct((N_DEV,) + b.shape, b.dtype) for b in shards],
        compiler_params=_params(("arbitrary",), 56),
        name="gather_proj",
    )(order, waits, x, g_pre, w_blk, *shards)


def _mla_prep(proj, g_q, g_kv, w_uq_p, w_kv_p, rc, rs1, rs2, casts):
    s = proj.shape[0]
    tm = 512
    nc = len(casts)
    scale = 1.0 / math.sqrt(QK)

    def body(cq_ref, ckv_ref, kpe_ref, gq_ref, gkv_ref, wuq_ref, wkv_ref, c_ref, s1_ref, s2_ref, *rest):
        cast_in, (qr_ref, kr_ref, v_ref, cqt_ref, ckvt_ref), cast_out = rest[:nc], rest[nc:nc + 5], rest[nc + 5:]

        @pl.when(pl.program_id(0) == 0)
        def _():
            for src, dst in zip(cast_in, cast_out):
                dst[...] = src[...].astype(BF16)

        cq = cq_ref[...]
        r = lax.rsqrt(jnp.mean(cq * cq, axis=-1, keepdims=True) + EPS)
        cqn = (cq * r * gq_ref[...]).astype(BF16)
        cqt_ref[...] = cqn.T
        q = _dot(cqn, wuq_ref[...])
        ckv = ckv_ref[...]
        r = lax.rsqrt(jnp.mean(ckv * ckv, axis=-1, keepdims=True) + EPS)
        ckvn = (ckv * r * gkv_ref[...]).astype(BF16)
        ckvt_ref[...] = ckvn.T
        kv = _dot(ckvn, wkv_ref[...])
        c, s1, s2 = c_ref[...], s1_ref[...], s2_ref[...]
        lane = lax.broadcasted_iota(jnp.int32, (tm, LANE), 1)
        kpe = _rope(kpe_ref[...], c, s1, s2) + jnp.where((lane == QK) | (lane == QK + 1), 1.0, 0.0)
        vone = jnp.where((lane == VDIM) | (lane == VDIM + 1), 1.0, 0.0)
        for h in range(HEADS):
            sl = slice(LANE * h, LANE * (h + 1))
            qr_ref[:, sl] = (_rope(q[:, sl], c, s1, s2) * scale).astype(BF16)
            kr_ref[:, sl] = (kv[:, sl] + kpe).astype(BF16)
            v_ref[:, sl] = (kv[:, HEADS * LANE + LANE * h:HEADS * LANE + LANE * (h + 1)] + vone).astype(BF16)

    row = lambda w, j: pl.BlockSpec((tm, w), lambda i: (i, j))
    col = lambda w: pl.BlockSpec((w, tm), lambda i: (0, i))
    full = lambda a: pl.BlockSpec(a.shape, lambda i: (0, 0))
    return pl.pallas_call(
        body,
        grid=(s // tm,),
        in_specs=[row(768, 6), row(256, 21), row(128, 44), full(g_q), full(g_kv), full(w_uq_p), full(w_kv_p),
                  row(128, 0), row(128, 0), row(128, 0)] + [full(a) for a in casts],
        out_specs=[row(1024, 0), row(1024, 0), row(1024, 0), col(768), col(256)] + [full(a) for a in casts],
        out_shape=[jax.ShapeDtypeStruct((s, 1024), BF16), jax.ShapeDtypeStruct((s, 1024), BF16),
                   jax.ShapeDtypeStruct((s, 1024), BF16), jax.ShapeDtypeStruct((768, s), BF16),
                   jax.ShapeDtypeStruct((256, s), BF16)] + [jax.ShapeDtypeStruct(a.shape, BF16) for a in casts],
        compiler_params=_params(("arbitrary",)),
        name="mla_prep",
    )(proj, proj, proj, g_q, g_kv, w_uq_p, w_kv_p, rc, rs1, rs2, *casts)


ATT_T = 512
ATT_FWD_HEADS = 4


def _chunk_mask(transposed):
    r = lax.broadcasted_iota(jnp.int32, (ATT_T, ATT_T), 0) >> ATT_CHUNK_SHIFT
    c = lax.broadcasted_iota(jnp.int32, (ATT_T, ATT_T), 1) >> ATT_CHUNK_SHIFT
    return (r <= c) if transposed else (c <= r)


def _attn_fwd(qr, kr, vp, shards):
    s = qr.shape[0]
    t = ATT_T
    g = ATT_FWD_HEADS
    ns = len(shards)

    def body(q_ref, k_ref, v_ref, *rest):
        shard_refs, (o_ref, qa_ref), got_refs = rest[:ns], rest[ns:ns + 2], rest[ns + 2:2 * ns + 2]
        sc_ref, sems = rest[2 * ns + 2], rest[2 * ns + 3:]
        qi = pl.program_id(1)

        @pl.when((pl.program_id(0) == 0) & (qi == 0))
        def _():
            _start_all(*_to_all_copies(shard_refs, got_refs, sems, True))
        lane = lax.broadcasted_iota(jnp.int32, (t, LANE), 1)
        sls = [slice(LANE * a, LANE * (a + 1)) for a in range(g)]
        qs = [q_ref[:, sl] for sl in sls]

        def scores(j):
            rows = pl.ds(pl.multiple_of(j * t, t), t)
            for a in range(g):
                sc_ref[j & 1, a] = _dotg(qs[a], k_ref[rows, sls[a]], NT)

        def step(j, carry, masked):
            rows = pl.ds(pl.multiple_of(j * t, t), t)
            out = []
            for a in range(g):
                m, acc = carry[a]
                sc = sc_ref[j & 1, a]
                if masked:
                    sc = jnp.where(_chunk_mask(False), sc, -1e30)
                m_new = jnp.maximum(m, jnp.max(sc, axis=-1, keepdims=True))
                p = jnp.exp(sc - m_new).astype(BF16)
                acc = jnp.exp(m - m_new) * acc + _dot(p, v_ref[rows, sls[a]])
                out.append((m_new, acc))
            return tuple(out)

        def loop(j, carry):
            carry = step(j, carry, False)
            scores(j + 1)
            return carry

        init = tuple((jnp.full((t, 1), -1e30, F32), jnp.zeros((t, LANE), F32)) for _ in range(g))
        scores(0)
        carry = lax.fori_loop(0, qi, loop, init)
        carry = step(qi, carry, True)
        outs = []
        for a in range(g):
            m, acc = carry[a]
            l = acc[:, VDIM:VDIM + 1]
            outs.append(acc / l)
            hi, lo_part = _hi_lo(-(m + jnp.log(l)))
            qa = jnp.where(lane == QK, hi, jnp.where(lane == QK + 1, lo_part, qs[a].astype(F32)))
            qa_ref[:, sls[a]] = qa.astype(BF16)
        for p in range(g // 2):
            o_ref[:, LANE * p:LANE * (p + 1)] = jnp.where(lane < VDIM, outs[2 * p], pltpu.roll(outs[2 * p + 1], VDIM, 1))

        @pl.when((pl.program_id(0) == HEADS // g - 1) & (qi == s // t - 1))
        def _():
            _wait_all(*_to_all_copies(shard_refs, got_refs, sems, True))

    return pl.pallas_call(
        body,
        grid=(HEADS // g, s // t),
        in_specs=[
            pl.BlockSpec((t, g * LANE), lambda h, i: (i, h)),
            pl.BlockSpec((s, g * LANE), lambda h, i: (0, h)),
            pl.BlockSpec((s, g * LANE), lambda h, i: (0, h)),
        ] + [ANY] * ns,
        out_specs=[
            pl.BlockSpec((t, g * VDIM), lambda h, i: (i, h)),
            pl.BlockSpec((t, g * LANE), lambda h, i: (i, h)),
        ] + [ANY] * ns,
        out_shape=[jax.ShapeDtypeStruct((s, 512), F32), jax.ShapeDtypeStruct((s, 1024), BF16)]
        + [jax.ShapeDtypeStruct((N_DEV,) + b.shape, b.dtype) for b in shards],
        scratch_shapes=[pltpu.VMEM((2, g, t, t), F32)] + _copy_sems(ns, 7),
        compiler_params=_params(("arbitrary", "arbitrary")),
        name="attn_fwd",
    )(qr, kr, vp, *shards)


def _attn_bwd(qa, kr, vp, dop, sends):
    s = qa.shape[0]
    t = ATT_T
    nq = s // t
    ns = len(sends)

    def body(q_ref, k_ref, v_ref, do_ref, *rest):
        send_refs, (dq_out, dk_out, dv_out) = rest[:ns], rest[ns:ns + 3]
        recv_refs = rest[ns + 3:2 * ns + 3]
        (dq_ref, dk_ref, dv_ref), sems = rest[2 * ns + 3:2 * ns + 6], rest[2 * ns + 6:]
        j = pl.program_id(1)
        sls = [slice(LANE * a, LANE * (a + 1)) for a in range(2)]

        @pl.when((pl.program_id(0) == 0) & (j == 0))
        def _():
            _start_all(*_to_all_copies(send_refs, recv_refs, sems, False))

        @pl.when(j == 0)
        def _():
            dq_ref[...] = jnp.zeros_like(dq_ref)

        dk_ref[...] = jnp.zeros_like(dk_ref)
        dv_ref[...] = jnp.zeros_like(dv_ref)
        ks = [k_ref[:, sl] for sl in sls]
        vs = [v_ref[:, sl] for sl in sls]

        def part(i, k_lo, k_n, q_lo, q_n, masked):
            rows = pl.ds(pl.multiple_of(i * t + q_lo, 256), q_n)
            keys = slice(k_lo, k_lo + k_n)
            for a in range(2):
                q = q_ref[rows, sls[a]]
                do = do_ref[rows, sls[a]]
                sc = _dotg(ks[a][keys], q, NT)
                if masked:
                    kc = lax.broadcasted_iota(jnp.int32, (k_n, q_n), 0) >> ATT_CHUNK_SHIFT
                    qc = lax.broadcasted_iota(jnp.int32, (k_n, q_n), 1) >> ATT_CHUNK_SHIFT
                    sc = jnp.where(kc <= qc, sc, -1e30)
                p = jnp.exp(sc)
                ds = (p * _dotg(vs[a][keys], do, NT)).astype(BF16)
                dv_ref[keys, sls[a]] += _dot(p.astype(BF16), do)
                dk_ref[keys, sls[a]] += _dot(ds, q)
                dq_ref[rows, sls[a]] += _dotg(ds, ks[a][keys], TN)

        half = t // 2
        part(j, 0, half, 0, t, True)
        part(j, half, half, half, half, True)

        def loop(i, c):
            part(i, 0, t, 0, t, False)
            return c

        lax.fori_loop(j + 1, nq, loop, 0)
        dk_out[...] = dk_ref[...].astype(BF16)
        dv_out[...] = dv_ref[...].astype(BF16)

        @pl.when(j == nq - 1)
        def _():
            dq_out[...] = dq_ref[...].astype(BF16)

        @pl.when((pl.program_id(0) == HEADS // 2 - 1) & (j == nq - 1))
        def _():
            _wait_all(*_to_all_copies(send_refs, recv_refs, sems, False))

    blk = pl.BlockSpec((t, 2 * LANE), lambda h, j: (j, h))
    whole = pl.BlockSpec((s, 2 * LANE), lambda h, j: (0, h))
    out = jax.ShapeDtypeStruct((s, 1024), BF16)
    return pl.pallas_call(
        body,
        grid=(HEADS // 2, nq),
        in_specs=[whole, blk, blk, whole] + [ANY] * ns,
        out_specs=[whole, blk, blk] + [ANY] * ns,
        out_shape=[out, out, out] + [jax.ShapeDtypeStruct(a.shape, a.dtype) for a in sends],
        scratch_shapes=[pltpu.VMEM((s, 2 * LANE), F32), pltpu.VMEM((t, 2 * LANE), F32),
                        pltpu.VMEM((t, 2 * LANE), F32)] + _copy_sems(ns, 7),
        compiler_params=_params(("arbitrary", "arbitrary")),
        name="attn_bwd",
    )(qa, kr, vp, dop, *sends)


HG_T = 256
HG_NC = HG_T // HG_BLOCK
HG_G = 4
GW = 64 * HG_G


def _hg_consts():
    r = jnp.arange(HG_T)[:, None]
    c = jnp.arange(HG_T)[None, :]
    same = (r // HG_BLOCK) == (c // HG_BLOCK)
    mcum = (same & (c <= r)).astype(BF16)
    mrev = (same & (c >= r)).astype(BF16)
    msum = same.astype(BF16)
    a = jnp.arange(GW) // 64
    bd = (a[:, None] == a[None, :]).astype(F32)
    return mcum, mrev, msum, bd


def _stack_heads(xg, head):
    return jnp.concatenate([jnp.where(head == h, xg, 0.0) for h in range(HG_G)], axis=0)


def _unstack_heads(r, head, t):
    out = r[(HG_G - 1) * t:]
    for h in range(HG_G - 2, -1, -1):
        out = jnp.where(head == h, r[h * t:(h + 1) * t], out)
    return out


def _compact_state(st):
    out = st[:64]
    for h in range(1, HG_G):
        out = out + st[64 * h:64 * (h + 1)]
    return out


def _expand_state(cs, head64):
    return jnp.concatenate([jnp.where(head64 == h, cs, 0.0) for h in range(HG_G)], axis=0)


def _hg_pre(hq, hf, lbl, mcum, msum):
    lb = _sigmoid(lbl[0:1, :] - lbl[1:2, :])
    sig = _sigmoid(hf)
    f = lb + (1.0 - lb) * sig
    lf = jnp.log(f)
    b = _sel_left(mcum, lf)
    big_l = _sel_left(msum, lf)
    k = 1.0 - f
    qd = hq * jnp.exp(b)
    ki = k * jnp.exp(-b)
    ke = k * jnp.exp(big_l - b)
    return lb, sig, f, b, big_l, qd, ki, ke


def _hgrn_fwd(proj, lbl):
    s = proj.shape[0]
    t = HG_T
    mcum, _, msum, bd = _hg_consts()

    def body(hq_ref, hf_ref, hi_ref, lbl_ref, mcum_ref, msum_ref, bd_ref, o_ref, sp_ref, st_ref):
        @pl.when(pl.program_id(0) == 0)
        def _():
            st_ref[...] = jnp.zeros_like(st_ref)

        mc = mcum_ref[...]
        _, _, _, _, big_l, qd, ki, ke = _hg_pre(hq_ref[...], hf_ref[...], lbl_ref[...], mc, msum_ref[...])
        el = jnp.exp(big_l)
        hi = hi_ref[...]
        head = lax.broadcasted_iota(jnp.int32, (t, GW), 1) >> 6
        mask = jnp.concatenate([mc] * HG_G, axis=0) > 0.5
        for p in range(HEADS // HG_G):
            sl = slice(GW * p, GW * (p + 1))
            vp = hi[:, sl].astype(BF16)
            qs = _stack_heads(qd[:, sl], head).astype(BF16)
            a = jnp.where(mask, _dotg(qs, ki[:, sl].astype(BF16), NT), 0.0)
            o_intra = _unstack_heads(_dot(a.astype(BF16), vp), head, t)
            qb = qd[:, sl].astype(BF16)
            kb = ke[:, sl].astype(BF16)
            st = st_ref[p]
            for c in range(HG_NC):
                rows = slice(HG_BLOCK * c, HG_BLOCK * (c + 1))
                sp_ref[c, :, sl] = _compact_state(st)
                o_ref[rows, sl] = o_intra[rows] + _dotg(qb[rows], st.astype(BF16), NT)
                u = _dotg(vp[rows], kb[rows], TN) * bd_ref[...]
                st = st * el[HG_BLOCK * c:HG_BLOCK * c + 1, sl] + u
            st_ref[p] = st

    row = lambda j: pl.BlockSpec((t, HG_WIDTH), lambda i: (i, j))
    full = lambda a: pl.BlockSpec(a.shape, lambda i: (0, 0))
    return pl.pallas_call(
        body,
        grid=(s // t,),
        in_specs=[row(6), row(7), row(8), full(lbl), full(mcum), full(msum), full(bd)],
        out_specs=[row(0), pl.BlockSpec((HG_NC, 64, HG_WIDTH), lambda i: (i, 0, 0))],
        out_shape=[jax.ShapeDtypeStruct((s, HG_WIDTH), F32),
                   jax.ShapeDtypeStruct((s // HG_BLOCK, 64, HG_WIDTH), F32)],
        scratch_shapes=[pltpu.VMEM((HEADS // HG_G, GW, GW), F32)],
        compiler_params=_params(("arbitrary",)),
        name="hgrn_fwd",
    )(proj, proj, proj, lbl, mcum, msum, bd)


def _slot_shape(name, r, c):
    return (N_DEV, r, c // N_DEV) if COL_SHARDED[name] else (N_DEV, r // N_DEV, c)


def _emit_slots(name, acc_ref, out_ref):
    r, c = acc_ref.shape
    for p in range(N_DEV):
        if COL_SHARDED[name]:
            out_ref[p] = acc_ref[:, c // N_DEV * p:c // N_DEV * (p + 1)].astype(BF16)
        else:
            out_ref[p] = acc_ref[r // N_DEV * p:r // N_DEV * (p + 1), :].astype(BF16)


def _hgrn_bwd(proj, lbl, do, sprev, dproj, pairs):
    s = proj.shape[0]
    t = HG_T
    nt = s // t
    npair = len(pairs)
    mcum, mrev, msum, bd = _hg_consts()

    def body(hq_ref, hf_ref, hi_ref, lbl_ref, do_ref, sp_ref, mcum_ref, mrev_ref, msum_ref, bd_ref,
             dproj_in, *rest):
        del dproj_in
        pair_refs, (dh_ref, dlbl_ref) = rest[:2 * npair], rest[2 * npair:2 * npair + 2]
        dw_refs, g_ref, acc_refs = rest[2 * npair + 2:3 * npair + 2], rest[3 * npair + 2], rest[3 * npair + 3:]

        @pl.when(pl.program_id(0) == 0)
        def _():
            g_ref[...] = jnp.zeros_like(g_ref)
            dlbl_ref[...] = jnp.zeros_like(dlbl_ref)
            for acc_ref in acc_refs:
                acc_ref[...] = jnp.zeros_like(acc_ref)

        for n, acc_ref in enumerate(acc_refs):
            acc_ref[...] += _dot(pair_refs[2 * n][...], pair_refs[2 * n + 1][...])

        @pl.when(pl.program_id(0) == nt - 1)
        def _():
            for (name, _, _), acc_ref, dw_ref in zip(pairs, acc_refs, dw_refs):
                _emit_slots(name, acc_ref, dw_ref)

        mc = mcum_ref[...]
        lb, sig, f, b, big_l, qd, ki, ke = _hg_pre(hq_ref[...], hf_ref[...], lbl_ref[...], mc, msum_ref[...])
        el = jnp.exp(big_l)
        hi = hi_ref[...]
        dov = do_ref[...]
        head = lax.broadcasted_iota(jnp.int32, (t, GW), 1) >> 6
        head64 = lax.broadcasted_iota(jnp.int32, (64, GW), 1) >> 6
        mask = jnp.concatenate([mc] * HG_G, axis=0) > 0.5
        dqd_parts, dke_parts, dv_parts, del_parts, dki_parts = [], [], [], [], []
        for p in range(HEADS // HG_G):
            sl = slice(GW * p, GW * (p + 1))
            vp = hi[:, sl].astype(BF16)
            qs = _stack_heads(qd[:, sl], head).astype(BF16)
            kip = ki[:, sl].astype(BF16)
            dos = _stack_heads(dov[:, sl], head).astype(BF16)
            a = jnp.where(mask, _dotg(qs, kip, NT), 0.0).astype(BF16)
            da = jnp.where(mask, _dotg(dos, vp, NT), 0.0).astype(BF16)
            r = _dot(da, kip)
            dki_parts.append(_dotg(da, qs, TN))
            qb = qd[:, sl].astype(BF16)
            kb = ke[:, sl].astype(BF16)
            dob = dov[:, sl].astype(BF16)
            g = g_ref[p]
            dqd_c, dv_c, dke_c, del_c = [], [], [], []
            for c in range(HG_NC - 1, -1, -1):
                rows = slice(HG_BLOCK * c, HG_BLOCK * (c + 1))
                gb = g.astype(BF16)
                st = _expand_state(sp_ref[c, :, sl], head64)
                dqd_c.append(_dot(dob[rows], st.astype(BF16)))
                dv_c.append(_dotg(kb[rows], gb, NT))
                dke_c.append(_dot(vp[rows], gb))
                del_c.append(jnp.broadcast_to(jnp.sum(g * st, axis=0, keepdims=True), (HG_BLOCK, GW)))
                g = g * el[HG_BLOCK * c:HG_BLOCK * c + 1, sl] + _dotg(dob[rows], qb[rows], TN) * bd_ref[...]
            g_ref[p] = g
            up = lambda parts: jnp.concatenate(parts[::-1], axis=0)
            dqd_parts.append(_unstack_heads(r, head, t) + up(dqd_c))
            dv_parts.append(_dotg(a, dos, TN) + up(dv_c))
            dke_parts.append(up(dke_c))
            del_parts.append(up(del_c))
        wide = lambda parts: jnp.concatenate(parts, axis=1)
        dqd, dke, dki, dvv, del_rows = wide(dqd_parts), wide(dke_parts), wide(dki_parts), wide(dv_parts), wide(del_parts)
        dh_ref[:, :HG_WIDTH] = (dqd * jnp.exp(b)).astype(BF16)
        dh_ref[:, 2 * HG_WIDTH:] = dvv.astype(BF16)
        dke_ke = dke * ke
        db = dqd * qd - dki * ki - dke_ke
        dl_rows = _sel_left(msum_ref[...], dke_ke) + del_rows * el
        is_last = (lax.broadcasted_iota(jnp.int32, (t, HG_WIDTH), 0) & (HG_BLOCK - 1)) == HG_BLOCK - 1
        db = db + jnp.where(is_last, dl_rows, 0.0)
        dlf = _sel_left(mrev_ref[...], db)
        dk = dki * jnp.exp(-b) + dke * jnp.exp(big_l - b)
        df = dlf / f - dk
        dh_ref[:, HG_WIDTH:2 * HG_WIDTH] = (df * (1.0 - lb) * sig * (1.0 - sig)).astype(BF16)
        dlb = jnp.sum(df * (1.0 - sig), axis=0, keepdims=True) * lb * (1.0 - lb)
        dlbl_ref[0:1, :] += dlb
        dlbl_ref[1:2, :] -= dlb

    rrow = lambda j: pl.BlockSpec((t, HG_WIDTH), lambda i: (nt - 1 - i, j))
    full = lambda a: pl.BlockSpec(a.shape, lambda i: (0, 0))
    pair_specs, dw_specs, dw_shapes, accs = [], [], [], []
    for name, at, b in pairs:
        pair_specs += [pl.BlockSpec((at.shape[0], t), lambda i: (0, i)), pl.BlockSpec((t, b.shape[1]), lambda i: (i, 0))]
        shape = _slot_shape(name, at.shape[0], b.shape[1])
        dw_specs.append(pl.BlockSpec(shape, lambda i: (0, 0, 0)))
        dw_shapes.append(jax.ShapeDtypeStruct(shape, BF16))
        accs.append(pltpu.VMEM((at.shape[0], b.shape[1]), F32))
    return pl.pallas_call(
        body,
        grid=(nt,),
        in_specs=[rrow(6), rrow(7), rrow(8), full(lbl), rrow(0),
                  pl.BlockSpec((HG_NC, 64, HG_WIDTH), lambda i: (nt - 1 - i, 0, 0)),
                  full(mcum), full(mrev), full(msum), full(bd), pl.BlockSpec(memory_space=pl.ANY)] + pair_specs,
        out_specs=[pl.BlockSpec((t, 3 * HG_WIDTH), lambda i: (nt - 1 - i, 2)),
                   pl.BlockSpec((2, HG_WIDTH), lambda i: (0, 0))] + dw_specs,
        out_shape=[jax.ShapeDtypeStruct(dproj.shape, BF16), jax.ShapeDtypeStruct((2, HG_WIDTH), F32)] + dw_shapes,
        input_output_aliases={10: 0},
        scratch_shapes=[pltpu.VMEM((HEADS // HG_G, GW, GW), F32)] + accs,
        compiler_params=_params(("arbitrary",)),
        name="hgrn_bwd",
    )(proj, proj, proj, lbl, do, sprev, mcum, mrev, msum, bd, dproj, *[a for pair in pairs for a in pair[1:]])


def _tail(x, tgt, proj, attn, o, w_a, w_b, w_out, w_at, w_bt, w_outt, b_gate, g_post, gh):
    s = x.shape[0]
    tm = 256
    ones64 = (jnp.arange(HG_WIDTH)[:, None] // 64 == jnp.arange(HG_WIDTH)[None, :] // 64).astype(BF16)
    weights = (w_a, w_b, w_out, w_at, w_bt, w_outt)

    def body(x_ref, t_ref, ml_ref, ga_ref, gb_ref, at_ref, o_ref, *rest):
        w_hbm, (bg_ref, gp_ref, gh_ref, ones_ref) = rest[:6], rest[6:10]
        (dout_ref, dpj_ref, dop_ref, do_ref, mt_ref, dy_ref, yat_ref, dya_ref, ybt_ref, dyb_ref,
         loss_ref, dgp_ref, dbg_ref, dgh_ref) = rest[10:24]
        (wa_ref, wb_ref, wo_ref, wat_ref, wbt_ref, wot_ref), w_sem = rest[24:30], rest[30]

        @pl.when(pl.program_id(0) == 0)
        def _():
            loads = [pltpu.make_async_copy(src, dst, w_sem.at[k])
                     for k, (src, dst) in enumerate(zip(w_hbm, rest[24:30]))]
            _start_all(loads, [])
            loss_ref[...] = jnp.zeros_like(loss_ref)
            dgp_ref[...] = jnp.zeros_like(dgp_ref)
            dbg_ref[...] = jnp.zeros_like(dbg_ref)
            dgh_ref[...] = jnp.zeros_like(dgh_ref)
            _wait_all(loads, [])

        ones = ones_ref[...]
        gate_a = ga_ref[...]
        sa = _sigmoid(gate_a)
        silu_a = gate_a * sa
        attn_v = at_ref[...]
        ya_in = attn_v * silu_a
        ov = o_ref[...]
        ro = lax.rsqrt(_sel_right(ov * ov, ones) * (1.0 / 64.0) + EPS)
        ohat = ov * ro
        ghv = gh_ref[...]
        on = ohat * ghv
        gate_b = gb_ref[...]
        sb = _sigmoid(gate_b)
        silu_b = gate_b * sb
        yb_in = on * silu_b
        ya_bf = ya_in.astype(BF16)
        yb_bf = yb_in.astype(BF16)
        yat_ref[...] = ya_bf.T
        ybt_ref[...] = yb_bf.T
        y_a = _dot(ya_bf, wa_ref[...])
        y_b = _dot(yb_bf, wb_ref[...])
        gts = _sigmoid(ml_ref[...] + bg_ref[...])
        g_a = gts[:, :D_MODEL]
        g_b = gts[:, D_MODEL:]
        m_bf = (g_a * y_a + g_b * y_b).astype(BF16)
        mt_ref[...] = m_bf.T
        y = _dot(m_bf, wo_ref[...])
        r1 = lax.rsqrt(jnp.mean(y * y, axis=-1, keepdims=True) + EPS)
        yn = y * r1
        gp = gp_ref[...]
        e = x_ref[...] + yn * gp - t_ref[...]
        loss_ref[...] += jnp.sum(e * e, axis=0, keepdims=True)
        dout = e * (1.0 / D_MODEL)
        dout_ref[...] = dout
        dgp_ref[...] += jnp.sum(dout * yn, axis=0, keepdims=True)
        dyn = dout * gp
        dy = r1 * (dyn - yn * jnp.mean(dyn * yn, axis=-1, keepdims=True))
        dy_bf = dy.astype(BF16)
        dy_ref[...] = dy_bf
        dm = _dot(dy_bf, wot_ref[...])
        dml_a = dm * y_a * g_a * (1.0 - g_a)
        dml_b = dm * y_b * g_b * (1.0 - g_b)
        dpj_ref[:, :D_MODEL] = dml_a.astype(BF16)
        dpj_ref[:, D_MODEL:2 * D_MODEL] = dml_b.astype(BF16)
        dbg_ref[:, :D_MODEL] += jnp.sum(dml_a, axis=0, keepdims=True)
        dbg_ref[:, D_MODEL:] += jnp.sum(dml_b, axis=0, keepdims=True)
        dya_bf = (dm * g_a).astype(BF16)
        dyb_bf = (dm * g_b).astype(BF16)
        dya_ref[...] = dya_bf
        dyb_ref[...] = dyb_bf
        dya_in = _dot(dya_bf, wat_ref[...])
        dyb_in = _dot(dyb_bf, wbt_ref[...])
        dattn = dya_in * silu_a
        delta = _sel_right(dattn * attn_v, ones)
        lane = lax.broadcasted_iota(jnp.int32, (tm, LANE), 1)
        for p in range(HEADS // 2):
            sl = slice(LANE * p, LANE * (p + 1))
            xs = (dattn[:, sl], pltpu.roll(dattn[:, sl], VDIM, 1))
            nds = (-pltpu.roll(delta[:, sl], VDIM, 1), -delta[:, sl])
            for a in range(2):
                hi, lo_part = _hi_lo(nds[a])
                blk = jnp.where(lane < VDIM, xs[a], jnp.where(lane == VDIM, hi, jnp.where(lane == VDIM + 1, lo_part, 0.0)))
                dop_ref[:, LANE * (2 * p + a):LANE * (2 * p + a + 1)] = blk.astype(BF16)
        dpj_ref[:, 2 * D_MODEL:2 * D_MODEL + HG_WIDTH] = (
            dya_in * attn_v * (sa * (1.0 + gate_a * (1.0 - sa)))).astype(BF16)
        don = dyb_in * silu_b
        dpj_ref[:, 2 * D_MODEL + HG_WIDTH:] = (dyb_in * on * (sb * (1.0 + gate_b * (1.0 - sb)))).astype(BF16)
        dgh_ref[...] += jnp.sum(don * ohat, axis=0, keepdims=True)
        dohat = don * ghv
        do_ref[...] = (ro * (dohat - ohat * (_sel_right(dohat * ohat, ones) * (1.0 / 64.0)))).astype(BF16)

    row = lambda w, j: pl.BlockSpec((tm, w), lambda i: (i, j))
    col = lambda w: pl.BlockSpec((w, tm), lambda i: (0, i))
    full = lambda a: pl.BlockSpec(a.shape, lambda i: (0, 0))
    acc = lambda w: pl.BlockSpec((1, w), lambda i: (0, 0))
    sds = lambda w, dt: jax.ShapeDtypeStruct((s, w), dt)
    sdt = lambda w: jax.ShapeDtypeStruct((w, s), BF16)
    return pl.pallas_call(
        body,
        grid=(s // tm,),
        in_specs=[row(1024, 0), row(1024, 0), row(2048, 0), row(512, 4), row(512, 5), row(512, 0), row(512, 0)]
        + [ANY] * 6 + [full(b_gate), full(g_post), full(gh), full(ones64)],
        out_specs=[row(1024, 0), row(3072, 0), row(1024, 0), row(512, 0),
                   col(1024), row(1024, 0), col(512), row(1024, 0), col(512), row(1024, 0),
                   acc(1024), acc(1024), acc(2048), acc(512)],
        out_shape=[sds(1024, F32), sds(D_IN_PAD, BF16), sds(1024, BF16), sds(512, BF16),
                   sdt(1024), sds(1024, BF16), sdt(512), sds(1024, BF16), sdt(512), sds(1024, BF16),
                   jax.ShapeDtypeStruct((1, 1024), F32), jax.ShapeDtypeStruct((1, 1024), F32),
                   jax.ShapeDtypeStruct((1, 2048), F32), jax.ShapeDtypeStruct((1, 512), F32)],
        scratch_shapes=[pltpu.VMEM(a.shape, BF16) for a in weights] + [pltpu.SemaphoreType.DMA((6,))],
        compiler_params=_params(("arbitrary",), 56),
        name="tail",
    )(x, tgt, proj, proj, proj, attn, o, *weights, b_gate, g_post, gh, ones64)


def _mla_bwd(proj, dqr, dkr, dv, g_q, g_kv, w_uq_pt, w_kv_pt, rc, rs1, rs2, cqt, ckvt, dproj):
    assert HEADS == N_DEV
    s = proj.shape[0]
    tm = 512
    scale = 1.0 / math.sqrt(QK)

    def body(cq_ref, ckv_ref, dqr_ref, dkr_ref, dv_ref, gq_ref, gkv_ref, wuqt_ref, wkvt_ref, c_ref, s1_ref, s2_ref,
             cqt_ref, ckvt_ref, dproj_in, dc_ref, dgq_ref, dgkv_ref, uq_slots, ukv_slots,
             dqf_ref, dkvf_ref, dwuq_ref, dwkv_ref):
        del dproj_in

        @pl.when(pl.program_id(0) == 0)
        def _():
            dgq_ref[...] = jnp.zeros_like(dgq_ref)
            dgkv_ref[...] = jnp.zeros_like(dgkv_ref)
            dwuq_ref[...] = jnp.zeros_like(dwuq_ref)
            dwkv_ref[...] = jnp.zeros_like(dwkv_ref)

        c, s1, s2 = c_ref[...], s1_ref[...], s2_ref[...]
        lane = lax.broadcasted_iota(jnp.int32, (tm, LANE), 1)
        ksum = jnp.zeros((tm, LANE), F32)
        for h in range(HEADS):
            sl = slice(LANE * h, LANE * (h + 1))
            dqf_ref[:, sl] = (_unrope(dqr_ref[:, sl], c, s1, s2) * scale).astype(BF16)
            dkh = dkr_ref[:, sl]
            ksum = ksum + dkh
            dkvf_ref[:, sl] = jnp.where(lane < NOPE, dkh, 0.0).astype(BF16)
            dkvf_ref[:, HEADS * LANE + LANE * h:HEADS * LANE + LANE * (h + 1)] = jnp.where(
                lane < VDIM, dv_ref[:, sl], 0.0).astype(BF16)
        dkpe = _unrope(ksum, c, s1, s2)
        dc_ref[:, Q_LORA + KV_LORA:] = jnp.where((lane >= NOPE) & (lane < QK), dkpe, 0.0).astype(BF16)
        dqf, dkvf = dqf_ref[...], dkvf_ref[...]
        dwuq_ref[...] += _dot(cqt_ref[...], dqf)
        dwkv_ref[...] += _dot(ckvt_ref[...], dkvf)
        dcqn = _dot(dqf, wuqt_ref[...])
        dckvn = _dot(dkvf, wkvt_ref[...])
        for x_ref, g_ref, dn, cols, dg_ref in ((cq_ref, gq_ref, dcqn, slice(0, Q_LORA), dgq_ref),
                                               (ckv_ref, gkv_ref, dckvn, slice(Q_LORA, Q_LORA + KV_LORA), dgkv_ref)):
            xv = x_ref[...]
            r = lax.rsqrt(jnp.mean(xv * xv, axis=-1, keepdims=True) + EPS)
            xh = xv * r
            dg_ref[...] += jnp.sum(dn * xh, axis=0, keepdims=True)
            dh = dn * g_ref[...]
            dc_ref[:, cols] = (r * (dh - xh * jnp.mean(dh * xh, axis=-1, keepdims=True))).astype(BF16)

        @pl.when(pl.program_id(0) == s // tm - 1)
        def _():
            ur = Q_LORA // N_DEV
            for p in range(N_DEV):
                uq_slots[p] = jnp.concatenate(
                    [dwuq_ref[ur * p:ur * (p + 1), LANE * h:LANE * h + QK] for h in range(HEADS)], axis=1).astype(BF16)
                ukv_slots[p] = jnp.concatenate(
                    [dwkv_ref[:, LANE * p:LANE * p + NOPE],
                     dwkv_ref[:, LANE * (HEADS + p):LANE * (HEADS + p) + VDIM]], axis=1).astype(BF16)

    row = lambda w, j: pl.BlockSpec((tm, w), lambda i: (i, j))
    full = lambda a: pl.BlockSpec(a.shape, lambda i: (0, 0))
    acc = lambda w: pl.BlockSpec((1, w), lambda i: (0, 0))
    col = lambda w: pl.BlockSpec((w, tm), lambda i: (0, i))
    whole = lambda shape: pl.BlockSpec(shape, lambda i: (0, 0, 0))
    uq_shape = (N_DEV, Q_LORA // N_DEV, HEADS * QK)
    ukv_shape = (N_DEV, KV_LORA, NOPE + VDIM)
    return pl.pallas_call(
        body,
        grid=(s // tm,),
        in_specs=[row(768, 6), row(256, 21), row(1024, 0), row(1024, 0), row(1024, 0), full(g_q), full(g_kv),
                  full(w_uq_pt), full(w_kv_pt), row(128, 0), row(128, 0), row(128, 0), col(Q_LORA), col(KV_LORA),
                  pl.BlockSpec(memory_space=pl.ANY)],
        out_specs=[row(1152, 4), acc(768), acc(256), whole(uq_shape), whole(ukv_shape)],
        out_shape=[jax.ShapeDtypeStruct(dproj.shape, BF16),
                   jax.ShapeDtypeStruct((1, 768), F32), jax.ShapeDtypeStruct((1, 256), F32),
                   jax.ShapeDtypeStruct(uq_shape, BF16), jax.ShapeDtypeStruct(ukv_shape, BF16)],
        input_output_aliases={14: 0},
        scratch_shapes=[pltpu.VMEM((tm, HEADS * LANE), BF16), pltpu.VMEM((tm, 2 * HEADS * LANE), BF16),
                        pltpu.VMEM((Q_LORA, HEADS * LANE), F32), pltpu.VMEM((KV_LORA, 2 * HEADS * LANE), F32)],
        compiler_params=_params(("arbitrary",)),
        name="mla_bwd",
    )(proj, proj, dqr, dkr, dv, g_q, g_kv, w_uq_pt, w_kv_pt, rc, rs1, rs2, cqt, ckvt, dproj)


def _dh_dx(dproj, w_in_pt, x, dout, g_pre, sends):
    s, k = dproj.shape
    tm = 256
    ns, ni = len(sends), s // tm

    def body(dp_ref, w_ref, x_ref, dout_ref, g_ref, *rest):
        send_refs, (dx_ref, dg_ref) = rest[:ns], rest[ns:ns + 2]
        recv_refs, sems = rest[ns + 2:2 * ns + 2], rest[2 * ns + 2:]

        @pl.when(pl.program_id(0) == 0)
        def _():
            _start_all(*_to_chips_copies(send_refs, recv_refs, sems))
            dg_ref[...] = jnp.zeros_like(dg_ref)

        dh = _dot(dp_ref[...], w_ref[...])
        xv = x_ref[...]
        r = lax.rsqrt(jnp.mean(xv * xv, axis=-1, keepdims=True) + EPS)
        xh = xv * r
        dg_ref[...] += jnp.sum(dh * xh, axis=0, keepdims=True)
        dxh = dh * g_ref[...]
        dx_ref[...] = dout_ref[...] + r * (dxh - xh * jnp.mean(dxh * xh, axis=-1, keepdims=True))

        @pl.when(pl.program_id(0) == ni - 1)
        def _():
            _wait_all(*_to_chips_copies(send_refs, recv_refs, sems))

    row = lambda w: pl.BlockSpec((tm, w), lambda i: (i, 0))
    return pl.pallas_call(
        body,
        grid=(ni,),
        in_specs=[row(k), pl.BlockSpec((k, D_MODEL), lambda i: (0, 0)), row(D_MODEL), row(D_MODEL),
                  pl.BlockSpec((1, D_MODEL), lambda i: (0, 0))] + [ANY] * ns,
        out_specs=[row(D_MODEL), pl.BlockSpec((1, D_MODEL), lambda i: (0, 0))] + [ANY] * ns,
        out_shape=[jax.ShapeDtypeStruct((s, D_MODEL), F32), jax.ShapeDtypeStruct((1, D_MODEL), F32)]
        + [jax.ShapeDtypeStruct(a.shape, a.dtype) for a in sends],
        scratch_shapes=_copy_sems(ns, 3),
        compiler_params=_params(("arbitrary",)),
        name="dh_dx",
    )(dproj, w_in_pt, x, dout, g_pre, *sends)


def _pair_reduce(slots):
    n = len(slots)
    half = [(N_DEV // 2,) + a.shape[1:] for a in slots]

    def body(*refs):
        s_refs, o_refs = refs[:n], refs[n:2 * n]
        mine, got = refs[2 * n:3 * n], refs[3 * n:4 * n]
        send_sems, recv_sems, local_sems, out_sems = refs[4 * n:]
        x, y, c = _my_place()
        copies, loads, stores = [], [], []
        for q in range(N_DEV // 2):
            for a in range(n):
                copies.append(pltpu.make_async_remote_copy(
                    src_ref=s_refs[a].at[2 * q + 1 - c], dst_ref=got[a].at[q],
                    send_sem=send_sems.at[4 * a + q], recv_sem=recv_sems.at[4 * a + q],
                    device_id=(x, y, 1 - c), device_id_type=MESH_ID))
                loads.append(pltpu.make_async_copy(s_refs[a].at[2 * q + c], mine[a].at[q], local_sems.at[4 * a + q]))
                stores.append(pltpu.make_async_copy(mine[a].at[q], o_refs[a].at[q], out_sems.at[4 * a + q]))
        _start_all(loads, copies)
        k = 0
        for q in range(N_DEV // 2):
            for a in range(n):
                loads[k].wait()
                copies[k].wait_recv()
                mine[a][q] = (mine[a][q].astype(F32) + got[a][q].astype(F32)).astype(mine[a].dtype)
                stores[k].start()
                k += 1
        for cp in copies:
            cp.wait_send()
        for cp in stores:
            cp.wait()

    vm = lambda: [pltpu.VMEM(h, a.dtype) for h, a in zip(half, slots)]
    return pl.pallas_call(
        body,
        in_specs=[ANY] * n,
        out_specs=[ANY] * n,
        out_shape=[jax.ShapeDtypeStruct(h, a.dtype) for h, a in zip(half, slots)],
        scratch_shapes=vm() + vm() + [pltpu.SemaphoreType.DMA((4 * n,)), pltpu.SemaphoreType.DMA((4 * n,)),
                                      pltpu.SemaphoreType.DMA((4 * n,)), pltpu.SemaphoreType.DMA((4 * n,))],
        compiler_params=pltpu.CompilerParams(vmem_limit_bytes=48 * 2**20),
        name="pair_reduce",
    )(*slots)


def _rope_tables(s):
    inv = (np.float32(ROPE_THETA) ** (-np.arange(0, ROPE, 2, dtype=np.float32) / np.float32(ROPE))).astype(np.float32)
    ang = (np.arange(s, dtype=np.float32)[:, None] * inv[None, :]).astype(np.float32)
    cos, sin = jnp.asarray(np.cos(ang.astype(np.float64)), F32), jnp.asarray(np.sin(ang.astype(np.float64)), F32)
    z = lambda w: jnp.zeros((s, w), F32)
    rc = jnp.concatenate([jnp.ones((s, NOPE), F32), cos, cos, z(32)], axis=1)
    rs1 = jnp.concatenate([z(NOPE), -sin, z(16), z(32)], axis=1)
    rs2 = jnp.concatenate([z(NOPE), z(16), sin, z(32)], axis=1)
    return rc, rs1, rs2


def _step(x, tgt, w_blk, shards, g_pre, b_gate, g_q, g_kv, lbl, g_hgrn, g_post):
    s = x.shape[0]
    rc, rs1, rs2 = _rope_tables(s)
    gh = jnp.tile(g_hgrn, (1, HEADS))

    proj, w_in_pt, ht, *got = _gather_proj(x, g_pre, w_blk, shards[:2])
    w_uq, w_ukv = (_from_slots(n, g) for n, g in zip(MATS[:2], got))
    w_uq_p = jnp.pad(w_uq.reshape(Q_LORA, HEADS, QK), ((0, 0), (0, 0), (0, LANE - QK))).reshape(Q_LORA, HEADS * LANE)
    kv3 = w_ukv.reshape(KV_LORA, HEADS, NOPE + VDIM)
    pad64 = lambda t: jnp.pad(t, ((0, 0), (0, 0), (0, LANE - 64))).reshape(KV_LORA, HEADS * LANE)
    w_kv_p = jnp.concatenate([pad64(kv3[:, :, :NOPE]), pad64(kv3[:, :, NOPE:])], axis=1)

    qr, kr, v, cqt, ckvt, *later_shards = _mla_prep(proj, g_q, g_kv, w_uq_p, w_kv_p, rc, rs1, rs2, shards[2:])
    attn, qa, *got = _attn_fwd(qr, kr, v, later_shards)
    w_a, w_b, w_out = (_from_slots(n, g) for n, g in zip(MATS[2:], got))
    o, sprev = _hgrn_fwd(proj, lbl)
    (dout, dproj, dop, do, mt, dy_bf, yat, dya_bf, ybt, dyb_bf,
     loss_vec, dg_post, db_gate, dgh) = _tail(x, tgt, proj, attn, o, w_a, w_b, w_out, w_a.T, w_b.T, w_out.T,
                                               b_gate, g_post, gh)
    dproj, dlbl, *early = _hgrn_bwd(proj, lbl, do, sprev, dproj,
                                    [("w_branch_a", yat, dya_bf), ("w_branch_b", ybt, dyb_bf), ("w_out", mt, dy_bf)])
    dqr, dkr, dv, *early_recv = _attn_bwd(qa, kr, v, dop, early)
    dproj, dg_q, dg_kv, dw_uq_slots, dw_ukv_slots = _mla_bwd(proj, dqr, dkr, dv, g_q, g_kv, w_uq_p.T, w_kv_p.T,
                                                             rc, rs1, rs2, cqt, ckvt, dproj)

    dw_in_slots = _dw_in_slots(ht, dproj)
    late = _pair_reduce([dw_in_slots, dw_uq_slots, dw_ukv_slots])
    dx, dg_pre, *late_recv = _dh_dx(dproj, w_in_pt, x, dout, g_pre, late)

    g_sum = _vectors_sum(dg_pre, db_gate, dg_q, dg_kv, dlbl, dgh, dg_post, loss_vec)
    return dx, late_recv[0], dict(zip(MATS, late_recv[1:] + early_recv)), g_sum


def _adamw(g, w, m, v):
    c1 = 1.0 / (1.0 - ADAM_B1 ** ADAM_STEP)
    c2 = 1.0 / (1.0 - ADAM_B2 ** ADAM_STEP)
    nm = ADAM_B1 * m + (1.0 - ADAM_B1) * g
    nv = ADAM_B2 * v + (1.0 - ADAM_B2) * (g * g)
    d = -ADAM_LR * ((nm * c1) / (jnp.sqrt(nv * c2) + ADAM_EPS) + ADAM_WD * w)
    return d, nm, nv


def _sum8(r_ref):
    g = r_ref[0].astype(F32)
    for k in range(1, r_ref.shape[0]):
        g = g + r_ref[k].astype(F32)
    return g


def _sum_adamw_w_in(recv, w, m, v):
    rows, _, cols = w.shape
    tc = 512
    nc = cols // tc

    def body(r_ref, w_hbm, m_hbm, v_hbm, g_hbm, d_hbm, nm_hbm, nv_hbm, ins, outs, in_sems, out_sems):
        i = pl.program_id(0)
        slot = i & 1
        cols_of = lambda step: pl.ds(pl.multiple_of(step * tc, tc), tc)

        def load(k, step, sl):
            return pltpu.make_async_copy((w_hbm, m_hbm, v_hbm)[k].at[:, 0, cols_of(step)], ins.at[sl, k],
                                         in_sems.at[sl, k])

        def store(k, step, sl):
            return pltpu.make_async_copy(outs.at[sl, k], (g_hbm, d_hbm, nm_hbm, nv_hbm)[k].at[:, 0, cols_of(step)],
                                         out_sems.at[sl, k])

        @pl.when(i == 0)
        def _():
            for k in range(3):
                load(k, 0, 0).start()

        @pl.when(i + 1 < nc)
        def _():
            for k in range(3):
                load(k, i + 1, 1 - slot).start()

        @pl.when(i >= 2)
        def _():
            for k in range(4):
                store(k, i - 2, slot).wait()

        for k in range(3):
            load(k, i, slot).wait()
        g = _sum8(r_ref)
        d, nm, nv = _adamw(g, ins[slot, 0], ins[slot, 1], ins[slot, 2])
        for k, val in enumerate((g, d, nm, nv)):
            outs[slot, k] = val
        for k in range(4):
            store(k, i, slot).start()

        @pl.when(i == nc - 1)
        def _():
            for k in range(4):
                store(k, i, slot).wait()
            if nc >= 2:
                for k in range(4):
                    store(k, i - 1, 1 - slot).wait()

    out = jax.ShapeDtypeStruct((rows, 1, cols), F32)
    return pl.pallas_call(
        body,
        grid=(nc,),
        in_specs=[pl.BlockSpec((recv.shape[0], rows, tc), lambda i: (0, 0, i)), ANY, ANY, ANY],
        out_specs=[ANY, ANY, ANY, ANY],
        out_shape=[out, out, out, out],
        scratch_shapes=[pltpu.VMEM((2, 3, rows, tc), F32), pltpu.VMEM((2, 4, rows, tc), F32),
                        pltpu.SemaphoreType.DMA((2, 3)), pltpu.SemaphoreType.DMA((2, 4))],
        compiler_params=_params(("arbitrary",)),
        name="sum_adamw_w_in",
    )(recv, w, m, v)


def _sum_adamw_whole(recvs, ws, ms, vs):
    n = len(ws)

    def body(*refs):
        r_refs, w_refs, m_refs, v_refs = refs[:n], refs[n:2 * n], refs[2 * n:3 * n], refs[3 * n:4 * n]
        outs = refs[4 * n:]
        for a in range(n):
            g = _sum8(r_refs[a])
            d, nm, nv = _adamw(g, w_refs[a][...], m_refs[a][...], v_refs[a][...])
            outs[a][...] = g
            outs[n + a][...] = d
            outs[2 * n + a][...] = nm
            outs[3 * n + a][...] = nv

    shapes = [jax.ShapeDtypeStruct(w.shape, F32) for w in ws]
    res = pl.pallas_call(
        body,
        out_shape=shapes * 4,
        compiler_params=pltpu.CompilerParams(vmem_limit_bytes=48 * 2**20),
        name="sum_adamw_mats",
    )(*recvs, *ws, *ms, *vs)
    return res[:n], res[n:2 * n], res[2 * n:3 * n], res[3 * n:]


SMALL = ("g_pre", "b_gate", "g_q", "g_kv", "lb_logits", "g_hgrn", "g_post")
SMALL_SHAPE = dict(g_pre=(1, 1024), b_gate=(1, 2048), g_q=(1, 768), g_kv=(1, 256), lb_logits=(2, 512),
                   g_hgrn=(1, 64), g_post=(1, 1024))


def _vectors_sum(dg_pre, db_gate, dg_q, dg_kv, dlbl, dgh, dg_post, loss_vec):
    def body(gpre_ref, bg_ref, gq_ref, gkv_ref, lbl_ref, gh_ref, gpost_ref, loss_ref, out_ref, mine, got,
             send_sems, recv_sems):
        mine[...] = jnp.zeros_like(mine)
        mine[0:1, :] = gpre_ref[...]
        mine[1:2, :] = bg_ref[:, :1024]
        mine[2:3, :] = bg_ref[:, 1024:]
        mine[3:4, :Q_LORA] = gq_ref[...]
        mine[4:5, :KV_LORA] = gkv_ref[...]
        loss = (0.5 / D_MODEL) * jnp.sum(loss_ref[...], axis=-1, keepdims=True)
        mine[4:5, KV_LORA:] = jnp.broadcast_to(loss, (1, 1024 - KV_LORA))
        mine[5:6, :HG_WIDTH] = lbl_ref[0:1, :]
        mine[5:6, HG_WIDTH:] = lbl_ref[1:2, :]
        gh = gh_ref[...]
        fold = gh[:, :VDIM]
        for h in range(1, HEADS):
            fold = fold + gh[:, VDIM * h:VDIM * (h + 1)]
        mine[6:7, :VDIM] = fold
        mine[7:8, :] = gpost_ref[...]
        x, y, c = _my_place()
        me = 4 * x + 2 * y + c
        got[me] = mine[...]
        copies = [pltpu.make_async_remote_copy(
            src_ref=mine, dst_ref=got.at[me], send_sem=send_sems.at[k], recv_sem=recv_sems.at[k],
            device_id=_flip(k, x, y, c), device_id_type=MESH_ID) for k in range(N_DEV - 1)]
        _start_all([], copies)
        _wait_all([], copies)
        out_ref[...] = _sum8(got)

    return pl.pallas_call(
        body,
        out_shape=jax.ShapeDtypeStruct((8, 1024), F32),
        scratch_shapes=[pltpu.VMEM((8, 1024), F32), pltpu.VMEM((N_DEV, 8, 1024), F32),
                        pltpu.SemaphoreType.DMA((7,)), pltpu.SemaphoreType.DMA((7,))],
        name="vectors_sum",
    )(dg_pre, db_gate, dg_q, dg_kv, dlbl, dgh, dg_post, loss_vec)


def _vectors_adamw(g_sum, ws, ms, vs):
    n = len(SMALL)

    def body(g_ref, *refs):
        w_refs, m_refs, v_refs = refs[:n], refs[n:2 * n], refs[2 * n:3 * n]
        loss_ref, outs = refs[3 * n], refs[3 * n + 1:]
        g = g_ref[...]
        loss_ref[...] = g[4:5, KV_LORA:KV_LORA + 1]
        grads = (g[0:1, :], jnp.concatenate([g[1:2, :], g[2:3, :]], axis=1), g[3:4, :Q_LORA], g[4:5, :KV_LORA],
                 jnp.concatenate([g[5:6, :HG_WIDTH], g[5:6, HG_WIDTH:]], axis=0), g[6:7, :VDIM], g[7:8, :])
        for a in range(n):
            d, nm, nv = _adamw(grads[a], w_refs[a][...], m_refs[a][...], v_refs[a][...])
            outs[a][...] = grads[a]
            outs[n + a][...] = d
            outs[2 * n + a][...] = nm
            outs[3 * n + a][...] = nv

    shapes = [jax.ShapeDtypeStruct(SMALL_SHAPE[k], F32) for k in SMALL]
    res = pl.pallas_call(
        body,
        out_shape=[jax.ShapeDtypeStruct((1, 1), F32)] + shapes * 4,
        name="vectors_adamw",
    )(g_sum, *ws, *ms, *vs)
    return res[0], res[1:n + 1], res[n + 1:2 * n + 1], res[2 * n + 1:3 * n + 1], res[3 * n + 1:]


MATS = ("w_uq", "w_ukv", "w_branch_a", "w_branch_b", "w_out")
COL_SHARDED = dict(w_uq=False, w_ukv=True, w_branch_a=True, w_branch_b=True, w_out=False)
ORDER = ("g_pre", "w_in", "b_gate", "g_q", "w_uq", "g_kv", "w_ukv", "lb_logits", "g_hgrn",
         "w_branch_a", "w_branch_b", "w_out", "g_post")


def _from_slots(name, slots):
    _, r, c = slots.shape
    if COL_SHARDED[name]:
        return slots.transpose(1, 0, 2).reshape(r, N_DEV * c)
    return slots.reshape(N_DEV * r, c)


def kernel(x, g_pre, w_in, b_gate, g_q, w_uq, g_kv, w_ukv, lb_logits, g_hgrn, w_branch_a, w_branch_b, w_out, g_post, loss_target, m_g_pre, m_w_in, m_b_gate, m_g_q, m_w_uq, m_g_kv, m_w_ukv, m_lb_logits, m_g_hgrn, m_w_branch_a, m_w_branch_b, m_w_out, m_g_post, v_g_pre, v_w_in, v_b_gate, v_g_q, v_w_uq, v_g_kv, v_w_ukv, v_lb_logits, v_g_hgrn, v_w_branch_a, v_w_branch_b, v_w_out, v_g_post):
    rows3 = lambda a: jnp.transpose(a, (2, 0, 1))
    w = dict(w_in=rows3(w_in), w_uq=w_uq[0], w_ukv=w_ukv[0], w_branch_a=w_branch_a[0], w_branch_b=w_branch_b[0],
             w_out=w_out[0], g_pre=g_pre, b_gate=b_gate, g_q=g_q, g_kv=g_kv, lb_logits=lb_logits, g_hgrn=g_hgrn,
             g_post=g_post)
    mom = dict(w_in=rows3(m_w_in), w_uq=m_w_uq[0], w_ukv=m_w_ukv[0], w_branch_a=m_w_branch_a[0],
               w_branch_b=m_w_branch_b[0], w_out=m_w_out[0], g_pre=m_g_pre, b_gate=m_b_gate, g_q=m_g_q, g_kv=m_g_kv,
               lb_logits=m_lb_logits, g_hgrn=m_g_hgrn, g_post=m_g_post)
    var = dict(w_in=rows3(v_w_in), w_uq=v_w_uq[0], w_ukv=v_w_ukv[0], w_branch_a=v_w_branch_a[0],
               w_branch_b=v_w_branch_b[0], w_out=v_w_out[0], g_pre=v_g_pre, b_gate=v_b_gate, g_q=v_g_q, g_kv=v_g_kv,
               lb_logits=v_lb_logits, g_hgrn=v_g_hgrn, g_post=v_g_post)

    w_blk = w["w_in"].reshape(W_IN_SHARD, D_MODEL).astype(BF16)
    shards = [w[n].astype(BF16) for n in MATS[:2]] + [w[n] for n in MATS[2:]]
    dx, recv_in, recv, g_sum = _step(x[0], loss_target[0], w_blk, shards,
                                     g_pre, b_gate, g_q, g_kv, lb_logits, g_hgrn, g_post)

    g_in, d_in, m_in, v_in = _sum_adamw_w_in(recv_in, w["w_in"], mom["w_in"], var["w_in"])
    res = _sum_adamw_whole([recv[n] for n in MATS], *([t[n] for n in MATS] for t in (w, mom, var)))
    total, *vec = _vectors_adamw(g_sum, *([t[n] for n in SMALL] for t in (w, mom, var)))

    outs = []
    for mats, vecs, big in zip(res, vec, (g_in, d_in, m_in, v_in)):
        t = {**{n: a[None] for n, a in zip(MATS, mats)}, **dict(zip(SMALL, vecs)),
             "w_in": jnp.transpose(big, (1, 2, 0))}
        outs += [t[n] for n in ORDER]
    return (total.reshape(()), dx[None], *outs)
```

```python
import math

import jax
import jax.numpy as jnp
import numpy as np
from jax import lax
from jax.experimental import pallas as pl
from jax.experimental.pallas import tpu as pltpu

F32, BF16 = jnp.float32, jnp.bfloat16

D_MODEL = 1024
EPS = 1e-6
HEADS = 8
NOPE, ROPE, VDIM = 64, 32, 64
QK = NOPE + ROPE
Q_LORA, KV_LORA = 768, 256
ROPE_THETA = 10000.0
ATT_CHUNK_SHIFT = 6
HG_BLOCK = 32
HG_WIDTH = 512
D_IN = 5664
D_IN_PAD = 5760
W_IN_SHARD = D_IN // 8
N_DEV = 8
LANE = 128

ADAM_LR, ADAM_B1, ADAM_B2, ADAM_EPS, ADAM_WD, ADAM_STEP = 0.001, 0.9, 0.999, 1e-08, 0.01, 10

W_IN_SEGMENTS = ((3616, 5664, 0), (1056, 1568, 2048), (3104, 3616, 2560), (1568, 3104, 3072),
                 (0, 1024, 4608), (1024, 1056, 5696))

NT = (((1,), (1,)), ((), ()))
TN = (((0,), (0,)), ((), ()))
MESH_ID = pl.DeviceIdType.MESH


def _w_in_pieces():
    out = []
    for lo, hi, dst in W_IN_SEGMENTS:
        c = lo
        while c < hi:
            p = c // W_IN_SHARD
            e = min(hi, (p + 1) * W_IN_SHARD)
            out.append((p, c - p * W_IN_SHARD, e - p * W_IN_SHARD, dst + c - lo))
            c = e
    return out


def _params(sem, vmem_mb=48):
    return pltpu.CompilerParams(dimension_semantics=sem, vmem_limit_bytes=vmem_mb * 2**20)


def _dot(a, b):
    return jnp.dot(a, b, preferred_element_type=F32)


def _dotg(a, b, dims):
    return lax.dot_general(a, b, dims, preferred_element_type=F32)


def _split2(x):
    hi = x.astype(BF16)
    return hi, (x - hi.astype(F32)).astype(BF16)


def _sel_left(m01, x):
    hi, lo = _split2(x)
    return _dot(m01, hi) + _dot(m01, lo)


def _sel_right(x, m01):
    hi, lo = _split2(x)
    return _dot(hi, m01) + _dot(lo, m01)


def _hi_lo(x):
    hi = x.astype(BF16).astype(F32)
    return hi, x - hi


def _sigmoid(x):
    return 0.5 * jnp.tanh(0.5 * x) + 0.5


def _rope(x, c, s1, s2):
    return x * c + pltpu.roll(x, 112, 1) * s1 + pltpu.roll(x, 16, 1) * s2


def _unrope(d, c, s1, s2):
    return d * c + pltpu.roll(d * s1, 16, 1) + pltpu.roll(d * s2, 112, 1)


def _my_place():
    return lax.axis_index("x"), lax.axis_index("y"), lax.axis_index("c")


def _flip(k, x, y, c):
    fx, fy, fc = (k + 1) >> 2 & 1, (k + 1) >> 1 & 1, (k + 1) & 1
    return (1 - x if fx else x), (1 - y if fy else y), (1 - c if fc else c)


def _to_all_copies(s_refs, r_refs, sems, spread):
    send_sems, recv_sems, local_sems = sems
    x, y, c = _my_place()
    me = 4 * x + 2 * y + c
    src = (lambda a, p: s_refs[a]) if spread else (lambda a, p: s_refs[a].at[p])
    local = [pltpu.make_async_copy(src(a, me), r_refs[a].at[me], local_sems.at[a]) for a in range(len(s_refs))]
    remote = []
    for k in range(N_DEV - 1):
        px, py, pc = _flip(k, x, y, c)
        for a in range(len(s_refs)):
            remote.append(pltpu.make_async_remote_copy(
                src_ref=src(a, 4 * px + 2 * py + pc), dst_ref=r_refs[a].at[me],
                send_sem=send_sems.at[7 * a + k], recv_sem=recv_sems.at[7 * a + k],
                device_id=(px, py, pc), device_id_type=MESH_ID))
    return local, remote


def _to_chips_copies(s_refs, r_refs, sems):
    send_sems, recv_sems, local_sems = sems
    x, y, c = _my_place()
    me = 2 * x + y
    local = [pltpu.make_async_copy(s_refs[a].at[me], r_refs[a].at[me], local_sems.at[a]) for a in range(len(s_refs))]
    remote = []
    for k in range(3):
        px = 1 - x if (k + 1) >> 1 & 1 else x
        py = 1 - y if (k + 1) & 1 else y
        for a in range(len(s_refs)):
            remote.append(pltpu.make_async_remote_copy(
                src_ref=s_refs[a].at[2 * px + py], dst_ref=r_refs[a].at[me],
                send_sem=send_sems.at[3 * a + k], recv_sem=recv_sems.at[3 * a + k],
                device_id=(px, py, c), device_id_type=MESH_ID))
    return local, remote


def _start_all(local, remote):
    for cp in local + remote:
        cp.start()


def _wait_all(local, remote):
    for cp in remote:
        cp.wait_recv()
    for cp in remote:
        cp.wait_send()
    for cp in local:
        cp.wait()


def _copy_sems(n, peers):
    return [pltpu.SemaphoreType.DMA((peers * n,)), pltpu.SemaphoreType.DMA((peers * n,)),
            pltpu.SemaphoreType.DMA((n,))]


ANY = pl.BlockSpec(memory_space=pl.ANY)


def _dw_in_slots(ht, dproj):
    m, k = ht.shape
    n = dproj.shape[1]
    tn, tk = 1920, 2048
    nj, nk = n // tn, k // tk
    by_tile = [[] for _ in range(nj)]
    for p, lo, hi, dst in _w_in_pieces():
        while lo < hi:
            j = dst // tn
            cnt = min(hi - lo, (j + 1) * tn - dst)
            by_tile[j].append((p, lo, lo + cnt, dst - j * tn))
            lo, dst = lo + cnt, dst + cnt

    def body(a_ref, b_ref, s_ref, acc_ref):
        j, l = pl.program_id(0), pl.program_id(1)

        @pl.when(l == 0)
        def _():
            acc_ref[...] = jnp.zeros_like(acc_ref)

        acc_ref[...] += _dot(a_ref[...], b_ref[...])

        @pl.when(l == nk - 1)
        def _():
            at = acc_ref[...].T
            for jj in range(nj):
                @pl.when(j == jj)
                def _(jj=jj):
                    for p, lo, hi, d in by_tile[jj]:
                        s_ref[p, lo:hi, :] = at[d:d + hi - lo, :].astype(BF16)

    return pl.pallas_call(
        body,
        grid=(nj, nk),
        in_specs=[pl.BlockSpec((m, tk), lambda j, l: (0, l)), pl.BlockSpec((tk, tn), lambda j, l: (l, j))],
        out_specs=pl.BlockSpec((N_DEV, W_IN_SHARD, m), lambda j, l: (0, 0, 0), pipeline_mode=pl.Buffered(1)),
        out_shape=jax.ShapeDtypeStruct((N_DEV, W_IN_SHARD, m), BF16),
        scratch_shapes=[pltpu.VMEM((m, tn), F32)],
        compiler_params=_params(("arbitrary", "arbitrary"), 56),
        name="dw_in",
    )(ht, dproj)


GP_TN = 256
GP_COLS = 5888
GP_NT = GP_COLS // GP_TN


def _gp_tile_pieces():
    tiles = [[] for _ in range(GP_NT)]
    for p, lo, hi, dst in _w_in_pieces():
        while lo < hi:
            t = dst // GP_TN
            n = min(hi - lo, (t + 1) * GP_TN - dst)
            tiles[t].append((p, lo, lo + n, dst - t * GP_TN))
            lo, dst = lo + n, dst + n
    return tiles


def _gp_tables():
    pieces = _gp_tile_pieces()
    rank_of = {None: 0, 0: 1, 1: 2, 2: 2, 4: 3, 5: 3, 3: 4, 6: 5}
    order = np.zeros((N_DEV, GP_NT), np.int32)
    waits = np.zeros((N_DEV, GP_NT), np.int32)
    for me in range(N_DEV):
        x, y, c = me >> 2 & 1, me >> 1 & 1, me & 1
        chips = [(1 - x, y), (x, 1 - y), (1 - x, 1 - y)]

        def sem_of(p):
            px, py, pc = p >> 2 & 1, p >> 1 & 1, p & 1
            if (px, py) == (x, y):
                return None if pc == c else 0
            j = chips.index((px, py))
            return 1 + j if pc == c else 4 + j

        needs = [sorted({sem_of(p) for p, _, _, _ in tile} - {None}) for tile in pieces]
        ranks = [max([rank_of[k] for k in ks], default=0) for ks in needs]
        seq = sorted(range(GP_NT), key=lambda t: (ranks[t], t))
        seen = set()
        for step, t in enumerate(seq):
            order[me, step] = t
            new = [k for k in needs[t] if k not in seen]
            for k in new:
                waits[me, step] |= 1 << k
            seen.update(new)
        assert seen == set(range(7)), (me, seen)
    return order, waits


def _gather_proj(x, g_pre, w_blk, shards):
    s = x.shape[0]
    tx = 512
    ns = len(shards)
    tile_pieces = _gp_tile_pieces()
    order_np, waits_np = _gp_tables()
    xq, yq, cq = _my_place()
    me_out = 4 * xq + 2 * yq + cq
    order = lax.dynamic_index_in_dim(jnp.asarray(order_np), me_out, 0, keepdims=False)
    waits = lax.dynamic_index_in_dim(jnp.asarray(waits_np), me_out, 0, keepdims=False)

    def body(order_ref, waits_ref, x_hbm, g_ref, wblk_hbm, *rest):
        shard_refs, (proj_ref, wt_ref, ht_hbm), got_refs = rest[:ns], rest[ns:ns + 3], rest[ns + 3:2 * ns + 3]
        recv, h_ref, wtile, xbuf, htbuf = rest[2 * ns + 3:2 * ns + 8]
        send_sems, recv_sems, misc_sems = rest[2 * ns + 8:2 * ns + 11]
        sems = rest[2 * ns + 11:]
        t = pl.program_id(0)
        x_, y_, c = _my_place()
        sibling = (x_, y_, 1 - c)
        chips = [(1 - x_, y_), (x_, 1 - y_), (1 - x_, 1 - y_)]
        idx = lambda px, py, pc: 4 * px + 2 * py + pc
        me = idx(x_, y_, c)

        def copy(k, slot, to, src=None):
            return pltpu.make_async_remote_copy(
                src_ref=recv.at[slot] if src is None else src, dst_ref=recv.at[slot],
                send_sem=send_sems.at[k], recv_sem=recv_sems.at[k], device_id=to, device_id_type=MESH_ID)

        mine = pltpu.make_async_copy(wblk_hbm, recv.at[me], misc_sems.at[0])
        first = [copy(0, me, sibling, src=wblk_hbm)] + [copy(1 + j, me, (*chips[j], c), src=wblk_hbm) for j in range(2)]
        passed = [copy(4 + j, idx(*ch, c), sibling) for j, ch in enumerate(chips)]
        onward = [copy(3, idx(*chips[0], c), (*chips[1], c)), copy(3, idx(*chips[1], c), (*chips[0], c))]
        arrivals = ([copy(0, idx(x_, y_, 1 - c), sibling)] + [copy(1 + j, idx(*ch, c), sibling) for j, ch in enumerate(chips)]
                    + [copy(4 + j, idx(*ch, 1 - c), sibling) for j, ch in enumerate(chips)])

        @pl.when(t == 0)
        def _():
            mine.start()
            for cp in first:
                cp.start()
            _start_all(*_to_all_copies(shard_refs, got_refs, sems, True))

            def load(i):
                return pltpu.make_async_copy(x_hbm.at[pl.ds(i * tx, tx), :], xbuf.at[i & 1], misc_sems.at[1 + (i & 1)])

            def store(i):
                return pltpu.make_async_copy(htbuf.at[i & 1], ht_hbm.at[:, pl.ds(i * tx, tx)], misc_sems.at[3 + (i & 1)])

            load(0).start()
            for i in range(s // tx):
                if i + 1 < s // tx:
                    load(i + 1).start()
                load(i).wait()
                xv = xbuf[i & 1]
                r = lax.rsqrt(jnp.mean(xv * xv, axis=-1, keepdims=True) + EPS)
                h = (xv * r * g_ref[...]).astype(BF16)
                h_ref[i * tx:(i + 1) * tx, :] = h
                if i >= 2:
                    store(i - 2).wait()
                htbuf[i & 1] = h.T
                store(i).start()
            for i in range(max(s // tx - 2, 0), s // tx):
                store(i).wait()
            mine.wait()

        w = waits_ref[t]
        for k in range(7):
            @pl.when((w >> k) & 1 == 1)
            def _(k=k):
                arrivals[k].wait_recv()
                if 1 <= k <= 3:
                    passed[k - 1].start()
                if 1 <= k <= 2:
                    @pl.when(c == k - 1)
                    def _():
                        onward[k - 1].start()

        tile = order_ref[t]
        for tt in range(GP_NT):
            @pl.when(tile == tt)
            def _(tt=tt):
                covered = sorted((d, d + hi - lo) for _, lo, hi, d in tile_pieces[tt])
                at = 0
                for lo_z, hi_z in covered + [(GP_TN, GP_TN)]:
                    if lo_z > at:
                        wtile[at:lo_z, :] = jnp.zeros((lo_z - at, D_MODEL), BF16)
                    at = max(at, hi_z)
                for p, lo, hi, d in tile_pieces[tt]:
                    wtile[d:d + hi - lo, :] = recv[p, lo:hi, :]

        wt = wtile[...]
        wt_ref[...] = wt
        proj_ref[...] = _dotg(h_ref[...], wt, NT)

        @pl.when(t == GP_NT - 1)
        def _():
            for cp in first + passed + onward[:1]:
                cp.wait_send()
            _wait_all(*_to_all_copies(shard_refs, got_refs, sems, True))

    grid_spec = pltpu.PrefetchScalarGridSpec(
        num_scalar_prefetch=2,
        grid=(GP_NT,),
        in_specs=[ANY, pl.BlockSpec((1, D_MODEL), lambda t, o, w: (0, 0)), ANY] + [ANY] * ns,
        out_specs=[pl.BlockSpec((s, GP_TN), lambda t, o, w: (0, o[t])),
                   pl.BlockSpec((GP_TN, D_MODEL), lambda t, o, w: (o[t], 0)), ANY] + [ANY] * ns,
        scratch_shapes=[pltpu.VMEM((N_DEV, W_IN_SHARD, D_MODEL), BF16), pltpu.VMEM((s, D_MODEL), BF16),
                        pltpu.VMEM((GP_TN, D_MODEL), BF16), pltpu.VMEM((2, tx, D_MODEL), F32),
                        pltpu.VMEM((2, D_MODEL, tx), BF16),
                        pltpu.SemaphoreType.DMA((7,)), pltpu.SemaphoreType.DMA((7,)), pltpu.SemaphoreType.DMA((5,))]
        + _copy_sems(ns, 7),
    )
    return pl.pallas_call(
        body,
        grid_spec=grid_spec,
        out_shape=[jax.ShapeDtypeStruct((s, GP_COLS), F32),jax.ShapeDtypeStruct((GP_COLS, D_MODEL), BF16),
                   jax.ShapeDtypeStruct((D_MODEL, s), BF16)]
        + [jax.ShapeDtypeStruct((N_DEV,) + b.shape, b.dtype) for b in shards],
        compiler_params=_params(("arbitrary",), 56),
        name="gather_proj",
    )(order, waits, x, g_pre, w_blk, *shards)


def _mla_prep(proj, g_q, g_kv, w_uq_p, w_kv_p, rc, rs1, rs2, casts):
    s = proj.shape[0]
    tm = 512
    nc = len(casts)
    scale = 1.0 / math.sqrt(QK)

    def body(cq_ref, ckv_ref, kpe_ref, gq_ref, gkv_ref, wuq_ref, wkv_ref, c_ref, s1_ref, s2_ref, *rest):
        cast_in, (qr_ref, kr_ref, v_ref, cqt_ref, ckvt_ref), cast_out = rest[:nc], rest[nc:nc + 5], rest[nc + 5:]

        @pl.when(pl.program_id(0) == 0)
        def _():
            for src, dst in zip(cast_in, cast_out):
                dst[...] = src[...].astype(BF16)

        cq = cq_ref[...]
        r = lax.rsqrt(jnp.mean(cq * cq, axis=-1, keepdims=True) + EPS)
        cqn = (cq * r * gq_ref[...]).astype(BF16)
        cqt_ref[...] = cqn.T
        q = _dot(cqn, wuq_ref[...])
        ckv = ckv_ref[...]
        r = lax.rsqrt(jnp.mean(ckv * ckv, axis=-1, keepdims=True) + EPS)
        ckvn = (ckv * r * gkv_ref[...]).astype(BF16)
        ckvt_ref[...] = ckvn.T
        kv = _dot(ckvn, wkv_ref[...])
        c, s1, s2 = c_ref[...], s1_ref[...], s2_ref[...]
        lane = lax.broadcasted_iota(jnp.int32, (tm, LANE), 1)
        kpe = _rope(kpe_ref[...], c, s1, s2) + jnp.where((lane == QK) | (lane == QK + 1), 1.0, 0.0)
        vone = jnp.where((lane == VDIM) | (lane == VDIM + 1), 1.0, 0.0)
        for h in range(HEADS):
            sl = slice(LANE * h, LANE * (h + 1))
            qr_ref[:, sl] = (_rope(q[:, sl], c, s1, s2) * scale).astype(BF16)
            kr_ref[:, sl] = (kv[:, sl] + kpe).astype(BF16)
            v_ref[:, sl] = (kv[:, HEADS * LANE + LANE * h:HEADS * LANE + LANE * (h + 1)] + vone).astype(BF16)

    row = lambda w, j: pl.BlockSpec((tm, w), lambda i: (i, j))
    col = lambda w: pl.BlockSpec((w, tm), lambda i: (0, i))
    full = lambda a: pl.BlockSpec(a.shape, lambda i: (0, 0))
    return pl.pallas_call(
        body,
        grid=(s // tm,),
        in_specs=[row(768, 6), row(256, 21), row(128, 44), full(g_q), full(g_kv), full(w_uq_p), full(w_kv_p),
                  row(128, 0), row(128, 0), row(128, 0)] + [full(a) for a in casts],
        out_specs=[row(1024, 0), row(1024, 0), row(1024, 0), col(768), col(256)] + [full(a) for a in casts],
        out_shape=[jax.ShapeDtypeStruct((s, 1024), BF16), jax.ShapeDtypeStruct((s, 1024), BF16),
                   jax.ShapeDtypeStruct((s, 1024), BF16), jax.ShapeDtypeStruct((768, s), BF16),
                   jax.ShapeDtypeStruct((256, s), BF16)] + [jax.ShapeDtypeStruct(a.shape, BF16) for a in casts],
        compiler_params=_params(("arbitrary",)),
        name="mla_prep",
    )(proj, proj, proj, g_q, g_kv, w_uq_p, w_kv_p, rc, rs1, rs2, *casts)


ATT_T = 512
ATT_FWD_HEADS = 4


def _chunk_mask(transposed):
    r = lax.broadcasted_iota(jnp.int32, (ATT_T, ATT_T), 0) >> ATT_CHUNK_SHIFT
    c = lax.broadcasted_iota(jnp.int32, (ATT_T, ATT_T), 1) >> ATT_CHUNK_SHIFT
    return (r <= c) if transposed else (c <= r)


def _attn_fwd(qr, kr, vp, shards):
    s = qr.shape[0]
    t = ATT_T
    g = ATT_FWD_HEADS
    ns = len(shards)

    def body(q_ref, k_ref, v_ref, *rest):
        shard_refs, (o_ref, qa_ref), got_refs = rest[:ns], rest[ns:ns + 2], rest[ns + 2:2 * ns + 2]
        sc_ref, sems = rest[2 * ns + 2], rest[2 * ns + 3:]
        qi = pl.program_id(1)

        @pl.when((pl.program_id(0) == 0) & (qi == 0))
        def _():
            _start_all(*_to_all_copies(shard_refs, got_refs, sems, True))
        lane = lax.broadcasted_iota(jnp.int32, (t, LANE), 1)
        sls = [slice(LANE * a, LANE * (a + 1)) for a in range(g)]
        qs = [q_ref[:, sl] for sl in sls]

        def scores(j):
            rows = pl.ds(pl.multiple_of(j * t, t), t)
            for a in range(g):
                sc_ref[j & 1, a] = _dotg(qs[a], k_ref[rows, sls[a]], NT)

        def step(j, carry, masked):
            rows = pl.ds(pl.multiple_of(j * t, t), t)
            out = []
            for a in range(g):
                m, acc = carry[a]
                sc = sc_ref[j & 1, a]
                if masked:
                    sc = jnp.where(_chunk_mask(False), sc, -1e30)
                m_new = jnp.maximum(m, jnp.max(sc, axis=-1, keepdims=True))
                p = jnp.exp(sc - m_new).astype(BF16)
                acc = jnp.exp(m - m_new) * acc + _dot(p, v_ref[rows, sls[a]])
                out.append((m_new, acc))
            return tuple(out)

        def loop(j, carry):
            carry = step(j, carry, False)
            scores(j + 1)
            return carry

        init = tuple((jnp.full((t, 1), -1e30, F32), jnp.zeros((t, LANE), F32)) for _ in range(g))
        scores(0)
        carry = lax.fori_loop(0, qi, loop, init)
        carry = step(qi, carry, True)
        outs = []
        for a in range(g):
            m, acc = carry[a]
            l = acc[:, VDIM:VDIM + 1]
            outs.append(acc / l)
            hi, lo_part = _hi_lo(-(m + jnp.log(l)))
            qa = jnp.where(lane == QK, hi, jnp.where(lane == QK + 1, lo_part, qs[a].astype(F32)))
            qa_ref[:, sls[a]] = qa.astype(BF16)
        for p in range(g // 2):
            o_ref[:, LANE * p:LANE * (p + 1)] = jnp.where(lane < VDIM, outs[2 * p], pltpu.roll(outs[2 * p + 1], VDIM, 1))

        @pl.when((pl.program_id(0) == HEADS // g - 1) & (qi == s // t - 1))
        def _():
            _wait_all(*_to_all_copies(shard_refs, got_refs, sems, True))

    return pl.pallas_call(
        body,
        grid=(HEADS // g, s // t),
        in_specs=[
            pl.BlockSpec((t, g * LANE), lambda h, i: (i, h)),
            pl.BlockSpec((s, g * LANE), lambda h, i: (0, h)),
            pl.BlockSpec((s, g * LANE), lambda h, i: (0, h)),
        ] + [ANY] * ns,
        out_specs=[
            pl.BlockSpec((t, g * VDIM), lambda h, i: (i, h)),
            pl.BlockSpec((t, g * LANE), lambda h, i: (i, h)),
        ] + [ANY] * ns,
        out_shape=[jax.ShapeDtypeStruct((s, 512), F32), jax.ShapeDtypeStruct((s, 1024), BF16)]
        + [jax.ShapeDtypeStruct((N_DEV,) + b.shape, b.dtype) for b in shards],
        scratch_shapes=[pltpu.VMEM((2, g, t, t), F32)] + _copy_sems(ns, 7),
        compiler_params=_params(("arbitrary", "arbitrary")),
        name="attn_fwd",
    )(qr, kr, vp, *shards)


def _attn_bwd(qa, kr, vp, dop, sends):
    s = qa.shape[0]
    t = ATT_T
    nq = s // t
    ns = len(sends)

    def body(q_ref, k_ref, v_ref, do_ref, *rest):
        send_refs, (dq_out, dk_out, dv_out) = rest[:ns], rest[ns:ns + 3]
        recv_refs = rest[ns + 3:2 * ns + 3]
        (dq_ref, dk_ref, dv_ref), sems = rest[2 * ns + 3:2 * ns + 6], rest[2 * ns + 6:]
        j = pl.program_id(1)
        sls = [slice(LANE * a, LANE * (a + 1)) for a in range(2)]

        @pl.when((pl.program_id(0) == 0) & (j == 0))
        def _():
            _start_all(*_to_all_copies(send_refs, recv_refs, sems, False))

        @pl.when(j == 0)
        def _():
            dq_ref[...] = jnp.zeros_like(dq_ref)

        dk_ref[...] = jnp.zeros_like(dk_ref)
        dv_ref[...] = jnp.zeros_like(dv_ref)
        ks = [k_ref[:, sl] for sl in sls]
        vs = [v_ref[:, sl] for sl in sls]

        def part(i, k_lo, k_n, q_lo, q_n, masked):
            rows = pl.ds(pl.multiple_of(i * t + q_lo, 256), q_n)
            keys = slice(k_lo, k_lo + k_n)
            for a in range(2):
                q = q_ref[rows, sls[a]]
                do = do_ref[rows, sls[a]]
                sc = _dotg(ks[a][keys], q, NT)
                if masked:
                    kc = lax.broadcasted_iota(jnp.int32, (k_n, q_n), 0) >> ATT_CHUNK_SHIFT
                    qc = lax.broadcasted_iota(jnp.int32, (k_n, q_n), 1) >> ATT_CHUNK_SHIFT
                    sc = jnp.where(kc <= qc, sc, -1e30)
                p = jnp.exp(sc)
                ds = (p * _dotg(vs[a][keys], do, NT)).astype(BF16)
                dv_ref[keys, sls[a]] += _dot(p.astype(BF16), do)
                dk_ref[keys, sls[a]] += _dot(ds, q)
                dq_ref[rows, sls[a]] += _dotg(ds, ks[a][keys], TN)

        half = t // 2
        part(j, 0, half, 0, t, True)
        part(j, half, half, half, half, True)

        def loop(i, c):
            part(i, 0, t, 0, t, False)
            return c

        lax.fori_loop(j + 1, nq, loop, 0)
        dk_out[...] = dk_ref[...].astype(BF16)
        dv_out[...] = dv_ref[...].astype(BF16)

        @pl.when(j == nq - 1)
        def _():
            dq_out[...] = dq_ref[...].astype(BF16)

        @pl.when((pl.program_id(0) == HEADS // 2 - 1) & (j == nq - 1))
        def _():
            _wait_all(*_to_all_copies(send_refs, recv_refs, sems, False))

    blk = pl.BlockSpec((t, 2 * LANE), lambda h, j: (j, h))
    whole = pl.BlockSpec((s, 2 * LANE), lambda h, j: (0, h))
    out = jax.ShapeDtypeStruct((s, 1024), BF16)
    return pl.pallas_call(
        body,
        grid=(HEADS // 2, nq),
        in_specs=[whole, blk, blk, whole] + [ANY] * ns,
        out_specs=[whole, blk, blk] + [ANY] * ns,
        out_shape=[out, out, out] + [jax.ShapeDtypeStruct(a.shape, a.dtype) for a in sends],
        scratch_shapes=[pltpu.VMEM((s, 2 * LANE), F32), pltpu.VMEM((t, 2 * LANE), F32),
                        pltpu.VMEM((t, 2 * LANE), F32)] + _copy_sems(ns, 7),
        compiler_params=_params(("arbitrary", "arbitrary")),
        name="attn_bwd",
    )(qa, kr, vp, dop, *sends)


HG_T = 256
HG_NC = HG_T // HG_BLOCK
HG_G = 4
GW = 64 * HG_G


def _hg_consts():
    r = jnp.arange(HG_T)[:, None]
    c = jnp.arange(HG_T)[None, :]
    same = (r // HG_BLOCK) == (c // HG_BLOCK)
    mcum = (same & (c <= r)).astype(BF16)
    mrev = (same & (c >= r)).astype(BF16)
    msum = same.astype(BF16)
    a = jnp.arange(GW) // 64
    bd = (a[:, None] == a[None, :]).astype(F32)
    return mcum, mrev, msum, bd


def _stack_heads(xg, head):
    return jnp.concatenate([jnp.where(head == h, xg, 0.0) for h in range(HG_G)], axis=0)


def _unstack_heads(r, head, t):
    out = r[(HG_G - 1) * t:]
    for h in range(HG_G - 2, -1, -1):
        out = jnp.where(head == h, r[h * t:(h + 1) * t], out)
    return out


def _compact_state(st):
    out = st[:64]
    for h in range(1, HG_G):
        out = out + st[64 * h:64 * (h + 1)]
    return out


def _expand_state(cs, head64):
    return jnp.concatenate([jnp.where(head64 == h, cs, 0.0) for h in range(HG_G)], axis=0)


def _hg_pre(hq, hf, lbl, mcum, msum):
    lb = _sigmoid(lbl[0:1, :] - lbl[1:2, :])
    sig = _sigmoid(hf)
    f = lb + (1.0 - lb) * sig
    lf = jnp.log(f)
    b = _sel_left(mcum, lf)
    big_l = _sel_left(msum, lf)
    k = 1.0 - f
    qd = hq * jnp.exp(b)
    ki = k * jnp.exp(-b)
    ke = k * jnp.exp(big_l - b)
    return lb, sig, f, b, big_l, qd, ki, ke


def _hgrn_fwd(proj, lbl):
    s = proj.shape[0]
    t = HG_T
    mcum, _, msum, bd = _hg_consts()

    def body(hq_ref, hf_ref, hi_ref, lbl_ref, mcum_ref, msum_ref, bd_ref, o_ref, sp_ref, st_ref):
        @pl.when(pl.program_id(0) == 0)
        def _():
            st_ref[...] = jnp.zeros_like(st_ref)

        mc = mcum_ref[...]
        _, _, _, _, big_l, qd, ki, ke = _hg_pre(hq_ref[...], hf_ref[...], lbl_ref[...], mc, msum_ref[...])
        el = jnp.exp(big_l)
        hi = hi_ref[...]
        head = lax.broadcasted_iota(jnp.int32, (t, GW), 1) >> 6
        mask = jnp.concatenate([mc] * HG_G, axis=0) > 0.5
        for p in range(HEADS // HG_G):
            sl = slice(GW * p, GW * (p + 1))
            vp = hi[:, sl].astype(BF16)
            qs = _stack_heads(qd[:, sl], head).astype(BF16)
            a = jnp.where(mask, _dotg(qs, ki[:, sl].astype(BF16), NT), 0.0)
            o_intra = _unstack_heads(_dot(a.astype(BF16), vp), head, t)
            qb = qd[:, sl].astype(BF16)
            kb = ke[:, sl].astype(BF16)
            st = st_ref[p]
            for c in range(HG_NC):
                rows = slice(HG_BLOCK * c, HG_BLOCK * (c + 1))
                sp_ref[c, :, sl] = _compact_state(st)
                o_ref[rows, sl] = o_intra[rows] + _dotg(qb[rows], st.astype(BF16), NT)
                u = _dotg(vp[rows], kb[rows], TN) * bd_ref[...]
                st = st * el[HG_BLOCK * c:HG_BLOCK * c + 1, sl] + u
            st_ref[p] = st

    row = lambda j: pl.BlockSpec((t, HG_WIDTH), lambda i: (i, j))
    full = lambda a: pl.BlockSpec(a.shape, lambda i: (0, 0))
    return pl.pallas_call(
        body,
        grid=(s // t,),
        in_specs=[row(6), row(7), row(8), full(lbl), full(mcum), full(msum), full(bd)],
        out_specs=[row(0), pl.BlockSpec((HG_NC, 64, HG_WIDTH), lambda i: (i, 0, 0))],
        out_shape=[jax.ShapeDtypeStruct((s, HG_WIDTH), F32),
                   jax.ShapeDtypeStruct((s // HG_BLOCK, 64, HG_WIDTH), F32)],
        scratch_shapes=[pltpu.VMEM((HEADS // HG_G, GW, GW), F32)],
        compiler_params=_params(("arbitrary",)),
        name="hgrn_fwd",
    )(proj, proj, proj, lbl, mcum, msum, bd)


def _slot_shape(name, r, c):
    return (N_DEV, r, c // N_DEV) if COL_SHARDED[name] else (N_DEV, r // N_DEV, c)


def _emit_slots(name, acc_ref, out_ref):
    r, c = acc_ref.shape
    for p in range(N_DEV):
        if COL_SHARDED[name]:
            out_ref[p] = acc_ref[:, c // N_DEV * p:c // N_DEV * (p + 1)].astype(BF16)
        else:
            out_ref[p] = acc_ref[r // N_DEV * p:r // N_DEV * (p + 1), :].astype(BF16)


def _hgrn_bwd(proj, lbl, do, sprev, dproj, pairs):
    s = proj.shape[0]
    t = HG_T
    nt = s // t
    npair = len(pairs)
    mcum, mrev, msum, bd = _hg_consts()

    def body(hq_ref, hf_ref, hi_ref, lbl_ref, do_ref, sp_ref, mcum_ref, mrev_ref, msum_ref, bd_ref,
             dproj_in, *rest):
        del dproj_in
        pair_refs, (dh_ref, dlbl_ref) = rest[:2 * npair], rest[2 * npair:2 * npair + 2]
        dw_refs, g_ref, acc_refs = rest[2 * npair + 2:3 * npair + 2], rest[3 * npair + 2], rest[3 * npair + 3:]

        @pl.when(pl.program_id(0) == 0)
        def _():
            g_ref[...] = jnp.zeros_like(g_ref)
            dlbl_ref[...] = jnp.zeros_like(dlbl_ref)
            for acc_ref in acc_refs:
                acc_ref[...] = jnp.zeros_like(acc_ref)

        for n, acc_ref in enumerate(acc_refs):
            acc_ref[...] += _dot(pair_refs[2 * n][...], pair_refs[2 * n + 1][...])

        @pl.when(pl.program_id(0) == nt - 1)
        def _():
            for (name, _, _), acc_ref, dw_ref in zip(pairs, acc_refs, dw_refs):
                _emit_slots(name, acc_ref, dw_ref)

        mc = mcum_ref[...]
        lb, sig, f, b, big_l, qd, ki, ke = _hg_pre(hq_ref[...], hf_ref[...], lbl_ref[...], mc, msum_ref[...])
        el = jnp.exp(big_l)
        hi = hi_ref[...]
        dov = do_ref[...]
        head = lax.broadcasted_iota(jnp.int32, (t, GW), 1) >> 6
        head64 = lax.broadcasted_iota(jnp.int32, (64, GW), 1) >> 6
        mask = jnp.concatenate([mc] * HG_G, axis=0) > 0.5
        dqd_parts, dke_parts, dv_parts, del_parts, dki_parts = [], [], [], [], []
        for p in range(HEADS // HG_G):
            sl = slice(GW * p, GW * (p + 1))
            vp = hi[:, sl].astype(BF16)
            qs = _stack_heads(qd[:, sl], head).astype(BF16)
            kip = ki[:, sl].astype(BF16)
            dos = _stack_heads(dov[:, sl], head).astype(BF16)
            a = jnp.where(mask, _dotg(qs, kip, NT), 0.0).astype(BF16)
            da = jnp.where(mask, _dotg(dos, vp, NT), 0.0).astype(BF16)
            r = _dot(da, kip)
            dki_parts.append(_dotg(da, qs, TN))
            qb = qd[:, sl].astype(BF16)
            kb = ke[:, sl].astype(BF16)
            dob = dov[:, sl].astype(BF16)
            g = g_ref[p]
            dqd_c, dv_c, dke_c, del_c = [], [], [], []
            for c in range(HG_NC - 1, -1, -1):
                rows = slice(HG_BLOCK * c, HG_BLOCK * (c + 1))
                gb = g.astype(BF16)
                st = _expand_state(sp_ref[c, :, sl], head64)
                dqd_c.append(_dot(dob[rows], st.astype(BF16)))
                dv_c.append(_dotg(kb[rows], gb, NT))
                dke_c.append(_dot(vp[rows], gb))
                del_c.append(jnp.broadcast_to(jnp.sum(g * st, axis=0, keepdims=True), (HG_BLOCK, GW)))
                g = g * el[HG_BLOCK * c:HG_BLOCK * c + 1, sl] + _dotg(dob[rows], qb[rows], TN) * bd_ref[...]
            g_ref[p] = g
            up = lambda parts: jnp.concatenate(parts[::-1], axis=0)
            dqd_parts.append(_unstack_heads(r, head, t) + up(dqd_c))
            dv_parts.append(_dotg(a, dos, TN) + up(dv_c))
            dke_parts.append(up(dke_c))
            del_parts.append(up(del_c))
        wide = lambda parts: jnp.concatenate(parts, axis=1)
        dqd, dke, dki, dvv, del_rows = wide(dqd_parts), wide(dke_parts), wide(dki_parts), wide(dv_parts), wide(del_parts)
        dh_ref[:, :HG_WIDTH] = (dqd * jnp.exp(b)).astype(BF16)
        dh_ref[:, 2 * HG_WIDTH:] = dvv.astype(BF16)
        dke_ke = dke * ke
        db = dqd * qd - dki * ki - dke_ke
        dl_rows = _sel_left(msum_ref[...], dke_ke) + del_rows * el
        is_last = (lax.broadcasted_iota(jnp.int32, (t, HG_WIDTH), 0) & (HG_BLOCK - 1)) == HG_BLOCK - 1
        db = db + jnp.where(is_last, dl_rows, 0.0)
        dlf = _sel_left(mrev_ref[...], db)
        dk = dki * jnp.exp(-b) + dke * jnp.exp(big_l - b)
        df = dlf / f - dk
        dh_ref[:, HG_WIDTH:2 * HG_WIDTH] = (df * (1.0 - lb) * sig * (1.0 - sig)).astype(BF16)
        dlb = jnp.sum(df * (1.0 - sig), axis=0, keepdims=True) * lb * (1.0 - lb)
        dlbl_ref[0:1, :] += dlb
        dlbl_ref[1:2, :] -= dlb

    rrow = lambda j: pl.BlockSpec((t, HG_WIDTH), lambda i: (nt - 1 - i, j))
    full = lambda a: pl.BlockSpec(a.shape, lambda i: (0, 0))
    pair_specs, dw_specs, dw_shapes, accs = [], [], [], []
    for name, at, b in pairs:
        pair_specs += [pl.BlockSpec((at.shape[0], t), lambda i: (0, i)), pl.BlockSpec((t, b.shape[1]), lambda i: (i, 0))]
        shape = _slot_shape(name, at.shape[0], b.shape[1])
        dw_specs.append(pl.BlockSpec(shape, lambda i: (0, 0, 0)))
        dw_shapes.append(jax.ShapeDtypeStruct(shape, BF16))
        accs.append(pltpu.VMEM((at.shape[0], b.shape[1]), F32))
    return pl.pallas_call(
        body,
        grid=(nt,),
        in_specs=[rrow(6), rrow(7), rrow(8), full(lbl), rrow(0),
                  pl.BlockSpec((HG_NC, 64, HG_WIDTH), lambda i: (nt - 1 - i, 0, 0)),
                  full(mcum), full(mrev), full(msum), full(bd), pl.BlockSpec(memory_space=pl.ANY)] + pair_specs,
        out_specs=[pl.BlockSpec((t, 3 * HG_WIDTH), lambda i: (nt - 1 - i, 2)),
                   pl.BlockSpec((2, HG_WIDTH), lambda i: (0, 0))] + dw_specs,
        out_shape=[jax.ShapeDtypeStruct(dproj.shape, BF16), jax.ShapeDtypeStruct((2, HG_WIDTH), F32)] + dw_shapes,
        input_output_aliases={10: 0},
        scratch_shapes=[pltpu.VMEM((HEADS // HG_G, GW, GW), F32)] + accs,
        compiler_params=_params(("arbitrary",)),
        name="hgrn_bwd",
    )(proj, proj, proj, lbl, do, sprev, mcum, mrev, msum, bd, dproj, *[a for pair in pairs for a in pair[1:]])


def _tail(x, tgt, proj, attn, o, w_a, w_b, w_out, w_at, w_bt, w_outt, b_gate, g_post, gh):
    s = x.shape[0]
    tm = 256
    ones64 = (jnp.arange(HG_WIDTH)[:, None] // 64 == jnp.arange(HG_WIDTH)[None, :] // 64).astype(BF16)
    weights = (w_a, w_b, w_out, w_at, w_bt, w_outt)

    def body(x_ref, t_ref, ml_ref, ga_ref, gb_ref, at_ref, o_ref, *rest):
        w_hbm, (bg_ref, gp_ref, gh_ref, ones_ref) = rest[:6], rest[6:10]
        (dout_ref, dpj_ref, dop_ref, do_ref, mt_ref, dy_ref, yat_ref, dya_ref, ybt_ref, dyb_ref,
         loss_ref, dgp_ref, dbg_ref, dgh_ref) = rest[10:24]
        (wa_ref, wb_ref, wo_ref, wat_ref, wbt_ref, wot_ref), w_sem = rest[24:30], rest[30]

        @pl.when(pl.program_id(0) == 0)
        def _():
            loads = [pltpu.make_async_copy(src, dst, w_sem.at[k])
                     for k, (src, dst) in enumerate(zip(w_hbm, rest[24:30]))]
            _start_all(loads, [])
            loss_ref[...] = jnp.zeros_like(loss_ref)
            dgp_ref[...] = jnp.zeros_like(dgp_ref)
            dbg_ref[...] = jnp.zeros_like(dbg_ref)
            dgh_ref[...] = jnp.zeros_like(dgh_ref)
            _wait_all(loads, [])

        ones = ones_ref[...]
        gate_a = ga_ref[...]
        sa = _sigmoid(gate_a)
        silu_a = gate_a * sa
        attn_v = at_ref[...]
        ya_in = attn_v * silu_a
        ov = o_ref[...]
        ro = lax.rsqrt(_sel_right(ov * ov, ones) * (1.0 / 64.0) + EPS)
        ohat = ov * ro
        ghv = gh_ref[...]
        on = ohat * ghv
        gate_b = gb_ref[...]
        sb = _sigmoid(gate_b)
        silu_b = gate_b * sb
        yb_in = on * silu_b
        ya_bf = ya_in.astype(BF16)
        yb_bf = yb_in.astype(BF16)
        yat_ref[...] = ya_bf.T
        ybt_ref[...] = yb_bf.T
        y_a = _dot(ya_bf, wa_ref[...])
        y_b = _dot(yb_bf, wb_ref[...])
        gts = _sigmoid(ml_ref[...] + bg_ref[...])
        g_a = gts[:, :D_MODEL]
        g_b = gts[:, D_MODEL:]
        m_bf = (g_a * y_a + g_b * y_b).astype(BF16)
        mt_ref[...] = m_bf.T
        y = _dot(m_bf, wo_ref[...])
        r1 = lax.rsqrt(jnp.mean(y * y, axis=-1, keepdims=True) + EPS)
        yn = y * r1
        gp = gp_ref[...]
        e = x_ref[...] + yn * gp - t_ref[...]
        loss_ref[...] += jnp.sum(e * e, axis=0, keepdims=True)
        dout = e * (1.0 / D_MODEL)
        dout_ref[...] = dout
        dgp_ref[...] += jnp.sum(dout * yn, axis=0, keepdims=True)
        dyn = dout * gp
        dy = r1 * (dyn - yn * jnp.mean(dyn * yn, axis=-1, keepdims=True))
        dy_bf = dy.astype(BF16)
        dy_ref[...] = dy_bf
        dm = _dot(dy_bf, wot_ref[...])
        dml_a = dm * y_a * g_a * (1.0 - g_a)
        dml_b = dm * y_b * g_b * (1.0 - g_b)
        dpj_ref[:, :D_MODEL] = dml_a.astype(BF16)
        dpj_ref[:, D_MODEL:2 * D_MODEL] = dml_b.astype(BF16)
        dbg_ref[:, :D_MODEL] += jnp.sum(dml_a, axis=0, keepdims=True)
        dbg_ref[:, D_MODEL:] += jnp.sum(dml_b, axis=0, keepdims=True)
        dya_bf = (dm * g_a).astype(BF16)
        dyb_bf = (dm * g_b).astype(BF16)
        dya_ref[...] = dya_bf
        dyb_ref[...] = dyb_bf
        dya_in = _dot(dya_bf, wat_ref[...])
        dyb_in = _dot(dyb_bf, wbt_ref[...])
        dattn = dya_in * silu_a
        delta = _sel_right(dattn * attn_v, ones)
        lane = lax.broadcasted_iota(jnp.int32, (tm, LANE), 1)
        for p in range(HEADS // 2):
            sl = slice(LANE * p, LANE * (p + 1))
            xs = (dattn[:, sl], pltpu.roll(dattn[:, sl], VDIM, 1))
            nds = (-pltpu.roll(delta[:, sl], VDIM, 1), -delta[:, sl])
            for a in range(2):
                hi, lo_part = _hi_lo(nds[a])
                blk = jnp.where(lane < VDIM, xs[a], jnp.where(lane == VDIM, hi, jnp.where(lane == VDIM + 1, lo_part, 0.0)))
                dop_ref[:, LANE * (2 * p + a):LANE * (2 * p + a + 1)] = blk.astype(BF16)
        dpj_ref[:, 2 * D_MODEL:2 * D_MODEL + HG_WIDTH] = (
            dya_in * attn_v * (sa * (1.0 + gate_a * (1.0 - sa)))).astype(BF16)
        don = dyb_in * silu_b
        dpj_ref[:, 2 * D_MODEL + HG_WIDTH:] = (dyb_in * on * (sb * (1.0 + gate_b * (1.0 - sb)))).astype(BF16)
        dgh_ref[...] += jnp.sum(don * ohat, axis=0, keepdims=True)
        dohat = don * ghv
        do_ref[...] = (ro * (dohat - ohat * (_sel_right(dohat * ohat, ones) * (1.0 / 64.0)))).astype(BF16)

    row = lambda w, j: pl.BlockSpec((tm, w), lambda i: (i, j))
    col = lambda w: pl.BlockSpec((w, tm), lambda i: (0, i))
    full = lambda a: pl.BlockSpec(a.shape, lambda i: (0, 0))
    acc = lambda w: pl.BlockSpec((1, w), lambda i: (0, 0))
    sds = lambda w, dt: jax.ShapeDtypeStruct((s, w), dt)
    sdt = lambda w: jax.ShapeDtypeStruct((w, s), BF16)
    return pl.pallas_call(
        body,
        grid=(s // tm,),
        in_specs=[row(1024, 0), row(1024, 0), row(2048, 0), row(512, 4), row(512, 5), row(512, 0), row(512, 0)]
        + [ANY] * 6 + [full(b_gate), full(g_post), full(gh), full(ones64)],
        out_specs=[row(1024, 0), row(3072, 0), row(1024, 0), row(512, 0),
                   col(1024), row(1024, 0), col(512), row(1024, 0), col(512), row(1024, 0),
                   acc(1024), acc(1024), acc(2048), acc(512)],
        out_shape=[sds(1024, F32), sds(D_IN_PAD, BF16), sds(1024, BF16), sds(512, BF16),
                   sdt(1024), sds(1024, BF16), sdt(512), sds(1024, BF16), sdt(512), sds(1024, BF16),
                   jax.ShapeDtypeStruct((1, 1024), F32), jax.ShapeDtypeStruct((1, 1024), F32),
                   jax.ShapeDtypeStruct((1, 2048), F32), jax.ShapeDtypeStruct((1, 512), F32)],
        scratch_shapes=[pltpu.VMEM(a.shape, BF16) for a in weights] + [pltpu.SemaphoreType.DMA((6,))],
        compiler_params=_params(("arbitrary",), 56),
        name="tail",
    )(x, tgt, proj, proj, proj, attn, o, *weights, b_gate, g_post, gh, ones64)


def _mla_bwd(proj, dqr, dkr, dv, g_q, g_kv, w_uq_pt, w_kv_pt, rc, rs1, rs2, cqt, ckvt, dproj):
    assert HEADS == N_DEV
    s = proj.shape[0]
    tm = 512
    scale = 1.0 / math.sqrt(QK)

    def body(cq_ref, ckv_ref, dqr_ref, dkr_ref, dv_ref, gq_ref, gkv_ref, wuqt_ref, wkvt_ref, c_ref, s1_ref, s2_ref,
             cqt_ref, ckvt_ref, dproj_in, dc_ref, dgq_ref, dgkv_ref, uq_slots, ukv_slots,
             dqf_ref, dkvf_ref, dwuq_ref, dwkv_ref):
        del dproj_in

        @pl.when(pl.program_id(0) == 0)
        def _():
            dgq_ref[...] = jnp.zeros_like(dgq_ref)
            dgkv_ref[...] = jnp.zeros_like(dgkv_ref)
            dwuq_ref[...] = jnp.zeros_like(dwuq_ref)
            dwkv_ref[...] = jnp.zeros_like(dwkv_ref)

        c, s1, s2 = c_ref[...], s1_ref[...], s2_ref[...]
        lane = lax.broadcasted_iota(jnp.int32, (tm, LANE), 1)
        ksum = jnp.zeros((tm, LANE), F32)
        for h in range(HEADS):
            sl = slice(LANE * h, LANE * (h + 1))
            dqf_ref[:, sl] = (_unrope(dqr_ref[:, sl], c, s1, s2) * scale).astype(BF16)
            dkh = dkr_ref[:, sl]
            ksum = ksum + dkh
            dkvf_ref[:, sl] = jnp.where(lane < NOPE, dkh, 0.0).astype(BF16)
            dkvf_ref[:, HEADS * LANE + LANE * h:HEADS * LANE + LANE * (h + 1)] = jnp.where(
                lane < VDIM, dv_ref[:, sl], 0.0).astype(BF16)
        dkpe = _unrope(ksum, c, s1, s2)
        dc_ref[:, Q_LORA + KV_LORA:] = jnp.where((lane >= NOPE) & (lane < QK), dkpe, 0.0).astype(BF16)
        dqf, dkvf = dqf_ref[...], dkvf_ref[...]
        dwuq_ref[...] += _dot(cqt_ref[...], dqf)
        dwkv_ref[...] += _dot(ckvt_ref[...], dkvf)
        dcqn = _dot(dqf, wuqt_ref[...])
        dckvn = _dot(dkvf, wkvt_ref[...])
        for x_ref, g_ref, dn, cols, dg_ref in ((cq_ref, gq_ref, dcqn, slice(0, Q_LORA), dgq_ref),
                                               (ckv_ref, gkv_ref, dckvn, slice(Q_LORA, Q_LORA + KV_LORA), dgkv_ref)):
            xv = x_ref[...]
            r = lax.rsqrt(jnp.mean(xv * xv, axis=-1, keepdims=True) + EPS)
            xh = xv * r
            dg_ref[...] += jnp.sum(dn * xh, axis=0, keepdims=True)
            dh = dn * g_ref[...]
            dc_ref[:, cols] = (r * (dh - xh * jnp.mean(dh * xh, axis=-1, keepdims=True))).astype(BF16)

        @pl.when(pl.program_id(0) == s // tm - 1)
        def _():
            ur = Q_LORA // N_DEV
            for p in range(N_DEV):
                uq_slots[p] = jnp.concatenate(
                    [dwuq_ref[ur * p:ur * (p + 1), LANE * h:LANE * h + QK] for h in range(HEADS)], axis=1).astype(BF16)
                ukv_slots[p] = jnp.concatenate(
                    [dwkv_ref[:, LANE * p:LANE * p + NOPE],
                     dwkv_ref[:, LANE * (HEADS + p):LANE * (HEADS + p) + VDIM]], axis=1).astype(BF16)

    row = lambda w, j: pl.BlockSpec((tm, w), lambda i: (i, j))
    full = lambda a: pl.BlockSpec(a.shape, lambda i: (0, 0))
    acc = lambda w: pl.BlockSpec((1, w), lambda i: (0, 0))
    col = lambda w: pl.BlockSpec((w, tm), lambda i: (0, i))
    whole = lambda shape: pl.BlockSpec(shape, lambda i: (0, 0, 0))
    uq_shape = (N_DEV, Q_LORA // N_DEV, HEADS * QK)
    ukv_shape = (N_DEV, KV_LORA, NOPE + VDIM)
    return pl.pallas_call(
        body,
        grid=(s // tm,),
        in_specs=[row(768, 6), row(256, 21), row(1024, 0), row(1024, 0), row(1024, 0), full(g_q), full(g_kv),
                  full(w_uq_pt), full(w_kv_pt), row(128, 0), row(128, 0), row(128, 0), col(Q_LORA), col(KV_LORA),
                  pl.BlockSpec(memory_space=pl.ANY)],
        out_specs=[row(1152, 4), acc(768), acc(256), whole(uq_shape), whole(ukv_shape)],
        out_shape=[jax.ShapeDtypeStruct(dproj.shape, BF16),
                   jax.ShapeDtypeStruct((1, 768), F32), jax.ShapeDtypeStruct((1, 256), F32),
                   jax.ShapeDtypeStruct(uq_shape, BF16), jax.ShapeDtypeStruct(ukv_shape, BF16)],
        input_output_aliases={14: 0},
        scratch_shapes=[pltpu.VMEM((tm, HEADS * LANE), BF16), pltpu.VMEM((tm, 2 * HEADS * LANE), BF16),
                        pltpu.VMEM((Q_LORA, HEADS * LANE), F32), pltpu.VMEM((KV_LORA, 2 * HEADS * LANE), F32)],
        compiler_params=_params(("arbitrary",)),
        name="mla_bwd",
    )(proj, proj, dqr, dkr, dv, g_q, g_kv, w_uq_pt, w_kv_pt, rc, rs1, rs2, cqt, ckvt, dproj)


def _dh_dx(dproj, w_in_pt, x, dout, g_pre, sends, vecs):
    s, k = dproj.shape
    tm = 256
    ns, ni, nv = len(sends), s // tm, len(vecs)

    def body(dp_ref, w_ref, x_ref, dout_ref, g_ref, *rest):
        send_refs, vec_refs, (dx_ref, dg_ref) = rest[:ns], rest[ns:ns + nv], rest[ns + nv:ns + nv + 2]
        recv_refs, gsum_ref = rest[ns + nv + 2:2 * ns + nv + 2], rest[2 * ns + nv + 2]
        sems, vec_scratch = rest[2 * ns + nv + 3:2 * ns + nv + 6], rest[2 * ns + nv + 6:]

        @pl.when(pl.program_id(0) == 0)
        def _():
            _start_all(*_to_chips_copies(send_refs, recv_refs, sems))
            dg_ref[...] = jnp.zeros_like(dg_ref)

        dh = _dot(dp_ref[...], w_ref[...])
        xv = x_ref[...]
        r = lax.rsqrt(jnp.mean(xv * xv, axis=-1, keepdims=True) + EPS)
        xh = xv * r
        dg_ref[...] += jnp.sum(dh * xh, axis=0, keepdims=True)
        dxh = dh * g_ref[...]
        dx_ref[...] = dout_ref[...] + r * (dxh - xh * jnp.mean(dxh * xh, axis=-1, keepdims=True))

        @pl.when(pl.program_id(0) == ni - 1)
        def _():
            _vectors_exchange(dg_ref, vec_refs, gsum_ref, *vec_scratch)
            _wait_all(*_to_chips_copies(send_refs, recv_refs, sems))

    row = lambda w: pl.BlockSpec((tm, w), lambda i: (i, 0))
    whole = lambda shape: pl.BlockSpec(shape, lambda i: (0, 0))
    return pl.pallas_call(
        body,
        grid=(ni,),
        in_specs=[row(k), whole((k, D_MODEL)), row(D_MODEL), row(D_MODEL), whole((1, D_MODEL))] + [ANY] * ns
        + [whole(a.shape) for a in vecs],
        out_specs=[row(D_MODEL), whole((1, D_MODEL))] + [ANY] * ns + [whole((8, 1024))],
        out_shape=[jax.ShapeDtypeStruct((s, D_MODEL), F32), jax.ShapeDtypeStruct((1, D_MODEL), F32)]
        + [jax.ShapeDtypeStruct(a.shape, a.dtype) for a in sends] + [jax.ShapeDtypeStruct((8, 1024), F32)],
        scratch_shapes=_copy_sems(ns, 3) + [pltpu.VMEM((8, 1024), F32), pltpu.VMEM((N_DEV, 8, 1024), F32),
                                            pltpu.SemaphoreType.DMA((7,)), pltpu.SemaphoreType.DMA((7,))],
        compiler_params=_params(("arbitrary",)),
        name="dh_dx",
    )(dproj, w_in_pt, x, dout, g_pre, *sends, *vecs)


def _pair_reduce(slots):
    n = len(slots)
    half = [(N_DEV // 2,) + a.shape[1:] for a in slots]

    def body(*refs):
        s_refs, o_refs = refs[:n], refs[n:2 * n]
        mine, got = refs[2 * n:3 * n], refs[3 * n:4 * n]
        send_sems, recv_sems, local_sems, out_sems = refs[4 * n:]
        x, y, c = _my_place()
        copies, loads, stores = [], [], []
        for q in range(N_DEV // 2):
            for a in range(n):
                copies.append(pltpu.make_async_remote_copy(
                    src_ref=s_refs[a].at[2 * q + 1 - c], dst_ref=got[a].at[q],
                    send_sem=send_sems.at[4 * a + q], recv_sem=recv_sems.at[4 * a + q],
                    device_id=(x, y, 1 - c), device_id_type=MESH_ID))
                loads.append(pltpu.make_async_copy(s_refs[a].at[2 * q + c], mine[a].at[q], local_sems.at[4 * a + q]))
                stores.append(pltpu.make_async_copy(mine[a].at[q], o_refs[a].at[q], out_sems.at[4 * a + q]))
        _start_all(loads, copies)
        k = 0
        for q in range(N_DEV // 2):
            for a in range(n):
                loads[k].wait()
                copies[k].wait_recv()
                mine[a][q] = (mine[a][q].astype(F32) + got[a][q].astype(F32)).astype(mine[a].dtype)
                stores[k].start()
                k += 1
        for cp in copies:
            cp.wait_send()
        for cp in stores:
            cp.wait()

    vm = lambda: [pltpu.VMEM(h, a.dtype) for h, a in zip(half, slots)]
    return pl.pallas_call(
        body,
        in_specs=[ANY] * n,
        out_specs=[ANY] * n,
        out_shape=[jax.ShapeDtypeStruct(h, a.dtype) for h, a in zip(half, slots)],
        scratch_shapes=vm() + vm() + [pltpu.SemaphoreType.DMA((4 * n,)), pltpu.SemaphoreType.DMA((4 * n,)),
                                      pltpu.SemaphoreType.DMA((4 * n,)), pltpu.SemaphoreType.DMA((4 * n,))],
        compiler_params=pltpu.CompilerParams(vmem_limit_bytes=48 * 2**20),
        name="pair_reduce",
    )(*slots)


def _rope_tables(s):
    inv = (np.float32(ROPE_THETA) ** (-np.arange(0, ROPE, 2, dtype=np.float32) / np.float32(ROPE))).astype(np.float32)
    ang = (np.arange(s, dtype=np.float32)[:, None] * inv[None, :]).astype(np.float32)
    cos, sin = jnp.asarray(np.cos(ang.astype(np.float64)), F32), jnp.asarray(np.sin(ang.astype(np.float64)), F32)
    z = lambda w: jnp.zeros((s, w), F32)
    rc = jnp.concatenate([jnp.ones((s, NOPE), F32), cos, cos, z(32)], axis=1)
    rs1 = jnp.concatenate([z(NOPE), -sin, z(16), z(32)], axis=1)
    rs2 = jnp.concatenate([z(NOPE), z(16), sin, z(32)], axis=1)
    return rc, rs1, rs2


def _step(x, tgt, w_blk, shards, g_pre, b_gate, g_q, g_kv, lbl, g_hgrn, g_post):
    s = x.shape[0]
    rc, rs1, rs2 = _rope_tables(s)
    gh = jnp.tile(g_hgrn, (1, HEADS))

    proj, w_in_pt, ht, *got = _gather_proj(x, g_pre, w_blk, shards[:2])
    w_uq, w_ukv = (_from_slots(n, g) for n, g in zip(MATS[:2], got))
    w_uq_p = jnp.pad(w_uq.reshape(Q_LORA, HEADS, QK), ((0, 0), (0, 0), (0, LANE - QK))).reshape(Q_LORA, HEADS * LANE)
    kv3 = w_ukv.reshape(KV_LORA, HEADS, NOPE + VDIM)
    pad64 = lambda t: jnp.pad(t, ((0, 0), (0, 0), (0, LANE - 64))).reshape(KV_LORA, HEADS * LANE)
    w_kv_p = jnp.concatenate([pad64(kv3[:, :, :NOPE]), pad64(kv3[:, :, NOPE:])], axis=1)

    qr, kr, v, cqt, ckvt, *later_shards = _mla_prep(proj, g_q, g_kv, w_uq_p, w_kv_p, rc, rs1, rs2, shards[2:])
    attn, qa, *got = _attn_fwd(qr, kr, v, later_shards)
    w_a, w_b, w_out = (_from_slots(n, g) for n, g in zip(MATS[2:], got))
    o, sprev = _hgrn_fwd(proj, lbl)
    (dout, dproj, dop, do, mt, dy_bf, yat, dya_bf, ybt, dyb_bf,
     loss_vec, dg_post, db_gate, dgh) = _tail(x, tgt, proj, attn, o, w_a, w_b, w_out, w_a.T, w_b.T, w_out.T,
                                               b_gate, g_post, gh)
    dproj, dlbl, *early = _hgrn_bwd(proj, lbl, do, sprev, dproj,
                                    [("w_branch_a", yat, dya_bf), ("w_branch_b", ybt, dyb_bf), ("w_out", mt, dy_bf)])
    dqr, dkr, dv, *early_recv = _attn_bwd(qa, kr, v, dop, early)
    dproj, dg_q, dg_kv, dw_uq_slots, dw_ukv_slots = _mla_bwd(proj, dqr, dkr, dv, g_q, g_kv, w_uq_p.T, w_kv_p.T,
                                                             rc, rs1, rs2, cqt, ckvt, dproj)

    dw_in_slots = _dw_in_slots(ht, dproj)
    late = _pair_reduce([dw_in_slots, dw_uq_slots, dw_ukv_slots])
    dx, _, *late_recv, g_sum = _dh_dx(dproj, w_in_pt, x, dout, g_pre, late,
                                      (db_gate, dg_q, dg_kv, dlbl, dgh, dg_post, loss_vec))
    return dx, late_recv[0], dict(zip(MATS, late_recv[1:] + early_recv)), g_sum


def _adamw(g, w, m, v):
    c1 = 1.0 / (1.0 - ADAM_B1 ** ADAM_STEP)
    c2 = 1.0 / (1.0 - ADAM_B2 ** ADAM_STEP)
    nm = ADAM_B1 * m + (1.0 - ADAM_B1) * g
    nv = ADAM_B2 * v + (1.0 - ADAM_B2) * (g * g)
    d = -ADAM_LR * ((nm * c1) / (jnp.sqrt(nv * c2) + ADAM_EPS) + ADAM_WD * w)
    return d, nm, nv


def _sum8(r_ref):
    g = r_ref[0].astype(F32)
    for k in range(1, r_ref.shape[0]):
        g = g + r_ref[k].astype(F32)
    return g


def _sum_adamw_w_in(recv, w, m, v):
    rows, _, cols = w.shape
    tc = 512
    nc = cols // tc

    def body(r_ref, w_hbm, m_hbm, v_hbm, g_hbm, d_hbm, nm_hbm, nv_hbm, ins, outs, in_sems, out_sems):
        i = pl.program_id(0)
        slot = i & 1
        cols_of = lambda step: pl.ds(pl.multiple_of(step * tc, tc), tc)

        def load(k, step, sl):
            return pltpu.make_async_copy((w_hbm, m_hbm, v_hbm)[k].at[:, 0, cols_of(step)], ins.at[sl, k],
                                         in_sems.at[sl, k])

        def store(k, step, sl):
            return pltpu.make_async_copy(outs.at[sl, k], (g_hbm, d_hbm, nm_hbm, nv_hbm)[k].at[:, 0, cols_of(step)],
                                         out_sems.at[sl, k])

        @pl.when(i == 0)
        def _():
            for k in range(3):
                load(k, 0, 0).start()

        @pl.when(i + 1 < nc)
        def _():
            for k in range(3):
                load(k, i + 1, 1 - slot).start()

        @pl.when(i >= 2)
        def _():
            for k in range(4):
                store(k, i - 2, slot).wait()

        for k in range(3):
            load(k, i, slot).wait()
        g = _sum8(r_ref)
        d, nm, nv = _adamw(g, ins[slot, 0], ins[slot, 1], ins[slot, 2])
        for k, val in enumerate((g, d, nm, nv)):
            outs[slot, k] = val
        for k in range(4):
            store(k, i, slot).start()

        @pl.when(i == nc - 1)
        def _():
            for k in range(4):
                store(k, i, slot).wait()
            if nc >= 2:
                for k in range(4):
                    store(k, i - 1, 1 - slot).wait()

    out = jax.ShapeDtypeStruct((rows, 1, cols), F32)
    return pl.pallas_call(
        body,
        grid=(nc,),
        in_specs=[pl.BlockSpec((recv.shape[0], rows, tc), lambda i: (0, 0, i)), ANY, ANY, ANY],
        out_specs=[ANY, ANY, ANY, ANY],
        out_shape=[out, out, out, out],
        scratch_shapes=[pltpu.VMEM((2, 3, rows, tc), F32), pltpu.VMEM((2, 4, rows, tc), F32),
                        pltpu.SemaphoreType.DMA((2, 3)), pltpu.SemaphoreType.DMA((2, 4))],
        compiler_params=_params(("arbitrary",)),
        name="sum_adamw_w_in",
    )(recv, w, m, v)


def _sum_adamw_whole(recvs, ws, ms, vs):
    n = len(ws)

    def body(*refs):
        r_refs, w_refs, m_refs, v_refs = refs[:n], refs[n:2 * n], refs[2 * n:3 * n], refs[3 * n:4 * n]
        outs = refs[4 * n:]
        for a in range(n):
            g = _sum8(r_refs[a])
            d, nm, nv = _adamw(g, w_refs[a][...], m_refs[a][...], v_refs[a][...])
            outs[a][...] = g
            outs[n + a][...] = d
            outs[2 * n + a][...] = nm
            outs[3 * n + a][...] = nv

    shapes = [jax.ShapeDtypeStruct(w.shape, F32) for w in ws]
    res = pl.pallas_call(
        body,
        out_shape=shapes * 4,
        compiler_params=pltpu.CompilerParams(vmem_limit_bytes=48 * 2**20),
        name="sum_adamw_mats",
    )(*recvs, *ws, *ms, *vs)
    return res[:n], res[n:2 * n], res[2 * n:3 * n], res[3 * n:]


SMALL = ("g_pre", "b_gate", "g_q", "g_kv", "lb_logits", "g_hgrn", "g_post")
SMALL_SHAPE = dict(g_pre=(1, 1024), b_gate=(1, 2048), g_q=(1, 768), g_kv=(1, 256), lb_logits=(2, 512),
                   g_hgrn=(1, 64), g_post=(1, 1024))


def _vectors_exchange(gpre_ref, vec_refs, out_ref, mine, got, send_sems, recv_sems):
    bg_ref, gq_ref, gkv_ref, lbl_ref, gh_ref, gpost_ref, loss_ref = vec_refs
    mine[...] = jnp.zeros_like(mine)
    mine[0:1, :] = gpre_ref[...]
    mine[1:2, :] = bg_ref[:, :1024]
    mine[2:3, :] = bg_ref[:, 1024:]
    mine[3:4, :Q_LORA] = gq_ref[...]
    mine[4:5, :KV_LORA] = gkv_ref[...]
    loss = (0.5 / D_MODEL) * jnp.sum(loss_ref[...], axis=-1, keepdims=True)
    mine[4:5, KV_LORA:] = jnp.broadcast_to(loss, (1, 1024 - KV_LORA))
    mine[5:6, :HG_WIDTH] = lbl_ref[0:1, :]
    mine[5:6, HG_WIDTH:] = lbl_ref[1:2, :]
    gh = gh_ref[...]
    fold = gh[:, :VDIM]
    for h in range(1, HEADS):
        fold = fold + gh[:, VDIM * h:VDIM * (h + 1)]
    mine[6:7, :VDIM] = fold
    mine[7:8, :] = gpost_ref[...]
    x, y, c = _my_place()
    me = 4 * x + 2 * y + c
    got[me] = mine[...]
    copies = [pltpu.make_async_remote_copy(
        src_ref=mine, dst_ref=got.at[me], send_sem=send_sems.at[k], recv_sem=recv_sems.at[k],
        device_id=_flip(k, x, y, c), device_id_type=MESH_ID) for k in range(N_DEV - 1)]
    _start_all([], copies)
    _wait_all([], copies)
    out_ref[...] = _sum8(got)


def _vectors_adamw(g_sum, ws, ms, vs):
    n = len(SMALL)

    def body(g_ref, *refs):
        w_refs, m_refs, v_refs = refs[:n], refs[n:2 * n], refs[2 * n:3 * n]
        loss_ref, outs = refs[3 * n], refs[3 * n + 1:]
        g = g_ref[...]
        loss_ref[...] = g[4:5, KV_LORA:KV_LORA + 1]
        grads = (g[0:1, :], jnp.concatenate([g[1:2, :], g[2:3, :]], axis=1), g[3:4, :Q_LORA], g[4:5, :KV_LORA],
                 jnp.concatenate([g[5:6, :HG_WIDTH], g[5:6, HG_WIDTH:]], axis=0), g[6:7, :VDIM], g[7:8, :])
        for a in range(n):
            d, nm, nv = _adamw(grads[a], w_refs[a][...], m_refs[a][...], v_refs[a][...])
            outs[a][...] = grads[a]
            outs[n + a][...] = d
            outs[2 * n + a][...] = nm
            outs[3 * n + a][...] = nv

    shapes = [jax.ShapeDtypeStruct(SMALL_SHAPE[k], F32) for k in SMALL]
    res = pl.pallas_call(
        body,
        out_shape=[jax.ShapeDtypeStruct((1, 1), F32)] + shapes * 4,
        name="vectors_adamw",
    )(g_sum, *ws, *ms, *vs)
    return res[0], res[1:n + 1], res[n + 1:2 * n + 1], res[2 * n + 1:3 * n + 1], res[3 * n + 1:]


MATS = ("w_uq", "w_ukv", "w_branch_a", "w_branch_b", "w_out")
COL_SHARDED = dict(w_uq=False, w_ukv=True, w_branch_a=True, w_branch_b=True, w_out=False)
ORDER = ("g_pre", "w_in", "b_gate", "g_q", "w_uq", "g_kv", "w_ukv", "lb_logits", "g_hgrn",
         "w_branch_a", "w_branch_b", "w_out", "g_post")


def _from_slots(name, slots):
    _, r, c = slots.shape
    if COL_SHARDED[name]:
        return slots.transpose(1, 0, 2).reshape(r, N_DEV * c)
    return slots.reshape(N_DEV * r, c)


def kernel(x, g_pre, w_in, b_gate, g_q, w_uq, g_kv, w_ukv, lb_logits, g_hgrn, w_branch_a, w_branch_b, w_out, g_post, loss_target, m_g_pre, m_w_in, m_b_gate, m_g_q, m_w_uq, m_g_kv, m_w_ukv, m_lb_logits, m_g_hgrn, m_w_branch_a, m_w_branch_b, m_w_out, m_g_post, v_g_pre, v_w_in, v_b_gate, v_g_q, v_w_uq, v_g_kv, v_w_ukv, v_lb_logits, v_g_hgrn, v_w_branch_a, v_w_branch_b, v_w_out, v_g_post):
    rows3 = lambda a: jnp.transpose(a, (2, 0, 1))
    w = dict(w_in=rows3(w_in), w_uq=w_uq[0], w_ukv=w_ukv[0], w_branch_a=w_branch_a[0], w_branch_b=w_branch_b[0],
             w_out=w_out[0], g_pre=g_pre, b_gate=b_gate, g_q=g_q, g_kv=g_kv, lb_logits=lb_logits, g_hgrn=g_hgrn,
             g_post=g_post)
    mom = dict(w_in=rows3(m_w_in), w_uq=m_w_uq[0], w_ukv=m_w_ukv[0], w_branch_a=m_w_branch_a[0],
               w_branch_b=m_w_branch_b[0], w_out=m_w_out[0], g_pre=m_g_pre, b_gate=m_b_gate, g_q=m_g_q, g_kv=m_g_kv,
               lb_logits=m_lb_logits, g_hgrn=m_g_hgrn, g_post=m_g_post)
    var = dict(w_in=rows3(v_w_in), w_uq=v_w_uq[0], w_ukv=v_w_ukv[0], w_branch_a=v_w_branch_a[0],
               w_branch_b=v_w_branch_b[0], w_out=v_w_out[0], g_pre=v_g_pre, b_gate=v_b_gate, g_q=v_g_q, g_kv=v_g_kv,
               lb_logits=v_lb_logits, g_hgrn=v_g_hgrn, g_post=v_g_post)

    w_blk = w["w_in"].reshape(W_IN_SHARD, D_MODEL).astype(BF16)
    shards = [w[n].astype(BF16) for n in MATS[:2]] + [w[n] for n in MATS[2:]]
    dx, recv_in, recv, g_sum = _step(x[0], loss_target[0], w_blk, shards,
                                     g_pre, b_gate, g_q, g_kv, lb_logits, g_hgrn, g_post)

    g_in, d_in, m_in, v_in = _sum_adamw_w_in(recv_in, w["w_in"], mom["w_in"], var["w_in"])
    res = _sum_adamw_whole([recv[n] for n in MATS], *([t[n] for n in MATS] for t in (w, mom, var)))
    total, *vec = _vectors_adamw(g_sum, *([t[n] for n in SMALL] for t in (w, mom, var)))

    outs = []
    for mats, vecs, big in zip(res, vec, (g_in, d_in, m_in, v_in)):
        t = {**{n: a[None] for n, a in zip(MATS, mats)}, **dict(zip(SMALL, vecs)),
             "w_in": jnp.transpose(big, (1, 2, 0))}
        outs += [t[n] for n in ORDER]
    return (total.reshape(()), dx[None], *outs)
```

```python
import math

import jax
import jax.numpy as jnp
import numpy as np
from jax import lax
from jax.experimental import pallas as pl
from jax.experimental.pallas import tpu as pltpu

F32, BF16 = jnp.float32, jnp.bfloat16

D_MODEL = 1024
EPS = 1e-6
HEADS = 8
NOPE, ROPE, VDIM = 64, 32, 64
QK = NOPE + ROPE
Q_LORA, KV_LORA = 768, 256
ROPE_THETA = 10000.0
ATT_CHUNK_SHIFT = 6
HG_BLOCK = 32
HG_WIDTH = 512
D_IN = 5664
D_IN_PAD = 5760
W_IN_SHARD = D_IN // 8
N_DEV = 8
LANE = 128

ADAM_LR, ADAM_B1, ADAM_B2, ADAM_EPS, ADAM_WD, ADAM_STEP = 0.001, 0.9, 0.999, 1e-08, 0.01, 10

W_IN_SEGMENTS = ((3616, 5664, 0), (1056, 1568, 2048), (3104, 3616, 2560), (1568, 3104, 3072),
                 (0, 1024, 4608), (1024, 1056, 5696))

NT = (((1,), (1,)), ((), ()))
TN = (((0,), (0,)), ((), ()))
MESH_ID = pl.DeviceIdType.MESH


def _w_in_pieces():
    out = []
    for lo, hi, dst in W_IN_SEGMENTS:
        c = lo
        while c < hi:
            p = c // W_IN_SHARD
            e = min(hi, (p + 1) * W_IN_SHARD)
            out.append((p, c - p * W_IN_SHARD, e - p * W_IN_SHARD, dst + c - lo))
            c = e
    return out


def _params(sem, vmem_mb=48):
    return pltpu.CompilerParams(dimension_semantics=sem, vmem_limit_bytes=vmem_mb * 2**20)


def _dot(a, b):
    return jnp.dot(a, b, preferred_element_type=F32)


def _dotg(a, b, dims):
    return lax.dot_general(a, b, dims, preferred_element_type=F32)


def _split2(x):
    hi = x.astype(BF16)
    return hi, (x - hi.astype(F32)).astype(BF16)


def _sel_left(m01, x):
    hi, lo = _split2(x)
    return _dot(m01, hi) + _dot(m01, lo)


def _sel_right(x, m01):
    hi, lo = _split2(x)
    return _dot(hi, m01) + _dot(lo, m01)


def _hi_lo(x):
    hi = x.astype(BF16).astype(F32)
    return hi, x - hi


def _sigmoid(x):
    return 0.5 * jnp.tanh(0.5 * x) + 0.5


def _rope(x, c, s1, s2):
    return x * c + pltpu.roll(x, 112, 1) * s1 + pltpu.roll(x, 16, 1) * s2


def _unrope(d, c, s1, s2):
    return d * c + pltpu.roll(d * s1, 16, 1) + pltpu.roll(d * s2, 112, 1)


def _my_place():
    return lax.axis_index("x"), lax.axis_index("y"), lax.axis_index("c")


def _flip(k, x, y, c):
    fx, fy, fc = (k + 1) >> 2 & 1, (k + 1) >> 1 & 1, (k + 1) & 1
    return (1 - x if fx else x), (1 - y if fy else y), (1 - c if fc else c)


def _to_all_copies(s_refs, r_refs, sems, spread):
    send_sems, recv_sems, local_sems = sems
    x, y, c = _my_place()
    me = 4 * x + 2 * y + c
    src = (lambda a, p: s_refs[a]) if spread else (lambda a, p: s_refs[a].at[p])
    local = [pltpu.make_async_copy(src(a, me), r_refs[a].at[me], local_sems.at[a]) for a in range(len(s_refs))]
    remote = []
    for k in range(N_DEV - 1):
        px, py, pc = _flip(k, x, y, c)
        for a in range(len(s_refs)):
            remote.append(pltpu.make_async_remote_copy(
                src_ref=src(a, 4 * px + 2 * py + pc), dst_ref=r_refs[a].at[me],
                send_sem=send_sems.at[7 * a + k], recv_sem=recv_sems.at[7 * a + k],
                device_id=(px, py, pc), device_id_type=MESH_ID))
    return local, remote


def _to_chips_copies(s_refs, r_refs, sems):
    send_sems, recv_sems, local_sems = sems
    x, y, c = _my_place()
    me = 2 * x + y
    local = [pltpu.make_async_copy(s_refs[a].at[me], r_refs[a].at[me], local_sems.at[a]) for a in range(len(s_refs))]
    remote = []
    for k in range(3):
        px = 1 - x if (k + 1) >> 1 & 1 else x
        py = 1 - y if (k + 1) & 1 else y
        for a in range(len(s_refs)):
            remote.append(pltpu.make_async_remote_copy(
                src_ref=s_refs[a].at[2 * px + py], dst_ref=r_refs[a].at[me],
                send_sem=send_sems.at[3 * a + k], recv_sem=recv_sems.at[3 * a + k],
                device_id=(px, py, c), device_id_type=MESH_ID))
    return local, remote


def _start_all(local, remote):
    for cp in local + remote:
        cp.start()


def _wait_all(local, remote):
    for cp in remote:
        cp.wait_recv()
    for cp in remote:
        cp.wait_send()
    for cp in local:
        cp.wait()


def _copy_sems(n, peers):
    return [pltpu.SemaphoreType.DMA((peers * n,)), pltpu.SemaphoreType.DMA((peers * n,)),
            pltpu.SemaphoreType.DMA((n,))]


ANY = pl.BlockSpec(memory_space=pl.ANY)


def _dw_in_slots(ht, dproj):
    m, k = ht.shape
    n = dproj.shape[1]
    tn, tk = 1920, 2048
    nj, nk = n // tn, k // tk
    by_tile = [[] for _ in range(nj)]
    for p, lo, hi, dst in _w_in_pieces():
        while lo < hi:
            j = dst // tn
            cnt = min(hi - lo, (j + 1) * tn - dst)
            by_tile[j].append((p, lo, lo + cnt, dst - j * tn))
            lo, dst = lo + cnt, dst + cnt

    def body(a_ref, b_ref, s_ref, acc_ref):
        j, l = pl.program_id(0), pl.program_id(1)

        @pl.when(l == 0)
        def _():
            acc_ref[...] = jnp.zeros_like(acc_ref)

        acc_ref[...] += _dot(a_ref[...], b_ref[...])

        @pl.when(l == nk - 1)
        def _():
            at = acc_ref[...].T
            for jj in range(nj):
                @pl.when(j == jj)
                def _(jj=jj):
                    for p, lo, hi, d in by_tile[jj]:
                        s_ref[p, lo:hi, :] = at[d:d + hi - lo, :].astype(BF16)

    return pl.pallas_call(
        body,
        grid=(nj, nk),
        in_specs=[pl.BlockSpec((m, tk), lambda j, l: (0, l)), pl.BlockSpec((tk, tn), lambda j, l: (l, j))],
        out_specs=pl.BlockSpec((N_DEV, W_IN_SHARD, m), lambda j, l: (0, 0, 0), pipeline_mode=pl.Buffered(1)),
        out_shape=jax.ShapeDtypeStruct((N_DEV, W_IN_SHARD, m), BF16),
        scratch_shapes=[pltpu.VMEM((m, tn), F32)],
        compiler_params=_params(("arbitrary", "arbitrary"), 56),
        name="dw_in",
    )(ht, dproj)


GP_TN = 256
GP_COLS = 5888
GP_NT = GP_COLS // GP_TN


def _gp_tile_pieces():
    tiles = [[] for _ in range(GP_NT)]
    for p, lo, hi, dst in _w_in_pieces():
        while lo < hi:
            t = dst // GP_TN
            n = min(hi - lo, (t + 1) * GP_TN - dst)
            tiles[t].append((p, lo, lo + n, dst - t * GP_TN))
            lo, dst = lo + n, dst + n
    return tiles


def _gp_tables():
    pieces = _gp_tile_pieces()
    rank_of = {None: 0, 0: 1, 1: 2, 2: 2, 4: 3, 5: 3, 3: 4, 6: 5}
    order = np.zeros((N_DEV, GP_NT), np.int32)
    waits = np.zeros((N_DEV, GP_NT), np.int32)
    for me in range(N_DEV):
        x, y, c = me >> 2 & 1, me >> 1 & 1, me & 1
        chips = [(1 - x, y), (x, 1 - y), (1 - x, 1 - y)]

        def sem_of(p):
            px, py, pc = p >> 2 & 1, p >> 1 & 1, p & 1
            if (px, py) == (x, y):
                return None if pc == c else 0
            j = chips.index((px, py))
            return 1 + j if pc == c else 4 + j

        needs = [sorted({sem_of(p) for p, _, _, _ in tile} - {None}) for tile in pieces]
        ranks = [max([rank_of[k] for k in ks], default=0) for ks in needs]
        seq = sorted(range(GP_NT), key=lambda t: (ranks[t], t))
        seen = set()
        for step, t in enumerate(seq):
            order[me, step] = t
            new = [k for k in needs[t] if k not in seen]
            for k in new:
                waits[me, step] |= 1 << k
            seen.update(new)
        assert seen == set(range(7)), (me, seen)
    return order, waits


def _gather_proj(x, g_pre, w_blk, shards):
    s = x.shape[0]
    tx = 512
    ns = len(shards)
    tile_pieces = _gp_tile_pieces()
    order_np, waits_np = _gp_tables()
    xq, yq, cq = _my_place()
    me_out = 4 * xq + 2 * yq + cq
    order = lax.dynamic_index_in_dim(jnp.asarray(order_np), me_out, 0, keepdims=False)
    waits = lax.dynamic_index_in_dim(jnp.asarray(waits_np), me_out, 0, keepdims=False)

    def body(order_ref, waits_ref, x_hbm, g_ref, wblk_hbm, *rest):
        shard_refs, (proj_ref, wt_ref, ht_hbm), got_refs = rest[:ns], rest[ns:ns + 3], rest[ns + 3:2 * ns + 3]
        recv, h_ref, wtile, xbuf, htbuf = rest[2 * ns + 3:2 * ns + 8]
        send_sems, recv_sems, misc_sems = rest[2 * ns + 8:2 * ns + 11]
        sems = rest[2 * ns + 11:]
        t = pl.program_id(0)
        x_, y_, c = _my_place()
        sibling = (x_, y_, 1 - c)
        chips = [(1 - x_, y_), (x_, 1 - y_), (1 - x_, 1 - y_)]
        idx = lambda px, py, pc: 4 * px + 2 * py + pc
        me = idx(x_, y_, c)

        def copy(k, slot, to, src=None):
            return pltpu.make_async_remote_copy(
                src_ref=recv.at[slot] if src is None else src, dst_ref=recv.at[slot],
                send_sem=send_sems.at[k], recv_sem=recv_sems.at[k], device_id=to, device_id_type=MESH_ID)

        mine = pltpu.make_async_copy(wblk_hbm, recv.at[me], misc_sems.at[0])
        first = [copy(0, me, sibling, src=wblk_hbm)] + [copy(1 + j, me, (*chips[j], c), src=wblk_hbm) for j in range(2)]
        passed = [copy(4 + j, idx(*ch, c), sibling) for j, ch in enumerate(chips)]
        onward = [copy(3, idx(*chips[0], c), (*chips[1], c)), copy(3, idx(*chips[1], c), (*chips[0], c))]
        arrivals = ([copy(0, idx(x_, y_, 1 - c), sibling)] + [copy(1 + j, idx(*ch, c), sibling) for j, ch in enumerate(chips)]
                    + [copy(4 + j, idx(*ch, 1 - c), sibling) for j, ch in enumerate(chips)])

        @pl.when(t == 0)
        def _():
            mine.start()
            for cp in first:
                cp.start()
            _start_all(*_to_all_copies(shard_refs, got_refs, sems, True))

            def load(i):
                return pltpu.make_async_copy(x_hbm.at[pl.ds(i * tx, tx), :], xbuf.at[i & 1], misc_sems.at[1 + (i & 1)])

            def store(i):
                return pltpu.make_async_copy(htbuf.at[i & 1], ht_hbm.at[:, pl.ds(i * tx, tx)], misc_sems.at[3 + (i & 1)])

            load(0).start()
            for i in range(s // tx):
                if i + 1 < s // tx:
                    load(i + 1).start()
                load(i).wait()
                xv = xbuf[i & 1]
                r = lax.rsqrt(jnp.mean(xv * xv, axis=-1, keepdims=True) + EPS)
                h = (xv * r * g_ref[...]).astype(BF16)
                h_ref[i * tx:(i + 1) * tx, :] = h
                if i >= 2:
                    store(i - 2).wait()
                htbuf[i & 1] = h.T
                store(i).start()
            for i in range(max(s // tx - 2, 0), s // tx):
                store(i).wait()
            mine.wait()

        w = waits_ref[t]
        for k in range(7):
            @pl.when((w >> k) & 1 == 1)
            def _(k=k):
                arrivals[k].wait_recv()
                if 1 <= k <= 3:
                    passed[k - 1].start()
                if 1 <= k <= 2:
                    @pl.when(c == k - 1)
                    def _():
                        onward[k - 1].start()

        tile = order_ref[t]
        for tt in range(GP_NT):
            @pl.when(tile == tt)
            def _(tt=tt):
                covered = sorted((d, d + hi - lo) for _, lo, hi, d in tile_pieces[tt])
                at = 0
                for lo_z, hi_z in covered + [(GP_TN, GP_TN)]:
                    if lo_z > at:
                        wtile[at:lo_z, :] = jnp.zeros((lo_z - at, D_MODEL), BF16)
                    at = max(at, hi_z)
                for p, lo, hi, d in tile_pieces[tt]:
                    wtile[d:d + hi - lo, :] = recv[p, lo:hi, :]

        wt = wtile[...]
        wt_ref[...] = wt
        proj_ref[...] = _dotg(h_ref[...], wt, NT)

        @pl.when(t == GP_NT - 1)
        def _():
            for cp in first + passed + onward[:1]:
                cp.wait_send()
            _wait_all(*_to_all_copies(shard_refs, got_refs, sems, True))

    grid_spec = pltpu.PrefetchScalarGridSpec(
        num_scalar_prefetch=2,
        grid=(GP_NT,),
        in_specs=[ANY, pl.BlockSpec((1, D_MODEL), lambda t, o, w: (0, 0)), ANY] + [ANY] * ns,
        out_specs=[pl.BlockSpec((s, GP_TN), lambda t, o, w: (0, o[t])),
                   pl.BlockSpec((GP_TN, D_MODEL), lambda t, o, w: (o[t], 0)), ANY] + [ANY] * ns,
        scratch_shapes=[pltpu.VMEM((N_DEV, W_IN_SHARD, D_MODEL), BF16), pltpu.VMEM((s, D_MODEL), BF16),
                        pltpu.VMEM((GP_TN, D_MODEL), BF16), pltpu.VMEM((2, tx, D_MODEL), F32),
                        pltpu.VMEM((2, D_MODEL, tx), BF16),
                        pltpu.SemaphoreType.DMA((7,)), pltpu.SemaphoreType.DMA((7,)), pltpu.SemaphoreType.DMA((5,))]
        + _copy_sems(ns, 7),
    )
    return pl.pallas_call(
        body,
        grid_spec=grid_spec,
        out_shape=[jax.ShapeDtypeStruct((s, GP_COLS), F32),jax.ShapeDtypeStruct((GP_COLS, D_MODEL), BF16),
                   jax.ShapeDtypeStruct((D_MODEL, s), BF16)]
        + [jax.ShapeDtypeStruct((N_DEV,) + b.shape, b.dtype) for b in shards],
        compiler_params=_params(("arbitrary",), 56),
        name="gather_proj",
    )(order, waits, x, g_pre, w_blk, *shards)


def _mla_prep(proj, g_q, g_kv, w_uq_p, w_kv_p, rc, rs1, rs2, casts):
    s = proj.shape[0]
    tm = 512
    nc = len(casts)
    scale = 1.0 / math.sqrt(QK)

    def body(cq_ref, ckv_ref, kpe_ref, gq_ref, gkv_ref, wuq_ref, wkv_ref, c_ref, s1_ref, s2_ref, *rest):
        cast_in, (qr_ref, kr_ref, v_ref, cqt_ref, ckvt_ref), cast_out = rest[:nc], rest[nc:nc + 5], rest[nc + 5:]

        @pl.when(pl.program_id(0) == 0)
        def _():
            for src, dst in zip(cast_in, cast_out):
                dst[...] = src[...].astype(BF16)

        cq = cq_ref[...]
        r = lax.rsqrt(jnp.mean(cq * cq, axis=-1, keepdims=True) + EPS)
        cqn = (cq * r * gq_ref[...]).astype(BF16)
        cqt_ref[...] = cqn.T
        q = _dot(cqn, wuq_ref[...])
        ckv = ckv_ref[...]
        r = lax.rsqrt(jnp.mean(ckv * ckv, axis=-1, keepdims=True) + EPS)
        ckvn = (ckv * r * gkv_ref[...]).astype(BF16)
        ckvt_ref[...] = ckvn.T
        kv = _dot(ckvn, wkv_ref[...])
        c, s1, s2 = c_ref[...], s1_ref[...], s2_ref[...]
        lane = lax.broadcasted_iota(jnp.int32, (tm, LANE), 1)
        kpe = _rope(kpe_ref[...], c, s1, s2) + jnp.where((lane == QK) | (lane == QK + 1), 1.0, 0.0)
        vone = jnp.where((lane == VDIM) | (lane == VDIM + 1), 1.0, 0.0)
        for h in range(HEADS):
            sl = slice(LANE * h, LANE * (h + 1))
            qr_ref[:, sl] = (_rope(q[:, sl], c, s1, s2) * scale).astype(BF16)
            kr_ref[:, sl] = (kv[:, sl] + kpe).astype(BF16)
            v_ref[:, sl] = (kv[:, HEADS * LANE + LANE * h:HEADS * LANE + LANE * (h + 1)] + vone).astype(BF16)

    row = lambda w, j: pl.BlockSpec((tm, w), lambda i: (i, j))
    col = lambda w: pl.BlockSpec((w, tm), lambda i: (0, i))
    full = lambda a: pl.BlockSpec(a.shape, lambda i: (0, 0))
    return pl.pallas_call(
        body,
        grid=(s // tm,),
        in_specs=[row(768, 6), row(256, 21), row(128, 44), full(g_q), full(g_kv), full(w_uq_p), full(w_kv_p),
                  row(128, 0), row(128, 0), row(128, 0)] + [full(a) for a in casts],
        out_specs=[row(1024, 0), row(1024, 0), row(1024, 0), col(768), col(256)] + [full(a) for a in casts],
        out_shape=[jax.ShapeDtypeStruct((s, 1024), BF16), jax.ShapeDtypeStruct((s, 1024), BF16),
                   jax.ShapeDtypeStruct((s, 1024), BF16), jax.ShapeDtypeStruct((768, s), BF16),
                   jax.ShapeDtypeStruct((256, s), BF16)] + [jax.ShapeDtypeStruct(a.shape, BF16) for a in casts],
        compiler_params=_params(("arbitrary",)),
        name="mla_prep",
    )(proj, proj, proj, g_q, g_kv, w_uq_p, w_kv_p, rc, rs1, rs2, *casts)


ATT_T = 512
ATT_FWD_HEADS = 4


def _chunk_mask(transposed):
    r = lax.broadcasted_iota(jnp.int32, (ATT_T, ATT_T), 0) >> ATT_CHUNK_SHIFT
    c = lax.broadcasted_iota(jnp.int32, (ATT_T, ATT_T), 1) >> ATT_CHUNK_SHIFT
    return (r <= c) if transposed else (c <= r)


def _attn_fwd(qr, kr, vp, shards):
    s = qr.shape[0]
    t = ATT_T
    g = ATT_FWD_HEADS
    ns = len(shards)

    def body(q_ref, k_ref, v_ref, *rest):
        shard_refs, (o_ref, qa_ref), got_refs = rest[:ns], rest[ns:ns + 2], rest[ns + 2:2 * ns + 2]
        sc_ref, sems = rest[2 * ns + 2], rest[2 * ns + 3:]
        qi = pl.program_id(1)

        @pl.when((pl.program_id(0) == 0) & (qi == 0))
        def _():
            _start_all(*_to_all_copies(shard_refs, got_refs, sems, True))
        lane = lax.broadcasted_iota(jnp.int32, (t, LANE), 1)
        sls = [slice(LANE * a, LANE * (a + 1)) for a in range(g)]
        qs = [q_ref[:, sl] for sl in sls]

        def scores(j):
            rows = pl.ds(pl.multiple_of(j * t, t), t)
            for a in range(g):
                sc_ref[j & 1, a] = _dotg(qs[a], k_ref[rows, sls[a]], NT)

        def step(j, carry, masked):
            rows = pl.ds(pl.multiple_of(j * t, t), t)
            out = []
            for a in range(g):
                m, acc = carry[a]
                sc = sc_ref[j & 1, a]
                if masked:
                    sc = jnp.where(_chunk_mask(False), sc, -1e30)
                m_new = jnp.maximum(m, jnp.max(sc, axis=-1, keepdims=True))
                p = jnp.exp(sc - m_new).astype(BF16)
                acc = jnp.exp(m - m_new) * acc + _dot(p, v_ref[rows, sls[a]])
                out.append((m_new, acc))
            return tuple(out)

        def loop(j, carry):
            carry = step(j, carry, False)
            scores(j + 1)
            return carry

        init = tuple((jnp.full((t, 1), -1e30, F32), jnp.zeros((t, LANE), F32)) for _ in range(g))
        scores(0)
        carry = lax.fori_loop(0, qi, loop, init)
        carry = step(qi, carry, True)
        outs = []
        for a in range(g):
            m, acc = carry[a]
            l = acc[:, VDIM:VDIM + 1]
            outs.append(acc / l)
            hi, lo_part = _hi_lo(-(m + jnp.log(l)))
            qa = jnp.where(lane == QK, hi, jnp.where(lane == QK + 1, lo_part, qs[a].astype(F32)))
            qa_ref[:, sls[a]] = qa.astype(BF16)
        for p in range(g // 2):
            o_ref[:, LANE * p:LANE * (p + 1)] = jnp.where(lane < VDIM, outs[2 * p], pltpu.roll(outs[2 * p + 1], VDIM, 1))

        @pl.when((pl.program_id(0) == HEADS // g - 1) & (qi == s // t - 1))
        def _():
            _wait_all(*_to_all_copies(shard_refs, got_refs, sems, True))

    return pl.pallas_call(
        body,
        grid=(HEADS // g, s // t),
        in_specs=[
            pl.BlockSpec((t, g * LANE), lambda h, i: (i, h)),
            pl.BlockSpec((s, g * LANE), lambda h, i: (0, h)),
            pl.BlockSpec((s, g * LANE), lambda h, i: (0, h)),
        ] + [ANY] * ns,
        out_specs=[
            pl.BlockSpec((t, g * VDIM), lambda h, i: (i, h)),
            pl.BlockSpec((t, g * LANE), lambda h, i: (i, h)),
        ] + [ANY] * ns,
        out_shape=[jax.ShapeDtypeStruct((s, 512), F32), jax.ShapeDtypeStruct((s, 1024), BF16)]
        + [jax.ShapeDtypeStruct((N_DEV,) + b.shape, b.dtype) for b in shards],
        scratch_shapes=[pltpu.VMEM((2, g, t, t), F32)] + _copy_sems(ns, 7),
        compiler_params=_params(("arbitrary", "arbitrary")),
        name="attn_fwd",
    )(qr, kr, vp, *shards)


def _attn_bwd(qa, kr, vp, dop, sends):
    s = qa.shape[0]
    t = ATT_T
    nq = s // t
    ns = len(sends)

    def body(q_ref, k_ref, v_ref, do_ref, *rest):
        send_refs, (dq_out, dk_out, dv_out) = rest[:ns], rest[ns:ns + 3]
        recv_refs = rest[ns + 3:2 * ns + 3]
        (dq_ref, dk_ref, dv_ref), sems = rest[2 * ns + 3:2 * ns + 6], rest[2 * ns + 6:]
        j = pl.program_id(1)
        sls = [slice(LANE * a, LANE * (a + 1)) for a in range(2)]

        @pl.when((pl.program_id(0) == 0) & (j == 0))
        def _():
            _start_all(*_to_all_copies(send_refs, recv_refs, sems, False))

        @pl.when(j == 0)
        def _():
            dq_ref[...] = jnp.zeros_like(dq_ref)

        dk_ref[...] = jnp.zeros_like(dk_ref)
        dv_ref[...] = jnp.zeros_like(dv_ref)
        ks = [k_ref[:, sl] for sl in sls]
        vs = [v_ref[:, sl] for sl in sls]

        def part(i, k_lo, k_n, q_lo, q_n, masked):
            rows = pl.ds(pl.multiple_of(i * t + q_lo, 256), q_n)
            keys = slice(k_lo, k_lo + k_n)
            for a in range(2):
                q = q_ref[rows, sls[a]]
                do = do_ref[rows, sls[a]]
                sc = _dotg(ks[a][keys], q, NT)
                if masked:
                    kc = lax.broadcasted_iota(jnp.int32, (k_n, q_n), 0) >> ATT_CHUNK_SHIFT
                    qc = lax.broadcasted_iota(jnp.int32, (k_n, q_n), 1) >> ATT_CHUNK_SHIFT
                    sc = jnp.where(kc <= qc, sc, -1e30)
                p = jnp.exp(sc)
                ds = (p * _dotg(vs[a][keys], do, NT)).astype(BF16)
                dv_ref[keys, sls[a]] += _dot(p.astype(BF16), do)
                dk_ref[keys, sls[a]] += _dot(ds, q)
                dq_ref[rows, sls[a]] += _dotg(ds, ks[a][keys], TN)

        half = t // 2
        part(j, 0, half, 0, t, True)
        part(j, half, half, half, half, True)

        def loop(i, c):
            part(i, 0, t, 0, t, False)
            return c

        lax.fori_loop(j + 1, nq, loop, 0)
        dk_out[...] = dk_ref[...].astype(BF16)
        dv_out[...] = dv_ref[...].astype(BF16)

        @pl.when(j == nq - 1)
        def _():
            dq_out[...] = dq_ref[...].astype(BF16)

        @pl.when((pl.program_id(0) == HEADS // 2 - 1) & (j == nq - 1))
        def _():
            _wait_all(*_to_all_copies(send_refs, recv_refs, sems, False))

    blk = pl.BlockSpec((t, 2 * LANE), lambda h, j: (j, h))
    whole = pl.BlockSpec((s, 2 * LANE), lambda h, j: (0, h))
    out = jax.ShapeDtypeStruct((s, 1024), BF16)
    return pl.pallas_call(
        body,
        grid=(HEADS // 2, nq),
        in_specs=[whole, blk, blk, whole] + [ANY] * ns,
        out_specs=[whole, blk, blk] + [ANY] * ns,
        out_shape=[out, out, out] + [jax.ShapeDtypeStruct(a.shape, a.dtype) for a in sends],
        scratch_shapes=[pltpu.VMEM((s, 2 * LANE), F32), pltpu.VMEM((t, 2 * LANE), F32),
                        pltpu.VMEM((t, 2 * LANE), F32)] + _copy_sems(ns, 7),
        compiler_params=_params(("arbitrary", "arbitrary")),
        name="attn_bwd",
    )(qa, kr, vp, dop, *sends)


HG_T = 256
HG_NC = HG_T // HG_BLOCK
HG_G = 4
GW = 64 * HG_G


def _hg_consts():
    r = jnp.arange(HG_T)[:, None]
    c = jnp.arange(HG_T)[None, :]
    same = (r // HG_BLOCK) == (c // HG_BLOCK)
    mcum = (same & (c <= r)).astype(BF16)
    mrev = (same & (c >= r)).astype(BF16)
    msum = same.astype(BF16)
    a = jnp.arange(GW) // 64
    bd = (a[:, None] == a[None, :]).astype(F32)
    return mcum, mrev, msum, bd


def _stack_heads(xg, head):
    return jnp.concatenate([jnp.where(head == h, xg, 0.0) for h in range(HG_G)], axis=0)


def _unstack_heads(r, head, t):
    out = r[(HG_G - 1) * t:]
    for h in range(HG_G - 2, -1, -1):
        out = jnp.where(head == h, r[h * t:(h + 1) * t], out)
    return out


def _compact_state(st):
    out = st[:64]
    for h in range(1, HG_G):
        out = out + st[64 * h:64 * (h + 1)]
    return out


def _expand_state(cs, head64):
    return jnp.concatenate([jnp.where(head64 == h, cs, 0.0) for h in range(HG_G)], axis=0)


def _hg_pre(hq, hf, lbl, mcum, msum):
    lb = _sigmoid(lbl[0:1, :] - lbl[1:2, :])
    sig = _sigmoid(hf)
    f = lb + (1.0 - lb) * sig
    lf = jnp.log(f)
    b = _sel_left(mcum, lf)
    big_l = _sel_left(msum, lf)
    k = 1.0 - f
    qd = hq * jnp.exp(b)
    ki = k * jnp.exp(-b)
    ke = k * jnp.exp(big_l - b)
    return lb, sig, f, b, big_l, qd, ki, ke


def _hgrn_fwd(proj, lbl):
    s = proj.shape[0]
    t = HG_T
    mcum, _, msum, bd = _hg_consts()

    def body(hq_ref, hf_ref, hi_ref, lbl_ref, mcum_ref, msum_ref, bd_ref, o_ref, sp_ref, st_ref):
        @pl.when(pl.program_id(0) == 0)
        def _():
            st_ref[...] = jnp.zeros_like(st_ref)

        mc = mcum_ref[...]
        _, _, _, _, big_l, qd, ki, ke = _hg_pre(hq_ref[...], hf_ref[...], lbl_ref[...], mc, msum_ref[...])
        el = jnp.exp(big_l)
        hi = hi_ref[...]
        head = lax.broadcasted_iota(jnp.int32, (t, GW), 1) >> 6
        mask = jnp.concatenate([mc] * HG_G, axis=0) > 0.5
        for p in range(HEADS // HG_G):
            sl = slice(GW * p, GW * (p + 1))
            vp = hi[:, sl].astype(BF16)
            qs = _stack_heads(qd[:, sl], head).astype(BF16)
            a = jnp.where(mask, _dotg(qs, ki[:, sl].astype(BF16), NT), 0.0)
            o_intra = _unstack_heads(_dot(a.astype(BF16), vp), head, t)
            qb = qd[:, sl].astype(BF16)
            kb = ke[:, sl].astype(BF16)
            st = st_ref[p]
            for c in range(HG_NC):
                rows = slice(HG_BLOCK * c, HG_BLOCK * (c + 1))
                sp_ref[c, :, sl] = _compact_state(st)
                o_ref[rows, sl] = o_intra[rows] + _dotg(qb[rows], st.astype(BF16), NT)
                u = _dotg(vp[rows], kb[rows], TN) * bd_ref[...]
                st = st * el[HG_BLOCK * c:HG_BLOCK * c + 1, sl] + u
            st_ref[p] = st

    row = lambda j: pl.BlockSpec((t, HG_WIDTH), lambda i: (i, j))
    full = lambda a: pl.BlockSpec(a.shape, lambda i: (0, 0))
    return pl.pallas_call(
        body,
        grid=(s // t,),
        in_specs=[row(6), row(7), row(8), full(lbl), full(mcum), full(msum), full(bd)],
        out_specs=[row(0), pl.BlockSpec((HG_NC, 64, HG_WIDTH), lambda i: (i, 0, 0))],
        out_shape=[jax.ShapeDtypeStruct((s, HG_WIDTH), F32),
                   jax.ShapeDtypeStruct((s // HG_BLOCK, 64, HG_WIDTH), F32)],
        scratch_shapes=[pltpu.VMEM((HEADS // HG_G, GW, GW), F32)],
        compiler_params=_params(("arbitrary",)),
        name="hgrn_fwd",
    )(proj, proj, proj, lbl, mcum, msum, bd)


def _slot_shape(name, r, c):
    return (N_DEV, r, c // N_DEV) if COL_SHARDED[name] else (N_DEV, r // N_DEV, c)


def _emit_slots(name, acc_ref, out_ref):
    r, c = acc_ref.shape
    for p in range(N_DEV):
        if COL_SHARDED[name]:
            out_ref[p] = acc_ref[:, c // N_DEV * p:c // N_DEV * (p + 1)].astype(BF16)
        else:
            out_ref[p] = acc_ref[r // N_DEV * p:r // N_DEV * (p + 1), :].astype(BF16)


def _hgrn_bwd(proj, lbl, do, sprev, dproj, pairs):
    s = proj.shape[0]
    t = HG_T
    nt = s // t
    npair = len(pairs)
    mcum, mrev, msum, bd = _hg_consts()

    def body(hq_ref, hf_ref, hi_ref, lbl_ref, do_ref, sp_ref, mcum_ref, mrev_ref, msum_ref, bd_ref,
             dproj_in, *rest):
        del dproj_in
        pair_refs, (dh_ref, dlbl_ref) = rest[:2 * npair], rest[2 * npair:2 * npair + 2]
        dw_refs, g_ref, acc_refs = rest[2 * npair + 2:3 * npair + 2], rest[3 * npair + 2], rest[3 * npair + 3:]

        @pl.when(pl.program_id(0) == 0)
        def _():
            g_ref[...] = jnp.zeros_like(g_ref)
            dlbl_ref[...] = jnp.zeros_like(dlbl_ref)
            for acc_ref in acc_refs:
                acc_ref[...] = jnp.zeros_like(acc_ref)

        for n, acc_ref in enumerate(acc_refs):
            acc_ref[...] += _dot(pair_refs[2 * n][...], pair_refs[2 * n + 1][...])

        @pl.when(pl.program_id(0) == nt - 1)
        def _():
            for (name, _, _), acc_ref, dw_ref in zip(pairs, acc_refs, dw_refs):
                _emit_slots(name, acc_ref, dw_ref)

        mc = mcum_ref[...]
        lb, sig, f, b, big_l, qd, ki, ke = _hg_pre(hq_ref[...], hf_ref[...], lbl_ref[...], mc, msum_ref[...])
        el = jnp.exp(big_l)
        hi = hi_ref[...]
        dov = do_ref[...]
        head = lax.broadcasted_iota(jnp.int32, (t, GW), 1) >> 6
        head64 = lax.broadcasted_iota(jnp.int32, (64, GW), 1) >> 6
        mask = jnp.concatenate([mc] * HG_G, axis=0) > 0.5
        dqd_parts, dke_parts, dv_parts, del_parts, dki_parts = [], [], [], [], []
        for p in range(HEADS // HG_G):
            sl = slice(GW * p, GW * (p + 1))
            vp = hi[:, sl].astype(BF16)
            qs = _stack_heads(qd[:, sl], head).astype(BF16)
            kip = ki[:, sl].astype(BF16)
            dos = _stack_heads(dov[:, sl], head).astype(BF16)
            a = jnp.where(mask, _dotg(qs, kip, NT), 0.0).astype(BF16)
            da = jnp.where(mask, _dotg(dos, vp, NT), 0.0).astype(BF16)
            r = _dot(da, kip)
            dki_parts.append(_dotg(da, qs, TN))
            qb = qd[:, sl].astype(BF16)
            kb = ke[:, sl].astype(BF16)
            dob = dov[:, sl].astype(BF16)
            g = g_ref[p]
            dqd_c, dv_c, dke_c, del_c = [], [], [], []
            for c in range(HG_NC - 1, -1, -1):
                rows = slice(HG_BLOCK * c, HG_BLOCK * (c + 1))
                gb = g.astype(BF16)
                st = _expand_state(sp_ref[c, :, sl], head64)
                dqd_c.append(_dot(dob[rows], st.astype(BF16)))
                dv_c.append(_dotg(kb[rows], gb, NT))
                dke_c.append(_dot(vp[rows], gb))
                del_c.append(jnp.broadcast_to(jnp.sum(g * st, axis=0, keepdims=True), (HG_BLOCK, GW)))
                g = g * el[HG_BLOCK * c:HG_BLOCK * c + 1, sl] + _dotg(dob[rows], qb[rows], TN) * bd_ref[...]
            g_ref[p] = g
            up = lambda parts: jnp.concatenate(parts[::-1], axis=0)
            dqd_parts.append(_unstack_heads(r, head, t) + up(dqd_c))
            dv_parts.append(_dotg(a, dos, TN) + up(dv_c))
            dke_parts.append(up(dke_c))
            del_parts.append(up(del_c))
        wide = lambda parts: jnp.concatenate(parts, axis=1)
        dqd, dke, dki, dvv, del_rows = wide(dqd_parts), wide(dke_parts), wide(dki_parts), wide(dv_parts), wide(del_parts)
        dh_ref[:, :HG_WIDTH] = (dqd * jnp.exp(b)).astype(BF16)
        dh_ref[:, 2 * HG_WIDTH:] = dvv.astype(BF16)
        dke_ke = dke * ke
        db = dqd * qd - dki * ki - dke_ke
        dl_rows = _sel_left(msum_ref[...], dke_ke) + del_rows * el
        is_last = (lax.broadcasted_iota(jnp.int32, (t, HG_WIDTH), 0) & (HG_BLOCK - 1)) == HG_BLOCK - 1
        db = db + jnp.where(is_last, dl_rows, 0.0)
        dlf = _sel_left(mrev_ref[...], db)
        dk = dki * jnp.exp(-b) + dke * jnp.exp(big_l - b)
        df = dlf / f - dk
        dh_ref[:, HG_WIDTH:2 * HG_WIDTH] = (df * (1.0 - lb) * sig * (1.0 - sig)).astype(BF16)
        dlb = jnp.sum(df * (1.0 - sig), axis=0, keepdims=True) * lb * (1.0 - lb)
        dlbl_ref[0:1, :] += dlb
        dlbl_ref[1:2, :] -= dlb

    rrow = lambda j: pl.BlockSpec((t, HG_WIDTH), lambda i: (nt - 1 - i, j))
    full = lambda a: pl.BlockSpec(a.shape, lambda i: (0, 0))
    pair_specs, dw_specs, dw_shapes, accs = [], [], [], []
    for name, at, b in pairs:
        pair_specs += [pl.BlockSpec((at.shape[0], t), lambda i: (0, i)), pl.BlockSpec((t, b.shape[1]), lambda i: (i, 0))]
        shape = _slot_shape(name, at.shape[0], b.shape[1])
        dw_specs.append(pl.BlockSpec(shape, lambda i: (0, 0, 0)))
        dw_shapes.append(jax.ShapeDtypeStruct(shape, BF16))
        accs.append(pltpu.VMEM((at.shape[0], b.shape[1]), F32))
    return pl.pallas_call(
        body,
        grid=(nt,),
        in_specs=[rrow(6), rrow(7), rrow(8), full(lbl), rrow(0),
                  pl.BlockSpec((HG_NC, 64, HG_WIDTH), lambda i: (nt - 1 - i, 0, 0)),
                  full(mcum), full(mrev), full(msum), full(bd), pl.BlockSpec(memory_space=pl.ANY)] + pair_specs,
        out_specs=[pl.BlockSpec((t, 3 * HG_WIDTH), lambda i: (nt - 1 - i, 2)),
                   pl.BlockSpec((2, HG_WIDTH), lambda i: (0, 0))] + dw_specs,
        out_shape=[jax.ShapeDtypeStruct(dproj.shape, BF16), jax.ShapeDtypeStruct((2, HG_WIDTH), F32)] + dw_shapes,
        input_output_aliases={10: 0},
        scratch_shapes=[pltpu.VMEM((HEADS // HG_G, GW, GW), F32)] + accs,
        compiler_params=_params(("arbitrary",)),
        name="hgrn_bwd",
    )(proj, proj, proj, lbl, do, sprev, mcum, mrev, msum, bd, dproj, *[a for pair in pairs for a in pair[1:]])


def _tail(x, tgt, proj, attn, o, w_a, w_b, w_out, w_at, w_bt, w_outt, b_gate, g_post, gh):
    s = x.shape[0]
    tm = 256
    ones64 = (jnp.arange(HG_WIDTH)[:, None] // 64 == jnp.arange(HG_WIDTH)[None, :] // 64).astype(BF16)
    weights = (w_a, w_b, w_out, w_at, w_bt, w_outt)

    def body(x_ref, t_ref, ml_ref, ga_ref, gb_ref, at_ref, o_ref, *rest):
        w_hbm, (bg_ref, gp_ref, gh_ref, ones_ref) = rest[:6], rest[6:10]
        (dout_ref, dpj_ref, dop_ref, do_ref, mt_ref, dy_ref, yat_ref, dya_ref, ybt_ref, dyb_ref,
         loss_ref, dgp_ref, dbg_ref, dgh_ref) = rest[10:24]
        (wa_ref, wb_ref, wo_ref, wat_ref, wbt_ref, wot_ref), w_sem = rest[24:30], rest[30]

        @pl.when(pl.program_id(0) == 0)
        def _():
            loads = [pltpu.make_async_copy(src, dst, w_sem.at[k])
                     for k, (src, dst) in enumerate(zip(w_hbm, rest[24:30]))]
            _start_all(loads, [])
            loss_ref[...] = jnp.zeros_like(loss_ref)
            dgp_ref[...] = jnp.zeros_like(dgp_ref)
            dbg_ref[...] = jnp.zeros_like(dbg_ref)
            dgh_ref[...] = jnp.zeros_like(dgh_ref)
            _wait_all(loads, [])

        ones = ones_ref[...]
        gate_a = ga_ref[...]
        sa = _sigmoid(gate_a)
        silu_a = gate_a * sa
        attn_v = at_ref[...]
        ya_in = attn_v * silu_a
        ov = o_ref[...]
        ro = lax.rsqrt(_sel_right(ov * ov, ones) * (1.0 / 64.0) + EPS)
        ohat = ov * ro
        ghv = gh_ref[...]
        on = ohat * ghv
        gate_b = gb_ref[...]
        sb = _sigmoid(gate_b)
        silu_b = gate_b * sb
        yb_in = on * silu_b
        ya_bf = ya_in.astype(BF16)
        yb_bf = yb_in.astype(BF16)
        yat_ref[...] = ya_bf.T
        ybt_ref[...] = yb_bf.T
        y_a = _dot(ya_bf, wa_ref[...])
        y_b = _dot(yb_bf, wb_ref[...])
        gts = _sigmoid(ml_ref[...] + bg_ref[...])
        g_a = gts[:, :D_MODEL]
        g_b = gts[:, D_MODEL:]
        m_bf = (g_a * y_a + g_b * y_b).astype(BF16)
        mt_ref[...] = m_bf.T
        y = _dot(m_bf, wo_ref[...])
        r1 = lax.rsqrt(jnp.mean(y * y, axis=-1, keepdims=True) + EPS)
        yn = y * r1
        gp = gp_ref[...]
        e = x_ref[...] + yn * gp - t_ref[...]
        loss_ref[...] += jnp.sum(e * e, axis=0, keepdims=True)
        dout = e * (1.0 / D_MODEL)
        dout_ref[...] = dout
        dgp_ref[...] += jnp.sum(dout * yn, axis=0, keepdims=True)
        dyn = dout * gp
        dy = r1 * (dyn - yn * jnp.mean(dyn * yn, axis=-1, keepdims=True))
        dy_bf = dy.astype(BF16)
        dy_ref[...] = dy_bf
        dm = _dot(dy_bf, wot_ref[...])
        dml_a = dm * y_a * g_a * (1.0 - g_a)
        dml_b = dm * y_b * g_b * (1.0 - g_b)
        dpj_ref[:, :D_MODEL] = dml_a.astype(BF16)
        dpj_ref[:, D_MODEL:2 * D_MODEL] = dml_b.astype(BF16)
        dbg_ref[:, :D_MODEL] += jnp.sum(dml_a, axis=0, keepdims=True)
        dbg_ref[:, D_MODEL:] += jnp.sum(dml_b, axis=0, keepdims=True)
        dya_bf = (dm * g_a).astype(BF16)
        dyb_bf = (dm * g_b).astype(BF16)
        dya_ref[...] = dya_bf
        dyb_ref[...] = dyb_bf
        dya_in = _dot(dya_bf, wat_ref[...])
        dyb_in = _dot(dyb_bf, wbt_ref[...])
        dattn = dya_in * silu_a
        delta = _sel_right(dattn * attn_v, ones)
        lane = lax.broadcasted_iota(jnp.int32, (tm, LANE), 1)
        for p in range(HEADS // 2):
            sl = slice(LANE * p, LANE * (p + 1))
            xs = (dattn[:, sl], pltpu.roll(dattn[:, sl], VDIM, 1))
            nds = (-pltpu.roll(delta[:, sl], VDIM, 1), -delta[:, sl])
            for a in range(2):
                hi, lo_part = _hi_lo(nds[a])
                blk = jnp.where(lane < VDIM, xs[a], jnp.where(lane == VDIM, hi, jnp.where(lane == VDIM + 1, lo_part, 0.0)))
                dop_ref[:, LANE * (2 * p + a):LANE * (2 * p + a + 1)] = blk.astype(BF16)
        dpj_ref[:, 2 * D_MODEL:2 * D_MODEL + HG_WIDTH] = (
            dya_in * attn_v * (sa * (1.0 + gate_a * (1.0 - sa)))).astype(BF16)
        don = dyb_in * silu_b
        dpj_ref[:, 2 * D_MODEL + HG_WIDTH:] = (dyb_in * on * (sb * (1.0 + gate_b * (1.0 - sb)))).astype(BF16)
        dgh_ref[...] += jnp.sum(don * ohat, axis=0, keepdims=True)
        dohat = don * ghv
        do_ref[...] = (ro * (dohat - ohat * (_sel_right(dohat * ohat, ones) * (1.0 / 64.0)))).astype(BF16)

    row = lambda w, j: pl.BlockSpec((tm, w), lambda i: (i, j))
    col = lambda w: pl.BlockSpec((w, tm), lambda i: (0, i))
    full = lambda a: pl.BlockSpec(a.shape, lambda i: (0, 0))
    acc = lambda w: pl.BlockSpec((1, w), lambda i: (0, 0))
    sds = lambda w, dt: jax.ShapeDtypeStruct((s, w), dt)
    sdt = lambda w: jax.ShapeDtypeStruct((w, s), BF16)
    return pl.pallas_call(
        body,
        grid=(s // tm,),
        in_specs=[row(1024, 0), row(1024, 0), row(2048, 0), row(512, 4), row(512, 5), row(512, 0), row(512, 0)]
        + [ANY] * 6 + [full(b_gate), full(g_post), full(gh), full(ones64)],
        out_specs=[row(1024, 0), row(3072, 0), row(1024, 0), row(512, 0),
                   col(1024), row(1024, 0), col(512), row(1024, 0), col(512), row(1024, 0),
                   acc(1024), acc(1024), acc(2048), acc(512)],
        out_shape=[sds(1024, F32), sds(D_IN_PAD, BF16), sds(1024, BF16), sds(512, BF16),
                   sdt(1024), sds(1024, BF16), sdt(512), sds(1024, BF16), sdt(512), sds(1024, BF16),
                   jax.ShapeDtypeStruct((1, 1024), F32), jax.ShapeDtypeStruct((1, 1024), F32),
                   jax.ShapeDtypeStruct((1, 2048), F32), jax.ShapeDtypeStruct((1, 512), F32)],
        scratch_shapes=[pltpu.VMEM(a.shape, BF16) for a in weights] + [pltpu.SemaphoreType.DMA((6,))],
        compiler_params=_params(("arbitrary",), 56),
        name="tail",
    )(x, tgt, proj, proj, proj, attn, o, *weights, b_gate, g_post, gh, ones64)


def _mla_bwd(proj, dqr, dkr, dv, g_q, g_kv, w_uq_pt, w_kv_pt, rc, rs1, rs2, cqt, ckvt, dproj):
    assert HEADS == N_DEV
    s = proj.shape[0]
    tm = 512
    scale = 1.0 / math.sqrt(QK)

    def body(cq_ref, ckv_ref, dqr_ref, dkr_ref, dv_ref, gq_ref, gkv_ref, wuqt_ref, wkvt_ref, c_ref, s1_ref, s2_ref,
             cqt_ref, ckvt_ref, dproj_in, dc_ref, dgq_ref, dgkv_ref, uq_slots, ukv_slots,
             dqf_ref, dkvf_ref, dwuq_ref, dwkv_ref):
        del dproj_in

        @pl.when(pl.program_id(0) == 0)
        def _():
            dgq_ref[...] = jnp.zeros_like(dgq_ref)
            dgkv_ref[...] = jnp.zeros_like(dgkv_ref)
            dwuq_ref[...] = jnp.zeros_like(dwuq_ref)
            dwkv_ref[...] = jnp.zeros_like(dwkv_ref)

        c, s1, s2 = c_ref[...], s1_ref[...], s2_ref[...]
        lane = lax.broadcasted_iota(jnp.int32, (tm, LANE), 1)
        ksum = jnp.zeros((tm, LANE), F32)
        for h in range(HEADS):
            sl = slice(LANE * h, LANE * (h + 1))
            dqf_ref[:, sl] = (_unrope(dqr_ref[:, sl], c, s1, s2) * scale).astype(BF16)
            dkh = dkr_ref[:, sl]
            ksum = ksum + dkh
            dkvf_ref[:, sl] = jnp.where(lane < NOPE, dkh, 0.0).astype(BF16)
            dkvf_ref[:, HEADS * LANE + LANE * h:HEADS * LANE + LANE * (h + 1)] = jnp.where(
                lane < VDIM, dv_ref[:, sl], 0.0).astype(BF16)
        dkpe = _unrope(ksum, c, s1, s2)
        dc_ref[:, Q_LORA + KV_LORA:] = jnp.where((lane >= NOPE) & (lane < QK), dkpe, 0.0).astype(BF16)
        dqf, dkvf = dqf_ref[...], dkvf_ref[...]
        dwuq_ref[...] += _dot(cqt_ref[...], dqf)
        dwkv_ref[...] += _dot(ckvt_ref[...], dkvf)
        dcqn = _dot(dqf, wuqt_ref[...])
        dckvn = _dot(dkvf, wkvt_ref[...])
        for x_ref, g_ref, dn, cols, dg_ref in ((cq_ref, gq_ref, dcqn, slice(0, Q_LORA), dgq_ref),
                                               (ckv_ref, gkv_ref, dckvn, slice(Q_LORA, Q_LORA + KV_LORA), dgkv_ref)):
            xv = x_ref[...]
            r = lax.rsqrt(jnp.mean(xv * xv, axis=-1, keepdims=True) + EPS)
            xh = xv * r
            dg_ref[...] += jnp.sum(dn * xh, axis=0, keepdims=True)
            dh = dn * g_ref[...]
            dc_ref[:, cols] = (r * (dh - xh * jnp.mean(dh * xh, axis=-1, keepdims=True))).astype(BF16)

        @pl.when(pl.program_id(0) == s // tm - 1)
        def _():
            ur = Q_LORA // N_DEV
            for p in range(N_DEV):
                uq_slots[p] = jnp.concatenate(
                    [dwuq_ref[ur * p:ur * (p + 1), LANE * h:LANE * h + QK] for h in range(HEADS)], axis=1).astype(BF16)
                ukv_slots[p] = jnp.concatenate(
                    [dwkv_ref[:, LANE * p:LANE * p + NOPE],
                     dwkv_ref[:, LANE * (HEADS + p):LANE * (HEADS + p) + VDIM]], axis=1).astype(BF16)

    row = lambda w, j: pl.BlockSpec((tm, w), lambda i: (i, j))
    full = lambda a: pl.BlockSpec(a.shape, lambda i: (0, 0))
    acc = lambda w: pl.BlockSpec((1, w), lambda i: (0, 0))
    col = lambda w: pl.BlockSpec((w, tm), lambda i: (0, i))
    whole = lambda shape: pl.BlockSpec(shape, lambda i: (0, 0, 0))
    uq_shape = (N_DEV, Q_LORA // N_DEV, HEADS * QK)
    ukv_shape = (N_DEV, KV_LORA, NOPE + VDIM)
    return pl.pallas_call(
        body,
        grid=(s // tm,),
        in_specs=[row(768, 6), row(256, 21), row(1024, 0), row(1024, 0), row(1024, 0), full(g_q), full(g_kv),
                  full(w_uq_pt), full(w_kv_pt), row(128, 0), row(128, 0), row(128, 0), col(Q_LORA), col(KV_LORA),
                  pl.BlockSpec(memory_space=pl.ANY)],
        out_specs=[row(1152, 4), acc(768), acc(256), whole(uq_shape), whole(ukv_shape)],
        out_shape=[jax.ShapeDtypeStruct(dproj.shape, BF16),
                   jax.ShapeDtypeStruct((1, 768), F32), jax.ShapeDtypeStruct((1, 256), F32),
                   jax.ShapeDtypeStruct(uq_shape, BF16), jax.ShapeDtypeStruct(ukv_shape, BF16)],
        input_output_aliases={14: 0},
        scratch_shapes=[pltpu.VMEM((tm, HEADS * LANE), BF16), pltpu.VMEM((tm, 2 * HEADS * LANE), BF16),
                        pltpu.VMEM((Q_LORA, HEADS * LANE), F32), pltpu.VMEM((KV_LORA, 2 * HEADS * LANE), F32)],
        compiler_params=_params(("arbitrary",)),
        name="mla_bwd",
    )(proj, proj, dqr, dkr, dv, g_q, g_kv, w_uq_pt, w_kv_pt, rc, rs1, rs2, cqt, ckvt, dproj)


def _dh_dx(dproj, w_in_pt, x, dout, g_pre, sends, vecs):
    s, k = dproj.shape
    tm = 256
    ns, ni, nv = len(sends), s // tm, len(vecs)

    def body(dp_ref, w_ref, x_ref, dout_ref, g_ref, *rest):
        send_refs, vec_refs, (dx_ref, dg_ref) = rest[:ns], rest[ns:ns + nv], rest[ns + nv:ns + nv + 2]
        recv_refs, gsum_ref = rest[ns + nv + 2:2 * ns + nv + 2], rest[2 * ns + nv + 2]
        sems, vec_scratch = rest[2 * ns + nv + 3:2 * ns + nv + 6], rest[2 * ns + nv + 6:]

        @pl.when(pl.program_id(0) == 0)
        def _():
            _start_all(*_to_chips_copies(send_refs, recv_refs, sems))
            dg_ref[...] = jnp.zeros_like(dg_ref)

        dh = _dot(dp_ref[...], w_ref[...])
        xv = x_ref[...]
        r = lax.rsqrt(jnp.mean(xv * xv, axis=-1, keepdims=True) + EPS)
        xh = xv * r
        dg_ref[...] += jnp.sum(dh * xh, axis=0, keepdims=True)
        dxh = dh * g_ref[...]
        dx_ref[...] = dout_ref[...] + r * (dxh - xh * jnp.mean(dxh * xh, axis=-1, keepdims=True))

        @pl.when(pl.program_id(0) == ni - 1)
        def _():
            _vectors_exchange(dg_ref, vec_refs, gsum_ref, *vec_scratch)
            _wait_all(*_to_chips_copies(send_refs, recv_refs, sems))

    row = lambda w: pl.BlockSpec((tm, w), lambda i: (i, 0))
    whole = lambda shape: pl.BlockSpec(shape, lambda i: (0, 0))
    return pl.pallas_call(
        body,
        grid=(ni,),
        in_specs=[row(k), whole((k, D_MODEL)), row(D_MODEL), row(D_MODEL), whole((1, D_MODEL))] + [ANY] * ns
        + [whole(a.shape) for a in vecs],
        out_specs=[row(D_MODEL), whole((1, D_MODEL))] + [ANY] * ns + [whole((8, 1024))],
        out_shape=[jax.ShapeDtypeStruct((s, D_MODEL), F32), jax.ShapeDtypeStruct((1, D_MODEL), F32)]
        + [jax.ShapeDtypeStruct(a.shape, a.dtype) for a in sends] + [jax.ShapeDtypeStruct((8, 1024), F32)],
        scratch_shapes=_copy_sems(ns, 3) + [pltpu.VMEM((8, 1024), F32), pltpu.VMEM((N_DEV, 8, 1024), F32),
                                            pltpu.SemaphoreType.DMA((7,)), pltpu.SemaphoreType.DMA((7,))],
        compiler_params=_params(("arbitrary",)),
        name="dh_dx",
    )(dproj, w_in_pt, x, dout, g_pre, *sends, *vecs)


def _pair_reduce(slots):
    n = len(slots)
    half = [(N_DEV // 2,) + a.shape[1:] for a in slots]

    def body(*refs):
        s_refs, o_refs = refs[:n], refs[n:2 * n]
        mine, got = refs[2 * n:3 * n], refs[3 * n:4 * n]
        send_sems, recv_sems, local_sems, out_sems = refs[4 * n:]
        x, y, c = _my_place()
        copies, loads, stores = [], [], []
        for q in range(N_DEV // 2):
            for a in range(n):
                copies.append(pltpu.make_async_remote_copy(
                    src_ref=s_refs[a].at[2 * q + 1 - c], dst_ref=got[a].at[q],
                    send_sem=send_sems.at[4 * a + q], recv_sem=recv_sems.at[4 * a + q],
                    device_id=(x, y, 1 - c), device_id_type=MESH_ID))
                loads.append(pltpu.make_async_copy(s_refs[a].at[2 * q + c], mine[a].at[q], local_sems.at[4 * a + q]))
                stores.append(pltpu.make_async_copy(mine[a].at[q], o_refs[a].at[q], out_sems.at[4 * a + q]))
        _start_all(loads, copies)
        k = 0
        for q in range(N_DEV // 2):
            for a in range(n):
                loads[k].wait()
                copies[k].wait_recv()
                mine[a][q] = (mine[a][q].astype(F32) + got[a][q].astype(F32)).astype(mine[a].dtype)
                stores[k].start()
                k += 1
        for cp in copies:
            cp.wait_send()
        for cp in stores:
            cp.wait()

    vm = lambda: [pltpu.VMEM(h, a.dtype) for h, a in zip(half, slots)]
    return pl.pallas_call(
        body,
        in_specs=[ANY] * n,
        out_specs=[ANY] * n,
        out_shape=[jax.ShapeDtypeStruct(h, a.dtype) for h, a in zip(half, slots)],
        scratch_shapes=vm() + vm() + [pltpu.SemaphoreType.DMA((4 * n,)), pltpu.SemaphoreType.DMA((4 * n,)),
                                      pltpu.SemaphoreType.DMA((4 * n,)), pltpu.SemaphoreType.DMA((4 * n,))],
        compiler_params=pltpu.CompilerParams(vmem_limit_bytes=48 * 2**20),
        name="pair_reduce",
    )(*slots)


def _rope_tables(s):
    inv = (np.float32(ROPE_THETA) ** (-np.arange(0, ROPE, 2, dtype=np.float32) / np.float32(ROPE))).astype(np.float32)
    ang = (np.arange(s, dtype=np.float32)[:, None] * inv[None, :]).astype(np.float32)
    cos, sin = jnp.asarray(np.cos(ang.astype(np.float64)), F32), jnp.asarray(np.sin(ang.astype(np.float64)), F32)
    z = lambda w: jnp.zeros((s, w), F32)
    rc = jnp.concatenate([jnp.ones((s, NOPE), F32), cos, cos, z(32)], axis=1)
    rs1 = jnp.concatenate([z(NOPE), -sin, z(16), z(32)], axis=1)
    rs2 = jnp.concatenate([z(NOPE), z(16), sin, z(32)], axis=1)
    return rc, rs1, rs2


def _step(x, tgt, w_blk, shards, g_pre, b_gate, g_q, g_kv, lbl, g_hgrn, g_post):
    s = x.shape[0]
    rc, rs1, rs2 = _rope_tables(s)
    gh = jnp.tile(g_hgrn, (1, HEADS))

    proj, w_in_pt, ht, *got = _gather_proj(x, g_pre, w_blk, shards[:2])
    w_uq, w_ukv = (_from_slots(n, g) for n, g in zip(MATS[:2], got))
    w_uq_p = jnp.pad(w_uq.reshape(Q_LORA, HEADS, QK), ((0, 0), (0, 0), (0, LANE - QK))).reshape(Q_LORA, HEADS * LANE)
    kv3 = w_ukv.reshape(KV_LORA, HEADS, NOPE + VDIM)
    pad64 = lambda t: jnp.pad(t, ((0, 0), (0, 0), (0, LANE - 64))).reshape(KV_LORA, HEADS * LANE)
    w_kv_p = jnp.concatenate([pad64(kv3[:, :, :NOPE]), pad64(kv3[:, :, NOPE:])], axis=1)

    qr, kr, v, cqt, ckvt, *later_shards = _mla_prep(proj, g_q, g_kv, w_uq_p, w_kv_p, rc, rs1, rs2, shards[2:])
    attn, qa, *got = _attn_fwd(qr, kr, v, later_shards)
    w_a, w_b, w_out = (_from_slots(n, g) for n, g in zip(MATS[2:], got))
    o, sprev = _hgrn_fwd(proj, lbl)
    (dout, dproj, dop, do, mt, dy_bf, yat, dya_bf, ybt, dyb_bf,
     loss_vec, dg_post, db_gate, dgh) = _tail(x, tgt, proj, attn, o, w_a, w_b, w_out, w_a.T, w_b.T, w_out.T,
                                               b_gate, g_post, gh)
    dproj, dlbl, *early = _hgrn_bwd(proj, lbl, do, sprev, dproj,
                                    [("w_branch_a", yat, dya_bf), ("w_branch_b", ybt, dyb_bf), ("w_out", mt, dy_bf)])
    dqr, dkr, dv, *early_recv = _attn_bwd(qa, kr, v, dop, early)
    dproj, dg_q, dg_kv, dw_uq_slots, dw_ukv_slots = _mla_bwd(proj, dqr, dkr, dv, g_q, g_kv, w_uq_p.T, w_kv_p.T,
                                                             rc, rs1, rs2, cqt, ckvt, dproj)

    dw_in_slots = _dw_in_slots(ht, dproj)
    late = _pair_reduce([dw_in_slots, dw_uq_slots, dw_ukv_slots])
    dx, _, *late_recv, g_sum = _dh_dx(dproj, w_in_pt, x, dout, g_pre, late,
                                      (db_gate, dg_q, dg_kv, dlbl, dgh, dg_post, loss_vec))
    return dx, late_recv[0], dict(zip(MATS, late_recv[1:] + early_recv)), g_sum


def _adamw(g, w, m, v):
    c1 = 1.0 / (1.0 - ADAM_B1 ** ADAM_STEP)
    c2 = 1.0 / (1.0 - ADAM_B2 ** ADAM_STEP)
    nm = ADAM_B1 * m + (1.0 - ADAM_B1) * g
    nv = ADAM_B2 * v + (1.0 - ADAM_B2) * (g * g)
    d = -ADAM_LR * ((nm * c1) / (jnp.sqrt(nv * c2) + ADAM_EPS) + ADAM_WD * w)
    return d, nm, nv


def _sum8(r_ref):
    g = r_ref[0].astype(F32)
    for k in range(1, r_ref.shape[0]):
        g = g + r_ref[k].astype(F32)
    return g


def _sum_adamw_w_in(recv, w, m, v):
    rows, _, cols = w.shape
    tc = 512
    nc = cols // tc

    def body(r_ref, w_hbm, m_hbm, v_hbm, g_hbm, d_hbm, nm_hbm, nv_hbm, ins, outs, in_sems, out_sems):
        i = pl.program_id(0)
        slot = i & 1
        cols_of = lambda step: pl.ds(pl.multiple_of(step * tc, tc), tc)

        def load(k, step, sl):
            return pltpu.make_async_copy((w_hbm, m_hbm, v_hbm)[k].at[:, 0, cols_of(step)], ins.at[sl, k],
                                         in_sems.at[sl, k])

        def store(k, step, sl):
            return pltpu.make_async_copy(outs.at[sl, k], (g_hbm, d_hbm, nm_hbm, nv_hbm)[k].at[:, 0, cols_of(step)],
                                         out_sems.at[sl, k])

        @pl.when(i == 0)
        def _():
            for k in range(3):
                load(k, 0, 0).start()

        @pl.when(i + 1 < nc)
        def _():
            for k in range(3):
                load(k, i + 1, 1 - slot).start()

        @pl.when(i >= 2)
        def _():
            for k in range(4):
                store(k, i - 2, slot).wait()

        for k in range(3):
            load(k, i, slot).wait()
        g = _sum8(r_ref)
        d, nm, nv = _adamw(g, ins[slot, 0], ins[slot, 1], ins[slot, 2])
        for k, val in enumerate((g, d, nm, nv)):
            outs[slot, k] = val
        for k in range(4):
            store(k, i, slot).start()

        @pl.when(i == nc - 1)
        def _():
            for k in range(4):
                store(k, i, slot).wait()
            if nc >= 2:
                for k in range(4):
                    store(k, i - 1, 1 - slot).wait()

    out = jax.ShapeDtypeStruct((rows, 1, cols), F32)
    return pl.pallas_call(
        body,
        grid=(nc,),
        in_specs=[pl.BlockSpec((recv.shape[0], rows, tc), lambda i: (0, 0, i)), ANY, ANY, ANY],
        out_specs=[ANY, ANY, ANY, ANY],
        out_shape=[out, out, out, out],
        scratch_shapes=[pltpu.VMEM((2, 3, rows, tc), F32), pltpu.VMEM((2, 4, rows, tc), F32),
                        pltpu.SemaphoreType.DMA((2, 3)), pltpu.SemaphoreType.DMA((2, 4))],
        compiler_params=_params(("arbitrary",)),
        name="sum_adamw_w_in",
    )(recv, w, m, v)


def _small_adamw(recvs, ws, ms, vs, g_sum, vws, vms, vvs):
    n, nv_ = len(ws), len(SMALL)

    def body(*refs):
        r_refs, w_refs, m_refs, v_refs = refs[:n], refs[n:2 * n], refs[2 * n:3 * n], refs[3 * n:4 * n]
        first_out = 4 * n + 1 + 3 * nv_
        g_ref, vec_in = refs[4 * n], refs[4 * n + 1:first_out]
        outs = refs[first_out:first_out + 4 * n]
        loss_ref, vouts = refs[first_out + 4 * n], refs[first_out + 4 * n + 1:]
        for a in range(n):
            g = _sum8(r_refs[a])
            d, nm, nv = _adamw(g, w_refs[a][...], m_refs[a][...], v_refs[a][...])
            outs[a][...] = g
            outs[n + a][...] = d
            outs[2 * n + a][...] = nm
            outs[3 * n + a][...] = nv
        g = g_ref[...]
        loss_ref[...] = g[4:5, KV_LORA:KV_LORA + 1]
        grads = (g[0:1, :], jnp.concatenate([g[1:2, :], g[2:3, :]], axis=1), g[3:4, :Q_LORA], g[4:5, :KV_LORA],
                 jnp.concatenate([g[5:6, :HG_WIDTH], g[5:6, HG_WIDTH:]], axis=0), g[6:7, :VDIM], g[7:8, :])
        for a in range(nv_):
            d, nm, nv = _adamw(grads[a], vec_in[a][...], vec_in[nv_ + a][...], vec_in[2 * nv_ + a][...])
            vouts[a][...] = grads[a]
            vouts[nv_ + a][...] = d
            vouts[2 * nv_ + a][...] = nm
            vouts[3 * nv_ + a][...] = nv

    shapes = [jax.ShapeDtypeStruct(w.shape, F32) for w in ws]
    vshapes = [jax.ShapeDtypeStruct(SMALL_SHAPE[k], F32) for k in SMALL]
    res = pl.pallas_call(
        body,
        out_shape=shapes * 4 + [jax.ShapeDtypeStruct((1, 1), F32)] + vshapes * 4,
        compiler_params=pltpu.CompilerParams(vmem_limit_bytes=48 * 2**20),
        name="small_adamw",
    )(*recvs, *ws, *ms, *vs, g_sum, *vws, *vms, *vvs)
    mats = (res[:n], res[n:2 * n], res[2 * n:3 * n], res[3 * n:4 * n])
    v0 = 4 * n + 1
    vecs = tuple(res[v0 + j * nv_:v0 + (j + 1) * nv_] for j in range(4))
    return mats, res[4 * n], vecs


SMALL = ("g_pre", "b_gate", "g_q", "g_kv", "lb_logits", "g_hgrn", "g_post")
SMALL_SHAPE = dict(g_pre=(1, 1024), b_gate=(1, 2048), g_q=(1, 768), g_kv=(1, 256), lb_logits=(2, 512),
                   g_hgrn=(1, 64), g_post=(1, 1024))


def _vectors_exchange(gpre_ref, vec_refs, out_ref, mine, got, send_sems, recv_sems):
    bg_ref, gq_ref, gkv_ref, lbl_ref, gh_ref, gpost_ref, loss_ref = vec_refs
    mine[...] = jnp.zeros_like(mine)
    mine[0:1, :] = gpre_ref[...]
    mine[1:2, :] = bg_ref[:, :1024]
    mine[2:3, :] = bg_ref[:, 1024:]
    mine[3:4, :Q_LORA] = gq_ref[...]
    mine[4:5, :KV_LORA] = gkv_ref[...]
    loss = (0.5 / D_MODEL) * jnp.sum(loss_ref[...], axis=-1, keepdims=True)
    mine[4:5, KV_LORA:] = jnp.broadcast_to(loss, (1, 1024 - KV_LORA))
    mine[5:6, :HG_WIDTH] = lbl_ref[0:1, :]
    mine[5:6, HG_WIDTH:] = lbl_ref[1:2, :]
    gh = gh_ref[...]
    fold = gh[:, :VDIM]
    for h in range(1, HEADS):
        fold = fold + gh[:, VDIM * h:VDIM * (h + 1)]
    mine[6:7, :VDIM] = fold
    mine[7:8, :] = gpost_ref[...]
    x, y, c = _my_place()
    me = 4 * x + 2 * y + c
    got[me] = mine[...]
    copies = [pltpu.make_async_remote_copy(
        src_ref=mine, dst_ref=got.at[me], send_sem=send_sems.at[k], recv_sem=recv_sems.at[k],
        device_id=_flip(k, x, y, c), device_id_type=MESH_ID) for k in range(N_DEV - 1)]
    _start_all([], copies)
    _wait_all([], copies)
    out_ref[...] = _sum8(got)


MATS = ("w_uq", "w_ukv", "w_branch_a", "w_branch_b", "w_out")
COL_SHARDED = dict(w_uq=False, w_ukv=True, w_branch_a=True, w_branch_b=True, w_out=False)
ORDER = ("g_pre", "w_in", "b_gate", "g_q", "w_uq", "g_kv", "w_ukv", "lb_logits", "g_hgrn",
         "w_branch_a", "w_branch_b", "w_out", "g_post")


def _from_slots(name, slots):
    _, r, c = slots.shape
    if COL_SHARDED[name]:
        return slots.transpose(1, 0, 2).reshape(r, N_DEV * c)
    return slots.reshape(N_DEV * r, c)


def kernel(x, g_pre, w_in, b_gate, g_q, w_uq, g_kv, w_ukv, lb_logits, g_hgrn, w_branch_a, w_branch_b, w_out, g_post, loss_target, m_g_pre, m_w_in, m_b_gate, m_g_q, m_w_uq, m_g_kv, m_w_ukv, m_lb_logits, m_g_hgrn, m_w_branch_a, m_w_branch_b, m_w_out, m_g_post, v_g_pre, v_w_in, v_b_gate, v_g_q, v_w_uq, v_g_kv, v_w_ukv, v_lb_logits, v_g_hgrn, v_w_branch_a, v_w_branch_b, v_w_out, v_g_post):
    rows3 = lambda a: jnp.transpose(a, (2, 0, 1))
    w = dict(w_in=rows3(w_in), w_uq=w_uq[0], w_ukv=w_ukv[0], w_branch_a=w_branch_a[0], w_branch_b=w_branch_b[0],
             w_out=w_out[0], g_pre=g_pre, b_gate=b_gate, g_q=g_q, g_kv=g_kv, lb_logits=lb_logits, g_hgrn=g_hgrn,
             g_post=g_post)
    mom = dict(w_in=rows3(m_w_in), w_uq=m_w_uq[0], w_ukv=m_w_ukv[0], w_branch_a=m_w_branch_a[0],
               w_branch_b=m_w_branch_b[0], w_out=m_w_out[0], g_pre=m_g_pre, b_gate=m_b_gate, g_q=m_g_q, g_kv=m_g_kv,
               lb_logits=m_lb_logits, g_hgrn=m_g_hgrn, g_post=m_g_post)
    var = dict(w_in=rows3(v_w_in), w_uq=v_w_uq[0], w_ukv=v_w_ukv[0], w_branch_a=v_w_branch_a[0],
               w_branch_b=v_w_branch_b[0], w_out=v_w_out[0], g_pre=v_g_pre, b_gate=v_b_gate, g_q=v_g_q, g_kv=v_g_kv,
               lb_logits=v_lb_logits, g_hgrn=v_g_hgrn, g_post=v_g_post)

    w_blk = w["w_in"].reshape(W_IN_SHARD, D_MODEL).astype(BF16)
    shards = [w[n].astype(BF16) for n in MATS[:2]] + [w[n] for n in MATS[2:]]
    dx, recv_in, recv, g_sum = _step(x[0], loss_target[0], w_blk, shards,
                                     g_pre, b_gate, g_q, g_kv, lb_logits, g_hgrn, g_post)

    g_in, d_in, m_in, v_in = _sum_adamw_w_in(recv_in, w["w_in"], mom["w_in"], var["w_in"])
    res, total, vec = _small_adamw([recv[n] for n in MATS], *([t[n] for n in MATS] for t in (w, mom, var)),
                                   g_sum, *([t[n] for n in SMALL] for t in (w, mom, var)))

    outs = []
    for mats, vecs, big in zip(res, vec, (g_in, d_in, m_in, v_in)):
        t = {**{n: a[None] for n, a in zip(MATS, mats)}, **dict(zip(SMALL, vecs)),
             "w_in": jnp.transpose(big, (1, 2, 0))}
        outs += [t[n] for n in ORDER]
    return (total.reshape(()), dx[None], *outs)
```
